```python
import math
import jax, jax.numpy as jnp
from jax import lax
import numpy as np

D_MODEL = 1024
BATCH = 8
SEQ = 2048
DEPTH = 1

HEAD_DIM = 64
DIL_CONFIGS = ((128, 1), (512, 4), (2048, 16))
N_DIL_GROUPS = len(DIL_CONFIGS)
DIL_HEADS_PER_GROUP = 4
N_DIL_HEADS = N_DIL_GROUPS * DIL_HEADS_PER_GROUP
N_FOX_HEADS = 8
BLOCK = 128
ROPE_THETA = 500000.0
ROPE_DIM = HEAD_DIM // 4
D_FF = -(-(8 * D_MODEL) // (3 * 256)) * 256
EPS = 1e-6
NEG_INF = -1e30

DIL_WIDTH = N_DIL_HEADS * HEAD_DIM
DIL_OUT_WIDTH = DIL_HEADS_PER_GROUP * HEAD_DIM
FOX_WIDTH = N_FOX_HEADS * HEAD_DIM
IN_SPLITS = (DIL_WIDTH, DIL_WIDTH, DIL_WIDTH, FOX_WIDTH, FOX_WIDTH, FOX_WIDTH,
             N_FOX_HEADS, D_MODEL, D_MODEL)
IN_COLS = sum(IN_SPLITS)

kernel_name = "hybrid_dilated_fox_gated_block"


def rmsnorm(x, gain):
    xf = x.astype(jnp.float32)
    y = xf * lax.rsqrt(jnp.mean(xf * xf, axis=-1, keepdims=True) + EPS)
    return (y * gain.astype(jnp.float32)).astype(x.dtype)


def partial_rope(x, positions):
    half = ROPE_DIM // 2
    inv_freq = jnp.power(ROPE_THETA, -jnp.arange(0, ROPE_DIM, 2, dtype=jnp.float32) / ROPE_DIM)
    ang = positions[:, None] * inv_freq[None, :]
    cos = jnp.cos(ang)[None, :, None, :].astype(x.dtype)
    sin = jnp.sin(ang)[None, :, None, :].astype(x.dtype)
    x1, x2, rest = x[..., :half], x[..., half:ROPE_DIM], x[..., ROPE_DIM:]
    return jnp.concatenate([x1 * cos - x2 * sin, x2 * cos + x1 * sin, rest], axis=-1)


def dilated_window_attention(q, k, v, window, dilation):
    B, S, H, Dh = q.shape
    steps = window // dilation
    span = dilation * BLOCK
    s_pad = -(-S // span) * span
    L = s_pad // dilation
    nb = L // BLOCK

    def to_blocks(t):
        t = jnp.pad(t, ((0, 0), (0, s_pad - S), (0, 0), (0, 0)))
        t = t.reshape(B, L, dilation, H, Dh).transpose(0, 2, 1, 3, 4)
        return t.reshape(B * dilation, nb, BLOCK, H, Dh)

    qb, kb, vb = to_blocks(q), to_blocks(k), to_blocks(v)

    def with_prev(t):
        prev = jnp.pad(t[:, :-1], ((0, 0), (1, 0), (0, 0), (0, 0), (0, 0)))
        return jnp.concatenate([prev, t], axis=2)

    kc, vc = with_prev(kb), with_prev(vb)
    scale = 1.0 / math.sqrt(Dh)
    scores = jnp.einsum('znqhd,znkhd->znhqk', qb, kc).astype(jnp.float32) * scale
    qi = jnp.arange(BLOCK)[:, None]
    kj = jnp.arange(2 * BLOCK)[None, :]
    rel = qi + BLOCK - kj
    blk = jnp.arange(nb)[:, None, None]
    valid = (rel >= 0) & (rel <= steps) & (blk * BLOCK + kj - BLOCK >= 0)
    scores = jnp.where(valid[None, :, None, :, :], scores, NEG_INF)
    m = jnp.max(scores, axis=-1, keepdims=True)
    p = jnp.exp(scores - m)
    den = jnp.sum(p, axis=-1, keepdims=True)
    out = jnp.einsum('znhqk,znkhd->znqhd', (p / den).astype(v.dtype), vc)
    lse = (m + jnp.log(den))[..., 0].transpose(0, 1, 3, 2)

    def from_blocks(t):
        t = t.reshape((B, dilation, L) + t.shape[3:])
        t = jnp.moveaxis(t, 1, 2).reshape((B, s_pad) + t.shape[3:])
        return t[:, :S]

    return from_blocks(out), from_blocks(lse)


def forgetting_attention(q, k, v, log_f):
    B, S, H, Dh = q.shape
    F = jnp.cumsum(log_f.astype(jnp.float32), axis=1).transpose(0, 2, 1)
    scale = 1.0 / math.sqrt(Dh)
    outs = []
    for n in range(S // BLOCK):
        lo, hi = n * BLOCK, (n + 1) * BLOCK
        s = jnp.einsum('bqhd,bkhd->bhqk', q[:, lo:hi], k[:, :hi]).astype(jnp.float32) * scale
        s = s + (F[:, :, lo:hi, None] - F[:, :, None, :hi])
        causal = jnp.arange(lo, hi)[:, None] >= jnp.arange(hi)[None, :]
        s = jnp.where(causal[None, None], s, NEG_INF)
        p = jax.nn.softmax(s, axis=-1)
        outs.append(jnp.einsum('bhqk,bkhd->bqhd', p.astype(v.dtype), v[:, :hi]))
    return jnp.concatenate(outs, axis=1)


def token_mixer(h, w_in, w_proj_a, w_proj_b, w_out, b_forget):
    B, S, _ = h.shape
    proj = h @ w_in
    idx = list(np.cumsum(IN_SPLITS)[:-1])
    qa, ka, va, qb, kb, vb, f_logit, g_a, g_b = jnp.split(proj, idx, axis=-1)
    positions = jnp.arange(S, dtype=jnp.float32)

    qa = partial_rope(qa.reshape(B, S, N_DIL_HEADS, HEAD_DIM), positions)
    ka = partial_rope(ka.reshape(B, S, N_DIL_HEADS, HEAD_DIM), positions)
    va = va.reshape(B, S, N_DIL_HEADS, HEAD_DIM)
    group_out, group_lse = [], []
    for g, (window, dilation) in enumerate(DIL_CONFIGS):
        sl = slice(g * DIL_HEADS_PER_GROUP, (g + 1) * DIL_HEADS_PER_GROUP)
        o, lse = dilated_window_attention(qa[:, :, sl], ka[:, :, sl], va[:, :, sl], window, dilation)
        group_out.append(o)
        group_lse.append(lse)
    w_groups = jax.nn.softmax(jnp.stack(group_lse, axis=0), axis=0)
    out_a = jnp.sum(w_groups[..., None].astype(h.dtype) * jnp.stack(group_out, axis=0), axis=0)
    out_a = out_a.reshape(B, S, DIL_OUT_WIDTH)

    log_f = jax.nn.log_sigmoid((f_logit + b_forget).astype(jnp.float32))
    out_b = forgetting_attention(qb.reshape(B, S, N_FOX_HEADS, HEAD_DIM),
                                 kb.reshape(B, S, N_FOX_HEADS, HEAD_DIM),
                                 vb.reshape(B, S, N_FOX_HEADS, HEAD_DIM), log_f)
    out_b = out_b.reshape(B, S, FOX_WIDTH)

    merged = jax.nn.sigmoid(g_a) * (out_a @ w_proj_a) + jax.nn.sigmoid(g_b) * (out_b @ w_proj_b)
    return merged @ w_out


def swiglu(h, w_gate, w_up, w_down):
    return (jax.nn.silu(h @ w_gate) * (h @ w_up)) @ w_down


def _fwd_setup_inputs(seed: int = 0) -> dict:
    key = jax.random.key(seed)
    ks = jax.random.split(key, 14)
    f32 = jnp.float32

    def dense(k, fan_in, fan_out):
        return jax.random.normal(k, (DEPTH, fan_in, fan_out), f32) * fan_in ** -0.5

    def gain(k):
        return 1.0 + 0.05 * jax.random.normal(k, (DEPTH, D_MODEL), f32)

    return {
        "x": jax.random.normal(ks[0], (BATCH, SEQ, D_MODEL), f32),
        "w_in": dense(ks[1], D_MODEL, IN_COLS),
        "w_proj_a": dense(ks[2], DIL_OUT_WIDTH, D_MODEL),
        "w_proj_b": dense(ks[3], FOX_WIDTH, D_MODEL),
        "w_out": dense(ks[4], D_MODEL, D_MODEL),
        "b_forget": jax.random.uniform(ks[5], (DEPTH, N_FOX_HEADS), f32, minval=1.0, maxval=5.0),
        "w_ffn_gate": dense(ks[6], D_MODEL, D_FF),
        "w_ffn_up": dense(ks[7], D_MODEL, D_FF),
        "w_ffn_down": dense(ks[8], D_FF, D_MODEL),
        "norm_mix_pre": gain(ks[9]),
        "norm_mix_post": gain(ks[10]),
        "norm_ffn_pre": gain(ks[11]),
        "norm_ffn_post": gain(ks[12]),
    }


def _fwd_reference(x, w_in, w_proj_a, w_proj_b, w_out, b_forget, w_ffn_gate, w_ffn_up, w_ffn_down,
              norm_mix_pre, norm_mix_post, norm_ffn_pre, norm_ffn_post):
    for layer in range(DEPTH):
        h = rmsnorm(x, norm_mix_pre[layer])
        mix = token_mixer(h, w_in[layer], w_proj_a[layer], w_proj_b[layer], w_out[layer], b_forget[layer])
        x = x + rmsnorm(mix, norm_mix_post[layer])
        h = rmsnorm(x, norm_ffn_pre[layer])
        ff = swiglu(h, w_ffn_gate[layer], w_ffn_up[layer], w_ffn_down[layer])
        x = x + rmsnorm(ff, norm_ffn_post[layer])
    return x


import jax as _jax
import jax.numpy as _jnp

TWIN_FORMAT = 'train_step'
FWD_PARAMS = ['x', 'w_in', 'w_proj_a', 'w_proj_b', 'w_out', 'b_forget', 'w_ffn_gate', 'w_ffn_up', 'w_ffn_down', 'norm_mix_pre', 'norm_mix_post', 'norm_ffn_pre', 'norm_ffn_post']
TWIN_WEIGHTS = ['w_in', 'w_proj_a', 'w_proj_b', 'w_out', 'b_forget', 'w_ffn_gate', 'w_ffn_up', 'w_ffn_down', 'norm_mix_pre', 'norm_mix_post', 'norm_ffn_pre', 'norm_ffn_post']
TWIN_DIFF_INPUT = 'x'
TWIN_INPUTS = ['x', 'w_in', 'w_proj_a', 'w_proj_b', 'w_out', 'b_forget', 'w_ffn_gate', 'w_ffn_up', 'w_ffn_down', 'norm_mix_pre', 'norm_mix_post', 'norm_ffn_pre', 'norm_ffn_post', 'loss_target', 'm_w_in', 'm_w_proj_a', 'm_w_proj_b', 'm_w_out', 'm_b_forget', 'm_w_ffn_gate', 'm_w_ffn_up', 'm_w_ffn_down', 'm_norm_mix_pre', 'm_norm_mix_post', 'm_norm_ffn_pre', 'm_norm_ffn_post', 'v_w_in', 'v_w_proj_a', 'v_w_proj_b', 'v_w_out', 'v_b_forget', 'v_w_ffn_gate', 'v_w_ffn_up', 'v_w_ffn_down', 'v_norm_mix_pre', 'v_norm_mix_post', 'v_norm_ffn_pre', 'v_norm_ffn_post']
TWIN_OUTPUTS = ['loss', 'grad_x', 'grad_w_in', 'grad_w_proj_a', 'grad_w_proj_b', 'grad_w_out', 'grad_b_forget', 'grad_w_ffn_gate', 'grad_w_ffn_up', 'grad_w_ffn_down', 'grad_norm_mix_pre', 'grad_norm_mix_post', 'grad_norm_ffn_pre', 'grad_norm_ffn_post', 'delta_w_in', 'delta_w_proj_a', 'delta_w_proj_b', 'delta_w_out', 'delta_b_forget', 'delta_w_ffn_gate', 'delta_w_ffn_up', 'delta_w_ffn_down', 'delta_norm_mix_pre', 'delta_norm_mix_post', 'delta_norm_ffn_pre', 'delta_norm_ffn_post', 'new_m_w_in', 'new_m_w_proj_a', 'new_m_w_proj_b', 'new_m_w_out', 'new_m_b_forget', 'new_m_w_ffn_gate', 'new_m_w_ffn_up', 'new_m_w_ffn_down', 'new_m_norm_mix_pre', 'new_m_norm_mix_post', 'new_m_norm_ffn_pre', 'new_m_norm_ffn_post', 'new_v_w_in', 'new_v_w_proj_a', 'new_v_w_proj_b', 'new_v_w_out', 'new_v_b_forget', 'new_v_w_ffn_gate', 'new_v_w_ffn_up', 'new_v_w_ffn_down', 'new_v_norm_mix_pre', 'new_v_norm_mix_post', 'new_v_norm_ffn_pre', 'new_v_norm_ffn_post']
TWIN_LEAF_KINDS = {'loss': 'loss', 'grad_x': 'grad_x', 'grad_w_in': 'grad_w', 'grad_w_proj_a': 'grad_w', 'grad_w_proj_b': 'grad_w', 'grad_w_out': 'grad_w', 'grad_b_forget': 'grad_w', 'grad_w_ffn_gate': 'grad_w', 'grad_w_ffn_up': 'grad_w', 'grad_w_ffn_down': 'grad_w', 'grad_norm_mix_pre': 'grad_w', 'grad_norm_mix_post': 'grad_w', 'grad_norm_ffn_pre': 'grad_w', 'grad_norm_ffn_post': 'grad_w', 'delta_w_in': 'delta_w', 'delta_w_proj_a': 'delta_w', 'delta_w_proj_b': 'delta_w', 'delta_w_out': 'delta_w', 'delta_b_forget': 'delta_w', 'delta_w_ffn_gate': 'delta_w', 'delta_w_ffn_up': 'delta_w', 'delta_w_ffn_down': 'delta_w', 'delta_norm_mix_pre': 'delta_w', 'delta_norm_mix_post': 'delta_w', 'delta_norm_ffn_pre': 'delta_w', 'delta_norm_ffn_post': 'delta_w', 'new_m_w_in': 'new_m', 'new_m_w_proj_a': 'new_m', 'new_m_w_proj_b': 'new_m', 'new_m_w_out': 'new_m', 'new_m_b_forget': 'new_m', 'new_m_w_ffn_gate': 'new_m', 'new_m_w_ffn_up': 'new_m', 'new_m_w_ffn_down': 'new_m', 'new_m_norm_mix_pre': 'new_m', 'new_m_norm_mix_post': 'new_m', 'new_m_norm_ffn_pre': 'new_m', 'new_m_norm_ffn_post': 'new_m', 'new_v_w_in': 'new_v', 'new_v_w_proj_a': 'new_v', 'new_v_w_proj_b': 'new_v', 'new_v_w_out': 'new_v', 'new_v_b_forget': 'new_v', 'new_v_w_ffn_gate': 'new_v', 'new_v_w_ffn_up': 'new_v', 'new_v_w_ffn_down': 'new_v', 'new_v_norm_mix_pre': 'new_v', 'new_v_norm_mix_post': 'new_v', 'new_v_norm_ffn_pre': 'new_v', 'new_v_norm_ffn_post': 'new_v'}


def _forward(args):
    return _fwd_reference(*[args[k] for k in FWD_PARAMS])


def _output_shape():
    out = _jax.eval_shape(lambda: _forward(_fwd_setup_inputs(0)))
    return out.shape, out.dtype

N_MICROBATCH = 1
ADAM_LR = 0.001
ADAM_B1 = 0.9
ADAM_B2 = 0.999
ADAM_EPS = 1e-08
ADAM_WD = 0.01
ADAM_STEP = 10
PER_EXAMPLE_BATCH_AXIS = {'x': 0, 'loss_target': 0}
SHARED_INPUTS = []
_WEIGHT_DTYPES = {'w_in': _jnp.float32, 'w_proj_a': _jnp.float32, 'w_proj_b': _jnp.float32, 'w_out': _jnp.float32, 'b_forget': _jnp.float32, 'w_ffn_gate': _jnp.float32, 'w_ffn_up': _jnp.float32, 'w_ffn_down': _jnp.float32, 'norm_mix_pre': _jnp.float32, 'norm_mix_post': _jnp.float32, 'norm_ffn_pre': _jnp.float32, 'norm_ffn_post': _jnp.float32}
MOMENT_SCALE = {'w_in': 2.239852e-01, 'w_proj_a': 1.711098e-01, 'w_proj_b': 3.918288e-01, 'w_out': 4.193627e-01, 'b_forget': 2.085337e+00, 'w_ffn_gate': 1.403608e-01, 'w_ffn_up': 2.002730e-01, 'w_ffn_down': 3.325671e-01, 'norm_mix_pre': 5.695830e-01, 'norm_mix_post': 1.600195e+01, 'norm_ffn_pre': 3.917467e-01, 'norm_ffn_post': 1.606835e+01}


def _to_microbatches(a, axis):
    t = _jnp.moveaxis(a, axis, 0)
    t = t.reshape((N_MICROBATCH, t.shape[0] // N_MICROBATCH) + t.shape[1:])
    return _jnp.moveaxis(t, 1, axis + 1)


def setup_inputs(seed: int = 0) -> dict:
    inp = _fwd_setup_inputs(seed)
    key = _jax.random.fold_in(_jax.random.key(seed), 7919)
    shape, _ = _output_shape()
    out = dict(inp)
    out["loss_target"] = _jax.random.normal(_jax.random.fold_in(key, 0), shape, _jnp.float32)
    for i, name in enumerate(TWIN_WEIGHTS):
        w = inp[name].astype(_jnp.float32)
        if MOMENT_SCALE is None:
            s = _jnp.sqrt(_jnp.mean(_jnp.square(w)) + 1e-30)
        else:
            s = MOMENT_SCALE[name]
        km, kv = _jax.random.split(_jax.random.fold_in(key, i + 1))
        out[name] = w
        out["m_" + name] = s * _jax.random.normal(km, w.shape, _jnp.float32)
        out["v_" + name] = (s * s) * _jax.random.uniform(kv, w.shape, _jnp.float32, 0.5, 1.5)
    if N_MICROBATCH > 1:
        for name, axis in PER_EXAMPLE_BATCH_AXIS.items():
            out[name] = _to_microbatches(out[name], axis)
    return {'x': out['x'], 'w_in': out['w_in'], 'w_proj_a': out['w_proj_a'], 'w_proj_b': out['w_proj_b'], 'w_out': out['w_out'], 'b_forget': out['b_forget'], 'w_ffn_gate': out['w_ffn_gate'], 'w_ffn_up': out['w_ffn_up'], 'w_ffn_down': out['w_ffn_down'], 'norm_mix_pre': out['norm_mix_pre'], 'norm_mix_post': out['norm_mix_post'], 'norm_ffn_pre': out['norm_ffn_pre'], 'norm_ffn_post': out['norm_ffn_post'], 'loss_target': out['loss_target'], 'm_w_in': out['m_w_in'], 'm_w_proj_a': out['m_w_proj_a'], 'm_w_proj_b': out['m_w_proj_b'], 'm_w_out': out['m_w_out'], 'm_b_forget': out['m_b_forget'], 'm_w_ffn_gate': out['m_w_ffn_gate'], 'm_w_ffn_up': out['m_w_ffn_up'], 'm_w_ffn_down': out['m_w_ffn_down'], 'm_norm_mix_pre': out['m_norm_mix_pre'], 'm_norm_mix_post': out['m_norm_mix_post'], 'm_norm_ffn_pre': out['m_norm_ffn_pre'], 'm_norm_ffn_post': out['m_norm_ffn_post'], 'v_w_in': out['v_w_in'], 'v_w_proj_a': out['v_w_proj_a'], 'v_w_proj_b': out['v_w_proj_b'], 'v_w_out': out['v_w_out'], 'v_b_forget': out['v_b_forget'], 'v_w_ffn_gate': out['v_w_ffn_gate'], 'v_w_ffn_up': out['v_w_ffn_up'], 'v_w_ffn_down': out['v_w_ffn_down'], 'v_norm_mix_pre': out['v_norm_mix_pre'], 'v_norm_mix_post': out['v_norm_mix_post'], 'v_norm_ffn_pre': out['v_norm_ffn_pre'], 'v_norm_ffn_post': out['v_norm_ffn_post']}


def _loss(weights, diff, rest, loss_target):
    with _jax.named_scope("forward"):
        args = {**rest, TWIN_DIFF_INPUT: diff, **{k: w.astype(_WEIGHT_DTYPES[k]) for k, w in weights.items()}}
        y = _forward(args)
    with _jax.named_scope("loss_head"):
        err = _jnp.square(y.astype(_jnp.float32) - loss_target)
        return 0.5 * _jnp.sum(_jnp.mean(err, axis=-1)) if err.ndim else 0.5 * err


def _adamw(w, g, m, v):
    m = ADAM_B1 * m + (1.0 - ADAM_B1) * g
    v = ADAM_B2 * v + (1.0 - ADAM_B2) * _jnp.square(g)
    m_hat = m / (1.0 - ADAM_B1 ** ADAM_STEP)
    v_hat = v / (1.0 - ADAM_B2 ** ADAM_STEP)
    delta = -ADAM_LR * (m_hat / (_jnp.sqrt(v_hat) + ADAM_EPS) + ADAM_WD * w)
    return delta, m, v


def reference(x, w_in, w_proj_a, w_proj_b, w_out, b_forget, w_ffn_gate, w_ffn_up, w_ffn_down, norm_mix_pre, norm_mix_post, norm_ffn_pre, norm_ffn_post, loss_target, m_w_in, m_w_proj_a, m_w_proj_b, m_w_out, m_b_forget, m_w_ffn_gate, m_w_ffn_up, m_w_ffn_down, m_norm_mix_pre, m_norm_mix_post, m_norm_ffn_pre, m_norm_ffn_post, v_w_in, v_w_proj_a, v_w_proj_b, v_w_out, v_b_forget, v_w_ffn_gate, v_w_ffn_up, v_w_ffn_down, v_norm_mix_pre, v_norm_mix_post, v_norm_ffn_pre, v_norm_ffn_post):
    given = dict(x=x, w_in=w_in, w_proj_a=w_proj_a, w_proj_b=w_proj_b, w_out=w_out, b_forget=b_forget, w_ffn_gate=w_ffn_gate, w_ffn_up=w_ffn_up, w_ffn_down=w_ffn_down, norm_mix_pre=norm_mix_pre, norm_mix_post=norm_mix_post, norm_ffn_pre=norm_ffn_pre, norm_ffn_post=norm_ffn_post, loss_target=loss_target, m_w_in=m_w_in, m_w_proj_a=m_w_proj_a, m_w_proj_b=m_w_proj_b, m_w_out=m_w_out, m_b_forget=m_b_forget, m_w_ffn_gate=m_w_ffn_gate, m_w_ffn_up=m_w_ffn_up, m_w_ffn_down=m_w_ffn_down, m_norm_mix_pre=m_norm_mix_pre, m_norm_mix_post=m_norm_mix_post, m_norm_ffn_pre=m_norm_ffn_pre, m_norm_ffn_post=m_norm_ffn_post, v_w_in=v_w_in, v_w_proj_a=v_w_proj_a, v_w_proj_b=v_w_proj_b, v_w_out=v_w_out, v_b_forget=v_b_forget, v_w_ffn_gate=v_w_ffn_gate, v_w_ffn_up=v_w_ffn_up, v_w_ffn_down=v_w_ffn_down, v_norm_mix_pre=v_norm_mix_pre, v_norm_mix_post=v_norm_mix_post, v_norm_ffn_pre=v_norm_ffn_pre, v_norm_ffn_post=v_norm_ffn_post)
    weights = {n: given[n] for n in TWIN_WEIGHTS}
    shared = {n: given[n] for n in SHARED_INPUTS}
    per_example = {n: given[n] for n in ['x']}
    grad_fn = _jax.value_and_grad(_loss, argnums=(0, 1))

    def one_microbatch(ex, loss_target):
        ex = dict(ex)
        diff = ex.pop(TWIN_DIFF_INPUT)
        return grad_fn(weights, diff, {**shared, **ex}, loss_target)

    if N_MICROBATCH == 1:
        loss, (grad_w, grad_x) = one_microbatch(per_example, given["loss_target"])
    else:
        def body(carry, xs):
            loss_sum, grad_sum = carry
            l_k, (gw_k, gx_k) = one_microbatch(xs[0], xs[1])
            with _jax.named_scope("update"):
                return (loss_sum + l_k, _jax.tree.map(_jnp.add, grad_sum, gw_k)), gx_k

        init = (_jnp.zeros((), _jnp.float32), _jax.tree.map(_jnp.zeros_like, weights))
        (loss, grad_w), grad_x = _jax.lax.scan(body, init, (per_example, given["loss_target"]))
    with _jax.named_scope("update"):
        delta_w, new_m, new_v = {}, {}, {}
        for n in TWIN_WEIGHTS:
            delta_w[n], new_m[n], new_v[n] = _adamw(weights[n], grad_w[n], given["m_" + n], given["v_" + n])
    return (loss, grad_x, *[grad_w[n] for n in TWIN_WEIGHTS], *[delta_w[n] for n in TWIN_WEIGHTS],
            *[new_m[n] for n in TWIN_WEIGHTS], *[new_v[n] for n in TWIN_WEIGHTS])
```

```python
import functools
import math

import jax
import jax.numpy as jnp
from jax import lax
from jax.experimental import pallas as pl
from jax.experimental.pallas import tpu as pltpu

f32 = jnp.float32
bf16 = jnp.bfloat16
SDS = jax.ShapeDtypeStruct
MESH = pl.DeviceIdType.MESH

S = 2048
D = 1024
HD = 64
BLK = 128
N_FOX = 8
FOX_W = N_FOX * HD
DIL_GROUPS = ((128, 1), (512, 4), (2048, 16))
SLOTS = 4
DIL_W = SLOTS * HD
QK_W = 2 * 3 * DIL_W
VR_W = 3 * FOX_W + 3 * DIL_W
GF_W = 2 * D + 128
F_FF = 2816
ROPE_DIM = 16
ROPE_THETA = 500000.0
EPS = 1e-6
NEG = -1e30
SCALE = 1.0 / math.sqrt(HD)
IN_COLS = 5896
N_SHARD = 4

ADAM_LR, ADAM_B1, ADAM_B2, ADAM_EPS, ADAM_WD, ADAM_STEP = 0.001, 0.9, 0.999, 1e-08, 0.01, 10

VMEM_V7X = 64 * 1024 * 1024
VMEM_PLAN_MAX = VMEM_V7X - 8 * 1024 * 1024

TM = 256
TQ = 256

FLAT_ROWS = (("w_in", 1474), ("w_proj_a", 64), ("w_proj_b", 128), ("w_out", 256),
             ("w_ffn_gate", 704), ("w_ffn_up", 704), ("w_ffn_down", 704))
SLOT_ALIGN = 16


def _slot(r):
    return -(-r // SLOT_ALIGN) * SLOT_ALIGN


FLAT_USED = sum(_slot(r) for _, r in FLAT_ROWS)
FLAT_PAD = 4096
HALF = FLAT_PAD // 2
N_CHUNK = 4
CHUNK = HALF // N_CHUNK
SMALL_ROWS = 8


def _nbytes(shape, dtype):
    return math.prod(shape) * jnp.dtype(dtype).itemsize


def _params(semantics, block_bytes, temp_bytes=0):
    need = 2 * block_bytes + temp_bytes + (2 << 20)
    return pltpu.CompilerParams(dimension_semantics=semantics,
                                vmem_limit_bytes=int(min(max(need, 16 << 20), VMEM_PLAN_MAX)))


def _row(w, tm=TM):
    return pl.BlockSpec((tm, w), lambda i: (i, 0))


def _vec(w):
    return pl.BlockSpec((1, w), lambda i: (0, 0))


def _mm(pairs, dims, out_dtype, *, tm, tn, name, m_inner=False):
    a0, b0 = pairs[0]
    m_dim = a0.shape[1] if dims == "tn" else a0.shape[0]
    n_dim = b0.shape[0] if dims == "nt" else b0.shape[1]
    contract = {"nn": ((1,), (0,)), "nt": ((1,), (1,)), "tn": ((0,), (0,))}[dims]
    n_pairs = len(pairs)
    assert m_dim % tm == 0 and n_dim % tn == 0, (name, m_dim, n_dim, tm, tn)

    def body(*refs):
        o_ref = refs[-1]
        acc = None
        for p in range(n_pairs):
            a = refs[2 * p][...].astype(bf16)
            b = refs[2 * p + 1][...].astype(bf16)
            t = lax.dot_general(a, b, (contract, ((), ())), preferred_element_type=f32)
            acc = t if acc is None else acc + t
        o_ref[...] = acc.astype(o_ref.dtype)

    if m_inner:
        grid = (n_dim // tn, m_dim // tm)
        mi = lambda j, i: i
        ni = lambda j, i: j
    else:
        grid = (m_dim // tm, n_dim // tn)
        mi = lambda i, j: i
        ni = lambda i, j: j
    in_specs, block_bytes, args = [], 0, []
    for a, b in pairs:
        k_dim = a.shape[0] if dims == "tn" else a.shape[1]
        if dims == "tn":
            in_specs.append(pl.BlockSpec((k_dim, tm), lambda *g: (0, mi(*g))))
        else:
            in_specs.append(pl.BlockSpec((tm, k_dim), lambda *g: (mi(*g), 0)))
        if dims == "nt":
            in_specs.append(pl.BlockSpec((tn, k_dim), lambda *g: (ni(*g), 0)))
        else:
            in_specs.append(pl.BlockSpec((k_dim, tn), lambda *g: (0, ni(*g))))
        block_bytes += _nbytes((tm, k_dim), a.dtype) + _nbytes((tn, k_dim), b.dtype)
        args += [a, b]
    block_bytes += _nbytes((tm, tn), out_dtype)
    temp = _nbytes((tm, tn), f32) * 2 + sum(_nbytes((tm, a.shape[0] if dims == "tn" else a.shape[1]), bf16)
                                            + _nbytes((tn, a.shape[0] if dims == "tn" else a.shape[1]), bf16)
                                            for a, _ in pairs)
    return pl.pallas_call(
        body, grid=grid, in_specs=in_specs,
        out_specs=pl.BlockSpec((tm, tn), lambda *g: (mi(*g), ni(*g))),
        out_shape=SDS((m_dim, n_dim), out_dtype), name=name,
        compiler_params=_params(("parallel", "parallel"), block_bytes, temp),
    )(*args)


def _rms(x, g):
    r = lax.rsqrt(jnp.mean(x * x, axis=-1, keepdims=True) + EPS)
    return x * r * g


def _rms_bwd(x, g, dy):
    r = lax.rsqrt(jnp.mean(x * x, axis=-1, keepdims=True) + EPS)
    xh = x * r
    dxh = dy * g
    dx = r * (dxh - xh * jnp.mean(dxh * xh, axis=-1, keepdims=True))
    return dx, jnp.sum(dy * xh, axis=0, keepdims=True)


def _acc_rows(ref, val):
    @pl.when(pl.program_id(0) == 0)
    def _():
        ref[...] = jnp.zeros_like(ref)
    ref[...] += val


def _norm_fwd(x, g):
    def body(x_ref, g_ref, h_ref):
        h_ref[...] = _rms(x_ref[...], g_ref[...]).astype(bf16)

    return pl.pallas_call(
        body, grid=(S // TM,), in_specs=[_row(D), _vec(D)], out_specs=_row(D), out_shape=SDS((S, D), bf16),
        name="norm_mix_pre", compiler_params=_params(("parallel",), 6 * TM * D, 8 * TM * D))(x, g)


def _resid_norm_fwd(x, mix, g_post, g_pre):
    def body(x_ref, mix_ref, gp_ref, gn_ref, x2_ref, h_ref):
        x2 = x_ref[...] + _rms(mix_ref[...], gp_ref[...])
        x2_ref[...] = x2
        h_ref[...] = _rms(x2, gn_ref[...]).astype(bf16)

    return pl.pallas_call(
        body, grid=(S // TM,), in_specs=[_row(D), _row(D), _vec(D), _vec(D)], out_specs=[_row(D), _row(D)],
        out_shape=[SDS((S, D), f32), SDS((S, D), bf16)], name="resid_norm_mid",
        compiler_params=_params(("parallel",), 14 * TM * D, 16 * TM * D))(x, mix, g_post, g_pre)


def _loss_head(x2, ff, g_post, target):
    def body(x2_ref, ff_ref, g_ref, t_ref, loss_ref, dy_ref, dff_ref, dg_ref):
        ff = ff_ref[...]
        g = g_ref[...]
        err = x2_ref[...] + _rms(ff, g) - t_ref[...]
        dy = err * (1.0 / D)
        dff, dg = _rms_bwd(ff, g, dy)
        dy_ref[...] = dy
        dff_ref[...] = dff.astype(bf16)
        _acc_rows(dg_ref, dg)
        _acc_rows(loss_ref, jnp.full((1, 128), jnp.sum(err * err), f32))

    return pl.pallas_call(
        body, grid=(S // TM,), in_specs=[_row(D), _row(D), _vec(D), _row(D)],
        out_specs=[_vec(128), _row(D), _row(D), _vec(D)],
        out_shape=[SDS((1, 128), f32), SDS((S, D), f32), SDS((S, D), bf16), SDS((1, D), f32)], name="loss_head",
        compiler_params=_params(("arbitrary",), 18 * TM * D, 24 * TM * D))(x2, ff, g_post, target)


def _norm_bwd_mid(dy, dh3, x2, mix, g_ffn_pre, g_mix_post):
    def body(dy_ref, dh_ref, x2_ref, mix_ref, g3_ref, g2_ref, dx2_ref, dmix_ref, dg3_ref, dg2_ref):
        d3, dg3 = _rms_bwd(x2_ref[...], g3_ref[...], dh_ref[...])
        dx2 = dy_ref[...] + d3
        dmix, dg2 = _rms_bwd(mix_ref[...], g2_ref[...], dx2)
        dx2_ref[...] = dx2
        dmix_ref[...] = dmix.astype(bf16)
        _acc_rows(dg3_ref, dg3)
        _acc_rows(dg2_ref, dg2)

    return pl.pallas_call(
        body, grid=(S // TM,), in_specs=[_row(D)] * 4 + [_vec(D)] * 2,
        out_specs=[_row(D), _row(D), _vec(D), _vec(D)],
        out_shape=[SDS((S, D), f32), SDS((S, D), bf16), SDS((1, D), f32), SDS((1, D), f32)], name="norm_bwd_mid",
        compiler_params=_params(("arbitrary",), 22 * TM * D, 24 * TM * D))(dy, dh3, x2, mix, g_ffn_pre, g_mix_post)


def _norm_bwd_in(dx2, dh1, x, g):
    def body(dx2_ref, dh_ref, x_ref, g_ref, gx_ref, dg_ref):
        d1, dg = _rms_bwd(x_ref[...], g_ref[...], dh_ref[...])
        gx_ref[...] = dx2_ref[...] + d1
        _acc_rows(dg_ref, dg)

    return pl.pallas_call(
        body, grid=(S // TM,), in_specs=[_row(D)] * 3 + [_vec(D)], out_specs=[_row(D), _vec(D)],
        out_shape=[SDS((S, D), f32), SDS((1, D), f32)], name="norm_bwd_in",
        compiler_params=_params(("arbitrary",), 16 * TM * D, 16 * TM * D))(dx2, dh1, x, g)


def _rope_tables():
    half = ROPE_DIM // 2
    inv_freq = jnp.power(ROPE_THETA, -jnp.arange(0, ROPE_DIM, 2, dtype=f32) / ROPE_DIM)
    ang = jnp.arange(S, dtype=f32)[:, None] * inv_freq[None, :]
    cos, sin = jnp.cos(ang), jnp.sin(ang)
    ones = jnp.ones((S, HD - ROPE_DIM), f32)
    zeros = jnp.zeros((S, HD - half), f32)
    c = jnp.concatenate([cos, cos, ones], axis=1)
    s_lo = jnp.concatenate([-sin, zeros], axis=1)
    s_hi = jnp.concatenate([jnp.zeros((S, half), f32), sin, jnp.zeros((S, HD - ROPE_DIM), f32)], axis=1)
    two = lambda t: jnp.concatenate([t, t], axis=1)
    return two(c), two(s_lo), two(s_hi)


def _rope(x, tables, name, transpose=False):
    c, s_lo, s_hi = tables
    sign = -1.0 if transpose else 1.0
    reps = QK_W // 128

    def body(x_ref, c_ref, lo_ref, hi_ref, o_ref):
        x = x_ref[...]
        c = jnp.tile(c_ref[...], (1, reps))
        lo = jnp.tile(lo_ref[...], (1, reps)) * sign
        hi = jnp.tile(hi_ref[...], (1, reps)) * sign
        o_ref[...] = (x * c + pltpu.roll(x, QK_W - ROPE_DIM // 2, 1) * lo + pltpu.roll(x, ROPE_DIM // 2, 1) * hi).astype(bf16)

    return pl.pallas_call(
        body, grid=(S // TM,), in_specs=[_row(QK_W), _row(128), _row(128), _row(128)], out_specs=_row(QK_W),
        out_shape=SDS((S, QK_W), bf16), name=name,
        compiler_params=_params(("parallel",), 6 * TM * QK_W + 12 * TM * 128, 24 * TM * QK_W))(x, c, s_lo, s_hi)


def _dil_masks(n_is_first):
    qi = lax.broadcasted_iota(jnp.int32, (BLK, BLK), 0)
    kj = lax.broadcasted_iota(jnp.int32, (BLK, BLK), 1)
    cur = kj <= qi
    prev = kj >= qi + jnp.where(n_is_first, BLK + 1, 0)
    return cur, prev


def _nt(a, b):
    return lax.dot_general(a, b, (((1,), (1,)), ((), ())), preferred_element_type=f32)


def _tn(a, b):
    return lax.dot_general(a, b, (((0,), (0,)), ((), ())), preferred_element_type=f32)


def _dil_views(d, qk, vr):
    return qk.reshape(S // d, d * QK_W), vr.reshape(S // d, d * VR_W)


def _dil_fwd(g, qk, vr):
    _, d = DIL_GROUPS[g]
    nb = S // d // BLK
    qk_v, vr_v = _dil_views(d, qk, vr)
    qc, vc = QK_W // DIL_W, VR_W // DIL_W
    blk = (BLK, DIL_W)

    def body(q_ref, kc_ref, kp_ref, vc_ref, vp_ref, o_ref, lse_ref):
        cur, prev = _dil_masks(pl.program_id(1) == 0)
        for h in range(SLOTS):
            hs = slice(h * HD, (h + 1) * HD)
            q = q_ref[:, hs]
            sc = jnp.where(cur, _nt(q, kc_ref[:, hs]) * SCALE, NEG)
            sp = jnp.where(prev, _nt(q, kp_ref[:, hs]) * SCALE, NEG)
            m = jnp.maximum(jnp.max(sc, axis=-1, keepdims=True), jnp.max(sp, axis=-1, keepdims=True))
            pc, pp = jnp.exp(sc - m), jnp.exp(sp - m)
            den = jnp.sum(pc, axis=-1, keepdims=True) + jnp.sum(pp, axis=-1, keepdims=True)
            inv = 1.0 / den
            o = jnp.dot((pc * inv).astype(bf16), vc_ref[:, hs], preferred_element_type=f32)
            o += jnp.dot((pp * inv).astype(bf16), vp_ref[:, hs], preferred_element_type=f32)
            o_ref[:, hs] = o
            lse_ref[:, hs] = jnp.broadcast_to(m + jnp.log(den), (BLK, HD))

    pm = lambda n: jnp.maximum(n - 1, 0)
    in_specs = [
        pl.BlockSpec(blk, lambda r, n: (n, r * qc + g)),
        pl.BlockSpec(blk, lambda r, n: (n, r * qc + 3 + g)),
        pl.BlockSpec(blk, lambda r, n: (pm(n), r * qc + 3 + g)),
        pl.BlockSpec(blk, lambda r, n: (n, r * vc + 6 + g)),
        pl.BlockSpec(blk, lambda r, n: (pm(n), r * vc + 6 + g)),
    ]
    out_spec = pl.BlockSpec(blk, lambda r, n: (n, r))
    o, lse = pl.pallas_call(
        body, grid=(d, nb), in_specs=in_specs, out_specs=[out_spec, out_spec],
        out_shape=[SDS((S // d, d * DIL_W), f32)] * 2, name=f"dil_fwd_{g}",
        compiler_params=_params(("parallel", "parallel"), 18 * BLK * DIL_W, 1 << 20))(qk_v, qk_v, qk_v, vr_v, vr_v)
    return o.reshape(S, DIL_W), lse.reshape(S, DIL_W)


def _dil_combine(outs, lses):
    def body(o0, o1, o2, l0, l1, l2, out_ref, lse_ref):
        a, b, c = l0[...], l1[...], l2[...]
        m = jnp.maximum(jnp.maximum(a, b), c)
        ea, eb, ec = jnp.exp(a - m), jnp.exp(b - m), jnp.exp(c - m)
        z = ea + eb + ec
        inv = 1.0 / z
        out_ref[...] = (ea * inv) * o0[...] + (eb * inv) * o1[...] + (ec * inv) * o2[...]
        lse_ref[...] = m + jnp.log(z)

    return pl.pallas_call(
        body, grid=(S // TM,), in_specs=[_row(DIL_W)] * 6, out_specs=[_row(DIL_W)] * 2,
        out_shape=[SDS((S, DIL_W), f32)] * 2, name="dil_combine",
        compiler_params=_params(("parallel",), 32 * TM * DIL_W, 32 * TM * DIL_W))(*outs, *lses)


def _dil_probs(q, k, mask, lse, do, v, delta):
    s = jnp.where(mask, _nt(q, k) * SCALE, NEG)
    p = jnp.exp(s - lse)
    ds = p * (_nt(do, v) - delta) * SCALE
    return p, ds


def _dil_bwd_q(g, qk, vr, d_out, out, lse):
    _, d = DIL_GROUPS[g]
    nb = S // d // BLK
    qk_v, vr_v = _dil_views(d, qk, vr)
    aux = lambda t: t.reshape(S // d, d * DIL_W)
    qc, vc = QK_W // DIL_W, VR_W // DIL_W
    blk = (BLK, DIL_W)

    def body(q_ref, kc_ref, kp_ref, vc_ref, vp_ref, do_ref, o_ref, lse_ref, dq_ref):
        cur, prev = _dil_masks(pl.program_id(1) == 0)
        for h in range(SLOTS):
            hs = slice(h * HD, (h + 1) * HD)
            q, do = q_ref[:, hs], do_ref[:, hs]
            lse = lse_ref[:, h * HD:h * HD + 1]
            delta = jnp.sum(do.astype(f32) * o_ref[:, hs], axis=-1, keepdims=True)
            _, dsc = _dil_probs(q, kc_ref[:, hs], cur, lse, do, vc_ref[:, hs], delta)
            _, dsp = _dil_probs(q, kp_ref[:, hs], prev, lse, do, vp_ref[:, hs], delta)
            dq = jnp.dot(dsc.astype(bf16), kc_ref[:, hs], preferred_element_type=f32)
            dq += jnp.dot(dsp.astype(bf16), kp_ref[:, hs], preferred_element_type=f32)
            dq_ref[:, hs] = dq

    pm = lambda n: jnp.maximum(n - 1, 0)
    own = pl.BlockSpec(blk, lambda r, n: (n, r))
    in_specs = [
        pl.BlockSpec(blk, lambda r, n: (n, r * qc + g)),
        pl.BlockSpec(blk, lambda r, n: (n, r * qc + 3 + g)),
        pl.BlockSpec(blk, lambda r, n: (pm(n), r * qc + 3 + g)),
        pl.BlockSpec(blk, lambda r, n: (n, r * vc + 6 + g)),
        pl.BlockSpec(blk, lambda r, n: (pm(n), r * vc + 6 + g)),
        own, own, own,
    ]
    dq = pl.pallas_call(
        body, grid=(d, nb), in_specs=in_specs, out_specs=own, out_shape=SDS((S // d, d * DIL_W), f32),
        name=f"dil_bwd_q_{g}", compiler_params=_params(("parallel", "parallel"), 24 * BLK * DIL_W, 1 << 20),
    )(qk_v, qk_v, qk_v, vr_v, vr_v, aux(d_out), aux(out), aux(lse))
    return dq.reshape(S, DIL_W)


def _dil_bwd_kv(g, qk, vr, d_out, out, lse):
    _, d = DIL_GROUPS[g]
    nb = S // d // BLK
    qk_v, vr_v = _dil_views(d, qk, vr)
    aux = lambda t: t.reshape(S // d, d * DIL_W)
    qc, vc = QK_W // DIL_W, VR_W // DIL_W
    blk = (BLK, DIL_W)

    def body(k_ref, v_ref, qc_ref, qn_ref, doc_ref, don_ref, oc_ref, on_ref, lc_ref, ln_ref, dk_ref, dv_ref):
        is_last = pl.program_id(1) == nb - 1
        qi = lax.broadcasted_iota(jnp.int32, (BLK, BLK), 0)
        kj = lax.broadcasted_iota(jnp.int32, (BLK, BLK), 1)
        cur = kj <= qi
        nxt = kj >= qi + jnp.where(is_last, BLK + 1, 0)
        for h in range(SLOTS):
            hs = slice(h * HD, (h + 1) * HD)
            k, v = k_ref[:, hs], v_ref[:, hs]
            dk = jnp.zeros((BLK, HD), f32)
            dv = jnp.zeros((BLK, HD), f32)
            for q_ref, do_ref, o_ref, l_ref, mask in ((qc_ref, doc_ref, oc_ref, lc_ref, cur),
                                                      (qn_ref, don_ref, on_ref, ln_ref, nxt)):
                q, do = q_ref[:, hs], do_ref[:, hs]
                delta = jnp.sum(do.astype(f32) * o_ref[:, hs], axis=-1, keepdims=True)
                p, ds = _dil_probs(q, k, mask, l_ref[:, h * HD:h * HD + 1], do, v, delta)
                dv += _tn(p.astype(bf16), do)
                dk += _tn(ds.astype(bf16), q)
            dk_ref[:, hs] = dk
            dv_ref[:, hs] = dv

    nx = lambda n: jnp.minimum(n + 1, nb - 1)
    own = pl.BlockSpec(blk, lambda r, n: (n, r))
    nxt_spec = pl.BlockSpec(blk, lambda r, n: (nx(n), r))
    in_specs = [
        pl.BlockSpec(blk, lambda r, n: (n, r * qc + 3 + g)),
        pl.BlockSpec(blk, lambda r, n: (n, r * vc + 6 + g)),
        pl.BlockSpec(blk, lambda r, n: (n, r * qc + g)),
        pl.BlockSpec(blk, lambda r, n: (nx(n), r * qc + g)),
        own, nxt_spec, own, nxt_spec, own, nxt_spec,
    ]
    dk, dv = pl.pallas_call(
        body, grid=(d, nb), in_specs=in_specs, out_specs=[own, own], out_shape=[SDS((S // d, d * DIL_W), f32)] * 2,
        name=f"dil_bwd_kv_{g}", compiler_params=_params(("parallel", "parallel"), 36 * BLK * DIL_W, 1 << 20),
    )(qk_v, vr_v, qk_v, qk_v, aux(d_out), aux(d_out), aux(out), aux(out), aux(lse), aux(lse))
    return dk.reshape(S, DIL_W), dv.reshape(S, DIL_W)


def _scan_rows(x, reverse):
    row = lax.broadcasted_iota(jnp.int32, x.shape, 0)
    k = 1
    while k < S:
        if reverse:
            x = x + jnp.where(row < S - k, pltpu.roll(x, S - k, 0), 0.0)
        else:
            x = x + jnp.where(row >= k, pltpu.roll(x, k, 0), 0.0)
        k *= 2
    return x


def _forget_fwd(gf, b128):
    def body(z_ref, b_ref, f_ref):
        z = z_ref[...] + b_ref[...]
        logf = jnp.minimum(z, 0.0) - jnp.log1p(jnp.exp(-jnp.abs(z)))
        f_ref[...] = _scan_rows(logf, reverse=False)

    return pl.pallas_call(
        body, grid=(1,), in_specs=[pl.BlockSpec((S, 128), lambda i: (0, 2 * D // 128)), _vec(128)],
        out_specs=pl.BlockSpec((S, 128), lambda i: (0, 0)), out_shape=SDS((S, 128), f32), name="forget_fwd",
        compiler_params=_params(("arbitrary",), 8 * S * 128, 16 * S * 128))(gf, b128)


def _forget_bwd(gf, b128, d_f_cols, d_f_rows):
    def body(z_ref, b_ref, dfc_ref, dfr_ref, dz_ref, db_ref):
        z = z_ref[...] + b_ref[...]
        dz = _scan_rows(dfc_ref[...] + dfr_ref[...], reverse=True) * jax.nn.sigmoid(-z)
        dz_ref[...] = dz
        db_ref[...] = jnp.sum(dz, axis=0, keepdims=True)

    full = pl.BlockSpec((S, 128), lambda i: (0, 0))
    return pl.pallas_call(
        body, grid=(1,), in_specs=[pl.BlockSpec((S, 128), lambda i: (0, 2 * D // 128)), _vec(128), full, full],
        out_specs=[full, _vec(128)], out_shape=[SDS((S, 128), f32), SDS((1, 128), f32)], name="forget_bwd",
        compiler_params=_params(("arbitrary",), 16 * S * 128, 16 * S * 128))(gf, b128, d_f_cols, d_f_rows)


def _fox_mask(i, j):
    qpos = i * TQ + lax.broadcasted_iota(jnp.int32, (TQ, TQ), 0)
    kpos = j * TQ + lax.broadcasted_iota(jnp.int32, (TQ, TQ), 1)
    return kpos <= qpos


def _fox_fwd(vr, f_cum, f_cum_t):
    nq = S // TQ

    def body(q_ref, k_ref, v_ref, fq_ref, fk_ref, o_ref, lse_ref, m_sc, l_sc, acc_sc):
        i, j = pl.program_id(0), pl.program_id(1)

        @pl.when(j == 0)
        def _():
            m_sc[...] = jnp.full_like(m_sc, NEG)
            l_sc[...] = jnp.zeros_like(l_sc)
            acc_sc[...] = jnp.zeros_like(acc_sc)

        @pl.when(j <= i)
        def _():
            mask = _fox_mask(i, j)
            for h in range(N_FOX):
                hs = slice(h * HD, (h + 1) * HD)
                s = _nt(q_ref[:, hs], k_ref[:, hs]) * SCALE + (fq_ref[:, h:h + 1] - fk_ref[h:h + 1, :])
                s = jnp.where(mask, s, NEG)
                m_old = m_sc[h]
                m_new = jnp.maximum(m_old, jnp.max(s, axis=-1, keepdims=True))
                alpha = jnp.exp(m_old - m_new)
                p = jnp.exp(s - m_new)
                l_sc[h] = alpha * l_sc[h] + jnp.sum(p, axis=-1, keepdims=True)
                acc_sc[:, hs] = alpha * acc_sc[:, hs] + jnp.dot(p.astype(bf16), v_ref[:, hs], preferred_element_type=f32)
                m_sc[h] = m_new

        @pl.when(j == i)
        def _():
            lse_ref[...] = jnp.zeros_like(lse_ref)
            for h in range(N_FOX):
                hs = slice(h * HD, (h + 1) * HD)
                o_ref[:, hs] = acc_sc[:, hs] * (1.0 / l_sc[h])
                lse_ref[:, h:h + 1] = m_sc[h] + jnp.log(l_sc[h])

    jm = lambda i, j: jnp.minimum(i, j)
    in_specs = [
        pl.BlockSpec((TQ, FOX_W), lambda i, j: (i, 0)),
        pl.BlockSpec((TQ, FOX_W), lambda i, j: (jm(i, j), 1)),
        pl.BlockSpec((TQ, FOX_W), lambda i, j: (jm(i, j), 2)),
        pl.BlockSpec((TQ, 128), lambda i, j: (i, 0)),
        pl.BlockSpec((N_FOX, TQ), lambda i, j: (0, jm(i, j))),
    ]
    return pl.pallas_call(
        body, grid=(nq, nq), in_specs=in_specs,
        out_specs=[pl.BlockSpec((TQ, FOX_W), lambda i, j: (i, 0)), pl.BlockSpec((TQ, 128), lambda i, j: (i, 0))],
        out_shape=[SDS((S, FOX_W), f32), SDS((S, 128), f32)],
        scratch_shapes=[pltpu.VMEM((N_FOX, TQ, 1), f32), pltpu.VMEM((N_FOX, TQ, 1), f32), pltpu.VMEM((TQ, FOX_W), f32)],
        name="fox_fwd", compiler_params=_params(("parallel", "arbitrary"), 12 * TQ * FOX_W, 8 << 20),
    )(vr, vr, vr, f_cum, f_cum_t)


def _fox_bwd(vr, f_cum, f_cum_t, lse, d_out, delta):
    nq = S // TQ

    def body(k_ref, v_ref, q_ref, do_ref, fq_ref, fk_ref, lse_ref, dl_ref, dq_ref, dk_ref, dv_ref, df_ref, dfr_ref):
        jb, i = pl.program_id(0), pl.program_id(1)

        @pl.when((jb == 0) & (i == 0))
        def _():
            dq_ref[...] = jnp.zeros_like(dq_ref)
            dfr_ref[...] = jnp.zeros_like(dfr_ref)

        @pl.when(i == 0)
        def _():
            dk_ref[...] = jnp.zeros_like(dk_ref)
            dv_ref[...] = jnp.zeros_like(dv_ref)
            df_ref[...] = jnp.zeros_like(df_ref)

        @pl.when(i >= jb)
        def _():
            mask = _fox_mask(i, jb)
            rows = pl.ds(pl.multiple_of(i * TQ, TQ), TQ)
            for h in range(N_FOX):
                hs = slice(h * HD, (h + 1) * HD)
                q, k, v, do = q_ref[:, hs], k_ref[:, hs], v_ref[:, hs], do_ref[:, hs]
                s = _nt(q, k) * SCALE + (fq_ref[:, h:h + 1] - fk_ref[h:h + 1, :])
                p = jnp.exp(jnp.where(mask, s, NEG) - lse_ref[:, h:h + 1])
                ds = p * (_nt(do, v) - dl_ref[:, h:h + 1])
                dsb = ds.astype(bf16)
                dv_ref[:, hs] += _tn(p.astype(bf16), do)
                dk_ref[:, hs] += _tn(dsb, q) * SCALE
                dq_ref[rows, hs] += jnp.dot(dsb, k, preferred_element_type=f32) * SCALE
                df_ref[h:h + 1, :] -= jnp.sum(ds, axis=0, keepdims=True)
                dfr_ref[rows, h:h + 1] += jnp.sum(ds, axis=-1, keepdims=True)

    im = lambda jb, i: jnp.maximum(jb, i)
    in_specs = [
        pl.BlockSpec((TQ, FOX_W), lambda jb, i: (jb, 1)),
        pl.BlockSpec((TQ, FOX_W), lambda jb, i: (jb, 2)),
        pl.BlockSpec((TQ, FOX_W), lambda jb, i: (im(jb, i), 0)),
        pl.BlockSpec((TQ, FOX_W), lambda jb, i: (im(jb, i), 0)),
        pl.BlockSpec((TQ, 128), lambda jb, i: (im(jb, i), 0)),
        pl.BlockSpec((N_FOX, TQ), lambda jb, i: (0, jb)),
        pl.BlockSpec((TQ, 128), lambda jb, i: (im(jb, i), 0)),
        pl.BlockSpec((TQ, 128), lambda jb, i: (im(jb, i), 0)),
    ]
    out_specs = [
        pl.BlockSpec((S, FOX_W), lambda jb, i: (0, 0)),
        pl.BlockSpec((TQ, FOX_W), lambda jb, i: (jb, 0)),
        pl.BlockSpec((TQ, FOX_W), lambda jb, i: (jb, 0)),
        pl.BlockSpec((N_FOX, TQ), lambda jb, i: (0, jb)),
        pl.BlockSpec((S, 128), lambda jb, i: (0, 0)),
    ]
    return pl.pallas_call(
        body, grid=(nq, nq), in_specs=in_specs, out_specs=out_specs,
        out_shape=[SDS((S, FOX_W), f32), SDS((S, FOX_W), f32), SDS((S, FOX_W), f32), SDS((N_FOX, S), f32),
                   SDS((S, 128), f32)],
        name="fox_bwd", compiler_params=_params(("arbitrary", "arbitrary"), 4 * S * (FOX_W + 128) + 20 * TQ * FOX_W, 8 << 20),
    )(vr, vr, vr, d_out, f_cum, f_cum_t, lse, delta)


def _merge_fwd(out_a, out_b, w_a, w_b, gf):
    def body(oa_ref, ob_ref, wa_ref, wb_ref, ga_ref, gb_ref, ya_ref, yb_ref, mg_ref):
        ya = jnp.dot(oa_ref[...].astype(bf16), wa_ref[...], preferred_element_type=f32)
        yb = jnp.dot(ob_ref[...].astype(bf16), wb_ref[...], preferred_element_type=f32)
        ya_ref[...] = ya
        yb_ref[...] = yb
        mg_ref[...] = (jax.nn.sigmoid(ga_ref[...]) * ya + jax.nn.sigmoid(gb_ref[...]) * yb).astype(bf16)

    full = lambda a: pl.BlockSpec(a.shape, lambda i: (0, 0))
    return pl.pallas_call(
        body, grid=(S // TM,),
        in_specs=[_row(DIL_W), _row(FOX_W), full(w_a), full(w_b), _row(D), pl.BlockSpec((TM, D), lambda i: (i, 1))],
        out_specs=[_row(D)] * 3, out_shape=[SDS((S, D), f32), SDS((S, D), f32), SDS((S, D), bf16)], name="merge_fwd",
        compiler_params=_params(("parallel",), 22 * TM * D + 2 * (DIL_W + FOX_W) * D, 16 * TM * D),
    )(out_a, out_b, w_a, w_b, gf, gf)


def _merge_bwd(d_merged, ya, yb, gf):
    def body(dm_ref, ya_ref, yb_ref, ga_ref, gb_ref, dya_ref, dyb_ref, dg_ref):
        dm = dm_ref[...]
        sa, sb = jax.nn.sigmoid(ga_ref[...]), jax.nn.sigmoid(gb_ref[...])
        dya_ref[...] = (dm * sa).astype(bf16)
        dyb_ref[...] = (dm * sb).astype(bf16)
        dg_ref[:, :D] = (dm * ya_ref[...] * sa * (1.0 - sa)).astype(bf16)
        dg_ref[:, D:] = (dm * yb_ref[...] * sb * (1.0 - sb)).astype(bf16)

    return pl.pallas_call(
        body, grid=(S // TM,),
        in_specs=[_row(D)] * 4 + [pl.BlockSpec((TM, D), lambda i: (i, 1))],
        out_specs=[_row(D), _row(D), _row(2 * D)],
        out_shape=[SDS((S, D), bf16), SDS((S, D), bf16), SDS((S, 2 * D), bf16)], name="merge_bwd",
        compiler_params=_params(("parallel",), 28 * TM * D, 24 * TM * D))(d_merged, ya, yb, gf, gf)


def _branch_bwd(d_ya, d_yb, w_a, w_b, out_a, out_b):
    def body(dya_ref, dyb_ref, wa_ref, wb_ref, oa_ref, ob_ref, doa_ref, dob_ref, dlb_ref):
        doa = _nt(dya_ref[...], wa_ref[...])
        dob = _nt(dyb_ref[...], wb_ref[...])
        doa_ref[...] = doa.astype(bf16)
        dob_ref[...] = dob.astype(bf16)
        prod = dob * ob_ref[...]
        dlb_ref[...] = jnp.zeros_like(dlb_ref)
        for h in range(N_FOX):
            dlb_ref[:, h:h + 1] = jnp.sum(prod[:, h * HD:(h + 1) * HD], axis=-1, keepdims=True)

    full = lambda a: pl.BlockSpec(a.shape, lambda i: (0, 0))
    return pl.pallas_call(
        body, grid=(S // TM,),
        in_specs=[_row(D), _row(D), full(w_a), full(w_b), _row(DIL_W), _row(FOX_W)],
        out_specs=[_row(DIL_W), _row(FOX_W), _row(128)],
        out_shape=[SDS((S, DIL_W), bf16), SDS((S, FOX_W), bf16), SDS((S, 128), f32)], name="branch_bwd",
        compiler_params=_params(("parallel",), 8 * TM * D + 2 * (DIL_W + FOX_W) * D, 8 * TM * D),
    )(d_ya, d_yb, w_a, w_b, out_a, out_b)


FF_TN = F_FF // 2
FF_TM = 512


def _ffn_fwd(h, w_gate, w_up):
    def body(h_ref, wg_ref, wu_ref, g_ref, u_ref, a_ref):
        hb = h_ref[...]
        g = jnp.dot(hb, wg_ref[...], preferred_element_type=f32)
        u = jnp.dot(hb, wu_ref[...], preferred_element_type=f32)
        g_ref[...] = g
        u_ref[...] = u
        a_ref[...] = (g * jax.nn.sigmoid(g) * u).astype(bf16)

    tile = pl.BlockSpec((FF_TM, FF_TN), lambda j, i: (i, j))
    wspec = pl.BlockSpec((D, FF_TN), lambda j, i: (0, j))
    return pl.pallas_call(
        body, grid=(F_FF // FF_TN, S // FF_TM),
        in_specs=[pl.BlockSpec((FF_TM, D), lambda j, i: (i, 0)), wspec, wspec], out_specs=[tile] * 3,
        out_shape=[SDS((S, F_FF), f32), SDS((S, F_FF), f32), SDS((S, F_FF), bf16)], name="ffn_fwd",
        compiler_params=_params(("parallel", "parallel"), 2 * FF_TM * D + 4 * D * FF_TN + 10 * FF_TM * FF_TN, 16 * FF_TM * FF_TN),
    )(h, w_gate, w_up)


def _ffn_bwd_act(d_ff, w_down, g_act, u_act):
    def body(d_ref, wd_ref, g_ref, u_ref, dg_ref, du_ref):
        da = _nt(d_ref[...], wd_ref[...])
        g = g_ref[...]
        sg = jax.nn.sigmoid(g)
        du_ref[...] = (da * g * sg).astype(bf16)
        dg_ref[...] = (da * u_ref[...] * sg * (1.0 + g * (1.0 - sg))).astype(bf16)

    tile = pl.BlockSpec((FF_TM, FF_TN), lambda j, i: (i, j))
    return pl.pallas_call(
        body, grid=(F_FF // FF_TN, S // FF_TM),
        in_specs=[pl.BlockSpec((FF_TM, D), lambda j, i: (i, 0)), pl.BlockSpec((FF_TN, D), lambda j, i: (j, 0)), tile, tile],
        out_specs=[tile, tile], out_shape=[SDS((S, F_FF), bf16)] * 2, name="ffn_bwd_act",
        compiler_params=_params(("parallel", "parallel"), 2 * FF_TM * D + 2 * D * FF_TN + 12 * FF_TM * FF_TN, 16 * FF_TM * FF_TN),
    )(d_ff, w_down, g_act, u_act)


def _adamw(w, g, m, v, name):
    rows, cols = w.shape
    tm = next(t for t in (128, 64, 32, 16, 8) if rows % t == 0)
    c1 = 1.0 - ADAM_B1 ** ADAM_STEP
    c2 = 1.0 - ADAM_B2 ** ADAM_STEP

    def body(w_ref, g_ref, m_ref, v_ref, d_ref, nm_ref, nv_ref):
        g = g_ref[...]
        m_new = ADAM_B1 * m_ref[...] + (1.0 - ADAM_B1) * g
        v_new = ADAM_B2 * v_ref[...] + (1.0 - ADAM_B2) * (g * g)
        d_ref[...] = -ADAM_LR * ((m_new / c1) / (jnp.sqrt(v_new / c2) + ADAM_EPS) + ADAM_WD * w_ref[...])
        nm_ref[...] = m_new
        nv_ref[...] = v_new

    spec = pl.BlockSpec((tm, cols), lambda i: (i, 0))
    return pl.pallas_call(
        body, grid=(rows // tm,), in_specs=[spec] * 4, out_specs=[spec] * 3, out_shape=[SDS(w.shape, f32)] * 3,
        name=name, compiler_params=_params(("parallel",), 28 * tm * cols, 16 * tm * cols))(w, g, m, v)


_ANY = pl.BlockSpec(memory_space=pl.ANY)


def _place():
    x, y, c = lax.axis_index("x"), lax.axis_index("y"), lax.axis_index("c")
    chips = [(1 - x, y), (x, 1 - y), (1 - x, 1 - y)]
    return x, y, c, chips


def _all_gather_weights(flat):
    def body(flat_ref, out_ref, send_ici, recv_ici, send_d2d, recv_d2d, local_sem):
        x, y, c, chips = _place()
        sibling = (x, y, 1 - c)
        me_j = 2 * x + y

        def piece(j, half, q):
            return out_ref.at[j, pl.ds(half * HALF + q * CHUNK, CHUNK), :]

        own = pltpu.make_async_copy(flat_ref, out_ref.at[me_j], local_sem)
        own.start()

        def ici(p, q, to):
            (px, py) = chips[p]
            return pltpu.make_async_remote_copy(
                src_ref=flat_ref.at[pl.ds(c * HALF + q * CHUNK, CHUNK), :], dst_ref=piece(me_j, c, q),
                send_sem=send_ici.at[p, q], recv_sem=recv_ici.at[p, q], device_id=to, device_id_type=MESH)

        def landed(p, q):
            (px, py) = chips[p]
            blk = piece(2 * px + py, c, q)
            return pltpu.make_async_remote_copy(
                src_ref=blk, dst_ref=blk, send_sem=send_ici.at[p, q], recv_sem=recv_ici.at[p, q],
                device_id=sibling, device_id_type=MESH)

        def d2d(p, q, half):
            (px, py) = chips[p]
            blk = piece(2 * px + py, half, q)
            return pltpu.make_async_remote_copy(
                src_ref=blk, dst_ref=blk, send_sem=send_d2d.at[p, q], recv_sem=recv_d2d.at[p, q],
                device_id=sibling, device_id_type=MESH)

        sends = []
        for q in range(N_CHUNK):
            for p in range(3):
                cp = ici(p, q, (*chips[p], c))
                cp.start()
                sends.append(cp)
        passed = []
        for q in range(N_CHUNK):
            for p in range(3):
                landed(p, q).wait_recv()
                fw = d2d(p, q, c)
                fw.start()
                passed.append(fw)
        for q in range(N_CHUNK):
            for p in range(3):
                d2d(p, q, 1 - c).wait_recv()
        for cp in sends + passed:
            cp.wait_send()
        own.wait()

    return pl.pallas_call(
        body, in_specs=[_ANY], out_specs=_ANY, out_shape=SDS((N_SHARD, FLAT_PAD, D), bf16),
        scratch_shapes=[pltpu.SemaphoreType.DMA((3, N_CHUNK))] * 4 + [pltpu.SemaphoreType.DMA],
        name="all_gather_weights", compiler_params=pltpu.CompilerParams(has_side_effects=True))(flat)


def _pair_swap(src, name):
    def body(src_ref, out_ref, send_sem, recv_sem):
        x, y, c, _ = _place()
        cp = pltpu.make_async_remote_copy(
            src_ref=src_ref.at[:, pl.ds((1 - c) * HALF, HALF), :], dst_ref=out_ref,
            send_sem=send_sem, recv_sem=recv_sem, device_id=(x, y, 1 - c), device_id_type=MESH)
        cp.start()
        cp.wait()

    return pl.pallas_call(
        body, in_specs=[_ANY], out_specs=_ANY, out_shape=SDS((N_SHARD, HALF, D), src.dtype),
        scratch_shapes=[pltpu.SemaphoreType.DMA, pltpu.SemaphoreType.DMA], name=name,
        compiler_params=pltpu.CompilerParams(has_side_effects=True))(src)


def _pair_sum(grads, other):
    rows = min(512, HALF)
    nblk = HALF // rows
    core = lax.axis_index("c").astype(jnp.int32).reshape(1)

    def body(c_ref, g_ref, o_ref, out_ref):
        out_ref[...] = (g_ref[...].astype(f32) + o_ref[...].astype(f32)).astype(bf16)

    grid_spec = pltpu.PrefetchScalarGridSpec(
        num_scalar_prefetch=1, grid=(N_SHARD, nblk),
        in_specs=[pl.BlockSpec((None, rows, D), lambda j, i, c_ref: (j, c_ref[0] * nblk + i, 0)),
                  pl.BlockSpec((None, rows, D), lambda j, i, c_ref: (j, i, 0))],
        out_specs=pl.BlockSpec((None, rows, D), lambda j, i, c_ref: (j, i, 0)))
    return pl.pallas_call(
        body, grid_spec=grid_spec, out_shape=SDS((N_SHARD, HALF, D), bf16), name="pair_sum",
        compiler_params=_params(("parallel", "parallel"), 10 * rows * D, 12 * rows * D))(core, grads, other)


def _scatter_partials(part, small):
    def body(part_ref, small_ref, recv_ref, small_all_ref, send_sems, recv_sems, ssend, srecv, local_sems):
        x, y, c, chips = _place()
        me_j = 2 * x + y
        me_dev = 4 * x + 2 * y + c
        own = [pltpu.make_async_copy(small_ref, small_all_ref.at[me_dev], local_sems.at[0]),
               pltpu.make_async_copy(part_ref.at[me_j], recv_ref.at[me_j], local_sems.at[1])]
        for cp in own:
            cp.start()
        sends = []
        for p, (px, py) in enumerate(chips):
            cp = pltpu.make_async_remote_copy(
                src_ref=part_ref.at[2 * px + py], dst_ref=recv_ref.at[me_j],
                send_sem=send_sems.at[p], recv_sem=recv_sems.at[p], device_id=(px, py, c), device_id_type=MESH)
            cp.start()
            sends.append(cp)
        flip = lambda a, bit: 1 - a if bit else a
        peers = [(flip(x, k & 4), flip(y, k & 2), flip(c, k & 1)) for k in range(1, 8)]
        for k, to in enumerate(peers):
            cp = pltpu.make_async_remote_copy(
                src_ref=small_ref, dst_ref=small_all_ref.at[me_dev],
                send_sem=ssend.at[k], recv_sem=srecv.at[k], device_id=to, device_id_type=MESH)
            cp.start()
            sends.append(cp)
        for p, (px, py) in enumerate(chips):
            slot = recv_ref.at[2 * px + py]
            pltpu.make_async_remote_copy(
                src_ref=slot, dst_ref=slot, send_sem=send_sems.at[p], recv_sem=recv_sems.at[p],
                device_id=(px, py, c), device_id_type=MESH).wait_recv()
        for k, (px, py, pc) in enumerate(peers):
            slot = small_all_ref.at[4 * px + 2 * py + pc]
            pltpu.make_async_remote_copy(
                src_ref=slot, dst_ref=slot, send_sem=ssend.at[k], recv_sem=srecv.at[k],
                device_id=(px, py, pc), device_id_type=MESH).wait_recv()
        for cp in sends:
            cp.wait_send()
        for cp in own:
            cp.wait()

    return pl.pallas_call(
        body, in_specs=[_ANY, _ANY], out_specs=[_ANY, _ANY],
        out_shape=[SDS((N_SHARD, HALF, D), bf16), SDS((8, SMALL_ROWS, D), f32)],
        scratch_shapes=[pltpu.SemaphoreType.DMA((3,)), pltpu.SemaphoreType.DMA((3,)),
                        pltpu.SemaphoreType.DMA((7,)), pltpu.SemaphoreType.DMA((7,)), pltpu.SemaphoreType.DMA((2,))],
        name="scatter_partials", compiler_params=pltpu.CompilerParams(has_side_effects=True))(part, small)


def _sum_partials(recv, small_all):
    rows = min(512, HALF)

    def body(r0, r1, r2, r3, small_ref, out_ref, small_out_ref):
        acc = r0[...].astype(f32)
        for r in (r1, r2, r3):
            acc = acc + r[...].astype(f32)
        out_ref[...] = acc
        tot = small_ref[0]
        for k in range(1, 8):
            tot = tot + small_ref[k]
        small_out_ref[...] = tot

    slot = lambda j: pl.BlockSpec((None, rows, D), lambda i: (j, i, 0))
    return pl.pallas_call(
        body, grid=(HALF // rows,),
        in_specs=[slot(0), slot(1), slot(2), slot(3), pl.BlockSpec((8, SMALL_ROWS, D), lambda i: (0, 0, 0))],
        out_specs=[pl.BlockSpec((rows, D), lambda i: (i, 0)), pl.BlockSpec((SMALL_ROWS, D), lambda i: (0, 0))],
        out_shape=[SDS((HALF, D), f32), SDS((SMALL_ROWS, D), f32)], name="sum_partials",
        compiler_params=_params(("arbitrary",), 12 * rows * D + 36 * SMALL_ROWS * D, 12 * rows * D),
    )(recv, recv, recv, recv, small_all)


def _swap_halves(half):
    def body(half_ref, out_ref, send_sem, recv_sem, local_sem):
        x, y, c, _ = _place()
        mine = out_ref.at[pl.ds(c * HALF, HALF), :]
        own = pltpu.make_async_copy(half_ref, mine, local_sem)
        own.start()
        cp = pltpu.make_async_remote_copy(
            src_ref=half_ref, dst_ref=mine, send_sem=send_sem, recv_sem=recv_sem,
            device_id=(x, y, 1 - c), device_id_type=MESH)
        cp.start()
        theirs = out_ref.at[pl.ds((1 - c) * HALF, HALF), :]
        pltpu.make_async_remote_copy(
            src_ref=theirs, dst_ref=theirs, send_sem=send_sem, recv_sem=recv_sem,
            device_id=(x, y, 1 - c), device_id_type=MESH).wait_recv()
        cp.wait_send()
        own.wait()

    return pl.pallas_call(
        body, in_specs=[_ANY], out_specs=_ANY, out_shape=SDS((FLAT_PAD, D), f32),
        scratch_shapes=[pltpu.SemaphoreType.DMA, pltpu.SemaphoreType.DMA, pltpu.SemaphoreType.DMA],
        name="swap_halves", compiler_params=pltpu.CompilerParams(has_side_effects=True))(half)


def _pack_shards(w):
    parts = [_pad_rows(w[n].astype(bf16).reshape(r, D), _slot(r)) for n, r in FLAT_ROWS]
    parts.append(jnp.zeros((FLAT_PAD - FLAT_USED, D), bf16))
    return jnp.concatenate(parts, axis=0)


def _pad_rows(t, rows):
    return jnp.pad(t, ((0, rows - t.shape[0]), (0, 0)))


def _unpack_full(gathered):
    off, full = 0, {}
    for n, r in FLAT_ROWS:
        full[n] = gathered[:, off:off + r, :]
        off += _slot(r)
    cols = lambda t, rows: jnp.concatenate([t[j].reshape(rows, -1) for j in range(N_SHARD)], axis=1)
    w_in = cols(full["w_in"], D)
    return dict(
        w_qk=w_in[:, :QK_W],
        w_vr=jnp.concatenate([w_in[:, 2304:3840], w_in[:, 1536:2304]], axis=1),
        w_gf=jnp.concatenate([w_in[:, 3848:5896], w_in[:, 3840:3848], jnp.zeros((D, 120), bf16)], axis=1),
        w_a=cols(full["w_proj_a"], DIL_W),
        w_b=cols(full["w_proj_b"], FOX_W),
        w_out=full["w_out"].reshape(D, D),
        w_gate=cols(full["w_ffn_gate"], D),
        w_up=cols(full["w_ffn_up"], D),
        w_down=full["w_ffn_down"].reshape(F_FF, D),
    )


def _pack_grads(g):
    d_w_in = jnp.concatenate([g["w_qk"], g["w_vr"][:, 1536:2304], g["w_vr"][:, :1536],
                              g["w_gf"][:, 2048:2056], g["w_gf"][:, :2048]], axis=1)
    col_shard = lambda t, j: t[:, j * (t.shape[1] // N_SHARD):(j + 1) * (t.shape[1] // N_SHARD)]
    row_shard = lambda t, j: t[j * (t.shape[0] // N_SHARD):(j + 1) * (t.shape[0] // N_SHARD)]
    blocks = []
    for j in range(N_SHARD):
        parts = [col_shard(d_w_in, j), col_shard(g["w_a"], j), col_shard(g["w_b"], j), row_shard(g["w_out"], j),
                 col_shard(g["w_gate"], j), col_shard(g["w_up"], j), row_shard(g["w_down"], j)]
        parts = [_pad_rows(p.reshape(r, D), _slot(r)) for p, (_, r) in zip(parts, FLAT_ROWS)]
        parts.append(jnp.zeros((FLAT_PAD - FLAT_USED, D), f32))
        blocks.append(jnp.concatenate(parts, axis=0))
    return jnp.stack(blocks, axis=0)


def _unpack_shard(flat, like):
    off, out = 0, {}
    for n, r in FLAT_ROWS:
        out[n] = flat[off:off + r].reshape(like[n].shape)
        off += _slot(r)
    return out


def _local_step(x, target, wt, b_forget, g_mix_pre, g_mix_post, g_ffn_pre, g_ffn_post):
    tables = _rope_tables()
    b128 = jnp.pad(b_forget, ((0, 0), (0, 128 - N_FOX)))

    h1 = _norm_fwd(x, g_mix_pre)
    p_qk = _mm([(h1, wt["w_qk"])], "nn", f32, tm=1024, tn=768, name="proj_qk")
    vr = _mm([(h1, wt["w_vr"])], "nn", bf16, tm=1024, tn=768, name="proj_vr")
    gf = _mm([(h1, wt["w_gf"])], "nn", f32, tm=512, tn=GF_W, name="proj_gf")
    qk = _rope(p_qk, tables, "rope_fwd")
    dil = [_dil_fwd(g, qk, vr) for g in range(3)]
    out_a, lse_a = _dil_combine([o for o, _ in dil], [l for _, l in dil])
    f_cum = _forget_fwd(gf, b128)
    f_cum_t = f_cum[:, :N_FOX].T
    out_b, lse_b = _fox_fwd(vr, f_cum, f_cum_t)
    ya, yb, merged = _merge_fwd(out_a, out_b, wt["w_a"], wt["w_b"], gf)
    mix = _mm([(merged, wt["w_out"])], "nn", f32, tm=1024, tn=D, name="proj_out")
    x2, h3 = _resid_norm_fwd(x, mix, g_mix_post, g_ffn_pre)
    g_act, u_act, a_act = _ffn_fwd(h3, wt["w_gate"], wt["w_up"])
    ff = _mm([(a_act, wt["w_down"])], "nn", f32, tm=1024, tn=D, name="ffn_down")
    sq_err, dy, d_ff, dg_ffn_post = _loss_head(x2, ff, g_ffn_post, target)

    grads = {}
    d_g, d_u = _ffn_bwd_act(d_ff, wt["w_down"], g_act, u_act)
    grads["w_down"] = _mm([(a_act, d_ff)], "tn", f32, tm=FF_TN, tn=512, name="grad_w_down")
    grads["w_gate"] = _mm([(h3, d_g)], "tn", f32, tm=512, tn=FF_TN, name="grad_w_gate", m_inner=True)
    grads["w_up"] = _mm([(h3, d_u)], "tn", f32, tm=512, tn=FF_TN, name="grad_w_up", m_inner=True)
    d_h3 = _mm([(d_g, wt["w_gate"]), (d_u, wt["w_up"])], "nt", f32, tm=512, tn=512, name="ffn_bwd_in")
    dx2, d_mix, dg_ffn_pre, dg_mix_post = _norm_bwd_mid(dy, d_h3, x2, mix, g_ffn_pre, g_mix_post)

    grads["w_out"] = _mm([(merged, d_mix)], "tn", f32, tm=D, tn=D, name="grad_w_out")
    d_merged = _mm([(d_mix, wt["w_out"])], "nt", f32, tm=1024, tn=D, name="proj_out_bwd")
    d_ya, d_yb, d_gab = _merge_bwd(d_merged, ya, yb, gf)
    grads["w_a"] = _mm([(out_a, d_ya)], "tn", f32, tm=DIL_W, tn=D, name="grad_w_proj_a")
    grads["w_b"] = _mm([(out_b, d_yb)], "tn", f32, tm=FOX_W, tn=D, name="grad_w_proj_b")
    d_out_a, d_out_b, delta_b = _branch_bwd(d_ya, d_yb, wt["w_a"], wt["w_b"], out_a, out_b)

    dq_a, dk_a, dv_a = [], [], []
    for g in range(3):
        dq_a.append(_dil_bwd_q(g, qk, vr, d_out_a, out_a, lse_a))
        dk, dv = _dil_bwd_kv(g, qk, vr, d_out_a, out_a, lse_a)
        dk_a.append(dk)
        dv_a.append(dv)
    d_qk = _rope(jnp.concatenate(dq_a + dk_a, axis=1), tables, "rope_bwd", transpose=True)
    dq_b, dk_b, dv_b, d_f_t, d_f_rows = _fox_bwd(vr, f_cum, f_cum_t, lse_b, d_out_b, delta_b)
    d_f_cols = jnp.pad(d_f_t.T, ((0, 0), (0, 128 - N_FOX)))
    d_z, d_b128 = _forget_bwd(gf, b128, d_f_cols, d_f_rows)
    d_vr = jnp.concatenate([dq_b, dk_b, dv_b] + dv_a, axis=1).astype(bf16)
    d_gf = jnp.concatenate([d_gab, d_z.astype(bf16)], axis=1)

    grads["w_qk"] = _mm([(h1, d_qk)], "tn", f32, tm=D, tn=768, name="grad_w_qk")
    grads["w_vr"] = _mm([(h1, d_vr)], "tn", f32, tm=D, tn=768, name="grad_w_vr")
    grads["w_gf"] = _mm([(h1, d_gf)], "tn", f32, tm=512, tn=GF_W, name="grad_w_gf")
    d_h1 = _mm([(d_qk, wt["w_qk"]), (d_vr, wt["w_vr"]), (d_gf, wt["w_gf"])], "nt", f32, tm=512, tn=512, name="proj_in_bwd")
    grad_x, dg_mix_pre = _norm_bwd_in(dx2, d_h1, x, g_mix_pre)

    small = dict(b_forget=d_b128[:, :N_FOX], norm_mix_pre=dg_mix_pre, norm_mix_post=dg_mix_post,
                 norm_ffn_pre=dg_ffn_pre, norm_ffn_post=dg_ffn_post)
    return sq_err, grad_x, grads, small


BIG = ("w_in", "w_proj_a", "w_proj_b", "w_out", "w_ffn_gate", "w_ffn_up", "w_ffn_down")
NORMS = ("norm_mix_pre", "norm_mix_post", "norm_ffn_pre", "norm_ffn_post")
ORDER = ("w_in", "w_proj_a", "w_proj_b", "w_out", "b_forget", "w_ffn_gate", "w_ffn_up", "w_ffn_down") + NORMS


def kernel(x, w_in, w_proj_a, w_proj_b, w_out, b_forget, w_ffn_gate, w_ffn_up, w_ffn_down, norm_mix_pre, norm_mix_post, norm_ffn_pre, norm_ffn_post, loss_target, m_w_in, m_w_proj_a, m_w_proj_b, m_w_out, m_b_forget, m_w_ffn_gate, m_w_ffn_up, m_w_ffn_down, m_norm_mix_pre, m_norm_mix_post, m_norm_ffn_pre, m_norm_ffn_post, v_w_in, v_w_proj_a, v_w_proj_b, v_w_out, v_b_forget, v_w_ffn_gate, v_w_ffn_up, v_w_ffn_down, v_norm_mix_pre, v_norm_mix_post, v_norm_ffn_pre, v_norm_ffn_post):
    w = dict(w_in=w_in[0], w_proj_a=w_proj_a[0], w_proj_b=w_proj_b[0], w_out=w_out[0], b_forget=b_forget,
             w_ffn_gate=w_ffn_gate[0], w_ffn_up=w_ffn_up[0], w_ffn_down=w_ffn_down[0], norm_mix_pre=norm_mix_pre,
             norm_mix_post=norm_mix_post, norm_ffn_pre=norm_ffn_pre, norm_ffn_post=norm_ffn_post)
    m = dict(w_in=m_w_in[0], w_proj_a=m_w_proj_a[0], w_proj_b=m_w_proj_b[0], w_out=m_w_out[0], b_forget=m_b_forget,
             w_ffn_gate=m_w_ffn_gate[0], w_ffn_up=m_w_ffn_up[0], w_ffn_down=m_w_ffn_down[0], norm_mix_pre=m_norm_mix_pre,
             norm_mix_post=m_norm_mix_post, norm_ffn_pre=m_norm_ffn_pre, norm_ffn_post=m_norm_ffn_post)
    v = dict(w_in=v_w_in[0], w_proj_a=v_w_proj_a[0], w_proj_b=v_w_proj_b[0], w_out=v_w_out[0], b_forget=v_b_forget,
             w_ffn_gate=v_w_ffn_gate[0], w_ffn_up=v_w_ffn_up[0], w_ffn_down=v_w_ffn_down[0], norm_mix_pre=v_norm_mix_pre,
             norm_mix_post=v_norm_mix_post, norm_ffn_pre=v_norm_ffn_pre, norm_ffn_post=v_norm_ffn_post)

    wt = _unpack_full(_all_gather_weights(_pack_shards(w)))

    sq_err, grad_x, grads, small = _local_step(x[0], loss_target[0], wt, b_forget, norm_mix_pre, norm_mix_post,
                                               norm_ffn_pre, norm_ffn_post)
    loss = lax.psum(sq_err[0, 0] * (0.5 / D), ("x", "y", "c"))

    flat_g = _pack_grads(grads)
    small_buf = jnp.concatenate(
        [jnp.pad(small["b_forget"], ((0, 0), (0, D - N_FOX)))] + [small[n] for n in NORMS]
        + [jnp.zeros((SMALL_ROWS - 5, D), f32)], axis=0)
    part = _pair_sum(flat_g, _pair_swap(flat_g, "pair_swap"))
    recv, small_all = _scatter_partials(part, small_buf)
    half, small_sum = _sum_partials(recv, small_all)
    g_shard = _unpack_shard(_swap_halves(half), w)
    g_shard["b_forget"] = small_sum[0:1, :N_FOX]
    for i, n in enumerate(NORMS):
        g_shard[n] = small_sum[i + 1:i + 2]

    delta, new_m, new_v = {}, {}, {}
    for n in BIG:
        delta[n], new_m[n], new_v[n] = _adamw(w[n], g_shard[n], m[n], v[n], "adamw_" + n)
    stack = lambda t: jnp.concatenate([jnp.pad(t["b_forget"], ((0, 0), (0, D - N_FOX)))] + [t[n] for n in NORMS]
                                      + [jnp.ones((SMALL_ROWS - 5, D), f32)], axis=0)
    sd, sm, sv = _adamw(stack(w), small_sum, stack(m), stack(v), "adamw_small")
    for t, st in ((delta, sd), (new_m, sm), (new_v, sv)):
        t["b_forget"] = st[0:1, :N_FOX]
        for i, n in enumerate(NORMS):
            t[n] = st[i + 1:i + 2]

    lead = lambda n, a: a[None] if n in BIG else a
    outs = [loss, grad_x[None]]
    for t in (g_shard, delta, new_m, new_v):
        outs += [lead(n, t[n]) for n in ORDER]
    return tuple(outs)
```

```python
import functools
import math

import jax
import jax.numpy as jnp
from jax import lax
from jax.experimental import pallas as pl
from jax.experimental.pallas import tpu as pltpu

f32 = jnp.float32
bf16 = jnp.bfloat16
SDS = jax.ShapeDtypeStruct
MESH = pl.DeviceIdType.MESH

S = 2048
D = 1024
HD = 64
BLK = 128
N_FOX = 8
FOX_W = N_FOX * HD
DIL_GROUPS = ((128, 1), (512, 4), (2048, 16))
SLOTS = 4
DIL_W = SLOTS * HD
QKV_W = 3 * DIL_W
VR_W = 3 * FOX_W
GF_W = 2 * D + 128
F_FF = 2816
ROPE_DIM = 16
ROPE_THETA = 500000.0
EPS = 1e-6
NEG = -1e30
SCALE = 1.0 / math.sqrt(HD)
IN_COLS = 5896
N_SHARD = 4

ADAM_LR, ADAM_B1, ADAM_B2, ADAM_EPS, ADAM_WD, ADAM_STEP = 0.001, 0.9, 0.999, 1e-08, 0.01, 10

VMEM_V7X = 64 * 1024 * 1024
VMEM_PLAN_MAX = VMEM_V7X - 8 * 1024 * 1024

TM = 256
TQ = 256

W_NAMES = ("w_in", "w_proj_a", "w_proj_b", "w_out", "w_ffn_gate", "w_ffn_up", "w_ffn_down")
TRANSPOSED = ("w_in", "w_ffn_gate", "w_ffn_up")
IN_SHARD = IN_COLS // N_SHARD
IN_SHARD_PAD = 1504
SHARD_SHAPE = dict(w_in=(IN_SHARD_PAD, D), w_proj_a=(DIL_W, D // N_SHARD), w_proj_b=(FOX_W, D // N_SHARD),
                   w_out=(D // N_SHARD, D), w_ffn_gate=(F_FF // N_SHARD, D), w_ffn_up=(F_FF // N_SHARD, D),
                   w_ffn_down=(F_FF // N_SHARD, D))
SMALL_ROWS = 8


def _nbytes(shape, dtype):
    return math.prod(shape) * jnp.dtype(dtype).itemsize


def _params(semantics, block_bytes, temp_bytes=0):
    need = 2 * block_bytes + temp_bytes + (2 << 20)
    return pltpu.CompilerParams(dimension_semantics=semantics,
                                vmem_limit_bytes=int(min(max(need, 16 << 20), VMEM_PLAN_MAX)))


def _row(w, tm=TM):
    return pl.BlockSpec((tm, w), lambda i: (i, 0))


def _vec(w):
    return pl.BlockSpec((1, w), lambda i: (0, 0))


def _mm(pairs, dims, out_dtype, *, tm, tn, name, m_inner=False):
    a0, b0 = pairs[0]
    m_dim = a0.shape[1] if dims == "tn" else a0.shape[0]
    n_dim = b0.shape[0] if dims == "nt" else b0.shape[1]
    contract = {"nn": ((1,), (0,)), "nt": ((1,), (1,)), "tn": ((0,), (0,))}[dims]
    n_pairs = len(pairs)
    assert m_dim % tm == 0 and n_dim % tn == 0, (name, m_dim, n_dim, tm, tn)

    def body(*refs):
        o_ref = refs[-1]
        acc = None
        for p in range(n_pairs):
            a = refs[2 * p][...].astype(bf16)
            b = refs[2 * p + 1][...].astype(bf16)
            t = lax.dot_general(a, b, (contract, ((), ())), preferred_element_type=f32)
            acc = t if acc is None else acc + t
        o_ref[...] = acc.astype(o_ref.dtype)

    if m_inner:
        grid = (n_dim // tn, m_dim // tm)
        mi = lambda j, i: i
        ni = lambda j, i: j
    else:
        grid = (m_dim // tm, n_dim // tn)
        mi = lambda i, j: i
        ni = lambda i, j: j
    in_specs, block_bytes, args = [], 0, []
    for a, b in pairs:
        k_dim = a.shape[0] if dims == "tn" else a.shape[1]
        if dims == "tn":
            in_specs.append(pl.BlockSpec((k_dim, tm), lambda *g: (0, mi(*g))))
        else:
            in_specs.append(pl.BlockSpec((tm, k_dim), lambda *g: (mi(*g), 0)))
        if dims == "nt":
            in_specs.append(pl.BlockSpec((tn, k_dim), lambda *g: (ni(*g), 0)))
        else:
            in_specs.append(pl.BlockSpec((k_dim, tn), lambda *g: (0, ni(*g))))
        block_bytes += _nbytes((tm, k_dim), a.dtype) + _nbytes((tn, k_dim), b.dtype)
        args += [a, b]
    block_bytes += _nbytes((tm, tn), out_dtype)
    temp = _nbytes((tm, tn), f32) * 2 + sum(_nbytes((tm, a.shape[0] if dims == "tn" else a.shape[1]), bf16)
                                            + _nbytes((tn, a.shape[0] if dims == "tn" else a.shape[1]), bf16)
                                            for a, _ in pairs)
    return pl.pallas_call(
        body, grid=grid, in_specs=in_specs,
        out_specs=pl.BlockSpec((tm, tn), lambda *g: (mi(*g), ni(*g))),
        out_shape=SDS((m_dim, n_dim), out_dtype), name=name,
        compiler_params=_params(("parallel", "parallel"), block_bytes, temp),
    )(*args)


def _rms(x, g):
    r = lax.rsqrt(jnp.mean(x * x, axis=-1, keepdims=True) + EPS)
    return x * r * g


def _rms_bwd(x, g, dy):
    r = lax.rsqrt(jnp.mean(x * x, axis=-1, keepdims=True) + EPS)
    xh = x * r
    dxh = dy * g
    dx = r * (dxh - xh * jnp.mean(dxh * xh, axis=-1, keepdims=True))
    return dx, jnp.sum(dy * xh, axis=0, keepdims=True)


def _acc_rows(ref, val):
    @pl.when(pl.program_id(0) == 0)
    def _():
        ref[...] = jnp.zeros_like(ref)
    ref[...] += val


def _norm_fwd(xs, g):
    n = len(xs)

    def body(*refs):
        g = refs[n][...]
        for x_ref, h_ref in zip(refs[:n], refs[n + 1:]):
            h_ref[...] = _rms(x_ref[...], g).astype(bf16)

    return pl.pallas_call(
        body, grid=(S // TM,), in_specs=[_row(D)] * n + [_vec(D)], out_specs=[_row(D)] * n,
        out_shape=[SDS((S, D), bf16)] * n, name="norm_mix_pre",
        compiler_params=_params(("parallel",), 6 * n * TM * D, 8 * n * TM * D))(*xs, g)


def _perm_rows(xs, ds, name):
    n = len(xs)

    def body(*refs):
        outs = iter(refs[n:])
        for x_ref in refs[:n]:
            for d in ds:
                o_ref, rows = next(outs), S // d
                for r in range(d):
                    o_ref[r * rows:(r + 1) * rows, :] = x_ref[pl.ds(r, rows, stride=d), :]

    blk = pl.BlockSpec((S, 128), lambda c: (0, c))
    w = xs[0].shape[1]
    return pl.pallas_call(
        body, grid=(w // 128,), in_specs=[blk] * n, out_specs=[blk] * (n * len(ds)),
        out_shape=[SDS((S, w), f32)] * (n * len(ds)), name=name,
        compiler_params=_params(("parallel",), 4 * S * 128 * n * (1 + len(ds))))(*xs)


def _unperm_sum(nat, perms, ds, name):
    n = len(perms)

    def body(*refs):
        a_ref, o_ref, sc = refs[0], refs[n + 1], refs[n + 2]
        acc = a_ref[...]
        for b_ref, d in zip(refs[1:n + 1], ds):
            rows = S // d
            for r in range(d):
                sc[pl.ds(r, rows, stride=d), :] = b_ref[r * rows:(r + 1) * rows, :]
            acc = acc + sc[...]
        o_ref[...] = acc

    blk = pl.BlockSpec((S, 128), lambda c: (0, c))
    w = nat.shape[1]
    return pl.pallas_call(
        body, grid=(w // 128,), in_specs=[blk] * (n + 1), out_specs=blk, out_shape=SDS((S, w), f32),
        scratch_shapes=[pltpu.VMEM((S, 128), f32)], name=name,
        compiler_params=_params(("parallel",), 4 * S * 128 * (n + 2), 8 * S * 128))(nat, *perms)


def _resid_norm_fwd(x, mix, g_post, g_pre):
    def body(x_ref, mix_ref, gp_ref, gn_ref, x2_ref, h_ref):
        x2 = x_ref[...] + _rms(mix_ref[...], gp_ref[...])
        x2_ref[...] = x2
        h_ref[...] = _rms(x2, gn_ref[...]).astype(bf16)

    return pl.pallas_call(
        body, grid=(S // TM,), in_specs=[_row(D), _row(D), _vec(D), _vec(D)], out_specs=[_row(D), _row(D)],
        out_shape=[SDS((S, D), f32), SDS((S, D), bf16)], name="resid_norm_mid",
        compiler_params=_params(("parallel",), 14 * TM * D, 16 * TM * D))(x, mix, g_post, g_pre)


def _loss_head(x2, ff, g_post, target):
    def body(x2_ref, ff_ref, g_ref, t_ref, loss_ref, dy_ref, dff_ref, dg_ref):
        ff = ff_ref[...]
        g = g_ref[...]
        err = x2_ref[...] + _rms(ff, g) - t_ref[...]
        dy = err * (1.0 / D)
        dff, dg = _rms_bwd(ff, g, dy)
        dy_ref[...] = dy
        dff_ref[...] = dff.astype(bf16)
        _acc_rows(dg_ref, dg)
        _acc_rows(loss_ref, jnp.full((1, 128), jnp.sum(err * err), f32))

    return pl.pallas_call(
        body, grid=(S // TM,), in_specs=[_row(D), _row(D), _vec(D), _row(D)],
        out_specs=[_vec(128), _row(D), _row(D), _vec(D)],
        out_shape=[SDS((1, 128), f32), SDS((S, D), f32), SDS((S, D), bf16), SDS((1, D), f32)], name="loss_head",
        compiler_params=_params(("arbitrary",), 18 * TM * D, 24 * TM * D))(x2, ff, g_post, target)


def _norm_bwd_mid(dy, dh3, x2, mix, g_ffn_pre, g_mix_post):
    def body(dy_ref, dh_ref, x2_ref, mix_ref, g3_ref, g2_ref, dx2_ref, dmix_ref, dg3_ref, dg2_ref):
        d3, dg3 = _rms_bwd(x2_ref[...], g3_ref[...], dh_ref[...])
        dx2 = dy_ref[...] + d3
        dmix, dg2 = _rms_bwd(mix_ref[...], g2_ref[...], dx2)
        dx2_ref[...] = dx2
        dmix_ref[...] = dmix.astype(bf16)
        _acc_rows(dg3_ref, dg3)
        _acc_rows(dg2_ref, dg2)

    return pl.pallas_call(
        body, grid=(S // TM,), in_specs=[_row(D)] * 4 + [_vec(D)] * 2,
        out_specs=[_row(D), _row(D), _vec(D), _vec(D)],
        out_shape=[SDS((S, D), f32), SDS((S, D), bf16), SDS((1, D), f32), SDS((1, D), f32)], name="norm_bwd_mid",
        compiler_params=_params(("arbitrary",), 22 * TM * D, 24 * TM * D))(dy, dh3, x2, mix, g_ffn_pre, g_mix_post)


def _norm_bwd_in(dx2, dh1, x, g):
    def body(dx2_ref, dh_ref, x_ref, g_ref, gx_ref, dg_ref):
        d1, dg = _rms_bwd(x_ref[...], g_ref[...], dh_ref[...])
        gx_ref[...] = dx2_ref[...] + d1
        _acc_rows(dg_ref, dg)

    return pl.pallas_call(
        body, grid=(S // TM,), in_specs=[_row(D)] * 3 + [_vec(D)], out_specs=[_row(D), _vec(D)],
        out_shape=[SDS((S, D), f32), SDS((1, D), f32)], name="norm_bwd_in",
        compiler_params=_params(("arbitrary",), 16 * TM * D, 16 * TM * D))(dx2, dh1, x, g)


def _rope_tables():
    half = ROPE_DIM // 2
    inv_freq = jnp.power(ROPE_THETA, -jnp.arange(0, ROPE_DIM, 2, dtype=f32) / ROPE_DIM)
    row = jnp.arange(S, dtype=jnp.int32)
    groups = []
    for _, d in DIL_GROUPS:
        pos = ((row % (S // d)) * d + row // (S // d)).astype(f32)
        ang = pos[:, None] * inv_freq[None, :]
        cos, sin = jnp.cos(ang), jnp.sin(ang)
        c = jnp.concatenate([cos, cos, jnp.ones((S, HD - ROPE_DIM), f32)], axis=1)
        s_lo = jnp.concatenate([-sin, jnp.zeros((S, HD - half), f32)], axis=1)
        s_hi = jnp.concatenate([jnp.zeros((S, half), f32), sin, jnp.zeros((S, HD - ROPE_DIM), f32)], axis=1)
        groups.append(jnp.stack([jnp.concatenate([t, t], axis=1) for t in (c, s_lo, s_hi)]))
    return jnp.stack(groups)


def _rotate(x, c, lo, hi, sign):
    tile = lambda t: jnp.tile(t, (1, DIL_W // 128))
    return (x * tile(c) + pltpu.roll(x, DIL_W - ROPE_DIM // 2, 1) * (tile(lo) * sign)
            + pltpu.roll(x, ROPE_DIM // 2, 1) * (tile(hi) * sign))


def _table_specs(g):
    return [pl.BlockSpec((None, None, TM, 128), lambda i, k=k: (g, k, i, 0)) for k in range(3)]


def _rope_fwd(g, p_qkv, tables):
    def body(x_ref, c_ref, lo_ref, hi_ref, o_ref):
        c, lo, hi = c_ref[...], lo_ref[...], hi_ref[...]
        for part in range(2):
            cols = slice(part * DIL_W, (part + 1) * DIL_W)
            o_ref[:, cols] = _rotate(x_ref[:, cols], c, lo, hi, 1.0).astype(bf16)
        o_ref[:, 2 * DIL_W:] = x_ref[:, 2 * DIL_W:].astype(bf16)

    return pl.pallas_call(
        body, grid=(S // TM,), in_specs=[_row(QKV_W)] + _table_specs(g), out_specs=_row(QKV_W),
        out_shape=SDS((S, QKV_W), bf16), name=f"rope_fwd_{g}",
        compiler_params=_params(("parallel",), 6 * TM * QKV_W + 12 * TM * 128, 24 * TM * QKV_W))(p_qkv, tables, tables, tables)


def _rope_bwd(g, dq, dk, dv, tables):
    def body(dq_ref, dk_ref, dv_ref, c_ref, lo_ref, hi_ref, o_ref):
        c, lo, hi = c_ref[...], lo_ref[...], hi_ref[...]
        o_ref[:, :DIL_W] = _rotate(dq_ref[...], c, lo, hi, -1.0).astype(bf16)
        o_ref[:, DIL_W:2 * DIL_W] = _rotate(dk_ref[...], c, lo, hi, -1.0).astype(bf16)
        o_ref[:, 2 * DIL_W:] = dv_ref[...].astype(bf16)

    return pl.pallas_call(
        body, grid=(S // TM,), in_specs=[_row(DIL_W)] * 3 + _table_specs(g), out_specs=_row(QKV_W),
        out_shape=SDS((S, QKV_W), bf16), name=f"rope_bwd_{g}",
        compiler_params=_params(("parallel",), 6 * TM * QKV_W + 12 * TM * 128, 24 * TM * QKV_W))(dq, dk, dv, tables, tables, tables)


def _dil_masks(n_is_first):
    qi = lax.broadcasted_iota(jnp.int32, (BLK, BLK), 0)
    kj = lax.broadcasted_iota(jnp.int32, (BLK, BLK), 1)
    cur = kj <= qi
    prev = kj >= qi + jnp.where(n_is_first, BLK + 1, 0)
    return cur, prev


def _nt(a, b):
    return lax.dot_general(a, b, (((1,), (1,)), ((), ())), preferred_element_type=f32)


def _tn(a, b):
    return lax.dot_general(a, b, (((0,), (0,)), ((), ())), preferred_element_type=f32)


def _dil_specs(g):
    _, d = DIL_GROUPS[g]
    nb = S // d // BLK
    blk = (BLK, DIL_W)
    own = lambda col: pl.BlockSpec(blk, lambda r, n: (r * nb + n, col))
    prev = lambda col: pl.BlockSpec(blk, lambda r, n: (r * nb + jnp.maximum(n - 1, 0), col))
    nxt = lambda col: pl.BlockSpec(blk, lambda r, n: (r * nb + jnp.minimum(n + 1, nb - 1), col))
    return (d, nb), own, prev, nxt


def _dil_fwd(g, qkv):
    grid, own, prev, _ = _dil_specs(g)

    def body(q_ref, kc_ref, kp_ref, vc_ref, vp_ref, o_ref, lse_ref):
        cur, prv = _dil_masks(pl.program_id(1) == 0)
        for h in range(SLOTS):
            hs = slice(h * HD, (h + 1) * HD)
            q = q_ref[:, hs]
            sc = jnp.where(cur, _nt(q, kc_ref[:, hs]) * SCALE, NEG)
            sp = jnp.where(prv, _nt(q, kp_ref[:, hs]) * SCALE, NEG)
            m = jnp.maximum(jnp.max(sc, axis=-1, keepdims=True), jnp.max(sp, axis=-1, keepdims=True))
            pc, pp = jnp.exp(sc - m), jnp.exp(sp - m)
            den = jnp.sum(pc, axis=-1, keepdims=True) + jnp.sum(pp, axis=-1, keepdims=True)
            inv = 1.0 / den
            o = jnp.dot((pc * inv).astype(bf16), vc_ref[:, hs], preferred_element_type=f32)
            o += jnp.dot((pp * inv).astype(bf16), vp_ref[:, hs], preferred_element_type=f32)
            o_ref[:, hs] = o
            lse_ref[:, hs] = jnp.broadcast_to(m + jnp.log(den), (BLK, HD))

    return pl.pallas_call(
        body, grid=grid, in_specs=[own(0), own(1), prev(1), own(2), prev(2)], out_specs=[own(0), own(0)],
        out_shape=[SDS((S, DIL_W), f32)] * 2, name=f"dil_fwd_{g}",
        compiler_params=_params(("parallel", "parallel"), 18 * BLK * DIL_W, 1 << 20))(qkv, qkv, qkv, qkv, qkv)


def _dil_combine(outs, lses):
    def body(o0, o1, o2, l0, l1, l2, out_ref, lse_ref, so1, so2, sl1, sl2):
        for (_, d), src, dst in ((DIL_GROUPS[1], o1, so1), (DIL_GROUPS[2], o2, so2),
                                 (DIL_GROUPS[1], l1, sl1), (DIL_GROUPS[2], l2, sl2)):
            rows = S // d
            for r in range(d):
                dst[pl.ds(r, rows, stride=d), :] = src[r * rows:(r + 1) * rows, :]
        a, b, c = l0[...], sl1[...], sl2[...]
        m = jnp.maximum(jnp.maximum(a, b), c)
        ea, eb, ec = jnp.exp(a - m), jnp.exp(b - m), jnp.exp(c - m)
        z = ea + eb + ec
        inv = 1.0 / z
        out_ref[...] = (ea * inv) * o0[...] + (eb * inv) * so1[...] + (ec * inv) * so2[...]
        lse_ref[...] = m + jnp.log(z)

    blk = pl.BlockSpec((S, 128), lambda c: (0, c))
    return pl.pallas_call(
        body, grid=(DIL_W // 128,), in_specs=[blk] * 6, out_specs=[blk] * 2,
        out_shape=[SDS((S, DIL_W), f32)] * 2, scratch_shapes=[pltpu.VMEM((S, 128), f32)] * 4, name="dil_combine",
        compiler_params=_params(("parallel",), 32 * S * 128, 32 * S * 128))(*outs, *lses)


def _dil_probs(q, k, mask, lse, do, v, delta):
    s = jnp.where(mask, _nt(q, k) * SCALE, NEG)
    p = jnp.exp(s - lse)
    ds = p * (_nt(do, v) - delta) * SCALE
    return p, ds


def _dil_bwd_q(g, qkv, d_out, delta, lse):
    grid, own, prev, _ = _dil_specs(g)

    def body(q_ref, kc_ref, kp_ref, vc_ref, vp_ref, do_ref, dl_ref, lse_ref, dq_ref):
        cur, prv = _dil_masks(pl.program_id(1) == 0)
        for h in range(SLOTS):
            hs = slice(h * HD, (h + 1) * HD)
            q, do = q_ref[:, hs], do_ref[:, hs].astype(bf16)
            lse, delta = lse_ref[:, h * HD:h * HD + 1], dl_ref[:, h * HD:h * HD + 1]
            _, dsc = _dil_probs(q, kc_ref[:, hs], cur, lse, do, vc_ref[:, hs], delta)
            _, dsp = _dil_probs(q, kp_ref[:, hs], prv, lse, do, vp_ref[:, hs], delta)
            dq = jnp.dot(dsc.astype(bf16), kc_ref[:, hs], preferred_element_type=f32)
            dq += jnp.dot(dsp.astype(bf16), kp_ref[:, hs], preferred_element_type=f32)
            dq_ref[:, hs] = dq

    return pl.pallas_call(
        body, grid=grid, in_specs=[own(0), own(1), prev(1), own(2), prev(2), own(0), own(0), own(0)],
        out_specs=own(0), out_shape=SDS((S, DIL_W), f32), name=f"dil_bwd_q_{g}",
        compiler_params=_params(("parallel", "parallel"), 26 * BLK * DIL_W, 1 << 20),
    )(qkv, qkv, qkv, qkv, qkv, d_out, delta, lse)


def _dil_bwd_kv(g, qkv, d_out, delta, lse):
    grid, own, _, nxt = _dil_specs(g)
    nb = grid[1]

    def body(k_ref, v_ref, qc_ref, qn_ref, doc_ref, don_ref, dc_ref, dn_ref, lc_ref, ln_ref, dk_ref, dv_ref):
        is_last = pl.program_id(1) == nb - 1
        qi = lax.broadcasted_iota(jnp.int32, (BLK, BLK), 0)
        kj = lax.broadcasted_iota(jnp.int32, (BLK, BLK), 1)
        cur = kj <= qi
        nxt = kj >= qi + jnp.where(is_last, BLK + 1, 0)
        for h in range(SLOTS):
            hs = slice(h * HD, (h + 1) * HD)
            k, v = k_ref[:, hs], v_ref[:, hs]
            dk = jnp.zeros((BLK, HD), f32)
            dv = jnp.zeros((BLK, HD), f32)
            for q_ref, do_ref, d_ref, l_ref, mask in ((qc_ref, doc_ref, dc_ref, lc_ref, cur),
                                                      (qn_ref, don_ref, dn_ref, ln_ref, nxt)):
                q, do = q_ref[:, hs], do_ref[:, hs].astype(bf16)
                p, ds = _dil_probs(q, k, mask, l_ref[:, h * HD:h * HD + 1], do, v, d_ref[:, h * HD:h * HD + 1])
                dv += _tn(p.astype(bf16), do)
                dk += _tn(ds.astype(bf16), q)
            dk_ref[:, hs] = dk
            dv_ref[:, hs] = dv

    in_specs = [own(1), own(2), own(0), nxt(0), own(0), nxt(0), own(0), nxt(0), own(0), nxt(0)]
    return pl.pallas_call(
        body, grid=grid, in_specs=in_specs, out_specs=[own(0), own(0)], out_shape=[SDS((S, DIL_W), f32)] * 2,
        name=f"dil_bwd_kv_{g}", compiler_params=_params(("parallel", "parallel"), 40 * BLK * DIL_W, 1 << 20),
    )(qkv, qkv, qkv, qkv, d_out, d_out, delta, delta, lse, lse)


def _scan_rows(x, reverse):
    row = lax.broadcasted_iota(jnp.int32, x.shape, 0)
    k = 1
    while k < S:
        if reverse:
            x = x + jnp.where(row < S - k, pltpu.roll(x, S - k, 0), 0.0)
        else:
            x = x + jnp.where(row >= k, pltpu.roll(x, k, 0), 0.0)
        k *= 2
    return x


def _forget_fwd(gf, b128):
    def body(z_ref, b_ref, f_ref):
        z = z_ref[...] + b_ref[...]
        logf = jnp.minimum(z, 0.0) - jnp.log1p(jnp.exp(-jnp.abs(z)))
        f_ref[...] = _scan_rows(logf, reverse=False)

    return pl.pallas_call(
        body, grid=(1,), in_specs=[pl.BlockSpec((S, 128), lambda i: (0, 0)), _vec(128)],
        out_specs=pl.BlockSpec((S, 128), lambda i: (0, 0)), out_shape=SDS((S, 128), f32), name="forget_fwd",
        compiler_params=_params(("arbitrary",), 8 * S * 128, 16 * S * 128))(gf, b128)


def _forget_bwd(gf, b128, d_f_cols, d_f_rows):
    def body(z_ref, b_ref, dfc_ref, dfr_ref, dz_ref, db_ref):
        z = z_ref[...] + b_ref[...]
        dz = _scan_rows(dfc_ref[...] + dfr_ref[...], reverse=True) * jax.nn.sigmoid(-z)
        dz_ref[...] = dz
        db_ref[...] = jnp.sum(dz, axis=0, keepdims=True)

    full = pl.BlockSpec((S, 128), lambda i: (0, 0))
    return pl.pallas_call(
        body, grid=(1,), in_specs=[full, _vec(128), full, full],
        out_specs=[full, _vec(128)], out_shape=[SDS((S, 128), f32), SDS((1, 128), f32)], name="forget_bwd",
        compiler_params=_params(("arbitrary",), 16 * S * 128, 16 * S * 128))(gf, b128, d_f_cols, d_f_rows)


def _fox_mask(i, j):
    qpos = i * TQ + lax.broadcasted_iota(jnp.int32, (TQ, TQ), 0)
    kpos = j * TQ + lax.broadcasted_iota(jnp.int32, (TQ, TQ), 1)
    return kpos <= qpos


def _fox_fwd(qb, kvb, f_cum, f_cum_t):
    nq = S // TQ

    def body(q_ref, k_ref, v_ref, fq_ref, fk_ref, o_ref, lse_ref, m_sc, l_sc, acc_sc):
        i, j = pl.program_id(0), pl.program_id(1)

        @pl.when(j == 0)
        def _():
            m_sc[...] = jnp.full_like(m_sc, NEG)
            l_sc[...] = jnp.zeros_like(l_sc)
            acc_sc[...] = jnp.zeros_like(acc_sc)

        @pl.when(j <= i)
        def _():
            mask = _fox_mask(i, j)
            for h in range(N_FOX):
                hs = slice(h * HD, (h + 1) * HD)
                s = _nt(q_ref[:, hs], k_ref[:, hs]) * SCALE + (fq_ref[:, h:h + 1] - fk_ref[h:h + 1, :])
                s = jnp.where(mask, s, NEG)
                m_old = m_sc[h]
                m_new = jnp.maximum(m_old, jnp.max(s, axis=-1, keepdims=True))
                alpha = jnp.exp(m_old - m_new)
                p = jnp.exp(s - m_new)
                l_sc[h] = alpha * l_sc[h] + jnp.sum(p, axis=-1, keepdims=True)
                acc_sc[:, hs] = alpha * acc_sc[:, hs] + jnp.dot(p.astype(bf16), v_ref[:, hs], preferred_element_type=f32)
                m_sc[h] = m_new

        @pl.when(j == i)
        def _():
            lse_ref[...] = jnp.zeros_like(lse_ref)
            for h in range(N_FOX):
                hs = slice(h * HD, (h + 1) * HD)
                o_ref[:, hs] = acc_sc[:, hs] * (1.0 / l_sc[h])
                lse_ref[:, h:h + 1] = m_sc[h] + jnp.log(l_sc[h])

    jm = lambda i, j: jnp.minimum(i, j)
    in_specs = [
        pl.BlockSpec((TQ, FOX_W), lambda i, j: (i, 0)),
        pl.BlockSpec((TQ, FOX_W), lambda i, j: (jm(i, j), 0)),
        pl.BlockSpec((TQ, FOX_W), lambda i, j: (jm(i, j), 1)),
        pl.BlockSpec((TQ, 128), lambda i, j: (i, 0)),
        pl.BlockSpec((N_FOX, TQ), lambda i, j: (0, jm(i, j))),
    ]
    return pl.pallas_call(
        body, grid=(nq, nq), in_specs=in_specs,
        out_specs=[pl.BlockSpec((TQ, FOX_W), lambda i, j: (i, 0)), pl.BlockSpec((TQ, 128), lambda i, j: (i, 0))],
        out_shape=[SDS((S, FOX_W), f32), SDS((S, 128), f32)],
        scratch_shapes=[pltpu.VMEM((N_FOX, TQ, 1), f32), pltpu.VMEM((N_FOX, TQ, 1), f32), pltpu.VMEM((TQ, FOX_W), f32)],
        name="fox_fwd", compiler_params=_params(("parallel", "arbitrary"), 12 * TQ * FOX_W, 8 << 20),
    )(qb, kvb, kvb, f_cum, f_cum_t)


def _fox_bwd(qb, kvb, f_cum, f_cum_t, lse, d_out, delta):
    nq = S // TQ

    def body(k_ref, v_ref, q_ref, do_ref, fq_ref, fk_ref, lse_ref, dl_ref, dq_ref, dkv_ref, df_ref, dfr_ref):
        jb, i = pl.program_id(0), pl.program_id(1)
        dk_ref, dv_ref = dkv_ref.at[:, :FOX_W], dkv_ref.at[:, FOX_W:]

        @pl.when((jb == 0) & (i == 0))
        def _():
            dq_ref[...] = jnp.zeros_like(dq_ref)
            dfr_ref[...] = jnp.zeros_like(dfr_ref)

        @pl.when(i == 0)
        def _():
            dkv_ref[...] = jnp.zeros_like(dkv_ref)
            df_ref[...] = jnp.zeros_like(df_ref)

        @pl.when(i >= jb)
        def _():
            mask = _fox_mask(i, jb)
            rows = pl.ds(pl.multiple_of(i * TQ, TQ), TQ)
            for h in range(N_FOX):
                hs = slice(h * HD, (h + 1) * HD)
                q, k, v, do = q_ref[:, hs], k_ref[:, hs], v_ref[:, hs], do_ref[:, hs]
                s = _nt(q, k) * SCALE + (fq_ref[:, h:h + 1] - fk_ref[h:h + 1, :])
                p = jnp.exp(jnp.where(mask, s, NEG) - lse_ref[:, h:h + 1])
                ds = p * (_nt(do, v) - dl_ref[:, h:h + 1])
                dsb = ds.astype(bf16)
                dv_ref[:, hs] += _tn(p.astype(bf16), do)
                dk_ref[:, hs] += _tn(dsb, q) * SCALE
                dq_ref[rows, hs] += jnp.dot(dsb, k, preferred_element_type=f32) * SCALE
                df_ref[h:h + 1, :] -= jnp.sum(ds, axis=0, keepdims=True)
                dfr_ref[rows, h:h + 1] += jnp.sum(ds, axis=-1, keepdims=True)

    im = lambda jb, i: jnp.maximum(jb, i)
    in_specs = [
        pl.BlockSpec((TQ, FOX_W), lambda jb, i: (jb, 0)),
        pl.BlockSpec((TQ, FOX_W), lambda jb, i: (jb, 1)),
        pl.BlockSpec((TQ, FOX_W), lambda jb, i: (im(jb, i), 0)),
        pl.BlockSpec((TQ, FOX_W), lambda jb, i: (im(jb, i), 0)),
        pl.BlockSpec((TQ, 128), lambda jb, i: (im(jb, i), 0)),
        pl.BlockSpec((N_FOX, TQ), lambda jb, i: (0, jb)),
        pl.BlockSpec((TQ, 128), lambda jb, i: (im(jb, i), 0)),
        pl.BlockSpec((TQ, 128), lambda jb, i: (im(jb, i), 0)),
    ]
    out_specs = [
        pl.BlockSpec((S, FOX_W), lambda jb, i: (0, 0)),
        pl.BlockSpec((TQ, 2 * FOX_W), lambda jb, i: (jb, 0)),
        pl.BlockSpec((N_FOX, TQ), lambda jb, i: (0, jb)),
        pl.BlockSpec((S, 128), lambda jb, i: (0, 0)),
    ]
    return pl.pallas_call(
        body, grid=(nq, nq), in_specs=in_specs, out_specs=out_specs,
        out_shape=[SDS((S, FOX_W), f32), SDS((S, 2 * FOX_W), f32), SDS((N_FOX, S), f32), SDS((S, 128), f32)],
        name="fox_bwd", compiler_params=_params(("arbitrary", "arbitrary"), 4 * S * (FOX_W + 128) + 20 * TQ * FOX_W, 8 << 20),
    )(kvb, kvb, qb, d_out, f_cum, f_cum_t, lse, delta)


def _merge_fwd(out_a, out_b, w_a, w_b, gf):
    cw = D // N_SHARD

    def body(oa_ref, ob_ref, wa_ref, wb_ref, ga_ref, gb_ref, ya_ref, yb_ref, mg_ref):
        oa, ob = oa_ref[...].astype(bf16), ob_ref[...].astype(bf16)
        for j in range(N_SHARD):
            cols = slice(j * cw, (j + 1) * cw)
            ya = jnp.dot(oa, wa_ref[j], preferred_element_type=f32)
            yb = jnp.dot(ob, wb_ref[j], preferred_element_type=f32)
            ya_ref[:, cols] = ya
            yb_ref[:, cols] = yb
            mg_ref[:, cols] = (jax.nn.sigmoid(ga_ref[:, cols]) * ya + jax.nn.sigmoid(gb_ref[:, cols]) * yb).astype(bf16)

    full = lambda a: pl.BlockSpec(a.shape, lambda i: (0, 0, 0))
    return pl.pallas_call(
        body, grid=(S // TM,),
        in_specs=[_row(DIL_W), _row(FOX_W), full(w_a), full(w_b), _row(D), pl.BlockSpec((TM, D), lambda i: (i, 1))],
        out_specs=[_row(D)] * 3, out_shape=[SDS((S, D), f32), SDS((S, D), f32), SDS((S, D), bf16)], name="merge_fwd",
        compiler_params=_params(("parallel",), 22 * TM * D + 2 * (DIL_W + FOX_W) * D, 16 * TM * D),
    )(out_a, out_b, w_a, w_b, gf, gf)


def _merge_bwd(d_merged, ya, yb, gf):
    def body(dm_ref, ya_ref, yb_ref, ga_ref, gb_ref, dya_ref, dyb_ref, dg_ref):
        dm = dm_ref[...]
        sa, sb = jax.nn.sigmoid(ga_ref[...]), jax.nn.sigmoid(gb_ref[...])
        dya_ref[...] = (dm * sa).astype(bf16)
        dyb_ref[...] = (dm * sb).astype(bf16)
        dg_ref[:, :D] = (dm * ya_ref[...] * sa * (1.0 - sa)).astype(bf16)
        dg_ref[:, D:] = (dm * yb_ref[...] * sb * (1.0 - sb)).astype(bf16)

    return pl.pallas_call(
        body, grid=(S // TM,),
        in_specs=[_row(D)] * 4 + [pl.BlockSpec((TM, D), lambda i: (i, 1))],
        out_specs=[_row(D), _row(D), _row(2 * D)],
        out_shape=[SDS((S, D), bf16), SDS((S, D), bf16), SDS((S, 2 * D), bf16)], name="merge_bwd",
        compiler_params=_params(("parallel",), 28 * TM * D, 24 * TM * D))(d_merged, ya, yb, gf, gf)


def _branch_bwd(d_ya, d_yb, w_a, w_b, out_a, out_b):
    cw = D // N_SHARD

    def body(dya_ref, dyb_ref, wa_ref, wb_ref, oa_ref, ob_ref, doa_ref, dla_ref, dob_ref, dlb_ref):
        doa = jnp.zeros((TM, DIL_W), f32)
        dob = jnp.zeros((TM, FOX_W), f32)
        for j in range(N_SHARD):
            cols = slice(j * cw, (j + 1) * cw)
            doa += _nt(dya_ref[:, cols], wa_ref[j])
            dob += _nt(dyb_ref[:, cols], wb_ref[j])
        doa_ref[...] = doa
        dob_ref[...] = dob.astype(bf16)
        prod_a = doa * oa_ref[...]
        for h in range(SLOTS):
            hs = slice(h * HD, (h + 1) * HD)
            dla_ref[:, hs] = jnp.broadcast_to(jnp.sum(prod_a[:, hs], axis=-1, keepdims=True), (TM, HD))
        prod_b = dob * ob_ref[...]
        dlb_ref[...] = jnp.zeros_like(dlb_ref)
        for h in range(N_FOX):
            dlb_ref[:, h:h + 1] = jnp.sum(prod_b[:, h * HD:(h + 1) * HD], axis=-1, keepdims=True)

    full = lambda a: pl.BlockSpec(a.shape, lambda i: (0, 0, 0))
    return pl.pallas_call(
        body, grid=(S // TM,),
        in_specs=[_row(D), _row(D), full(w_a), full(w_b), _row(DIL_W), _row(FOX_W)],
        out_specs=[_row(DIL_W), _row(DIL_W), _row(FOX_W), _row(128)],
        out_shape=[SDS((S, DIL_W), f32), SDS((S, DIL_W), f32), SDS((S, FOX_W), bf16), SDS((S, 128), f32)],
        name="branch_bwd", compiler_params=_params(("parallel",), 8 * TM * D + 2 * (DIL_W + FOX_W) * D, 8 * TM * D),
    )(d_ya, d_yb, w_a, w_b, out_a, out_b)


def _branch_grads(out_a, out_b, d_ya, d_yb):
    cw = D // N_SHARD

    def body(oa_ref, ob_ref, dya_ref, dyb_ref, ga_ref, gb_ref):
        ga_ref[...] = _tn(oa_ref[...].astype(bf16), dya_ref[...])
        gb_ref[...] = _tn(ob_ref[...].astype(bf16), dyb_ref[...])

    whole = lambda w: pl.BlockSpec((S, w), lambda j: (0, 0))
    cols = pl.BlockSpec((S, cw), lambda j: (0, j))
    return pl.pallas_call(
        body, grid=(N_SHARD,), in_specs=[whole(DIL_W), whole(FOX_W), cols, cols],
        out_specs=[pl.BlockSpec((None, DIL_W, cw), lambda j: (j, 0, 0)), pl.BlockSpec((None, FOX_W, cw), lambda j: (j, 0, 0))],
        out_shape=[SDS((N_SHARD, DIL_W, cw), f32), SDS((N_SHARD, FOX_W, cw), f32)], name="grad_w_proj_ab",
        compiler_params=_params(("parallel",), 4 * S * (DIL_W + FOX_W) + 4 * S * cw + 4 * (DIL_W + FOX_W) * cw,
                                4 * S * (DIL_W + FOX_W)))(out_a, out_b, d_ya, d_yb)


FF_TN = F_FF // 2
FF_TM = 512


def _ffn_fwd(h, w_gate_t, w_up_t):
    def body(h_ref, wg_ref, wu_ref, g_ref, u_ref, a_ref):
        hb = h_ref[...]
        g = _nt(hb, wg_ref[...])
        u = _nt(hb, wu_ref[...])
        g_ref[...] = g
        u_ref[...] = u
        a_ref[...] = (g * jax.nn.sigmoid(g) * u).astype(bf16)

    tile = pl.BlockSpec((FF_TM, FF_TN), lambda j, i: (i, j))
    wspec = pl.BlockSpec((FF_TN, D), lambda j, i: (j, 0))
    return pl.pallas_call(
        body, grid=(F_FF // FF_TN, S // FF_TM),
        in_specs=[pl.BlockSpec((FF_TM, D), lambda j, i: (i, 0)), wspec, wspec], out_specs=[tile] * 3,
        out_shape=[SDS((S, F_FF), f32), SDS((S, F_FF), f32), SDS((S, F_FF), bf16)], name="ffn_fwd",
        compiler_params=_params(("parallel", "parallel"), 2 * FF_TM * D + 4 * D * FF_TN + 10 * FF_TM * FF_TN, 16 * FF_TM * FF_TN),
    )(h, w_gate_t, w_up_t)


def _ffn_bwd_act(d_ff, w_down, g_act, u_act):
    def body(d_ref, wd_ref, g_ref, u_ref, dg_ref, du_ref):
        da = _nt(d_ref[...], wd_ref[...])
        g = g_ref[...]
        sg = jax.nn.sigmoid(g)
        du_ref[...] = (da * g * sg).astype(bf16)
        dg_ref[...] = (da * u_ref[...] * sg * (1.0 + g * (1.0 - sg))).astype(bf16)

    tile = pl.BlockSpec((FF_TM, FF_TN), lambda j, i: (i, j))
    return pl.pallas_call(
        body, grid=(F_FF // FF_TN, S // FF_TM),
        in_specs=[pl.BlockSpec((FF_TM, D), lambda j, i: (i, 0)), pl.BlockSpec((FF_TN, D), lambda j, i: (j, 0)), tile, tile],
        out_specs=[tile, tile], out_shape=[SDS((S, F_FF), bf16)] * 2, name="ffn_bwd_act",
        compiler_params=_params(("parallel", "parallel"), 2 * FF_TM * D + 2 * D * FF_TN + 12 * FF_TM * FF_TN, 16 * FF_TM * FF_TN),
    )(d_ff, w_down, g_act, u_act)


def _row_tile(rows):
    return next(t for t in (376, 128, 176, 64, 32, 16, 8) if rows % t == 0)


def _adamw(w, g, m, v, name):
    rows, cols = w.shape
    tm = _row_tile(rows)
    c1 = 1.0 - ADAM_B1 ** ADAM_STEP
    c2 = 1.0 - ADAM_B2 ** ADAM_STEP

    def body(w_ref, g_ref, m_ref, v_ref, d_ref, nm_ref, nv_ref):
        g = g_ref[...]
        m_new = ADAM_B1 * m_ref[...] + (1.0 - ADAM_B1) * g
        v_new = ADAM_B2 * v_ref[...] + (1.0 - ADAM_B2) * (g * g)
        d_ref[...] = -ADAM_LR * ((m_new / c1) / (jnp.sqrt(v_new / c2) + ADAM_EPS) + ADAM_WD * w_ref[...])
        nm_ref[...] = m_new
        nv_ref[...] = v_new

    spec = pl.BlockSpec((tm, cols), lambda i: (i, 0))
    return pl.pallas_call(
        body, grid=(rows // tm,), in_specs=[spec] * 4, out_specs=[spec] * 3, out_shape=[SDS(w.shape, f32)] * 3,
        name=name, compiler_params=_params(("parallel",), 28 * tm * cols, 16 * tm * cols))(w, g, m, v)


_ANY = pl.BlockSpec(memory_space=pl.ANY)


def _place():
    x, y, c = lax.axis_index("x"), lax.axis_index("y"), lax.axis_index("c")
    chips = [(1 - x, y), (x, 1 - y), (1 - x, 1 - y)]
    return x, y, c, chips


def _halved(t):
    return t.reshape(t.shape[:-2] + (2, t.shape[-2] // 2, t.shape[-1]))


def _all_gather_weights(shards):
    n = len(shards)

    def body(*refs):
        src, out = refs[:n], refs[n:2 * n]
        send_ici, recv_ici, send_d2d, recv_d2d, local_sems = refs[2 * n:]
        x, y, c, chips = _place()
        sibling = (x, y, 1 - c)
        me_j = 2 * x + y
        own = [pltpu.make_async_copy(src[a], out[a].at[me_j], local_sems.at[a]) for a in range(n)]
        for cp in own:
            cp.start()
        sends = []
        for a in range(n):
            for p in range(3):
                cp = pltpu.make_async_remote_copy(
                    src_ref=src[a].at[c], dst_ref=out[a].at[me_j, c], send_sem=send_ici.at[a, p],
                    recv_sem=recv_ici.at[a, p], device_id=(*chips[p], c), device_id_type=MESH)
                cp.start()
                sends.append(cp)
        for a in range(n):
            for p, (px, py) in enumerate(chips):
                blk = out[a].at[2 * px + py, c]
                pltpu.make_async_remote_copy(
                    src_ref=blk, dst_ref=blk, send_sem=send_ici.at[a, p], recv_sem=recv_ici.at[a, p],
                    device_id=sibling, device_id_type=MESH).wait_recv()
                fw = pltpu.make_async_remote_copy(
                    src_ref=blk, dst_ref=blk, send_sem=send_d2d.at[a, p], recv_sem=recv_d2d.at[a, p],
                    device_id=sibling, device_id_type=MESH)
                fw.start()
                sends.append(fw)
        for a in range(n):
            for p, (px, py) in enumerate(chips):
                blk = out[a].at[2 * px + py, 1 - c]
                pltpu.make_async_remote_copy(
                    src_ref=blk, dst_ref=blk, send_sem=send_d2d.at[a, p], recv_sem=recv_d2d.at[a, p],
                    device_id=sibling, device_id_type=MESH).wait_recv()
        for cp in sends:
            cp.wait_send()
        for cp in own:
            cp.wait()

    return pl.pallas_call(
        body, in_specs=[_ANY] * n, out_specs=[_ANY] * n,
        out_shape=[SDS((N_SHARD,) + t.shape, t.dtype) for t in shards],
        scratch_shapes=[pltpu.SemaphoreType.DMA((n, 3))] * 4 + [pltpu.SemaphoreType.DMA((n,))],
        name="all_gather_weights", compiler_params=pltpu.CompilerParams(has_side_effects=True))(*shards)


def _pair_swap(grads):
    n = len(grads)

    def body(*refs):
        src, out, send_sems, recv_sems = refs[:n], refs[n:2 * n], refs[2 * n], refs[2 * n + 1]
        x, y, c, _ = _place()
        copies = [pltpu.make_async_remote_copy(
            src_ref=src[a].at[:, 1 - c], dst_ref=out[a], send_sem=send_sems.at[a], recv_sem=recv_sems.at[a],
            device_id=(x, y, 1 - c), device_id_type=MESH) for a in range(n)]
        for cp in copies:
            cp.start()
        for cp in copies:
            cp.wait()

    return pl.pallas_call(
        body, in_specs=[_ANY] * n, out_specs=[_ANY] * n,
        out_shape=[SDS((N_SHARD,) + t.shape[2:], t.dtype) for t in grads],
        scratch_shapes=[pltpu.SemaphoreType.DMA((n,)), pltpu.SemaphoreType.DMA((n,))], name="pair_swap",
        compiler_params=pltpu.CompilerParams(has_side_effects=True))(*grads)


def _pair_sum(grads, other, name):
    _, _, rows, cols = grads.shape
    tr = _row_tile(rows)
    core = lax.axis_index("c").astype(jnp.int32).reshape(1)

    def body(c_ref, g_ref, o_ref, out_ref):
        out_ref[...] = (g_ref[...] + o_ref[...]).astype(bf16)

    grid_spec = pltpu.PrefetchScalarGridSpec(
        num_scalar_prefetch=1, grid=(N_SHARD, rows // tr),
        in_specs=[pl.BlockSpec((None, None, tr, cols), lambda j, i, c_ref: (j, c_ref[0], i, 0)),
                  pl.BlockSpec((None, tr, cols), lambda j, i, c_ref: (j, i, 0))],
        out_specs=pl.BlockSpec((None, tr, cols), lambda j, i, c_ref: (j, i, 0)))
    return pl.pallas_call(
        body, grid_spec=grid_spec, out_shape=SDS((N_SHARD, rows, cols), bf16), name=name,
        compiler_params=_params(("parallel", "parallel"), 10 * tr * cols, 12 * tr * cols))(core, grads, other)


def _scatter_partials(parts, small):
    n = len(parts)

    def body(*refs):
        part, small_ref, recv, small_all_ref = refs[:n], refs[n], refs[n + 1:2 * n + 1], refs[2 * n + 1]
        send_sems, recv_sems, ssend, srecv, local_sems = refs[2 * n + 2:]
        x, y, c, chips = _place()
        me_j = 2 * x + y
        me_dev = 4 * x + 2 * y + c
        own = [pltpu.make_async_copy(small_ref, small_all_ref.at[me_dev], local_sems.at[n])]
        own += [pltpu.make_async_copy(part[a].at[me_j], recv[a].at[me_j], local_sems.at[a]) for a in range(n)]
        for cp in own:
            cp.start()
        sends = []
        for a in range(n):
            for p, (px, py) in enumerate(chips):
                cp = pltpu.make_async_remote_copy(
                    src_ref=part[a].at[2 * px + py], dst_ref=recv[a].at[me_j], send_sem=send_sems.at[a, p],
                    recv_sem=recv_sems.at[a, p], device_id=(px, py, c), device_id_type=MESH)
                cp.start()
                sends.append(cp)
        flip = lambda a, bit: 1 - a if bit else a
        peers = [(flip(x, k & 4), flip(y, k & 2), flip(c, k & 1)) for k in range(1, 8)]
        for k, to in enumerate(peers):
            cp = pltpu.make_async_remote_copy(
                src_ref=small_ref, dst_ref=small_all_ref.at[me_dev],
                send_sem=ssend.at[k], recv_sem=srecv.at[k], device_id=to, device_id_type=MESH)
            cp.start()
            sends.append(cp)
        for a in range(n):
            for p, (px, py) in enumerate(chips):
                slot = recv[a].at[2 * px + py]
                pltpu.make_async_remote_copy(
                    src_ref=slot, dst_ref=slot, send_sem=send_sems.at[a, p], recv_sem=recv_sems.at[a, p],
                    device_id=(px, py, c), device_id_type=MESH).wait_recv()
        for k, (px, py, pc) in enumerate(peers):
            slot = small_all_ref.at[4 * px + 2 * py + pc]
            pltpu.make_async_remote_copy(
                src_ref=slot, dst_ref=slot, send_sem=ssend.at[k], recv_sem=srecv.at[k],
                device_id=(px, py, pc), device_id_type=MESH).wait_recv()
        for cp in sends:
            cp.wait_send()
        for cp in own:
            cp.wait()

    return pl.pallas_call(
        body, in_specs=[_ANY] * (n + 1), out_specs=[_ANY] * (n + 1),
        out_shape=[SDS(t.shape, t.dtype) for t in parts] + [SDS((8, SMALL_ROWS, D), f32)],
        scratch_shapes=[pltpu.SemaphoreType.DMA((n, 3)), pltpu.SemaphoreType.DMA((n, 3)),
                        pltpu.SemaphoreType.DMA((7,)), pltpu.SemaphoreType.DMA((7,)), pltpu.SemaphoreType.DMA((n + 1,))],
        name="scatter_partials", compiler_params=pltpu.CompilerParams(has_side_effects=True))(*parts, small)


def _sum_partials(recv, name):
    _, rows, cols = recv.shape
    tr = _row_tile(rows)

    def body(r0, r1, r2, r3, out_ref):
        acc = r0[...].astype(f32)
        for r in (r1, r2, r3):
            acc = acc + r[...].astype(f32)
        out_ref[...] = acc

    slot = lambda j: pl.BlockSpec((None, tr, cols), lambda i: (j, i, 0))
    return pl.pallas_call(
        body, grid=(rows // tr,), in_specs=[slot(0), slot(1), slot(2), slot(3)],
        out_specs=pl.BlockSpec((tr, cols), lambda i: (i, 0)), out_shape=SDS((rows, cols), f32), name=name,
        compiler_params=_params(("parallel",), 12 * tr * cols, 12 * tr * cols))(recv, recv, recv, recv)


def _sum_small(small_all):
    def body(small_ref, out_ref):
        tot = small_ref[0]
        for k in range(1, 8):
            tot = tot + small_ref[k]
        out_ref[...] = tot

    return pl.pallas_call(
        body, grid=(1,), in_specs=[pl.BlockSpec((8, SMALL_ROWS, D), lambda i: (0, 0, 0))],
        out_specs=pl.BlockSpec((SMALL_ROWS, D), lambda i: (0, 0)), out_shape=SDS((SMALL_ROWS, D), f32),
        name="sum_small", compiler_params=_params(("arbitrary",), 36 * SMALL_ROWS * D))(small_all)


def _swap_halves(halves):
    n = len(halves)

    def body(*refs):
        src, out, send_sems, recv_sems, local_sems = refs[:n], refs[n:2 * n], refs[2 * n], refs[2 * n + 1], refs[2 * n + 2]
        x, y, c, _ = _place()
        own, sends = [], []
        for a in range(n):
            own.append(pltpu.make_async_copy(src[a], out[a].at[c], local_sems.at[a]))
            own[-1].start()
            sends.append(pltpu.make_async_remote_copy(
                src_ref=src[a], dst_ref=out[a].at[c], send_sem=send_sems.at[a], recv_sem=recv_sems.at[a],
                device_id=(x, y, 1 - c), device_id_type=MESH))
            sends[-1].start()
        for a in range(n):
            theirs = out[a].at[1 - c]
            pltpu.make_async_remote_copy(
                src_ref=theirs, dst_ref=theirs, send_sem=send_sems.at[a], recv_sem=recv_sems.at[a],
                device_id=(x, y, 1 - c), device_id_type=MESH).wait_recv()
        for cp in sends:
            cp.wait_send()
        for cp in own:
            cp.wait()

    return pl.pallas_call(
        body, in_specs=[_ANY] * n, out_specs=[_ANY] * n, out_shape=[SDS((2,) + t.shape, f32) for t in halves],
        scratch_shapes=[pltpu.SemaphoreType.DMA((n,))] * 3, name="swap_halves",
        compiler_params=pltpu.CompilerParams(has_side_effects=True))(*halves)


def _kernel_layout(name, t):
    t = t[0]
    if name in TRANSPOSED:
        t = jnp.swapaxes(t, 0, 1)
    return _pad_rows(t, SHARD_SHAPE[name][0])


def _harness_layout(name, t):
    if name == "w_in":
        t = t[:IN_SHARD]
    if name in TRANSPOSED:
        t = jnp.swapaxes(t, 0, 1)
    return t[None]


def _pad_rows(t, rows):
    return t if t.shape[0] == rows else jnp.pad(t, ((0, rows - t.shape[0]), (0, 0)))


_QA, _KA, _VA, _QB, _KVB, _F, _GAB = 0, 768, 1536, 2304, 2816, 3840, 3848


def _full_weights(gathered):
    full = {n: t.reshape((N_SHARD,) + SHARD_SHAPE[n]) for n, t in gathered.items()}
    w_in_t = full["w_in"][:, :IN_SHARD].reshape(IN_COLS, D)
    group = lambda g: jnp.concatenate([w_in_t[o + g * DIL_W:o + (g + 1) * DIL_W] for o in (_QA, _KA, _VA)], axis=0)
    return dict(
        w_a_t=[group(g) for g in range(3)],
        w_qb_t=w_in_t[_QB:_KVB],
        w_kvb_t=w_in_t[_KVB:_F],
        w_f_t=jnp.concatenate([w_in_t[_F:_GAB], jnp.zeros((128 - N_FOX, D), bf16)], axis=0),
        w_gab_t=w_in_t[_GAB:],
        w_a4=full["w_proj_a"],
        w_b4=full["w_proj_b"],
        w_out=full["w_out"].reshape(D, D),
        w_gate_t=full["w_ffn_gate"].reshape(F_FF, D),
        w_up_t=full["w_ffn_up"].reshape(F_FF, D),
        w_down=full["w_ffn_down"].reshape(F_FF, D),
    )


def _sharded_grads(g):
    parts = [g["w_a_t"][k][o:o + DIL_W] for o in (0, DIL_W, 2 * DIL_W) for k in range(3)]
    parts += [g["w_qb_t"], g["w_kvb_t"], g["w_f_t"][:N_FOX], g["w_gab_t"]]
    w_in_t = jnp.concatenate(parts, axis=0).reshape(N_SHARD, IN_SHARD, D)
    full = dict(w_in=jnp.pad(w_in_t, ((0, 0), (0, IN_SHARD_PAD - IN_SHARD), (0, 0))), w_proj_a=g["w_a4"],
                w_proj_b=g["w_b4"], w_out=g["w_out"], w_ffn_gate=g["w_gate_t"], w_ffn_up=g["w_up_t"],
                w_ffn_down=g["w_down"])
    return {n: _halved(full[n].reshape((N_SHARD,) + SHARD_SHAPE[n])) for n in W_NAMES}


def _local_step(x, target, wt, b_forget, g_mix_pre, g_mix_post, g_ffn_pre, g_ffn_post):
    tables = _rope_tables()
    b128 = jnp.pad(b_forget, ((0, 0), (0, 128 - N_FOX)))
    dils = tuple(d for _, d in DIL_GROUPS[1:])

    hs = _norm_fwd([x] + list(_perm_rows([x], dils, "perm_x")), g_mix_pre)
    h1 = hs[0]
    qkv = [_rope_fwd(g, _mm([(hs[g], wt["w_a_t"][g])], "nt", f32, tm=1024, tn=QKV_W, name=f"proj_a_{g}"), tables)
           for g in range(3)]
    qb = _mm([(h1, wt["w_qb_t"])], "nt", bf16, tm=1024, tn=FOX_W, name="proj_qb")
    kvb = _mm([(h1, wt["w_kvb_t"])], "nt", bf16, tm=1024, tn=2 * FOX_W, name="proj_kvb")
    gab = _mm([(h1, wt["w_gab_t"])], "nt", f32, tm=512, tn=2 * D, name="proj_gab")
    fz = _mm([(h1, wt["w_f_t"])], "nt", f32, tm=1024, tn=128, name="proj_f")
    dil = [_dil_fwd(g, qkv[g]) for g in range(3)]
    out_a, lse_a = _dil_combine([o for o, _ in dil], [l for _, l in dil])
    f_cum = _forget_fwd(fz, b128)
    f_cum_t = f_cum[:, :N_FOX].T
    out_b, lse_b = _fox_fwd(qb, kvb, f_cum, f_cum_t)
    ya, yb, merged = _merge_fwd(out_a, out_b, wt["w_a4"], wt["w_b4"], gab)
    mix = _mm([(merged, wt["w_out"])], "nn", f32, tm=1024, tn=D, name="proj_out")
    x2, h3 = _resid_norm_fwd(x, mix, g_mix_post, g_ffn_pre)
    g_act, u_act, a_act = _ffn_fwd(h3, wt["w_gate_t"], wt["w_up_t"])
    ff = _mm([(a_act, wt["w_down"])], "nn", f32, tm=1024, tn=D, name="ffn_down")
    sq_err, dy, d_ff, dg_ffn_post = _loss_head(x2, ff, g_ffn_post, target)

    grads = {}
    d_g, d_u = _ffn_bwd_act(d_ff, wt["w_down"], g_act, u_act)
    grads["w_down"] = _mm([(a_act, d_ff)], "tn", f32, tm=FF_TN, tn=512, name="grad_w_down")
    grads["w_gate_t"] = _mm([(d_g, h3)], "tn", f32, tm=FF_TN, tn=512, name="grad_w_gate")
    grads["w_up_t"] = _mm([(d_u, h3)], "tn", f32, tm=FF_TN, tn=512, name="grad_w_up")
    d_h3 = _mm([(d_g, wt["w_gate_t"]), (d_u, wt["w_up_t"])], "nn", f32, tm=512, tn=512, name="ffn_bwd_in")
    dx2, d_mix, dg_ffn_pre, dg_mix_post = _norm_bwd_mid(dy, d_h3, x2, mix, g_ffn_pre, g_mix_post)

    grads["w_out"] = _mm([(merged, d_mix)], "tn", f32, tm=D, tn=D, name="grad_w_out")
    d_merged = _mm([(d_mix, wt["w_out"])], "nt", f32, tm=1024, tn=D, name="proj_out_bwd")
    d_ya, d_yb, d_gab = _merge_bwd(d_merged, ya, yb, gab)
    grads["w_a4"], grads["w_b4"] = _branch_grads(out_a, out_b, d_ya, d_yb)
    d_out_a, delta_a, d_out_b, delta_b = _branch_bwd(d_ya, d_yb, wt["w_a4"], wt["w_b4"], out_a, out_b)

    perm = _perm_rows([d_out_a, delta_a, lse_a], dils, "perm_dil_bwd")
    aux = [(d_out_a, delta_a, lse_a)] + [tuple(perm[k * len(dils) + i] for k in range(3)) for i in range(len(dils))]
    d_qkv = []
    for g in range(3):
        dq = _dil_bwd_q(g, qkv[g], *aux[g])
        dk, dv = _dil_bwd_kv(g, qkv[g], *aux[g])
        d_qkv.append(_rope_bwd(g, dq, dk, dv, tables))
    dq_b, dkv_b, d_f_t, d_f_rows = _fox_bwd(qb, kvb, f_cum, f_cum_t, lse_b, d_out_b, delta_b)
    d_f_cols = jnp.pad(d_f_t.T, ((0, 0), (0, 128 - N_FOX)))
    d_z, d_b128 = _forget_bwd(fz, b128, d_f_cols, d_f_rows)

    grads["w_a_t"] = [_mm([(d_qkv[g], hs[g])], "tn", f32, tm=QKV_W, tn=D, name=f"grad_w_a_{g}") for g in range(3)]
    grads["w_qb_t"] = _mm([(dq_b, h1)], "tn", f32, tm=FOX_W, tn=D, name="grad_w_qb")
    grads["w_kvb_t"] = _mm([(dkv_b, h1)], "tn", f32, tm=FOX_W, tn=D, name="grad_w_kvb")
    grads["w_gab_t"] = _mm([(d_gab, h1)], "tn", f32, tm=D, tn=D, name="grad_w_gab")
    grads["w_f_t"] = _mm([(d_z, h1)], "tn", f32, tm=128, tn=D, name="grad_w_f")
    d_h1_nat = _mm([(d_qkv[0], wt["w_a_t"][0]), (dq_b, wt["w_qb_t"]), (dkv_b, wt["w_kvb_t"]), (d_gab, wt["w_gab_t"]),
                    (d_z, wt["w_f_t"])], "nn", f32, tm=512, tn=512, name="proj_in_bwd")
    d_h1_dil = [_mm([(d_qkv[g], wt["w_a_t"][g])], "nn", f32, tm=1024, tn=D, name=f"proj_a_bwd_{g}") for g in (1, 2)]
    d_h1 = _unperm_sum(d_h1_nat, d_h1_dil, dils, "unperm_d_h1")
    grad_x, dg_mix_pre = _norm_bwd_in(dx2, d_h1, x, g_mix_pre)

    small = dict(b_forget=d_b128[:, :N_FOX], norm_mix_pre=dg_mix_pre, norm_mix_post=dg_mix_post,
                 norm_ffn_pre=dg_ffn_pre, norm_ffn_post=dg_ffn_post)
    return sq_err, grad_x, grads, small


NORMS = ("norm_mix_pre", "norm_mix_post", "norm_ffn_pre", "norm_ffn_post")
ORDER = ("w_in", "w_proj_a", "w_proj_b", "w_out", "b_forget", "w_ffn_gate", "w_ffn_up", "w_ffn_down") + NORMS


def kernel(x, w_in, w_proj_a, w_proj_b, w_out, b_forget, w_ffn_gate, w_ffn_up, w_ffn_down, norm_mix_pre, norm_mix_post, norm_ffn_pre, norm_ffn_post, loss_target, m_w_in, m_w_proj_a, m_w_proj_b, m_w_out, m_b_forget, m_w_ffn_gate, m_w_ffn_up, m_w_ffn_down, m_norm_mix_pre, m_norm_mix_post, m_norm_ffn_pre, m_norm_ffn_post, v_w_in, v_w_proj_a, v_w_proj_b, v_w_out, v_b_forget, v_w_ffn_gate, v_w_ffn_up, v_w_ffn_down, v_norm_mix_pre, v_norm_mix_post, v_norm_ffn_pre, v_norm_ffn_post):
    given = dict(w_in=w_in, w_proj_a=w_proj_a, w_proj_b=w_proj_b, w_out=w_out, w_ffn_gate=w_ffn_gate,
                 w_ffn_up=w_ffn_up, w_ffn_down=w_ffn_down)
    given_m = dict(w_in=m_w_in, w_proj_a=m_w_proj_a, w_proj_b=m_w_proj_b, w_out=m_w_out, w_ffn_gate=m_w_ffn_gate,
                   w_ffn_up=m_w_ffn_up, w_ffn_down=m_w_ffn_down)
    given_v = dict(w_in=v_w_in, w_proj_a=v_w_proj_a, w_proj_b=v_w_proj_b, w_out=v_w_out, w_ffn_gate=v_w_ffn_gate,
                   w_ffn_up=v_w_ffn_up, w_ffn_down=v_w_ffn_down)
    w, m, v = ({n: _kernel_layout(n, t[n]) for n in W_NAMES} for t in (given, given_m, given_v))
    small_w = dict(b_forget=b_forget, norm_mix_pre=norm_mix_pre, norm_mix_post=norm_mix_post,
                   norm_ffn_pre=norm_ffn_pre, norm_ffn_post=norm_ffn_post)
    small_m = dict(b_forget=m_b_forget, norm_mix_pre=m_norm_mix_pre, norm_mix_post=m_norm_mix_post,
                   norm_ffn_pre=m_norm_ffn_pre, norm_ffn_post=m_norm_ffn_post)
    small_v = dict(b_forget=v_b_forget, norm_mix_pre=v_norm_mix_pre, norm_mix_post=v_norm_mix_post,
                   norm_ffn_pre=v_norm_ffn_pre, norm_ffn_post=v_norm_ffn_post)

    gathered = _all_gather_weights([_halved(w[n].astype(bf16)) for n in W_NAMES])
    wt = _full_weights(dict(zip(W_NAMES, gathered)))

    sq_err, grad_x, grads, small = _local_step(x[0], loss_target[0], wt, b_forget, norm_mix_pre, norm_mix_post,
                                               norm_ffn_pre, norm_ffn_post)
    loss = lax.psum(sq_err[0, 0] * (0.5 / D), ("x", "y", "c"))

    g4 = _sharded_grads(grads)
    stack = lambda t, fill: jnp.concatenate(
        [jnp.pad(t["b_forget"], ((0, 0), (0, D - N_FOX)))] + [t[n] for n in NORMS]
        + [jnp.full((SMALL_ROWS - 1 - len(NORMS), D), fill, f32)], axis=0)
    other = _pair_swap([g4[n] for n in W_NAMES])
    parts = [_pair_sum(g4[n], o, "pair_sum_" + n) for n, o in zip(W_NAMES, other)]
    *recv, small_all = _scatter_partials(parts, stack(small, 0.0))
    halves = [_sum_partials(r, "sum_partials_" + n) for n, r in zip(W_NAMES, recv)]
    g_shard = {n: t.reshape(SHARD_SHAPE[n]) for n, t in zip(W_NAMES, _swap_halves(halves))}
    small_sum = _sum_small(small_all)

    delta, new_m, new_v = {}, {}, {}
    for n in W_NAMES:
        delta[n], new_m[n], new_v[n] = _adamw(w[n], g_shard[n], m[n], v[n], "adamw_" + n)
    sd, sm, sv = _adamw(stack(small_w, 1.0), small_sum, stack(small_m, 1.0), stack(small_v, 1.0), "adamw_small")

    outs = [loss, grad_x[None]]
    for big, st in ((g_shard, small_sum), (delta, sd), (new_m, sm), (new_v, sv)):
        t = {n: _harness_layout(n, big[n]) for n in W_NAMES}
        t["b_forget"] = st[0:1, :N_FOX]
        for i, n in enumerate(NORMS):
            t[n] = st[i + 1:i + 2]
        outs += [t[n] for n in ORDER]
    return tuple(outs)
```

```python
import functools
import math

import jax
import jax.numpy as jnp
from jax import lax
from jax.experimental import pallas as pl
from jax.experimental.pallas import tpu as pltpu

f32 = jnp.float32
bf16 = jnp.bfloat16
SDS = jax.ShapeDtypeStruct
MESH = pl.DeviceIdType.MESH

S = 2048
D = 1024
HD = 64
BLK = 128
N_FOX = 8
FOX_W = N_FOX * HD
DIL_GROUPS = ((128, 1), (512, 4), (2048, 16))
SLOTS = 4
DIL_W = SLOTS * HD
QKV_W = 3 * DIL_W
VR_W = 3 * FOX_W
GF_W = 2 * D + 128
F_FF = 2816
ROPE_DIM = 16
ROPE_THETA = 500000.0
EPS = 1e-6
NEG = -1e30
SCALE = 1.0 / math.sqrt(HD)
IN_COLS = 5896
N_SHARD = 4

ADAM_LR, ADAM_B1, ADAM_B2, ADAM_EPS, ADAM_WD, ADAM_STEP = 0.001, 0.9, 0.999, 1e-08, 0.01, 10

VMEM_V7X = 64 * 1024 * 1024
VMEM_PLAN_MAX = VMEM_V7X - 8 * 1024 * 1024

TM = 256
TQ = 256

W_NAMES = ("w_in", "w_proj_a", "w_proj_b", "w_out", "w_ffn_gate", "w_ffn_up", "w_ffn_down")
TRANSPOSED = ("w_in", "w_ffn_gate", "w_ffn_up")
IN_SHARD = IN_COLS // N_SHARD
IN_SHARD_PAD = 1504
SHARD_SHAPE = dict(w_in=(IN_SHARD_PAD, D), w_proj_a=(DIL_W, D // N_SHARD), w_proj_b=(FOX_W, D // N_SHARD),
                   w_out=(D // N_SHARD, D), w_ffn_gate=(F_FF // N_SHARD, D), w_ffn_up=(F_FF // N_SHARD, D),
                   w_ffn_down=(F_FF // N_SHARD, D))
SMALL_ROWS = 8


def _nbytes(shape, dtype):
    return math.prod(shape) * jnp.dtype(dtype).itemsize


def _params(semantics, block_bytes, temp_bytes=0):
    need = 2 * block_bytes + temp_bytes + (2 << 20)
    return pltpu.CompilerParams(dimension_semantics=semantics,
                                vmem_limit_bytes=int(min(max(need, 16 << 20), VMEM_PLAN_MAX)))


def _row(w, tm=TM):
    return pl.BlockSpec((tm, w), lambda i: (i, 0))


def _vec(w):
    return pl.BlockSpec((1, w), lambda i: (0, 0))


def _mm(pairs, dims, out_dtype, *, tm, tn, name, m_inner=False):
    a0, b0 = pairs[0]
    m_dim = a0.shape[1] if dims == "tn" else a0.shape[0]
    n_dim = b0.shape[0] if dims == "nt" else b0.shape[1]
    contract = {"nn": ((1,), (0,)), "nt": ((1,), (1,)), "tn": ((0,), (0,))}[dims]
    n_pairs = len(pairs)
    assert m_dim % tm == 0 and n_dim % tn == 0, (name, m_dim, n_dim, tm, tn)

    def body(*refs):
        o_ref = refs[-1]
        acc = None
        for p in range(n_pairs):
            a = refs[2 * p][...].astype(bf16)
            b = refs[2 * p + 1][...].astype(bf16)
            t = lax.dot_general(a, b, (contract, ((), ())), preferred_element_type=f32)
            acc = t if acc is None else acc + t
        o_ref[...] = acc.astype(o_ref.dtype)

    if m_inner:
        grid = (n_dim // tn, m_dim // tm)
        mi = lambda j, i: i
        ni = lambda j, i: j
    else:
        grid = (m_dim // tm, n_dim // tn)
        mi = lambda i, j: i
        ni = lambda i, j: j
    in_specs, block_bytes, args = [], 0, []
    for a, b in pairs:
        k_dim = a.shape[0] if dims == "tn" else a.shape[1]
        if dims == "tn":
            in_specs.append(pl.BlockSpec((k_dim, tm), lambda *g: (0, mi(*g))))
        else:
            in_specs.append(pl.BlockSpec((tm, k_dim), lambda *g: (mi(*g), 0)))
        if dims == "nt":
            in_specs.append(pl.BlockSpec((tn, k_dim), lambda *g: (ni(*g), 0)))
        else:
            in_specs.append(pl.BlockSpec((k_dim, tn), lambda *g: (0, ni(*g))))
        block_bytes += _nbytes((tm, k_dim), a.dtype) + _nbytes((tn, k_dim), b.dtype)
        args += [a, b]
    block_bytes += _nbytes((tm, tn), out_dtype)
    temp = _nbytes((tm, tn), f32) * 2 + sum(_nbytes((tm, a.shape[0] if dims == "tn" else a.shape[1]), bf16)
                                            + _nbytes((tn, a.shape[0] if dims == "tn" else a.shape[1]), bf16)
                                            for a, _ in pairs)
    return pl.pallas_call(
        body, grid=grid, in_specs=in_specs,
        out_specs=pl.BlockSpec((tm, tn), lambda *g: (mi(*g), ni(*g))),
        out_shape=SDS((m_dim, n_dim), out_dtype), name=name,
        compiler_params=_params(("parallel", "parallel"), block_bytes, temp),
    )(*args)


def _rms(x, g):
    r = lax.rsqrt(jnp.mean(x * x, axis=-1, keepdims=True) + EPS)
    return x * r * g


def _rms_bwd(x, g, dy):
    r = lax.rsqrt(jnp.mean(x * x, axis=-1, keepdims=True) + EPS)
    xh = x * r
    dxh = dy * g
    dx = r * (dxh - xh * jnp.mean(dxh * xh, axis=-1, keepdims=True))
    return dx, jnp.sum(dy * xh, axis=0, keepdims=True)


def _acc_rows(ref, val):
    @pl.when(pl.program_id(0) == 0)
    def _():
        ref[...] = jnp.zeros_like(ref)
    ref[...] += val


def _norm_fwd(xs, g):
    n = len(xs)

    def body(*refs):
        g = refs[n][...]
        for x_ref, h_ref in zip(refs[:n], refs[n + 1:]):
            h_ref[...] = _rms(x_ref[...], g).astype(bf16)

    return pl.pallas_call(
        body, grid=(S // TM,), in_specs=[_row(D)] * n + [_vec(D)], out_specs=[_row(D)] * n,
        out_shape=[SDS((S, D), bf16)] * n, name="norm_mix_pre",
        compiler_params=_params(("parallel",), 6 * n * TM * D, 8 * n * TM * D))(*xs, g)


def _perm_rows(xs, ds, name):
    n = len(xs)

    def body(*refs):
        outs = iter(refs[n:])
        for x_ref in refs[:n]:
            for d in ds:
                o_ref, rows = next(outs), S // d
                for r in range(d):
                    o_ref[r * rows:(r + 1) * rows, :] = x_ref[pl.ds(r, rows, stride=d), :]

    blk = pl.BlockSpec((S, 128), lambda c: (0, c))
    w = xs[0].shape[1]
    return pl.pallas_call(
        body, grid=(w // 128,), in_specs=[blk] * n, out_specs=[blk] * (n * len(ds)),
        out_shape=[SDS((S, w), f32)] * (n * len(ds)), name=name,
        compiler_params=_params(("parallel",), 4 * S * 128 * n * (1 + len(ds))))(*xs)


def _unperm_sum(nat, perms, ds, name):
    n = len(perms)

    def body(*refs):
        a_ref, o_ref, sc = refs[0], refs[n + 1], refs[n + 2]
        acc = a_ref[...]
        for b_ref, d in zip(refs[1:n + 1], ds):
            rows = S // d
            for r in range(d):
                sc[pl.ds(r, rows, stride=d), :] = b_ref[r * rows:(r + 1) * rows, :]
            acc = acc + sc[...]
        o_ref[...] = acc

    blk = pl.BlockSpec((S, 128), lambda c: (0, c))
    w = nat.shape[1]
    return pl.pallas_call(
        body, grid=(w // 128,), in_specs=[blk] * (n + 1), out_specs=blk, out_shape=SDS((S, w), f32),
        scratch_shapes=[pltpu.VMEM((S, 128), f32)], name=name,
        compiler_params=_params(("parallel",), 4 * S * 128 * (n + 2), 8 * S * 128))(nat, *perms)


def _resid_norm_fwd(x, mix, g_post, g_pre):
    def body(x_ref, mix_ref, gp_ref, gn_ref, x2_ref, h_ref):
        x2 = x_ref[...] + _rms(mix_ref[...], gp_ref[...])
        x2_ref[...] = x2
        h_ref[...] = _rms(x2, gn_ref[...]).astype(bf16)

    return pl.pallas_call(
        body, grid=(S // TM,), in_specs=[_row(D), _row(D), _vec(D), _vec(D)], out_specs=[_row(D), _row(D)],
        out_shape=[SDS((S, D), f32), SDS((S, D), bf16)], name="resid_norm_mid",
        compiler_params=_params(("parallel",), 14 * TM * D, 16 * TM * D))(x, mix, g_post, g_pre)


def _loss_head(x2, ff, g_post, target):
    def body(x2_ref, ff_ref, g_ref, t_ref, loss_ref, dy_ref, dff_ref, dg_ref):
        ff = ff_ref[...]
        g = g_ref[...]
        err = x2_ref[...] + _rms(ff, g) - t_ref[...]
        dy = err * (1.0 / D)
        dff, dg = _rms_bwd(ff, g, dy)
        dy_ref[...] = dy
        dff_ref[...] = dff.astype(bf16)
        _acc_rows(dg_ref, dg)
        _acc_rows(loss_ref, jnp.full((1, 128), jnp.sum(err * err), f32))

    return pl.pallas_call(
        body, grid=(S // TM,), in_specs=[_row(D), _row(D), _vec(D), _row(D)],
        out_specs=[_vec(128), _row(D), _row(D), _vec(D)],
        out_shape=[SDS((1, 128), f32), SDS((S, D), f32), SDS((S, D), bf16), SDS((1, D), f32)], name="loss_head",
        compiler_params=_params(("arbitrary",), 18 * TM * D, 24 * TM * D))(x2, ff, g_post, target)


def _norm_bwd_mid(dy, dh3, x2, mix, g_ffn_pre, g_mix_post):
    def body(dy_ref, dh_ref, x2_ref, mix_ref, g3_ref, g2_ref, dx2_ref, dmix_ref, dg3_ref, dg2_ref):
        d3, dg3 = _rms_bwd(x2_ref[...], g3_ref[...], dh_ref[...])
        dx2 = dy_ref[...] + d3
        dmix, dg2 = _rms_bwd(mix_ref[...], g2_ref[...], dx2)
        dx2_ref[...] = dx2
        dmix_ref[...] = dmix.astype(bf16)
        _acc_rows(dg3_ref, dg3)
        _acc_rows(dg2_ref, dg2)

    return pl.pallas_call(
        body, grid=(S // TM,), in_specs=[_row(D)] * 4 + [_vec(D)] * 2,
        out_specs=[_row(D), _row(D), _vec(D), _vec(D)],
        out_shape=[SDS((S, D), f32), SDS((S, D), bf16), SDS((1, D), f32), SDS((1, D), f32)], name="norm_bwd_mid",
        compiler_params=_params(("arbitrary",), 22 * TM * D, 24 * TM * D))(dy, dh3, x2, mix, g_ffn_pre, g_mix_post)


def _norm_bwd_in(dx2, dh1, x, g):
    def body(dx2_ref, dh_ref, x_ref, g_ref, gx_ref, dg_ref):
        d1, dg = _rms_bwd(x_ref[...], g_ref[...], dh_ref[...])
        gx_ref[...] = dx2_ref[...] + d1
        _acc_rows(dg_ref, dg)

    return pl.pallas_call(
        body, grid=(S // TM,), in_specs=[_row(D)] * 3 + [_vec(D)], out_specs=[_row(D), _vec(D)],
        out_shape=[SDS((S, D), f32), SDS((1, D), f32)], name="norm_bwd_in",
        compiler_params=_params(("arbitrary",), 16 * TM * D, 16 * TM * D))(dx2, dh1, x, g)


def _rope_tables():
    half = ROPE_DIM // 2
    inv_freq = jnp.power(ROPE_THETA, -jnp.arange(0, ROPE_DIM, 2, dtype=f32) / ROPE_DIM)
    row = jnp.arange(S, dtype=jnp.int32)
    groups = []
    for _, d in DIL_GROUPS:
        pos = ((row % (S // d)) * d + row // (S // d)).astype(f32)
        ang = pos[:, None] * inv_freq[None, :]
        cos, sin = jnp.cos(ang), jnp.sin(ang)
        c = jnp.concatenate([cos, cos, jnp.ones((S, HD - ROPE_DIM), f32)], axis=1)
        s_lo = jnp.concatenate([-sin, jnp.zeros((S, HD - half), f32)], axis=1)
        s_hi = jnp.concatenate([jnp.zeros((S, half), f32), sin, jnp.zeros((S, HD - ROPE_DIM), f32)], axis=1)
        groups.append(jnp.stack([jnp.concatenate([t, t], axis=1) for t in (c, s_lo, s_hi)]))
    return jnp.stack(groups)


def _rotate(x, c, lo, hi, sign):
    tile = lambda t: jnp.tile(t, (1, DIL_W // 128))
    return (x * tile(c) + pltpu.roll(x, DIL_W - ROPE_DIM // 2, 1) * (tile(lo) * sign)
            + pltpu.roll(x, ROPE_DIM // 2, 1) * (tile(hi) * sign))


def _table_specs(g):
    return [pl.BlockSpec((None, None, TM, 128), lambda i, k=k: (g, k, i, 0)) for k in range(3)]


def _rope_fwd(g, p_qkv, tables):
    def body(x_ref, c_ref, lo_ref, hi_ref, o_ref):
        c, lo, hi = c_ref[...], lo_ref[...], hi_ref[...]
        for part in range(2):
            cols = slice(part * DIL_W, (part + 1) * DIL_W)
            o_ref[:, cols] = _rotate(x_ref[:, cols], c, lo, hi, 1.0).astype(bf16)
        o_ref[:, 2 * DIL_W:] = x_ref[:, 2 * DIL_W:].astype(bf16)

    return pl.pallas_call(
        body, grid=(S // TM,), in_specs=[_row(QKV_W)] + _table_specs(g), out_specs=_row(QKV_W),
        out_shape=SDS((S, QKV_W), bf16), name=f"rope_fwd_{g}",
        compiler_params=_params(("parallel",), 6 * TM * QKV_W + 12 * TM * 128, 24 * TM * QKV_W))(p_qkv, tables, tables, tables)


def _rope_bwd(g, dq, dk, dv, tables):
    def body(dq_ref, dk_ref, dv_ref, c_ref, lo_ref, hi_ref, o_ref):
        c, lo, hi = c_ref[...], lo_ref[...], hi_ref[...]
        o_ref[:, :DIL_W] = _rotate(dq_ref[...], c, lo, hi, -1.0).astype(bf16)
        o_ref[:, DIL_W:2 * DIL_W] = _rotate(dk_ref[...], c, lo, hi, -1.0).astype(bf16)
        o_ref[:, 2 * DIL_W:] = dv_ref[...].astype(bf16)

    return pl.pallas_call(
        body, grid=(S // TM,), in_specs=[_row(DIL_W)] * 3 + _table_specs(g), out_specs=_row(QKV_W),
        out_shape=SDS((S, QKV_W), bf16), name=f"rope_bwd_{g}",
        compiler_params=_params(("parallel",), 6 * TM * QKV_W + 12 * TM * 128, 24 * TM * QKV_W))(dq, dk, dv, tables, tables, tables)


def _dil_masks(n_is_first):
    qi = lax.broadcasted_iota(jnp.int32, (BLK, BLK), 0)
    kj = lax.broadcasted_iota(jnp.int32, (BLK, BLK), 1)
    cur = kj <= qi
    prev = kj >= qi + jnp.where(n_is_first, BLK + 1, 0)
    return cur, prev


def _nt(a, b):
    return lax.dot_general(a, b, (((1,), (1,)), ((), ())), preferred_element_type=f32)


def _tn(a, b):
    return lax.dot_general(a, b, (((0,), (0,)), ((), ())), preferred_element_type=f32)


def _dil_specs(g):
    _, d = DIL_GROUPS[g]
    nb = S // d // BLK
    blk = (BLK, DIL_W)
    own = lambda col: pl.BlockSpec(blk, lambda r, n: (r * nb + n, col))
    prev = lambda col: pl.BlockSpec(blk, lambda r, n: (r * nb + jnp.maximum(n - 1, 0), col))
    nxt = lambda col: pl.BlockSpec(blk, lambda r, n: (r * nb + jnp.minimum(n + 1, nb - 1), col))
    return (d, nb), own, prev, nxt


def _dil_fwd(g, qkv):
    grid, own, prev, _ = _dil_specs(g)

    def body(q_ref, kc_ref, kp_ref, vc_ref, vp_ref, o_ref, lse_ref):
        cur, prv = _dil_masks(pl.program_id(1) == 0)
        for h in range(SLOTS):
            hs = slice(h * HD, (h + 1) * HD)
            q = q_ref[:, hs]
            sc = jnp.where(cur, _nt(q, kc_ref[:, hs]) * SCALE, NEG)
            sp = jnp.where(prv, _nt(q, kp_ref[:, hs]) * SCALE, NEG)
            m = jnp.maximum(jnp.max(sc, axis=-1, keepdims=True), jnp.max(sp, axis=-1, keepdims=True))
            pc, pp = jnp.exp(sc - m), jnp.exp(sp - m)
            den = jnp.sum(pc, axis=-1, keepdims=True) + jnp.sum(pp, axis=-1, keepdims=True)
            inv = 1.0 / den
            o = jnp.dot((pc * inv).astype(bf16), vc_ref[:, hs], preferred_element_type=f32)
            o += jnp.dot((pp * inv).astype(bf16), vp_ref[:, hs], preferred_element_type=f32)
            o_ref[:, hs] = o
            lse_ref[:, hs] = jnp.broadcast_to(m + jnp.log(den), (BLK, HD))

    return pl.pallas_call(
        body, grid=grid, in_specs=[own(0), own(1), prev(1), own(2), prev(2)], out_specs=[own(0), own(0)],
        out_shape=[SDS((S, DIL_W), f32)] * 2, name=f"dil_fwd_{g}",
        compiler_params=_params(("parallel", "parallel"), 18 * BLK * DIL_W, 1 << 20))(qkv, qkv, qkv, qkv, qkv)


def _dil_combine(outs, lses):
    def body(o0, o1, o2, l0, l1, l2, out_ref, lse_ref, so1, so2, sl1, sl2):
        for (_, d), src, dst in ((DIL_GROUPS[1], o1, so1), (DIL_GROUPS[2], o2, so2),
                                 (DIL_GROUPS[1], l1, sl1), (DIL_GROUPS[2], l2, sl2)):
            rows = S // d
            for r in range(d):
                dst[pl.ds(r, rows, stride=d), :] = src[r * rows:(r + 1) * rows, :]
        a, b, c = l0[...], sl1[...], sl2[...]
        m = jnp.maximum(jnp.maximum(a, b), c)
        ea, eb, ec = jnp.exp(a - m), jnp.exp(b - m), jnp.exp(c - m)
        z = ea + eb + ec
        inv = 1.0 / z
        out_ref[...] = (ea * inv) * o0[...] + (eb * inv) * so1[...] + (ec * inv) * so2[...]
        lse_ref[...] = m + jnp.log(z)

    blk = pl.BlockSpec((S, 128), lambda c: (0, c))
    return pl.pallas_call(
        body, grid=(DIL_W // 128,), in_specs=[blk] * 6, out_specs=[blk] * 2,
        out_shape=[SDS((S, DIL_W), f32)] * 2, scratch_shapes=[pltpu.VMEM((S, 128), f32)] * 4, name="dil_combine",
        compiler_params=_params(("parallel",), 32 * S * 128, 32 * S * 128))(*outs, *lses)


def _dil_probs(q, k, mask, lse, do, v, delta):
    s = jnp.where(mask, _nt(q, k) * SCALE, NEG)
    p = jnp.exp(s - lse)
    ds = p * (_nt(do, v) - delta) * SCALE
    return p, ds


def _dil_bwd_q(g, qkv, d_out, delta, lse):
    grid, own, prev, _ = _dil_specs(g)

    def body(q_ref, kc_ref, kp_ref, vc_ref, vp_ref, do_ref, dl_ref, lse_ref, dq_ref):
        cur, prv = _dil_masks(pl.program_id(1) == 0)
        for h in range(SLOTS):
            hs = slice(h * HD, (h + 1) * HD)
            q, do = q_ref[:, hs], do_ref[:, hs].astype(bf16)
            lse, delta = lse_ref[:, h * HD:h * HD + 1], dl_ref[:, h * HD:h * HD + 1]
            _, dsc = _dil_probs(q, kc_ref[:, hs], cur, lse, do, vc_ref[:, hs], delta)
            _, dsp = _dil_probs(q, kp_ref[:, hs], prv, lse, do, vp_ref[:, hs], delta)
            dq = jnp.dot(dsc.astype(bf16), kc_ref[:, hs], preferred_element_type=f32)
            dq += jnp.dot(dsp.astype(bf16), kp_ref[:, hs], preferred_element_type=f32)
            dq_ref[:, hs] = dq

    return pl.pallas_call(
        body, grid=grid, in_specs=[own(0), own(1), prev(1), own(2), prev(2), own(0), own(0), own(0)],
        out_specs=own(0), out_shape=SDS((S, DIL_W), f32), name=f"dil_bwd_q_{g}",
        compiler_params=_params(("parallel", "parallel"), 26 * BLK * DIL_W, 1 << 20),
    )(qkv, qkv, qkv, qkv, qkv, d_out, delta, lse)


def _dil_bwd_kv(g, qkv, d_out, delta, lse):
    grid, own, _, nxt = _dil_specs(g)
    nb = grid[1]

    def body(k_ref, v_ref, qc_ref, qn_ref, doc_ref, don_ref, dc_ref, dn_ref, lc_ref, ln_ref, dk_ref, dv_ref):
        is_last = pl.program_id(1) == nb - 1
        qi = lax.broadcasted_iota(jnp.int32, (BLK, BLK), 0)
        kj = lax.broadcasted_iota(jnp.int32, (BLK, BLK), 1)
        cur = kj <= qi
        nxt = kj >= qi + jnp.where(is_last, BLK + 1, 0)
        for h in range(SLOTS):
            hs = slice(h * HD, (h + 1) * HD)
            k, v = k_ref[:, hs], v_ref[:, hs]
            dk = jnp.zeros((BLK, HD), f32)
            dv = jnp.zeros((BLK, HD), f32)
            for q_ref, do_ref, d_ref, l_ref, mask in ((qc_ref, doc_ref, dc_ref, lc_ref, cur),
                                                      (qn_ref, don_ref, dn_ref, ln_ref, nxt)):
                q, do = q_ref[:, hs], do_ref[:, hs].astype(bf16)
                p, ds = _dil_probs(q, k, mask, l_ref[:, h * HD:h * HD + 1], do, v, d_ref[:, h * HD:h * HD + 1])
                dv += _tn(p.astype(bf16), do)
                dk += _tn(ds.astype(bf16), q)
            dk_ref[:, hs] = dk
            dv_ref[:, hs] = dv

    in_specs = [own(1), own(2), own(0), nxt(0), own(0), nxt(0), own(0), nxt(0), own(0), nxt(0)]
    return pl.pallas_call(
        body, grid=grid, in_specs=in_specs, out_specs=[own(0), own(0)], out_shape=[SDS((S, DIL_W), f32)] * 2,
        name=f"dil_bwd_kv_{g}", compiler_params=_params(("parallel", "parallel"), 40 * BLK * DIL_W, 1 << 20),
    )(qkv, qkv, qkv, qkv, d_out, d_out, delta, delta, lse, lse)


def _scan_rows(x, reverse):
    row = lax.broadcasted_iota(jnp.int32, x.shape, 0)
    k = 1
    while k < S:
        if reverse:
            x = x + jnp.where(row < S - k, pltpu.roll(x, S - k, 0), 0.0)
        else:
            x = x + jnp.where(row >= k, pltpu.roll(x, k, 0), 0.0)
        k *= 2
    return x


N_PAIR = N_FOX // 2
_PAIR_Q = pl.BlockSpec((None, S, 128), lambda p: (p, 0, 0))
_PAIR_K = pl.BlockSpec((None, 8, S), lambda p: (p, 0, 0))


def _forget_fwd(fz, b128):
    def body(z_ref, b_ref, fq_ref, fk_ref):
        z = z_ref[...] + b_ref[...]
        logf = jnp.minimum(z, 0.0) - jnp.log1p(jnp.exp(-jnp.abs(z)))
        f_cum = _scan_rows(logf, reverse=False)
        f_cum_t = f_cum.T
        fq_ref[...] = jnp.zeros_like(fq_ref)
        fk_ref[...] = jnp.zeros_like(fk_ref)
        for p in range(N_PAIR):
            fq_ref[p, :, 0:2] = f_cum[:, 2 * p:2 * p + 2]
            fk_ref[p, 0:2, :] = f_cum_t[2 * p:2 * p + 2, :]

    return pl.pallas_call(
        body, grid=(1,), in_specs=[pl.BlockSpec((S, 128), lambda i: (0, 0)), _vec(128)],
        out_specs=[pl.BlockSpec((N_PAIR, S, 128), lambda i: (0, 0, 0)), pl.BlockSpec((N_PAIR, 8, S), lambda i: (0, 0, 0))],
        out_shape=[SDS((N_PAIR, S, 128), f32), SDS((N_PAIR, 8, S), f32)], name="forget_fwd",
        compiler_params=_params(("arbitrary",), 24 * S * 128, 24 * S * 128))(fz, b128)


def _forget_bwd(fz, b128, d_f_cols, d_f_rows):
    def body(z_ref, b_ref, dfc_ref, dfr_ref, dz_ref, db_ref, df_sc):
        z = z_ref[...] + b_ref[...]
        df_sc[...] = jnp.zeros_like(df_sc)
        for p in range(N_PAIR):
            df_sc[:, 2 * p:2 * p + 2] = dfr_ref[p, :, 0:2] + dfc_ref[p].T[:, 0:2]
        dz = _scan_rows(df_sc[...], reverse=True) * jax.nn.sigmoid(-z)
        dz_ref[...] = dz
        db_ref[...] = jnp.sum(dz, axis=0, keepdims=True)

    full = pl.BlockSpec((S, 128), lambda i: (0, 0))
    return pl.pallas_call(
        body, grid=(1,),
        in_specs=[full, _vec(128), pl.BlockSpec((N_PAIR, 8, S), lambda i: (0, 0, 0)), pl.BlockSpec((N_PAIR, S, 128), lambda i: (0, 0, 0))],
        out_specs=[full, _vec(128)], out_shape=[SDS((S, 128), f32), SDS((1, 128), f32)],
        scratch_shapes=[pltpu.VMEM((S, 128), f32)], name="forget_bwd",
        compiler_params=_params(("arbitrary",), 32 * S * 128, 24 * S * 128))(fz, b128, d_f_cols, d_f_rows)


def _fox_scores(q_ref, k_ref, fq_ref, fk_ref, qi, hh):
    n = (qi + 1) * TQ
    rows, hs = slice(qi * TQ, n), slice(hh * HD, (hh + 1) * HD)
    s = _nt(q_ref[rows, hs], k_ref[0:n, hs]) * SCALE + (fq_ref[rows, hh:hh + 1] - fk_ref[hh:hh + 1, 0:n])
    qpos = qi * TQ + lax.broadcasted_iota(jnp.int32, (TQ, n), 0)
    kpos = lax.broadcasted_iota(jnp.int32, (TQ, n), 1)
    return jnp.where(kpos <= qpos, s, NEG)


def _pair_cols(first):
    return pl.BlockSpec((S, 128), lambda p: (0, first + p))


def _fox_fwd(vr, fq, fk):
    def body(q_ref, k_ref, v_ref, fq_ref, fk_ref, o_ref, lse_ref):
        lse_ref[...] = jnp.zeros_like(lse_ref)
        for hh in range(2):
            hs = slice(hh * HD, (hh + 1) * HD)
            for qi in range(S // TQ):
                n = (qi + 1) * TQ
                rows = slice(qi * TQ, n)
                s = _fox_scores(q_ref, k_ref, fq_ref, fk_ref, qi, hh)
                m = jnp.max(s, axis=-1, keepdims=True)
                p = jnp.exp(s - m)
                den = jnp.sum(p, axis=-1, keepdims=True)
                o_ref[rows, hs] = jnp.dot((p * (1.0 / den)).astype(bf16), v_ref[0:n, hs], preferred_element_type=f32)
                lse_ref[rows, hh:hh + 1] = m + jnp.log(den)

    return pl.pallas_call(
        body, grid=(N_PAIR,), in_specs=[_pair_cols(0), _pair_cols(N_PAIR), _pair_cols(2 * N_PAIR), _PAIR_Q, _PAIR_K],
        out_specs=[_pair_cols(0), _PAIR_Q], out_shape=[SDS((S, FOX_W), f32), SDS((N_PAIR, S, 128), f32)],
        name="fox_fwd", compiler_params=_params(("parallel",), 12 * S * 128, 16 * TQ * S),
    )(vr, vr, vr, fq, fk)


def _fox_bwd(vr, fq, fk, lse, d_out, delta):
    def body(q_ref, k_ref, v_ref, do_ref, fq_ref, fk_ref, lse_ref, dl_ref, dq_ref, dk_ref, dv_ref, dfc_ref, dfr_ref,
             dk_sc, dv_sc):
        dfc_ref[...] = jnp.zeros_like(dfc_ref)
        dfr_ref[...] = jnp.zeros_like(dfr_ref)
        for hh in range(2):
            hs = slice(hh * HD, (hh + 1) * HD)
            dk_sc[...] = jnp.zeros_like(dk_sc)
            dv_sc[...] = jnp.zeros_like(dv_sc)
            for qi in range(S // TQ):
                n = (qi + 1) * TQ
                rows = slice(qi * TQ, n)
                q, do, k, v = q_ref[rows, hs], do_ref[rows, hs], k_ref[0:n, hs], v_ref[0:n, hs]
                p = jnp.exp(_fox_scores(q_ref, k_ref, fq_ref, fk_ref, qi, hh) - lse_ref[rows, hh:hh + 1])
                ds = p * (_nt(do, v) - dl_ref[rows, hh:hh + 1])
                dsb = ds.astype(bf16)
                dq_ref[rows, hs] = jnp.dot(dsb, k, preferred_element_type=f32) * SCALE
                dk_sc[0:n, :] += _tn(dsb, q) * SCALE
                dv_sc[0:n, :] += _tn(p.astype(bf16), do)
                dfc_ref[hh:hh + 1, 0:n] -= jnp.sum(ds, axis=0, keepdims=True)
                dfr_ref[rows, hh:hh + 1] = jnp.sum(ds, axis=-1, keepdims=True)
            dk_ref[:, hs] = dk_sc[...]
            dv_ref[:, hs] = dv_sc[...]

    cols = [_pair_cols(k * N_PAIR) for k in range(3)]
    return pl.pallas_call(
        body, grid=(N_PAIR,), in_specs=cols + [_pair_cols(0), _PAIR_Q, _PAIR_K, _PAIR_Q, _PAIR_Q],
        out_specs=[_pair_cols(0)] * 3 + [_PAIR_K, _PAIR_Q],
        out_shape=[SDS((S, FOX_W), f32)] * 3 + [SDS((N_PAIR, 8, S), f32), SDS((N_PAIR, S, 128), f32)],
        scratch_shapes=[pltpu.VMEM((S, HD), f32)] * 2, name="fox_bwd",
        compiler_params=_params(("parallel",), 32 * S * 128, 24 * TQ * S),
    )(vr, vr, vr, d_out, fq, fk, lse, delta)


def _merge_fwd(out_a, out_b, w_a, w_b, gf):
    cw = D // N_SHARD

    def body(oa_ref, ob_ref, wa_ref, wb_ref, ga_ref, gb_ref, ya_ref, yb_ref, mg_ref):
        oa, ob = oa_ref[...].astype(bf16), ob_ref[...].astype(bf16)
        for j in range(N_SHARD):
            cols = slice(j * cw, (j + 1) * cw)
            ya = jnp.dot(oa, wa_ref[j], preferred_element_type=f32)
            yb = jnp.dot(ob, wb_ref[j], preferred_element_type=f32)
            ya_ref[:, cols] = ya
            yb_ref[:, cols] = yb
            mg_ref[:, cols] = (jax.nn.sigmoid(ga_ref[:, cols]) * ya + jax.nn.sigmoid(gb_ref[:, cols]) * yb).astype(bf16)

    full = lambda a: pl.BlockSpec(a.shape, lambda i: (0, 0, 0))
    return pl.pallas_call(
        body, grid=(S // TM,),
        in_specs=[_row(DIL_W), _row(FOX_W), full(w_a), full(w_b), _row(D), pl.BlockSpec((TM, D), lambda i: (i, 1))],
        out_specs=[_row(D)] * 3, out_shape=[SDS((S, D), f32), SDS((S, D), f32), SDS((S, D), bf16)], name="merge_fwd",
        compiler_params=_params(("parallel",), 22 * TM * D + 2 * (DIL_W + FOX_W) * D, 16 * TM * D),
    )(out_a, out_b, w_a, w_b, gf, gf)


def _merge_bwd(d_merged, ya, yb, gf):
    def body(dm_ref, ya_ref, yb_ref, ga_ref, gb_ref, dya_ref, dyb_ref, dg_ref):
        dm = dm_ref[...]
        sa, sb = jax.nn.sigmoid(ga_ref[...]), jax.nn.sigmoid(gb_ref[...])
        dya_ref[...] = (dm * sa).astype(bf16)
        dyb_ref[...] = (dm * sb).astype(bf16)
        dg_ref[:, :D] = (dm * ya_ref[...] * sa * (1.0 - sa)).astype(bf16)
        dg_ref[:, D:] = (dm * yb_ref[...] * sb * (1.0 - sb)).astype(bf16)

    return pl.pallas_call(
        body, grid=(S // TM,),
        in_specs=[_row(D)] * 4 + [pl.BlockSpec((TM, D), lambda i: (i, 1))],
        out_specs=[_row(D), _row(D), _row(2 * D)],
        out_shape=[SDS((S, D), bf16), SDS((S, D), bf16), SDS((S, 2 * D), bf16)], name="merge_bwd",
        compiler_params=_params(("parallel",), 28 * TM * D, 24 * TM * D))(d_merged, ya, yb, gf, gf)


def _branch_bwd(d_ya, d_yb, w_a, w_b, out_a, out_b):
    cw = D // N_SHARD

    def body(dya_ref, dyb_ref, wa_ref, wb_ref, oa_ref, ob_ref, doa_ref, dla_ref, dob_ref, dlb_ref):
        doa = jnp.zeros((TM, DIL_W), f32)
        dob = jnp.zeros((TM, FOX_W), f32)
        for j in range(N_SHARD):
            cols = slice(j * cw, (j + 1) * cw)
            doa += _nt(dya_ref[:, cols], wa_ref[j])
            dob += _nt(dyb_ref[:, cols], wb_ref[j])
        doa_ref[...] = doa
        dob_ref[...] = dob.astype(bf16)
        prod_a = doa * oa_ref[...]
        for h in range(SLOTS):
            hs = slice(h * HD, (h + 1) * HD)
            dla_ref[:, hs] = jnp.broadcast_to(jnp.sum(prod_a[:, hs], axis=-1, keepdims=True), (TM, HD))
        prod_b = dob * ob_ref[...]
        dlb_ref[...] = jnp.zeros_like(dlb_ref)
        for h in range(N_FOX):
            dlb_ref[h // 2, :, h % 2:h % 2 + 1] = jnp.sum(prod_b[:, h * HD:(h + 1) * HD], axis=-1, keepdims=True)

    full = lambda a: pl.BlockSpec(a.shape, lambda i: (0, 0, 0))
    return pl.pallas_call(
        body, grid=(S // TM,),
        in_specs=[_row(D), _row(D), full(w_a), full(w_b), _row(DIL_W), _row(FOX_W)],
        out_specs=[_row(DIL_W), _row(DIL_W), _row(FOX_W), pl.BlockSpec((N_PAIR, TM, 128), lambda i: (0, i, 0))],
        out_shape=[SDS((S, DIL_W), f32), SDS((S, DIL_W), f32), SDS((S, FOX_W), bf16), SDS((N_PAIR, S, 128), f32)],
        name="branch_bwd", compiler_params=_params(("parallel",), 8 * TM * D + 2 * (DIL_W + FOX_W) * D, 8 * TM * D),
    )(d_ya, d_yb, w_a, w_b, out_a, out_b)


def _branch_grads(out_a, out_b, d_ya, d_yb):
    cw = D // N_SHARD

    def body(oa_ref, ob_ref, dya_ref, dyb_ref, ga_ref, gb_ref):
        ga_ref[...] = _tn(oa_ref[...].astype(bf16), dya_ref[...])
        gb_ref[...] = _tn(ob_ref[...].astype(bf16), dyb_ref[...])

    whole = lambda w: pl.BlockSpec((S, w), lambda j: (0, 0))
    cols = pl.BlockSpec((S, cw), lambda j: (0, j))
    return pl.pallas_call(
        body, grid=(N_SHARD,), in_specs=[whole(DIL_W), whole(FOX_W), cols, cols],
        out_specs=[pl.BlockSpec((None, DIL_W, cw), lambda j: (j, 0, 0)), pl.BlockSpec((None, FOX_W, cw), lambda j: (j, 0, 0))],
        out_shape=[SDS((N_SHARD, DIL_W, cw), f32), SDS((N_SHARD, FOX_W, cw), f32)], name="grad_w_proj_ab",
        compiler_params=_params(("parallel",), 4 * S * (DIL_W + FOX_W) + 4 * S * cw + 4 * (DIL_W + FOX_W) * cw,
                                4 * S * (DIL_W + FOX_W)))(out_a, out_b, d_ya, d_yb)


FF_TN = F_FF // 2
FF_TM = 512


def _ffn_fwd(h, w_gate_t, w_up_t):
    def body(h_ref, wg_ref, wu_ref, g_ref, u_ref, a_ref):
        hb = h_ref[...]
        g = _nt(hb, wg_ref[...])
        u = _nt(hb, wu_ref[...])
        g_ref[...] = g
        u_ref[...] = u
        a_ref[...] = (g * jax.nn.sigmoid(g) * u).astype(bf16)

    tile = pl.BlockSpec((FF_TM, FF_TN), lambda j, i: (i, j))
    wspec = pl.BlockSpec((FF_TN, D), lambda j, i: (j, 0))
    return pl.pallas_call(
        body, grid=(F_FF // FF_TN, S // FF_TM),
        in_specs=[pl.BlockSpec((FF_TM, D), lambda j, i: (i, 0)), wspec, wspec], out_specs=[tile] * 3,
        out_shape=[SDS((S, F_FF), f32), SDS((S, F_FF), f32), SDS((S, F_FF), bf16)], name="ffn_fwd",
        compiler_params=_params(("parallel", "parallel"), 2 * FF_TM * D + 4 * D * FF_TN + 10 * FF_TM * FF_TN, 16 * FF_TM * FF_TN),
    )(h, w_gate_t, w_up_t)


def _ffn_bwd_act(d_ff, w_down, g_act, u_act):
    def body(d_ref, wd_ref, g_ref, u_ref, dg_ref, du_ref):
        da = _nt(d_ref[...], wd_ref[...])
        g = g_ref[...]
        sg = jax.nn.sigmoid(g)
        du_ref[...] = (da * g * sg).astype(bf16)
        dg_ref[...] = (da * u_ref[...] * sg * (1.0 + g * (1.0 - sg))).astype(bf16)

    tile = pl.BlockSpec((FF_TM, FF_TN), lambda j, i: (i, j))
    return pl.pallas_call(
        body, grid=(F_FF // FF_TN, S // FF_TM),
        in_specs=[pl.BlockSpec((FF_TM, D), lambda j, i: (i, 0)), pl.BlockSpec((FF_TN, D), lambda j, i: (j, 0)), tile, tile],
        out_specs=[tile, tile], out_shape=[SDS((S, F_FF), bf16)] * 2, name="ffn_bwd_act",
        compiler_params=_params(("parallel", "parallel"), 2 * FF_TM * D + 2 * D * FF_TN + 12 * FF_TM * FF_TN, 16 * FF_TM * FF_TN),
    )(d_ff, w_down, g_act, u_act)


def _row_tile(rows):
    return next(t for t in (376, 128, 176, 64, 32, 16, 8) if rows % t == 0)


def _adamw_math(w, g, m, v):
    c1 = 1.0 - ADAM_B1 ** ADAM_STEP
    c2 = 1.0 - ADAM_B2 ** ADAM_STEP
    m_new = ADAM_B1 * m + (1.0 - ADAM_B1) * g
    v_new = ADAM_B2 * v + (1.0 - ADAM_B2) * (g * g)
    return -ADAM_LR * ((m_new / c1) / (jnp.sqrt(v_new / c2) + ADAM_EPS) + ADAM_WD * w), m_new, v_new


def _adamw(w, g, m, v, name):
    rows, cols = w.shape
    tm = _row_tile(rows)

    def body(w_ref, g_ref, m_ref, v_ref, d_ref, nm_ref, nv_ref):
        d_ref[...], nm_ref[...], nv_ref[...] = _adamw_math(w_ref[...], g_ref[...], m_ref[...], v_ref[...])

    spec = pl.BlockSpec((tm, cols), lambda i: (i, 0))
    return pl.pallas_call(
        body, grid=(rows // tm,), in_specs=[spec] * 4, out_specs=[spec] * 3, out_shape=[SDS(w.shape, f32)] * 3,
        name=name, compiler_params=_params(("parallel",), 28 * tm * cols, 16 * tm * cols))(w, g, m, v)


def _adamw_halves(w, g_mine, g_theirs, m, v, name):
    rows, cols = w.shape
    tm = _row_tile(rows // 2)
    per_half = rows // 2 // tm
    core = lax.axis_index("c").astype(jnp.int32).reshape(1)

    def body(c_ref, w_ref, gm_ref, gt_ref, m_ref, v_ref, g_ref, d_ref, nm_ref, nv_ref):
        mine = pl.program_id(0) // per_half == c_ref[0]
        g = jnp.where(mine, gm_ref[...], gt_ref[...])
        g_ref[...] = g
        d_ref[...], nm_ref[...], nv_ref[...] = _adamw_math(w_ref[...], g, m_ref[...], v_ref[...])

    spec = pl.BlockSpec((tm, cols), lambda i, c_ref: (i, 0))
    in_half = lambda i, first: jnp.clip(i - first * per_half, 0, per_half - 1)
    grid_spec = pltpu.PrefetchScalarGridSpec(
        num_scalar_prefetch=1, grid=(rows // tm,),
        in_specs=[spec, pl.BlockSpec((tm, cols), lambda i, c_ref: (in_half(i, c_ref[0]), 0)),
                  pl.BlockSpec((tm, cols), lambda i, c_ref: (in_half(i, 1 - c_ref[0]), 0)), spec, spec],
        out_specs=[spec] * 4)
    return pl.pallas_call(
        body, grid_spec=grid_spec, out_shape=[SDS(w.shape, f32)] * 4, name=name,
        compiler_params=_params(("parallel",), 36 * tm * cols, 16 * tm * cols))(core, w, g_mine, g_theirs, m, v)


_ANY = pl.BlockSpec(memory_space=pl.ANY)


def _place():
    x, y, c = lax.axis_index("x"), lax.axis_index("y"), lax.axis_index("c")
    chips = [(1 - x, y), (x, 1 - y), (1 - x, 1 - y)]
    return x, y, c, chips


def _halved(t):
    return t.reshape(t.shape[:-2] + (2, t.shape[-2] // 2, t.shape[-1]))


def _all_gather_weights(shards):
    n = len(shards)

    def body(*refs):
        src, out = refs[:n], refs[n:2 * n]
        send_ici, recv_ici, send_d2d, recv_d2d = refs[2 * n:]
        x, y, c, chips = _place()
        sibling = (x, y, 1 - c)
        me_j = 2 * x + y
        sends = []
        for a in range(n):
            for p in range(3):
                cp = pltpu.make_async_remote_copy(
                    src_ref=src[a].at[c], dst_ref=out[a].at[me_j, c], send_sem=send_ici.at[a, p],
                    recv_sem=recv_ici.at[a, p], device_id=(*chips[p], c), device_id_type=MESH)
                cp.start()
                sends.append(cp)
        for a in range(n):
            for p, (px, py) in enumerate(chips):
                blk = out[a].at[2 * px + py, c]
                pltpu.make_async_remote_copy(
                    src_ref=blk, dst_ref=blk, send_sem=send_ici.at[a, p], recv_sem=recv_ici.at[a, p],
                    device_id=sibling, device_id_type=MESH).wait_recv()
                fw = pltpu.make_async_remote_copy(
                    src_ref=blk, dst_ref=blk, send_sem=send_d2d.at[a, p], recv_sem=recv_d2d.at[a, p],
                    device_id=sibling, device_id_type=MESH)
                fw.start()
                sends.append(fw)
        for a in range(n):
            for p, (px, py) in enumerate(chips):
                blk = out[a].at[2 * px + py, 1 - c]
                pltpu.make_async_remote_copy(
                    src_ref=blk, dst_ref=blk, send_sem=send_d2d.at[a, p], recv_sem=recv_d2d.at[a, p],
                    device_id=sibling, device_id_type=MESH).wait_recv()
        for cp in sends:
            cp.wait_send()

    return pl.pallas_call(
        body, in_specs=[_ANY] * n, out_specs=[_ANY] * n,
        out_shape=[SDS((N_SHARD,) + t.shape, t.dtype) for t in shards],
        scratch_shapes=[pltpu.SemaphoreType.DMA((n, 3))] * 4,
        name="all_gather_weights", compiler_params=pltpu.CompilerParams(has_side_effects=True))(*shards)


def _pair_swap(grads):
    n = len(grads)

    def body(*refs):
        src, out, send_sems, recv_sems = refs[:n], refs[n:2 * n], refs[2 * n], refs[2 * n + 1]
        x, y, c, _ = _place()
        copies = [pltpu.make_async_remote_copy(
            src_ref=src[a].at[:, 1 - c], dst_ref=out[a], send_sem=send_sems.at[a], recv_sem=recv_sems.at[a],
            device_id=(x, y, 1 - c), device_id_type=MESH) for a in range(n)]
        for cp in copies:
            cp.start()
        for cp in copies:
            cp.wait()

    return pl.pallas_call(
        body, in_specs=[_ANY] * n, out_specs=[_ANY] * n,
        out_shape=[SDS((N_SHARD,) + t.shape[2:], t.dtype) for t in grads],
        scratch_shapes=[pltpu.SemaphoreType.DMA((n,)), pltpu.SemaphoreType.DMA((n,))], name="pair_swap",
        compiler_params=pltpu.CompilerParams(has_side_effects=True))(*grads)


def _pair_sum(grads, other, name):
    _, _, rows, cols = grads.shape
    tr = _row_tile(rows)
    core = lax.axis_index("c").astype(jnp.int32).reshape(1)

    def body(c_ref, g_ref, o_ref, out_ref):
        out_ref[...] = (g_ref[...] + o_ref[...]).astype(bf16)

    grid_spec = pltpu.PrefetchScalarGridSpec(
        num_scalar_prefetch=1, grid=(N_SHARD, rows // tr),
        in_specs=[pl.BlockSpec((None, None, tr, cols), lambda j, i, c_ref: (j, c_ref[0], i, 0)),
                  pl.BlockSpec((None, tr, cols), lambda j, i, c_ref: (j, i, 0))],
        out_specs=pl.BlockSpec((None, tr, cols), lambda j, i, c_ref: (j, i, 0)))
    return pl.pallas_call(
        body, grid_spec=grid_spec, out_shape=SDS((N_SHARD, rows, cols), bf16), name=name,
        compiler_params=_params(("parallel", "parallel"), 10 * tr * cols, 12 * tr * cols))(core, grads, other)


def _scatter_partials(parts, small):
    n = len(parts)

    def body(*refs):
        part, small_ref, recv, small_all_ref = refs[:n], refs[n], refs[n + 1:2 * n + 1], refs[2 * n + 1]
        send_sems, recv_sems, ssend, srecv, local_sem = refs[2 * n + 2:]
        x, y, c, chips = _place()
        me_j = 2 * x + y
        me_dev = 4 * x + 2 * y + c
        own = pltpu.make_async_copy(small_ref, small_all_ref.at[me_dev], local_sem)
        own.start()
        sends = []
        for a in range(n):
            for p, (px, py) in enumerate(chips):
                cp = pltpu.make_async_remote_copy(
                    src_ref=part[a].at[2 * px + py], dst_ref=recv[a].at[me_j], send_sem=send_sems.at[a, p],
                    recv_sem=recv_sems.at[a, p], device_id=(px, py, c), device_id_type=MESH)
                cp.start()
                sends.append(cp)
        flip = lambda a, bit: 1 - a if bit else a
        peers = [(flip(x, k & 4), flip(y, k & 2), flip(c, k & 1)) for k in range(1, 8)]
        for k, to in enumerate(peers):
            cp = pltpu.make_async_remote_copy(
                src_ref=small_ref, dst_ref=small_all_ref.at[me_dev],
                send_sem=ssend.at[k], recv_sem=srecv.at[k], device_id=to, device_id_type=MESH)
            cp.start()
            sends.append(cp)
        for a in range(n):
            for p, (px, py) in enumerate(chips):
                slot = recv[a].at[2 * px + py]
                pltpu.make_async_remote_copy(
                    src_ref=slot, dst_ref=slot, send_sem=send_sems.at[a, p], recv_sem=recv_sems.at[a, p],
                    device_id=(px, py, c), device_id_type=MESH).wait_recv()
        for k, (px, py, pc) in enumerate(peers):
            slot = small_all_ref.at[4 * px + 2 * py + pc]
            pltpu.make_async_remote_copy(
                src_ref=slot, dst_ref=slot, send_sem=ssend.at[k], recv_sem=srecv.at[k],
                device_id=(px, py, pc), device_id_type=MESH).wait_recv()
        for cp in sends:
            cp.wait_send()
        own.wait()

    return pl.pallas_call(
        body, in_specs=[_ANY] * (n + 1), out_specs=[_ANY] * (n + 1),
        out_shape=[SDS(t.shape, t.dtype) for t in parts] + [SDS((8, SMALL_ROWS, D), f32)],
        scratch_shapes=[pltpu.SemaphoreType.DMA((n, 3)), pltpu.SemaphoreType.DMA((n, 3)),
                        pltpu.SemaphoreType.DMA((7,)), pltpu.SemaphoreType.DMA((7,)), pltpu.SemaphoreType.DMA],
        name="scatter_partials", compiler_params=pltpu.CompilerParams(has_side_effects=True))(*parts, small)


def _sum_partials(part, recv, name):
    _, rows, cols = recv.shape
    tr = _row_tile(rows)
    me = (2 * lax.axis_index("x") + lax.axis_index("y")).astype(jnp.int32).reshape(1)

    def body(me_ref, mine, r0, r1, r2, r3, out_ref):
        acc = None
        for j, r in enumerate((r0, r1, r2, r3)):
            term = jnp.where(me_ref[0] == j, mine[...], r[...]).astype(f32)
            acc = term if acc is None else acc + term
        out_ref[...] = acc

    slot = lambda j: pl.BlockSpec((None, tr, cols), lambda i, me_ref: (jnp.where(me_ref[0] == j, j ^ 1, j), i, 0))
    grid_spec = pltpu.PrefetchScalarGridSpec(
        num_scalar_prefetch=1, grid=(rows // tr,),
        in_specs=[pl.BlockSpec((None, tr, cols), lambda i, me_ref: (me_ref[0], i, 0)), slot(0), slot(1), slot(2), slot(3)],
        out_specs=pl.BlockSpec((tr, cols), lambda i, me_ref: (i, 0)))
    return pl.pallas_call(
        body, grid_spec=grid_spec, out_shape=SDS((rows, cols), f32), name=name,
        compiler_params=_params(("parallel",), 14 * tr * cols, 12 * tr * cols))(me, part, recv, recv, recv, recv)


def _sum_small(small_all):
    def body(small_ref, out_ref):
        tot = small_ref[0]
        for k in range(1, 8):
            tot = tot + small_ref[k]
        out_ref[...] = tot

    return pl.pallas_call(
        body, grid=(1,), in_specs=[pl.BlockSpec((8, SMALL_ROWS, D), lambda i: (0, 0, 0))],
        out_specs=pl.BlockSpec((SMALL_ROWS, D), lambda i: (0, 0)), out_shape=SDS((SMALL_ROWS, D), f32),
        name="sum_small", compiler_params=_params(("arbitrary",), 36 * SMALL_ROWS * D))(small_all)


def _swap_halves(halves):
    n = len(halves)

    def body(*refs):
        src, out, send_sems, recv_sems = refs[:n], refs[n:2 * n], refs[2 * n], refs[2 * n + 1]
        x, y, c, _ = _place()
        copies = [pltpu.make_async_remote_copy(
            src_ref=src[a], dst_ref=out[a], send_sem=send_sems.at[a], recv_sem=recv_sems.at[a],
            device_id=(x, y, 1 - c), device_id_type=MESH) for a in range(n)]
        for cp in copies:
            cp.start()
        for cp in copies:
            cp.wait()

    return pl.pallas_call(
        body, in_specs=[_ANY] * n, out_specs=[_ANY] * n, out_shape=[SDS(t.shape, f32) for t in halves],
        scratch_shapes=[pltpu.SemaphoreType.DMA((n,))] * 2, name="swap_halves",
        compiler_params=pltpu.CompilerParams(has_side_effects=True))(*halves)


def _kernel_layout(name, t):
    t = t[0]
    if name in TRANSPOSED:
        t = jnp.swapaxes(t, 0, 1)
    return _pad_rows(t, SHARD_SHAPE[name][0])


def _harness_layout(name, t):
    if name == "w_in":
        t = t[:IN_SHARD]
    if name in TRANSPOSED:
        t = jnp.swapaxes(t, 0, 1)
    return t[None]


def _pad_rows(t, rows):
    return t if t.shape[0] == rows else jnp.pad(t, ((0, rows - t.shape[0]), (0, 0)))


_QA, _KA, _VA, _QB, _F, _GAB = 0, 768, 1536, 2304, 3840, 3848


def _full_weights(gathered):
    full = {n: t.reshape((N_SHARD,) + SHARD_SHAPE[n]) for n, t in gathered.items()}
    w_in_t = full["w_in"][:, :IN_SHARD].reshape(IN_COLS, D)
    group = lambda g: jnp.concatenate([w_in_t[o + g * DIL_W:o + (g + 1) * DIL_W] for o in (_QA, _KA, _VA)], axis=0)
    return dict(
        w_a_t=[group(g) for g in range(3)],
        w_vr_t=w_in_t[_QB:_F],
        w_fox_t=[w_in_t[_QB + k * FOX_W:_QB + (k + 1) * FOX_W] for k in range(3)],
        w_f_t=jnp.concatenate([w_in_t[_F:_GAB], jnp.zeros((128 - N_FOX, D), bf16)], axis=0),
        w_gab_t=w_in_t[_GAB:],
        w_a4=full["w_proj_a"],
        w_b4=full["w_proj_b"],
        w_out=full["w_out"].reshape(D, D),
        w_gate_t=full["w_ffn_gate"].reshape(F_FF, D),
        w_up_t=full["w_ffn_up"].reshape(F_FF, D),
        w_down=full["w_ffn_down"].reshape(F_FF, D),
    )


def _sharded_grads(g):
    parts = [g["w_a_t"][k][o:o + DIL_W] for o in (0, DIL_W, 2 * DIL_W) for k in range(3)]
    parts += g["w_fox_t"] + [g["w_f_t"][:N_FOX], g["w_gab_t"]]
    w_in_t = jnp.concatenate(parts, axis=0).reshape(N_SHARD, IN_SHARD, D)
    full = dict(w_in=jnp.pad(w_in_t, ((0, 0), (0, IN_SHARD_PAD - IN_SHARD), (0, 0))), w_proj_a=g["w_a4"],
                w_proj_b=g["w_b4"], w_out=g["w_out"], w_ffn_gate=g["w_gate_t"], w_ffn_up=g["w_up_t"],
                w_ffn_down=g["w_down"])
    return {n: _halved(full[n].reshape((N_SHARD,) + SHARD_SHAPE[n])) for n in W_NAMES}


def _local_step(x, target, wt, b_forget, g_mix_pre, g_mix_post, g_ffn_pre, g_ffn_post):
    tables = _rope_tables()
    b128 = jnp.pad(b_forget, ((0, 0), (0, 128 - N_FOX)))
    dils = tuple(d for _, d in DIL_GROUPS[1:])

    hs = _norm_fwd([x] + list(_perm_rows([x], dils, "perm_x")), g_mix_pre)
    h1 = hs[0]
    qkv = [_rope_fwd(g, _mm([(hs[g], wt["w_a_t"][g])], "nt", f32, tm=1024, tn=QKV_W, name=f"proj_a_{g}"), tables)
           for g in range(3)]
    vr = _mm([(h1, wt["w_vr_t"])], "nt", bf16, tm=1024, tn=VR_W // 2, name="proj_vr")
    gab = _mm([(h1, wt["w_gab_t"])], "nt", f32, tm=512, tn=2 * D, name="proj_gab")
    fz = _mm([(h1, wt["w_f_t"])], "nt", f32, tm=1024, tn=128, name="proj_f")
    dil = [_dil_fwd(g, qkv[g]) for g in range(3)]
    out_a, lse_a = _dil_combine([o for o, _ in dil], [l for _, l in dil])
    f_q, f_k = _forget_fwd(fz, b128)
    out_b, lse_b = _fox_fwd(vr, f_q, f_k)
    ya, yb, merged = _merge_fwd(out_a, out_b, wt["w_a4"], wt["w_b4"], gab)
    mix = _mm([(merged, wt["w_out"])], "nn", f32, tm=1024, tn=D, name="proj_out")
    x2, h3 = _resid_norm_fwd(x, mix, g_mix_post, g_ffn_pre)
    g_act, u_act, a_act = _ffn_fwd(h3, wt["w_gate_t"], wt["w_up_t"])
    ff = _mm([(a_act, wt["w_down"])], "nn", f32, tm=1024, tn=D, name="ffn_down")
    sq_err, dy, d_ff, dg_ffn_post = _loss_head(x2, ff, g_ffn_post, target)

    grads = {}
    d_g, d_u = _ffn_bwd_act(d_ff, wt["w_down"], g_act, u_act)
    grads["w_down"] = _mm([(a_act, d_ff)], "tn", f32, tm=FF_TN, tn=512, name="grad_w_down")
    grads["w_gate_t"] = _mm([(d_g, h3)], "tn", f32, tm=FF_TN, tn=512, name="grad_w_gate")
    grads["w_up_t"] = _mm([(d_u, h3)], "tn", f32, tm=FF_TN, tn=512, name="grad_w_up")
    d_h3 = _mm([(d_g, wt["w_gate_t"]), (d_u, wt["w_up_t"])], "nn", f32, tm=512, tn=512, name="ffn_bwd_in")
    dx2, d_mix, dg_ffn_pre, dg_mix_post = _norm_bwd_mid(dy, d_h3, x2, mix, g_ffn_pre, g_mix_post)

    grads["w_out"] = _mm([(merged, d_mix)], "tn", f32, tm=D, tn=D, name="grad_w_out")
    d_merged = _mm([(d_mix, wt["w_out"])], "nt", f32, tm=1024, tn=D, name="proj_out_bwd")
    d_ya, d_yb, d_gab = _merge_bwd(d_merged, ya, yb, gab)
    grads["w_a4"], grads["w_b4"] = _branch_grads(out_a, out_b, d_ya, d_yb)
    d_out_a, delta_a, d_out_b, delta_b = _branch_bwd(d_ya, d_yb, wt["w_a4"], wt["w_b4"], out_a, out_b)

    perm = _perm_rows([d_out_a, delta_a, lse_a], dils, "perm_dil_bwd")
    aux = [(d_out_a, delta_a, lse_a)] + [tuple(perm[k * len(dils) + i] for k in range(3)) for i in range(len(dils))]
    d_qkv = []
    for g in range(3):
        dq = _dil_bwd_q(g, qkv[g], *aux[g])
        dk, dv = _dil_bwd_kv(g, qkv[g], *aux[g])
        d_qkv.append(_rope_bwd(g, dq, dk, dv, tables))
    *d_fox, d_f_cols, d_f_rows = _fox_bwd(vr, f_q, f_k, lse_b, d_out_b, delta_b)
    d_z, d_b128 = _forget_bwd(fz, b128, d_f_cols, d_f_rows)

    grads["w_a_t"] = [_mm([(d_qkv[g], hs[g])], "tn", f32, tm=QKV_W, tn=D, name=f"grad_w_a_{g}") for g in range(3)]
    grads["w_fox_t"] = [_mm([(d_fox[k], h1)], "tn", f32, tm=FOX_W, tn=D, name=f"grad_w_fox_{k}") for k in range(3)]
    grads["w_gab_t"] = _mm([(d_gab, h1)], "tn", f32, tm=D, tn=D, name="grad_w_gab")
    grads["w_f_t"] = _mm([(d_z, h1)], "tn", f32, tm=128, tn=D, name="grad_w_f")
    d_h1_nat = _mm([(d_qkv[0], wt["w_a_t"][0])] + list(zip(d_fox, wt["w_fox_t"]))
                   + [(d_gab, wt["w_gab_t"]), (d_z, wt["w_f_t"])], "nn", f32, tm=512, tn=512, name="proj_in_bwd")
    d_h1_dil = [_mm([(d_qkv[g], wt["w_a_t"][g])], "nn", f32, tm=1024, tn=D, name=f"proj_a_bwd_{g}") for g in (1, 2)]
    d_h1 = _unperm_sum(d_h1_nat, d_h1_dil, dils, "unperm_d_h1")
    grad_x, dg_mix_pre = _norm_bwd_in(dx2, d_h1, x, g_mix_pre)

    small = dict(b_forget=d_b128[:, :N_FOX], norm_mix_pre=dg_mix_pre, norm_mix_post=dg_mix_post,
                 norm_ffn_pre=dg_ffn_pre, norm_ffn_post=dg_ffn_post)
    return sq_err, grad_x, grads, small


NORMS = ("norm_mix_pre", "norm_mix_post", "norm_ffn_pre", "norm_ffn_post")
ORDER = ("w_in", "w_proj_a", "w_proj_b", "w_out", "b_forget", "w_ffn_gate", "w_ffn_up", "w_ffn_down") + NORMS


def kernel(x, w_in, w_proj_a, w_proj_b, w_out, b_forget, w_ffn_gate, w_ffn_up, w_ffn_down, norm_mix_pre, norm_mix_post, norm_ffn_pre, norm_ffn_post, loss_target, m_w_in, m_w_proj_a, m_w_proj_b, m_w_out, m_b_forget, m_w_ffn_gate, m_w_ffn_up, m_w_ffn_down, m_norm_mix_pre, m_norm_mix_post, m_norm_ffn_pre, m_norm_ffn_post, v_w_in, v_w_proj_a, v_w_proj_b, v_w_out, v_b_forget, v_w_ffn_gate, v_w_ffn_up, v_w_ffn_down, v_norm_mix_pre, v_norm_mix_post, v_norm_ffn_pre, v_norm_ffn_post):
    given = dict(w_in=w_in, w_proj_a=w_proj_a, w_proj_b=w_proj_b, w_out=w_out, w_ffn_gate=w_ffn_gate,
                 w_ffn_up=w_ffn_up, w_ffn_down=w_ffn_down)
    given_m = dict(w_in=m_w_in, w_proj_a=m_w_proj_a, w_proj_b=m_w_proj_b, w_out=m_w_out, w_ffn_gate=m_w_ffn_gate,
                   w_ffn_up=m_w_ffn_up, w_ffn_down=m_w_ffn_down)
    given_v = dict(w_in=v_w_in, w_proj_a=v_w_proj_a, w_proj_b=v_w_proj_b, w_out=v_w_out, w_ffn_gate=v_w_ffn_gate,
                   w_ffn_up=v_w_ffn_up, w_ffn_down=v_w_ffn_down)
    w, m, v = ({n: _kernel_layout(n, t[n]) for n in W_NAMES} for t in (given, given_m, given_v))
    small_w = dict(b_forget=b_forget, norm_mix_pre=norm_mix_pre, norm_mix_post=norm_mix_post,
                   norm_ffn_pre=norm_ffn_pre, norm_ffn_post=norm_ffn_post)
    small_m = dict(b_forget=m_b_forget, norm_mix_pre=m_norm_mix_pre, norm_mix_post=m_norm_mix_post,
                   norm_ffn_pre=m_norm_ffn_pre, norm_ffn_post=m_norm_ffn_post)
    small_v = dict(b_forget=v_b_forget, norm_mix_pre=v_norm_mix_pre, norm_mix_post=v_norm_mix_post,
                   norm_ffn_pre=v_norm_ffn_pre, norm_ffn_post=v_norm_ffn_post)

    own = [_halved(w[n].astype(bf16)) for n in W_NAMES]
    chip = 2 * lax.axis_index("x") + lax.axis_index("y")
    gathered = [lax.dynamic_update_index_in_dim(t, o, chip, 0) for t, o in zip(_all_gather_weights(own), own)]
    wt = _full_weights(dict(zip(W_NAMES, gathered)))

    sq_err, grad_x, grads, small = _local_step(x[0], loss_target[0], wt, b_forget, norm_mix_pre, norm_mix_post,
                                               norm_ffn_pre, norm_ffn_post)
    loss = lax.psum(sq_err[0, 0] * (0.5 / D), ("x", "y", "c"))

    g4 = _sharded_grads(grads)
    stack = lambda t, fill: jnp.concatenate(
        [jnp.pad(t["b_forget"], ((0, 0), (0, D - N_FOX)))] + [t[n] for n in NORMS]
        + [jnp.full((SMALL_ROWS - 1 - len(NORMS), D), fill, f32)], axis=0)
    other = _pair_swap([g4[n] for n in W_NAMES])
    parts = [_pair_sum(g4[n], o, "pair_sum_" + n) for n, o in zip(W_NAMES, other)]
    *recv, small_all = _scatter_partials(parts, stack(small, 0.0))
    halves = [_sum_partials(p, r, "sum_partials_" + n) for n, p, r in zip(W_NAMES, parts, recv)]
    theirs = _swap_halves(halves)
    small_sum = _sum_small(small_all)

    g_shard, delta, new_m, new_v = {}, {}, {}, {}
    for n, mine, other_half in zip(W_NAMES, halves, theirs):
        g_shard[n], delta[n], new_m[n], new_v[n] = _adamw_halves(w[n], mine, other_half, m[n], v[n], "adamw_" + n)
    sd, sm, sv = _adamw(stack(small_w, 1.0), small_sum, stack(small_m, 1.0), stack(small_v, 1.0), "adamw_small")

    outs = [loss, grad_x[None]]
    for big, st in ((g_shard, small_sum), (delta, sd), (new_m, sm), (new_v, sv)):
        t = {n: _harness_layout(n, big[n]) for n in W_NAMES}
        t["b_forget"] = st[0:1, :N_FOX]
        for i, n in enumerate(NORMS):
            t[n] = st[i + 1:i + 2]
        outs += [t[n] for n in ORDER]
    return tuple(outs)
```

```python
import functools
import math

import jax
import jax.numpy as jnp
from jax import lax
from jax.experimental import pallas as pl
from jax.experimental.pallas import tpu as pltpu

f32 = jnp.float32
bf16 = jnp.bfloat16
SDS = jax.ShapeDtypeStruct
MESH = pl.DeviceIdType.MESH

S = 2048
D = 1024
HD = 64
BLK = 128
N_FOX = 8
FOX_W = N_FOX * HD
DIL_GROUPS = ((128, 1), (512, 4), (2048, 16))
SLOTS = 4
DIL_W = SLOTS * HD
QKV_W = 3 * DIL_W
VR_W = 3 * FOX_W
GF_W = 2 * D + 128
F_FF = 2816
ROPE_DIM = 16
ROPE_THETA = 500000.0
EPS = 1e-6
NEG = -1e30
SCALE = 1.0 / math.sqrt(HD)
IN_COLS = 5896
N_SHARD = 4

ADAM_LR, ADAM_B1, ADAM_B2, ADAM_EPS, ADAM_WD, ADAM_STEP = 0.001, 0.9, 0.999, 1e-08, 0.01, 10

VMEM_V7X = 64 * 1024 * 1024
VMEM_PLAN_MAX = VMEM_V7X - 2 * 1024 * 1024

TM = 256
TQ = 256

W_NAMES = ("w_in", "w_proj_a", "w_proj_b", "w_out", "w_ffn_gate", "w_ffn_up", "w_ffn_down")
TRANSPOSED = ("w_in", "w_ffn_gate", "w_ffn_up")
IN_SHARD = IN_COLS // N_SHARD
IN_SHARD_PAD = 1504
SHARD_SHAPE = dict(w_in=(IN_SHARD_PAD, D), w_proj_a=(DIL_W, D // N_SHARD), w_proj_b=(FOX_W, D // N_SHARD),
                   w_out=(D // N_SHARD, D), w_ffn_gate=(F_FF // N_SHARD, D), w_ffn_up=(F_FF // N_SHARD, D),
                   w_ffn_down=(F_FF // N_SHARD, D))
SMALL_ROWS = 8
LOSS_ROW = 5


def _nbytes(shape, dtype):
    return math.prod(shape) * jnp.dtype(dtype).itemsize


def _params(semantics, block_bytes, temp_bytes=0):
    need = 2 * block_bytes + temp_bytes + (2 << 20)
    assert need <= VMEM_PLAN_MAX, need
    return pltpu.CompilerParams(dimension_semantics=semantics, vmem_limit_bytes=VMEM_PLAN_MAX)


def _row(w, tm=TM):
    return pl.BlockSpec((tm, w), lambda i: (i, 0))


def _vec(w):
    return pl.BlockSpec((1, w), lambda i: (0, 0))


def _mm(pairs, dims, out_dtype, *, tm, tn, name, m_inner=False):
    a0, b0 = pairs[0]
    m_dim = a0.shape[1] if dims == "tn" else a0.shape[0]
    n_dim = b0.shape[0] if dims == "nt" else b0.shape[1]
    contract = {"nn": ((1,), (0,)), "nt": ((1,), (1,)), "tn": ((0,), (0,))}[dims]
    n_pairs = len(pairs)
    assert m_dim % tm == 0 and n_dim % tn == 0, (name, m_dim, n_dim, tm, tn)

    def body(*refs):
        o_ref = refs[-1]
        acc = None
        for p in range(n_pairs):
            a = refs[2 * p][...].astype(bf16)
            b = refs[2 * p + 1][...].astype(bf16)
            t = lax.dot_general(a, b, (contract, ((), ())), preferred_element_type=f32)
            acc = t if acc is None else acc + t
        o_ref[...] = acc.astype(o_ref.dtype)

    if m_inner:
        grid = (n_dim // tn, m_dim // tm)
        mi = lambda j, i: i
        ni = lambda j, i: j
    else:
        grid = (m_dim // tm, n_dim // tn)
        mi = lambda i, j: i
        ni = lambda i, j: j
    in_specs, block_bytes, args = [], 0, []
    for a, b in pairs:
        k_dim = a.shape[0] if dims == "tn" else a.shape[1]
        if dims == "tn":
            in_specs.append(pl.BlockSpec((k_dim, tm), lambda *g: (0, mi(*g))))
        else:
            in_specs.append(pl.BlockSpec((tm, k_dim), lambda *g: (mi(*g), 0)))
        if dims == "nt":
            in_specs.append(pl.BlockSpec((tn, k_dim), lambda *g: (ni(*g), 0)))
        else:
            in_specs.append(pl.BlockSpec((k_dim, tn), lambda *g: (0, ni(*g))))
        block_bytes += _nbytes((tm, k_dim), a.dtype) + _nbytes((tn, k_dim), b.dtype)
        args += [a, b]
    block_bytes += _nbytes((tm, tn), out_dtype)
    temp = _nbytes((tm, tn), f32) * 2 + sum(_nbytes((tm, a.shape[0] if dims == "tn" else a.shape[1]), bf16)
                                            + _nbytes((tn, a.shape[0] if dims == "tn" else a.shape[1]), bf16)
                                            for a, _ in pairs)
    return pl.pallas_call(
        body, grid=grid, in_specs=in_specs,
        out_specs=pl.BlockSpec((tm, tn), lambda *g: (mi(*g), ni(*g))),
        out_shape=SDS((m_dim, n_dim), out_dtype), name=name,
        compiler_params=_params(("parallel", "parallel"), block_bytes, temp),
    )(*args)


def _rms(x, g):
    r = lax.rsqrt(jnp.mean(x * x, axis=-1, keepdims=True) + EPS)
    return x * r * g


def _rms_bwd(x, g, dy):
    r = lax.rsqrt(jnp.mean(x * x, axis=-1, keepdims=True) + EPS)
    xh = x * r
    dxh = dy * g
    dx = r * (dxh - xh * jnp.mean(dxh * xh, axis=-1, keepdims=True))
    return dx, jnp.sum(dy * xh, axis=0, keepdims=True)


def _acc_rows(ref, val):
    @pl.when(pl.program_id(0) == 0)
    def _():
        ref[...] = jnp.zeros_like(ref)
    ref[...] += val


def _norm_fwd(xs, g):
    n = len(xs)

    def body(*refs):
        g = refs[n][...]
        for x_ref, h_ref in zip(refs[:n], refs[n + 1:]):
            h_ref[...] = _rms(x_ref[...], g).astype(bf16)

    return pl.pallas_call(
        body, grid=(S // TM,), in_specs=[_row(D)] * n + [_vec(D)], out_specs=[_row(D)] * n,
        out_shape=[SDS((S, D), bf16)] * n, name="norm_mix_pre",
        compiler_params=_params(("parallel",), 6 * n * TM * D, 8 * n * TM * D))(*xs, g)


def _perm_rows(xs, ds, name):
    n = len(xs)

    def body(*refs):
        outs = iter(refs[n:])
        for x_ref in refs[:n]:
            for d in ds:
                o_ref, rows = next(outs), S // d
                for r in range(d):
                    o_ref[r * rows:(r + 1) * rows, :] = x_ref[pl.ds(r, rows, stride=d), :]

    blk = pl.BlockSpec((S, 128), lambda c: (0, c))
    w = xs[0].shape[1]
    return pl.pallas_call(
        body, grid=(w // 128,), in_specs=[blk] * n, out_specs=[blk] * (n * len(ds)),
        out_shape=[SDS((S, w), f32)] * (n * len(ds)), name=name,
        compiler_params=_params(("parallel",), 4 * S * 128 * n * (1 + len(ds))))(*xs)


def _unperm_sum(nat, perms, ds, name):
    n = len(perms)

    def body(*refs):
        a_ref, o_ref, sc = refs[0], refs[n + 1], refs[n + 2]
        acc = a_ref[...]
        for b_ref, d in zip(refs[1:n + 1], ds):
            rows = S // d
            for r in range(d):
                sc[pl.ds(r, rows, stride=d), :] = b_ref[r * rows:(r + 1) * rows, :]
            acc = acc + sc[...]
        o_ref[...] = acc

    blk = pl.BlockSpec((S, 128), lambda c: (0, c))
    w = nat.shape[1]
    return pl.pallas_call(
        body, grid=(w // 128,), in_specs=[blk] * (n + 1), out_specs=blk, out_shape=SDS((S, w), f32),
        scratch_shapes=[pltpu.VMEM((S, 128), f32)], name=name,
        compiler_params=_params(("parallel",), 4 * S * 128 * (n + 2), 8 * S * 128))(nat, *perms)


def _resid_norm_fwd(x, mix, g_post, g_pre):
    def body(x_ref, mix_ref, gp_ref, gn_ref, x2_ref, h_ref):
        x2 = x_ref[...] + _rms(mix_ref[...], gp_ref[...])
        x2_ref[...] = x2
        h_ref[...] = _rms(x2, gn_ref[...]).astype(bf16)

    return pl.pallas_call(
        body, grid=(S // TM,), in_specs=[_row(D), _row(D), _vec(D), _vec(D)], out_specs=[_row(D), _row(D)],
        out_shape=[SDS((S, D), f32), SDS((S, D), bf16)], name="resid_norm_mid",
        compiler_params=_params(("parallel",), 14 * TM * D, 16 * TM * D))(x, mix, g_post, g_pre)


def _loss_head(x2, ff, g_post, target):
    def body(x2_ref, ff_ref, g_ref, t_ref, loss_ref, dy_ref, dff_ref, dg_ref):
        ff = ff_ref[...]
        g = g_ref[...]
        err = x2_ref[...] + _rms(ff, g) - t_ref[...]
        dy = err * (1.0 / D)
        dff, dg = _rms_bwd(ff, g, dy)
        dy_ref[...] = dy
        dff_ref[...] = dff.astype(bf16)
        _acc_rows(dg_ref, dg)
        _acc_rows(loss_ref, jnp.full((1, 128), jnp.sum(err * err), f32))

    return pl.pallas_call(
        body, grid=(S // TM,), in_specs=[_row(D), _row(D), _vec(D), _row(D)],
        out_specs=[_vec(128), _row(D), _row(D), _vec(D)],
        out_shape=[SDS((1, 128), f32), SDS((S, D), f32), SDS((S, D), bf16), SDS((1, D), f32)], name="loss_head",
        compiler_params=_params(("arbitrary",), 18 * TM * D, 24 * TM * D))(x2, ff, g_post, target)


def _norm_bwd_mid(dy, dh3, x2, mix, g_ffn_pre, g_mix_post):
    def body(dy_ref, dh_ref, x2_ref, mix_ref, g3_ref, g2_ref, dx2_ref, dmix_ref, dg3_ref, dg2_ref):
        d3, dg3 = _rms_bwd(x2_ref[...], g3_ref[...], dh_ref[...])
        dx2 = dy_ref[...] + d3
        dmix, dg2 = _rms_bwd(mix_ref[...], g2_ref[...], dx2)
        dx2_ref[...] = dx2
        dmix_ref[...] = dmix.astype(bf16)
        _acc_rows(dg3_ref, dg3)
        _acc_rows(dg2_ref, dg2)

    return pl.pallas_call(
        body, grid=(S // TM,), in_specs=[_row(D)] * 4 + [_vec(D)] * 2,
        out_specs=[_row(D), _row(D), _vec(D), _vec(D)],
        out_shape=[SDS((S, D), f32), SDS((S, D), bf16), SDS((1, D), f32), SDS((1, D), f32)], name="norm_bwd_mid",
        compiler_params=_params(("arbitrary",), 22 * TM * D, 24 * TM * D))(dy, dh3, x2, mix, g_ffn_pre, g_mix_post)


def _norm_bwd_in(dx2, dh1, x, g):
    def body(dx2_ref, dh_ref, x_ref, g_ref, gx_ref, dg_ref):
        d1, dg = _rms_bwd(x_ref[...], g_ref[...], dh_ref[...])
        gx_ref[...] = dx2_ref[...] + d1
        _acc_rows(dg_ref, dg)

    return pl.pallas_call(
        body, grid=(S // TM,), in_specs=[_row(D)] * 3 + [_vec(D)], out_specs=[_row(D), _vec(D)],
        out_shape=[SDS((S, D), f32), SDS((1, D), f32)], name="norm_bwd_in",
        compiler_params=_params(("arbitrary",), 16 * TM * D, 16 * TM * D))(dx2, dh1, x, g)


def _rope_tables():
    half = ROPE_DIM // 2
    inv_freq = jnp.power(ROPE_THETA, -jnp.arange(0, ROPE_DIM, 2, dtype=f32) / ROPE_DIM)
    row = jnp.arange(S, dtype=jnp.int32)
    groups = []
    for _, d in DIL_GROUPS:
        pos = ((row % (S // d)) * d + row // (S // d)).astype(f32)
        ang = pos[:, None] * inv_freq[None, :]
        cos, sin = jnp.cos(ang), jnp.sin(ang)
        c = jnp.concatenate([cos, cos, jnp.ones((S, HD - ROPE_DIM), f32)], axis=1)
        s_lo = jnp.concatenate([-sin, jnp.zeros((S, HD - half), f32)], axis=1)
        s_hi = jnp.concatenate([jnp.zeros((S, half), f32), sin, jnp.zeros((S, HD - ROPE_DIM), f32)], axis=1)
        groups.append(jnp.stack([jnp.concatenate([t, t], axis=1) for t in (c, s_lo, s_hi)]))
    return jnp.stack(groups)


def _rotate(x, c, lo, hi, sign):
    tile = lambda t: jnp.tile(t, (1, DIL_W // 128))
    return (x * tile(c) + pltpu.roll(x, DIL_W - ROPE_DIM // 2, 1) * (tile(lo) * sign)
            + pltpu.roll(x, ROPE_DIM // 2, 1) * (tile(hi) * sign))


def _table_specs(g):
    return [pl.BlockSpec((None, None, TM, 128), lambda i, k=k: (g, k, i, 0)) for k in range(3)]


def _rope_fwd(g, p_qkv, tables):
    def body(x_ref, c_ref, lo_ref, hi_ref, o_ref):
        c, lo, hi = c_ref[...], lo_ref[...], hi_ref[...]
        for part in range(2):
            cols = slice(part * DIL_W, (part + 1) * DIL_W)
            o_ref[:, cols] = _rotate(x_ref[:, cols], c, lo, hi, 1.0).astype(bf16)
        o_ref[:, 2 * DIL_W:] = x_ref[:, 2 * DIL_W:].astype(bf16)

    return pl.pallas_call(
        body, grid=(S // TM,), in_specs=[_row(QKV_W)] + _table_specs(g), out_specs=_row(QKV_W),
        out_shape=SDS((S, QKV_W), bf16), name=f"rope_fwd_{g}",
        compiler_params=_params(("parallel",), 6 * TM * QKV_W + 12 * TM * 128, 24 * TM * QKV_W))(p_qkv, tables, tables, tables)


def _rope_bwd(g, dq, dk, dv, tables):
    def body(dq_ref, dk_ref, dv_ref, c_ref, lo_ref, hi_ref, o_ref):
        c, lo, hi = c_ref[...], lo_ref[...], hi_ref[...]
        o_ref[:, :DIL_W] = _rotate(dq_ref[...], c, lo, hi, -1.0).astype(bf16)
        o_ref[:, DIL_W:2 * DIL_W] = _rotate(dk_ref[...], c, lo, hi, -1.0).astype(bf16)
        o_ref[:, 2 * DIL_W:] = dv_ref[...].astype(bf16)

    return pl.pallas_call(
        body, grid=(S // TM,), in_specs=[_row(DIL_W)] * 3 + _table_specs(g), out_specs=_row(QKV_W),
        out_shape=SDS((S, QKV_W), bf16), name=f"rope_bwd_{g}",
        compiler_params=_params(("parallel",), 6 * TM * QKV_W + 12 * TM * 128, 24 * TM * QKV_W))(dq, dk, dv, tables, tables, tables)


def _dil_masks(n_is_first):
    qi = lax.broadcasted_iota(jnp.int32, (BLK, BLK), 0)
    kj = lax.broadcasted_iota(jnp.int32, (BLK, BLK), 1)
    cur = kj <= qi
    prev = kj >= qi + jnp.where(n_is_first, BLK + 1, 0)
    return cur, prev


def _nt(a, b):
    return lax.dot_general(a, b, (((1,), (1,)), ((), ())), preferred_element_type=f32)


def _tn(a, b):
    return lax.dot_general(a, b, (((0,), (0,)), ((), ())), preferred_element_type=f32)


def _dil_specs(g):
    _, d = DIL_GROUPS[g]
    nb = S // d // BLK
    blk = (BLK, DIL_W)
    own = lambda col: pl.BlockSpec(blk, lambda r, n: (r * nb + n, col))
    prev = lambda col: pl.BlockSpec(blk, lambda r, n: (r * nb + jnp.maximum(n - 1, 0), col))
    nxt = lambda col: pl.BlockSpec(blk, lambda r, n: (r * nb + jnp.minimum(n + 1, nb - 1), col))
    return (d, nb), own, prev, nxt


def _dil_fwd(g, qkv):
    grid, own, prev, _ = _dil_specs(g)

    def body(q_ref, kc_ref, kp_ref, vc_ref, vp_ref, o_ref, lse_ref):
        cur, prv = _dil_masks(pl.program_id(1) == 0)
        for h in range(SLOTS):
            hs = slice(h * HD, (h + 1) * HD)
            q = q_ref[:, hs]
            sc = jnp.where(cur, _nt(q, kc_ref[:, hs]) * SCALE, NEG)
            sp = jnp.where(prv, _nt(q, kp_ref[:, hs]) * SCALE, NEG)
            m = jnp.maximum(jnp.max(sc, axis=-1, keepdims=True), jnp.max(sp, axis=-1, keepdims=True))
            pc, pp = jnp.exp(sc - m), jnp.exp(sp - m)
            den = jnp.sum(pc, axis=-1, keepdims=True) + jnp.sum(pp, axis=-1, keepdims=True)
            inv = 1.0 / den
            o = jnp.dot((pc * inv).astype(bf16), vc_ref[:, hs], preferred_element_type=f32)
            o += jnp.dot((pp * inv).astype(bf16), vp_ref[:, hs], preferred_element_type=f32)
            o_ref[:, hs] = o
            lse_ref[:, hs] = jnp.broadcast_to(m + jnp.log(den), (BLK, HD))

    return pl.pallas_call(
        body, grid=grid, in_specs=[own(0), own(1), prev(1), own(2), prev(2)], out_specs=[own(0), own(0)],
        out_shape=[SDS((S, DIL_W), f32)] * 2, name=f"dil_fwd_{g}",
        compiler_params=_params(("parallel", "parallel"), 18 * BLK * DIL_W, 1 << 20))(qkv, qkv, qkv, qkv, qkv)


def _dil_combine(outs, lses):
    def body(o0, o1, o2, l0, l1, l2, out_ref, lse_ref, so1, so2, sl1, sl2):
        for (_, d), src, dst in ((DIL_GROUPS[1], o1, so1), (DIL_GROUPS[2], o2, so2),
                                 (DIL_GROUPS[1], l1, sl1), (DIL_GROUPS[2], l2, sl2)):
            rows = S // d
            for r in range(d):
                dst[pl.ds(r, rows, stride=d), :] = src[r * rows:(r + 1) * rows, :]
        a, b, c = l0[...], sl1[...], sl2[...]
        m = jnp.maximum(jnp.maximum(a, b), c)
        ea, eb, ec = jnp.exp(a - m), jnp.exp(b - m), jnp.exp(c - m)
        z = ea + eb + ec
        inv = 1.0 / z
        out_ref[...] = (ea * inv) * o0[...] + (eb * inv) * so1[...] + (ec * inv) * so2[...]
        lse_ref[...] = m + jnp.log(z)

    blk = pl.BlockSpec((S, 128), lambda c: (0, c))
    return pl.pallas_call(
        body, grid=(DIL_W // 128,), in_specs=[blk] * 6, out_specs=[blk] * 2,
        out_shape=[SDS((S, DIL_W), f32)] * 2, scratch_shapes=[pltpu.VMEM((S, 128), f32)] * 4, name="dil_combine",
        compiler_params=_params(("parallel",), 32 * S * 128, 32 * S * 128))(*outs, *lses)


def _dil_probs(q, k, mask, lse, do, v, delta):
    s = jnp.where(mask, _nt(q, k) * SCALE, NEG)
    p = jnp.exp(s - lse)
    ds = p * (_nt(do, v) - delta) * SCALE
    return p, ds


def _dil_bwd_q(g, qkv, d_out, delta, lse):
    grid, own, prev, _ = _dil_specs(g)

    def body(q_ref, kc_ref, kp_ref, vc_ref, vp_ref, do_ref, dl_ref, lse_ref, dq_ref):
        cur, prv = _dil_masks(pl.program_id(1) == 0)
        for h in range(SLOTS):
            hs = slice(h * HD, (h + 1) * HD)
            q, do = q_ref[:, hs], do_ref[:, hs].astype(bf16)
            lse, delta = lse_ref[:, h * HD:h * HD + 1], dl_ref[:, h * HD:h * HD + 1]
            _, dsc = _dil_probs(q, kc_ref[:, hs], cur, lse, do, vc_ref[:, hs], delta)
            _, dsp = _dil_probs(q, kp_ref[:, hs], prv, lse, do, vp_ref[:, hs], delta)
            dq = jnp.dot(dsc.astype(bf16), kc_ref[:, hs], preferred_element_type=f32)
            dq += jnp.dot(dsp.astype(bf16), kp_ref[:, hs], preferred_element_type=f32)
            dq_ref[:, hs] = dq

    return pl.pallas_call(
        body, grid=grid, in_specs=[own(0), own(1), prev(1), own(2), prev(2), own(0), own(0), own(0)],
        out_specs=own(0), out_shape=SDS((S, DIL_W), f32), name=f"dil_bwd_q_{g}",
        compiler_params=_params(("parallel", "parallel"), 26 * BLK * DIL_W, 1 << 20),
    )(qkv, qkv, qkv, qkv, qkv, d_out, delta, lse)


def _dil_bwd_kv(g, qkv, d_out, delta, lse):
    grid, own, _, nxt = _dil_specs(g)
    nb = grid[1]

    def body(k_ref, v_ref, qc_ref, qn_ref, doc_ref, don_ref, dc_ref, dn_ref, lc_ref, ln_ref, dk_ref, dv_ref):
        is_last = pl.program_id(1) == nb - 1
        qi = lax.broadcasted_iota(jnp.int32, (BLK, BLK), 0)
        kj = lax.broadcasted_iota(jnp.int32, (BLK, BLK), 1)
        cur = kj <= qi
        nxt = kj >= qi + jnp.where(is_last, BLK + 1, 0)
        for h in range(SLOTS):
            hs = slice(h * HD, (h + 1) * HD)
            k, v = k_ref[:, hs], v_ref[:, hs]
            dk = jnp.zeros((BLK, HD), f32)
            dv = jnp.zeros((BLK, HD), f32)
            for q_ref, do_ref, d_ref, l_ref, mask in ((qc_ref, doc_ref, dc_ref, lc_ref, cur),
                                                      (qn_ref, don_ref, dn_ref, ln_ref, nxt)):
                q, do = q_ref[:, hs], do_ref[:, hs].astype(bf16)
                p, ds = _dil_probs(q, k, mask, l_ref[:, h * HD:h * HD + 1], do, v, d_ref[:, h * HD:h * HD + 1])
                dv += _tn(p.astype(bf16), do)
                dk += _tn(ds.astype(bf16), q)
            dk_ref[:, hs] = dk
            dv_ref[:, hs] = dv

    in_specs = [own(1), own(2), own(0), nxt(0), own(0), nxt(0), own(0), nxt(0), own(0), nxt(0)]
    return pl.pallas_call(
        body, grid=grid, in_specs=in_specs, out_specs=[own(0), own(0)], out_shape=[SDS((S, DIL_W), f32)] * 2,
        name=f"dil_bwd_kv_{g}", compiler_params=_params(("parallel", "parallel"), 40 * BLK * DIL_W, 1 << 20),
    )(qkv, qkv, qkv, qkv, d_out, d_out, delta, delta, lse, lse)


def _scan_rows(x, reverse):
    row = lax.broadcasted_iota(jnp.int32, x.shape, 0)
    k = 1
    while k < S:
        if reverse:
            x = x + jnp.where(row < S - k, pltpu.roll(x, S - k, 0), 0.0)
        else:
            x = x + jnp.where(row >= k, pltpu.roll(x, k, 0), 0.0)
        k *= 2
    return x


N_PAIR = N_FOX // 2
_PAIR_Q = pl.BlockSpec((None, S, 128), lambda p: (p, 0, 0))
_PAIR_K = pl.BlockSpec((None, 8, S), lambda p: (p, 0, 0))


def _forget_fwd(fz, b128):
    def body(z_ref, b_ref, fq_ref, fk_ref):
        z = z_ref[...] + b_ref[...]
        logf = jnp.minimum(z, 0.0) - jnp.log1p(jnp.exp(-jnp.abs(z)))
        f_cum = _scan_rows(logf, reverse=False)
        f_cum_t = f_cum.T
        fq_ref[...] = jnp.zeros_like(fq_ref)
        fk_ref[...] = jnp.zeros_like(fk_ref)
        for p in range(N_PAIR):
            fq_ref[p, :, 0:2] = f_cum[:, 2 * p:2 * p + 2]
            fk_ref[p, 0:2, :] = f_cum_t[2 * p:2 * p + 2, :]

    return pl.pallas_call(
        body, grid=(1,), in_specs=[pl.BlockSpec((S, 128), lambda i: (0, 0)), _vec(128)],
        out_specs=[pl.BlockSpec((N_PAIR, S, 128), lambda i: (0, 0, 0)), pl.BlockSpec((N_PAIR, 8, S), lambda i: (0, 0, 0))],
        out_shape=[SDS((N_PAIR, S, 128), f32), SDS((N_PAIR, 8, S), f32)], name="forget_fwd",
        compiler_params=_params(("arbitrary",), 24 * S * 128, 24 * S * 128))(fz, b128)


def _forget_bwd(fz, b128, d_f_cols, d_f_rows):
    def body(z_ref, b_ref, dfc_ref, dfr_ref, dz_ref, db_ref, df_sc):
        z = z_ref[...] + b_ref[...]
        df_sc[...] = jnp.zeros_like(df_sc)
        for p in range(N_PAIR):
            df_sc[:, 2 * p:2 * p + 2] = dfr_ref[p, :, 0:2] + dfc_ref[p].T[:, 0:2]
        dz = _scan_rows(df_sc[...], reverse=True) * jax.nn.sigmoid(-z)
        dz_ref[...] = dz
        db_ref[...] = jnp.sum(dz, axis=0, keepdims=True)

    full = pl.BlockSpec((S, 128), lambda i: (0, 0))
    return pl.pallas_call(
        body, grid=(1,),
        in_specs=[full, _vec(128), pl.BlockSpec((N_PAIR, 8, S), lambda i: (0, 0, 0)), pl.BlockSpec((N_PAIR, S, 128), lambda i: (0, 0, 0))],
        out_specs=[full, _vec(128)], out_shape=[SDS((S, 128), f32), SDS((1, 128), f32)],
        scratch_shapes=[pltpu.VMEM((S, 128), f32)], name="forget_bwd",
        compiler_params=_params(("arbitrary",), 32 * S * 128, 24 * S * 128))(fz, b128, d_f_cols, d_f_rows)


def _fox_scores(q_ref, k_ref, fq_ref, fk_ref, qi, hh):
    n = (qi + 1) * TQ
    rows, hs = slice(qi * TQ, n), slice(hh * HD, (hh + 1) * HD)
    s = _nt(q_ref[rows, hs], k_ref[0:n, hs]) * SCALE + (fq_ref[rows, hh:hh + 1] - fk_ref[hh:hh + 1, 0:n])
    qpos = qi * TQ + lax.broadcasted_iota(jnp.int32, (TQ, n), 0)
    kpos = lax.broadcasted_iota(jnp.int32, (TQ, n), 1)
    return jnp.where(kpos <= qpos, s, NEG)


def _pair_cols(first):
    return pl.BlockSpec((S, 128), lambda p: (0, first + p))


def _fox_fwd(vr, fq, fk):
    def body(q_ref, k_ref, v_ref, fq_ref, fk_ref, o_ref, lse_ref):
        lse_ref[...] = jnp.zeros_like(lse_ref)
        for hh in range(2):
            hs = slice(hh * HD, (hh + 1) * HD)
            for qi in range(S // TQ):
                n = (qi + 1) * TQ
                rows = slice(qi * TQ, n)
                s = _fox_scores(q_ref, k_ref, fq_ref, fk_ref, qi, hh)
                m = jnp.max(s, axis=-1, keepdims=True)
                p = jnp.exp(s - m)
                den = jnp.sum(p, axis=-1, keepdims=True)
                o_ref[rows, hs] = jnp.dot((p * (1.0 / den)).astype(bf16), v_ref[0:n, hs], preferred_element_type=f32)
                lse_ref[rows, hh:hh + 1] = m + jnp.log(den)

    return pl.pallas_call(
        body, grid=(N_PAIR,), in_specs=[_pair_cols(0), _pair_cols(N_PAIR), _pair_cols(2 * N_PAIR), _PAIR_Q, _PAIR_K],
        out_specs=[_pair_cols(0), _PAIR_Q], out_shape=[SDS((S, FOX_W), f32), SDS((N_PAIR, S, 128), f32)],
        name="fox_fwd", compiler_params=_params(("parallel",), 12 * S * 128, 16 * TQ * S),
    )(vr, vr, vr, fq, fk)


def _fox_bwd(vr, fq, fk, lse, d_out, delta):
    def body(q_ref, k_ref, v_ref, do_ref, fq_ref, fk_ref, lse_ref, dl_ref, dq_ref, dk_ref, dv_ref, dfc_ref, dfr_ref,
             dk_sc, dv_sc):
        dfc_ref[...] = jnp.zeros_like(dfc_ref)
        dfr_ref[...] = jnp.zeros_like(dfr_ref)
        for hh in range(2):
            hs = slice(hh * HD, (hh + 1) * HD)
            dk_sc[...] = jnp.zeros_like(dk_sc)
            dv_sc[...] = jnp.zeros_like(dv_sc)
            for qi in range(S // TQ):
                n = (qi + 1) * TQ
                rows = slice(qi * TQ, n)
                q, do, k, v = q_ref[rows, hs], do_ref[rows, hs], k_ref[0:n, hs], v_ref[0:n, hs]
                p = jnp.exp(_fox_scores(q_ref, k_ref, fq_ref, fk_ref, qi, hh) - lse_ref[rows, hh:hh + 1])
                ds = p * (_nt(do, v) - dl_ref[rows, hh:hh + 1])
                dsb = ds.astype(bf16)
                dq_ref[rows, hs] = jnp.dot(dsb, k, preferred_element_type=f32) * SCALE
                dk_sc[0:n, :] += _tn(dsb, q) * SCALE
                dv_sc[0:n, :] += _tn(p.astype(bf16), do)
                dfc_ref[hh:hh + 1, 0:n] -= jnp.sum(ds, axis=0, keepdims=True)
                dfr_ref[rows, hh:hh + 1] = jnp.sum(ds, axis=-1, keepdims=True)
            dk_ref[:, hs] = dk_sc[...]
            dv_ref[:, hs] = dv_sc[...]

    cols = [_pair_cols(k * N_PAIR) for k in range(3)]
    return pl.pallas_call(
        body, grid=(N_PAIR,), in_specs=cols + [_pair_cols(0), _PAIR_Q, _PAIR_K, _PAIR_Q, _PAIR_Q],
        out_specs=[_pair_cols(0)] * 3 + [_PAIR_K, _PAIR_Q],
        out_shape=[SDS((S, FOX_W), f32)] * 3 + [SDS((N_PAIR, 8, S), f32), SDS((N_PAIR, S, 128), f32)],
        scratch_shapes=[pltpu.VMEM((S, HD), f32)] * 2, name="fox_bwd",
        compiler_params=_params(("parallel",), 32 * S * 128, 24 * TQ * S),
    )(vr, vr, vr, d_out, fq, fk, lse, delta)


def _merge_fwd(out_a, out_b, w_a, w_b, gf):
    cw = D // N_SHARD

    def body(oa_ref, ob_ref, wa_ref, wb_ref, ga_ref, gb_ref, ya_ref, yb_ref, mg_ref):
        oa, ob = oa_ref[...].astype(bf16), ob_ref[...].astype(bf16)
        for j in range(N_SHARD):
            cols = slice(j * cw, (j + 1) * cw)
            ya = jnp.dot(oa, wa_ref[j], preferred_element_type=f32)
            yb = jnp.dot(ob, wb_ref[j], preferred_element_type=f32)
            ya_ref[:, cols] = ya
            yb_ref[:, cols] = yb
            mg_ref[:, cols] = (jax.nn.sigmoid(ga_ref[:, cols]) * ya + jax.nn.sigmoid(gb_ref[:, cols]) * yb).astype(bf16)

    full = lambda a: pl.BlockSpec(a.shape, lambda i: (0, 0, 0))
    return pl.pallas_call(
        body, grid=(S // TM,),
        in_specs=[_row(DIL_W), _row(FOX_W), full(w_a), full(w_b), _row(D), pl.BlockSpec((TM, D), lambda i: (i, 1))],
        out_specs=[_row(D)] * 3, out_shape=[SDS((S, D), f32), SDS((S, D), f32), SDS((S, D), bf16)], name="merge_fwd",
        compiler_params=_params(("parallel",), 22 * TM * D + 2 * (DIL_W + FOX_W) * D, 16 * TM * D),
    )(out_a, out_b, w_a, w_b, gf, gf)


def _merge_bwd(d_merged, ya, yb, gf):
    def body(dm_ref, ya_ref, yb_ref, ga_ref, gb_ref, dya_ref, dyb_ref, dg_ref):
        dm = dm_ref[...]
        sa, sb = jax.nn.sigmoid(ga_ref[...]), jax.nn.sigmoid(gb_ref[...])
        dya_ref[...] = (dm * sa).astype(bf16)
        dyb_ref[...] = (dm * sb).astype(bf16)
        dg_ref[:, :D] = (dm * ya_ref[...] * sa * (1.0 - sa)).astype(bf16)
        dg_ref[:, D:] = (dm * yb_ref[...] * sb * (1.0 - sb)).astype(bf16)

    return pl.pallas_call(
        body, grid=(S // TM,),
        in_specs=[_row(D)] * 4 + [pl.BlockSpec((TM, D), lambda i: (i, 1))],
        out_specs=[_row(D), _row(D), _row(2 * D)],
        out_shape=[SDS((S, D), bf16), SDS((S, D), bf16), SDS((S, 2 * D), bf16)], name="merge_bwd",
        compiler_params=_params(("parallel",), 28 * TM * D, 24 * TM * D))(d_merged, ya, yb, gf, gf)


def _branch_bwd(d_ya, d_yb, w_a, w_b, out_a, out_b):
    cw = D // N_SHARD

    def body(dya_ref, dyb_ref, wa_ref, wb_ref, oa_ref, ob_ref, doa_ref, dla_ref, dob_ref, dlb_ref):
        doa = jnp.zeros((TM, DIL_W), f32)
        dob = jnp.zeros((TM, FOX_W), f32)
        for j in range(N_SHARD):
            cols = slice(j * cw, (j + 1) * cw)
            doa += _nt(dya_ref[:, cols], wa_ref[j])
            dob += _nt(dyb_ref[:, cols], wb_ref[j])
        doa_ref[...] = doa
        dob_ref[...] = dob.astype(bf16)
        prod_a = doa * oa_ref[...]
        for h in range(SLOTS):
            hs = slice(h * HD, (h + 1) * HD)
            dla_ref[:, hs] = jnp.broadcast_to(jnp.sum(prod_a[:, hs], axis=-1, keepdims=True), (TM, HD))
        prod_b = dob * ob_ref[...]
        dlb_ref[...] = jnp.zeros_like(dlb_ref)
        for h in range(N_FOX):
            dlb_ref[h // 2, :, h % 2:h % 2 + 1] = jnp.sum(prod_b[:, h * HD:(h + 1) * HD], axis=-1, keepdims=True)

    full = lambda a: pl.BlockSpec(a.shape, lambda i: (0, 0, 0))
    return pl.pallas_call(
        body, grid=(S // TM,),
        in_specs=[_row(D), _row(D), full(w_a), full(w_b), _row(DIL_W), _row(FOX_W)],
        out_specs=[_row(DIL_W), _row(DIL_W), _row(FOX_W), pl.BlockSpec((N_PAIR, TM, 128), lambda i: (0, i, 0))],
        out_shape=[SDS((S, DIL_W), f32), SDS((S, DIL_W), f32), SDS((S, FOX_W), bf16), SDS((N_PAIR, S, 128), f32)],
        name="branch_bwd", compiler_params=_params(("parallel",), 8 * TM * D + 2 * (DIL_W + FOX_W) * D, 8 * TM * D),
    )(d_ya, d_yb, w_a, w_b, out_a, out_b)


def _branch_grads(out_a, out_b, d_ya, d_yb):
    cw = D // N_SHARD

    def body(oa_ref, ob_ref, dya_ref, dyb_ref, ga_ref, gb_ref):
        ga_ref[...] = _tn(oa_ref[...].astype(bf16), dya_ref[...])
        gb_ref[...] = _tn(ob_ref[...].astype(bf16), dyb_ref[...])

    whole = lambda w: pl.BlockSpec((S, w), lambda j: (0, 0))
    cols = pl.BlockSpec((S, cw), lambda j: (0, j))
    return pl.pallas_call(
        body, grid=(N_SHARD,), in_specs=[whole(DIL_W), whole(FOX_W), cols, cols],
        out_specs=[pl.BlockSpec((None, DIL_W, cw), lambda j: (j, 0, 0)), pl.BlockSpec((None, FOX_W, cw), lambda j: (j, 0, 0))],
        out_shape=[SDS((N_SHARD, DIL_W, cw), f32), SDS((N_SHARD, FOX_W, cw), f32)], name="grad_w_proj_ab",
        compiler_params=_params(("parallel",), 4 * S * (DIL_W + FOX_W) + 4 * S * cw + 4 * (DIL_W + FOX_W) * cw,
                                4 * S * (DIL_W + FOX_W)))(out_a, out_b, d_ya, d_yb)


FF_TN = F_FF // 2
FF_TM = 512


def _ffn_fwd(h, w_gate_t, w_up_t):
    def body(h_ref, wg_ref, wu_ref, g_ref, u_ref, a_ref):
        hb = h_ref[...]
        g = _nt(hb, wg_ref[...])
        u = _nt(hb, wu_ref[...])
        g_ref[...] = g
        u_ref[...] = u
        a_ref[...] = (g * jax.nn.sigmoid(g) * u).astype(bf16)

    tile = pl.BlockSpec((FF_TM, FF_TN), lambda j, i: (i, j))
    wspec = pl.BlockSpec((FF_TN, D), lambda j, i: (j, 0))
    return pl.pallas_call(
        body, grid=(F_FF // FF_TN, S // FF_TM),
        in_specs=[pl.BlockSpec((FF_TM, D), lambda j, i: (i, 0)), wspec, wspec], out_specs=[tile] * 3,
        out_shape=[SDS((S, F_FF), f32), SDS((S, F_FF), f32), SDS((S, F_FF), bf16)], name="ffn_fwd",
        compiler_params=_params(("parallel", "parallel"), 2 * FF_TM * D + 4 * D * FF_TN + 10 * FF_TM * FF_TN, 16 * FF_TM * FF_TN),
    )(h, w_gate_t, w_up_t)


def _ffn_bwd_act(d_ff, w_down, g_act, u_act):
    def body(d_ref, wd_ref, g_ref, u_ref, dg_ref, du_ref):
        da = _nt(d_ref[...], wd_ref[...])
        g = g_ref[...]
        sg = jax.nn.sigmoid(g)
        du_ref[...] = (da * g * sg).astype(bf16)
        dg_ref[...] = (da * u_ref[...] * sg * (1.0 + g * (1.0 - sg))).astype(bf16)

    tile = pl.BlockSpec((FF_TM, FF_TN), lambda j, i: (i, j))
    return pl.pallas_call(
        body, grid=(F_FF // FF_TN, S // FF_TM),
        in_specs=[pl.BlockSpec((FF_TM, D), lambda j, i: (i, 0)), pl.BlockSpec((FF_TN, D), lambda j, i: (j, 0)), tile, tile],
        out_specs=[tile, tile], out_shape=[SDS((S, F_FF), bf16)] * 2, name="ffn_bwd_act",
        compiler_params=_params(("parallel", "parallel"), 2 * FF_TM * D + 2 * D * FF_TN + 12 * FF_TM * FF_TN, 16 * FF_TM * FF_TN),
    )(d_ff, w_down, g_act, u_act)


def _row_tile(rows):
    return next(t for t in (376, 128, 176, 64, 32, 16, 8) if rows % t == 0)


def _adamw_math(w, g, m, v):
    c1 = 1.0 - ADAM_B1 ** ADAM_STEP
    c2 = 1.0 - ADAM_B2 ** ADAM_STEP
    m_new = ADAM_B1 * m + (1.0 - ADAM_B1) * g
    v_new = ADAM_B2 * v + (1.0 - ADAM_B2) * (g * g)
    return -ADAM_LR * ((m_new / c1) / (jnp.sqrt(v_new / c2) + ADAM_EPS) + ADAM_WD * w), m_new, v_new


def _adamw(w, g, m, v, name):
    rows, cols = w.shape
    tm = _row_tile(rows)

    def body(w_ref, g_ref, m_ref, v_ref, d_ref, nm_ref, nv_ref):
        d_ref[...], nm_ref[...], nv_ref[...] = _adamw_math(w_ref[...], g_ref[...], m_ref[...], v_ref[...])

    spec = pl.BlockSpec((tm, cols), lambda i: (i, 0))
    return pl.pallas_call(
        body, grid=(rows // tm,), in_specs=[spec] * 4, out_specs=[spec] * 3, out_shape=[SDS(w.shape, f32)] * 3,
        name=name, compiler_params=_params(("parallel",), 28 * tm * cols, 16 * tm * cols))(w, g, m, v)


def _adamw_halves(w, g_mine, g_theirs, m, v, name):
    rows, cols = w.shape
    tm = _row_tile(rows // 2)
    per_half = rows // 2 // tm
    core = lax.axis_index("c").astype(jnp.int32).reshape(1)

    def body(c_ref, w_ref, gm_ref, gt_ref, m_ref, v_ref, g_ref, d_ref, nm_ref, nv_ref):
        mine = pl.program_id(0) // per_half == c_ref[0]
        g = jnp.where(mine, gm_ref[...], gt_ref[...])
        g_ref[...] = g
        d_ref[...], nm_ref[...], nv_ref[...] = _adamw_math(w_ref[...], g, m_ref[...], v_ref[...])

    spec = pl.BlockSpec((tm, cols), lambda i, c_ref: (i, 0))
    in_half = lambda i, first: jnp.clip(i - first * per_half, 0, per_half - 1)
    grid_spec = pltpu.PrefetchScalarGridSpec(
        num_scalar_prefetch=1, grid=(rows // tm,),
        in_specs=[spec, pl.BlockSpec((tm, cols), lambda i, c_ref: (in_half(i, c_ref[0]), 0)),
                  pl.BlockSpec((tm, cols), lambda i, c_ref: (in_half(i, 1 - c_ref[0]), 0)), spec, spec],
        out_specs=[spec] * 4)
    return pl.pallas_call(
        body, grid_spec=grid_spec, out_shape=[SDS(w.shape, f32)] * 4, name=name,
        compiler_params=_params(("parallel",), 36 * tm * cols, 16 * tm * cols))(core, w, g_mine, g_theirs, m, v)


_ANY = pl.BlockSpec(memory_space=pl.ANY)


def _place():
    x, y, c = lax.axis_index("x"), lax.axis_index("y"), lax.axis_index("c")
    chips = [(1 - x, y), (x, 1 - y), (1 - x, 1 - y)]
    return x, y, c, chips


def _halved(t):
    return t.reshape(t.shape[:-2] + (2, t.shape[-2] // 2, t.shape[-1]))


def _all_gather_weights(shards):
    n = len(shards)

    def body(*refs):
        src, out = refs[:n], refs[n:2 * n]
        send_ici, recv_ici, send_d2d, recv_d2d = refs[2 * n:]
        x, y, c, chips = _place()
        sibling = (x, y, 1 - c)
        me_j = 2 * x + y
        sends = []
        for a in range(n):
            for p in range(3):
                cp = pltpu.make_async_remote_copy(
                    src_ref=src[a].at[c], dst_ref=out[a].at[me_j, c], send_sem=send_ici.at[a, p],
                    recv_sem=recv_ici.at[a, p], device_id=(*chips[p], c), device_id_type=MESH)
                cp.start()
                sends.append(cp)
        for a in range(n):
            for p, (px, py) in enumerate(chips):
                blk = out[a].at[2 * px + py, c]
                pltpu.make_async_remote_copy(
                    src_ref=blk, dst_ref=blk, send_sem=send_ici.at[a, p], recv_sem=recv_ici.at[a, p],
                    device_id=sibling, device_id_type=MESH).wait_recv()
                fw = pltpu.make_async_remote_copy(
                    src_ref=blk, dst_ref=blk, send_sem=send_d2d.at[a, p], recv_sem=recv_d2d.at[a, p],
                    device_id=sibling, device_id_type=MESH)
                fw.start()
                sends.append(fw)
        for a in range(n):
            for p, (px, py) in enumerate(chips):
                blk = out[a].at[2 * px + py, 1 - c]
                pltpu.make_async_remote_copy(
                    src_ref=blk, dst_ref=blk, send_sem=send_d2d.at[a, p], recv_sem=recv_d2d.at[a, p],
                    device_id=sibling, device_id_type=MESH).wait_recv()
        for cp in sends:
            cp.wait_send()

    return pl.pallas_call(
        body, in_specs=[_ANY] * n, out_specs=[_ANY] * n,
        out_shape=[SDS((N_SHARD,) + t.shape, t.dtype) for t in shards],
        scratch_shapes=[pltpu.SemaphoreType.DMA((n, 3))] * 4,
        name="all_gather_weights", compiler_params=pltpu.CompilerParams(has_side_effects=True))(*shards)


def _pair_swap(grads):
    n = len(grads)

    def body(*refs):
        src, out, send_sems, recv_sems = refs[:n], refs[n:2 * n], refs[2 * n], refs[2 * n + 1]
        x, y, c, _ = _place()
        copies = [pltpu.make_async_remote_copy(
            src_ref=src[a].at[:, 1 - c], dst_ref=out[a], send_sem=send_sems.at[a], recv_sem=recv_sems.at[a],
            device_id=(x, y, 1 - c), device_id_type=MESH) for a in range(n)]
        for cp in copies:
            cp.start()
        for cp in copies:
            cp.wait()

    return pl.pallas_call(
        body, in_specs=[_ANY] * n, out_specs=[_ANY] * n,
        out_shape=[SDS((N_SHARD,) + t.shape[2:], t.dtype) for t in grads],
        scratch_shapes=[pltpu.SemaphoreType.DMA((n,)), pltpu.SemaphoreType.DMA((n,))], name="pair_swap",
        compiler_params=pltpu.CompilerParams(has_side_effects=True))(*grads)


def _pair_sum(grads, other, name):
    _, _, rows, cols = grads.shape
    tr = _row_tile(rows)
    core = lax.axis_index("c").astype(jnp.int32).reshape(1)

    def body(c_ref, g_ref, o_ref, out_ref):
        out_ref[...] = (g_ref[...] + o_ref[...]).astype(bf16)

    grid_spec = pltpu.PrefetchScalarGridSpec(
        num_scalar_prefetch=1, grid=(N_SHARD, rows // tr),
        in_specs=[pl.BlockSpec((None, None, tr, cols), lambda j, i, c_ref: (j, c_ref[0], i, 0)),
                  pl.BlockSpec((None, tr, cols), lambda j, i, c_ref: (j, i, 0))],
        out_specs=pl.BlockSpec((None, tr, cols), lambda j, i, c_ref: (j, i, 0)))
    return pl.pallas_call(
        body, grid_spec=grid_spec, out_shape=SDS((N_SHARD, rows, cols), bf16), name=name,
        compiler_params=_params(("parallel", "parallel"), 10 * tr * cols, 12 * tr * cols))(core, grads, other)


def _scatter_partials(parts, small):
    n = len(parts)

    def body(*refs):
        part, small_ref, recv, small_all_ref = refs[:n], refs[n], refs[n + 1:2 * n + 1], refs[2 * n + 1]
        send_sems, recv_sems, ssend, srecv, local_sem = refs[2 * n + 2:]
        x, y, c, chips = _place()
        me_j = 2 * x + y
        me_dev = 4 * x + 2 * y + c
        own = pltpu.make_async_copy(small_ref, small_all_ref.at[me_dev], local_sem)
        own.start()
        sends = []
        for a in range(n):
            for p, (px, py) in enumerate(chips):
                cp = pltpu.make_async_remote_copy(
                    src_ref=part[a].at[2 * px + py], dst_ref=recv[a].at[me_j], send_sem=send_sems.at[a, p],
                    recv_sem=recv_sems.at[a, p], device_id=(px, py, c), device_id_type=MESH)
                cp.start()
                sends.append(cp)
        flip = lambda a, bit: 1 - a if bit else a
        peers = [(flip(x, k & 4), flip(y, k & 2), flip(c, k & 1)) for k in range(1, 8)]
        for k, to in enumerate(peers):
            cp = pltpu.make_async_remote_copy(
                src_ref=small_ref, dst_ref=small_all_ref.at[me_dev],
                send_sem=ssend.at[k], recv_sem=srecv.at[k], device_id=to, device_id_type=MESH)
            cp.start()
            sends.append(cp)
        for a in range(n):
            for p, (px, py) in enumerate(chips):
                slot = recv[a].at[2 * px + py]
                pltpu.make_async_remote_copy(
                    src_ref=slot, dst_ref=slot, send_sem=send_sems.at[a, p], recv_sem=recv_sems.at[a, p],
                    device_id=(px, py, c), device_id_type=MESH).wait_recv()
        for k, (px, py, pc) in enumerate(peers):
            slot = small_all_ref.at[4 * px + 2 * py + pc]
            pltpu.make_async_remote_copy(
                src_ref=slot, dst_ref=slot, send_sem=ssend.at[k], recv_sem=srecv.at[k],
                device_id=(px, py, pc), device_id_type=MESH).wait_recv()
        for cp in sends:
            cp.wait_send()
        own.wait()

    return pl.pallas_call(
        body, in_specs=[_ANY] * (n + 1), out_specs=[_ANY] * (n + 1),
        out_shape=[SDS(t.shape, t.dtype) for t in parts] + [SDS((8, SMALL_ROWS, D), f32)],
        scratch_shapes=[pltpu.SemaphoreType.DMA((n, 3)), pltpu.SemaphoreType.DMA((n, 3)),
                        pltpu.SemaphoreType.DMA((7,)), pltpu.SemaphoreType.DMA((7,)), pltpu.SemaphoreType.DMA],
        name="scatter_partials", compiler_params=pltpu.CompilerParams(has_side_effects=True))(*parts, small)


def _sum_partials(part, recv, name):
    _, rows, cols = recv.shape
    tr = _row_tile(rows)
    me = (2 * lax.axis_index("x") + lax.axis_index("y")).astype(jnp.int32).reshape(1)

    def body(me_ref, mine, r0, r1, r2, r3, out_ref):
        acc = None
        for j, r in enumerate((r0, r1, r2, r3)):
            term = jnp.where(me_ref[0] == j, mine[...], r[...]).astype(f32)
            acc = term if acc is None else acc + term
        out_ref[...] = acc

    slot = lambda j: pl.BlockSpec((None, tr, cols), lambda i, me_ref: (jnp.where(me_ref[0] == j, j ^ 1, j), i, 0))
    grid_spec = pltpu.PrefetchScalarGridSpec(
        num_scalar_prefetch=1, grid=(rows // tr,),
        in_specs=[pl.BlockSpec((None, tr, cols), lambda i, me_ref: (me_ref[0], i, 0)), slot(0), slot(1), slot(2), slot(3)],
        out_specs=pl.BlockSpec((tr, cols), lambda i, me_ref: (i, 0)))
    return pl.pallas_call(
        body, grid_spec=grid_spec, out_shape=SDS((rows, cols), f32), name=name,
        compiler_params=_params(("parallel",), 14 * tr * cols, 12 * tr * cols))(me, part, recv, recv, recv, recv)


def _sum_small(small_all):
    def body(small_ref, out_ref):
        tot = small_ref[0]
        for k in range(1, 8):
            tot = tot + small_ref[k]
        out_ref[...] = tot

    return pl.pallas_call(
        body, grid=(1,), in_specs=[pl.BlockSpec((8, SMALL_ROWS, D), lambda i: (0, 0, 0))],
        out_specs=pl.BlockSpec((SMALL_ROWS, D), lambda i: (0, 0)), out_shape=SDS((SMALL_ROWS, D), f32),
        name="sum_small", compiler_params=_params(("arbitrary",), 36 * SMALL_ROWS * D))(small_all)


def _swap_halves(halves):
    n = len(halves)

    def body(*refs):
        src, out, send_sems, recv_sems = refs[:n], refs[n:2 * n], refs[2 * n], refs[2 * n + 1]
        x, y, c, _ = _place()
        copies = [pltpu.make_async_remote_copy(
            src_ref=src[a], dst_ref=out[a], send_sem=send_sems.at[a], recv_sem=recv_sems.at[a],
            device_id=(x, y, 1 - c), device_id_type=MESH) for a in range(n)]
        for cp in copies:
            cp.start()
        for cp in copies:
            cp.wait()

    return pl.pallas_call(
        body, in_specs=[_ANY] * n, out_specs=[_ANY] * n, out_shape=[SDS(t.shape, f32) for t in halves],
        scratch_shapes=[pltpu.SemaphoreType.DMA((n,))] * 2, name="swap_halves",
        compiler_params=pltpu.CompilerParams(has_side_effects=True))(*halves)


def _kernel_layout(name, t):
    t = t[0]
    if name in TRANSPOSED:
        t = jnp.swapaxes(t, 0, 1)
    return _pad_rows(t, SHARD_SHAPE[name][0])


def _harness_layout(name, t):
    if name == "w_in":
        t = t[:IN_SHARD]
    if name in TRANSPOSED:
        t = jnp.swapaxes(t, 0, 1)
    return t[None]


def _pad_rows(t, rows):
    return t if t.shape[0] == rows else jnp.pad(t, ((0, rows - t.shape[0]), (0, 0)))


_QA, _KA, _VA, _QB, _F, _GAB = 0, 768, 1536, 2304, 3840, 3848


def _full_weights(gathered):
    full = {n: t.reshape((N_SHARD,) + SHARD_SHAPE[n]) for n, t in gathered.items()}
    w_in_t = full["w_in"][:, :IN_SHARD].reshape(IN_COLS, D)
    group = lambda g: jnp.concatenate([w_in_t[o + g * DIL_W:o + (g + 1) * DIL_W] for o in (_QA, _KA, _VA)], axis=0)
    return dict(
        w_a_t=[group(g) for g in range(3)],
        w_vr_t=w_in_t[_QB:_F],
        w_fox_t=[w_in_t[_QB + k * FOX_W:_QB + (k + 1) * FOX_W] for k in range(3)],
        w_f_t=jnp.concatenate([w_in_t[_F:_GAB], jnp.zeros((128 - N_FOX, D), bf16)], axis=0),
        w_gab_t=w_in_t[_GAB:],
        w_a4=full["w_proj_a"],
        w_b4=full["w_proj_b"],
        w_out=full["w_out"].reshape(D, D),
        w_gate_t=full["w_ffn_gate"].reshape(F_FF, D),
        w_up_t=full["w_ffn_up"].reshape(F_FF, D),
        w_down=full["w_ffn_down"].reshape(F_FF, D),
    )


def _sharded_grads(g):
    parts = [g["w_a_t"][k][o:o + DIL_W] for o in (0, DIL_W, 2 * DIL_W) for k in range(3)]
    parts += g["w_fox_t"] + [g["w_f_t"][:N_FOX], g["w_gab_t"]]
    w_in_t = jnp.concatenate(parts, axis=0).reshape(N_SHARD, IN_SHARD, D)
    full = dict(w_in=jnp.pad(w_in_t, ((0, 0), (0, IN_SHARD_PAD - IN_SHARD), (0, 0))), w_proj_a=g["w_a4"],
                w_proj_b=g["w_b4"], w_out=g["w_out"], w_ffn_gate=g["w_gate_t"], w_ffn_up=g["w_up_t"],
                w_ffn_down=g["w_down"])
    return {n: _halved(full[n].reshape((N_SHARD,) + SHARD_SHAPE[n])) for n in W_NAMES}


def _local_step(x, target, wt, b_forget, g_mix_pre, g_mix_post, g_ffn_pre, g_ffn_post):
    tables = _rope_tables()
    b128 = jnp.pad(b_forget, ((0, 0), (0, 128 - N_FOX)))
    dils = tuple(d for _, d in DIL_GROUPS[1:])

    hs = _norm_fwd([x] + list(_perm_rows([x], dils, "perm_x")), g_mix_pre)
    h1 = hs[0]
    qkv = [_rope_fwd(g, _mm([(hs[g], wt["w_a_t"][g])], "nt", f32, tm=1024, tn=QKV_W, name=f"proj_a_{g}"), tables)
           for g in range(3)]
    vr = _mm([(h1, wt["w_vr_t"])], "nt", bf16, tm=1024, tn=VR_W // 2, name="proj_vr")
    gab = _mm([(h1, wt["w_gab_t"])], "nt", f32, tm=512, tn=2 * D, name="proj_gab")
    fz = _mm([(h1, wt["w_f_t"])], "nt", f32, tm=1024, tn=128, name="proj_f")
    dil = [_dil_fwd(g, qkv[g]) for g in range(3)]
    out_a, lse_a = _dil_combine([o for o, _ in dil], [l for _, l in dil])
    f_q, f_k = _forget_fwd(fz, b128)
    out_b, lse_b = _fox_fwd(vr, f_q, f_k)
    ya, yb, merged = _merge_fwd(out_a, out_b, wt["w_a4"], wt["w_b4"], gab)
    mix = _mm([(merged, wt["w_out"])], "nn", f32, tm=1024, tn=D, name="proj_out")
    x2, h3 = _resid_norm_fwd(x, mix, g_mix_post, g_ffn_pre)
    g_act, u_act, a_act = _ffn_fwd(h3, wt["w_gate_t"], wt["w_up_t"])
    ff = _mm([(a_act, wt["w_down"])], "nn", f32, tm=1024, tn=D, name="ffn_down")
    sq_err, dy, d_ff, dg_ffn_post = _loss_head(x2, ff, g_ffn_post, target)

    grads = {}
    d_g, d_u = _ffn_bwd_act(d_ff, wt["w_down"], g_act, u_act)
    grads["w_down"] = _mm([(a_act, d_ff)], "tn", f32, tm=FF_TN, tn=512, name="grad_w_down")
    grads["w_gate_t"] = _mm([(d_g, h3)], "tn", f32, tm=FF_TN, tn=512, name="grad_w_gate")
    grads["w_up_t"] = _mm([(d_u, h3)], "tn", f32, tm=FF_TN, tn=512, name="grad_w_up")
    d_h3 = _mm([(d_g, wt["w_gate_t"]), (d_u, wt["w_up_t"])], "nn", f32, tm=512, tn=512, name="ffn_bwd_in")
    dx2, d_mix, dg_ffn_pre, dg_mix_post = _norm_bwd_mid(dy, d_h3, x2, mix, g_ffn_pre, g_mix_post)

    grads["w_out"] = _mm([(merged, d_mix)], "tn", f32, tm=D, tn=D, name="grad_w_out")
    d_merged = _mm([(d_mix, wt["w_out"])], "nt", f32, tm=1024, tn=D, name="proj_out_bwd")
    d_ya, d_yb, d_gab = _merge_bwd(d_merged, ya, yb, gab)
    grads["w_a4"], grads["w_b4"] = _branch_grads(out_a, out_b, d_ya, d_yb)
    d_out_a, delta_a, d_out_b, delta_b = _branch_bwd(d_ya, d_yb, wt["w_a4"], wt["w_b4"], out_a, out_b)

    perm = _perm_rows([d_out_a, delta_a, lse_a], dils, "perm_dil_bwd")
    aux = [(d_out_a, delta_a, lse_a)] + [tuple(perm[k * len(dils) + i] for k in range(3)) for i in range(len(dils))]
    d_qkv = []
    for g in range(3):
        dq = _dil_bwd_q(g, qkv[g], *aux[g])
        dk, dv = _dil_bwd_kv(g, qkv[g], *aux[g])
        d_qkv.append(_rope_bwd(g, dq, dk, dv, tables))
    *d_fox, d_f_cols, d_f_rows = _fox_bwd(vr, f_q, f_k, lse_b, d_out_b, delta_b)
    d_z, d_b128 = _forget_bwd(fz, b128, d_f_cols, d_f_rows)

    grads["w_a_t"] = [_mm([(d_qkv[g], hs[g])], "tn", f32, tm=QKV_W, tn=D, name=f"grad_w_a_{g}") for g in range(3)]
    grads["w_fox_t"] = [_mm([(d_fox[k], h1)], "tn", f32, tm=FOX_W, tn=D, name=f"grad_w_fox_{k}") for k in range(3)]
    grads["w_gab_t"] = _mm([(d_gab, h1)], "tn", f32, tm=D, tn=D, name="grad_w_gab")
    grads["w_f_t"] = _mm([(d_z, h1)], "tn", f32, tm=128, tn=D, name="grad_w_f")
    d_h1_nat = _mm([(d_qkv[0], wt["w_a_t"][0])] + list(zip(d_fox, wt["w_fox_t"]))
                   + [(d_gab, wt["w_gab_t"]), (d_z, wt["w_f_t"])], "nn", f32, tm=512, tn=512, name="proj_in_bwd")
    d_h1_dil = [_mm([(d_qkv[g], wt["w_a_t"][g])], "nn", f32, tm=1024, tn=D, name=f"proj_a_bwd_{g}") for g in (1, 2)]
    d_h1 = _unperm_sum(d_h1_nat, d_h1_dil, dils, "unperm_d_h1")
    grad_x, dg_mix_pre = _norm_bwd_in(dx2, d_h1, x, g_mix_pre)

    small = dict(b_forget=d_b128[:, :N_FOX], norm_mix_pre=dg_mix_pre, norm_mix_post=dg_mix_post,
                 norm_ffn_pre=dg_ffn_pre, norm_ffn_post=dg_ffn_post)
    return sq_err, grad_x, grads, small


NORMS = ("norm_mix_pre", "norm_mix_post", "norm_ffn_pre", "norm_ffn_post")
ORDER = ("w_in", "w_proj_a", "w_proj_b", "w_out", "b_forget", "w_ffn_gate", "w_ffn_up", "w_ffn_down") + NORMS


def kernel(x, w_in, w_proj_a, w_proj_b, w_out, b_forget, w_ffn_gate, w_ffn_up, w_ffn_down, norm_mix_pre, norm_mix_post, norm_ffn_pre, norm_ffn_post, loss_target, m_w_in, m_w_proj_a, m_w_proj_b, m_w_out, m_b_forget, m_w_ffn_gate, m_w_ffn_up, m_w_ffn_down, m_norm_mix_pre, m_norm_mix_post, m_norm_ffn_pre, m_norm_ffn_post, v_w_in, v_w_proj_a, v_w_proj_b, v_w_out, v_b_forget, v_w_ffn_gate, v_w_ffn_up, v_w_ffn_down, v_norm_mix_pre, v_norm_mix_post, v_norm_ffn_pre, v_norm_ffn_post):
    given = dict(w_in=w_in, w_proj_a=w_proj_a, w_proj_b=w_proj_b, w_out=w_out, w_ffn_gate=w_ffn_gate,
                 w_ffn_up=w_ffn_up, w_ffn_down=w_ffn_down)
    given_m = dict(w_in=m_w_in, w_proj_a=m_w_proj_a, w_proj_b=m_w_proj_b, w_out=m_w_out, w_ffn_gate=m_w_ffn_gate,
                   w_ffn_up=m_w_ffn_up, w_ffn_down=m_w_ffn_down)
    given_v = dict(w_in=v_w_in, w_proj_a=v_w_proj_a, w_proj_b=v_w_proj_b, w_out=v_w_out, w_ffn_gate=v_w_ffn_gate,
                   w_ffn_up=v_w_ffn_up, w_ffn_down=v_w_ffn_down)
    w, m, v = ({n: _kernel_layout(n, t[n]) for n in W_NAMES} for t in (given, given_m, given_v))
    small_w = dict(b_forget=b_forget, norm_mix_pre=norm_mix_pre, norm_mix_post=norm_mix_post,
                   norm_ffn_pre=norm_ffn_pre, norm_ffn_post=norm_ffn_post)
    small_m = dict(b_forget=m_b_forget, norm_mix_pre=m_norm_mix_pre, norm_mix_post=m_norm_mix_post,
                   norm_ffn_pre=m_norm_ffn_pre, norm_ffn_post=m_norm_ffn_post)
    small_v = dict(b_forget=v_b_forget, norm_mix_pre=v_norm_mix_pre, norm_mix_post=v_norm_mix_post,
                   norm_ffn_pre=v_norm_ffn_pre, norm_ffn_post=v_norm_ffn_post)

    own = [_halved(w[n].astype(bf16)) for n in W_NAMES]
    chip = 2 * lax.axis_index("x") + lax.axis_index("y")
    gathered = [lax.dynamic_update_index_in_dim(t, o, chip, 0) for t, o in zip(_all_gather_weights(own), own)]
    wt = _full_weights(dict(zip(W_NAMES, gathered)))

    sq_err, grad_x, grads, small = _local_step(x[0], loss_target[0], wt, b_forget, norm_mix_pre, norm_mix_post,
                                               norm_ffn_pre, norm_ffn_post)

    g4 = _sharded_grads(grads)
    stack = lambda t, extra: jnp.concatenate(
        [jnp.pad(t["b_forget"], ((0, 0), (0, D - N_FOX)))] + [t[n] for n in NORMS]
        + [jnp.pad(extra, ((0, SMALL_ROWS - LOSS_ROW - 1), (0, D - extra.shape[1])), constant_values=1.0)], axis=0)
    other = _pair_swap([g4[n] for n in W_NAMES])
    parts = [_pair_sum(g4[n], o, "pair_sum_" + n) for n, o in zip(W_NAMES, other)]
    *recv, small_all = _scatter_partials(parts, stack(small, sq_err))
    halves = [_sum_partials(p, r, "sum_partials_" + n) for n, p, r in zip(W_NAMES, parts, recv)]
    theirs = _swap_halves(halves)
    small_sum = _sum_small(small_all)
    loss = small_sum[LOSS_ROW, 0] * (0.5 / D)

    g_shard, delta, new_m, new_v = {}, {}, {}, {}
    for n, mine, other_half in zip(W_NAMES, halves, theirs):
        g_shard[n], delta[n], new_m[n], new_v[n] = _adamw_halves(w[n], mine, other_half, m[n], v[n], "adamw_" + n)
    ones = jnp.ones((1, 128), f32)
    sd, sm, sv = _adamw(stack(small_w, ones), small_sum, stack(small_m, ones), stack(small_v, ones), "adamw_small")

    outs = [loss, grad_x[None]]
    for big, st in ((g_shard, small_sum), (delta, sd), (new_m, sm), (new_v, sv)):
        t = {n: _harness_layout(n, big[n]) for n in W_NAMES}
        t["b_forget"] = st[0:1, :N_FOX]
        for i, n in enumerate(NORMS):
            t[n] = st[i + 1:i + 2]
        outs += [t[n] for n in ORDER]
    return tuple(outs)
```

```python
import functools
import math

import jax
import jax.numpy as jnp
from jax import lax
from jax.experimental import pallas as pl
from jax.experimental.pallas import tpu as pltpu
from jax.experimental.pallas import tpu_sc as plsc

f32 = jnp.float32
bf16 = jnp.bfloat16
SDS = jax.ShapeDtypeStruct
MESH = pl.DeviceIdType.MESH

S = 2048
D = 1024
HD = 64
BLK = 128
N_FOX = 8
FOX_W = N_FOX * HD
DIL_GROUPS = ((128, 1), (512, 4), (2048, 16))
SLOTS = 4
DIL_W = SLOTS * HD
QKV_W = 3 * DIL_W
VR_W = 3 * FOX_W
GF_W = 2 * D + 128
F_FF = 2816
ROPE_DIM = 16
ROPE_THETA = 500000.0
EPS = 1e-6
NEG = -1e30
SCALE = 1.0 / math.sqrt(HD)
IN_COLS = 5896
N_SHARD = 4

ADAM_LR, ADAM_B1, ADAM_B2, ADAM_EPS, ADAM_WD, ADAM_STEP = 0.001, 0.9, 0.999, 1e-08, 0.01, 10

VMEM_V7X = 64 * 1024 * 1024
VMEM_PLAN_MAX = VMEM_V7X - 8 * 1024 * 1024

TM = 256
TQ = 256

W_NAMES = ("w_in", "w_proj_a", "w_proj_b", "w_out", "w_ffn_gate", "w_ffn_up", "w_ffn_down")
TRANSPOSED = ("w_in", "w_ffn_gate", "w_ffn_up")
IN_SHARD = IN_COLS // N_SHARD
IN_SHARD_PAD = 1504
SHARD_SHAPE = dict(w_in=(IN_SHARD_PAD, D), w_proj_a=(DIL_W, D // N_SHARD), w_proj_b=(FOX_W, D // N_SHARD),
                   w_out=(D // N_SHARD, D), w_ffn_gate=(F_FF // N_SHARD, D), w_ffn_up=(F_FF // N_SHARD, D),
                   w_ffn_down=(F_FF // N_SHARD, D))
SMALL_ROWS = 8
LOSS_ROW = 5


def _nbytes(shape, dtype):
    return math.prod(shape) * jnp.dtype(dtype).itemsize


def _params(semantics, block_bytes, temp_bytes=0):
    need = 2 * block_bytes + temp_bytes + (2 << 20)
    return pltpu.CompilerParams(dimension_semantics=semantics, vmem_limit_bytes=int(min(need, VMEM_PLAN_MAX)))


def _row(w, tm=TM):
    return pl.BlockSpec((tm, w), lambda i: (i, 0))


def _vec(w):
    return pl.BlockSpec((1, w), lambda i: (0, 0))


def _mm(pairs, dims, out_dtype, *, tm, tn, name, m_inner=False):
    a0, b0 = pairs[0]
    m_dim = a0.shape[1] if dims == "tn" else a0.shape[0]
    n_dim = b0.shape[0] if dims == "nt" else b0.shape[1]
    contract = {"nn": ((1,), (0,)), "nt": ((1,), (1,)), "tn": ((0,), (0,))}[dims]
    n_pairs = len(pairs)
    assert m_dim % tm == 0 and n_dim % tn == 0, (name, m_dim, n_dim, tm, tn)

    def body(*refs):
        o_ref = refs[-1]
        acc = None
        for p in range(n_pairs):
            a = refs[2 * p][...].astype(bf16)
            b = refs[2 * p + 1][...].astype(bf16)
            t = lax.dot_general(a, b, (contract, ((), ())), preferred_element_type=f32)
            acc = t if acc is None else acc + t
        o_ref[...] = acc.astype(o_ref.dtype)

    if m_inner:
        grid = (n_dim // tn, m_dim // tm)
        mi = lambda j, i: i
        ni = lambda j, i: j
    else:
        grid = (m_dim // tm, n_dim // tn)
        mi = lambda i, j: i
        ni = lambda i, j: j
    in_specs, block_bytes, args = [], 0, []
    for a, b in pairs:
        k_dim = a.shape[0] if dims == "tn" else a.shape[1]
        if dims == "tn":
            in_specs.append(pl.BlockSpec((k_dim, tm), lambda *g: (0, mi(*g))))
        else:
            in_specs.append(pl.BlockSpec((tm, k_dim), lambda *g: (mi(*g), 0)))
        if dims == "nt":
            in_specs.append(pl.BlockSpec((tn, k_dim), lambda *g: (ni(*g), 0)))
        else:
            in_specs.append(pl.BlockSpec((k_dim, tn), lambda *g: (0, ni(*g))))
        block_bytes += _nbytes((tm, k_dim), a.dtype) + _nbytes((tn, k_dim), b.dtype)
        args += [a, b]
    block_bytes += _nbytes((tm, tn), out_dtype)
    temp = _nbytes((tm, tn), f32) * 2 + sum(_nbytes((tm, a.shape[0] if dims == "tn" else a.shape[1]), bf16)
                                            + _nbytes((tn, a.shape[0] if dims == "tn" else a.shape[1]), bf16)
                                            for a, _ in pairs)
    return pl.pallas_call(
        body, grid=grid, in_specs=in_specs,
        out_specs=pl.BlockSpec((tm, tn), lambda *g: (mi(*g), ni(*g))),
        out_shape=SDS((m_dim, n_dim), out_dtype), name=name,
        compiler_params=_params(("parallel", "parallel"), block_bytes, temp),
    )(*args)


def _rms(x, g):
    r = lax.rsqrt(jnp.mean(x * x, axis=-1, keepdims=True) + EPS)
    return x * r * g


def _rms_bwd(x, g, dy):
    r = lax.rsqrt(jnp.mean(x * x, axis=-1, keepdims=True) + EPS)
    xh = x * r
    dxh = dy * g
    dx = r * (dxh - xh * jnp.mean(dxh * xh, axis=-1, keepdims=True))
    return dx, jnp.sum(dy * xh, axis=0, keepdims=True)


def _acc_rows(ref, val):
    @pl.when(pl.program_id(0) == 0)
    def _():
        ref[...] = jnp.zeros_like(ref)
    ref[...] += val


def _norm_fwd(xs, g):
    n = len(xs)

    def body(*refs):
        g = refs[n][...]
        for x_ref, h_ref in zip(refs[:n], refs[n + 1:]):
            h_ref[...] = _rms(x_ref[...], g).astype(bf16)

    return pl.pallas_call(
        body, grid=(S // TM,), in_specs=[_row(D)] * n + [_vec(D)], out_specs=[_row(D)] * n,
        out_shape=[SDS((S, D), bf16)] * n, name="norm_mix_pre",
        compiler_params=_params(("parallel",), 6 * n * TM * D, 8 * n * TM * D))(*xs, g)


def _perm_rows(xs, ds, name):
    n = len(xs)

    def body(*refs):
        outs = iter(refs[n:])
        for x_ref in refs[:n]:
            for d in ds:
                o_ref, rows = next(outs), S // d
                for r in range(d):
                    o_ref[r * rows:(r + 1) * rows, :] = x_ref[pl.ds(r, rows, stride=d), :]

    blk = pl.BlockSpec((S, 128), lambda c: (0, c))
    w = xs[0].shape[1]
    return pl.pallas_call(
        body, grid=(w // 128,), in_specs=[blk] * n, out_specs=[blk] * (n * len(ds)),
        out_shape=[SDS((S, w), f32)] * (n * len(ds)), name=name,
        compiler_params=_params(("parallel",), 4 * S * 128 * n * (1 + len(ds))))(*xs)


def _unperm_sum(nat, perms, ds, name):
    n = len(perms)

    def body(*refs):
        a_ref, o_ref, sc = refs[0], refs[n + 1], refs[n + 2]
        acc = a_ref[...]
        for b_ref, d in zip(refs[1:n + 1], ds):
            rows = S // d
            for r in range(d):
                sc[pl.ds(r, rows, stride=d), :] = b_ref[r * rows:(r + 1) * rows, :]
            acc = acc + sc[...]
        o_ref[...] = acc

    blk = pl.BlockSpec((S, 128), lambda c: (0, c))
    w = nat.shape[1]
    return pl.pallas_call(
        body, grid=(w // 128,), in_specs=[blk] * (n + 1), out_specs=blk, out_shape=SDS((S, w), f32),
        scratch_shapes=[pltpu.VMEM((S, 128), f32)], name=name,
        compiler_params=_params(("parallel",), 4 * S * 128 * (n + 2), 8 * S * 128))(nat, *perms)


def _resid_norm_fwd(x, mix, g_post, g_pre):
    def body(x_ref, mix_ref, gp_ref, gn_ref, x2_ref, h_ref):
        x2 = x_ref[...] + _rms(mix_ref[...], gp_ref[...])
        x2_ref[...] = x2
        h_ref[...] = _rms(x2, gn_ref[...]).astype(bf16)

    return pl.pallas_call(
        body, grid=(S // TM,), in_specs=[_row(D), _row(D), _vec(D), _vec(D)], out_specs=[_row(D), _row(D)],
        out_shape=[SDS((S, D), f32), SDS((S, D), bf16)], name="resid_norm_mid",
        compiler_params=_params(("parallel",), 14 * TM * D, 16 * TM * D))(x, mix, g_post, g_pre)


def _loss_head(x2, ff, g_post, target):
    def body(x2_ref, ff_ref, g_ref, t_ref, loss_ref, dy_ref, dff_ref, dg_ref):
        ff = ff_ref[...]
        g = g_ref[...]
        err = x2_ref[...] + _rms(ff, g) - t_ref[...]
        dy = err * (1.0 / D)
        dff, dg = _rms_bwd(ff, g, dy)
        dy_ref[...] = dy
        dff_ref[...] = dff.astype(bf16)
        _acc_rows(dg_ref, dg)
        _acc_rows(loss_ref, jnp.full((1, 128), jnp.sum(err * err), f32))

    return pl.pallas_call(
        body, grid=(S // TM,), in_specs=[_row(D), _row(D), _vec(D), _row(D)],
        out_specs=[_vec(128), _row(D), _row(D), _vec(D)],
        out_shape=[SDS((1, 128), f32), SDS((S, D), f32), SDS((S, D), bf16), SDS((1, D), f32)], name="loss_head",
        compiler_params=_params(("arbitrary",), 18 * TM * D, 24 * TM * D))(x2, ff, g_post, target)


def _norm_bwd_mid(dy, dh3, x2, mix, g_ffn_pre, g_mix_post):
    def body(dy_ref, dh_ref, x2_ref, mix_ref, g3_ref, g2_ref, dx2_ref, dmix_ref, dg3_ref, dg2_ref):
        d3, dg3 = _rms_bwd(x2_ref[...], g3_ref[...], dh_ref[...])
        dx2 = dy_ref[...] + d3
        dmix, dg2 = _rms_bwd(mix_ref[...], g2_ref[...], dx2)
        dx2_ref[...] = dx2
        dmix_ref[...] = dmix.astype(bf16)
        _acc_rows(dg3_ref, dg3)
        _acc_rows(dg2_ref, dg2)

    return pl.pallas_call(
        body, grid=(S // TM,), in_specs=[_row(D)] * 4 + [_vec(D)] * 2,
        out_specs=[_row(D), _row(D), _vec(D), _vec(D)],
        out_shape=[SDS((S, D), f32), SDS((S, D), bf16), SDS((1, D), f32), SDS((1, D), f32)], name="norm_bwd_mid",
        compiler_params=_params(("arbitrary",), 22 * TM * D, 24 * TM * D))(dy, dh3, x2, mix, g_ffn_pre, g_mix_post)


def _norm_bwd_in(dx2, dh1, x, g):
    def body(dx2_ref, dh_ref, x_ref, g_ref, gx_ref, dg_ref):
        d1, dg = _rms_bwd(x_ref[...], g_ref[...], dh_ref[...])
        gx_ref[...] = dx2_ref[...] + d1
        _acc_rows(dg_ref, dg)

    return pl.pallas_call(
        body, grid=(S // TM,), in_specs=[_row(D)] * 3 + [_vec(D)], out_specs=[_row(D), _vec(D)],
        out_shape=[SDS((S, D), f32), SDS((1, D), f32)], name="norm_bwd_in",
        compiler_params=_params(("arbitrary",), 16 * TM * D, 16 * TM * D))(dx2, dh1, x, g)


def _rope_tables():
    half = ROPE_DIM // 2
    inv_freq = jnp.power(ROPE_THETA, -jnp.arange(0, ROPE_DIM, 2, dtype=f32) / ROPE_DIM)
    row = jnp.arange(S, dtype=jnp.int32)
    groups = []
    for _, d in DIL_GROUPS:
        pos = ((row % (S // d)) * d + row // (S // d)).astype(f32)
        ang = pos[:, None] * inv_freq[None, :]
        cos, sin = jnp.cos(ang), jnp.sin(ang)
        c = jnp.concatenate([cos, cos, jnp.ones((S, HD - ROPE_DIM), f32)], axis=1)
        s_lo = jnp.concatenate([-sin, jnp.zeros((S, HD - half), f32)], axis=1)
        s_hi = jnp.concatenate([jnp.zeros((S, half), f32), sin, jnp.zeros((S, HD - ROPE_DIM), f32)], axis=1)
        groups.append(jnp.stack([jnp.concatenate([t, t], axis=1) for t in (c, s_lo, s_hi)]))
    return jnp.stack(groups)


def _rotate(x, c, lo, hi, sign):
    tile = lambda t: jnp.tile(t, (1, DIL_W // 128))
    return (x * tile(c) + pltpu.roll(x, DIL_W - ROPE_DIM // 2, 1) * (tile(lo) * sign)
            + pltpu.roll(x, ROPE_DIM // 2, 1) * (tile(hi) * sign))


def _table_specs(g):
    return [pl.BlockSpec((None, None, TM, 128), lambda i, k=k: (g, k, i, 0)) for k in range(3)]


def _rope_fwd(g, p_qkv, tables):
    def body(x_ref, c_ref, lo_ref, hi_ref, o_ref):
        c, lo, hi = c_ref[...], lo_ref[...], hi_ref[...]
        for part in range(2):
            cols = slice(part * DIL_W, (part + 1) * DIL_W)
            o_ref[:, cols] = _rotate(x_ref[:, cols], c, lo, hi, 1.0).astype(bf16)
        o_ref[:, 2 * DIL_W:] = x_ref[:, 2 * DIL_W:].astype(bf16)

    return pl.pallas_call(
        body, grid=(S // TM,), in_specs=[_row(QKV_W)] + _table_specs(g), out_specs=_row(QKV_W),
        out_shape=SDS((S, QKV_W), bf16), name=f"rope_fwd_{g}",
        compiler_params=_params(("parallel",), 6 * TM * QKV_W + 12 * TM * 128, 24 * TM * QKV_W))(p_qkv, tables, tables, tables)


def _rope_bwd(g, dq, dk, dv, tables):
    def body(dq_ref, dk_ref, dv_ref, c_ref, lo_ref, hi_ref, o_ref):
        c, lo, hi = c_ref[...], lo_ref[...], hi_ref[...]
        o_ref[:, :DIL_W] = _rotate(dq_ref[...], c, lo, hi, -1.0).astype(bf16)
        o_ref[:, DIL_W:2 * DIL_W] = _rotate(dk_ref[...], c, lo, hi, -1.0).astype(bf16)
        o_ref[:, 2 * DIL_W:] = dv_ref[...].astype(bf16)

    return pl.pallas_call(
        body, grid=(S // TM,), in_specs=[_row(DIL_W)] * 3 + _table_specs(g), out_specs=_row(QKV_W),
        out_shape=SDS((S, QKV_W), bf16), name=f"rope_bwd_{g}",
        compiler_params=_params(("parallel",), 6 * TM * QKV_W + 12 * TM * 128, 24 * TM * QKV_W))(dq, dk, dv, tables, tables, tables)


def _dil_masks(n_is_first):
    qi = lax.broadcasted_iota(jnp.int32, (BLK, BLK), 0)
    kj = lax.broadcasted_iota(jnp.int32, (BLK, BLK), 1)
    cur = kj <= qi
    prev = kj >= qi + jnp.where(n_is_first, BLK + 1, 0)
    return cur, prev


def _nt(a, b):
    return lax.dot_general(a, b, (((1,), (1,)), ((), ())), preferred_element_type=f32)


def _tn(a, b):
    return lax.dot_general(a, b, (((0,), (0,)), ((), ())), preferred_element_type=f32)


def _dil_specs(g):
    _, d = DIL_GROUPS[g]
    nb = S // d // BLK
    blk = (BLK, DIL_W)
    own = lambda col: pl.BlockSpec(blk, lambda r, n: (r * nb + n, col))
    prev = lambda col: pl.BlockSpec(blk, lambda r, n: (r * nb + jnp.maximum(n - 1, 0), col))
    nxt = lambda col: pl.BlockSpec(blk, lambda r, n: (r * nb + jnp.minimum(n + 1, nb - 1), col))
    return (d, nb), own, prev, nxt


def _dil_fwd(g, qkv):
    grid, own, prev, _ = _dil_specs(g)

    def body(q_ref, kc_ref, kp_ref, vc_ref, vp_ref, o_ref, lse_ref):
        cur, prv = _dil_masks(pl.program_id(1) == 0)
        for h in range(SLOTS):
            hs = slice(h * HD, (h + 1) * HD)
            q = q_ref[:, hs]
            sc = jnp.where(cur, _nt(q, kc_ref[:, hs]) * SCALE, NEG)
            sp = jnp.where(prv, _nt(q, kp_ref[:, hs]) * SCALE, NEG)
            m = jnp.maximum(jnp.max(sc, axis=-1, keepdims=True), jnp.max(sp, axis=-1, keepdims=True))
            pc, pp = jnp.exp(sc - m), jnp.exp(sp - m)
            den = jnp.sum(pc, axis=-1, keepdims=True) + jnp.sum(pp, axis=-1, keepdims=True)
            inv = 1.0 / den
            o = jnp.dot((pc * inv).astype(bf16), vc_ref[:, hs], preferred_element_type=f32)
            o += jnp.dot((pp * inv).astype(bf16), vp_ref[:, hs], preferred_element_type=f32)
            o_ref[:, hs] = o
            lse_ref[:, hs] = jnp.broadcast_to(m + jnp.log(den), (BLK, HD))

    return pl.pallas_call(
        body, grid=grid, in_specs=[own(0), own(1), prev(1), own(2), prev(2)], out_specs=[own(0), own(0)],
        out_shape=[SDS((S, DIL_W), f32)] * 2, name=f"dil_fwd_{g}",
        compiler_params=_params(("parallel", "parallel"), 18 * BLK * DIL_W, 1 << 20))(qkv, qkv, qkv, qkv, qkv)


def _dil_combine(outs, lses):
    def body(o0, o1, o2, l0, l1, l2, out_ref, lse_ref, so1, so2, sl1, sl2):
        for (_, d), src, dst in ((DIL_GROUPS[1], o1, so1), (DIL_GROUPS[2], o2, so2),
                                 (DIL_GROUPS[1], l1, sl1), (DIL_GROUPS[2], l2, sl2)):
            rows = S // d
            for r in range(d):
                dst[pl.ds(r, rows, stride=d), :] = src[r * rows:(r + 1) * rows, :]
        a, b, c = l0[...], sl1[...], sl2[...]
        m = jnp.maximum(jnp.maximum(a, b), c)
        ea, eb, ec = jnp.exp(a - m), jnp.exp(b - m), jnp.exp(c - m)
        z = ea + eb + ec
        inv = 1.0 / z
        out_ref[...] = (ea * inv) * o0[...] + (eb * inv) * so1[...] + (ec * inv) * so2[...]
        lse_ref[...] = m + jnp.log(z)

    blk = pl.BlockSpec((S, 128), lambda c: (0, c))
    return pl.pallas_call(
        body, grid=(DIL_W // 128,), in_specs=[blk] * 6, out_specs=[blk] * 2,
        out_shape=[SDS((S, DIL_W), f32)] * 2, scratch_shapes=[pltpu.VMEM((S, 128), f32)] * 4, name="dil_combine",
        compiler_params=_params(("parallel",), 32 * S * 128, 32 * S * 128))(*outs, *lses)


def _dil_probs(q, k, mask, lse, do, v, delta):
    s = jnp.where(mask, _nt(q, k) * SCALE, NEG)
    p = jnp.exp(s - lse)
    ds = p * (_nt(do, v) - delta) * SCALE
    return p, ds


def _dil_bwd_q(g, qkv, d_out, delta, lse):
    grid, own, prev, _ = _dil_specs(g)

    def body(q_ref, kc_ref, kp_ref, vc_ref, vp_ref, do_ref, dl_ref, lse_ref, dq_ref):
        cur, prv = _dil_masks(pl.program_id(1) == 0)
        for h in range(SLOTS):
            hs = slice(h * HD, (h + 1) * HD)
            q, do = q_ref[:, hs], do_ref[:, hs].astype(bf16)
            lse, delta = lse_ref[:, h * HD:h * HD + 1], dl_ref[:, h * HD:h * HD + 1]
            _, dsc = _dil_probs(q, kc_ref[:, hs], cur, lse, do, vc_ref[:, hs], delta)
            _, dsp = _dil_probs(q, kp_ref[:, hs], prv, lse, do, vp_ref[:, hs], delta)
            dq = jnp.dot(dsc.astype(bf16), kc_ref[:, hs], preferred_element_type=f32)
            dq += jnp.dot(dsp.astype(bf16), kp_ref[:, hs], preferred_element_type=f32)
            dq_ref[:, hs] = dq

    return pl.pallas_call(
        body, grid=grid, in_specs=[own(0), own(1), prev(1), own(2), prev(2), own(0), own(0), own(0)],
        out_specs=own(0), out_shape=SDS((S, DIL_W), f32), name=f"dil_bwd_q_{g}",
        compiler_params=_params(("parallel", "parallel"), 26 * BLK * DIL_W, 1 << 20),
    )(qkv, qkv, qkv, qkv, qkv, d_out, delta, lse)


def _dil_bwd_kv(g, qkv, d_out, delta, lse):
    grid, own, _, nxt = _dil_specs(g)
    nb = grid[1]

    def body(k_ref, v_ref, qc_ref, qn_ref, doc_ref, don_ref, dc_ref, dn_ref, lc_ref, ln_ref, dk_ref, dv_ref):
        is_last = pl.program_id(1) == nb - 1
        qi = lax.broadcasted_iota(jnp.int32, (BLK, BLK), 0)
        kj = lax.broadcasted_iota(jnp.int32, (BLK, BLK), 1)
        cur = kj <= qi
        nxt = kj >= qi + jnp.where(is_last, BLK + 1, 0)
        for h in range(SLOTS):
            hs = slice(h * HD, (h + 1) * HD)
            k, v = k_ref[:, hs], v_ref[:, hs]
            dk = jnp.zeros((BLK, HD), f32)
            dv = jnp.zeros((BLK, HD), f32)
            for q_ref, do_ref, d_ref, l_ref, mask in ((qc_ref, doc_ref, dc_ref, lc_ref, cur),
                                                      (qn_ref, don_ref, dn_ref, ln_ref, nxt)):
                q, do = q_ref[:, hs], do_ref[:, hs].astype(bf16)
                p, ds = _dil_probs(q, k, mask, l_ref[:, h * HD:h * HD + 1], do, v, d_ref[:, h * HD:h * HD + 1])
                dv += _tn(p.astype(bf16), do)
                dk += _tn(ds.astype(bf16), q)
            dk_ref[:, hs] = dk
            dv_ref[:, hs] = dv

    in_specs = [own(1), own(2), own(0), nxt(0), own(0), nxt(0), own(0), nxt(0), own(0), nxt(0)]
    return pl.pallas_call(
        body, grid=grid, in_specs=in_specs, out_specs=[own(0), own(0)], out_shape=[SDS((S, DIL_W), f32)] * 2,
        name=f"dil_bwd_kv_{g}", compiler_params=_params(("parallel", "parallel"), 40 * BLK * DIL_W, 1 << 20),
    )(qkv, qkv, qkv, qkv, d_out, d_out, delta, delta, lse, lse)


def _scan_rows(x, reverse):
    row = lax.broadcasted_iota(jnp.int32, x.shape, 0)
    k = 1
    while k < S:
        if reverse:
            x = x + jnp.where(row < S - k, pltpu.roll(x, S - k, 0), 0.0)
        else:
            x = x + jnp.where(row >= k, pltpu.roll(x, k, 0), 0.0)
        k *= 2
    return x


N_PAIR = N_FOX // 2
_PAIR_Q = pl.BlockSpec((None, S, 128), lambda p: (p, 0, 0))
_PAIR_K = pl.BlockSpec((None, 8, S), lambda p: (p, 0, 0))


def _forget_fwd(fz, b128):
    def body(z_ref, b_ref, fq_ref, fk_ref):
        z = z_ref[...] + b_ref[...]
        logf = jnp.minimum(z, 0.0) - jnp.log1p(jnp.exp(-jnp.abs(z)))
        f_cum = _scan_rows(logf, reverse=False)
        f_cum_t = f_cum.T
        fq_ref[...] = jnp.zeros_like(fq_ref)
        fk_ref[...] = jnp.zeros_like(fk_ref)
        for p in range(N_PAIR):
            fq_ref[p, :, 0:2] = f_cum[:, 2 * p:2 * p + 2]
            fk_ref[p, 0:2, :] = f_cum_t[2 * p:2 * p + 2, :]

    return pl.pallas_call(
        body, grid=(1,), in_specs=[pl.BlockSpec((S, 128), lambda i: (0, 0)), _vec(128)],
        out_specs=[pl.BlockSpec((N_PAIR, S, 128), lambda i: (0, 0, 0)), pl.BlockSpec((N_PAIR, 8, S), lambda i: (0, 0, 0))],
        out_shape=[SDS((N_PAIR, S, 128), f32), SDS((N_PAIR, 8, S), f32)], name="forget_fwd",
        compiler_params=_params(("arbitrary",), 24 * S * 128, 24 * S * 128))(fz, b128)


def _forget_bwd(fz, b128, d_f_cols, d_f_rows):
    def body(z_ref, b_ref, dfc_ref, dfr_ref, dz_ref, db_ref, df_sc):
        z = z_ref[...] + b_ref[...]
        df_sc[...] = jnp.zeros_like(df_sc)
        for p in range(N_PAIR):
            df_sc[:, 2 * p:2 * p + 2] = dfr_ref[p, :, 0:2] + dfc_ref[p].T[:, 0:2]
        dz = _scan_rows(df_sc[...], reverse=True) * jax.nn.sigmoid(-z)
        dz_ref[...] = dz
        db_ref[...] = jnp.sum(dz, axis=0, keepdims=True)

    full = pl.BlockSpec((S, 128), lambda i: (0, 0))
    return pl.pallas_call(
        body, grid=(1,),
        in_specs=[full, _vec(128), pl.BlockSpec((N_PAIR, 8, S), lambda i: (0, 0, 0)), pl.BlockSpec((N_PAIR, S, 128), lambda i: (0, 0, 0))],
        out_specs=[full, _vec(128)], out_shape=[SDS((S, 128), f32), SDS((1, 128), f32)],
        scratch_shapes=[pltpu.VMEM((S, 128), f32)], name="forget_bwd",
        compiler_params=_params(("arbitrary",), 32 * S * 128, 24 * S * 128))(fz, b128, d_f_cols, d_f_rows)


def _fox_scores(q_ref, k_ref, fq_ref, fk_ref, qi, hh):
    n = (qi + 1) * TQ
    rows, hs = slice(qi * TQ, n), slice(hh * HD, (hh + 1) * HD)
    s = _nt(q_ref[rows, hs], k_ref[0:n, hs]) * SCALE + (fq_ref[rows, hh:hh + 1] - fk_ref[hh:hh + 1, 0:n])
    qpos = qi * TQ + lax.broadcasted_iota(jnp.int32, (TQ, n), 0)
    kpos = lax.broadcasted_iota(jnp.int32, (TQ, n), 1)
    return jnp.where(kpos <= qpos, s, NEG)


def _pair_cols(first):
    return pl.BlockSpec((S, 128), lambda p: (0, first + p))


def _fox_fwd(vr, fq, fk):
    def body(q_ref, k_ref, v_ref, fq_ref, fk_ref, o_ref, lse_ref):
        lse_ref[...] = jnp.zeros_like(lse_ref)
        for hh in range(2):
            hs = slice(hh * HD, (hh + 1) * HD)
            for qi in range(S // TQ):
                n = (qi + 1) * TQ
                rows = slice(qi * TQ, n)
                s = _fox_scores(q_ref, k_ref, fq_ref, fk_ref, qi, hh)
                m = jnp.max(s, axis=-1, keepdims=True)
                p = jnp.exp(s - m)
                den = jnp.sum(p, axis=-1, keepdims=True)
                o_ref[rows, hs] = jnp.dot((p * (1.0 / den)).astype(bf16), v_ref[0:n, hs], preferred_element_type=f32)
                lse_ref[rows, hh:hh + 1] = m + jnp.log(den)

    return pl.pallas_call(
        body, grid=(N_PAIR,), in_specs=[_pair_cols(0), _pair_cols(N_PAIR), _pair_cols(2 * N_PAIR), _PAIR_Q, _PAIR_K],
        out_specs=[_pair_cols(0), _PAIR_Q], out_shape=[SDS((S, FOX_W), f32), SDS((N_PAIR, S, 128), f32)],
        name="fox_fwd", compiler_params=_params(("parallel",), 12 * S * 128, 16 * TQ * S),
    )(vr, vr, vr, fq, fk)


def _fox_bwd(vr, fq, fk, lse, d_out, delta):
    def body(q_ref, k_ref, v_ref, do_ref, fq_ref, fk_ref, lse_ref, dl_ref, dq_ref, dk_ref, dv_ref, dfc_ref, dfr_ref,
             dk_sc, dv_sc):
        dfc_ref[...] = jnp.zeros_like(dfc_ref)
        dfr_ref[...] = jnp.zeros_like(dfr_ref)
        for hh in range(2):
            hs = slice(hh * HD, (hh + 1) * HD)
            dk_sc[...] = jnp.zeros_like(dk_sc)
            dv_sc[...] = jnp.zeros_like(dv_sc)
            for qi in range(S // TQ):
                n = (qi + 1) * TQ
                rows = slice(qi * TQ, n)
                q, do, k, v = q_ref[rows, hs], do_ref[rows, hs], k_ref[0:n, hs], v_ref[0:n, hs]
                p = jnp.exp(_fox_scores(q_ref, k_ref, fq_ref, fk_ref, qi, hh) - lse_ref[rows, hh:hh + 1])
                ds = p * (_nt(do, v) - dl_ref[rows, hh:hh + 1])
                dsb = ds.astype(bf16)
                dq_ref[rows, hs] = jnp.dot(dsb, k, preferred_element_type=f32) * SCALE
                dk_sc[0:n, :] += _tn(dsb, q) * SCALE
                dv_sc[0:n, :] += _tn(p.astype(bf16), do)
                dfc_ref[hh:hh + 1, 0:n] -= jnp.sum(ds, axis=0, keepdims=True)
                dfr_ref[rows, hh:hh + 1] = jnp.sum(ds, axis=-1, keepdims=True)
            dk_ref[:, hs] = dk_sc[...]
            dv_ref[:, hs] = dv_sc[...]

    cols = [_pair_cols(k * N_PAIR) for k in range(3)]
    return pl.pallas_call(
        body, grid=(N_PAIR,), in_specs=cols + [_pair_cols(0), _PAIR_Q, _PAIR_K, _PAIR_Q, _PAIR_Q],
        out_specs=[_pair_cols(0)] * 3 + [_PAIR_K, _PAIR_Q],
        out_shape=[SDS((S, FOX_W), f32)] * 3 + [SDS((N_PAIR, 8, S), f32), SDS((N_PAIR, S, 128), f32)],
        scratch_shapes=[pltpu.VMEM((S, HD), f32)] * 2, name="fox_bwd",
        compiler_params=_params(("parallel",), 32 * S * 128, 24 * TQ * S),
    )(vr, vr, vr, d_out, fq, fk, lse, delta)


def _merge_fwd(out_a, out_b, w_a, w_b, gf):
    cw = D // N_SHARD

    def body(oa_ref, ob_ref, wa_ref, wb_ref, ga_ref, gb_ref, ya_ref, yb_ref, mg_ref):
        oa, ob = oa_ref[...].astype(bf16), ob_ref[...].astype(bf16)
        for j in range(N_SHARD):
            cols = slice(j * cw, (j + 1) * cw)
            ya = jnp.dot(oa, wa_ref[j], preferred_element_type=f32)
            yb = jnp.dot(ob, wb_ref[j], preferred_element_type=f32)
            ya_ref[:, cols] = ya
            yb_ref[:, cols] = yb
            mg_ref[:, cols] = (jax.nn.sigmoid(ga_ref[:, cols]) * ya + jax.nn.sigmoid(gb_ref[:, cols]) * yb).astype(bf16)

    full = lambda a: pl.BlockSpec(a.shape, lambda i: (0, 0, 0))
    return pl.pallas_call(
        body, grid=(S // TM,),
        in_specs=[_row(DIL_W), _row(FOX_W), full(w_a), full(w_b), _row(D), pl.BlockSpec((TM, D), lambda i: (i, 1))],
        out_specs=[_row(D)] * 3, out_shape=[SDS((S, D), f32), SDS((S, D), f32), SDS((S, D), bf16)], name="merge_fwd",
        compiler_params=_params(("parallel",), 22 * TM * D + 2 * (DIL_W + FOX_W) * D, 16 * TM * D),
    )(out_a, out_b, w_a, w_b, gf, gf)


def _merge_bwd(d_merged, ya, yb, gf):
    def body(dm_ref, ya_ref, yb_ref, ga_ref, gb_ref, dya_ref, dyb_ref, dg_ref):
        dm = dm_ref[...]
        sa, sb = jax.nn.sigmoid(ga_ref[...]), jax.nn.sigmoid(gb_ref[...])
        dya_ref[...] = (dm * sa).astype(bf16)
        dyb_ref[...] = (dm * sb).astype(bf16)
        dg_ref[:, :D] = (dm * ya_ref[...] * sa * (1.0 - sa)).astype(bf16)
        dg_ref[:, D:] = (dm * yb_ref[...] * sb * (1.0 - sb)).astype(bf16)

    return pl.pallas_call(
        body, grid=(S // TM,),
        in_specs=[_row(D)] * 4 + [pl.BlockSpec((TM, D), lambda i: (i, 1))],
        out_specs=[_row(D), _row(D), _row(2 * D)],
        out_shape=[SDS((S, D), bf16), SDS((S, D), bf16), SDS((S, 2 * D), bf16)], name="merge_bwd",
        compiler_params=_params(("parallel",), 28 * TM * D, 24 * TM * D))(d_merged, ya, yb, gf, gf)


def _branch_bwd(d_ya, d_yb, w_a, w_b, out_a, out_b):
    cw = D // N_SHARD

    def body(dya_ref, dyb_ref, wa_ref, wb_ref, oa_ref, ob_ref, doa_ref, dla_ref, dob_ref, dlb_ref):
        doa = jnp.zeros((TM, DIL_W), f32)
        dob = jnp.zeros((TM, FOX_W), f32)
        for j in range(N_SHARD):
            cols = slice(j * cw, (j + 1) * cw)
            doa += _nt(dya_ref[:, cols], wa_ref[j])
            dob += _nt(dyb_ref[:, cols], wb_ref[j])
        doa_ref[...] = doa
        dob_ref[...] = dob.astype(bf16)
        prod_a = doa * oa_ref[...]
        for h in range(SLOTS):
            hs = slice(h * HD, (h + 1) * HD)
            dla_ref[:, hs] = jnp.broadcast_to(jnp.sum(prod_a[:, hs], axis=-1, keepdims=True), (TM, HD))
        prod_b = dob * ob_ref[...]
        dlb_ref[...] = jnp.zeros_like(dlb_ref)
        for h in range(N_FOX):
            dlb_ref[h // 2, :, h % 2:h % 2 + 1] = jnp.sum(prod_b[:, h * HD:(h + 1) * HD], axis=-1, keepdims=True)

    full = lambda a: pl.BlockSpec(a.shape, lambda i: (0, 0, 0))
    return pl.pallas_call(
        body, grid=(S // TM,),
        in_specs=[_row(D), _row(D), full(w_a), full(w_b), _row(DIL_W), _row(FOX_W)],
        out_specs=[_row(DIL_W), _row(DIL_W), _row(FOX_W), pl.BlockSpec((N_PAIR, TM, 128), lambda i: (0, i, 0))],
        out_shape=[SDS((S, DIL_W), f32), SDS((S, DIL_W), f32), SDS((S, FOX_W), bf16), SDS((N_PAIR, S, 128), f32)],
        name="branch_bwd", compiler_params=_params(("parallel",), 8 * TM * D + 2 * (DIL_W + FOX_W) * D, 8 * TM * D),
    )(d_ya, d_yb, w_a, w_b, out_a, out_b)


def _branch_grads(out_a, out_b, d_ya, d_yb):
    cw = D // N_SHARD

    def body(oa_ref, ob_ref, dya_ref, dyb_ref, ga_ref, gb_ref):
        ga_ref[...] = _tn(oa_ref[...].astype(bf16), dya_ref[...])
        gb_ref[...] = _tn(ob_ref[...].astype(bf16), dyb_ref[...])

    whole = lambda w: pl.BlockSpec((S, w), lambda j: (0, 0))
    cols = pl.BlockSpec((S, cw), lambda j: (0, j))
    return pl.pallas_call(
        body, grid=(N_SHARD,), in_specs=[whole(DIL_W), whole(FOX_W), cols, cols],
        out_specs=[pl.BlockSpec((None, DIL_W, cw), lambda j: (j, 0, 0)), pl.BlockSpec((None, FOX_W, cw), lambda j: (j, 0, 0))],
        out_shape=[SDS((N_SHARD, DIL_W, cw), f32), SDS((N_SHARD, FOX_W, cw), f32)], name="grad_w_proj_ab",
        compiler_params=_params(("parallel",), 4 * S * (DIL_W + FOX_W) + 4 * S * cw + 4 * (DIL_W + FOX_W) * cw,
                                4 * S * (DIL_W + FOX_W)))(out_a, out_b, d_ya, d_yb)


FF_TN = F_FF // 2
FF_TM = 512


def _ffn_fwd(h, w_gate_t, w_up_t):
    def body(h_ref, wg_ref, wu_ref, g_ref, u_ref, a_ref):
        hb = h_ref[...]
        g = _nt(hb, wg_ref[...])
        u = _nt(hb, wu_ref[...])
        g_ref[...] = g
        u_ref[...] = u
        a_ref[...] = (g * jax.nn.sigmoid(g) * u).astype(bf16)

    tile = pl.BlockSpec((FF_TM, FF_TN), lambda j, i: (i, j))
    wspec = pl.BlockSpec((FF_TN, D), lambda j, i: (j, 0))
    return pl.pallas_call(
        body, grid=(F_FF // FF_TN, S // FF_TM),
        in_specs=[pl.BlockSpec((FF_TM, D), lambda j, i: (i, 0)), wspec, wspec], out_specs=[tile] * 3,
        out_shape=[SDS((S, F_FF), f32), SDS((S, F_FF), f32), SDS((S, F_FF), bf16)], name="ffn_fwd",
        compiler_params=_params(("parallel", "parallel"), 2 * FF_TM * D + 4 * D * FF_TN + 10 * FF_TM * FF_TN, 16 * FF_TM * FF_TN),
    )(h, w_gate_t, w_up_t)


def _ffn_bwd_act(d_ff, w_down, g_act, u_act):
    def body(d_ref, wd_ref, g_ref, u_ref, dg_ref, du_ref):
        da = _nt(d_ref[...], wd_ref[...])
        g = g_ref[...]
        sg = jax.nn.sigmoid(g)
        du_ref[...] = (da * g * sg).astype(bf16)
        dg_ref[...] = (da * u_ref[...] * sg * (1.0 + g * (1.0 - sg))).astype(bf16)

    tile = pl.BlockSpec((FF_TM, FF_TN), lambda j, i: (i, j))
    return pl.pallas_call(
        body, grid=(F_FF // FF_TN, S // FF_TM),
        in_specs=[pl.BlockSpec((FF_TM, D), lambda j, i: (i, 0)), pl.BlockSpec((FF_TN, D), lambda j, i: (j, 0)), tile, tile],
        out_specs=[tile, tile], out_shape=[SDS((S, F_FF), bf16)] * 2, name="ffn_bwd_act",
        compiler_params=_params(("parallel", "parallel"), 2 * FF_TM * D + 2 * D * FF_TN + 12 * FF_TM * FF_TN, 16 * FF_TM * FF_TN),
    )(d_ff, w_down, g_act, u_act)


def _row_tile(rows):
    return next(t for t in (376, 128, 176, 64, 32, 16, 8) if rows % t == 0)


def _adamw_math(w, g, m, v):
    c1 = 1.0 - ADAM_B1 ** ADAM_STEP
    c2 = 1.0 - ADAM_B2 ** ADAM_STEP
    m_new = ADAM_B1 * m + (1.0 - ADAM_B1) * g
    v_new = ADAM_B2 * v + (1.0 - ADAM_B2) * (g * g)
    return -ADAM_LR * ((m_new / c1) / (jnp.sqrt(v_new / c2) + ADAM_EPS) + ADAM_WD * w), m_new, v_new


def _adamw(w, g, m, v, name):
    rows, cols = w.shape
    tm = _row_tile(rows)

    def body(w_ref, g_ref, m_ref, v_ref, d_ref, nm_ref, nv_ref):
        d_ref[...], nm_ref[...], nv_ref[...] = _adamw_math(w_ref[...], g_ref[...], m_ref[...], v_ref[...])

    spec = pl.BlockSpec((tm, cols), lambda i: (i, 0))
    return pl.pallas_call(
        body, grid=(rows // tm,), in_specs=[spec] * 4, out_specs=[spec] * 3, out_shape=[SDS(w.shape, f32)] * 3,
        name=name, compiler_params=_params(("parallel",), 28 * tm * cols, 16 * tm * cols))(w, g, m, v)


def _adamw_halves(w, g_mine, g_theirs, m, v, name):
    rows, cols = w.shape
    tm = _row_tile(rows // 2)
    per_half = rows // 2 // tm
    core = lax.axis_index("c").astype(jnp.int32).reshape(1)

    def body(c_ref, w_ref, gm_ref, gt_ref, m_ref, v_ref, g_ref, d_ref, nm_ref, nv_ref):
        mine = pl.program_id(0) // per_half == c_ref[0]
        g = jnp.where(mine, gm_ref[...], gt_ref[...])
        g_ref[...] = g
        d_ref[...], nm_ref[...], nv_ref[...] = _adamw_math(w_ref[...], g, m_ref[...], v_ref[...])

    spec = pl.BlockSpec((tm, cols), lambda i, c_ref: (i, 0))
    in_half = lambda i, first: jnp.clip(i - first * per_half, 0, per_half - 1)
    grid_spec = pltpu.PrefetchScalarGridSpec(
        num_scalar_prefetch=1, grid=(rows // tm,),
        in_specs=[spec, pl.BlockSpec((tm, cols), lambda i, c_ref: (in_half(i, c_ref[0]), 0)),
                  pl.BlockSpec((tm, cols), lambda i, c_ref: (in_half(i, 1 - c_ref[0]), 0)), spec, spec],
        out_specs=[spec] * 4)
    return pl.pallas_call(
        body, grid_spec=grid_spec, out_shape=[SDS(w.shape, f32)] * 4, name=name,
        compiler_params=_params(("parallel",), 36 * tm * cols, 16 * tm * cols))(core, w, g_mine, g_theirs, m, v)


_ANY = pl.BlockSpec(memory_space=pl.ANY)


def _place():
    x, y, c = lax.axis_index("x"), lax.axis_index("y"), lax.axis_index("c")
    chips = [(1 - x, y), (x, 1 - y), (1 - x, 1 - y)]
    return x, y, c, chips


def _halved(t):
    return t.reshape(t.shape[:-2] + (2, t.shape[-2] // 2, t.shape[-1]))


def _all_gather_weights(shards):
    n = len(shards)

    def body(*refs):
        _gather_body(refs[:n], refs[n:2 * n], *refs[2 * n:])

    return pl.pallas_call(
        body, in_specs=[_ANY] * n, out_specs=[_ANY] * n,
        out_shape=[SDS((N_SHARD,) + t.shape, t.dtype) for t in shards],
        scratch_shapes=[pltpu.SemaphoreType.DMA((n, 3))] * 4,
        name="all_gather_weights", compiler_params=pltpu.CompilerParams(has_side_effects=True))(*shards)


def _gather_body(src, out, send_ici, recv_ici, send_d2d, recv_d2d):
    x, y, c, chips = _place()
    sibling = (x, y, 1 - c)
    me_j = 2 * x + y
    sends = []
    for a in range(len(src)):
        for p in range(3):
            cp = pltpu.make_async_remote_copy(
                src_ref=src[a].at[c], dst_ref=out[a].at[me_j, c], send_sem=send_ici.at[a, p],
                recv_sem=recv_ici.at[a, p], device_id=(*chips[p], c), device_id_type=MESH)
            cp.start()
            sends.append(cp)
    for a in range(len(src)):
        for p, (px, py) in enumerate(chips):
            blk = out[a].at[2 * px + py, c]
            pltpu.make_async_remote_copy(
                src_ref=blk, dst_ref=blk, send_sem=send_ici.at[a, p], recv_sem=recv_ici.at[a, p],
                device_id=sibling, device_id_type=MESH).wait_recv()
            fw = pltpu.make_async_remote_copy(
                src_ref=blk, dst_ref=blk, send_sem=send_d2d.at[a, p], recv_sem=recv_d2d.at[a, p],
                device_id=sibling, device_id_type=MESH)
            fw.start()
            sends.append(fw)
    for a in range(len(src)):
        for p, (px, py) in enumerate(chips):
            blk = out[a].at[2 * px + py, 1 - c]
            pltpu.make_async_remote_copy(
                src_ref=blk, dst_ref=blk, send_sem=send_d2d.at[a, p], recv_sem=recv_d2d.at[a, p],
                device_id=sibling, device_id_type=MESH).wait_recv()
    for cp in sends:
        cp.wait_send()


def _handshake(peers):
    barrier = pltpu.get_barrier_semaphore()
    for peer in peers:
        pl.semaphore_signal(barrier, inc=1, device_id=peer, device_id_type=MESH)
    pl.semaphore_wait(barrier, len(peers))


_SEQUENCER = dict(axis_name="sequencer", num_cores=1)
GATHER_LATE_ID, SCATTER_EARLY_ID, SWAP_EARLY_ID = 1, 2, 3


def _all_gather_late(shards, after):
    n = len(shards)

    def body(*refs):
        x, y, c, chips = _place()
        _handshake([(*chip, c) for chip in chips] + [(x, y, 1 - c)])
        _gather_body(refs[:n], refs[n + 1:2 * n + 1], *refs[2 * n + 1:])

    return pl.kernel(
        body, out_type=[SDS((N_SHARD,) + t.shape, t.dtype) for t in shards],
        mesh=plsc.ScalarSubcoreMesh(**_SEQUENCER), scratch_types=[pltpu.SemaphoreType.DMA((n, 3))] * 4,
        compiler_params=pltpu.CompilerParams(collective_id=GATHER_LATE_ID), name="all_gather_late")(*shards, after)


def _pair_swap(grads):
    n = len(grads)

    def body(*refs):
        src, out, send_sems, recv_sems = refs[:n], refs[n:2 * n], refs[2 * n], refs[2 * n + 1]
        x, y, c, _ = _place()
        copies = [pltpu.make_async_remote_copy(
            src_ref=src[a].at[:, 1 - c], dst_ref=out[a], send_sem=send_sems.at[a], recv_sem=recv_sems.at[a],
            device_id=(x, y, 1 - c), device_id_type=MESH) for a in range(n)]
        for cp in copies:
            cp.start()
        for cp in copies:
            cp.wait()

    return pl.pallas_call(
        body, in_specs=[_ANY] * n, out_specs=[_ANY] * n,
        out_shape=[SDS((N_SHARD,) + t.shape[2:], t.dtype) for t in grads],
        scratch_shapes=[pltpu.SemaphoreType.DMA((n,)), pltpu.SemaphoreType.DMA((n,))], name="pair_swap",
        compiler_params=pltpu.CompilerParams(has_side_effects=True))(*grads)


def _pair_swap_early(grads):
    n = len(grads)

    def body(*refs):
        src, out, send_sems, recv_sems = refs[:n], refs[n:2 * n], refs[2 * n], refs[2 * n + 1]
        x, y, c, _ = _place()
        _handshake([(x, y, 1 - c)])
        copies = [pltpu.make_async_remote_copy(
            src_ref=src[a].at[:, 1 - c], dst_ref=out[a], send_sem=send_sems.at[a], recv_sem=recv_sems.at[a],
            device_id=(x, y, 1 - c), device_id_type=MESH) for a in range(n)]
        for cp in copies:
            cp.start()
        for cp in copies:
            cp.wait()

    return pl.kernel(
        body, out_type=[SDS((N_SHARD,) + t.shape[2:], t.dtype) for t in grads],
        mesh=plsc.ScalarSubcoreMesh(**_SEQUENCER), scratch_types=[pltpu.SemaphoreType.DMA((n,))] * 2,
        compiler_params=pltpu.CompilerParams(collective_id=SWAP_EARLY_ID), name="pair_swap_early")(*grads)


def _scatter_early(parts):
    n = len(parts)

    def body(*refs):
        part, recv, send_sems, recv_sems = refs[:n], refs[n:2 * n], refs[2 * n], refs[2 * n + 1]
        x, y, c, chips = _place()
        _handshake([(*chip, c) for chip in chips])
        me_j = 2 * x + y
        sends = []
        for a in range(n):
            for p, (px, py) in enumerate(chips):
                cp = pltpu.make_async_remote_copy(
                    src_ref=part[a].at[2 * px + py], dst_ref=recv[a].at[me_j], send_sem=send_sems.at[a, p],
                    recv_sem=recv_sems.at[a, p], device_id=(px, py, c), device_id_type=MESH)
                cp.start()
                sends.append(cp)
        for a in range(n):
            for p, (px, py) in enumerate(chips):
                slot = recv[a].at[2 * px + py]
                pltpu.make_async_remote_copy(
                    src_ref=slot, dst_ref=slot, send_sem=send_sems.at[a, p], recv_sem=recv_sems.at[a, p],
                    device_id=(px, py, c), device_id_type=MESH).wait_recv()
        for cp in sends:
            cp.wait_send()

    return pl.kernel(
        body, out_type=[SDS(t.shape, t.dtype) for t in parts],
        mesh=plsc.ScalarSubcoreMesh(**_SEQUENCER), scratch_types=[pltpu.SemaphoreType.DMA((n, 3))] * 2,
        compiler_params=pltpu.CompilerParams(collective_id=SCATTER_EARLY_ID), name="scatter_early")(*parts)


def _pair_sum(grads, other, name):
    _, _, rows, cols = grads.shape
    tr = _row_tile(rows)
    core = lax.axis_index("c").astype(jnp.int32).reshape(1)

    def body(c_ref, g_ref, o_ref, out_ref):
        out_ref[...] = (g_ref[...] + o_ref[...]).astype(bf16)

    grid_spec = pltpu.PrefetchScalarGridSpec(
        num_scalar_prefetch=1, grid=(N_SHARD, rows // tr),
        in_specs=[pl.BlockSpec((None, None, tr, cols), lambda j, i, c_ref: (j, c_ref[0], i, 0)),
                  pl.BlockSpec((None, tr, cols), lambda j, i, c_ref: (j, i, 0))],
        out_specs=pl.BlockSpec((None, tr, cols), lambda j, i, c_ref: (j, i, 0)))
    return pl.pallas_call(
        body, grid_spec=grid_spec, out_shape=SDS((N_SHARD, rows, cols), bf16), name=name,
        compiler_params=_params(("parallel", "parallel"), 10 * tr * cols, 12 * tr * cols))(core, grads, other)


def _scatter_partials(parts, small):
    n = len(parts)

    def body(*refs):
        part, small_ref, recv, small_all_ref = refs[:n], refs[n], refs[n + 1:2 * n + 1], refs[2 * n + 1]
        send_sems, recv_sems, ssend, srecv, local_sem = refs[2 * n + 2:]
        x, y, c, chips = _place()
        me_j = 2 * x + y
        me_dev = 4 * x + 2 * y + c
        own = pltpu.make_async_copy(small_ref, small_all_ref.at[me_dev], local_sem)
        own.start()
        sends = []
        for a in range(n):
            for p, (px, py) in enumerate(chips):
                cp = pltpu.make_async_remote_copy(
                    src_ref=part[a].at[2 * px + py], dst_ref=recv[a].at[me_j], send_sem=send_sems.at[a, p],
                    recv_sem=recv_sems.at[a, p], device_id=(px, py, c), device_id_type=MESH)
                cp.start()
                sends.append(cp)
        flip = lambda a, bit: 1 - a if bit else a
        peers = [(flip(x, k & 4), flip(y, k & 2), flip(c, k & 1)) for k in range(1, 8)]
        for k, to in enumerate(peers):
            cp = pltpu.make_async_remote_copy(
                src_ref=small_ref, dst_ref=small_all_ref.at[me_dev],
                send_sem=ssend.at[k], recv_sem=srecv.at[k], device_id=to, device_id_type=MESH)
            cp.start()
            sends.append(cp)
        for a in range(n):
            for p, (px, py) in enumerate(chips):
                slot = recv[a].at[2 * px + py]
                pltpu.make_async_remote_copy(
                    src_ref=slot, dst_ref=slot, send_sem=send_sems.at[a, p], recv_sem=recv_sems.at[a, p],
                    device_id=(px, py, c), device_id_type=MESH).wait_recv()
        for k, (px, py, pc) in enumerate(peers):
            slot = small_all_ref.at[4 * px + 2 * py + pc]
            pltpu.make_async_remote_copy(
                src_ref=slot, dst_ref=slot, send_sem=ssend.at[k], recv_sem=srecv.at[k],
                device_id=(px, py, pc), device_id_type=MESH).wait_recv()
        for cp in sends:
            cp.wait_send()
        own.wait()

    return pl.pallas_call(
        body, in_specs=[_ANY] * (n + 1), out_specs=[_ANY] * (n + 1),
        out_shape=[SDS(t.shape, t.dtype) for t in parts] + [SDS((8, SMALL_ROWS, D), f32)],
        scratch_shapes=[pltpu.SemaphoreType.DMA((n, 3)), pltpu.SemaphoreType.DMA((n, 3)),
                        pltpu.SemaphoreType.DMA((7,)), pltpu.SemaphoreType.DMA((7,)), pltpu.SemaphoreType.DMA],
        name="scatter_partials", compiler_params=pltpu.CompilerParams(has_side_effects=True))(*parts, small)


def _sum_partials(part, recv, name):
    _, rows, cols = recv.shape
    tr = _row_tile(rows)
    me = (2 * lax.axis_index("x") + lax.axis_index("y")).astype(jnp.int32).reshape(1)

    def body(me_ref, mine, r0, r1, r2, r3, out_ref):
        acc = None
        for j, r in enumerate((r0, r1, r2, r3)):
            term = jnp.where(me_ref[0] == j, mine[...], r[...]).astype(f32)
            acc = term if acc is None else acc + term
        out_ref[...] = acc

    slot = lambda j: pl.BlockSpec((None, tr, cols), lambda i, me_ref: (jnp.where(me_ref[0] == j, j ^ 1, j), i, 0))
    grid_spec = pltpu.PrefetchScalarGridSpec(
        num_scalar_prefetch=1, grid=(rows // tr,),
        in_specs=[pl.BlockSpec((None, tr, cols), lambda i, me_ref: (me_ref[0], i, 0)), slot(0), slot(1), slot(2), slot(3)],
        out_specs=pl.BlockSpec((tr, cols), lambda i, me_ref: (i, 0)))
    return pl.pallas_call(
        body, grid_spec=grid_spec, out_shape=SDS((rows, cols), f32), name=name,
        compiler_params=_params(("parallel",), 14 * tr * cols, 12 * tr * cols))(me, part, recv, recv, recv, recv)


def _sum_small(small_all):
    def body(small_ref, out_ref):
        tot = small_ref[0]
        for k in range(1, 8):
            tot = tot + small_ref[k]
        out_ref[...] = tot

    return pl.pallas_call(
        body, grid=(1,), in_specs=[pl.BlockSpec((8, SMALL_ROWS, D), lambda i: (0, 0, 0))],
        out_specs=pl.BlockSpec((SMALL_ROWS, D), lambda i: (0, 0)), out_shape=SDS((SMALL_ROWS, D), f32),
        name="sum_small", compiler_params=_params(("arbitrary",), 36 * SMALL_ROWS * D))(small_all)


def _swap_halves(halves):
    n = len(halves)

    def body(*refs):
        src, out, send_sems, recv_sems = refs[:n], refs[n:2 * n], refs[2 * n], refs[2 * n + 1]
        x, y, c, _ = _place()
        copies = [pltpu.make_async_remote_copy(
            src_ref=src[a], dst_ref=out[a], send_sem=send_sems.at[a], recv_sem=recv_sems.at[a],
            device_id=(x, y, 1 - c), device_id_type=MESH) for a in range(n)]
        for cp in copies:
            cp.start()
        for cp in copies:
            cp.wait()

    return pl.pallas_call(
        body, in_specs=[_ANY] * n, out_specs=[_ANY] * n, out_shape=[SDS(t.shape, f32) for t in halves],
        scratch_shapes=[pltpu.SemaphoreType.DMA((n,))] * 2, name="swap_halves",
        compiler_params=pltpu.CompilerParams(has_side_effects=True))(*halves)


def _kernel_layout(name, t):
    t = t[0]
    if name in TRANSPOSED:
        t = jnp.swapaxes(t, 0, 1)
    return _pad_rows(t, SHARD_SHAPE[name][0])


def _harness_layout(name, t):
    if name == "w_in":
        t = t[:IN_SHARD]
    if name in TRANSPOSED:
        t = jnp.swapaxes(t, 0, 1)
    return t[None]


def _pad_rows(t, rows):
    return t if t.shape[0] == rows else jnp.pad(t, ((0, rows - t.shape[0]), (0, 0)))


_QA, _KA, _VA, _QB, _F, _GAB = 0, 768, 1536, 2304, 3840, 3848


def _full_weights(gathered):
    full = {n: t.reshape((N_SHARD,) + SHARD_SHAPE[n]) for n, t in gathered.items()}
    out = {}
    if "w_in" in full:
        w_in_t = full["w_in"][:, :IN_SHARD].reshape(IN_COLS, D)
        group = lambda g: jnp.concatenate([w_in_t[o + g * DIL_W:o + (g + 1) * DIL_W] for o in (_QA, _KA, _VA)], axis=0)
        out.update(
            w_a_t=[group(g) for g in range(3)],
            w_vr_t=w_in_t[_QB:_F],
            w_fox_t=[w_in_t[_QB + k * FOX_W:_QB + (k + 1) * FOX_W] for k in range(3)],
            w_f_t=jnp.concatenate([w_in_t[_F:_GAB], jnp.zeros((128 - N_FOX, D), bf16)], axis=0),
            w_gab_t=w_in_t[_GAB:])
    if "w_out" in full:
        out.update(
            w_a4=full["w_proj_a"],
            w_b4=full["w_proj_b"],
            w_out=full["w_out"].reshape(D, D),
            w_gate_t=full["w_ffn_gate"].reshape(F_FF, D),
            w_up_t=full["w_ffn_up"].reshape(F_FF, D),
            w_down=full["w_ffn_down"].reshape(F_FF, D))
    return out


def _sharded_grads(g):
    parts = [g["w_a_t"][k][o:o + DIL_W] for o in (0, DIL_W, 2 * DIL_W) for k in range(3)]
    parts += g["w_fox_t"] + [g["w_f_t"][:N_FOX], g["w_gab_t"]]
    w_in_t = jnp.concatenate(parts, axis=0).reshape(N_SHARD, IN_SHARD, D)
    full = dict(w_in=jnp.pad(w_in_t, ((0, 0), (0, IN_SHARD_PAD - IN_SHARD), (0, 0))), w_proj_a=g["w_a4"],
                w_proj_b=g["w_b4"], w_out=g["w_out"], w_ffn_gate=g["w_gate_t"], w_ffn_up=g["w_up_t"],
                w_ffn_down=g["w_down"])
    return {n: _halved(full[n].reshape((N_SHARD,) + SHARD_SHAPE[n])) for n in W_NAMES}


def _local_step(x, target, wt, b_forget, g_mix_pre, g_mix_post, g_ffn_pre, g_ffn_post, late=None):
    tables = _rope_tables()
    b128 = jnp.pad(b_forget, ((0, 0), (0, 128 - N_FOX)))
    dils = tuple(d for _, d in DIL_GROUPS[1:])

    hs = _norm_fwd([x] + list(_perm_rows([x], dils, "perm_x")), g_mix_pre)
    h1 = hs[0]
    qkv = [_rope_fwd(g, _mm([(hs[g], wt["w_a_t"][g])], "nt", f32, tm=1024, tn=QKV_W, name=f"proj_a_{g}"), tables)
           for g in range(3)]
    vr = _mm([(h1, wt["w_vr_t"])], "nt", bf16, tm=1024, tn=VR_W // 2, name="proj_vr")
    gab = _mm([(h1, wt["w_gab_t"])], "nt", f32, tm=512, tn=2 * D, name="proj_gab")
    fz = _mm([(h1, wt["w_f_t"])], "nt", f32, tm=1024, tn=128, name="proj_f")
    dil = [_dil_fwd(g, qkv[g]) for g in range(3)]
    out_a, lse_a = _dil_combine([o for o, _ in dil], [l for _, l in dil])
    f_q, f_k = _forget_fwd(fz, b128)
    out_b, lse_b = _fox_fwd(vr, f_q, f_k)
    if late is not None:
        wt = {**wt, **late(out_b)}
    ya, yb, merged = _merge_fwd(out_a, out_b, wt["w_a4"], wt["w_b4"], gab)
    mix = _mm([(merged, wt["w_out"])], "nn", f32, tm=1024, tn=D, name="proj_out")
    x2, h3 = _resid_norm_fwd(x, mix, g_mix_post, g_ffn_pre)
    g_act, u_act, a_act = _ffn_fwd(h3, wt["w_gate_t"], wt["w_up_t"])
    ff = _mm([(a_act, wt["w_down"])], "nn", f32, tm=1024, tn=D, name="ffn_down")
    sq_err, dy, d_ff, dg_ffn_post = _loss_head(x2, ff, g_ffn_post, target)

    grads = {}
    d_g, d_u = _ffn_bwd_act(d_ff, wt["w_down"], g_act, u_act)
    grads["w_down"] = _mm([(a_act, d_ff)], "tn", f32, tm=FF_TN, tn=512, name="grad_w_down")
    grads["w_gate_t"] = _mm([(d_g, h3)], "tn", f32, tm=FF_TN, tn=512, name="grad_w_gate")
    grads["w_up_t"] = _mm([(d_u, h3)], "tn", f32, tm=FF_TN, tn=512, name="grad_w_up")
    d_h3 = _mm([(d_g, wt["w_gate_t"]), (d_u, wt["w_up_t"])], "nn", f32, tm=512, tn=512, name="ffn_bwd_in")
    dx2, d_mix, dg_ffn_pre, dg_mix_post = _norm_bwd_mid(dy, d_h3, x2, mix, g_ffn_pre, g_mix_post)

    grads["w_out"] = _mm([(merged, d_mix)], "tn", f32, tm=D, tn=D, name="grad_w_out")
    d_merged = _mm([(d_mix, wt["w_out"])], "nt", f32, tm=1024, tn=D, name="proj_out_bwd")
    d_ya, d_yb, d_gab = _merge_bwd(d_merged, ya, yb, gab)
    grads["w_a4"], grads["w_b4"] = _branch_grads(out_a, out_b, d_ya, d_yb)
    d_out_a, delta_a, d_out_b, delta_b = _branch_bwd(d_ya, d_yb, wt["w_a4"], wt["w_b4"], out_a, out_b)

    perm = _perm_rows([d_out_a, delta_a, lse_a], dils, "perm_dil_bwd")
    aux = [(d_out_a, delta_a, lse_a)] + [tuple(perm[k * len(dils) + i] for k in range(3)) for i in range(len(dils))]
    d_qkv = []
    for g in range(3):
        dq = _dil_bwd_q(g, qkv[g], *aux[g])
        dk, dv = _dil_bwd_kv(g, qkv[g], *aux[g])
        d_qkv.append(_rope_bwd(g, dq, dk, dv, tables))
    *d_fox, d_f_cols, d_f_rows = _fox_bwd(vr, f_q, f_k, lse_b, d_out_b, delta_b)
    d_z, d_b128 = _forget_bwd(fz, b128, d_f_cols, d_f_rows)

    grads["w_a_t"] = [_mm([(d_qkv[g], hs[g])], "tn", f32, tm=QKV_W, tn=D, name=f"grad_w_a_{g}") for g in range(3)]
    grads["w_fox_t"] = [_mm([(d_fox[k], h1)], "tn", f32, tm=FOX_W, tn=D, name=f"grad_w_fox_{k}") for k in range(3)]
    grads["w_gab_t"] = _mm([(d_gab, h1)], "tn", f32, tm=D, tn=D, name="grad_w_gab")
    grads["w_f_t"] = _mm([(d_z, h1)], "tn", f32, tm=128, tn=D, name="grad_w_f")
    d_h1_nat = _mm([(d_qkv[0], wt["w_a_t"][0])] + list(zip(d_fox, wt["w_fox_t"]))
                   + [(d_gab, wt["w_gab_t"]), (d_z, wt["w_f_t"])], "nn", f32, tm=512, tn=512, name="proj_in_bwd")
    d_h1_dil = [_mm([(d_qkv[g], wt["w_a_t"][g])], "nn", f32, tm=1024, tn=D, name=f"proj_a_bwd_{g}") for g in (1, 2)]
    d_h1 = _unperm_sum(d_h1_nat, d_h1_dil, dils, "unperm_d_h1")
    grad_x, dg_mix_pre = _norm_bwd_in(dx2, d_h1, x, g_mix_pre)

    small = dict(b_forget=d_b128[:, :N_FOX], norm_mix_pre=dg_mix_pre, norm_mix_post=dg_mix_post,
                 norm_ffn_pre=dg_ffn_pre, norm_ffn_post=dg_ffn_post)
    grads["mid_backward"] = d_qkv[0]
    return sq_err, grad_x, grads, small


NORMS = ("norm_mix_pre", "norm_mix_post", "norm_ffn_pre", "norm_ffn_post")
ORDER = ("w_in", "w_proj_a", "w_proj_b", "w_out", "b_forget", "w_ffn_gate", "w_ffn_up", "w_ffn_down") + NORMS


def kernel(x, w_in, w_proj_a, w_proj_b, w_out, b_forget, w_ffn_gate, w_ffn_up, w_ffn_down, norm_mix_pre, norm_mix_post, norm_ffn_pre, norm_ffn_post, loss_target, m_w_in, m_w_proj_a, m_w_proj_b, m_w_out, m_b_forget, m_w_ffn_gate, m_w_ffn_up, m_w_ffn_down, m_norm_mix_pre, m_norm_mix_post, m_norm_ffn_pre, m_norm_ffn_post, v_w_in, v_w_proj_a, v_w_proj_b, v_w_out, v_b_forget, v_w_ffn_gate, v_w_ffn_up, v_w_ffn_down, v_norm_mix_pre, v_norm_mix_post, v_norm_ffn_pre, v_norm_ffn_post):
    given = dict(w_in=w_in, w_proj_a=w_proj_a, w_proj_b=w_proj_b, w_out=w_out, w_ffn_gate=w_ffn_gate,
                 w_ffn_up=w_ffn_up, w_ffn_down=w_ffn_down)
    given_m = dict(w_in=m_w_in, w_proj_a=m_w_proj_a, w_proj_b=m_w_proj_b, w_out=m_w_out, w_ffn_gate=m_w_ffn_gate,
                   w_ffn_up=m_w_ffn_up, w_ffn_down=m_w_ffn_down)
    given_v = dict(w_in=v_w_in, w_proj_a=v_w_proj_a, w_proj_b=v_w_proj_b, w_out=v_w_out, w_ffn_gate=v_w_ffn_gate,
                   w_ffn_up=v_w_ffn_up, w_ffn_down=v_w_ffn_down)
    w, m, v = ({n: _kernel_layout(n, t[n]) for n in W_NAMES} for t in (given, given_m, given_v))
    small_w = dict(b_forget=b_forget, norm_mix_pre=norm_mix_pre, norm_mix_post=norm_mix_post,
                   norm_ffn_pre=norm_ffn_pre, norm_ffn_post=norm_ffn_post)
    small_m = dict(b_forget=m_b_forget, norm_mix_pre=m_norm_mix_pre, norm_mix_post=m_norm_mix_post,
                   norm_ffn_pre=m_norm_ffn_pre, norm_ffn_post=m_norm_ffn_post)
    small_v = dict(b_forget=v_b_forget, norm_mix_pre=v_norm_mix_pre, norm_mix_post=v_norm_mix_post,
                   norm_ffn_pre=v_norm_ffn_pre, norm_ffn_post=v_norm_ffn_post)

    own = [_halved(w[n].astype(bf16)) for n in W_NAMES]
    chip = 2 * lax.axis_index("x") + lax.axis_index("y")
    first = _all_gather_weights(own[:1])
    late = _all_gather_late(own[1:], first[0][0, 0, :16, :128])
    fill = lambda ts, mine, at: [lax.dynamic_update_index_in_dim(t, o, at, 0) for t, o in zip(ts, mine)]
    wt = _full_weights(dict(zip(W_NAMES[:1], fill(first, own[:1], chip))))

    def late_weights(ready):
        arrived, _ = lax.optimization_barrier((list(late), ready))
        return _full_weights(dict(zip(W_NAMES[1:], fill(arrived, own[1:], chip))))

    sq_err, grad_x, grads, small = _local_step(x[0], loss_target[0], wt, b_forget, norm_mix_pre, norm_mix_post,
                                               norm_ffn_pre, norm_ffn_post, late=late_weights)

    g4 = _sharded_grads(grads)
    stack = lambda t, extra: jnp.concatenate(
        [jnp.pad(t["b_forget"], ((0, 0), (0, D - N_FOX)))] + [t[n] for n in NORMS]
        + [jnp.pad(extra, ((0, SMALL_ROWS - LOSS_ROW - 1), (0, D - extra.shape[1])), constant_values=1.0)], axis=0)
    early, _ = lax.optimization_barrier((list(_pair_swap_early([g4[n] for n in W_NAMES[1:]])), grads["mid_backward"]))
    other = list(_pair_swap([g4["w_in"]])) + early
    parts = [_pair_sum(g4[n], o, "pair_sum_" + n) for n, o in zip(W_NAMES, other)]
    recv_early = _scatter_early(parts[1:])
    recv_in, small_all = _scatter_partials(parts[:1], stack(small, sq_err))
    recv = [recv_in] + list(recv_early)
    halves = [_sum_partials(p, r, "sum_partials_" + n) for n, p, r in zip(W_NAMES, parts, recv)]
    theirs = _swap_halves(halves)
    small_sum = _sum_small(small_all)
    loss = small_sum[LOSS_ROW, 0] * (0.5 / D)

    g_shard, delta, new_m, new_v = {}, {}, {}, {}
    for n, mine, other_half in zip(W_NAMES, halves, theirs):
        g_shard[n], delta[n], new_m[n], new_v[n] = _adamw_halves(w[n], mine, other_half, m[n], v[n], "adamw_" + n)
    ones = jnp.ones((1, 128), f32)
    sd, sm, sv = _adamw(stack(small_w, ones), small_sum, stack(small_m, ones), stack(small_v, ones), "adamw_small")

    outs = [loss, grad_x[None]]
    for big, st in ((g_shard, small_sum), (delta, sd), (new_m, sm), (new_v, sv)):
        t = {n: _harness_layout(n, big[n]) for n in W_NAMES}
        t["b_forget"] = st[0:1, :N_FOX]
        for i, n in enumerate(NORMS):
            t[n] = st[i + 1:i + 2]
        outs += [t[n] for n in ORDER]
    return tuple(outs)
```

```python
import functools
import math

import jax
import jax.numpy as jnp
import numpy as np
from jax import lax
from jax.experimental import pallas as pl
from jax.experimental.pallas import tpu as pltpu
from jax.experimental.pallas import tpu_sc as plsc

f32 = jnp.float32
bf16 = jnp.bfloat16
SDS = jax.ShapeDtypeStruct
MESH = pl.DeviceIdType.MESH

S = 2048
D = 1024
HD = 64
BLK = 128
N_FOX = 8
FOX_W = N_FOX * HD
DIL_GROUPS = ((128, 1), (512, 4), (2048, 16))
SLOTS = 4
DIL_W = SLOTS * HD
QKV_W = 3 * DIL_W
VR_W = 3 * FOX_W
GF_W = 2 * D + 128
F_FF = 2816
ROPE_DIM = 16
ROPE_THETA = 500000.0
EPS = 1e-6
NEG = -1e30
SCALE = 1.0 / math.sqrt(HD)
IN_COLS = 5896
N_SHARD = 4

ADAM_LR, ADAM_B1, ADAM_B2, ADAM_EPS, ADAM_WD, ADAM_STEP = 0.001, 0.9, 0.999, 1e-08, 0.01, 10

VMEM_V7X = 64 * 1024 * 1024
VMEM_PLAN_MAX = VMEM_V7X - 8 * 1024 * 1024

TM = 256
TQ = 256

W_NAMES = ("w_in", "w_proj_a", "w_proj_b", "w_out", "w_ffn_gate", "w_ffn_up", "w_ffn_down")
TRANSPOSED = ("w_in", "w_ffn_gate", "w_ffn_up")
IN_SHARD = IN_COLS // N_SHARD
IN_SHARD_PAD = 1504
SHARD_SHAPE = dict(w_in=(IN_SHARD_PAD, D), w_proj_a=(DIL_W, D // N_SHARD), w_proj_b=(FOX_W, D // N_SHARD),
                   w_out=(D // N_SHARD, D), w_ffn_gate=(F_FF // N_SHARD, D), w_ffn_up=(F_FF // N_SHARD, D),
                   w_ffn_down=(F_FF // N_SHARD, D))
SMALL_ROWS = 8
LOSS_ROW = 5


def _nbytes(shape, dtype):
    return math.prod(shape) * jnp.dtype(dtype).itemsize


def _params(semantics, block_bytes, temp_bytes=0):
    need = 2 * block_bytes + temp_bytes + (2 << 20)
    return pltpu.CompilerParams(dimension_semantics=semantics, vmem_limit_bytes=int(min(need, VMEM_PLAN_MAX)))


def _row(w, tm=TM):
    return pl.BlockSpec((tm, w), lambda i: (i, 0))


def _vec(w):
    return pl.BlockSpec((1, w), lambda i: (0, 0))


def _mm(pairs, dims, out_dtype, *, tm, tn, name, m_inner=False):
    a0, b0 = pairs[0]
    m_dim = a0.shape[1] if dims == "tn" else a0.shape[0]
    n_dim = b0.shape[0] if dims == "nt" else b0.shape[1]
    contract = {"nn": ((1,), (0,)), "nt": ((1,), (1,)), "tn": ((0,), (0,))}[dims]
    n_pairs = len(pairs)
    assert m_dim % tm == 0 and n_dim % tn == 0, (name, m_dim, n_dim, tm, tn)

    def body(*refs):
        o_ref = refs[-1]
        acc = None
        for p in range(n_pairs):
            a = refs[2 * p][...].astype(bf16)
            b = refs[2 * p + 1][...].astype(bf16)
            t = lax.dot_general(a, b, (contract, ((), ())), preferred_element_type=f32)
            acc = t if acc is None else acc + t
        o_ref[...] = acc.astype(o_ref.dtype)

    if m_inner:
        grid = (n_dim // tn, m_dim // tm)
        mi = lambda j, i: i
        ni = lambda j, i: j
    else:
        grid = (m_dim // tm, n_dim // tn)
        mi = lambda i, j: i
        ni = lambda i, j: j
    in_specs, block_bytes, args = [], 0, []
    for a, b in pairs:
        k_dim = a.shape[0] if dims == "tn" else a.shape[1]
        if dims == "tn":
            in_specs.append(pl.BlockSpec((k_dim, tm), lambda *g: (0, mi(*g))))
        else:
            in_specs.append(pl.BlockSpec((tm, k_dim), lambda *g: (mi(*g), 0)))
        if dims == "nt":
            in_specs.append(pl.BlockSpec((tn, k_dim), lambda *g: (ni(*g), 0)))
        else:
            in_specs.append(pl.BlockSpec((k_dim, tn), lambda *g: (0, ni(*g))))
        block_bytes += _nbytes((tm, k_dim), a.dtype) + _nbytes((tn, k_dim), b.dtype)
        args += [a, b]
    block_bytes += _nbytes((tm, tn), out_dtype)
    temp = _nbytes((tm, tn), f32) * 2 + sum(_nbytes((tm, a.shape[0] if dims == "tn" else a.shape[1]), bf16)
                                            + _nbytes((tn, a.shape[0] if dims == "tn" else a.shape[1]), bf16)
                                            for a, _ in pairs)
    return pl.pallas_call(
        body, grid=grid, in_specs=in_specs,
        out_specs=pl.BlockSpec((tm, tn), lambda *g: (mi(*g), ni(*g))),
        out_shape=SDS((m_dim, n_dim), out_dtype), name=name,
        compiler_params=_params(("parallel", "parallel"), block_bytes, temp),
    )(*args)


def _rms(x, g):
    r = lax.rsqrt(jnp.mean(x * x, axis=-1, keepdims=True) + EPS)
    return x * r * g


def _rms_bwd(x, g, dy):
    r = lax.rsqrt(jnp.mean(x * x, axis=-1, keepdims=True) + EPS)
    xh = x * r
    dxh = dy * g
    dx = r * (dxh - xh * jnp.mean(dxh * xh, axis=-1, keepdims=True))
    return dx, jnp.sum(dy * xh, axis=0, keepdims=True)


def _acc_rows(ref, val):
    @pl.when(pl.program_id(0) == 0)
    def _():
        ref[...] = jnp.zeros_like(ref)
    ref[...] += val


def _norm_fwd(xs, g):
    n = len(xs)

    def body(*refs):
        g = refs[n][...]
        for x_ref, h_ref in zip(refs[:n], refs[n + 1:]):
            h_ref[...] = _rms(x_ref[...], g).astype(bf16)

    return pl.pallas_call(
        body, grid=(S // TM,), in_specs=[_row(D)] * n + [_vec(D)], out_specs=[_row(D)] * n,
        out_shape=[SDS((S, D), bf16)] * n, name="norm_mix_pre",
        compiler_params=_params(("parallel",), 6 * n * TM * D, 8 * n * TM * D))(*xs, g)


def _perm_rows(xs, ds, name):
    n = len(xs)

    def body(*refs):
        outs = iter(refs[n:])
        for x_ref in refs[:n]:
            for d in ds:
                o_ref, rows = next(outs), S // d
                for r in range(d):
                    o_ref[r * rows:(r + 1) * rows, :] = x_ref[pl.ds(r, rows, stride=d), :]

    blk = pl.BlockSpec((S, 128), lambda c: (0, c))
    w = xs[0].shape[1]
    return pl.pallas_call(
        body, grid=(w // 128,), in_specs=[blk] * n, out_specs=[blk] * (n * len(ds)),
        out_shape=[SDS((S, w), f32)] * (n * len(ds)), name=name,
        compiler_params=_params(("parallel",), 4 * S * 128 * n * (1 + len(ds))))(*xs)


def _unperm_sum(nat, perms, ds, name):
    n = len(perms)

    def body(*refs):
        a_ref, o_ref, sc = refs[0], refs[n + 1], refs[n + 2]
        acc = a_ref[...]
        for b_ref, d in zip(refs[1:n + 1], ds):
            rows = S // d
            for r in range(d):
                sc[pl.ds(r, rows, stride=d), :] = b_ref[r * rows:(r + 1) * rows, :]
            acc = acc + sc[...]
        o_ref[...] = acc

    blk = pl.BlockSpec((S, 128), lambda c: (0, c))
    w = nat.shape[1]
    return pl.pallas_call(
        body, grid=(w // 128,), in_specs=[blk] * (n + 1), out_specs=blk, out_shape=SDS((S, w), f32),
        scratch_shapes=[pltpu.VMEM((S, 128), f32)], name=name,
        compiler_params=_params(("parallel",), 4 * S * 128 * (n + 2), 8 * S * 128))(nat, *perms)


def _resid_norm_fwd(x, mix, g_post, g_pre):
    def body(x_ref, mix_ref, gp_ref, gn_ref, x2_ref, h_ref):
        x2 = x_ref[...] + _rms(mix_ref[...], gp_ref[...])
        x2_ref[...] = x2
        h_ref[...] = _rms(x2, gn_ref[...]).astype(bf16)

    return pl.pallas_call(
        body, grid=(S // TM,), in_specs=[_row(D), _row(D), _vec(D), _vec(D)], out_specs=[_row(D), _row(D)],
        out_shape=[SDS((S, D), f32), SDS((S, D), bf16)], name="resid_norm_mid",
        compiler_params=_params(("parallel",), 14 * TM * D, 16 * TM * D))(x, mix, g_post, g_pre)


def _loss_head(x2, ff, g_post, target):
    def body(x2_ref, ff_ref, g_ref, t_ref, loss_ref, dy_ref, dff_ref, dg_ref):
        ff = ff_ref[...]
        g = g_ref[...]
        err = x2_ref[...] + _rms(ff, g) - t_ref[...]
        dy = err * (1.0 / D)
        dff, dg = _rms_bwd(ff, g, dy)
        dy_ref[...] = dy
        dff_ref[...] = dff.astype(bf16)
        _acc_rows(dg_ref, dg)
        _acc_rows(loss_ref, jnp.full((1, 128), jnp.sum(err * err), f32))

    return pl.pallas_call(
        body, grid=(S // TM,), in_specs=[_row(D), _row(D), _vec(D), _row(D)],
        out_specs=[_vec(128), _row(D), _row(D), _vec(D)],
        out_shape=[SDS((1, 128), f32), SDS((S, D), f32), SDS((S, D), bf16), SDS((1, D), f32)], name="loss_head",
        compiler_params=_params(("arbitrary",), 18 * TM * D, 24 * TM * D))(x2, ff, g_post, target)


def _norm_bwd_mid(dy, dh3, x2, mix, g_ffn_pre, g_mix_post):
    def body(dy_ref, dh_ref, x2_ref, mix_ref, g3_ref, g2_ref, dx2_ref, dmix_ref, dg3_ref, dg2_ref):
        d3, dg3 = _rms_bwd(x2_ref[...], g3_ref[...], dh_ref[...])
        dx2 = dy_ref[...] + d3
        dmix, dg2 = _rms_bwd(mix_ref[...], g2_ref[...], dx2)
        dx2_ref[...] = dx2
        dmix_ref[...] = dmix.astype(bf16)
        _acc_rows(dg3_ref, dg3)
        _acc_rows(dg2_ref, dg2)

    return pl.pallas_call(
        body, grid=(S // TM,), in_specs=[_row(D)] * 4 + [_vec(D)] * 2,
        out_specs=[_row(D), _row(D), _vec(D), _vec(D)],
        out_shape=[SDS((S, D), f32), SDS((S, D), bf16), SDS((1, D), f32), SDS((1, D), f32)], name="norm_bwd_mid",
        compiler_params=_params(("arbitrary",), 22 * TM * D, 24 * TM * D))(dy, dh3, x2, mix, g_ffn_pre, g_mix_post)


def _norm_bwd_in(dx2, dh1, x, g):
    def body(dx2_ref, dh_ref, x_ref, g_ref, gx_ref, dg_ref):
        d1, dg = _rms_bwd(x_ref[...], g_ref[...], dh_ref[...])
        gx_ref[...] = dx2_ref[...] + d1
        _acc_rows(dg_ref, dg)

    return pl.pallas_call(
        body, grid=(S // TM,), in_specs=[_row(D)] * 3 + [_vec(D)], out_specs=[_row(D), _vec(D)],
        out_shape=[SDS((S, D), f32), SDS((1, D), f32)], name="norm_bwd_in",
        compiler_params=_params(("arbitrary",), 16 * TM * D, 16 * TM * D))(dx2, dh1, x, g)


def _rope_tables():
    half = ROPE_DIM // 2
    inv_freq = np.power(np.float32(ROPE_THETA), -np.arange(0, ROPE_DIM, 2, dtype=np.float32) / np.float32(ROPE_DIM))
    row = np.arange(S)
    groups = []
    for _, d in DIL_GROUPS:
        pos = ((row % (S // d)) * d + row // (S // d)).astype(np.float32)
        ang = pos[:, None] * inv_freq[None, :].astype(np.float32)
        cos, sin = np.cos(ang).astype(np.float32), np.sin(ang).astype(np.float32)
        c = np.concatenate([cos, cos, np.ones((S, HD - ROPE_DIM), np.float32)], axis=1)
        s_lo = np.concatenate([-sin, np.zeros((S, HD - half), np.float32)], axis=1)
        s_hi = np.concatenate([np.zeros((S, half), np.float32), sin, np.zeros((S, HD - ROPE_DIM), np.float32)], axis=1)
        groups.append(np.stack([np.concatenate([t, t], axis=1) for t in (c, s_lo, s_hi)]))
    return jnp.asarray(np.stack(groups))


def _rotate(x, c, lo, hi, sign):
    tile = lambda t: jnp.tile(t, (1, DIL_W // 128))
    return (x * tile(c) + pltpu.roll(x, DIL_W - ROPE_DIM // 2, 1) * (tile(lo) * sign)
            + pltpu.roll(x, ROPE_DIM // 2, 1) * (tile(hi) * sign))


def _table_specs(g):
    return [pl.BlockSpec((None, None, TM, 128), lambda i, k=k: (g, k, i, 0)) for k in range(3)]


def _rope_fwd(g, p_qkv, tables):
    def body(x_ref, c_ref, lo_ref, hi_ref, o_ref):
        c, lo, hi = c_ref[...], lo_ref[...], hi_ref[...]
        for part in range(2):
            cols = slice(part * DIL_W, (part + 1) * DIL_W)
            o_ref[:, cols] = _rotate(x_ref[:, cols], c, lo, hi, 1.0).astype(bf16)
        o_ref[:, 2 * DIL_W:] = x_ref[:, 2 * DIL_W:].astype(bf16)

    return pl.pallas_call(
        body, grid=(S // TM,), in_specs=[_row(QKV_W)] + _table_specs(g), out_specs=_row(QKV_W),
        out_shape=SDS((S, QKV_W), bf16), name=f"rope_fwd_{g}",
        compiler_params=_params(("parallel",), 6 * TM * QKV_W + 12 * TM * 128, 24 * TM * QKV_W))(p_qkv, tables, tables, tables)


def _rope_bwd(g, dq, dk, dv, tables):
    def body(dq_ref, dk_ref, dv_ref, c_ref, lo_ref, hi_ref, o_ref):
        c, lo, hi = c_ref[...], lo_ref[...], hi_ref[...]
        o_ref[:, :DIL_W] = _rotate(dq_ref[...], c, lo, hi, -1.0).astype(bf16)
        o_ref[:, DIL_W:2 * DIL_W] = _rotate(dk_ref[...], c, lo, hi, -1.0).astype(bf16)
        o_ref[:, 2 * DIL_W:] = dv_ref[...].astype(bf16)

    return pl.pallas_call(
        body, grid=(S // TM,), in_specs=[_row(DIL_W)] * 3 + _table_specs(g), out_specs=_row(QKV_W),
        out_shape=SDS((S, QKV_W), bf16), name=f"rope_bwd_{g}",
        compiler_params=_params(("parallel",), 6 * TM * QKV_W + 12 * TM * 128, 24 * TM * QKV_W))(dq, dk, dv, tables, tables, tables)


def _dil_masks(n_is_first):
    qi = lax.broadcasted_iota(jnp.int32, (BLK, BLK), 0)
    kj = lax.broadcasted_iota(jnp.int32, (BLK, BLK), 1)
    cur = kj <= qi
    prev = kj >= qi + jnp.where(n_is_first, BLK + 1, 0)
    return cur, prev


def _nt(a, b):
    return lax.dot_general(a, b, (((1,), (1,)), ((), ())), preferred_element_type=f32)


def _tn(a, b):
    return lax.dot_general(a, b, (((0,), (0,)), ((), ())), preferred_element_type=f32)


def _dil_specs(g):
    _, d = DIL_GROUPS[g]
    nb = S // d // BLK
    blk = (BLK, DIL_W)
    own = lambda col: pl.BlockSpec(blk, lambda r, n: (r * nb + n, col))
    prev = lambda col: pl.BlockSpec(blk, lambda r, n: (r * nb + jnp.maximum(n - 1, 0), col))
    nxt = lambda col: pl.BlockSpec(blk, lambda r, n: (r * nb + jnp.minimum(n + 1, nb - 1), col))
    return (d, nb), own, prev, nxt


def _dil_fwd(g, qkv):
    grid, own, prev, _ = _dil_specs(g)

    def body(q_ref, kc_ref, kp_ref, vc_ref, vp_ref, o_ref, lse_ref):
        cur, prv = _dil_masks(pl.program_id(1) == 0)
        for h in range(SLOTS):
            hs = slice(h * HD, (h + 1) * HD)
            q = q_ref[:, hs]
            sc = jnp.where(cur, _nt(q, kc_ref[:, hs]) * SCALE, NEG)
            sp = jnp.where(prv, _nt(q, kp_ref[:, hs]) * SCALE, NEG)
            m = jnp.maximum(jnp.max(sc, axis=-1, keepdims=True), jnp.max(sp, axis=-1, keepdims=True))
            pc, pp = jnp.exp(sc - m), jnp.exp(sp - m)
            den = jnp.sum(pc, axis=-1, keepdims=True) + jnp.sum(pp, axis=-1, keepdims=True)
            inv = 1.0 / den
            o = jnp.dot((pc * inv).astype(bf16), vc_ref[:, hs], preferred_element_type=f32)
            o += jnp.dot((pp * inv).astype(bf16), vp_ref[:, hs], preferred_element_type=f32)
            o_ref[:, hs] = o
            lse_ref[:, hs] = jnp.broadcast_to(m + jnp.log(den), (BLK, HD))

    return pl.pallas_call(
        body, grid=grid, in_specs=[own(0), own(1), prev(1), own(2), prev(2)], out_specs=[own(0), own(0)],
        out_shape=[SDS((S, DIL_W), f32)] * 2, name=f"dil_fwd_{g}",
        compiler_params=_params(("parallel", "parallel"), 18 * BLK * DIL_W, 1 << 20))(qkv, qkv, qkv, qkv, qkv)


def _dil_combine(outs, lses):
    def body(o0, o1, o2, l0, l1, l2, out_ref, lse_ref, so1, so2, sl1, sl2):
        for (_, d), src, dst in ((DIL_GROUPS[1], o1, so1), (DIL_GROUPS[2], o2, so2),
                                 (DIL_GROUPS[1], l1, sl1), (DIL_GROUPS[2], l2, sl2)):
            rows = S // d
            for r in range(d):
                dst[pl.ds(r, rows, stride=d), :] = src[r * rows:(r + 1) * rows, :]
        a, b, c = l0[...], sl1[...], sl2[...]
        m = jnp.maximum(jnp.maximum(a, b), c)
        ea, eb, ec = jnp.exp(a - m), jnp.exp(b - m), jnp.exp(c - m)
        z = ea + eb + ec
        inv = 1.0 / z
        out_ref[...] = (ea * inv) * o0[...] + (eb * inv) * so1[...] + (ec * inv) * so2[...]
        lse_ref[...] = m + jnp.log(z)

    blk = pl.BlockSpec((S, 128), lambda c: (0, c))
    return pl.pallas_call(
        body, grid=(DIL_W // 128,), in_specs=[blk] * 6, out_specs=[blk] * 2,
        out_shape=[SDS((S, DIL_W), f32)] * 2, scratch_shapes=[pltpu.VMEM((S, 128), f32)] * 4, name="dil_combine",
        compiler_params=_params(("parallel",), 32 * S * 128, 32 * S * 128))(*outs, *lses)


def _dil_probs(q, k, mask, lse, do, v, delta):
    s = jnp.where(mask, _nt(q, k) * SCALE, NEG)
    p = jnp.exp(s - lse)
    ds = p * (_nt(do, v) - delta) * SCALE
    return p, ds


def _dil_bwd_q(g, qkv, d_out, delta, lse):
    grid, own, prev, _ = _dil_specs(g)

    def body(q_ref, kc_ref, kp_ref, vc_ref, vp_ref, do_ref, dl_ref, lse_ref, dq_ref):
        cur, prv = _dil_masks(pl.program_id(1) == 0)
        for h in range(SLOTS):
            hs = slice(h * HD, (h + 1) * HD)
            q, do = q_ref[:, hs], do_ref[:, hs].astype(bf16)
            lse, delta = lse_ref[:, h * HD:h * HD + 1], dl_ref[:, h * HD:h * HD + 1]
            _, dsc = _dil_probs(q, kc_ref[:, hs], cur, lse, do, vc_ref[:, hs], delta)
            _, dsp = _dil_probs(q, kp_ref[:, hs], prv, lse, do, vp_ref[:, hs], delta)
            dq = jnp.dot(dsc.astype(bf16), kc_ref[:, hs], preferred_element_type=f32)
            dq += jnp.dot(dsp.astype(bf16), kp_ref[:, hs], preferred_element_type=f32)
            dq_ref[:, hs] = dq

    return pl.pallas_call(
        body, grid=grid, in_specs=[own(0), own(1), prev(1), own(2), prev(2), own(0), own(0), own(0)],
        out_specs=own(0), out_shape=SDS((S, DIL_W), f32), name=f"dil_bwd_q_{g}",
        compiler_params=_params(("parallel", "parallel"), 26 * BLK * DIL_W, 1 << 20),
    )(qkv, qkv, qkv, qkv, qkv, d_out, delta, lse)


def _dil_bwd_kv(g, qkv, d_out, delta, lse):
    grid, own, _, nxt = _dil_specs(g)
    nb = grid[1]

    def body(k_ref, v_ref, qc_ref, qn_ref, doc_ref, don_ref, dc_ref, dn_ref, lc_ref, ln_ref, dk_ref, dv_ref):
        is_last = pl.program_id(1) == nb - 1
        qi = lax.broadcasted_iota(jnp.int32, (BLK, BLK), 0)
        kj = lax.broadcasted_iota(jnp.int32, (BLK, BLK), 1)
        cur = kj <= qi
        nxt = kj >= qi + jnp.where(is_last, BLK + 1, 0)
        for h in range(SLOTS):
            hs = slice(h * HD, (h + 1) * HD)
            k, v = k_ref[:, hs], v_ref[:, hs]
            dk = jnp.zeros((BLK, HD), f32)
            dv = jnp.zeros((BLK, HD), f32)
            for q_ref, do_ref, d_ref, l_ref, mask in ((qc_ref, doc_ref, dc_ref, lc_ref, cur),
                                                      (qn_ref, don_ref, dn_ref, ln_ref, nxt)):
                q, do = q_ref[:, hs], do_ref[:, hs].astype(bf16)
                p, ds = _dil_probs(q, k, mask, l_ref[:, h * HD:h * HD + 1], do, v, d_ref[:, h * HD:h * HD + 1])
                dv += _tn(p.astype(bf16), do)
                dk += _tn(ds.astype(bf16), q)
            dk_ref[:, hs] = dk
            dv_ref[:, hs] = dv

    in_specs = [own(1), own(2), own(0), nxt(0), own(0), nxt(0), own(0), nxt(0), own(0), nxt(0)]
    return pl.pallas_call(
        body, grid=grid, in_specs=in_specs, out_specs=[own(0), own(0)], out_shape=[SDS((S, DIL_W), f32)] * 2,
        name=f"dil_bwd_kv_{g}", compiler_params=_params(("parallel", "parallel"), 40 * BLK * DIL_W, 1 << 20),
    )(qkv, qkv, qkv, qkv, d_out, d_out, delta, delta, lse, lse)


def _scan_rows(x, reverse):
    row = lax.broadcasted_iota(jnp.int32, x.shape, 0)
    k = 1
    while k < S:
        if reverse:
            x = x + jnp.where(row < S - k, pltpu.roll(x, S - k, 0), 0.0)
        else:
            x = x + jnp.where(row >= k, pltpu.roll(x, k, 0), 0.0)
        k *= 2
    return x


N_PAIR = N_FOX // 2
_PAIR_Q = pl.BlockSpec((None, S, 128), lambda p: (p, 0, 0))
_PAIR_K = pl.BlockSpec((None, 8, S), lambda p: (p, 0, 0))


def _forget_fwd(fz, b128):
    def body(z_ref, b_ref, fq_ref, fk_ref):
        z = z_ref[...] + b_ref[...]
        logf = jnp.minimum(z, 0.0) - jnp.log1p(jnp.exp(-jnp.abs(z)))
        f_cum = _scan_rows(logf, reverse=False)
        f_cum_t = f_cum.T
        fq_ref[...] = jnp.zeros_like(fq_ref)
        fk_ref[...] = jnp.zeros_like(fk_ref)
        for p in range(N_PAIR):
            fq_ref[p, :, 0:2] = f_cum[:, 2 * p:2 * p + 2]
            fk_ref[p, 0:2, :] = f_cum_t[2 * p:2 * p + 2, :]

    return pl.pallas_call(
        body, grid=(1,), in_specs=[pl.BlockSpec((S, 128), lambda i: (0, 0)), _vec(128)],
        out_specs=[pl.BlockSpec((N_PAIR, S, 128), lambda i: (0, 0, 0)), pl.BlockSpec((N_PAIR, 8, S), lambda i: (0, 0, 0))],
        out_shape=[SDS((N_PAIR, S, 128), f32), SDS((N_PAIR, 8, S), f32)], name="forget_fwd",
        compiler_params=_params(("arbitrary",), 24 * S * 128, 24 * S * 128))(fz, b128)


def _forget_bwd(fz, b128, d_f_cols, d_f_rows):
    def body(z_ref, b_ref, dfc_ref, dfr_ref, dz_ref, db_ref, df_sc):
        z = z_ref[...] + b_ref[...]
        df_sc[...] = jnp.zeros_like(df_sc)
        for p in range(N_PAIR):
            df_sc[:, 2 * p:2 * p + 2] = dfr_ref[p, :, 0:2] + dfc_ref[p].T[:, 0:2]
        dz = _scan_rows(df_sc[...], reverse=True) * jax.nn.sigmoid(-z)
        dz_ref[...] = dz
        db_ref[...] = jnp.sum(dz, axis=0, keepdims=True)

    full = pl.BlockSpec((S, 128), lambda i: (0, 0))
    return pl.pallas_call(
        body, grid=(1,),
        in_specs=[full, _vec(128), pl.BlockSpec((N_PAIR, 8, S), lambda i: (0, 0, 0)), pl.BlockSpec((N_PAIR, S, 128), lambda i: (0, 0, 0))],
        out_specs=[full, _vec(128)], out_shape=[SDS((S, 128), f32), SDS((1, 128), f32)],
        scratch_shapes=[pltpu.VMEM((S, 128), f32)], name="forget_bwd",
        compiler_params=_params(("arbitrary",), 32 * S * 128, 24 * S * 128))(fz, b128, d_f_cols, d_f_rows)


def _fox_scores(q_ref, k_ref, fq_ref, fk_ref, qi, hh):
    n = (qi + 1) * TQ
    rows, hs = slice(qi * TQ, n), slice(hh * HD, (hh + 1) * HD)
    s = _nt(q_ref[rows, hs], k_ref[0:n, hs]) * SCALE + (fq_ref[rows, hh:hh + 1] - fk_ref[hh:hh + 1, 0:n])
    qpos = qi * TQ + lax.broadcasted_iota(jnp.int32, (TQ, n), 0)
    kpos = lax.broadcasted_iota(jnp.int32, (TQ, n), 1)
    return jnp.where(kpos <= qpos, s, NEG)


def _pair_cols(first):
    return pl.BlockSpec((S, 128), lambda p: (0, first + p))


def _fox_fwd(vr, fq, fk):
    def body(q_ref, k_ref, v_ref, fq_ref, fk_ref, o_ref, lse_ref):
        lse_ref[...] = jnp.zeros_like(lse_ref)
        for hh in range(2):
            hs = slice(hh * HD, (hh + 1) * HD)
            for qi in range(S // TQ):
                n = (qi + 1) * TQ
                rows = slice(qi * TQ, n)
                s = _fox_scores(q_ref, k_ref, fq_ref, fk_ref, qi, hh)
                m = jnp.max(s, axis=-1, keepdims=True)
                p = jnp.exp(s - m)
                den = jnp.sum(p, axis=-1, keepdims=True)
                o_ref[rows, hs] = jnp.dot((p * (1.0 / den)).astype(bf16), v_ref[0:n, hs], preferred_element_type=f32)
                lse_ref[rows, hh:hh + 1] = m + jnp.log(den)

    return pl.pallas_call(
        body, grid=(N_PAIR,), in_specs=[_pair_cols(0), _pair_cols(N_PAIR), _pair_cols(2 * N_PAIR), _PAIR_Q, _PAIR_K],
        out_specs=[_pair_cols(0), _PAIR_Q], out_shape=[SDS((S, FOX_W), f32), SDS((N_PAIR, S, 128), f32)],
        name="fox_fwd", compiler_params=_params(("parallel",), 12 * S * 128, 16 * TQ * S),
    )(vr, vr, vr, fq, fk)


def _fox_bwd(vr, fq, fk, lse, d_out, delta):
    def body(q_ref, k_ref, v_ref, do_ref, fq_ref, fk_ref, lse_ref, dl_ref, dq_ref, dk_ref, dv_ref, dfc_ref, dfr_ref,
             dk_sc, dv_sc):
        dfc_ref[...] = jnp.zeros_like(dfc_ref)
        dfr_ref[...] = jnp.zeros_like(dfr_ref)
        for hh in range(2):
            hs = slice(hh * HD, (hh + 1) * HD)
            dk_sc[...] = jnp.zeros_like(dk_sc)
            dv_sc[...] = jnp.zeros_like(dv_sc)
            for qi in range(S // TQ):
                n = (qi + 1) * TQ
                rows = slice(qi * TQ, n)
                q, do, k, v = q_ref[rows, hs], do_ref[rows, hs], k_ref[0:n, hs], v_ref[0:n, hs]
                p = jnp.exp(_fox_scores(q_ref, k_ref, fq_ref, fk_ref, qi, hh) - lse_ref[rows, hh:hh + 1])
                ds = p * (_nt(do, v) - dl_ref[rows, hh:hh + 1])
                dsb = ds.astype(bf16)
                dq_ref[rows, hs] = jnp.dot(dsb, k, preferred_element_type=f32) * SCALE
                dk_sc[0:n, :] += _tn(dsb, q) * SCALE
                dv_sc[0:n, :] += _tn(p.astype(bf16), do)
                dfc_ref[hh:hh + 1, 0:n] -= jnp.sum(ds, axis=0, keepdims=True)
                dfr_ref[rows, hh:hh + 1] = jnp.sum(ds, axis=-1, keepdims=True)
            dk_ref[:, hs] = dk_sc[...]
            dv_ref[:, hs] = dv_sc[...]

    cols = [_pair_cols(k * N_PAIR) for k in range(3)]
    return pl.pallas_call(
        body, grid=(N_PAIR,), in_specs=cols + [_pair_cols(0), _PAIR_Q, _PAIR_K, _PAIR_Q, _PAIR_Q],
        out_specs=[_pair_cols(0)] * 3 + [_PAIR_K, _PAIR_Q],
        out_shape=[SDS((S, FOX_W), f32)] * 3 + [SDS((N_PAIR, 8, S), f32), SDS((N_PAIR, S, 128), f32)],
        scratch_shapes=[pltpu.VMEM((S, HD), f32)] * 2, name="fox_bwd",
        compiler_params=_params(("parallel",), 32 * S * 128, 24 * TQ * S),
    )(vr, vr, vr, d_out, fq, fk, lse, delta)


def _merge_fwd(out_a, out_b, w_a, w_b, gf):
    cw = D // N_SHARD

    def body(oa_ref, ob_ref, wa_ref, wb_ref, ga_ref, gb_ref, ya_ref, yb_ref, mg_ref):
        oa, ob = oa_ref[...].astype(bf16), ob_ref[...].astype(bf16)
        for j in range(N_SHARD):
            cols = slice(j * cw, (j + 1) * cw)
            ya = jnp.dot(oa, wa_ref[j], preferred_element_type=f32)
            yb = jnp.dot(ob, wb_ref[j], preferred_element_type=f32)
            ya_ref[:, cols] = ya
            yb_ref[:, cols] = yb
            mg_ref[:, cols] = (jax.nn.sigmoid(ga_ref[:, cols]) * ya + jax.nn.sigmoid(gb_ref[:, cols]) * yb).astype(bf16)

    full = lambda a: pl.BlockSpec(a.shape, lambda i: (0, 0, 0))
    return pl.pallas_call(
        body, grid=(S // TM,),
        in_specs=[_row(DIL_W), _row(FOX_W), full(w_a), full(w_b), _row(D), pl.BlockSpec((TM, D), lambda i: (i, 1))],
        out_specs=[_row(D)] * 3, out_shape=[SDS((S, D), f32), SDS((S, D), f32), SDS((S, D), bf16)], name="merge_fwd",
        compiler_params=_params(("parallel",), 22 * TM * D + 2 * (DIL_W + FOX_W) * D, 16 * TM * D),
    )(out_a, out_b, w_a, w_b, gf, gf)


def _merge_bwd(d_merged, ya, yb, gf):
    def body(dm_ref, ya_ref, yb_ref, ga_ref, gb_ref, dya_ref, dyb_ref, dg_ref):
        dm = dm_ref[...]
        sa, sb = jax.nn.sigmoid(ga_ref[...]), jax.nn.sigmoid(gb_ref[...])
        dya_ref[...] = (dm * sa).astype(bf16)
        dyb_ref[...] = (dm * sb).astype(bf16)
        dg_ref[:, :D] = (dm * ya_ref[...] * sa * (1.0 - sa)).astype(bf16)
        dg_ref[:, D:] = (dm * yb_ref[...] * sb * (1.0 - sb)).astype(bf16)

    return pl.pallas_call(
        body, grid=(S // TM,),
        in_specs=[_row(D)] * 4 + [pl.BlockSpec((TM, D), lambda i: (i, 1))],
        out_specs=[_row(D), _row(D), _row(2 * D)],
        out_shape=[SDS((S, D), bf16), SDS((S, D), bf16), SDS((S, 2 * D), bf16)], name="merge_bwd",
        compiler_params=_params(("parallel",), 28 * TM * D, 24 * TM * D))(d_merged, ya, yb, gf, gf)


def _branch_bwd(d_ya, d_yb, w_a, w_b, out_a, out_b):
    cw = D // N_SHARD

    def body(dya_ref, dyb_ref, wa_ref, wb_ref, oa_ref, ob_ref, doa_ref, dla_ref, dob_ref, dlb_ref):
        doa = jnp.zeros((TM, DIL_W), f32)
        dob = jnp.zeros((TM, FOX_W), f32)
        for j in range(N_SHARD):
            cols = slice(j * cw, (j + 1) * cw)
            doa += _nt(dya_ref[:, cols], wa_ref[j])
            dob += _nt(dyb_ref[:, cols], wb_ref[j])
        doa_ref[...] = doa
        dob_ref[...] = dob.astype(bf16)
        prod_a = doa * oa_ref[...]
        for h in range(SLOTS):
            hs = slice(h * HD, (h + 1) * HD)
            dla_ref[:, hs] = jnp.broadcast_to(jnp.sum(prod_a[:, hs], axis=-1, keepdims=True), (TM, HD))
        prod_b = dob * ob_ref[...]
        dlb_ref[...] = jnp.zeros_like(dlb_ref)
        for h in range(N_FOX):
            dlb_ref[h // 2, :, h % 2:h % 2 + 1] = jnp.sum(prod_b[:, h * HD:(h + 1) * HD], axis=-1, keepdims=True)

    full = lambda a: pl.BlockSpec(a.shape, lambda i: (0, 0, 0))
    return pl.pallas_call(
        body, grid=(S // TM,),
        in_specs=[_row(D), _row(D), full(w_a), full(w_b), _row(DIL_W), _row(FOX_W)],
        out_specs=[_row(DIL_W), _row(DIL_W), _row(FOX_W), pl.BlockSpec((N_PAIR, TM, 128), lambda i: (0, i, 0))],
        out_shape=[SDS((S, DIL_W), f32), SDS((S, DIL_W), f32), SDS((S, FOX_W), bf16), SDS((N_PAIR, S, 128), f32)],
        name="branch_bwd", compiler_params=_params(("parallel",), 8 * TM * D + 2 * (DIL_W + FOX_W) * D, 8 * TM * D),
    )(d_ya, d_yb, w_a, w_b, out_a, out_b)


def _branch_grads(out_a, out_b, d_ya, d_yb):
    cw = D // N_SHARD

    def body(oa_ref, ob_ref, dya_ref, dyb_ref, ga_ref, gb_ref):
        ga_ref[...] = _tn(oa_ref[...].astype(bf16), dya_ref[...])
        gb_ref[...] = _tn(ob_ref[...].astype(bf16), dyb_ref[...])

    whole = lambda w: pl.BlockSpec((S, w), lambda j: (0, 0))
    cols = pl.BlockSpec((S, cw), lambda j: (0, j))
    return pl.pallas_call(
        body, grid=(N_SHARD,), in_specs=[whole(DIL_W), whole(FOX_W), cols, cols],
        out_specs=[pl.BlockSpec((None, DIL_W, cw), lambda j: (j, 0, 0)), pl.BlockSpec((None, FOX_W, cw), lambda j: (j, 0, 0))],
        out_shape=[SDS((N_SHARD, DIL_W, cw), f32), SDS((N_SHARD, FOX_W, cw), f32)], name="grad_w_proj_ab",
        compiler_params=_params(("parallel",), 4 * S * (DIL_W + FOX_W) + 4 * S * cw + 4 * (DIL_W + FOX_W) * cw,
                                4 * S * (DIL_W + FOX_W)))(out_a, out_b, d_ya, d_yb)


FF_TN = F_FF // 2
FF_TM = 512


def _ffn_fwd(h, w_gate_t, w_up_t):
    def body(h_ref, wg_ref, wu_ref, g_ref, u_ref, a_ref):
        hb = h_ref[...]
        g = _nt(hb, wg_ref[...])
        u = _nt(hb, wu_ref[...])
        g_ref[...] = g
        u_ref[...] = u
        a_ref[...] = (g * jax.nn.sigmoid(g) * u).astype(bf16)

    tile = pl.BlockSpec((FF_TM, FF_TN), lambda j, i: (i, j))
    wspec = pl.BlockSpec((FF_TN, D), lambda j, i: (j, 0))
    return pl.pallas_call(
        body, grid=(F_FF // FF_TN, S // FF_TM),
        in_specs=[pl.BlockSpec((FF_TM, D), lambda j, i: (i, 0)), wspec, wspec], out_specs=[tile] * 3,
        out_shape=[SDS((S, F_FF), f32), SDS((S, F_FF), f32), SDS((S, F_FF), bf16)], name="ffn_fwd",
        compiler_params=_params(("parallel", "parallel"), 2 * FF_TM * D + 4 * D * FF_TN + 10 * FF_TM * FF_TN, 16 * FF_TM * FF_TN),
    )(h, w_gate_t, w_up_t)


def _ffn_bwd_act(d_ff, w_down, g_act, u_act):
    def body(d_ref, wd_ref, g_ref, u_ref, dg_ref, du_ref):
        da = _nt(d_ref[...], wd_ref[...])
        g = g_ref[...]
        sg = jax.nn.sigmoid(g)
        du_ref[...] = (da * g * sg).astype(bf16)
        dg_ref[...] = (da * u_ref[...] * sg * (1.0 + g * (1.0 - sg))).astype(bf16)

    tile = pl.BlockSpec((FF_TM, FF_TN), lambda j, i: (i, j))
    return pl.pallas_call(
        body, grid=(F_FF // FF_TN, S // FF_TM),
        in_specs=[pl.BlockSpec((FF_TM, D), lambda j, i: (i, 0)), pl.BlockSpec((FF_TN, D), lambda j, i: (j, 0)), tile, tile],
        out_specs=[tile, tile], out_shape=[SDS((S, F_FF), bf16)] * 2, name="ffn_bwd_act",
        compiler_params=_params(("parallel", "parallel"), 2 * FF_TM * D + 2 * D * FF_TN + 12 * FF_TM * FF_TN, 16 * FF_TM * FF_TN),
    )(d_ff, w_down, g_act, u_act)


def _row_tile(rows):
    return next(t for t in (376, 128, 176, 64, 32, 16, 8) if rows % t == 0)


def _adamw_math(w, g, m, v):
    c1 = 1.0 - ADAM_B1 ** ADAM_STEP
    c2 = 1.0 - ADAM_B2 ** ADAM_STEP
    m_new = ADAM_B1 * m + (1.0 - ADAM_B1) * g
    v_new = ADAM_B2 * v + (1.0 - ADAM_B2) * (g * g)
    return -ADAM_LR * ((m_new / c1) / (jnp.sqrt(v_new / c2) + ADAM_EPS) + ADAM_WD * w), m_new, v_new


def _adamw(w, g, m, v, name):
    rows, cols = w.shape
    tm = _row_tile(rows)

    def body(w_ref, g_ref, m_ref, v_ref, d_ref, nm_ref, nv_ref):
        d_ref[...], nm_ref[...], nv_ref[...] = _adamw_math(w_ref[...], g_ref[...], m_ref[...], v_ref[...])

    spec = pl.BlockSpec((tm, cols), lambda i: (i, 0))
    return pl.pallas_call(
        body, grid=(rows // tm,), in_specs=[spec] * 4, out_specs=[spec] * 3, out_shape=[SDS(w.shape, f32)] * 3,
        name=name, compiler_params=_params(("parallel",), 28 * tm * cols, 16 * tm * cols))(w, g, m, v)


def _adamw_halves(w, g_mine, g_theirs, m, v, name):
    rows, cols = w.shape
    tm = _row_tile(rows // 2)
    per_half = rows // 2 // tm
    core = lax.axis_index("c").astype(jnp.int32).reshape(1)

    def body(c_ref, w_ref, gm_ref, gt_ref, m_ref, v_ref, g_ref, d_ref, nm_ref, nv_ref):
        mine = pl.program_id(0) // per_half == c_ref[0]
        g = jnp.where(mine, gm_ref[...], gt_ref[...])
        g_ref[...] = g
        d_ref[...], nm_ref[...], nv_ref[...] = _adamw_math(w_ref[...], g, m_ref[...], v_ref[...])

    spec = pl.BlockSpec((tm, cols), lambda i, c_ref: (i, 0))
    in_half = lambda i, first: jnp.clip(i - first * per_half, 0, per_half - 1)
    grid_spec = pltpu.PrefetchScalarGridSpec(
        num_scalar_prefetch=1, grid=(rows // tm,),
        in_specs=[spec, pl.BlockSpec((tm, cols), lambda i, c_ref: (in_half(i, c_ref[0]), 0)),
                  pl.BlockSpec((tm, cols), lambda i, c_ref: (in_half(i, 1 - c_ref[0]), 0)), spec, spec],
        out_specs=[spec] * 4)
    return pl.pallas_call(
        body, grid_spec=grid_spec, out_shape=[SDS(w.shape, f32)] * 4, name=name,
        compiler_params=_params(("parallel",), 36 * tm * cols, 16 * tm * cols))(core, w, g_mine, g_theirs, m, v)


_ANY = pl.BlockSpec(memory_space=pl.ANY)


def _place():
    x, y, c = lax.axis_index("x"), lax.axis_index("y"), lax.axis_index("c")
    chips = [(1 - x, y), (x, 1 - y), (1 - x, 1 - y)]
    return x, y, c, chips


def _halved(t):
    return t.reshape(t.shape[:-2] + (2, t.shape[-2] // 2, t.shape[-1]))


def _gather_body(src, out, send_ici, recv_ici, send_d2d, recv_d2d):
    x, y, c, chips = _place()
    sibling = (x, y, 1 - c)
    me_j = 2 * x + y
    sends = []
    for a in range(len(src)):
        for p in range(3):
            cp = pltpu.make_async_remote_copy(
                src_ref=src[a].at[c], dst_ref=out[a].at[me_j, c], send_sem=send_ici.at[a, p],
                recv_sem=recv_ici.at[a, p], device_id=(*chips[p], c), device_id_type=MESH)
            cp.start()
            sends.append(cp)
    for a in range(len(src)):
        for p, (px, py) in enumerate(chips):
            blk = out[a].at[2 * px + py, c]
            pltpu.make_async_remote_copy(
                src_ref=blk, dst_ref=blk, send_sem=send_ici.at[a, p], recv_sem=recv_ici.at[a, p],
                device_id=sibling, device_id_type=MESH).wait_recv()
            fw = pltpu.make_async_remote_copy(
                src_ref=blk, dst_ref=blk, send_sem=send_d2d.at[a, p], recv_sem=recv_d2d.at[a, p],
                device_id=sibling, device_id_type=MESH)
            fw.start()
            sends.append(fw)
    for a in range(len(src)):
        for p, (px, py) in enumerate(chips):
            blk = out[a].at[2 * px + py, 1 - c]
            pltpu.make_async_remote_copy(
                src_ref=blk, dst_ref=blk, send_sem=send_d2d.at[a, p], recv_sem=recv_d2d.at[a, p],
                device_id=sibling, device_id_type=MESH).wait_recv()
    for cp in sends:
        cp.wait_send()


def _handshake(peers):
    barrier = pltpu.get_barrier_semaphore()
    for peer in peers:
        pl.semaphore_signal(barrier, inc=1, device_id=peer, device_id_type=MESH)
    pl.semaphore_wait(barrier, len(peers))


_SEQUENCER = dict(axis_name="sequencer", num_cores=1)
GATHER_LATE_ID, SCATTER_EARLY_ID, SWAP_EARLY_ID, GATHER_FIRST_ID, SCATTER_LATE_ID = 1, 2, 3, 4, 5


def _all_gather_async(shards, after, name, collective_id):
    n, k = len(shards), len(after)

    def body(*refs):
        x, y, c, chips = _place()
        _handshake([(*chip, c) for chip in chips] + [(x, y, 1 - c)])
        _gather_body(refs[:n], refs[n + k:2 * n + k], *refs[2 * n + k:])

    return pl.kernel(
        body, out_type=[SDS((N_SHARD,) + t.shape, t.dtype) for t in shards],
        mesh=plsc.ScalarSubcoreMesh(**_SEQUENCER), scratch_types=[pltpu.SemaphoreType.DMA((n, 3))] * 4,
        compiler_params=pltpu.CompilerParams(collective_id=collective_id), name=name)(*shards, *after)


def _pair_swap(grads):
    n = len(grads)

    def body(*refs):
        src, out, send_sems, recv_sems = refs[:n], refs[n:2 * n], refs[2 * n], refs[2 * n + 1]
        x, y, c, _ = _place()
        copies = [pltpu.make_async_remote_copy(
            src_ref=src[a].at[:, 1 - c], dst_ref=out[a], send_sem=send_sems.at[a], recv_sem=recv_sems.at[a],
            device_id=(x, y, 1 - c), device_id_type=MESH) for a in range(n)]
        for cp in copies:
            cp.start()
        for cp in copies:
            cp.wait()

    return pl.pallas_call(
        body, in_specs=[_ANY] * n, out_specs=[_ANY] * n,
        out_shape=[SDS((N_SHARD,) + t.shape[2:], t.dtype) for t in grads],
        scratch_shapes=[pltpu.SemaphoreType.DMA((n,)), pltpu.SemaphoreType.DMA((n,))], name="pair_swap",
        compiler_params=pltpu.CompilerParams(has_side_effects=True))(*grads)


def _pair_swap_early(grads):
    n = len(grads)

    def body(*refs):
        src, out, send_sems, recv_sems = refs[:n], refs[n:2 * n], refs[2 * n], refs[2 * n + 1]
        x, y, c, _ = _place()
        _handshake([(x, y, 1 - c)])
        copies = [pltpu.make_async_remote_copy(
            src_ref=src[a].at[:, 1 - c], dst_ref=out[a], send_sem=send_sems.at[a], recv_sem=recv_sems.at[a],
            device_id=(x, y, 1 - c), device_id_type=MESH) for a in range(n)]
        for cp in copies:
            cp.start()
        for cp in copies:
            cp.wait()

    return pl.kernel(
        body, out_type=[SDS((N_SHARD,) + t.shape[2:], t.dtype) for t in grads],
        mesh=plsc.ScalarSubcoreMesh(**_SEQUENCER), scratch_types=[pltpu.SemaphoreType.DMA((n,))] * 2,
        compiler_params=pltpu.CompilerParams(collective_id=SWAP_EARLY_ID), name="pair_swap_early")(*grads)


def _scatter_early(parts):
    n = len(parts)

    def body(*refs):
        part, recv, send_sems, recv_sems = refs[:n], refs[n:2 * n], refs[2 * n], refs[2 * n + 1]
        x, y, c, chips = _place()
        _handshake([(*chip, c) for chip in chips])
        me_j = 2 * x + y
        sends = []
        for a in range(n):
            for p, (px, py) in enumerate(chips):
                cp = pltpu.make_async_remote_copy(
                    src_ref=part[a].at[2 * px + py], dst_ref=recv[a].at[me_j], send_sem=send_sems.at[a, p],
                    recv_sem=recv_sems.at[a, p], device_id=(px, py, c), device_id_type=MESH)
                cp.start()
                sends.append(cp)
        for a in range(n):
            for p, (px, py) in enumerate(chips):
                slot = recv[a].at[2 * px + py]
                pltpu.make_async_remote_copy(
                    src_ref=slot, dst_ref=slot, send_sem=send_sems.at[a, p], recv_sem=recv_sems.at[a, p],
                    device_id=(px, py, c), device_id_type=MESH).wait_recv()
        for cp in sends:
            cp.wait_send()

    return pl.kernel(
        body, out_type=[SDS(t.shape, t.dtype) for t in parts],
        mesh=plsc.ScalarSubcoreMesh(**_SEQUENCER), scratch_types=[pltpu.SemaphoreType.DMA((n, 3))] * 2,
        compiler_params=pltpu.CompilerParams(collective_id=SCATTER_EARLY_ID), name="scatter_early")(*parts)


def _pair_sum(grads, other, name):
    _, _, rows, cols = grads.shape
    tr = _row_tile(rows)
    core = lax.axis_index("c").astype(jnp.int32).reshape(1)

    def body(c_ref, g_ref, o_ref, out_ref):
        out_ref[...] = (g_ref[...] + o_ref[...]).astype(bf16)

    grid_spec = pltpu.PrefetchScalarGridSpec(
        num_scalar_prefetch=1, grid=(N_SHARD, rows // tr),
        in_specs=[pl.BlockSpec((None, None, tr, cols), lambda j, i, c_ref: (j, c_ref[0], i, 0)),
                  pl.BlockSpec((None, tr, cols), lambda j, i, c_ref: (j, i, 0))],
        out_specs=pl.BlockSpec((None, tr, cols), lambda j, i, c_ref: (j, i, 0)))
    return pl.pallas_call(
        body, grid_spec=grid_spec, out_shape=SDS((N_SHARD, rows, cols), bf16), name=name,
        compiler_params=_params(("parallel", "parallel"), 10 * tr * cols, 12 * tr * cols))(core, grads, other)


def _scatter_partials(parts, small):
    n = len(parts)

    def body(*refs):
        part, small_ref, recv, small_all_ref = refs[:n], refs[n], refs[n + 1:2 * n + 1], refs[2 * n + 1]
        send_sems, recv_sems, ssend, srecv, local_sem = refs[2 * n + 2:]
        x, y, c, chips = _place()
        flip = lambda a, bit: 1 - a if bit else a
        peers = [(flip(x, k & 4), flip(y, k & 2), flip(c, k & 1)) for k in range(1, 8)]
        _handshake(peers)
        me_j = 2 * x + y
        me_dev = 4 * x + 2 * y + c
        own = pltpu.make_async_copy(small_ref, small_all_ref.at[me_dev], local_sem)
        own.start()
        sends = []
        for a in range(n):
            for p, (px, py) in enumerate(chips):
                cp = pltpu.make_async_remote_copy(
                    src_ref=part[a].at[2 * px + py], dst_ref=recv[a].at[me_j], send_sem=send_sems.at[a, p],
                    recv_sem=recv_sems.at[a, p], device_id=(px, py, c), device_id_type=MESH)
                cp.start()
                sends.append(cp)
        for k, to in enumerate(peers):
            cp = pltpu.make_async_remote_copy(
                src_ref=small_ref, dst_ref=small_all_ref.at[me_dev],
                send_sem=ssend.at[k], recv_sem=srecv.at[k], device_id=to, device_id_type=MESH)
            cp.start()
            sends.append(cp)
        for a in range(n):
            for p, (px, py) in enumerate(chips):
                slot = recv[a].at[2 * px + py]
                pltpu.make_async_remote_copy(
                    src_ref=slot, dst_ref=slot, send_sem=send_sems.at[a, p], recv_sem=recv_sems.at[a, p],
                    device_id=(px, py, c), device_id_type=MESH).wait_recv()
        for k, (px, py, pc) in enumerate(peers):
            slot = small_all_ref.at[4 * px + 2 * py + pc]
            pltpu.make_async_remote_copy(
                src_ref=slot, dst_ref=slot, send_sem=ssend.at[k], recv_sem=srecv.at[k],
                device_id=(px, py, pc), device_id_type=MESH).wait_recv()
        for cp in sends:
            cp.wait_send()
        own.wait()

    return pl.kernel(
        body, out_type=[SDS(t.shape, t.dtype) for t in parts] + [SDS((8, SMALL_ROWS, D), f32)],
        mesh=plsc.ScalarSubcoreMesh(**_SEQUENCER),
        scratch_types=[pltpu.SemaphoreType.DMA((n, 3)), pltpu.SemaphoreType.DMA((n, 3)),
                       pltpu.SemaphoreType.DMA((7,)), pltpu.SemaphoreType.DMA((7,)), pltpu.SemaphoreType.DMA],
        compiler_params=pltpu.CompilerParams(collective_id=SCATTER_LATE_ID), name="scatter_partials")(*parts, small)


def _sum_partials(part, recv, name):
    _, rows, cols = recv.shape
    tr = _row_tile(rows)
    me = (2 * lax.axis_index("x") + lax.axis_index("y")).astype(jnp.int32).reshape(1)

    def body(me_ref, mine, r0, r1, r2, r3, out_ref):
        acc = None
        for j, r in enumerate((r0, r1, r2, r3)):
            term = jnp.where(me_ref[0] == j, mine[...], r[...]).astype(f32)
            acc = term if acc is None else acc + term
        out_ref[...] = acc

    slot = lambda j: pl.BlockSpec((None, tr, cols), lambda i, me_ref: (jnp.where(me_ref[0] == j, j ^ 1, j), i, 0))
    grid_spec = pltpu.PrefetchScalarGridSpec(
        num_scalar_prefetch=1, grid=(rows // tr,),
        in_specs=[pl.BlockSpec((None, tr, cols), lambda i, me_ref: (me_ref[0], i, 0)), slot(0), slot(1), slot(2), slot(3)],
        out_specs=pl.BlockSpec((tr, cols), lambda i, me_ref: (i, 0)))
    return pl.pallas_call(
        body, grid_spec=grid_spec, out_shape=SDS((rows, cols), f32), name=name,
        compiler_params=_params(("parallel",), 14 * tr * cols, 12 * tr * cols))(me, part, recv, recv, recv, recv)


def _sum_small(small_all):
    def body(small_ref, out_ref):
        tot = small_ref[0]
        for k in range(1, 8):
            tot = tot + small_ref[k]
        out_ref[...] = tot

    return pl.pallas_call(
        body, grid=(1,), in_specs=[pl.BlockSpec((8, SMALL_ROWS, D), lambda i: (0, 0, 0))],
        out_specs=pl.BlockSpec((SMALL_ROWS, D), lambda i: (0, 0)), out_shape=SDS((SMALL_ROWS, D), f32),
        name="sum_small", compiler_params=_params(("arbitrary",), 36 * SMALL_ROWS * D))(small_all)


def _swap_halves(halves, name):
    n = len(halves)

    def body(*refs):
        src, out, send_sems, recv_sems = refs[:n], refs[n:2 * n], refs[2 * n], refs[2 * n + 1]
        x, y, c, _ = _place()
        copies = [pltpu.make_async_remote_copy(
            src_ref=src[a], dst_ref=out[a], send_sem=send_sems.at[a], recv_sem=recv_sems.at[a],
            device_id=(x, y, 1 - c), device_id_type=MESH) for a in range(n)]
        for cp in copies:
            cp.start()
        for cp in copies:
            cp.wait()

    return pl.pallas_call(
        body, in_specs=[_ANY] * n, out_specs=[_ANY] * n, out_shape=[SDS(t.shape, f32) for t in halves],
        scratch_shapes=[pltpu.SemaphoreType.DMA((n,))] * 2, name=name,
        compiler_params=pltpu.CompilerParams(has_side_effects=True))(*halves)


def _kernel_layout(name, t):
    t = t[0]
    if name in TRANSPOSED:
        t = jnp.swapaxes(t, 0, 1)
    return _pad_rows(t, SHARD_SHAPE[name][0])


def _harness_layout(name, t):
    if name == "w_in":
        t = t[:IN_SHARD]
    if name in TRANSPOSED:
        t = jnp.swapaxes(t, 0, 1)
    return t[None]


def _pad_rows(t, rows):
    return t if t.shape[0] == rows else jnp.pad(t, ((0, rows - t.shape[0]), (0, 0)))


_QA, _KA, _VA, _QB, _F, _GAB = 0, 768, 1536, 2304, 3840, 3848


def _full_weights(gathered):
    full = {n: t.reshape((N_SHARD,) + SHARD_SHAPE[n]) for n, t in gathered.items()}
    out = {}
    if "w_in" in full:
        w_in_t = full["w_in"][:, :IN_SHARD].reshape(IN_COLS, D)
        group = lambda g: jnp.concatenate([w_in_t[o + g * DIL_W:o + (g + 1) * DIL_W] for o in (_QA, _KA, _VA)], axis=0)
        out.update(
            w_a_t=[group(g) for g in range(3)],
            w_vr_t=w_in_t[_QB:_F],
            w_fox_t=[w_in_t[_QB + k * FOX_W:_QB + (k + 1) * FOX_W] for k in range(3)],
            w_f_t=jnp.concatenate([w_in_t[_F:_GAB], jnp.zeros((128 - N_FOX, D), bf16)], axis=0),
            w_gab_t=w_in_t[_GAB:])
    if "w_out" in full:
        out.update(
            w_a4=full["w_proj_a"],
            w_b4=full["w_proj_b"],
            w_out=full["w_out"].reshape(D, D),
            w_gate_t=full["w_ffn_gate"].reshape(F_FF, D),
            w_up_t=full["w_ffn_up"].reshape(F_FF, D),
            w_down=full["w_ffn_down"].reshape(F_FF, D))
    return out


def _sharded_grads(g):
    parts = [g["w_a_t"][k][o:o + DIL_W] for o in (0, DIL_W, 2 * DIL_W) for k in range(3)]
    parts += g["w_fox_t"] + [g["w_f_t"][:N_FOX], g["w_gab_t"]]
    w_in_t = jnp.concatenate(parts, axis=0).reshape(N_SHARD, IN_SHARD, D)
    full = dict(w_in=jnp.pad(w_in_t, ((0, 0), (0, IN_SHARD_PAD - IN_SHARD), (0, 0))), w_proj_a=g["w_a4"],
                w_proj_b=g["w_b4"], w_out=g["w_out"], w_ffn_gate=g["w_gate_t"], w_ffn_up=g["w_up_t"],
                w_ffn_down=g["w_down"])
    return {n: _halved(full[n].reshape((N_SHARD,) + SHARD_SHAPE[n])) for n in W_NAMES}


def _local_step(x, target, wt, b_forget, g_mix_pre, g_mix_post, g_ffn_pre, g_ffn_post, late=None):
    tables = _rope_tables()
    b128 = jnp.pad(b_forget, ((0, 0), (0, 128 - N_FOX)))
    dils = tuple(d for _, d in DIL_GROUPS[1:])

    hs = _norm_fwd([x] + list(_perm_rows([x], dils, "perm_x")), g_mix_pre)
    h1 = hs[0]
    if callable(wt):
        wt = wt(h1)
    qkv = [_rope_fwd(g, _mm([(hs[g], wt["w_a_t"][g])], "nt", f32, tm=1024, tn=QKV_W, name=f"proj_a_{g}"), tables)
           for g in range(3)]
    vr = _mm([(h1, wt["w_vr_t"])], "nt", bf16, tm=1024, tn=VR_W // 2, name="proj_vr")
    gab = _mm([(h1, wt["w_gab_t"])], "nt", f32, tm=512, tn=2 * D, name="proj_gab")
    fz = _mm([(h1, wt["w_f_t"])], "nt", f32, tm=1024, tn=128, name="proj_f")
    dil = [_dil_fwd(g, qkv[g]) for g in range(3)]
    out_a, lse_a = _dil_combine([o for o, _ in dil], [l for _, l in dil])
    f_q, f_k = _forget_fwd(fz, b128)
    out_b, lse_b = _fox_fwd(vr, f_q, f_k)
    if late is not None:
        wt = {**wt, **late(out_b)}
    ya, yb, merged = _merge_fwd(out_a, out_b, wt["w_a4"], wt["w_b4"], gab)
    mix = _mm([(merged, wt["w_out"])], "nn", f32, tm=1024, tn=D, name="proj_out")
    x2, h3 = _resid_norm_fwd(x, mix, g_mix_post, g_ffn_pre)
    g_act, u_act, a_act = _ffn_fwd(h3, wt["w_gate_t"], wt["w_up_t"])
    ff = _mm([(a_act, wt["w_down"])], "nn", f32, tm=1024, tn=D, name="ffn_down")
    sq_err, dy, d_ff, dg_ffn_post = _loss_head(x2, ff, g_ffn_post, target)

    grads = {}
    d_g, d_u = _ffn_bwd_act(d_ff, wt["w_down"], g_act, u_act)
    grads["w_down"] = _mm([(a_act, d_ff)], "tn", f32, tm=FF_TN, tn=512, name="grad_w_down")
    grads["w_gate_t"] = _mm([(d_g, h3)], "tn", f32, tm=FF_TN, tn=512, name="grad_w_gate")
    grads["w_up_t"] = _mm([(d_u, h3)], "tn", f32, tm=FF_TN, tn=512, name="grad_w_up")
    d_h3 = _mm([(d_g, wt["w_gate_t"]), (d_u, wt["w_up_t"])], "nn", f32, tm=512, tn=512, name="ffn_bwd_in")
    dx2, d_mix, dg_ffn_pre, dg_mix_post = _norm_bwd_mid(dy, d_h3, x2, mix, g_ffn_pre, g_mix_post)

    grads["w_out"] = _mm([(merged, d_mix)], "tn", f32, tm=D, tn=D, name="grad_w_out")
    d_merged = _mm([(d_mix, wt["w_out"])], "nt", f32, tm=1024, tn=D, name="proj_out_bwd")
    d_ya, d_yb, d_gab = _merge_bwd(d_merged, ya, yb, gab)
    grads["w_a4"], grads["w_b4"] = _branch_grads(out_a, out_b, d_ya, d_yb)
    d_out_a, delta_a, d_out_b, delta_b = _branch_bwd(d_ya, d_yb, wt["w_a4"], wt["w_b4"], out_a, out_b)

    perm = _perm_rows([d_out_a, delta_a, lse_a], dils, "perm_dil_bwd")
    aux = [(d_out_a, delta_a, lse_a)] + [tuple(perm[k * len(dils) + i] for k in range(3)) for i in range(len(dils))]
    d_qkv = []
    for g in range(3):
        dq = _dil_bwd_q(g, qkv[g], *aux[g])
        dk, dv = _dil_bwd_kv(g, qkv[g], *aux[g])
        d_qkv.append(_rope_bwd(g, dq, dk, dv, tables))
    *d_fox, d_f_cols, d_f_rows = _fox_bwd(vr, f_q, f_k, lse_b, d_out_b, delta_b)
    d_z, d_b128 = _forget_bwd(fz, b128, d_f_cols, d_f_rows)

    grads["w_a_t"] = [_mm([(d_qkv[g], hs[g])], "tn", f32, tm=QKV_W, tn=D, name=f"grad_w_a_{g}") for g in range(3)]
    grads["w_fox_t"] = [_mm([(d_fox[k], h1)], "tn", f32, tm=FOX_W, tn=D, name=f"grad_w_fox_{k}") for k in range(3)]
    grads["w_gab_t"] = _mm([(d_gab, h1)], "tn", f32, tm=D, tn=D, name="grad_w_gab")
    grads["w_f_t"] = _mm([(d_z, h1)], "tn", f32, tm=128, tn=D, name="grad_w_f")
    d_h1_nat = _mm([(d_qkv[0], wt["w_a_t"][0])] + list(zip(d_fox, wt["w_fox_t"]))
                   + [(d_gab, wt["w_gab_t"]), (d_z, wt["w_f_t"])], "nn", f32, tm=512, tn=512, name="proj_in_bwd")
    d_h1_dil = [_mm([(d_qkv[g], wt["w_a_t"][g])], "nn", f32, tm=1024, tn=D, name=f"proj_a_bwd_{g}") for g in (1, 2)]
    d_h1 = _unperm_sum(d_h1_nat, d_h1_dil, dils, "unperm_d_h1")
    grad_x, dg_mix_pre = _norm_bwd_in(dx2, d_h1, x, g_mix_pre)

    small = dict(b_forget=d_b128[:, :N_FOX], norm_mix_pre=dg_mix_pre, norm_mix_post=dg_mix_post,
                 norm_ffn_pre=dg_ffn_pre, norm_ffn_post=dg_ffn_post)
    grads["mid_backward"] = d_qkv[0]
    return sq_err, grad_x, grads, small


NORMS = ("norm_mix_pre", "norm_mix_post", "norm_ffn_pre", "norm_ffn_post")
ORDER = ("w_in", "w_proj_a", "w_proj_b", "w_out", "b_forget", "w_ffn_gate", "w_ffn_up", "w_ffn_down") + NORMS


def kernel(x, w_in, w_proj_a, w_proj_b, w_out, b_forget, w_ffn_gate, w_ffn_up, w_ffn_down, norm_mix_pre, norm_mix_post, norm_ffn_pre, norm_ffn_post, loss_target, m_w_in, m_w_proj_a, m_w_proj_b, m_w_out, m_b_forget, m_w_ffn_gate, m_w_ffn_up, m_w_ffn_down, m_norm_mix_pre, m_norm_mix_post, m_norm_ffn_pre, m_norm_ffn_post, v_w_in, v_w_proj_a, v_w_proj_b, v_w_out, v_b_forget, v_w_ffn_gate, v_w_ffn_up, v_w_ffn_down, v_norm_mix_pre, v_norm_mix_post, v_norm_ffn_pre, v_norm_ffn_post):
    given = dict(w_in=w_in, w_proj_a=w_proj_a, w_proj_b=w_proj_b, w_out=w_out, w_ffn_gate=w_ffn_gate,
                 w_ffn_up=w_ffn_up, w_ffn_down=w_ffn_down)
    given_m = dict(w_in=m_w_in, w_proj_a=m_w_proj_a, w_proj_b=m_w_proj_b, w_out=m_w_out, w_ffn_gate=m_w_ffn_gate,
                   w_ffn_up=m_w_ffn_up, w_ffn_down=m_w_ffn_down)
    given_v = dict(w_in=v_w_in, w_proj_a=v_w_proj_a, w_proj_b=v_w_proj_b, w_out=v_w_out, w_ffn_gate=v_w_ffn_gate,
                   w_ffn_up=v_w_ffn_up, w_ffn_down=v_w_ffn_down)
    w, m, v = ({n: _kernel_layout(n, t[n]) for n in W_NAMES} for t in (given, given_m, given_v))
    small_w = dict(b_forget=b_forget, norm_mix_pre=norm_mix_pre, norm_mix_post=norm_mix_post,
                   norm_ffn_pre=norm_ffn_pre, norm_ffn_post=norm_ffn_post)
    small_m = dict(b_forget=m_b_forget, norm_mix_pre=m_norm_mix_pre, norm_mix_post=m_norm_mix_post,
                   norm_ffn_pre=m_norm_ffn_pre, norm_ffn_post=m_norm_ffn_post)
    small_v = dict(b_forget=v_b_forget, norm_mix_pre=v_norm_mix_pre, norm_mix_post=v_norm_mix_post,
                   norm_ffn_pre=v_norm_ffn_pre, norm_ffn_post=v_norm_ffn_post)

    own = [_halved(w[n].astype(bf16)) for n in W_NAMES]
    chip = 2 * lax.axis_index("x") + lax.axis_index("y")
    exchanged = {"first": _all_gather_async(own[:1], [], "all_gather_first", GATHER_FIRST_ID)}
    fill = lambda ts, mine: [lax.dynamic_update_index_in_dim(t, o, chip, 0) for t, o in zip(ts, mine)]

    def first_weights(ready):
        arrived, _ = lax.optimization_barrier((list(exchanged["first"]), ready))
        exchanged["late"] = _all_gather_async(own[1:], [arrived[0][0, 0, :16, :128]], "all_gather_late", GATHER_LATE_ID)
        return _full_weights(dict(zip(W_NAMES[:1], fill(arrived, own[:1]))))

    def late_weights(ready):
        arrived, _ = lax.optimization_barrier((list(exchanged["late"]), ready))
        return _full_weights(dict(zip(W_NAMES[1:], fill(arrived, own[1:]))))

    sq_err, grad_x, grads, small = _local_step(x[0], loss_target[0], first_weights, b_forget, norm_mix_pre,
                                               norm_mix_post, norm_ffn_pre, norm_ffn_post, late=late_weights)

    g4 = _sharded_grads(grads)
    stack = lambda t, extra: jnp.concatenate(
        [jnp.pad(t["b_forget"], ((0, 0), (0, D - N_FOX)))] + [t[n] for n in NORMS]
        + [jnp.pad(extra, ((0, SMALL_ROWS - LOSS_ROW - 1), (0, D - extra.shape[1])), constant_values=1.0)], axis=0)
    early, _ = lax.optimization_barrier((list(_pair_swap_early([g4[n] for n in W_NAMES[1:]])), grads["mid_backward"]))
    other = list(_pair_swap([g4["w_in"]])) + early
    parts = [_pair_sum(g4[n], o, "pair_sum_" + n) for n, o in zip(W_NAMES, other)]
    recv_early = _scatter_early(parts[1:])
    recv_in, small_all = _scatter_partials(parts[:1], stack(small, sq_err))

    g_shard, delta, new_m, new_v = {}, {}, {}, {}

    def finish(names, parts, recv):
        halves = [_sum_partials(p, r, "sum_partials_" + n) for n, p, r in zip(names, parts, recv)]
        theirs = _swap_halves(halves, "swap_halves_" + names[0])
        for n, mine, other_half in zip(names, halves, theirs):
            g_shard[n], delta[n], new_m[n], new_v[n] = _adamw_halves(w[n], mine, other_half, m[n], v[n], "adamw_" + n)

    recv_early, _ = lax.optimization_barrier((list(recv_early), parts[0]))
    finish(W_NAMES[1:], parts[1:], recv_early)
    (recv_in, small_all), _ = lax.optimization_barrier(((recv_in, small_all), [delta[n] for n in W_NAMES[1:]]))
    finish(W_NAMES[:1], parts[:1], [recv_in])
    small_sum = _sum_small(small_all)
    loss = small_sum[LOSS_ROW, 0] * (0.5 / D)
    ones = jnp.ones((1, 128), f32)
    sd, sm, sv = _adamw(stack(small_w, ones), small_sum, stack(small_m, ones), stack(small_v, ones), "adamw_small")

    outs = [loss, grad_x[None]]
    for big, st in ((g_shard, small_sum), (delta, sd), (new_m, sm), (new_v, sv)):
        t = {n: _harness_layout(n, big[n]) for n in W_NAMES}
        t["b_forget"] = st[0:1, :N_FOX]
        for i, n in enumerate(NORMS):
            t[n] = st[i + 1:i + 2]
        outs += [t[n] for n in ORDER]
    return tuple(outs)
```

```python
import functools
import math

import jax
import jax.numpy as jnp
import numpy as np
from jax import lax
from jax.experimental import pallas as pl
from jax.experimental.pallas import tpu as pltpu
from jax.experimental.pallas import tpu_sc as plsc

f32 = jnp.float32
bf16 = jnp.bfloat16
SDS = jax.ShapeDtypeStruct
MESH = pl.DeviceIdType.MESH

S = 2048
D = 1024
HD = 64
BLK = 128
N_FOX = 8
FOX_W = N_FOX * HD
DIL_GROUPS = ((128, 1), (512, 4), (2048, 16))
SLOTS = 4
DIL_W = SLOTS * HD
QKV_W = 3 * DIL_W
VR_W = 3 * FOX_W
GF_W = 2 * D + 128
F_FF = 2816
ROPE_DIM = 16
ROPE_THETA = 500000.0
EPS = 1e-6
NEG = -1e30
SCALE = 1.0 / math.sqrt(HD)
IN_COLS = 5896
N_SHARD = 4

ADAM_LR, ADAM_B1, ADAM_B2, ADAM_EPS, ADAM_WD, ADAM_STEP = 0.001, 0.9, 0.999, 1e-08, 0.01, 10

VMEM_V7X = 64 * 1024 * 1024
VMEM_PLAN_MAX = VMEM_V7X - 8 * 1024 * 1024

TM = 256
TQ = 256

W_NAMES = ("w_in", "w_proj_a", "w_proj_b", "w_out", "w_ffn_gate", "w_ffn_up", "w_ffn_down")
TRANSPOSED = ("w_in", "w_ffn_gate", "w_ffn_up")
IN_SHARD = IN_COLS // N_SHARD
IN_SHARD_PAD = 1504
SHARD_SHAPE = dict(w_in=(IN_SHARD_PAD, D), w_proj_a=(DIL_W, D // N_SHARD), w_proj_b=(FOX_W, D // N_SHARD),
                   w_out=(D // N_SHARD, D), w_ffn_gate=(F_FF // N_SHARD, D), w_ffn_up=(F_FF // N_SHARD, D),
                   w_ffn_down=(F_FF // N_SHARD, D))
SMALL_ROWS = 8
LOSS_ROW = 5


def _nbytes(shape, dtype):
    return math.prod(shape) * jnp.dtype(dtype).itemsize


def _params(semantics, block_bytes, temp_bytes=0):
    need = 2 * block_bytes + temp_bytes + (2 << 20)
    return pltpu.CompilerParams(dimension_semantics=semantics, vmem_limit_bytes=int(min(need, VMEM_PLAN_MAX)))


def _row(w, tm=TM):
    return pl.BlockSpec((tm, w), lambda i: (i, 0))


def _vec(w):
    return pl.BlockSpec((1, w), lambda i: (0, 0))


def _mm(pairs, dims, out_dtype, *, tm, tn, name, m_inner=False):
    a0, b0 = pairs[0]
    m_dim = a0.shape[1] if dims == "tn" else a0.shape[0]
    n_dim = b0.shape[0] if dims == "nt" else b0.shape[1]
    contract = {"nn": ((1,), (0,)), "nt": ((1,), (1,)), "tn": ((0,), (0,))}[dims]
    n_pairs = len(pairs)
    assert m_dim % tm == 0 and n_dim % tn == 0, (name, m_dim, n_dim, tm, tn)

    def body(*refs):
        o_ref = refs[-1]
        acc = None
        for p in range(n_pairs):
            a = refs[2 * p][...].astype(bf16)
            b = refs[2 * p + 1][...].astype(bf16)
            t = lax.dot_general(a, b, (contract, ((), ())), preferred_element_type=f32)
            acc = t if acc is None else acc + t
        o_ref[...] = acc.astype(o_ref.dtype)

    if m_inner:
        grid = (n_dim // tn, m_dim // tm)
        mi = lambda j, i: i
        ni = lambda j, i: j
    else:
        grid = (m_dim // tm, n_dim // tn)
        mi = lambda i, j: i
        ni = lambda i, j: j
    in_specs, block_bytes, args = [], 0, []
    for a, b in pairs:
        k_dim = a.shape[0] if dims == "tn" else a.shape[1]
        if dims == "tn":
            in_specs.append(pl.BlockSpec((k_dim, tm), lambda *g: (0, mi(*g))))
        else:
            in_specs.append(pl.BlockSpec((tm, k_dim), lambda *g: (mi(*g), 0)))
        if dims == "nt":
            in_specs.append(pl.BlockSpec((tn, k_dim), lambda *g: (ni(*g), 0)))
        else:
            in_specs.append(pl.BlockSpec((k_dim, tn), lambda *g: (0, ni(*g))))
        block_bytes += _nbytes((tm, k_dim), a.dtype) + _nbytes((tn, k_dim), b.dtype)
        args += [a, b]
    block_bytes += _nbytes((tm, tn), out_dtype)
    temp = _nbytes((tm, tn), f32) * 2 + sum(_nbytes((tm, a.shape[0] if dims == "tn" else a.shape[1]), bf16)
                                            + _nbytes((tn, a.shape[0] if dims == "tn" else a.shape[1]), bf16)
                                            for a, _ in pairs)
    return pl.pallas_call(
        body, grid=grid, in_specs=in_specs,
        out_specs=pl.BlockSpec((tm, tn), lambda *g: (mi(*g), ni(*g))),
        out_shape=SDS((m_dim, n_dim), out_dtype), name=name,
        compiler_params=_params(("parallel", "parallel"), block_bytes, temp),
    )(*args)


def _rms(x, g):
    r = lax.rsqrt(jnp.mean(x * x, axis=-1, keepdims=True) + EPS)
    return x * r * g


def _rms_bwd(x, g, dy):
    r = lax.rsqrt(jnp.mean(x * x, axis=-1, keepdims=True) + EPS)
    xh = x * r
    dxh = dy * g
    dx = r * (dxh - xh * jnp.mean(dxh * xh, axis=-1, keepdims=True))
    return dx, jnp.sum(dy * xh, axis=0, keepdims=True)


def _acc_rows(ref, val):
    @pl.when(pl.program_id(0) == 0)
    def _():
        ref[...] = jnp.zeros_like(ref)
    ref[...] += val


def _norm_fwd(xs, g):
    n = len(xs)

    def body(*refs):
        g = refs[n][...]
        for x_ref, h_ref in zip(refs[:n], refs[n + 1:]):
            h_ref[...] = _rms(x_ref[...], g).astype(bf16)

    return pl.pallas_call(
        body, grid=(S // TM,), in_specs=[_row(D)] * n + [_vec(D)], out_specs=[_row(D)] * n,
        out_shape=[SDS((S, D), bf16)] * n, name="norm_mix_pre",
        compiler_params=_params(("parallel",), 6 * n * TM * D, 8 * n * TM * D))(*xs, g)


def _perm_rows(xs, ds, name):
    n = len(xs)

    def body(*refs):
        outs = iter(refs[n:])
        for x_ref in refs[:n]:
            for d in ds:
                o_ref, rows = next(outs), S // d
                for r in range(d):
                    o_ref[r * rows:(r + 1) * rows, :] = x_ref[pl.ds(r, rows, stride=d), :]

    blk = pl.BlockSpec((S, 128), lambda c: (0, c))
    w = xs[0].shape[1]
    return pl.pallas_call(
        body, grid=(w // 128,), in_specs=[blk] * n, out_specs=[blk] * (n * len(ds)),
        out_shape=[SDS((S, w), f32)] * (n * len(ds)), name=name,
        compiler_params=_params(("parallel",), 4 * S * 128 * n * (1 + len(ds))))(*xs)


def _unperm_sum(nat, perms, ds, name):
    n = len(perms)

    def body(*refs):
        a_ref, o_ref, sc = refs[0], refs[n + 1], refs[n + 2]
        acc = a_ref[...]
        for b_ref, d in zip(refs[1:n + 1], ds):
            rows = S // d
            for r in range(d):
                sc[pl.ds(r, rows, stride=d), :] = b_ref[r * rows:(r + 1) * rows, :]
            acc = acc + sc[...]
        o_ref[...] = acc

    blk = pl.BlockSpec((S, 128), lambda c: (0, c))
    w = nat.shape[1]
    return pl.pallas_call(
        body, grid=(w // 128,), in_specs=[blk] * (n + 1), out_specs=blk, out_shape=SDS((S, w), f32),
        scratch_shapes=[pltpu.VMEM((S, 128), f32)], name=name,
        compiler_params=_params(("parallel",), 4 * S * 128 * (n + 2), 8 * S * 128))(nat, *perms)


def _resid_norm_fwd(x, mix, g_post, g_pre):
    def body(x_ref, mix_ref, gp_ref, gn_ref, x2_ref, h_ref):
        x2 = x_ref[...] + _rms(mix_ref[...], gp_ref[...])
        x2_ref[...] = x2
        h_ref[...] = _rms(x2, gn_ref[...]).astype(bf16)

    return pl.pallas_call(
        body, grid=(S // TM,), in_specs=[_row(D), _row(D), _vec(D), _vec(D)], out_specs=[_row(D), _row(D)],
        out_shape=[SDS((S, D), f32), SDS((S, D), bf16)], name="resid_norm_mid",
        compiler_params=_params(("parallel",), 14 * TM * D, 16 * TM * D))(x, mix, g_post, g_pre)


def _loss_head(x2, ff, g_post, target):
    def body(x2_ref, ff_ref, g_ref, t_ref, loss_ref, dy_ref, dff_ref, dg_ref):
        ff = ff_ref[...]
        g = g_ref[...]
        err = x2_ref[...] + _rms(ff, g) - t_ref[...]
        dy = err * (1.0 / D)
        dff, dg = _rms_bwd(ff, g, dy)
        dy_ref[...] = dy
        dff_ref[...] = dff.astype(bf16)
        _acc_rows(dg_ref, dg)
        _acc_rows(loss_ref, jnp.full((1, 128), jnp.sum(err * err), f32))

    return pl.pallas_call(
        body, grid=(S // TM,), in_specs=[_row(D), _row(D), _vec(D), _row(D)],
        out_specs=[_vec(128), _row(D), _row(D), _vec(D)],
        out_shape=[SDS((1, 128), f32), SDS((S, D), f32), SDS((S, D), bf16), SDS((1, D), f32)], name="loss_head",
        compiler_params=_params(("arbitrary",), 18 * TM * D, 24 * TM * D))(x2, ff, g_post, target)


def _norm_bwd_mid(dy, dh3, x2, mix, g_ffn_pre, g_mix_post):
    def body(dy_ref, dh_ref, x2_ref, mix_ref, g3_ref, g2_ref, dx2_ref, dmix_ref, dg3_ref, dg2_ref):
        d3, dg3 = _rms_bwd(x2_ref[...], g3_ref[...], dh_ref[...])
        dx2 = dy_ref[...] + d3
        dmix, dg2 = _rms_bwd(mix_ref[...], g2_ref[...], dx2)
        dx2_ref[...] = dx2
        dmix_ref[...] = dmix.astype(bf16)
        _acc_rows(dg3_ref, dg3)
        _acc_rows(dg2_ref, dg2)

    return pl.pallas_call(
        body, grid=(S // TM,), in_specs=[_row(D)] * 4 + [_vec(D)] * 2,
        out_specs=[_row(D), _row(D), _vec(D), _vec(D)],
        out_shape=[SDS((S, D), f32), SDS((S, D), bf16), SDS((1, D), f32), SDS((1, D), f32)], name="norm_bwd_mid",
        compiler_params=_params(("arbitrary",), 22 * TM * D, 24 * TM * D))(dy, dh3, x2, mix, g_ffn_pre, g_mix_post)


def _norm_bwd_in(dx2, dh1, x, g):
    def body(dx2_ref, dh_ref, x_ref, g_ref, gx_ref, dg_ref):
        d1, dg = _rms_bwd(x_ref[...], g_ref[...], dh_ref[...])
        gx_ref[...] = dx2_ref[...] + d1
        _acc_rows(dg_ref, dg)

    return pl.pallas_call(
        body, grid=(S // TM,), in_specs=[_row(D)] * 3 + [_vec(D)], out_specs=[_row(D), _vec(D)],
        out_shape=[SDS((S, D), f32), SDS((1, D), f32)], name="norm_bwd_in",
        compiler_params=_params(("arbitrary",), 16 * TM * D, 16 * TM * D))(dx2, dh1, x, g)


def _rope_tables():
    half = ROPE_DIM // 2
    inv_freq = np.power(np.float32(ROPE_THETA), -np.arange(0, ROPE_DIM, 2, dtype=np.float32) / np.float32(ROPE_DIM))
    row = np.arange(S)
    groups = []
    for _, d in DIL_GROUPS:
        pos = ((row % (S // d)) * d + row // (S // d)).astype(np.float32)
        ang = pos[:, None] * inv_freq[None, :].astype(np.float32)
        cos, sin = np.cos(ang).astype(np.float32), np.sin(ang).astype(np.float32)
        c = np.concatenate([cos, cos, np.ones((S, HD - ROPE_DIM), np.float32)], axis=1)
        s_lo = np.concatenate([-sin, np.zeros((S, HD - half), np.float32)], axis=1)
        s_hi = np.concatenate([np.zeros((S, half), np.float32), sin, np.zeros((S, HD - ROPE_DIM), np.float32)], axis=1)
        groups.append(np.stack([np.concatenate([t, t], axis=1) for t in (c, s_lo, s_hi)]))
    return jnp.asarray(np.stack(groups))


def _rotate(x, c, lo, hi, sign):
    tile = lambda t: jnp.tile(t, (1, DIL_W // 128))
    return (x * tile(c) + pltpu.roll(x, DIL_W - ROPE_DIM // 2, 1) * (tile(lo) * sign)
            + pltpu.roll(x, ROPE_DIM // 2, 1) * (tile(hi) * sign))


def _table_specs(g):
    return [pl.BlockSpec((None, None, TM, 128), lambda i, k=k: (g, k, i, 0)) for k in range(3)]


def _rope_fwd(g, p_qkv, tables):
    def body(x_ref, c_ref, lo_ref, hi_ref, o_ref):
        c, lo, hi = c_ref[...], lo_ref[...], hi_ref[...]
        for part in range(2):
            cols = slice(part * DIL_W, (part + 1) * DIL_W)
            o_ref[:, cols] = _rotate(x_ref[:, cols], c, lo, hi, 1.0).astype(bf16)
        o_ref[:, 2 * DIL_W:] = x_ref[:, 2 * DIL_W:].astype(bf16)

    return pl.pallas_call(
        body, grid=(S // TM,), in_specs=[_row(QKV_W)] + _table_specs(g), out_specs=_row(QKV_W),
        out_shape=SDS((S, QKV_W), bf16), name=f"rope_fwd_{g}",
        compiler_params=_params(("parallel",), 6 * TM * QKV_W + 12 * TM * 128, 24 * TM * QKV_W))(p_qkv, tables, tables, tables)


def _rope_bwd(g, dq, dk, dv, tables):
    def body(dq_ref, dk_ref, dv_ref, c_ref, lo_ref, hi_ref, o_ref):
        c, lo, hi = c_ref[...], lo_ref[...], hi_ref[...]
        o_ref[:, :DIL_W] = _rotate(dq_ref[...], c, lo, hi, -1.0).astype(bf16)
        o_ref[:, DIL_W:2 * DIL_W] = _rotate(dk_ref[...], c, lo, hi, -1.0).astype(bf16)
        o_ref[:, 2 * DIL_W:] = dv_ref[...].astype(bf16)

    return pl.pallas_call(
        body, grid=(S // TM,), in_specs=[_row(DIL_W)] * 3 + _table_specs(g), out_specs=_row(QKV_W),
        out_shape=SDS((S, QKV_W), bf16), name=f"rope_bwd_{g}",
        compiler_params=_params(("parallel",), 6 * TM * QKV_W + 12 * TM * 128, 24 * TM * QKV_W))(dq, dk, dv, tables, tables, tables)


def _nt(a, b):
    return lax.dot_general(a, b, (((1,), (1,)), ((), ())), preferred_element_type=f32)


def _tn(a, b):
    return lax.dot_general(a, b, (((0,), (0,)), ((), ())), preferred_element_type=f32)


STEP_BLOCKS = 4
STEP_ROWS = STEP_BLOCKS * BLK


def _dil_prev(g, b):
    _, d = DIL_GROUPS[g]
    nb = S // d // BLK
    if nb == 1 or (b == 0 and nb <= STEP_BLOCKS):
        return None
    return "in" if b > 0 else "halo"


def _dil_blocks(g, qkv_ref, halo_ref):
    qi = lax.broadcasted_iota(jnp.int32, (BLK, BLK), 0)
    kj = lax.broadcasted_iota(jnp.int32, (BLK, BLK), 1)
    for b in range(STEP_BLOCKS):
        rows = slice(b * BLK, (b + 1) * BLK)
        where = _dil_prev(g, b)
        for h in range(SLOTS):
            cols = [slice(k * DIL_W + h * HD, k * DIL_W + (h + 1) * HD) for k in range(3)]
            q, kc, vc = (qkv_ref[rows, c] for c in cols)
            if where is None:
                yield b, h, rows, q, kc, vc, None, None, None
            elif where == "in":
                prev = slice((b - 1) * BLK, b * BLK)
                yield b, h, rows, q, kc, vc, qkv_ref[prev, cols[1]], qkv_ref[prev, cols[2]], kj >= qi
            else:
                mask = kj >= qi + jnp.where(pl.program_id(0) == 0, BLK + 1, 0)
                yield b, h, rows, q, kc, vc, halo_ref[:, cols[1]], halo_ref[:, cols[2]], mask


def _dil_in_specs(g, n_aux):
    step = lambda w: pl.BlockSpec((STEP_ROWS, w), lambda i: (i, 0))
    halo = [pl.BlockSpec((BLK, QKV_W), lambda i: (jnp.maximum(i * STEP_BLOCKS - 1, 0), 0))]
    needs_halo = _dil_prev(g, 0) == "halo"
    return [step(QKV_W)] + (halo if needs_halo else []) + [step(DIL_W)] * n_aux, needs_halo


def _dil_fwd(g, qkv):
    in_specs, needs_halo = _dil_in_specs(g, 0)

    def body(*refs):
        qkv_ref, halo_ref = refs[0], refs[1] if needs_halo else None
        o_ref, lse_ref = refs[-2:]
        cur = lax.broadcasted_iota(jnp.int32, (BLK, BLK), 1) <= lax.broadcasted_iota(jnp.int32, (BLK, BLK), 0)
        for b, h, rows, q, kc, vc, kp, vp, prev in _dil_blocks(g, qkv_ref, halo_ref):
            hs = slice(h * HD, (h + 1) * HD)
            sc = jnp.where(cur, _nt(q, kc) * SCALE, NEG)
            m = jnp.max(sc, axis=-1, keepdims=True)
            if kp is not None:
                sp = jnp.where(prev, _nt(q, kp) * SCALE, NEG)
                m = jnp.maximum(m, jnp.max(sp, axis=-1, keepdims=True))
                pp = jnp.exp(sp - m)
            pc = jnp.exp(sc - m)
            den = jnp.sum(pc, axis=-1, keepdims=True)
            if kp is not None:
                den = den + jnp.sum(pp, axis=-1, keepdims=True)
            inv = 1.0 / den
            o = jnp.dot((pc * inv).astype(bf16), vc, preferred_element_type=f32)
            if kp is not None:
                o += jnp.dot((pp * inv).astype(bf16), vp, preferred_element_type=f32)
            o_ref[rows, hs] = o
            lse_ref[rows, hs] = jnp.broadcast_to(m + jnp.log(den), (BLK, HD))

    out = pl.BlockSpec((STEP_ROWS, DIL_W), lambda i: (i, 0))
    return pl.pallas_call(
        body, grid=(S // STEP_ROWS,), in_specs=in_specs, out_specs=[out, out], out_shape=[SDS((S, DIL_W), f32)] * 2,
        name=f"dil_fwd_{g}", compiler_params=_params(("parallel",), 12 * STEP_ROWS * DIL_W, 2 << 20),
    )(*([qkv] * (2 if needs_halo else 1)))


def _dil_combine(outs, lses):
    def body(o0, o1, o2, l0, l1, l2, out_ref, lse_ref, so1, so2, sl1, sl2):
        for (_, d), src, dst in ((DIL_GROUPS[1], o1, so1), (DIL_GROUPS[2], o2, so2),
                                 (DIL_GROUPS[1], l1, sl1), (DIL_GROUPS[2], l2, sl2)):
            rows = S // d
            for r in range(d):
                dst[pl.ds(r, rows, stride=d), :] = src[r * rows:(r + 1) * rows, :]
        a, b, c = l0[...], sl1[...], sl2[...]
        m = jnp.maximum(jnp.maximum(a, b), c)
        ea, eb, ec = jnp.exp(a - m), jnp.exp(b - m), jnp.exp(c - m)
        z = ea + eb + ec
        inv = 1.0 / z
        out_ref[...] = (ea * inv) * o0[...] + (eb * inv) * so1[...] + (ec * inv) * so2[...]
        lse_ref[...] = m + jnp.log(z)

    blk = pl.BlockSpec((S, 128), lambda c: (0, c))
    return pl.pallas_call(
        body, grid=(DIL_W // 128,), in_specs=[blk] * 6, out_specs=[blk] * 2,
        out_shape=[SDS((S, DIL_W), f32)] * 2, scratch_shapes=[pltpu.VMEM((S, 128), f32)] * 4, name="dil_combine",
        compiler_params=_params(("parallel",), 32 * S * 128, 32 * S * 128))(*outs, *lses)


def _dil_probs(q, k, mask, lse, do, v, delta):
    s = jnp.where(mask, _nt(q, k) * SCALE, NEG)
    p = jnp.exp(s - lse)
    ds = p * (_nt(do, v) - delta) * SCALE
    return p, ds


def _dil_bwd(g, qkv, d_out, delta, lse):
    in_specs, needs_halo = _dil_in_specs(g, 3)

    def body(*refs):
        qkv_ref, halo_ref = refs[0], refs[1] if needs_halo else None
        do_ref, dl_ref, lse_ref, dq_ref, dk_ref, dv_ref = refs[-6:]
        step = pl.program_id(0)

        @pl.when(step == 0)
        def _():
            dk_ref[...] = jnp.zeros_like(dk_ref)
            dv_ref[...] = jnp.zeros_like(dv_ref)

        first = pl.multiple_of(step * STEP_ROWS, STEP_ROWS)
        cur = lax.broadcasted_iota(jnp.int32, (BLK, BLK), 1) <= lax.broadcasted_iota(jnp.int32, (BLK, BLK), 0)
        for b, h, rows, q, kc, vc, kp, vp, prev in _dil_blocks(g, qkv_ref, halo_ref):
            hs = slice(h * HD, (h + 1) * HD)
            do = do_ref[rows, hs].astype(bf16)
            lse, delta = lse_ref[rows, h * HD:h * HD + 1], dl_ref[rows, h * HD:h * HD + 1]
            own = pl.ds(pl.multiple_of(first + b * BLK, BLK), BLK)
            p, ds = _dil_probs(q, kc, cur, lse, do, vc, delta)
            dq = jnp.dot(ds.astype(bf16), kc, preferred_element_type=f32)
            dk_ref[own, hs] += _tn(ds.astype(bf16), q)
            dv_ref[own, hs] += _tn(p.astype(bf16), do)
            if kp is not None:
                before = pl.ds(pl.multiple_of(jnp.maximum(first + (b - 1) * BLK, 0), BLK), BLK)
                p, ds = _dil_probs(q, kp, prev, lse, do, vp, delta)
                dq += jnp.dot(ds.astype(bf16), kp, preferred_element_type=f32)
                dk_ref[before, hs] += _tn(ds.astype(bf16), q)
                dv_ref[before, hs] += _tn(p.astype(bf16), do)
            dq_ref[rows, hs] = dq

    whole = pl.BlockSpec((S, DIL_W), lambda i: (0, 0))
    return pl.pallas_call(
        body, grid=(S // STEP_ROWS,), in_specs=in_specs,
        out_specs=[pl.BlockSpec((STEP_ROWS, DIL_W), lambda i: (i, 0)), whole, whole],
        out_shape=[SDS((S, DIL_W), f32)] * 3, name=f"dil_bwd_{g}",
        compiler_params=_params(("arbitrary",), 20 * STEP_ROWS * DIL_W + 8 * S * DIL_W, 2 << 20),
    )(*([qkv] * (2 if needs_halo else 1)), d_out, delta, lse)


def _scan_rows(x, reverse):
    row = lax.broadcasted_iota(jnp.int32, x.shape, 0)
    k = 1
    while k < S:
        if reverse:
            x = x + jnp.where(row < S - k, pltpu.roll(x, S - k, 0), 0.0)
        else:
            x = x + jnp.where(row >= k, pltpu.roll(x, k, 0), 0.0)
        k *= 2
    return x


N_PAIR = N_FOX // 2
_PAIR_Q = pl.BlockSpec((None, S, 128), lambda p: (p, 0, 0))
_PAIR_K = pl.BlockSpec((None, 8, S), lambda p: (p, 0, 0))


def _forget_fwd(fz, b128):
    def body(z_ref, b_ref, fq_ref, fk_ref):
        z = z_ref[...] + b_ref[...]
        logf = jnp.minimum(z, 0.0) - jnp.log1p(jnp.exp(-jnp.abs(z)))
        f_cum = _scan_rows(logf, reverse=False)
        f_cum_t = f_cum.T
        fq_ref[...] = jnp.zeros_like(fq_ref)
        fk_ref[...] = jnp.zeros_like(fk_ref)
        for p in range(N_PAIR):
            fq_ref[p, :, 0:2] = f_cum[:, 2 * p:2 * p + 2]
            fk_ref[p, 0:2, :] = f_cum_t[2 * p:2 * p + 2, :]

    return pl.pallas_call(
        body, grid=(1,), in_specs=[pl.BlockSpec((S, 128), lambda i: (0, 0)), _vec(128)],
        out_specs=[pl.BlockSpec((N_PAIR, S, 128), lambda i: (0, 0, 0)), pl.BlockSpec((N_PAIR, 8, S), lambda i: (0, 0, 0))],
        out_shape=[SDS((N_PAIR, S, 128), f32), SDS((N_PAIR, 8, S), f32)], name="forget_fwd",
        compiler_params=_params(("arbitrary",), 24 * S * 128, 24 * S * 128))(fz, b128)


def _forget_bwd(fz, b128, d_f_cols, d_f_rows):
    def body(z_ref, b_ref, dfc_ref, dfr_ref, dz_ref, db_ref, df_sc):
        z = z_ref[...] + b_ref[...]
        df_sc[...] = jnp.zeros_like(df_sc)
        for p in range(N_PAIR):
            df_sc[:, 2 * p:2 * p + 2] = dfr_ref[p, :, 0:2] + dfc_ref[p].T[:, 0:2]
        dz = _scan_rows(df_sc[...], reverse=True) * jax.nn.sigmoid(-z)
        dz_ref[...] = dz
        db_ref[...] = jnp.sum(dz, axis=0, keepdims=True)

    full = pl.BlockSpec((S, 128), lambda i: (0, 0))
    return pl.pallas_call(
        body, grid=(1,),
        in_specs=[full, _vec(128), pl.BlockSpec((N_PAIR, 8, S), lambda i: (0, 0, 0)), pl.BlockSpec((N_PAIR, S, 128), lambda i: (0, 0, 0))],
        out_specs=[full, _vec(128)], out_shape=[SDS((S, 128), f32), SDS((1, 128), f32)],
        scratch_shapes=[pltpu.VMEM((S, 128), f32)], name="forget_bwd",
        compiler_params=_params(("arbitrary",), 32 * S * 128, 24 * S * 128))(fz, b128, d_f_cols, d_f_rows)


def _fox_scores(q_ref, k_ref, fq_ref, fk_ref, qi, hh):
    n = (qi + 1) * TQ
    rows, hs = slice(qi * TQ, n), slice(hh * HD, (hh + 1) * HD)
    s = _nt(q_ref[rows, hs], k_ref[0:n, hs]) * SCALE + (fq_ref[rows, hh:hh + 1] - fk_ref[hh:hh + 1, 0:n])
    qpos = qi * TQ + lax.broadcasted_iota(jnp.int32, (TQ, n), 0)
    kpos = lax.broadcasted_iota(jnp.int32, (TQ, n), 1)
    return jnp.where(kpos <= qpos, s, NEG)


def _pair_cols(first):
    return pl.BlockSpec((S, 128), lambda p: (0, first + p))


def _fox_fwd(vr, fq, fk):
    def body(q_ref, k_ref, v_ref, fq_ref, fk_ref, o_ref, lse_ref):
        lse_ref[...] = jnp.zeros_like(lse_ref)
        for hh in range(2):
            hs = slice(hh * HD, (hh + 1) * HD)
            for qi in range(S // TQ):
                n = (qi + 1) * TQ
                rows = slice(qi * TQ, n)
                s = _fox_scores(q_ref, k_ref, fq_ref, fk_ref, qi, hh)
                m = jnp.max(s, axis=-1, keepdims=True)
                p = jnp.exp(s - m)
                den = jnp.sum(p, axis=-1, keepdims=True)
                o_ref[rows, hs] = jnp.dot((p * (1.0 / den)).astype(bf16), v_ref[0:n, hs], preferred_element_type=f32)
                lse_ref[rows, hh:hh + 1] = m + jnp.log(den)

    return pl.pallas_call(
        body, grid=(N_PAIR,), in_specs=[_pair_cols(0), _pair_cols(N_PAIR), _pair_cols(2 * N_PAIR), _PAIR_Q, _PAIR_K],
        out_specs=[_pair_cols(0), _PAIR_Q], out_shape=[SDS((S, FOX_W), f32), SDS((N_PAIR, S, 128), f32)],
        name="fox_fwd", compiler_params=_params(("parallel",), 12 * S * 128, 16 * TQ * S),
    )(vr, vr, vr, fq, fk)


def _fox_bwd(vr, fq, fk, lse, d_out, delta):
    def body(q_ref, k_ref, v_ref, do_ref, fq_ref, fk_ref, lse_ref, dl_ref, dq_ref, dk_ref, dv_ref, dfc_ref, dfr_ref,
             dk_sc, dv_sc):
        dfc_ref[...] = jnp.zeros_like(dfc_ref)
        dfr_ref[...] = jnp.zeros_like(dfr_ref)
        for hh in range(2):
            hs = slice(hh * HD, (hh + 1) * HD)
            dk_sc[...] = jnp.zeros_like(dk_sc)
            dv_sc[...] = jnp.zeros_like(dv_sc)
            for qi in range(S // TQ):
                n = (qi + 1) * TQ
                rows = slice(qi * TQ, n)
                q, do, k, v = q_ref[rows, hs], do_ref[rows, hs], k_ref[0:n, hs], v_ref[0:n, hs]
                p = jnp.exp(_fox_scores(q_ref, k_ref, fq_ref, fk_ref, qi, hh) - lse_ref[rows, hh:hh + 1])
                ds = p * (_nt(do, v) - dl_ref[rows, hh:hh + 1])
                dsb = ds.astype(bf16)
                dq_ref[rows, hs] = jnp.dot(dsb, k, preferred_element_type=f32) * SCALE
                dk_sc[0:n, :] += _tn(dsb, q) * SCALE
                dv_sc[0:n, :] += _tn(p.astype(bf16), do)
                dfc_ref[hh:hh + 1, 0:n] -= jnp.sum(ds, axis=0, keepdims=True)
                dfr_ref[rows, hh:hh + 1] = jnp.sum(ds, axis=-1, keepdims=True)
            dk_ref[:, hs] = dk_sc[...]
            dv_ref[:, hs] = dv_sc[...]

    cols = [_pair_cols(k * N_PAIR) for k in range(3)]
    return pl.pallas_call(
        body, grid=(N_PAIR,), in_specs=cols + [_pair_cols(0), _PAIR_Q, _PAIR_K, _PAIR_Q, _PAIR_Q],
        out_specs=[_pair_cols(0)] * 3 + [_PAIR_K, _PAIR_Q],
        out_shape=[SDS((S, FOX_W), f32)] * 3 + [SDS((N_PAIR, 8, S), f32), SDS((N_PAIR, S, 128), f32)],
        scratch_shapes=[pltpu.VMEM((S, HD), f32)] * 2, name="fox_bwd",
        compiler_params=_params(("parallel",), 32 * S * 128, 24 * TQ * S),
    )(vr, vr, vr, d_out, fq, fk, lse, delta)


def _merge_fwd(out_a, out_b, w_a, w_b, gf):
    cw = D // N_SHARD

    def body(oa_ref, ob_ref, wa_ref, wb_ref, ga_ref, gb_ref, ya_ref, yb_ref, mg_ref):
        oa, ob = oa_ref[...].astype(bf16), ob_ref[...].astype(bf16)
        for j in range(N_SHARD):
            cols = slice(j * cw, (j + 1) * cw)
            ya = jnp.dot(oa, wa_ref[j], preferred_element_type=f32)
            yb = jnp.dot(ob, wb_ref[j], preferred_element_type=f32)
            ya_ref[:, cols] = ya
            yb_ref[:, cols] = yb
            mg_ref[:, cols] = (jax.nn.sigmoid(ga_ref[:, cols]) * ya + jax.nn.sigmoid(gb_ref[:, cols]) * yb).astype(bf16)

    full = lambda a: pl.BlockSpec(a.shape, lambda i: (0, 0, 0))
    return pl.pallas_call(
        body, grid=(S // TM,),
        in_specs=[_row(DIL_W), _row(FOX_W), full(w_a), full(w_b), _row(D), pl.BlockSpec((TM, D), lambda i: (i, 1))],
        out_specs=[_row(D)] * 3, out_shape=[SDS((S, D), f32), SDS((S, D), f32), SDS((S, D), bf16)], name="merge_fwd",
        compiler_params=_params(("parallel",), 22 * TM * D + 2 * (DIL_W + FOX_W) * D, 16 * TM * D),
    )(out_a, out_b, w_a, w_b, gf, gf)


def _merge_bwd(d_merged, ya, yb, gf):
    def body(dm_ref, ya_ref, yb_ref, ga_ref, gb_ref, dya_ref, dyb_ref, dg_ref):
        dm = dm_ref[...]
        sa, sb = jax.nn.sigmoid(ga_ref[...]), jax.nn.sigmoid(gb_ref[...])
        dya_ref[...] = (dm * sa).astype(bf16)
        dyb_ref[...] = (dm * sb).astype(bf16)
        dg_ref[:, :D] = (dm * ya_ref[...] * sa * (1.0 - sa)).astype(bf16)
        dg_ref[:, D:] = (dm * yb_ref[...] * sb * (1.0 - sb)).astype(bf16)

    return pl.pallas_call(
        body, grid=(S // TM,),
        in_specs=[_row(D)] * 4 + [pl.BlockSpec((TM, D), lambda i: (i, 1))],
        out_specs=[_row(D), _row(D), _row(2 * D)],
        out_shape=[SDS((S, D), bf16), SDS((S, D), bf16), SDS((S, 2 * D), bf16)], name="merge_bwd",
        compiler_params=_params(("parallel",), 28 * TM * D, 24 * TM * D))(d_merged, ya, yb, gf, gf)


def _branch_bwd(d_ya, d_yb, w_a, w_b, out_a, out_b):
    cw = D // N_SHARD

    def body(dya_ref, dyb_ref, wa_ref, wb_ref, oa_ref, ob_ref, doa_ref, dla_ref, dob_ref, dlb_ref):
        doa = jnp.zeros((TM, DIL_W), f32)
        dob = jnp.zeros((TM, FOX_W), f32)
        for j in range(N_SHARD):
            cols = slice(j * cw, (j + 1) * cw)
            doa += _nt(dya_ref[:, cols], wa_ref[j])
            dob += _nt(dyb_ref[:, cols], wb_ref[j])
        doa_ref[...] = doa
        dob_ref[...] = dob.astype(bf16)
        prod_a = doa * oa_ref[...]
        for h in range(SLOTS):
            hs = slice(h * HD, (h + 1) * HD)
            dla_ref[:, hs] = jnp.broadcast_to(jnp.sum(prod_a[:, hs], axis=-1, keepdims=True), (TM, HD))
        prod_b = dob * ob_ref[...]
        dlb_ref[...] = jnp.zeros_like(dlb_ref)
        for h in range(N_FOX):
            dlb_ref[h // 2, :, h % 2:h % 2 + 1] = jnp.sum(prod_b[:, h * HD:(h + 1) * HD], axis=-1, keepdims=True)

    full = lambda a: pl.BlockSpec(a.shape, lambda i: (0, 0, 0))
    return pl.pallas_call(
        body, grid=(S // TM,),
        in_specs=[_row(D), _row(D), full(w_a), full(w_b), _row(DIL_W), _row(FOX_W)],
        out_specs=[_row(DIL_W), _row(DIL_W), _row(FOX_W), pl.BlockSpec((N_PAIR, TM, 128), lambda i: (0, i, 0))],
        out_shape=[SDS((S, DIL_W), f32), SDS((S, DIL_W), f32), SDS((S, FOX_W), bf16), SDS((N_PAIR, S, 128), f32)],
        name="branch_bwd", compiler_params=_params(("parallel",), 8 * TM * D + 2 * (DIL_W + FOX_W) * D, 8 * TM * D),
    )(d_ya, d_yb, w_a, w_b, out_a, out_b)


def _branch_grads(out_a, out_b, d_ya, d_yb):
    cw = D // N_SHARD

    def body(oa_ref, ob_ref, dya_ref, dyb_ref, ga_ref, gb_ref):
        ga_ref[...] = _tn(oa_ref[...].astype(bf16), dya_ref[...])
        gb_ref[...] = _tn(ob_ref[...].astype(bf16), dyb_ref[...])

    whole = lambda w: pl.BlockSpec((S, w), lambda j: (0, 0))
    cols = pl.BlockSpec((S, cw), lambda j: (0, j))
    return pl.pallas_call(
        body, grid=(N_SHARD,), in_specs=[whole(DIL_W), whole(FOX_W), cols, cols],
        out_specs=[pl.BlockSpec((None, DIL_W, cw), lambda j: (j, 0, 0)), pl.BlockSpec((None, FOX_W, cw), lambda j: (j, 0, 0))],
        out_shape=[SDS((N_SHARD, DIL_W, cw), f32), SDS((N_SHARD, FOX_W, cw), f32)], name="grad_w_proj_ab",
        compiler_params=_params(("parallel",), 4 * S * (DIL_W + FOX_W) + 4 * S * cw + 4 * (DIL_W + FOX_W) * cw,
                                4 * S * (DIL_W + FOX_W)))(out_a, out_b, d_ya, d_yb)


FF_TN = F_FF // 2
FF_TM = 512


def _ffn_fwd(h, w_gate_t, w_up_t):
    def body(h_ref, wg_ref, wu_ref, g_ref, u_ref, a_ref):
        hb = h_ref[...]
        g = _nt(hb, wg_ref[...])
        u = _nt(hb, wu_ref[...])
        g_ref[...] = g
        u_ref[...] = u
        a_ref[...] = (g * jax.nn.sigmoid(g) * u).astype(bf16)

    tile = pl.BlockSpec((FF_TM, FF_TN), lambda j, i: (i, j))
    wspec = pl.BlockSpec((FF_TN, D), lambda j, i: (j, 0))
    return pl.pallas_call(
        body, grid=(F_FF // FF_TN, S // FF_TM),
        in_specs=[pl.BlockSpec((FF_TM, D), lambda j, i: (i, 0)), wspec, wspec], out_specs=[tile] * 3,
        out_shape=[SDS((S, F_FF), f32), SDS((S, F_FF), f32), SDS((S, F_FF), bf16)], name="ffn_fwd",
        compiler_params=_params(("parallel", "parallel"), 2 * FF_TM * D + 4 * D * FF_TN + 10 * FF_TM * FF_TN, 16 * FF_TM * FF_TN),
    )(h, w_gate_t, w_up_t)


def _ffn_bwd_act(d_ff, w_down, g_act, u_act):
    def body(d_ref, wd_ref, g_ref, u_ref, dg_ref, du_ref):
        da = _nt(d_ref[...], wd_ref[...])
        g = g_ref[...]
        sg = jax.nn.sigmoid(g)
        du_ref[...] = (da * g * sg).astype(bf16)
        dg_ref[...] = (da * u_ref[...] * sg * (1.0 + g * (1.0 - sg))).astype(bf16)

    tile = pl.BlockSpec((FF_TM, FF_TN), lambda j, i: (i, j))
    return pl.pallas_call(
        body, grid=(F_FF // FF_TN, S // FF_TM),
        in_specs=[pl.BlockSpec((FF_TM, D), lambda j, i: (i, 0)), pl.BlockSpec((FF_TN, D), lambda j, i: (j, 0)), tile, tile],
        out_specs=[tile, tile], out_shape=[SDS((S, F_FF), bf16)] * 2, name="ffn_bwd_act",
        compiler_params=_params(("parallel", "parallel"), 2 * FF_TM * D + 2 * D * FF_TN + 12 * FF_TM * FF_TN, 16 * FF_TM * FF_TN),
    )(d_ff, w_down, g_act, u_act)


def _row_tile(rows):
    return next(t for t in (376, 128, 176, 64, 32, 16, 8) if rows % t == 0)


def _adamw_math(w, g, m, v):
    c1 = 1.0 - ADAM_B1 ** ADAM_STEP
    c2 = 1.0 - ADAM_B2 ** ADAM_STEP
    m_new = ADAM_B1 * m + (1.0 - ADAM_B1) * g
    v_new = ADAM_B2 * v + (1.0 - ADAM_B2) * (g * g)
    return -ADAM_LR * ((m_new / c1) / (jnp.sqrt(v_new / c2) + ADAM_EPS) + ADAM_WD * w), m_new, v_new


def _adamw(w, g, m, v, name):
    rows, cols = w.shape
    tm = _row_tile(rows)

    def body(w_ref, g_ref, m_ref, v_ref, d_ref, nm_ref, nv_ref):
        d_ref[...], nm_ref[...], nv_ref[...] = _adamw_math(w_ref[...], g_ref[...], m_ref[...], v_ref[...])

    spec = pl.BlockSpec((tm, cols), lambda i: (i, 0))
    return pl.pallas_call(
        body, grid=(rows // tm,), in_specs=[spec] * 4, out_specs=[spec] * 3, out_shape=[SDS(w.shape, f32)] * 3,
        name=name, compiler_params=_params(("parallel",), 28 * tm * cols, 16 * tm * cols))(w, g, m, v)


def _adamw_halves(w, g_mine, g_theirs, m, v, name):
    rows, cols = w.shape
    tm = _row_tile(rows // 2)
    per_half = rows // 2 // tm
    core = lax.axis_index("c").astype(jnp.int32).reshape(1)

    def body(c_ref, w_ref, gm_ref, gt_ref, m_ref, v_ref, g_ref, d_ref, nm_ref, nv_ref):
        mine = pl.program_id(0) // per_half == c_ref[0]
        g = jnp.where(mine, gm_ref[...], gt_ref[...])
        g_ref[...] = g
        d_ref[...], nm_ref[...], nv_ref[...] = _adamw_math(w_ref[...], g, m_ref[...], v_ref[...])

    spec = pl.BlockSpec((tm, cols), lambda i, c_ref: (i, 0))
    in_half = lambda i, first: jnp.clip(i - first * per_half, 0, per_half - 1)
    grid_spec = pltpu.PrefetchScalarGridSpec(
        num_scalar_prefetch=1, grid=(rows // tm,),
        in_specs=[spec, pl.BlockSpec((tm, cols), lambda i, c_ref: (in_half(i, c_ref[0]), 0)),
                  pl.BlockSpec((tm, cols), lambda i, c_ref: (in_half(i, 1 - c_ref[0]), 0)), spec, spec],
        out_specs=[spec] * 4)
    return pl.pallas_call(
        body, grid_spec=grid_spec, out_shape=[SDS(w.shape, f32)] * 4, name=name,
        compiler_params=_params(("parallel",), 36 * tm * cols, 16 * tm * cols))(core, w, g_mine, g_theirs, m, v)


_ANY = pl.BlockSpec(memory_space=pl.ANY)


def _place():
    x, y, c = lax.axis_index("x"), lax.axis_index("y"), lax.axis_index("c")
    chips = [(1 - x, y), (x, 1 - y), (1 - x, 1 - y)]
    return x, y, c, chips


def _halved(t):
    return t.reshape(t.shape[:-2] + (2, t.shape[-2] // 2, t.shape[-1]))


def _gather_body(src, out, send_ici, recv_ici, send_d2d, recv_d2d):
    x, y, c, chips = _place()
    sibling = (x, y, 1 - c)
    me_j = 2 * x + y
    sends = []
    for a in range(len(src)):
        for p in range(3):
            cp = pltpu.make_async_remote_copy(
                src_ref=src[a].at[c], dst_ref=out[a].at[me_j, c], send_sem=send_ici.at[a, p],
                recv_sem=recv_ici.at[a, p], device_id=(*chips[p], c), device_id_type=MESH)
            cp.start()
            sends.append(cp)
    for a in range(len(src)):
        for p, (px, py) in enumerate(chips):
            blk = out[a].at[2 * px + py, c]
            pltpu.make_async_remote_copy(
                src_ref=blk, dst_ref=blk, send_sem=send_ici.at[a, p], recv_sem=recv_ici.at[a, p],
                device_id=sibling, device_id_type=MESH).wait_recv()
            fw = pltpu.make_async_remote_copy(
                src_ref=blk, dst_ref=blk, send_sem=send_d2d.at[a, p], recv_sem=recv_d2d.at[a, p],
                device_id=sibling, device_id_type=MESH)
            fw.start()
            sends.append(fw)
    for a in range(len(src)):
        for p, (px, py) in enumerate(chips):
            blk = out[a].at[2 * px + py, 1 - c]
            pltpu.make_async_remote_copy(
                src_ref=blk, dst_ref=blk, send_sem=send_d2d.at[a, p], recv_sem=recv_d2d.at[a, p],
                device_id=sibling, device_id_type=MESH).wait_recv()
    for cp in sends:
        cp.wait_send()


def _handshake(peers):
    barrier = pltpu.get_barrier_semaphore()
    for peer in peers:
        pl.semaphore_signal(barrier, inc=1, device_id=peer, device_id_type=MESH)
    pl.semaphore_wait(barrier, len(peers))


_SEQUENCER = dict(axis_name="sequencer", num_cores=1)
GATHER_LATE_ID, SCATTER_EARLY_ID, SWAP_EARLY_ID, GATHER_FIRST_ID, SCATTER_LATE_ID = 1, 2, 3, 4, 5


def _all_gather_async(shards, after, name, collective_id):
    n, k = len(shards), len(after)

    def body(*refs):
        x, y, c, chips = _place()
        _handshake([(*chip, c) for chip in chips] + [(x, y, 1 - c)])
        _gather_body(refs[:n], refs[n + k:2 * n + k], *refs[2 * n + k:])

    return pl.kernel(
        body, out_type=[SDS((N_SHARD,) + t.shape, t.dtype) for t in shards],
        mesh=plsc.ScalarSubcoreMesh(**_SEQUENCER), scratch_types=[pltpu.SemaphoreType.DMA((n, 3))] * 4,
        compiler_params=pltpu.CompilerParams(collective_id=collective_id), name=name)(*shards, *after)


def _pair_swap(grads):
    n = len(grads)

    def body(*refs):
        src, out, send_sems, recv_sems = refs[:n], refs[n:2 * n], refs[2 * n], refs[2 * n + 1]
        x, y, c, _ = _place()
        copies = [pltpu.make_async_remote_copy(
            src_ref=src[a].at[:, 1 - c], dst_ref=out[a], send_sem=send_sems.at[a], recv_sem=recv_sems.at[a],
            device_id=(x, y, 1 - c), device_id_type=MESH) for a in range(n)]
        for cp in copies:
            cp.start()
        for cp in copies:
            cp.wait()

    return pl.pallas_call(
        body, in_specs=[_ANY] * n, out_specs=[_ANY] * n,
        out_shape=[SDS((N_SHARD,) + t.shape[2:], t.dtype) for t in grads],
        scratch_shapes=[pltpu.SemaphoreType.DMA((n,)), pltpu.SemaphoreType.DMA((n,))], name="pair_swap",
        compiler_params=pltpu.CompilerParams(has_side_effects=True))(*grads)


def _pair_swap_early(grads):
    n = len(grads)

    def body(*refs):
        src, out, send_sems, recv_sems = refs[:n], refs[n:2 * n], refs[2 * n], refs[2 * n + 1]
        x, y, c, _ = _place()
        _handshake([(x, y, 1 - c)])
        copies = [pltpu.make_async_remote_copy(
            src_ref=src[a].at[:, 1 - c], dst_ref=out[a], send_sem=send_sems.at[a], recv_sem=recv_sems.at[a],
            device_id=(x, y, 1 - c), device_id_type=MESH) for a in range(n)]
        for cp in copies:
            cp.start()
        for cp in copies:
            cp.wait()

    return pl.kernel(
        body, out_type=[SDS((N_SHARD,) + t.shape[2:], t.dtype) for t in grads],
        mesh=plsc.ScalarSubcoreMesh(**_SEQUENCER), scratch_types=[pltpu.SemaphoreType.DMA((n,))] * 2,
        compiler_params=pltpu.CompilerParams(collective_id=SWAP_EARLY_ID), name="pair_swap_early")(*grads)


def _scatter_early(parts):
    n = len(parts)

    def body(*refs):
        part, recv, send_sems, recv_sems = refs[:n], refs[n:2 * n], refs[2 * n], refs[2 * n + 1]
        x, y, c, chips = _place()
        _handshake([(*chip, c) for chip in chips])
        me_j = 2 * x + y
        sends = []
        for a in range(n):
            for p, (px, py) in enumerate(chips):
                cp = pltpu.make_async_remote_copy(
                    src_ref=part[a].at[2 * px + py], dst_ref=recv[a].at[me_j], send_sem=send_sems.at[a, p],
                    recv_sem=recv_sems.at[a, p], device_id=(px, py, c), device_id_type=MESH)
                cp.start()
                sends.append(cp)
        for a in range(n):
            for p, (px, py) in enumerate(chips):
                slot = recv[a].at[2 * px + py]
                pltpu.make_async_remote_copy(
                    src_ref=slot, dst_ref=slot, send_sem=send_sems.at[a, p], recv_sem=recv_sems.at[a, p],
                    device_id=(px, py, c), device_id_type=MESH).wait_recv()
        for cp in sends:
            cp.wait_send()

    return pl.kernel(
        body, out_type=[SDS(t.shape, t.dtype) for t in parts],
        mesh=plsc.ScalarSubcoreMesh(**_SEQUENCER), scratch_types=[pltpu.SemaphoreType.DMA((n, 3))] * 2,
        compiler_params=pltpu.CompilerParams(collective_id=SCATTER_EARLY_ID), name="scatter_early")(*parts)


def _pair_sum(grads, other, name):
    _, _, rows, cols = grads.shape
    tr = _row_tile(rows)
    core = lax.axis_index("c").astype(jnp.int32).reshape(1)

    def body(c_ref, g_ref, o_ref, out_ref):
        out_ref[...] = (g_ref[...] + o_ref[...]).astype(bf16)

    grid_spec = pltpu.PrefetchScalarGridSpec(
        num_scalar_prefetch=1, grid=(N_SHARD, rows // tr),
        in_specs=[pl.BlockSpec((None, None, tr, cols), lambda j, i, c_ref: (j, c_ref[0], i, 0)),
                  pl.BlockSpec((None, tr, cols), lambda j, i, c_ref: (j, i, 0))],
        out_specs=pl.BlockSpec((None, tr, cols), lambda j, i, c_ref: (j, i, 0)))
    return pl.pallas_call(
        body, grid_spec=grid_spec, out_shape=SDS((N_SHARD, rows, cols), bf16), name=name,
        compiler_params=_params(("parallel", "parallel"), 10 * tr * cols, 12 * tr * cols))(core, grads, other)


def _scatter_partials(parts, small):
    n = len(parts)

    def body(*refs):
        part, small_ref, recv, small_all_ref = refs[:n], refs[n], refs[n + 1:2 * n + 1], refs[2 * n + 1]
        send_sems, recv_sems, ssend, srecv, local_sem = refs[2 * n + 2:]
        x, y, c, chips = _place()
        flip = lambda a, bit: 1 - a if bit else a
        peers = [(flip(x, k & 4), flip(y, k & 2), flip(c, k & 1)) for k in range(1, 8)]
        _handshake(peers)
        me_j = 2 * x + y
        me_dev = 4 * x + 2 * y + c
        own = pltpu.make_async_copy(small_ref, small_all_ref.at[me_dev], local_sem)
        own.start()
        sends = []
        for a in range(n):
            for p, (px, py) in enumerate(chips):
                cp = pltpu.make_async_remote_copy(
                    src_ref=part[a].at[2 * px + py], dst_ref=recv[a].at[me_j], send_sem=send_sems.at[a, p],
                    recv_sem=recv_sems.at[a, p], device_id=(px, py, c), device_id_type=MESH)
                cp.start()
                sends.append(cp)
        for k, to in enumerate(peers):
            cp = pltpu.make_async_remote_copy(
                src_ref=small_ref, dst_ref=small_all_ref.at[me_dev],
                send_sem=ssend.at[k], recv_sem=srecv.at[k], device_id=to, device_id_type=MESH)
            cp.start()
            sends.append(cp)
        for a in range(n):
            for p, (px, py) in enumerate(chips):
                slot = recv[a].at[2 * px + py]
                pltpu.make_async_remote_copy(
                    src_ref=slot, dst_ref=slot, send_sem=send_sems.at[a, p], recv_sem=recv_sems.at[a, p],
                    device_id=(px, py, c), device_id_type=MESH).wait_recv()
        for k, (px, py, pc) in enumerate(peers):
            slot = small_all_ref.at[4 * px + 2 * py + pc]
            pltpu.make_async_remote_copy(
                src_ref=slot, dst_ref=slot, send_sem=ssend.at[k], recv_sem=srecv.at[k],
                device_id=(px, py, pc), device_id_type=MESH).wait_recv()
        for cp in sends:
            cp.wait_send()
        own.wait()

    return pl.kernel(
        body, out_type=[SDS(t.shape, t.dtype) for t in parts] + [SDS((8, SMALL_ROWS, D), f32)],
        mesh=plsc.ScalarSubcoreMesh(**_SEQUENCER),
        scratch_types=[pltpu.SemaphoreType.DMA((n, 3)), pltpu.SemaphoreType.DMA((n, 3)),
                       pltpu.SemaphoreType.DMA((7,)), pltpu.SemaphoreType.DMA((7,)), pltpu.SemaphoreType.DMA],
        compiler_params=pltpu.CompilerParams(collective_id=SCATTER_LATE_ID), name="scatter_partials")(*parts, small)


def _sum_partials(part, recv, name):
    _, rows, cols = recv.shape
    tr = _row_tile(rows)
    me = (2 * lax.axis_index("x") + lax.axis_index("y")).astype(jnp.int32).reshape(1)

    def body(me_ref, mine, r0, r1, r2, r3, out_ref):
        acc = None
        for j, r in enumerate((r0, r1, r2, r3)):
            term = jnp.where(me_ref[0] == j, mine[...], r[...]).astype(f32)
            acc = term if acc is None else acc + term
        out_ref[...] = acc

    slot = lambda j: pl.BlockSpec((None, tr, cols), lambda i, me_ref: (jnp.where(me_ref[0] == j, j ^ 1, j), i, 0))
    grid_spec = pltpu.PrefetchScalarGridSpec(
        num_scalar_prefetch=1, grid=(rows // tr,),
        in_specs=[pl.BlockSpec((None, tr, cols), lambda i, me_ref: (me_ref[0], i, 0)), slot(0), slot(1), slot(2), slot(3)],
        out_specs=pl.BlockSpec((tr, cols), lambda i, me_ref: (i, 0)))
    return pl.pallas_call(
        body, grid_spec=grid_spec, out_shape=SDS((rows, cols), f32), name=name,
        compiler_params=_params(("parallel",), 14 * tr * cols, 12 * tr * cols))(me, part, recv, recv, recv, recv)


def _sum_small(small_all):
    def body(small_ref, out_ref):
        tot = small_ref[0]
        for k in range(1, 8):
            tot = tot + small_ref[k]
        out_ref[...] = tot

    return pl.pallas_call(
        body, grid=(1,), in_specs=[pl.BlockSpec((8, SMALL_ROWS, D), lambda i: (0, 0, 0))],
        out_specs=pl.BlockSpec((SMALL_ROWS, D), lambda i: (0, 0)), out_shape=SDS((SMALL_ROWS, D), f32),
        name="sum_small", compiler_params=_params(("arbitrary",), 36 * SMALL_ROWS * D))(small_all)


def _swap_halves(halves, name):
    n = len(halves)

    def body(*refs):
        src, out, send_sems, recv_sems = refs[:n], refs[n:2 * n], refs[2 * n], refs[2 * n + 1]
        x, y, c, _ = _place()
        copies = [pltpu.make_async_remote_copy(
            src_ref=src[a], dst_ref=out[a], send_sem=send_sems.at[a], recv_sem=recv_sems.at[a],
            device_id=(x, y, 1 - c), device_id_type=MESH) for a in range(n)]
        for cp in copies:
            cp.start()
        for cp in copies:
            cp.wait()

    return pl.pallas_call(
        body, in_specs=[_ANY] * n, out_specs=[_ANY] * n, out_shape=[SDS(t.shape, f32) for t in halves],
        scratch_shapes=[pltpu.SemaphoreType.DMA((n,))] * 2, name=name,
        compiler_params=pltpu.CompilerParams(has_side_effects=True))(*halves)


def _kernel_layout(name, t):
    t = t[0]
    if name in TRANSPOSED:
        t = jnp.swapaxes(t, 0, 1)
    return _pad_rows(t, SHARD_SHAPE[name][0])


def _harness_layout(name, t):
    if name == "w_in":
        t = t[:IN_SHARD]
    if name in TRANSPOSED:
        t = jnp.swapaxes(t, 0, 1)
    return t[None]


def _pad_rows(t, rows):
    return t if t.shape[0] == rows else jnp.pad(t, ((0, rows - t.shape[0]), (0, 0)))


_QA, _KA, _VA, _QB, _F, _GAB = 0, 768, 1536, 2304, 3840, 3848


def _full_weights(gathered):
    full = {n: t.reshape((N_SHARD,) + SHARD_SHAPE[n]) for n, t in gathered.items()}
    out = {}
    if "w_in" in full:
        w_in_t = full["w_in"][:, :IN_SHARD].reshape(IN_COLS, D)
        group = lambda g: jnp.concatenate([w_in_t[o + g * DIL_W:o + (g + 1) * DIL_W] for o in (_QA, _KA, _VA)], axis=0)
        out.update(
            w_a_t=[group(g) for g in range(3)],
            w_vr_t=w_in_t[_QB:_F],
            w_fox_t=[w_in_t[_QB + k * FOX_W:_QB + (k + 1) * FOX_W] for k in range(3)],
            w_f_t=jnp.concatenate([w_in_t[_F:_GAB], jnp.zeros((128 - N_FOX, D), bf16)], axis=0),
            w_gab_t=w_in_t[_GAB:])
    if "w_out" in full:
        out.update(
            w_a4=full["w_proj_a"],
            w_b4=full["w_proj_b"],
            w_out=full["w_out"].reshape(D, D),
            w_gate_t=full["w_ffn_gate"].reshape(F_FF, D),
            w_up_t=full["w_ffn_up"].reshape(F_FF, D),
            w_down=full["w_ffn_down"].reshape(F_FF, D))
    return out


def _sharded_grads(g):
    parts = [g["w_a_t"][k][o:o + DIL_W] for o in (0, DIL_W, 2 * DIL_W) for k in range(3)]
    parts += g["w_fox_t"] + [g["w_f_t"][:N_FOX], g["w_gab_t"]]
    w_in_t = jnp.concatenate(parts, axis=0).reshape(N_SHARD, IN_SHARD, D)
    full = dict(w_in=jnp.pad(w_in_t, ((0, 0), (0, IN_SHARD_PAD - IN_SHARD), (0, 0))), w_proj_a=g["w_a4"],
                w_proj_b=g["w_b4"], w_out=g["w_out"], w_ffn_gate=g["w_gate_t"], w_ffn_up=g["w_up_t"],
                w_ffn_down=g["w_down"])
    return {n: _halved(full[n].reshape((N_SHARD,) + SHARD_SHAPE[n])) for n in W_NAMES}


def _local_step(x, target, wt, b_forget, g_mix_pre, g_mix_post, g_ffn_pre, g_ffn_post, late=None):
    tables = _rope_tables()
    b128 = jnp.pad(b_forget, ((0, 0), (0, 128 - N_FOX)))
    dils = tuple(d for _, d in DIL_GROUPS[1:])

    hs = _norm_fwd([x] + list(_perm_rows([x], dils, "perm_x")), g_mix_pre)
    h1 = hs[0]
    if callable(wt):
        wt = wt(h1)
    qkv = [_rope_fwd(g, _mm([(hs[g], wt["w_a_t"][g])], "nt", f32, tm=1024, tn=QKV_W, name=f"proj_a_{g}"), tables)
           for g in range(3)]
    vr = _mm([(h1, wt["w_vr_t"])], "nt", bf16, tm=1024, tn=VR_W // 2, name="proj_vr")
    gab = _mm([(h1, wt["w_gab_t"])], "nt", f32, tm=512, tn=2 * D, name="proj_gab")
    fz = _mm([(h1, wt["w_f_t"])], "nt", f32, tm=1024, tn=128, name="proj_f")
    dil = [_dil_fwd(g, qkv[g]) for g in range(3)]
    out_a, lse_a = _dil_combine([o for o, _ in dil], [l for _, l in dil])
    f_q, f_k = _forget_fwd(fz, b128)
    out_b, lse_b = _fox_fwd(vr, f_q, f_k)
    if late is not None:
        wt = {**wt, **late(out_b)}
    ya, yb, merged = _merge_fwd(out_a, out_b, wt["w_a4"], wt["w_b4"], gab)
    mix = _mm([(merged, wt["w_out"])], "nn", f32, tm=1024, tn=D, name="proj_out")
    x2, h3 = _resid_norm_fwd(x, mix, g_mix_post, g_ffn_pre)
    g_act, u_act, a_act = _ffn_fwd(h3, wt["w_gate_t"], wt["w_up_t"])
    ff = _mm([(a_act, wt["w_down"])], "nn", f32, tm=1024, tn=D, name="ffn_down")
    sq_err, dy, d_ff, dg_ffn_post = _loss_head(x2, ff, g_ffn_post, target)

    grads = {}
    d_g, d_u = _ffn_bwd_act(d_ff, wt["w_down"], g_act, u_act)
    grads["w_down"] = _mm([(a_act, d_ff)], "tn", f32, tm=FF_TN, tn=512, name="grad_w_down")
    grads["w_gate_t"] = _mm([(d_g, h3)], "tn", f32, tm=FF_TN, tn=512, name="grad_w_gate")
    grads["w_up_t"] = _mm([(d_u, h3)], "tn", f32, tm=FF_TN, tn=512, name="grad_w_up")
    d_h3 = _mm([(d_g, wt["w_gate_t"]), (d_u, wt["w_up_t"])], "nn", f32, tm=512, tn=512, name="ffn_bwd_in")
    dx2, d_mix, dg_ffn_pre, dg_mix_post = _norm_bwd_mid(dy, d_h3, x2, mix, g_ffn_pre, g_mix_post)

    grads["w_out"] = _mm([(merged, d_mix)], "tn", f32, tm=D, tn=D, name="grad_w_out")
    d_merged = _mm([(d_mix, wt["w_out"])], "nt", f32, tm=1024, tn=D, name="proj_out_bwd")
    d_ya, d_yb, d_gab = _merge_bwd(d_merged, ya, yb, gab)
    grads["w_a4"], grads["w_b4"] = _branch_grads(out_a, out_b, d_ya, d_yb)
    d_out_a, delta_a, d_out_b, delta_b = _branch_bwd(d_ya, d_yb, wt["w_a4"], wt["w_b4"], out_a, out_b)

    perm = _perm_rows([d_out_a, delta_a, lse_a], dils, "perm_dil_bwd")
    aux = [(d_out_a, delta_a, lse_a)] + [tuple(perm[k * len(dils) + i] for k in range(3)) for i in range(len(dils))]
    d_qkv = []
    for g in range(3):
        dq, dk, dv = _dil_bwd(g, qkv[g], *aux[g])
        d_qkv.append(_rope_bwd(g, dq, dk, dv, tables))
    *d_fox, d_f_cols, d_f_rows = _fox_bwd(vr, f_q, f_k, lse_b, d_out_b, delta_b)
    d_z, d_b128 = _forget_bwd(fz, b128, d_f_cols, d_f_rows)

    grads["w_a_t"] = [_mm([(d_qkv[g], hs[g])], "tn", f32, tm=QKV_W, tn=D, name=f"grad_w_a_{g}") for g in range(3)]
    grads["w_fox_t"] = [_mm([(d_fox[k], h1)], "tn", f32, tm=FOX_W, tn=D, name=f"grad_w_fox_{k}") for k in range(3)]
    grads["w_gab_t"] = _mm([(d_gab, h1)], "tn", f32, tm=D, tn=D, name="grad_w_gab")
    grads["w_f_t"] = _mm([(d_z, h1)], "tn", f32, tm=128, tn=D, name="grad_w_f")
    d_h1_nat = _mm([(d_qkv[0], wt["w_a_t"][0])] + list(zip(d_fox, wt["w_fox_t"]))
                   + [(d_gab, wt["w_gab_t"]), (d_z, wt["w_f_t"])], "nn", f32, tm=512, tn=512, name="proj_in_bwd")
    d_h1_dil = [_mm([(d_qkv[g], wt["w_a_t"][g])], "nn", f32, tm=1024, tn=D, name=f"proj_a_bwd_{g}") for g in (1, 2)]
    d_h1 = _unperm_sum(d_h1_nat, d_h1_dil, dils, "unperm_d_h1")
    grad_x, dg_mix_pre = _norm_bwd_in(dx2, d_h1, x, g_mix_pre)

    small = dict(b_forget=d_b128[:, :N_FOX], norm_mix_pre=dg_mix_pre, norm_mix_post=dg_mix_post,
                 norm_ffn_pre=dg_ffn_pre, norm_ffn_post=dg_ffn_post)
    grads["mid_backward"] = d_qkv[0]
    return sq_err, grad_x, grads, small


NORMS = ("norm_mix_pre", "norm_mix_post", "norm_ffn_pre", "norm_ffn_post")
ORDER = ("w_in", "w_proj_a", "w_proj_b", "w_out", "b_forget", "w_ffn_gate", "w_ffn_up", "w_ffn_down") + NORMS


def kernel(x, w_in, w_proj_a, w_proj_b, w_out, b_forget, w_ffn_gate, w_ffn_up, w_ffn_down, norm_mix_pre, norm_mix_post, norm_ffn_pre, norm_ffn_post, loss_target, m_w_in, m_w_proj_a, m_w_proj_b, m_w_out, m_b_forget, m_w_ffn_gate, m_w_ffn_up, m_w_ffn_down, m_norm_mix_pre, m_norm_mix_post, m_norm_ffn_pre, m_norm_ffn_post, v_w_in, v_w_proj_a, v_w_proj_b, v_w_out, v_b_forget, v_w_ffn_gate, v_w_ffn_up, v_w_ffn_down, v_norm_mix_pre, v_norm_mix_post, v_norm_ffn_pre, v_norm_ffn_post):
    given = dict(w_in=w_in, w_proj_a=w_proj_a, w_proj_b=w_proj_b, w_out=w_out, w_ffn_gate=w_ffn_gate,
                 w_ffn_up=w_ffn_up, w_ffn_down=w_ffn_down)
    given_m = dict(w_in=m_w_in, w_proj_a=m_w_proj_a, w_proj_b=m_w_proj_b, w_out=m_w_out, w_ffn_gate=m_w_ffn_gate,
                   w_ffn_up=m_w_ffn_up, w_ffn_down=m_w_ffn_down)
    given_v = dict(w_in=v_w_in, w_proj_a=v_w_proj_a, w_proj_b=v_w_proj_b, w_out=v_w_out, w_ffn_gate=v_w_ffn_gate,
                   w_ffn_up=v_w_ffn_up, w_ffn_down=v_w_ffn_down)
    w, m, v = ({n: _kernel_layout(n, t[n]) for n in W_NAMES} for t in (given, given_m, given_v))
    small_w = dict(b_forget=b_forget, norm_mix_pre=norm_mix_pre, norm_mix_post=norm_mix_post,
                   norm_ffn_pre=norm_ffn_pre, norm_ffn_post=norm_ffn_post)
    small_m = dict(b_forget=m_b_forget, norm_mix_pre=m_norm_mix_pre, norm_mix_post=m_norm_mix_post,
                   norm_ffn_pre=m_norm_ffn_pre, norm_ffn_post=m_norm_ffn_post)
    small_v = dict(b_forget=v_b_forget, norm_mix_pre=v_norm_mix_pre, norm_mix_post=v_norm_mix_post,
                   norm_ffn_pre=v_norm_ffn_pre, norm_ffn_post=v_norm_ffn_post)

    own = [_halved(w[n].astype(bf16)) for n in W_NAMES]
    chip = 2 * lax.axis_index("x") + lax.axis_index("y")
    exchanged = {"first": _all_gather_async(own[:1], [], "all_gather_first", GATHER_FIRST_ID)}
    fill = lambda ts, mine: [lax.dynamic_update_index_in_dim(t, o, chip, 0) for t, o in zip(ts, mine)]

    def first_weights(ready):
        arrived, _ = lax.optimization_barrier((list(exchanged["first"]), ready))
        exchanged["late"] = _all_gather_async(own[1:], [arrived[0][0, 0, :16, :128]], "all_gather_late", GATHER_LATE_ID)
        return _full_weights(dict(zip(W_NAMES[:1], fill(arrived, own[:1]))))

    def late_weights(ready):
        arrived, _ = lax.optimization_barrier((list(exchanged["late"]), ready))
        return _full_weights(dict(zip(W_NAMES[1:], fill(arrived, own[1:]))))

    sq_err, grad_x, grads, small = _local_step(x[0], loss_target[0], first_weights, b_forget, norm_mix_pre,
                                               norm_mix_post, norm_ffn_pre, norm_ffn_post, late=late_weights)

    g4 = _sharded_grads(grads)
    stack = lambda t, extra: jnp.concatenate(
        [jnp.pad(t["b_forget"], ((0, 0), (0, D - N_FOX)))] + [t[n] for n in NORMS]
        + [jnp.pad(extra, ((0, SMALL_ROWS - LOSS_ROW - 1), (0, D - extra.shape[1])), constant_values=1.0)], axis=0)
    early, _ = lax.optimization_barrier((list(_pair_swap_early([g4[n] for n in W_NAMES[1:]])), grads["mid_backward"]))
    other = list(_pair_swap([g4["w_in"]])) + early
    parts = [_pair_sum(g4[n], o, "pair_sum_" + n) for n, o in zip(W_NAMES, other)]
    recv_early = _scatter_early(parts[1:])
    recv_in, small_all = _scatter_partials(parts[:1], stack(small, sq_err))

    g_shard, delta, new_m, new_v = {}, {}, {}, {}

    def finish(names, parts, recv):
        halves = [_sum_partials(p, r, "sum_partials_" + n) for n, p, r in zip(names, parts, recv)]
        theirs = _swap_halves(halves, "swap_halves_" + names[0])
        for n, mine, other_half in zip(names, halves, theirs):
            g_shard[n], delta[n], new_m[n], new_v[n] = _adamw_halves(w[n], mine, other_half, m[n], v[n], "adamw_" + n)

    recv_early, _ = lax.optimization_barrier((list(recv_early), parts[0]))
    finish(W_NAMES[1:], parts[1:], recv_early)
    (recv_in, small_all), _ = lax.optimization_barrier(((recv_in, small_all), [delta[n] for n in W_NAMES[1:]]))
    finish(W_NAMES[:1], parts[:1], [recv_in])
    small_sum = _sum_small(small_all)
    loss = small_sum[LOSS_ROW, 0] * (0.5 / D)
    ones = jnp.ones((1, 128), f32)
    sd, sm, sv = _adamw(stack(small_w, ones), small_sum, stack(small_m, ones), stack(small_v, ones), "adamw_small")

    outs = [loss, grad_x[None]]
    for big, st in ((g_shard, small_sum), (delta, sd), (new_m, sm), (new_v, sv)):
        t = {n: _harness_layout(n, big[n]) for n in W_NAMES}
        t["b_forget"] = st[0:1, :N_FOX]
        for i, n in enumerate(NORMS):
            t[n] = st[i + 1:i + 2]
        outs += [t[n] for n in ORDER]
    return tuple(outs)
```

```python
import functools
import math

import jax
import jax.numpy as jnp
import numpy as np
from jax import lax
from jax.experimental import pallas as pl
from jax.experimental.pallas import tpu as pltpu
from jax.experimental.pallas import tpu_sc as plsc

f32 = jnp.float32
bf16 = jnp.bfloat16
SDS = jax.ShapeDtypeStruct
MESH = pl.DeviceIdType.MESH

S = 2048
D = 1024
HD = 64
BLK = 128
N_FOX = 8
FOX_W = N_FOX * HD
DIL_GROUPS = ((128, 1), (512, 4), (2048, 16))
SLOTS = 4
DIL_W = SLOTS * HD
QKV_W = 3 * DIL_W
VR_W = 3 * FOX_W
GF_W = 2 * D + 128
F_FF = 2816
ROPE_DIM = 16
ROPE_THETA = 500000.0
EPS = 1e-6
NEG = -1e30
SCALE = 1.0 / math.sqrt(HD)
IN_COLS = 5896
N_SHARD = 4

ADAM_LR, ADAM_B1, ADAM_B2, ADAM_EPS, ADAM_WD, ADAM_STEP = 0.001, 0.9, 0.999, 1e-08, 0.01, 10

VMEM_V7X = 64 * 1024 * 1024
VMEM_PLAN_MAX = VMEM_V7X - 8 * 1024 * 1024

TM = 256
TQ = 256

W_NAMES = ("w_in", "w_proj_a", "w_proj_b", "w_out", "w_ffn_gate", "w_ffn_up", "w_ffn_down")
TRANSPOSED = ("w_in", "w_ffn_gate", "w_ffn_up")
IN_SHARD = IN_COLS // N_SHARD
IN_SHARD_PAD = 1504
SHARD_SHAPE = dict(w_in=(IN_SHARD_PAD, D), w_proj_a=(DIL_W, D // N_SHARD), w_proj_b=(FOX_W, D // N_SHARD),
                   w_out=(D // N_SHARD, D), w_ffn_gate=(F_FF // N_SHARD, D), w_ffn_up=(F_FF // N_SHARD, D),
                   w_ffn_down=(F_FF // N_SHARD, D))
SMALL_ROWS = 8
LOSS_ROW = 5


def _nbytes(shape, dtype):
    return math.prod(shape) * jnp.dtype(dtype).itemsize


def _params(semantics, block_bytes, temp_bytes=0):
    need = 2 * block_bytes + temp_bytes + (2 << 20)
    return pltpu.CompilerParams(dimension_semantics=semantics, vmem_limit_bytes=int(min(need, VMEM_PLAN_MAX)))


def _row(w, tm=TM):
    return pl.BlockSpec((tm, w), lambda i: (i, 0))


def _vec(w):
    return pl.BlockSpec((1, w), lambda i: (0, 0))


def _mm(pairs, dims, out_dtype, *, tm, tn, name, m_inner=False):
    a0, b0 = pairs[0]
    m_dim = a0.shape[1] if dims == "tn" else a0.shape[0]
    n_dim = b0.shape[0] if dims == "nt" else b0.shape[1]
    contract = {"nn": ((1,), (0,)), "nt": ((1,), (1,)), "tn": ((0,), (0,))}[dims]
    n_pairs = len(pairs)
    assert m_dim % tm == 0 and n_dim % tn == 0, (name, m_dim, n_dim, tm, tn)

    def body(*refs):
        o_ref = refs[-1]
        acc = None
        for p in range(n_pairs):
            a = refs[2 * p][...].astype(bf16)
            b = refs[2 * p + 1][...].astype(bf16)
            t = lax.dot_general(a, b, (contract, ((), ())), preferred_element_type=f32)
            acc = t if acc is None else acc + t
        o_ref[...] = acc.astype(o_ref.dtype)

    if m_inner:
        grid = (n_dim // tn, m_dim // tm)
        mi = lambda j, i: i
        ni = lambda j, i: j
    else:
        grid = (m_dim // tm, n_dim // tn)
        mi = lambda i, j: i
        ni = lambda i, j: j
    in_specs, block_bytes, args = [], 0, []
    for a, b in pairs:
        k_dim = a.shape[0] if dims == "tn" else a.shape[1]
        if dims == "tn":
            in_specs.append(pl.BlockSpec((k_dim, tm), lambda *g: (0, mi(*g))))
        else:
            in_specs.append(pl.BlockSpec((tm, k_dim), lambda *g: (mi(*g), 0)))
        if dims == "nt":
            in_specs.append(pl.BlockSpec((tn, k_dim), lambda *g: (ni(*g), 0)))
        else:
            in_specs.append(pl.BlockSpec((k_dim, tn), lambda *g: (0, ni(*g))))
        block_bytes += _nbytes((tm, k_dim), a.dtype) + _nbytes((tn, k_dim), b.dtype)
        args += [a, b]
    block_bytes += _nbytes((tm, tn), out_dtype)
    temp = _nbytes((tm, tn), f32) * 2 + sum(_nbytes((tm, a.shape[0] if dims == "tn" else a.shape[1]), bf16)
                                            + _nbytes((tn, a.shape[0] if dims == "tn" else a.shape[1]), bf16)
                                            for a, _ in pairs)
    return pl.pallas_call(
        body, grid=grid, in_specs=in_specs,
        out_specs=pl.BlockSpec((tm, tn), lambda *g: (mi(*g), ni(*g))),
        out_shape=SDS((m_dim, n_dim), out_dtype), name=name,
        compiler_params=_params(("parallel", "parallel"), block_bytes, temp),
    )(*args)


def _rms(x, g):
    r = lax.rsqrt(jnp.mean(x * x, axis=-1, keepdims=True) + EPS)
    return x * r * g


def _rms_bwd(x, g, dy):
    r = lax.rsqrt(jnp.mean(x * x, axis=-1, keepdims=True) + EPS)
    xh = x * r
    dxh = dy * g
    dx = r * (dxh - xh * jnp.mean(dxh * xh, axis=-1, keepdims=True))
    return dx, jnp.sum(dy * xh, axis=0, keepdims=True)


def _acc_rows(ref, val):
    @pl.when(pl.program_id(0) == 0)
    def _():
        ref[...] = jnp.zeros_like(ref)
    ref[...] += val


def _norm_fwd(xs, g):
    n = len(xs)

    def body(*refs):
        g = refs[n][...]
        for x_ref, h_ref in zip(refs[:n], refs[n + 1:]):
            h_ref[...] = _rms(x_ref[...], g).astype(bf16)

    return pl.pallas_call(
        body, grid=(S // TM,), in_specs=[_row(D)] * n + [_vec(D)], out_specs=[_row(D)] * n,
        out_shape=[SDS((S, D), bf16)] * n, name="norm_mix_pre",
        compiler_params=_params(("parallel",), 6 * n * TM * D, 8 * n * TM * D))(*xs, g)


def _perm_rows(xs, ds, name):
    n = len(xs)

    def body(*refs):
        outs = iter(refs[n:])
        for x_ref in refs[:n]:
            for d in ds:
                o_ref, rows = next(outs), S // d
                for r in range(d):
                    o_ref[r * rows:(r + 1) * rows, :] = x_ref[pl.ds(r, rows, stride=d), :]

    blk = pl.BlockSpec((S, 128), lambda c: (0, c))
    w = xs[0].shape[1]
    return pl.pallas_call(
        body, grid=(w // 128,), in_specs=[blk] * n, out_specs=[blk] * (n * len(ds)),
        out_shape=[SDS((S, w), f32)] * (n * len(ds)), name=name,
        compiler_params=_params(("parallel",), 4 * S * 128 * n * (1 + len(ds))))(*xs)


def _unperm_sum(nat, perms, ds, name):
    n = len(perms)

    def body(*refs):
        a_ref, o_ref, sc = refs[0], refs[n + 1], refs[n + 2]
        acc = a_ref[...]
        for b_ref, d in zip(refs[1:n + 1], ds):
            rows = S // d
            for r in range(d):
                sc[pl.ds(r, rows, stride=d), :] = b_ref[r * rows:(r + 1) * rows, :]
            acc = acc + sc[...]
        o_ref[...] = acc

    blk = pl.BlockSpec((S, 128), lambda c: (0, c))
    w = nat.shape[1]
    return pl.pallas_call(
        body, grid=(w // 128,), in_specs=[blk] * (n + 1), out_specs=blk, out_shape=SDS((S, w), f32),
        scratch_shapes=[pltpu.VMEM((S, 128), f32)], name=name,
        compiler_params=_params(("parallel",), 4 * S * 128 * (n + 2), 8 * S * 128))(nat, *perms)


def _resid_norm_fwd(x, mix, g_post, g_pre):
    def body(x_ref, mix_ref, gp_ref, gn_ref, x2_ref, h_ref):
        x2 = x_ref[...] + _rms(mix_ref[...], gp_ref[...])
        x2_ref[...] = x2
        h_ref[...] = _rms(x2, gn_ref[...]).astype(bf16)

    return pl.pallas_call(
        body, grid=(S // TM,), in_specs=[_row(D), _row(D), _vec(D), _vec(D)], out_specs=[_row(D), _row(D)],
        out_shape=[SDS((S, D), f32), SDS((S, D), bf16)], name="resid_norm_mid",
        compiler_params=_params(("parallel",), 14 * TM * D, 16 * TM * D))(x, mix, g_post, g_pre)


def _loss_head(x2, ff, g_post, target):
    def body(x2_ref, ff_ref, g_ref, t_ref, loss_ref, dy_ref, dff_ref, dg_ref):
        ff = ff_ref[...]
        g = g_ref[...]
        err = x2_ref[...] + _rms(ff, g) - t_ref[...]
        dy = err * (1.0 / D)
        dff, dg = _rms_bwd(ff, g, dy)
        dy_ref[...] = dy
        dff_ref[...] = dff.astype(bf16)
        _acc_rows(dg_ref, dg)
        _acc_rows(loss_ref, jnp.full((1, 128), jnp.sum(err * err), f32))

    return pl.pallas_call(
        body, grid=(S // TM,), in_specs=[_row(D), _row(D), _vec(D), _row(D)],
        out_specs=[_vec(128), _row(D), _row(D), _vec(D)],
        out_shape=[SDS((1, 128), f32), SDS((S, D), f32), SDS((S, D), bf16), SDS((1, D), f32)], name="loss_head",
        compiler_params=_params(("arbitrary",), 18 * TM * D, 24 * TM * D))(x2, ff, g_post, target)


def _norm_bwd_mid(dy, dh3, x2, mix, g_ffn_pre, g_mix_post):
    def body(dy_ref, dh_ref, x2_ref, mix_ref, g3_ref, g2_ref, dx2_ref, dmix_ref, dg3_ref, dg2_ref):
        d3, dg3 = _rms_bwd(x2_ref[...], g3_ref[...], dh_ref[...])
        dx2 = dy_ref[...] + d3
        dmix, dg2 = _rms_bwd(mix_ref[...], g2_ref[...], dx2)
        dx2_ref[...] = dx2
        dmix_ref[...] = dmix.astype(bf16)
        _acc_rows(dg3_ref, dg3)
        _acc_rows(dg2_ref, dg2)

    return pl.pallas_call(
        body, grid=(S // TM,), in_specs=[_row(D)] * 4 + [_vec(D)] * 2,
        out_specs=[_row(D), _row(D), _vec(D), _vec(D)],
        out_shape=[SDS((S, D), f32), SDS((S, D), bf16), SDS((1, D), f32), SDS((1, D), f32)], name="norm_bwd_mid",
        compiler_params=_params(("arbitrary",), 22 * TM * D, 24 * TM * D))(dy, dh3, x2, mix, g_ffn_pre, g_mix_post)


def _norm_bwd_in(dx2, dh1, x, g):
    def body(dx2_ref, dh_ref, x_ref, g_ref, gx_ref, dg_ref):
        d1, dg = _rms_bwd(x_ref[...], g_ref[...], dh_ref[...])
        gx_ref[...] = dx2_ref[...] + d1
        _acc_rows(dg_ref, dg)

    return pl.pallas_call(
        body, grid=(S // TM,), in_specs=[_row(D)] * 3 + [_vec(D)], out_specs=[_row(D), _vec(D)],
        out_shape=[SDS((S, D), f32), SDS((1, D), f32)], name="norm_bwd_in",
        compiler_params=_params(("arbitrary",), 16 * TM * D, 16 * TM * D))(dx2, dh1, x, g)


def _rope_tables():
    half = ROPE_DIM // 2
    inv_freq = np.power(np.float32(ROPE_THETA), -np.arange(0, ROPE_DIM, 2, dtype=np.float32) / np.float32(ROPE_DIM))
    row = np.arange(S)
    groups = []
    for _, d in DIL_GROUPS:
        pos = ((row % (S // d)) * d + row // (S // d)).astype(np.float32)
        ang = pos[:, None] * inv_freq[None, :].astype(np.float32)
        cos, sin = np.cos(ang).astype(np.float32), np.sin(ang).astype(np.float32)
        c = np.concatenate([cos, cos, np.ones((S, HD - ROPE_DIM), np.float32)], axis=1)
        s_lo = np.concatenate([-sin, np.zeros((S, HD - half), np.float32)], axis=1)
        s_hi = np.concatenate([np.zeros((S, half), np.float32), sin, np.zeros((S, HD - ROPE_DIM), np.float32)], axis=1)
        groups.append(np.stack([np.concatenate([t, t], axis=1) for t in (c, s_lo, s_hi)]))
    return jnp.asarray(np.stack(groups))


def _rotate(x, c, lo, hi, sign):
    tile = lambda t: jnp.tile(t, (1, DIL_W // 128))
    return (x * tile(c) + pltpu.roll(x, DIL_W - ROPE_DIM // 2, 1) * (tile(lo) * sign)
            + pltpu.roll(x, ROPE_DIM // 2, 1) * (tile(hi) * sign))


def _table_specs(g):
    return [pl.BlockSpec((None, None, TM, 128), lambda i, k=k: (g, k, i, 0)) for k in range(3)]


def _rope_fwd(g, p_qkv, tables):
    def body(x_ref, c_ref, lo_ref, hi_ref, o_ref):
        c, lo, hi = c_ref[...], lo_ref[...], hi_ref[...]
        for part in range(2):
            cols = slice(part * DIL_W, (part + 1) * DIL_W)
            o_ref[:, cols] = _rotate(x_ref[:, cols], c, lo, hi, 1.0).astype(bf16)
        o_ref[:, 2 * DIL_W:] = x_ref[:, 2 * DIL_W:].astype(bf16)

    return pl.pallas_call(
        body, grid=(S // TM,), in_specs=[_row(QKV_W)] + _table_specs(g), out_specs=_row(QKV_W),
        out_shape=SDS((S, QKV_W), bf16), name=f"rope_fwd_{g}",
        compiler_params=_params(("parallel",), 6 * TM * QKV_W + 12 * TM * 128, 24 * TM * QKV_W))(p_qkv, tables, tables, tables)


def _rope_bwd(g, dq, dk, dv, tables):
    def body(dq_ref, dk_ref, dv_ref, c_ref, lo_ref, hi_ref, o_ref):
        c, lo, hi = c_ref[...], lo_ref[...], hi_ref[...]
        o_ref[:, :DIL_W] = _rotate(dq_ref[...], c, lo, hi, -1.0).astype(bf16)
        o_ref[:, DIL_W:2 * DIL_W] = _rotate(dk_ref[...], c, lo, hi, -1.0).astype(bf16)
        o_ref[:, 2 * DIL_W:] = dv_ref[...].astype(bf16)

    return pl.pallas_call(
        body, grid=(S // TM,), in_specs=[_row(DIL_W)] * 3 + _table_specs(g), out_specs=_row(QKV_W),
        out_shape=SDS((S, QKV_W), bf16), name=f"rope_bwd_{g}",
        compiler_params=_params(("parallel",), 6 * TM * QKV_W + 12 * TM * 128, 24 * TM * QKV_W))(dq, dk, dv, tables, tables, tables)


def _nt(a, b):
    return lax.dot_general(a, b, (((1,), (1,)), ((), ())), preferred_element_type=f32)


def _tn(a, b):
    return lax.dot_general(a, b, (((0,), (0,)), ((), ())), preferred_element_type=f32)


STEP_BLOCKS = 4
STEP_ROWS = STEP_BLOCKS * BLK


def _dil_prev(g, b):
    _, d = DIL_GROUPS[g]
    nb = S // d // BLK
    if nb == 1 or (b == 0 and nb <= STEP_BLOCKS):
        return None
    return "in" if b > 0 else "halo"


def _bnt(a, b):
    return lax.dot_general(a, b, (((2,), (2,)), ((0,), (0,))), preferred_element_type=f32)


def _bnn(a, b):
    return lax.dot_general(a, b, (((2,), (1,)), ((0,), (0,))), preferred_element_type=f32)


def _btn(a, b):
    return lax.dot_general(a, b, (((1,), (1,)), ((0,), (0,))), preferred_element_type=f32)


def _on_tail(x, tail, fn):
    if tail == x.shape[0]:
        return fn(x)
    return jnp.concatenate([x[:-tail], fn(x[-tail:])], axis=0)


def _heads(ref, part):
    n = ref.shape[0] // BLK
    return jnp.stack([ref[b * BLK:(b + 1) * BLK, part * DIL_W + h * HD:part * DIL_W + (h + 1) * HD]
                      for b in range(n) for h in range(SLOTS)])


def _dil_operands(g, qkv_ref, halo_ref):
    q, kc, vc = (_heads(qkv_ref, part) for part in range(3))
    qi = lax.broadcasted_iota(jnp.int32, (1, BLK, BLK), 1)
    kj = lax.broadcasted_iota(jnp.int32, (1, BLK, BLK), 2)
    with_prev = [b for b in range(STEP_BLOCKS) if _dil_prev(g, b) is not None]
    tail = SLOTS * len(with_prev)
    if not tail:
        return q, kc, vc, None, None, kj <= qi, None, 0
    assert with_prev == list(range(STEP_BLOCKS - len(with_prev), STEP_BLOCKS))
    inside = SLOTS * sum(_dil_prev(g, b) == "in" for b in with_prev)
    kp, vp, prev = kc[:inside], vc[:inside], jnp.broadcast_to(kj >= qi, (inside, BLK, BLK))
    if inside < tail:
        no_halo = jnp.where(pl.program_id(0) == 0, BLK + 1, 0)
        kp = jnp.concatenate([_heads(halo_ref, 1), kp], axis=0)
        vp = jnp.concatenate([_heads(halo_ref, 2), vp], axis=0)
        prev = jnp.concatenate([jnp.broadcast_to(kj >= qi + no_halo, (SLOTS, BLK, BLK)), prev], axis=0)
    return q, kc, vc, kp, vp, kj <= qi, prev, tail


def _dil_in_specs(g, n_aux):
    step = lambda w: pl.BlockSpec((STEP_ROWS, w), lambda i: (i, 0))
    halo = [pl.BlockSpec((BLK, QKV_W), lambda i: (jnp.maximum(i * STEP_BLOCKS - 1, 0), 0))]
    needs_halo = _dil_prev(g, 0) == "halo"
    return [step(QKV_W)] + (halo if needs_halo else []) + [step(DIL_W)] * n_aux, needs_halo


def _dil_fwd(g, qkv):
    in_specs, needs_halo = _dil_in_specs(g, 0)

    def body(*refs):
        qkv_ref, halo_ref = refs[0], refs[1] if needs_halo else None
        o_ref, lse_ref = refs[-2:]
        q, kc, vc, kp, vp, cur, prev, tail = _dil_operands(g, qkv_ref, halo_ref)
        sc = jnp.where(cur, _bnt(q, kc) * SCALE, NEG)
        m = jnp.max(sc, axis=-1, keepdims=True)
        if tail:
            sp = jnp.where(prev, _bnt(q[-tail:], kp) * SCALE, NEG)
            m = _on_tail(m, tail, lambda t: jnp.maximum(t, jnp.max(sp, axis=-1, keepdims=True)))
            pp = jnp.exp(sp - m[-tail:])
        pc = jnp.exp(sc - m)
        den = jnp.sum(pc, axis=-1, keepdims=True)
        if tail:
            den = _on_tail(den, tail, lambda t: t + jnp.sum(pp, axis=-1, keepdims=True))
        inv = 1.0 / den
        o = _bnn((pc * inv).astype(bf16), vc)
        if tail:
            o = _on_tail(o, tail, lambda t: t + _bnn((pp * inv[-tail:]).astype(bf16), vp))
        lse = m + jnp.log(den)
        for b in range(STEP_BLOCKS):
            for h in range(SLOTS):
                rows, hs = slice(b * BLK, (b + 1) * BLK), slice(h * HD, (h + 1) * HD)
                o_ref[rows, hs] = o[SLOTS * b + h]
                lse_ref[rows, hs] = jnp.broadcast_to(lse[SLOTS * b + h], (BLK, HD))

    out = pl.BlockSpec((STEP_ROWS, DIL_W), lambda i: (i, 0))
    return pl.pallas_call(
        body, grid=(S // STEP_ROWS,), in_specs=in_specs, out_specs=[out, out], out_shape=[SDS((S, DIL_W), f32)] * 2,
        name=f"dil_fwd_{g}", compiler_params=_params(("parallel",), 12 * STEP_ROWS * DIL_W, 2 << 20),
    )(*([qkv] * (2 if needs_halo else 1)))


def _dil_combine(outs, lses):
    def body(o0, o1, o2, l0, l1, l2, out_ref, lse_ref, so1, so2, sl1, sl2):
        for (_, d), src, dst in ((DIL_GROUPS[1], o1, so1), (DIL_GROUPS[2], o2, so2),
                                 (DIL_GROUPS[1], l1, sl1), (DIL_GROUPS[2], l2, sl2)):
            rows = S // d
            for r in range(d):
                dst[pl.ds(r, rows, stride=d), :] = src[r * rows:(r + 1) * rows, :]
        a, b, c = l0[...], sl1[...], sl2[...]
        m = jnp.maximum(jnp.maximum(a, b), c)
        ea, eb, ec = jnp.exp(a - m), jnp.exp(b - m), jnp.exp(c - m)
        z = ea + eb + ec
        inv = 1.0 / z
        out_ref[...] = (ea * inv) * o0[...] + (eb * inv) * so1[...] + (ec * inv) * so2[...]
        lse_ref[...] = m + jnp.log(z)

    blk = pl.BlockSpec((S, 128), lambda c: (0, c))
    return pl.pallas_call(
        body, grid=(DIL_W // 128,), in_specs=[blk] * 6, out_specs=[blk] * 2,
        out_shape=[SDS((S, DIL_W), f32)] * 2, scratch_shapes=[pltpu.VMEM((S, 128), f32)] * 4, name="dil_combine",
        compiler_params=_params(("parallel",), 32 * S * 128, 32 * S * 128))(*outs, *lses)


def _dil_bwd(g, qkv, d_out, delta, lse):
    in_specs, needs_halo = _dil_in_specs(g, 3)

    def body(*refs):
        qkv_ref, halo_ref = refs[0], refs[1] if needs_halo else None
        do_ref, dl_ref, lse_ref, dq_ref, dk_ref, dv_ref = refs[-6:]
        q, kc, vc, kp, vp, cur, prev, tail = _dil_operands(g, qkv_ref, halo_ref)
        tiles = [(slice(b * BLK, (b + 1) * BLK), h) for b in range(STEP_BLOCKS) for h in range(SLOTS)]
        do = jnp.stack([do_ref[rows, h * HD:(h + 1) * HD] for rows, h in tiles]).astype(bf16)
        lse = jnp.stack([lse_ref[rows, h * HD:h * HD + 1] for rows, h in tiles])
        delta = jnp.stack([dl_ref[rows, h * HD:h * HD + 1] for rows, h in tiles])

        def probs(q, k, mask, lse, do, v, delta):
            p = jnp.exp(jnp.where(mask, _bnt(q, k) * SCALE, NEG) - lse)
            ds = p * (_bnt(do, v) - delta) * SCALE
            return p.astype(bf16), ds.astype(bf16)

        p, ds = probs(q, kc, cur, lse, do, vc, delta)
        dq, dk, dv = _bnn(ds, kc), _btn(ds, q), _btn(p, do)
        if tail:
            p, ds = probs(q[-tail:], kp, prev, lse[-tail:], do[-tail:], vp, delta[-tail:])
            dq = _on_tail(dq, tail, lambda t: t + _bnn(ds, kp))
            dk_p, dv_p = _btn(ds, q[-tail:]), _btn(p, do[-tail:])
            inside = tail - SLOTS if needs_halo else tail
            pad = jnp.zeros((len(tiles) - inside, BLK, HD), f32)
            dk = dk + jnp.concatenate([dk_p[tail - inside:], pad], axis=0)
            dv = dv + jnp.concatenate([dv_p[tail - inside:], pad], axis=0)
        first = pl.multiple_of(pl.program_id(0) * STEP_ROWS, STEP_ROWS)
        for t, (rows, h) in enumerate(tiles):
            hs = slice(h * HD, (h + 1) * HD)
            own = pl.ds(pl.multiple_of(first + rows.start, BLK), BLK)
            dq_ref[rows, hs] = dq[t]
            dk_ref[own, hs] = dk[t]
            dv_ref[own, hs] = dv[t]
        if needs_halo:
            before = pl.ds(pl.multiple_of(jnp.maximum(first - BLK, 0), BLK), BLK)
            for h in range(SLOTS):
                hs = slice(h * HD, (h + 1) * HD)
                dk_ref[before, hs] += dk_p[h]
                dv_ref[before, hs] += dv_p[h]

    whole = pl.BlockSpec((S, DIL_W), lambda i: (0, 0))
    return pl.pallas_call(
        body, grid=(S // STEP_ROWS,), in_specs=in_specs,
        out_specs=[pl.BlockSpec((STEP_ROWS, DIL_W), lambda i: (i, 0)), whole, whole],
        out_shape=[SDS((S, DIL_W), f32)] * 3, name=f"dil_bwd_{g}",
        compiler_params=_params(("arbitrary",), 20 * STEP_ROWS * DIL_W + 8 * S * DIL_W, 2 << 20),
    )(*([qkv] * (2 if needs_halo else 1)), d_out, delta, lse)


def _scan_rows(x, reverse):
    row = lax.broadcasted_iota(jnp.int32, x.shape, 0)
    k = 1
    while k < S:
        if reverse:
            x = x + jnp.where(row < S - k, pltpu.roll(x, S - k, 0), 0.0)
        else:
            x = x + jnp.where(row >= k, pltpu.roll(x, k, 0), 0.0)
        k *= 2
    return x


N_PAIR = N_FOX // 2
_PAIR_Q = pl.BlockSpec((None, S, 128), lambda p: (p, 0, 0))
_PAIR_K = pl.BlockSpec((None, 8, S), lambda p: (p, 0, 0))


def _forget_fwd(fz, b128):
    def body(z_ref, b_ref, fq_ref, fk_ref):
        z = z_ref[...] + b_ref[...]
        logf = jnp.minimum(z, 0.0) - jnp.log1p(jnp.exp(-jnp.abs(z)))
        f_cum = _scan_rows(logf, reverse=False)
        f_cum_t = f_cum.T
        fq_ref[...] = jnp.zeros_like(fq_ref)
        fk_ref[...] = jnp.zeros_like(fk_ref)
        for p in range(N_PAIR):
            fq_ref[p, :, 0:2] = f_cum[:, 2 * p:2 * p + 2]
            fk_ref[p, 0:2, :] = f_cum_t[2 * p:2 * p + 2, :]

    return pl.pallas_call(
        body, grid=(1,), in_specs=[pl.BlockSpec((S, 128), lambda i: (0, 0)), _vec(128)],
        out_specs=[pl.BlockSpec((N_PAIR, S, 128), lambda i: (0, 0, 0)), pl.BlockSpec((N_PAIR, 8, S), lambda i: (0, 0, 0))],
        out_shape=[SDS((N_PAIR, S, 128), f32), SDS((N_PAIR, 8, S), f32)], name="forget_fwd",
        compiler_params=_params(("arbitrary",), 24 * S * 128, 24 * S * 128))(fz, b128)


def _forget_bwd(fz, b128, d_f_cols, d_f_rows):
    def body(z_ref, b_ref, dfc_ref, dfr_ref, dz_ref, db_ref, df_sc):
        z = z_ref[...] + b_ref[...]
        df_sc[...] = jnp.zeros_like(df_sc)
        for p in range(N_PAIR):
            df_sc[:, 2 * p:2 * p + 2] = dfr_ref[p, :, 0:2] + dfc_ref[p].T[:, 0:2]
        dz = _scan_rows(df_sc[...], reverse=True) * jax.nn.sigmoid(-z)
        dz_ref[...] = dz
        db_ref[...] = jnp.sum(dz, axis=0, keepdims=True)

    full = pl.BlockSpec((S, 128), lambda i: (0, 0))
    return pl.pallas_call(
        body, grid=(1,),
        in_specs=[full, _vec(128), pl.BlockSpec((N_PAIR, 8, S), lambda i: (0, 0, 0)), pl.BlockSpec((N_PAIR, S, 128), lambda i: (0, 0, 0))],
        out_specs=[full, _vec(128)], out_shape=[SDS((S, 128), f32), SDS((1, 128), f32)],
        scratch_shapes=[pltpu.VMEM((S, 128), f32)], name="forget_bwd",
        compiler_params=_params(("arbitrary",), 32 * S * 128, 24 * S * 128))(fz, b128, d_f_cols, d_f_rows)


def _fox_scores(q_ref, k_ref, fq_ref, fk_ref, qi, hh):
    n = (qi + 1) * TQ
    rows, hs = slice(qi * TQ, n), slice(hh * HD, (hh + 1) * HD)
    s = _nt(q_ref[rows, hs], k_ref[0:n, hs]) * SCALE + (fq_ref[rows, hh:hh + 1] - fk_ref[hh:hh + 1, 0:n])
    qpos = qi * TQ + lax.broadcasted_iota(jnp.int32, (TQ, n), 0)
    kpos = lax.broadcasted_iota(jnp.int32, (TQ, n), 1)
    return jnp.where(kpos <= qpos, s, NEG)


def _pair_cols(first):
    return pl.BlockSpec((S, 128), lambda p: (0, first + p))


def _fox_fwd(vr, fq, fk):
    def body(q_ref, k_ref, v_ref, fq_ref, fk_ref, o_ref, lse_ref):
        lse_ref[...] = jnp.zeros_like(lse_ref)
        for hh in range(2):
            hs = slice(hh * HD, (hh + 1) * HD)
            for qi in range(S // TQ):
                n = (qi + 1) * TQ
                rows = slice(qi * TQ, n)
                s = _fox_scores(q_ref, k_ref, fq_ref, fk_ref, qi, hh)
                m = jnp.max(s, axis=-1, keepdims=True)
                p = jnp.exp(s - m)
                den = jnp.sum(p, axis=-1, keepdims=True)
                o_ref[rows, hs] = jnp.dot((p * (1.0 / den)).astype(bf16), v_ref[0:n, hs], preferred_element_type=f32)
                lse_ref[rows, hh:hh + 1] = m + jnp.log(den)

    return pl.pallas_call(
        body, grid=(N_PAIR,), in_specs=[_pair_cols(0), _pair_cols(N_PAIR), _pair_cols(2 * N_PAIR), _PAIR_Q, _PAIR_K],
        out_specs=[_pair_cols(0), _PAIR_Q], out_shape=[SDS((S, FOX_W), f32), SDS((N_PAIR, S, 128), f32)],
        name="fox_fwd", compiler_params=_params(("parallel",), 12 * S * 128, 16 * TQ * S),
    )(vr, vr, vr, fq, fk)


def _fox_bwd(vr, fq, fk, lse, d_out, delta):
    def body(q_ref, k_ref, v_ref, do_ref, fq_ref, fk_ref, lse_ref, dl_ref, dq_ref, dk_ref, dv_ref, dfc_ref, dfr_ref,
             dk_sc, dv_sc):
        dfc_ref[...] = jnp.zeros_like(dfc_ref)
        dfr_ref[...] = jnp.zeros_like(dfr_ref)
        for hh in range(2):
            hs = slice(hh * HD, (hh + 1) * HD)
            dk_sc[...] = jnp.zeros_like(dk_sc)
            dv_sc[...] = jnp.zeros_like(dv_sc)
            for qi in range(S // TQ):
                n = (qi + 1) * TQ
                rows = slice(qi * TQ, n)
                q, do, k, v = q_ref[rows, hs], do_ref[rows, hs], k_ref[0:n, hs], v_ref[0:n, hs]
                p = jnp.exp(_fox_scores(q_ref, k_ref, fq_ref, fk_ref, qi, hh) - lse_ref[rows, hh:hh + 1])
                ds = p * (_nt(do, v) - dl_ref[rows, hh:hh + 1])
                dsb = ds.astype(bf16)
                dq_ref[rows, hs] = jnp.dot(dsb, k, preferred_element_type=f32) * SCALE
                dk_sc[0:n, :] += _tn(dsb, q) * SCALE
                dv_sc[0:n, :] += _tn(p.astype(bf16), do)
                dfc_ref[hh:hh + 1, 0:n] -= jnp.sum(ds, axis=0, keepdims=True)
                dfr_ref[rows, hh:hh + 1] = jnp.sum(ds, axis=-1, keepdims=True)
            dk_ref[:, hs] = dk_sc[...]
            dv_ref[:, hs] = dv_sc[...]

    cols = [_pair_cols(k * N_PAIR) for k in range(3)]
    return pl.pallas_call(
        body, grid=(N_PAIR,), in_specs=cols + [_pair_cols(0), _PAIR_Q, _PAIR_K, _PAIR_Q, _PAIR_Q],
        out_specs=[_pair_cols(0)] * 3 + [_PAIR_K, _PAIR_Q],
        out_shape=[SDS((S, FOX_W), f32)] * 3 + [SDS((N_PAIR, 8, S), f32), SDS((N_PAIR, S, 128), f32)],
        scratch_shapes=[pltpu.VMEM((S, HD), f32)] * 2, name="fox_bwd",
        compiler_params=_params(("parallel",), 32 * S * 128, 24 * TQ * S),
    )(vr, vr, vr, d_out, fq, fk, lse, delta)


def _merge_fwd(out_a, out_b, w_a, w_b, gf):
    cw = D // N_SHARD

    def body(oa_ref, ob_ref, wa_ref, wb_ref, ga_ref, gb_ref, ya_ref, yb_ref, mg_ref):
        oa, ob = oa_ref[...].astype(bf16), ob_ref[...].astype(bf16)
        for j in range(N_SHARD):
            cols = slice(j * cw, (j + 1) * cw)
            ya = jnp.dot(oa, wa_ref[j], preferred_element_type=f32)
            yb = jnp.dot(ob, wb_ref[j], preferred_element_type=f32)
            ya_ref[:, cols] = ya
            yb_ref[:, cols] = yb
            mg_ref[:, cols] = (jax.nn.sigmoid(ga_ref[:, cols]) * ya + jax.nn.sigmoid(gb_ref[:, cols]) * yb).astype(bf16)

    full = lambda a: pl.BlockSpec(a.shape, lambda i: (0, 0, 0))
    return pl.pallas_call(
        body, grid=(S // TM,),
        in_specs=[_row(DIL_W), _row(FOX_W), full(w_a), full(w_b), _row(D), pl.BlockSpec((TM, D), lambda i: (i, 1))],
        out_specs=[_row(D)] * 3, out_shape=[SDS((S, D), f32), SDS((S, D), f32), SDS((S, D), bf16)], name="merge_fwd",
        compiler_params=_params(("parallel",), 22 * TM * D + 2 * (DIL_W + FOX_W) * D, 16 * TM * D),
    )(out_a, out_b, w_a, w_b, gf, gf)


def _merge_bwd(d_merged, ya, yb, gf):
    def body(dm_ref, ya_ref, yb_ref, ga_ref, gb_ref, dya_ref, dyb_ref, dg_ref):
        dm = dm_ref[...]
        sa, sb = jax.nn.sigmoid(ga_ref[...]), jax.nn.sigmoid(gb_ref[...])
        dya_ref[...] = (dm * sa).astype(bf16)
        dyb_ref[...] = (dm * sb).astype(bf16)
        dg_ref[:, :D] = (dm * ya_ref[...] * sa * (1.0 - sa)).astype(bf16)
        dg_ref[:, D:] = (dm * yb_ref[...] * sb * (1.0 - sb)).astype(bf16)

    return pl.pallas_call(
        body, grid=(S // TM,),
        in_specs=[_row(D)] * 4 + [pl.BlockSpec((TM, D), lambda i: (i, 1))],
        out_specs=[_row(D), _row(D), _row(2 * D)],
        out_shape=[SDS((S, D), bf16), SDS((S, D), bf16), SDS((S, 2 * D), bf16)], name="merge_bwd",
        compiler_params=_params(("parallel",), 28 * TM * D, 24 * TM * D))(d_merged, ya, yb, gf, gf)


def _branch_bwd(d_ya, d_yb, w_a, w_b, out_a, out_b):
    cw = D // N_SHARD

    def body(dya_ref, dyb_ref, wa_ref, wb_ref, oa_ref, ob_ref, doa_ref, dla_ref, dob_ref, dlb_ref):
        doa = jnp.zeros((TM, DIL_W), f32)
        dob = jnp.zeros((TM, FOX_W), f32)
        for j in range(N_SHARD):
            cols = slice(j * cw, (j + 1) * cw)
            doa += _nt(dya_ref[:, cols], wa_ref[j])
            dob += _nt(dyb_ref[:, cols], wb_ref[j])
        doa_ref[...] = doa
        dob_ref[...] = dob.astype(bf16)
        prod_a = doa * oa_ref[...]
        for h in range(SLOTS):
            hs = slice(h * HD, (h + 1) * HD)
            dla_ref[:, hs] = jnp.broadcast_to(jnp.sum(prod_a[:, hs], axis=-1, keepdims=True), (TM, HD))
        prod_b = dob * ob_ref[...]
        dlb_ref[...] = jnp.zeros_like(dlb_ref)
        for h in range(N_FOX):
            dlb_ref[h // 2, :, h % 2:h % 2 + 1] = jnp.sum(prod_b[:, h * HD:(h + 1) * HD], axis=-1, keepdims=True)

    full = lambda a: pl.BlockSpec(a.shape, lambda i: (0, 0, 0))
    return pl.pallas_call(
        body, grid=(S // TM,),
        in_specs=[_row(D), _row(D), full(w_a), full(w_b), _row(DIL_W), _row(FOX_W)],
        out_specs=[_row(DIL_W), _row(DIL_W), _row(FOX_W), pl.BlockSpec((N_PAIR, TM, 128), lambda i: (0, i, 0))],
        out_shape=[SDS((S, DIL_W), f32), SDS((S, DIL_W), f32), SDS((S, FOX_W), bf16), SDS((N_PAIR, S, 128), f32)],
        name="branch_bwd", compiler_params=_params(("parallel",), 8 * TM * D + 2 * (DIL_W + FOX_W) * D, 8 * TM * D),
    )(d_ya, d_yb, w_a, w_b, out_a, out_b)


def _branch_grads(out_a, out_b, d_ya, d_yb):
    cw = D // N_SHARD

    def body(oa_ref, ob_ref, dya_ref, dyb_ref, ga_ref, gb_ref):
        ga_ref[...] = _tn(oa_ref[...].astype(bf16), dya_ref[...])
        gb_ref[...] = _tn(ob_ref[...].astype(bf16), dyb_ref[...])

    whole = lambda w: pl.BlockSpec((S, w), lambda j: (0, 0))
    cols = pl.BlockSpec((S, cw), lambda j: (0, j))
    return pl.pallas_call(
        body, grid=(N_SHARD,), in_specs=[whole(DIL_W), whole(FOX_W), cols, cols],
        out_specs=[pl.BlockSpec((None, DIL_W, cw), lambda j: (j, 0, 0)), pl.BlockSpec((None, FOX_W, cw), lambda j: (j, 0, 0))],
        out_shape=[SDS((N_SHARD, DIL_W, cw), f32), SDS((N_SHARD, FOX_W, cw), f32)], name="grad_w_proj_ab",
        compiler_params=_params(("parallel",), 4 * S * (DIL_W + FOX_W) + 4 * S * cw + 4 * (DIL_W + FOX_W) * cw,
                                4 * S * (DIL_W + FOX_W)))(out_a, out_b, d_ya, d_yb)


FF_TN = F_FF // 2
FF_TM = 512


def _ffn_fwd(h, w_gate_t, w_up_t):
    def body(h_ref, wg_ref, wu_ref, g_ref, u_ref, a_ref):
        hb = h_ref[...]
        g = _nt(hb, wg_ref[...])
        u = _nt(hb, wu_ref[...])
        g_ref[...] = g
        u_ref[...] = u
        a_ref[...] = (g * jax.nn.sigmoid(g) * u).astype(bf16)

    tile = pl.BlockSpec((FF_TM, FF_TN), lambda j, i: (i, j))
    wspec = pl.BlockSpec((FF_TN, D), lambda j, i: (j, 0))
    return pl.pallas_call(
        body, grid=(F_FF // FF_TN, S // FF_TM),
        in_specs=[pl.BlockSpec((FF_TM, D), lambda j, i: (i, 0)), wspec, wspec], out_specs=[tile] * 3,
        out_shape=[SDS((S, F_FF), f32), SDS((S, F_FF), f32), SDS((S, F_FF), bf16)], name="ffn_fwd",
        compiler_params=_params(("parallel", "parallel"), 2 * FF_TM * D + 4 * D * FF_TN + 10 * FF_TM * FF_TN, 16 * FF_TM * FF_TN),
    )(h, w_gate_t, w_up_t)


def _ffn_bwd_act(d_ff, w_down, g_act, u_act):
    def body(d_ref, wd_ref, g_ref, u_ref, dg_ref, du_ref):
        da = _nt(d_ref[...], wd_ref[...])
        g = g_ref[...]
        sg = jax.nn.sigmoid(g)
        du_ref[...] = (da * g * sg).astype(bf16)
        dg_ref[...] = (da * u_ref[...] * sg * (1.0 + g * (1.0 - sg))).astype(bf16)

    tile = pl.BlockSpec((FF_TM, FF_TN), lambda j, i: (i, j))
    return pl.pallas_call(
        body, grid=(F_FF // FF_TN, S // FF_TM),
        in_specs=[pl.BlockSpec((FF_TM, D), lambda j, i: (i, 0)), pl.BlockSpec((FF_TN, D), lambda j, i: (j, 0)), tile, tile],
        out_specs=[tile, tile], out_shape=[SDS((S, F_FF), bf16)] * 2, name="ffn_bwd_act",
        compiler_params=_params(("parallel", "parallel"), 2 * FF_TM * D + 2 * D * FF_TN + 12 * FF_TM * FF_TN, 16 * FF_TM * FF_TN),
    )(d_ff, w_down, g_act, u_act)


def _row_tile(rows):
    return next(t for t in (376, 128, 176, 64, 32, 16, 8) if rows % t == 0)


def _adamw_math(w, g, m, v):
    c1 = 1.0 - ADAM_B1 ** ADAM_STEP
    c2 = 1.0 - ADAM_B2 ** ADAM_STEP
    m_new = ADAM_B1 * m + (1.0 - ADAM_B1) * g
    v_new = ADAM_B2 * v + (1.0 - ADAM_B2) * (g * g)
    return -ADAM_LR * ((m_new / c1) / (jnp.sqrt(v_new / c2) + ADAM_EPS) + ADAM_WD * w), m_new, v_new


def _adamw(w, g, m, v, name):
    rows, cols = w.shape
    tm = _row_tile(rows)

    def body(w_ref, g_ref, m_ref, v_ref, d_ref, nm_ref, nv_ref):
        d_ref[...], nm_ref[...], nv_ref[...] = _adamw_math(w_ref[...], g_ref[...], m_ref[...], v_ref[...])

    spec = pl.BlockSpec((tm, cols), lambda i: (i, 0))
    return pl.pallas_call(
        body, grid=(rows // tm,), in_specs=[spec] * 4, out_specs=[spec] * 3, out_shape=[SDS(w.shape, f32)] * 3,
        name=name, compiler_params=_params(("parallel",), 28 * tm * cols, 16 * tm * cols))(w, g, m, v)


def _adamw_halves(w, g_mine, g_theirs, m, v, name):
    rows, cols = w.shape
    tm = _row_tile(rows // 2)
    per_half = rows // 2 // tm
    core = lax.axis_index("c").astype(jnp.int32).reshape(1)

    def body(c_ref, w_ref, gm_ref, gt_ref, m_ref, v_ref, g_ref, d_ref, nm_ref, nv_ref):
        mine = pl.program_id(0) // per_half == c_ref[0]
        g = jnp.where(mine, gm_ref[...], gt_ref[...])
        g_ref[...] = g
        d_ref[...], nm_ref[...], nv_ref[...] = _adamw_math(w_ref[...], g, m_ref[...], v_ref[...])

    spec = pl.BlockSpec((tm, cols), lambda i, c_ref: (i, 0))
    in_half = lambda i, first: jnp.clip(i - first * per_half, 0, per_half - 1)
    grid_spec = pltpu.PrefetchScalarGridSpec(
        num_scalar_prefetch=1, grid=(rows // tm,),
        in_specs=[spec, pl.BlockSpec((tm, cols), lambda i, c_ref: (in_half(i, c_ref[0]), 0)),
                  pl.BlockSpec((tm, cols), lambda i, c_ref: (in_half(i, 1 - c_ref[0]), 0)), spec, spec],
        out_specs=[spec] * 4)
    return pl.pallas_call(
        body, grid_spec=grid_spec, out_shape=[SDS(w.shape, f32)] * 4, name=name,
        compiler_params=_params(("parallel",), 36 * tm * cols, 16 * tm * cols))(core, w, g_mine, g_theirs, m, v)


_ANY = pl.BlockSpec(memory_space=pl.ANY)


def _place():
    x, y, c = lax.axis_index("x"), lax.axis_index("y"), lax.axis_index("c")
    chips = [(1 - x, y), (x, 1 - y), (1 - x, 1 - y)]
    return x, y, c, chips


def _halved(t):
    return t.reshape(t.shape[:-2] + (2, t.shape[-2] // 2, t.shape[-1]))


def _gather_body(src, out, send_ici, recv_ici, send_d2d, recv_d2d):
    x, y, c, chips = _place()
    sibling = (x, y, 1 - c)
    me_j = 2 * x + y
    sends = []
    for a in range(len(src)):
        for p in range(3):
            cp = pltpu.make_async_remote_copy(
                src_ref=src[a].at[c], dst_ref=out[a].at[me_j, c], send_sem=send_ici.at[a, p],
                recv_sem=recv_ici.at[a, p], device_id=(*chips[p], c), device_id_type=MESH)
            cp.start()
            sends.append(cp)
    for a in range(len(src)):
        for p, (px, py) in enumerate(chips):
            blk = out[a].at[2 * px + py, c]
            pltpu.make_async_remote_copy(
                src_ref=blk, dst_ref=blk, send_sem=send_ici.at[a, p], recv_sem=recv_ici.at[a, p],
                device_id=sibling, device_id_type=MESH).wait_recv()
            fw = pltpu.make_async_remote_copy(
                src_ref=blk, dst_ref=blk, send_sem=send_d2d.at[a, p], recv_sem=recv_d2d.at[a, p],
                device_id=sibling, device_id_type=MESH)
            fw.start()
            sends.append(fw)
    for a in range(len(src)):
        for p, (px, py) in enumerate(chips):
            blk = out[a].at[2 * px + py, 1 - c]
            pltpu.make_async_remote_copy(
                src_ref=blk, dst_ref=blk, send_sem=send_d2d.at[a, p], recv_sem=recv_d2d.at[a, p],
                device_id=sibling, device_id_type=MESH).wait_recv()
    for cp in sends:
        cp.wait_send()


def _handshake(peers):
    barrier = pltpu.get_barrier_semaphore()
    for peer in peers:
        pl.semaphore_signal(barrier, inc=1, device_id=peer, device_id_type=MESH)
    pl.semaphore_wait(barrier, len(peers))


_SEQUENCER = dict(axis_name="sequencer", num_cores=1)
GATHER_LATE_ID, SCATTER_EARLY_ID, SWAP_EARLY_ID, GATHER_FIRST_ID, SCATTER_LATE_ID = 1, 2, 3, 4, 5


def _all_gather_async(shards, after, name, collective_id):
    n, k = len(shards), len(after)

    def body(*refs):
        x, y, c, chips = _place()
        _handshake([(*chip, c) for chip in chips] + [(x, y, 1 - c)])
        _gather_body(refs[:n], refs[n + k:2 * n + k], *refs[2 * n + k:])

    return pl.kernel(
        body, out_type=[SDS((N_SHARD,) + t.shape, t.dtype) for t in shards],
        mesh=plsc.ScalarSubcoreMesh(**_SEQUENCER), scratch_types=[pltpu.SemaphoreType.DMA((n, 3))] * 4,
        compiler_params=pltpu.CompilerParams(collective_id=collective_id), name=name)(*shards, *after)


def _pair_swap(grads):
    n = len(grads)

    def body(*refs):
        src, out, send_sems, recv_sems = refs[:n], refs[n:2 * n], refs[2 * n], refs[2 * n + 1]
        x, y, c, _ = _place()
        copies = [pltpu.make_async_remote_copy(
            src_ref=src[a].at[:, 1 - c], dst_ref=out[a], send_sem=send_sems.at[a], recv_sem=recv_sems.at[a],
            device_id=(x, y, 1 - c), device_id_type=MESH) for a in range(n)]
        for cp in copies:
            cp.start()
        for cp in copies:
            cp.wait()

    return pl.pallas_call(
        body, in_specs=[_ANY] * n, out_specs=[_ANY] * n,
        out_shape=[SDS((N_SHARD,) + t.shape[2:], t.dtype) for t in grads],
        scratch_shapes=[pltpu.SemaphoreType.DMA((n,)), pltpu.SemaphoreType.DMA((n,))], name="pair_swap",
        compiler_params=pltpu.CompilerParams(has_side_effects=True))(*grads)


def _pair_swap_early(grads):
    n = len(grads)

    def body(*refs):
        src, out, send_sems, recv_sems = refs[:n], refs[n:2 * n], refs[2 * n], refs[2 * n + 1]
        x, y, c, _ = _place()
        _handshake([(x, y, 1 - c)])
        copies = [pltpu.make_async_remote_copy(
            src_ref=src[a].at[:, 1 - c], dst_ref=out[a], send_sem=send_sems.at[a], recv_sem=recv_sems.at[a],
            device_id=(x, y, 1 - c), device_id_type=MESH) for a in range(n)]
        for cp in copies:
            cp.start()
        for cp in copies:
            cp.wait()

    return pl.kernel(
        body, out_type=[SDS((N_SHARD,) + t.shape[2:], t.dtype) for t in grads],
        mesh=plsc.ScalarSubcoreMesh(**_SEQUENCER), scratch_types=[pltpu.SemaphoreType.DMA((n,))] * 2,
        compiler_params=pltpu.CompilerParams(collective_id=SWAP_EARLY_ID), name="pair_swap_early")(*grads)


def _scatter_early(parts):
    n = len(parts)

    def body(*refs):
        part, recv, send_sems, recv_sems = refs[:n], refs[n:2 * n], refs[2 * n], refs[2 * n + 1]
        x, y, c, chips = _place()
        _handshake([(*chip, c) for chip in chips])
        me_j = 2 * x + y
        sends = []
        for a in range(n):
            for p, (px, py) in enumerate(chips):
                cp = pltpu.make_async_remote_copy(
                    src_ref=part[a].at[2 * px + py], dst_ref=recv[a].at[me_j], send_sem=send_sems.at[a, p],
                    recv_sem=recv_sems.at[a, p], device_id=(px, py, c), device_id_type=MESH)
                cp.start()
                sends.append(cp)
        for a in range(n):
            for p, (px, py) in enumerate(chips):
                slot = recv[a].at[2 * px + py]
                pltpu.make_async_remote_copy(
                    src_ref=slot, dst_ref=slot, send_sem=send_sems.at[a, p], recv_sem=recv_sems.at[a, p],
                    device_id=(px, py, c), device_id_type=MESH).wait_recv()
        for cp in sends:
            cp.wait_send()

    return pl.kernel(
        body, out_type=[SDS(t.shape, t.dtype) for t in parts],
        mesh=plsc.ScalarSubcoreMesh(**_SEQUENCER), scratch_types=[pltpu.SemaphoreType.DMA((n, 3))] * 2,
        compiler_params=pltpu.CompilerParams(collective_id=SCATTER_EARLY_ID), name="scatter_early")(*parts)


def _pair_sum(grads, other, name):
    _, _, rows, cols = grads.shape
    tr = _row_tile(rows)
    core = lax.axis_index("c").astype(jnp.int32).reshape(1)

    def body(c_ref, g_ref, o_ref, out_ref):
        out_ref[...] = (g_ref[...] + o_ref[...]).astype(bf16)

    grid_spec = pltpu.PrefetchScalarGridSpec(
        num_scalar_prefetch=1, grid=(N_SHARD, rows // tr),
        in_specs=[pl.BlockSpec((None, None, tr, cols), lambda j, i, c_ref: (j, c_ref[0], i, 0)),
                  pl.BlockSpec((None, tr, cols), lambda j, i, c_ref: (j, i, 0))],
        out_specs=pl.BlockSpec((None, tr, cols), lambda j, i, c_ref: (j, i, 0)))
    return pl.pallas_call(
        body, grid_spec=grid_spec, out_shape=SDS((N_SHARD, rows, cols), bf16), name=name,
        compiler_params=_params(("parallel", "parallel"), 10 * tr * cols, 12 * tr * cols))(core, grads, other)


def _scatter_partials(parts, small):
    n = len(parts)

    def body(*refs):
        part, small_ref, recv, small_all_ref = refs[:n], refs[n], refs[n + 1:2 * n + 1], refs[2 * n + 1]
        send_sems, recv_sems, ssend, srecv, local_sem = refs[2 * n + 2:]
        x, y, c, chips = _place()
        flip = lambda a, bit: 1 - a if bit else a
        peers = [(flip(x, k & 4), flip(y, k & 2), flip(c, k & 1)) for k in range(1, 8)]
        _handshake(peers)
        me_j = 2 * x + y
        me_dev = 4 * x + 2 * y + c
        own = pltpu.make_async_copy(small_ref, small_all_ref.at[me_dev], local_sem)
        own.start()
        sends = []
        for a in range(n):
            for p, (px, py) in enumerate(chips):
                cp = pltpu.make_async_remote_copy(
                    src_ref=part[a].at[2 * px + py], dst_ref=recv[a].at[me_j], send_sem=send_sems.at[a, p],
                    recv_sem=recv_sems.at[a, p], device_id=(px, py, c), device_id_type=MESH)
                cp.start()
                sends.append(cp)
        for k, to in enumerate(peers):
            cp = pltpu.make_async_remote_copy(
                src_ref=small_ref, dst_ref=small_all_ref.at[me_dev],
                send_sem=ssend.at[k], recv_sem=srecv.at[k], device_id=to, device_id_type=MESH)
            cp.start()
            sends.append(cp)
        for a in range(n):
            for p, (px, py) in enumerate(chips):
                slot = recv[a].at[2 * px + py]
                pltpu.make_async_remote_copy(
                    src_ref=slot, dst_ref=slot, send_sem=send_sems.at[a, p], recv_sem=recv_sems.at[a, p],
                    device_id=(px, py, c), device_id_type=MESH).wait_recv()
        for k, (px, py, pc) in enumerate(peers):
            slot = small_all_ref.at[4 * px + 2 * py + pc]
            pltpu.make_async_remote_copy(
                src_ref=slot, dst_ref=slot, send_sem=ssend.at[k], recv_sem=srecv.at[k],
                device_id=(px, py, pc), device_id_type=MESH).wait_recv()
        for cp in sends:
            cp.wait_send()
        own.wait()

    return pl.kernel(
        body, out_type=[SDS(t.shape, t.dtype) for t in parts] + [SDS((8, SMALL_ROWS, D), f32)],
        mesh=plsc.ScalarSubcoreMesh(**_SEQUENCER),
        scratch_types=[pltpu.SemaphoreType.DMA((n, 3)), pltpu.SemaphoreType.DMA((n, 3)),
                       pltpu.SemaphoreType.DMA((7,)), pltpu.SemaphoreType.DMA((7,)), pltpu.SemaphoreType.DMA],
        compiler_params=pltpu.CompilerParams(collective_id=SCATTER_LATE_ID), name="scatter_partials")(*parts, small)


def _sum_partials(part, recv, name):
    _, rows, cols = recv.shape
    tr = _row_tile(rows)
    me = (2 * lax.axis_index("x") + lax.axis_index("y")).astype(jnp.int32).reshape(1)

    def body(me_ref, mine, r0, r1, r2, r3, out_ref):
        acc = None
        for j, r in enumerate((r0, r1, r2, r3)):
            term = jnp.where(me_ref[0] == j, mine[...], r[...]).astype(f32)
            acc = term if acc is None else acc + term
        out_ref[...] = acc

    slot = lambda j: pl.BlockSpec((None, tr, cols), lambda i, me_ref: (jnp.where(me_ref[0] == j, j ^ 1, j), i, 0))
    grid_spec = pltpu.PrefetchScalarGridSpec(
        num_scalar_prefetch=1, grid=(rows // tr,),
        in_specs=[pl.BlockSpec((None, tr, cols), lambda i, me_ref: (me_ref[0], i, 0)), slot(0), slot(1), slot(2), slot(3)],
        out_specs=pl.BlockSpec((tr, cols), lambda i, me_ref: (i, 0)))
    return pl.pallas_call(
        body, grid_spec=grid_spec, out_shape=SDS((rows, cols), f32), name=name,
        compiler_params=_params(("parallel",), 14 * tr * cols, 12 * tr * cols))(me, part, recv, recv, recv, recv)


def _sum_small(small_all):
    def body(small_ref, out_ref):
        tot = small_ref[0]
        for k in range(1, 8):
            tot = tot + small_ref[k]
        out_ref[...] = tot

    return pl.pallas_call(
        body, grid=(1,), in_specs=[pl.BlockSpec((8, SMALL_ROWS, D), lambda i: (0, 0, 0))],
        out_specs=pl.BlockSpec((SMALL_ROWS, D), lambda i: (0, 0)), out_shape=SDS((SMALL_ROWS, D), f32),
        name="sum_small", compiler_params=_params(("arbitrary",), 36 * SMALL_ROWS * D))(small_all)


def _swap_halves(halves, name):
    n = len(halves)

    def body(*refs):
        src, out, send_sems, recv_sems = refs[:n], refs[n:2 * n], refs[2 * n], refs[2 * n + 1]
        x, y, c, _ = _place()
        copies = [pltpu.make_async_remote_copy(
            src_ref=src[a], dst_ref=out[a], send_sem=send_sems.at[a], recv_sem=recv_sems.at[a],
            device_id=(x, y, 1 - c), device_id_type=MESH) for a in range(n)]
        for cp in copies:
            cp.start()
        for cp in copies:
            cp.wait()

    return pl.pallas_call(
        body, in_specs=[_ANY] * n, out_specs=[_ANY] * n, out_shape=[SDS(t.shape, f32) for t in halves],
        scratch_shapes=[pltpu.SemaphoreType.DMA((n,))] * 2, name=name,
        compiler_params=pltpu.CompilerParams(has_side_effects=True))(*halves)


def _kernel_layout(name, t):
    t = t[0]
    if name in TRANSPOSED:
        t = jnp.swapaxes(t, 0, 1)
    return _pad_rows(t, SHARD_SHAPE[name][0])


def _harness_layout(name, t):
    if name == "w_in":
        t = t[:IN_SHARD]
    if name in TRANSPOSED:
        t = jnp.swapaxes(t, 0, 1)
    return t[None]


def _pad_rows(t, rows):
    return t if t.shape[0] == rows else jnp.pad(t, ((0, rows - t.shape[0]), (0, 0)))


_QA, _KA, _VA, _QB, _F, _GAB = 0, 768, 1536, 2304, 3840, 3848


def _full_weights(gathered):
    full = {n: t.reshape((N_SHARD,) + SHARD_SHAPE[n]) for n, t in gathered.items()}
    out = {}
    if "w_in" in full:
        w_in_t = full["w_in"][:, :IN_SHARD].reshape(IN_COLS, D)
        group = lambda g: jnp.concatenate([w_in_t[o + g * DIL_W:o + (g + 1) * DIL_W] for o in (_QA, _KA, _VA)], axis=0)
        out.update(
            w_a_t=[group(g) for g in range(3)],
            w_vr_t=w_in_t[_QB:_F],
            w_fox_t=[w_in_t[_QB + k * FOX_W:_QB + (k + 1) * FOX_W] for k in range(3)],
            w_f_t=jnp.concatenate([w_in_t[_F:_GAB], jnp.zeros((128 - N_FOX, D), bf16)], axis=0),
            w_gab_t=w_in_t[_GAB:])
    if "w_out" in full:
        out.update(
            w_a4=full["w_proj_a"],
            w_b4=full["w_proj_b"],
            w_out=full["w_out"].reshape(D, D),
            w_gate_t=full["w_ffn_gate"].reshape(F_FF, D),
            w_up_t=full["w_ffn_up"].reshape(F_FF, D),
            w_down=full["w_ffn_down"].reshape(F_FF, D))
    return out


def _sharded_grads(g):
    parts = [g["w_a_t"][k][o:o + DIL_W] for o in (0, DIL_W, 2 * DIL_W) for k in range(3)]
    parts += g["w_fox_t"] + [g["w_f_t"][:N_FOX], g["w_gab_t"]]
    w_in_t = jnp.concatenate(parts, axis=0).reshape(N_SHARD, IN_SHARD, D)
    full = dict(w_in=jnp.pad(w_in_t, ((0, 0), (0, IN_SHARD_PAD - IN_SHARD), (0, 0))), w_proj_a=g["w_a4"],
                w_proj_b=g["w_b4"], w_out=g["w_out"], w_ffn_gate=g["w_gate_t"], w_ffn_up=g["w_up_t"],
                w_ffn_down=g["w_down"])
    return {n: _halved(full[n].reshape((N_SHARD,) + SHARD_SHAPE[n])) for n in W_NAMES}


def _local_step(x, target, wt, b_forget, g_mix_pre, g_mix_post, g_ffn_pre, g_ffn_post, late=None):
    tables = _rope_tables()
    b128 = jnp.pad(b_forget, ((0, 0), (0, 128 - N_FOX)))
    dils = tuple(d for _, d in DIL_GROUPS[1:])

    hs = _norm_fwd([x] + list(_perm_rows([x], dils, "perm_x")), g_mix_pre)
    h1 = hs[0]
    if callable(wt):
        wt = wt(h1)
    qkv = [_rope_fwd(g, _mm([(hs[g], wt["w_a_t"][g])], "nt", f32, tm=1024, tn=QKV_W, name=f"proj_a_{g}"), tables)
           for g in range(3)]
    vr = _mm([(h1, wt["w_vr_t"])], "nt", bf16, tm=1024, tn=VR_W // 2, name="proj_vr")
    gab = _mm([(h1, wt["w_gab_t"])], "nt", f32, tm=512, tn=2 * D, name="proj_gab")
    fz = _mm([(h1, wt["w_f_t"])], "nt", f32, tm=1024, tn=128, name="proj_f")
    dil = [_dil_fwd(g, qkv[g]) for g in range(3)]
    out_a, lse_a = _dil_combine([o for o, _ in dil], [l for _, l in dil])
    f_q, f_k = _forget_fwd(fz, b128)
    out_b, lse_b = _fox_fwd(vr, f_q, f_k)
    if late is not None:
        wt = {**wt, **late(out_b)}
    ya, yb, merged = _merge_fwd(out_a, out_b, wt["w_a4"], wt["w_b4"], gab)
    mix = _mm([(merged, wt["w_out"])], "nn", f32, tm=1024, tn=D, name="proj_out")
    x2, h3 = _resid_norm_fwd(x, mix, g_mix_post, g_ffn_pre)
    g_act, u_act, a_act = _ffn_fwd(h3, wt["w_gate_t"], wt["w_up_t"])
    ff = _mm([(a_act, wt["w_down"])], "nn", f32, tm=1024, tn=D, name="ffn_down")
    sq_err, dy, d_ff, dg_ffn_post = _loss_head(x2, ff, g_ffn_post, target)

    grads = {}
    d_g, d_u = _ffn_bwd_act(d_ff, wt["w_down"], g_act, u_act)
    grads["w_down"] = _mm([(a_act, d_ff)], "tn", f32, tm=FF_TN, tn=512, name="grad_w_down")
    grads["w_gate_t"] = _mm([(d_g, h3)], "tn", f32, tm=FF_TN, tn=512, name="grad_w_gate")
    grads["w_up_t"] = _mm([(d_u, h3)], "tn", f32, tm=FF_TN, tn=512, name="grad_w_up")
    d_h3 = _mm([(d_g, wt["w_gate_t"]), (d_u, wt["w_up_t"])], "nn", f32, tm=512, tn=512, name="ffn_bwd_in")
    dx2, d_mix, dg_ffn_pre, dg_mix_post = _norm_bwd_mid(dy, d_h3, x2, mix, g_ffn_pre, g_mix_post)

    grads["w_out"] = _mm([(merged, d_mix)], "tn", f32, tm=D, tn=D, name="grad_w_out")
    d_merged = _mm([(d_mix, wt["w_out"])], "nt", f32, tm=1024, tn=D, name="proj_out_bwd")
    d_ya, d_yb, d_gab = _merge_bwd(d_merged, ya, yb, gab)
    grads["w_a4"], grads["w_b4"] = _branch_grads(out_a, out_b, d_ya, d_yb)
    d_out_a, delta_a, d_out_b, delta_b = _branch_bwd(d_ya, d_yb, wt["w_a4"], wt["w_b4"], out_a, out_b)

    perm = _perm_rows([d_out_a, delta_a, lse_a], dils, "perm_dil_bwd")
    aux = [(d_out_a, delta_a, lse_a)] + [tuple(perm[k * len(dils) + i] for k in range(3)) for i in range(len(dils))]
    d_qkv = []
    for g in range(3):
        dq, dk, dv = _dil_bwd(g, qkv[g], *aux[g])
        d_qkv.append(_rope_bwd(g, dq, dk, dv, tables))
    *d_fox, d_f_cols, d_f_rows = _fox_bwd(vr, f_q, f_k, lse_b, d_out_b, delta_b)
    d_z, d_b128 = _forget_bwd(fz, b128, d_f_cols, d_f_rows)

    grads["w_a_t"] = [_mm([(d_qkv[g], hs[g])], "tn", f32, tm=QKV_W, tn=D, name=f"grad_w_a_{g}") for g in range(3)]
    grads["w_fox_t"] = [_mm([(d_fox[k], h1)], "tn", f32, tm=FOX_W, tn=D, name=f"grad_w_fox_{k}") for k in range(3)]
    grads["w_gab_t"] = _mm([(d_gab, h1)], "tn", f32, tm=D, tn=D, name="grad_w_gab")
    grads["w_f_t"] = _mm([(d_z, h1)], "tn", f32, tm=128, tn=D, name="grad_w_f")
    d_h1_nat = _mm([(d_qkv[0], wt["w_a_t"][0])] + list(zip(d_fox, wt["w_fox_t"]))
                   + [(d_gab, wt["w_gab_t"]), (d_z, wt["w_f_t"])], "nn", f32, tm=512, tn=512, name="proj_in_bwd")
    d_h1_dil = [_mm([(d_qkv[g], wt["w_a_t"][g])], "nn", f32, tm=1024, tn=D, name=f"proj_a_bwd_{g}") for g in (1, 2)]
    d_h1 = _unperm_sum(d_h1_nat, d_h1_dil, dils, "unperm_d_h1")
    grad_x, dg_mix_pre = _norm_bwd_in(dx2, d_h1, x, g_mix_pre)

    small = dict(b_forget=d_b128[:, :N_FOX], norm_mix_pre=dg_mix_pre, norm_mix_post=dg_mix_post,
                 norm_ffn_pre=dg_ffn_pre, norm_ffn_post=dg_ffn_post)
    grads["mid_backward"] = d_qkv[0]
    return sq_err, grad_x, grads, small


NORMS = ("norm_mix_pre", "norm_mix_post", "norm_ffn_pre", "norm_ffn_post")
ORDER = ("w_in", "w_proj_a", "w_proj_b", "w_out", "b_forget", "w_ffn_gate", "w_ffn_up", "w_ffn_down") + NORMS


def kernel(x, w_in, w_proj_a, w_proj_b, w_out, b_forget, w_ffn_gate, w_ffn_up, w_ffn_down, norm_mix_pre, norm_mix_post, norm_ffn_pre, norm_ffn_post, loss_target, m_w_in, m_w_proj_a, m_w_proj_b, m_w_out, m_b_forget, m_w_ffn_gate, m_w_ffn_up, m_w_ffn_down, m_norm_mix_pre, m_norm_mix_post, m_norm_ffn_pre, m_norm_ffn_post, v_w_in, v_w_proj_a, v_w_proj_b, v_w_out, v_b_forget, v_w_ffn_gate, v_w_ffn_up, v_w_ffn_down, v_norm_mix_pre, v_norm_mix_post, v_norm_ffn_pre, v_norm_ffn_post):
    given = dict(w_in=w_in, w_proj_a=w_proj_a, w_proj_b=w_proj_b, w_out=w_out, w_ffn_gate=w_ffn_gate,
                 w_ffn_up=w_ffn_up, w_ffn_down=w_ffn_down)
    given_m = dict(w_in=m_w_in, w_proj_a=m_w_proj_a, w_proj_b=m_w_proj_b, w_out=m_w_out, w_ffn_gate=m_w_ffn_gate,
                   w_ffn_up=m_w_ffn_up, w_ffn_down=m_w_ffn_down)
    given_v = dict(w_in=v_w_in, w_proj_a=v_w_proj_a, w_proj_b=v_w_proj_b, w_out=v_w_out, w_ffn_gate=v_w_ffn_gate,
                   w_ffn_up=v_w_ffn_up, w_ffn_down=v_w_ffn_down)
    w, m, v = ({n: _kernel_layout(n, t[n]) for n in W_NAMES} for t in (given, given_m, given_v))
    small_w = dict(b_forget=b_forget, norm_mix_pre=norm_mix_pre, norm_mix_post=norm_mix_post,
                   norm_ffn_pre=norm_ffn_pre, norm_ffn_post=norm_ffn_post)
    small_m = dict(b_forget=m_b_forget, norm_mix_pre=m_norm_mix_pre, norm_mix_post=m_norm_mix_post,
                   norm_ffn_pre=m_norm_ffn_pre, norm_ffn_post=m_norm_ffn_post)
    small_v = dict(b_forget=v_b_forget, norm_mix_pre=v_norm_mix_pre, norm_mix_post=v_norm_mix_post,
                   norm_ffn_pre=v_norm_ffn_pre, norm_ffn_post=v_norm_ffn_post)

    own = [_halved(w[n].astype(bf16)) for n in W_NAMES]
    chip = 2 * lax.axis_index("x") + lax.axis_index("y")
    exchanged = {"first": _all_gather_async(own[:1], [], "all_gather_first", GATHER_FIRST_ID)}
    fill = lambda ts, mine: [lax.dynamic_update_index_in_dim(t, o, chip, 0) for t, o in zip(ts, mine)]

    def first_weights(ready):
        arrived, _ = lax.optimization_barrier((list(exchanged["first"]), ready))
        exchanged["late"] = _all_gather_async(own[1:], [arrived[0][0, 0, :16, :128]], "all_gather_late", GATHER_LATE_ID)
        return _full_weights(dict(zip(W_NAMES[:1], fill(arrived, own[:1]))))

    def late_weights(ready):
        arrived, _ = lax.optimization_barrier((list(exchanged["late"]), ready))
        return _full_weights(dict(zip(W_NAMES[1:], fill(arrived, own[1:]))))

    sq_err, grad_x, grads, small = _local_step(x[0], loss_target[0], first_weights, b_forget, norm_mix_pre,
                                               norm_mix_post, norm_ffn_pre, norm_ffn_post, late=late_weights)

    g4 = _sharded_grads(grads)
    stack = lambda t, extra: jnp.concatenate(
        [jnp.pad(t["b_forget"], ((0, 0), (0, D - N_FOX)))] + [t[n] for n in NORMS]
        + [jnp.pad(extra, ((0, SMALL_ROWS - LOSS_ROW - 1), (0, D - extra.shape[1])), constant_values=1.0)], axis=0)
    early, _ = lax.optimization_barrier((list(_pair_swap_early([g4[n] for n in W_NAMES[1:]])), grads["mid_backward"]))
    other = list(_pair_swap([g4["w_in"]])) + early
    parts = [_pair_sum(g4[n], o, "pair_sum_" + n) for n, o in zip(W_NAMES, other)]
    recv_early = _scatter_early(parts[1:])
    recv_in, small_all = _scatter_partials(parts[:1], stack(small, sq_err))

    g_shard, delta, new_m, new_v = {}, {}, {}, {}

    def finish(names, parts, recv):
        halves = [_sum_partials(p, r, "sum_partials_" + n) for n, p, r in zip(names, parts, recv)]
        theirs = _swap_halves(halves, "swap_halves_" + names[0])
        for n, mine, other_half in zip(names, halves, theirs):
            g_shard[n], delta[n], new_m[n], new_v[n] = _adamw_halves(w[n], mine, other_half, m[n], v[n], "adamw_" + n)

    recv_early, _ = lax.optimization_barrier((list(recv_early), parts[0]))
    finish(W_NAMES[1:], parts[1:], recv_early)
    (recv_in, small_all), _ = lax.optimization_barrier(((recv_in, small_all), [delta[n] for n in W_NAMES[1:]]))
    finish(W_NAMES[:1], parts[:1], [recv_in])
    small_sum = _sum_small(small_all)
    loss = small_sum[LOSS_ROW, 0] * (0.5 / D)
    ones = jnp.ones((1, 128), f32)
    sd, sm, sv = _adamw(stack(small_w, ones), small_sum, stack(small_m, ones), stack(small_v, ones), "adamw_small")

    outs = [loss, grad_x[None]]
    for big, st in ((g_shard, small_sum), (delta, sd), (new_m, sm), (new_v, sv)):
        t = {n: _harness_layout(n, big[n]) for n in W_NAMES}
        t["b_forget"] = st[0:1, :N_FOX]
        for i, n in enumerate(NORMS):
            t[n] = st[i + 1:i + 2]
        outs += [t[n] for n in ORDER]
    return tuple(outs)
```

```python
import functools
import math

import jax
import jax.numpy as jnp
import numpy as np
from jax import lax
from jax.experimental import pallas as pl
from jax.experimental.pallas import tpu as pltpu
from jax.experimental.pallas import tpu_sc as plsc

f32 = jnp.float32
bf16 = jnp.bfloat16
SDS = jax.ShapeDtypeStruct
MESH = pl.DeviceIdType.MESH

S = 2048
D = 1024
HD = 64
BLK = 128
N_FOX = 8
FOX_W = N_FOX * HD
DIL_GROUPS = ((128, 1), (512, 4), (2048, 16))
SLOTS = 4
DIL_W = SLOTS * HD
QKV_W = 3 * DIL_W
VR_W = 3 * FOX_W
GF_W = 2 * D + 128
F_FF = 2816
ROPE_DIM = 16
ROPE_THETA = 500000.0
EPS = 1e-6
NEG = -1e30
SCALE = 1.0 / math.sqrt(HD)
IN_COLS = 5896
N_SHARD = 4

ADAM_LR, ADAM_B1, ADAM_B2, ADAM_EPS, ADAM_WD, ADAM_STEP = 0.001, 0.9, 0.999, 1e-08, 0.01, 10

VMEM_V7X = 64 * 1024 * 1024
VMEM_PLAN_MAX = VMEM_V7X - 8 * 1024 * 1024

TM = 256
TQ = 256

W_NAMES = ("w_in", "w_proj_a", "w_proj_b", "w_out", "w_ffn_gate", "w_ffn_up", "w_ffn_down")
TRANSPOSED = ("w_in", "w_ffn_gate", "w_ffn_up")
IN_SHARD = IN_COLS // N_SHARD
IN_SHARD_PAD = 1504
SHARD_SHAPE = dict(w_in=(IN_SHARD_PAD, D), w_proj_a=(DIL_W, D // N_SHARD), w_proj_b=(FOX_W, D // N_SHARD),
                   w_out=(D // N_SHARD, D), w_ffn_gate=(F_FF // N_SHARD, D), w_ffn_up=(F_FF // N_SHARD, D),
                   w_ffn_down=(F_FF // N_SHARD, D))
SMALL_ROWS = 8
LOSS_ROW = 5


def _nbytes(shape, dtype):
    return math.prod(shape) * jnp.dtype(dtype).itemsize


def _params(semantics, block_bytes, temp_bytes=0):
    need = 2 * block_bytes + temp_bytes + (2 << 20)
    return pltpu.CompilerParams(dimension_semantics=semantics, vmem_limit_bytes=int(min(need, VMEM_PLAN_MAX)))


def _row(w, tm=TM):
    return pl.BlockSpec((tm, w), lambda i: (i, 0))


def _vec(w):
    return pl.BlockSpec((1, w), lambda i: (0, 0))


def _mm(pairs, dims, out_dtype, *, tm, tn, name, m_inner=False):
    a0, b0 = pairs[0]
    m_dim = a0.shape[1] if dims == "tn" else a0.shape[0]
    n_dim = b0.shape[0] if dims == "nt" else b0.shape[1]
    contract = {"nn": ((1,), (0,)), "nt": ((1,), (1,)), "tn": ((0,), (0,))}[dims]
    n_pairs = len(pairs)
    assert m_dim % tm == 0 and n_dim % tn == 0, (name, m_dim, n_dim, tm, tn)

    def body(*refs):
        o_ref = refs[-1]
        acc = None
        for p in range(n_pairs):
            a = refs[2 * p][...].astype(bf16)
            b = refs[2 * p + 1][...].astype(bf16)
            t = lax.dot_general(a, b, (contract, ((), ())), preferred_element_type=f32)
            acc = t if acc is None else acc + t
        o_ref[...] = acc.astype(o_ref.dtype)

    if m_inner:
        grid = (n_dim // tn, m_dim // tm)
        mi = lambda j, i: i
        ni = lambda j, i: j
    else:
        grid = (m_dim // tm, n_dim // tn)
        mi = lambda i, j: i
        ni = lambda i, j: j
    in_specs, block_bytes, args = [], 0, []
    for a, b in pairs:
        k_dim = a.shape[0] if dims == "tn" else a.shape[1]
        if dims == "tn":
            in_specs.append(pl.BlockSpec((k_dim, tm), lambda *g: (0, mi(*g))))
        else:
            in_specs.append(pl.BlockSpec((tm, k_dim), lambda *g: (mi(*g), 0)))
        if dims == "nt":
            in_specs.append(pl.BlockSpec((tn, k_dim), lambda *g: (ni(*g), 0)))
        else:
            in_specs.append(pl.BlockSpec((k_dim, tn), lambda *g: (0, ni(*g))))
        block_bytes += _nbytes((tm, k_dim), a.dtype) + _nbytes((tn, k_dim), b.dtype)
        args += [a, b]
    block_bytes += _nbytes((tm, tn), out_dtype)
    temp = _nbytes((tm, tn), f32) * 2 + sum(_nbytes((tm, a.shape[0] if dims == "tn" else a.shape[1]), bf16)
                                            + _nbytes((tn, a.shape[0] if dims == "tn" else a.shape[1]), bf16)
                                            for a, _ in pairs)
    return pl.pallas_call(
        body, grid=grid, in_specs=in_specs,
        out_specs=pl.BlockSpec((tm, tn), lambda *g: (mi(*g), ni(*g))),
        out_shape=SDS((m_dim, n_dim), out_dtype), name=name,
        compiler_params=_params(("parallel", "parallel"), block_bytes, temp),
    )(*args)


def _rms(x, g):
    r = lax.rsqrt(jnp.mean(x * x, axis=-1, keepdims=True) + EPS)
    return x * r * g


def _rms_bwd(x, g, dy):
    r = lax.rsqrt(jnp.mean(x * x, axis=-1, keepdims=True) + EPS)
    xh = x * r
    dxh = dy * g
    dx = r * (dxh - xh * jnp.mean(dxh * xh, axis=-1, keepdims=True))
    return dx, jnp.sum(dy * xh, axis=0, keepdims=True)


def _acc_rows(ref, val):
    @pl.when(pl.program_id(0) == 0)
    def _():
        ref[...] = jnp.zeros_like(ref)
    ref[...] += val


def _norm_fwd(xs, g):
    n = len(xs)

    def body(*refs):
        g = refs[n][...]
        for x_ref, h_ref in zip(refs[:n], refs[n + 1:]):
            h_ref[...] = _rms(x_ref[...], g).astype(bf16)

    return pl.pallas_call(
        body, grid=(S // TM,), in_specs=[_row(D)] * n + [_vec(D)], out_specs=[_row(D)] * n,
        out_shape=[SDS((S, D), bf16)] * n, name="norm_mix_pre",
        compiler_params=_params(("parallel",), 6 * n * TM * D, 8 * n * TM * D))(*xs, g)


def _perm_rows(xs, ds, name):
    n = len(xs)

    def body(*refs):
        outs = iter(refs[n:])
        for x_ref in refs[:n]:
            for d in ds:
                o_ref, rows = next(outs), S // d
                for r in range(d):
                    o_ref[r * rows:(r + 1) * rows, :] = x_ref[pl.ds(r, rows, stride=d), :]

    blk = pl.BlockSpec((S, 128), lambda c: (0, c))
    w = xs[0].shape[1]
    return pl.pallas_call(
        body, grid=(w // 128,), in_specs=[blk] * n, out_specs=[blk] * (n * len(ds)),
        out_shape=[SDS((S, w), f32)] * (n * len(ds)), name=name,
        compiler_params=_params(("parallel",), 4 * S * 128 * n * (1 + len(ds))))(*xs)


def _unperm_sum(nat, perms, ds, name):
    n = len(perms)

    def body(*refs):
        a_ref, o_ref, sc = refs[0], refs[n + 1], refs[n + 2]
        acc = a_ref[...]
        for b_ref, d in zip(refs[1:n + 1], ds):
            rows = S // d
            for r in range(d):
                sc[pl.ds(r, rows, stride=d), :] = b_ref[r * rows:(r + 1) * rows, :]
            acc = acc + sc[...]
        o_ref[...] = acc

    blk = pl.BlockSpec((S, 128), lambda c: (0, c))
    w = nat.shape[1]
    return pl.pallas_call(
        body, grid=(w // 128,), in_specs=[blk] * (n + 1), out_specs=blk, out_shape=SDS((S, w), f32),
        scratch_shapes=[pltpu.VMEM((S, 128), f32)], name=name,
        compiler_params=_params(("parallel",), 4 * S * 128 * (n + 2), 8 * S * 128))(nat, *perms)


def _whole(a):
    return pl.BlockSpec(a.shape, lambda i: (0,) * a.ndim)


def _resid_norm_fwd(x, merged, w_out, g_post, g_pre):
    def body(x_ref, mg_ref, w_ref, gp_ref, gn_ref, mix_ref, x2_ref, h_ref):
        mix = jnp.dot(mg_ref[...], w_ref[...], preferred_element_type=f32)
        x2 = x_ref[...] + _rms(mix, gp_ref[...])
        mix_ref[...] = mix
        x2_ref[...] = x2
        h_ref[...] = _rms(x2, gn_ref[...]).astype(bf16)

    return pl.pallas_call(
        body, grid=(S // TM,), in_specs=[_row(D), _row(D), _whole(w_out), _vec(D), _vec(D)], out_specs=[_row(D)] * 3,
        out_shape=[SDS((S, D), f32), SDS((S, D), f32), SDS((S, D), bf16)], name="proj_out_norm",
        compiler_params=_params(("parallel",), 16 * TM * D + 2 * D * D, 16 * TM * D))(x, merged, w_out, g_post, g_pre)


def _loss_head(x2, a_act, w_down, g_post, target):
    def body(x2_ref, a_ref, w_ref, g_ref, t_ref, loss_ref, dy_ref, dff_ref, dg_ref):
        ff = jnp.dot(a_ref[...], w_ref[...], preferred_element_type=f32)
        g = g_ref[...]
        err = x2_ref[...] + _rms(ff, g) - t_ref[...]
        dy = err * (1.0 / D)
        dff, dg = _rms_bwd(ff, g, dy)
        dy_ref[...] = dy
        dff_ref[...] = dff.astype(bf16)
        _acc_rows(dg_ref, dg)
        _acc_rows(loss_ref, jnp.full((1, 128), jnp.sum(err * err), f32))

    return pl.pallas_call(
        body, grid=(S // TM,), in_specs=[_row(D), _row(F_FF), _whole(w_down), _vec(D), _row(D)],
        out_specs=[_vec(128), _row(D), _row(D), _vec(D)],
        out_shape=[SDS((1, 128), f32), SDS((S, D), f32), SDS((S, D), bf16), SDS((1, D), f32)], name="ffn_down_loss",
        compiler_params=_params(("arbitrary",), 14 * TM * D + 2 * TM * F_FF + 2 * F_FF * D, 28 * TM * D),
    )(x2, a_act, w_down, g_post, target)


def _norm_bwd_mid(dy, d_g, d_u, w_gate_t, w_up_t, x2, mix, g_ffn_pre, g_mix_post):
    def body(dy_ref, dgt_ref, dut_ref, wg_ref, wu_ref, x2_ref, mix_ref, g3_ref, g2_ref, dx2_ref, dmix_ref, dg3_ref, dg2_ref):
        dh = jnp.dot(dgt_ref[...], wg_ref[...], preferred_element_type=f32)
        dh += jnp.dot(dut_ref[...], wu_ref[...], preferred_element_type=f32)
        d3, dg3 = _rms_bwd(x2_ref[...], g3_ref[...], dh)
        dx2 = dy_ref[...] + d3
        dmix, dg2 = _rms_bwd(mix_ref[...], g2_ref[...], dx2)
        dx2_ref[...] = dx2
        dmix_ref[...] = dmix.astype(bf16)
        _acc_rows(dg3_ref, dg3)
        _acc_rows(dg2_ref, dg2)

    return pl.pallas_call(
        body, grid=(S // TM,),
        in_specs=[_row(D), _row(F_FF), _row(F_FF), _whole(w_gate_t), _whole(w_up_t), _row(D), _row(D), _vec(D), _vec(D)],
        out_specs=[_row(D), _row(D), _vec(D), _vec(D)],
        out_shape=[SDS((S, D), f32), SDS((S, D), bf16), SDS((1, D), f32), SDS((1, D), f32)], name="ffn_bwd_in_norm",
        compiler_params=_params(("arbitrary",), 18 * TM * D + 4 * TM * F_FF + 4 * F_FF * D, 28 * TM * D),
    )(dy, d_g, d_u, w_gate_t, w_up_t, x2, mix, g_ffn_pre, g_mix_post)


def _norm_bwd_in(dx2, dh1, x, g):
    def body(dx2_ref, dh_ref, x_ref, g_ref, gx_ref, dg_ref):
        d1, dg = _rms_bwd(x_ref[...], g_ref[...], dh_ref[...])
        gx_ref[...] = dx2_ref[...] + d1
        _acc_rows(dg_ref, dg)

    return pl.pallas_call(
        body, grid=(S // TM,), in_specs=[_row(D)] * 3 + [_vec(D)], out_specs=[_row(D), _vec(D)],
        out_shape=[SDS((S, D), f32), SDS((1, D), f32)], name="norm_bwd_in",
        compiler_params=_params(("arbitrary",), 16 * TM * D, 16 * TM * D))(dx2, dh1, x, g)


def _rope_tables():
    half = ROPE_DIM // 2
    inv_freq = np.power(np.float32(ROPE_THETA), -np.arange(0, ROPE_DIM, 2, dtype=np.float32) / np.float32(ROPE_DIM))
    row = np.arange(S)
    groups = []
    for _, d in DIL_GROUPS:
        pos = ((row % (S // d)) * d + row // (S // d)).astype(np.float32)
        ang = pos[:, None] * inv_freq[None, :].astype(np.float32)
        cos, sin = np.cos(ang).astype(np.float32), np.sin(ang).astype(np.float32)
        c = np.concatenate([cos, cos, np.ones((S, HD - ROPE_DIM), np.float32)], axis=1)
        s_lo = np.concatenate([-sin, np.zeros((S, HD - half), np.float32)], axis=1)
        s_hi = np.concatenate([np.zeros((S, half), np.float32), sin, np.zeros((S, HD - ROPE_DIM), np.float32)], axis=1)
        groups.append(np.stack([np.concatenate([t, t], axis=1) for t in (c, s_lo, s_hi)]))
    return jnp.asarray(np.stack(groups))


def _rotate(x, c, lo, hi, sign):
    tile = lambda t: jnp.tile(t, (1, DIL_W // 128))
    return (x * tile(c) + pltpu.roll(x, DIL_W - ROPE_DIM // 2, 1) * (tile(lo) * sign)
            + pltpu.roll(x, ROPE_DIM // 2, 1) * (tile(hi) * sign))


def _table_specs(g):
    return [pl.BlockSpec((None, None, TM, 128), lambda i, k=k: (g, k, i, 0)) for k in range(3)]


def _rope_fwd(g, p_qkv, tables):
    def body(x_ref, c_ref, lo_ref, hi_ref, o_ref):
        c, lo, hi = c_ref[...], lo_ref[...], hi_ref[...]
        for part in range(2):
            cols = slice(part * DIL_W, (part + 1) * DIL_W)
            o_ref[:, cols] = _rotate(x_ref[:, cols], c, lo, hi, 1.0).astype(bf16)
        o_ref[:, 2 * DIL_W:] = x_ref[:, 2 * DIL_W:].astype(bf16)

    return pl.pallas_call(
        body, grid=(S // TM,), in_specs=[_row(QKV_W)] + _table_specs(g), out_specs=_row(QKV_W),
        out_shape=SDS((S, QKV_W), bf16), name=f"rope_fwd_{g}",
        compiler_params=_params(("parallel",), 6 * TM * QKV_W + 12 * TM * 128, 24 * TM * QKV_W))(p_qkv, tables, tables, tables)


def _rope_bwd(g, dq, dk, dv, tables):
    def body(dq_ref, dk_ref, dv_ref, c_ref, lo_ref, hi_ref, o_ref):
        c, lo, hi = c_ref[...], lo_ref[...], hi_ref[...]
        o_ref[:, :DIL_W] = _rotate(dq_ref[...], c, lo, hi, -1.0).astype(bf16)
        o_ref[:, DIL_W:2 * DIL_W] = _rotate(dk_ref[...], c, lo, hi, -1.0).astype(bf16)
        o_ref[:, 2 * DIL_W:] = dv_ref[...].astype(bf16)

    return pl.pallas_call(
        body, grid=(S // TM,), in_specs=[_row(DIL_W)] * 3 + _table_specs(g), out_specs=_row(QKV_W),
        out_shape=SDS((S, QKV_W), bf16), name=f"rope_bwd_{g}",
        compiler_params=_params(("parallel",), 6 * TM * QKV_W + 12 * TM * 128, 24 * TM * QKV_W))(dq, dk, dv, tables, tables, tables)


def _nt(a, b):
    return lax.dot_general(a, b, (((1,), (1,)), ((), ())), preferred_element_type=f32)


def _tn(a, b):
    return lax.dot_general(a, b, (((0,), (0,)), ((), ())), preferred_element_type=f32)


STEP_BLOCKS = 4
STEP_ROWS = STEP_BLOCKS * BLK


def _dil_prev(g, b):
    _, d = DIL_GROUPS[g]
    nb = S // d // BLK
    if nb == 1 or (b == 0 and nb <= STEP_BLOCKS):
        return None
    return "in" if b > 0 else "halo"


def _bnt(a, b):
    return lax.dot_general(a, b, (((2,), (2,)), ((0,), (0,))), preferred_element_type=f32)


def _bnn(a, b):
    return lax.dot_general(a, b, (((2,), (1,)), ((0,), (0,))), preferred_element_type=f32)


def _btn(a, b):
    return lax.dot_general(a, b, (((1,), (1,)), ((0,), (0,))), preferred_element_type=f32)


def _on_tail(x, tail, fn):
    if tail == x.shape[0]:
        return fn(x)
    return jnp.concatenate([x[:-tail], fn(x[-tail:])], axis=0)


def _heads(ref, part):
    n = ref.shape[0] // BLK
    return jnp.stack([ref[b * BLK:(b + 1) * BLK, part * DIL_W + h * HD:part * DIL_W + (h + 1) * HD]
                      for b in range(n) for h in range(SLOTS)])


def _dil_operands(g, qkv_ref, halo_ref):
    q, kc, vc = (_heads(qkv_ref, part) for part in range(3))
    qi = lax.broadcasted_iota(jnp.int32, (1, BLK, BLK), 1)
    kj = lax.broadcasted_iota(jnp.int32, (1, BLK, BLK), 2)
    with_prev = [b for b in range(STEP_BLOCKS) if _dil_prev(g, b) is not None]
    tail = SLOTS * len(with_prev)
    if not tail:
        return q, kc, vc, None, None, kj <= qi, None, 0
    assert with_prev == list(range(STEP_BLOCKS - len(with_prev), STEP_BLOCKS))
    inside = SLOTS * sum(_dil_prev(g, b) == "in" for b in with_prev)
    kp, vp, prev = kc[:inside], vc[:inside], jnp.broadcast_to(kj >= qi, (inside, BLK, BLK))
    if inside < tail:
        no_halo = jnp.where(pl.program_id(0) == 0, BLK + 1, 0)
        kp = jnp.concatenate([_heads(halo_ref, 1), kp], axis=0)
        vp = jnp.concatenate([_heads(halo_ref, 2), vp], axis=0)
        prev = jnp.concatenate([jnp.broadcast_to(kj >= qi + no_halo, (SLOTS, BLK, BLK)), prev], axis=0)
    return q, kc, vc, kp, vp, kj <= qi, prev, tail


def _dil_in_specs(g, n_aux):
    step = lambda w: pl.BlockSpec((STEP_ROWS, w), lambda i: (i, 0))
    halo = [pl.BlockSpec((BLK, QKV_W), lambda i: (jnp.maximum(i * STEP_BLOCKS - 1, 0), 0))]
    needs_halo = _dil_prev(g, 0) == "halo"
    return [step(QKV_W)] + (halo if needs_halo else []) + [step(DIL_W)] * n_aux, needs_halo


def _dil_fwd(g, qkv):
    in_specs, needs_halo = _dil_in_specs(g, 0)

    def body(*refs):
        qkv_ref, halo_ref = refs[0], refs[1] if needs_halo else None
        o_ref, lse_ref = refs[-2:]
        q, kc, vc, kp, vp, cur, prev, tail = _dil_operands(g, qkv_ref, halo_ref)
        sc = jnp.where(cur, _bnt(q, kc) * SCALE, NEG)
        m = jnp.max(sc, axis=-1, keepdims=True)
        if tail:
            sp = jnp.where(prev, _bnt(q[-tail:], kp) * SCALE, NEG)
            m = _on_tail(m, tail, lambda t: jnp.maximum(t, jnp.max(sp, axis=-1, keepdims=True)))
            pp = jnp.exp(sp - m[-tail:])
        pc = jnp.exp(sc - m)
        den = jnp.sum(pc, axis=-1, keepdims=True)
        if tail:
            den = _on_tail(den, tail, lambda t: t + jnp.sum(pp, axis=-1, keepdims=True))
        inv = 1.0 / den
        o = _bnn((pc * inv).astype(bf16), vc)
        if tail:
            o = _on_tail(o, tail, lambda t: t + _bnn((pp * inv[-tail:]).astype(bf16), vp))
        lse = m + jnp.log(den)
        for b in range(STEP_BLOCKS):
            for h in range(SLOTS):
                rows, hs = slice(b * BLK, (b + 1) * BLK), slice(h * HD, (h + 1) * HD)
                o_ref[rows, hs] = o[SLOTS * b + h]
                lse_ref[rows, hs] = jnp.broadcast_to(lse[SLOTS * b + h], (BLK, HD))

    out = pl.BlockSpec((STEP_ROWS, DIL_W), lambda i: (i, 0))
    return pl.pallas_call(
        body, grid=(S // STEP_ROWS,), in_specs=in_specs, out_specs=[out, out], out_shape=[SDS((S, DIL_W), f32)] * 2,
        name=f"dil_fwd_{g}", compiler_params=_params(("parallel",), 12 * STEP_ROWS * DIL_W, 2 << 20),
    )(*([qkv] * (2 if needs_halo else 1)))


def _dil_combine(outs, lses):
    def body(o0, o1, o2, l0, l1, l2, out_ref, lse_ref, so1, so2, sl1, sl2):
        for (_, d), src, dst in ((DIL_GROUPS[1], o1, so1), (DIL_GROUPS[2], o2, so2),
                                 (DIL_GROUPS[1], l1, sl1), (DIL_GROUPS[2], l2, sl2)):
            rows = S // d
            for r in range(d):
                dst[pl.ds(r, rows, stride=d), :] = src[r * rows:(r + 1) * rows, :]
        a, b, c = l0[...], sl1[...], sl2[...]
        m = jnp.maximum(jnp.maximum(a, b), c)
        ea, eb, ec = jnp.exp(a - m), jnp.exp(b - m), jnp.exp(c - m)
        z = ea + eb + ec
        inv = 1.0 / z
        out_ref[...] = (ea * inv) * o0[...] + (eb * inv) * so1[...] + (ec * inv) * so2[...]
        lse_ref[...] = m + jnp.log(z)

    blk = pl.BlockSpec((S, 128), lambda c: (0, c))
    return pl.pallas_call(
        body, grid=(DIL_W // 128,), in_specs=[blk] * 6, out_specs=[blk] * 2,
        out_shape=[SDS((S, DIL_W), f32)] * 2, scratch_shapes=[pltpu.VMEM((S, 128), f32)] * 4, name="dil_combine",
        compiler_params=_params(("parallel",), 32 * S * 128, 32 * S * 128))(*outs, *lses)


def _dil_bwd(g, qkv, d_out, delta, lse):
    in_specs, needs_halo = _dil_in_specs(g, 3)

    def body(*refs):
        qkv_ref, halo_ref = refs[0], refs[1] if needs_halo else None
        do_ref, dl_ref, lse_ref, dq_ref, dk_ref, dv_ref = refs[-6:]
        q, kc, vc, kp, vp, cur, prev, tail = _dil_operands(g, qkv_ref, halo_ref)
        tiles = [(slice(b * BLK, (b + 1) * BLK), h) for b in range(STEP_BLOCKS) for h in range(SLOTS)]
        do = jnp.stack([do_ref[rows, h * HD:(h + 1) * HD] for rows, h in tiles]).astype(bf16)
        lse = jnp.stack([lse_ref[rows, h * HD:h * HD + 1] for rows, h in tiles])
        delta = jnp.stack([dl_ref[rows, h * HD:h * HD + 1] for rows, h in tiles])

        def probs(q, k, mask, lse, do, v, delta):
            p = jnp.exp(jnp.where(mask, _bnt(q, k) * SCALE, NEG) - lse)
            ds = p * (_bnt(do, v) - delta) * SCALE
            return p.astype(bf16), ds.astype(bf16)

        p, ds = probs(q, kc, cur, lse, do, vc, delta)
        dq, dk, dv = _bnn(ds, kc), _btn(ds, q), _btn(p, do)
        if tail:
            p, ds = probs(q[-tail:], kp, prev, lse[-tail:], do[-tail:], vp, delta[-tail:])
            dq = _on_tail(dq, tail, lambda t: t + _bnn(ds, kp))
            dk_p, dv_p = _btn(ds, q[-tail:]), _btn(p, do[-tail:])
            inside = tail - SLOTS if needs_halo else tail
            pad = jnp.zeros((len(tiles) - inside, BLK, HD), f32)
            dk = dk + jnp.concatenate([dk_p[tail - inside:], pad], axis=0)
            dv = dv + jnp.concatenate([dv_p[tail - inside:], pad], axis=0)
        first = pl.multiple_of(pl.program_id(0) * STEP_ROWS, STEP_ROWS)
        for t, (rows, h) in enumerate(tiles):
            hs = slice(h * HD, (h + 1) * HD)
            own = pl.ds(pl.multiple_of(first + rows.start, BLK), BLK)
            dq_ref[rows, hs] = dq[t]
            dk_ref[own, hs] = dk[t]
            dv_ref[own, hs] = dv[t]
        if needs_halo:
            before = pl.ds(pl.multiple_of(jnp.maximum(first - BLK, 0), BLK), BLK)
            for h in range(SLOTS):
                hs = slice(h * HD, (h + 1) * HD)
                dk_ref[before, hs] += dk_p[h]
                dv_ref[before, hs] += dv_p[h]

    whole = pl.BlockSpec((S, DIL_W), lambda i: (0, 0))
    return pl.pallas_call(
        body, grid=(S // STEP_ROWS,), in_specs=in_specs,
        out_specs=[pl.BlockSpec((STEP_ROWS, DIL_W), lambda i: (i, 0)), whole, whole],
        out_shape=[SDS((S, DIL_W), f32)] * 3, name=f"dil_bwd_{g}",
        compiler_params=_params(("arbitrary",), 20 * STEP_ROWS * DIL_W + 8 * S * DIL_W, 2 << 20),
    )(*([qkv] * (2 if needs_halo else 1)), d_out, delta, lse)


def _scan_rows(x, reverse):
    row = lax.broadcasted_iota(jnp.int32, x.shape, 0)
    k = 1
    while k < S:
        if reverse:
            x = x + jnp.where(row < S - k, pltpu.roll(x, S - k, 0), 0.0)
        else:
            x = x + jnp.where(row >= k, pltpu.roll(x, k, 0), 0.0)
        k *= 2
    return x


N_PAIR = N_FOX // 2
_PAIR_Q = pl.BlockSpec((None, S, 128), lambda p: (p, 0, 0))
_PAIR_K = pl.BlockSpec((None, 8, S), lambda p: (p, 0, 0))


def _forget_fwd(fz, b128):
    def body(z_ref, b_ref, fq_ref, fk_ref):
        z = z_ref[...] + b_ref[...]
        logf = jnp.minimum(z, 0.0) - jnp.log1p(jnp.exp(-jnp.abs(z)))
        f_cum = _scan_rows(logf, reverse=False)
        f_cum_t = f_cum.T
        fq_ref[...] = jnp.zeros_like(fq_ref)
        fk_ref[...] = jnp.zeros_like(fk_ref)
        for p in range(N_PAIR):
            fq_ref[p, :, 0:2] = f_cum[:, 2 * p:2 * p + 2]
            fk_ref[p, 0:2, :] = f_cum_t[2 * p:2 * p + 2, :]

    return pl.pallas_call(
        body, grid=(1,), in_specs=[pl.BlockSpec((S, 128), lambda i: (0, 0)), _vec(128)],
        out_specs=[pl.BlockSpec((N_PAIR, S, 128), lambda i: (0, 0, 0)), pl.BlockSpec((N_PAIR, 8, S), lambda i: (0, 0, 0))],
        out_shape=[SDS((N_PAIR, S, 128), f32), SDS((N_PAIR, 8, S), f32)], name="forget_fwd",
        compiler_params=_params(("arbitrary",), 24 * S * 128, 24 * S * 128))(fz, b128)


def _forget_bwd(fz, b128, d_f_cols, d_f_rows):
    def body(z_ref, b_ref, dfc_ref, dfr_ref, dz_ref, db_ref, df_sc):
        z = z_ref[...] + b_ref[...]
        df_sc[...] = jnp.zeros_like(df_sc)
        for p in range(N_PAIR):
            df_sc[:, 2 * p:2 * p + 2] = dfr_ref[p, :, 0:2] + dfc_ref[p].T[:, 0:2]
        dz = _scan_rows(df_sc[...], reverse=True) * jax.nn.sigmoid(-z)
        dz_ref[...] = dz
        db_ref[...] = jnp.sum(dz, axis=0, keepdims=True)

    full = pl.BlockSpec((S, 128), lambda i: (0, 0))
    return pl.pallas_call(
        body, grid=(1,),
        in_specs=[full, _vec(128), pl.BlockSpec((N_PAIR, 8, S), lambda i: (0, 0, 0)), pl.BlockSpec((N_PAIR, S, 128), lambda i: (0, 0, 0))],
        out_specs=[full, _vec(128)], out_shape=[SDS((S, 128), f32), SDS((1, 128), f32)],
        scratch_shapes=[pltpu.VMEM((S, 128), f32)], name="forget_bwd",
        compiler_params=_params(("arbitrary",), 32 * S * 128, 24 * S * 128))(fz, b128, d_f_cols, d_f_rows)


def _fox_scores(q_ref, k_ref, fq_ref, fk_ref, qi, hh):
    n = (qi + 1) * TQ
    rows, hs = slice(qi * TQ, n), slice(hh * HD, (hh + 1) * HD)
    s = _nt(q_ref[rows, hs], k_ref[0:n, hs]) * SCALE + (fq_ref[rows, hh:hh + 1] - fk_ref[hh:hh + 1, 0:n])
    qpos = qi * TQ + lax.broadcasted_iota(jnp.int32, (TQ, n), 0)
    kpos = lax.broadcasted_iota(jnp.int32, (TQ, n), 1)
    return jnp.where(kpos <= qpos, s, NEG)


def _pair_cols(first):
    return pl.BlockSpec((S, 128), lambda p: (0, first + p))


def _fox_fwd(vr, fq, fk):
    def body(q_ref, k_ref, v_ref, fq_ref, fk_ref, o_ref, lse_ref):
        lse_ref[...] = jnp.zeros_like(lse_ref)
        for hh in range(2):
            hs = slice(hh * HD, (hh + 1) * HD)
            for qi in range(S // TQ):
                n = (qi + 1) * TQ
                rows = slice(qi * TQ, n)
                s = _fox_scores(q_ref, k_ref, fq_ref, fk_ref, qi, hh)
                m = jnp.max(s, axis=-1, keepdims=True)
                p = jnp.exp(s - m)
                den = jnp.sum(p, axis=-1, keepdims=True)
                o_ref[rows, hs] = jnp.dot((p * (1.0 / den)).astype(bf16), v_ref[0:n, hs], preferred_element_type=f32)
                lse_ref[rows, hh:hh + 1] = m + jnp.log(den)

    return pl.pallas_call(
        body, grid=(N_PAIR,), in_specs=[_pair_cols(0), _pair_cols(N_PAIR), _pair_cols(2 * N_PAIR), _PAIR_Q, _PAIR_K],
        out_specs=[_pair_cols(0), _PAIR_Q], out_shape=[SDS((S, FOX_W), f32), SDS((N_PAIR, S, 128), f32)],
        name="fox_fwd", compiler_params=_params(("parallel",), 12 * S * 128, 16 * TQ * S),
    )(vr, vr, vr, fq, fk)


def _fox_bwd(vr, fq, fk, lse, d_out, delta):
    def body(q_ref, k_ref, v_ref, do_ref, fq_ref, fk_ref, lse_ref, dl_ref, dq_ref, dk_ref, dv_ref, dfc_ref, dfr_ref,
             dk_sc, dv_sc):
        dfc_ref[...] = jnp.zeros_like(dfc_ref)
        dfr_ref[...] = jnp.zeros_like(dfr_ref)
        for hh in range(2):
            hs = slice(hh * HD, (hh + 1) * HD)
            dk_sc[...] = jnp.zeros_like(dk_sc)
            dv_sc[...] = jnp.zeros_like(dv_sc)
            for qi in range(S // TQ):
                n = (qi + 1) * TQ
                rows = slice(qi * TQ, n)
                q, do, k, v = q_ref[rows, hs], do_ref[rows, hs], k_ref[0:n, hs], v_ref[0:n, hs]
                p = jnp.exp(_fox_scores(q_ref, k_ref, fq_ref, fk_ref, qi, hh) - lse_ref[rows, hh:hh + 1])
                ds = p * (_nt(do, v) - dl_ref[rows, hh:hh + 1])
                dsb = ds.astype(bf16)
                dq_ref[rows, hs] = jnp.dot(dsb, k, preferred_element_type=f32) * SCALE
                dk_sc[0:n, :] += _tn(dsb, q) * SCALE
                dv_sc[0:n, :] += _tn(p.astype(bf16), do)
                dfc_ref[hh:hh + 1, 0:n] -= jnp.sum(ds, axis=0, keepdims=True)
                dfr_ref[rows, hh:hh + 1] = jnp.sum(ds, axis=-1, keepdims=True)
            dk_ref[:, hs] = dk_sc[...]
            dv_ref[:, hs] = dv_sc[...]

    cols = [_pair_cols(k * N_PAIR) for k in range(3)]
    return pl.pallas_call(
        body, grid=(N_PAIR,), in_specs=cols + [_pair_cols(0), _PAIR_Q, _PAIR_K, _PAIR_Q, _PAIR_Q],
        out_specs=[_pair_cols(0)] * 3 + [_PAIR_K, _PAIR_Q],
        out_shape=[SDS((S, FOX_W), f32)] * 3 + [SDS((N_PAIR, 8, S), f32), SDS((N_PAIR, S, 128), f32)],
        scratch_shapes=[pltpu.VMEM((S, HD), f32)] * 2, name="fox_bwd",
        compiler_params=_params(("parallel",), 32 * S * 128, 24 * TQ * S),
    )(vr, vr, vr, d_out, fq, fk, lse, delta)


def _merge_fwd(out_a, out_b, w_a, w_b, gf):
    cw = D // N_SHARD

    def body(oa_ref, ob_ref, wa_ref, wb_ref, ga_ref, gb_ref, ya_ref, yb_ref, mg_ref):
        oa, ob = oa_ref[...].astype(bf16), ob_ref[...].astype(bf16)
        for j in range(N_SHARD):
            cols = slice(j * cw, (j + 1) * cw)
            ya = jnp.dot(oa, wa_ref[j], preferred_element_type=f32)
            yb = jnp.dot(ob, wb_ref[j], preferred_element_type=f32)
            ya_ref[:, cols] = ya
            yb_ref[:, cols] = yb
            mg_ref[:, cols] = (jax.nn.sigmoid(ga_ref[:, cols]) * ya + jax.nn.sigmoid(gb_ref[:, cols]) * yb).astype(bf16)

    full = lambda a: pl.BlockSpec(a.shape, lambda i: (0, 0, 0))
    return pl.pallas_call(
        body, grid=(S // TM,),
        in_specs=[_row(DIL_W), _row(FOX_W), full(w_a), full(w_b), _row(D), pl.BlockSpec((TM, D), lambda i: (i, 1))],
        out_specs=[_row(D)] * 3, out_shape=[SDS((S, D), f32), SDS((S, D), f32), SDS((S, D), bf16)], name="merge_fwd",
        compiler_params=_params(("parallel",), 22 * TM * D + 2 * (DIL_W + FOX_W) * D, 16 * TM * D),
    )(out_a, out_b, w_a, w_b, gf, gf)


def _merge_bwd(d_mix, w_out, ya, yb, gf):
    def body(dx_ref, w_ref, ya_ref, yb_ref, ga_ref, gb_ref, dya_ref, dyb_ref, dg_ref):
        dm = _nt(dx_ref[...], w_ref[...])
        sa, sb = jax.nn.sigmoid(ga_ref[...]), jax.nn.sigmoid(gb_ref[...])
        dya_ref[...] = (dm * sa).astype(bf16)
        dyb_ref[...] = (dm * sb).astype(bf16)
        dg_ref[:, :D] = (dm * ya_ref[...] * sa * (1.0 - sa)).astype(bf16)
        dg_ref[:, D:] = (dm * yb_ref[...] * sb * (1.0 - sb)).astype(bf16)

    return pl.pallas_call(
        body, grid=(S // TM,),
        in_specs=[_row(D), _whole(w_out)] + [_row(D)] * 3 + [pl.BlockSpec((TM, D), lambda i: (i, 1))],
        out_specs=[_row(D), _row(D), _row(2 * D)],
        out_shape=[SDS((S, D), bf16), SDS((S, D), bf16), SDS((S, 2 * D), bf16)], name="proj_out_bwd_merge",
        compiler_params=_params(("parallel",), 26 * TM * D + 2 * D * D, 28 * TM * D))(d_mix, w_out, ya, yb, gf, gf)


def _branch_bwd(d_ya, d_yb, w_a, w_b, out_a, out_b):
    cw = D // N_SHARD

    def body(dya_ref, dyb_ref, wa_ref, wb_ref, oa_ref, ob_ref, doa_ref, dla_ref, dob_ref, dlb_ref):
        doa = jnp.zeros((TM, DIL_W), f32)
        dob = jnp.zeros((TM, FOX_W), f32)
        for j in range(N_SHARD):
            cols = slice(j * cw, (j + 1) * cw)
            doa += _nt(dya_ref[:, cols], wa_ref[j])
            dob += _nt(dyb_ref[:, cols], wb_ref[j])
        doa_ref[...] = doa
        dob_ref[...] = dob.astype(bf16)
        prod_a = doa * oa_ref[...]
        for h in range(SLOTS):
            hs = slice(h * HD, (h + 1) * HD)
            dla_ref[:, hs] = jnp.broadcast_to(jnp.sum(prod_a[:, hs], axis=-1, keepdims=True), (TM, HD))
        prod_b = dob * ob_ref[...]
        dlb_ref[...] = jnp.zeros_like(dlb_ref)
        for h in range(N_FOX):
            dlb_ref[h // 2, :, h % 2:h % 2 + 1] = jnp.sum(prod_b[:, h * HD:(h + 1) * HD], axis=-1, keepdims=True)

    full = lambda a: pl.BlockSpec(a.shape, lambda i: (0, 0, 0))
    return pl.pallas_call(
        body, grid=(S // TM,),
        in_specs=[_row(D), _row(D), full(w_a), full(w_b), _row(DIL_W), _row(FOX_W)],
        out_specs=[_row(DIL_W), _row(DIL_W), _row(FOX_W), pl.BlockSpec((N_PAIR, TM, 128), lambda i: (0, i, 0))],
        out_shape=[SDS((S, DIL_W), f32), SDS((S, DIL_W), f32), SDS((S, FOX_W), bf16), SDS((N_PAIR, S, 128), f32)],
        name="branch_bwd", compiler_params=_params(("parallel",), 8 * TM * D + 2 * (DIL_W + FOX_W) * D, 8 * TM * D),
    )(d_ya, d_yb, w_a, w_b, out_a, out_b)


def _branch_grads(out_a, out_b, d_ya, d_yb):
    cw = D // N_SHARD

    def body(oa_ref, ob_ref, dya_ref, dyb_ref, ga_ref, gb_ref):
        ga_ref[...] = _tn(oa_ref[...].astype(bf16), dya_ref[...]).astype(bf16)
        gb_ref[...] = _tn(ob_ref[...].astype(bf16), dyb_ref[...]).astype(bf16)

    whole = lambda w: pl.BlockSpec((S, w), lambda j: (0, 0))
    cols = pl.BlockSpec((S, cw), lambda j: (0, j))
    return pl.pallas_call(
        body, grid=(N_SHARD,), in_specs=[whole(DIL_W), whole(FOX_W), cols, cols],
        out_specs=[pl.BlockSpec((None, DIL_W, cw), lambda j: (j, 0, 0)), pl.BlockSpec((None, FOX_W, cw), lambda j: (j, 0, 0))],
        out_shape=[SDS((N_SHARD, DIL_W, cw), bf16), SDS((N_SHARD, FOX_W, cw), bf16)], name="grad_w_proj_ab",
        compiler_params=_params(("parallel",), 4 * S * (DIL_W + FOX_W) + 4 * S * cw + 4 * (DIL_W + FOX_W) * cw,
                                4 * S * (DIL_W + FOX_W)))(out_a, out_b, d_ya, d_yb)


FF_TN = F_FF // 2
FF_TM = 512


def _ffn_fwd(h, w_gate_t, w_up_t):
    def body(h_ref, wg_ref, wu_ref, g_ref, u_ref, a_ref):
        hb = h_ref[...]
        g = _nt(hb, wg_ref[...])
        u = _nt(hb, wu_ref[...])
        g_ref[...] = g
        u_ref[...] = u
        a_ref[...] = (g * jax.nn.sigmoid(g) * u).astype(bf16)

    tile = pl.BlockSpec((FF_TM, FF_TN), lambda j, i: (i, j))
    wspec = pl.BlockSpec((FF_TN, D), lambda j, i: (j, 0))
    return pl.pallas_call(
        body, grid=(F_FF // FF_TN, S // FF_TM),
        in_specs=[pl.BlockSpec((FF_TM, D), lambda j, i: (i, 0)), wspec, wspec], out_specs=[tile] * 3,
        out_shape=[SDS((S, F_FF), f32), SDS((S, F_FF), f32), SDS((S, F_FF), bf16)], name="ffn_fwd",
        compiler_params=_params(("parallel", "parallel"), 2 * FF_TM * D + 4 * D * FF_TN + 10 * FF_TM * FF_TN, 16 * FF_TM * FF_TN),
    )(h, w_gate_t, w_up_t)


def _ffn_bwd_act(d_ff, w_down, g_act, u_act):
    def body(d_ref, wd_ref, g_ref, u_ref, dg_ref, du_ref):
        da = _nt(d_ref[...], wd_ref[...])
        g = g_ref[...]
        sg = jax.nn.sigmoid(g)
        du_ref[...] = (da * g * sg).astype(bf16)
        dg_ref[...] = (da * u_ref[...] * sg * (1.0 + g * (1.0 - sg))).astype(bf16)

    tile = pl.BlockSpec((FF_TM, FF_TN), lambda j, i: (i, j))
    return pl.pallas_call(
        body, grid=(F_FF // FF_TN, S // FF_TM),
        in_specs=[pl.BlockSpec((FF_TM, D), lambda j, i: (i, 0)), pl.BlockSpec((FF_TN, D), lambda j, i: (j, 0)), tile, tile],
        out_specs=[tile, tile], out_shape=[SDS((S, F_FF), bf16)] * 2, name="ffn_bwd_act",
        compiler_params=_params(("parallel", "parallel"), 2 * FF_TM * D + 2 * D * FF_TN + 12 * FF_TM * FF_TN, 16 * FF_TM * FF_TN),
    )(d_ff, w_down, g_act, u_act)


def _row_tile(rows):
    return next(t for t in (376, 128, 176, 64, 32, 16, 8) if rows % t == 0)


def _adamw_math(w, g, m, v):
    c1 = 1.0 - ADAM_B1 ** ADAM_STEP
    c2 = 1.0 - ADAM_B2 ** ADAM_STEP
    m_new = ADAM_B1 * m + (1.0 - ADAM_B1) * g
    v_new = ADAM_B2 * v + (1.0 - ADAM_B2) * (g * g)
    return -ADAM_LR * ((m_new / c1) / (jnp.sqrt(v_new / c2) + ADAM_EPS) + ADAM_WD * w), m_new, v_new


def _adamw(w, g, m, v, name):
    rows, cols = w.shape
    tm = _row_tile(rows)

    def body(w_ref, g_ref, m_ref, v_ref, d_ref, nm_ref, nv_ref):
        d_ref[...], nm_ref[...], nv_ref[...] = _adamw_math(w_ref[...], g_ref[...], m_ref[...], v_ref[...])

    spec = pl.BlockSpec((tm, cols), lambda i: (i, 0))
    return pl.pallas_call(
        body, grid=(rows // tm,), in_specs=[spec] * 4, out_specs=[spec] * 3, out_shape=[SDS(w.shape, f32)] * 3,
        name=name, compiler_params=_params(("parallel",), 28 * tm * cols, 16 * tm * cols))(w, g, m, v)


def _adamw_halves(w, g_mine, g_theirs, m, v, name):
    rows, cols = w.shape
    tm = _row_tile(rows // 2)
    per_half = rows // 2 // tm
    core = lax.axis_index("c").astype(jnp.int32).reshape(1)

    def body(c_ref, w_ref, gm_ref, gt_ref, m_ref, v_ref, g_ref, d_ref, nm_ref, nv_ref):
        mine = pl.program_id(0) // per_half == c_ref[0]
        g = jnp.where(mine, gm_ref[...], gt_ref[...])
        g_ref[...] = g
        d_ref[...], nm_ref[...], nv_ref[...] = _adamw_math(w_ref[...], g, m_ref[...], v_ref[...])

    spec = pl.BlockSpec((tm, cols), lambda i, c_ref: (i, 0))
    in_half = lambda i, first: jnp.clip(i - first * per_half, 0, per_half - 1)
    grid_spec = pltpu.PrefetchScalarGridSpec(
        num_scalar_prefetch=1, grid=(rows // tm,),
        in_specs=[spec, pl.BlockSpec((tm, cols), lambda i, c_ref: (in_half(i, c_ref[0]), 0)),
                  pl.BlockSpec((tm, cols), lambda i, c_ref: (in_half(i, 1 - c_ref[0]), 0)), spec, spec],
        out_specs=[spec] * 4)
    return pl.pallas_call(
        body, grid_spec=grid_spec, out_shape=[SDS(w.shape, f32)] * 4, name=name,
        compiler_params=_params(("parallel",), 36 * tm * cols, 16 * tm * cols))(core, w, g_mine, g_theirs, m, v)


_ANY = pl.BlockSpec(memory_space=pl.ANY)


def _place():
    x, y, c = lax.axis_index("x"), lax.axis_index("y"), lax.axis_index("c")
    chips = [(1 - x, y), (x, 1 - y), (1 - x, 1 - y)]
    return x, y, c, chips


def _halved(t):
    return t.reshape(t.shape[:-2] + (2, t.shape[-2] // 2, t.shape[-1]))


def _gather_body(src, out, send_ici, recv_ici, send_d2d, recv_d2d):
    x, y, c, chips = _place()
    sibling = (x, y, 1 - c)
    me_j = 2 * x + y
    sends = []
    for a in range(len(src)):
        for p in range(3):
            cp = pltpu.make_async_remote_copy(
                src_ref=src[a].at[c], dst_ref=out[a].at[me_j, c], send_sem=send_ici.at[a, p],
                recv_sem=recv_ici.at[a, p], device_id=(*chips[p], c), device_id_type=MESH)
            cp.start()
            sends.append(cp)
    for a in range(len(src)):
        for p, (px, py) in enumerate(chips):
            blk = out[a].at[2 * px + py, c]
            pltpu.make_async_remote_copy(
                src_ref=blk, dst_ref=blk, send_sem=send_ici.at[a, p], recv_sem=recv_ici.at[a, p],
                device_id=sibling, device_id_type=MESH).wait_recv()
            fw = pltpu.make_async_remote_copy(
                src_ref=blk, dst_ref=blk, send_sem=send_d2d.at[a, p], recv_sem=recv_d2d.at[a, p],
                device_id=sibling, device_id_type=MESH)
            fw.start()
            sends.append(fw)
    for a in range(len(src)):
        for p, (px, py) in enumerate(chips):
            blk = out[a].at[2 * px + py, 1 - c]
            pltpu.make_async_remote_copy(
                src_ref=blk, dst_ref=blk, send_sem=send_d2d.at[a, p], recv_sem=recv_d2d.at[a, p],
                device_id=sibling, device_id_type=MESH).wait_recv()
    for cp in sends:
        cp.wait_send()


def _handshake(peers):
    barrier = pltpu.get_barrier_semaphore()
    for peer in peers:
        pl.semaphore_signal(barrier, inc=1, device_id=peer, device_id_type=MESH)
    pl.semaphore_wait(barrier, len(peers))


_SEQUENCER = dict(axis_name="sequencer", num_cores=1)
GATHER_LATE_ID, SCATTER_EARLY_ID, SWAP_EARLY_ID, GATHER_FIRST_ID, SCATTER_LATE_ID = 1, 2, 3, 4, 5


def _all_gather_async(shards, after, name, collective_id):
    n, k = len(shards), len(after)

    def body(*refs):
        x, y, c, chips = _place()
        _handshake([(*chip, c) for chip in chips] + [(x, y, 1 - c)])
        _gather_body(refs[:n], refs[n + k:2 * n + k], *refs[2 * n + k:])

    return pl.kernel(
        body, out_type=[SDS((N_SHARD,) + t.shape, t.dtype) for t in shards],
        mesh=plsc.ScalarSubcoreMesh(**_SEQUENCER), scratch_types=[pltpu.SemaphoreType.DMA((n, 3))] * 4,
        compiler_params=pltpu.CompilerParams(collective_id=collective_id), name=name)(*shards, *after)


def _pair_swap(grads):
    n = len(grads)

    def body(*refs):
        src, out, send_sems, recv_sems = refs[:n], refs[n:2 * n], refs[2 * n], refs[2 * n + 1]
        x, y, c, _ = _place()
        copies = [pltpu.make_async_remote_copy(
            src_ref=src[a].at[:, 1 - c], dst_ref=out[a], send_sem=send_sems.at[a], recv_sem=recv_sems.at[a],
            device_id=(x, y, 1 - c), device_id_type=MESH) for a in range(n)]
        for cp in copies:
            cp.start()
        for cp in copies:
            cp.wait()

    return pl.pallas_call(
        body, in_specs=[_ANY] * n, out_specs=[_ANY] * n,
        out_shape=[SDS((N_SHARD,) + t.shape[2:], t.dtype) for t in grads],
        scratch_shapes=[pltpu.SemaphoreType.DMA((n,)), pltpu.SemaphoreType.DMA((n,))], name="pair_swap",
        compiler_params=pltpu.CompilerParams(has_side_effects=True))(*grads)


def _pair_swap_early(grads):
    n = len(grads)

    def body(*refs):
        src, out, send_sems, recv_sems = refs[:n], refs[n:2 * n], refs[2 * n], refs[2 * n + 1]
        x, y, c, _ = _place()
        _handshake([(x, y, 1 - c)])
        copies = [pltpu.make_async_remote_copy(
            src_ref=src[a].at[:, 1 - c], dst_ref=out[a], send_sem=send_sems.at[a], recv_sem=recv_sems.at[a],
            device_id=(x, y, 1 - c), device_id_type=MESH) for a in range(n)]
        for cp in copies:
            cp.start()
        for cp in copies:
            cp.wait()

    return pl.kernel(
        body, out_type=[SDS((N_SHARD,) + t.shape[2:], t.dtype) for t in grads],
        mesh=plsc.ScalarSubcoreMesh(**_SEQUENCER), scratch_types=[pltpu.SemaphoreType.DMA((n,))] * 2,
        compiler_params=pltpu.CompilerParams(collective_id=SWAP_EARLY_ID), name="pair_swap_early")(*grads)


def _scatter_early(parts):
    n = len(parts)

    def body(*refs):
        part, recv, send_sems, recv_sems = refs[:n], refs[n:2 * n], refs[2 * n], refs[2 * n + 1]
        x, y, c, chips = _place()
        _handshake([(*chip, c) for chip in chips])
        me_j = 2 * x + y
        sends = []
        for a in range(n):
            for p, (px, py) in enumerate(chips):
                cp = pltpu.make_async_remote_copy(
                    src_ref=part[a].at[2 * px + py], dst_ref=recv[a].at[me_j], send_sem=send_sems.at[a, p],
                    recv_sem=recv_sems.at[a, p], device_id=(px, py, c), device_id_type=MESH)
                cp.start()
                sends.append(cp)
        for a in range(n):
            for p, (px, py) in enumerate(chips):
                slot = recv[a].at[2 * px + py]
                pltpu.make_async_remote_copy(
                    src_ref=slot, dst_ref=slot, send_sem=send_sems.at[a, p], recv_sem=recv_sems.at[a, p],
                    device_id=(px, py, c), device_id_type=MESH).wait_recv()
        for cp in sends:
            cp.wait_send()

    return pl.kernel(
        body, out_type=[SDS(t.shape, t.dtype) for t in parts],
        mesh=plsc.ScalarSubcoreMesh(**_SEQUENCER), scratch_types=[pltpu.SemaphoreType.DMA((n, 3))] * 2,
        compiler_params=pltpu.CompilerParams(collective_id=SCATTER_EARLY_ID), name="scatter_early")(*parts)


def _pair_sum(grads, other, name):
    _, _, rows, cols = grads.shape
    tr = _row_tile(rows)
    core = lax.axis_index("c").astype(jnp.int32).reshape(1)

    def body(c_ref, g_ref, o_ref, out_ref):
        out_ref[...] = (g_ref[...].astype(f32) + o_ref[...].astype(f32)).astype(bf16)

    grid_spec = pltpu.PrefetchScalarGridSpec(
        num_scalar_prefetch=1, grid=(N_SHARD, rows // tr),
        in_specs=[pl.BlockSpec((None, None, tr, cols), lambda j, i, c_ref: (j, c_ref[0], i, 0)),
                  pl.BlockSpec((None, tr, cols), lambda j, i, c_ref: (j, i, 0))],
        out_specs=pl.BlockSpec((None, tr, cols), lambda j, i, c_ref: (j, i, 0)))
    return pl.pallas_call(
        body, grid_spec=grid_spec, out_shape=SDS((N_SHARD, rows, cols), bf16), name=name,
        compiler_params=_params(("parallel", "parallel"), 10 * tr * cols, 12 * tr * cols))(core, grads, other)


def _scatter_partials(parts, small):
    n = len(parts)

    def body(*refs):
        part, small_ref, recv, small_all_ref = refs[:n], refs[n], refs[n + 1:2 * n + 1], refs[2 * n + 1]
        send_sems, recv_sems, ssend, srecv, local_sem = refs[2 * n + 2:]
        x, y, c, chips = _place()
        flip = lambda a, bit: 1 - a if bit else a
        peers = [(flip(x, k & 4), flip(y, k & 2), flip(c, k & 1)) for k in range(1, 8)]
        _handshake(peers)
        me_j = 2 * x + y
        me_dev = 4 * x + 2 * y + c
        own = pltpu.make_async_copy(small_ref, small_all_ref.at[me_dev], local_sem)
        own.start()
        sends = []
        for a in range(n):
            for p, (px, py) in enumerate(chips):
                cp = pltpu.make_async_remote_copy(
                    src_ref=part[a].at[2 * px + py], dst_ref=recv[a].at[me_j], send_sem=send_sems.at[a, p],
                    recv_sem=recv_sems.at[a, p], device_id=(px, py, c), device_id_type=MESH)
                cp.start()
                sends.append(cp)
        for k, to in enumerate(peers):
            cp = pltpu.make_async_remote_copy(
                src_ref=small_ref, dst_ref=small_all_ref.at[me_dev],
                send_sem=ssend.at[k], recv_sem=srecv.at[k], device_id=to, device_id_type=MESH)
            cp.start()
            sends.append(cp)
        for a in range(n):
            for p, (px, py) in enumerate(chips):
                slot = recv[a].at[2 * px + py]
                pltpu.make_async_remote_copy(
                    src_ref=slot, dst_ref=slot, send_sem=send_sems.at[a, p], recv_sem=recv_sems.at[a, p],
                    device_id=(px, py, c), device_id_type=MESH).wait_recv()
        for k, (px, py, pc) in enumerate(peers):
            slot = small_all_ref.at[4 * px + 2 * py + pc]
            pltpu.make_async_remote_copy(
                src_ref=slot, dst_ref=slot, send_sem=ssend.at[k], recv_sem=srecv.at[k],
                device_id=(px, py, pc), device_id_type=MESH).wait_recv()
        for cp in sends:
            cp.wait_send()
        own.wait()

    return pl.kernel(
        body, out_type=[SDS(t.shape, t.dtype) for t in parts] + [SDS((8, SMALL_ROWS, D), f32)],
        mesh=plsc.ScalarSubcoreMesh(**_SEQUENCER),
        scratch_types=[pltpu.SemaphoreType.DMA((n, 3)), pltpu.SemaphoreType.DMA((n, 3)),
                       pltpu.SemaphoreType.DMA((7,)), pltpu.SemaphoreType.DMA((7,)), pltpu.SemaphoreType.DMA],
        compiler_params=pltpu.CompilerParams(collective_id=SCATTER_LATE_ID), name="scatter_partials")(*parts, small)


def _sum_partials(part, recv, name):
    _, rows, cols = recv.shape
    tr = _row_tile(rows)
    me = (2 * lax.axis_index("x") + lax.axis_index("y")).astype(jnp.int32).reshape(1)

    def body(me_ref, mine, r0, r1, r2, r3, out_ref):
        acc = None
        for j, r in enumerate((r0, r1, r2, r3)):
            term = jnp.where(me_ref[0] == j, mine[...], r[...]).astype(f32)
            acc = term if acc is None else acc + term
        out_ref[...] = acc

    slot = lambda j: pl.BlockSpec((None, tr, cols), lambda i, me_ref: (jnp.where(me_ref[0] == j, j ^ 1, j), i, 0))
    grid_spec = pltpu.PrefetchScalarGridSpec(
        num_scalar_prefetch=1, grid=(rows // tr,),
        in_specs=[pl.BlockSpec((None, tr, cols), lambda i, me_ref: (me_ref[0], i, 0)), slot(0), slot(1), slot(2), slot(3)],
        out_specs=pl.BlockSpec((tr, cols), lambda i, me_ref: (i, 0)))
    return pl.pallas_call(
        body, grid_spec=grid_spec, out_shape=SDS((rows, cols), f32), name=name,
        compiler_params=_params(("parallel",), 14 * tr * cols, 12 * tr * cols))(me, part, recv, recv, recv, recv)


def _sum_small(small_all):
    def body(small_ref, out_ref):
        tot = small_ref[0]
        for k in range(1, 8):
            tot = tot + small_ref[k]
        out_ref[...] = tot

    return pl.pallas_call(
        body, grid=(1,), in_specs=[pl.BlockSpec((8, SMALL_ROWS, D), lambda i: (0, 0, 0))],
        out_specs=pl.BlockSpec((SMALL_ROWS, D), lambda i: (0, 0)), out_shape=SDS((SMALL_ROWS, D), f32),
        name="sum_small", compiler_params=_params(("arbitrary",), 36 * SMALL_ROWS * D))(small_all)


def _swap_halves(halves, name):
    n = len(halves)

    def body(*refs):
        src, out, send_sems, recv_sems = refs[:n], refs[n:2 * n], refs[2 * n], refs[2 * n + 1]
        x, y, c, _ = _place()
        copies = [pltpu.make_async_remote_copy(
            src_ref=src[a], dst_ref=out[a], send_sem=send_sems.at[a], recv_sem=recv_sems.at[a],
            device_id=(x, y, 1 - c), device_id_type=MESH) for a in range(n)]
        for cp in copies:
            cp.start()
        for cp in copies:
            cp.wait()

    return pl.pallas_call(
        body, in_specs=[_ANY] * n, out_specs=[_ANY] * n, out_shape=[SDS(t.shape, f32) for t in halves],
        scratch_shapes=[pltpu.SemaphoreType.DMA((n,))] * 2, name=name,
        compiler_params=pltpu.CompilerParams(has_side_effects=True))(*halves)


def _kernel_layout(name, t):
    t = t[0]
    if name in TRANSPOSED:
        t = jnp.swapaxes(t, 0, 1)
    return _pad_rows(t, SHARD_SHAPE[name][0])


def _harness_layout(name, t):
    if name == "w_in":
        t = t[:IN_SHARD]
    if name in TRANSPOSED:
        t = jnp.swapaxes(t, 0, 1)
    return t[None]


def _pad_rows(t, rows):
    return t if t.shape[0] == rows else jnp.pad(t, ((0, rows - t.shape[0]), (0, 0)))


_QA, _KA, _VA, _QB, _F, _GAB = 0, 768, 1536, 2304, 3840, 3848


def _full_weights(gathered):
    full = {n: t.reshape((N_SHARD,) + SHARD_SHAPE[n]) for n, t in gathered.items()}
    out = {}
    if "w_in" in full:
        w_in_t = full["w_in"][:, :IN_SHARD].reshape(IN_COLS, D)
        group = lambda g: jnp.concatenate([w_in_t[o + g * DIL_W:o + (g + 1) * DIL_W] for o in (_QA, _KA, _VA)], axis=0)
        out.update(
            w_a_t=[group(g) for g in range(3)],
            w_vr_t=w_in_t[_QB:_F],
            w_fox_t=[w_in_t[_QB + k * FOX_W:_QB + (k + 1) * FOX_W] for k in range(3)],
            w_f_t=jnp.concatenate([w_in_t[_F:_GAB], jnp.zeros((128 - N_FOX, D), bf16)], axis=0),
            w_gab_t=w_in_t[_GAB:])
    if "w_out" in full:
        out.update(
            w_a4=full["w_proj_a"],
            w_b4=full["w_proj_b"],
            w_out=full["w_out"].reshape(D, D),
            w_gate_t=full["w_ffn_gate"].reshape(F_FF, D),
            w_up_t=full["w_ffn_up"].reshape(F_FF, D),
            w_down=full["w_ffn_down"].reshape(F_FF, D))
    return out


def _sharded_grads(g):
    parts = [g["w_a_t"][k][o:o + DIL_W] for o in (0, DIL_W, 2 * DIL_W) for k in range(3)] + g["w_fox_t"]
    tail = jnp.concatenate([g["w_f_t"][:N_FOX], g["w_gab_t"]], axis=0).astype(bf16)
    w_in_t = jnp.concatenate(parts + [tail], axis=0).reshape(N_SHARD, IN_SHARD, D)
    full = dict(w_in=jnp.pad(w_in_t, ((0, 0), (0, IN_SHARD_PAD - IN_SHARD), (0, 0))), w_proj_a=g["w_a4"],
                w_proj_b=g["w_b4"], w_out=g["w_out"], w_ffn_gate=g["w_gate_t"], w_ffn_up=g["w_up_t"],
                w_ffn_down=g["w_down"])
    return {n: _halved(full[n].reshape((N_SHARD,) + SHARD_SHAPE[n])) for n in W_NAMES}


def _local_step(x, target, wt, b_forget, g_mix_pre, g_mix_post, g_ffn_pre, g_ffn_post, late=None):
    tables = _rope_tables()
    b128 = jnp.pad(b_forget, ((0, 0), (0, 128 - N_FOX)))
    dils = tuple(d for _, d in DIL_GROUPS[1:])

    hs = _norm_fwd([x] + list(_perm_rows([x], dils, "perm_x")), g_mix_pre)
    h1 = hs[0]
    if callable(wt):
        wt = wt(h1)
    qkv = [_rope_fwd(g, _mm([(hs[g], wt["w_a_t"][g])], "nt", f32, tm=1024, tn=QKV_W, name=f"proj_a_{g}"), tables)
           for g in range(3)]
    vr = _mm([(h1, wt["w_vr_t"])], "nt", bf16, tm=1024, tn=VR_W // 2, name="proj_vr")
    gab = _mm([(h1, wt["w_gab_t"])], "nt", f32, tm=512, tn=2 * D, name="proj_gab")
    fz = _mm([(h1, wt["w_f_t"])], "nt", f32, tm=1024, tn=128, name="proj_f")
    dil = [_dil_fwd(g, qkv[g]) for g in range(3)]
    out_a, lse_a = _dil_combine([o for o, _ in dil], [l for _, l in dil])
    f_q, f_k = _forget_fwd(fz, b128)
    out_b, lse_b = _fox_fwd(vr, f_q, f_k)
    if late is not None:
        wt = {**wt, **late(out_b)}
    ya, yb, merged = _merge_fwd(out_a, out_b, wt["w_a4"], wt["w_b4"], gab)
    mix, x2, h3 = _resid_norm_fwd(x, merged, wt["w_out"], g_mix_post, g_ffn_pre)
    g_act, u_act, a_act = _ffn_fwd(h3, wt["w_gate_t"], wt["w_up_t"])
    sq_err, dy, d_ff, dg_ffn_post = _loss_head(x2, a_act, wt["w_down"], g_ffn_post, target)

    grads = {}
    d_g, d_u = _ffn_bwd_act(d_ff, wt["w_down"], g_act, u_act)
    grads["w_down"] = _mm([(a_act, d_ff)], "tn", bf16, tm=FF_TN, tn=512, name="grad_w_down")
    grads["w_gate_t"] = _mm([(d_g, h3)], "tn", bf16, tm=FF_TN, tn=512, name="grad_w_gate")
    grads["w_up_t"] = _mm([(d_u, h3)], "tn", bf16, tm=FF_TN, tn=512, name="grad_w_up")
    dx2, d_mix, dg_ffn_pre, dg_mix_post = _norm_bwd_mid(dy, d_g, d_u, wt["w_gate_t"], wt["w_up_t"], x2, mix,
                                                        g_ffn_pre, g_mix_post)

    grads["w_out"] = _mm([(merged, d_mix)], "tn", bf16, tm=D, tn=D, name="grad_w_out")
    d_ya, d_yb, d_gab = _merge_bwd(d_mix, wt["w_out"], ya, yb, gab)
    grads["w_a4"], grads["w_b4"] = _branch_grads(out_a, out_b, d_ya, d_yb)
    d_out_a, delta_a, d_out_b, delta_b = _branch_bwd(d_ya, d_yb, wt["w_a4"], wt["w_b4"], out_a, out_b)

    perm = _perm_rows([d_out_a, delta_a, lse_a], dils, "perm_dil_bwd")
    aux = [(d_out_a, delta_a, lse_a)] + [tuple(perm[k * len(dils) + i] for k in range(3)) for i in range(len(dils))]
    d_qkv = []
    for g in range(3):
        dq, dk, dv = _dil_bwd(g, qkv[g], *aux[g])
        d_qkv.append(_rope_bwd(g, dq, dk, dv, tables))
    *d_fox, d_f_cols, d_f_rows = _fox_bwd(vr, f_q, f_k, lse_b, d_out_b, delta_b)
    d_z, d_b128 = _forget_bwd(fz, b128, d_f_cols, d_f_rows)

    grads["w_a_t"] = [_mm([(d_qkv[g], hs[g])], "tn", bf16, tm=QKV_W, tn=D, name=f"grad_w_a_{g}") for g in range(3)]
    grads["w_fox_t"] = [_mm([(d_fox[k], h1)], "tn", bf16, tm=FOX_W, tn=D, name=f"grad_w_fox_{k}") for k in range(3)]
    grads["w_gab_t"] = _mm([(d_gab, h1)], "tn", f32, tm=D, tn=D, name="grad_w_gab")
    grads["w_f_t"] = _mm([(d_z, h1)], "tn", f32, tm=128, tn=D, name="grad_w_f")
    d_h1_nat = _mm([(d_qkv[0], wt["w_a_t"][0])] + list(zip(d_fox, wt["w_fox_t"]))
                   + [(d_gab, wt["w_gab_t"]), (d_z, wt["w_f_t"])], "nn", f32, tm=512, tn=512, name="proj_in_bwd")
    d_h1_dil = [_mm([(d_qkv[g], wt["w_a_t"][g])], "nn", f32, tm=1024, tn=D, name=f"proj_a_bwd_{g}") for g in (1, 2)]
    d_h1 = _unperm_sum(d_h1_nat, d_h1_dil, dils, "unperm_d_h1")
    grad_x, dg_mix_pre = _norm_bwd_in(dx2, d_h1, x, g_mix_pre)

    small = dict(b_forget=d_b128[:, :N_FOX], norm_mix_pre=dg_mix_pre, norm_mix_post=dg_mix_post,
                 norm_ffn_pre=dg_ffn_pre, norm_ffn_post=dg_ffn_post)
    grads["mid_backward"] = d_qkv[0]
    return sq_err, grad_x, grads, small


NORMS = ("norm_mix_pre", "norm_mix_post", "norm_ffn_pre", "norm_ffn_post")
ORDER = ("w_in", "w_proj_a", "w_proj_b", "w_out", "b_forget", "w_ffn_gate", "w_ffn_up", "w_ffn_down") + NORMS


def kernel(x, w_in, w_proj_a, w_proj_b, w_out, b_forget, w_ffn_gate, w_ffn_up, w_ffn_down, norm_mix_pre, norm_mix_post, norm_ffn_pre, norm_ffn_post, loss_target, m_w_in, m_w_proj_a, m_w_proj_b, m_w_out, m_b_forget, m_w_ffn_gate, m_w_ffn_up, m_w_ffn_down, m_norm_mix_pre, m_norm_mix_post, m_norm_ffn_pre, m_norm_ffn_post, v_w_in, v_w_proj_a, v_w_proj_b, v_w_out, v_b_forget, v_w_ffn_gate, v_w_ffn_up, v_w_ffn_down, v_norm_mix_pre, v_norm_mix_post, v_norm_ffn_pre, v_norm_ffn_post):
    given = dict(w_in=w_in, w_proj_a=w_proj_a, w_proj_b=w_proj_b, w_out=w_out, w_ffn_gate=w_ffn_gate,
                 w_ffn_up=w_ffn_up, w_ffn_down=w_ffn_down)
    given_m = dict(w_in=m_w_in, w_proj_a=m_w_proj_a, w_proj_b=m_w_proj_b, w_out=m_w_out, w_ffn_gate=m_w_ffn_gate,
                   w_ffn_up=m_w_ffn_up, w_ffn_down=m_w_ffn_down)
    given_v = dict(w_in=v_w_in, w_proj_a=v_w_proj_a, w_proj_b=v_w_proj_b, w_out=v_w_out, w_ffn_gate=v_w_ffn_gate,
                   w_ffn_up=v_w_ffn_up, w_ffn_down=v_w_ffn_down)
    w, m, v = ({n: _kernel_layout(n, t[n]) for n in W_NAMES} for t in (given, given_m, given_v))
    small_w = dict(b_forget=b_forget, norm_mix_pre=norm_mix_pre, norm_mix_post=norm_mix_post,
                   norm_ffn_pre=norm_ffn_pre, norm_ffn_post=norm_ffn_post)
    small_m = dict(b_forget=m_b_forget, norm_mix_pre=m_norm_mix_pre, norm_mix_post=m_norm_mix_post,
                   norm_ffn_pre=m_norm_ffn_pre, norm_ffn_post=m_norm_ffn_post)
    small_v = dict(b_forget=v_b_forget, norm_mix_pre=v_norm_mix_pre, norm_mix_post=v_norm_mix_post,
                   norm_ffn_pre=v_norm_ffn_pre, norm_ffn_post=v_norm_ffn_post)

    own = [_halved(w[n].astype(bf16)) for n in W_NAMES]
    chip = 2 * lax.axis_index("x") + lax.axis_index("y")
    exchanged = {"first": _all_gather_async(own[:1], [], "all_gather_first", GATHER_FIRST_ID)}
    fill = lambda ts, mine: [lax.dynamic_update_index_in_dim(t, o, chip, 0) for t, o in zip(ts, mine)]

    def first_weights(ready):
        arrived, _ = lax.optimization_barrier((list(exchanged["first"]), ready))
        exchanged["late"] = _all_gather_async(own[1:], [arrived[0][0, 0, :16, :128]], "all_gather_late", GATHER_LATE_ID)
        return _full_weights(dict(zip(W_NAMES[:1], fill(arrived, own[:1]))))

    def late_weights(ready):
        arrived, _ = lax.optimization_barrier((list(exchanged["late"]), ready))
        return _full_weights(dict(zip(W_NAMES[1:], fill(arrived, own[1:]))))

    sq_err, grad_x, grads, small = _local_step(x[0], loss_target[0], first_weights, b_forget, norm_mix_pre,
                                               norm_mix_post, norm_ffn_pre, norm_ffn_post, late=late_weights)

    g4 = _sharded_grads(grads)
    stack = lambda t, extra: jnp.concatenate(
        [jnp.pad(t["b_forget"], ((0, 0), (0, D - N_FOX)))] + [t[n] for n in NORMS]
        + [jnp.pad(extra, ((0, SMALL_ROWS - LOSS_ROW - 1), (0, D - extra.shape[1])), constant_values=1.0)], axis=0)
    early, _ = lax.optimization_barrier((list(_pair_swap_early([g4[n] for n in W_NAMES[1:]])), grads["mid_backward"]))
    other = list(_pair_swap([g4["w_in"]])) + early
    parts = [_pair_sum(g4[n], o, "pair_sum_" + n) for n, o in zip(W_NAMES, other)]
    recv_early = _scatter_early(parts[1:])
    recv_in, small_all = _scatter_partials(parts[:1], stack(small, sq_err))

    g_shard, delta, new_m, new_v = {}, {}, {}, {}

    def finish(names, parts, recv):
        halves = [_sum_partials(p, r, "sum_partials_" + n) for n, p, r in zip(names, parts, recv)]
        theirs = _swap_halves(halves, "swap_halves_" + names[0])
        for n, mine, other_half in zip(names, halves, theirs):
            g_shard[n], delta[n], new_m[n], new_v[n] = _adamw_halves(w[n], mine, other_half, m[n], v[n], "adamw_" + n)

    recv_early, _ = lax.optimization_barrier((list(recv_early), parts[0]))
    finish(W_NAMES[1:], parts[1:], recv_early)
    (recv_in, small_all), _ = lax.optimization_barrier(((recv_in, small_all), [delta[n] for n in W_NAMES[1:]]))
    finish(W_NAMES[:1], parts[:1], [recv_in])
    small_sum = _sum_small(small_all)
    loss = small_sum[LOSS_ROW, 0] * (0.5 / D)
    ones = jnp.ones((1, 128), f32)
    sd, sm, sv = _adamw(stack(small_w, ones), small_sum, stack(small_m, ones), stack(small_v, ones), "adamw_small")

    outs = [loss, grad_x[None]]
    for big, st in ((g_shard, small_sum), (delta, sd), (new_m, sm), (new_v, sv)):
        t = {n: _harness_layout(n, big[n]) for n in W_NAMES}
        t["b_forget"] = st[0:1, :N_FOX]
        for i, n in enumerate(NORMS):
            t[n] = st[i + 1:i + 2]
        outs += [t[n] for n in ORDER]
    return tuple(outs)
```

```python
import functools
import math

import jax
import jax.numpy as jnp
import numpy as np
from jax import lax
from jax.experimental import pallas as pl
from jax.experimental.pallas import tpu as pltpu
from jax.experimental.pallas import tpu_sc as plsc

f32 = jnp.float32
bf16 = jnp.bfloat16
SDS = jax.ShapeDtypeStruct
MESH = pl.DeviceIdType.MESH

S = 2048
D = 1024
HD = 64
BLK = 128
N_FOX = 8
FOX_W = N_FOX * HD
DIL_GROUPS = ((128, 1), (512, 4), (2048, 16))
SLOTS = 4
DIL_W = SLOTS * HD
QKV_W = 3 * DIL_W
VR_W = 3 * FOX_W
GF_W = 2 * D + 128
F_FF = 2816
ROPE_DIM = 16
ROPE_THETA = 500000.0
EPS = 1e-6
NEG = -1e30
SCALE = 1.0 / math.sqrt(HD)
IN_COLS = 5896
N_SHARD = 4

ADAM_LR, ADAM_B1, ADAM_B2, ADAM_EPS, ADAM_WD, ADAM_STEP = 0.001, 0.9, 0.999, 1e-08, 0.01, 10

VMEM_V7X = 64 * 1024 * 1024
VMEM_PLAN_MAX = VMEM_V7X - 8 * 1024 * 1024

TM = 256
TQ = 256

W_NAMES = ("w_in", "w_proj_a", "w_proj_b", "w_out", "w_ffn_gate", "w_ffn_up", "w_ffn_down")
TRANSPOSED = ("w_in", "w_ffn_gate", "w_ffn_up")
IN_SHARD = IN_COLS // N_SHARD
IN_SHARD_PAD = 1504
SHARD_SHAPE = dict(w_in=(IN_SHARD_PAD, D), w_proj_a=(DIL_W, D // N_SHARD), w_proj_b=(FOX_W, D // N_SHARD),
                   w_out=(D // N_SHARD, D), w_ffn_gate=(F_FF // N_SHARD, D), w_ffn_up=(F_FF // N_SHARD, D),
                   w_ffn_down=(F_FF // N_SHARD, D))
SMALL_ROWS = 8
LOSS_ROW = 5


def _nbytes(shape, dtype):
    return math.prod(shape) * jnp.dtype(dtype).itemsize


def _params(semantics, block_bytes, temp_bytes=0):
    need = 2 * block_bytes + temp_bytes + (2 << 20)
    return pltpu.CompilerParams(dimension_semantics=semantics, vmem_limit_bytes=int(min(need, VMEM_PLAN_MAX)))


def _row(w, tm=TM):
    return pl.BlockSpec((tm, w), lambda i: (i, 0))


def _vec(w):
    return pl.BlockSpec((1, w), lambda i: (0, 0))


def _mm(pairs, dims, out_dtype, *, tm, tn, name, m_inner=False):
    a0, b0 = pairs[0]
    m_dim = a0.shape[1] if dims == "tn" else a0.shape[0]
    n_dim = b0.shape[0] if dims == "nt" else b0.shape[1]
    contract = {"nn": ((1,), (0,)), "nt": ((1,), (1,)), "tn": ((0,), (0,))}[dims]
    n_pairs = len(pairs)
    assert m_dim % tm == 0 and n_dim % tn == 0, (name, m_dim, n_dim, tm, tn)

    def body(*refs):
        o_ref = refs[-1]
        acc = None
        for p in range(n_pairs):
            a = refs[2 * p][...].astype(bf16)
            b = refs[2 * p + 1][...].astype(bf16)
            t = lax.dot_general(a, b, (contract, ((), ())), preferred_element_type=f32)
            acc = t if acc is None else acc + t
        o_ref[...] = acc.astype(o_ref.dtype)

    if m_inner:
        grid = (n_dim // tn, m_dim // tm)
        mi = lambda j, i: i
        ni = lambda j, i: j
    else:
        grid = (m_dim // tm, n_dim // tn)
        mi = lambda i, j: i
        ni = lambda i, j: j
    in_specs, block_bytes, args = [], 0, []
    for a, b in pairs:
        k_dim = a.shape[0] if dims == "tn" else a.shape[1]
        if dims == "tn":
            in_specs.append(pl.BlockSpec((k_dim, tm), lambda *g: (0, mi(*g))))
        else:
            in_specs.append(pl.BlockSpec((tm, k_dim), lambda *g: (mi(*g), 0)))
        if dims == "nt":
            in_specs.append(pl.BlockSpec((tn, k_dim), lambda *g: (ni(*g), 0)))
        else:
            in_specs.append(pl.BlockSpec((k_dim, tn), lambda *g: (0, ni(*g))))
        block_bytes += _nbytes((tm, k_dim), a.dtype) + _nbytes((tn, k_dim), b.dtype)
        args += [a, b]
    block_bytes += _nbytes((tm, tn), out_dtype)
    temp = _nbytes((tm, tn), f32) * 2 + sum(_nbytes((tm, a.shape[0] if dims == "tn" else a.shape[1]), bf16)
                                            + _nbytes((tn, a.shape[0] if dims == "tn" else a.shape[1]), bf16)
                                            for a, _ in pairs)
    return pl.pallas_call(
        body, grid=grid, in_specs=in_specs,
        out_specs=pl.BlockSpec((tm, tn), lambda *g: (mi(*g), ni(*g))),
        out_shape=SDS((m_dim, n_dim), out_dtype), name=name,
        compiler_params=_params(("parallel", "parallel"), block_bytes, temp),
    )(*args)


def _rms(x, g):
    r = lax.rsqrt(jnp.mean(x * x, axis=-1, keepdims=True) + EPS)
    return x * r * g


def _rms_bwd(x, g, dy):
    r = lax.rsqrt(jnp.mean(x * x, axis=-1, keepdims=True) + EPS)
    xh = x * r
    dxh = dy * g
    dx = r * (dxh - xh * jnp.mean(dxh * xh, axis=-1, keepdims=True))
    return dx, jnp.sum(dy * xh, axis=0, keepdims=True)


def _acc_rows(ref, val):
    @pl.when(pl.program_id(0) == 0)
    def _():
        ref[...] = jnp.zeros_like(ref)
    ref[...] += val


def _norm_fwd(xs, g):
    n = len(xs)

    def body(*refs):
        g = refs[n][...]
        for x_ref, h_ref in zip(refs[:n], refs[n + 1:]):
            h_ref[...] = _rms(x_ref[...], g).astype(bf16)

    return pl.pallas_call(
        body, grid=(S // TM,), in_specs=[_row(D)] * n + [_vec(D)], out_specs=[_row(D)] * n,
        out_shape=[SDS((S, D), bf16)] * n, name="norm_mix_pre",
        compiler_params=_params(("parallel",), 6 * n * TM * D, 8 * n * TM * D))(*xs, g)


def _perm_rows(xs, ds, name):
    n = len(xs)

    def body(*refs):
        outs = iter(refs[n:])
        for x_ref in refs[:n]:
            for d in ds:
                o_ref, rows = next(outs), S // d
                for r in range(d):
                    o_ref[r * rows:(r + 1) * rows, :] = x_ref[pl.ds(r, rows, stride=d), :]

    blk = pl.BlockSpec((S, 128), lambda c: (0, c))
    w = xs[0].shape[1]
    return pl.pallas_call(
        body, grid=(w // 128,), in_specs=[blk] * n, out_specs=[blk] * (n * len(ds)),
        out_shape=[SDS((S, w), f32)] * (n * len(ds)), name=name,
        compiler_params=_params(("parallel",), 4 * S * 128 * n * (1 + len(ds))))(*xs)


def _unperm_sum(nat, perms, ds, name):
    n = len(perms)

    def body(*refs):
        a_ref, o_ref, sc = refs[0], refs[n + 1], refs[n + 2]
        acc = a_ref[...]
        for b_ref, d in zip(refs[1:n + 1], ds):
            rows = S // d
            for r in range(d):
                sc[pl.ds(r, rows, stride=d), :] = b_ref[r * rows:(r + 1) * rows, :]
            acc = acc + sc[...]
        o_ref[...] = acc

    blk = pl.BlockSpec((S, 128), lambda c: (0, c))
    w = nat.shape[1]
    return pl.pallas_call(
        body, grid=(w // 128,), in_specs=[blk] * (n + 1), out_specs=blk, out_shape=SDS((S, w), f32),
        scratch_shapes=[pltpu.VMEM((S, 128), f32)], name=name,
        compiler_params=_params(("parallel",), 4 * S * 128 * (n + 2), 8 * S * 128))(nat, *perms)


def _whole(a):
    return pl.BlockSpec(a.shape, lambda i: (0,) * a.ndim)


def _resid_norm_fwd(x, merged, w_out, g_post, g_pre):
    def body(x_ref, mg_ref, w_ref, gp_ref, gn_ref, mix_ref, x2_ref, h_ref):
        mix = jnp.dot(mg_ref[...], w_ref[...], preferred_element_type=f32)
        x2 = x_ref[...] + _rms(mix, gp_ref[...])
        mix_ref[...] = mix
        x2_ref[...] = x2
        h_ref[...] = _rms(x2, gn_ref[...]).astype(bf16)

    return pl.pallas_call(
        body, grid=(S // TM,), in_specs=[_row(D), _row(D), _whole(w_out), _vec(D), _vec(D)], out_specs=[_row(D)] * 3,
        out_shape=[SDS((S, D), f32), SDS((S, D), f32), SDS((S, D), bf16)], name="proj_out_norm",
        compiler_params=_params(("parallel",), 16 * TM * D + 2 * D * D, 16 * TM * D))(x, merged, w_out, g_post, g_pre)


def _loss_head(x2, a_act, w_down, g_post, target):
    def body(x2_ref, a_ref, w_ref, g_ref, t_ref, loss_ref, dy_ref, dff_ref, dg_ref):
        ff = jnp.dot(a_ref[...], w_ref[...], preferred_element_type=f32)
        g = g_ref[...]
        err = x2_ref[...] + _rms(ff, g) - t_ref[...]
        dy = err * (1.0 / D)
        dff, dg = _rms_bwd(ff, g, dy)
        dy_ref[...] = dy
        dff_ref[...] = dff.astype(bf16)
        _acc_rows(dg_ref, dg)
        _acc_rows(loss_ref, jnp.full((1, 128), jnp.sum(err * err), f32))

    return pl.pallas_call(
        body, grid=(S // TM,), in_specs=[_row(D), _row(F_FF), _whole(w_down), _vec(D), _row(D)],
        out_specs=[_vec(128), _row(D), _row(D), _vec(D)],
        out_shape=[SDS((1, 128), f32), SDS((S, D), f32), SDS((S, D), bf16), SDS((1, D), f32)], name="ffn_down_loss",
        compiler_params=_params(("arbitrary",), 14 * TM * D + 2 * TM * F_FF + 2 * F_FF * D, 28 * TM * D),
    )(x2, a_act, w_down, g_post, target)


def _norm_bwd_mid(dy, d_g, d_u, w_gate_t, w_up_t, x2, mix, g_ffn_pre, g_mix_post):
    def body(dy_ref, dgt_ref, dut_ref, wg_ref, wu_ref, x2_ref, mix_ref, g3_ref, g2_ref, dx2_ref, dmix_ref, dg3_ref, dg2_ref):
        dh = jnp.dot(dgt_ref[...], wg_ref[...], preferred_element_type=f32)
        dh += jnp.dot(dut_ref[...], wu_ref[...], preferred_element_type=f32)
        d3, dg3 = _rms_bwd(x2_ref[...], g3_ref[...], dh)
        dx2 = dy_ref[...] + d3
        dmix, dg2 = _rms_bwd(mix_ref[...], g2_ref[...], dx2)
        dx2_ref[...] = dx2
        dmix_ref[...] = dmix.astype(bf16)
        _acc_rows(dg3_ref, dg3)
        _acc_rows(dg2_ref, dg2)

    return pl.pallas_call(
        body, grid=(S // TM,),
        in_specs=[_row(D), _row(F_FF), _row(F_FF), _whole(w_gate_t), _whole(w_up_t), _row(D), _row(D), _vec(D), _vec(D)],
        out_specs=[_row(D), _row(D), _vec(D), _vec(D)],
        out_shape=[SDS((S, D), f32), SDS((S, D), bf16), SDS((1, D), f32), SDS((1, D), f32)], name="ffn_bwd_in_norm",
        compiler_params=_params(("arbitrary",), 18 * TM * D + 4 * TM * F_FF + 4 * F_FF * D, 28 * TM * D),
    )(dy, d_g, d_u, w_gate_t, w_up_t, x2, mix, g_ffn_pre, g_mix_post)


def _norm_bwd_in(dx2, dh1, x, g):
    def body(dx2_ref, dh_ref, x_ref, g_ref, gx_ref, dg_ref):
        d1, dg = _rms_bwd(x_ref[...], g_ref[...], dh_ref[...])
        gx_ref[...] = dx2_ref[...] + d1
        _acc_rows(dg_ref, dg)

    return pl.pallas_call(
        body, grid=(S // TM,), in_specs=[_row(D)] * 3 + [_vec(D)], out_specs=[_row(D), _vec(D)],
        out_shape=[SDS((S, D), f32), SDS((1, D), f32)], name="norm_bwd_in",
        compiler_params=_params(("arbitrary",), 16 * TM * D, 16 * TM * D))(dx2, dh1, x, g)


def _rope_tables():
    half = ROPE_DIM // 2
    inv_freq = np.power(np.float32(ROPE_THETA), -np.arange(0, ROPE_DIM, 2, dtype=np.float32) / np.float32(ROPE_DIM))
    row = np.arange(S)
    groups = []
    for _, d in DIL_GROUPS:
        pos = ((row % (S // d)) * d + row // (S // d)).astype(np.float32)
        ang = pos[:, None] * inv_freq[None, :].astype(np.float32)
        cos, sin = np.cos(ang).astype(np.float32), np.sin(ang).astype(np.float32)
        c = np.concatenate([cos, cos, np.ones((S, HD - ROPE_DIM), np.float32)], axis=1)
        s_lo = np.concatenate([-sin, np.zeros((S, HD - half), np.float32)], axis=1)
        s_hi = np.concatenate([np.zeros((S, half), np.float32), sin, np.zeros((S, HD - ROPE_DIM), np.float32)], axis=1)
        groups.append(np.stack([np.concatenate([t, t], axis=1) for t in (c, s_lo, s_hi)]))
    return jnp.asarray(np.stack(groups))


def _rotate(x, c, lo, hi, sign):
    tile = lambda t: jnp.tile(t, (1, DIL_W // 128))
    return (x * tile(c) + pltpu.roll(x, DIL_W - ROPE_DIM // 2, 1) * (tile(lo) * sign)
            + pltpu.roll(x, ROPE_DIM // 2, 1) * (tile(hi) * sign))


def _table_specs(g):
    return [pl.BlockSpec((None, None, TM, 128), lambda i, k=k: (g, k, i, 0)) for k in range(3)]


def _rope_fwd(g, p_qkv, tables):
    def body(x_ref, c_ref, lo_ref, hi_ref, o_ref):
        c, lo, hi = c_ref[...], lo_ref[...], hi_ref[...]
        for part in range(2):
            cols = slice(part * DIL_W, (part + 1) * DIL_W)
            o_ref[:, cols] = _rotate(x_ref[:, cols], c, lo, hi, 1.0).astype(bf16)
        o_ref[:, 2 * DIL_W:] = x_ref[:, 2 * DIL_W:].astype(bf16)

    return pl.pallas_call(
        body, grid=(S // TM,), in_specs=[_row(QKV_W)] + _table_specs(g), out_specs=_row(QKV_W),
        out_shape=SDS((S, QKV_W), bf16), name=f"rope_fwd_{g}",
        compiler_params=_params(("parallel",), 6 * TM * QKV_W + 12 * TM * 128, 24 * TM * QKV_W))(p_qkv, tables, tables, tables)


def _rope_bwd(g, dq, dk, dv, tables):
    def body(dq_ref, dk_ref, dv_ref, c_ref, lo_ref, hi_ref, o_ref):
        c, lo, hi = c_ref[...], lo_ref[...], hi_ref[...]
        o_ref[:, :DIL_W] = _rotate(dq_ref[...], c, lo, hi, -1.0).astype(bf16)
        o_ref[:, DIL_W:2 * DIL_W] = _rotate(dk_ref[...], c, lo, hi, -1.0).astype(bf16)
        o_ref[:, 2 * DIL_W:] = dv_ref[...].astype(bf16)

    return pl.pallas_call(
        body, grid=(S // TM,), in_specs=[_row(DIL_W)] * 3 + _table_specs(g), out_specs=_row(QKV_W),
        out_shape=SDS((S, QKV_W), bf16), name=f"rope_bwd_{g}",
        compiler_params=_params(("parallel",), 6 * TM * QKV_W + 12 * TM * 128, 24 * TM * QKV_W))(dq, dk, dv, tables, tables, tables)


def _nt(a, b):
    return lax.dot_general(a, b, (((1,), (1,)), ((), ())), preferred_element_type=f32)


def _tn(a, b):
    return lax.dot_general(a, b, (((0,), (0,)), ((), ())), preferred_element_type=f32)


STEP_BLOCKS = 4
STEP_ROWS = STEP_BLOCKS * BLK


def _dil_prev(g, b):
    _, d = DIL_GROUPS[g]
    nb = S // d // BLK
    if nb == 1 or (b == 0 and nb <= STEP_BLOCKS):
        return None
    return "in" if b > 0 else "halo"


def _bnt(a, b):
    return lax.dot_general(a, b, (((2,), (2,)), ((0,), (0,))), preferred_element_type=f32)


def _bnn(a, b):
    return lax.dot_general(a, b, (((2,), (1,)), ((0,), (0,))), preferred_element_type=f32)


def _btn(a, b):
    return lax.dot_general(a, b, (((1,), (1,)), ((0,), (0,))), preferred_element_type=f32)


def _on_tail(x, tail, fn):
    if tail == x.shape[0]:
        return fn(x)
    return jnp.concatenate([x[:-tail], fn(x[-tail:])], axis=0)


def _heads(ref, part):
    n = ref.shape[0] // BLK
    return jnp.stack([ref[b * BLK:(b + 1) * BLK, part * DIL_W + h * HD:part * DIL_W + (h + 1) * HD]
                      for b in range(n) for h in range(SLOTS)])


def _dil_operands(g, qkv_ref, halo_ref):
    q, kc, vc = (_heads(qkv_ref, part) for part in range(3))
    qi = lax.broadcasted_iota(jnp.int32, (1, BLK, BLK), 1)
    kj = lax.broadcasted_iota(jnp.int32, (1, BLK, BLK), 2)
    with_prev = [b for b in range(STEP_BLOCKS) if _dil_prev(g, b) is not None]
    tail = SLOTS * len(with_prev)
    if not tail:
        return q, kc, vc, None, None, kj <= qi, None, 0
    assert with_prev == list(range(STEP_BLOCKS - len(with_prev), STEP_BLOCKS))
    inside = SLOTS * sum(_dil_prev(g, b) == "in" for b in with_prev)
    kp, vp, prev = kc[:inside], vc[:inside], jnp.broadcast_to(kj >= qi, (inside, BLK, BLK))
    if inside < tail:
        no_halo = jnp.where(pl.program_id(0) == 0, BLK + 1, 0)
        kp = jnp.concatenate([_heads(halo_ref, 1), kp], axis=0)
        vp = jnp.concatenate([_heads(halo_ref, 2), vp], axis=0)
        prev = jnp.concatenate([jnp.broadcast_to(kj >= qi + no_halo, (SLOTS, BLK, BLK)), prev], axis=0)
    return q, kc, vc, kp, vp, kj <= qi, prev, tail


def _dil_in_specs(g, n_aux):
    step = lambda w: pl.BlockSpec((STEP_ROWS, w), lambda i: (i, 0))
    halo = [pl.BlockSpec((BLK, QKV_W), lambda i: (jnp.maximum(i * STEP_BLOCKS - 1, 0), 0))]
    needs_halo = _dil_prev(g, 0) == "halo"
    return [step(QKV_W)] + (halo if needs_halo else []) + [step(DIL_W)] * n_aux, needs_halo


def _dil_fwd(g, qkv):
    in_specs, needs_halo = _dil_in_specs(g, 0)

    def body(*refs):
        qkv_ref, halo_ref = refs[0], refs[1] if needs_halo else None
        o_ref, lse_ref = refs[-2:]
        q, kc, vc, kp, vp, cur, prev, tail = _dil_operands(g, qkv_ref, halo_ref)
        sc = jnp.where(cur, _bnt(q, kc) * SCALE, NEG)
        m = jnp.max(sc, axis=-1, keepdims=True)
        if tail:
            sp = jnp.where(prev, _bnt(q[-tail:], kp) * SCALE, NEG)
            m = _on_tail(m, tail, lambda t: jnp.maximum(t, jnp.max(sp, axis=-1, keepdims=True)))
            pp = jnp.exp(sp - m[-tail:])
        pc = jnp.exp(sc - m)
        den = jnp.sum(pc, axis=-1, keepdims=True)
        if tail:
            den = _on_tail(den, tail, lambda t: t + jnp.sum(pp, axis=-1, keepdims=True))
        inv = 1.0 / den
        o = _bnn((pc * inv).astype(bf16), vc)
        if tail:
            o = _on_tail(o, tail, lambda t: t + _bnn((pp * inv[-tail:]).astype(bf16), vp))
        lse = m + jnp.log(den)
        for b in range(STEP_BLOCKS):
            for h in range(SLOTS):
                rows, hs = slice(b * BLK, (b + 1) * BLK), slice(h * HD, (h + 1) * HD)
                o_ref[rows, hs] = o[SLOTS * b + h]
                lse_ref[rows, hs] = jnp.broadcast_to(lse[SLOTS * b + h], (BLK, HD))

    out = pl.BlockSpec((STEP_ROWS, DIL_W), lambda i: (i, 0))
    return pl.pallas_call(
        body, grid=(S // STEP_ROWS,), in_specs=in_specs, out_specs=[out, out], out_shape=[SDS((S, DIL_W), f32)] * 2,
        name=f"dil_fwd_{g}", compiler_params=_params(("parallel",), 12 * STEP_ROWS * DIL_W, 2 << 20),
    )(*([qkv] * (2 if needs_halo else 1)))


def _dil_combine(outs, lses):
    def body(o0, o1, o2, l0, l1, l2, out_ref, lse_ref, so1, so2, sl1, sl2):
        for (_, d), src, dst in ((DIL_GROUPS[1], o1, so1), (DIL_GROUPS[2], o2, so2),
                                 (DIL_GROUPS[1], l1, sl1), (DIL_GROUPS[2], l2, sl2)):
            rows = S // d
            for r in range(d):
                dst[pl.ds(r, rows, stride=d), :] = src[r * rows:(r + 1) * rows, :]
        a, b, c = l0[...], sl1[...], sl2[...]
        m = jnp.maximum(jnp.maximum(a, b), c)
        ea, eb, ec = jnp.exp(a - m), jnp.exp(b - m), jnp.exp(c - m)
        z = ea + eb + ec
        inv = 1.0 / z
        out_ref[...] = (ea * inv) * o0[...] + (eb * inv) * so1[...] + (ec * inv) * so2[...]
        lse_ref[...] = m + jnp.log(z)

    blk = pl.BlockSpec((S, 128), lambda c: (0, c))
    return pl.pallas_call(
        body, grid=(DIL_W // 128,), in_specs=[blk] * 6, out_specs=[blk] * 2,
        out_shape=[SDS((S, DIL_W), f32)] * 2, scratch_shapes=[pltpu.VMEM((S, 128), f32)] * 4, name="dil_combine",
        compiler_params=_params(("parallel",), 32 * S * 128, 32 * S * 128))(*outs, *lses)


def _dil_bwd(g, qkv, d_out, delta, lse):
    in_specs, needs_halo = _dil_in_specs(g, 3)

    def body(*refs):
        qkv_ref, halo_ref = refs[0], refs[1] if needs_halo else None
        do_ref, dl_ref, lse_ref, dq_ref, dk_ref, dv_ref = refs[-6:]
        q, kc, vc, kp, vp, cur, prev, tail = _dil_operands(g, qkv_ref, halo_ref)
        tiles = [(slice(b * BLK, (b + 1) * BLK), h) for b in range(STEP_BLOCKS) for h in range(SLOTS)]
        do = jnp.stack([do_ref[rows, h * HD:(h + 1) * HD] for rows, h in tiles]).astype(bf16)
        lse = jnp.stack([lse_ref[rows, h * HD:h * HD + 1] for rows, h in tiles])
        delta = jnp.stack([dl_ref[rows, h * HD:h * HD + 1] for rows, h in tiles])

        def probs(q, k, mask, lse, do, v, delta):
            p = jnp.exp(jnp.where(mask, _bnt(q, k) * SCALE, NEG) - lse)
            ds = p * (_bnt(do, v) - delta) * SCALE
            return p.astype(bf16), ds.astype(bf16)

        p, ds = probs(q, kc, cur, lse, do, vc, delta)
        dq, dk, dv = _bnn(ds, kc), _btn(ds, q), _btn(p, do)
        if tail:
            p, ds = probs(q[-tail:], kp, prev, lse[-tail:], do[-tail:], vp, delta[-tail:])
            dq = _on_tail(dq, tail, lambda t: t + _bnn(ds, kp))
            dk_p, dv_p = _btn(ds, q[-tail:]), _btn(p, do[-tail:])
            inside = tail - SLOTS if needs_halo else tail
            pad = jnp.zeros((len(tiles) - inside, BLK, HD), f32)
            dk = dk + jnp.concatenate([dk_p[tail - inside:], pad], axis=0)
            dv = dv + jnp.concatenate([dv_p[tail - inside:], pad], axis=0)
        first = pl.multiple_of(pl.program_id(0) * STEP_ROWS, STEP_ROWS)
        for t, (rows, h) in enumerate(tiles):
            hs = slice(h * HD, (h + 1) * HD)
            own = pl.ds(pl.multiple_of(first + rows.start, BLK), BLK)
            dq_ref[rows, hs] = dq[t]
            dk_ref[own, hs] = dk[t]
            dv_ref[own, hs] = dv[t]
        if needs_halo:
            before = pl.ds(pl.multiple_of(jnp.maximum(first - BLK, 0), BLK), BLK)
            for h in range(SLOTS):
                hs = slice(h * HD, (h + 1) * HD)
                dk_ref[before, hs] += dk_p[h]
                dv_ref[before, hs] += dv_p[h]

    whole = pl.BlockSpec((S, DIL_W), lambda i: (0, 0))
    return pl.pallas_call(
        body, grid=(S // STEP_ROWS,), in_specs=in_specs,
        out_specs=[pl.BlockSpec((STEP_ROWS, DIL_W), lambda i: (i, 0)), whole, whole],
        out_shape=[SDS((S, DIL_W), f32)] * 3, name=f"dil_bwd_{g}",
        compiler_params=_params(("arbitrary",), 20 * STEP_ROWS * DIL_W + 8 * S * DIL_W, 2 << 20),
    )(*([qkv] * (2 if needs_halo else 1)), d_out, delta, lse)


def _scan_rows(x, reverse):
    row = lax.broadcasted_iota(jnp.int32, x.shape, 0)
    k = 1
    while k < S:
        if reverse:
            x = x + jnp.where(row < S - k, pltpu.roll(x, S - k, 0), 0.0)
        else:
            x = x + jnp.where(row >= k, pltpu.roll(x, k, 0), 0.0)
        k *= 2
    return x


N_PAIR = N_FOX // 2
_PAIR_Q = pl.BlockSpec((None, S, 128), lambda p: (p, 0, 0))
_PAIR_K = pl.BlockSpec((None, 8, S), lambda p: (p, 0, 0))


def _forget_fwd(fz, b128):
    def body(z_ref, b_ref, fq_ref, fk_ref):
        z = z_ref[...] + b_ref[...]
        logf = jnp.minimum(z, 0.0) - jnp.log1p(jnp.exp(-jnp.abs(z)))
        f_cum = _scan_rows(logf, reverse=False)
        f_cum_t = f_cum.T
        fq_ref[...] = jnp.zeros_like(fq_ref)
        fk_ref[...] = jnp.zeros_like(fk_ref)
        for p in range(N_PAIR):
            fq_ref[p, :, 0:2] = f_cum[:, 2 * p:2 * p + 2]
            fk_ref[p, 0:2, :] = f_cum_t[2 * p:2 * p + 2, :]

    return pl.pallas_call(
        body, grid=(1,), in_specs=[pl.BlockSpec((S, 128), lambda i: (0, 0)), _vec(128)],
        out_specs=[pl.BlockSpec((N_PAIR, S, 128), lambda i: (0, 0, 0)), pl.BlockSpec((N_PAIR, 8, S), lambda i: (0, 0, 0))],
        out_shape=[SDS((N_PAIR, S, 128), f32), SDS((N_PAIR, 8, S), f32)], name="forget_fwd",
        compiler_params=_params(("arbitrary",), 24 * S * 128, 24 * S * 128))(fz, b128)


def _forget_bwd(fz, b128, d_f_cols, d_f_rows):
    def body(z_ref, b_ref, dfc_ref, dfr_ref, dz_ref, db_ref, df_sc):
        z = z_ref[...] + b_ref[...]
        df_sc[...] = jnp.zeros_like(df_sc)
        for p in range(N_PAIR):
            df_sc[:, 2 * p:2 * p + 2] = dfr_ref[p, :, 0:2] + dfc_ref[p].T[:, 0:2]
        dz = _scan_rows(df_sc[...], reverse=True) * jax.nn.sigmoid(-z)
        dz_ref[...] = dz
        db_ref[...] = jnp.sum(dz, axis=0, keepdims=True)

    full = pl.BlockSpec((S, 128), lambda i: (0, 0))
    return pl.pallas_call(
        body, grid=(1,),
        in_specs=[full, _vec(128), pl.BlockSpec((N_PAIR, 8, S), lambda i: (0, 0, 0)), pl.BlockSpec((N_PAIR, S, 128), lambda i: (0, 0, 0))],
        out_specs=[full, _vec(128)], out_shape=[SDS((S, 128), f32), SDS((1, 128), f32)],
        scratch_shapes=[pltpu.VMEM((S, 128), f32)], name="forget_bwd",
        compiler_params=_params(("arbitrary",), 32 * S * 128, 24 * S * 128))(fz, b128, d_f_cols, d_f_rows)


def _fox_scores(q_ref, k_ref, fq_ref, fk_ref, qi, hh):
    n = (qi + 1) * TQ
    rows, hs = slice(qi * TQ, n), slice(hh * HD, (hh + 1) * HD)
    s = _nt(q_ref[rows, hs], k_ref[0:n, hs]) * SCALE + (fq_ref[rows, hh:hh + 1] - fk_ref[hh:hh + 1, 0:n])
    qpos = qi * TQ + lax.broadcasted_iota(jnp.int32, (TQ, n), 0)
    kpos = lax.broadcasted_iota(jnp.int32, (TQ, n), 1)
    return jnp.where(kpos <= qpos, s, NEG)


def _pair_cols(first):
    return pl.BlockSpec((S, 128), lambda p: (0, first + p))


def _fox_fwd(vr, fq, fk):
    def body(q_ref, k_ref, v_ref, fq_ref, fk_ref, o_ref, lse_ref):
        lse_ref[...] = jnp.zeros_like(lse_ref)
        for hh in range(2):
            hs = slice(hh * HD, (hh + 1) * HD)
            for qi in range(S // TQ):
                n = (qi + 1) * TQ
                rows = slice(qi * TQ, n)
                s = _fox_scores(q_ref, k_ref, fq_ref, fk_ref, qi, hh)
                m = jnp.max(s, axis=-1, keepdims=True)
                p = jnp.exp(s - m)
                den = jnp.sum(p, axis=-1, keepdims=True)
                o_ref[rows, hs] = jnp.dot((p * (1.0 / den)).astype(bf16), v_ref[0:n, hs], preferred_element_type=f32)
                lse_ref[rows, hh:hh + 1] = m + jnp.log(den)

    return pl.pallas_call(
        body, grid=(N_PAIR,), in_specs=[_pair_cols(0), _pair_cols(N_PAIR), _pair_cols(2 * N_PAIR), _PAIR_Q, _PAIR_K],
        out_specs=[_pair_cols(0), _PAIR_Q], out_shape=[SDS((S, FOX_W), f32), SDS((N_PAIR, S, 128), f32)],
        name="fox_fwd", compiler_params=_params(("parallel",), 12 * S * 128, 16 * TQ * S),
    )(vr, vr, vr, fq, fk)


def _fox_bwd(vr, fq, fk, lse, d_out, delta):
    def body(q_ref, k_ref, v_ref, do_ref, fq_ref, fk_ref, lse_ref, dl_ref, dq_ref, dk_ref, dv_ref, dfc_ref, dfr_ref,
             dk_sc, dv_sc):
        dfc_ref[...] = jnp.zeros_like(dfc_ref)
        dfr_ref[...] = jnp.zeros_like(dfr_ref)
        for hh in range(2):
            hs = slice(hh * HD, (hh + 1) * HD)
            dk_sc[...] = jnp.zeros_like(dk_sc)
            dv_sc[...] = jnp.zeros_like(dv_sc)
            for qi in range(S // TQ):
                n = (qi + 1) * TQ
                rows = slice(qi * TQ, n)
                q, do, k, v = q_ref[rows, hs], do_ref[rows, hs], k_ref[0:n, hs], v_ref[0:n, hs]
                p = jnp.exp(_fox_scores(q_ref, k_ref, fq_ref, fk_ref, qi, hh) - lse_ref[rows, hh:hh + 1])
                ds = p * (_nt(do, v) - dl_ref[rows, hh:hh + 1])
                dsb = ds.astype(bf16)
                dq_ref[rows, hs] = jnp.dot(dsb, k, preferred_element_type=f32) * SCALE
                dk_sc[0:n, :] += _tn(dsb, q) * SCALE
                dv_sc[0:n, :] += _tn(p.astype(bf16), do)
                dfc_ref[hh:hh + 1, 0:n] -= jnp.sum(ds, axis=0, keepdims=True)
                dfr_ref[rows, hh:hh + 1] = jnp.sum(ds, axis=-1, keepdims=True)
            dk_ref[:, hs] = dk_sc[...]
            dv_ref[:, hs] = dv_sc[...]

    cols = [_pair_cols(k * N_PAIR) for k in range(3)]
    return pl.pallas_call(
        body, grid=(N_PAIR,), in_specs=cols + [_pair_cols(0), _PAIR_Q, _PAIR_K, _PAIR_Q, _PAIR_Q],
        out_specs=[_pair_cols(0)] * 3 + [_PAIR_K, _PAIR_Q],
        out_shape=[SDS((S, FOX_W), f32)] * 3 + [SDS((N_PAIR, 8, S), f32), SDS((N_PAIR, S, 128), f32)],
        scratch_shapes=[pltpu.VMEM((S, HD), f32)] * 2, name="fox_bwd",
        compiler_params=_params(("parallel",), 32 * S * 128, 24 * TQ * S),
    )(vr, vr, vr, d_out, fq, fk, lse, delta)


def _merge_fwd(out_a, out_b, w_a, w_b, gf):
    cw = D // N_SHARD

    def body(oa_ref, ob_ref, wa_ref, wb_ref, ga_ref, gb_ref, ya_ref, yb_ref, mg_ref):
        oa, ob = oa_ref[...].astype(bf16), ob_ref[...].astype(bf16)
        for j in range(N_SHARD):
            cols = slice(j * cw, (j + 1) * cw)
            ya = jnp.dot(oa, wa_ref[j], preferred_element_type=f32)
            yb = jnp.dot(ob, wb_ref[j], preferred_element_type=f32)
            ya_ref[:, cols] = ya.astype(bf16)
            yb_ref[:, cols] = yb.astype(bf16)
            mg_ref[:, cols] = (jax.nn.sigmoid(ga_ref[:, cols]) * ya + jax.nn.sigmoid(gb_ref[:, cols]) * yb).astype(bf16)

    full = lambda a: pl.BlockSpec(a.shape, lambda i: (0, 0, 0))
    return pl.pallas_call(
        body, grid=(S // TM,),
        in_specs=[_row(DIL_W), _row(FOX_W), full(w_a), full(w_b), _row(D), pl.BlockSpec((TM, D), lambda i: (i, 1))],
        out_specs=[_row(D)] * 3, out_shape=[SDS((S, D), bf16)] * 3, name="merge_fwd",
        compiler_params=_params(("parallel",), 22 * TM * D + 2 * (DIL_W + FOX_W) * D, 16 * TM * D),
    )(out_a, out_b, w_a, w_b, gf, gf)


def _merge_bwd(d_mix, w_out, ya, yb, gf):
    def body(dx_ref, w_ref, ya_ref, yb_ref, ga_ref, gb_ref, dya_ref, dyb_ref, dg_ref):
        dm = _nt(dx_ref[...], w_ref[...])
        sa, sb = jax.nn.sigmoid(ga_ref[...]), jax.nn.sigmoid(gb_ref[...])
        dya_ref[...] = (dm * sa).astype(bf16)
        dyb_ref[...] = (dm * sb).astype(bf16)
        dg_ref[:, :D] = (dm * ya_ref[...].astype(f32) * sa * (1.0 - sa)).astype(bf16)
        dg_ref[:, D:] = (dm * yb_ref[...].astype(f32) * sb * (1.0 - sb)).astype(bf16)

    return pl.pallas_call(
        body, grid=(S // TM,),
        in_specs=[_row(D), _whole(w_out)] + [_row(D)] * 3 + [pl.BlockSpec((TM, D), lambda i: (i, 1))],
        out_specs=[_row(D), _row(D), _row(2 * D)],
        out_shape=[SDS((S, D), bf16), SDS((S, D), bf16), SDS((S, 2 * D), bf16)], name="proj_out_bwd_merge",
        compiler_params=_params(("parallel",), 26 * TM * D + 2 * D * D, 28 * TM * D))(d_mix, w_out, ya, yb, gf, gf)


def _branch_bwd(d_ya, d_yb, w_a, w_b, out_a, out_b):
    cw = D // N_SHARD

    def body(dya_ref, dyb_ref, wa_ref, wb_ref, oa_ref, ob_ref, doa_ref, dla_ref, dob_ref, dlb_ref):
        doa = jnp.zeros((TM, DIL_W), f32)
        dob = jnp.zeros((TM, FOX_W), f32)
        for j in range(N_SHARD):
            cols = slice(j * cw, (j + 1) * cw)
            doa += _nt(dya_ref[:, cols], wa_ref[j])
            dob += _nt(dyb_ref[:, cols], wb_ref[j])
        doa_ref[...] = doa
        dob_ref[...] = dob.astype(bf16)
        prod_a = doa * oa_ref[...]
        for h in range(SLOTS):
            hs = slice(h * HD, (h + 1) * HD)
            dla_ref[:, hs] = jnp.broadcast_to(jnp.sum(prod_a[:, hs], axis=-1, keepdims=True), (TM, HD))
        prod_b = dob * ob_ref[...]
        dlb_ref[...] = jnp.zeros_like(dlb_ref)
        for h in range(N_FOX):
            dlb_ref[h // 2, :, h % 2:h % 2 + 1] = jnp.sum(prod_b[:, h * HD:(h + 1) * HD], axis=-1, keepdims=True)

    full = lambda a: pl.BlockSpec(a.shape, lambda i: (0, 0, 0))
    return pl.pallas_call(
        body, grid=(S // TM,),
        in_specs=[_row(D), _row(D), full(w_a), full(w_b), _row(DIL_W), _row(FOX_W)],
        out_specs=[_row(DIL_W), _row(DIL_W), _row(FOX_W), pl.BlockSpec((N_PAIR, TM, 128), lambda i: (0, i, 0))],
        out_shape=[SDS((S, DIL_W), f32), SDS((S, DIL_W), f32), SDS((S, FOX_W), bf16), SDS((N_PAIR, S, 128), f32)],
        name="branch_bwd", compiler_params=_params(("parallel",), 8 * TM * D + 2 * (DIL_W + FOX_W) * D, 8 * TM * D),
    )(d_ya, d_yb, w_a, w_b, out_a, out_b)


def _branch_grads(out_a, out_b, d_ya, d_yb):
    cw = D // N_SHARD

    def body(oa_ref, ob_ref, dya_ref, dyb_ref, ga_ref, gb_ref):
        ga_ref[...] = _tn(oa_ref[...].astype(bf16), dya_ref[...]).astype(bf16)
        gb_ref[...] = _tn(ob_ref[...].astype(bf16), dyb_ref[...]).astype(bf16)

    whole = lambda w: pl.BlockSpec((S, w), lambda j: (0, 0))
    cols = pl.BlockSpec((S, cw), lambda j: (0, j))
    return pl.pallas_call(
        body, grid=(N_SHARD,), in_specs=[whole(DIL_W), whole(FOX_W), cols, cols],
        out_specs=[pl.BlockSpec((None, DIL_W, cw), lambda j: (j, 0, 0)), pl.BlockSpec((None, FOX_W, cw), lambda j: (j, 0, 0))],
        out_shape=[SDS((N_SHARD, DIL_W, cw), bf16), SDS((N_SHARD, FOX_W, cw), bf16)], name="grad_w_proj_ab",
        compiler_params=_params(("parallel",), 4 * S * (DIL_W + FOX_W) + 4 * S * cw + 4 * (DIL_W + FOX_W) * cw,
                                4 * S * (DIL_W + FOX_W)))(out_a, out_b, d_ya, d_yb)


FF_TN = F_FF // 2
FF_TM = 512


def _ffn_fwd(h, w_gate_t, w_up_t):
    def body(h_ref, wg_ref, wu_ref, g_ref, u_ref, a_ref):
        hb = h_ref[...]
        g = _nt(hb, wg_ref[...])
        u = _nt(hb, wu_ref[...])
        g_ref[...] = g.astype(bf16)
        u_ref[...] = u.astype(bf16)
        a_ref[...] = (g * jax.nn.sigmoid(g) * u).astype(bf16)

    tile = pl.BlockSpec((FF_TM, FF_TN), lambda j, i: (i, j))
    wspec = pl.BlockSpec((FF_TN, D), lambda j, i: (j, 0))
    return pl.pallas_call(
        body, grid=(F_FF // FF_TN, S // FF_TM),
        in_specs=[pl.BlockSpec((FF_TM, D), lambda j, i: (i, 0)), wspec, wspec], out_specs=[tile] * 3,
        out_shape=[SDS((S, F_FF), bf16)] * 3, name="ffn_fwd",
        compiler_params=_params(("parallel", "parallel"), 2 * FF_TM * D + 4 * D * FF_TN + 6 * FF_TM * FF_TN, 16 * FF_TM * FF_TN),
    )(h, w_gate_t, w_up_t)


def _ffn_bwd_act(d_ff, w_down, g_act, u_act):
    def body(d_ref, wd_ref, g_ref, u_ref, dg_ref, du_ref):
        da = _nt(d_ref[...], wd_ref[...])
        g = g_ref[...].astype(f32)
        sg = jax.nn.sigmoid(g)
        du_ref[...] = (da * g * sg).astype(bf16)
        dg_ref[...] = (da * u_ref[...].astype(f32) * sg * (1.0 + g * (1.0 - sg))).astype(bf16)

    tile = pl.BlockSpec((FF_TM, FF_TN), lambda j, i: (i, j))
    return pl.pallas_call(
        body, grid=(F_FF // FF_TN, S // FF_TM),
        in_specs=[pl.BlockSpec((FF_TM, D), lambda j, i: (i, 0)), pl.BlockSpec((FF_TN, D), lambda j, i: (j, 0)), tile, tile],
        out_specs=[tile, tile], out_shape=[SDS((S, F_FF), bf16)] * 2, name="ffn_bwd_act",
        compiler_params=_params(("parallel", "parallel"), 2 * FF_TM * D + 2 * D * FF_TN + 8 * FF_TM * FF_TN, 16 * FF_TM * FF_TN),
    )(d_ff, w_down, g_act, u_act)


def _row_tile(rows):
    return next(t for t in (376, 128, 176, 64, 32, 16, 8) if rows % t == 0)


def _adamw_math(w, g, m, v):
    c1 = 1.0 - ADAM_B1 ** ADAM_STEP
    c2 = 1.0 - ADAM_B2 ** ADAM_STEP
    m_new = ADAM_B1 * m + (1.0 - ADAM_B1) * g
    v_new = ADAM_B2 * v + (1.0 - ADAM_B2) * (g * g)
    return -ADAM_LR * ((m_new / c1) / (jnp.sqrt(v_new / c2) + ADAM_EPS) + ADAM_WD * w), m_new, v_new


def _adamw(w, g, m, v, name):
    rows, cols = w.shape
    tm = _row_tile(rows)

    def body(w_ref, g_ref, m_ref, v_ref, d_ref, nm_ref, nv_ref):
        d_ref[...], nm_ref[...], nv_ref[...] = _adamw_math(w_ref[...], g_ref[...], m_ref[...], v_ref[...])

    spec = pl.BlockSpec((tm, cols), lambda i: (i, 0))
    return pl.pallas_call(
        body, grid=(rows // tm,), in_specs=[spec] * 4, out_specs=[spec] * 3, out_shape=[SDS(w.shape, f32)] * 3,
        name=name, compiler_params=_params(("parallel",), 28 * tm * cols, 16 * tm * cols))(w, g, m, v)


def _adamw_halves(w, g_mine, g_theirs, m, v, name):
    rows, cols = w.shape
    tm = _row_tile(rows // 2)
    per_half = rows // 2 // tm
    core = lax.axis_index("c").astype(jnp.int32).reshape(1)

    def body(c_ref, w_ref, gm_ref, gt_ref, m_ref, v_ref, g_ref, d_ref, nm_ref, nv_ref):
        mine = pl.program_id(0) // per_half == c_ref[0]
        g = jnp.where(mine, gm_ref[...], gt_ref[...])
        g_ref[...] = g
        d_ref[...], nm_ref[...], nv_ref[...] = _adamw_math(w_ref[...], g, m_ref[...], v_ref[...])

    spec = pl.BlockSpec((tm, cols), lambda i, c_ref: (i, 0))
    in_half = lambda i, first: jnp.clip(i - first * per_half, 0, per_half - 1)
    grid_spec = pltpu.PrefetchScalarGridSpec(
        num_scalar_prefetch=1, grid=(rows // tm,),
        in_specs=[spec, pl.BlockSpec((tm, cols), lambda i, c_ref: (in_half(i, c_ref[0]), 0)),
                  pl.BlockSpec((tm, cols), lambda i, c_ref: (in_half(i, 1 - c_ref[0]), 0)), spec, spec],
        out_specs=[spec] * 4)
    return pl.pallas_call(
        body, grid_spec=grid_spec, out_shape=[SDS(w.shape, f32)] * 4, name=name,
        compiler_params=_params(("parallel",), 36 * tm * cols, 16 * tm * cols))(core, w, g_mine, g_theirs, m, v)


_ANY = pl.BlockSpec(memory_space=pl.ANY)


def _place():
    x, y, c = lax.axis_index("x"), lax.axis_index("y"), lax.axis_index("c")
    chips = [(1 - x, y), (x, 1 - y), (1 - x, 1 - y)]
    return x, y, c, chips


def _halved(t):
    return t.reshape(t.shape[:-2] + (2, t.shape[-2] // 2, t.shape[-1]))


def _gather_body(src, out, send_ici, recv_ici, send_d2d, recv_d2d):
    x, y, c, chips = _place()
    sibling = (x, y, 1 - c)
    me_j = 2 * x + y
    sends = []
    for a in range(len(src)):
        for p in range(3):
            cp = pltpu.make_async_remote_copy(
                src_ref=src[a].at[c], dst_ref=out[a].at[me_j, c], send_sem=send_ici.at[a, p],
                recv_sem=recv_ici.at[a, p], device_id=(*chips[p], c), device_id_type=MESH)
            cp.start()
            sends.append(cp)
    for a in range(len(src)):
        for p, (px, py) in enumerate(chips):
            blk = out[a].at[2 * px + py, c]
            pltpu.make_async_remote_copy(
                src_ref=blk, dst_ref=blk, send_sem=send_ici.at[a, p], recv_sem=recv_ici.at[a, p],
                device_id=sibling, device_id_type=MESH).wait_recv()
            fw = pltpu.make_async_remote_copy(
                src_ref=blk, dst_ref=blk, send_sem=send_d2d.at[a, p], recv_sem=recv_d2d.at[a, p],
                device_id=sibling, device_id_type=MESH)
            fw.start()
            sends.append(fw)
    for a in range(len(src)):
        for p, (px, py) in enumerate(chips):
            blk = out[a].at[2 * px + py, 1 - c]
            pltpu.make_async_remote_copy(
                src_ref=blk, dst_ref=blk, send_sem=send_d2d.at[a, p], recv_sem=recv_d2d.at[a, p],
                device_id=sibling, device_id_type=MESH).wait_recv()
    for cp in sends:
        cp.wait_send()


def _handshake(peers):
    barrier = pltpu.get_barrier_semaphore()
    for peer in peers:
        pl.semaphore_signal(barrier, inc=1, device_id=peer, device_id_type=MESH)
    pl.semaphore_wait(barrier, len(peers))


_SEQUENCER = dict(axis_name="sequencer", num_cores=1)
GATHER_LATE_ID, SCATTER_EARLY_ID, SWAP_EARLY_ID, GATHER_FIRST_ID, SCATTER_LATE_ID = 1, 2, 3, 4, 5


def _all_gather_async(shards, after, name, collective_id):
    n, k = len(shards), len(after)

    def body(*refs):
        x, y, c, chips = _place()
        _handshake([(*chip, c) for chip in chips] + [(x, y, 1 - c)])
        _gather_body(refs[:n], refs[n + k:2 * n + k], *refs[2 * n + k:])

    return pl.kernel(
        body, out_type=[SDS((N_SHARD,) + t.shape, t.dtype) for t in shards],
        mesh=plsc.ScalarSubcoreMesh(**_SEQUENCER), scratch_types=[pltpu.SemaphoreType.DMA((n, 3))] * 4,
        compiler_params=pltpu.CompilerParams(collective_id=collective_id), name=name)(*shards, *after)


def _pair_swap(grads):
    n = len(grads)

    def body(*refs):
        src, out, send_sems, recv_sems = refs[:n], refs[n:2 * n], refs[2 * n], refs[2 * n + 1]
        x, y, c, _ = _place()
        copies = [pltpu.make_async_remote_copy(
            src_ref=src[a].at[:, 1 - c], dst_ref=out[a], send_sem=send_sems.at[a], recv_sem=recv_sems.at[a],
            device_id=(x, y, 1 - c), device_id_type=MESH) for a in range(n)]
        for cp in copies:
            cp.start()
        for cp in copies:
            cp.wait()

    return pl.pallas_call(
        body, in_specs=[_ANY] * n, out_specs=[_ANY] * n,
        out_shape=[SDS((N_SHARD,) + t.shape[2:], t.dtype) for t in grads],
        scratch_shapes=[pltpu.SemaphoreType.DMA((n,)), pltpu.SemaphoreType.DMA((n,))], name="pair_swap",
        compiler_params=pltpu.CompilerParams(has_side_effects=True))(*grads)


def _pair_swap_early(grads):
    n = len(grads)

    def body(*refs):
        src, out, send_sems, recv_sems = refs[:n], refs[n:2 * n], refs[2 * n], refs[2 * n + 1]
        x, y, c, _ = _place()
        _handshake([(x, y, 1 - c)])
        copies = [pltpu.make_async_remote_copy(
            src_ref=src[a].at[:, 1 - c], dst_ref=out[a], send_sem=send_sems.at[a], recv_sem=recv_sems.at[a],
            device_id=(x, y, 1 - c), device_id_type=MESH) for a in range(n)]
        for cp in copies:
            cp.start()
        for cp in copies:
            cp.wait()

    return pl.kernel(
        body, out_type=[SDS((N_SHARD,) + t.shape[2:], t.dtype) for t in grads],
        mesh=plsc.ScalarSubcoreMesh(**_SEQUENCER), scratch_types=[pltpu.SemaphoreType.DMA((n,))] * 2,
        compiler_params=pltpu.CompilerParams(collective_id=SWAP_EARLY_ID), name="pair_swap_early")(*grads)


def _scatter_early(parts):
    n = len(parts)

    def body(*refs):
        part, recv, send_sems, recv_sems = refs[:n], refs[n:2 * n], refs[2 * n], refs[2 * n + 1]
        x, y, c, chips = _place()
        _handshake([(*chip, c) for chip in chips])
        me_j = 2 * x + y
        sends = []
        for a in range(n):
            for p, (px, py) in enumerate(chips):
                cp = pltpu.make_async_remote_copy(
                    src_ref=part[a].at[2 * px + py], dst_ref=recv[a].at[me_j], send_sem=send_sems.at[a, p],
                    recv_sem=recv_sems.at[a, p], device_id=(px, py, c), device_id_type=MESH)
                cp.start()
                sends.append(cp)
        for a in range(n):
            for p, (px, py) in enumerate(chips):
                slot = recv[a].at[2 * px + py]
                pltpu.make_async_remote_copy(
                    src_ref=slot, dst_ref=slot, send_sem=send_sems.at[a, p], recv_sem=recv_sems.at[a, p],
                    device_id=(px, py, c), device_id_type=MESH).wait_recv()
        for cp in sends:
            cp.wait_send()

    return pl.kernel(
        body, out_type=[SDS(t.shape, t.dtype) for t in parts],
        mesh=plsc.ScalarSubcoreMesh(**_SEQUENCER), scratch_types=[pltpu.SemaphoreType.DMA((n, 3))] * 2,
        compiler_params=pltpu.CompilerParams(collective_id=SCATTER_EARLY_ID), name="scatter_early")(*parts)


def _pair_sum(grads, other, name):
    _, _, rows, cols = grads.shape
    tr = _row_tile(rows)
    core = lax.axis_index("c").astype(jnp.int32).reshape(1)

    def body(c_ref, g_ref, o_ref, out_ref):
        out_ref[...] = (g_ref[...].astype(f32) + o_ref[...].astype(f32)).astype(bf16)

    grid_spec = pltpu.PrefetchScalarGridSpec(
        num_scalar_prefetch=1, grid=(N_SHARD, rows // tr),
        in_specs=[pl.BlockSpec((None, None, tr, cols), lambda j, i, c_ref: (j, c_ref[0], i, 0)),
                  pl.BlockSpec((None, tr, cols), lambda j, i, c_ref: (j, i, 0))],
        out_specs=pl.BlockSpec((None, tr, cols), lambda j, i, c_ref: (j, i, 0)))
    return pl.pallas_call(
        body, grid_spec=grid_spec, out_shape=SDS((N_SHARD, rows, cols), bf16), name=name,
        compiler_params=_params(("parallel", "parallel"), 10 * tr * cols, 12 * tr * cols))(core, grads, other)


def _scatter_partials(parts, small):
    n = len(parts)

    def body(*refs):
        part, small_ref, recv, small_all_ref = refs[:n], refs[n], refs[n + 1:2 * n + 1], refs[2 * n + 1]
        send_sems, recv_sems, ssend, srecv, local_sem = refs[2 * n + 2:]
        x, y, c, chips = _place()
        flip = lambda a, bit: 1 - a if bit else a
        peers = [(flip(x, k & 4), flip(y, k & 2), flip(c, k & 1)) for k in range(1, 8)]
        _handshake(peers)
        me_j = 2 * x + y
        me_dev = 4 * x + 2 * y + c
        own = pltpu.make_async_copy(small_ref, small_all_ref.at[me_dev], local_sem)
        own.start()
        sends = []
        for a in range(n):
            for p, (px, py) in enumerate(chips):
                cp = pltpu.make_async_remote_copy(
                    src_ref=part[a].at[2 * px + py], dst_ref=recv[a].at[me_j], send_sem=send_sems.at[a, p],
                    recv_sem=recv_sems.at[a, p], device_id=(px, py, c), device_id_type=MESH)
                cp.start()
                sends.append(cp)
        for k, to in enumerate(peers):
            cp = pltpu.make_async_remote_copy(
                src_ref=small_ref, dst_ref=small_all_ref.at[me_dev],
                send_sem=ssend.at[k], recv_sem=srecv.at[k], device_id=to, device_id_type=MESH)
            cp.start()
            sends.append(cp)
        for a in range(n):
            for p, (px, py) in enumerate(chips):
                slot = recv[a].at[2 * px + py]
                pltpu.make_async_remote_copy(
                    src_ref=slot, dst_ref=slot, send_sem=send_sems.at[a, p], recv_sem=recv_sems.at[a, p],
                    device_id=(px, py, c), device_id_type=MESH).wait_recv()
        for k, (px, py, pc) in enumerate(peers):
            slot = small_all_ref.at[4 * px + 2 * py + pc]
            pltpu.make_async_remote_copy(
                src_ref=slot, dst_ref=slot, send_sem=ssend.at[k], recv_sem=srecv.at[k],
                device_id=(px, py, pc), device_id_type=MESH).wait_recv()
        for cp in sends:
            cp.wait_send()
        own.wait()

    return pl.kernel(
        body, out_type=[SDS(t.shape, t.dtype) for t in parts] + [SDS((8, SMALL_ROWS, D), f32)],
        mesh=plsc.ScalarSubcoreMesh(**_SEQUENCER),
        scratch_types=[pltpu.SemaphoreType.DMA((n, 3)), pltpu.SemaphoreType.DMA((n, 3)),
                       pltpu.SemaphoreType.DMA((7,)), pltpu.SemaphoreType.DMA((7,)), pltpu.SemaphoreType.DMA],
        compiler_params=pltpu.CompilerParams(collective_id=SCATTER_LATE_ID), name="scatter_partials")(*parts, small)


def _sum_partials(part, recv, name):
    _, rows, cols = recv.shape
    tr = _row_tile(rows)
    me = (2 * lax.axis_index("x") + lax.axis_index("y")).astype(jnp.int32).reshape(1)

    def body(me_ref, mine, r0, r1, r2, r3, out_ref):
        acc = None
        for j, r in enumerate((r0, r1, r2, r3)):
            term = jnp.where(me_ref[0] == j, mine[...], r[...]).astype(f32)
            acc = term if acc is None else acc + term
        out_ref[...] = acc

    slot = lambda j: pl.BlockSpec((None, tr, cols), lambda i, me_ref: (jnp.where(me_ref[0] == j, j ^ 1, j), i, 0))
    grid_spec = pltpu.PrefetchScalarGridSpec(
        num_scalar_prefetch=1, grid=(rows // tr,),
        in_specs=[pl.BlockSpec((None, tr, cols), lambda i, me_ref: (me_ref[0], i, 0)), slot(0), slot(1), slot(2), slot(3)],
        out_specs=pl.BlockSpec((tr, cols), lambda i, me_ref: (i, 0)))
    return pl.pallas_call(
        body, grid_spec=grid_spec, out_shape=SDS((rows, cols), f32), name=name,
        compiler_params=_params(("parallel",), 14 * tr * cols, 12 * tr * cols))(me, part, recv, recv, recv, recv)


def _sum_small(small_all):
    def body(small_ref, out_ref):
        tot = small_ref[0]
        for k in range(1, 8):
            tot = tot + small_ref[k]
        out_ref[...] = tot

    return pl.pallas_call(
        body, grid=(1,), in_specs=[pl.BlockSpec((8, SMALL_ROWS, D), lambda i: (0, 0, 0))],
        out_specs=pl.BlockSpec((SMALL_ROWS, D), lambda i: (0, 0)), out_shape=SDS((SMALL_ROWS, D), f32),
        name="sum_small", compiler_params=_params(("arbitrary",), 36 * SMALL_ROWS * D))(small_all)


def _swap_halves(halves, name):
    n = len(halves)

    def body(*refs):
        src, out, send_sems, recv_sems = refs[:n], refs[n:2 * n], refs[2 * n], refs[2 * n + 1]
        x, y, c, _ = _place()
        copies = [pltpu.make_async_remote_copy(
            src_ref=src[a], dst_ref=out[a], send_sem=send_sems.at[a], recv_sem=recv_sems.at[a],
            device_id=(x, y, 1 - c), device_id_type=MESH) for a in range(n)]
        for cp in copies:
            cp.start()
        for cp in copies:
            cp.wait()

    return pl.pallas_call(
        body, in_specs=[_ANY] * n, out_specs=[_ANY] * n, out_shape=[SDS(t.shape, f32) for t in halves],
        scratch_shapes=[pltpu.SemaphoreType.DMA((n,))] * 2, name=name,
        compiler_params=pltpu.CompilerParams(has_side_effects=True))(*halves)


def _kernel_layout(name, t):
    t = t[0]
    if name in TRANSPOSED:
        t = jnp.swapaxes(t, 0, 1)
    return _pad_rows(t, SHARD_SHAPE[name][0])


def _harness_layout(name, t):
    if name == "w_in":
        t = t[:IN_SHARD]
    if name in TRANSPOSED:
        t = jnp.swapaxes(t, 0, 1)
    return t[None]


def _pad_rows(t, rows):
    return t if t.shape[0] == rows else jnp.pad(t, ((0, rows - t.shape[0]), (0, 0)))


_QA, _KA, _VA, _QB, _F, _GAB = 0, 768, 1536, 2304, 3840, 3848


def _full_weights(gathered):
    full = {n: t.reshape((N_SHARD,) + SHARD_SHAPE[n]) for n, t in gathered.items()}
    out = {}
    if "w_in" in full:
        w_in_t = full["w_in"][:, :IN_SHARD].reshape(IN_COLS, D)
        group = lambda g: jnp.concatenate([w_in_t[o + g * DIL_W:o + (g + 1) * DIL_W] for o in (_QA, _KA, _VA)], axis=0)
        out.update(
            w_a_t=[group(g) for g in range(3)],
            w_vr_t=w_in_t[_QB:_F],
            w_fox_t=[w_in_t[_QB + k * FOX_W:_QB + (k + 1) * FOX_W] for k in range(3)],
            w_f_t=jnp.concatenate([w_in_t[_F:_GAB], jnp.zeros((128 - N_FOX, D), bf16)], axis=0),
            w_gab_t=w_in_t[_GAB:])
    if "w_out" in full:
        out.update(
            w_a4=full["w_proj_a"],
            w_b4=full["w_proj_b"],
            w_out=full["w_out"].reshape(D, D),
            w_gate_t=full["w_ffn_gate"].reshape(F_FF, D),
            w_up_t=full["w_ffn_up"].reshape(F_FF, D),
            w_down=full["w_ffn_down"].reshape(F_FF, D))
    return out


def _sharded_grads(g):
    parts = [g["w_a_t"][k][o:o + DIL_W] for o in (0, DIL_W, 2 * DIL_W) for k in range(3)] + g["w_fox_t"]
    tail = jnp.concatenate([g["w_f_t"][:N_FOX], g["w_gab_t"]], axis=0).astype(bf16)
    w_in_t = jnp.concatenate(parts + [tail], axis=0).reshape(N_SHARD, IN_SHARD, D)
    full = dict(w_in=jnp.pad(w_in_t, ((0, 0), (0, IN_SHARD_PAD - IN_SHARD), (0, 0))), w_proj_a=g["w_a4"],
                w_proj_b=g["w_b4"], w_out=g["w_out"], w_ffn_gate=g["w_gate_t"], w_ffn_up=g["w_up_t"],
                w_ffn_down=g["w_down"])
    return {n: _halved(full[n].reshape((N_SHARD,) + SHARD_SHAPE[n])) for n in W_NAMES}


def _local_step(x, target, wt, b_forget, g_mix_pre, g_mix_post, g_ffn_pre, g_ffn_post, late=None):
    tables = _rope_tables()
    b128 = jnp.pad(b_forget, ((0, 0), (0, 128 - N_FOX)))
    dils = tuple(d for _, d in DIL_GROUPS[1:])

    hs = _norm_fwd([x] + list(_perm_rows([x], dils, "perm_x")), g_mix_pre)
    h1 = hs[0]
    if callable(wt):
        wt = wt(h1)
    qkv = [_rope_fwd(g, _mm([(hs[g], wt["w_a_t"][g])], "nt", f32, tm=1024, tn=QKV_W, name=f"proj_a_{g}"), tables)
           for g in range(3)]
    vr = _mm([(h1, wt["w_vr_t"])], "nt", bf16, tm=1024, tn=VR_W // 2, name="proj_vr")
    gab = _mm([(h1, wt["w_gab_t"])], "nt", f32, tm=512, tn=2 * D, name="proj_gab")
    fz = _mm([(h1, wt["w_f_t"])], "nt", f32, tm=1024, tn=128, name="proj_f")
    dil = [_dil_fwd(g, qkv[g]) for g in range(3)]
    out_a, lse_a = _dil_combine([o for o, _ in dil], [l for _, l in dil])
    f_q, f_k = _forget_fwd(fz, b128)
    out_b, lse_b = _fox_fwd(vr, f_q, f_k)
    if late is not None:
        wt = {**wt, **late(out_b)}
    ya, yb, merged = _merge_fwd(out_a, out_b, wt["w_a4"], wt["w_b4"], gab)
    mix, x2, h3 = _resid_norm_fwd(x, merged, wt["w_out"], g_mix_post, g_ffn_pre)
    g_act, u_act, a_act = _ffn_fwd(h3, wt["w_gate_t"], wt["w_up_t"])
    sq_err, dy, d_ff, dg_ffn_post = _loss_head(x2, a_act, wt["w_down"], g_ffn_post, target)

    grads = {}
    d_g, d_u = _ffn_bwd_act(d_ff, wt["w_down"], g_act, u_act)
    grads["w_down"] = _mm([(a_act, d_ff)], "tn", bf16, tm=FF_TN, tn=512, name="grad_w_down")
    grads["w_gate_t"] = _mm([(d_g, h3)], "tn", bf16, tm=FF_TN, tn=512, name="grad_w_gate")
    grads["w_up_t"] = _mm([(d_u, h3)], "tn", bf16, tm=FF_TN, tn=512, name="grad_w_up")
    dx2, d_mix, dg_ffn_pre, dg_mix_post = _norm_bwd_mid(dy, d_g, d_u, wt["w_gate_t"], wt["w_up_t"], x2, mix,
                                                        g_ffn_pre, g_mix_post)

    grads["w_out"] = _mm([(merged, d_mix)], "tn", bf16, tm=D, tn=D, name="grad_w_out")
    d_ya, d_yb, d_gab = _merge_bwd(d_mix, wt["w_out"], ya, yb, gab)
    grads["w_a4"], grads["w_b4"] = _branch_grads(out_a, out_b, d_ya, d_yb)
    d_out_a, delta_a, d_out_b, delta_b = _branch_bwd(d_ya, d_yb, wt["w_a4"], wt["w_b4"], out_a, out_b)

    perm = _perm_rows([d_out_a, delta_a, lse_a], dils, "perm_dil_bwd")
    aux = [(d_out_a, delta_a, lse_a)] + [tuple(perm[k * len(dils) + i] for k in range(3)) for i in range(len(dils))]
    d_qkv = []
    for g in range(3):
        dq, dk, dv = _dil_bwd(g, qkv[g], *aux[g])
        d_qkv.append(_rope_bwd(g, dq, dk, dv, tables))
    *d_fox, d_f_cols, d_f_rows = _fox_bwd(vr, f_q, f_k, lse_b, d_out_b, delta_b)
    d_z, d_b128 = _forget_bwd(fz, b128, d_f_cols, d_f_rows)

    grads["w_a_t"] = [_mm([(d_qkv[g], hs[g])], "tn", bf16, tm=QKV_W, tn=D, name=f"grad_w_a_{g}") for g in range(3)]
    grads["w_fox_t"] = [_mm([(d_fox[k], h1)], "tn", bf16, tm=FOX_W, tn=D, name=f"grad_w_fox_{k}") for k in range(3)]
    grads["w_gab_t"] = _mm([(d_gab, h1)], "tn", f32, tm=D, tn=D, name="grad_w_gab")
    grads["w_f_t"] = _mm([(d_z, h1)], "tn", f32, tm=128, tn=D, name="grad_w_f")
    d_h1_nat = _mm([(d_qkv[0], wt["w_a_t"][0])] + list(zip(d_fox, wt["w_fox_t"]))
                   + [(d_gab, wt["w_gab_t"]), (d_z, wt["w_f_t"])], "nn", f32, tm=512, tn=512, name="proj_in_bwd")
    d_h1_dil = [_mm([(d_qkv[g], wt["w_a_t"][g])], "nn", f32, tm=1024, tn=D, name=f"proj_a_bwd_{g}") for g in (1, 2)]
    d_h1 = _unperm_sum(d_h1_nat, d_h1_dil, dils, "unperm_d_h1")
    grad_x, dg_mix_pre = _norm_bwd_in(dx2, d_h1, x, g_mix_pre)

    small = dict(b_forget=d_b128[:, :N_FOX], norm_mix_pre=dg_mix_pre, norm_mix_post=dg_mix_post,
                 norm_ffn_pre=dg_ffn_pre, norm_ffn_post=dg_ffn_post)
    grads["mid_backward"] = d_qkv[0]
    return sq_err, grad_x, grads, small


NORMS = ("norm_mix_pre", "norm_mix_post", "norm_ffn_pre", "norm_ffn_post")
ORDER = ("w_in", "w_proj_a", "w_proj_b", "w_out", "b_forget", "w_ffn_gate", "w_ffn_up", "w_ffn_down") + NORMS


def kernel(x, w_in, w_proj_a, w_proj_b, w_out, b_forget, w_ffn_gate, w_ffn_up, w_ffn_down, norm_mix_pre, norm_mix_post, norm_ffn_pre, norm_ffn_post, loss_target, m_w_in, m_w_proj_a, m_w_proj_b, m_w_out, m_b_forget, m_w_ffn_gate, m_w_ffn_up, m_w_ffn_down, m_norm_mix_pre, m_norm_mix_post, m_norm_ffn_pre, m_norm_ffn_post, v_w_in, v_w_proj_a, v_w_proj_b, v_w_out, v_b_forget, v_w_ffn_gate, v_w_ffn_up, v_w_ffn_down, v_norm_mix_pre, v_norm_mix_post, v_norm_ffn_pre, v_norm_ffn_post):
    given = dict(w_in=w_in, w_proj_a=w_proj_a, w_proj_b=w_proj_b, w_out=w_out, w_ffn_gate=w_ffn_gate,
                 w_ffn_up=w_ffn_up, w_ffn_down=w_ffn_down)
    given_m = dict(w_in=m_w_in, w_proj_a=m_w_proj_a, w_proj_b=m_w_proj_b, w_out=m_w_out, w_ffn_gate=m_w_ffn_gate,
                   w_ffn_up=m_w_ffn_up, w_ffn_down=m_w_ffn_down)
    given_v = dict(w_in=v_w_in, w_proj_a=v_w_proj_a, w_proj_b=v_w_proj_b, w_out=v_w_out, w_ffn_gate=v_w_ffn_gate,
                   w_ffn_up=v_w_ffn_up, w_ffn_down=v_w_ffn_down)
    w, m, v = ({n: _kernel_layout(n, t[n]) for n in W_NAMES} for t in (given, given_m, given_v))
    small_w = dict(b_forget=b_forget, norm_mix_pre=norm_mix_pre, norm_mix_post=norm_mix_post,
                   norm_ffn_pre=norm_ffn_pre, norm_ffn_post=norm_ffn_post)
    small_m = dict(b_forget=m_b_forget, norm_mix_pre=m_norm_mix_pre, norm_mix_post=m_norm_mix_post,
                   norm_ffn_pre=m_norm_ffn_pre, norm_ffn_post=m_norm_ffn_post)
    small_v = dict(b_forget=v_b_forget, norm_mix_pre=v_norm_mix_pre, norm_mix_post=v_norm_mix_post,
                   norm_ffn_pre=v_norm_ffn_pre, norm_ffn_post=v_norm_ffn_post)

    own = [_halved(w[n].astype(bf16)) for n in W_NAMES]
    chip = 2 * lax.axis_index("x") + lax.axis_index("y")
    exchanged = {"first": _all_gather_async(own[:1], [], "all_gather_first", GATHER_FIRST_ID)}
    fill = lambda ts, mine: [lax.dynamic_update_index_in_dim(t, o, chip, 0) for t, o in zip(ts, mine)]

    def first_weights(ready):
        arrived, _ = lax.optimization_barrier((list(exchanged["first"]), ready))
        exchanged["late"] = _all_gather_async(own[1:], [arrived[0][0, 0, :16, :128]], "all_gather_late", GATHER_LATE_ID)
        return _full_weights(dict(zip(W_NAMES[:1], fill(arrived, own[:1]))))

    def late_weights(ready):
        arrived, _ = lax.optimization_barrier((list(exchanged["late"]), ready))
        return _full_weights(dict(zip(W_NAMES[1:], fill(arrived, own[1:]))))

    sq_err, grad_x, grads, small = _local_step(x[0], loss_target[0], first_weights, b_forget, norm_mix_pre,
                                               norm_mix_post, norm_ffn_pre, norm_ffn_post, late=late_weights)

    g4 = _sharded_grads(grads)
    stack = lambda t, extra: jnp.concatenate(
        [jnp.pad(t["b_forget"], ((0, 0), (0, D - N_FOX)))] + [t[n] for n in NORMS]
        + [jnp.pad(extra, ((0, SMALL_ROWS - LOSS_ROW - 1), (0, D - extra.shape[1])), constant_values=1.0)], axis=0)
    early, _ = lax.optimization_barrier((list(_pair_swap_early([g4[n] for n in W_NAMES[1:]])), grads["mid_backward"]))
    other = list(_pair_swap([g4["w_in"]])) + early
    parts = [_pair_sum(g4[n], o, "pair_sum_" + n) for n, o in zip(W_NAMES, other)]
    recv_early = _scatter_early(parts[1:])
    recv_in, small_all = _scatter_partials(parts[:1], stack(small, sq_err))

    g_shard, delta, new_m, new_v = {}, {}, {}, {}

    def finish(names, parts, recv):
        halves = [_sum_partials(p, r, "sum_partials_" + n) for n, p, r in zip(names, parts, recv)]
        theirs = _swap_halves(halves, "swap_halves_" + names[0])
        for n, mine, other_half in zip(names, halves, theirs):
            g_shard[n], delta[n], new_m[n], new_v[n] = _adamw_halves(w[n], mine, other_half, m[n], v[n], "adamw_" + n)

    recv_early, _ = lax.optimization_barrier((list(recv_early), parts[0]))
    finish(W_NAMES[1:], parts[1:], recv_early)
    (recv_in, small_all), _ = lax.optimization_barrier(((recv_in, small_all), [delta[n] for n in W_NAMES[1:]]))
    finish(W_NAMES[:1], parts[:1], [recv_in])
    small_sum = _sum_small(small_all)
    loss = small_sum[LOSS_ROW, 0] * (0.5 / D)
    ones = jnp.ones((1, 128), f32)
    sd, sm, sv = _adamw(stack(small_w, ones), small_sum, stack(small_m, ones), stack(small_v, ones), "adamw_small")

    outs = [loss, grad_x[None]]
    for big, st in ((g_shard, small_sum), (delta, sd), (new_m, sm), (new_v, sv)):
        t = {n: _harness_layout(n, big[n]) for n in W_NAMES}
        t["b_forget"] = st[0:1, :N_FOX]
        for i, n in enumerate(NORMS):
            t[n] = st[i + 1:i + 2]
        outs += [t[n] for n in ORDER]
    return tuple(outs)
```

```python
import functools
import math

import jax
import jax.numpy as jnp
import numpy as np
from jax import lax
from jax.experimental import pallas as pl
from jax.experimental.pallas import tpu as pltpu
from jax.experimental.pallas import tpu_sc as plsc

f32 = jnp.float32
bf16 = jnp.bfloat16
SDS = jax.ShapeDtypeStruct
MESH = pl.DeviceIdType.MESH

S = 2048
D = 1024
HD = 64
BLK = 128
N_FOX = 8
FOX_W = N_FOX * HD
DIL_GROUPS = ((128, 1), (512, 4), (2048, 16))
SLOTS = 4
DIL_W = SLOTS * HD
QKV_W = 3 * DIL_W
VR_W = 3 * FOX_W
GF_W = 2 * D + 128
F_FF = 2816
ROPE_DIM = 16
ROPE_THETA = 500000.0
EPS = 1e-6
NEG = -1e30
SCALE = 1.0 / math.sqrt(HD)
IN_COLS = 5896
N_SHARD = 4

ADAM_LR, ADAM_B1, ADAM_B2, ADAM_EPS, ADAM_WD, ADAM_STEP = 0.001, 0.9, 0.999, 1e-08, 0.01, 10

VMEM_V7X = 64 * 1024 * 1024
VMEM_PLAN_MAX = VMEM_V7X - 8 * 1024 * 1024

TM = 256
TQ = 256

W_NAMES = ("w_in", "w_proj_a", "w_proj_b", "w_out", "w_ffn_gate", "w_ffn_up", "w_ffn_down")
TRANSPOSED = ("w_in", "w_ffn_gate", "w_ffn_up")
IN_SHARD = IN_COLS // N_SHARD
IN_SHARD_PAD = 1504
SHARD_SHAPE = dict(w_in=(IN_SHARD_PAD, D), w_proj_a=(DIL_W, D // N_SHARD), w_proj_b=(FOX_W, D // N_SHARD),
                   w_out=(D // N_SHARD, D), w_ffn_gate=(F_FF // N_SHARD, D), w_ffn_up=(F_FF // N_SHARD, D),
                   w_ffn_down=(F_FF // N_SHARD, D))
SMALL_ROWS = 8
LOSS_ROW = 5


def _nbytes(shape, dtype):
    return math.prod(shape) * jnp.dtype(dtype).itemsize


def _params(semantics, block_bytes, temp_bytes=0):
    need = 2 * block_bytes + temp_bytes + (2 << 20)
    return pltpu.CompilerParams(dimension_semantics=semantics, vmem_limit_bytes=int(min(need, VMEM_PLAN_MAX)))


def _row(w, tm=TM):
    return pl.BlockSpec((tm, w), lambda i: (i, 0))


def _vec(w):
    return pl.BlockSpec((1, w), lambda i: (0, 0))


def _mm(pairs, dims, out_dtype, *, tm, tn, name, m_inner=False):
    a0, b0 = pairs[0]
    m_dim = a0.shape[1] if dims == "tn" else a0.shape[0]
    n_dim = b0.shape[0] if dims == "nt" else b0.shape[1]
    contract = {"nn": ((1,), (0,)), "nt": ((1,), (1,)), "tn": ((0,), (0,))}[dims]
    n_pairs = len(pairs)
    assert m_dim % tm == 0 and n_dim % tn == 0, (name, m_dim, n_dim, tm, tn)

    def body(*refs):
        o_ref = refs[-1]
        acc = None
        for p in range(n_pairs):
            a = refs[2 * p][...].astype(bf16)
            b = refs[2 * p + 1][...].astype(bf16)
            t = lax.dot_general(a, b, (contract, ((), ())), preferred_element_type=f32)
            acc = t if acc is None else acc + t
        o_ref[...] = acc.astype(o_ref.dtype)

    if m_inner:
        grid = (n_dim // tn, m_dim // tm)
        mi = lambda j, i: i
        ni = lambda j, i: j
    else:
        grid = (m_dim // tm, n_dim // tn)
        mi = lambda i, j: i
        ni = lambda i, j: j
    in_specs, block_bytes, args = [], 0, []
    for a, b in pairs:
        k_dim = a.shape[0] if dims == "tn" else a.shape[1]
        if dims == "tn":
            in_specs.append(pl.BlockSpec((k_dim, tm), lambda *g: (0, mi(*g))))
        else:
            in_specs.append(pl.BlockSpec((tm, k_dim), lambda *g: (mi(*g), 0)))
        if dims == "nt":
            in_specs.append(pl.BlockSpec((tn, k_dim), lambda *g: (ni(*g), 0)))
        else:
            in_specs.append(pl.BlockSpec((k_dim, tn), lambda *g: (0, ni(*g))))
        block_bytes += _nbytes((tm, k_dim), a.dtype) + _nbytes((tn, k_dim), b.dtype)
        args += [a, b]
    block_bytes += _nbytes((tm, tn), out_dtype)
    temp = _nbytes((tm, tn), f32) * 2 + sum(_nbytes((tm, a.shape[0] if dims == "tn" else a.shape[1]), bf16)
                                            + _nbytes((tn, a.shape[0] if dims == "tn" else a.shape[1]), bf16)
                                            for a, _ in pairs)
    return pl.pallas_call(
        body, grid=grid, in_specs=in_specs,
        out_specs=pl.BlockSpec((tm, tn), lambda *g: (mi(*g), ni(*g))),
        out_shape=SDS((m_dim, n_dim), out_dtype), name=name,
        compiler_params=_params(("parallel", "parallel"), block_bytes, temp),
    )(*args)


def _rms(x, g):
    r = lax.rsqrt(jnp.mean(x * x, axis=-1, keepdims=True) + EPS)
    return x * r * g


def _rms_bwd(x, g, dy):
    r = lax.rsqrt(jnp.mean(x * x, axis=-1, keepdims=True) + EPS)
    xh = x * r
    dxh = dy * g
    dx = r * (dxh - xh * jnp.mean(dxh * xh, axis=-1, keepdims=True))
    return dx, jnp.sum(dy * xh, axis=0, keepdims=True)


def _acc_rows(ref, val):
    @pl.when(pl.program_id(0) == 0)
    def _():
        ref[...] = jnp.zeros_like(ref)
    ref[...] += val


def _norm_fwd(xs, g):
    n = len(xs)

    def body(*refs):
        g = refs[n][...]
        for x_ref, h_ref in zip(refs[:n], refs[n + 1:]):
            h_ref[...] = _rms(x_ref[...], g).astype(bf16)

    return pl.pallas_call(
        body, grid=(S // TM,), in_specs=[_row(D)] * n + [_vec(D)], out_specs=[_row(D)] * n,
        out_shape=[SDS((S, D), bf16)] * n, name="norm_mix_pre",
        compiler_params=_params(("parallel",), 6 * n * TM * D, 8 * n * TM * D))(*xs, g)


def _perm_rows(xs, ds, name):
    n = len(xs)

    def body(*refs):
        outs = iter(refs[n:])
        for x_ref in refs[:n]:
            for d in ds:
                o_ref, rows = next(outs), S // d
                for r in range(d):
                    o_ref[r * rows:(r + 1) * rows, :] = x_ref[pl.ds(r, rows, stride=d), :]

    blk = pl.BlockSpec((S, 128), lambda c: (0, c))
    w = xs[0].shape[1]
    return pl.pallas_call(
        body, grid=(w // 128,), in_specs=[blk] * n, out_specs=[blk] * (n * len(ds)),
        out_shape=[SDS((S, w), f32)] * (n * len(ds)), name=name,
        compiler_params=_params(("parallel",), 4 * S * 128 * n * (1 + len(ds))))(*xs)


def _unperm_sum(nat, perms, ds, name):
    n = len(perms)

    def body(*refs):
        a_ref, o_ref, sc = refs[0], refs[n + 1], refs[n + 2]
        acc = a_ref[...]
        for b_ref, d in zip(refs[1:n + 1], ds):
            rows = S // d
            for r in range(d):
                sc[pl.ds(r, rows, stride=d), :] = b_ref[r * rows:(r + 1) * rows, :]
            acc = acc + sc[...]
        o_ref[...] = acc

    blk = pl.BlockSpec((S, 128), lambda c: (0, c))
    w = nat.shape[1]
    return pl.pallas_call(
        body, grid=(w // 128,), in_specs=[blk] * (n + 1), out_specs=blk, out_shape=SDS((S, w), f32),
        scratch_shapes=[pltpu.VMEM((S, 128), f32)], name=name,
        compiler_params=_params(("parallel",), 4 * S * 128 * (n + 2), 8 * S * 128))(nat, *perms)


def _whole(a):
    return pl.BlockSpec(a.shape, lambda i: (0,) * a.ndim)


def _resid_norm_fwd(x, merged, w_out, g_post, g_pre):
    def body(x_ref, mg_ref, w_ref, gp_ref, gn_ref, mix_ref, x2_ref, h_ref):
        mix = jnp.dot(mg_ref[...], w_ref[...], preferred_element_type=f32)
        x2 = x_ref[...] + _rms(mix, gp_ref[...])
        mix_ref[...] = mix
        x2_ref[...] = x2
        h_ref[...] = _rms(x2, gn_ref[...]).astype(bf16)

    return pl.pallas_call(
        body, grid=(S // TM,), in_specs=[_row(D), _row(D), _whole(w_out), _vec(D), _vec(D)], out_specs=[_row(D)] * 3,
        out_shape=[SDS((S, D), f32), SDS((S, D), f32), SDS((S, D), bf16)], name="proj_out_norm",
        compiler_params=_params(("parallel",), 16 * TM * D + 2 * D * D, 16 * TM * D))(x, merged, w_out, g_post, g_pre)


def _loss_head(x2, a_act, w_down, g_post, target):
    def body(x2_ref, a_ref, w_ref, g_ref, t_ref, loss_ref, dy_ref, dff_ref, dg_ref):
        ff = jnp.dot(a_ref[...], w_ref[...], preferred_element_type=f32)
        g = g_ref[...]
        err = x2_ref[...] + _rms(ff, g) - t_ref[...]
        dy = err * (1.0 / D)
        dff, dg = _rms_bwd(ff, g, dy)
        dy_ref[...] = dy
        dff_ref[...] = dff.astype(bf16)
        _acc_rows(dg_ref, dg)
        _acc_rows(loss_ref, jnp.full((1, 128), jnp.sum(err * err), f32))

    return pl.pallas_call(
        body, grid=(S // TM,), in_specs=[_row(D), _row(F_FF), _whole(w_down), _vec(D), _row(D)],
        out_specs=[_vec(128), _row(D), _row(D), _vec(D)],
        out_shape=[SDS((1, 128), f32), SDS((S, D), f32), SDS((S, D), bf16), SDS((1, D), f32)], name="ffn_down_loss",
        compiler_params=_params(("arbitrary",), 14 * TM * D + 2 * TM * F_FF + 2 * F_FF * D, 28 * TM * D),
    )(x2, a_act, w_down, g_post, target)


def _norm_bwd_mid(dy, d_g, d_u, w_gate_t, w_up_t, x2, mix, g_ffn_pre, g_mix_post):
    def body(dy_ref, dgt_ref, dut_ref, wg_ref, wu_ref, x2_ref, mix_ref, g3_ref, g2_ref, dx2_ref, dmix_ref, dg3_ref, dg2_ref):
        dh = jnp.dot(dgt_ref[...], wg_ref[...], preferred_element_type=f32)
        dh += jnp.dot(dut_ref[...], wu_ref[...], preferred_element_type=f32)
        d3, dg3 = _rms_bwd(x2_ref[...], g3_ref[...], dh)
        dx2 = dy_ref[...] + d3
        dmix, dg2 = _rms_bwd(mix_ref[...], g2_ref[...], dx2)
        dx2_ref[...] = dx2
        dmix_ref[...] = dmix.astype(bf16)
        _acc_rows(dg3_ref, dg3)
        _acc_rows(dg2_ref, dg2)

    return pl.pallas_call(
        body, grid=(S // TM,),
        in_specs=[_row(D), _row(F_FF), _row(F_FF), _whole(w_gate_t), _whole(w_up_t), _row(D), _row(D), _vec(D), _vec(D)],
        out_specs=[_row(D), _row(D), _vec(D), _vec(D)],
        out_shape=[SDS((S, D), f32), SDS((S, D), bf16), SDS((1, D), f32), SDS((1, D), f32)], name="ffn_bwd_in_norm",
        compiler_params=_params(("arbitrary",), 18 * TM * D + 4 * TM * F_FF + 4 * F_FF * D, 28 * TM * D),
    )(dy, d_g, d_u, w_gate_t, w_up_t, x2, mix, g_ffn_pre, g_mix_post)


def _norm_bwd_in(dx2, dh1, x, g):
    def body(dx2_ref, dh_ref, x_ref, g_ref, gx_ref, dg_ref):
        d1, dg = _rms_bwd(x_ref[...], g_ref[...], dh_ref[...])
        gx_ref[...] = dx2_ref[...] + d1
        _acc_rows(dg_ref, dg)

    return pl.pallas_call(
        body, grid=(S // TM,), in_specs=[_row(D)] * 3 + [_vec(D)], out_specs=[_row(D), _vec(D)],
        out_shape=[SDS((S, D), f32), SDS((1, D), f32)], name="norm_bwd_in",
        compiler_params=_params(("arbitrary",), 16 * TM * D, 16 * TM * D))(dx2, dh1, x, g)


def _rope_tables():
    half = ROPE_DIM // 2
    inv_freq = np.power(np.float32(ROPE_THETA), -np.arange(0, ROPE_DIM, 2, dtype=np.float32) / np.float32(ROPE_DIM))
    row = np.arange(S)
    groups = []
    for _, d in DIL_GROUPS:
        pos = ((row % (S // d)) * d + row // (S // d)).astype(np.float32)
        ang = pos[:, None] * inv_freq[None, :].astype(np.float32)
        cos, sin = np.cos(ang).astype(np.float32), np.sin(ang).astype(np.float32)
        c = np.concatenate([cos, cos, np.ones((S, HD - ROPE_DIM), np.float32)], axis=1)
        s_lo = np.concatenate([-sin, np.zeros((S, HD - half), np.float32)], axis=1)
        s_hi = np.concatenate([np.zeros((S, half), np.float32), sin, np.zeros((S, HD - ROPE_DIM), np.float32)], axis=1)
        groups.append(np.stack([np.concatenate([t, t], axis=1) for t in (c, s_lo, s_hi)]))
    return jnp.asarray(np.stack(groups))


def _rotate(x, c, lo, hi, sign):
    tile = lambda t: jnp.tile(t, (1, DIL_W // 128))
    return (x * tile(c) + pltpu.roll(x, DIL_W - ROPE_DIM // 2, 1) * (tile(lo) * sign)
            + pltpu.roll(x, ROPE_DIM // 2, 1) * (tile(hi) * sign))


def _table_specs(g):
    return [pl.BlockSpec((None, None, TM, 128), lambda i, k=k: (g, k, i, 0)) for k in range(3)]


def _rope_fwd(g, p_qkv, tables):
    def body(x_ref, c_ref, lo_ref, hi_ref, o_ref):
        c, lo, hi = c_ref[...], lo_ref[...], hi_ref[...]
        for part in range(2):
            cols = slice(part * DIL_W, (part + 1) * DIL_W)
            o_ref[:, cols] = _rotate(x_ref[:, cols], c, lo, hi, 1.0).astype(bf16)
        o_ref[:, 2 * DIL_W:] = x_ref[:, 2 * DIL_W:].astype(bf16)

    return pl.pallas_call(
        body, grid=(S // TM,), in_specs=[_row(QKV_W)] + _table_specs(g), out_specs=_row(QKV_W),
        out_shape=SDS((S, QKV_W), bf16), name=f"rope_fwd_{g}",
        compiler_params=_params(("parallel",), 6 * TM * QKV_W + 12 * TM * 128, 24 * TM * QKV_W))(p_qkv, tables, tables, tables)


def _rope_bwd(g, dq, dk, dv, tables):
    def body(dq_ref, dk_ref, dv_ref, c_ref, lo_ref, hi_ref, o_ref):
        c, lo, hi = c_ref[...], lo_ref[...], hi_ref[...]
        o_ref[:, :DIL_W] = _rotate(dq_ref[...], c, lo, hi, -1.0).astype(bf16)
        o_ref[:, DIL_W:2 * DIL_W] = _rotate(dk_ref[...], c, lo, hi, -1.0).astype(bf16)
        o_ref[:, 2 * DIL_W:] = dv_ref[...].astype(bf16)

    return pl.pallas_call(
        body, grid=(S // TM,), in_specs=[_row(DIL_W)] * 3 + _table_specs(g), out_specs=_row(QKV_W),
        out_shape=SDS((S, QKV_W), bf16), name=f"rope_bwd_{g}",
        compiler_params=_params(("parallel",), 6 * TM * QKV_W + 12 * TM * 128, 24 * TM * QKV_W))(dq, dk, dv, tables, tables, tables)


def _nt(a, b):
    return lax.dot_general(a, b, (((1,), (1,)), ((), ())), preferred_element_type=f32)


def _tn(a, b):
    return lax.dot_general(a, b, (((0,), (0,)), ((), ())), preferred_element_type=f32)


STEP_BLOCKS = 4
STEP_ROWS = STEP_BLOCKS * BLK


def _dil_prev(g, b):
    _, d = DIL_GROUPS[g]
    nb = S // d // BLK
    if nb == 1 or (b == 0 and nb <= STEP_BLOCKS):
        return None
    return "in" if b > 0 else "halo"


def _bnt(a, b):
    return lax.dot_general(a, b, (((2,), (2,)), ((0,), (0,))), preferred_element_type=f32)


def _bnn(a, b):
    return lax.dot_general(a, b, (((2,), (1,)), ((0,), (0,))), preferred_element_type=f32)


def _btn(a, b):
    return lax.dot_general(a, b, (((1,), (1,)), ((0,), (0,))), preferred_element_type=f32)


def _on_tail(x, tail, fn):
    if tail == x.shape[0]:
        return fn(x)
    return jnp.concatenate([x[:-tail], fn(x[-tail:])], axis=0)


def _heads(ref, part):
    n = ref.shape[0] // BLK
    return jnp.stack([ref[b * BLK:(b + 1) * BLK, part * DIL_W + h * HD:part * DIL_W + (h + 1) * HD]
                      for b in range(n) for h in range(SLOTS)])


def _dil_operands(g, qkv_ref, halo_ref):
    q, kc, vc = (_heads(qkv_ref, part) for part in range(3))
    qi = lax.broadcasted_iota(jnp.int32, (1, BLK, BLK), 1)
    kj = lax.broadcasted_iota(jnp.int32, (1, BLK, BLK), 2)
    with_prev = [b for b in range(STEP_BLOCKS) if _dil_prev(g, b) is not None]
    tail = SLOTS * len(with_prev)
    if not tail:
        return q, kc, vc, None, None, kj <= qi, None, 0
    assert with_prev == list(range(STEP_BLOCKS - len(with_prev), STEP_BLOCKS))
    inside = SLOTS * sum(_dil_prev(g, b) == "in" for b in with_prev)
    kp, vp, prev = kc[:inside], vc[:inside], jnp.broadcast_to(kj >= qi, (inside, BLK, BLK))
    if inside < tail:
        no_halo = jnp.where(pl.program_id(0) == 0, BLK + 1, 0)
        kp = jnp.concatenate([_heads(halo_ref, 1), kp], axis=0)
        vp = jnp.concatenate([_heads(halo_ref, 2), vp], axis=0)
        prev = jnp.concatenate([jnp.broadcast_to(kj >= qi + no_halo, (SLOTS, BLK, BLK)), prev], axis=0)
    return q, kc, vc, kp, vp, kj <= qi, prev, tail


def _dil_in_specs(g, n_aux):
    step = lambda w: pl.BlockSpec((STEP_ROWS, w), lambda i: (i, 0))
    halo = [pl.BlockSpec((BLK, QKV_W), lambda i: (jnp.maximum(i * STEP_BLOCKS - 1, 0), 0))]
    needs_halo = _dil_prev(g, 0) == "halo"
    return [step(QKV_W)] + (halo if needs_halo else []) + [step(DIL_W)] * n_aux, needs_halo


def _dil_fwd(g, qkv):
    in_specs, needs_halo = _dil_in_specs(g, 0)

    def body(*refs):
        qkv_ref, halo_ref = refs[0], refs[1] if needs_halo else None
        o_ref, lse_ref = refs[-2:]
        q, kc, vc, kp, vp, cur, prev, tail = _dil_operands(g, qkv_ref, halo_ref)
        sc = jnp.where(cur, _bnt(q, kc) * SCALE, NEG)
        m = jnp.max(sc, axis=-1, keepdims=True)
        if tail:
            sp = jnp.where(prev, _bnt(q[-tail:], kp) * SCALE, NEG)
            m = _on_tail(m, tail, lambda t: jnp.maximum(t, jnp.max(sp, axis=-1, keepdims=True)))
            pp = jnp.exp(sp - m[-tail:])
        pc = jnp.exp(sc - m)
        den = jnp.sum(pc, axis=-1, keepdims=True)
        if tail:
            den = _on_tail(den, tail, lambda t: t + jnp.sum(pp, axis=-1, keepdims=True))
        inv = 1.0 / den
        o = _bnn((pc * inv).astype(bf16), vc)
        if tail:
            o = _on_tail(o, tail, lambda t: t + _bnn((pp * inv[-tail:]).astype(bf16), vp))
        lse = m + jnp.log(den)
        for b in range(STEP_BLOCKS):
            for h in range(SLOTS):
                rows, hs = slice(b * BLK, (b + 1) * BLK), slice(h * HD, (h + 1) * HD)
                o_ref[rows, hs] = o[SLOTS * b + h]
                lse_ref[rows, hs] = jnp.broadcast_to(lse[SLOTS * b + h], (BLK, HD))

    out = pl.BlockSpec((STEP_ROWS, DIL_W), lambda i: (i, 0))
    return pl.pallas_call(
        body, grid=(S // STEP_ROWS,), in_specs=in_specs, out_specs=[out, out], out_shape=[SDS((S, DIL_W), f32)] * 2,
        name=f"dil_fwd_{g}", compiler_params=_params(("parallel",), 12 * STEP_ROWS * DIL_W, 2 << 20),
    )(*([qkv] * (2 if needs_halo else 1)))


def _dil_combine(outs, lses):
    def body(o0, o1, o2, l0, l1, l2, out_ref, lse_ref, so1, so2, sl1, sl2):
        for (_, d), src, dst in ((DIL_GROUPS[1], o1, so1), (DIL_GROUPS[2], o2, so2),
                                 (DIL_GROUPS[1], l1, sl1), (DIL_GROUPS[2], l2, sl2)):
            rows = S // d
            for r in range(d):
                dst[pl.ds(r, rows, stride=d), :] = src[r * rows:(r + 1) * rows, :]
        a, b, c = l0[...], sl1[...], sl2[...]
        m = jnp.maximum(jnp.maximum(a, b), c)
        ea, eb, ec = jnp.exp(a - m), jnp.exp(b - m), jnp.exp(c - m)
        z = ea + eb + ec
        inv = 1.0 / z
        out_ref[...] = (ea * inv) * o0[...] + (eb * inv) * so1[...] + (ec * inv) * so2[...]
        lse_ref[...] = m + jnp.log(z)

    blk = pl.BlockSpec((S, 128), lambda c: (0, c))
    return pl.pallas_call(
        body, grid=(DIL_W // 128,), in_specs=[blk] * 6, out_specs=[blk] * 2,
        out_shape=[SDS((S, DIL_W), f32)] * 2, scratch_shapes=[pltpu.VMEM((S, 128), f32)] * 4, name="dil_combine",
        compiler_params=_params(("parallel",), 32 * S * 128, 32 * S * 128))(*outs, *lses)


def _dil_bwd(g, qkv, d_out, delta, lse):
    in_specs, needs_halo = _dil_in_specs(g, 3)

    def body(*refs):
        qkv_ref, halo_ref = refs[0], refs[1] if needs_halo else None
        do_ref, dl_ref, lse_ref, dq_ref, dk_ref, dv_ref = refs[-6:]
        q, kc, vc, kp, vp, cur, prev, tail = _dil_operands(g, qkv_ref, halo_ref)
        tiles = [(slice(b * BLK, (b + 1) * BLK), h) for b in range(STEP_BLOCKS) for h in range(SLOTS)]
        do = jnp.stack([do_ref[rows, h * HD:(h + 1) * HD] for rows, h in tiles]).astype(bf16)
        lse = jnp.stack([lse_ref[rows, h * HD:h * HD + 1] for rows, h in tiles])
        delta = jnp.stack([dl_ref[rows, h * HD:h * HD + 1] for rows, h in tiles])

        def probs(q, k, mask, lse, do, v, delta):
            p = jnp.exp(jnp.where(mask, _bnt(q, k) * SCALE, NEG) - lse)
            ds = p * (_bnt(do, v) - delta) * SCALE
            return p.astype(bf16), ds.astype(bf16)

        p, ds = probs(q, kc, cur, lse, do, vc, delta)
        dq, dk, dv = _bnn(ds, kc), _btn(ds, q), _btn(p, do)
        if tail:
            p, ds = probs(q[-tail:], kp, prev, lse[-tail:], do[-tail:], vp, delta[-tail:])
            dq = _on_tail(dq, tail, lambda t: t + _bnn(ds, kp))
            dk_p, dv_p = _btn(ds, q[-tail:]), _btn(p, do[-tail:])
            inside = tail - SLOTS if needs_halo else tail
            pad = jnp.zeros((len(tiles) - inside, BLK, HD), f32)
            dk = dk + jnp.concatenate([dk_p[tail - inside:], pad], axis=0)
            dv = dv + jnp.concatenate([dv_p[tail - inside:], pad], axis=0)
        first = pl.multiple_of(pl.program_id(0) * STEP_ROWS, STEP_ROWS)
        for t, (rows, h) in enumerate(tiles):
            hs = slice(h * HD, (h + 1) * HD)
            own = pl.ds(pl.multiple_of(first + rows.start, BLK), BLK)
            dq_ref[rows, hs] = dq[t]
            dk_ref[own, hs] = dk[t]
            dv_ref[own, hs] = dv[t]
        if needs_halo:
            before = pl.ds(pl.multiple_of(jnp.maximum(first - BLK, 0), BLK), BLK)
            for h in range(SLOTS):
                hs = slice(h * HD, (h + 1) * HD)
                dk_ref[before, hs] += dk_p[h]
                dv_ref[before, hs] += dv_p[h]

    whole = pl.BlockSpec((S, DIL_W), lambda i: (0, 0))
    return pl.pallas_call(
        body, grid=(S // STEP_ROWS,), in_specs=in_specs,
        out_specs=[pl.BlockSpec((STEP_ROWS, DIL_W), lambda i: (i, 0)), whole, whole],
        out_shape=[SDS((S, DIL_W), f32)] * 3, name=f"dil_bwd_{g}",
        compiler_params=_params(("arbitrary",), 20 * STEP_ROWS * DIL_W + 8 * S * DIL_W, 2 << 20),
    )(*([qkv] * (2 if needs_halo else 1)), d_out, delta, lse)


def _scan_rows(x, reverse):
    row = lax.broadcasted_iota(jnp.int32, x.shape, 0)
    k = 1
    while k < S:
        if reverse:
            x = x + jnp.where(row < S - k, pltpu.roll(x, S - k, 0), 0.0)
        else:
            x = x + jnp.where(row >= k, pltpu.roll(x, k, 0), 0.0)
        k *= 2
    return x


N_PAIR = N_FOX // 2
_PAIR_Q = pl.BlockSpec((None, S, 128), lambda p: (p, 0, 0))
_PAIR_K = pl.BlockSpec((None, 8, S), lambda p: (p, 0, 0))


def _forget_fwd(fz, b128):
    def body(z_ref, b_ref, fq_ref, fk_ref):
        z = z_ref[...] + b_ref[...]
        logf = jnp.minimum(z, 0.0) - jnp.log1p(jnp.exp(-jnp.abs(z)))
        f_cum = _scan_rows(logf, reverse=False)
        f_cum_t = f_cum.T
        fq_ref[...] = jnp.zeros_like(fq_ref)
        fk_ref[...] = jnp.zeros_like(fk_ref)
        for p in range(N_PAIR):
            fq_ref[p, :, 0:2] = f_cum[:, 2 * p:2 * p + 2]
            fk_ref[p, 0:2, :] = f_cum_t[2 * p:2 * p + 2, :]

    return pl.pallas_call(
        body, grid=(1,), in_specs=[pl.BlockSpec((S, 128), lambda i: (0, 0)), _vec(128)],
        out_specs=[pl.BlockSpec((N_PAIR, S, 128), lambda i: (0, 0, 0)), pl.BlockSpec((N_PAIR, 8, S), lambda i: (0, 0, 0))],
        out_shape=[SDS((N_PAIR, S, 128), f32), SDS((N_PAIR, 8, S), f32)], name="forget_fwd",
        compiler_params=_params(("arbitrary",), 24 * S * 128, 24 * S * 128))(fz, b128)


def _forget_bwd(fz, b128, d_f_cols, d_f_rows):
    def body(z_ref, b_ref, dfc_ref, dfr_ref, dz_ref, db_ref, df_sc):
        z = z_ref[...] + b_ref[...]
        df_sc[...] = jnp.zeros_like(df_sc)
        for p in range(N_PAIR):
            df_sc[:, 2 * p:2 * p + 2] = dfr_ref[p, :, 0:2] + dfc_ref[p].T[:, 0:2]
        dz = _scan_rows(df_sc[...], reverse=True) * jax.nn.sigmoid(-z)
        dz_ref[...] = dz
        db_ref[...] = jnp.sum(dz, axis=0, keepdims=True)

    full = pl.BlockSpec((S, 128), lambda i: (0, 0))
    return pl.pallas_call(
        body, grid=(1,),
        in_specs=[full, _vec(128), pl.BlockSpec((N_PAIR, 8, S), lambda i: (0, 0, 0)), pl.BlockSpec((N_PAIR, S, 128), lambda i: (0, 0, 0))],
        out_specs=[full, _vec(128)], out_shape=[SDS((S, 128), f32), SDS((1, 128), f32)],
        scratch_shapes=[pltpu.VMEM((S, 128), f32)], name="forget_bwd",
        compiler_params=_params(("arbitrary",), 32 * S * 128, 24 * S * 128))(fz, b128, d_f_cols, d_f_rows)


def _fox_scores(q_ref, k_ref, fq_ref, fk_ref, qi, hh):
    n = (qi + 1) * TQ
    rows, hs = slice(qi * TQ, n), slice(hh * HD, (hh + 1) * HD)
    s = _nt(q_ref[rows, hs], k_ref[0:n, hs]) * SCALE + (fq_ref[rows, hh:hh + 1] - fk_ref[hh:hh + 1, 0:n])
    qpos = qi * TQ + lax.broadcasted_iota(jnp.int32, (TQ, n), 0)
    kpos = lax.broadcasted_iota(jnp.int32, (TQ, n), 1)
    return jnp.where(kpos <= qpos, s, NEG)


def _pair_cols(first):
    return pl.BlockSpec((S, 128), lambda p: (0, first + p))


def _fox_fwd(vr, fq, fk):
    def body(q_ref, k_ref, v_ref, fq_ref, fk_ref, o_ref, lse_ref):
        lse_ref[...] = jnp.zeros_like(lse_ref)
        for hh in range(2):
            hs = slice(hh * HD, (hh + 1) * HD)
            for qi in range(S // TQ):
                n = (qi + 1) * TQ
                rows = slice(qi * TQ, n)
                s = _fox_scores(q_ref, k_ref, fq_ref, fk_ref, qi, hh)
                m = jnp.max(s, axis=-1, keepdims=True)
                p = jnp.exp(s - m)
                den = jnp.sum(p, axis=-1, keepdims=True)
                o_ref[rows, hs] = jnp.dot((p * (1.0 / den)).astype(bf16), v_ref[0:n, hs], preferred_element_type=f32)
                lse_ref[rows, hh:hh + 1] = m + jnp.log(den)

    return pl.pallas_call(
        body, grid=(N_PAIR,), in_specs=[_pair_cols(0), _pair_cols(N_PAIR), _pair_cols(2 * N_PAIR), _PAIR_Q, _PAIR_K],
        out_specs=[_pair_cols(0), _PAIR_Q], out_shape=[SDS((S, FOX_W), f32), SDS((N_PAIR, S, 128), f32)],
        name="fox_fwd", compiler_params=_params(("parallel",), 12 * S * 128, 16 * TQ * S),
    )(vr, vr, vr, fq, fk)


def _fox_bwd(vr, fq, fk, lse, d_out, delta):
    def body(q_ref, k_ref, v_ref, do_ref, fq_ref, fk_ref, lse_ref, dl_ref, dq_ref, dk_ref, dv_ref, dfc_ref, dfr_ref,
             dk_sc, dv_sc):
        dfc_ref[...] = jnp.zeros_like(dfc_ref)
        dfr_ref[...] = jnp.zeros_like(dfr_ref)
        for hh in range(2):
            hs = slice(hh * HD, (hh + 1) * HD)
            dk_sc[...] = jnp.zeros_like(dk_sc)
            dv_sc[...] = jnp.zeros_like(dv_sc)
            for qi in range(S // TQ):
                n = (qi + 1) * TQ
                rows = slice(qi * TQ, n)
                q, do, k, v = q_ref[rows, hs], do_ref[rows, hs], k_ref[0:n, hs], v_ref[0:n, hs]
                p = jnp.exp(_fox_scores(q_ref, k_ref, fq_ref, fk_ref, qi, hh) - lse_ref[rows, hh:hh + 1])
                ds = p * (_nt(do, v) - dl_ref[rows, hh:hh + 1])
                dsb = ds.astype(bf16)
                dq_ref[rows, hs] = jnp.dot(dsb, k, preferred_element_type=f32) * SCALE
                dk_sc[0:n, :] += _tn(dsb, q) * SCALE
                dv_sc[0:n, :] += _tn(p.astype(bf16), do)
                dfc_ref[hh:hh + 1, 0:n] -= jnp.sum(ds, axis=0, keepdims=True)
                dfr_ref[rows, hh:hh + 1] = jnp.sum(ds, axis=-1, keepdims=True)
            dk_ref[:, hs] = dk_sc[...]
            dv_ref[:, hs] = dv_sc[...]

    cols = [_pair_cols(k * N_PAIR) for k in range(3)]
    return pl.pallas_call(
        body, grid=(N_PAIR,), in_specs=cols + [_pair_cols(0), _PAIR_Q, _PAIR_K, _PAIR_Q, _PAIR_Q],
        out_specs=[_pair_cols(0)] * 3 + [_PAIR_K, _PAIR_Q],
        out_shape=[SDS((S, FOX_W), f32)] * 3 + [SDS((N_PAIR, 8, S), f32), SDS((N_PAIR, S, 128), f32)],
        scratch_shapes=[pltpu.VMEM((S, HD), f32)] * 2, name="fox_bwd",
        compiler_params=_params(("parallel",), 32 * S * 128, 24 * TQ * S),
    )(vr, vr, vr, d_out, fq, fk, lse, delta)


def _merge_fwd(out_a, out_b, w_a, w_b, gf):
    cw = D // N_SHARD

    def body(oa_ref, ob_ref, wa_ref, wb_ref, ga_ref, gb_ref, ya_ref, yb_ref, mg_ref):
        oa, ob = oa_ref[...].astype(bf16), ob_ref[...].astype(bf16)
        for j in range(N_SHARD):
            cols = slice(j * cw, (j + 1) * cw)
            ya = jnp.dot(oa, wa_ref[j], preferred_element_type=f32)
            yb = jnp.dot(ob, wb_ref[j], preferred_element_type=f32)
            ya_ref[:, cols] = ya
            yb_ref[:, cols] = yb
            mg_ref[:, cols] = (jax.nn.sigmoid(ga_ref[:, cols]) * ya + jax.nn.sigmoid(gb_ref[:, cols]) * yb).astype(bf16)

    full = lambda a: pl.BlockSpec(a.shape, lambda i: (0, 0, 0))
    return pl.pallas_call(
        body, grid=(S // TM,),
        in_specs=[_row(DIL_W), _row(FOX_W), full(w_a), full(w_b), _row(D), pl.BlockSpec((TM, D), lambda i: (i, 1))],
        out_specs=[_row(D)] * 3, out_shape=[SDS((S, D), f32), SDS((S, D), f32), SDS((S, D), bf16)], name="merge_fwd",
        compiler_params=_params(("parallel",), 22 * TM * D + 2 * (DIL_W + FOX_W) * D, 16 * TM * D),
    )(out_a, out_b, w_a, w_b, gf, gf)


def _merge_bwd(d_mix, w_out, ya, yb, gf):
    def body(dx_ref, w_ref, ya_ref, yb_ref, ga_ref, gb_ref, dya_ref, dyb_ref, dg_ref):
        dm = _nt(dx_ref[...], w_ref[...])
        sa, sb = jax.nn.sigmoid(ga_ref[...]), jax.nn.sigmoid(gb_ref[...])
        dya_ref[...] = (dm * sa).astype(bf16)
        dyb_ref[...] = (dm * sb).astype(bf16)
        dg_ref[:, :D] = (dm * ya_ref[...] * sa * (1.0 - sa)).astype(bf16)
        dg_ref[:, D:] = (dm * yb_ref[...] * sb * (1.0 - sb)).astype(bf16)

    return pl.pallas_call(
        body, grid=(S // TM,),
        in_specs=[_row(D), _whole(w_out)] + [_row(D)] * 3 + [pl.BlockSpec((TM, D), lambda i: (i, 1))],
        out_specs=[_row(D), _row(D), _row(2 * D)],
        out_shape=[SDS((S, D), bf16), SDS((S, D), bf16), SDS((S, 2 * D), bf16)], name="proj_out_bwd_merge",
        compiler_params=_params(("parallel",), 26 * TM * D + 2 * D * D, 28 * TM * D))(d_mix, w_out, ya, yb, gf, gf)


def _branch_bwd(d_ya, d_yb, w_a, w_b, out_a, out_b):
    cw = D // N_SHARD

    def body(dya_ref, dyb_ref, wa_ref, wb_ref, oa_ref, ob_ref, doa_ref, dla_ref, dob_ref, dlb_ref):
        doa = jnp.zeros((TM, DIL_W), f32)
        dob = jnp.zeros((TM, FOX_W), f32)
        for j in range(N_SHARD):
            cols = slice(j * cw, (j + 1) * cw)
            doa += _nt(dya_ref[:, cols], wa_ref[j])
            dob += _nt(dyb_ref[:, cols], wb_ref[j])
        doa_ref[...] = doa
        dob_ref[...] = dob.astype(bf16)
        prod_a = doa * oa_ref[...]
        for h in range(SLOTS):
            hs = slice(h * HD, (h + 1) * HD)
            dla_ref[:, hs] = jnp.broadcast_to(jnp.sum(prod_a[:, hs], axis=-1, keepdims=True), (TM, HD))
        prod_b = dob * ob_ref[...]
        dlb_ref[...] = jnp.zeros_like(dlb_ref)
        for h in range(N_FOX):
            dlb_ref[h // 2, :, h % 2:h % 2 + 1] = jnp.sum(prod_b[:, h * HD:(h + 1) * HD], axis=-1, keepdims=True)

    full = lambda a: pl.BlockSpec(a.shape, lambda i: (0, 0, 0))
    return pl.pallas_call(
        body, grid=(S // TM,),
        in_specs=[_row(D), _row(D), full(w_a), full(w_b), _row(DIL_W), _row(FOX_W)],
        out_specs=[_row(DIL_W), _row(DIL_W), _row(FOX_W), pl.BlockSpec((N_PAIR, TM, 128), lambda i: (0, i, 0))],
        out_shape=[SDS((S, DIL_W), f32), SDS((S, DIL_W), f32), SDS((S, FOX_W), bf16), SDS((N_PAIR, S, 128), f32)],
        name="branch_bwd", compiler_params=_params(("parallel",), 8 * TM * D + 2 * (DIL_W + FOX_W) * D, 8 * TM * D),
    )(d_ya, d_yb, w_a, w_b, out_a, out_b)


def _branch_grads(out_a, out_b, d_ya, d_yb):
    cw = D // N_SHARD

    def body(oa_ref, ob_ref, dya_ref, dyb_ref, ga_ref, gb_ref):
        ga_ref[...] = _tn(oa_ref[...].astype(bf16), dya_ref[...]).astype(bf16)
        gb_ref[...] = _tn(ob_ref[...].astype(bf16), dyb_ref[...]).astype(bf16)

    whole = lambda w: pl.BlockSpec((S, w), lambda j: (0, 0))
    cols = pl.BlockSpec((S, cw), lambda j: (0, j))
    return pl.pallas_call(
        body, grid=(N_SHARD,), in_specs=[whole(DIL_W), whole(FOX_W), cols, cols],
        out_specs=[pl.BlockSpec((None, DIL_W, cw), lambda j: (j, 0, 0)), pl.BlockSpec((None, FOX_W, cw), lambda j: (j, 0, 0))],
        out_shape=[SDS((N_SHARD, DIL_W, cw), bf16), SDS((N_SHARD, FOX_W, cw), bf16)], name="grad_w_proj_ab",
        compiler_params=_params(("parallel",), 4 * S * (DIL_W + FOX_W) + 4 * S * cw + 4 * (DIL_W + FOX_W) * cw,
                                4 * S * (DIL_W + FOX_W)))(out_a, out_b, d_ya, d_yb)


FF_TN = F_FF // 2
FF_TM = 512


def _ffn_fwd(h, w_gate_t, w_up_t):
    def body(h_ref, wg_ref, wu_ref, g_ref, u_ref, a_ref):
        hb = h_ref[...]
        g = _nt(hb, wg_ref[...])
        u = _nt(hb, wu_ref[...])
        g_ref[...] = g
        u_ref[...] = u
        a_ref[...] = (g * jax.nn.sigmoid(g) * u).astype(bf16)

    tile = pl.BlockSpec((FF_TM, FF_TN), lambda j, i: (i, j))
    wspec = pl.BlockSpec((FF_TN, D), lambda j, i: (j, 0))
    return pl.pallas_call(
        body, grid=(F_FF // FF_TN, S // FF_TM),
        in_specs=[pl.BlockSpec((FF_TM, D), lambda j, i: (i, 0)), wspec, wspec], out_specs=[tile] * 3,
        out_shape=[SDS((S, F_FF), f32), SDS((S, F_FF), f32), SDS((S, F_FF), bf16)], name="ffn_fwd",
        compiler_params=_params(("parallel", "parallel"), 2 * FF_TM * D + 4 * D * FF_TN + 10 * FF_TM * FF_TN, 16 * FF_TM * FF_TN),
    )(h, w_gate_t, w_up_t)


def _ffn_bwd_act(d_ff, w_down, g_act, u_act):
    def body(d_ref, wd_ref, g_ref, u_ref, dg_ref, du_ref):
        da = _nt(d_ref[...], wd_ref[...])
        g = g_ref[...]
        sg = jax.nn.sigmoid(g)
        du_ref[...] = (da * g * sg).astype(bf16)
        dg_ref[...] = (da * u_ref[...] * sg * (1.0 + g * (1.0 - sg))).astype(bf16)

    tile = pl.BlockSpec((FF_TM, FF_TN), lambda j, i: (i, j))
    return pl.pallas_call(
        body, grid=(F_FF // FF_TN, S // FF_TM),
        in_specs=[pl.BlockSpec((FF_TM, D), lambda j, i: (i, 0)), pl.BlockSpec((FF_TN, D), lambda j, i: (j, 0)), tile, tile],
        out_specs=[tile, tile], out_shape=[SDS((S, F_FF), bf16)] * 2, name="ffn_bwd_act",
        compiler_params=_params(("parallel", "parallel"), 2 * FF_TM * D + 2 * D * FF_TN + 12 * FF_TM * FF_TN, 16 * FF_TM * FF_TN),
    )(d_ff, w_down, g_act, u_act)


def _row_tile(rows):
    return next(t for t in (376, 128, 176, 64, 32, 16, 8) if rows % t == 0)


def _adamw_math(w, g, m, v):
    c1 = 1.0 - ADAM_B1 ** ADAM_STEP
    c2 = 1.0 - ADAM_B2 ** ADAM_STEP
    m_new = ADAM_B1 * m + (1.0 - ADAM_B1) * g
    v_new = ADAM_B2 * v + (1.0 - ADAM_B2) * (g * g)
    return -ADAM_LR * ((m_new / c1) / (jnp.sqrt(v_new / c2) + ADAM_EPS) + ADAM_WD * w), m_new, v_new


def _adamw(w, g, m, v, name):
    rows, cols = w.shape
    tm = _row_tile(rows)

    def body(w_ref, g_ref, m_ref, v_ref, d_ref, nm_ref, nv_ref):
        d_ref[...], nm_ref[...], nv_ref[...] = _adamw_math(w_ref[...], g_ref[...], m_ref[...], v_ref[...])

    spec = pl.BlockSpec((tm, cols), lambda i: (i, 0))
    return pl.pallas_call(
        body, grid=(rows // tm,), in_specs=[spec] * 4, out_specs=[spec] * 3, out_shape=[SDS(w.shape, f32)] * 3,
        name=name, compiler_params=_params(("parallel",), 28 * tm * cols, 16 * tm * cols))(w, g, m, v)


def _adamw_halves(w, g_mine, g_theirs, m, v, name):
    rows, cols = w.shape
    tm = _row_tile(rows // 2)
    per_half = rows // 2 // tm
    core = lax.axis_index("c").astype(jnp.int32).reshape(1)

    def body(c_ref, w_ref, gm_ref, gt_ref, m_ref, v_ref, g_ref, d_ref, nm_ref, nv_ref):
        mine = pl.program_id(0) // per_half == c_ref[0]
        g = jnp.where(mine, gm_ref[...], gt_ref[...])
        g_ref[...] = g
        d_ref[...], nm_ref[...], nv_ref[...] = _adamw_math(w_ref[...], g, m_ref[...], v_ref[...])

    spec = pl.BlockSpec((tm, cols), lambda i, c_ref: (i, 0))
    in_half = lambda i, first: jnp.clip(i - first * per_half, 0, per_half - 1)
    grid_spec = pltpu.PrefetchScalarGridSpec(
        num_scalar_prefetch=1, grid=(rows // tm,),
        in_specs=[spec, pl.BlockSpec((tm, cols), lambda i, c_ref: (in_half(i, c_ref[0]), 0)),
                  pl.BlockSpec((tm, cols), lambda i, c_ref: (in_half(i, 1 - c_ref[0]), 0)), spec, spec],
        out_specs=[spec] * 4)
    return pl.pallas_call(
        body, grid_spec=grid_spec, out_shape=[SDS(w.shape, f32)] * 4, name=name,
        compiler_params=_params(("parallel",), 36 * tm * cols, 16 * tm * cols))(core, w, g_mine, g_theirs, m, v)


_ANY = pl.BlockSpec(memory_space=pl.ANY)


def _place():
    x, y, c = lax.axis_index("x"), lax.axis_index("y"), lax.axis_index("c")
    chips = [(1 - x, y), (x, 1 - y), (1 - x, 1 - y)]
    return x, y, c, chips


def _halved(t):
    return t.reshape(t.shape[:-2] + (2, t.shape[-2] // 2, t.shape[-1]))


def _gather_body(src, out, send_ici, recv_ici, send_d2d, recv_d2d):
    x, y, c, chips = _place()
    sibling = (x, y, 1 - c)
    me_j = 2 * x + y
    sends = []
    for a in range(len(src)):
        for p in range(3):
            cp = pltpu.make_async_remote_copy(
                src_ref=src[a].at[c], dst_ref=out[a].at[me_j, c], send_sem=send_ici.at[a, p],
                recv_sem=recv_ici.at[a, p], device_id=(*chips[p], c), device_id_type=MESH)
            cp.start()
            sends.append(cp)
    for a in range(len(src)):
        for p, (px, py) in enumerate(chips):
            blk = out[a].at[2 * px + py, c]
            pltpu.make_async_remote_copy(
                src_ref=blk, dst_ref=blk, send_sem=send_ici.at[a, p], recv_sem=recv_ici.at[a, p],
                device_id=sibling, device_id_type=MESH).wait_recv()
            fw = pltpu.make_async_remote_copy(
                src_ref=blk, dst_ref=blk, send_sem=send_d2d.at[a, p], recv_sem=recv_d2d.at[a, p],
                device_id=sibling, device_id_type=MESH)
            fw.start()
            sends.append(fw)
    for a in range(len(src)):
        for p, (px, py) in enumerate(chips):
            blk = out[a].at[2 * px + py, 1 - c]
            pltpu.make_async_remote_copy(
                src_ref=blk, dst_ref=blk, send_sem=send_d2d.at[a, p], recv_sem=recv_d2d.at[a, p],
                device_id=sibling, device_id_type=MESH).wait_recv()
    for cp in sends:
        cp.wait_send()


def _handshake(peers):
    barrier = pltpu.get_barrier_semaphore()
    for peer in peers:
        pl.semaphore_signal(barrier, inc=1, device_id=peer, device_id_type=MESH)
    pl.semaphore_wait(barrier, len(peers))


_SEQUENCER = dict(axis_name="sequencer", num_cores=1)
GATHER_LATE_ID, SCATTER_EARLY_ID, SWAP_EARLY_ID, GATHER_FIRST_ID, SCATTER_LATE_ID = 1, 2, 3, 4, 5


def _all_gather_async(shards, after, name, collective_id):
    n, k = len(shards), len(after)

    def body(*refs):
        x, y, c, chips = _place()
        _handshake([(*chip, c) for chip in chips] + [(x, y, 1 - c)])
        _gather_body(refs[:n], refs[n + k:2 * n + k], *refs[2 * n + k:])

    return pl.kernel(
        body, out_type=[SDS((N_SHARD,) + t.shape, t.dtype) for t in shards],
        mesh=plsc.ScalarSubcoreMesh(**_SEQUENCER), scratch_types=[pltpu.SemaphoreType.DMA((n, 3))] * 4,
        compiler_params=pltpu.CompilerParams(collective_id=collective_id), name=name)(*shards, *after)


def _pair_swap(grads):
    n = len(grads)

    def body(*refs):
        src, out, send_sems, recv_sems = refs[:n], refs[n:2 * n], refs[2 * n], refs[2 * n + 1]
        x, y, c, _ = _place()
        copies = [pltpu.make_async_remote_copy(
            src_ref=src[a].at[:, 1 - c], dst_ref=out[a], send_sem=send_sems.at[a], recv_sem=recv_sems.at[a],
            device_id=(x, y, 1 - c), device_id_type=MESH) for a in range(n)]
        for cp in copies:
            cp.start()
        for cp in copies:
            cp.wait()

    return pl.pallas_call(
        body, in_specs=[_ANY] * n, out_specs=[_ANY] * n,
        out_shape=[SDS((N_SHARD,) + t.shape[2:], t.dtype) for t in grads],
        scratch_shapes=[pltpu.SemaphoreType.DMA((n,)), pltpu.SemaphoreType.DMA((n,))], name="pair_swap",
        compiler_params=pltpu.CompilerParams(has_side_effects=True))(*grads)


def _pair_swap_early(grads):
    n = len(grads)

    def body(*refs):
        src, out, send_sems, recv_sems = refs[:n], refs[n:2 * n], refs[2 * n], refs[2 * n + 1]
        x, y, c, _ = _place()
        _handshake([(x, y, 1 - c)])
        copies = [pltpu.make_async_remote_copy(
            src_ref=src[a].at[:, 1 - c], dst_ref=out[a], send_sem=send_sems.at[a], recv_sem=recv_sems.at[a],
            device_id=(x, y, 1 - c), device_id_type=MESH) for a in range(n)]
        for cp in copies:
            cp.start()
        for cp in copies:
            cp.wait()

    return pl.kernel(
        body, out_type=[SDS((N_SHARD,) + t.shape[2:], t.dtype) for t in grads],
        mesh=plsc.ScalarSubcoreMesh(**_SEQUENCER), scratch_types=[pltpu.SemaphoreType.DMA((n,))] * 2,
        compiler_params=pltpu.CompilerParams(collective_id=SWAP_EARLY_ID), name="pair_swap_early")(*grads)


def _scatter_early(parts):
    n = len(parts)

    def body(*refs):
        part, recv, send_sems, recv_sems = refs[:n], refs[n:2 * n], refs[2 * n], refs[2 * n + 1]
        x, y, c, chips = _place()
        _handshake([(*chip, c) for chip in chips])
        me_j = 2 * x + y
        sends = []
        for a in range(n):
            for p, (px, py) in enumerate(chips):
                cp = pltpu.make_async_remote_copy(
                    src_ref=part[a].at[2 * px + py], dst_ref=recv[a].at[me_j], send_sem=send_sems.at[a, p],
                    recv_sem=recv_sems.at[a, p], device_id=(px, py, c), device_id_type=MESH)
                cp.start()
                sends.append(cp)
        for a in range(n):
            for p, (px, py) in enumerate(chips):
                slot = recv[a].at[2 * px + py]
                pltpu.make_async_remote_copy(
                    src_ref=slot, dst_ref=slot, send_sem=send_sems.at[a, p], recv_sem=recv_sems.at[a, p],
                    device_id=(px, py, c), device_id_type=MESH).wait_recv()
        for cp in sends:
            cp.wait_send()

    return pl.kernel(
        body, out_type=[SDS(t.shape, t.dtype) for t in parts],
        mesh=plsc.ScalarSubcoreMesh(**_SEQUENCER), scratch_types=[pltpu.SemaphoreType.DMA((n, 3))] * 2,
        compiler_params=pltpu.CompilerParams(collective_id=SCATTER_EARLY_ID), name="scatter_early")(*parts)


def _pair_sum(grads, other, name):
    _, _, rows, cols = grads.shape
    tr = _row_tile(rows)
    core = lax.axis_index("c").astype(jnp.int32).reshape(1)

    def body(c_ref, g_ref, o_ref, out_ref):
        out_ref[...] = (g_ref[...].astype(f32) + o_ref[...].astype(f32)).astype(bf16)

    grid_spec = pltpu.PrefetchScalarGridSpec(
        num_scalar_prefetch=1, grid=(N_SHARD, rows // tr),
        in_specs=[pl.BlockSpec((None, None, tr, cols), lambda j, i, c_ref: (j, c_ref[0], i, 0)),
                  pl.BlockSpec((None, tr, cols), lambda j, i, c_ref: (j, i, 0))],
        out_specs=pl.BlockSpec((None, tr, cols), lambda j, i, c_ref: (j, i, 0)))
    return pl.pallas_call(
        body, grid_spec=grid_spec, out_shape=SDS((N_SHARD, rows, cols), bf16), name=name,
        compiler_params=_params(("parallel", "parallel"), 10 * tr * cols, 12 * tr * cols))(core, grads, other)


def _scatter_partials(parts, small):
    n = len(parts)

    def body(*refs):
        part, small_ref, recv, small_all_ref = refs[:n], refs[n], refs[n + 1:2 * n + 1], refs[2 * n + 1]
        send_sems, recv_sems, ssend, srecv, local_sem = refs[2 * n + 2:]
        x, y, c, chips = _place()
        flip = lambda a, bit: 1 - a if bit else a
        peers = [(flip(x, k & 4), flip(y, k & 2), flip(c, k & 1)) for k in range(1, 8)]
        _handshake(peers)
        me_j = 2 * x + y
        me_dev = 4 * x + 2 * y + c
        own = pltpu.make_async_copy(small_ref, small_all_ref.at[me_dev], local_sem)
        own.start()
        sends = []
        for a in range(n):
            for p, (px, py) in enumerate(chips):
                cp = pltpu.make_async_remote_copy(
                    src_ref=part[a].at[2 * px + py], dst_ref=recv[a].at[me_j], send_sem=send_sems.at[a, p],
                    recv_sem=recv_sems.at[a, p], device_id=(px, py, c), device_id_type=MESH)
                cp.start()
                sends.append(cp)
        for k, to in enumerate(peers):
            cp = pltpu.make_async_remote_copy(
                src_ref=small_ref, dst_ref=small_all_ref.at[me_dev],
                send_sem=ssend.at[k], recv_sem=srecv.at[k], device_id=to, device_id_type=MESH)
            cp.start()
            sends.append(cp)
        for a in range(n):
            for p, (px, py) in enumerate(chips):
                slot = recv[a].at[2 * px + py]
                pltpu.make_async_remote_copy(
                    src_ref=slot, dst_ref=slot, send_sem=send_sems.at[a, p], recv_sem=recv_sems.at[a, p],
                    device_id=(px, py, c), device_id_type=MESH).wait_recv()
        for k, (px, py, pc) in enumerate(peers):
            slot = small_all_ref.at[4 * px + 2 * py + pc]
            pltpu.make_async_remote_copy(
                src_ref=slot, dst_ref=slot, send_sem=ssend.at[k], recv_sem=srecv.at[k],
                device_id=(px, py, pc), device_id_type=MESH).wait_recv()
        for cp in sends:
            cp.wait_send()
        own.wait()

    return pl.kernel(
        body, out_type=[SDS(t.shape, t.dtype) for t in parts] + [SDS((8, SMALL_ROWS, D), f32)],
        mesh=plsc.ScalarSubcoreMesh(**_SEQUENCER),
        scratch_types=[pltpu.SemaphoreType.DMA((n, 3)), pltpu.SemaphoreType.DMA((n, 3)),
                       pltpu.SemaphoreType.DMA((7,)), pltpu.SemaphoreType.DMA((7,)), pltpu.SemaphoreType.DMA],
        compiler_params=pltpu.CompilerParams(collective_id=SCATTER_LATE_ID), name="scatter_partials")(*parts, small)


def _sum_partials(part, recv, name):
    _, rows, cols = recv.shape
    tr = _row_tile(rows)
    me = (2 * lax.axis_index("x") + lax.axis_index("y")).astype(jnp.int32).reshape(1)

    def body(me_ref, mine, r0, r1, r2, r3, out_ref):
        acc = None
        for j, r in enumerate((r0, r1, r2, r3)):
            term = jnp.where(me_ref[0] == j, mine[...], r[...]).astype(f32)
            acc = term if acc is None else acc + term
        out_ref[...] = acc

    slot = lambda j: pl.BlockSpec((None, tr, cols), lambda i, me_ref: (jnp.where(me_ref[0] == j, j ^ 1, j), i, 0))
    grid_spec = pltpu.PrefetchScalarGridSpec(
        num_scalar_prefetch=1, grid=(rows // tr,),
        in_specs=[pl.BlockSpec((None, tr, cols), lambda i, me_ref: (me_ref[0], i, 0)), slot(0), slot(1), slot(2), slot(3)],
        out_specs=pl.BlockSpec((tr, cols), lambda i, me_ref: (i, 0)))
    return pl.pallas_call(
        body, grid_spec=grid_spec, out_shape=SDS((rows, cols), f32), name=name,
        compiler_params=_params(("parallel",), 14 * tr * cols, 12 * tr * cols))(me, part, recv, recv, recv, recv)


def _sum_small(small_all):
    def body(small_ref, out_ref):
        tot = small_ref[0]
        for k in range(1, 8):
            tot = tot + small_ref[k]
        out_ref[...] = tot

    return pl.pallas_call(
        body, grid=(1,), in_specs=[pl.BlockSpec((8, SMALL_ROWS, D), lambda i: (0, 0, 0))],
        out_specs=pl.BlockSpec((SMALL_ROWS, D), lambda i: (0, 0)), out_shape=SDS((SMALL_ROWS, D), f32),
        name="sum_small", compiler_params=_params(("arbitrary",), 36 * SMALL_ROWS * D))(small_all)


def _swap_halves(halves, name):
    n = len(halves)

    def body(*refs):
        src, out, send_sems, recv_sems = refs[:n], refs[n:2 * n], refs[2 * n], refs[2 * n + 1]
        x, y, c, _ = _place()
        copies = [pltpu.make_async_remote_copy(
            src_ref=src[a], dst_ref=out[a], send_sem=send_sems.at[a], recv_sem=recv_sems.at[a],
            device_id=(x, y, 1 - c), device_id_type=MESH) for a in range(n)]
        for cp in copies:
            cp.start()
        for cp in copies:
            cp.wait()

    return pl.pallas_call(
        body, in_specs=[_ANY] * n, out_specs=[_ANY] * n, out_shape=[SDS(t.shape, f32) for t in halves],
        scratch_shapes=[pltpu.SemaphoreType.DMA((n,))] * 2, name=name,
        compiler_params=pltpu.CompilerParams(has_side_effects=True))(*halves)


def _kernel_layout(name, t):
    t = t[0]
    if name in TRANSPOSED:
        t = jnp.swapaxes(t, 0, 1)
    return _pad_rows(t, SHARD_SHAPE[name][0])


def _harness_layout(name, t):
    if name == "w_in":
        t = t[:IN_SHARD]
    if name in TRANSPOSED:
        t = jnp.swapaxes(t, 0, 1)
    return t[None]


def _pad_rows(t, rows):
    return t if t.shape[0] == rows else jnp.pad(t, ((0, rows - t.shape[0]), (0, 0)))


_QA, _KA, _VA, _QB, _F, _GAB = 0, 768, 1536, 2304, 3840, 3848


def _spans(a, b):
    return [(j, max(a, j * IN_SHARD) - j * IN_SHARD, max(a, j * IN_SHARD) - a,
             min(b, (j + 1) * IN_SHARD) - max(a, j * IN_SHARD))
            for j in range(N_SHARD) if max(a, j * IN_SHARD) < min(b, (j + 1) * IN_SHARD)]


_LANES = pl.BlockSpec((N_SHARD, IN_SHARD_PAD, 128), lambda c: (0, 0, c))


def _split_w_in(shards):
    group = [[(o + g * DIL_W, o + (g + 1) * DIL_W) for o in (_QA, _KA, _VA)] for g in range(3)]
    fox = [[(_QB + k * FOX_W, _QB + (k + 1) * FOX_W)] for k in range(3)]
    wanted = group + fox + [[(_QB, _F)], [(_F, _GAB)], [(_GAB, IN_COLS)]]
    rows = [sum(b - a for a, b in w) for w in wanted]
    rows[7] = 128

    def body(s_ref, *o_refs):
        for o_ref, want in zip(o_refs, wanted):
            at = 0
            for a, b in want:
                for j, src, off, n in _spans(a, b):
                    o_ref[at + off:at + off + n, :] = s_ref[j, src:src + n, :]
                at += b - a
        o_refs[7][N_FOX:, :] = jnp.zeros((128 - N_FOX, 128), bf16)

    return pl.pallas_call(
        body, grid=(D // 128,), in_specs=[_LANES], out_specs=[pl.BlockSpec((r, 128), lambda c: (0, c)) for r in rows],
        out_shape=[SDS((r, D), bf16) for r in rows], name="split_w_in",
        compiler_params=_params(("parallel",), 2 * 128 * (N_SHARD * IN_SHARD_PAD + sum(rows))))(shards)


def _join_w_in(g_a, g_fox, g_f, g_gab):
    parts = [(g_a[k], o, o + DIL_W) for o in (0, DIL_W, 2 * DIL_W) for k in range(3)]
    parts += [(t, 0, FOX_W) for t in g_fox] + [(g_f, 0, N_FOX), (g_gab, 0, 2 * D)]
    arrays = list(g_a) + list(g_fox) + [g_f, g_gab]
    index = {id(t): i for i, t in enumerate(arrays)}

    def body(*refs):
        o_ref = refs[-1]
        o_ref[:, IN_SHARD:, :] = jnp.zeros((N_SHARD, IN_SHARD_PAD - IN_SHARD, 128), bf16)
        at = 0
        for t, lo, hi in parts:
            src_ref = refs[index[id(t)]]
            for j, dst, off, n in _spans(at, at + hi - lo):
                o_ref[j, dst:dst + n, :] = src_ref[lo + off:lo + off + n, :].astype(bf16)
            at += hi - lo

    return pl.pallas_call(
        body, grid=(D // 128,), in_specs=[pl.BlockSpec((t.shape[0], 128), lambda c: (0, c)) for t in arrays],
        out_specs=_LANES, out_shape=SDS((N_SHARD, IN_SHARD_PAD, D), bf16), name="join_w_in",
        compiler_params=_params(("parallel",), 2 * 128 * (N_SHARD * IN_SHARD_PAD + sum(t.shape[0] for t in arrays))),
    )(*arrays)


def _full_weights(gathered):
    full = {n: t.reshape((N_SHARD,) + SHARD_SHAPE[n]) for n, t in gathered.items()}
    out = {}
    if "w_in" in full:
        pieces = _split_w_in(full["w_in"])
        out.update(w_a_t=pieces[0:3], w_fox_t=pieces[3:6], w_vr_t=pieces[6], w_f_t=pieces[7], w_gab_t=pieces[8])
    if "w_out" in full:
        out.update(
            w_a4=full["w_proj_a"],
            w_b4=full["w_proj_b"],
            w_out=full["w_out"].reshape(D, D),
            w_gate_t=full["w_ffn_gate"].reshape(F_FF, D),
            w_up_t=full["w_ffn_up"].reshape(F_FF, D),
            w_down=full["w_ffn_down"].reshape(F_FF, D))
    return out


def _sharded_grads(g):
    full = dict(w_in=_join_w_in(g["w_a_t"], g["w_fox_t"], g["w_f_t"], g["w_gab_t"]), w_proj_a=g["w_a4"],
                w_proj_b=g["w_b4"], w_out=g["w_out"], w_ffn_gate=g["w_gate_t"], w_ffn_up=g["w_up_t"],
                w_ffn_down=g["w_down"])
    return {n: _halved(full[n].reshape((N_SHARD,) + SHARD_SHAPE[n])) for n in W_NAMES}


def _local_step(x, target, wt, b_forget, g_mix_pre, g_mix_post, g_ffn_pre, g_ffn_post, late=None):
    tables = _rope_tables()
    b128 = jnp.pad(b_forget, ((0, 0), (0, 128 - N_FOX)))
    dils = tuple(d for _, d in DIL_GROUPS[1:])

    hs = _norm_fwd([x] + list(_perm_rows([x], dils, "perm_x")), g_mix_pre)
    h1 = hs[0]
    if callable(wt):
        wt = wt(h1)
    qkv = [_rope_fwd(g, _mm([(hs[g], wt["w_a_t"][g])], "nt", f32, tm=1024, tn=QKV_W, name=f"proj_a_{g}"), tables)
           for g in range(3)]
    vr = _mm([(h1, wt["w_vr_t"])], "nt", bf16, tm=1024, tn=VR_W // 2, name="proj_vr")
    gab = _mm([(h1, wt["w_gab_t"])], "nt", f32, tm=512, tn=2 * D, name="proj_gab")
    fz = _mm([(h1, wt["w_f_t"])], "nt", f32, tm=1024, tn=128, name="proj_f")
    dil = [_dil_fwd(g, qkv[g]) for g in range(3)]
    out_a, lse_a = _dil_combine([o for o, _ in dil], [l for _, l in dil])
    f_q, f_k = _forget_fwd(fz, b128)
    out_b, lse_b = _fox_fwd(vr, f_q, f_k)
    if late is not None:
        wt = {**wt, **late(out_b)}
    ya, yb, merged = _merge_fwd(out_a, out_b, wt["w_a4"], wt["w_b4"], gab)
    mix, x2, h3 = _resid_norm_fwd(x, merged, wt["w_out"], g_mix_post, g_ffn_pre)
    g_act, u_act, a_act = _ffn_fwd(h3, wt["w_gate_t"], wt["w_up_t"])
    sq_err, dy, d_ff, dg_ffn_post = _loss_head(x2, a_act, wt["w_down"], g_ffn_post, target)

    grads = {}
    d_g, d_u = _ffn_bwd_act(d_ff, wt["w_down"], g_act, u_act)
    grads["w_down"] = _mm([(a_act, d_ff)], "tn", bf16, tm=FF_TN, tn=512, name="grad_w_down")
    grads["w_gate_t"] = _mm([(d_g, h3)], "tn", bf16, tm=FF_TN, tn=512, name="grad_w_gate")
    grads["w_up_t"] = _mm([(d_u, h3)], "tn", bf16, tm=FF_TN, tn=512, name="grad_w_up")
    dx2, d_mix, dg_ffn_pre, dg_mix_post = _norm_bwd_mid(dy, d_g, d_u, wt["w_gate_t"], wt["w_up_t"], x2, mix,
                                                        g_ffn_pre, g_mix_post)

    grads["w_out"] = _mm([(merged, d_mix)], "tn", bf16, tm=D, tn=D, name="grad_w_out")
    d_ya, d_yb, d_gab = _merge_bwd(d_mix, wt["w_out"], ya, yb, gab)
    grads["w_a4"], grads["w_b4"] = _branch_grads(out_a, out_b, d_ya, d_yb)
    d_out_a, delta_a, d_out_b, delta_b = _branch_bwd(d_ya, d_yb, wt["w_a4"], wt["w_b4"], out_a, out_b)

    perm = _perm_rows([d_out_a, delta_a, lse_a], dils, "perm_dil_bwd")
    aux = [(d_out_a, delta_a, lse_a)] + [tuple(perm[k * len(dils) + i] for k in range(3)) for i in range(len(dils))]
    d_qkv = []
    for g in range(3):
        dq, dk, dv = _dil_bwd(g, qkv[g], *aux[g])
        d_qkv.append(_rope_bwd(g, dq, dk, dv, tables))
    *d_fox, d_f_cols, d_f_rows = _fox_bwd(vr, f_q, f_k, lse_b, d_out_b, delta_b)
    d_z, d_b128 = _forget_bwd(fz, b128, d_f_cols, d_f_rows)

    grads["w_a_t"] = [_mm([(d_qkv[g], hs[g])], "tn", bf16, tm=QKV_W, tn=D, name=f"grad_w_a_{g}") for g in range(3)]
    grads["w_fox_t"] = [_mm([(d_fox[k], h1)], "tn", bf16, tm=FOX_W, tn=D, name=f"grad_w_fox_{k}") for k in range(3)]
    grads["w_gab_t"] = _mm([(d_gab, h1)], "tn", bf16, tm=D, tn=D, name="grad_w_gab")
    grads["w_f_t"] = _mm([(d_z, h1)], "tn", bf16, tm=128, tn=D, name="grad_w_f")
    d_h1_nat = _mm([(d_qkv[0], wt["w_a_t"][0])] + list(zip(d_fox, wt["w_fox_t"]))
                   + [(d_gab, wt["w_gab_t"]), (d_z, wt["w_f_t"])], "nn", f32, tm=512, tn=512, name="proj_in_bwd")
    d_h1_dil = [_mm([(d_qkv[g], wt["w_a_t"][g])], "nn", f32, tm=1024, tn=D, name=f"proj_a_bwd_{g}") for g in (1, 2)]
    d_h1 = _unperm_sum(d_h1_nat, d_h1_dil, dils, "unperm_d_h1")
    grad_x, dg_mix_pre = _norm_bwd_in(dx2, d_h1, x, g_mix_pre)

    small = dict(b_forget=d_b128[:, :N_FOX], norm_mix_pre=dg_mix_pre, norm_mix_post=dg_mix_post,
                 norm_ffn_pre=dg_ffn_pre, norm_ffn_post=dg_ffn_post)
    grads["mid_backward"] = d_qkv[0]
    return sq_err, grad_x, grads, small


NORMS = ("norm_mix_pre", "norm_mix_post", "norm_ffn_pre", "norm_ffn_post")
ORDER = ("w_in", "w_proj_a", "w_proj_b", "w_out", "b_forget", "w_ffn_gate", "w_ffn_up", "w_ffn_down") + NORMS


def kernel(x, w_in, w_proj_a, w_proj_b, w_out, b_forget, w_ffn_gate, w_ffn_up, w_ffn_down, norm_mix_pre, norm_mix_post, norm_ffn_pre, norm_ffn_post, loss_target, m_w_in, m_w_proj_a, m_w_proj_b, m_w_out, m_b_forget, m_w_ffn_gate, m_w_ffn_up, m_w_ffn_down, m_norm_mix_pre, m_norm_mix_post, m_norm_ffn_pre, m_norm_ffn_post, v_w_in, v_w_proj_a, v_w_proj_b, v_w_out, v_b_forget, v_w_ffn_gate, v_w_ffn_up, v_w_ffn_down, v_norm_mix_pre, v_norm_mix_post, v_norm_ffn_pre, v_norm_ffn_post):
    given = dict(w_in=w_in, w_proj_a=w_proj_a, w_proj_b=w_proj_b, w_out=w_out, w_ffn_gate=w_ffn_gate,
                 w_ffn_up=w_ffn_up, w_ffn_down=w_ffn_down)
    given_m = dict(w_in=m_w_in, w_proj_a=m_w_proj_a, w_proj_b=m_w_proj_b, w_out=m_w_out, w_ffn_gate=m_w_ffn_gate,
                   w_ffn_up=m_w_ffn_up, w_ffn_down=m_w_ffn_down)
    given_v = dict(w_in=v_w_in, w_proj_a=v_w_proj_a, w_proj_b=v_w_proj_b, w_out=v_w_out, w_ffn_gate=v_w_ffn_gate,
                   w_ffn_up=v_w_ffn_up, w_ffn_down=v_w_ffn_down)
    w, m, v = ({n: _kernel_layout(n, t[n]) for n in W_NAMES} for t in (given, given_m, given_v))
    small_w = dict(b_forget=b_forget, norm_mix_pre=norm_mix_pre, norm_mix_post=norm_mix_post,
                   norm_ffn_pre=norm_ffn_pre, norm_ffn_post=norm_ffn_post)
    small_m = dict(b_forget=m_b_forget, norm_mix_pre=m_norm_mix_pre, norm_mix_post=m_norm_mix_post,
                   norm_ffn_pre=m_norm_ffn_pre, norm_ffn_post=m_norm_ffn_post)
    small_v = dict(b_forget=v_b_forget, norm_mix_pre=v_norm_mix_pre, norm_mix_post=v_norm_mix_post,
                   norm_ffn_pre=v_norm_ffn_pre, norm_ffn_post=v_norm_ffn_post)

    own = [_halved(w[n].astype(bf16)) for n in W_NAMES]
    chip = 2 * lax.axis_index("x") + lax.axis_index("y")
    exchanged = {"first": _all_gather_async(own[:1], [], "all_gather_first", GATHER_FIRST_ID)}
    fill = lambda ts, mine: [lax.dynamic_update_index_in_dim(t, o, chip, 0) for t, o in zip(ts, mine)]

    def first_weights(ready):
        arrived, _ = lax.optimization_barrier((list(exchanged["first"]), ready))
        exchanged["late"] = _all_gather_async(own[1:], [arrived[0][0, 0, :16, :128]], "all_gather_late", GATHER_LATE_ID)
        return _full_weights(dict(zip(W_NAMES[:1], fill(arrived, own[:1]))))

    def late_weights(ready):
        arrived, _ = lax.optimization_barrier((list(exchanged["late"]), ready))
        return _full_weights(dict(zip(W_NAMES[1:], fill(arrived, own[1:]))))

    sq_err, grad_x, grads, small = _local_step(x[0], loss_target[0], first_weights, b_forget, norm_mix_pre,
                                               norm_mix_post, norm_ffn_pre, norm_ffn_post, late=late_weights)

    g4 = _sharded_grads(grads)
    stack = lambda t, extra: jnp.concatenate(
        [jnp.pad(t["b_forget"], ((0, 0), (0, D - N_FOX)))] + [t[n] for n in NORMS]
        + [jnp.pad(extra, ((0, SMALL_ROWS - LOSS_ROW - 1), (0, D - extra.shape[1])), constant_values=1.0)], axis=0)
    early, _ = lax.optimization_barrier((list(_pair_swap_early([g4[n] for n in W_NAMES[1:]])), grads["mid_backward"]))
    other = list(_pair_swap([g4["w_in"]])) + early
    parts = [_pair_sum(g4[n], o, "pair_sum_" + n) for n, o in zip(W_NAMES, other)]
    recv_early = _scatter_early(parts[1:])
    recv_in, small_all = _scatter_partials(parts[:1], stack(small, sq_err))

    g_shard, delta, new_m, new_v = {}, {}, {}, {}

    def finish(names, parts, recv):
        halves = [_sum_partials(p, r, "sum_partials_" + n) for n, p, r in zip(names, parts, recv)]
        theirs = _swap_halves(halves, "swap_halves_" + names[0])
        for n, mine, other_half in zip(names, halves, theirs):
            g_shard[n], delta[n], new_m[n], new_v[n] = _adamw_halves(w[n], mine, other_half, m[n], v[n], "adamw_" + n)

    recv_early, _ = lax.optimization_barrier((list(recv_early), parts[0]))
    finish(W_NAMES[1:], parts[1:], recv_early)
    (recv_in, small_all), _ = lax.optimization_barrier(((recv_in, small_all), [delta[n] for n in W_NAMES[1:]]))
    finish(W_NAMES[:1], parts[:1], [recv_in])
    small_sum = _sum_small(small_all)
    loss = small_sum[LOSS_ROW, 0] * (0.5 / D)
    ones = jnp.ones((1, 128), f32)
    sd, sm, sv = _adamw(stack(small_w, ones), small_sum, stack(small_m, ones), stack(small_v, ones), "adamw_small")

    outs = [loss, grad_x[None]]
    for big, st in ((g_shard, small_sum), (delta, sd), (new_m, sm), (new_v, sv)):
        t = {n: _harness_layout(n, big[n]) for n in W_NAMES}
        t["b_forget"] = st[0:1, :N_FOX]
        for i, n in enumerate(NORMS):
            t[n] = st[i + 1:i + 2]
        outs += [t[n] for n in ORDER]
    return tuple(outs)
```

```python
import functools
import math

import jax
import jax.numpy as jnp
import numpy as np
from jax import lax
from jax.experimental import pallas as pl
from jax.experimental.pallas import tpu as pltpu
from jax.experimental.pallas import tpu_sc as plsc

f32 = jnp.float32
bf16 = jnp.bfloat16
SDS = jax.ShapeDtypeStruct
MESH = pl.DeviceIdType.MESH

S = 2048
D = 1024
HD = 64
BLK = 128
N_FOX = 8
FOX_W = N_FOX * HD
DIL_GROUPS = ((128, 1), (512, 4), (2048, 16))
SLOTS = 4
DIL_W = SLOTS * HD
QKV_W = 3 * DIL_W
VR_W = 3 * FOX_W
GF_W = 2 * D + 128
F_FF = 2816
ROPE_DIM = 16
ROPE_THETA = 500000.0
EPS = 1e-6
NEG = -1e30
SCALE = 1.0 / math.sqrt(HD)
IN_COLS = 5896
N_SHARD = 4

ADAM_LR, ADAM_B1, ADAM_B2, ADAM_EPS, ADAM_WD, ADAM_STEP = 0.001, 0.9, 0.999, 1e-08, 0.01, 10

VMEM_V7X = 64 * 1024 * 1024
VMEM_PLAN_MAX = VMEM_V7X - 8 * 1024 * 1024

TM = 256
TQ = 256

W_NAMES = ("w_in", "w_proj_a", "w_proj_b", "w_out", "w_ffn_gate", "w_ffn_up", "w_ffn_down")
TRANSPOSED = ("w_in", "w_ffn_gate", "w_ffn_up")
IN_SHARD = IN_COLS // N_SHARD
IN_SHARD_PAD = 1504
SHARD_SHAPE = dict(w_in=(IN_SHARD_PAD, D), w_proj_a=(DIL_W, D // N_SHARD), w_proj_b=(FOX_W, D // N_SHARD),
                   w_out=(D // N_SHARD, D), w_ffn_gate=(F_FF // N_SHARD, D), w_ffn_up=(F_FF // N_SHARD, D),
                   w_ffn_down=(F_FF // N_SHARD, D))
SMALL_ROWS = 8
LOSS_ROW = 5


def _nbytes(shape, dtype):
    return math.prod(shape) * jnp.dtype(dtype).itemsize


def _params(semantics, block_bytes, temp_bytes=0):
    need = 2 * block_bytes + temp_bytes + (2 << 20)
    return pltpu.CompilerParams(dimension_semantics=semantics, vmem_limit_bytes=int(min(need, VMEM_PLAN_MAX)))


def _row(w, tm=TM):
    return pl.BlockSpec((tm, w), lambda i: (i, 0))


def _vec(w):
    return pl.BlockSpec((1, w), lambda i: (0, 0))


def _in_hbm(x):
    return pltpu.with_memory_space_constraint(x, pltpu.HBM)


def _mm(pairs, dims, out_dtype, *, tm, tn, name, m_inner=False):
    a0, b0 = pairs[0]
    m_dim = a0.shape[1] if dims == "tn" else a0.shape[0]
    n_dim = b0.shape[0] if dims == "nt" else b0.shape[1]
    contract = {"nn": ((1,), (0,)), "nt": ((1,), (1,)), "tn": ((0,), (0,))}[dims]
    n_pairs = len(pairs)
    assert m_dim % tm == 0 and n_dim % tn == 0, (name, m_dim, n_dim, tm, tn)

    def body(*refs):
        o_ref = refs[-1]
        acc = None
        for p in range(n_pairs):
            a = refs[2 * p][...].astype(bf16)
            b = refs[2 * p + 1][...].astype(bf16)
            t = lax.dot_general(a, b, (contract, ((), ())), preferred_element_type=f32)
            acc = t if acc is None else acc + t
        o_ref[...] = acc.astype(o_ref.dtype)

    if m_inner:
        grid = (n_dim // tn, m_dim // tm)
        mi = lambda j, i: i
        ni = lambda j, i: j
    else:
        grid = (m_dim // tm, n_dim // tn)
        mi = lambda i, j: i
        ni = lambda i, j: j
    in_specs, block_bytes, args = [], 0, []
    for a, b in pairs:
        k_dim = a.shape[0] if dims == "tn" else a.shape[1]
        if dims == "tn":
            in_specs.append(pl.BlockSpec((k_dim, tm), lambda *g: (0, mi(*g))))
        else:
            in_specs.append(pl.BlockSpec((tm, k_dim), lambda *g: (mi(*g), 0)))
        if dims == "nt":
            in_specs.append(pl.BlockSpec((tn, k_dim), lambda *g: (ni(*g), 0)))
        else:
            in_specs.append(pl.BlockSpec((k_dim, tn), lambda *g: (0, ni(*g))))
        block_bytes += _nbytes((tm, k_dim), a.dtype) + _nbytes((tn, k_dim), b.dtype)
        args += [a, b]
    block_bytes += _nbytes((tm, tn), out_dtype)
    temp = _nbytes((tm, tn), f32) * 2 + sum(_nbytes((tm, a.shape[0] if dims == "tn" else a.shape[1]), bf16)
                                            + _nbytes((tn, a.shape[0] if dims == "tn" else a.shape[1]), bf16)
                                            for a, _ in pairs)
    return pl.pallas_call(
        body, grid=grid, in_specs=in_specs,
        out_specs=pl.BlockSpec((tm, tn), lambda *g: (mi(*g), ni(*g))),
        out_shape=SDS((m_dim, n_dim), out_dtype), name=name,
        compiler_params=_params(("parallel", "parallel"), block_bytes, temp),
    )(*args)


def _rms(x, g):
    r = lax.rsqrt(jnp.mean(x * x, axis=-1, keepdims=True) + EPS)
    return x * r * g


def _rms_bwd(x, g, dy):
    r = lax.rsqrt(jnp.mean(x * x, axis=-1, keepdims=True) + EPS)
    xh = x * r
    dxh = dy * g
    dx = r * (dxh - xh * jnp.mean(dxh * xh, axis=-1, keepdims=True))
    return dx, jnp.sum(dy * xh, axis=0, keepdims=True)


def _acc_rows(ref, val):
    @pl.when(pl.program_id(0) == 0)
    def _():
        ref[...] = jnp.zeros_like(ref)
    ref[...] += val


def _norm_fwd(xs, g):
    n = len(xs)

    def body(*refs):
        g = refs[n][...]
        for x_ref, h_ref in zip(refs[:n], refs[n + 1:]):
            h_ref[...] = _rms(x_ref[...], g).astype(bf16)

    return pl.pallas_call(
        body, grid=(S // TM,), in_specs=[_row(D)] * n + [_vec(D)], out_specs=[_row(D)] * n,
        out_shape=[SDS((S, D), bf16)] * n, name="norm_mix_pre",
        compiler_params=_params(("parallel",), 6 * n * TM * D, 8 * n * TM * D))(*xs, g)


def _perm_rows(xs, ds, name):
    n = len(xs)

    def body(*refs):
        outs = iter(refs[n:])
        for x_ref in refs[:n]:
            for d in ds:
                o_ref, rows = next(outs), S // d
                for r in range(d):
                    o_ref[r * rows:(r + 1) * rows, :] = x_ref[pl.ds(r, rows, stride=d), :]

    blk = pl.BlockSpec((S, 128), lambda c: (0, c))
    w = xs[0].shape[1]
    return pl.pallas_call(
        body, grid=(w // 128,), in_specs=[blk] * n, out_specs=[blk] * (n * len(ds)),
        out_shape=[SDS((S, w), f32)] * (n * len(ds)), name=name,
        compiler_params=_params(("parallel",), 4 * S * 128 * n * (1 + len(ds))))(*xs)


def _unperm_sum(nat, perms, ds, name):
    n = len(perms)

    def body(*refs):
        a_ref, o_ref, sc = refs[0], refs[n + 1], refs[n + 2]
        acc = a_ref[...]
        for b_ref, d in zip(refs[1:n + 1], ds):
            rows = S // d
            for r in range(d):
                sc[pl.ds(r, rows, stride=d), :] = b_ref[r * rows:(r + 1) * rows, :]
            acc = acc + sc[...]
        o_ref[...] = acc

    blk = pl.BlockSpec((S, 128), lambda c: (0, c))
    w = nat.shape[1]
    return pl.pallas_call(
        body, grid=(w // 128,), in_specs=[blk] * (n + 1), out_specs=blk, out_shape=SDS((S, w), f32),
        scratch_shapes=[pltpu.VMEM((S, 128), f32)], name=name,
        compiler_params=_params(("parallel",), 4 * S * 128 * (n + 2), 8 * S * 128))(nat, *perms)


def _whole(a):
    return pl.BlockSpec(a.shape, lambda i: (0,) * a.ndim)


def _resid_norm_fwd(x, merged, w_out, g_post, g_pre):
    def body(x_ref, mg_ref, w_ref, gp_ref, gn_ref, mix_ref, x2_ref, h_ref):
        mix = jnp.dot(mg_ref[...], w_ref[...], preferred_element_type=f32)
        x2 = x_ref[...] + _rms(mix, gp_ref[...])
        mix_ref[...] = mix
        x2_ref[...] = x2
        h_ref[...] = _rms(x2, gn_ref[...]).astype(bf16)

    return pl.pallas_call(
        body, grid=(S // TM,), in_specs=[_row(D), _row(D), _whole(w_out), _vec(D), _vec(D)], out_specs=[_row(D)] * 3,
        out_shape=[SDS((S, D), f32), SDS((S, D), f32), SDS((S, D), bf16)], name="proj_out_norm",
        compiler_params=_params(("parallel",), 16 * TM * D + 2 * D * D, 16 * TM * D))(x, merged, w_out, g_post, g_pre)


def _loss_head(x2, a_act, w_down, g_post, target):
    def body(x2_ref, a_ref, w_ref, g_ref, t_ref, loss_ref, dy_ref, dff_ref, dg_ref):
        ff = jnp.dot(a_ref[...], w_ref[...], preferred_element_type=f32)
        g = g_ref[...]
        err = x2_ref[...] + _rms(ff, g) - t_ref[...]
        dy = err * (1.0 / D)
        dff, dg = _rms_bwd(ff, g, dy)
        dy_ref[...] = dy
        dff_ref[...] = dff.astype(bf16)
        _acc_rows(dg_ref, dg)
        _acc_rows(loss_ref, jnp.full((1, 128), jnp.sum(err * err), f32))

    return pl.pallas_call(
        body, grid=(S // TM,), in_specs=[_row(D), _row(F_FF), _whole(w_down), _vec(D), _row(D)],
        out_specs=[_vec(128), _row(D), _row(D), _vec(D)],
        out_shape=[SDS((1, 128), f32), SDS((S, D), f32), SDS((S, D), bf16), SDS((1, D), f32)], name="ffn_down_loss",
        compiler_params=_params(("arbitrary",), 14 * TM * D + 2 * TM * F_FF + 2 * F_FF * D, 28 * TM * D),
    )(x2, a_act, w_down, g_post, target)


def _norm_bwd_mid(dy, d_g, d_u, w_gate_t, w_up_t, x2, mix, g_ffn_pre, g_mix_post):
    def body(dy_ref, dgt_ref, dut_ref, wg_ref, wu_ref, x2_ref, mix_ref, g3_ref, g2_ref, dx2_ref, dmix_ref, dg3_ref, dg2_ref):
        dh = jnp.dot(dgt_ref[...], wg_ref[...], preferred_element_type=f32)
        dh += jnp.dot(dut_ref[...], wu_ref[...], preferred_element_type=f32)
        d3, dg3 = _rms_bwd(x2_ref[...], g3_ref[...], dh)
        dx2 = dy_ref[...] + d3
        dmix, dg2 = _rms_bwd(mix_ref[...], g2_ref[...], dx2)
        dx2_ref[...] = dx2
        dmix_ref[...] = dmix.astype(bf16)
        _acc_rows(dg3_ref, dg3)
        _acc_rows(dg2_ref, dg2)

    return pl.pallas_call(
        body, grid=(S // TM,),
        in_specs=[_row(D), _row(F_FF), _row(F_FF), _whole(w_gate_t), _whole(w_up_t), _row(D), _row(D), _vec(D), _vec(D)],
        out_specs=[_row(D), _row(D), _vec(D), _vec(D)],
        out_shape=[SDS((S, D), f32), SDS((S, D), bf16), SDS((1, D), f32), SDS((1, D), f32)], name="ffn_bwd_in_norm",
        compiler_params=_params(("arbitrary",), 18 * TM * D + 4 * TM * F_FF + 4 * F_FF * D, 28 * TM * D),
    )(dy, d_g, d_u, w_gate_t, w_up_t, x2, mix, g_ffn_pre, g_mix_post)


def _norm_bwd_in(dx2, dh1, x, g):
    def body(dx2_ref, dh_ref, x_ref, g_ref, gx_ref, dg_ref):
        d1, dg = _rms_bwd(x_ref[...], g_ref[...], dh_ref[...])
        gx_ref[...] = dx2_ref[...] + d1
        _acc_rows(dg_ref, dg)

    return pl.pallas_call(
        body, grid=(S // TM,), in_specs=[_row(D)] * 3 + [_vec(D)], out_specs=[_row(D), _vec(D)],
        out_shape=[SDS((S, D), f32), SDS((1, D), f32)], name="norm_bwd_in",
        compiler_params=_params(("arbitrary",), 16 * TM * D, 16 * TM * D))(dx2, dh1, x, g)


def _rope_tables():
    half = ROPE_DIM // 2
    inv_freq = np.power(np.float32(ROPE_THETA), -np.arange(0, ROPE_DIM, 2, dtype=np.float32) / np.float32(ROPE_DIM))
    row = np.arange(S)
    groups = []
    for _, d in DIL_GROUPS:
        pos = ((row % (S // d)) * d + row // (S // d)).astype(np.float32)
        ang = pos[:, None] * inv_freq[None, :].astype(np.float32)
        cos, sin = np.cos(ang).astype(np.float32), np.sin(ang).astype(np.float32)
        c = np.concatenate([cos, cos, np.ones((S, HD - ROPE_DIM), np.float32)], axis=1)
        s_lo = np.concatenate([-sin, np.zeros((S, HD - half), np.float32)], axis=1)
        s_hi = np.concatenate([np.zeros((S, half), np.float32), sin, np.zeros((S, HD - ROPE_DIM), np.float32)], axis=1)
        groups.append(np.stack([np.concatenate([t, t], axis=1) for t in (c, s_lo, s_hi)]))
    return jnp.asarray(np.stack(groups))


def _rotate(x, c, lo, hi, sign):
    tile = lambda t: jnp.tile(t, (1, DIL_W // 128))
    return (x * tile(c) + pltpu.roll(x, DIL_W - ROPE_DIM // 2, 1) * (tile(lo) * sign)
            + pltpu.roll(x, ROPE_DIM // 2, 1) * (tile(hi) * sign))


def _table_specs(g):
    return [pl.BlockSpec((None, None, TM, 128), lambda i, k=k: (g, k, i, 0)) for k in range(3)]


def _rope_fwd(g, p_qkv, tables):
    def body(x_ref, c_ref, lo_ref, hi_ref, o_ref):
        c, lo, hi = c_ref[...], lo_ref[...], hi_ref[...]
        for part in range(2):
            cols = slice(part * DIL_W, (part + 1) * DIL_W)
            o_ref[:, cols] = _rotate(x_ref[:, cols], c, lo, hi, 1.0).astype(bf16)
        o_ref[:, 2 * DIL_W:] = x_ref[:, 2 * DIL_W:].astype(bf16)

    return pl.pallas_call(
        body, grid=(S // TM,), in_specs=[_row(QKV_W)] + _table_specs(g), out_specs=_row(QKV_W),
        out_shape=SDS((S, QKV_W), bf16), name=f"rope_fwd_{g}",
        compiler_params=_params(("parallel",), 6 * TM * QKV_W + 12 * TM * 128, 24 * TM * QKV_W))(p_qkv, tables, tables, tables)


def _rope_bwd(g, dq, dk, dv, tables):
    def body(dq_ref, dk_ref, dv_ref, c_ref, lo_ref, hi_ref, o_ref):
        c, lo, hi = c_ref[...], lo_ref[...], hi_ref[...]
        o_ref[:, :DIL_W] = _rotate(dq_ref[...], c, lo, hi, -1.0).astype(bf16)
        o_ref[:, DIL_W:2 * DIL_W] = _rotate(dk_ref[...], c, lo, hi, -1.0).astype(bf16)
        o_ref[:, 2 * DIL_W:] = dv_ref[...].astype(bf16)

    return pl.pallas_call(
        body, grid=(S // TM,), in_specs=[_row(DIL_W)] * 3 + _table_specs(g), out_specs=_row(QKV_W),
        out_shape=SDS((S, QKV_W), bf16), name=f"rope_bwd_{g}",
        compiler_params=_params(("parallel",), 6 * TM * QKV_W + 12 * TM * 128, 24 * TM * QKV_W))(dq, dk, dv, tables, tables, tables)


def _nt(a, b):
    return lax.dot_general(a, b, (((1,), (1,)), ((), ())), preferred_element_type=f32)


def _tn(a, b):
    return lax.dot_general(a, b, (((0,), (0,)), ((), ())), preferred_element_type=f32)


STEP_BLOCKS = 4
STEP_ROWS = STEP_BLOCKS * BLK


def _dil_prev(g, b):
    _, d = DIL_GROUPS[g]
    nb = S // d // BLK
    if nb == 1 or (b == 0 and nb <= STEP_BLOCKS):
        return None
    return "in" if b > 0 else "halo"


def _bnt(a, b):
    return lax.dot_general(a, b, (((2,), (2,)), ((0,), (0,))), preferred_element_type=f32)


def _bnn(a, b):
    return lax.dot_general(a, b, (((2,), (1,)), ((0,), (0,))), preferred_element_type=f32)


def _btn(a, b):
    return lax.dot_general(a, b, (((1,), (1,)), ((0,), (0,))), preferred_element_type=f32)


def _on_tail(x, tail, fn):
    if tail == x.shape[0]:
        return fn(x)
    return jnp.concatenate([x[:-tail], fn(x[-tail:])], axis=0)


def _heads(ref, part):
    n = ref.shape[0] // BLK
    return jnp.stack([ref[b * BLK:(b + 1) * BLK, part * DIL_W + h * HD:part * DIL_W + (h + 1) * HD]
                      for b in range(n) for h in range(SLOTS)])


def _dil_operands(g, qkv_ref, halo_ref):
    q, kc, vc = (_heads(qkv_ref, part) for part in range(3))
    qi = lax.broadcasted_iota(jnp.int32, (1, BLK, BLK), 1)
    kj = lax.broadcasted_iota(jnp.int32, (1, BLK, BLK), 2)
    with_prev = [b for b in range(STEP_BLOCKS) if _dil_prev(g, b) is not None]
    tail = SLOTS * len(with_prev)
    if not tail:
        return q, kc, vc, None, None, kj <= qi, None, 0
    assert with_prev == list(range(STEP_BLOCKS - len(with_prev), STEP_BLOCKS))
    inside = SLOTS * sum(_dil_prev(g, b) == "in" for b in with_prev)
    kp, vp, prev = kc[:inside], vc[:inside], jnp.broadcast_to(kj >= qi, (inside, BLK, BLK))
    if inside < tail:
        no_halo = jnp.where(pl.program_id(0) == 0, BLK + 1, 0)
        kp = jnp.concatenate([_heads(halo_ref, 1), kp], axis=0)
        vp = jnp.concatenate([_heads(halo_ref, 2), vp], axis=0)
        prev = jnp.concatenate([jnp.broadcast_to(kj >= qi + no_halo, (SLOTS, BLK, BLK)), prev], axis=0)
    return q, kc, vc, kp, vp, kj <= qi, prev, tail


def _dil_in_specs(g, n_aux):
    step = lambda w: pl.BlockSpec((STEP_ROWS, w), lambda i: (i, 0))
    halo = [pl.BlockSpec((BLK, QKV_W), lambda i: (jnp.maximum(i * STEP_BLOCKS - 1, 0), 0))]
    needs_halo = _dil_prev(g, 0) == "halo"
    return [step(QKV_W)] + (halo if needs_halo else []) + [step(DIL_W)] * n_aux, needs_halo


def _dil_fwd(g, qkv):
    in_specs, needs_halo = _dil_in_specs(g, 0)

    def body(*refs):
        qkv_ref, halo_ref = refs[0], refs[1] if needs_halo else None
        o_ref, lse_ref = refs[-2:]
        q, kc, vc, kp, vp, cur, prev, tail = _dil_operands(g, qkv_ref, halo_ref)
        sc = jnp.where(cur, _bnt(q, kc) * SCALE, NEG)
        m = jnp.max(sc, axis=-1, keepdims=True)
        if tail:
            sp = jnp.where(prev, _bnt(q[-tail:], kp) * SCALE, NEG)
            m = _on_tail(m, tail, lambda t: jnp.maximum(t, jnp.max(sp, axis=-1, keepdims=True)))
            pp = jnp.exp(sp - m[-tail:])
        pc = jnp.exp(sc - m)
        den = jnp.sum(pc, axis=-1, keepdims=True)
        if tail:
            den = _on_tail(den, tail, lambda t: t + jnp.sum(pp, axis=-1, keepdims=True))
        inv = 1.0 / den
        o = _bnn((pc * inv).astype(bf16), vc)
        if tail:
            o = _on_tail(o, tail, lambda t: t + _bnn((pp * inv[-tail:]).astype(bf16), vp))
        lse = m + jnp.log(den)
        for b in range(STEP_BLOCKS):
            for h in range(SLOTS):
                rows, hs = slice(b * BLK, (b + 1) * BLK), slice(h * HD, (h + 1) * HD)
                o_ref[rows, hs] = o[SLOTS * b + h]
                lse_ref[rows, hs] = jnp.broadcast_to(lse[SLOTS * b + h], (BLK, HD))

    out = pl.BlockSpec((STEP_ROWS, DIL_W), lambda i: (i, 0))
    return pl.pallas_call(
        body, grid=(S // STEP_ROWS,), in_specs=in_specs, out_specs=[out, out], out_shape=[SDS((S, DIL_W), f32)] * 2,
        name=f"dil_fwd_{g}", compiler_params=_params(("parallel",), 12 * STEP_ROWS * DIL_W, 2 << 20),
    )(*([qkv] * (2 if needs_halo else 1)))


def _dil_combine(outs, lses):
    def body(o0, o1, o2, l0, l1, l2, out_ref, lse_ref, so1, so2, sl1, sl2):
        for (_, d), src, dst in ((DIL_GROUPS[1], o1, so1), (DIL_GROUPS[2], o2, so2),
                                 (DIL_GROUPS[1], l1, sl1), (DIL_GROUPS[2], l2, sl2)):
            rows = S // d
            for r in range(d):
                dst[pl.ds(r, rows, stride=d), :] = src[r * rows:(r + 1) * rows, :]
        a, b, c = l0[...], sl1[...], sl2[...]
        m = jnp.maximum(jnp.maximum(a, b), c)
        ea, eb, ec = jnp.exp(a - m), jnp.exp(b - m), jnp.exp(c - m)
        z = ea + eb + ec
        inv = 1.0 / z
        out_ref[...] = (ea * inv) * o0[...] + (eb * inv) * so1[...] + (ec * inv) * so2[...]
        lse_ref[...] = m + jnp.log(z)

    blk = pl.BlockSpec((S, 128), lambda c: (0, c))
    return pl.pallas_call(
        body, grid=(DIL_W // 128,), in_specs=[blk] * 6, out_specs=[blk] * 2,
        out_shape=[SDS((S, DIL_W), f32)] * 2, scratch_shapes=[pltpu.VMEM((S, 128), f32)] * 4, name="dil_combine",
        compiler_params=_params(("parallel",), 32 * S * 128, 32 * S * 128))(*outs, *lses)


def _dil_bwd(g, qkv, d_out, delta, lse):
    in_specs, needs_halo = _dil_in_specs(g, 3)

    def body(*refs):
        qkv_ref, halo_ref = refs[0], refs[1] if needs_halo else None
        do_ref, dl_ref, lse_ref, dq_ref, dk_ref, dv_ref = refs[-6:]
        q, kc, vc, kp, vp, cur, prev, tail = _dil_operands(g, qkv_ref, halo_ref)
        tiles = [(slice(b * BLK, (b + 1) * BLK), h) for b in range(STEP_BLOCKS) for h in range(SLOTS)]
        do = jnp.stack([do_ref[rows, h * HD:(h + 1) * HD] for rows, h in tiles]).astype(bf16)
        lse = jnp.stack([lse_ref[rows, h * HD:h * HD + 1] for rows, h in tiles])
        delta = jnp.stack([dl_ref[rows, h * HD:h * HD + 1] for rows, h in tiles])

        def probs(q, k, mask, lse, do, v, delta):
            p = jnp.exp(jnp.where(mask, _bnt(q, k) * SCALE, NEG) - lse)
            ds = p * (_bnt(do, v) - delta) * SCALE
            return p.astype(bf16), ds.astype(bf16)

        p, ds = probs(q, kc, cur, lse, do, vc, delta)
        dq, dk, dv = _bnn(ds, kc), _btn(ds, q), _btn(p, do)
        if tail:
            p, ds = probs(q[-tail:], kp, prev, lse[-tail:], do[-tail:], vp, delta[-tail:])
            dq = _on_tail(dq, tail, lambda t: t + _bnn(ds, kp))
            dk_p, dv_p = _btn(ds, q[-tail:]), _btn(p, do[-tail:])
            inside = tail - SLOTS if needs_halo else tail
            pad = jnp.zeros((len(tiles) - inside, BLK, HD), f32)
            dk = dk + jnp.concatenate([dk_p[tail - inside:], pad], axis=0)
            dv = dv + jnp.concatenate([dv_p[tail - inside:], pad], axis=0)
        first = pl.multiple_of(pl.program_id(0) * STEP_ROWS, STEP_ROWS)
        for t, (rows, h) in enumerate(tiles):
            hs = slice(h * HD, (h + 1) * HD)
            own = pl.ds(pl.multiple_of(first + rows.start, BLK), BLK)
            dq_ref[rows, hs] = dq[t]
            dk_ref[own, hs] = dk[t]
            dv_ref[own, hs] = dv[t]
        if needs_halo:
            before = pl.ds(pl.multiple_of(jnp.maximum(first - BLK, 0), BLK), BLK)
            for h in range(SLOTS):
                hs = slice(h * HD, (h + 1) * HD)
                dk_ref[before, hs] += dk_p[h]
                dv_ref[before, hs] += dv_p[h]

    whole = pl.BlockSpec((S, DIL_W), lambda i: (0, 0))
    return pl.pallas_call(
        body, grid=(S // STEP_ROWS,), in_specs=in_specs,
        out_specs=[pl.BlockSpec((STEP_ROWS, DIL_W), lambda i: (i, 0)), whole, whole],
        out_shape=[SDS((S, DIL_W), f32)] * 3, name=f"dil_bwd_{g}",
        compiler_params=_params(("arbitrary",), 20 * STEP_ROWS * DIL_W + 8 * S * DIL_W, 2 << 20),
    )(*([qkv] * (2 if needs_halo else 1)), d_out, delta, lse)


def _scan_rows(x, reverse):
    row = lax.broadcasted_iota(jnp.int32, x.shape, 0)
    k = 1
    while k < S:
        if reverse:
            x = x + jnp.where(row < S - k, pltpu.roll(x, S - k, 0), 0.0)
        else:
            x = x + jnp.where(row >= k, pltpu.roll(x, k, 0), 0.0)
        k *= 2
    return x


N_PAIR = N_FOX // 2
_PAIR_Q = pl.BlockSpec((None, S, 128), lambda p: (p, 0, 0))
_PAIR_K = pl.BlockSpec((None, 8, S), lambda p: (p, 0, 0))


def _forget_fwd(fz, b128):
    def body(z_ref, b_ref, fq_ref, fk_ref):
        z = z_ref[...] + b_ref[...]
        logf = jnp.minimum(z, 0.0) - jnp.log1p(jnp.exp(-jnp.abs(z)))
        f_cum = _scan_rows(logf, reverse=False)
        f_cum_t = f_cum.T
        fq_ref[...] = jnp.zeros_like(fq_ref)
        fk_ref[...] = jnp.zeros_like(fk_ref)
        for p in range(N_PAIR):
            fq_ref[p, :, 0:2] = f_cum[:, 2 * p:2 * p + 2]
            fk_ref[p, 0:2, :] = f_cum_t[2 * p:2 * p + 2, :]

    return pl.pallas_call(
        body, grid=(1,), in_specs=[pl.BlockSpec((S, 128), lambda i: (0, 0)), _vec(128)],
        out_specs=[pl.BlockSpec((N_PAIR, S, 128), lambda i: (0, 0, 0)), pl.BlockSpec((N_PAIR, 8, S), lambda i: (0, 0, 0))],
        out_shape=[SDS((N_PAIR, S, 128), f32), SDS((N_PAIR, 8, S), f32)], name="forget_fwd",
        compiler_params=_params(("arbitrary",), 24 * S * 128, 24 * S * 128))(fz, b128)


def _forget_bwd(fz, b128, d_f_cols, d_f_rows):
    def body(z_ref, b_ref, dfc_ref, dfr_ref, dz_ref, db_ref, df_sc):
        z = z_ref[...] + b_ref[...]
        df_sc[...] = jnp.zeros_like(df_sc)
        for p in range(N_PAIR):
            df_sc[:, 2 * p:2 * p + 2] = dfr_ref[p, :, 0:2] + dfc_ref[p].T[:, 0:2]
        dz = _scan_rows(df_sc[...], reverse=True) * jax.nn.sigmoid(-z)
        dz_ref[...] = dz
        db_ref[...] = jnp.sum(dz, axis=0, keepdims=True)

    full = pl.BlockSpec((S, 128), lambda i: (0, 0))
    return pl.pallas_call(
        body, grid=(1,),
        in_specs=[full, _vec(128), pl.BlockSpec((N_PAIR, 8, S), lambda i: (0, 0, 0)), pl.BlockSpec((N_PAIR, S, 128), lambda i: (0, 0, 0))],
        out_specs=[full, _vec(128)], out_shape=[SDS((S, 128), f32), SDS((1, 128), f32)],
        scratch_shapes=[pltpu.VMEM((S, 128), f32)], name="forget_bwd",
        compiler_params=_params(("arbitrary",), 32 * S * 128, 24 * S * 128))(fz, b128, d_f_cols, d_f_rows)


def _fox_scores(q_ref, k_ref, fq_ref, fk_ref, qi, hh):
    n = (qi + 1) * TQ
    rows, hs = slice(qi * TQ, n), slice(hh * HD, (hh + 1) * HD)
    s = _nt(q_ref[rows, hs], k_ref[0:n, hs]) * SCALE + (fq_ref[rows, hh:hh + 1] - fk_ref[hh:hh + 1, 0:n])
    qpos = qi * TQ + lax.broadcasted_iota(jnp.int32, (TQ, n), 0)
    kpos = lax.broadcasted_iota(jnp.int32, (TQ, n), 1)
    return jnp.where(kpos <= qpos, s, NEG)


def _pair_cols(first):
    return pl.BlockSpec((S, 128), lambda p: (0, first + p))


def _fox_fwd(vr, fq, fk):
    def body(q_ref, k_ref, v_ref, fq_ref, fk_ref, o_ref, lse_ref):
        lse_ref[...] = jnp.zeros_like(lse_ref)
        for hh in range(2):
            hs = slice(hh * HD, (hh + 1) * HD)
            for qi in range(S // TQ):
                n = (qi + 1) * TQ
                rows = slice(qi * TQ, n)
                s = _fox_scores(q_ref, k_ref, fq_ref, fk_ref, qi, hh)
                m = jnp.max(s, axis=-1, keepdims=True)
                p = jnp.exp(s - m)
                den = jnp.sum(p, axis=-1, keepdims=True)
                o_ref[rows, hs] = jnp.dot((p * (1.0 / den)).astype(bf16), v_ref[0:n, hs], preferred_element_type=f32)
                lse_ref[rows, hh:hh + 1] = m + jnp.log(den)

    return pl.pallas_call(
        body, grid=(N_PAIR,), in_specs=[_pair_cols(0), _pair_cols(N_PAIR), _pair_cols(2 * N_PAIR), _PAIR_Q, _PAIR_K],
        out_specs=[_pair_cols(0), _PAIR_Q], out_shape=[SDS((S, FOX_W), f32), SDS((N_PAIR, S, 128), f32)],
        name="fox_fwd", compiler_params=_params(("parallel",), 12 * S * 128, 16 * TQ * S),
    )(vr, vr, vr, fq, fk)


def _fox_bwd(vr, fq, fk, lse, d_out, delta):
    def body(q_ref, k_ref, v_ref, do_ref, fq_ref, fk_ref, lse_ref, dl_ref, dq_ref, dk_ref, dv_ref, dfc_ref, dfr_ref,
             dk_sc, dv_sc):
        dfc_ref[...] = jnp.zeros_like(dfc_ref)
        dfr_ref[...] = jnp.zeros_like(dfr_ref)
        for hh in range(2):
            hs = slice(hh * HD, (hh + 1) * HD)
            dk_sc[...] = jnp.zeros_like(dk_sc)
            dv_sc[...] = jnp.zeros_like(dv_sc)
            for qi in range(S // TQ):
                n = (qi + 1) * TQ
                rows = slice(qi * TQ, n)
                q, do, k, v = q_ref[rows, hs], do_ref[rows, hs], k_ref[0:n, hs], v_ref[0:n, hs]
                p = jnp.exp(_fox_scores(q_ref, k_ref, fq_ref, fk_ref, qi, hh) - lse_ref[rows, hh:hh + 1])
                ds = p * (_nt(do, v) - dl_ref[rows, hh:hh + 1])
                dsb = ds.astype(bf16)
                dq_ref[rows, hs] = jnp.dot(dsb, k, preferred_element_type=f32) * SCALE
                dk_sc[0:n, :] += _tn(dsb, q) * SCALE
                dv_sc[0:n, :] += _tn(p.astype(bf16), do)
                dfc_ref[hh:hh + 1, 0:n] -= jnp.sum(ds, axis=0, keepdims=True)
                dfr_ref[rows, hh:hh + 1] = jnp.sum(ds, axis=-1, keepdims=True)
            dk_ref[:, hs] = dk_sc[...]
            dv_ref[:, hs] = dv_sc[...]

    cols = [_pair_cols(k * N_PAIR) for k in range(3)]
    return pl.pallas_call(
        body, grid=(N_PAIR,), in_specs=cols + [_pair_cols(0), _PAIR_Q, _PAIR_K, _PAIR_Q, _PAIR_Q],
        out_specs=[_pair_cols(0)] * 3 + [_PAIR_K, _PAIR_Q],
        out_shape=[SDS((S, FOX_W), f32)] * 3 + [SDS((N_PAIR, 8, S), f32), SDS((N_PAIR, S, 128), f32)],
        scratch_shapes=[pltpu.VMEM((S, HD), f32)] * 2, name="fox_bwd",
        compiler_params=_params(("parallel",), 32 * S * 128, 24 * TQ * S),
    )(vr, vr, vr, d_out, fq, fk, lse, delta)


def _merge_fwd(out_a, out_b, w_a, w_b, gf):
    cw = D // N_SHARD

    def body(oa_ref, ob_ref, wa_ref, wb_ref, ga_ref, gb_ref, ya_ref, yb_ref, mg_ref):
        oa, ob = oa_ref[...].astype(bf16), ob_ref[...].astype(bf16)
        for j in range(N_SHARD):
            cols = slice(j * cw, (j + 1) * cw)
            ya = jnp.dot(oa, wa_ref[j], preferred_element_type=f32)
            yb = jnp.dot(ob, wb_ref[j], preferred_element_type=f32)
            ya_ref[:, cols] = ya
            yb_ref[:, cols] = yb
            mg_ref[:, cols] = (jax.nn.sigmoid(ga_ref[:, cols]) * ya + jax.nn.sigmoid(gb_ref[:, cols]) * yb).astype(bf16)

    full = lambda a: pl.BlockSpec(a.shape, lambda i: (0, 0, 0))
    return pl.pallas_call(
        body, grid=(S // TM,),
        in_specs=[_row(DIL_W), _row(FOX_W), full(w_a), full(w_b), _row(D), pl.BlockSpec((TM, D), lambda i: (i, 1))],
        out_specs=[_row(D)] * 3, out_shape=[SDS((S, D), f32), SDS((S, D), f32), SDS((S, D), bf16)], name="merge_fwd",
        compiler_params=_params(("parallel",), 22 * TM * D + 2 * (DIL_W + FOX_W) * D, 16 * TM * D),
    )(out_a, out_b, w_a, w_b, gf, gf)


def _merge_bwd(d_mix, w_out, ya, yb, gf):
    def body(dx_ref, w_ref, ya_ref, yb_ref, ga_ref, gb_ref, dya_ref, dyb_ref, dg_ref):
        dm = _nt(dx_ref[...], w_ref[...])
        sa, sb = jax.nn.sigmoid(ga_ref[...]), jax.nn.sigmoid(gb_ref[...])
        dya_ref[...] = (dm * sa).astype(bf16)
        dyb_ref[...] = (dm * sb).astype(bf16)
        dg_ref[:, :D] = (dm * ya_ref[...] * sa * (1.0 - sa)).astype(bf16)
        dg_ref[:, D:] = (dm * yb_ref[...] * sb * (1.0 - sb)).astype(bf16)

    return pl.pallas_call(
        body, grid=(S // TM,),
        in_specs=[_row(D), _whole(w_out)] + [_row(D)] * 3 + [pl.BlockSpec((TM, D), lambda i: (i, 1))],
        out_specs=[_row(D), _row(D), _row(2 * D)],
        out_shape=[SDS((S, D), bf16), SDS((S, D), bf16), SDS((S, 2 * D), bf16)], name="proj_out_bwd_merge",
        compiler_params=_params(("parallel",), 26 * TM * D + 2 * D * D, 28 * TM * D))(d_mix, w_out, ya, yb, gf, gf)


def _branch_bwd(d_ya, d_yb, w_a, w_b, out_a, out_b):
    cw = D // N_SHARD

    def body(dya_ref, dyb_ref, wa_ref, wb_ref, oa_ref, ob_ref, doa_ref, dla_ref, dob_ref, dlb_ref):
        doa = jnp.zeros((TM, DIL_W), f32)
        dob = jnp.zeros((TM, FOX_W), f32)
        for j in range(N_SHARD):
            cols = slice(j * cw, (j + 1) * cw)
            doa += _nt(dya_ref[:, cols], wa_ref[j])
            dob += _nt(dyb_ref[:, cols], wb_ref[j])
        doa_ref[...] = doa
        dob_ref[...] = dob.astype(bf16)
        prod_a = doa * oa_ref[...]
        for h in range(SLOTS):
            hs = slice(h * HD, (h + 1) * HD)
            dla_ref[:, hs] = jnp.broadcast_to(jnp.sum(prod_a[:, hs], axis=-1, keepdims=True), (TM, HD))
        prod_b = dob * ob_ref[...]
        dlb_ref[...] = jnp.zeros_like(dlb_ref)
        for h in range(N_FOX):
            dlb_ref[h // 2, :, h % 2:h % 2 + 1] = jnp.sum(prod_b[:, h * HD:(h + 1) * HD], axis=-1, keepdims=True)

    full = lambda a: pl.BlockSpec(a.shape, lambda i: (0, 0, 0))
    return pl.pallas_call(
        body, grid=(S // TM,),
        in_specs=[_row(D), _row(D), full(w_a), full(w_b), _row(DIL_W), _row(FOX_W)],
        out_specs=[_row(DIL_W), _row(DIL_W), _row(FOX_W), pl.BlockSpec((N_PAIR, TM, 128), lambda i: (0, i, 0))],
        out_shape=[SDS((S, DIL_W), f32), SDS((S, DIL_W), f32), SDS((S, FOX_W), bf16), SDS((N_PAIR, S, 128), f32)],
        name="branch_bwd", compiler_params=_params(("parallel",), 8 * TM * D + 2 * (DIL_W + FOX_W) * D, 8 * TM * D),
    )(d_ya, d_yb, w_a, w_b, out_a, out_b)


def _branch_grads(out_a, out_b, d_ya, d_yb):
    cw = D // N_SHARD

    def body(oa_ref, ob_ref, dya_ref, dyb_ref, ga_ref, gb_ref):
        ga_ref[...] = _tn(oa_ref[...].astype(bf16), dya_ref[...]).astype(bf16)
        gb_ref[...] = _tn(ob_ref[...].astype(bf16), dyb_ref[...]).astype(bf16)

    whole = lambda w: pl.BlockSpec((S, w), lambda j: (0, 0))
    cols = pl.BlockSpec((S, cw), lambda j: (0, j))
    return pl.pallas_call(
        body, grid=(N_SHARD,), in_specs=[whole(DIL_W), whole(FOX_W), cols, cols],
        out_specs=[pl.BlockSpec((None, DIL_W, cw), lambda j: (j, 0, 0)), pl.BlockSpec((None, FOX_W, cw), lambda j: (j, 0, 0))],
        out_shape=[SDS((N_SHARD, DIL_W, cw), bf16), SDS((N_SHARD, FOX_W, cw), bf16)], name="grad_w_proj_ab",
        compiler_params=_params(("parallel",), 4 * S * (DIL_W + FOX_W) + 4 * S * cw + 4 * (DIL_W + FOX_W) * cw,
                                4 * S * (DIL_W + FOX_W)))(out_a, out_b, d_ya, d_yb)


FF_TN = F_FF // 2
FF_TM = 512


def _ffn_fwd(h, w_gate_t, w_up_t):
    def body(h_ref, wg_ref, wu_ref, g_ref, u_ref, a_ref):
        hb = h_ref[...]
        g = _nt(hb, wg_ref[...])
        u = _nt(hb, wu_ref[...])
        g_ref[...] = g
        u_ref[...] = u
        a_ref[...] = (g * jax.nn.sigmoid(g) * u).astype(bf16)

    tile = pl.BlockSpec((FF_TM, FF_TN), lambda j, i: (i, j))
    wspec = pl.BlockSpec((FF_TN, D), lambda j, i: (j, 0))
    return pl.pallas_call(
        body, grid=(F_FF // FF_TN, S // FF_TM),
        in_specs=[pl.BlockSpec((FF_TM, D), lambda j, i: (i, 0)), wspec, wspec], out_specs=[tile] * 3,
        out_shape=[pltpu.HBM((S, F_FF), f32), pltpu.HBM((S, F_FF), f32), pltpu.HBM((S, F_FF), bf16)], name="ffn_fwd",
        compiler_params=_params(("parallel", "parallel"), 2 * FF_TM * D + 4 * D * FF_TN + 10 * FF_TM * FF_TN, 16 * FF_TM * FF_TN),
    )(h, w_gate_t, w_up_t)


def _ffn_bwd_act(d_ff, w_down, g_act, u_act):
    def body(d_ref, wd_ref, g_ref, u_ref, dg_ref, du_ref):
        da = _nt(d_ref[...], wd_ref[...])
        g = g_ref[...]
        sg = jax.nn.sigmoid(g)
        du_ref[...] = (da * g * sg).astype(bf16)
        dg_ref[...] = (da * u_ref[...] * sg * (1.0 + g * (1.0 - sg))).astype(bf16)

    tile = pl.BlockSpec((FF_TM, FF_TN), lambda j, i: (i, j))
    return pl.pallas_call(
        body, grid=(F_FF // FF_TN, S // FF_TM),
        in_specs=[pl.BlockSpec((FF_TM, D), lambda j, i: (i, 0)), pl.BlockSpec((FF_TN, D), lambda j, i: (j, 0)), tile, tile],
        out_specs=[tile, tile], out_shape=[pltpu.HBM((S, F_FF), bf16)] * 2, name="ffn_bwd_act",
        compiler_params=_params(("parallel", "parallel"), 2 * FF_TM * D + 2 * D * FF_TN + 12 * FF_TM * FF_TN, 16 * FF_TM * FF_TN),
    )(d_ff, w_down, g_act, u_act)


def _row_tile(rows):
    return next(t for t in (376, 128, 176, 64, 32, 16, 8) if rows % t == 0)


def _adamw_math(w, g, m, v):
    c1 = 1.0 - ADAM_B1 ** ADAM_STEP
    c2 = 1.0 - ADAM_B2 ** ADAM_STEP
    m_new = ADAM_B1 * m + (1.0 - ADAM_B1) * g
    v_new = ADAM_B2 * v + (1.0 - ADAM_B2) * (g * g)
    return -ADAM_LR * ((m_new / c1) / (jnp.sqrt(v_new / c2) + ADAM_EPS) + ADAM_WD * w), m_new, v_new


def _adamw(w, g, m, v, name):
    rows, cols = w.shape
    tm = _row_tile(rows)

    def body(w_ref, g_ref, m_ref, v_ref, d_ref, nm_ref, nv_ref):
        d_ref[...], nm_ref[...], nv_ref[...] = _adamw_math(w_ref[...], g_ref[...], m_ref[...], v_ref[...])

    spec = pl.BlockSpec((tm, cols), lambda i: (i, 0))
    return pl.pallas_call(
        body, grid=(rows // tm,), in_specs=[spec] * 4, out_specs=[spec] * 3, out_shape=[SDS(w.shape, f32)] * 3,
        name=name, compiler_params=_params(("parallel",), 28 * tm * cols, 16 * tm * cols))(w, g, m, v)


def _adamw_halves(w, g_mine, g_theirs, m, v, name):
    rows, cols = w.shape
    tm = _row_tile(rows // 2)
    per_half = rows // 2 // tm
    core = lax.axis_index("c").astype(jnp.int32).reshape(1)

    def body(c_ref, w_ref, gm_ref, gt_ref, m_ref, v_ref, g_ref, d_ref, nm_ref, nv_ref):
        mine = pl.program_id(0) // per_half == c_ref[0]
        g = jnp.where(mine, gm_ref[...], gt_ref[...])
        g_ref[...] = g
        d_ref[...], nm_ref[...], nv_ref[...] = _adamw_math(w_ref[...], g, m_ref[...], v_ref[...])

    spec = pl.BlockSpec((tm, cols), lambda i, c_ref: (i, 0))
    in_half = lambda i, first: jnp.clip(i - first * per_half, 0, per_half - 1)
    grid_spec = pltpu.PrefetchScalarGridSpec(
        num_scalar_prefetch=1, grid=(rows // tm,),
        in_specs=[spec, pl.BlockSpec((tm, cols), lambda i, c_ref: (in_half(i, c_ref[0]), 0)),
                  pl.BlockSpec((tm, cols), lambda i, c_ref: (in_half(i, 1 - c_ref[0]), 0)), spec, spec],
        out_specs=[spec] * 4)
    return pl.pallas_call(
        body, grid_spec=grid_spec, out_shape=[SDS(w.shape, f32)] * 4, name=name,
        compiler_params=_params(("parallel",), 36 * tm * cols, 16 * tm * cols))(core, w, g_mine, g_theirs, m, v)


_ANY = pl.BlockSpec(memory_space=pl.ANY)


def _place():
    x, y, c = lax.axis_index("x"), lax.axis_index("y"), lax.axis_index("c")
    chips = [(1 - x, y), (x, 1 - y), (1 - x, 1 - y)]
    return x, y, c, chips


def _halved(t):
    return t.reshape(t.shape[:-2] + (2, t.shape[-2] // 2, t.shape[-1]))


def _gather_body(src, out, send_ici, recv_ici, send_d2d, recv_d2d):
    x, y, c, chips = _place()
    sibling = (x, y, 1 - c)
    me_j = 2 * x + y
    sends = []
    for a in range(len(src)):
        for p in range(3):
            cp = pltpu.make_async_remote_copy(
                src_ref=src[a].at[c], dst_ref=out[a].at[me_j, c], send_sem=send_ici.at[a, p],
                recv_sem=recv_ici.at[a, p], device_id=(*chips[p], c), device_id_type=MESH)
            cp.start()
            sends.append(cp)
    for a in range(len(src)):
        for p, (px, py) in enumerate(chips):
            blk = out[a].at[2 * px + py, c]
            pltpu.make_async_remote_copy(
                src_ref=blk, dst_ref=blk, send_sem=send_ici.at[a, p], recv_sem=recv_ici.at[a, p],
                device_id=sibling, device_id_type=MESH).wait_recv()
            fw = pltpu.make_async_remote_copy(
                src_ref=blk, dst_ref=blk, send_sem=send_d2d.at[a, p], recv_sem=recv_d2d.at[a, p],
                device_id=sibling, device_id_type=MESH)
            fw.start()
            sends.append(fw)
    for a in range(len(src)):
        for p, (px, py) in enumerate(chips):
            blk = out[a].at[2 * px + py, 1 - c]
            pltpu.make_async_remote_copy(
                src_ref=blk, dst_ref=blk, send_sem=send_d2d.at[a, p], recv_sem=recv_d2d.at[a, p],
                device_id=sibling, device_id_type=MESH).wait_recv()
    for cp in sends:
        cp.wait_send()


def _handshake(peers):
    barrier = pltpu.get_barrier_semaphore()
    for peer in peers:
        pl.semaphore_signal(barrier, inc=1, device_id=peer, device_id_type=MESH)
    pl.semaphore_wait(barrier, len(peers))


_SEQUENCER = dict(axis_name="sequencer", num_cores=1)
GATHER_LATE_ID, SCATTER_EARLY_ID, SWAP_EARLY_ID, GATHER_FIRST_ID, SCATTER_LATE_ID = 1, 2, 3, 4, 5


def _all_gather_async(shards, after, name, collective_id):
    n, k = len(shards), len(after)

    def body(*refs):
        x, y, c, chips = _place()
        _handshake([(*chip, c) for chip in chips] + [(x, y, 1 - c)])
        _gather_body(refs[:n], refs[n + k:2 * n + k], *refs[2 * n + k:])

    return pl.kernel(
        body, out_type=[SDS((N_SHARD,) + t.shape, t.dtype) for t in shards],
        mesh=plsc.ScalarSubcoreMesh(**_SEQUENCER), scratch_types=[pltpu.SemaphoreType.DMA((n, 3))] * 4,
        compiler_params=pltpu.CompilerParams(collective_id=collective_id), name=name)(*shards, *after)


def _pair_swap(grads):
    n = len(grads)

    def body(*refs):
        src, out, send_sems, recv_sems = refs[:n], refs[n:2 * n], refs[2 * n], refs[2 * n + 1]
        x, y, c, _ = _place()
        copies = [pltpu.make_async_remote_copy(
            src_ref=src[a].at[:, 1 - c], dst_ref=out[a], send_sem=send_sems.at[a], recv_sem=recv_sems.at[a],
            device_id=(x, y, 1 - c), device_id_type=MESH) for a in range(n)]
        for cp in copies:
            cp.start()
        for cp in copies:
            cp.wait()

    return pl.pallas_call(
        body, in_specs=[_ANY] * n, out_specs=[_ANY] * n,
        out_shape=[SDS((N_SHARD,) + t.shape[2:], t.dtype) for t in grads],
        scratch_shapes=[pltpu.SemaphoreType.DMA((n,)), pltpu.SemaphoreType.DMA((n,))], name="pair_swap",
        compiler_params=pltpu.CompilerParams(has_side_effects=True))(*grads)


def _pair_swap_early(grads):
    n = len(grads)

    def body(*refs):
        src, out, send_sems, recv_sems = refs[:n], refs[n:2 * n], refs[2 * n], refs[2 * n + 1]
        x, y, c, _ = _place()
        _handshake([(x, y, 1 - c)])
        copies = [pltpu.make_async_remote_copy(
            src_ref=src[a].at[:, 1 - c], dst_ref=out[a], send_sem=send_sems.at[a], recv_sem=recv_sems.at[a],
            device_id=(x, y, 1 - c), device_id_type=MESH) for a in range(n)]
        for cp in copies:
            cp.start()
        for cp in copies:
            cp.wait()

    return pl.kernel(
        body, out_type=[SDS((N_SHARD,) + t.shape[2:], t.dtype) for t in grads],
        mesh=plsc.ScalarSubcoreMesh(**_SEQUENCER), scratch_types=[pltpu.SemaphoreType.DMA((n,))] * 2,
        compiler_params=pltpu.CompilerParams(collective_id=SWAP_EARLY_ID), name="pair_swap_early")(*grads)


def _scatter_early(parts):
    n = len(parts)

    def body(*refs):
        part, recv, send_sems, recv_sems = refs[:n], refs[n:2 * n], refs[2 * n], refs[2 * n + 1]
        x, y, c, chips = _place()
        _handshake([(*chip, c) for chip in chips])
        me_j = 2 * x + y
        sends = []
        for a in range(n):
            for p, (px, py) in enumerate(chips):
                cp = pltpu.make_async_remote_copy(
                    src_ref=part[a].at[2 * px + py], dst_ref=recv[a].at[me_j], send_sem=send_sems.at[a, p],
                    recv_sem=recv_sems.at[a, p], device_id=(px, py, c), device_id_type=MESH)
                cp.start()
                sends.append(cp)
        for a in range(n):
            for p, (px, py) in enumerate(chips):
                slot = recv[a].at[2 * px + py]
                pltpu.make_async_remote_copy(
                    src_ref=slot, dst_ref=slot, send_sem=send_sems.at[a, p], recv_sem=recv_sems.at[a, p],
                    device_id=(px, py, c), device_id_type=MESH).wait_recv()
        for cp in sends:
            cp.wait_send()

    return pl.kernel(
        body, out_type=[SDS(t.shape, t.dtype) for t in parts],
        mesh=plsc.ScalarSubcoreMesh(**_SEQUENCER), scratch_types=[pltpu.SemaphoreType.DMA((n, 3))] * 2,
        compiler_params=pltpu.CompilerParams(collective_id=SCATTER_EARLY_ID), name="scatter_early")(*parts)


def _pair_sum(grads, other, name):
    _, _, rows, cols = grads.shape
    tr = _row_tile(rows)
    core = lax.axis_index("c").astype(jnp.int32).reshape(1)

    def body(c_ref, g_ref, o_ref, out_ref):
        out_ref[...] = (g_ref[...].astype(f32) + o_ref[...].astype(f32)).astype(bf16)

    grid_spec = pltpu.PrefetchScalarGridSpec(
        num_scalar_prefetch=1, grid=(N_SHARD, rows // tr),
        in_specs=[pl.BlockSpec((None, None, tr, cols), lambda j, i, c_ref: (j, c_ref[0], i, 0)),
                  pl.BlockSpec((None, tr, cols), lambda j, i, c_ref: (j, i, 0))],
        out_specs=pl.BlockSpec((None, tr, cols), lambda j, i, c_ref: (j, i, 0)))
    return pl.pallas_call(
        body, grid_spec=grid_spec, out_shape=SDS((N_SHARD, rows, cols), bf16), name=name,
        compiler_params=_params(("parallel", "parallel"), 10 * tr * cols, 12 * tr * cols))(core, grads, other)


def _scatter_partials(parts, small):
    n = len(parts)

    def body(*refs):
        part, small_ref, recv, small_all_ref = refs[:n], refs[n], refs[n + 1:2 * n + 1], refs[2 * n + 1]
        send_sems, recv_sems, ssend, srecv, local_sem = refs[2 * n + 2:]
        x, y, c, chips = _place()
        flip = lambda a, bit: 1 - a if bit else a
        peers = [(flip(x, k & 4), flip(y, k & 2), flip(c, k & 1)) for k in range(1, 8)]
        _handshake(peers)
        me_j = 2 * x + y
        me_dev = 4 * x + 2 * y + c
        own = pltpu.make_async_copy(small_ref, small_all_ref.at[me_dev], local_sem)
        own.start()
        sends = []
        for a in range(n):
            for p, (px, py) in enumerate(chips):
                cp = pltpu.make_async_remote_copy(
                    src_ref=part[a].at[2 * px + py], dst_ref=recv[a].at[me_j], send_sem=send_sems.at[a, p],
                    recv_sem=recv_sems.at[a, p], device_id=(px, py, c), device_id_type=MESH)
                cp.start()
                sends.append(cp)
        for k, to in enumerate(peers):
            cp = pltpu.make_async_remote_copy(
                src_ref=small_ref, dst_ref=small_all_ref.at[me_dev],
                send_sem=ssend.at[k], recv_sem=srecv.at[k], device_id=to, device_id_type=MESH)
            cp.start()
            sends.append(cp)
        for a in range(n):
            for p, (px, py) in enumerate(chips):
                slot = recv[a].at[2 * px + py]
                pltpu.make_async_remote_copy(
                    src_ref=slot, dst_ref=slot, send_sem=send_sems.at[a, p], recv_sem=recv_sems.at[a, p],
                    device_id=(px, py, c), device_id_type=MESH).wait_recv()
        for k, (px, py, pc) in enumerate(peers):
            slot = small_all_ref.at[4 * px + 2 * py + pc]
            pltpu.make_async_remote_copy(
                src_ref=slot, dst_ref=slot, send_sem=ssend.at[k], recv_sem=srecv.at[k],
                device_id=(px, py, pc), device_id_type=MESH).wait_recv()
        for cp in sends:
            cp.wait_send()
        own.wait()

    return pl.kernel(
        body, out_type=[SDS(t.shape, t.dtype) for t in parts] + [SDS((8, SMALL_ROWS, D), f32)],
        mesh=plsc.ScalarSubcoreMesh(**_SEQUENCER),
        scratch_types=[pltpu.SemaphoreType.DMA((n, 3)), pltpu.SemaphoreType.DMA((n, 3)),
                       pltpu.SemaphoreType.DMA((7,)), pltpu.SemaphoreType.DMA((7,)), pltpu.SemaphoreType.DMA],
        compiler_params=pltpu.CompilerParams(collective_id=SCATTER_LATE_ID), name="scatter_partials")(*parts, small)


def _sum_partials(part, recv, name):
    _, rows, cols = recv.shape
    tr = _row_tile(rows)
    me = (2 * lax.axis_index("x") + lax.axis_index("y")).astype(jnp.int32).reshape(1)

    def body(me_ref, mine, r0, r1, r2, r3, out_ref):
        acc = None
        for j, r in enumerate((r0, r1, r2, r3)):
            term = jnp.where(me_ref[0] == j, mine[...], r[...]).astype(f32)
            acc = term if acc is None else acc + term
        out_ref[...] = acc

    slot = lambda j: pl.BlockSpec((None, tr, cols), lambda i, me_ref: (jnp.where(me_ref[0] == j, j ^ 1, j), i, 0))
    grid_spec = pltpu.PrefetchScalarGridSpec(
        num_scalar_prefetch=1, grid=(rows // tr,),
        in_specs=[pl.BlockSpec((None, tr, cols), lambda i, me_ref: (me_ref[0], i, 0)), slot(0), slot(1), slot(2), slot(3)],
        out_specs=pl.BlockSpec((tr, cols), lambda i, me_ref: (i, 0)))
    return pl.pallas_call(
        body, grid_spec=grid_spec, out_shape=SDS((rows, cols), f32), name=name,
        compiler_params=_params(("parallel",), 14 * tr * cols, 12 * tr * cols))(me, part, recv, recv, recv, recv)


def _sum_small(small_all):
    def body(small_ref, out_ref):
        tot = small_ref[0]
        for k in range(1, 8):
            tot = tot + small_ref[k]
        out_ref[...] = tot

    return pl.pallas_call(
        body, grid=(1,), in_specs=[pl.BlockSpec((8, SMALL_ROWS, D), lambda i: (0, 0, 0))],
        out_specs=pl.BlockSpec((SMALL_ROWS, D), lambda i: (0, 0)), out_shape=SDS((SMALL_ROWS, D), f32),
        name="sum_small", compiler_params=_params(("arbitrary",), 36 * SMALL_ROWS * D))(small_all)


def _swap_halves(halves, name):
    n = len(halves)

    def body(*refs):
        src, out, send_sems, recv_sems = refs[:n], refs[n:2 * n], refs[2 * n], refs[2 * n + 1]
        x, y, c, _ = _place()
        copies = [pltpu.make_async_remote_copy(
            src_ref=src[a], dst_ref=out[a], send_sem=send_sems.at[a], recv_sem=recv_sems.at[a],
            device_id=(x, y, 1 - c), device_id_type=MESH) for a in range(n)]
        for cp in copies:
            cp.start()
        for cp in copies:
            cp.wait()

    return pl.pallas_call(
        body, in_specs=[_ANY] * n, out_specs=[_ANY] * n, out_shape=[SDS(t.shape, f32) for t in halves],
        scratch_shapes=[pltpu.SemaphoreType.DMA((n,))] * 2, name=name,
        compiler_params=pltpu.CompilerParams(has_side_effects=True))(*halves)


def _kernel_layout(name, t):
    t = t[0]
    if name in TRANSPOSED:
        t = jnp.swapaxes(t, 0, 1)
    return _pad_rows(t, SHARD_SHAPE[name][0])


def _harness_layout(name, t):
    if name == "w_in":
        t = t[:IN_SHARD]
    if name in TRANSPOSED:
        t = jnp.swapaxes(t, 0, 1)
    return t[None]


def _pad_rows(t, rows):
    return t if t.shape[0] == rows else jnp.pad(t, ((0, rows - t.shape[0]), (0, 0)))


_QA, _KA, _VA, _QB, _F, _GAB = 0, 768, 1536, 2304, 3840, 3848


def _spans(a, b):
    return [(j, max(a, j * IN_SHARD) - j * IN_SHARD, max(a, j * IN_SHARD) - a,
             min(b, (j + 1) * IN_SHARD) - max(a, j * IN_SHARD))
            for j in range(N_SHARD) if max(a, j * IN_SHARD) < min(b, (j + 1) * IN_SHARD)]


_LANES = pl.BlockSpec((N_SHARD, IN_SHARD_PAD, 128), lambda c: (0, 0, c))


def _split_w_in(shards):
    group = [[(o + g * DIL_W, o + (g + 1) * DIL_W) for o in (_QA, _KA, _VA)] for g in range(3)]
    fox = [[(_QB + k * FOX_W, _QB + (k + 1) * FOX_W)] for k in range(3)]
    wanted = group + fox + [[(_QB, _F)], [(_F, _GAB)], [(_GAB, IN_COLS)]]
    rows = [sum(b - a for a, b in w) for w in wanted]
    rows[7] = 128

    def body(s_ref, *o_refs):
        for o_ref, want in zip(o_refs, wanted):
            at = 0
            for a, b in want:
                for j, src, off, n in _spans(a, b):
                    o_ref[at + off:at + off + n, :] = s_ref[j, src:src + n, :]
                at += b - a
        o_refs[7][N_FOX:, :] = jnp.zeros((128 - N_FOX, 128), bf16)

    return pl.pallas_call(
        body, grid=(D // 128,), in_specs=[_LANES], out_specs=[pl.BlockSpec((r, 128), lambda c: (0, c)) for r in rows],
        out_shape=[SDS((r, D), bf16) for r in rows], name="split_w_in",
        compiler_params=_params(("parallel",), 2 * 128 * (N_SHARD * IN_SHARD_PAD + sum(rows))))(shards)


def _join_w_in(g_a, g_fox, g_f, g_gab):
    parts = [(g_a[k], o, o + DIL_W) for o in (0, DIL_W, 2 * DIL_W) for k in range(3)]
    parts += [(t, 0, FOX_W) for t in g_fox] + [(g_f, 0, N_FOX), (g_gab, 0, 2 * D)]
    arrays = list(g_a) + list(g_fox) + [g_f, g_gab]
    index = {id(t): i for i, t in enumerate(arrays)}

    def body(*refs):
        o_ref = refs[-1]
        o_ref[:, IN_SHARD:, :] = jnp.zeros((N_SHARD, IN_SHARD_PAD - IN_SHARD, 128), bf16)
        at = 0
        for t, lo, hi in parts:
            src_ref = refs[index[id(t)]]
            for j, dst, off, n in _spans(at, at + hi - lo):
                o_ref[j, dst:dst + n, :] = src_ref[lo + off:lo + off + n, :].astype(bf16)
            at += hi - lo

    return pl.pallas_call(
        body, grid=(D // 128,), in_specs=[pl.BlockSpec((t.shape[0], 128), lambda c: (0, c)) for t in arrays],
        out_specs=_LANES, out_shape=SDS((N_SHARD, IN_SHARD_PAD, D), bf16), name="join_w_in",
        compiler_params=_params(("parallel",), 2 * 128 * (N_SHARD * IN_SHARD_PAD + sum(t.shape[0] for t in arrays))),
    )(*arrays)


def _full_weights(gathered):
    full = {n: t.reshape((N_SHARD,) + SHARD_SHAPE[n]) for n, t in gathered.items()}
    out = {}
    if "w_in" in full:
        pieces = _split_w_in(full["w_in"])
        out.update(w_a_t=pieces[0:3], w_fox_t=pieces[3:6], w_vr_t=pieces[6], w_f_t=pieces[7], w_gab_t=pieces[8])
    if "w_out" in full:
        out.update(
            w_a4=full["w_proj_a"],
            w_b4=full["w_proj_b"],
            w_out=full["w_out"].reshape(D, D),
            w_gate_t=_in_hbm(full["w_ffn_gate"].reshape(F_FF, D)),
            w_up_t=_in_hbm(full["w_ffn_up"].reshape(F_FF, D)),
            w_down=_in_hbm(full["w_ffn_down"].reshape(F_FF, D)))
    return out


def _sharded_grads(g):
    full = dict(w_in=_join_w_in(g["w_a_t"], g["w_fox_t"], g["w_f_t"], g["w_gab_t"]), w_proj_a=g["w_a4"],
                w_proj_b=g["w_b4"], w_out=g["w_out"], w_ffn_gate=g["w_gate_t"], w_ffn_up=g["w_up_t"],
                w_ffn_down=g["w_down"])
    return {n: _halved(full[n].reshape((N_SHARD,) + SHARD_SHAPE[n])) for n in W_NAMES}


def _local_step(x, target, wt, b_forget, g_mix_pre, g_mix_post, g_ffn_pre, g_ffn_post, late=None):
    tables = _rope_tables()
    b128 = jnp.pad(b_forget, ((0, 0), (0, 128 - N_FOX)))
    dils = tuple(d for _, d in DIL_GROUPS[1:])

    hs = _norm_fwd([x] + list(_perm_rows([x], dils, "perm_x")), g_mix_pre)
    h1 = hs[0]
    if callable(wt):
        wt = wt(h1)
    qkv = [_rope_fwd(g, _mm([(hs[g], wt["w_a_t"][g])], "nt", f32, tm=1024, tn=QKV_W, name=f"proj_a_{g}"), tables)
           for g in range(3)]
    vr = _mm([(h1, wt["w_vr_t"])], "nt", bf16, tm=1024, tn=VR_W // 2, name="proj_vr")
    gab = _mm([(h1, wt["w_gab_t"])], "nt", f32, tm=512, tn=2 * D, name="proj_gab")
    fz = _mm([(h1, wt["w_f_t"])], "nt", f32, tm=1024, tn=128, name="proj_f")
    dil = [_dil_fwd(g, qkv[g]) for g in range(3)]
    out_a, lse_a = _dil_combine([o for o, _ in dil], [l for _, l in dil])
    f_q, f_k = _forget_fwd(fz, b128)
    out_b, lse_b = _fox_fwd(vr, f_q, f_k)
    if late is not None:
        wt = {**wt, **late(out_b)}
    ya, yb, merged = _merge_fwd(out_a, out_b, wt["w_a4"], wt["w_b4"], gab)
    mix, x2, h3 = _resid_norm_fwd(x, merged, wt["w_out"], g_mix_post, g_ffn_pre)
    g_act, u_act, a_act = map(_in_hbm, _ffn_fwd(h3, wt["w_gate_t"], wt["w_up_t"]))
    sq_err, dy, d_ff, dg_ffn_post = _loss_head(x2, a_act, wt["w_down"], g_ffn_post, target)

    grads = {}
    d_g, d_u = map(_in_hbm, _ffn_bwd_act(d_ff, wt["w_down"], g_act, u_act))
    grads["w_down"] = _mm([(a_act, d_ff)], "tn", bf16, tm=FF_TN, tn=512, name="grad_w_down")
    grads["w_gate_t"] = _mm([(d_g, h3)], "tn", bf16, tm=FF_TN, tn=512, name="grad_w_gate")
    grads["w_up_t"] = _mm([(d_u, h3)], "tn", bf16, tm=FF_TN, tn=512, name="grad_w_up")
    dx2, d_mix, dg_ffn_pre, dg_mix_post = _norm_bwd_mid(dy, d_g, d_u, wt["w_gate_t"], wt["w_up_t"], x2, mix,
                                                        g_ffn_pre, g_mix_post)

    grads["w_out"] = _mm([(merged, d_mix)], "tn", bf16, tm=D, tn=D, name="grad_w_out")
    d_ya, d_yb, d_gab = _merge_bwd(d_mix, wt["w_out"], ya, yb, gab)
    grads["w_a4"], grads["w_b4"] = _branch_grads(out_a, out_b, d_ya, d_yb)
    d_out_a, delta_a, d_out_b, delta_b = _branch_bwd(d_ya, d_yb, wt["w_a4"], wt["w_b4"], out_a, out_b)

    perm = _perm_rows([d_out_a, delta_a, lse_a], dils, "perm_dil_bwd")
    aux = [(d_out_a, delta_a, lse_a)] + [tuple(perm[k * len(dils) + i] for k in range(3)) for i in range(len(dils))]
    d_qkv = []
    for g in range(3):
        dq, dk, dv = _dil_bwd(g, qkv[g], *aux[g])
        d_qkv.append(_rope_bwd(g, dq, dk, dv, tables))
    *d_fox, d_f_cols, d_f_rows = _fox_bwd(vr, f_q, f_k, lse_b, d_out_b, delta_b)
    d_z, d_b128 = _forget_bwd(fz, b128, d_f_cols, d_f_rows)

    grads["w_a_t"] = [_mm([(d_qkv[g], hs[g])], "tn", bf16, tm=QKV_W, tn=D, name=f"grad_w_a_{g}") for g in range(3)]
    grads["w_fox_t"] = [_mm([(d_fox[k], h1)], "tn", bf16, tm=FOX_W, tn=D, name=f"grad_w_fox_{k}") for k in range(3)]
    grads["w_gab_t"] = _mm([(d_gab, h1)], "tn", bf16, tm=D, tn=D, name="grad_w_gab")
    grads["w_f_t"] = _mm([(d_z, h1)], "tn", bf16, tm=128, tn=D, name="grad_w_f")
    d_h1_nat = _mm([(d_qkv[0], wt["w_a_t"][0])] + list(zip(d_fox, wt["w_fox_t"]))
                   + [(d_gab, wt["w_gab_t"]), (d_z, wt["w_f_t"])], "nn", f32, tm=512, tn=512, name="proj_in_bwd")
    d_h1_dil = [_mm([(d_qkv[g], wt["w_a_t"][g])], "nn", f32, tm=1024, tn=D, name=f"proj_a_bwd_{g}") for g in (1, 2)]
    d_h1 = _unperm_sum(d_h1_nat, d_h1_dil, dils, "unperm_d_h1")
    grad_x, dg_mix_pre = _norm_bwd_in(dx2, d_h1, x, g_mix_pre)

    small = dict(b_forget=d_b128[:, :N_FOX], norm_mix_pre=dg_mix_pre, norm_mix_post=dg_mix_post,
                 norm_ffn_pre=dg_ffn_pre, norm_ffn_post=dg_ffn_post)
    grads["mid_backward"] = d_qkv[0]
    return sq_err, grad_x, grads, small


NORMS = ("norm_mix_pre", "norm_mix_post", "norm_ffn_pre", "norm_ffn_post")
ORDER = ("w_in", "w_proj_a", "w_proj_b", "w_out", "b_forget", "w_ffn_gate", "w_ffn_up", "w_ffn_down") + NORMS


def kernel(x, w_in, w_proj_a, w_proj_b, w_out, b_forget, w_ffn_gate, w_ffn_up, w_ffn_down, norm_mix_pre, norm_mix_post, norm_ffn_pre, norm_ffn_post, loss_target, m_w_in, m_w_proj_a, m_w_proj_b, m_w_out, m_b_forget, m_w_ffn_gate, m_w_ffn_up, m_w_ffn_down, m_norm_mix_pre, m_norm_mix_post, m_norm_ffn_pre, m_norm_ffn_post, v_w_in, v_w_proj_a, v_w_proj_b, v_w_out, v_b_forget, v_w_ffn_gate, v_w_ffn_up, v_w_ffn_down, v_norm_mix_pre, v_norm_mix_post, v_norm_ffn_pre, v_norm_ffn_post):
    given = dict(w_in=w_in, w_proj_a=w_proj_a, w_proj_b=w_proj_b, w_out=w_out, w_ffn_gate=w_ffn_gate,
                 w_ffn_up=w_ffn_up, w_ffn_down=w_ffn_down)
    given_m = dict(w_in=m_w_in, w_proj_a=m_w_proj_a, w_proj_b=m_w_proj_b, w_out=m_w_out, w_ffn_gate=m_w_ffn_gate,
                   w_ffn_up=m_w_ffn_up, w_ffn_down=m_w_ffn_down)
    given_v = dict(w_in=v_w_in, w_proj_a=v_w_proj_a, w_proj_b=v_w_proj_b, w_out=v_w_out, w_ffn_gate=v_w_ffn_gate,
                   w_ffn_up=v_w_ffn_up, w_ffn_down=v_w_ffn_down)
    w, m, v = ({n: _kernel_layout(n, t[n]) for n in W_NAMES} for t in (given, given_m, given_v))
    small_w = dict(b_forget=b_forget, norm_mix_pre=norm_mix_pre, norm_mix_post=norm_mix_post,
                   norm_ffn_pre=norm_ffn_pre, norm_ffn_post=norm_ffn_post)
    small_m = dict(b_forget=m_b_forget, norm_mix_pre=m_norm_mix_pre, norm_mix_post=m_norm_mix_post,
                   norm_ffn_pre=m_norm_ffn_pre, norm_ffn_post=m_norm_ffn_post)
    small_v = dict(b_forget=v_b_forget, norm_mix_pre=v_norm_mix_pre, norm_mix_post=v_norm_mix_post,
                   norm_ffn_pre=v_norm_ffn_pre, norm_ffn_post=v_norm_ffn_post)

    own = [_halved(w[n].astype(bf16)) for n in W_NAMES]
    chip = 2 * lax.axis_index("x") + lax.axis_index("y")
    exchanged = {"first": _all_gather_async(own[:1], [], "all_gather_first", GATHER_FIRST_ID)}
    fill = lambda ts, mine: [lax.dynamic_update_index_in_dim(t, o, chip, 0) for t, o in zip(ts, mine)]

    def first_weights(ready):
        arrived, _, (w["w_in"], m["w_in"], v["w_in"]) = lax.optimization_barrier(
            (list(exchanged["first"]), ready, (w["w_in"], m["w_in"], v["w_in"])))
        exchanged["late"] = _all_gather_async(own[1:], [arrived[0][0, 0, :16, :128]], "all_gather_late", GATHER_LATE_ID)
        return _full_weights(dict(zip(W_NAMES[:1], fill(arrived, own[:1]))))

    def late_weights(ready):
        arrived, _ = lax.optimization_barrier((list(exchanged["late"]), ready))
        return _full_weights(dict(zip(W_NAMES[1:], fill(arrived, own[1:]))))

    sq_err, grad_x, grads, small = _local_step(x[0], loss_target[0], first_weights, b_forget, norm_mix_pre,
                                               norm_mix_post, norm_ffn_pre, norm_ffn_post, late=late_weights)

    g4 = _sharded_grads(grads)
    stack = lambda t, extra: jnp.concatenate(
        [jnp.pad(t["b_forget"], ((0, 0), (0, D - N_FOX)))] + [t[n] for n in NORMS]
        + [jnp.pad(extra, ((0, SMALL_ROWS - LOSS_ROW - 1), (0, D - extra.shape[1])), constant_values=1.0)], axis=0)
    early, _ = lax.optimization_barrier((list(_pair_swap_early([g4[n] for n in W_NAMES[1:]])), grads["mid_backward"]))
    other = list(_pair_swap([g4["w_in"]])) + early
    parts = [_pair_sum(g4[n], o, "pair_sum_" + n) for n, o in zip(W_NAMES, other)]
    recv_early = _scatter_early(parts[1:])
    recv_in, small_all = _scatter_partials(parts[:1], stack(small, sq_err))

    g_shard, delta, new_m, new_v = {}, {}, {}, {}

    def finish(names, parts, recv):
        halves = [_sum_partials(p, r, "sum_partials_" + n) for n, p, r in zip(names, parts, recv)]
        theirs = _swap_halves(halves, "swap_halves_" + names[0])
        for n, mine, other_half in zip(names, halves, theirs):
            g_shard[n], delta[n], new_m[n], new_v[n] = _adamw_halves(w[n], mine, other_half, m[n], v[n], "adamw_" + n)

    recv_early, _ = lax.optimization_barrier((list(recv_early), parts[0]))
    finish(W_NAMES[1:], parts[1:], recv_early)
    (recv_in, small_all), _ = lax.optimization_barrier(((recv_in, small_all), [delta[n] for n in W_NAMES[1:]]))
    finish(W_NAMES[:1], parts[:1], [recv_in])
    small_sum = _sum_small(small_all)
    loss = small_sum[LOSS_ROW, 0] * (0.5 / D)
    ones = jnp.ones((1, 128), f32)
    sd, sm, sv = _adamw(stack(small_w, ones), small_sum, stack(small_m, ones), stack(small_v, ones), "adamw_small")

    outs = [loss, grad_x[None]]
    for big, st in ((g_shard, small_sum), (delta, sd), (new_m, sm), (new_v, sv)):
        t = {n: _harness_layout(n, big[n]) for n in W_NAMES}
        t["b_forget"] = st[0:1, :N_FOX]
        for i, n in enumerate(NORMS):
            t[n] = st[i + 1:i + 2]
        outs += [t[n] for n in ORDER]
    return tuple(outs)
```

```python
import functools
import math

import jax
import jax.numpy as jnp
import numpy as np
from jax import lax
from jax.experimental import pallas as pl
from jax.experimental.pallas import tpu as pltpu
from jax.experimental.pallas import tpu_sc as plsc

f32 = jnp.float32
bf16 = jnp.bfloat16
SDS = jax.ShapeDtypeStruct
MESH = pl.DeviceIdType.MESH

S = 2048
D = 1024
HD = 64
BLK = 128
N_FOX = 8
FOX_W = N_FOX * HD
DIL_GROUPS = ((128, 1), (512, 4), (2048, 16))
SLOTS = 4
DIL_W = SLOTS * HD
QKV_W = 3 * DIL_W
VR_W = 3 * FOX_W
GF_W = 2 * D + 128
F_FF = 2816
ROPE_DIM = 16
ROPE_THETA = 500000.0
EPS = 1e-6
NEG = -1e30
SCALE = 1.0 / math.sqrt(HD)
IN_COLS = 5896
N_SHARD = 4

ADAM_LR, ADAM_B1, ADAM_B2, ADAM_EPS, ADAM_WD, ADAM_STEP = 0.001, 0.9, 0.999, 1e-08, 0.01, 10

VMEM_V7X = 64 * 1024 * 1024
VMEM_PLAN_MAX = VMEM_V7X - 8 * 1024 * 1024

TM = 256
TQ = 256

W_NAMES = ("w_in", "w_proj_a", "w_proj_b", "w_out", "w_ffn_gate", "w_ffn_up", "w_ffn_down")
TRANSPOSED = ("w_in", "w_ffn_gate", "w_ffn_up")
IN_SHARD = IN_COLS // N_SHARD
IN_SHARD_PAD = 1504
SHARD_SHAPE = dict(w_in=(IN_SHARD_PAD, D), w_proj_a=(DIL_W, D // N_SHARD), w_proj_b=(FOX_W, D // N_SHARD),
                   w_out=(D // N_SHARD, D), w_ffn_gate=(F_FF // N_SHARD, D), w_ffn_up=(F_FF // N_SHARD, D),
                   w_ffn_down=(F_FF // N_SHARD, D))
SMALL_ROWS = 8
LOSS_ROW = 5


def _nbytes(shape, dtype):
    return math.prod(shape) * jnp.dtype(dtype).itemsize


def _params(semantics, block_bytes, temp_bytes=0):
    need = 2 * block_bytes + temp_bytes + (2 << 20)
    return pltpu.CompilerParams(dimension_semantics=semantics, vmem_limit_bytes=int(min(need, VMEM_PLAN_MAX)))


def _row(w, tm=TM):
    return pl.BlockSpec((tm, w), lambda i: (i, 0))


def _vec(w):
    return pl.BlockSpec((1, w), lambda i: (0, 0))


def _mm(pairs, dims, out_dtype, *, tm, tn, name, m_inner=False):
    a0, b0 = pairs[0]
    m_dim = a0.shape[1] if dims == "tn" else a0.shape[0]
    n_dim = b0.shape[0] if dims == "nt" else b0.shape[1]
    contract = {"nn": ((1,), (0,)), "nt": ((1,), (1,)), "tn": ((0,), (0,))}[dims]
    n_pairs = len(pairs)
    assert m_dim % tm == 0 and n_dim % tn == 0, (name, m_dim, n_dim, tm, tn)

    def body(*refs):
        o_ref = refs[-1]
        acc = None
        for p in range(n_pairs):
            a = refs[2 * p][...].astype(bf16)
            b = refs[2 * p + 1][...].astype(bf16)
            t = lax.dot_general(a, b, (contract, ((), ())), preferred_element_type=f32)
            acc = t if acc is None else acc + t
        o_ref[...] = acc.astype(o_ref.dtype)

    if m_inner:
        grid = (n_dim // tn, m_dim // tm)
        mi = lambda j, i: i
        ni = lambda j, i: j
    else:
        grid = (m_dim // tm, n_dim // tn)
        mi = lambda i, j: i
        ni = lambda i, j: j
    in_specs, block_bytes, args = [], 0, []
    for a, b in pairs:
        k_dim = a.shape[0] if dims == "tn" else a.shape[1]
        if dims == "tn":
            in_specs.append(pl.BlockSpec((k_dim, tm), lambda *g: (0, mi(*g))))
        else:
            in_specs.append(pl.BlockSpec((tm, k_dim), lambda *g: (mi(*g), 0)))
        if dims == "nt":
            in_specs.append(pl.BlockSpec((tn, k_dim), lambda *g: (ni(*g), 0)))
        else:
            in_specs.append(pl.BlockSpec((k_dim, tn), lambda *g: (0, ni(*g))))
        block_bytes += _nbytes((tm, k_dim), a.dtype) + _nbytes((tn, k_dim), b.dtype)
        args += [a, b]
    block_bytes += _nbytes((tm, tn), out_dtype)
    temp = _nbytes((tm, tn), f32) * 2 + sum(_nbytes((tm, a.shape[0] if dims == "tn" else a.shape[1]), bf16)
                                            + _nbytes((tn, a.shape[0] if dims == "tn" else a.shape[1]), bf16)
                                            for a, _ in pairs)
    return pl.pallas_call(
        body, grid=grid, in_specs=in_specs,
        out_specs=pl.BlockSpec((tm, tn), lambda *g: (mi(*g), ni(*g))),
        out_shape=SDS((m_dim, n_dim), out_dtype), name=name,
        compiler_params=_params(("parallel", "parallel"), block_bytes, temp),
    )(*args)


def _rms(x, g):
    r = lax.rsqrt(jnp.mean(x * x, axis=-1, keepdims=True) + EPS)
    return x * r * g


def _rms_bwd(x, g, dy):
    r = lax.rsqrt(jnp.mean(x * x, axis=-1, keepdims=True) + EPS)
    xh = x * r
    dxh = dy * g
    dx = r * (dxh - xh * jnp.mean(dxh * xh, axis=-1, keepdims=True))
    return dx, jnp.sum(dy * xh, axis=0, keepdims=True)


def _acc_rows(ref, val):
    @pl.when(pl.program_id(0) == 0)
    def _():
        ref[...] = jnp.zeros_like(ref)
    ref[...] += val


def _norm_fwd(x, g, ds):
    def scale_body(x_ref, r_ref):
        x = x_ref[...]
        r_ref[...] = jnp.broadcast_to(lax.rsqrt(jnp.mean(x * x, axis=-1, keepdims=True) + EPS), (TM, 128))

    scale = pl.pallas_call(
        scale_body, grid=(S // TM,), in_specs=[_row(D)], out_specs=_row(128), out_shape=SDS((S, 128), f32),
        name="norm_mix_pre_scale", compiler_params=_params(("parallel",), 5 * TM * D, 8 * TM * D))(x)

    def body(x_ref, r_ref, g_ref, *h_refs):
        g = g_ref[...]
        h_refs[0][...] = (x_ref[...] * r_ref[...] * g).astype(bf16)
        for d, h_ref in zip(ds, h_refs[1:]):
            rows = S // d
            for r in range(d):
                at = pl.ds(r, rows, stride=d)
                h_ref[r * rows:(r + 1) * rows, :] = (x_ref[at, :] * r_ref[at, :] * g).astype(bf16)

    blk = pl.BlockSpec((S, 128), lambda c: (0, c))
    return pl.pallas_call(
        body, grid=(D // 128,), in_specs=[blk, pl.BlockSpec((S, 128), lambda c: (0, 0)), pl.BlockSpec((1, 128), lambda c: (0, c))],
        out_specs=[blk] * (1 + len(ds)), out_shape=[SDS((S, D), bf16)] * (1 + len(ds)), name="norm_mix_pre",
        compiler_params=_params(("parallel",), S * 128 * (8 + 2 * (1 + len(ds))), 8 * S * 128))(x, scale, g)


def _perm_rows(xs, ds, name):
    n = len(xs)

    def body(*refs):
        outs = iter(refs[n:])
        for x_ref in refs[:n]:
            for d in ds:
                o_ref, rows = next(outs), S // d
                for r in range(d):
                    o_ref[r * rows:(r + 1) * rows, :] = x_ref[pl.ds(r, rows, stride=d), :]

    blk = pl.BlockSpec((S, 128), lambda c: (0, c))
    w = xs[0].shape[1]
    return pl.pallas_call(
        body, grid=(w // 128,), in_specs=[blk] * n, out_specs=[blk] * (n * len(ds)),
        out_shape=[SDS((S, w), f32)] * (n * len(ds)), name=name,
        compiler_params=_params(("parallel",), 4 * S * 128 * n * (1 + len(ds))))(*xs)


def _unperm_sum(nat, perms, ds, name):
    n = len(perms)

    def body(*refs):
        a_ref, o_ref, sc = refs[0], refs[n + 1], refs[n + 2]
        acc = a_ref[...]
        for b_ref, d in zip(refs[1:n + 1], ds):
            rows = S // d
            for r in range(d):
                sc[pl.ds(r, rows, stride=d), :] = b_ref[r * rows:(r + 1) * rows, :]
            acc = acc + sc[...]
        o_ref[...] = acc

    blk = pl.BlockSpec((S, 128), lambda c: (0, c))
    w = nat.shape[1]
    return pl.pallas_call(
        body, grid=(w // 128,), in_specs=[blk] * (n + 1), out_specs=blk, out_shape=SDS((S, w), f32),
        scratch_shapes=[pltpu.VMEM((S, 128), f32)], name=name,
        compiler_params=_params(("parallel",), 4 * S * 128 * (n + 2), 8 * S * 128))(nat, *perms)


def _whole(a):
    return pl.BlockSpec(a.shape, lambda i: (0,) * a.ndim)


def _resid_norm_fwd(x, merged, w_out, g_post, g_pre):
    def body(x_ref, mg_ref, w_ref, gp_ref, gn_ref, mix_ref, x2_ref, h_ref):
        mix = jnp.dot(mg_ref[...], w_ref[...], preferred_element_type=f32)
        x2 = x_ref[...] + _rms(mix, gp_ref[...])
        mix_ref[...] = mix
        x2_ref[...] = x2
        h_ref[...] = _rms(x2, gn_ref[...]).astype(bf16)

    return pl.pallas_call(
        body, grid=(S // TM,), in_specs=[_row(D), _row(D), _whole(w_out), _vec(D), _vec(D)], out_specs=[_row(D)] * 3,
        out_shape=[SDS((S, D), f32), SDS((S, D), f32), SDS((S, D), bf16)], name="proj_out_norm",
        compiler_params=_params(("parallel",), 16 * TM * D + 2 * D * D, 16 * TM * D))(x, merged, w_out, g_post, g_pre)


def _loss_head(x2, a_act, w_down, g_post, target):
    def body(x2_ref, a_ref, w_ref, g_ref, t_ref, loss_ref, dy_ref, dff_ref, dg_ref):
        ff = jnp.dot(a_ref[...], w_ref[...], preferred_element_type=f32)
        g = g_ref[...]
        err = x2_ref[...] + _rms(ff, g) - t_ref[...]
        dy = err * (1.0 / D)
        dff, dg = _rms_bwd(ff, g, dy)
        dy_ref[...] = dy
        dff_ref[...] = dff.astype(bf16)
        _acc_rows(dg_ref, dg)
        _acc_rows(loss_ref, jnp.full((1, 128), jnp.sum(err * err), f32))

    return pl.pallas_call(
        body, grid=(S // TM,), in_specs=[_row(D), _row(F_FF), _whole(w_down), _vec(D), _row(D)],
        out_specs=[_vec(128), _row(D), _row(D), _vec(D)],
        out_shape=[SDS((1, 128), f32), SDS((S, D), f32), SDS((S, D), bf16), SDS((1, D), f32)], name="ffn_down_loss",
        compiler_params=_params(("arbitrary",), 14 * TM * D + 2 * TM * F_FF + 2 * F_FF * D, 28 * TM * D),
    )(x2, a_act, w_down, g_post, target)


def _norm_bwd_mid(dy, d_g, d_u, w_gate_t, w_up_t, x2, mix, g_ffn_pre, g_mix_post):
    def body(dy_ref, dgt_ref, dut_ref, wg_ref, wu_ref, x2_ref, mix_ref, g3_ref, g2_ref, dx2_ref, dmix_ref, dg3_ref, dg2_ref):
        dh = jnp.dot(dgt_ref[...], wg_ref[...], preferred_element_type=f32)
        dh += jnp.dot(dut_ref[...], wu_ref[...], preferred_element_type=f32)
        d3, dg3 = _rms_bwd(x2_ref[...], g3_ref[...], dh)
        dx2 = dy_ref[...] + d3
        dmix, dg2 = _rms_bwd(mix_ref[...], g2_ref[...], dx2)
        dx2_ref[...] = dx2
        dmix_ref[...] = dmix.astype(bf16)
        _acc_rows(dg3_ref, dg3)
        _acc_rows(dg2_ref, dg2)

    return pl.pallas_call(
        body, grid=(S // TM,),
        in_specs=[_row(D), _row(F_FF), _row(F_FF), _whole(w_gate_t), _whole(w_up_t), _row(D), _row(D), _vec(D), _vec(D)],
        out_specs=[_row(D), _row(D), _vec(D), _vec(D)],
        out_shape=[SDS((S, D), f32), SDS((S, D), bf16), SDS((1, D), f32), SDS((1, D), f32)], name="ffn_bwd_in_norm",
        compiler_params=_params(("arbitrary",), 18 * TM * D + 4 * TM * F_FF + 4 * F_FF * D, 28 * TM * D),
    )(dy, d_g, d_u, w_gate_t, w_up_t, x2, mix, g_ffn_pre, g_mix_post)


def _norm_bwd_in(dx2, dh1, x, g):
    def body(dx2_ref, dh_ref, x_ref, g_ref, gx_ref, dg_ref):
        d1, dg = _rms_bwd(x_ref[...], g_ref[...], dh_ref[...])
        gx_ref[...] = dx2_ref[...] + d1
        _acc_rows(dg_ref, dg)

    return pl.pallas_call(
        body, grid=(S // TM,), in_specs=[_row(D)] * 3 + [_vec(D)], out_specs=[_row(D), _vec(D)],
        out_shape=[SDS((S, D), f32), SDS((1, D), f32)], name="norm_bwd_in",
        compiler_params=_params(("arbitrary",), 16 * TM * D, 16 * TM * D))(dx2, dh1, x, g)


def _rope_tables():
    half = ROPE_DIM // 2
    inv_freq = np.power(np.float32(ROPE_THETA), -np.arange(0, ROPE_DIM, 2, dtype=np.float32) / np.float32(ROPE_DIM))
    row = np.arange(S)
    groups = []
    for _, d in DIL_GROUPS:
        pos = ((row % (S // d)) * d + row // (S // d)).astype(np.float32)
        ang = pos[:, None] * inv_freq[None, :].astype(np.float32)
        cos, sin = np.cos(ang).astype(np.float32), np.sin(ang).astype(np.float32)
        c = np.concatenate([cos, cos, np.ones((S, HD - ROPE_DIM), np.float32)], axis=1)
        s_lo = np.concatenate([-sin, np.zeros((S, HD - half), np.float32)], axis=1)
        s_hi = np.concatenate([np.zeros((S, half), np.float32), sin, np.zeros((S, HD - ROPE_DIM), np.float32)], axis=1)
        groups.append(np.stack([np.concatenate([t, t], axis=1) for t in (c, s_lo, s_hi)]))
    return jnp.asarray(np.stack(groups))


def _rotate(x, c, lo, hi, sign):
    tile = lambda t: jnp.tile(t, (1, DIL_W // 128))
    return (x * tile(c) + pltpu.roll(x, DIL_W - ROPE_DIM // 2, 1) * (tile(lo) * sign)
            + pltpu.roll(x, ROPE_DIM // 2, 1) * (tile(hi) * sign))


def _table_specs(g):
    return [pl.BlockSpec((None, None, TM, 128), lambda i, k=k: (g, k, i, 0)) for k in range(3)]


def _proj_rope(g, h, w_t, tables):
    def body(h_ref, w_ref, c_ref, lo_ref, hi_ref, o_ref):
        x = _nt(h_ref[...], w_ref[...])
        c, lo, hi = c_ref[...], lo_ref[...], hi_ref[...]
        for part in range(2):
            cols = slice(part * DIL_W, (part + 1) * DIL_W)
            o_ref[:, cols] = _rotate(x[:, cols], c, lo, hi, 1.0).astype(bf16)
        o_ref[:, 2 * DIL_W:] = x[:, 2 * DIL_W:].astype(bf16)

    return pl.pallas_call(
        body, grid=(S // TM,), in_specs=[_row(D), pl.BlockSpec(w_t.shape, lambda i: (0, 0))] + _table_specs(g),
        out_specs=_row(QKV_W), out_shape=SDS((S, QKV_W), bf16), name=f"proj_a_{g}",
        compiler_params=_params(("parallel",), 2 * TM * (D + QKV_W) + 2 * QKV_W * D + 12 * TM * 128, 24 * TM * QKV_W),
    )(h, w_t, tables, tables, tables)


def _rope_bwd(g, dq, dk, dv, tables):
    def body(dq_ref, dk_ref, dv_ref, c_ref, lo_ref, hi_ref, o_ref):
        c, lo, hi = c_ref[...], lo_ref[...], hi_ref[...]
        o_ref[:, :DIL_W] = _rotate(dq_ref[...], c, lo, hi, -1.0).astype(bf16)
        o_ref[:, DIL_W:2 * DIL_W] = _rotate(dk_ref[...], c, lo, hi, -1.0).astype(bf16)
        o_ref[:, 2 * DIL_W:] = dv_ref[...].astype(bf16)

    return pl.pallas_call(
        body, grid=(S // TM,), in_specs=[_row(DIL_W)] * 3 + _table_specs(g), out_specs=_row(QKV_W),
        out_shape=SDS((S, QKV_W), bf16), name=f"rope_bwd_{g}",
        compiler_params=_params(("parallel",), 6 * TM * QKV_W + 12 * TM * 128, 24 * TM * QKV_W))(dq, dk, dv, tables, tables, tables)


def _nt(a, b):
    return lax.dot_general(a, b, (((1,), (1,)), ((), ())), preferred_element_type=f32)


def _tn(a, b):
    return lax.dot_general(a, b, (((0,), (0,)), ((), ())), preferred_element_type=f32)


STEP_BLOCKS = 4
STEP_ROWS = STEP_BLOCKS * BLK


def _dil_prev(g, b):
    _, d = DIL_GROUPS[g]
    nb = S // d // BLK
    if nb == 1 or (b == 0 and nb <= STEP_BLOCKS):
        return None
    return "in" if b > 0 else "halo"


def _bnt(a, b):
    return lax.dot_general(a, b, (((2,), (2,)), ((0,), (0,))), preferred_element_type=f32)


def _bnn(a, b):
    return lax.dot_general(a, b, (((2,), (1,)), ((0,), (0,))), preferred_element_type=f32)


def _btn(a, b):
    return lax.dot_general(a, b, (((1,), (1,)), ((0,), (0,))), preferred_element_type=f32)


def _on_tail(x, tail, fn):
    if tail == x.shape[0]:
        return fn(x)
    return jnp.concatenate([x[:-tail], fn(x[-tail:])], axis=0)


def _heads(ref, part):
    n = ref.shape[0] // BLK
    return jnp.stack([ref[b * BLK:(b + 1) * BLK, part * DIL_W + h * HD:part * DIL_W + (h + 1) * HD]
                      for b in range(n) for h in range(SLOTS)])


def _dil_operands(g, qkv_ref, halo_ref):
    q, kc, vc = (_heads(qkv_ref, part) for part in range(3))
    qi = lax.broadcasted_iota(jnp.int32, (1, BLK, BLK), 1)
    kj = lax.broadcasted_iota(jnp.int32, (1, BLK, BLK), 2)
    with_prev = [b for b in range(STEP_BLOCKS) if _dil_prev(g, b) is not None]
    tail = SLOTS * len(with_prev)
    if not tail:
        return q, kc, vc, None, None, kj <= qi, None, 0
    assert with_prev == list(range(STEP_BLOCKS - len(with_prev), STEP_BLOCKS))
    inside = SLOTS * sum(_dil_prev(g, b) == "in" for b in with_prev)
    kp, vp, prev = kc[:inside], vc[:inside], jnp.broadcast_to(kj >= qi, (inside, BLK, BLK))
    if inside < tail:
        no_halo = jnp.where(pl.program_id(0) == 0, BLK + 1, 0)
        kp = jnp.concatenate([_heads(halo_ref, 1), kp], axis=0)
        vp = jnp.concatenate([_heads(halo_ref, 2), vp], axis=0)
        prev = jnp.concatenate([jnp.broadcast_to(kj >= qi + no_halo, (SLOTS, BLK, BLK)), prev], axis=0)
    return q, kc, vc, kp, vp, kj <= qi, prev, tail


def _dil_in_specs(g, n_aux):
    step = lambda w: pl.BlockSpec((STEP_ROWS, w), lambda i: (i, 0))
    halo = [pl.BlockSpec((BLK, QKV_W), lambda i: (jnp.maximum(i * STEP_BLOCKS - 1, 0), 0))]
    needs_halo = _dil_prev(g, 0) == "halo"
    return [step(QKV_W)] + (halo if needs_halo else []) + [step(DIL_W)] * n_aux, needs_halo


def _dil_fwd(g, qkv):
    in_specs, needs_halo = _dil_in_specs(g, 0)

    def body(*refs):
        qkv_ref, halo_ref = refs[0], refs[1] if needs_halo else None
        o_ref, lse_ref = refs[-2:]
        q, kc, vc, kp, vp, cur, prev, tail = _dil_operands(g, qkv_ref, halo_ref)
        sc = jnp.where(cur, _bnt(q, kc) * SCALE, NEG)
        m = jnp.max(sc, axis=-1, keepdims=True)
        if tail:
            sp = jnp.where(prev, _bnt(q[-tail:], kp) * SCALE, NEG)
            m = _on_tail(m, tail, lambda t: jnp.maximum(t, jnp.max(sp, axis=-1, keepdims=True)))
            pp = jnp.exp(sp - m[-tail:])
        pc = jnp.exp(sc - m)
        den = jnp.sum(pc, axis=-1, keepdims=True)
        if tail:
            den = _on_tail(den, tail, lambda t: t + jnp.sum(pp, axis=-1, keepdims=True))
        inv = 1.0 / den
        o = _bnn((pc * inv).astype(bf16), vc)
        if tail:
            o = _on_tail(o, tail, lambda t: t + _bnn((pp * inv[-tail:]).astype(bf16), vp))
        lse = m + jnp.log(den)
        for b in range(STEP_BLOCKS):
            for h in range(SLOTS):
                rows, hs = slice(b * BLK, (b + 1) * BLK), slice(h * HD, (h + 1) * HD)
                o_ref[rows, hs] = o[SLOTS * b + h]
                lse_ref[rows, hs] = jnp.broadcast_to(lse[SLOTS * b + h], (BLK, HD))

    out = pl.BlockSpec((STEP_ROWS, DIL_W), lambda i: (i, 0))
    return pl.pallas_call(
        body, grid=(S // STEP_ROWS,), in_specs=in_specs, out_specs=[out, out], out_shape=[SDS((S, DIL_W), f32)] * 2,
        name=f"dil_fwd_{g}", compiler_params=_params(("parallel",), 12 * STEP_ROWS * DIL_W, 2 << 20),
    )(*([qkv] * (2 if needs_halo else 1)))


def _dil_combine(outs, lses):
    def body(o0, o1, o2, l0, l1, l2, out_ref, lse_ref, so1, so2, sl1, sl2):
        for (_, d), src, dst in ((DIL_GROUPS[1], o1, so1), (DIL_GROUPS[2], o2, so2),
                                 (DIL_GROUPS[1], l1, sl1), (DIL_GROUPS[2], l2, sl2)):
            rows = S // d
            for r in range(d):
                dst[pl.ds(r, rows, stride=d), :] = src[r * rows:(r + 1) * rows, :]
        a, b, c = l0[...], sl1[...], sl2[...]
        m = jnp.maximum(jnp.maximum(a, b), c)
        ea, eb, ec = jnp.exp(a - m), jnp.exp(b - m), jnp.exp(c - m)
        z = ea + eb + ec
        inv = 1.0 / z
        out_ref[...] = (ea * inv) * o0[...] + (eb * inv) * so1[...] + (ec * inv) * so2[...]
        lse_ref[...] = m + jnp.log(z)

    blk = pl.BlockSpec((S, 128), lambda c: (0, c))
    return pl.pallas_call(
        body, grid=(DIL_W // 128,), in_specs=[blk] * 6, out_specs=[blk] * 2,
        out_shape=[SDS((S, DIL_W), f32)] * 2, scratch_shapes=[pltpu.VMEM((S, 128), f32)] * 4, name="dil_combine",
        compiler_params=_params(("parallel",), 32 * S * 128, 32 * S * 128))(*outs, *lses)


def _dil_bwd(g, qkv, d_out, delta, lse):
    in_specs, needs_halo = _dil_in_specs(g, 3)

    def body(*refs):
        qkv_ref, halo_ref = refs[0], refs[1] if needs_halo else None
        do_ref, dl_ref, lse_ref, dq_ref, dk_ref, dv_ref = refs[-6:]
        q, kc, vc, kp, vp, cur, prev, tail = _dil_operands(g, qkv_ref, halo_ref)
        tiles = [(slice(b * BLK, (b + 1) * BLK), h) for b in range(STEP_BLOCKS) for h in range(SLOTS)]
        do = jnp.stack([do_ref[rows, h * HD:(h + 1) * HD] for rows, h in tiles]).astype(bf16)
        lse = jnp.stack([lse_ref[rows, h * HD:h * HD + 1] for rows, h in tiles])
        delta = jnp.stack([dl_ref[rows, h * HD:h * HD + 1] for rows, h in tiles])

        def probs(q, k, mask, lse, do, v, delta):
            p = jnp.exp(jnp.where(mask, _bnt(q, k) * SCALE, NEG) - lse)
            ds = p * (_bnt(do, v) - delta) * SCALE
            return p.astype(bf16), ds.astype(bf16)

        p, ds = probs(q, kc, cur, lse, do, vc, delta)
        dq, dk, dv = _bnn(ds, kc), _btn(ds, q), _btn(p, do)
        if tail:
            p, ds = probs(q[-tail:], kp, prev, lse[-tail:], do[-tail:], vp, delta[-tail:])
            dq = _on_tail(dq, tail, lambda t: t + _bnn(ds, kp))
            dk_p, dv_p = _btn(ds, q[-tail:]), _btn(p, do[-tail:])
            inside = tail - SLOTS if needs_halo else tail
            pad = jnp.zeros((len(tiles) - inside, BLK, HD), f32)
            dk = dk + jnp.concatenate([dk_p[tail - inside:], pad], axis=0)
            dv = dv + jnp.concatenate([dv_p[tail - inside:], pad], axis=0)
        first = pl.multiple_of(pl.program_id(0) * STEP_ROWS, STEP_ROWS)
        for t, (rows, h) in enumerate(tiles):
            hs = slice(h * HD, (h + 1) * HD)
            own = pl.ds(pl.multiple_of(first + rows.start, BLK), BLK)
            dq_ref[rows, hs] = dq[t]
            dk_ref[own, hs] = dk[t]
            dv_ref[own, hs] = dv[t]
        if needs_halo:
            before = pl.ds(pl.multiple_of(jnp.maximum(first - BLK, 0), BLK), BLK)
            for h in range(SLOTS):
                hs = slice(h * HD, (h + 1) * HD)
                dk_ref[before, hs] += dk_p[h]
                dv_ref[before, hs] += dv_p[h]

    whole = pl.BlockSpec((S, DIL_W), lambda i: (0, 0))
    return pl.pallas_call(
        body, grid=(S // STEP_ROWS,), in_specs=in_specs,
        out_specs=[pl.BlockSpec((STEP_ROWS, DIL_W), lambda i: (i, 0)), whole, whole],
        out_shape=[SDS((S, DIL_W), f32)] * 3, name=f"dil_bwd_{g}",
        compiler_params=_params(("arbitrary",), 20 * STEP_ROWS * DIL_W + 8 * S * DIL_W, 2 << 20),
    )(*([qkv] * (2 if needs_halo else 1)), d_out, delta, lse)


def _scan_rows(x, reverse):
    row = lax.broadcasted_iota(jnp.int32, x.shape, 0)
    k = 1
    while k < S:
        if reverse:
            x = x + jnp.where(row < S - k, pltpu.roll(x, S - k, 0), 0.0)
        else:
            x = x + jnp.where(row >= k, pltpu.roll(x, k, 0), 0.0)
        k *= 2
    return x


N_PAIR = N_FOX // 2
_PAIR_Q = pl.BlockSpec((None, S, 128), lambda p: (p, 0, 0))
_PAIR_K = pl.BlockSpec((None, 8, S), lambda p: (p, 0, 0))


def _forget_fwd(fz, b128):
    def body(z_ref, b_ref, fq_ref, fk_ref):
        z = z_ref[...] + b_ref[...]
        logf = jnp.minimum(z, 0.0) - jnp.log1p(jnp.exp(-jnp.abs(z)))
        f_cum = _scan_rows(logf, reverse=False)
        f_cum_t = f_cum.T
        fq_ref[...] = jnp.zeros_like(fq_ref)
        fk_ref[...] = jnp.zeros_like(fk_ref)
        for p in range(N_PAIR):
            fq_ref[p, :, 0:2] = f_cum[:, 2 * p:2 * p + 2]
            fk_ref[p, 0:2, :] = f_cum_t[2 * p:2 * p + 2, :]

    return pl.pallas_call(
        body, grid=(1,), in_specs=[pl.BlockSpec((S, 128), lambda i: (0, 0)), _vec(128)],
        out_specs=[pl.BlockSpec((N_PAIR, S, 128), lambda i: (0, 0, 0)), pl.BlockSpec((N_PAIR, 8, S), lambda i: (0, 0, 0))],
        out_shape=[SDS((N_PAIR, S, 128), f32), SDS((N_PAIR, 8, S), f32)], name="forget_fwd",
        compiler_params=_params(("arbitrary",), 24 * S * 128, 24 * S * 128))(fz, b128)


def _forget_bwd(fz, b128, d_f_cols, d_f_rows):
    def body(z_ref, b_ref, dfc_ref, dfr_ref, dz_ref, db_ref, df_sc):
        z = z_ref[...] + b_ref[...]
        df_sc[...] = jnp.zeros_like(df_sc)
        for p in range(N_PAIR):
            df_sc[:, 2 * p:2 * p + 2] = dfr_ref[p, :, 0:2] + dfc_ref[p].T[:, 0:2]
        dz = _scan_rows(df_sc[...], reverse=True) * jax.nn.sigmoid(-z)
        dz_ref[...] = dz
        db_ref[...] = jnp.sum(dz, axis=0, keepdims=True)

    full = pl.BlockSpec((S, 128), lambda i: (0, 0))
    return pl.pallas_call(
        body, grid=(1,),
        in_specs=[full, _vec(128), pl.BlockSpec((N_PAIR, 8, S), lambda i: (0, 0, 0)), pl.BlockSpec((N_PAIR, S, 128), lambda i: (0, 0, 0))],
        out_specs=[full, _vec(128)], out_shape=[SDS((S, 128), f32), SDS((1, 128), f32)],
        scratch_shapes=[pltpu.VMEM((S, 128), f32)], name="forget_bwd",
        compiler_params=_params(("arbitrary",), 32 * S * 128, 24 * S * 128))(fz, b128, d_f_cols, d_f_rows)


def _fox_scores(q_ref, k_ref, fq_ref, fk_ref, qi, hh):
    n = (qi + 1) * TQ
    rows, hs = slice(qi * TQ, n), slice(hh * HD, (hh + 1) * HD)
    s = _nt(q_ref[rows, hs], k_ref[0:n, hs]) * SCALE + (fq_ref[rows, hh:hh + 1] - fk_ref[hh:hh + 1, 0:n])
    qpos = qi * TQ + lax.broadcasted_iota(jnp.int32, (TQ, n), 0)
    kpos = lax.broadcasted_iota(jnp.int32, (TQ, n), 1)
    return jnp.where(kpos <= qpos, s, NEG)


def _pair_cols(first):
    return pl.BlockSpec((S, 128), lambda p: (0, first + p))


def _fox_fwd(vr, fq, fk):
    def body(q_ref, k_ref, v_ref, fq_ref, fk_ref, o_ref, lse_ref):
        lse_ref[...] = jnp.zeros_like(lse_ref)
        for hh in range(2):
            hs = slice(hh * HD, (hh + 1) * HD)
            for qi in range(S // TQ):
                n = (qi + 1) * TQ
                rows = slice(qi * TQ, n)
                s = _fox_scores(q_ref, k_ref, fq_ref, fk_ref, qi, hh)
                m = jnp.max(s, axis=-1, keepdims=True)
                p = jnp.exp(s - m)
                den = jnp.sum(p, axis=-1, keepdims=True)
                o_ref[rows, hs] = jnp.dot((p * (1.0 / den)).astype(bf16), v_ref[0:n, hs], preferred_element_type=f32)
                lse_ref[rows, hh:hh + 1] = m + jnp.log(den)

    return pl.pallas_call(
        body, grid=(N_PAIR,), in_specs=[_pair_cols(0), _pair_cols(N_PAIR), _pair_cols(2 * N_PAIR), _PAIR_Q, _PAIR_K],
        out_specs=[_pair_cols(0), _PAIR_Q], out_shape=[SDS((S, FOX_W), f32), SDS((N_PAIR, S, 128), f32)],
        name="fox_fwd", compiler_params=_params(("parallel",), 12 * S * 128, 16 * TQ * S),
    )(vr, vr, vr, fq, fk)


def _fox_bwd(vr, fq, fk, lse, d_out, delta):
    def body(q_ref, k_ref, v_ref, do_ref, fq_ref, fk_ref, lse_ref, dl_ref, dq_ref, dk_ref, dv_ref, dfc_ref, dfr_ref,
             dk_sc, dv_sc):
        dfc_ref[...] = jnp.zeros_like(dfc_ref)
        dfr_ref[...] = jnp.zeros_like(dfr_ref)
        for hh in range(2):
            hs = slice(hh * HD, (hh + 1) * HD)
            dk_sc[...] = jnp.zeros_like(dk_sc)
            dv_sc[...] = jnp.zeros_like(dv_sc)
            for qi in range(S // TQ):
                n = (qi + 1) * TQ
                rows = slice(qi * TQ, n)
                q, do, k, v = q_ref[rows, hs], do_ref[rows, hs], k_ref[0:n, hs], v_ref[0:n, hs]
                p = jnp.exp(_fox_scores(q_ref, k_ref, fq_ref, fk_ref, qi, hh) - lse_ref[rows, hh:hh + 1])
                ds = p * (_nt(do, v) - dl_ref[rows, hh:hh + 1])
                dsb = ds.astype(bf16)
                dq_ref[rows, hs] = jnp.dot(dsb, k, preferred_element_type=f32) * SCALE
                dk_sc[0:n, :] += _tn(dsb, q) * SCALE
                dv_sc[0:n, :] += _tn(p.astype(bf16), do)
                dfc_ref[hh:hh + 1, 0:n] -= jnp.sum(ds, axis=0, keepdims=True)
                dfr_ref[rows, hh:hh + 1] = jnp.sum(ds, axis=-1, keepdims=True)
            dk_ref[:, hs] = dk_sc[...]
            dv_ref[:, hs] = dv_sc[...]

    cols = [_pair_cols(k * N_PAIR) for k in range(3)]
    return pl.pallas_call(
        body, grid=(N_PAIR,), in_specs=cols + [_pair_cols(0), _PAIR_Q, _PAIR_K, _PAIR_Q, _PAIR_Q],
        out_specs=[_pair_cols(0)] * 3 + [_PAIR_K, _PAIR_Q],
        out_shape=[SDS((S, FOX_W), f32)] * 3 + [SDS((N_PAIR, 8, S), f32), SDS((N_PAIR, S, 128), f32)],
        scratch_shapes=[pltpu.VMEM((S, HD), f32)] * 2, name="fox_bwd",
        compiler_params=_params(("parallel",), 32 * S * 128, 24 * TQ * S),
    )(vr, vr, vr, d_out, fq, fk, lse, delta)


def _merge_fwd(out_a, out_b, w_a, w_b, gf):
    cw = D // N_SHARD

    def body(oa_ref, ob_ref, wa_ref, wb_ref, ga_ref, gb_ref, ya_ref, yb_ref, mg_ref):
        oa, ob = oa_ref[...].astype(bf16), ob_ref[...].astype(bf16)
        for j in range(N_SHARD):
            cols = slice(j * cw, (j + 1) * cw)
            ya = jnp.dot(oa, wa_ref[j], preferred_element_type=f32)
            yb = jnp.dot(ob, wb_ref[j], preferred_element_type=f32)
            ya_ref[:, cols] = ya
            yb_ref[:, cols] = yb
            mg_ref[:, cols] = (jax.nn.sigmoid(ga_ref[:, cols]) * ya + jax.nn.sigmoid(gb_ref[:, cols]) * yb).astype(bf16)

    full = lambda a: pl.BlockSpec(a.shape, lambda i: (0, 0, 0))
    return pl.pallas_call(
        body, grid=(S // TM,),
        in_specs=[_row(DIL_W), _row(FOX_W), full(w_a), full(w_b), _row(D), pl.BlockSpec((TM, D), lambda i: (i, 1))],
        out_specs=[_row(D)] * 3, out_shape=[SDS((S, D), f32), SDS((S, D), f32), SDS((S, D), bf16)], name="merge_fwd",
        compiler_params=_params(("parallel",), 22 * TM * D + 2 * (DIL_W + FOX_W) * D, 16 * TM * D),
    )(out_a, out_b, w_a, w_b, gf, gf)


def _merge_bwd(d_mix, w_out, ya, yb, gf):
    def body(dx_ref, w_ref, ya_ref, yb_ref, ga_ref, gb_ref, dya_ref, dyb_ref, dg_ref):
        dm = _nt(dx_ref[...], w_ref[...])
        sa, sb = jax.nn.sigmoid(ga_ref[...]), jax.nn.sigmoid(gb_ref[...])
        dya_ref[...] = (dm * sa).astype(bf16)
        dyb_ref[...] = (dm * sb).astype(bf16)
        dg_ref[:, :D] = (dm * ya_ref[...] * sa * (1.0 - sa)).astype(bf16)
        dg_ref[:, D:] = (dm * yb_ref[...] * sb * (1.0 - sb)).astype(bf16)

    return pl.pallas_call(
        body, grid=(S // TM,),
        in_specs=[_row(D), _whole(w_out)] + [_row(D)] * 3 + [pl.BlockSpec((TM, D), lambda i: (i, 1))],
        out_specs=[_row(D), _row(D), _row(2 * D)],
        out_shape=[SDS((S, D), bf16), SDS((S, D), bf16), SDS((S, 2 * D), bf16)], name="proj_out_bwd_merge",
        compiler_params=_params(("parallel",), 26 * TM * D + 2 * D * D, 28 * TM * D))(d_mix, w_out, ya, yb, gf, gf)


def _branch_bwd(d_ya, d_yb, w_a, w_b, out_a, out_b):
    cw = D // N_SHARD

    def body(dya_ref, dyb_ref, wa_ref, wb_ref, oa_ref, ob_ref, doa_ref, dla_ref, dob_ref, dlb_ref):
        doa = jnp.zeros((TM, DIL_W), f32)
        dob = jnp.zeros((TM, FOX_W), f32)
        for j in range(N_SHARD):
            cols = slice(j * cw, (j + 1) * cw)
            doa += _nt(dya_ref[:, cols], wa_ref[j])
            dob += _nt(dyb_ref[:, cols], wb_ref[j])
        doa_ref[...] = doa
        dob_ref[...] = dob.astype(bf16)
        prod_a = doa * oa_ref[...]
        for h in range(SLOTS):
            hs = slice(h * HD, (h + 1) * HD)
            dla_ref[:, hs] = jnp.broadcast_to(jnp.sum(prod_a[:, hs], axis=-1, keepdims=True), (TM, HD))
        prod_b = dob * ob_ref[...]
        dlb_ref[...] = jnp.zeros_like(dlb_ref)
        for h in range(N_FOX):
            dlb_ref[h // 2, :, h % 2:h % 2 + 1] = jnp.sum(prod_b[:, h * HD:(h + 1) * HD], axis=-1, keepdims=True)

    full = lambda a: pl.BlockSpec(a.shape, lambda i: (0, 0, 0))
    return pl.pallas_call(
        body, grid=(S // TM,),
        in_specs=[_row(D), _row(D), full(w_a), full(w_b), _row(DIL_W), _row(FOX_W)],
        out_specs=[_row(DIL_W), _row(DIL_W), _row(FOX_W), pl.BlockSpec((N_PAIR, TM, 128), lambda i: (0, i, 0))],
        out_shape=[SDS((S, DIL_W), f32), SDS((S, DIL_W), f32), SDS((S, FOX_W), bf16), SDS((N_PAIR, S, 128), f32)],
        name="branch_bwd", compiler_params=_params(("parallel",), 8 * TM * D + 2 * (DIL_W + FOX_W) * D, 8 * TM * D),
    )(d_ya, d_yb, w_a, w_b, out_a, out_b)


def _branch_grads(out_a, out_b, d_ya, d_yb):
    cw = D // N_SHARD

    def body(oa_ref, ob_ref, dya_ref, dyb_ref, ga_ref, gb_ref):
        ga_ref[...] = _tn(oa_ref[...].astype(bf16), dya_ref[...]).astype(bf16)
        gb_ref[...] = _tn(ob_ref[...].astype(bf16), dyb_ref[...]).astype(bf16)

    whole = lambda w: pl.BlockSpec((S, w), lambda j: (0, 0))
    cols = pl.BlockSpec((S, cw), lambda j: (0, j))
    return pl.pallas_call(
        body, grid=(N_SHARD,), in_specs=[whole(DIL_W), whole(FOX_W), cols, cols],
        out_specs=[pl.BlockSpec((None, DIL_W, cw), lambda j: (j, 0, 0)), pl.BlockSpec((None, FOX_W, cw), lambda j: (j, 0, 0))],
        out_shape=[SDS((N_SHARD, DIL_W, cw), bf16), SDS((N_SHARD, FOX_W, cw), bf16)], name="grad_w_proj_ab",
        compiler_params=_params(("parallel",), 4 * S * (DIL_W + FOX_W) + 4 * S * cw + 4 * (DIL_W + FOX_W) * cw,
                                4 * S * (DIL_W + FOX_W)))(out_a, out_b, d_ya, d_yb)


FF_TN = F_FF // 2
FF_TM = 512


def _ffn_fwd(h, w_gate_t, w_up_t):
    def body(h_ref, wg_ref, wu_ref, g_ref, u_ref, a_ref):
        hb = h_ref[...]
        g = _nt(hb, wg_ref[...])
        u = _nt(hb, wu_ref[...])
        g_ref[...] = g
        u_ref[...] = u
        a_ref[...] = (g * jax.nn.sigmoid(g) * u).astype(bf16)

    tile = pl.BlockSpec((FF_TM, FF_TN), lambda j, i: (i, j))
    wspec = pl.BlockSpec((FF_TN, D), lambda j, i: (j, 0))
    return pl.pallas_call(
        body, grid=(F_FF // FF_TN, S // FF_TM),
        in_specs=[pl.BlockSpec((FF_TM, D), lambda j, i: (i, 0)), wspec, wspec], out_specs=[tile] * 3,
        out_shape=[SDS((S, F_FF), f32), SDS((S, F_FF), f32), SDS((S, F_FF), bf16)], name="ffn_fwd",
        compiler_params=_params(("parallel", "parallel"), 2 * FF_TM * D + 4 * D * FF_TN + 10 * FF_TM * FF_TN, 16 * FF_TM * FF_TN),
    )(h, w_gate_t, w_up_t)


def _ffn_bwd_act(d_ff, w_down, g_act, u_act):
    def body(d_ref, wd_ref, g_ref, u_ref, dg_ref, du_ref):
        da = _nt(d_ref[...], wd_ref[...])
        g = g_ref[...]
        sg = jax.nn.sigmoid(g)
        du_ref[...] = (da * g * sg).astype(bf16)
        dg_ref[...] = (da * u_ref[...] * sg * (1.0 + g * (1.0 - sg))).astype(bf16)

    tile = pl.BlockSpec((FF_TM, FF_TN), lambda j, i: (i, j))
    return pl.pallas_call(
        body, grid=(F_FF // FF_TN, S // FF_TM),
        in_specs=[pl.BlockSpec((FF_TM, D), lambda j, i: (i, 0)), pl.BlockSpec((FF_TN, D), lambda j, i: (j, 0)), tile, tile],
        out_specs=[tile, tile], out_shape=[SDS((S, F_FF), bf16)] * 2, name="ffn_bwd_act",
        compiler_params=_params(("parallel", "parallel"), 2 * FF_TM * D + 2 * D * FF_TN + 12 * FF_TM * FF_TN, 16 * FF_TM * FF_TN),
    )(d_ff, w_down, g_act, u_act)


def _row_tile(rows):
    return next(t for t in (376, 128, 176, 64, 32, 16, 8) if rows % t == 0)


def _adamw_math(w, g, m, v):
    c1 = 1.0 - ADAM_B1 ** ADAM_STEP
    c2 = 1.0 - ADAM_B2 ** ADAM_STEP
    m_new = ADAM_B1 * m + (1.0 - ADAM_B1) * g
    v_new = ADAM_B2 * v + (1.0 - ADAM_B2) * (g * g)
    return -ADAM_LR * ((m_new / c1) / (jnp.sqrt(v_new / c2) + ADAM_EPS) + ADAM_WD * w), m_new, v_new


def _adamw(w, g, m, v, name):
    rows, cols = w.shape
    tm = _row_tile(rows)

    def body(w_ref, g_ref, m_ref, v_ref, d_ref, nm_ref, nv_ref):
        d_ref[...], nm_ref[...], nv_ref[...] = _adamw_math(w_ref[...], g_ref[...], m_ref[...], v_ref[...])

    spec = pl.BlockSpec((tm, cols), lambda i: (i, 0))
    return pl.pallas_call(
        body, grid=(rows // tm,), in_specs=[spec] * 4, out_specs=[spec] * 3, out_shape=[SDS(w.shape, f32)] * 3,
        name=name, compiler_params=_params(("parallel",), 28 * tm * cols, 16 * tm * cols))(w, g, m, v)


def _adamw_halves(w, g_mine, g_theirs, m, v, name):
    rows, cols = w.shape
    tm = _row_tile(rows // 2)
    per_half = rows // 2 // tm
    core = lax.axis_index("c").astype(jnp.int32).reshape(1)

    def body(c_ref, w_ref, gm_ref, gt_ref, m_ref, v_ref, g_ref, d_ref, nm_ref, nv_ref):
        mine = pl.program_id(0) // per_half == c_ref[0]
        g = jnp.where(mine, gm_ref[...], gt_ref[...])
        g_ref[...] = g
        d_ref[...], nm_ref[...], nv_ref[...] = _adamw_math(w_ref[...], g, m_ref[...], v_ref[...])

    spec = pl.BlockSpec((tm, cols), lambda i, c_ref: (i, 0))
    in_half = lambda i, first: jnp.clip(i - first * per_half, 0, per_half - 1)
    grid_spec = pltpu.PrefetchScalarGridSpec(
        num_scalar_prefetch=1, grid=(rows // tm,),
        in_specs=[spec, pl.BlockSpec((tm, cols), lambda i, c_ref: (in_half(i, c_ref[0]), 0)),
                  pl.BlockSpec((tm, cols), lambda i, c_ref: (in_half(i, 1 - c_ref[0]), 0)), spec, spec],
        out_specs=[spec] * 4)
    return pl.pallas_call(
        body, grid_spec=grid_spec, out_shape=[SDS(w.shape, f32)] * 4, name=name,
        compiler_params=_params(("parallel",), 36 * tm * cols, 16 * tm * cols))(core, w, g_mine, g_theirs, m, v)


_ANY = pl.BlockSpec(memory_space=pl.ANY)


def _place():
    x, y, c = lax.axis_index("x"), lax.axis_index("y"), lax.axis_index("c")
    chips = [(1 - x, y), (x, 1 - y), (1 - x, 1 - y)]
    return x, y, c, chips


def _halved(t):
    return t.reshape(t.shape[:-2] + (2, t.shape[-2] // 2, t.shape[-1]))


def _gather_body(src, out, send_ici, recv_ici, send_d2d, recv_d2d):
    x, y, c, chips = _place()
    sibling = (x, y, 1 - c)
    me_j = 2 * x + y
    sends = []
    for a in range(len(src)):
        for p in range(3):
            cp = pltpu.make_async_remote_copy(
                src_ref=src[a].at[c], dst_ref=out[a].at[me_j, c], send_sem=send_ici.at[a, p],
                recv_sem=recv_ici.at[a, p], device_id=(*chips[p], c), device_id_type=MESH)
            cp.start()
            sends.append(cp)
    for a in range(len(src)):
        for p, (px, py) in enumerate(chips):
            blk = out[a].at[2 * px + py, c]
            pltpu.make_async_remote_copy(
                src_ref=blk, dst_ref=blk, send_sem=send_ici.at[a, p], recv_sem=recv_ici.at[a, p],
                device_id=sibling, device_id_type=MESH).wait_recv()
            fw = pltpu.make_async_remote_copy(
                src_ref=blk, dst_ref=blk, send_sem=send_d2d.at[a, p], recv_sem=recv_d2d.at[a, p],
                device_id=sibling, device_id_type=MESH)
            fw.start()
            sends.append(fw)
    for a in range(len(src)):
        for p, (px, py) in enumerate(chips):
            blk = out[a].at[2 * px + py, 1 - c]
            pltpu.make_async_remote_copy(
                src_ref=blk, dst_ref=blk, send_sem=send_d2d.at[a, p], recv_sem=recv_d2d.at[a, p],
                device_id=sibling, device_id_type=MESH).wait_recv()
    for cp in sends:
        cp.wait_send()


def _handshake(peers):
    barrier = pltpu.get_barrier_semaphore()
    for peer in peers:
        pl.semaphore_signal(barrier, inc=1, device_id=peer, device_id_type=MESH)
    pl.semaphore_wait(barrier, len(peers))


_SEQUENCER = dict(axis_name="sequencer", num_cores=1)
GATHER_LATE_ID, SCATTER_EARLY_ID, SWAP_EARLY_ID, GATHER_FIRST_ID, SCATTER_LATE_ID, SWAP_LATE_ID = 1, 2, 3, 4, 5, 6


def _all_gather_async(shards, after, name, collective_id):
    n, k = len(shards), len(after)

    def body(*refs):
        x, y, c, chips = _place()
        _handshake([(*chip, c) for chip in chips] + [(x, y, 1 - c)])
        _gather_body(refs[:n], refs[n + k:2 * n + k], *refs[2 * n + k:])

    return pl.kernel(
        body, out_type=[SDS((N_SHARD,) + t.shape, t.dtype) for t in shards],
        mesh=plsc.ScalarSubcoreMesh(**_SEQUENCER), scratch_types=[pltpu.SemaphoreType.DMA((n, 3))] * 4,
        compiler_params=pltpu.CompilerParams(collective_id=collective_id), name=name)(*shards, *after)


def _pair_swap(grads, name, collective_id):
    n = len(grads)

    def body(*refs):
        src, out, send_sems, recv_sems = refs[:n], refs[n:2 * n], refs[2 * n], refs[2 * n + 1]
        x, y, c, _ = _place()
        _handshake([(x, y, 1 - c)])
        copies = [pltpu.make_async_remote_copy(
            src_ref=src[a].at[:, 1 - c], dst_ref=out[a], send_sem=send_sems.at[a], recv_sem=recv_sems.at[a],
            device_id=(x, y, 1 - c), device_id_type=MESH) for a in range(n)]
        for cp in copies:
            cp.start()
        for cp in copies:
            cp.wait()

    return pl.kernel(
        body, out_type=[SDS((N_SHARD,) + t.shape[2:], t.dtype) for t in grads],
        mesh=plsc.ScalarSubcoreMesh(**_SEQUENCER), scratch_types=[pltpu.SemaphoreType.DMA((n,))] * 2,
        compiler_params=pltpu.CompilerParams(collective_id=collective_id), name=name)(*grads)


def _scatter_early(parts):
    n = len(parts)

    def body(*refs):
        part, recv, send_sems, recv_sems = refs[:n], refs[n:2 * n], refs[2 * n], refs[2 * n + 1]
        x, y, c, chips = _place()
        _handshake([(*chip, c) for chip in chips])
        me_j = 2 * x + y
        sends = []
        for a in range(n):
            for p, (px, py) in enumerate(chips):
                cp = pltpu.make_async_remote_copy(
                    src_ref=part[a].at[2 * px + py], dst_ref=recv[a].at[me_j], send_sem=send_sems.at[a, p],
                    recv_sem=recv_sems.at[a, p], device_id=(px, py, c), device_id_type=MESH)
                cp.start()
                sends.append(cp)
        for a in range(n):
            for p, (px, py) in enumerate(chips):
                slot = recv[a].at[2 * px + py]
                pltpu.make_async_remote_copy(
                    src_ref=slot, dst_ref=slot, send_sem=send_sems.at[a, p], recv_sem=recv_sems.at[a, p],
                    device_id=(px, py, c), device_id_type=MESH).wait_recv()
        for cp in sends:
            cp.wait_send()

    return pl.kernel(
        body, out_type=[SDS(t.shape, t.dtype) for t in parts],
        mesh=plsc.ScalarSubcoreMesh(**_SEQUENCER), scratch_types=[pltpu.SemaphoreType.DMA((n, 3))] * 2,
        compiler_params=pltpu.CompilerParams(collective_id=SCATTER_EARLY_ID), name="scatter_early")(*parts)


def _pair_sum(grads, other, name):
    _, _, rows, cols = grads.shape
    tr = _row_tile(rows)
    core = lax.axis_index("c").astype(jnp.int32).reshape(1)

    def body(c_ref, g_ref, o_ref, out_ref):
        out_ref[...] = (g_ref[...].astype(f32) + o_ref[...].astype(f32)).astype(bf16)

    grid_spec = pltpu.PrefetchScalarGridSpec(
        num_scalar_prefetch=1, grid=(N_SHARD, rows // tr),
        in_specs=[pl.BlockSpec((None, None, tr, cols), lambda j, i, c_ref: (j, c_ref[0], i, 0)),
                  pl.BlockSpec((None, tr, cols), lambda j, i, c_ref: (j, i, 0))],
        out_specs=pl.BlockSpec((None, tr, cols), lambda j, i, c_ref: (j, i, 0)))
    return pl.pallas_call(
        body, grid_spec=grid_spec, out_shape=SDS((N_SHARD, rows, cols), bf16), name=name,
        compiler_params=_params(("parallel", "parallel"), 10 * tr * cols, 12 * tr * cols))(core, grads, other)


def _scatter_partials(parts, small):
    n = len(parts)

    def body(*refs):
        part, small_ref, recv, small_all_ref = refs[:n], refs[n], refs[n + 1:2 * n + 1], refs[2 * n + 1]
        send_sems, recv_sems, ssend, srecv, local_sem = refs[2 * n + 2:]
        x, y, c, chips = _place()
        flip = lambda a, bit: 1 - a if bit else a
        peers = [(flip(x, k & 4), flip(y, k & 2), flip(c, k & 1)) for k in range(1, 8)]
        _handshake(peers)
        me_j = 2 * x + y
        me_dev = 4 * x + 2 * y + c
        own = pltpu.make_async_copy(small_ref, small_all_ref.at[me_dev], local_sem)
        own.start()
        sends = []
        for a in range(n):
            for p, (px, py) in enumerate(chips):
                cp = pltpu.make_async_remote_copy(
                    src_ref=part[a].at[2 * px + py], dst_ref=recv[a].at[me_j], send_sem=send_sems.at[a, p],
                    recv_sem=recv_sems.at[a, p], device_id=(px, py, c), device_id_type=MESH)
                cp.start()
                sends.append(cp)
        for k, to in enumerate(peers):
            cp = pltpu.make_async_remote_copy(
                src_ref=small_ref, dst_ref=small_all_ref.at[me_dev],
                send_sem=ssend.at[k], recv_sem=srecv.at[k], device_id=to, device_id_type=MESH)
            cp.start()
            sends.append(cp)
        for a in range(n):
            for p, (px, py) in enumerate(chips):
                slot = recv[a].at[2 * px + py]
                pltpu.make_async_remote_copy(
                    src_ref=slot, dst_ref=slot, send_sem=send_sems.at[a, p], recv_sem=recv_sems.at[a, p],
                    device_id=(px, py, c), device_id_type=MESH).wait_recv()
        for k, (px, py, pc) in enumerate(peers):
            slot = small_all_ref.at[4 * px + 2 * py + pc]
            pltpu.make_async_remote_copy(
                src_ref=slot, dst_ref=slot, send_sem=ssend.at[k], recv_sem=srecv.at[k],
                device_id=(px, py, pc), device_id_type=MESH).wait_recv()
        for cp in sends:
            cp.wait_send()
        own.wait()

    return pl.kernel(
        body, out_type=[SDS(t.shape, t.dtype) for t in parts] + [SDS((8, SMALL_ROWS, D), f32)],
        mesh=plsc.ScalarSubcoreMesh(**_SEQUENCER),
        scratch_types=[pltpu.SemaphoreType.DMA((n, 3)), pltpu.SemaphoreType.DMA((n, 3)),
                       pltpu.SemaphoreType.DMA((7,)), pltpu.SemaphoreType.DMA((7,)), pltpu.SemaphoreType.DMA],
        compiler_params=pltpu.CompilerParams(collective_id=SCATTER_LATE_ID), name="scatter_partials")(*parts, small)


def _sum_partials(part, recv, name):
    _, rows, cols = recv.shape
    tr = _row_tile(rows)
    me = (2 * lax.axis_index("x") + lax.axis_index("y")).astype(jnp.int32).reshape(1)

    def body(me_ref, mine, r0, r1, r2, r3, out_ref):
        acc = None
        for j, r in enumerate((r0, r1, r2, r3)):
            term = jnp.where(me_ref[0] == j, mine[...], r[...]).astype(f32)
            acc = term if acc is None else acc + term
        out_ref[...] = acc

    slot = lambda j: pl.BlockSpec((None, tr, cols), lambda i, me_ref: (jnp.where(me_ref[0] == j, j ^ 1, j), i, 0))
    grid_spec = pltpu.PrefetchScalarGridSpec(
        num_scalar_prefetch=1, grid=(rows // tr,),
        in_specs=[pl.BlockSpec((None, tr, cols), lambda i, me_ref: (me_ref[0], i, 0)), slot(0), slot(1), slot(2), slot(3)],
        out_specs=pl.BlockSpec((tr, cols), lambda i, me_ref: (i, 0)))
    return pl.pallas_call(
        body, grid_spec=grid_spec, out_shape=SDS((rows, cols), f32), name=name,
        compiler_params=_params(("parallel",), 14 * tr * cols, 12 * tr * cols))(me, part, recv, recv, recv, recv)


def _sum_small(small_all):
    def body(small_ref, out_ref):
        tot = small_ref[0]
        for k in range(1, 8):
            tot = tot + small_ref[k]
        out_ref[...] = tot

    return pl.pallas_call(
        body, grid=(1,), in_specs=[pl.BlockSpec((8, SMALL_ROWS, D), lambda i: (0, 0, 0))],
        out_specs=pl.BlockSpec((SMALL_ROWS, D), lambda i: (0, 0)), out_shape=SDS((SMALL_ROWS, D), f32),
        name="sum_small", compiler_params=_params(("arbitrary",), 36 * SMALL_ROWS * D))(small_all)


def _swap_halves(halves, name):
    n = len(halves)

    def body(*refs):
        src, out, send_sems, recv_sems = refs[:n], refs[n:2 * n], refs[2 * n], refs[2 * n + 1]
        x, y, c, _ = _place()
        copies = [pltpu.make_async_remote_copy(
            src_ref=src[a], dst_ref=out[a], send_sem=send_sems.at[a], recv_sem=recv_sems.at[a],
            device_id=(x, y, 1 - c), device_id_type=MESH) for a in range(n)]
        for cp in copies:
            cp.start()
        for cp in copies:
            cp.wait()

    return pl.pallas_call(
        body, in_specs=[_ANY] * n, out_specs=[_ANY] * n, out_shape=[SDS(t.shape, f32) for t in halves],
        scratch_shapes=[pltpu.SemaphoreType.DMA((n,))] * 2, name=name,
        compiler_params=pltpu.CompilerParams(has_side_effects=True))(*halves)


def _kernel_layout(name, t):
    t = t[0]
    if name in TRANSPOSED:
        t = jnp.swapaxes(t, 0, 1)
    return _pad_rows(t, SHARD_SHAPE[name][0])


def _harness_layout(name, t):
    if name == "w_in":
        t = t[:IN_SHARD]
    if name in TRANSPOSED:
        t = jnp.swapaxes(t, 0, 1)
    return t[None]


def _pad_rows(t, rows):
    return t if t.shape[0] == rows else jnp.pad(t, ((0, rows - t.shape[0]), (0, 0)))


_QA, _KA, _VA, _QB, _F, _GAB = 0, 768, 1536, 2304, 3840, 3848


def _spans(a, b):
    return [(j, max(a, j * IN_SHARD) - j * IN_SHARD, max(a, j * IN_SHARD) - a,
             min(b, (j + 1) * IN_SHARD) - max(a, j * IN_SHARD))
            for j in range(N_SHARD) if max(a, j * IN_SHARD) < min(b, (j + 1) * IN_SHARD)]


_LANES = pl.BlockSpec((N_SHARD, IN_SHARD_PAD, 128), lambda c: (0, 0, c))


def _split_w_in(shards):
    group = [[(o + g * DIL_W, o + (g + 1) * DIL_W) for o in (_QA, _KA, _VA)] for g in range(3)]
    fox = [[(_QB + k * FOX_W, _QB + (k + 1) * FOX_W)] for k in range(3)]
    wanted = group + fox + [[(_QB, _F)], [(_F, _GAB)], [(_GAB, IN_COLS)]]
    rows = [sum(b - a for a, b in w) for w in wanted]
    rows[7] = 128

    def body(s_ref, *o_refs):
        for o_ref, want in zip(o_refs, wanted):
            at = 0
            for a, b in want:
                for j, src, off, n in _spans(a, b):
                    o_ref[at + off:at + off + n, :] = s_ref[j, src:src + n, :]
                at += b - a
        o_refs[7][N_FOX:, :] = jnp.zeros((128 - N_FOX, 128), bf16)

    return pl.pallas_call(
        body, grid=(D // 128,), in_specs=[_LANES], out_specs=[pl.BlockSpec((r, 128), lambda c: (0, c)) for r in rows],
        out_shape=[SDS((r, D), bf16) for r in rows], name="split_w_in",
        compiler_params=_params(("parallel",), 2 * 128 * (N_SHARD * IN_SHARD_PAD + sum(rows))))(shards)


def _join_w_in(g_a, g_fox, g_f, g_gab):
    parts = [(g_a[k], o, o + DIL_W) for o in (0, DIL_W, 2 * DIL_W) for k in range(3)]
    parts += [(t, 0, FOX_W) for t in g_fox] + [(g_f, 0, N_FOX), (g_gab, 0, 2 * D)]
    arrays = list(g_a) + list(g_fox) + [g_f, g_gab]
    index = {id(t): i for i, t in enumerate(arrays)}

    def body(*refs):
        o_ref = refs[-1]
        o_ref[:, IN_SHARD:, :] = jnp.zeros((N_SHARD, IN_SHARD_PAD - IN_SHARD, 128), bf16)
        at = 0
        for t, lo, hi in parts:
            src_ref = refs[index[id(t)]]
            for j, dst, off, n in _spans(at, at + hi - lo):
                o_ref[j, dst:dst + n, :] = src_ref[lo + off:lo + off + n, :].astype(bf16)
            at += hi - lo

    return pl.pallas_call(
        body, grid=(D // 128,), in_specs=[pl.BlockSpec((t.shape[0], 128), lambda c: (0, c)) for t in arrays],
        out_specs=_LANES, out_shape=SDS((N_SHARD, IN_SHARD_PAD, D), bf16), name="join_w_in",
        compiler_params=_params(("parallel",), 2 * 128 * (N_SHARD * IN_SHARD_PAD + sum(t.shape[0] for t in arrays))),
    )(*arrays)


def _full_weights(gathered):
    full = {n: t.reshape((N_SHARD,) + SHARD_SHAPE[n]) for n, t in gathered.items()}
    out = {}
    if "w_in" in full:
        pieces = _split_w_in(full["w_in"])
        out.update(w_a_t=pieces[0:3], w_fox_t=pieces[3:6], w_vr_t=pieces[6], w_f_t=pieces[7], w_gab_t=pieces[8])
    if "w_out" in full:
        out.update(
            w_a4=full["w_proj_a"],
            w_b4=full["w_proj_b"],
            w_out=full["w_out"].reshape(D, D),
            w_gate_t=full["w_ffn_gate"].reshape(F_FF, D),
            w_up_t=full["w_ffn_up"].reshape(F_FF, D),
            w_down=full["w_ffn_down"].reshape(F_FF, D))
    return out


def _sharded_grads(g):
    full = dict(w_in=_join_w_in(g["w_a_t"], g["w_fox_t"], g["w_f_t"], g["w_gab_t"]), w_proj_a=g["w_a4"],
                w_proj_b=g["w_b4"], w_out=g["w_out"], w_ffn_gate=g["w_gate_t"], w_ffn_up=g["w_up_t"],
                w_ffn_down=g["w_down"])
    return {n: _halved(full[n].reshape((N_SHARD,) + SHARD_SHAPE[n])) for n in W_NAMES}


def _local_step(x, target, wt, b_forget, g_mix_pre, g_mix_post, g_ffn_pre, g_ffn_post, late=None):
    tables = _rope_tables()
    b128 = jnp.pad(b_forget, ((0, 0), (0, 128 - N_FOX)))
    dils = tuple(d for _, d in DIL_GROUPS[1:])

    hs = _norm_fwd(x, g_mix_pre, dils)
    h1 = hs[0]
    if callable(wt):
        wt = wt(h1)
    qkv = [_proj_rope(g, hs[g], wt["w_a_t"][g], tables) for g in range(3)]
    vr = _mm([(h1, wt["w_vr_t"])], "nt", bf16, tm=1024, tn=VR_W // 2, name="proj_vr")
    gab = _mm([(h1, wt["w_gab_t"])], "nt", f32, tm=512, tn=2 * D, name="proj_gab")
    fz = _mm([(h1, wt["w_f_t"])], "nt", f32, tm=1024, tn=128, name="proj_f")
    dil = [_dil_fwd(g, qkv[g]) for g in range(3)]
    out_a, lse_a = _dil_combine([o for o, _ in dil], [l for _, l in dil])
    f_q, f_k = _forget_fwd(fz, b128)
    out_b, lse_b = _fox_fwd(vr, f_q, f_k)
    if late is not None:
        wt = {**wt, **late(out_b)}
    ya, yb, merged = _merge_fwd(out_a, out_b, wt["w_a4"], wt["w_b4"], gab)
    mix, x2, h3 = _resid_norm_fwd(x, merged, wt["w_out"], g_mix_post, g_ffn_pre)
    g_act, u_act, a_act = _ffn_fwd(h3, wt["w_gate_t"], wt["w_up_t"])
    sq_err, dy, d_ff, dg_ffn_post = _loss_head(x2, a_act, wt["w_down"], g_ffn_post, target)

    grads = {}
    d_g, d_u = _ffn_bwd_act(d_ff, wt["w_down"], g_act, u_act)
    grads["w_down"] = _mm([(a_act, d_ff)], "tn", bf16, tm=FF_TN, tn=512, name="grad_w_down")
    grads["w_gate_t"] = _mm([(d_g, h3)], "tn", bf16, tm=FF_TN, tn=512, name="grad_w_gate")
    grads["w_up_t"] = _mm([(d_u, h3)], "tn", bf16, tm=FF_TN, tn=512, name="grad_w_up")
    dx2, d_mix, dg_ffn_pre, dg_mix_post = _norm_bwd_mid(dy, d_g, d_u, wt["w_gate_t"], wt["w_up_t"], x2, mix,
                                                        g_ffn_pre, g_mix_post)

    grads["w_out"] = _mm([(merged, d_mix)], "tn", bf16, tm=D, tn=D, name="grad_w_out")
    d_ya, d_yb, d_gab = _merge_bwd(d_mix, wt["w_out"], ya, yb, gab)
    grads["w_a4"], grads["w_b4"] = _branch_grads(out_a, out_b, d_ya, d_yb)
    d_out_a, delta_a, d_out_b, delta_b = _branch_bwd(d_ya, d_yb, wt["w_a4"], wt["w_b4"], out_a, out_b)

    perm = _perm_rows([d_out_a, delta_a, lse_a], dils, "perm_dil_bwd")
    aux = [(d_out_a, delta_a, lse_a)] + [tuple(perm[k * len(dils) + i] for k in range(3)) for i in range(len(dils))]
    d_qkv = []
    for g in range(3):
        dq, dk, dv = _dil_bwd(g, qkv[g], *aux[g])
        d_qkv.append(_rope_bwd(g, dq, dk, dv, tables))
    *d_fox, d_f_cols, d_f_rows = _fox_bwd(vr, f_q, f_k, lse_b, d_out_b, delta_b)
    d_z, d_b128 = _forget_bwd(fz, b128, d_f_cols, d_f_rows)

    grads["w_a_t"] = [_mm([(d_qkv[g], hs[g])], "tn", bf16, tm=QKV_W, tn=D, name=f"grad_w_a_{g}") for g in range(3)]
    grads["w_fox_t"] = [_mm([(d_fox[k], h1)], "tn", bf16, tm=FOX_W, tn=D, name=f"grad_w_fox_{k}") for k in range(3)]
    grads["w_gab_t"] = _mm([(d_gab, h1)], "tn", bf16, tm=D, tn=D, name="grad_w_gab")
    grads["w_f_t"] = _mm([(d_z, h1)], "tn", bf16, tm=128, tn=D, name="grad_w_f")
    d_h1_nat = _mm([(d_qkv[0], wt["w_a_t"][0])] + list(zip(d_fox, wt["w_fox_t"]))
                   + [(d_gab, wt["w_gab_t"]), (d_z, wt["w_f_t"])], "nn", f32, tm=512, tn=512, name="proj_in_bwd")
    d_h1_dil = [_mm([(d_qkv[g], wt["w_a_t"][g])], "nn", f32, tm=1024, tn=D, name=f"proj_a_bwd_{g}") for g in (1, 2)]
    d_h1 = _unperm_sum(d_h1_nat, d_h1_dil, dils, "unperm_d_h1")
    grad_x, dg_mix_pre = _norm_bwd_in(dx2, d_h1, x, g_mix_pre)

    small = dict(b_forget=d_b128[:, :N_FOX], norm_mix_pre=dg_mix_pre, norm_mix_post=dg_mix_post,
                 norm_ffn_pre=dg_ffn_pre, norm_ffn_post=dg_ffn_post)
    grads["mid_backward"], grads["late_backward"] = d_qkv[0], d_h1_nat
    return sq_err, grad_x, grads, small


NORMS = ("norm_mix_pre", "norm_mix_post", "norm_ffn_pre", "norm_ffn_post")
ORDER = ("w_in", "w_proj_a", "w_proj_b", "w_out", "b_forget", "w_ffn_gate", "w_ffn_up", "w_ffn_down") + NORMS


def kernel(x, w_in, w_proj_a, w_proj_b, w_out, b_forget, w_ffn_gate, w_ffn_up, w_ffn_down, norm_mix_pre, norm_mix_post, norm_ffn_pre, norm_ffn_post, loss_target, m_w_in, m_w_proj_a, m_w_proj_b, m_w_out, m_b_forget, m_w_ffn_gate, m_w_ffn_up, m_w_ffn_down, m_norm_mix_pre, m_norm_mix_post, m_norm_ffn_pre, m_norm_ffn_post, v_w_in, v_w_proj_a, v_w_proj_b, v_w_out, v_b_forget, v_w_ffn_gate, v_w_ffn_up, v_w_ffn_down, v_norm_mix_pre, v_norm_mix_post, v_norm_ffn_pre, v_norm_ffn_post):
    given = dict(w_in=w_in, w_proj_a=w_proj_a, w_proj_b=w_proj_b, w_out=w_out, w_ffn_gate=w_ffn_gate,
                 w_ffn_up=w_ffn_up, w_ffn_down=w_ffn_down)
    given_m = dict(w_in=m_w_in, w_proj_a=m_w_proj_a, w_proj_b=m_w_proj_b, w_out=m_w_out, w_ffn_gate=m_w_ffn_gate,
                   w_ffn_up=m_w_ffn_up, w_ffn_down=m_w_ffn_down)
    given_v = dict(w_in=v_w_in, w_proj_a=v_w_proj_a, w_proj_b=v_w_proj_b, w_out=v_w_out, w_ffn_gate=v_w_ffn_gate,
                   w_ffn_up=v_w_ffn_up, w_ffn_down=v_w_ffn_down)
    w, m, v = ({n: _kernel_layout(n, t[n]) for n in W_NAMES} for t in (given, given_m, given_v))
    small_w = dict(b_forget=b_forget, norm_mix_pre=norm_mix_pre, norm_mix_post=norm_mix_post,
                   norm_ffn_pre=norm_ffn_pre, norm_ffn_post=norm_ffn_post)
    small_m = dict(b_forget=m_b_forget, norm_mix_pre=m_norm_mix_pre, norm_mix_post=m_norm_mix_post,
                   norm_ffn_pre=m_norm_ffn_pre, norm_ffn_post=m_norm_ffn_post)
    small_v = dict(b_forget=v_b_forget, norm_mix_pre=v_norm_mix_pre, norm_mix_post=v_norm_mix_post,
                   norm_ffn_pre=v_norm_ffn_pre, norm_ffn_post=v_norm_ffn_post)

    own = [_halved(w[n].astype(bf16)) for n in W_NAMES]
    chip = 2 * lax.axis_index("x") + lax.axis_index("y")
    exchanged = {"first": _all_gather_async(own[:1], [], "all_gather_first", GATHER_FIRST_ID)}
    fill = lambda ts, mine: [lax.dynamic_update_index_in_dim(t, o, chip, 0) for t, o in zip(ts, mine)]

    def first_weights(ready):
        arrived, _ = lax.optimization_barrier((list(exchanged["first"]), ready))
        exchanged["late"] = _all_gather_async(own[1:], [arrived[0][0, 0, :16, :128]], "all_gather_late", GATHER_LATE_ID)
        return _full_weights(dict(zip(W_NAMES[:1], fill(arrived, own[:1]))))

    def late_weights(ready):
        arrived, _ = lax.optimization_barrier((list(exchanged["late"]), ready))
        return _full_weights(dict(zip(W_NAMES[1:], fill(arrived, own[1:]))))

    sq_err, grad_x, grads, small = _local_step(x[0], loss_target[0], first_weights, b_forget, norm_mix_pre,
                                               norm_mix_post, norm_ffn_pre, norm_ffn_post, late=late_weights)

    g4 = _sharded_grads(grads)
    stack = lambda t, extra: jnp.concatenate(
        [jnp.pad(t["b_forget"], ((0, 0), (0, D - N_FOX)))] + [t[n] for n in NORMS]
        + [jnp.pad(extra, ((0, SMALL_ROWS - LOSS_ROW - 1), (0, D - extra.shape[1])), constant_values=1.0)], axis=0)
    early, _ = lax.optimization_barrier(
        (list(_pair_swap([g4[n] for n in W_NAMES[1:]], "pair_swap_early", SWAP_EARLY_ID)), grads["mid_backward"]))
    last, _ = lax.optimization_barrier(
        (list(_pair_swap([g4["w_in"]], "pair_swap_late", SWAP_LATE_ID)), grads["late_backward"]))
    other = last + early
    parts = [_pair_sum(g4[n], o, "pair_sum_" + n) for n, o in zip(W_NAMES, other)]
    recv_early = _scatter_early(parts[1:])
    recv_in, small_all = _scatter_partials(parts[:1], stack(small, sq_err))

    g_shard, delta, new_m, new_v = {}, {}, {}, {}

    def finish(names, parts, recv):
        halves = [_sum_partials(p, r, "sum_partials_" + n) for n, p, r in zip(names, parts, recv)]
        theirs = _swap_halves(halves, "swap_halves_" + names[0])
        for n, mine, other_half in zip(names, halves, theirs):
            g_shard[n], delta[n], new_m[n], new_v[n] = _adamw_halves(w[n], mine, other_half, m[n], v[n], "adamw_" + n)

    recv_early, _ = lax.optimization_barrier((list(recv_early), parts[0]))
    finish(W_NAMES[1:], parts[1:], recv_early)
    (recv_in, small_all), _ = lax.optimization_barrier(((recv_in, small_all), [delta[n] for n in W_NAMES[1:]]))
    finish(W_NAMES[:1], parts[:1], [recv_in])
    small_sum = _sum_small(small_all)
    loss = small_sum[LOSS_ROW, 0] * (0.5 / D)
    ones = jnp.ones((1, 128), f32)
    sd, sm, sv = _adamw(stack(small_w, ones), small_sum, stack(small_m, ones), stack(small_v, ones), "adamw_small")

    outs = [loss, grad_x[None]]
    for big, st in ((g_shard, small_sum), (delta, sd), (new_m, sm), (new_v, sv)):
        t = {n: _harness_layout(n, big[n]) for n in W_NAMES}
        t["b_forget"] = st[0:1, :N_FOX]
        for i, n in enumerate(NORMS):
            t[n] = st[i + 1:i + 2]
        outs += [t[n] for n in ORDER]
    return tuple(outs)
```

```python
import functools
import math

import jax
import jax.numpy as jnp
import numpy as np
from jax import lax
from jax.experimental import pallas as pl
from jax.experimental.pallas import tpu as pltpu
from jax.experimental.pallas import tpu_sc as plsc

f32 = jnp.float32
bf16 = jnp.bfloat16
SDS = jax.ShapeDtypeStruct
MESH = pl.DeviceIdType.MESH

S = 2048
D = 1024
HD = 64
BLK = 128
N_FOX = 8
FOX_W = N_FOX * HD
DIL_GROUPS = ((128, 1), (512, 4), (2048, 16))
SLOTS = 4
DIL_W = SLOTS * HD
QKV_W = 3 * DIL_W
VR_W = 3 * FOX_W
GF_W = 2 * D + 128
F_FF = 2816
ROPE_DIM = 16
ROPE_THETA = 500000.0
EPS = 1e-6
NEG = -1e30
SCALE = 1.0 / math.sqrt(HD)
IN_COLS = 5896
N_SHARD = 4

ADAM_LR, ADAM_B1, ADAM_B2, ADAM_EPS, ADAM_WD, ADAM_STEP = 0.001, 0.9, 0.999, 1e-08, 0.01, 10

VMEM_V7X = 64 * 1024 * 1024
VMEM_PLAN_MAX = VMEM_V7X - 8 * 1024 * 1024

TM = 256
TQ = 256

W_NAMES = ("w_in", "w_proj_a", "w_proj_b", "w_out", "w_ffn_gate", "w_ffn_up", "w_ffn_down")
TRANSPOSED = ("w_in", "w_ffn_gate", "w_ffn_up")
IN_SHARD = IN_COLS // N_SHARD
IN_SHARD_PAD = 1504
SHARD_SHAPE = dict(w_in=(IN_SHARD_PAD, D), w_proj_a=(DIL_W, D // N_SHARD), w_proj_b=(FOX_W, D // N_SHARD),
                   w_out=(D // N_SHARD, D), w_ffn_gate=(F_FF // N_SHARD, D), w_ffn_up=(F_FF // N_SHARD, D),
                   w_ffn_down=(F_FF // N_SHARD, D))
SMALL_ROWS = 8
LOSS_ROW = 5


def _nbytes(shape, dtype):
    return math.prod(shape) * jnp.dtype(dtype).itemsize


def _params(semantics, block_bytes, temp_bytes=0):
    need = 2 * block_bytes + temp_bytes + (2 << 20)
    return pltpu.CompilerParams(dimension_semantics=semantics, vmem_limit_bytes=int(min(need, VMEM_PLAN_MAX)))


def _row(w, tm=TM):
    return pl.BlockSpec((tm, w), lambda i: (i, 0))


def _vec(w):
    return pl.BlockSpec((1, w), lambda i: (0, 0))


def _mm(pairs, dims, out_dtype, *, tm, tn, name, m_inner=False):
    a0, b0 = pairs[0]
    m_dim = a0.shape[1] if dims == "tn" else a0.shape[0]
    n_dim = b0.shape[0] if dims == "nt" else b0.shape[1]
    contract = {"nn": ((1,), (0,)), "nt": ((1,), (1,)), "tn": ((0,), (0,))}[dims]
    n_pairs = len(pairs)
    assert m_dim % tm == 0 and n_dim % tn == 0, (name, m_dim, n_dim, tm, tn)

    def body(*refs):
        o_ref = refs[-1]
        acc = None
        for p in range(n_pairs):
            a = refs[2 * p][...].astype(bf16)
            b = refs[2 * p + 1][...].astype(bf16)
            t = lax.dot_general(a, b, (contract, ((), ())), preferred_element_type=f32)
            acc = t if acc is None else acc + t
        o_ref[...] = acc.astype(o_ref.dtype)

    if m_inner:
        grid = (n_dim // tn, m_dim // tm)
        mi = lambda j, i: i
        ni = lambda j, i: j
    else:
        grid = (m_dim // tm, n_dim // tn)
        mi = lambda i, j: i
        ni = lambda i, j: j
    in_specs, block_bytes, args = [], 0, []
    for a, b in pairs:
        k_dim = a.shape[0] if dims == "tn" else a.shape[1]
        if dims == "tn":
            in_specs.append(pl.BlockSpec((k_dim, tm), lambda *g: (0, mi(*g))))
        else:
            in_specs.append(pl.BlockSpec((tm, k_dim), lambda *g: (mi(*g), 0)))
        if dims == "nt":
            in_specs.append(pl.BlockSpec((tn, k_dim), lambda *g: (ni(*g), 0)))
        else:
            in_specs.append(pl.BlockSpec((k_dim, tn), lambda *g: (0, ni(*g))))
        block_bytes += _nbytes((tm, k_dim), a.dtype) + _nbytes((tn, k_dim), b.dtype)
        args += [a, b]
    block_bytes += _nbytes((tm, tn), out_dtype)
    temp = _nbytes((tm, tn), f32) * 2 + sum(_nbytes((tm, a.shape[0] if dims == "tn" else a.shape[1]), bf16)
                                            + _nbytes((tn, a.shape[0] if dims == "tn" else a.shape[1]), bf16)
                                            for a, _ in pairs)
    return pl.pallas_call(
        body, grid=grid, in_specs=in_specs,
        out_specs=pl.BlockSpec((tm, tn), lambda *g: (mi(*g), ni(*g))),
        out_shape=SDS((m_dim, n_dim), out_dtype), name=name,
        compiler_params=_params(("parallel", "parallel"), block_bytes, temp),
    )(*args)


def _rms(x, g):
    r = lax.rsqrt(jnp.mean(x * x, axis=-1, keepdims=True) + EPS)
    return x * r * g


def _rms_bwd(x, g, dy):
    r = lax.rsqrt(jnp.mean(x * x, axis=-1, keepdims=True) + EPS)
    xh = x * r
    dxh = dy * g
    dx = r * (dxh - xh * jnp.mean(dxh * xh, axis=-1, keepdims=True))
    return dx, jnp.sum(dy * xh, axis=0, keepdims=True)


def _acc_rows(ref, val):
    @pl.when(pl.program_id(0) == 0)
    def _():
        ref[...] = jnp.zeros_like(ref)
    ref[...] += val


def _norm_fwd(xs, g):
    n = len(xs)

    def body(*refs):
        g = refs[n][...]
        for x_ref, h_ref in zip(refs[:n], refs[n + 1:]):
            h_ref[...] = _rms(x_ref[...], g).astype(bf16)

    return pl.pallas_call(
        body, grid=(S // TM,), in_specs=[_row(D)] * n + [_vec(D)], out_specs=[_row(D)] * n,
        out_shape=[SDS((S, D), bf16)] * n, name="norm_mix_pre",
        compiler_params=_params(("parallel",), 6 * n * TM * D, 8 * n * TM * D))(*xs, g)


def _perm_rows(xs, ds, name):
    n = len(xs)

    def body(*refs):
        outs = iter(refs[n:])
        for x_ref in refs[:n]:
            for d in ds:
                o_ref, rows = next(outs), S // d
                for r in range(d):
                    o_ref[r * rows:(r + 1) * rows, :] = x_ref[pl.ds(r, rows, stride=d), :]

    blk = pl.BlockSpec((S, 128), lambda c: (0, c))
    w = xs[0].shape[1]
    return pl.pallas_call(
        body, grid=(w // 128,), in_specs=[blk] * n, out_specs=[blk] * (n * len(ds)),
        out_shape=[SDS((S, w), f32)] * (n * len(ds)), name=name,
        compiler_params=_params(("parallel",), 4 * S * 128 * n * (1 + len(ds))))(*xs)


def _unperm_sum(nat, perms, ds, name):
    n = len(perms)

    def body(*refs):
        a_ref, o_ref, sc = refs[0], refs[n + 1], refs[n + 2]
        acc = a_ref[...]
        for b_ref, d in zip(refs[1:n + 1], ds):
            rows = S // d
            for r in range(d):
                sc[pl.ds(r, rows, stride=d), :] = b_ref[r * rows:(r + 1) * rows, :]
            acc = acc + sc[...]
        o_ref[...] = acc

    blk = pl.BlockSpec((S, 128), lambda c: (0, c))
    w = nat.shape[1]
    return pl.pallas_call(
        body, grid=(w // 128,), in_specs=[blk] * (n + 1), out_specs=blk, out_shape=SDS((S, w), f32),
        scratch_shapes=[pltpu.VMEM((S, 128), f32)], name=name,
        compiler_params=_params(("parallel",), 4 * S * 128 * (n + 2), 8 * S * 128))(nat, *perms)


def _whole(a):
    return pl.BlockSpec(a.shape, lambda i: (0,) * a.ndim)


def _resid_norm_fwd(x, merged, w_out, g_post, g_pre):
    def body(x_ref, mg_ref, w_ref, gp_ref, gn_ref, mix_ref, x2_ref, h_ref):
        mix = jnp.dot(mg_ref[...], w_ref[...], preferred_element_type=f32)
        x2 = x_ref[...] + _rms(mix, gp_ref[...])
        mix_ref[...] = mix
        x2_ref[...] = x2
        h_ref[...] = _rms(x2, gn_ref[...]).astype(bf16)

    return pl.pallas_call(
        body, grid=(S // TM,), in_specs=[_row(D), _row(D), _whole(w_out), _vec(D), _vec(D)], out_specs=[_row(D)] * 3,
        out_shape=[SDS((S, D), f32), SDS((S, D), f32), SDS((S, D), bf16)], name="proj_out_norm",
        compiler_params=_params(("parallel",), 16 * TM * D + 2 * D * D, 16 * TM * D))(x, merged, w_out, g_post, g_pre)


def _loss_head(x2, a_act, w_down, g_post, target):
    def body(x2_ref, a_ref, w_ref, g_ref, t_ref, loss_ref, dy_ref, dff_ref, dg_ref):
        ff = jnp.dot(a_ref[...], w_ref[...], preferred_element_type=f32)
        g = g_ref[...]
        err = x2_ref[...] + _rms(ff, g) - t_ref[...]
        dy = err * (1.0 / D)
        dff, dg = _rms_bwd(ff, g, dy)
        dy_ref[...] = dy
        dff_ref[...] = dff.astype(bf16)
        _acc_rows(dg_ref, dg)
        _acc_rows(loss_ref, jnp.full((1, 128), jnp.sum(err * err), f32))

    return pl.pallas_call(
        body, grid=(S // TM,), in_specs=[_row(D), _row(F_FF), _whole(w_down), _vec(D), _row(D)],
        out_specs=[_vec(128), _row(D), _row(D), _vec(D)],
        out_shape=[SDS((1, 128), f32), SDS((S, D), f32), SDS((S, D), bf16), SDS((1, D), f32)], name="ffn_down_loss",
        compiler_params=_params(("arbitrary",), 14 * TM * D + 2 * TM * F_FF + 2 * F_FF * D, 28 * TM * D),
    )(x2, a_act, w_down, g_post, target)


def _norm_bwd_mid(dy, d_g, d_u, w_gate_t, w_up_t, x2, mix, g_ffn_pre, g_mix_post):
    def body(dy_ref, dgt_ref, dut_ref, wg_ref, wu_ref, x2_ref, mix_ref, g3_ref, g2_ref, dx2_ref, dmix_ref, dg3_ref, dg2_ref):
        dh = jnp.dot(dgt_ref[...], wg_ref[...], preferred_element_type=f32)
        dh += jnp.dot(dut_ref[...], wu_ref[...], preferred_element_type=f32)
        d3, dg3 = _rms_bwd(x2_ref[...], g3_ref[...], dh)
        dx2 = dy_ref[...] + d3
        dmix, dg2 = _rms_bwd(mix_ref[...], g2_ref[...], dx2)
        dx2_ref[...] = dx2
        dmix_ref[...] = dmix.astype(bf16)
        _acc_rows(dg3_ref, dg3)
        _acc_rows(dg2_ref, dg2)

    return pl.pallas_call(
        body, grid=(S // TM,),
        in_specs=[_row(D), _row(F_FF), _row(F_FF), _whole(w_gate_t), _whole(w_up_t), _row(D), _row(D), _vec(D), _vec(D)],
        out_specs=[_row(D), _row(D), _vec(D), _vec(D)],
        out_shape=[SDS((S, D), f32), SDS((S, D), bf16), SDS((1, D), f32), SDS((1, D), f32)], name="ffn_bwd_in_norm",
        compiler_params=_params(("arbitrary",), 18 * TM * D + 4 * TM * F_FF + 4 * F_FF * D, 28 * TM * D),
    )(dy, d_g, d_u, w_gate_t, w_up_t, x2, mix, g_ffn_pre, g_mix_post)


def _norm_bwd_in(dx2, dh1, x, g):
    def body(dx2_ref, dh_ref, x_ref, g_ref, gx_ref, dg_ref):
        d1, dg = _rms_bwd(x_ref[...], g_ref[...], dh_ref[...])
        gx_ref[...] = dx2_ref[...] + d1
        _acc_rows(dg_ref, dg)

    return pl.pallas_call(
        body, grid=(S // TM,), in_specs=[_row(D)] * 3 + [_vec(D)], out_specs=[_row(D), _vec(D)],
        out_shape=[SDS((S, D), f32), SDS((1, D), f32)], name="norm_bwd_in",
        compiler_params=_params(("arbitrary",), 16 * TM * D, 16 * TM * D))(dx2, dh1, x, g)


def _rope_tables():
    half = ROPE_DIM // 2
    inv_freq = np.power(np.float32(ROPE_THETA), -np.arange(0, ROPE_DIM, 2, dtype=np.float32) / np.float32(ROPE_DIM))
    row = np.arange(S)
    groups = []
    for _, d in DIL_GROUPS:
        pos = ((row % (S // d)) * d + row // (S // d)).astype(np.float32)
        ang = pos[:, None] * inv_freq[None, :].astype(np.float32)
        cos, sin = np.cos(ang).astype(np.float32), np.sin(ang).astype(np.float32)
        c = np.concatenate([cos, cos, np.ones((S, HD - ROPE_DIM), np.float32)], axis=1)
        s_lo = np.concatenate([-sin, np.zeros((S, HD - half), np.float32)], axis=1)
        s_hi = np.concatenate([np.zeros((S, half), np.float32), sin, np.zeros((S, HD - ROPE_DIM), np.float32)], axis=1)
        groups.append(np.stack([np.concatenate([t, t], axis=1) for t in (c, s_lo, s_hi)]))
    return jnp.asarray(np.stack(groups))


def _rotate(x, c, lo, hi, sign):
    tile = lambda t: jnp.tile(t, (1, DIL_W // 128))
    return (x * tile(c) + pltpu.roll(x, DIL_W - ROPE_DIM // 2, 1) * (tile(lo) * sign)
            + pltpu.roll(x, ROPE_DIM // 2, 1) * (tile(hi) * sign))


def _table_specs(g):
    return [pl.BlockSpec((None, None, TM, 128), lambda i, k=k: (g, k, i, 0)) for k in range(3)]


def _rope_fwd(g, p_qkv, tables):
    def body(x_ref, c_ref, lo_ref, hi_ref, o_ref):
        c, lo, hi = c_ref[...], lo_ref[...], hi_ref[...]
        for part in range(2):
            cols = slice(part * DIL_W, (part + 1) * DIL_W)
            o_ref[:, cols] = _rotate(x_ref[:, cols], c, lo, hi, 1.0).astype(bf16)
        o_ref[:, 2 * DIL_W:] = x_ref[:, 2 * DIL_W:].astype(bf16)

    return pl.pallas_call(
        body, grid=(S // TM,), in_specs=[_row(QKV_W)] + _table_specs(g), out_specs=_row(QKV_W),
        out_shape=SDS((S, QKV_W), bf16), name=f"rope_fwd_{g}",
        compiler_params=_params(("parallel",), 6 * TM * QKV_W + 12 * TM * 128, 24 * TM * QKV_W))(p_qkv, tables, tables, tables)


def _rope_bwd(g, dq, dk, dv, tables):
    def body(dq_ref, dk_ref, dv_ref, c_ref, lo_ref, hi_ref, o_ref):
        c, lo, hi = c_ref[...], lo_ref[...], hi_ref[...]
        o_ref[:, :DIL_W] = _rotate(dq_ref[...], c, lo, hi, -1.0).astype(bf16)
        o_ref[:, DIL_W:2 * DIL_W] = _rotate(dk_ref[...], c, lo, hi, -1.0).astype(bf16)
        o_ref[:, 2 * DIL_W:] = dv_ref[...].astype(bf16)

    return pl.pallas_call(
        body, grid=(S // TM,), in_specs=[_row(DIL_W)] * 3 + _table_specs(g), out_specs=_row(QKV_W),
        out_shape=SDS((S, QKV_W), bf16), name=f"rope_bwd_{g}",
        compiler_params=_params(("parallel",), 6 * TM * QKV_W + 12 * TM * 128, 24 * TM * QKV_W))(dq, dk, dv, tables, tables, tables)


def _nt(a, b):
    return lax.dot_general(a, b, (((1,), (1,)), ((), ())), preferred_element_type=f32)


def _tn(a, b):
    return lax.dot_general(a, b, (((0,), (0,)), ((), ())), preferred_element_type=f32)


STEP_BLOCKS = 4
STEP_ROWS = STEP_BLOCKS * BLK


def _dil_prev(g, b):
    _, d = DIL_GROUPS[g]
    nb = S // d // BLK
    if nb == 1 or (b == 0 and nb <= STEP_BLOCKS):
        return None
    return "in" if b > 0 else "halo"


def _bnt(a, b):
    return lax.dot_general(a, b, (((2,), (2,)), ((0,), (0,))), preferred_element_type=f32)


def _bnn(a, b):
    return lax.dot_general(a, b, (((2,), (1,)), ((0,), (0,))), preferred_element_type=f32)


def _btn(a, b):
    return lax.dot_general(a, b, (((1,), (1,)), ((0,), (0,))), preferred_element_type=f32)


def _on_tail(x, tail, fn):
    if tail == x.shape[0]:
        return fn(x)
    return jnp.concatenate([x[:-tail], fn(x[-tail:])], axis=0)


def _heads(ref, part):
    n = ref.shape[0] // BLK
    return jnp.stack([ref[b * BLK:(b + 1) * BLK, part * DIL_W + h * HD:part * DIL_W + (h + 1) * HD]
                      for b in range(n) for h in range(SLOTS)])


def _dil_operands(g, qkv_ref, halo_ref):
    q, kc, vc = (_heads(qkv_ref, part) for part in range(3))
    qi = lax.broadcasted_iota(jnp.int32, (1, BLK, BLK), 1)
    kj = lax.broadcasted_iota(jnp.int32, (1, BLK, BLK), 2)
    with_prev = [b for b in range(STEP_BLOCKS) if _dil_prev(g, b) is not None]
    tail = SLOTS * len(with_prev)
    if not tail:
        return q, kc, vc, None, None, kj <= qi, None, 0
    assert with_prev == list(range(STEP_BLOCKS - len(with_prev), STEP_BLOCKS))
    inside = SLOTS * sum(_dil_prev(g, b) == "in" for b in with_prev)
    kp, vp, prev = kc[:inside], vc[:inside], jnp.broadcast_to(kj >= qi, (inside, BLK, BLK))
    if inside < tail:
        no_halo = jnp.where(pl.program_id(0) == 0, BLK + 1, 0)
        kp = jnp.concatenate([_heads(halo_ref, 1), kp], axis=0)
        vp = jnp.concatenate([_heads(halo_ref, 2), vp], axis=0)
        prev = jnp.concatenate([jnp.broadcast_to(kj >= qi + no_halo, (SLOTS, BLK, BLK)), prev], axis=0)
    return q, kc, vc, kp, vp, kj <= qi, prev, tail


def _dil_in_specs(g, n_aux):
    step = lambda w: pl.BlockSpec((STEP_ROWS, w), lambda i: (i, 0))
    halo = [pl.BlockSpec((BLK, QKV_W), lambda i: (jnp.maximum(i * STEP_BLOCKS - 1, 0), 0))]
    needs_halo = _dil_prev(g, 0) == "halo"
    return [step(QKV_W)] + (halo if needs_halo else []) + [step(DIL_W)] * n_aux, needs_halo


def _dil_fwd(g, qkv):
    in_specs, needs_halo = _dil_in_specs(g, 0)

    def body(*refs):
        qkv_ref, halo_ref = refs[0], refs[1] if needs_halo else None
        o_ref, lse_ref = refs[-2:]
        q, kc, vc, kp, vp, cur, prev, tail = _dil_operands(g, qkv_ref, halo_ref)
        sc = jnp.where(cur, _bnt(q, kc) * SCALE, NEG)
        m = jnp.max(sc, axis=-1, keepdims=True)
        if tail:
            sp = jnp.where(prev, _bnt(q[-tail:], kp) * SCALE, NEG)
            m = _on_tail(m, tail, lambda t: jnp.maximum(t, jnp.max(sp, axis=-1, keepdims=True)))
            pp = jnp.exp(sp - m[-tail:])
        pc = jnp.exp(sc - m)
        den = jnp.sum(pc, axis=-1, keepdims=True)
        if tail:
            den = _on_tail(den, tail, lambda t: t + jnp.sum(pp, axis=-1, keepdims=True))
        inv = 1.0 / den
        o = _bnn((pc * inv).astype(bf16), vc)
        if tail:
            o = _on_tail(o, tail, lambda t: t + _bnn((pp * inv[-tail:]).astype(bf16), vp))
        lse = m + jnp.log(den)
        for b in range(STEP_BLOCKS):
            for h in range(SLOTS):
                rows, hs = slice(b * BLK, (b + 1) * BLK), slice(h * HD, (h + 1) * HD)
                o_ref[rows, hs] = o[SLOTS * b + h]
                lse_ref[rows, hs] = jnp.broadcast_to(lse[SLOTS * b + h], (BLK, HD))

    out = pl.BlockSpec((STEP_ROWS, DIL_W), lambda i: (i, 0))
    return pl.pallas_call(
        body, grid=(S // STEP_ROWS,), in_specs=in_specs, out_specs=[out, out], out_shape=[SDS((S, DIL_W), f32)] * 2,
        name=f"dil_fwd_{g}", compiler_params=_params(("parallel",), 12 * STEP_ROWS * DIL_W, 2 << 20),
    )(*([qkv] * (2 if needs_halo else 1)))


def _dil_combine(outs, lses):
    def body(o0, o1, o2, l0, l1, l2, out_ref, lse_ref, so1, so2, sl1, sl2):
        for (_, d), src, dst in ((DIL_GROUPS[1], o1, so1), (DIL_GROUPS[2], o2, so2),
                                 (DIL_GROUPS[1], l1, sl1), (DIL_GROUPS[2], l2, sl2)):
            rows = S // d
            for r in range(d):
                dst[pl.ds(r, rows, stride=d), :] = src[r * rows:(r + 1) * rows, :]
        a, b, c = l0[...], sl1[...], sl2[...]
        m = jnp.maximum(jnp.maximum(a, b), c)
        ea, eb, ec = jnp.exp(a - m), jnp.exp(b - m), jnp.exp(c - m)
        z = ea + eb + ec
        inv = 1.0 / z
        out_ref[...] = (ea * inv) * o0[...] + (eb * inv) * so1[...] + (ec * inv) * so2[...]
        lse_ref[...] = m + jnp.log(z)

    blk = pl.BlockSpec((S, 128), lambda c: (0, c))
    return pl.pallas_call(
        body, grid=(DIL_W // 128,), in_specs=[blk] * 6, out_specs=[blk] * 2,
        out_shape=[SDS((S, DIL_W), f32)] * 2, scratch_shapes=[pltpu.VMEM((S, 128), f32)] * 4, name="dil_combine",
        compiler_params=_params(("parallel",), 32 * S * 128, 32 * S * 128))(*outs, *lses)


def _dil_bwd(g, qkv, d_out, delta, lse):
    in_specs, needs_halo = _dil_in_specs(g, 3)

    def body(*refs):
        qkv_ref, halo_ref = refs[0], refs[1] if needs_halo else None
        do_ref, dl_ref, lse_ref, dq_ref, dk_ref, dv_ref = refs[-6:]
        q, kc, vc, kp, vp, cur, prev, tail = _dil_operands(g, qkv_ref, halo_ref)
        tiles = [(slice(b * BLK, (b + 1) * BLK), h) for b in range(STEP_BLOCKS) for h in range(SLOTS)]
        do = jnp.stack([do_ref[rows, h * HD:(h + 1) * HD] for rows, h in tiles]).astype(bf16)
        lse = jnp.stack([lse_ref[rows, h * HD:h * HD + 1] for rows, h in tiles])
        delta = jnp.stack([dl_ref[rows, h * HD:h * HD + 1] for rows, h in tiles])

        def probs(q, k, mask, lse, do, v, delta):
            p = jnp.exp(jnp.where(mask, _bnt(q, k) * SCALE, NEG) - lse)
            ds = p * (_bnt(do, v) - delta) * SCALE
            return p.astype(bf16), ds.astype(bf16)

        p, ds = probs(q, kc, cur, lse, do, vc, delta)
        dq, dk, dv = _bnn(ds, kc), _btn(ds, q), _btn(p, do)
        if tail:
            p, ds = probs(q[-tail:], kp, prev, lse[-tail:], do[-tail:], vp, delta[-tail:])
            dq = _on_tail(dq, tail, lambda t: t + _bnn(ds, kp))
            dk_p, dv_p = _btn(ds, q[-tail:]), _btn(p, do[-tail:])
            inside = tail - SLOTS if needs_halo else tail
            pad = jnp.zeros((len(tiles) - inside, BLK, HD), f32)
            dk = dk + jnp.concatenate([dk_p[tail - inside:], pad], axis=0)
            dv = dv + jnp.concatenate([dv_p[tail - inside:], pad], axis=0)
        first = pl.multiple_of(pl.program_id(0) * STEP_ROWS, STEP_ROWS)
        for t, (rows, h) in enumerate(tiles):
            hs = slice(h * HD, (h + 1) * HD)
            own = pl.ds(pl.multiple_of(first + rows.start, BLK), BLK)
            dq_ref[rows, hs] = dq[t]
            dk_ref[own, hs] = dk[t]
            dv_ref[own, hs] = dv[t]
        if needs_halo:
            before = pl.ds(pl.multiple_of(jnp.maximum(first - BLK, 0), BLK), BLK)
            for h in range(SLOTS):
                hs = slice(h * HD, (h + 1) * HD)
                dk_ref[before, hs] += dk_p[h]
                dv_ref[before, hs] += dv_p[h]

    whole = pl.BlockSpec((S, DIL_W), lambda i: (0, 0))
    return pl.pallas_call(
        body, grid=(S // STEP_ROWS,), in_specs=in_specs,
        out_specs=[pl.BlockSpec((STEP_ROWS, DIL_W), lambda i: (i, 0)), whole, whole],
        out_shape=[SDS((S, DIL_W), f32)] * 3, name=f"dil_bwd_{g}",
        compiler_params=_params(("arbitrary",), 20 * STEP_ROWS * DIL_W + 8 * S * DIL_W, 2 << 20),
    )(*([qkv] * (2 if needs_halo else 1)), d_out, delta, lse)


def _scan_rows(x, reverse):
    row = lax.broadcasted_iota(jnp.int32, x.shape, 0)
    k = 1
    while k < S:
        if reverse:
            x = x + jnp.where(row < S - k, pltpu.roll(x, S - k, 0), 0.0)
        else:
            x = x + jnp.where(row >= k, pltpu.roll(x, k, 0), 0.0)
        k *= 2
    return x


N_PAIR = N_FOX // 2
_PAIR_Q = pl.BlockSpec((None, S, 128), lambda p: (p, 0, 0))
_PAIR_K = pl.BlockSpec((None, 8, S), lambda p: (p, 0, 0))


def _forget_fwd(fz, b128):
    def body(z_ref, b_ref, fq_ref, fk_ref):
        z = z_ref[...] + b_ref[...]
        logf = jnp.minimum(z, 0.0) - jnp.log1p(jnp.exp(-jnp.abs(z)))
        f_cum = _scan_rows(logf, reverse=False)
        f_cum_t = f_cum.T
        fq_ref[...] = jnp.zeros_like(fq_ref)
        fk_ref[...] = jnp.zeros_like(fk_ref)
        for p in range(N_PAIR):
            fq_ref[p, :, 0:2] = f_cum[:, 2 * p:2 * p + 2]
            fk_ref[p, 0:2, :] = f_cum_t[2 * p:2 * p + 2, :]

    return pl.pallas_call(
        body, grid=(1,), in_specs=[pl.BlockSpec((S, 128), lambda i: (0, 0)), _vec(128)],
        out_specs=[pl.BlockSpec((N_PAIR, S, 128), lambda i: (0, 0, 0)), pl.BlockSpec((N_PAIR, 8, S), lambda i: (0, 0, 0))],
        out_shape=[SDS((N_PAIR, S, 128), f32), SDS((N_PAIR, 8, S), f32)], name="forget_fwd",
        compiler_params=_params(("arbitrary",), 24 * S * 128, 24 * S * 128))(fz, b128)


def _forget_bwd(fz, b128, d_f_cols, d_f_rows):
    def body(z_ref, b_ref, dfc_ref, dfr_ref, dz_ref, db_ref, df_sc):
        z = z_ref[...] + b_ref[...]
        df_sc[...] = jnp.zeros_like(df_sc)
        for p in range(N_PAIR):
            df_sc[:, 2 * p:2 * p + 2] = dfr_ref[p, :, 0:2] + dfc_ref[p].T[:, 0:2]
        dz = _scan_rows(df_sc[...], reverse=True) * jax.nn.sigmoid(-z)
        dz_ref[...] = dz
        db_ref[...] = jnp.sum(dz, axis=0, keepdims=True)

    full = pl.BlockSpec((S, 128), lambda i: (0, 0))
    return pl.pallas_call(
        body, grid=(1,),
        in_specs=[full, _vec(128), pl.BlockSpec((N_PAIR, 8, S), lambda i: (0, 0, 0)), pl.BlockSpec((N_PAIR, S, 128), lambda i: (0, 0, 0))],
        out_specs=[full, _vec(128)], out_shape=[SDS((S, 128), f32), SDS((1, 128), f32)],
        scratch_shapes=[pltpu.VMEM((S, 128), f32)], name="forget_bwd",
        compiler_params=_params(("arbitrary",), 32 * S * 128, 24 * S * 128))(fz, b128, d_f_cols, d_f_rows)


def _fox_scores(q_ref, k_ref, fq_ref, fk_ref, qi, hh):
    n = (qi + 1) * TQ
    rows, hs = slice(qi * TQ, n), slice(hh * HD, (hh + 1) * HD)
    s = _nt(q_ref[rows, hs], k_ref[0:n, hs]) * SCALE + (fq_ref[rows, hh:hh + 1] - fk_ref[hh:hh + 1, 0:n])
    qpos = qi * TQ + lax.broadcasted_iota(jnp.int32, (TQ, n), 0)
    kpos = lax.broadcasted_iota(jnp.int32, (TQ, n), 1)
    return jnp.where(kpos <= qpos, s, NEG)


def _pair_cols(first):
    return pl.BlockSpec((S, 128), lambda p: (0, first + p))


def _fox_fwd(vr, fq, fk):
    def body(q_ref, k_ref, v_ref, fq_ref, fk_ref, o_ref, lse_ref):
        lse_ref[...] = jnp.zeros_like(lse_ref)
        for hh in range(2):
            hs = slice(hh * HD, (hh + 1) * HD)
            for qi in range(S // TQ):
                n = (qi + 1) * TQ
                rows = slice(qi * TQ, n)
                s = _fox_scores(q_ref, k_ref, fq_ref, fk_ref, qi, hh)
                m = jnp.max(s, axis=-1, keepdims=True)
                p = jnp.exp(s - m)
                den = jnp.sum(p, axis=-1, keepdims=True)
                o_ref[rows, hs] = jnp.dot((p * (1.0 / den)).astype(bf16), v_ref[0:n, hs], preferred_element_type=f32)
                lse_ref[rows, hh:hh + 1] = m + jnp.log(den)

    return pl.pallas_call(
        body, grid=(N_PAIR,), in_specs=[_pair_cols(0), _pair_cols(N_PAIR), _pair_cols(2 * N_PAIR), _PAIR_Q, _PAIR_K],
        out_specs=[_pair_cols(0), _PAIR_Q], out_shape=[SDS((S, FOX_W), f32), SDS((N_PAIR, S, 128), f32)],
        name="fox_fwd", compiler_params=_params(("parallel",), 12 * S * 128, 16 * TQ * S),
    )(vr, vr, vr, fq, fk)


def _fox_bwd(vr, fq, fk, lse, d_out, delta):
    def body(q_ref, k_ref, v_ref, do_ref, fq_ref, fk_ref, lse_ref, dl_ref, dq_ref, dk_ref, dv_ref, dfc_ref, dfr_ref,
             dk_sc, dv_sc):
        dfc_ref[...] = jnp.zeros_like(dfc_ref)
        dfr_ref[...] = jnp.zeros_like(dfr_ref)
        for hh in range(2):
            hs = slice(hh * HD, (hh + 1) * HD)
            dk_sc[...] = jnp.zeros_like(dk_sc)
            dv_sc[...] = jnp.zeros_like(dv_sc)
            for qi in range(S // TQ):
                n = (qi + 1) * TQ
                rows = slice(qi * TQ, n)
                q, do, k, v = q_ref[rows, hs], do_ref[rows, hs], k_ref[0:n, hs], v_ref[0:n, hs]
                p = jnp.exp(_fox_scores(q_ref, k_ref, fq_ref, fk_ref, qi, hh) - lse_ref[rows, hh:hh + 1])
                ds = p * (_nt(do, v) - dl_ref[rows, hh:hh + 1])
                dsb = ds.astype(bf16)
                dq_ref[rows, hs] = jnp.dot(dsb, k, preferred_element_type=f32) * SCALE
                dk_sc[0:n, :] += _tn(dsb, q) * SCALE
                dv_sc[0:n, :] += _tn(p.astype(bf16), do)
                dfc_ref[hh:hh + 1, 0:n] -= jnp.sum(ds, axis=0, keepdims=True)
                dfr_ref[rows, hh:hh + 1] = jnp.sum(ds, axis=-1, keepdims=True)
            dk_ref[:, hs] = dk_sc[...]
            dv_ref[:, hs] = dv_sc[...]

    cols = [_pair_cols(k * N_PAIR) for k in range(3)]
    return pl.pallas_call(
        body, grid=(N_PAIR,), in_specs=cols + [_pair_cols(0), _PAIR_Q, _PAIR_K, _PAIR_Q, _PAIR_Q],
        out_specs=[_pair_cols(0)] * 3 + [_PAIR_K, _PAIR_Q],
        out_shape=[SDS((S, FOX_W), f32)] * 3 + [SDS((N_PAIR, 8, S), f32), SDS((N_PAIR, S, 128), f32)],
        scratch_shapes=[pltpu.VMEM((S, HD), f32)] * 2, name="fox_bwd",
        compiler_params=_params(("parallel",), 32 * S * 128, 24 * TQ * S),
    )(vr, vr, vr, d_out, fq, fk, lse, delta)


def _merge_fwd(out_a, out_b, w_a, w_b, gf):
    cw = D // N_SHARD

    def body(oa_ref, ob_ref, wa_ref, wb_ref, ga_ref, gb_ref, ya_ref, yb_ref, mg_ref):
        oa, ob = oa_ref[...].astype(bf16), ob_ref[...].astype(bf16)
        for j in range(N_SHARD):
            cols = slice(j * cw, (j + 1) * cw)
            ya = jnp.dot(oa, wa_ref[j], preferred_element_type=f32)
            yb = jnp.dot(ob, wb_ref[j], preferred_element_type=f32)
            ya_ref[:, cols] = ya
            yb_ref[:, cols] = yb
            mg_ref[:, cols] = (jax.nn.sigmoid(ga_ref[:, cols]) * ya + jax.nn.sigmoid(gb_ref[:, cols]) * yb).astype(bf16)

    full = lambda a: pl.BlockSpec(a.shape, lambda i: (0, 0, 0))
    return pl.pallas_call(
        body, grid=(S // TM,),
        in_specs=[_row(DIL_W), _row(FOX_W), full(w_a), full(w_b), _row(D), pl.BlockSpec((TM, D), lambda i: (i, 1))],
        out_specs=[_row(D)] * 3, out_shape=[SDS((S, D), f32), SDS((S, D), f32), SDS((S, D), bf16)], name="merge_fwd",
        compiler_params=_params(("parallel",), 22 * TM * D + 2 * (DIL_W + FOX_W) * D, 16 * TM * D),
    )(out_a, out_b, w_a, w_b, gf, gf)


def _merge_bwd(d_mix, w_out, ya, yb, gf):
    def body(dx_ref, w_ref, ya_ref, yb_ref, ga_ref, gb_ref, dya_ref, dyb_ref, dg_ref):
        dm = _nt(dx_ref[...], w_ref[...])
        sa, sb = jax.nn.sigmoid(ga_ref[...]), jax.nn.sigmoid(gb_ref[...])
        dya_ref[...] = (dm * sa).astype(bf16)
        dyb_ref[...] = (dm * sb).astype(bf16)
        dg_ref[:, :D] = (dm * ya_ref[...] * sa * (1.0 - sa)).astype(bf16)
        dg_ref[:, D:] = (dm * yb_ref[...] * sb * (1.0 - sb)).astype(bf16)

    return pl.pallas_call(
        body, grid=(S // TM,),
        in_specs=[_row(D), _whole(w_out)] + [_row(D)] * 3 + [pl.BlockSpec((TM, D), lambda i: (i, 1))],
        out_specs=[_row(D), _row(D), _row(2 * D)],
        out_shape=[SDS((S, D), bf16), SDS((S, D), bf16), SDS((S, 2 * D), bf16)], name="proj_out_bwd_merge",
        compiler_params=_params(("parallel",), 26 * TM * D + 2 * D * D, 28 * TM * D))(d_mix, w_out, ya, yb, gf, gf)


def _branch_bwd(d_ya, d_yb, w_a, w_b, out_a, out_b):
    cw = D // N_SHARD

    def body(dya_ref, dyb_ref, wa_ref, wb_ref, oa_ref, ob_ref, doa_ref, dla_ref, dob_ref, dlb_ref):
        doa = jnp.zeros((TM, DIL_W), f32)
        dob = jnp.zeros((TM, FOX_W), f32)
        for j in range(N_SHARD):
            cols = slice(j * cw, (j + 1) * cw)
            doa += _nt(dya_ref[:, cols], wa_ref[j])
            dob += _nt(dyb_ref[:, cols], wb_ref[j])
        doa_ref[...] = doa
        dob_ref[...] = dob.astype(bf16)
        prod_a = doa * oa_ref[...]
        for h in range(SLOTS):
            hs = slice(h * HD, (h + 1) * HD)
            dla_ref[:, hs] = jnp.broadcast_to(jnp.sum(prod_a[:, hs], axis=-1, keepdims=True), (TM, HD))
        prod_b = dob * ob_ref[...]
        dlb_ref[...] = jnp.zeros_like(dlb_ref)
        for h in range(N_FOX):
            dlb_ref[h // 2, :, h % 2:h % 2 + 1] = jnp.sum(prod_b[:, h * HD:(h + 1) * HD], axis=-1, keepdims=True)

    full = lambda a: pl.BlockSpec(a.shape, lambda i: (0, 0, 0))
    return pl.pallas_call(
        body, grid=(S // TM,),
        in_specs=[_row(D), _row(D), full(w_a), full(w_b), _row(DIL_W), _row(FOX_W)],
        out_specs=[_row(DIL_W), _row(DIL_W), _row(FOX_W), pl.BlockSpec((N_PAIR, TM, 128), lambda i: (0, i, 0))],
        out_shape=[SDS((S, DIL_W), f32), SDS((S, DIL_W), f32), SDS((S, FOX_W), bf16), SDS((N_PAIR, S, 128), f32)],
        name="branch_bwd", compiler_params=_params(("parallel",), 8 * TM * D + 2 * (DIL_W + FOX_W) * D, 8 * TM * D),
    )(d_ya, d_yb, w_a, w_b, out_a, out_b)


def _branch_grads(out_a, out_b, d_ya, d_yb):
    cw = D // N_SHARD

    def body(oa_ref, ob_ref, dya_ref, dyb_ref, ga_ref, gb_ref):
        ga_ref[...] = _tn(oa_ref[...].astype(bf16), dya_ref[...]).astype(bf16)
        gb_ref[...] = _tn(ob_ref[...].astype(bf16), dyb_ref[...]).astype(bf16)

    whole = lambda w: pl.BlockSpec((S, w), lambda j: (0, 0))
    cols = pl.BlockSpec((S, cw), lambda j: (0, j))
    return pl.pallas_call(
        body, grid=(N_SHARD,), in_specs=[whole(DIL_W), whole(FOX_W), cols, cols],
        out_specs=[pl.BlockSpec((None, DIL_W, cw), lambda j: (j, 0, 0)), pl.BlockSpec((None, FOX_W, cw), lambda j: (j, 0, 0))],
        out_shape=[SDS((N_SHARD, DIL_W, cw), bf16), SDS((N_SHARD, FOX_W, cw), bf16)], name="grad_w_proj_ab",
        compiler_params=_params(("parallel",), 4 * S * (DIL_W + FOX_W) + 4 * S * cw + 4 * (DIL_W + FOX_W) * cw,
                                4 * S * (DIL_W + FOX_W)))(out_a, out_b, d_ya, d_yb)


FF_TN = F_FF // 2
FF_TM = 1024


def _ffn_fwd(h, w_gate_t, w_up_t):
    def body(h_ref, wg_ref, wu_ref, g_ref, u_ref, a_ref):
        hb = h_ref[...]
        g = _nt(hb, wg_ref[...])
        u = _nt(hb, wu_ref[...])
        g_ref[...] = g
        u_ref[...] = u
        a_ref[...] = (g * jax.nn.sigmoid(g) * u).astype(bf16)

    tile = pl.BlockSpec((FF_TM, FF_TN), lambda j, i: (i, j))
    wspec = pl.BlockSpec((FF_TN, D), lambda j, i: (j, 0))
    return pl.pallas_call(
        body, grid=(F_FF // FF_TN, S // FF_TM),
        in_specs=[pl.BlockSpec((FF_TM, D), lambda j, i: (i, 0)), wspec, wspec], out_specs=[tile] * 3,
        out_shape=[SDS((S, F_FF), f32), SDS((S, F_FF), f32), SDS((S, F_FF), bf16)], name="ffn_fwd",
        compiler_params=_params(("parallel", "parallel"), 2 * FF_TM * D + 4 * D * FF_TN + 10 * FF_TM * FF_TN, 16 * FF_TM * FF_TN),
    )(h, w_gate_t, w_up_t)


def _ffn_bwd_act(d_ff, w_down, g_act, u_act):
    def body(d_ref, wd_ref, g_ref, u_ref, dg_ref, du_ref):
        da = _nt(d_ref[...], wd_ref[...])
        g = g_ref[...]
        sg = jax.nn.sigmoid(g)
        du_ref[...] = (da * g * sg).astype(bf16)
        dg_ref[...] = (da * u_ref[...] * sg * (1.0 + g * (1.0 - sg))).astype(bf16)

    tile = pl.BlockSpec((FF_TM, FF_TN), lambda j, i: (i, j))
    return pl.pallas_call(
        body, grid=(F_FF // FF_TN, S // FF_TM),
        in_specs=[pl.BlockSpec((FF_TM, D), lambda j, i: (i, 0)), pl.BlockSpec((FF_TN, D), lambda j, i: (j, 0)), tile, tile],
        out_specs=[tile, tile], out_shape=[SDS((S, F_FF), bf16)] * 2, name="ffn_bwd_act",
        compiler_params=_params(("parallel", "parallel"), 2 * FF_TM * D + 2 * D * FF_TN + 12 * FF_TM * FF_TN, 16 * FF_TM * FF_TN),
    )(d_ff, w_down, g_act, u_act)


def _row_tile(rows):
    return next(t for t in (376, 128, 176, 64, 32, 16, 8) if rows % t == 0)


def _adamw_math(w, g, m, v):
    c1 = 1.0 - ADAM_B1 ** ADAM_STEP
    c2 = 1.0 - ADAM_B2 ** ADAM_STEP
    m_new = ADAM_B1 * m + (1.0 - ADAM_B1) * g
    v_new = ADAM_B2 * v + (1.0 - ADAM_B2) * (g * g)
    return -ADAM_LR * ((m_new / c1) / (jnp.sqrt(v_new / c2) + ADAM_EPS) + ADAM_WD * w), m_new, v_new


def _adamw(w, g, m, v, name):
    rows, cols = w.shape
    tm = _row_tile(rows)

    def body(w_ref, g_ref, m_ref, v_ref, d_ref, nm_ref, nv_ref):
        d_ref[...], nm_ref[...], nv_ref[...] = _adamw_math(w_ref[...], g_ref[...], m_ref[...], v_ref[...])

    spec = pl.BlockSpec((tm, cols), lambda i: (i, 0))
    return pl.pallas_call(
        body, grid=(rows // tm,), in_specs=[spec] * 4, out_specs=[spec] * 3, out_shape=[SDS(w.shape, f32)] * 3,
        name=name, compiler_params=_params(("parallel",), 28 * tm * cols, 16 * tm * cols))(w, g, m, v)


def _adamw_halves(w, g_mine, g_theirs, m, v, name):
    rows, cols = w.shape
    tm = _row_tile(rows // 2)
    per_half = rows // 2 // tm
    core = lax.axis_index("c").astype(jnp.int32).reshape(1)

    def body(c_ref, w_ref, gm_ref, gt_ref, m_ref, v_ref, g_ref, d_ref, nm_ref, nv_ref):
        mine = pl.program_id(0) // per_half == c_ref[0]
        g = jnp.where(mine, gm_ref[...], gt_ref[...])
        g_ref[...] = g
        d_ref[...], nm_ref[...], nv_ref[...] = _adamw_math(w_ref[...], g, m_ref[...], v_ref[...])

    spec = pl.BlockSpec((tm, cols), lambda i, c_ref: (i, 0))
    in_half = lambda i, first: jnp.clip(i - first * per_half, 0, per_half - 1)
    grid_spec = pltpu.PrefetchScalarGridSpec(
        num_scalar_prefetch=1, grid=(rows // tm,),
        in_specs=[spec, pl.BlockSpec((tm, cols), lambda i, c_ref: (in_half(i, c_ref[0]), 0)),
                  pl.BlockSpec((tm, cols), lambda i, c_ref: (in_half(i, 1 - c_ref[0]), 0)), spec, spec],
        out_specs=[spec] * 4)
    return pl.pallas_call(
        body, grid_spec=grid_spec, out_shape=[SDS(w.shape, f32)] * 4, name=name,
        compiler_params=_params(("parallel",), 36 * tm * cols, 16 * tm * cols))(core, w, g_mine, g_theirs, m, v)


_ANY = pl.BlockSpec(memory_space=pl.ANY)


def _place():
    x, y, c = lax.axis_index("x"), lax.axis_index("y"), lax.axis_index("c")
    chips = [(1 - x, y), (x, 1 - y), (1 - x, 1 - y)]
    return x, y, c, chips


def _halved(t):
    return t.reshape(t.shape[:-2] + (2, t.shape[-2] // 2, t.shape[-1]))


def _gather_body(src, out, send_ici, recv_ici, send_d2d, recv_d2d):
    x, y, c, chips = _place()
    sibling = (x, y, 1 - c)
    me_j = 2 * x + y
    sends = []
    for a in range(len(src)):
        for p in range(3):
            cp = pltpu.make_async_remote_copy(
                src_ref=src[a].at[c], dst_ref=out[a].at[me_j, c], send_sem=send_ici.at[a, p],
                recv_sem=recv_ici.at[a, p], device_id=(*chips[p], c), device_id_type=MESH)
            cp.start()
            sends.append(cp)
    for a in range(len(src)):
        for p, (px, py) in enumerate(chips):
            blk = out[a].at[2 * px + py, c]
            pltpu.make_async_remote_copy(
                src_ref=blk, dst_ref=blk, send_sem=send_ici.at[a, p], recv_sem=recv_ici.at[a, p],
                device_id=sibling, device_id_type=MESH).wait_recv()
            fw = pltpu.make_async_remote_copy(
                src_ref=blk, dst_ref=blk, send_sem=send_d2d.at[a, p], recv_sem=recv_d2d.at[a, p],
                device_id=sibling, device_id_type=MESH)
            fw.start()
            sends.append(fw)
    for a in range(len(src)):
        for p, (px, py) in enumerate(chips):
            blk = out[a].at[2 * px + py, 1 - c]
            pltpu.make_async_remote_copy(
                src_ref=blk, dst_ref=blk, send_sem=send_d2d.at[a, p], recv_sem=recv_d2d.at[a, p],
                device_id=sibling, device_id_type=MESH).wait_recv()
    for cp in sends:
        cp.wait_send()


def _handshake(peers):
    barrier = pltpu.get_barrier_semaphore()
    for peer in peers:
        pl.semaphore_signal(barrier, inc=1, device_id=peer, device_id_type=MESH)
    pl.semaphore_wait(barrier, len(peers))


_SEQUENCER = dict(axis_name="sequencer", num_cores=1)
GATHER_LATE_ID, SCATTER_EARLY_ID, SWAP_EARLY_ID, GATHER_FIRST_ID, SCATTER_LATE_ID = 1, 2, 3, 4, 5


def _all_gather_async(shards, after, name, collective_id):
    n, k = len(shards), len(after)

    def body(*refs):
        x, y, c, chips = _place()
        _handshake([(*chip, c) for chip in chips] + [(x, y, 1 - c)])
        _gather_body(refs[:n], refs[n + k:2 * n + k], *refs[2 * n + k:])

    return pl.kernel(
        body, out_type=[SDS((N_SHARD,) + t.shape, t.dtype) for t in shards],
        mesh=plsc.ScalarSubcoreMesh(**_SEQUENCER), scratch_types=[pltpu.SemaphoreType.DMA((n, 3))] * 4,
        compiler_params=pltpu.CompilerParams(collective_id=collective_id), name=name)(*shards, *after)


def _pair_swap(grads):
    n = len(grads)

    def body(*refs):
        src, out, send_sems, recv_sems = refs[:n], refs[n:2 * n], refs[2 * n], refs[2 * n + 1]
        x, y, c, _ = _place()
        copies = [pltpu.make_async_remote_copy(
            src_ref=src[a].at[:, 1 - c], dst_ref=out[a], send_sem=send_sems.at[a], recv_sem=recv_sems.at[a],
            device_id=(x, y, 1 - c), device_id_type=MESH) for a in range(n)]
        for cp in copies:
            cp.start()
        for cp in copies:
            cp.wait()

    return pl.pallas_call(
        body, in_specs=[_ANY] * n, out_specs=[_ANY] * n,
        out_shape=[SDS((N_SHARD,) + t.shape[2:], t.dtype) for t in grads],
        scratch_shapes=[pltpu.SemaphoreType.DMA((n,)), pltpu.SemaphoreType.DMA((n,))], name="pair_swap",
        compiler_params=pltpu.CompilerParams(has_side_effects=True))(*grads)


def _pair_swap_early(grads):
    n = len(grads)

    def body(*refs):
        src, out, send_sems, recv_sems = refs[:n], refs[n:2 * n], refs[2 * n], refs[2 * n + 1]
        x, y, c, _ = _place()
        _handshake([(x, y, 1 - c)])
        copies = [pltpu.make_async_remote_copy(
            src_ref=src[a].at[:, 1 - c], dst_ref=out[a], send_sem=send_sems.at[a], recv_sem=recv_sems.at[a],
            device_id=(x, y, 1 - c), device_id_type=MESH) for a in range(n)]
        for cp in copies:
            cp.start()
        for cp in copies:
            cp.wait()

    return pl.kernel(
        body, out_type=[SDS((N_SHARD,) + t.shape[2:], t.dtype) for t in grads],
        mesh=plsc.ScalarSubcoreMesh(**_SEQUENCER), scratch_types=[pltpu.SemaphoreType.DMA((n,))] * 2,
        compiler_params=pltpu.CompilerParams(collective_id=SWAP_EARLY_ID), name="pair_swap_early")(*grads)


def _scatter_early(parts):
    n = len(parts)

    def body(*refs):
        part, recv, send_sems, recv_sems = refs[:n], refs[n:2 * n], refs[2 * n], refs[2 * n + 1]
        x, y, c, chips = _place()
        _handshake([(*chip, c) for chip in chips])
        me_j = 2 * x + y
        sends = []
        for a in range(n):
            for p, (px, py) in enumerate(chips):
                cp = pltpu.make_async_remote_copy(
                    src_ref=part[a].at[2 * px + py], dst_ref=recv[a].at[me_j], send_sem=send_sems.at[a, p],
                    recv_sem=recv_sems.at[a, p], device_id=(px, py, c), device_id_type=MESH)
                cp.start()
                sends.append(cp)
        for a in range(n):
            for p, (px, py) in enumerate(chips):
                slot = recv[a].at[2 * px + py]
                pltpu.make_async_remote_copy(
                    src_ref=slot, dst_ref=slot, send_sem=send_sems.at[a, p], recv_sem=recv_sems.at[a, p],
                    device_id=(px, py, c), device_id_type=MESH).wait_recv()
        for cp in sends:
            cp.wait_send()

    return pl.kernel(
        body, out_type=[SDS(t.shape, t.dtype) for t in parts],
        mesh=plsc.ScalarSubcoreMesh(**_SEQUENCER), scratch_types=[pltpu.SemaphoreType.DMA((n, 3))] * 2,
        compiler_params=pltpu.CompilerParams(collective_id=SCATTER_EARLY_ID), name="scatter_early")(*parts)


def _pair_sum(grads, other, name):
    _, _, rows, cols = grads.shape
    tr = _row_tile(rows)
    core = lax.axis_index("c").astype(jnp.int32).reshape(1)

    def body(c_ref, g_ref, o_ref, out_ref):
        out_ref[...] = (g_ref[...].astype(f32) + o_ref[...].astype(f32)).astype(bf16)

    grid_spec = pltpu.PrefetchScalarGridSpec(
        num_scalar_prefetch=1, grid=(N_SHARD, rows // tr),
        in_specs=[pl.BlockSpec((None, None, tr, cols), lambda j, i, c_ref: (j, c_ref[0], i, 0)),
                  pl.BlockSpec((None, tr, cols), lambda j, i, c_ref: (j, i, 0))],
        out_specs=pl.BlockSpec((None, tr, cols), lambda j, i, c_ref: (j, i, 0)))
    return pl.pallas_call(
        body, grid_spec=grid_spec, out_shape=SDS((N_SHARD, rows, cols), bf16), name=name,
        compiler_params=_params(("parallel", "parallel"), 10 * tr * cols, 12 * tr * cols))(core, grads, other)


def _scatter_partials(parts, small):
    n = len(parts)

    def body(*refs):
        part, small_ref, recv, small_all_ref = refs[:n], refs[n], refs[n + 1:2 * n + 1], refs[2 * n + 1]
        send_sems, recv_sems, ssend, srecv, local_sem = refs[2 * n + 2:]
        x, y, c, chips = _place()
        flip = lambda a, bit: 1 - a if bit else a
        peers = [(flip(x, k & 4), flip(y, k & 2), flip(c, k & 1)) for k in range(1, 8)]
        _handshake(peers)
        me_j = 2 * x + y
        me_dev = 4 * x + 2 * y + c
        own = pltpu.make_async_copy(small_ref, small_all_ref.at[me_dev], local_sem)
        own.start()
        sends = []
        for a in range(n):
            for p, (px, py) in enumerate(chips):
                cp = pltpu.make_async_remote_copy(
                    src_ref=part[a].at[2 * px + py], dst_ref=recv[a].at[me_j], send_sem=send_sems.at[a, p],
                    recv_sem=recv_sems.at[a, p], device_id=(px, py, c), device_id_type=MESH)
                cp.start()
                sends.append(cp)
        for k, to in enumerate(peers):
            cp = pltpu.make_async_remote_copy(
                src_ref=small_ref, dst_ref=small_all_ref.at[me_dev],
                send_sem=ssend.at[k], recv_sem=srecv.at[k], device_id=to, device_id_type=MESH)
            cp.start()
            sends.append(cp)
        for a in range(n):
            for p, (px, py) in enumerate(chips):
                slot = recv[a].at[2 * px + py]
                pltpu.make_async_remote_copy(
                    src_ref=slot, dst_ref=slot, send_sem=send_sems.at[a, p], recv_sem=recv_sems.at[a, p],
                    device_id=(px, py, c), device_id_type=MESH).wait_recv()
        for k, (px, py, pc) in enumerate(peers):
            slot = small_all_ref.at[4 * px + 2 * py + pc]
            pltpu.make_async_remote_copy(
                src_ref=slot, dst_ref=slot, send_sem=ssend.at[k], recv_sem=srecv.at[k],
                device_id=(px, py, pc), device_id_type=MESH).wait_recv()
        for cp in sends:
            cp.wait_send()
        own.wait()

    return pl.kernel(
        body, out_type=[SDS(t.shape, t.dtype) for t in parts] + [SDS((8, SMALL_ROWS, D), f32)],
        mesh=plsc.ScalarSubcoreMesh(**_SEQUENCER),
        scratch_types=[pltpu.SemaphoreType.DMA((n, 3)), pltpu.SemaphoreType.DMA((n, 3)),
                       pltpu.SemaphoreType.DMA((7,)), pltpu.SemaphoreType.DMA((7,)), pltpu.SemaphoreType.DMA],
        compiler_params=pltpu.CompilerParams(collective_id=SCATTER_LATE_ID), name="scatter_partials")(*parts, small)


def _sum_partials(part, recv, name):
    _, rows, cols = recv.shape
    tr = _row_tile(rows)
    me = (2 * lax.axis_index("x") + lax.axis_index("y")).astype(jnp.int32).reshape(1)

    def body(me_ref, mine, r0, r1, r2, r3, out_ref):
        acc = None
        for j, r in enumerate((r0, r1, r2, r3)):
            term = jnp.where(me_ref[0] == j, mine[...], r[...]).astype(f32)
            acc = term if acc is None else acc + term
        out_ref[...] = acc

    slot = lambda j: pl.BlockSpec((None, tr, cols), lambda i, me_ref: (jnp.where(me_ref[0] == j, j ^ 1, j), i, 0))
    grid_spec = pltpu.PrefetchScalarGridSpec(
        num_scalar_prefetch=1, grid=(rows // tr,),
        in_specs=[pl.BlockSpec((None, tr, cols), lambda i, me_ref: (me_ref[0], i, 0)), slot(0), slot(1), slot(2), slot(3)],
        out_specs=pl.BlockSpec((tr, cols), lambda i, me_ref: (i, 0)))
    return pl.pallas_call(
        body, grid_spec=grid_spec, out_shape=SDS((rows, cols), f32), name=name,
        compiler_params=_params(("parallel",), 14 * tr * cols, 12 * tr * cols))(me, part, recv, recv, recv, recv)


def _sum_small(small_all):
    def body(small_ref, out_ref):
        tot = small_ref[0]
        for k in range(1, 8):
            tot = tot + small_ref[k]
        out_ref[...] = tot

    return pl.pallas_call(
        body, grid=(1,), in_specs=[pl.BlockSpec((8, SMALL_ROWS, D), lambda i: (0, 0, 0))],
        out_specs=pl.BlockSpec((SMALL_ROWS, D), lambda i: (0, 0)), out_shape=SDS((SMALL_ROWS, D), f32),
        name="sum_small", compiler_params=_params(("arbitrary",), 36 * SMALL_ROWS * D))(small_all)


def _swap_halves(halves, name):
    n = len(halves)

    def body(*refs):
        src, out, send_sems, recv_sems = refs[:n], refs[n:2 * n], refs[2 * n], refs[2 * n + 1]
        x, y, c, _ = _place()
        copies = [pltpu.make_async_remote_copy(
            src_ref=src[a], dst_ref=out[a], send_sem=send_sems.at[a], recv_sem=recv_sems.at[a],
            device_id=(x, y, 1 - c), device_id_type=MESH) for a in range(n)]
        for cp in copies:
            cp.start()
        for cp in copies:
            cp.wait()

    return pl.pallas_call(
        body, in_specs=[_ANY] * n, out_specs=[_ANY] * n, out_shape=[SDS(t.shape, f32) for t in halves],
        scratch_shapes=[pltpu.SemaphoreType.DMA((n,))] * 2, name=name,
        compiler_params=pltpu.CompilerParams(has_side_effects=True))(*halves)


def _kernel_layout(name, t):
    t = t[0]
    if name in TRANSPOSED:
        t = jnp.swapaxes(t, 0, 1)
    return _pad_rows(t, SHARD_SHAPE[name][0])


def _harness_layout(name, t):
    if name == "w_in":
        t = t[:IN_SHARD]
    if name in TRANSPOSED:
        t = jnp.swapaxes(t, 0, 1)
    return t[None]


def _pad_rows(t, rows):
    return t if t.shape[0] == rows else jnp.pad(t, ((0, rows - t.shape[0]), (0, 0)))


_QA, _KA, _VA, _QB, _F, _GAB = 0, 768, 1536, 2304, 3840, 3848


def _spans(a, b):
    return [(j, max(a, j * IN_SHARD) - j * IN_SHARD, max(a, j * IN_SHARD) - a,
             min(b, (j + 1) * IN_SHARD) - max(a, j * IN_SHARD))
            for j in range(N_SHARD) if max(a, j * IN_SHARD) < min(b, (j + 1) * IN_SHARD)]


_LANES = pl.BlockSpec((N_SHARD, IN_SHARD_PAD, 128), lambda c: (0, 0, c))


def _split_w_in(shards):
    group = [[(o + g * DIL_W, o + (g + 1) * DIL_W) for o in (_QA, _KA, _VA)] for g in range(3)]
    fox = [[(_QB + k * FOX_W, _QB + (k + 1) * FOX_W)] for k in range(3)]
    wanted = group + fox + [[(_QB, _F)], [(_F, _GAB)], [(_GAB, IN_COLS)]]
    rows = [sum(b - a for a, b in w) for w in wanted]
    rows[7] = 128

    def body(s_ref, *o_refs):
        for o_ref, want in zip(o_refs, wanted):
            at = 0
            for a, b in want:
                for j, src, off, n in _spans(a, b):
                    o_ref[at + off:at + off + n, :] = s_ref[j, src:src + n, :]
                at += b - a
        o_refs[7][N_FOX:, :] = jnp.zeros((128 - N_FOX, 128), bf16)

    return pl.pallas_call(
        body, grid=(D // 128,), in_specs=[_LANES], out_specs=[pl.BlockSpec((r, 128), lambda c: (0, c)) for r in rows],
        out_shape=[SDS((r, D), bf16) for r in rows], name="split_w_in",
        compiler_params=_params(("parallel",), 2 * 128 * (N_SHARD * IN_SHARD_PAD + sum(rows))))(shards)


def _join_w_in(g_a, g_fox, g_f, g_gab):
    parts = [(g_a[k], o, o + DIL_W) for o in (0, DIL_W, 2 * DIL_W) for k in range(3)]
    parts += [(t, 0, FOX_W) for t in g_fox] + [(g_f, 0, N_FOX), (g_gab, 0, 2 * D)]
    arrays = list(g_a) + list(g_fox) + [g_f, g_gab]
    index = {id(t): i for i, t in enumerate(arrays)}

    def body(*refs):
        o_ref = refs[-1]
        o_ref[:, IN_SHARD:, :] = jnp.zeros((N_SHARD, IN_SHARD_PAD - IN_SHARD, 128), bf16)
        at = 0
        for t, lo, hi in parts:
            src_ref = refs[index[id(t)]]
            for j, dst, off, n in _spans(at, at + hi - lo):
                o_ref[j, dst:dst + n, :] = src_ref[lo + off:lo + off + n, :].astype(bf16)
            at += hi - lo

    return pl.pallas_call(
        body, grid=(D // 128,), in_specs=[pl.BlockSpec((t.shape[0], 128), lambda c: (0, c)) for t in arrays],
        out_specs=_LANES, out_shape=SDS((N_SHARD, IN_SHARD_PAD, D), bf16), name="join_w_in",
        compiler_params=_params(("parallel",), 2 * 128 * (N_SHARD * IN_SHARD_PAD + sum(t.shape[0] for t in arrays))),
    )(*arrays)


def _full_weights(gathered):
    full = {n: t.reshape((N_SHARD,) + SHARD_SHAPE[n]) for n, t in gathered.items()}
    out = {}
    if "w_in" in full:
        pieces = _split_w_in(full["w_in"])
        out.update(w_a_t=pieces[0:3], w_fox_t=pieces[3:6], w_vr_t=pieces[6], w_f_t=pieces[7], w_gab_t=pieces[8])
    if "w_out" in full:
        out.update(
            w_a4=full["w_proj_a"],
            w_b4=full["w_proj_b"],
            w_out=full["w_out"].reshape(D, D),
            w_gate_t=full["w_ffn_gate"].reshape(F_FF, D),
            w_up_t=full["w_ffn_up"].reshape(F_FF, D),
            w_down=full["w_ffn_down"].reshape(F_FF, D))
    return out


def _sharded_grads(g):
    full = dict(w_in=_join_w_in(g["w_a_t"], g["w_fox_t"], g["w_f_t"], g["w_gab_t"]), w_proj_a=g["w_a4"],
                w_proj_b=g["w_b4"], w_out=g["w_out"], w_ffn_gate=g["w_gate_t"], w_ffn_up=g["w_up_t"],
                w_ffn_down=g["w_down"])
    return {n: _halved(full[n].reshape((N_SHARD,) + SHARD_SHAPE[n])) for n in W_NAMES}


def _local_step(x, target, wt, b_forget, g_mix_pre, g_mix_post, g_ffn_pre, g_ffn_post, late=None):
    tables = _rope_tables()
    b128 = jnp.pad(b_forget, ((0, 0), (0, 128 - N_FOX)))
    dils = tuple(d for _, d in DIL_GROUPS[1:])

    hs = _norm_fwd([x] + list(_perm_rows([x], dils, "perm_x")), g_mix_pre)
    h1 = hs[0]
    if callable(wt):
        wt = wt(h1)
    qkv = [_rope_fwd(g, _mm([(hs[g], wt["w_a_t"][g])], "nt", f32, tm=1024, tn=QKV_W, name=f"proj_a_{g}"), tables)
           for g in range(3)]
    vr = _mm([(h1, wt["w_vr_t"])], "nt", bf16, tm=1024, tn=VR_W // 2, name="proj_vr")
    gab = _mm([(h1, wt["w_gab_t"])], "nt", f32, tm=512, tn=2 * D, name="proj_gab")
    fz = _mm([(h1, wt["w_f_t"])], "nt", f32, tm=1024, tn=128, name="proj_f")
    dil = [_dil_fwd(g, qkv[g]) for g in range(3)]
    out_a, lse_a = _dil_combine([o for o, _ in dil], [l for _, l in dil])
    f_q, f_k = _forget_fwd(fz, b128)
    out_b, lse_b = _fox_fwd(vr, f_q, f_k)
    if late is not None:
        wt = {**wt, **late(out_b)}
    ya, yb, merged = _merge_fwd(out_a, out_b, wt["w_a4"], wt["w_b4"], gab)
    mix, x2, h3 = _resid_norm_fwd(x, merged, wt["w_out"], g_mix_post, g_ffn_pre)
    g_act, u_act, a_act = _ffn_fwd(h3, wt["w_gate_t"], wt["w_up_t"])
    sq_err, dy, d_ff, dg_ffn_post = _loss_head(x2, a_act, wt["w_down"], g_ffn_post, target)

    grads = {}
    d_g, d_u = _ffn_bwd_act(d_ff, wt["w_down"], g_act, u_act)
    grads["w_down"] = _mm([(a_act, d_ff)], "tn", bf16, tm=FF_TN, tn=D, name="grad_w_down")
    grads["w_gate_t"] = _mm([(d_g, h3)], "tn", bf16, tm=FF_TN, tn=D, name="grad_w_gate")
    grads["w_up_t"] = _mm([(d_u, h3)], "tn", bf16, tm=FF_TN, tn=D, name="grad_w_up")
    dx2, d_mix, dg_ffn_pre, dg_mix_post = _norm_bwd_mid(dy, d_g, d_u, wt["w_gate_t"], wt["w_up_t"], x2, mix,
                                                        g_ffn_pre, g_mix_post)

    grads["w_out"] = _mm([(merged, d_mix)], "tn", bf16, tm=D, tn=D, name="grad_w_out")
    d_ya, d_yb, d_gab = _merge_bwd(d_mix, wt["w_out"], ya, yb, gab)
    grads["w_a4"], grads["w_b4"] = _branch_grads(out_a, out_b, d_ya, d_yb)
    d_out_a, delta_a, d_out_b, delta_b = _branch_bwd(d_ya, d_yb, wt["w_a4"], wt["w_b4"], out_a, out_b)

    perm = _perm_rows([d_out_a, delta_a, lse_a], dils, "perm_dil_bwd")
    aux = [(d_out_a, delta_a, lse_a)] + [tuple(perm[k * len(dils) + i] for k in range(3)) for i in range(len(dils))]
    d_qkv = []
    for g in range(3):
        dq, dk, dv = _dil_bwd(g, qkv[g], *aux[g])
        d_qkv.append(_rope_bwd(g, dq, dk, dv, tables))
    *d_fox, d_f_cols, d_f_rows = _fox_bwd(vr, f_q, f_k, lse_b, d_out_b, delta_b)
    d_z, d_b128 = _forget_bwd(fz, b128, d_f_cols, d_f_rows)

    grads["w_a_t"] = [_mm([(d_qkv[g], hs[g])], "tn", bf16, tm=QKV_W, tn=D, name=f"grad_w_a_{g}") for g in range(3)]
    grads["w_fox_t"] = [_mm([(d_fox[k], h1)], "tn", bf16, tm=FOX_W, tn=D, name=f"grad_w_fox_{k}") for k in range(3)]
    grads["w_gab_t"] = _mm([(d_gab, h1)], "tn", bf16, tm=D, tn=D, name="grad_w_gab")
    grads["w_f_t"] = _mm([(d_z, h1)], "tn", bf16, tm=128, tn=D, name="grad_w_f")
    d_h1_nat = _mm([(d_qkv[0], wt["w_a_t"][0])] + list(zip(d_fox, wt["w_fox_t"]))
                   + [(d_gab, wt["w_gab_t"]), (d_z, wt["w_f_t"])], "nn", f32, tm=512, tn=D, name="proj_in_bwd")
    d_h1_dil = [_mm([(d_qkv[g], wt["w_a_t"][g])], "nn", f32, tm=1024, tn=D, name=f"proj_a_bwd_{g}") for g in (1, 2)]
    d_h1 = _unperm_sum(d_h1_nat, d_h1_dil, dils, "unperm_d_h1")
    grad_x, dg_mix_pre = _norm_bwd_in(dx2, d_h1, x, g_mix_pre)

    small = dict(b_forget=d_b128[:, :N_FOX], norm_mix_pre=dg_mix_pre, norm_mix_post=dg_mix_post,
                 norm_ffn_pre=dg_ffn_pre, norm_ffn_post=dg_ffn_post)
    grads["mid_backward"] = d_qkv[0]
    return sq_err, grad_x, grads, small


NORMS = ("norm_mix_pre", "norm_mix_post", "norm_ffn_pre", "norm_ffn_post")
ORDER = ("w_in", "w_proj_a", "w_proj_b", "w_out", "b_forget", "w_ffn_gate", "w_ffn_up", "w_ffn_down") + NORMS


def kernel(x, w_in, w_proj_a, w_proj_b, w_out, b_forget, w_ffn_gate, w_ffn_up, w_ffn_down, norm_mix_pre, norm_mix_post, norm_ffn_pre, norm_ffn_post, loss_target, m_w_in, m_w_proj_a, m_w_proj_b, m_w_out, m_b_forget, m_w_ffn_gate, m_w_ffn_up, m_w_ffn_down, m_norm_mix_pre, m_norm_mix_post, m_norm_ffn_pre, m_norm_ffn_post, v_w_in, v_w_proj_a, v_w_proj_b, v_w_out, v_b_forget, v_w_ffn_gate, v_w_ffn_up, v_w_ffn_down, v_norm_mix_pre, v_norm_mix_post, v_norm_ffn_pre, v_norm_ffn_post):
    given = dict(w_in=w_in, w_proj_a=w_proj_a, w_proj_b=w_proj_b, w_out=w_out, w_ffn_gate=w_ffn_gate,
                 w_ffn_up=w_ffn_up, w_ffn_down=w_ffn_down)
    given_m = dict(w_in=m_w_in, w_proj_a=m_w_proj_a, w_proj_b=m_w_proj_b, w_out=m_w_out, w_ffn_gate=m_w_ffn_gate,
                   w_ffn_up=m_w_ffn_up, w_ffn_down=m_w_ffn_down)
    given_v = dict(w_in=v_w_in, w_proj_a=v_w_proj_a, w_proj_b=v_w_proj_b, w_out=v_w_out, w_ffn_gate=v_w_ffn_gate,
                   w_ffn_up=v_w_ffn_up, w_ffn_down=v_w_ffn_down)
    w, m, v = ({n: _kernel_layout(n, t[n]) for n in W_NAMES} for t in (given, given_m, given_v))
    small_w = dict(b_forget=b_forget, norm_mix_pre=norm_mix_pre, norm_mix_post=norm_mix_post,
                   norm_ffn_pre=norm_ffn_pre, norm_ffn_post=norm_ffn_post)
    small_m = dict(b_forget=m_b_forget, norm_mix_pre=m_norm_mix_pre, norm_mix_post=m_norm_mix_post,
                   norm_ffn_pre=m_norm_ffn_pre, norm_ffn_post=m_norm_ffn_post)
    small_v = dict(b_forget=v_b_forget, norm_mix_pre=v_norm_mix_pre, norm_mix_post=v_norm_mix_post,
                   norm_ffn_pre=v_norm_ffn_pre, norm_ffn_post=v_norm_ffn_post)

    own = [_halved(w[n].astype(bf16)) for n in W_NAMES]
    chip = 2 * lax.axis_index("x") + lax.axis_index("y")
    exchanged = {"first": _all_gather_async(own[:1], [], "all_gather_first", GATHER_FIRST_ID)}
    fill = lambda ts, mine: [lax.dynamic_update_index_in_dim(t, o, chip, 0) for t, o in zip(ts, mine)]

    def first_weights(ready):
        arrived, _ = lax.optimization_barrier((list(exchanged["first"]), ready))
        exchanged["late"] = _all_gather_async(own[1:], [arrived[0][0, 0, :16, :128]], "all_gather_late", GATHER_LATE_ID)
        return _full_weights(dict(zip(W_NAMES[:1], fill(arrived, own[:1]))))

    def late_weights(ready):
        arrived, _ = lax.optimization_barrier((list(exchanged["late"]), ready))
        return _full_weights(dict(zip(W_NAMES[1:], fill(arrived, own[1:]))))

    sq_err, grad_x, grads, small = _local_step(x[0], loss_target[0], first_weights, b_forget, norm_mix_pre,
                                               norm_mix_post, norm_ffn_pre, norm_ffn_post, late=late_weights)

    g4 = _sharded_grads(grads)
    stack = lambda t, extra: jnp.concatenate(
        [jnp.pad(t["b_forget"], ((0, 0), (0, D - N_FOX)))] + [t[n] for n in NORMS]
        + [jnp.pad(extra, ((0, SMALL_ROWS - LOSS_ROW - 1), (0, D - extra.shape[1])), constant_values=1.0)], axis=0)
    early, _ = lax.optimization_barrier((list(_pair_swap_early([g4[n] for n in W_NAMES[1:]])), grads["mid_backward"]))
    other = list(_pair_swap([g4["w_in"]])) + early
    parts = [_pair_sum(g4[n], o, "pair_sum_" + n) for n, o in zip(W_NAMES, other)]
    recv_early = _scatter_early(parts[1:])
    recv_in, small_all = _scatter_partials(parts[:1], stack(small, sq_err))

    g_shard, delta, new_m, new_v = {}, {}, {}, {}

    def finish(names, parts, recv):
        halves = [_sum_partials(p, r, "sum_partials_" + n) for n, p, r in zip(names, parts, recv)]
        theirs = _swap_halves(halves, "swap_halves_" + names[0])
        for n, mine, other_half in zip(names, halves, theirs):
            g_shard[n], delta[n], new_m[n], new_v[n] = _adamw_halves(w[n], mine, other_half, m[n], v[n], "adamw_" + n)

    recv_early, _ = lax.optimization_barrier((list(recv_early), parts[0]))
    finish(W_NAMES[1:], parts[1:], recv_early)
    (recv_in, small_all), _ = lax.optimization_barrier(((recv_in, small_all), [delta[n] for n in W_NAMES[1:]]))
    finish(W_NAMES[:1], parts[:1], [recv_in])
    small_sum = _sum_small(small_all)
    loss = small_sum[LOSS_ROW, 0] * (0.5 / D)
    ones = jnp.ones((1, 128), f32)
    sd, sm, sv = _adamw(stack(small_w, ones), small_sum, stack(small_m, ones), stack(small_v, ones), "adamw_small")

    outs = [loss, grad_x[None]]
    for big, st in ((g_shard, small_sum), (delta, sd), (new_m, sm), (new_v, sv)):
        t = {n: _harness_layout(n, big[n]) for n in W_NAMES}
        t["b_forget"] = st[0:1, :N_FOX]
        for i, n in enumerate(NORMS):
            t[n] = st[i + 1:i + 2]
        outs += [t[n] for n in ORDER]
    return tuple(outs)
```

```python
import functools
import math

import jax
import jax.numpy as jnp
import numpy as np
from jax import lax
from jax.experimental import pallas as pl
from jax.experimental.pallas import tpu as pltpu
from jax.experimental.pallas import tpu_sc as plsc

f32 = jnp.float32
bf16 = jnp.bfloat16
SDS = jax.ShapeDtypeStruct
MESH = pl.DeviceIdType.MESH

S = 2048
D = 1024
HD = 64
BLK = 128
N_FOX = 8
FOX_W = N_FOX * HD
DIL_GROUPS = ((128, 1), (512, 4), (2048, 16))
SLOTS = 4
DIL_W = SLOTS * HD
QKV_W = 3 * DIL_W
VR_W = 3 * FOX_W
GF_W = 2 * D + 128
F_FF = 2816
ROPE_DIM = 16
ROPE_THETA = 500000.0
EPS = 1e-6
NEG = -1e30
SCALE = 1.0 / math.sqrt(HD)
IN_COLS = 5896
N_SHARD = 4

ADAM_LR, ADAM_B1, ADAM_B2, ADAM_EPS, ADAM_WD, ADAM_STEP = 0.001, 0.9, 0.999, 1e-08, 0.01, 10

VMEM_V7X = 64 * 1024 * 1024
VMEM_PLAN_MAX = VMEM_V7X - 8 * 1024 * 1024

TM = 256
TQ = 256

W_NAMES = ("w_in", "w_proj_a", "w_proj_b", "w_out", "w_ffn_gate", "w_ffn_up", "w_ffn_down")
TRANSPOSED = ("w_in", "w_ffn_gate", "w_ffn_up")
IN_SHARD = IN_COLS // N_SHARD
IN_SHARD_PAD = 1504
SHARD_SHAPE = dict(w_in=(IN_SHARD_PAD, D), w_proj_a=(DIL_W, D // N_SHARD), w_proj_b=(FOX_W, D // N_SHARD),
                   w_out=(D // N_SHARD, D), w_ffn_gate=(F_FF // N_SHARD, D), w_ffn_up=(F_FF // N_SHARD, D),
                   w_ffn_down=(F_FF // N_SHARD, D))
SMALL_ROWS = 8
LOSS_ROW = 5


def _nbytes(shape, dtype):
    return math.prod(shape) * jnp.dtype(dtype).itemsize


def _params(semantics, block_bytes, temp_bytes=0):
    need = 2 * block_bytes + temp_bytes + (2 << 20)
    return pltpu.CompilerParams(dimension_semantics=semantics, vmem_limit_bytes=int(min(need, VMEM_PLAN_MAX)))


def _params_mxu(semantics, block_bytes, temp_bytes=0):
    assert 2 * block_bytes + temp_bytes <= VMEM_PLAN_MAX, (block_bytes, temp_bytes)
    return pltpu.CompilerParams(dimension_semantics=semantics, vmem_limit_bytes=VMEM_PLAN_MAX)


def _row(w, tm=TM):
    return pl.BlockSpec((tm, w), lambda i: (i, 0))


def _vec(w):
    return pl.BlockSpec((1, w), lambda i: (0, 0))


def _mm(pairs, dims, out_dtype, *, tm, tn, name, m_inner=False):
    a0, b0 = pairs[0]
    m_dim = a0.shape[1] if dims == "tn" else a0.shape[0]
    n_dim = b0.shape[0] if dims == "nt" else b0.shape[1]
    contract = {"nn": ((1,), (0,)), "nt": ((1,), (1,)), "tn": ((0,), (0,))}[dims]
    n_pairs = len(pairs)
    assert m_dim % tm == 0 and n_dim % tn == 0, (name, m_dim, n_dim, tm, tn)

    def body(*refs):
        o_ref = refs[-1]
        acc = None
        for p in range(n_pairs):
            a = refs[2 * p][...].astype(bf16)
            b = refs[2 * p + 1][...].astype(bf16)
            t = lax.dot_general(a, b, (contract, ((), ())), preferred_element_type=f32)
            acc = t if acc is None else acc + t
        o_ref[...] = acc.astype(o_ref.dtype)

    if m_inner:
        grid = (n_dim // tn, m_dim // tm)
        mi = lambda j, i: i
        ni = lambda j, i: j
    else:
        grid = (m_dim // tm, n_dim // tn)
        mi = lambda i, j: i
        ni = lambda i, j: j
    in_specs, block_bytes, args = [], 0, []
    for a, b in pairs:
        k_dim = a.shape[0] if dims == "tn" else a.shape[1]
        if dims == "tn":
            in_specs.append(pl.BlockSpec((k_dim, tm), lambda *g: (0, mi(*g))))
        else:
            in_specs.append(pl.BlockSpec((tm, k_dim), lambda *g: (mi(*g), 0)))
        if dims == "nt":
            in_specs.append(pl.BlockSpec((tn, k_dim), lambda *g: (ni(*g), 0)))
        else:
            in_specs.append(pl.BlockSpec((k_dim, tn), lambda *g: (0, ni(*g))))
        block_bytes += _nbytes((tm, k_dim), a.dtype) + _nbytes((tn, k_dim), b.dtype)
        args += [a, b]
    block_bytes += _nbytes((tm, tn), out_dtype)
    temp = _nbytes((tm, tn), f32) * 2 + sum(_nbytes((tm, a.shape[0] if dims == "tn" else a.shape[1]), bf16)
                                            + _nbytes((tn, a.shape[0] if dims == "tn" else a.shape[1]), bf16)
                                            for a, _ in pairs)
    return pl.pallas_call(
        body, grid=grid, in_specs=in_specs,
        out_specs=pl.BlockSpec((tm, tn), lambda *g: (mi(*g), ni(*g))),
        out_shape=SDS((m_dim, n_dim), out_dtype), name=name,
        compiler_params=_params_mxu(("parallel", "parallel"), block_bytes, temp),
    )(*args)


def _rms(x, g):
    r = lax.rsqrt(jnp.mean(x * x, axis=-1, keepdims=True) + EPS)
    return x * r * g


def _rms_bwd(x, g, dy):
    r = lax.rsqrt(jnp.mean(x * x, axis=-1, keepdims=True) + EPS)
    xh = x * r
    dxh = dy * g
    dx = r * (dxh - xh * jnp.mean(dxh * xh, axis=-1, keepdims=True))
    return dx, jnp.sum(dy * xh, axis=0, keepdims=True)


def _acc_rows(ref, val):
    @pl.when(pl.program_id(0) == 0)
    def _():
        ref[...] = jnp.zeros_like(ref)
    ref[...] += val


def _norm_fwd(xs, g):
    n = len(xs)

    def body(*refs):
        g = refs[n][...]
        for x_ref, h_ref in zip(refs[:n], refs[n + 1:]):
            h_ref[...] = _rms(x_ref[...], g).astype(bf16)

    return pl.pallas_call(
        body, grid=(S // TM,), in_specs=[_row(D)] * n + [_vec(D)], out_specs=[_row(D)] * n,
        out_shape=[SDS((S, D), bf16)] * n, name="norm_mix_pre",
        compiler_params=_params(("parallel",), 6 * n * TM * D, 8 * n * TM * D))(*xs, g)


def _perm_rows(xs, ds, name):
    n = len(xs)

    def body(*refs):
        outs = iter(refs[n:])
        for x_ref in refs[:n]:
            for d in ds:
                o_ref, rows = next(outs), S // d
                for r in range(d):
                    o_ref[r * rows:(r + 1) * rows, :] = x_ref[pl.ds(r, rows, stride=d), :]

    blk = pl.BlockSpec((S, 128), lambda c: (0, c))
    w = xs[0].shape[1]
    return pl.pallas_call(
        body, grid=(w // 128,), in_specs=[blk] * n, out_specs=[blk] * (n * len(ds)),
        out_shape=[SDS((S, w), f32)] * (n * len(ds)), name=name,
        compiler_params=_params(("parallel",), 4 * S * 128 * n * (1 + len(ds))))(*xs)


def _unperm_sum(nat, perms, ds, name):
    n = len(perms)

    def body(*refs):
        a_ref, o_ref, sc = refs[0], refs[n + 1], refs[n + 2]
        acc = a_ref[...]
        for b_ref, d in zip(refs[1:n + 1], ds):
            rows = S // d
            for r in range(d):
                sc[pl.ds(r, rows, stride=d), :] = b_ref[r * rows:(r + 1) * rows, :]
            acc = acc + sc[...]
        o_ref[...] = acc

    blk = pl.BlockSpec((S, 128), lambda c: (0, c))
    w = nat.shape[1]
    return pl.pallas_call(
        body, grid=(w // 128,), in_specs=[blk] * (n + 1), out_specs=blk, out_shape=SDS((S, w), f32),
        scratch_shapes=[pltpu.VMEM((S, 128), f32)], name=name,
        compiler_params=_params(("parallel",), 4 * S * 128 * (n + 2), 8 * S * 128))(nat, *perms)


def _whole(a):
    return pl.BlockSpec(a.shape, lambda i: (0,) * a.ndim)


def _resid_norm_fwd(x, merged, w_out, g_post, g_pre):
    def body(x_ref, mg_ref, w_ref, gp_ref, gn_ref, mix_ref, x2_ref, h_ref):
        mix = jnp.dot(mg_ref[...], w_ref[...], preferred_element_type=f32)
        x2 = x_ref[...] + _rms(mix, gp_ref[...])
        mix_ref[...] = mix
        x2_ref[...] = x2
        h_ref[...] = _rms(x2, gn_ref[...]).astype(bf16)

    return pl.pallas_call(
        body, grid=(S // TM,), in_specs=[_row(D), _row(D), _whole(w_out), _vec(D), _vec(D)], out_specs=[_row(D)] * 3,
        out_shape=[SDS((S, D), f32), SDS((S, D), f32), SDS((S, D), bf16)], name="proj_out_norm",
        compiler_params=_params_mxu(("parallel",), 16 * TM * D + 2 * D * D, 16 * TM * D))(x, merged, w_out, g_post, g_pre)


def _loss_head(x2, a_act, w_down, g_post, target):
    def body(x2_ref, a_ref, w_ref, g_ref, t_ref, loss_ref, dy_ref, dff_ref, dg_ref):
        ff = jnp.dot(a_ref[...], w_ref[...], preferred_element_type=f32)
        g = g_ref[...]
        err = x2_ref[...] + _rms(ff, g) - t_ref[...]
        dy = err * (1.0 / D)
        dff, dg = _rms_bwd(ff, g, dy)
        dy_ref[...] = dy
        dff_ref[...] = dff.astype(bf16)
        _acc_rows(dg_ref, dg)
        _acc_rows(loss_ref, jnp.full((1, 128), jnp.sum(err * err), f32))

    return pl.pallas_call(
        body, grid=(S // TM,), in_specs=[_row(D), _row(F_FF), _whole(w_down), _vec(D), _row(D)],
        out_specs=[_vec(128), _row(D), _row(D), _vec(D)],
        out_shape=[SDS((1, 128), f32), SDS((S, D), f32), SDS((S, D), bf16), SDS((1, D), f32)], name="ffn_down_loss",
        compiler_params=_params_mxu(("arbitrary",), 14 * TM * D + 2 * TM * F_FF + 2 * F_FF * D, 28 * TM * D),
    )(x2, a_act, w_down, g_post, target)


def _norm_bwd_mid(dy, d_g, d_u, w_gate_t, w_up_t, x2, mix, g_ffn_pre, g_mix_post):
    def body(dy_ref, dgt_ref, dut_ref, wg_ref, wu_ref, x2_ref, mix_ref, g3_ref, g2_ref, dx2_ref, dmix_ref, dg3_ref, dg2_ref):
        dh = jnp.dot(dgt_ref[...], wg_ref[...], preferred_element_type=f32)
        dh += jnp.dot(dut_ref[...], wu_ref[...], preferred_element_type=f32)
        d3, dg3 = _rms_bwd(x2_ref[...], g3_ref[...], dh)
        dx2 = dy_ref[...] + d3
        dmix, dg2 = _rms_bwd(mix_ref[...], g2_ref[...], dx2)
        dx2_ref[...] = dx2
        dmix_ref[...] = dmix.astype(bf16)
        _acc_rows(dg3_ref, dg3)
        _acc_rows(dg2_ref, dg2)

    return pl.pallas_call(
        body, grid=(S // TM,),
        in_specs=[_row(D), _row(F_FF), _row(F_FF), _whole(w_gate_t), _whole(w_up_t), _row(D), _row(D), _vec(D), _vec(D)],
        out_specs=[_row(D), _row(D), _vec(D), _vec(D)],
        out_shape=[SDS((S, D), f32), SDS((S, D), bf16), SDS((1, D), f32), SDS((1, D), f32)], name="ffn_bwd_in_norm",
        compiler_params=_params_mxu(("arbitrary",), 18 * TM * D + 4 * TM * F_FF + 4 * F_FF * D, 28 * TM * D),
    )(dy, d_g, d_u, w_gate_t, w_up_t, x2, mix, g_ffn_pre, g_mix_post)


def _norm_bwd_in(dx2, dh1, x, g):
    def body(dx2_ref, dh_ref, x_ref, g_ref, gx_ref, dg_ref):
        d1, dg = _rms_bwd(x_ref[...], g_ref[...], dh_ref[...])
        gx_ref[...] = dx2_ref[...] + d1
        _acc_rows(dg_ref, dg)

    return pl.pallas_call(
        body, grid=(S // TM,), in_specs=[_row(D)] * 3 + [_vec(D)], out_specs=[_row(D), _vec(D)],
        out_shape=[SDS((S, D), f32), SDS((1, D), f32)], name="norm_bwd_in",
        compiler_params=_params(("arbitrary",), 16 * TM * D, 16 * TM * D))(dx2, dh1, x, g)


def _rope_tables():
    half = ROPE_DIM // 2
    inv_freq = np.power(np.float32(ROPE_THETA), -np.arange(0, ROPE_DIM, 2, dtype=np.float32) / np.float32(ROPE_DIM))
    row = np.arange(S)
    groups = []
    for _, d in DIL_GROUPS:
        pos = ((row % (S // d)) * d + row // (S // d)).astype(np.float32)
        ang = pos[:, None] * inv_freq[None, :].astype(np.float32)
        cos, sin = np.cos(ang).astype(np.float32), np.sin(ang).astype(np.float32)
        c = np.concatenate([cos, cos, np.ones((S, HD - ROPE_DIM), np.float32)], axis=1)
        s_lo = np.concatenate([-sin, np.zeros((S, HD - half), np.float32)], axis=1)
        s_hi = np.concatenate([np.zeros((S, half), np.float32), sin, np.zeros((S, HD - ROPE_DIM), np.float32)], axis=1)
        groups.append(np.stack([np.concatenate([t, t], axis=1) for t in (c, s_lo, s_hi)]))
    return jnp.asarray(np.stack(groups))


def _rotate(x, c, lo, hi, sign):
    tile = lambda t: jnp.tile(t, (1, DIL_W // 128))
    return (x * tile(c) + pltpu.roll(x, DIL_W - ROPE_DIM // 2, 1) * (tile(lo) * sign)
            + pltpu.roll(x, ROPE_DIM // 2, 1) * (tile(hi) * sign))


def _table_specs(g):
    return [pl.BlockSpec((None, None, TM, 128), lambda i, k=k: (g, k, i, 0)) for k in range(3)]


def _rope_fwd(g, p_qkv, tables):
    def body(x_ref, c_ref, lo_ref, hi_ref, o_ref):
        c, lo, hi = c_ref[...], lo_ref[...], hi_ref[...]
        for part in range(2):
            cols = slice(part * DIL_W, (part + 1) * DIL_W)
            o_ref[:, cols] = _rotate(x_ref[:, cols], c, lo, hi, 1.0).astype(bf16)
        o_ref[:, 2 * DIL_W:] = x_ref[:, 2 * DIL_W:].astype(bf16)

    return pl.pallas_call(
        body, grid=(S // TM,), in_specs=[_row(QKV_W)] + _table_specs(g), out_specs=_row(QKV_W),
        out_shape=SDS((S, QKV_W), bf16), name=f"rope_fwd_{g}",
        compiler_params=_params(("parallel",), 6 * TM * QKV_W + 12 * TM * 128, 24 * TM * QKV_W))(p_qkv, tables, tables, tables)


def _rope_bwd(g, dq, dk, dv, tables):
    def body(dq_ref, dk_ref, dv_ref, c_ref, lo_ref, hi_ref, o_ref):
        c, lo, hi = c_ref[...], lo_ref[...], hi_ref[...]
        o_ref[:, :DIL_W] = _rotate(dq_ref[...], c, lo, hi, -1.0).astype(bf16)
        o_ref[:, DIL_W:2 * DIL_W] = _rotate(dk_ref[...], c, lo, hi, -1.0).astype(bf16)
        o_ref[:, 2 * DIL_W:] = dv_ref[...].astype(bf16)

    return pl.pallas_call(
        body, grid=(S // TM,), in_specs=[_row(DIL_W)] * 3 + _table_specs(g), out_specs=_row(QKV_W),
        out_shape=SDS((S, QKV_W), bf16), name=f"rope_bwd_{g}",
        compiler_params=_params(("parallel",), 6 * TM * QKV_W + 12 * TM * 128, 24 * TM * QKV_W))(dq, dk, dv, tables, tables, tables)


def _nt(a, b):
    return lax.dot_general(a, b, (((1,), (1,)), ((), ())), preferred_element_type=f32)


def _tn(a, b):
    return lax.dot_general(a, b, (((0,), (0,)), ((), ())), preferred_element_type=f32)


STEP_BLOCKS = 4
STEP_ROWS = STEP_BLOCKS * BLK


def _dil_prev(g, b):
    _, d = DIL_GROUPS[g]
    nb = S // d // BLK
    if nb == 1 or (b == 0 and nb <= STEP_BLOCKS):
        return None
    return "in" if b > 0 else "halo"


def _bnt(a, b):
    return lax.dot_general(a, b, (((2,), (2,)), ((0,), (0,))), preferred_element_type=f32)


def _bnn(a, b):
    return lax.dot_general(a, b, (((2,), (1,)), ((0,), (0,))), preferred_element_type=f32)


def _btn(a, b):
    return lax.dot_general(a, b, (((1,), (1,)), ((0,), (0,))), preferred_element_type=f32)


def _on_tail(x, tail, fn):
    if tail == x.shape[0]:
        return fn(x)
    return jnp.concatenate([x[:-tail], fn(x[-tail:])], axis=0)


def _heads(ref, part):
    n = ref.shape[0] // BLK
    return jnp.stack([ref[b * BLK:(b + 1) * BLK, part * DIL_W + h * HD:part * DIL_W + (h + 1) * HD]
                      for b in range(n) for h in range(SLOTS)])


def _dil_operands(g, qkv_ref, halo_ref):
    q, kc, vc = (_heads(qkv_ref, part) for part in range(3))
    qi = lax.broadcasted_iota(jnp.int32, (1, BLK, BLK), 1)
    kj = lax.broadcasted_iota(jnp.int32, (1, BLK, BLK), 2)
    with_prev = [b for b in range(STEP_BLOCKS) if _dil_prev(g, b) is not None]
    tail = SLOTS * len(with_prev)
    if not tail:
        return q, kc, vc, None, None, kj <= qi, None, 0
    assert with_prev == list(range(STEP_BLOCKS - len(with_prev), STEP_BLOCKS))
    inside = SLOTS * sum(_dil_prev(g, b) == "in" for b in with_prev)
    kp, vp, prev = kc[:inside], vc[:inside], jnp.broadcast_to(kj >= qi, (inside, BLK, BLK))
    if inside < tail:
        no_halo = jnp.where(pl.program_id(0) == 0, BLK + 1, 0)
        kp = jnp.concatenate([_heads(halo_ref, 1), kp], axis=0)
        vp = jnp.concatenate([_heads(halo_ref, 2), vp], axis=0)
        prev = jnp.concatenate([jnp.broadcast_to(kj >= qi + no_halo, (SLOTS, BLK, BLK)), prev], axis=0)
    return q, kc, vc, kp, vp, kj <= qi, prev, tail


def _dil_in_specs(g, n_aux):
    step = lambda w: pl.BlockSpec((STEP_ROWS, w), lambda i: (i, 0))
    halo = [pl.BlockSpec((BLK, QKV_W), lambda i: (jnp.maximum(i * STEP_BLOCKS - 1, 0), 0))]
    needs_halo = _dil_prev(g, 0) == "halo"
    return [step(QKV_W)] + (halo if needs_halo else []) + [step(DIL_W)] * n_aux, needs_halo


def _dil_fwd(g, qkv):
    in_specs, needs_halo = _dil_in_specs(g, 0)

    def body(*refs):
        qkv_ref, halo_ref = refs[0], refs[1] if needs_halo else None
        o_ref, lse_ref = refs[-2:]
        q, kc, vc, kp, vp, cur, prev, tail = _dil_operands(g, qkv_ref, halo_ref)
        sc = jnp.where(cur, _bnt(q, kc) * SCALE, NEG)
        m = jnp.max(sc, axis=-1, keepdims=True)
        if tail:
            sp = jnp.where(prev, _bnt(q[-tail:], kp) * SCALE, NEG)
            m = _on_tail(m, tail, lambda t: jnp.maximum(t, jnp.max(sp, axis=-1, keepdims=True)))
            pp = jnp.exp(sp - m[-tail:])
        pc = jnp.exp(sc - m)
        den = jnp.sum(pc, axis=-1, keepdims=True)
        if tail:
            den = _on_tail(den, tail, lambda t: t + jnp.sum(pp, axis=-1, keepdims=True))
        inv = 1.0 / den
        o = _bnn((pc * inv).astype(bf16), vc)
        if tail:
            o = _on_tail(o, tail, lambda t: t + _bnn((pp * inv[-tail:]).astype(bf16), vp))
        lse = m + jnp.log(den)
        for b in range(STEP_BLOCKS):
            for h in range(SLOTS):
                rows, hs = slice(b * BLK, (b + 1) * BLK), slice(h * HD, (h + 1) * HD)
                o_ref[rows, hs] = o[SLOTS * b + h]
                lse_ref[rows, hs] = jnp.broadcast_to(lse[SLOTS * b + h], (BLK, HD))

    out = pl.BlockSpec((STEP_ROWS, DIL_W), lambda i: (i, 0))
    return pl.pallas_call(
        body, grid=(S // STEP_ROWS,), in_specs=in_specs, out_specs=[out, out], out_shape=[SDS((S, DIL_W), f32)] * 2,
        name=f"dil_fwd_{g}", compiler_params=_params(("parallel",), 12 * STEP_ROWS * DIL_W, 2 << 20),
    )(*([qkv] * (2 if needs_halo else 1)))


def _dil_combine(outs, lses):
    def body(o0, o1, o2, l0, l1, l2, out_ref, lse_ref, so1, so2, sl1, sl2):
        for (_, d), src, dst in ((DIL_GROUPS[1], o1, so1), (DIL_GROUPS[2], o2, so2),
                                 (DIL_GROUPS[1], l1, sl1), (DIL_GROUPS[2], l2, sl2)):
            rows = S // d
            for r in range(d):
                dst[pl.ds(r, rows, stride=d), :] = src[r * rows:(r + 1) * rows, :]
        a, b, c = l0[...], sl1[...], sl2[...]
        m = jnp.maximum(jnp.maximum(a, b), c)
        ea, eb, ec = jnp.exp(a - m), jnp.exp(b - m), jnp.exp(c - m)
        z = ea + eb + ec
        inv = 1.0 / z
        out_ref[...] = (ea * inv) * o0[...] + (eb * inv) * so1[...] + (ec * inv) * so2[...]
        lse_ref[...] = m + jnp.log(z)

    blk = pl.BlockSpec((S, 128), lambda c: (0, c))
    return pl.pallas_call(
        body, grid=(DIL_W // 128,), in_specs=[blk] * 6, out_specs=[blk] * 2,
        out_shape=[SDS((S, DIL_W), f32)] * 2, scratch_shapes=[pltpu.VMEM((S, 128), f32)] * 4, name="dil_combine",
        compiler_params=_params(("parallel",), 32 * S * 128, 32 * S * 128))(*outs, *lses)


def _dil_bwd(g, qkv, d_out, delta, lse):
    in_specs, needs_halo = _dil_in_specs(g, 3)

    def body(*refs):
        qkv_ref, halo_ref = refs[0], refs[1] if needs_halo else None
        do_ref, dl_ref, lse_ref, dq_ref, dk_ref, dv_ref = refs[-6:]
        q, kc, vc, kp, vp, cur, prev, tail = _dil_operands(g, qkv_ref, halo_ref)
        tiles = [(slice(b * BLK, (b + 1) * BLK), h) for b in range(STEP_BLOCKS) for h in range(SLOTS)]
        do = jnp.stack([do_ref[rows, h * HD:(h + 1) * HD] for rows, h in tiles]).astype(bf16)
        lse = jnp.stack([lse_ref[rows, h * HD:h * HD + 1] for rows, h in tiles])
        delta = jnp.stack([dl_ref[rows, h * HD:h * HD + 1] for rows, h in tiles])

        def probs(q, k, mask, lse, do, v, delta):
            p = jnp.exp(jnp.where(mask, _bnt(q, k) * SCALE, NEG) - lse)
            ds = p * (_bnt(do, v) - delta) * SCALE
            return p.astype(bf16), ds.astype(bf16)

        p, ds = probs(q, kc, cur, lse, do, vc, delta)
        dq, dk, dv = _bnn(ds, kc), _btn(ds, q), _btn(p, do)
        if tail:
            p, ds = probs(q[-tail:], kp, prev, lse[-tail:], do[-tail:], vp, delta[-tail:])
            dq = _on_tail(dq, tail, lambda t: t + _bnn(ds, kp))
            dk_p, dv_p = _btn(ds, q[-tail:]), _btn(p, do[-tail:])
            inside = tail - SLOTS if needs_halo else tail
            pad = jnp.zeros((len(tiles) - inside, BLK, HD), f32)
            dk = dk + jnp.concatenate([dk_p[tail - inside:], pad], axis=0)
            dv = dv + jnp.concatenate([dv_p[tail - inside:], pad], axis=0)
        first = pl.multiple_of(pl.program_id(0) * STEP_ROWS, STEP_ROWS)
        for t, (rows, h) in enumerate(tiles):
            hs = slice(h * HD, (h + 1) * HD)
            own = pl.ds(pl.multiple_of(first + rows.start, BLK), BLK)
            dq_ref[rows, hs] = dq[t]
            dk_ref[own, hs] = dk[t]
            dv_ref[own, hs] = dv[t]
        if needs_halo:
            before = pl.ds(pl.multiple_of(jnp.maximum(first - BLK, 0), BLK), BLK)
            for h in range(SLOTS):
                hs = slice(h * HD, (h + 1) * HD)
                dk_ref[before, hs] += dk_p[h]
                dv_ref[before, hs] += dv_p[h]

    whole = pl.BlockSpec((S, DIL_W), lambda i: (0, 0))
    return pl.pallas_call(
        body, grid=(S // STEP_ROWS,), in_specs=in_specs,
        out_specs=[pl.BlockSpec((STEP_ROWS, DIL_W), lambda i: (i, 0)), whole, whole],
        out_shape=[SDS((S, DIL_W), f32)] * 3, name=f"dil_bwd_{g}",
        compiler_params=_params(("arbitrary",), 20 * STEP_ROWS * DIL_W + 8 * S * DIL_W, 2 << 20),
    )(*([qkv] * (2 if needs_halo else 1)), d_out, delta, lse)


def _scan_rows(x, reverse):
    row = lax.broadcasted_iota(jnp.int32, x.shape, 0)
    k = 1
    while k < S:
        if reverse:
            x = x + jnp.where(row < S - k, pltpu.roll(x, S - k, 0), 0.0)
        else:
            x = x + jnp.where(row >= k, pltpu.roll(x, k, 0), 0.0)
        k *= 2
    return x


N_PAIR = N_FOX // 2
_PAIR_Q = pl.BlockSpec((None, S, 128), lambda p: (p, 0, 0))
_PAIR_K = pl.BlockSpec((None, 8, S), lambda p: (p, 0, 0))


def _forget_fwd(fz, b128):
    def body(z_ref, b_ref, fq_ref, fk_ref):
        z = z_ref[...] + b_ref[...]
        logf = jnp.minimum(z, 0.0) - jnp.log1p(jnp.exp(-jnp.abs(z)))
        f_cum = _scan_rows(logf, reverse=False)
        f_cum_t = f_cum.T
        fq_ref[...] = jnp.zeros_like(fq_ref)
        fk_ref[...] = jnp.zeros_like(fk_ref)
        for p in range(N_PAIR):
            fq_ref[p, :, 0:2] = f_cum[:, 2 * p:2 * p + 2]
            fk_ref[p, 0:2, :] = f_cum_t[2 * p:2 * p + 2, :]

    return pl.pallas_call(
        body, grid=(1,), in_specs=[pl.BlockSpec((S, 128), lambda i: (0, 0)), _vec(128)],
        out_specs=[pl.BlockSpec((N_PAIR, S, 128), lambda i: (0, 0, 0)), pl.BlockSpec((N_PAIR, 8, S), lambda i: (0, 0, 0))],
        out_shape=[SDS((N_PAIR, S, 128), f32), SDS((N_PAIR, 8, S), f32)], name="forget_fwd",
        compiler_params=_params(("arbitrary",), 24 * S * 128, 24 * S * 128))(fz, b128)


def _forget_bwd(fz, b128, d_f_cols, d_f_rows):
    def body(z_ref, b_ref, dfc_ref, dfr_ref, dz_ref, db_ref, df_sc):
        z = z_ref[...] + b_ref[...]
        df_sc[...] = jnp.zeros_like(df_sc)
        for p in range(N_PAIR):
            df_sc[:, 2 * p:2 * p + 2] = dfr_ref[p, :, 0:2] + dfc_ref[p].T[:, 0:2]
        dz = _scan_rows(df_sc[...], reverse=True) * jax.nn.sigmoid(-z)
        dz_ref[...] = dz
        db_ref[...] = jnp.sum(dz, axis=0, keepdims=True)

    full = pl.BlockSpec((S, 128), lambda i: (0, 0))
    return pl.pallas_call(
        body, grid=(1,),
        in_specs=[full, _vec(128), pl.BlockSpec((N_PAIR, 8, S), lambda i: (0, 0, 0)), pl.BlockSpec((N_PAIR, S, 128), lambda i: (0, 0, 0))],
        out_specs=[full, _vec(128)], out_shape=[SDS((S, 128), f32), SDS((1, 128), f32)],
        scratch_shapes=[pltpu.VMEM((S, 128), f32)], name="forget_bwd",
        compiler_params=_params(("arbitrary",), 32 * S * 128, 24 * S * 128))(fz, b128, d_f_cols, d_f_rows)


def _fox_scores(q_ref, k_ref, fq_ref, fk_ref, qi, hh):
    n = (qi + 1) * TQ
    rows, hs = slice(qi * TQ, n), slice(hh * HD, (hh + 1) * HD)
    s = _nt(q_ref[rows, hs], k_ref[0:n, hs]) * SCALE + (fq_ref[rows, hh:hh + 1] - fk_ref[hh:hh + 1, 0:n])
    qpos = qi * TQ + lax.broadcasted_iota(jnp.int32, (TQ, n), 0)
    kpos = lax.broadcasted_iota(jnp.int32, (TQ, n), 1)
    return jnp.where(kpos <= qpos, s, NEG)


def _pair_cols(first):
    return pl.BlockSpec((S, 128), lambda p: (0, first + p))


def _fox_fwd(vr, fq, fk):
    def body(q_ref, k_ref, v_ref, fq_ref, fk_ref, o_ref, lse_ref):
        lse_ref[...] = jnp.zeros_like(lse_ref)
        for hh in range(2):
            hs = slice(hh * HD, (hh + 1) * HD)
            for qi in range(S // TQ):
                n = (qi + 1) * TQ
                rows = slice(qi * TQ, n)
                s = _fox_scores(q_ref, k_ref, fq_ref, fk_ref, qi, hh)
                m = jnp.max(s, axis=-1, keepdims=True)
                p = jnp.exp(s - m)
                den = jnp.sum(p, axis=-1, keepdims=True)
                o_ref[rows, hs] = jnp.dot((p * (1.0 / den)).astype(bf16), v_ref[0:n, hs], preferred_element_type=f32)
                lse_ref[rows, hh:hh + 1] = m + jnp.log(den)

    return pl.pallas_call(
        body, grid=(N_PAIR,), in_specs=[_pair_cols(0), _pair_cols(N_PAIR), _pair_cols(2 * N_PAIR), _PAIR_Q, _PAIR_K],
        out_specs=[_pair_cols(0), _PAIR_Q], out_shape=[SDS((S, FOX_W), f32), SDS((N_PAIR, S, 128), f32)],
        name="fox_fwd", compiler_params=_params_mxu(("parallel",), 12 * S * 128, 16 * TQ * S),
    )(vr, vr, vr, fq, fk)


def _fox_bwd(vr, fq, fk, lse, d_out, delta):
    def body(q_ref, k_ref, v_ref, do_ref, fq_ref, fk_ref, lse_ref, dl_ref, dq_ref, dk_ref, dv_ref, dfc_ref, dfr_ref,
             dk_sc, dv_sc):
        dfc_ref[...] = jnp.zeros_like(dfc_ref)
        dfr_ref[...] = jnp.zeros_like(dfr_ref)
        for hh in range(2):
            hs = slice(hh * HD, (hh + 1) * HD)
            dk_sc[...] = jnp.zeros_like(dk_sc)
            dv_sc[...] = jnp.zeros_like(dv_sc)
            for qi in range(S // TQ):
                n = (qi + 1) * TQ
                rows = slice(qi * TQ, n)
                q, do, k, v = q_ref[rows, hs], do_ref[rows, hs], k_ref[0:n, hs], v_ref[0:n, hs]
                p = jnp.exp(_fox_scores(q_ref, k_ref, fq_ref, fk_ref, qi, hh) - lse_ref[rows, hh:hh + 1])
                ds = p * (_nt(do, v) - dl_ref[rows, hh:hh + 1])
                dsb = ds.astype(bf16)
                dq_ref[rows, hs] = jnp.dot(dsb, k, preferred_element_type=f32) * SCALE
                dk_sc[0:n, :] += _tn(dsb, q) * SCALE
                dv_sc[0:n, :] += _tn(p.astype(bf16), do)
                dfc_ref[hh:hh + 1, 0:n] -= jnp.sum(ds, axis=0, keepdims=True)
                dfr_ref[rows, hh:hh + 1] = jnp.sum(ds, axis=-1, keepdims=True)
            dk_ref[:, hs] = dk_sc[...]
            dv_ref[:, hs] = dv_sc[...]

    cols = [_pair_cols(k * N_PAIR) for k in range(3)]
    return pl.pallas_call(
        body, grid=(N_PAIR,), in_specs=cols + [_pair_cols(0), _PAIR_Q, _PAIR_K, _PAIR_Q, _PAIR_Q],
        out_specs=[_pair_cols(0)] * 3 + [_PAIR_K, _PAIR_Q],
        out_shape=[SDS((S, FOX_W), f32)] * 3 + [SDS((N_PAIR, 8, S), f32), SDS((N_PAIR, S, 128), f32)],
        scratch_shapes=[pltpu.VMEM((S, HD), f32)] * 2, name="fox_bwd",
        compiler_params=_params_mxu(("parallel",), 32 * S * 128, 24 * TQ * S),
    )(vr, vr, vr, d_out, fq, fk, lse, delta)


def _merge_fwd(out_a, out_b, w_a, w_b, gf):
    cw = D // N_SHARD

    def body(oa_ref, ob_ref, wa_ref, wb_ref, ga_ref, gb_ref, ya_ref, yb_ref, mg_ref):
        oa, ob = oa_ref[...].astype(bf16), ob_ref[...].astype(bf16)
        for j in range(N_SHARD):
            cols = slice(j * cw, (j + 1) * cw)
            ya = jnp.dot(oa, wa_ref[j], preferred_element_type=f32)
            yb = jnp.dot(ob, wb_ref[j], preferred_element_type=f32)
            ya_ref[:, cols] = ya
            yb_ref[:, cols] = yb
            mg_ref[:, cols] = (jax.nn.sigmoid(ga_ref[:, cols]) * ya + jax.nn.sigmoid(gb_ref[:, cols]) * yb).astype(bf16)

    full = lambda a: pl.BlockSpec(a.shape, lambda i: (0, 0, 0))
    return pl.pallas_call(
        body, grid=(S // TM,),
        in_specs=[_row(DIL_W), _row(FOX_W), full(w_a), full(w_b), _row(D), pl.BlockSpec((TM, D), lambda i: (i, 1))],
        out_specs=[_row(D)] * 3, out_shape=[SDS((S, D), f32), SDS((S, D), f32), SDS((S, D), bf16)], name="merge_fwd",
        compiler_params=_params(("parallel",), 22 * TM * D + 2 * (DIL_W + FOX_W) * D, 16 * TM * D),
    )(out_a, out_b, w_a, w_b, gf, gf)


def _merge_bwd(d_mix, w_out, ya, yb, gf):
    def body(dx_ref, w_ref, ya_ref, yb_ref, ga_ref, gb_ref, dya_ref, dyb_ref, dg_ref):
        dm = _nt(dx_ref[...], w_ref[...])
        sa, sb = jax.nn.sigmoid(ga_ref[...]), jax.nn.sigmoid(gb_ref[...])
        dya_ref[...] = (dm * sa).astype(bf16)
        dyb_ref[...] = (dm * sb).astype(bf16)
        dg_ref[:, :D] = (dm * ya_ref[...] * sa * (1.0 - sa)).astype(bf16)
        dg_ref[:, D:] = (dm * yb_ref[...] * sb * (1.0 - sb)).astype(bf16)

    return pl.pallas_call(
        body, grid=(S // TM,),
        in_specs=[_row(D), _whole(w_out)] + [_row(D)] * 3 + [pl.BlockSpec((TM, D), lambda i: (i, 1))],
        out_specs=[_row(D), _row(D), _row(2 * D)],
        out_shape=[SDS((S, D), bf16), SDS((S, D), bf16), SDS((S, 2 * D), bf16)], name="proj_out_bwd_merge",
        compiler_params=_params_mxu(("parallel",), 26 * TM * D + 2 * D * D, 28 * TM * D))(d_mix, w_out, ya, yb, gf, gf)


def _branch_bwd(d_ya, d_yb, w_a, w_b, out_a, out_b):
    cw = D // N_SHARD

    def body(dya_ref, dyb_ref, wa_ref, wb_ref, oa_ref, ob_ref, doa_ref, dla_ref, dob_ref, dlb_ref):
        doa = jnp.zeros((TM, DIL_W), f32)
        dob = jnp.zeros((TM, FOX_W), f32)
        for j in range(N_SHARD):
            cols = slice(j * cw, (j + 1) * cw)
            doa += _nt(dya_ref[:, cols], wa_ref[j])
            dob += _nt(dyb_ref[:, cols], wb_ref[j])
        doa_ref[...] = doa
        dob_ref[...] = dob.astype(bf16)
        prod_a = doa * oa_ref[...]
        for h in range(SLOTS):
            hs = slice(h * HD, (h + 1) * HD)
            dla_ref[:, hs] = jnp.broadcast_to(jnp.sum(prod_a[:, hs], axis=-1, keepdims=True), (TM, HD))
        prod_b = dob * ob_ref[...]
        dlb_ref[...] = jnp.zeros_like(dlb_ref)
        for h in range(N_FOX):
            dlb_ref[h // 2, :, h % 2:h % 2 + 1] = jnp.sum(prod_b[:, h * HD:(h + 1) * HD], axis=-1, keepdims=True)

    full = lambda a: pl.BlockSpec(a.shape, lambda i: (0, 0, 0))
    return pl.pallas_call(
        body, grid=(S // TM,),
        in_specs=[_row(D), _row(D), full(w_a), full(w_b), _row(DIL_W), _row(FOX_W)],
        out_specs=[_row(DIL_W), _row(DIL_W), _row(FOX_W), pl.BlockSpec((N_PAIR, TM, 128), lambda i: (0, i, 0))],
        out_shape=[SDS((S, DIL_W), f32), SDS((S, DIL_W), f32), SDS((S, FOX_W), bf16), SDS((N_PAIR, S, 128), f32)],
        name="branch_bwd", compiler_params=_params_mxu(("parallel",), 8 * TM * D + 2 * (DIL_W + FOX_W) * D, 8 * TM * D),
    )(d_ya, d_yb, w_a, w_b, out_a, out_b)


def _branch_grads(out_a, out_b, d_ya, d_yb):
    cw = D // N_SHARD

    def body(oa_ref, ob_ref, dya_ref, dyb_ref, ga_ref, gb_ref):
        ga_ref[...] = _tn(oa_ref[...].astype(bf16), dya_ref[...]).astype(bf16)
        gb_ref[...] = _tn(ob_ref[...].astype(bf16), dyb_ref[...]).astype(bf16)

    whole = lambda w: pl.BlockSpec((S, w), lambda j: (0, 0))
    cols = pl.BlockSpec((S, cw), lambda j: (0, j))
    return pl.pallas_call(
        body, grid=(N_SHARD,), in_specs=[whole(DIL_W), whole(FOX_W), cols, cols],
        out_specs=[pl.BlockSpec((None, DIL_W, cw), lambda j: (j, 0, 0)), pl.BlockSpec((None, FOX_W, cw), lambda j: (j, 0, 0))],
        out_shape=[SDS((N_SHARD, DIL_W, cw), bf16), SDS((N_SHARD, FOX_W, cw), bf16)], name="grad_w_proj_ab",
        compiler_params=_params(("parallel",), 4 * S * (DIL_W + FOX_W) + 4 * S * cw + 4 * (DIL_W + FOX_W) * cw,
                                4 * S * (DIL_W + FOX_W)))(out_a, out_b, d_ya, d_yb)


FF_TN = F_FF // 2
FF_TM = 512


def _ffn_fwd(h, w_gate_t, w_up_t):
    def body(h_ref, wg_ref, wu_ref, g_ref, u_ref, a_ref):
        hb = h_ref[...]
        g = _nt(hb, wg_ref[...])
        u = _nt(hb, wu_ref[...])
        g_ref[...] = g
        u_ref[...] = u
        a_ref[...] = (g * jax.nn.sigmoid(g) * u).astype(bf16)

    tile = pl.BlockSpec((FF_TM, FF_TN), lambda j, i: (i, j))
    wspec = pl.BlockSpec((FF_TN, D), lambda j, i: (j, 0))
    return pl.pallas_call(
        body, grid=(F_FF // FF_TN, S // FF_TM),
        in_specs=[pl.BlockSpec((FF_TM, D), lambda j, i: (i, 0)), wspec, wspec], out_specs=[tile] * 3,
        out_shape=[SDS((S, F_FF), f32), SDS((S, F_FF), f32), SDS((S, F_FF), bf16)], name="ffn_fwd",
        compiler_params=_params_mxu(("parallel", "parallel"), 2 * FF_TM * D + 4 * D * FF_TN + 10 * FF_TM * FF_TN, 8 * FF_TM * FF_TN),
    )(h, w_gate_t, w_up_t)


def _ffn_bwd_act(d_ff, w_down, g_act, u_act):
    def body(d_ref, wd_ref, g_ref, u_ref, dg_ref, du_ref):
        da = _nt(d_ref[...], wd_ref[...])
        g = g_ref[...]
        sg = jax.nn.sigmoid(g)
        du_ref[...] = (da * g * sg).astype(bf16)
        dg_ref[...] = (da * u_ref[...] * sg * (1.0 + g * (1.0 - sg))).astype(bf16)

    tile = pl.BlockSpec((FF_TM, FF_TN), lambda j, i: (i, j))
    return pl.pallas_call(
        body, grid=(F_FF // FF_TN, S // FF_TM),
        in_specs=[pl.BlockSpec((FF_TM, D), lambda j, i: (i, 0)), pl.BlockSpec((FF_TN, D), lambda j, i: (j, 0)), tile, tile],
        out_specs=[tile, tile], out_shape=[SDS((S, F_FF), bf16)] * 2, name="ffn_bwd_act",
        compiler_params=_params_mxu(("parallel", "parallel"), 2 * FF_TM * D + 2 * D * FF_TN + 12 * FF_TM * FF_TN, 8 * FF_TM * FF_TN),
    )(d_ff, w_down, g_act, u_act)


def _row_tile(rows):
    return next(t for t in (376, 128, 176, 64, 32, 16, 8) if rows % t == 0)


def _adamw_math(w, g, m, v):
    c1 = 1.0 - ADAM_B1 ** ADAM_STEP
    c2 = 1.0 - ADAM_B2 ** ADAM_STEP
    m_new = ADAM_B1 * m + (1.0 - ADAM_B1) * g
    v_new = ADAM_B2 * v + (1.0 - ADAM_B2) * (g * g)
    return -ADAM_LR * ((m_new / c1) / (jnp.sqrt(v_new / c2) + ADAM_EPS) + ADAM_WD * w), m_new, v_new


def _adamw(w, g, m, v, name):
    rows, cols = w.shape
    tm = _row_tile(rows)

    def body(w_ref, g_ref, m_ref, v_ref, d_ref, nm_ref, nv_ref):
        d_ref[...], nm_ref[...], nv_ref[...] = _adamw_math(w_ref[...], g_ref[...], m_ref[...], v_ref[...])

    spec = pl.BlockSpec((tm, cols), lambda i: (i, 0))
    return pl.pallas_call(
        body, grid=(rows // tm,), in_specs=[spec] * 4, out_specs=[spec] * 3, out_shape=[SDS(w.shape, f32)] * 3,
        name=name, compiler_params=_params(("parallel",), 28 * tm * cols, 16 * tm * cols))(w, g, m, v)


def _adamw_halves(w, g_mine, g_theirs, m, v, name):
    rows, cols = w.shape
    tm = _row_tile(rows // 2)
    per_half = rows // 2 // tm
    core = lax.axis_index("c").astype(jnp.int32).reshape(1)

    def body(c_ref, w_ref, gm_ref, gt_ref, m_ref, v_ref, g_ref, d_ref, nm_ref, nv_ref):
        mine = pl.program_id(0) // per_half == c_ref[0]
        g = jnp.where(mine, gm_ref[...], gt_ref[...])
        g_ref[...] = g
        d_ref[...], nm_ref[...], nv_ref[...] = _adamw_math(w_ref[...], g, m_ref[...], v_ref[...])

    spec = pl.BlockSpec((tm, cols), lambda i, c_ref: (i, 0))
    in_half = lambda i, first: jnp.clip(i - first * per_half, 0, per_half - 1)
    grid_spec = pltpu.PrefetchScalarGridSpec(
        num_scalar_prefetch=1, grid=(rows // tm,),
        in_specs=[spec, pl.BlockSpec((tm, cols), lambda i, c_ref: (in_half(i, c_ref[0]), 0)),
                  pl.BlockSpec((tm, cols), lambda i, c_ref: (in_half(i, 1 - c_ref[0]), 0)), spec, spec],
        out_specs=[spec] * 4)
    return pl.pallas_call(
        body, grid_spec=grid_spec, out_shape=[SDS(w.shape, f32)] * 4, name=name,
        compiler_params=_params(("parallel",), 36 * tm * cols, 16 * tm * cols))(core, w, g_mine, g_theirs, m, v)


_ANY = pl.BlockSpec(memory_space=pl.ANY)


def _place():
    x, y, c = lax.axis_index("x"), lax.axis_index("y"), lax.axis_index("c")
    chips = [(1 - x, y), (x, 1 - y), (1 - x, 1 - y)]
    return x, y, c, chips


def _halved(t):
    return t.reshape(t.shape[:-2] + (2, t.shape[-2] // 2, t.shape[-1]))


def _gather_body(src, out, send_ici, recv_ici, send_d2d, recv_d2d):
    x, y, c, chips = _place()
    sibling = (x, y, 1 - c)
    me_j = 2 * x + y
    sends = []
    for a in range(len(src)):
        for p in range(3):
            cp = pltpu.make_async_remote_copy(
                src_ref=src[a].at[c], dst_ref=out[a].at[me_j, c], send_sem=send_ici.at[a, p],
                recv_sem=recv_ici.at[a, p], device_id=(*chips[p], c), device_id_type=MESH)
            cp.start()
            sends.append(cp)
    for a in range(len(src)):
        for p, (px, py) in enumerate(chips):
            blk = out[a].at[2 * px + py, c]
            pltpu.make_async_remote_copy(
                src_ref=blk, dst_ref=blk, send_sem=send_ici.at[a, p], recv_sem=recv_ici.at[a, p],
                device_id=sibling, device_id_type=MESH).wait_recv()
            fw = pltpu.make_async_remote_copy(
                src_ref=blk, dst_ref=blk, send_sem=send_d2d.at[a, p], recv_sem=recv_d2d.at[a, p],
                device_id=sibling, device_id_type=MESH)
            fw.start()
            sends.append(fw)
    for a in range(len(src)):
        for p, (px, py) in enumerate(chips):
            blk = out[a].at[2 * px + py, 1 - c]
            pltpu.make_async_remote_copy(
                src_ref=blk, dst_ref=blk, send_sem=send_d2d.at[a, p], recv_sem=recv_d2d.at[a, p],
                device_id=sibling, device_id_type=MESH).wait_recv()
    for cp in sends:
        cp.wait_send()


def _handshake(peers):
    barrier = pltpu.get_barrier_semaphore()
    for peer in peers:
        pl.semaphore_signal(barrier, inc=1, device_id=peer, device_id_type=MESH)
    pl.semaphore_wait(barrier, len(peers))


_SEQUENCER = dict(axis_name="sequencer", num_cores=1)
GATHER_LATE_ID, SCATTER_EARLY_ID, SWAP_EARLY_ID, GATHER_FIRST_ID, SCATTER_LATE_ID = 1, 2, 3, 4, 5


def _all_gather_async(shards, after, name, collective_id):
    n, k = len(shards), len(after)

    def body(*refs):
        x, y, c, chips = _place()
        _handshake([(*chip, c) for chip in chips] + [(x, y, 1 - c)])
        _gather_body(refs[:n], refs[n + k:2 * n + k], *refs[2 * n + k:])

    return pl.kernel(
        body, out_type=[SDS((N_SHARD,) + t.shape, t.dtype) for t in shards],
        mesh=plsc.ScalarSubcoreMesh(**_SEQUENCER), scratch_types=[pltpu.SemaphoreType.DMA((n, 3))] * 4,
        compiler_params=pltpu.CompilerParams(collective_id=collective_id), name=name)(*shards, *after)


def _pair_swap(grads):
    n = len(grads)

    def body(*refs):
        src, out, send_sems, recv_sems = refs[:n], refs[n:2 * n], refs[2 * n], refs[2 * n + 1]
        x, y, c, _ = _place()
        copies = [pltpu.make_async_remote_copy(
            src_ref=src[a].at[:, 1 - c], dst_ref=out[a], send_sem=send_sems.at[a], recv_sem=recv_sems.at[a],
            device_id=(x, y, 1 - c), device_id_type=MESH) for a in range(n)]
        for cp in copies:
            cp.start()
        for cp in copies:
            cp.wait()

    return pl.pallas_call(
        body, in_specs=[_ANY] * n, out_specs=[_ANY] * n,
        out_shape=[SDS((N_SHARD,) + t.shape[2:], t.dtype) for t in grads],
        scratch_shapes=[pltpu.SemaphoreType.DMA((n,)), pltpu.SemaphoreType.DMA((n,))], name="pair_swap",
        compiler_params=pltpu.CompilerParams(has_side_effects=True))(*grads)


def _pair_swap_early(grads):
    n = len(grads)

    def body(*refs):
        src, out, send_sems, recv_sems = refs[:n], refs[n:2 * n], refs[2 * n], refs[2 * n + 1]
        x, y, c, _ = _place()
        _handshake([(x, y, 1 - c)])
        copies = [pltpu.make_async_remote_copy(
            src_ref=src[a].at[:, 1 - c], dst_ref=out[a], send_sem=send_sems.at[a], recv_sem=recv_sems.at[a],
            device_id=(x, y, 1 - c), device_id_type=MESH) for a in range(n)]
        for cp in copies:
            cp.start()
        for cp in copies:
            cp.wait()

    return pl.kernel(
        body, out_type=[SDS((N_SHARD,) + t.shape[2:], t.dtype) for t in grads],
        mesh=plsc.ScalarSubcoreMesh(**_SEQUENCER), scratch_types=[pltpu.SemaphoreType.DMA((n,))] * 2,
        compiler_params=pltpu.CompilerParams(collective_id=SWAP_EARLY_ID), name="pair_swap_early")(*grads)


def _scatter_early(parts):
    n = len(parts)

    def body(*refs):
        part, recv, send_sems, recv_sems = refs[:n], refs[n:2 * n], refs[2 * n], refs[2 * n + 1]
        x, y, c, chips = _place()
        _handshake([(*chip, c) for chip in chips])
        me_j = 2 * x + y
        sends = []
        for a in range(n):
            for p, (px, py) in enumerate(chips):
                cp = pltpu.make_async_remote_copy(
                    src_ref=part[a].at[2 * px + py], dst_ref=recv[a].at[me_j], send_sem=send_sems.at[a, p],
                    recv_sem=recv_sems.at[a, p], device_id=(px, py, c), device_id_type=MESH)
                cp.start()
                sends.append(cp)
        for a in range(n):
            for p, (px, py) in enumerate(chips):
                slot = recv[a].at[2 * px + py]
                pltpu.make_async_remote_copy(
                    src_ref=slot, dst_ref=slot, send_sem=send_sems.at[a, p], recv_sem=recv_sems.at[a, p],
                    device_id=(px, py, c), device_id_type=MESH).wait_recv()
        for cp in sends:
            cp.wait_send()

    return pl.kernel(
        body, out_type=[SDS(t.shape, t.dtype) for t in parts],
        mesh=plsc.ScalarSubcoreMesh(**_SEQUENCER), scratch_types=[pltpu.SemaphoreType.DMA((n, 3))] * 2,
        compiler_params=pltpu.CompilerParams(collective_id=SCATTER_EARLY_ID), name="scatter_early")(*parts)


def _pair_sum(grads, other, name):
    _, _, rows, cols = grads.shape
    tr = _row_tile(rows)
    core = lax.axis_index("c").astype(jnp.int32).reshape(1)

    def body(c_ref, g_ref, o_ref, out_ref):
        out_ref[...] = (g_ref[...].astype(f32) + o_ref[...].astype(f32)).astype(bf16)

    grid_spec = pltpu.PrefetchScalarGridSpec(
        num_scalar_prefetch=1, grid=(N_SHARD, rows // tr),
        in_specs=[pl.BlockSpec((None, None, tr, cols), lambda j, i, c_ref: (j, c_ref[0], i, 0)),
                  pl.BlockSpec((None, tr, cols), lambda j, i, c_ref: (j, i, 0))],
        out_specs=pl.BlockSpec((None, tr, cols), lambda j, i, c_ref: (j, i, 0)))
    return pl.pallas_call(
        body, grid_spec=grid_spec, out_shape=SDS((N_SHARD, rows, cols), bf16), name=name,
        compiler_params=_params(("parallel", "parallel"), 10 * tr * cols, 12 * tr * cols))(core, grads, other)


def _scatter_partials(parts, small):
    n = len(parts)

    def body(*refs):
        part, small_ref, recv, small_all_ref = refs[:n], refs[n], refs[n + 1:2 * n + 1], refs[2 * n + 1]
        send_sems, recv_sems, ssend, srecv, local_sem = refs[2 * n + 2:]
        x, y, c, chips = _place()
        flip = lambda a, bit: 1 - a if bit else a
        peers = [(flip(x, k & 4), flip(y, k & 2), flip(c, k & 1)) for k in range(1, 8)]
        _handshake(peers)
        me_j = 2 * x + y
        me_dev = 4 * x + 2 * y + c
        own = pltpu.make_async_copy(small_ref, small_all_ref.at[me_dev], local_sem)
        own.start()
        sends = []
        for a in range(n):
            for p, (px, py) in enumerate(chips):
                cp = pltpu.make_async_remote_copy(
                    src_ref=part[a].at[2 * px + py], dst_ref=recv[a].at[me_j], send_sem=send_sems.at[a, p],
                    recv_sem=recv_sems.at[a, p], device_id=(px, py, c), device_id_type=MESH)
                cp.start()
                sends.append(cp)
        for k, to in enumerate(peers):
            cp = pltpu.make_async_remote_copy(
                src_ref=small_ref, dst_ref=small_all_ref.at[me_dev],
                send_sem=ssend.at[k], recv_sem=srecv.at[k], device_id=to, device_id_type=MESH)
            cp.start()
            sends.append(cp)
        for a in range(n):
            for p, (px, py) in enumerate(chips):
                slot = recv[a].at[2 * px + py]
                pltpu.make_async_remote_copy(
                    src_ref=slot, dst_ref=slot, send_sem=send_sems.at[a, p], recv_sem=recv_sems.at[a, p],
                    device_id=(px, py, c), device_id_type=MESH).wait_recv()
        for k, (px, py, pc) in enumerate(peers):
            slot = small_all_ref.at[4 * px + 2 * py + pc]
            pltpu.make_async_remote_copy(
                src_ref=slot, dst_ref=slot, send_sem=ssend.at[k], recv_sem=srecv.at[k],
                device_id=(px, py, pc), device_id_type=MESH).wait_recv()
        for cp in sends:
            cp.wait_send()
        own.wait()

    return pl.kernel(
        body, out_type=[SDS(t.shape, t.dtype) for t in parts] + [SDS((8, SMALL_ROWS, D), f32)],
        mesh=plsc.ScalarSubcoreMesh(**_SEQUENCER),
        scratch_types=[pltpu.SemaphoreType.DMA((n, 3)), pltpu.SemaphoreType.DMA((n, 3)),
                       pltpu.SemaphoreType.DMA((7,)), pltpu.SemaphoreType.DMA((7,)), pltpu.SemaphoreType.DMA],
        compiler_params=pltpu.CompilerParams(collective_id=SCATTER_LATE_ID), name="scatter_partials")(*parts, small)


def _sum_partials(part, recv, name):
    _, rows, cols = recv.shape
    tr = _row_tile(rows)
    me = (2 * lax.axis_index("x") + lax.axis_index("y")).astype(jnp.int32).reshape(1)

    def body(me_ref, mine, r0, r1, r2, r3, out_ref):
        acc = None
        for j, r in enumerate((r0, r1, r2, r3)):
            term = jnp.where(me_ref[0] == j, mine[...], r[...]).astype(f32)
            acc = term if acc is None else acc + term
        out_ref[...] = acc

    slot = lambda j: pl.BlockSpec((None, tr, cols), lambda i, me_ref: (jnp.where(me_ref[0] == j, j ^ 1, j), i, 0))
    grid_spec = pltpu.PrefetchScalarGridSpec(
        num_scalar_prefetch=1, grid=(rows // tr,),
        in_specs=[pl.BlockSpec((None, tr, cols), lambda i, me_ref: (me_ref[0], i, 0)), slot(0), slot(1), slot(2), slot(3)],
        out_specs=pl.BlockSpec((tr, cols), lambda i, me_ref: (i, 0)))
    return pl.pallas_call(
        body, grid_spec=grid_spec, out_shape=SDS((rows, cols), f32), name=name,
        compiler_params=_params(("parallel",), 14 * tr * cols, 12 * tr * cols))(me, part, recv, recv, recv, recv)


def _sum_small(small_all):
    def body(small_ref, out_ref):
        tot = small_ref[0]
        for k in range(1, 8):
            tot = tot + small_ref[k]
        out_ref[...] = tot

    return pl.pallas_call(
        body, grid=(1,), in_specs=[pl.BlockSpec((8, SMALL_ROWS, D), lambda i: (0, 0, 0))],
        out_specs=pl.BlockSpec((SMALL_ROWS, D), lambda i: (0, 0)), out_shape=SDS((SMALL_ROWS, D), f32),
        name="sum_small", compiler_params=_params(("arbitrary",), 36 * SMALL_ROWS * D))(small_all)


def _swap_halves(halves, name):
    n = len(halves)

    def body(*refs):
        src, out, send_sems, recv_sems = refs[:n], refs[n:2 * n], refs[2 * n], refs[2 * n + 1]
        x, y, c, _ = _place()
        copies = [pltpu.make_async_remote_copy(
            src_ref=src[a], dst_ref=out[a], send_sem=send_sems.at[a], recv_sem=recv_sems.at[a],
            device_id=(x, y, 1 - c), device_id_type=MESH) for a in range(n)]
        for cp in copies:
            cp.start()
        for cp in copies:
            cp.wait()

    return pl.pallas_call(
        body, in_specs=[_ANY] * n, out_specs=[_ANY] * n, out_shape=[SDS(t.shape, f32) for t in halves],
        scratch_shapes=[pltpu.SemaphoreType.DMA((n,))] * 2, name=name,
        compiler_params=pltpu.CompilerParams(has_side_effects=True))(*halves)


def _kernel_layout(name, t):
    t = t[0]
    if name in TRANSPOSED:
        t = jnp.swapaxes(t, 0, 1)
    return _pad_rows(t, SHARD_SHAPE[name][0])


def _harness_layout(name, t):
    if name == "w_in":
        t = t[:IN_SHARD]
    if name in TRANSPOSED:
        t = jnp.swapaxes(t, 0, 1)
    return t[None]


def _pad_rows(t, rows):
    return t if t.shape[0] == rows else jnp.pad(t, ((0, rows - t.shape[0]), (0, 0)))


_QA, _KA, _VA, _QB, _F, _GAB = 0, 768, 1536, 2304, 3840, 3848


def _spans(a, b):
    return [(j, max(a, j * IN_SHARD) - j * IN_SHARD, max(a, j * IN_SHARD) - a,
             min(b, (j + 1) * IN_SHARD) - max(a, j * IN_SHARD))
            for j in range(N_SHARD) if max(a, j * IN_SHARD) < min(b, (j + 1) * IN_SHARD)]


_LANES = pl.BlockSpec((N_SHARD, IN_SHARD_PAD, 128), lambda c: (0, 0, c))


def _split_w_in(shards):
    group = [[(o + g * DIL_W, o + (g + 1) * DIL_W) for o in (_QA, _KA, _VA)] for g in range(3)]
    fox = [[(_QB + k * FOX_W, _QB + (k + 1) * FOX_W)] for k in range(3)]
    wanted = group + fox + [[(_QB, _F)], [(_F, _GAB)], [(_GAB, IN_COLS)]]
    rows = [sum(b - a for a, b in w) for w in wanted]
    rows[7] = 128

    def body(s_ref, *o_refs):
        for o_ref, want in zip(o_refs, wanted):
            at = 0
            for a, b in want:
                for j, src, off, n in _spans(a, b):
                    o_ref[at + off:at + off + n, :] = s_ref[j, src:src + n, :]
                at += b - a
        o_refs[7][N_FOX:, :] = jnp.zeros((128 - N_FOX, 128), bf16)

    return pl.pallas_call(
        body, grid=(D // 128,), in_specs=[_LANES], out_specs=[pl.BlockSpec((r, 128), lambda c: (0, c)) for r in rows],
        out_shape=[SDS((r, D), bf16) for r in rows], name="split_w_in",
        compiler_params=_params(("parallel",), 2 * 128 * (N_SHARD * IN_SHARD_PAD + sum(rows))))(shards)


def _join_w_in(g_a, g_fox, g_f, g_gab):
    parts = [(g_a[k], o, o + DIL_W) for o in (0, DIL_W, 2 * DIL_W) for k in range(3)]
    parts += [(t, 0, FOX_W) for t in g_fox] + [(g_f, 0, N_FOX), (g_gab, 0, 2 * D)]
    arrays = list(g_a) + list(g_fox) + [g_f, g_gab]
    index = {id(t): i for i, t in enumerate(arrays)}

    def body(*refs):
        o_ref = refs[-1]
        o_ref[:, IN_SHARD:, :] = jnp.zeros((N_SHARD, IN_SHARD_PAD - IN_SHARD, 128), bf16)
        at = 0
        for t, lo, hi in parts:
            src_ref = refs[index[id(t)]]
            for j, dst, off, n in _spans(at, at + hi - lo):
                o_ref[j, dst:dst + n, :] = src_ref[lo + off:lo + off + n, :].astype(bf16)
            at += hi - lo

    return pl.pallas_call(
        body, grid=(D // 128,), in_specs=[pl.BlockSpec((t.shape[0], 128), lambda c: (0, c)) for t in arrays],
        out_specs=_LANES, out_shape=SDS((N_SHARD, IN_SHARD_PAD, D), bf16), name="join_w_in",
        compiler_params=_params(("parallel",), 2 * 128 * (N_SHARD * IN_SHARD_PAD + sum(t.shape[0] for t in arrays))),
    )(*arrays)


def _full_weights(gathered):
    full = {n: t.reshape((N_SHARD,) + SHARD_SHAPE[n]) for n, t in gathered.items()}
    out = {}
    if "w_in" in full:
        pieces = _split_w_in(full["w_in"])
        out.update(w_a_t=pieces[0:3], w_fox_t=pieces[3:6], w_vr_t=pieces[6], w_f_t=pieces[7], w_gab_t=pieces[8])
    if "w_out" in full:
        out.update(
            w_a4=full["w_proj_a"],
            w_b4=full["w_proj_b"],
            w_out=full["w_out"].reshape(D, D),
            w_gate_t=full["w_ffn_gate"].reshape(F_FF, D),
            w_up_t=full["w_ffn_up"].reshape(F_FF, D),
            w_down=full["w_ffn_down"].reshape(F_FF, D))
    return out


def _sharded_grads(g):
    full = dict(w_in=_join_w_in(g["w_a_t"], g["w_fox_t"], g["w_f_t"], g["w_gab_t"]), w_proj_a=g["w_a4"],
                w_proj_b=g["w_b4"], w_out=g["w_out"], w_ffn_gate=g["w_gate_t"], w_ffn_up=g["w_up_t"],
                w_ffn_down=g["w_down"])
    return {n: _halved(full[n].reshape((N_SHARD,) + SHARD_SHAPE[n])) for n in W_NAMES}


def _local_step(x, target, wt, b_forget, g_mix_pre, g_mix_post, g_ffn_pre, g_ffn_post, late=None):
    tables = _rope_tables()
    b128 = jnp.pad(b_forget, ((0, 0), (0, 128 - N_FOX)))
    dils = tuple(d for _, d in DIL_GROUPS[1:])

    hs = _norm_fwd([x] + list(_perm_rows([x], dils, "perm_x")), g_mix_pre)
    h1 = hs[0]
    if callable(wt):
        wt = wt(h1)
    qkv = [_rope_fwd(g, _mm([(hs[g], wt["w_a_t"][g])], "nt", f32, tm=1024, tn=QKV_W, name=f"proj_a_{g}"), tables)
           for g in range(3)]
    vr = _mm([(h1, wt["w_vr_t"])], "nt", bf16, tm=1024, tn=VR_W // 2, name="proj_vr")
    gab = _mm([(h1, wt["w_gab_t"])], "nt", f32, tm=512, tn=2 * D, name="proj_gab")
    fz = _mm([(h1, wt["w_f_t"])], "nt", f32, tm=1024, tn=128, name="proj_f")
    dil = [_dil_fwd(g, qkv[g]) for g in range(3)]
    out_a, lse_a = _dil_combine([o for o, _ in dil], [l for _, l in dil])
    f_q, f_k = _forget_fwd(fz, b128)
    out_b, lse_b = _fox_fwd(vr, f_q, f_k)
    if late is not None:
        wt = {**wt, **late(out_b)}
    ya, yb, merged = _merge_fwd(out_a, out_b, wt["w_a4"], wt["w_b4"], gab)
    mix, x2, h3 = _resid_norm_fwd(x, merged, wt["w_out"], g_mix_post, g_ffn_pre)
    g_act, u_act, a_act = _ffn_fwd(h3, wt["w_gate_t"], wt["w_up_t"])
    sq_err, dy, d_ff, dg_ffn_post = _loss_head(x2, a_act, wt["w_down"], g_ffn_post, target)

    grads = {}
    d_g, d_u = _ffn_bwd_act(d_ff, wt["w_down"], g_act, u_act)
    grads["w_down"] = _mm([(a_act, d_ff)], "tn", bf16, tm=FF_TN, tn=512, name="grad_w_down")
    grads["w_gate_t"] = _mm([(d_g, h3)], "tn", bf16, tm=FF_TN, tn=512, name="grad_w_gate")
    grads["w_up_t"] = _mm([(d_u, h3)], "tn", bf16, tm=FF_TN, tn=512, name="grad_w_up")
    dx2, d_mix, dg_ffn_pre, dg_mix_post = _norm_bwd_mid(dy, d_g, d_u, wt["w_gate_t"], wt["w_up_t"], x2, mix,
                                                        g_ffn_pre, g_mix_post)

    grads["w_out"] = _mm([(merged, d_mix)], "tn", bf16, tm=D, tn=D, name="grad_w_out")
    d_ya, d_yb, d_gab = _merge_bwd(d_mix, wt["w_out"], ya, yb, gab)
    grads["w_a4"], grads["w_b4"] = _branch_grads(out_a, out_b, d_ya, d_yb)
    d_out_a, delta_a, d_out_b, delta_b = _branch_bwd(d_ya, d_yb, wt["w_a4"], wt["w_b4"], out_a, out_b)

    perm = _perm_rows([d_out_a, delta_a, lse_a], dils, "perm_dil_bwd")
    aux = [(d_out_a, delta_a, lse_a)] + [tuple(perm[k * len(dils) + i] for k in range(3)) for i in range(len(dils))]
    d_qkv = []
    for g in range(3):
        dq, dk, dv = _dil_bwd(g, qkv[g], *aux[g])
        d_qkv.append(_rope_bwd(g, dq, dk, dv, tables))
    *d_fox, d_f_cols, d_f_rows = _fox_bwd(vr, f_q, f_k, lse_b, d_out_b, delta_b)
    d_z, d_b128 = _forget_bwd(fz, b128, d_f_cols, d_f_rows)

    grads["w_a_t"] = [_mm([(d_qkv[g], hs[g])], "tn", bf16, tm=QKV_W, tn=D, name=f"grad_w_a_{g}") for g in range(3)]
    grads["w_fox_t"] = [_mm([(d_fox[k], h1)], "tn", bf16, tm=FOX_W, tn=D, name=f"grad_w_fox_{k}") for k in range(3)]
    grads["w_gab_t"] = _mm([(d_gab, h1)], "tn", bf16, tm=D, tn=D, name="grad_w_gab")
    grads["w_f_t"] = _mm([(d_z, h1)], "tn", bf16, tm=128, tn=D, name="grad_w_f")
    d_h1_nat = _mm([(d_qkv[0], wt["w_a_t"][0])] + list(zip(d_fox, wt["w_fox_t"]))
                   + [(d_gab, wt["w_gab_t"]), (d_z, wt["w_f_t"])], "nn", f32, tm=512, tn=512, name="proj_in_bwd")
    d_h1_dil = [_mm([(d_qkv[g], wt["w_a_t"][g])], "nn", f32, tm=1024, tn=D, name=f"proj_a_bwd_{g}") for g in (1, 2)]
    d_h1 = _unperm_sum(d_h1_nat, d_h1_dil, dils, "unperm_d_h1")
    grad_x, dg_mix_pre = _norm_bwd_in(dx2, d_h1, x, g_mix_pre)

    small = dict(b_forget=d_b128[:, :N_FOX], norm_mix_pre=dg_mix_pre, norm_mix_post=dg_mix_post,
                 norm_ffn_pre=dg_ffn_pre, norm_ffn_post=dg_ffn_post)
    grads["mid_backward"] = d_qkv[0]
    return sq_err, grad_x, grads, small


NORMS = ("norm_mix_pre", "norm_mix_post", "norm_ffn_pre", "norm_ffn_post")
ORDER = ("w_in", "w_proj_a", "w_proj_b", "w_out", "b_forget", "w_ffn_gate", "w_ffn_up", "w_ffn_down") + NORMS


def kernel(x, w_in, w_proj_a, w_proj_b, w_out, b_forget, w_ffn_gate, w_ffn_up, w_ffn_down, norm_mix_pre, norm_mix_post, norm_ffn_pre, norm_ffn_post, loss_target, m_w_in, m_w_proj_a, m_w_proj_b, m_w_out, m_b_forget, m_w_ffn_gate, m_w_ffn_up, m_w_ffn_down, m_norm_mix_pre, m_norm_mix_post, m_norm_ffn_pre, m_norm_ffn_post, v_w_in, v_w_proj_a, v_w_proj_b, v_w_out, v_b_forget, v_w_ffn_gate, v_w_ffn_up, v_w_ffn_down, v_norm_mix_pre, v_norm_mix_post, v_norm_ffn_pre, v_norm_ffn_post):
    given = dict(w_in=w_in, w_proj_a=w_proj_a, w_proj_b=w_proj_b, w_out=w_out, w_ffn_gate=w_ffn_gate,
                 w_ffn_up=w_ffn_up, w_ffn_down=w_ffn_down)
    given_m = dict(w_in=m_w_in, w_proj_a=m_w_proj_a, w_proj_b=m_w_proj_b, w_out=m_w_out, w_ffn_gate=m_w_ffn_gate,
                   w_ffn_up=m_w_ffn_up, w_ffn_down=m_w_ffn_down)
    given_v = dict(w_in=v_w_in, w_proj_a=v_w_proj_a, w_proj_b=v_w_proj_b, w_out=v_w_out, w_ffn_gate=v_w_ffn_gate,
                   w_ffn_up=v_w_ffn_up, w_ffn_down=v_w_ffn_down)
    w, m, v = ({n: _kernel_layout(n, t[n]) for n in W_NAMES} for t in (given, given_m, given_v))
    small_w = dict(b_forget=b_forget, norm_mix_pre=norm_mix_pre, norm_mix_post=norm_mix_post,
                   norm_ffn_pre=norm_ffn_pre, norm_ffn_post=norm_ffn_post)
    small_m = dict(b_forget=m_b_forget, norm_mix_pre=m_norm_mix_pre, norm_mix_post=m_norm_mix_post,
                   norm_ffn_pre=m_norm_ffn_pre, norm_ffn_post=m_norm_ffn_post)
    small_v = dict(b_forget=v_b_forget, norm_mix_pre=v_norm_mix_pre, norm_mix_post=v_norm_mix_post,
                   norm_ffn_pre=v_norm_ffn_pre, norm_ffn_post=v_norm_ffn_post)

    own = [_halved(w[n].astype(bf16)) for n in W_NAMES]
    chip = 2 * lax.axis_index("x") + lax.axis_index("y")
    exchanged = {"first": _all_gather_async(own[:1], [], "all_gather_first", GATHER_FIRST_ID)}
    fill = lambda ts, mine: [lax.dynamic_update_index_in_dim(t, o, chip, 0) for t, o in zip(ts, mine)]

    def first_weights(ready):
        arrived, _ = lax.optimization_barrier((list(exchanged["first"]), ready))
        exchanged["late"] = _all_gather_async(own[1:], [arrived[0][0, 0, :16, :128]], "all_gather_late", GATHER_LATE_ID)
        return _full_weights(dict(zip(W_NAMES[:1], fill(arrived, own[:1]))))

    def late_weights(ready):
        arrived, _ = lax.optimization_barrier((list(exchanged["late"]), ready))
        return _full_weights(dict(zip(W_NAMES[1:], fill(arrived, own[1:]))))

    sq_err, grad_x, grads, small = _local_step(x[0], loss_target[0], first_weights, b_forget, norm_mix_pre,
                                               norm_mix_post, norm_ffn_pre, norm_ffn_post, late=late_weights)

    g4 = _sharded_grads(grads)
    stack = lambda t, extra: jnp.concatenate(
        [jnp.pad(t["b_forget"], ((0, 0), (0, D - N_FOX)))] + [t[n] for n in NORMS]
        + [jnp.pad(extra, ((0, SMALL_ROWS - LOSS_ROW - 1), (0, D - extra.shape[1])), constant_values=1.0)], axis=0)
    early, _ = lax.optimization_barrier((list(_pair_swap_early([g4[n] for n in W_NAMES[1:]])), grads["mid_backward"]))
    other = list(_pair_swap([g4["w_in"]])) + early
    parts = [_pair_sum(g4[n], o, "pair_sum_" + n) for n, o in zip(W_NAMES, other)]
    recv_early = _scatter_early(parts[1:])
    recv_in, small_all = _scatter_partials(parts[:1], stack(small, sq_err))

    g_shard, delta, new_m, new_v = {}, {}, {}, {}

    def finish(names, parts, recv):
        halves = [_sum_partials(p, r, "sum_partials_" + n) for n, p, r in zip(names, parts, recv)]
        theirs = _swap_halves(halves, "swap_halves_" + names[0])
        for n, mine, other_half in zip(names, halves, theirs):
            g_shard[n], delta[n], new_m[n], new_v[n] = _adamw_halves(w[n], mine, other_half, m[n], v[n], "adamw_" + n)

    recv_early, _ = lax.optimization_barrier((list(recv_early), parts[0]))
    finish(W_NAMES[1:], parts[1:], recv_early)
    (recv_in, small_all), _ = lax.optimization_barrier(((recv_in, small_all), [delta[n] for n in W_NAMES[1:]]))
    finish(W_NAMES[:1], parts[:1], [recv_in])
    small_sum = _sum_small(small_all)
    loss = small_sum[LOSS_ROW, 0] * (0.5 / D)
    ones = jnp.ones((1, 128), f32)
    sd, sm, sv = _adamw(stack(small_w, ones), small_sum, stack(small_m, ones), stack(small_v, ones), "adamw_small")

    outs = [loss, grad_x[None]]
    for big, st in ((g_shard, small_sum), (delta, sd), (new_m, sm), (new_v, sv)):
        t = {n: _harness_layout(n, big[n]) for n in W_NAMES}
        t["b_forget"] = st[0:1, :N_FOX]
        for i, n in enumerate(NORMS):
            t[n] = st[i + 1:i + 2]
        outs += [t[n] for n in ORDER]
    return tuple(outs)
```

```python
import functools
import math

import jax
import jax.numpy as jnp
import numpy as np
from jax import lax
from jax.experimental import pallas as pl
from jax.experimental.pallas import tpu as pltpu
from jax.experimental.pallas import tpu_sc as plsc

f32 = jnp.float32
bf16 = jnp.bfloat16
SDS = jax.ShapeDtypeStruct
MESH = pl.DeviceIdType.MESH

S = 2048
D = 1024
HD = 64
BLK = 128
N_FOX = 8
FOX_W = N_FOX * HD
DIL_GROUPS = ((128, 1), (512, 4), (2048, 16))
SLOTS = 4
DIL_W = SLOTS * HD
QKV_W = 3 * DIL_W
VR_W = 3 * FOX_W
GF_W = 2 * D + 128
F_FF = 2816
ROPE_DIM = 16
ROPE_THETA = 500000.0
EPS = 1e-6
NEG = -1e30
SCALE = 1.0 / math.sqrt(HD)
IN_COLS = 5896
N_SHARD = 4

ADAM_LR, ADAM_B1, ADAM_B2, ADAM_EPS, ADAM_WD, ADAM_STEP = 0.001, 0.9, 0.999, 1e-08, 0.01, 10

VMEM_V7X = 64 * 1024 * 1024
VMEM_PLAN_MAX = VMEM_V7X - 8 * 1024 * 1024

TM = 256
TQ = 256

W_NAMES = ("w_in", "w_proj_a", "w_proj_b", "w_out", "w_ffn_gate", "w_ffn_up", "w_ffn_down")
TRANSPOSED = ("w_in", "w_ffn_gate", "w_ffn_up")
IN_SHARD = IN_COLS // N_SHARD
IN_SHARD_PAD = 1504
SHARD_SHAPE = dict(w_in=(IN_SHARD_PAD, D), w_proj_a=(DIL_W, D // N_SHARD), w_proj_b=(FOX_W, D // N_SHARD),
                   w_out=(D // N_SHARD, D), w_ffn_gate=(F_FF // N_SHARD, D), w_ffn_up=(F_FF // N_SHARD, D),
                   w_ffn_down=(F_FF // N_SHARD, D))
SMALL_ROWS = 8
LOSS_ROW = 5


def _nbytes(shape, dtype):
    return math.prod(shape) * jnp.dtype(dtype).itemsize


def _params(semantics, block_bytes, temp_bytes=0):
    need = 2 * block_bytes + temp_bytes + (2 << 20)
    return pltpu.CompilerParams(dimension_semantics=semantics, vmem_limit_bytes=int(min(need, VMEM_PLAN_MAX)))


def _row(w, tm=TM):
    return pl.BlockSpec((tm, w), lambda i: (i, 0))


def _vec(w):
    return pl.BlockSpec((1, w), lambda i: (0, 0))


def _mm(pairs, dims, out_dtype, *, tm, tn, name, m_inner=False):
    a0, b0 = pairs[0]
    m_dim = a0.shape[1] if dims == "tn" else a0.shape[0]
    n_dim = b0.shape[0] if dims == "nt" else b0.shape[1]
    contract = {"nn": ((1,), (0,)), "nt": ((1,), (1,)), "tn": ((0,), (0,))}[dims]
    n_pairs = len(pairs)
    assert m_dim % tm == 0 and n_dim % tn == 0, (name, m_dim, n_dim, tm, tn)

    def body(*refs):
        o_ref = refs[-1]
        acc = None
        for p in range(n_pairs):
            a = refs[2 * p][...].astype(bf16)
            b = refs[2 * p + 1][...].astype(bf16)
            t = lax.dot_general(a, b, (contract, ((), ())), preferred_element_type=f32)
            acc = t if acc is None else acc + t
        o_ref[...] = acc.astype(o_ref.dtype)

    if m_inner:
        grid = (n_dim // tn, m_dim // tm)
        mi = lambda j, i: i
        ni = lambda j, i: j
    else:
        grid = (m_dim // tm, n_dim // tn)
        mi = lambda i, j: i
        ni = lambda i, j: j
    in_specs, block_bytes, args = [], 0, []
    for a, b in pairs:
        k_dim = a.shape[0] if dims == "tn" else a.shape[1]
        if dims == "tn":
            in_specs.append(pl.BlockSpec((k_dim, tm), lambda *g: (0, mi(*g))))
        else:
            in_specs.append(pl.BlockSpec((tm, k_dim), lambda *g: (mi(*g), 0)))
        if dims == "nt":
            in_specs.append(pl.BlockSpec((tn, k_dim), lambda *g: (ni(*g), 0)))
        else:
            in_specs.append(pl.BlockSpec((k_dim, tn), lambda *g: (0, ni(*g))))
        block_bytes += _nbytes((tm, k_dim), a.dtype) + _nbytes((tn, k_dim), b.dtype)
        args += [a, b]
    block_bytes += _nbytes((tm, tn), out_dtype)
    temp = _nbytes((tm, tn), f32) * 2 + sum(_nbytes((tm, a.shape[0] if dims == "tn" else a.shape[1]), bf16)
                                            + _nbytes((tn, a.shape[0] if dims == "tn" else a.shape[1]), bf16)
                                            for a, _ in pairs)
    return pl.pallas_call(
        body, grid=grid, in_specs=in_specs,
        out_specs=pl.BlockSpec((tm, tn), lambda *g: (mi(*g), ni(*g))),
        out_shape=SDS((m_dim, n_dim), out_dtype), name=name,
        compiler_params=_params(("parallel", "parallel"), block_bytes, temp),
    )(*args)


def _rms(x, g):
    r = lax.rsqrt(jnp.mean(x * x, axis=-1, keepdims=True) + EPS)
    return x * r * g


def _rms_bwd(x, g, dy):
    r = lax.rsqrt(jnp.mean(x * x, axis=-1, keepdims=True) + EPS)
    xh = x * r
    dxh = dy * g
    dx = r * (dxh - xh * jnp.mean(dxh * xh, axis=-1, keepdims=True))
    return dx, jnp.sum(dy * xh, axis=0, keepdims=True)


def _acc_rows(ref, val):
    @pl.when(pl.program_id(0) == 0)
    def _():
        ref[...] = jnp.zeros_like(ref)
    ref[...] += val


def _norm_fwd(xs, g):
    n = len(xs)

    def body(*refs):
        g = refs[n][...]
        for x_ref, h_ref in zip(refs[:n], refs[n + 1:]):
            h_ref[...] = _rms(x_ref[...], g).astype(bf16)

    return pl.pallas_call(
        body, grid=(S // TM,), in_specs=[_row(D)] * n + [_vec(D)], out_specs=[_row(D)] * n,
        out_shape=[SDS((S, D), bf16)] * n, name="norm_mix_pre",
        compiler_params=_params(("parallel",), 6 * n * TM * D, 8 * n * TM * D))(*xs, g)


def _perm_rows(xs, ds, name):
    n = len(xs)

    def body(*refs):
        outs = iter(refs[n:])
        for x_ref in refs[:n]:
            for d in ds:
                o_ref, rows = next(outs), S // d
                for r in range(d):
                    o_ref[r * rows:(r + 1) * rows, :] = x_ref[pl.ds(r, rows, stride=d), :]

    blk = pl.BlockSpec((S, 128), lambda c: (0, c))
    w = xs[0].shape[1]
    return pl.pallas_call(
        body, grid=(w // 128,), in_specs=[blk] * n, out_specs=[blk] * (n * len(ds)),
        out_shape=[SDS((S, w), f32)] * (n * len(ds)), name=name,
        compiler_params=_params(("parallel",), 4 * S * 128 * n * (1 + len(ds))))(*xs)


def _unperm_sum(nat, perms, ds, name):
    n = len(perms)

    def body(*refs):
        a_ref, o_ref, sc = refs[0], refs[n + 1], refs[n + 2]
        acc = a_ref[...]
        for b_ref, d in zip(refs[1:n + 1], ds):
            rows = S // d
            for r in range(d):
                sc[pl.ds(r, rows, stride=d), :] = b_ref[r * rows:(r + 1) * rows, :]
            acc = acc + sc[...]
        o_ref[...] = acc

    blk = pl.BlockSpec((S, 128), lambda c: (0, c))
    w = nat.shape[1]
    return pl.pallas_call(
        body, grid=(w // 128,), in_specs=[blk] * (n + 1), out_specs=blk, out_shape=SDS((S, w), f32),
        scratch_shapes=[pltpu.VMEM((S, 128), f32)], name=name,
        compiler_params=_params(("parallel",), 4 * S * 128 * (n + 2), 8 * S * 128))(nat, *perms)


def _whole(a):
    return pl.BlockSpec(a.shape, lambda i: (0,) * a.ndim)


def _resid_norm_fwd(x, merged, w_out, g_post, g_pre):
    def body(x_ref, mg_ref, w_ref, gp_ref, gn_ref, mix_ref, x2_ref, h_ref):
        mix = jnp.dot(mg_ref[...], w_ref[...], preferred_element_type=f32)
        x2 = x_ref[...] + _rms(mix, gp_ref[...])
        mix_ref[...] = mix
        x2_ref[...] = x2
        h_ref[...] = _rms(x2, gn_ref[...]).astype(bf16)

    return pl.pallas_call(
        body, grid=(S // TM,), in_specs=[_row(D), _row(D), _whole(w_out), _vec(D), _vec(D)], out_specs=[_row(D)] * 3,
        out_shape=[SDS((S, D), f32), SDS((S, D), f32), SDS((S, D), bf16)], name="proj_out_norm",
        compiler_params=_params(("parallel",), 16 * TM * D + 2 * D * D, 16 * TM * D))(x, merged, w_out, g_post, g_pre)


def _loss_head(x2, a_act, w_down, g_post, target):
    def body(x2_ref, a_ref, w_ref, g_ref, t_ref, loss_ref, dy_ref, dff_ref, dg_ref):
        ff = jnp.dot(a_ref[...], w_ref[...], preferred_element_type=f32)
        g = g_ref[...]
        err = x2_ref[...] + _rms(ff, g) - t_ref[...]
        dy = err * (1.0 / D)
        dff, dg = _rms_bwd(ff, g, dy)
        dy_ref[...] = dy
        dff_ref[...] = dff.astype(bf16)
        _acc_rows(dg_ref, dg)
        _acc_rows(loss_ref, jnp.full((1, 128), jnp.sum(err * err), f32))

    return pl.pallas_call(
        body, grid=(S // TM,), in_specs=[_row(D), _row(F_FF), _whole(w_down), _vec(D), _row(D)],
        out_specs=[_vec(128), _row(D), _row(D), _vec(D)],
        out_shape=[SDS((1, 128), f32), SDS((S, D), f32), SDS((S, D), bf16), SDS((1, D), f32)], name="ffn_down_loss",
        compiler_params=_params(("arbitrary",), 14 * TM * D + 2 * TM * F_FF + 2 * F_FF * D, 28 * TM * D),
    )(x2, a_act, w_down, g_post, target)


def _norm_bwd_mid(dy, d_g, d_u, w_gate_t, w_up_t, x2, mix, g_ffn_pre, g_mix_post):
    def body(dy_ref, dgt_ref, dut_ref, wg_ref, wu_ref, x2_ref, mix_ref, g3_ref, g2_ref, dx2_ref, dmix_ref, dg3_ref, dg2_ref):
        dh = jnp.dot(dgt_ref[...], wg_ref[...], preferred_element_type=f32)
        dh += jnp.dot(dut_ref[...], wu_ref[...], preferred_element_type=f32)
        d3, dg3 = _rms_bwd(x2_ref[...], g3_ref[...], dh)
        dx2 = dy_ref[...] + d3
        dmix, dg2 = _rms_bwd(mix_ref[...], g2_ref[...], dx2)
        dx2_ref[...] = dx2
        dmix_ref[...] = dmix.astype(bf16)
        _acc_rows(dg3_ref, dg3)
        _acc_rows(dg2_ref, dg2)

    return pl.pallas_call(
        body, grid=(S // TM,),
        in_specs=[_row(D), _row(F_FF), _row(F_FF), _whole(w_gate_t), _whole(w_up_t), _row(D), _row(D), _vec(D), _vec(D)],
        out_specs=[_row(D), _row(D), _vec(D), _vec(D)],
        out_shape=[SDS((S, D), f32), SDS((S, D), bf16), SDS((1, D), f32), SDS((1, D), f32)], name="ffn_bwd_in_norm",
        compiler_params=_params(("arbitrary",), 18 * TM * D + 4 * TM * F_FF + 4 * F_FF * D, 28 * TM * D),
    )(dy, d_g, d_u, w_gate_t, w_up_t, x2, mix, g_ffn_pre, g_mix_post)


def _norm_bwd_in(dx2, dh1, x, g):
    def body(dx2_ref, dh_ref, x_ref, g_ref, gx_ref, dg_ref):
        d1, dg = _rms_bwd(x_ref[...], g_ref[...], dh_ref[...])
        gx_ref[...] = dx2_ref[...] + d1
        _acc_rows(dg_ref, dg)

    return pl.pallas_call(
        body, grid=(S // TM,), in_specs=[_row(D)] * 3 + [_vec(D)], out_specs=[_row(D), _vec(D)],
        out_shape=[SDS((S, D), f32), SDS((1, D), f32)], name="norm_bwd_in",
        compiler_params=_params(("arbitrary",), 16 * TM * D, 16 * TM * D))(dx2, dh1, x, g)


def _rope_tables():
    half = ROPE_DIM // 2
    inv_freq = np.power(np.float32(ROPE_THETA), -np.arange(0, ROPE_DIM, 2, dtype=np.float32) / np.float32(ROPE_DIM))
    row = np.arange(S)
    groups = []
    for _, d in DIL_GROUPS:
        pos = ((row % (S // d)) * d + row // (S // d)).astype(np.float32)
        ang = pos[:, None] * inv_freq[None, :].astype(np.float32)
        cos, sin = np.cos(ang).astype(np.float32), np.sin(ang).astype(np.float32)
        c = np.concatenate([cos, cos, np.ones((S, HD - ROPE_DIM), np.float32)], axis=1)
        s_lo = np.concatenate([-sin, np.zeros((S, HD - half), np.float32)], axis=1)
        s_hi = np.concatenate([np.zeros((S, half), np.float32), sin, np.zeros((S, HD - ROPE_DIM), np.float32)], axis=1)
        groups.append(np.stack([np.concatenate([t, t], axis=1) for t in (c, s_lo, s_hi)]))
    return jnp.asarray(np.stack(groups))


def _rotate(x, c, lo, hi, sign):
    tile = lambda t: jnp.tile(t, (1, DIL_W // 128))
    return (x * tile(c) + pltpu.roll(x, DIL_W - ROPE_DIM // 2, 1) * (tile(lo) * sign)
            + pltpu.roll(x, ROPE_DIM // 2, 1) * (tile(hi) * sign))


def _table_specs(g):
    return [pl.BlockSpec((None, None, TM, 128), lambda i, k=k: (g, k, i, 0)) for k in range(3)]


def _rope_fwd(g, p_qkv, tables):
    def body(x_ref, c_ref, lo_ref, hi_ref, o_ref):
        c, lo, hi = c_ref[...], lo_ref[...], hi_ref[...]
        for part in range(2):
            cols = slice(part * DIL_W, (part + 1) * DIL_W)
            o_ref[:, cols] = _rotate(x_ref[:, cols], c, lo, hi, 1.0).astype(bf16)
        o_ref[:, 2 * DIL_W:] = x_ref[:, 2 * DIL_W:].astype(bf16)

    return pl.pallas_call(
        body, grid=(S // TM,), in_specs=[_row(QKV_W)] + _table_specs(g), out_specs=_row(QKV_W),
        out_shape=SDS((S, QKV_W), bf16), name=f"rope_fwd_{g}",
        compiler_params=_params(("parallel",), 6 * TM * QKV_W + 12 * TM * 128, 24 * TM * QKV_W))(p_qkv, tables, tables, tables)


def _rope_bwd(g, dq, dk, dv, tables):
    def body(dq_ref, dk_ref, dv_ref, c_ref, lo_ref, hi_ref, o_ref):
        c, lo, hi = c_ref[...], lo_ref[...], hi_ref[...]
        o_ref[:, :DIL_W] = _rotate(dq_ref[...], c, lo, hi, -1.0).astype(bf16)
        o_ref[:, DIL_W:2 * DIL_W] = _rotate(dk_ref[...], c, lo, hi, -1.0).astype(bf16)
        o_ref[:, 2 * DIL_W:] = dv_ref[...].astype(bf16)

    return pl.pallas_call(
        body, grid=(S // TM,), in_specs=[_row(DIL_W)] * 3 + _table_specs(g), out_specs=_row(QKV_W),
        out_shape=SDS((S, QKV_W), bf16), name=f"rope_bwd_{g}",
        compiler_params=_params(("parallel",), 6 * TM * QKV_W + 12 * TM * 128, 24 * TM * QKV_W))(dq, dk, dv, tables, tables, tables)


def _nt(a, b):
    return lax.dot_general(a, b, (((1,), (1,)), ((), ())), preferred_element_type=f32)


def _tn(a, b):
    return lax.dot_general(a, b, (((0,), (0,)), ((), ())), preferred_element_type=f32)


STEP_BLOCKS = 4
STEP_ROWS = STEP_BLOCKS * BLK


def _dil_prev(g, b):
    _, d = DIL_GROUPS[g]
    nb = S // d // BLK
    if nb == 1 or (b == 0 and nb <= STEP_BLOCKS):
        return None
    return "in" if b > 0 else "halo"


def _bnt(a, b):
    return lax.dot_general(a, b, (((2,), (2,)), ((0,), (0,))), preferred_element_type=f32)


def _bnn(a, b):
    return lax.dot_general(a, b, (((2,), (1,)), ((0,), (0,))), preferred_element_type=f32)


def _btn(a, b):
    return lax.dot_general(a, b, (((1,), (1,)), ((0,), (0,))), preferred_element_type=f32)


def _on_tail(x, tail, fn):
    if tail == x.shape[0]:
        return fn(x)
    return jnp.concatenate([x[:-tail], fn(x[-tail:])], axis=0)


def _heads(ref, part):
    n = ref.shape[0] // BLK
    return jnp.stack([ref[b * BLK:(b + 1) * BLK, part * DIL_W + h * HD:part * DIL_W + (h + 1) * HD]
                      for b in range(n) for h in range(SLOTS)])


def _dil_operands(g, qkv_ref, halo_ref):
    q, kc, vc = (_heads(qkv_ref, part) for part in range(3))
    qi = lax.broadcasted_iota(jnp.int32, (1, BLK, BLK), 1)
    kj = lax.broadcasted_iota(jnp.int32, (1, BLK, BLK), 2)
    with_prev = [b for b in range(STEP_BLOCKS) if _dil_prev(g, b) is not None]
    tail = SLOTS * len(with_prev)
    if not tail:
        return q, kc, vc, None, None, kj <= qi, None, 0
    assert with_prev == list(range(STEP_BLOCKS - len(with_prev), STEP_BLOCKS))
    inside = SLOTS * sum(_dil_prev(g, b) == "in" for b in with_prev)
    kp, vp, prev = kc[:inside], vc[:inside], jnp.broadcast_to(kj >= qi, (inside, BLK, BLK))
    if inside < tail:
        no_halo = jnp.where(pl.program_id(0) == 0, BLK + 1, 0)
        kp = jnp.concatenate([_heads(halo_ref, 1), kp], axis=0)
        vp = jnp.concatenate([_heads(halo_ref, 2), vp], axis=0)
        prev = jnp.concatenate([jnp.broadcast_to(kj >= qi + no_halo, (SLOTS, BLK, BLK)), prev], axis=0)
    return q, kc, vc, kp, vp, kj <= qi, prev, tail


def _dil_in_specs(g, n_aux):
    step = lambda w: pl.BlockSpec((STEP_ROWS, w), lambda i: (i, 0))
    halo = [pl.BlockSpec((BLK, QKV_W), lambda i: (jnp.maximum(i * STEP_BLOCKS - 1, 0), 0))]
    needs_halo = _dil_prev(g, 0) == "halo"
    return [step(QKV_W)] + (halo if needs_halo else []) + [step(DIL_W)] * n_aux, needs_halo


def _dil_fwd(g, qkv):
    in_specs, needs_halo = _dil_in_specs(g, 0)

    def body(*refs):
        qkv_ref, halo_ref = refs[0], refs[1] if needs_halo else None
        o_ref, lse_ref = refs[-2:]
        q, kc, vc, kp, vp, cur, prev, tail = _dil_operands(g, qkv_ref, halo_ref)
        sc = jnp.where(cur, _bnt(q, kc) * SCALE, NEG)
        m = jnp.max(sc, axis=-1, keepdims=True)
        if tail:
            sp = jnp.where(prev, _bnt(q[-tail:], kp) * SCALE, NEG)
            m = _on_tail(m, tail, lambda t: jnp.maximum(t, jnp.max(sp, axis=-1, keepdims=True)))
            pp = jnp.exp(sp - m[-tail:])
        pc = jnp.exp(sc - m)
        den = jnp.sum(pc, axis=-1, keepdims=True)
        if tail:
            den = _on_tail(den, tail, lambda t: t + jnp.sum(pp, axis=-1, keepdims=True))
        inv = 1.0 / den
        o = _bnn((pc * inv).astype(bf16), vc)
        if tail:
            o = _on_tail(o, tail, lambda t: t + _bnn((pp * inv[-tail:]).astype(bf16), vp))
        lse = m + jnp.log(den)
        for b in range(STEP_BLOCKS):
            for h in range(SLOTS):
                rows, hs = slice(b * BLK, (b + 1) * BLK), slice(h * HD, (h + 1) * HD)
                o_ref[rows, hs] = o[SLOTS * b + h]
                lse_ref[rows, hs] = jnp.broadcast_to(lse[SLOTS * b + h], (BLK, HD))

    out = pl.BlockSpec((STEP_ROWS, DIL_W), lambda i: (i, 0))
    return pl.pallas_call(
        body, grid=(S // STEP_ROWS,), in_specs=in_specs, out_specs=[out, out], out_shape=[SDS((S, DIL_W), f32)] * 2,
        name=f"dil_fwd_{g}", compiler_params=_params(("parallel",), 12 * STEP_ROWS * DIL_W, 2 << 20),
    )(*([qkv] * (2 if needs_halo else 1)))


def _dil_combine(outs, lses):
    def body(o0, o1, o2, l0, l1, l2, out_ref, lse_ref, so1, so2, sl1, sl2):
        for (_, d), src, dst in ((DIL_GROUPS[1], o1, so1), (DIL_GROUPS[2], o2, so2),
                                 (DIL_GROUPS[1], l1, sl1), (DIL_GROUPS[2], l2, sl2)):
            rows = S // d
            for r in range(d):
                dst[pl.ds(r, rows, stride=d), :] = src[r * rows:(r + 1) * rows, :]
        a, b, c = l0[...], sl1[...], sl2[...]
        m = jnp.maximum(jnp.maximum(a, b), c)
        ea, eb, ec = jnp.exp(a - m), jnp.exp(b - m), jnp.exp(c - m)
        z = ea + eb + ec
        inv = 1.0 / z
        out_ref[...] = (ea * inv) * o0[...] + (eb * inv) * so1[...] + (ec * inv) * so2[...]
        lse_ref[...] = m + jnp.log(z)

    blk = pl.BlockSpec((S, 128), lambda c: (0, c))
    return pl.pallas_call(
        body, grid=(DIL_W // 128,), in_specs=[blk] * 6, out_specs=[blk] * 2,
        out_shape=[SDS((S, DIL_W), f32)] * 2, scratch_shapes=[pltpu.VMEM((S, 128), f32)] * 4, name="dil_combine",
        compiler_params=_params(("parallel",), 32 * S * 128, 32 * S * 128))(*outs, *lses)


def _dil_bwd(g, qkv, d_out, delta, lse):
    in_specs, needs_halo = _dil_in_specs(g, 3)

    def body(*refs):
        qkv_ref, halo_ref = refs[0], refs[1] if needs_halo else None
        do_ref, dl_ref, lse_ref, dq_ref, dk_ref, dv_ref = refs[-6:]
        q, kc, vc, kp, vp, cur, prev, tail = _dil_operands(g, qkv_ref, halo_ref)
        tiles = [(slice(b * BLK, (b + 1) * BLK), h) for b in range(STEP_BLOCKS) for h in range(SLOTS)]
        do = jnp.stack([do_ref[rows, h * HD:(h + 1) * HD] for rows, h in tiles]).astype(bf16)
        lse = jnp.stack([lse_ref[rows, h * HD:h * HD + 1] for rows, h in tiles])
        delta = jnp.stack([dl_ref[rows, h * HD:h * HD + 1] for rows, h in tiles])

        def probs(q, k, mask, lse, do, v, delta):
            p = jnp.exp(jnp.where(mask, _bnt(q, k) * SCALE, NEG) - lse)
            ds = p * (_bnt(do, v) - delta) * SCALE
            return p.astype(bf16), ds.astype(bf16)

        p, ds = probs(q, kc, cur, lse, do, vc, delta)
        dq, dk, dv = _bnn(ds, kc), _btn(ds, q), _btn(p, do)
        if tail:
            p, ds = probs(q[-tail:], kp, prev, lse[-tail:], do[-tail:], vp, delta[-tail:])
            dq = _on_tail(dq, tail, lambda t: t + _bnn(ds, kp))
            dk_p, dv_p = _btn(ds, q[-tail:]), _btn(p, do[-tail:])
            inside = tail - SLOTS if needs_halo else tail
            pad = jnp.zeros((len(tiles) - inside, BLK, HD), f32)
            dk = dk + jnp.concatenate([dk_p[tail - inside:], pad], axis=0)
            dv = dv + jnp.concatenate([dv_p[tail - inside:], pad], axis=0)
        first = pl.multiple_of(pl.program_id(0) * STEP_ROWS, STEP_ROWS)
        for t, (rows, h) in enumerate(tiles):
            hs = slice(h * HD, (h + 1) * HD)
            own = pl.ds(pl.multiple_of(first + rows.start, BLK), BLK)
            dq_ref[rows, hs] = dq[t]
            dk_ref[own, hs] = dk[t]
            dv_ref[own, hs] = dv[t]
        if needs_halo:
            before = pl.ds(pl.multiple_of(jnp.maximum(first - BLK, 0), BLK), BLK)
            for h in range(SLOTS):
                hs = slice(h * HD, (h + 1) * HD)
                dk_ref[before, hs] += dk_p[h]
                dv_ref[before, hs] += dv_p[h]

    whole = pl.BlockSpec((S, DIL_W), lambda i: (0, 0))
    return pl.pallas_call(
        body, grid=(S // STEP_ROWS,), in_specs=in_specs,
        out_specs=[pl.BlockSpec((STEP_ROWS, DIL_W), lambda i: (i, 0)), whole, whole],
        out_shape=[SDS((S, DIL_W), f32)] * 3, name=f"dil_bwd_{g}",
        compiler_params=_params(("arbitrary",), 20 * STEP_ROWS * DIL_W + 8 * S * DIL_W, 2 << 20),
    )(*([qkv] * (2 if needs_halo else 1)), d_out, delta, lse)


def _scan_rows(x, reverse):
    row = lax.broadcasted_iota(jnp.int32, x.shape, 0)
    k = 1
    while k < S:
        if reverse:
            x = x + jnp.where(row < S - k, pltpu.roll(x, S - k, 0), 0.0)
        else:
            x = x + jnp.where(row >= k, pltpu.roll(x, k, 0), 0.0)
        k *= 2
    return x


N_PAIR = N_FOX // 2
_PAIR_Q = pl.BlockSpec((None, S, 128), lambda p: (p, 0, 0))
_PAIR_K = pl.BlockSpec((None, 8, S), lambda p: (p, 0, 0))


def _forget_fwd(fz, b128):
    def body(z_ref, b_ref, fq_ref, fk_ref):
        z = z_ref[...] + b_ref[...]
        logf = jnp.minimum(z, 0.0) - jnp.log1p(jnp.exp(-jnp.abs(z)))
        f_cum = _scan_rows(logf, reverse=False)
        f_cum_t = f_cum.T
        fq_ref[...] = jnp.zeros_like(fq_ref)
        fk_ref[...] = jnp.zeros_like(fk_ref)
        for p in range(N_PAIR):
            fq_ref[p, :, 0:2] = f_cum[:, 2 * p:2 * p + 2]
            fk_ref[p, 0:2, :] = f_cum_t[2 * p:2 * p + 2, :]

    return pl.pallas_call(
        body, grid=(1,), in_specs=[pl.BlockSpec((S, 128), lambda i: (0, 0)), _vec(128)],
        out_specs=[pl.BlockSpec((N_PAIR, S, 128), lambda i: (0, 0, 0)), pl.BlockSpec((N_PAIR, 8, S), lambda i: (0, 0, 0))],
        out_shape=[SDS((N_PAIR, S, 128), f32), SDS((N_PAIR, 8, S), f32)], name="forget_fwd",
        compiler_params=_params(("arbitrary",), 24 * S * 128, 24 * S * 128))(fz, b128)


def _forget_bwd(fz, b128, d_f_cols, d_f_rows):
    def body(z_ref, b_ref, dfc_ref, dfr_ref, dz_ref, db_ref, df_sc):
        z = z_ref[...] + b_ref[...]
        df_sc[...] = jnp.zeros_like(df_sc)
        for p in range(N_PAIR):
            df_sc[:, 2 * p:2 * p + 2] = dfr_ref[p, :, 0:2] + dfc_ref[p].T[:, 0:2]
        dz = _scan_rows(df_sc[...], reverse=True) * jax.nn.sigmoid(-z)
        dz_ref[...] = dz
        db_ref[...] = jnp.sum(dz, axis=0, keepdims=True)

    full = pl.BlockSpec((S, 128), lambda i: (0, 0))
    return pl.pallas_call(
        body, grid=(1,),
        in_specs=[full, _vec(128), pl.BlockSpec((N_PAIR, 8, S), lambda i: (0, 0, 0)), pl.BlockSpec((N_PAIR, S, 128), lambda i: (0, 0, 0))],
        out_specs=[full, _vec(128)], out_shape=[SDS((S, 128), f32), SDS((1, 128), f32)],
        scratch_shapes=[pltpu.VMEM((S, 128), f32)], name="forget_bwd",
        compiler_params=_params(("arbitrary",), 32 * S * 128, 24 * S * 128))(fz, b128, d_f_cols, d_f_rows)


def _fox_scores(q_ref, k_ref, fq_ref, fk_ref, qi, hh):
    n = (qi + 1) * TQ
    rows, hs = slice(qi * TQ, n), slice(hh * HD, (hh + 1) * HD)
    s = _nt(q_ref[rows, hs], k_ref[0:n, hs]) * SCALE + (fq_ref[rows, hh:hh + 1] - fk_ref[hh:hh + 1, 0:n])
    qpos = qi * TQ + lax.broadcasted_iota(jnp.int32, (TQ, n), 0)
    kpos = lax.broadcasted_iota(jnp.int32, (TQ, n), 1)
    return jnp.where(kpos <= qpos, s, NEG)


def _pair_cols(first):
    return pl.BlockSpec((S, 128), lambda p: (0, first + p))


def _fox_fwd(vr, fq, fk):
    def body(q_ref, k_ref, v_ref, fq_ref, fk_ref, o_ref, lse_ref):
        lse_ref[...] = jnp.zeros_like(lse_ref)
        for hh in range(2):
            hs = slice(hh * HD, (hh + 1) * HD)
            for qi in range(S // TQ):
                n = (qi + 1) * TQ
                rows = slice(qi * TQ, n)
                s = _fox_scores(q_ref, k_ref, fq_ref, fk_ref, qi, hh)
                m = jnp.max(s, axis=-1, keepdims=True)
                p = jnp.exp(s - m)
                den = jnp.sum(p, axis=-1, keepdims=True)
                o_ref[rows, hs] = jnp.dot((p * (1.0 / den)).astype(bf16), v_ref[0:n, hs], preferred_element_type=f32)
                lse_ref[rows, hh:hh + 1] = m + jnp.log(den)

    return pl.pallas_call(
        body, grid=(N_PAIR,), in_specs=[_pair_cols(0), _pair_cols(N_PAIR), _pair_cols(2 * N_PAIR), _PAIR_Q, _PAIR_K],
        out_specs=[_pair_cols(0), _PAIR_Q], out_shape=[SDS((S, FOX_W), f32), SDS((N_PAIR, S, 128), f32)],
        name="fox_fwd", compiler_params=_params(("parallel",), 12 * S * 128, 16 * TQ * S),
    )(vr, vr, vr, fq, fk)


def _fox_bwd(vr, fq, fk, lse, d_out, delta):
    def body(q_ref, k_ref, v_ref, do_ref, fq_ref, fk_ref, lse_ref, dl_ref, dq_ref, dk_ref, dv_ref, dfc_ref, dfr_ref,
             dk_sc, dv_sc):
        dfc_ref[...] = jnp.zeros_like(dfc_ref)
        dfr_ref[...] = jnp.zeros_like(dfr_ref)
        for hh in range(2):
            hs = slice(hh * HD, (hh + 1) * HD)
            dk_sc[...] = jnp.zeros_like(dk_sc)
            dv_sc[...] = jnp.zeros_like(dv_sc)
            for qi in range(S // TQ):
                n = (qi + 1) * TQ
                rows = slice(qi * TQ, n)
                q, do, k, v = q_ref[rows, hs], do_ref[rows, hs], k_ref[0:n, hs], v_ref[0:n, hs]
                p = jnp.exp(_fox_scores(q_ref, k_ref, fq_ref, fk_ref, qi, hh) - lse_ref[rows, hh:hh + 1])
                ds = p * (_nt(do, v) - dl_ref[rows, hh:hh + 1])
                dsb = ds.astype(bf16)
                dq_ref[rows, hs] = jnp.dot(dsb, k, preferred_element_type=f32) * SCALE
                dk_sc[0:n, :] += _tn(dsb, q) * SCALE
                dv_sc[0:n, :] += _tn(p.astype(bf16), do)
                dfc_ref[hh:hh + 1, 0:n] -= jnp.sum(ds, axis=0, keepdims=True)
                dfr_ref[rows, hh:hh + 1] = jnp.sum(ds, axis=-1, keepdims=True)
            dk_ref[:, hs] = dk_sc[...]
            dv_ref[:, hs] = dv_sc[...]

    cols = [_pair_cols(k * N_PAIR) for k in range(3)]
    return pl.pallas_call(
        body, grid=(N_PAIR,), in_specs=cols + [_pair_cols(0), _PAIR_Q, _PAIR_K, _PAIR_Q, _PAIR_Q],
        out_specs=[_pair_cols(0)] * 3 + [_PAIR_K, _PAIR_Q],
        out_shape=[SDS((S, FOX_W), f32)] * 3 + [SDS((N_PAIR, 8, S), f32), SDS((N_PAIR, S, 128), f32)],
        scratch_shapes=[pltpu.VMEM((S, HD), f32)] * 2, name="fox_bwd",
        compiler_params=_params(("parallel",), 32 * S * 128, 24 * TQ * S),
    )(vr, vr, vr, d_out, fq, fk, lse, delta)


def _merge_fwd(out_a, out_b, w_a, w_b, gf):
    cw = D // N_SHARD

    def body(oa_ref, ob_ref, wa_ref, wb_ref, ga_ref, gb_ref, ya_ref, yb_ref, mg_ref):
        oa, ob = oa_ref[...].astype(bf16), ob_ref[...].astype(bf16)
        for j in range(N_SHARD):
            cols = slice(j * cw, (j + 1) * cw)
            ya = jnp.dot(oa, wa_ref[j], preferred_element_type=f32)
            yb = jnp.dot(ob, wb_ref[j], preferred_element_type=f32)
            ya_ref[:, cols] = ya
            yb_ref[:, cols] = yb
            mg_ref[:, cols] = (jax.nn.sigmoid(ga_ref[:, cols]) * ya + jax.nn.sigmoid(gb_ref[:, cols]) * yb).astype(bf16)

    full = lambda a: pl.BlockSpec(a.shape, lambda i: (0, 0, 0))
    return pl.pallas_call(
        body, grid=(S // TM,),
        in_specs=[_row(DIL_W), _row(FOX_W), full(w_a), full(w_b), _row(D), pl.BlockSpec((TM, D), lambda i: (i, 1))],
        out_specs=[_row(D)] * 3, out_shape=[SDS((S, D), f32), SDS((S, D), f32), SDS((S, D), bf16)], name="merge_fwd",
        compiler_params=_params(("parallel",), 22 * TM * D + 2 * (DIL_W + FOX_W) * D, 16 * TM * D),
    )(out_a, out_b, w_a, w_b, gf, gf)


def _merge_bwd(d_mix, w_out, ya, yb, gf):
    def body(dx_ref, w_ref, ya_ref, yb_ref, ga_ref, gb_ref, dya_ref, dyb_ref, dg_ref):
        dm = _nt(dx_ref[...], w_ref[...])
        sa, sb = jax.nn.sigmoid(ga_ref[...]), jax.nn.sigmoid(gb_ref[...])
        dya_ref[...] = (dm * sa).astype(bf16)
        dyb_ref[...] = (dm * sb).astype(bf16)
        dg_ref[:, :D] = (dm * ya_ref[...] * sa * (1.0 - sa)).astype(bf16)
        dg_ref[:, D:] = (dm * yb_ref[...] * sb * (1.0 - sb)).astype(bf16)

    return pl.pallas_call(
        body, grid=(S // TM,),
        in_specs=[_row(D), _whole(w_out)] + [_row(D)] * 3 + [pl.BlockSpec((TM, D), lambda i: (i, 1))],
        out_specs=[_row(D), _row(D), _row(2 * D)],
        out_shape=[SDS((S, D), bf16), SDS((S, D), bf16), SDS((S, 2 * D), bf16)], name="proj_out_bwd_merge",
        compiler_params=_params(("parallel",), 26 * TM * D + 2 * D * D, 28 * TM * D))(d_mix, w_out, ya, yb, gf, gf)


def _branch_bwd(d_ya, d_yb, w_a, w_b, out_a, out_b):
    cw = D // N_SHARD

    def body(dya_ref, dyb_ref, wa_ref, wb_ref, oa_ref, ob_ref, doa_ref, dla_ref, dob_ref, dlb_ref):
        doa = jnp.zeros((TM, DIL_W), f32)
        dob = jnp.zeros((TM, FOX_W), f32)
        for j in range(N_SHARD):
            cols = slice(j * cw, (j + 1) * cw)
            doa += _nt(dya_ref[:, cols], wa_ref[j])
            dob += _nt(dyb_ref[:, cols], wb_ref[j])
        doa_ref[...] = doa
        dob_ref[...] = dob.astype(bf16)
        prod_a = doa * oa_ref[...]
        for h in range(SLOTS):
            hs = slice(h * HD, (h + 1) * HD)
            dla_ref[:, hs] = jnp.broadcast_to(jnp.sum(prod_a[:, hs], axis=-1, keepdims=True), (TM, HD))
        prod_b = dob * ob_ref[...]
        dlb_ref[...] = jnp.zeros_like(dlb_ref)
        for h in range(N_FOX):
            dlb_ref[h // 2, :, h % 2:h % 2 + 1] = jnp.sum(prod_b[:, h * HD:(h + 1) * HD], axis=-1, keepdims=True)

    full = lambda a: pl.BlockSpec(a.shape, lambda i: (0, 0, 0))
    return pl.pallas_call(
        body, grid=(S // TM,),
        in_specs=[_row(D), _row(D), full(w_a), full(w_b), _row(DIL_W), _row(FOX_W)],
        out_specs=[_row(DIL_W), _row(DIL_W), _row(FOX_W), pl.BlockSpec((N_PAIR, TM, 128), lambda i: (0, i, 0))],
        out_shape=[SDS((S, DIL_W), f32), SDS((S, DIL_W), f32), SDS((S, FOX_W), bf16), SDS((N_PAIR, S, 128), f32)],
        name="branch_bwd", compiler_params=_params(("parallel",), 8 * TM * D + 2 * (DIL_W + FOX_W) * D, 8 * TM * D),
    )(d_ya, d_yb, w_a, w_b, out_a, out_b)


def _branch_grads(out_a, out_b, d_ya, d_yb):
    cw = D // N_SHARD

    def body(oa_ref, ob_ref, dya_ref, dyb_ref, ga_ref, gb_ref):
        ga_ref[...] = _tn(oa_ref[...].astype(bf16), dya_ref[...]).astype(bf16)
        gb_ref[...] = _tn(ob_ref[...].astype(bf16), dyb_ref[...]).astype(bf16)

    whole = lambda w: pl.BlockSpec((S, w), lambda j: (0, 0))
    cols = pl.BlockSpec((S, cw), lambda j: (0, j))
    return pl.pallas_call(
        body, grid=(N_SHARD,), in_specs=[whole(DIL_W), whole(FOX_W), cols, cols],
        out_specs=[pl.BlockSpec((None, DIL_W, cw), lambda j: (j, 0, 0)), pl.BlockSpec((None, FOX_W, cw), lambda j: (j, 0, 0))],
        out_shape=[SDS((N_SHARD, DIL_W, cw), bf16), SDS((N_SHARD, FOX_W, cw), bf16)], name="grad_w_proj_ab",
        compiler_params=_params(("parallel",), 4 * S * (DIL_W + FOX_W) + 4 * S * cw + 4 * (DIL_W + FOX_W) * cw,
                                4 * S * (DIL_W + FOX_W)))(out_a, out_b, d_ya, d_yb)


FF_TN = F_FF // 2
FF_TM = 1024


def _ffn_fwd(h, w_gate_t, w_up_t):
    def body(h_ref, wg_ref, wu_ref, g_ref, u_ref, a_ref):
        hb = h_ref[...]
        g = _nt(hb, wg_ref[...])
        u = _nt(hb, wu_ref[...])
        g_ref[...] = g
        u_ref[...] = u
        a_ref[...] = (g * jax.nn.sigmoid(g) * u).astype(bf16)

    tile = pl.BlockSpec((FF_TM, FF_TN), lambda j, i: (i, j))
    wspec = pl.BlockSpec((FF_TN, D), lambda j, i: (j, 0))
    return pl.pallas_call(
        body, grid=(F_FF // FF_TN, S // FF_TM),
        in_specs=[pl.BlockSpec((FF_TM, D), lambda j, i: (i, 0)), wspec, wspec], out_specs=[tile] * 3,
        out_shape=[SDS((S, F_FF), f32), SDS((S, F_FF), f32), SDS((S, F_FF), bf16)], name="ffn_fwd",
        compiler_params=_params(("parallel", "parallel"), 2 * FF_TM * D + 4 * D * FF_TN + 10 * FF_TM * FF_TN, 16 * FF_TM * FF_TN),
    )(h, w_gate_t, w_up_t)


def _ffn_bwd_act(d_ff, w_down, g_act, u_act):
    def body(d_ref, wd_ref, g_ref, u_ref, dg_ref, du_ref):
        da = _nt(d_ref[...], wd_ref[...])
        g = g_ref[...]
        sg = jax.nn.sigmoid(g)
        du_ref[...] = (da * g * sg).astype(bf16)
        dg_ref[...] = (da * u_ref[...] * sg * (1.0 + g * (1.0 - sg))).astype(bf16)

    tile = pl.BlockSpec((FF_TM, FF_TN), lambda j, i: (i, j))
    return pl.pallas_call(
        body, grid=(F_FF // FF_TN, S // FF_TM),
        in_specs=[pl.BlockSpec((FF_TM, D), lambda j, i: (i, 0)), pl.BlockSpec((FF_TN, D), lambda j, i: (j, 0)), tile, tile],
        out_specs=[tile, tile], out_shape=[SDS((S, F_FF), bf16)] * 2, name="ffn_bwd_act",
        compiler_params=_params(("parallel", "parallel"), 2 * FF_TM * D + 2 * D * FF_TN + 12 * FF_TM * FF_TN, 16 * FF_TM * FF_TN),
    )(d_ff, w_down, g_act, u_act)


def _row_tile(rows):
    return next(t for t in (376, 128, 176, 64, 32, 16, 8) if rows % t == 0)


def _adamw_math(w, g, m, v):
    c1 = 1.0 - ADAM_B1 ** ADAM_STEP
    c2 = 1.0 - ADAM_B2 ** ADAM_STEP
    m_new = ADAM_B1 * m + (1.0 - ADAM_B1) * g
    v_new = ADAM_B2 * v + (1.0 - ADAM_B2) * (g * g)
    return -ADAM_LR * ((m_new / c1) / (jnp.sqrt(v_new / c2) + ADAM_EPS) + ADAM_WD * w), m_new, v_new


def _adamw(w, g, m, v, name):
    rows, cols = w.shape
    tm = _row_tile(rows)

    def body(w_ref, g_ref, m_ref, v_ref, d_ref, nm_ref, nv_ref):
        d_ref[...], nm_ref[...], nv_ref[...] = _adamw_math(w_ref[...], g_ref[...], m_ref[...], v_ref[...])

    spec = pl.BlockSpec((tm, cols), lambda i: (i, 0))
    return pl.pallas_call(
        body, grid=(rows // tm,), in_specs=[spec] * 4, out_specs=[spec] * 3, out_shape=[SDS(w.shape, f32)] * 3,
        name=name, compiler_params=_params(("parallel",), 28 * tm * cols, 16 * tm * cols))(w, g, m, v)


def _adamw_halves(w, g_mine, g_theirs, m, v, name):
    rows, cols = w.shape
    tm = _row_tile(rows // 2)
    per_half = rows // 2 // tm
    core = lax.axis_index("c").astype(jnp.int32).reshape(1)

    def body(c_ref, w_ref, gm_ref, gt_ref, m_ref, v_ref, g_ref, d_ref, nm_ref, nv_ref):
        mine = pl.program_id(0) // per_half == c_ref[0]
        g = jnp.where(mine, gm_ref[...], gt_ref[...])
        g_ref[...] = g
        d_ref[...], nm_ref[...], nv_ref[...] = _adamw_math(w_ref[...], g, m_ref[...], v_ref[...])

    spec = pl.BlockSpec((tm, cols), lambda i, c_ref: (i, 0))
    in_half = lambda i, first: jnp.clip(i - first * per_half, 0, per_half - 1)
    grid_spec = pltpu.PrefetchScalarGridSpec(
        num_scalar_prefetch=1, grid=(rows // tm,),
        in_specs=[spec, pl.BlockSpec((tm, cols), lambda i, c_ref: (in_half(i, c_ref[0]), 0)),
                  pl.BlockSpec((tm, cols), lambda i, c_ref: (in_half(i, 1 - c_ref[0]), 0)), spec, spec],
        out_specs=[spec] * 4)
    return pl.pallas_call(
        body, grid_spec=grid_spec, out_shape=[SDS(w.shape, f32)] * 4, name=name,
        compiler_params=_params(("parallel",), 36 * tm * cols, 16 * tm * cols))(core, w, g_mine, g_theirs, m, v)


_ANY = pl.BlockSpec(memory_space=pl.ANY)


def _place():
    x, y, c = lax.axis_index("x"), lax.axis_index("y"), lax.axis_index("c")
    chips = [(1 - x, y), (x, 1 - y), (1 - x, 1 - y)]
    return x, y, c, chips


def _halved(t):
    return t.reshape(t.shape[:-2] + (2, t.shape[-2] // 2, t.shape[-1]))


def _gather_body(src, out, send_ici, recv_ici, send_d2d, recv_d2d):
    x, y, c, chips = _place()
    sibling = (x, y, 1 - c)
    me_j = 2 * x + y
    sends = []
    for a in range(len(src)):
        for p in range(3):
            cp = pltpu.make_async_remote_copy(
                src_ref=src[a].at[c], dst_ref=out[a].at[me_j, c], send_sem=send_ici.at[a, p],
                recv_sem=recv_ici.at[a, p], device_id=(*chips[p], c), device_id_type=MESH)
            cp.start()
            sends.append(cp)
    for a in range(len(src)):
        for p, (px, py) in enumerate(chips):
            blk = out[a].at[2 * px + py, c]
            pltpu.make_async_remote_copy(
                src_ref=blk, dst_ref=blk, send_sem=send_ici.at[a, p], recv_sem=recv_ici.at[a, p],
                device_id=sibling, device_id_type=MESH).wait_recv()
            fw = pltpu.make_async_remote_copy(
                src_ref=blk, dst_ref=blk, send_sem=send_d2d.at[a, p], recv_sem=recv_d2d.at[a, p],
                device_id=sibling, device_id_type=MESH)
            fw.start()
            sends.append(fw)
    for a in range(len(src)):
        for p, (px, py) in enumerate(chips):
            blk = out[a].at[2 * px + py, 1 - c]
            pltpu.make_async_remote_copy(
                src_ref=blk, dst_ref=blk, send_sem=send_d2d.at[a, p], recv_sem=recv_d2d.at[a, p],
                device_id=sibling, device_id_type=MESH).wait_recv()
    for cp in sends:
        cp.wait_send()


def _handshake(peers):
    barrier = pltpu.get_barrier_semaphore()
    for peer in peers:
        pl.semaphore_signal(barrier, inc=1, device_id=peer, device_id_type=MESH)
    pl.semaphore_wait(barrier, len(peers))


_SEQUENCER = dict(axis_name="sequencer", num_cores=1)
GATHER_LATE_ID, SCATTER_EARLY_ID, SWAP_EARLY_ID, GATHER_FIRST_ID, SCATTER_LATE_ID = 1, 2, 3, 4, 5


def _all_gather_async(shards, after, name, collective_id):
    n, k = len(shards), len(after)

    def body(*refs):
        x, y, c, chips = _place()
        _handshake([(*chip, c) for chip in chips] + [(x, y, 1 - c)])
        _gather_body(refs[:n], refs[n + k:2 * n + k], *refs[2 * n + k:])

    return pl.kernel(
        body, out_type=[SDS((N_SHARD,) + t.shape, t.dtype) for t in shards],
        mesh=plsc.ScalarSubcoreMesh(**_SEQUENCER), scratch_types=[pltpu.SemaphoreType.DMA((n, 3))] * 4,
        compiler_params=pltpu.CompilerParams(collective_id=collective_id), name=name)(*shards, *after)


def _pair_swap(grads):
    n = len(grads)

    def body(*refs):
        src, out, send_sems, recv_sems = refs[:n], refs[n:2 * n], refs[2 * n], refs[2 * n + 1]
        x, y, c, _ = _place()
        copies = [pltpu.make_async_remote_copy(
            src_ref=src[a].at[:, 1 - c], dst_ref=out[a], send_sem=send_sems.at[a], recv_sem=recv_sems.at[a],
            device_id=(x, y, 1 - c), device_id_type=MESH) for a in range(n)]
        for cp in copies:
            cp.start()
        for cp in copies:
            cp.wait()

    return pl.pallas_call(
        body, in_specs=[_ANY] * n, out_specs=[_ANY] * n,
        out_shape=[SDS((N_SHARD,) + t.shape[2:], t.dtype) for t in grads],
        scratch_shapes=[pltpu.SemaphoreType.DMA((n,)), pltpu.SemaphoreType.DMA((n,))], name="pair_swap",
        compiler_params=pltpu.CompilerParams(has_side_effects=True))(*grads)


def _pair_swap_early(grads):
    n = len(grads)

    def body(*refs):
        src, out, send_sems, recv_sems = refs[:n], refs[n:2 * n], refs[2 * n], refs[2 * n + 1]
        x, y, c, _ = _place()
        _handshake([(x, y, 1 - c)])
        copies = [pltpu.make_async_remote_copy(
            src_ref=src[a].at[:, 1 - c], dst_ref=out[a], send_sem=send_sems.at[a], recv_sem=recv_sems.at[a],
            device_id=(x, y, 1 - c), device_id_type=MESH) for a in range(n)]
        for cp in copies:
            cp.start()
        for cp in copies:
            cp.wait()

    return pl.kernel(
        body, out_type=[SDS((N_SHARD,) + t.shape[2:], t.dtype) for t in grads],
        mesh=plsc.ScalarSubcoreMesh(**_SEQUENCER), scratch_types=[pltpu.SemaphoreType.DMA((n,))] * 2,
        compiler_params=pltpu.CompilerParams(collective_id=SWAP_EARLY_ID), name="pair_swap_early")(*grads)


def _scatter_early(parts):
    n = len(parts)

    def body(*refs):
        part, recv, send_sems, recv_sems = refs[:n], refs[n:2 * n], refs[2 * n], refs[2 * n + 1]
        x, y, c, chips = _place()
        _handshake([(*chip, c) for chip in chips])
        me_j = 2 * x + y
        sends = []
        for a in range(n):
            for p, (px, py) in enumerate(chips):
                cp = pltpu.make_async_remote_copy(
                    src_ref=part[a].at[2 * px + py], dst_ref=recv[a].at[me_j], send_sem=send_sems.at[a, p],
                    recv_sem=recv_sems.at[a, p], device_id=(px, py, c), device_id_type=MESH)
                cp.start()
                sends.append(cp)
        for a in range(n):
            for p, (px, py) in enumerate(chips):
                slot = recv[a].at[2 * px + py]
                pltpu.make_async_remote_copy(
                    src_ref=slot, dst_ref=slot, send_sem=send_sems.at[a, p], recv_sem=recv_sems.at[a, p],
                    device_id=(px, py, c), device_id_type=MESH).wait_recv()
        for cp in sends:
            cp.wait_send()

    return pl.kernel(
        body, out_type=[SDS(t.shape, t.dtype) for t in parts],
        mesh=plsc.ScalarSubcoreMesh(**_SEQUENCER), scratch_types=[pltpu.SemaphoreType.DMA((n, 3))] * 2,
        compiler_params=pltpu.CompilerParams(collective_id=SCATTER_EARLY_ID), name="scatter_early")(*parts)


def _pair_sum(grads, other, name):
    _, _, rows, cols = grads.shape
    tr = _row_tile(rows)
    core = lax.axis_index("c").astype(jnp.int32).reshape(1)

    def body(c_ref, g_ref, o_ref, out_ref):
        out_ref[...] = (g_ref[...].astype(f32) + o_ref[...].astype(f32)).astype(bf16)

    grid_spec = pltpu.PrefetchScalarGridSpec(
        num_scalar_prefetch=1, grid=(N_SHARD, rows // tr),
        in_specs=[pl.BlockSpec((None, None, tr, cols), lambda j, i, c_ref: (j, c_ref[0], i, 0)),
                  pl.BlockSpec((None, tr, cols), lambda j, i, c_ref: (j, i, 0))],
        out_specs=pl.BlockSpec((None, tr, cols), lambda j, i, c_ref: (j, i, 0)))
    return pl.pallas_call(
        body, grid_spec=grid_spec, out_shape=SDS((N_SHARD, rows, cols), bf16), name=name,
        compiler_params=_params(("parallel", "parallel"), 10 * tr * cols, 12 * tr * cols))(core, grads, other)


def _scatter_partials(parts, small):
    n = len(parts)

    def body(*refs):
        part, small_ref, recv, small_all_ref = refs[:n], refs[n], refs[n + 1:2 * n + 1], refs[2 * n + 1]
        send_sems, recv_sems, ssend, srecv, local_sem = refs[2 * n + 2:]
        x, y, c, chips = _place()
        flip = lambda a, bit: 1 - a if bit else a
        peers = [(flip(x, k & 4), flip(y, k & 2), flip(c, k & 1)) for k in range(1, 8)]
        _handshake(peers)
        me_j = 2 * x + y
        me_dev = 4 * x + 2 * y + c
        own = pltpu.make_async_copy(small_ref, small_all_ref.at[me_dev], local_sem)
        own.start()
        sends = []
        for a in range(n):
            for p, (px, py) in enumerate(chips):
                cp = pltpu.make_async_remote_copy(
                    src_ref=part[a].at[2 * px + py], dst_ref=recv[a].at[me_j], send_sem=send_sems.at[a, p],
                    recv_sem=recv_sems.at[a, p], device_id=(px, py, c), device_id_type=MESH)
                cp.start()
                sends.append(cp)
        for k, to in enumerate(peers):
            cp = pltpu.make_async_remote_copy(
                src_ref=small_ref, dst_ref=small_all_ref.at[me_dev],
                send_sem=ssend.at[k], recv_sem=srecv.at[k], device_id=to, device_id_type=MESH)
            cp.start()
            sends.append(cp)
        for a in range(n):
            for p, (px, py) in enumerate(chips):
                slot = recv[a].at[2 * px + py]
                pltpu.make_async_remote_copy(
                    src_ref=slot, dst_ref=slot, send_sem=send_sems.at[a, p], recv_sem=recv_sems.at[a, p],
                    device_id=(px, py, c), device_id_type=MESH).wait_recv()
        for k, (px, py, pc) in enumerate(peers):
            slot = small_all_ref.at[4 * px + 2 * py + pc]
            pltpu.make_async_remote_copy(
                src_ref=slot, dst_ref=slot, send_sem=ssend.at[k], recv_sem=srecv.at[k],
                device_id=(px, py, pc), device_id_type=MESH).wait_recv()
        for cp in sends:
            cp.wait_send()
        own.wait()

    return pl.kernel(
        body, out_type=[SDS(t.shape, t.dtype) for t in parts] + [SDS((8, SMALL_ROWS, D), f32)],
        mesh=plsc.ScalarSubcoreMesh(**_SEQUENCER),
        scratch_types=[pltpu.SemaphoreType.DMA((n, 3)), pltpu.SemaphoreType.DMA((n, 3)),
                       pltpu.SemaphoreType.DMA((7,)), pltpu.SemaphoreType.DMA((7,)), pltpu.SemaphoreType.DMA],
        compiler_params=pltpu.CompilerParams(collective_id=SCATTER_LATE_ID), name="scatter_partials")(*parts, small)


def _sum_partials(part, recv, name):
    _, rows, cols = recv.shape
    tr = _row_tile(rows)
    me = (2 * lax.axis_index("x") + lax.axis_index("y")).astype(jnp.int32).reshape(1)

    def body(me_ref, mine, r0, r1, r2, r3, out_ref):
        acc = None
        for j, r in enumerate((r0, r1, r2, r3)):
            term = jnp.where(me_ref[0] == j, mine[...], r[...]).astype(f32)
            acc = term if acc is None else acc + term
        out_ref[...] = acc

    slot = lambda j: pl.BlockSpec((None, tr, cols), lambda i, me_ref: (jnp.where(me_ref[0] == j, j ^ 1, j), i, 0))
    grid_spec = pltpu.PrefetchScalarGridSpec(
        num_scalar_prefetch=1, grid=(rows // tr,),
        in_specs=[pl.BlockSpec((None, tr, cols), lambda i, me_ref: (me_ref[0], i, 0)), slot(0), slot(1), slot(2), slot(3)],
        out_specs=pl.BlockSpec((tr, cols), lambda i, me_ref: (i, 0)))
    return pl.pallas_call(
        body, grid_spec=grid_spec, out_shape=SDS((rows, cols), f32), name=name,
        compiler_params=_params(("parallel",), 14 * tr * cols, 12 * tr * cols))(me, part, recv, recv, recv, recv)


def _sum_small(small_all):
    def body(small_ref, out_ref):
        tot = small_ref[0]
        for k in range(1, 8):
            tot = tot + small_ref[k]
        out_ref[...] = tot

    return pl.pallas_call(
        body, grid=(1,), in_specs=[pl.BlockSpec((8, SMALL_ROWS, D), lambda i: (0, 0, 0))],
        out_specs=pl.BlockSpec((SMALL_ROWS, D), lambda i: (0, 0)), out_shape=SDS((SMALL_ROWS, D), f32),
        name="sum_small", compiler_params=_params(("arbitrary",), 36 * SMALL_ROWS * D))(small_all)


def _swap_halves(halves, name):
    n = len(halves)

    def body(*refs):
        src, out, send_sems, recv_sems = refs[:n], refs[n:2 * n], refs[2 * n], refs[2 * n + 1]
        x, y, c, _ = _place()
        copies = [pltpu.make_async_remote_copy(
            src_ref=src[a], dst_ref=out[a], send_sem=send_sems.at[a], recv_sem=recv_sems.at[a],
            device_id=(x, y, 1 - c), device_id_type=MESH) for a in range(n)]
        for cp in copies:
            cp.start()
        for cp in copies:
            cp.wait()

    return pl.pallas_call(
        body, in_specs=[_ANY] * n, out_specs=[_ANY] * n, out_shape=[SDS(t.shape, f32) for t in halves],
        scratch_shapes=[pltpu.SemaphoreType.DMA((n,))] * 2, name=name,
        compiler_params=pltpu.CompilerParams(has_side_effects=True))(*halves)


def _kernel_layout(name, t):
    t = t[0]
    if name in TRANSPOSED:
        t = jnp.swapaxes(t, 0, 1)
    return _pad_rows(t, SHARD_SHAPE[name][0])


def _harness_layout(name, t):
    if name == "w_in":
        t = t[:IN_SHARD]
    if name in TRANSPOSED:
        t = jnp.swapaxes(t, 0, 1)
    return t[None]


def _pad_rows(t, rows):
    return t if t.shape[0] == rows else jnp.pad(t, ((0, rows - t.shape[0]), (0, 0)))


_QA, _KA, _VA, _QB, _F, _GAB = 0, 768, 1536, 2304, 3840, 3848


def _spans(a, b):
    return [(j, max(a, j * IN_SHARD) - j * IN_SHARD, max(a, j * IN_SHARD) - a,
             min(b, (j + 1) * IN_SHARD) - max(a, j * IN_SHARD))
            for j in range(N_SHARD) if max(a, j * IN_SHARD) < min(b, (j + 1) * IN_SHARD)]


_LANES = pl.BlockSpec((N_SHARD, IN_SHARD_PAD, 128), lambda c: (0, 0, c))


def _split_w_in(shards):
    group = [[(o + g * DIL_W, o + (g + 1) * DIL_W) for o in (_QA, _KA, _VA)] for g in range(3)]
    fox = [[(_QB + k * FOX_W, _QB + (k + 1) * FOX_W)] for k in range(3)]
    wanted = group + fox + [[(_QB, _F)], [(_F, _GAB)], [(_GAB, IN_COLS)]]
    rows = [sum(b - a for a, b in w) for w in wanted]
    rows[7] = 128

    def body(s_ref, *o_refs):
        for o_ref, want in zip(o_refs, wanted):
            at = 0
            for a, b in want:
                for j, src, off, n in _spans(a, b):
                    o_ref[at + off:at + off + n, :] = s_ref[j, src:src + n, :]
                at += b - a
        o_refs[7][N_FOX:, :] = jnp.zeros((128 - N_FOX, 128), bf16)

    return pl.pallas_call(
        body, grid=(D // 128,), in_specs=[_LANES], out_specs=[pl.BlockSpec((r, 128), lambda c: (0, c)) for r in rows],
        out_shape=[SDS((r, D), bf16) for r in rows], name="split_w_in",
        compiler_params=_params(("parallel",), 2 * 128 * (N_SHARD * IN_SHARD_PAD + sum(rows))))(shards)


def _join_w_in(g_a, g_fox, g_f, g_gab):
    parts = [(g_a[k], o, o + DIL_W) for o in (0, DIL_W, 2 * DIL_W) for k in range(3)]
    parts += [(t, 0, FOX_W) for t in g_fox] + [(g_f, 0, N_FOX), (g_gab, 0, 2 * D)]
    arrays = list(g_a) + list(g_fox) + [g_f, g_gab]
    index = {id(t): i for i, t in enumerate(arrays)}

    def body(*refs):
        o_ref = refs[-1]
        o_ref[:, IN_SHARD:, :] = jnp.zeros((N_SHARD, IN_SHARD_PAD - IN_SHARD, 128), bf16)
        at = 0
        for t, lo, hi in parts:
            src_ref = refs[index[id(t)]]
            for j, dst, off, n in _spans(at, at + hi - lo):
                o_ref[j, dst:dst + n, :] = src_ref[lo + off:lo + off + n, :].astype(bf16)
            at += hi - lo

    return pl.pallas_call(
        body, grid=(D // 128,), in_specs=[pl.BlockSpec((t.shape[0], 128), lambda c: (0, c)) for t in arrays],
        out_specs=_LANES, out_shape=SDS((N_SHARD, IN_SHARD_PAD, D), bf16), name="join_w_in",
        compiler_params=_params(("parallel",), 2 * 128 * (N_SHARD * IN_SHARD_PAD + sum(t.shape[0] for t in arrays))),
    )(*arrays)


def _full_weights(gathered):
    full = {n: t.reshape((N_SHARD,) + SHARD_SHAPE[n]) for n, t in gathered.items()}
    out = {}
    if "w_in" in full:
        pieces = _split_w_in(full["w_in"])
        out.update(w_a_t=pieces[0:3], w_fox_t=pieces[3:6], w_vr_t=pieces[6], w_f_t=pieces[7], w_gab_t=pieces[8])
    if "w_out" in full:
        out.update(
            w_a4=full["w_proj_a"],
            w_b4=full["w_proj_b"],
            w_out=full["w_out"].reshape(D, D),
            w_gate_t=full["w_ffn_gate"].reshape(F_FF, D),
            w_up_t=full["w_ffn_up"].reshape(F_FF, D),
            w_down=full["w_ffn_down"].reshape(F_FF, D))
    return out


def _sharded_grads(g):
    full = dict(w_in=_join_w_in(g["w_a_t"], g["w_fox_t"], g["w_f_t"], g["w_gab_t"]), w_proj_a=g["w_a4"],
                w_proj_b=g["w_b4"], w_out=g["w_out"], w_ffn_gate=g["w_gate_t"], w_ffn_up=g["w_up_t"],
                w_ffn_down=g["w_down"])
    return {n: _halved(full[n].reshape((N_SHARD,) + SHARD_SHAPE[n])) for n in W_NAMES}


def _local_step(x, target, wt, b_forget, g_mix_pre, g_mix_post, g_ffn_pre, g_ffn_post, late=None):
    tables = _rope_tables()
    b128 = jnp.pad(b_forget, ((0, 0), (0, 128 - N_FOX)))
    dils = tuple(d for _, d in DIL_GROUPS[1:])

    hs = _norm_fwd([x] + list(_perm_rows([x], dils, "perm_x")), g_mix_pre)
    h1 = hs[0]
    if callable(wt):
        wt = wt(h1)
    qkv = [_rope_fwd(g, _mm([(hs[g], wt["w_a_t"][g])], "nt", f32, tm=1024, tn=QKV_W, name=f"proj_a_{g}"), tables)
           for g in range(3)]
    vr = _mm([(h1, wt["w_vr_t"])], "nt", bf16, tm=1024, tn=VR_W // 2, name="proj_vr")
    gab = _mm([(h1, wt["w_gab_t"])], "nt", f32, tm=512, tn=2 * D, name="proj_gab")
    fz = _mm([(h1, wt["w_f_t"])], "nt", f32, tm=1024, tn=128, name="proj_f")
    dil = [_dil_fwd(g, qkv[g]) for g in range(3)]
    out_a, lse_a = _dil_combine([o for o, _ in dil], [l for _, l in dil])
    f_q, f_k = _forget_fwd(fz, b128)
    out_b, lse_b = _fox_fwd(vr, f_q, f_k)
    if late is not None:
        wt = {**wt, **late(out_b)}
    ya, yb, merged = _merge_fwd(out_a, out_b, wt["w_a4"], wt["w_b4"], gab)
    mix, x2, h3 = _resid_norm_fwd(x, merged, wt["w_out"], g_mix_post, g_ffn_pre)
    g_act, u_act, a_act = _ffn_fwd(h3, wt["w_gate_t"], wt["w_up_t"])
    sq_err, dy, d_ff, dg_ffn_post = _loss_head(x2, a_act, wt["w_down"], g_ffn_post, target)

    grads = {}
    d_g, d_u = _ffn_bwd_act(d_ff, wt["w_down"], g_act, u_act)
    grads["w_down"] = _mm([(a_act, d_ff)], "tn", bf16, tm=FF_TN, tn=D, name="grad_w_down")
    grads["w_gate_t"] = _mm([(d_g, h3)], "tn", bf16, tm=FF_TN, tn=D, name="grad_w_gate")
    grads["w_up_t"] = _mm([(d_u, h3)], "tn", bf16, tm=FF_TN, tn=D, name="grad_w_up")
    dx2, d_mix, dg_ffn_pre, dg_mix_post = _norm_bwd_mid(dy, d_g, d_u, wt["w_gate_t"], wt["w_up_t"], x2, mix,
                                                        g_ffn_pre, g_mix_post)

    grads["w_out"] = _mm([(merged, d_mix)], "tn", bf16, tm=D, tn=D, name="grad_w_out")
    d_ya, d_yb, d_gab = _merge_bwd(d_mix, wt["w_out"], ya, yb, gab)
    grads["w_a4"], grads["w_b4"] = _branch_grads(out_a, out_b, d_ya, d_yb)
    d_out_a, delta_a, d_out_b, delta_b = _branch_bwd(d_ya, d_yb, wt["w_a4"], wt["w_b4"], out_a, out_b)

    perm = _perm_rows([d_out_a, delta_a, lse_a], dils, "perm_dil_bwd")
    aux = [(d_out_a, delta_a, lse_a)] + [tuple(perm[k * len(dils) + i] for k in range(3)) for i in range(len(dils))]
    d_qkv = []
    for g in range(3):
        dq, dk, dv = _dil_bwd(g, qkv[g], *aux[g])
        d_qkv.append(_rope_bwd(g, dq, dk, dv, tables))
    *d_fox, d_f_cols, d_f_rows = _fox_bwd(vr, f_q, f_k, lse_b, d_out_b, delta_b)
    d_z, d_b128 = _forget_bwd(fz, b128, d_f_cols, d_f_rows)

    grads["w_a_t"] = [_mm([(d_qkv[g], hs[g])], "tn", bf16, tm=QKV_W, tn=D, name=f"grad_w_a_{g}") for g in range(3)]
    grads["w_fox_t"] = [_mm([(d_fox[k], h1)], "tn", bf16, tm=FOX_W, tn=D, name=f"grad_w_fox_{k}") for k in range(3)]
    grads["w_gab_t"] = _mm([(d_gab, h1)], "tn", bf16, tm=D, tn=D, name="grad_w_gab")
    grads["w_f_t"] = _mm([(d_z, h1)], "tn", bf16, tm=128, tn=D, name="grad_w_f")
    d_h1_nat = _mm([(d_qkv[0], wt["w_a_t"][0])] + list(zip(d_fox, wt["w_fox_t"]))
                   + [(d_gab, wt["w_gab_t"]), (d_z, wt["w_f_t"])], "nn", f32, tm=512, tn=D, name="proj_in_bwd")
    d_h1_dil = [_mm([(d_qkv[g], wt["w_a_t"][g])], "nn", f32, tm=1024, tn=D, name=f"proj_a_bwd_{g}") for g in (1, 2)]
    d_h1 = _unperm_sum(d_h1_nat, d_h1_dil, dils, "unperm_d_h1")
    grad_x, dg_mix_pre = _norm_bwd_in(dx2, d_h1, x, g_mix_pre)

    small = dict(b_forget=d_b128[:, :N_FOX], norm_mix_pre=dg_mix_pre, norm_mix_post=dg_mix_post,
                 norm_ffn_pre=dg_ffn_pre, norm_ffn_post=dg_ffn_post)
    grads["mid_backward"] = d_qkv[0]
    return sq_err, grad_x, grads, small


NORMS = ("norm_mix_pre", "norm_mix_post", "norm_ffn_pre", "norm_ffn_post")
ORDER = ("w_in", "w_proj_a", "w_proj_b", "w_out", "b_forget", "w_ffn_gate", "w_ffn_up", "w_ffn_down") + NORMS


def kernel(x, w_in, w_proj_a, w_proj_b, w_out, b_forget, w_ffn_gate, w_ffn_up, w_ffn_down, norm_mix_pre, norm_mix_post, norm_ffn_pre, norm_ffn_post, loss_target, m_w_in, m_w_proj_a, m_w_proj_b, m_w_out, m_b_forget, m_w_ffn_gate, m_w_ffn_up, m_w_ffn_down, m_norm_mix_pre, m_norm_mix_post, m_norm_ffn_pre, m_norm_ffn_post, v_w_in, v_w_proj_a, v_w_proj_b, v_w_out, v_b_forget, v_w_ffn_gate, v_w_ffn_up, v_w_ffn_down, v_norm_mix_pre, v_norm_mix_post, v_norm_ffn_pre, v_norm_ffn_post):
    given = dict(w_in=w_in, w_proj_a=w_proj_a, w_proj_b=w_proj_b, w_out=w_out, w_ffn_gate=w_ffn_gate,
                 w_ffn_up=w_ffn_up, w_ffn_down=w_ffn_down)
    given_m = dict(w_in=m_w_in, w_proj_a=m_w_proj_a, w_proj_b=m_w_proj_b, w_out=m_w_out, w_ffn_gate=m_w_ffn_gate,
                   w_ffn_up=m_w_ffn_up, w_ffn_down=m_w_ffn_down)
    given_v = dict(w_in=v_w_in, w_proj_a=v_w_proj_a, w_proj_b=v_w_proj_b, w_out=v_w_out, w_ffn_gate=v_w_ffn_gate,
                   w_ffn_up=v_w_ffn_up, w_ffn_down=v_w_ffn_down)
    w, m, v = ({n: _kernel_layout(n, t[n]) for n in W_NAMES} for t in (given, given_m, given_v))
    small_w = dict(b_forget=b_forget, norm_mix_pre=norm_mix_pre, norm_mix_post=norm_mix_post,
                   norm_ffn_pre=norm_ffn_pre, norm_ffn_post=norm_ffn_post)
    small_m = dict(b_forget=m_b_forget, norm_mix_pre=m_norm_mix_pre, norm_mix_post=m_norm_mix_post,
                   norm_ffn_pre=m_norm_ffn_pre, norm_ffn_post=m_norm_ffn_post)
    small_v = dict(b_forget=v_b_forget, norm_mix_pre=v_norm_mix_pre, norm_mix_post=v_norm_mix_post,
                   norm_ffn_pre=v_norm_ffn_pre, norm_ffn_post=v_norm_ffn_post)

    own = [_halved(w[n].astype(bf16)) for n in W_NAMES]
    chip = 2 * lax.axis_index("x") + lax.axis_index("y")
    exchanged = {"first": _all_gather_async(own[:1], [], "all_gather_first", GATHER_FIRST_ID)}
    fill = lambda ts, mine: [lax.dynamic_update_index_in_dim(t, o, chip, 0) for t, o in zip(ts, mine)]

    def first_weights(ready):
        arrived, _, (w["w_in"], m["w_in"], v["w_in"]) = lax.optimization_barrier(
            (list(exchanged["first"]), ready, (w["w_in"], m["w_in"], v["w_in"])))
        exchanged["late"] = _all_gather_async(own[1:], [arrived[0][0, 0, :16, :128]], "all_gather_late", GATHER_LATE_ID)
        return _full_weights(dict(zip(W_NAMES[:1], fill(arrived, own[:1]))))

    def late_weights(ready):
        arrived, _ = lax.optimization_barrier((list(exchanged["late"]), ready))
        return _full_weights(dict(zip(W_NAMES[1:], fill(arrived, own[1:]))))

    sq_err, grad_x, grads, small = _local_step(x[0], loss_target[0], first_weights, b_forget, norm_mix_pre,
                                               norm_mix_post, norm_ffn_pre, norm_ffn_post, late=late_weights)

    g4 = _sharded_grads(grads)
    stack = lambda t, extra: jnp.concatenate(
        [jnp.pad(t["b_forget"], ((0, 0), (0, D - N_FOX)))] + [t[n] for n in NORMS]
        + [jnp.pad(extra, ((0, SMALL_ROWS - LOSS_ROW - 1), (0, D - extra.shape[1])), constant_values=1.0)], axis=0)
    early, _ = lax.optimization_barrier((list(_pair_swap_early([g4[n] for n in W_NAMES[1:]])), grads["mid_backward"]))
    other = list(_pair_swap([g4["w_in"]])) + early
    parts = [_pair_sum(g4[n], o, "pair_sum_" + n) for n, o in zip(W_NAMES, other)]
    recv_early = _scatter_early(parts[1:])
    recv_in, small_all = _scatter_partials(parts[:1], stack(small, sq_err))

    g_shard, delta, new_m, new_v = {}, {}, {}, {}

    def finish(names, parts, recv):
        halves = [_sum_partials(p, r, "sum_partials_" + n) for n, p, r in zip(names, parts, recv)]
        theirs = _swap_halves(halves, "swap_halves_" + names[0])
        for n, mine, other_half in zip(names, halves, theirs):
            g_shard[n], delta[n], new_m[n], new_v[n] = _adamw_halves(w[n], mine, other_half, m[n], v[n], "adamw_" + n)

    recv_early, _ = lax.optimization_barrier((list(recv_early), parts[0]))
    finish(W_NAMES[1:], parts[1:], recv_early)
    (recv_in, small_all), _ = lax.optimization_barrier(((recv_in, small_all), [delta[n] for n in W_NAMES[1:]]))
    finish(W_NAMES[:1], parts[:1], [recv_in])
    small_sum = _sum_small(small_all)
    loss = small_sum[LOSS_ROW, 0] * (0.5 / D)
    ones = jnp.ones((1, 128), f32)
    sd, sm, sv = _adamw(stack(small_w, ones), small_sum, stack(small_m, ones), stack(small_v, ones), "adamw_small")

    outs = [loss, grad_x[None]]
    for big, st in ((g_shard, small_sum), (delta, sd), (new_m, sm), (new_v, sv)):
        t = {n: _harness_layout(n, big[n]) for n in W_NAMES}
        t["b_forget"] = st[0:1, :N_FOX]
        for i, n in enumerate(NORMS):
            t[n] = st[i + 1:i + 2]
        outs += [t[n] for n in ORDER]
    return tuple(outs)
```

```python
import functools
import math

import jax
import jax.numpy as jnp
import numpy as np
from jax import lax
from jax.experimental import pallas as pl
from jax.experimental.pallas import tpu as pltpu
from jax.experimental.pallas import tpu_sc as plsc

f32 = jnp.float32
bf16 = jnp.bfloat16
SDS = jax.ShapeDtypeStruct
MESH = pl.DeviceIdType.MESH

S = 2048
D = 1024
HD = 64
BLK = 128
N_FOX = 8
FOX_W = N_FOX * HD
DIL_GROUPS = ((128, 1), (512, 4), (2048, 16))
SLOTS = 4
DIL_W = SLOTS * HD
QKV_W = 3 * DIL_W
VR_W = 3 * FOX_W
GF_W = 2 * D + 128
F_FF = 2816
ROPE_DIM = 16
ROPE_THETA = 500000.0
EPS = 1e-6
NEG = -1e30
SCALE = 1.0 / math.sqrt(HD)
IN_COLS = 5896
N_SHARD = 4

ADAM_LR, ADAM_B1, ADAM_B2, ADAM_EPS, ADAM_WD, ADAM_STEP = 0.001, 0.9, 0.999, 1e-08, 0.01, 10

VMEM_V7X = 64 * 1024 * 1024
VMEM_PLAN_MAX = VMEM_V7X - 8 * 1024 * 1024

TM = 512
TQ = 256

W_NAMES = ("w_in", "w_proj_a", "w_proj_b", "w_out", "w_ffn_gate", "w_ffn_up", "w_ffn_down")
TRANSPOSED = ("w_in", "w_ffn_gate", "w_ffn_up")
IN_SHARD = IN_COLS // N_SHARD
IN_SHARD_PAD = 1504
SHARD_SHAPE = dict(w_in=(IN_SHARD_PAD, D), w_proj_a=(DIL_W, D // N_SHARD), w_proj_b=(FOX_W, D // N_SHARD),
                   w_out=(D // N_SHARD, D), w_ffn_gate=(F_FF // N_SHARD, D), w_ffn_up=(F_FF // N_SHARD, D),
                   w_ffn_down=(F_FF // N_SHARD, D))
SMALL_ROWS = 8
LOSS_ROW = 5


def _nbytes(shape, dtype):
    return math.prod(shape) * jnp.dtype(dtype).itemsize


def _params(semantics, block_bytes, temp_bytes=0):
    need = 2 * block_bytes + temp_bytes + (2 << 20)
    return pltpu.CompilerParams(dimension_semantics=semantics, vmem_limit_bytes=int(min(need, VMEM_PLAN_MAX)))


def _row(w, tm=TM):
    return pl.BlockSpec((tm, w), lambda i: (i, 0))


def _vec(w):
    return pl.BlockSpec((1, w), lambda i: (0, 0))


def _mm(pairs, dims, out_dtype, *, tm, tn, name, m_inner=False):
    a0, b0 = pairs[0]
    m_dim = a0.shape[1] if dims == "tn" else a0.shape[0]
    n_dim = b0.shape[0] if dims == "nt" else b0.shape[1]
    contract = {"nn": ((1,), (0,)), "nt": ((1,), (1,)), "tn": ((0,), (0,))}[dims]
    n_pairs = len(pairs)
    assert m_dim % tm == 0 and n_dim % tn == 0, (name, m_dim, n_dim, tm, tn)

    def body(*refs):
        o_ref = refs[-1]
        acc = None
        for p in range(n_pairs):
            a = refs[2 * p][...].astype(bf16)
            b = refs[2 * p + 1][...].astype(bf16)
            t = lax.dot_general(a, b, (contract, ((), ())), preferred_element_type=f32)
            acc = t if acc is None else acc + t
        o_ref[...] = acc.astype(o_ref.dtype)

    if m_inner:
        grid = (n_dim // tn, m_dim // tm)
        mi = lambda j, i: i
        ni = lambda j, i: j
    else:
        grid = (m_dim // tm, n_dim // tn)
        mi = lambda i, j: i
        ni = lambda i, j: j
    in_specs, block_bytes, args = [], 0, []
    for a, b in pairs:
        k_dim = a.shape[0] if dims == "tn" else a.shape[1]
        if dims == "tn":
            in_specs.append(pl.BlockSpec((k_dim, tm), lambda *g: (0, mi(*g))))
        else:
            in_specs.append(pl.BlockSpec((tm, k_dim), lambda *g: (mi(*g), 0)))
        if dims == "nt":
            in_specs.append(pl.BlockSpec((tn, k_dim), lambda *g: (ni(*g), 0)))
        else:
            in_specs.append(pl.BlockSpec((k_dim, tn), lambda *g: (0, ni(*g))))
        block_bytes += _nbytes((tm, k_dim), a.dtype) + _nbytes((tn, k_dim), b.dtype)
        args += [a, b]
    block_bytes += _nbytes((tm, tn), out_dtype)
    temp = _nbytes((tm, tn), f32) * 2 + sum(_nbytes((tm, a.shape[0] if dims == "tn" else a.shape[1]), bf16)
                                            + _nbytes((tn, a.shape[0] if dims == "tn" else a.shape[1]), bf16)
                                            for a, _ in pairs)
    return pl.pallas_call(
        body, grid=grid, in_specs=in_specs,
        out_specs=pl.BlockSpec((tm, tn), lambda *g: (mi(*g), ni(*g))),
        out_shape=SDS((m_dim, n_dim), out_dtype), name=name,
        compiler_params=_params(("parallel", "parallel"), block_bytes, temp),
    )(*args)


def _rms(x, g):
    r = lax.rsqrt(jnp.mean(x * x, axis=-1, keepdims=True) + EPS)
    return x * r * g


def _rms_bwd(x, g, dy):
    r = lax.rsqrt(jnp.mean(x * x, axis=-1, keepdims=True) + EPS)
    xh = x * r
    dxh = dy * g
    dx = r * (dxh - xh * jnp.mean(dxh * xh, axis=-1, keepdims=True))
    return dx, jnp.sum(dy * xh, axis=0, keepdims=True)


def _acc_rows(ref, val):
    @pl.when(pl.program_id(0) == 0)
    def _():
        ref[...] = jnp.zeros_like(ref)
    ref[...] += val


def _norm_fwd(xs, g):
    n = len(xs)

    def body(*refs):
        g = refs[n][...]
        for x_ref, h_ref in zip(refs[:n], refs[n + 1:]):
            h_ref[...] = _rms(x_ref[...], g).astype(bf16)

    return pl.pallas_call(
        body, grid=(S // TM,), in_specs=[_row(D)] * n + [_vec(D)], out_specs=[_row(D)] * n,
        out_shape=[SDS((S, D), bf16)] * n, name="norm_mix_pre",
        compiler_params=_params(("parallel",), 6 * n * TM * D, 8 * n * TM * D))(*xs, g)


def _perm_rows(xs, ds, name):
    n = len(xs)

    def body(*refs):
        outs = iter(refs[n:])
        for x_ref in refs[:n]:
            for d in ds:
                o_ref, rows = next(outs), S // d
                for r in range(d):
                    o_ref[r * rows:(r + 1) * rows, :] = x_ref[pl.ds(r, rows, stride=d), :]

    blk = pl.BlockSpec((S, 128), lambda c: (0, c))
    w = xs[0].shape[1]
    return pl.pallas_call(
        body, grid=(w // 128,), in_specs=[blk] * n, out_specs=[blk] * (n * len(ds)),
        out_shape=[SDS((S, w), f32)] * (n * len(ds)), name=name,
        compiler_params=_params(("parallel",), 4 * S * 128 * n * (1 + len(ds))))(*xs)


def _unperm_sum(nat, perms, ds, name):
    n = len(perms)

    def body(*refs):
        a_ref, o_ref, sc = refs[0], refs[n + 1], refs[n + 2]
        acc = a_ref[...]
        for b_ref, d in zip(refs[1:n + 1], ds):
            rows = S // d
            for r in range(d):
                sc[pl.ds(r, rows, stride=d), :] = b_ref[r * rows:(r + 1) * rows, :]
            acc = acc + sc[...]
        o_ref[...] = acc

    blk = pl.BlockSpec((S, 128), lambda c: (0, c))
    w = nat.shape[1]
    return pl.pallas_call(
        body, grid=(w // 128,), in_specs=[blk] * (n + 1), out_specs=blk, out_shape=SDS((S, w), f32),
        scratch_shapes=[pltpu.VMEM((S, 128), f32)], name=name,
        compiler_params=_params(("parallel",), 4 * S * 128 * (n + 2), 8 * S * 128))(nat, *perms)


def _whole(a):
    return pl.BlockSpec(a.shape, lambda i: (0,) * a.ndim)


def _resid_norm_fwd(x, merged, w_out, g_post, g_pre):
    def body(x_ref, mg_ref, w_ref, gp_ref, gn_ref, mix_ref, x2_ref, h_ref):
        mix = jnp.dot(mg_ref[...], w_ref[...], preferred_element_type=f32)
        x2 = x_ref[...] + _rms(mix, gp_ref[...])
        mix_ref[...] = mix
        x2_ref[...] = x2
        h_ref[...] = _rms(x2, gn_ref[...]).astype(bf16)

    return pl.pallas_call(
        body, grid=(S // TM,), in_specs=[_row(D), _row(D), _whole(w_out), _vec(D), _vec(D)], out_specs=[_row(D)] * 3,
        out_shape=[SDS((S, D), f32), SDS((S, D), f32), SDS((S, D), bf16)], name="proj_out_norm",
        compiler_params=_params(("parallel",), 16 * TM * D + 2 * D * D, 16 * TM * D))(x, merged, w_out, g_post, g_pre)


def _loss_head(x2, a_act, w_down, g_post, target):
    def body(x2_ref, a_ref, w_ref, g_ref, t_ref, loss_ref, dy_ref, dff_ref, dg_ref):
        ff = jnp.dot(a_ref[...], w_ref[...], preferred_element_type=f32)
        g = g_ref[...]
        err = x2_ref[...] + _rms(ff, g) - t_ref[...]
        dy = err * (1.0 / D)
        dff, dg = _rms_bwd(ff, g, dy)
        dy_ref[...] = dy
        dff_ref[...] = dff.astype(bf16)
        _acc_rows(dg_ref, dg)
        _acc_rows(loss_ref, jnp.full((1, 128), jnp.sum(err * err), f32))

    return pl.pallas_call(
        body, grid=(S // TM,), in_specs=[_row(D), _row(F_FF), _whole(w_down), _vec(D), _row(D)],
        out_specs=[_vec(128), _row(D), _row(D), _vec(D)],
        out_shape=[SDS((1, 128), f32), SDS((S, D), f32), SDS((S, D), bf16), SDS((1, D), f32)], name="ffn_down_loss",
        compiler_params=_params(("arbitrary",), 14 * TM * D + 2 * TM * F_FF + 2 * F_FF * D, 28 * TM * D),
    )(x2, a_act, w_down, g_post, target)


def _norm_bwd_mid(dy, d_g, d_u, w_gate_t, w_up_t, x2, mix, g_ffn_pre, g_mix_post):
    def body(dy_ref, dgt_ref, dut_ref, wg_ref, wu_ref, x2_ref, mix_ref, g3_ref, g2_ref, dx2_ref, dmix_ref, dg3_ref, dg2_ref):
        dh = jnp.dot(dgt_ref[...], wg_ref[...], preferred_element_type=f32)
        dh += jnp.dot(dut_ref[...], wu_ref[...], preferred_element_type=f32)
        d3, dg3 = _rms_bwd(x2_ref[...], g3_ref[...], dh)
        dx2 = dy_ref[...] + d3
        dmix, dg2 = _rms_bwd(mix_ref[...], g2_ref[...], dx2)
        dx2_ref[...] = dx2
        dmix_ref[...] = dmix.astype(bf16)
        _acc_rows(dg3_ref, dg3)
        _acc_rows(dg2_ref, dg2)

    tm = TM // 2
    row = lambda w: _row(w, tm)
    return pl.pallas_call(
        body, grid=(S // tm,),
        in_specs=[row(D), row(F_FF), row(F_FF), _whole(w_gate_t), _whole(w_up_t), row(D), row(D), _vec(D), _vec(D)],
        out_specs=[row(D), row(D), _vec(D), _vec(D)],
        out_shape=[SDS((S, D), f32), SDS((S, D), bf16), SDS((1, D), f32), SDS((1, D), f32)], name="ffn_bwd_in_norm",
        compiler_params=_params(("arbitrary",), 18 * tm * D + 4 * tm * F_FF + 4 * F_FF * D, 28 * tm * D),
    )(dy, d_g, d_u, w_gate_t, w_up_t, x2, mix, g_ffn_pre, g_mix_post)


def _norm_bwd_in(dx2, dh1, x, g):
    def body(dx2_ref, dh_ref, x_ref, g_ref, gx_ref, dg_ref):
        d1, dg = _rms_bwd(x_ref[...], g_ref[...], dh_ref[...])
        gx_ref[...] = dx2_ref[...] + d1
        _acc_rows(dg_ref, dg)

    return pl.pallas_call(
        body, grid=(S // TM,), in_specs=[_row(D)] * 3 + [_vec(D)], out_specs=[_row(D), _vec(D)],
        out_shape=[SDS((S, D), f32), SDS((1, D), f32)], name="norm_bwd_in",
        compiler_params=_params(("arbitrary",), 16 * TM * D, 16 * TM * D))(dx2, dh1, x, g)


def _rope_tables():
    half = ROPE_DIM // 2
    inv_freq = np.power(np.float32(ROPE_THETA), -np.arange(0, ROPE_DIM, 2, dtype=np.float32) / np.float32(ROPE_DIM))
    row = np.arange(S)
    groups = []
    for _, d in DIL_GROUPS:
        pos = ((row % (S // d)) * d + row // (S // d)).astype(np.float32)
        ang = pos[:, None] * inv_freq[None, :].astype(np.float32)
        cos, sin = np.cos(ang).astype(np.float32), np.sin(ang).astype(np.float32)
        c = np.concatenate([cos, cos, np.ones((S, HD - ROPE_DIM), np.float32)], axis=1)
        s_lo = np.concatenate([-sin, np.zeros((S, HD - half), np.float32)], axis=1)
        s_hi = np.concatenate([np.zeros((S, half), np.float32), sin, np.zeros((S, HD - ROPE_DIM), np.float32)], axis=1)
        groups.append(np.stack([np.concatenate([t, t], axis=1) for t in (c, s_lo, s_hi)]))
    return jnp.asarray(np.stack(groups))


def _rotate(x, c, lo, hi, sign):
    tile = lambda t: jnp.tile(t, (1, DIL_W // 128))
    return (x * tile(c) + pltpu.roll(x, DIL_W - ROPE_DIM // 2, 1) * (tile(lo) * sign)
            + pltpu.roll(x, ROPE_DIM // 2, 1) * (tile(hi) * sign))


def _table_specs(g):
    return [pl.BlockSpec((None, None, TM, 128), lambda i, k=k: (g, k, i, 0)) for k in range(3)]


def _rope_fwd(g, p_qkv, tables):
    def body(x_ref, c_ref, lo_ref, hi_ref, o_ref):
        c, lo, hi = c_ref[...], lo_ref[...], hi_ref[...]
        for part in range(2):
            cols = slice(part * DIL_W, (part + 1) * DIL_W)
            o_ref[:, cols] = _rotate(x_ref[:, cols], c, lo, hi, 1.0).astype(bf16)
        o_ref[:, 2 * DIL_W:] = x_ref[:, 2 * DIL_W:].astype(bf16)

    return pl.pallas_call(
        body, grid=(S // TM,), in_specs=[_row(QKV_W)] + _table_specs(g), out_specs=_row(QKV_W),
        out_shape=SDS((S, QKV_W), bf16), name=f"rope_fwd_{g}",
        compiler_params=_params(("parallel",), 6 * TM * QKV_W + 12 * TM * 128, 24 * TM * QKV_W))(p_qkv, tables, tables, tables)


def _rope_bwd(g, dq, dk, dv, tables):
    def body(dq_ref, dk_ref, dv_ref, c_ref, lo_ref, hi_ref, o_ref):
        c, lo, hi = c_ref[...], lo_ref[...], hi_ref[...]
        o_ref[:, :DIL_W] = _rotate(dq_ref[...], c, lo, hi, -1.0).astype(bf16)
        o_ref[:, DIL_W:2 * DIL_W] = _rotate(dk_ref[...], c, lo, hi, -1.0).astype(bf16)
        o_ref[:, 2 * DIL_W:] = dv_ref[...].astype(bf16)

    return pl.pallas_call(
        body, grid=(S // TM,), in_specs=[_row(DIL_W)] * 3 + _table_specs(g), out_specs=_row(QKV_W),
        out_shape=SDS((S, QKV_W), bf16), name=f"rope_bwd_{g}",
        compiler_params=_params(("parallel",), 6 * TM * QKV_W + 12 * TM * 128, 24 * TM * QKV_W))(dq, dk, dv, tables, tables, tables)


def _nt(a, b):
    return lax.dot_general(a, b, (((1,), (1,)), ((), ())), preferred_element_type=f32)


def _tn(a, b):
    return lax.dot_general(a, b, (((0,), (0,)), ((), ())), preferred_element_type=f32)


STEP_BLOCKS = 4
STEP_ROWS = STEP_BLOCKS * BLK


def _dil_prev(g, b):
    _, d = DIL_GROUPS[g]
    nb = S // d // BLK
    if nb == 1 or (b == 0 and nb <= STEP_BLOCKS):
        return None
    return "in" if b > 0 else "halo"


def _bnt(a, b):
    return lax.dot_general(a, b, (((2,), (2,)), ((0,), (0,))), preferred_element_type=f32)


def _bnn(a, b):
    return lax.dot_general(a, b, (((2,), (1,)), ((0,), (0,))), preferred_element_type=f32)


def _btn(a, b):
    return lax.dot_general(a, b, (((1,), (1,)), ((0,), (0,))), preferred_element_type=f32)


def _on_tail(x, tail, fn):
    if tail == x.shape[0]:
        return fn(x)
    return jnp.concatenate([x[:-tail], fn(x[-tail:])], axis=0)


def _heads(ref, part):
    n = ref.shape[0] // BLK
    return jnp.stack([ref[b * BLK:(b + 1) * BLK, part * DIL_W + h * HD:part * DIL_W + (h + 1) * HD]
                      for b in range(n) for h in range(SLOTS)])


def _dil_operands(g, qkv_ref, halo_ref):
    q, kc, vc = (_heads(qkv_ref, part) for part in range(3))
    qi = lax.broadcasted_iota(jnp.int32, (1, BLK, BLK), 1)
    kj = lax.broadcasted_iota(jnp.int32, (1, BLK, BLK), 2)
    with_prev = [b for b in range(STEP_BLOCKS) if _dil_prev(g, b) is not None]
    tail = SLOTS * len(with_prev)
    if not tail:
        return q, kc, vc, None, None, kj <= qi, None, 0
    assert with_prev == list(range(STEP_BLOCKS - len(with_prev), STEP_BLOCKS))
    inside = SLOTS * sum(_dil_prev(g, b) == "in" for b in with_prev)
    kp, vp, prev = kc[:inside], vc[:inside], jnp.broadcast_to(kj >= qi, (inside, BLK, BLK))
    if inside < tail:
        no_halo = jnp.where(pl.program_id(0) == 0, BLK + 1, 0)
        kp = jnp.concatenate([_heads(halo_ref, 1), kp], axis=0)
        vp = jnp.concatenate([_heads(halo_ref, 2), vp], axis=0)
        prev = jnp.concatenate([jnp.broadcast_to(kj >= qi + no_halo, (SLOTS, BLK, BLK)), prev], axis=0)
    return q, kc, vc, kp, vp, kj <= qi, prev, tail


def _dil_in_specs(g, n_aux):
    step = lambda w: pl.BlockSpec((STEP_ROWS, w), lambda i: (i, 0))
    halo = [pl.BlockSpec((BLK, QKV_W), lambda i: (jnp.maximum(i * STEP_BLOCKS - 1, 0), 0))]
    needs_halo = _dil_prev(g, 0) == "halo"
    return [step(QKV_W)] + (halo if needs_halo else []) + [step(DIL_W)] * n_aux, needs_halo


def _dil_fwd(g, qkv):
    in_specs, needs_halo = _dil_in_specs(g, 0)

    def body(*refs):
        qkv_ref, halo_ref = refs[0], refs[1] if needs_halo else None
        o_ref, lse_ref = refs[-2:]
        q, kc, vc, kp, vp, cur, prev, tail = _dil_operands(g, qkv_ref, halo_ref)
        sc = jnp.where(cur, _bnt(q, kc) * SCALE, NEG)
        m = jnp.max(sc, axis=-1, keepdims=True)
        if tail:
            sp = jnp.where(prev, _bnt(q[-tail:], kp) * SCALE, NEG)
            m = _on_tail(m, tail, lambda t: jnp.maximum(t, jnp.max(sp, axis=-1, keepdims=True)))
            pp = jnp.exp(sp - m[-tail:])
        pc = jnp.exp(sc - m)
        den = jnp.sum(pc, axis=-1, keepdims=True)
        if tail:
            den = _on_tail(den, tail, lambda t: t + jnp.sum(pp, axis=-1, keepdims=True))
        inv = 1.0 / den
        o = _bnn((pc * inv).astype(bf16), vc)
        if tail:
            o = _on_tail(o, tail, lambda t: t + _bnn((pp * inv[-tail:]).astype(bf16), vp))
        lse = m + jnp.log(den)
        for b in range(STEP_BLOCKS):
            for h in range(SLOTS):
                rows, hs = slice(b * BLK, (b + 1) * BLK), slice(h * HD, (h + 1) * HD)
                o_ref[rows, hs] = o[SLOTS * b + h]
                lse_ref[rows, hs] = jnp.broadcast_to(lse[SLOTS * b + h], (BLK, HD))

    out = pl.BlockSpec((STEP_ROWS, DIL_W), lambda i: (i, 0))
    return pl.pallas_call(
        body, grid=(S // STEP_ROWS,), in_specs=in_specs, out_specs=[out, out], out_shape=[SDS((S, DIL_W), f32)] * 2,
        name=f"dil_fwd_{g}", compiler_params=_params(("parallel",), 12 * STEP_ROWS * DIL_W, 2 << 20),
    )(*([qkv] * (2 if needs_halo else 1)))


def _dil_combine(outs, lses):
    def body(o0, o1, o2, l0, l1, l2, out_ref, lse_ref, so1, so2, sl1, sl2):
        for (_, d), src, dst in ((DIL_GROUPS[1], o1, so1), (DIL_GROUPS[2], o2, so2),
                                 (DIL_GROUPS[1], l1, sl1), (DIL_GROUPS[2], l2, sl2)):
            rows = S // d
            for r in range(d):
                dst[pl.ds(r, rows, stride=d), :] = src[r * rows:(r + 1) * rows, :]
        a, b, c = l0[...], sl1[...], sl2[...]
        m = jnp.maximum(jnp.maximum(a, b), c)
        ea, eb, ec = jnp.exp(a - m), jnp.exp(b - m), jnp.exp(c - m)
        z = ea + eb + ec
        inv = 1.0 / z
        out_ref[...] = (ea * inv) * o0[...] + (eb * inv) * so1[...] + (ec * inv) * so2[...]
        lse_ref[...] = m + jnp.log(z)

    blk = pl.BlockSpec((S, 128), lambda c: (0, c))
    return pl.pallas_call(
        body, grid=(DIL_W // 128,), in_specs=[blk] * 6, out_specs=[blk] * 2,
        out_shape=[SDS((S, DIL_W), f32)] * 2, scratch_shapes=[pltpu.VMEM((S, 128), f32)] * 4, name="dil_combine",
        compiler_params=_params(("parallel",), 32 * S * 128, 32 * S * 128))(*outs, *lses)


def _dil_bwd(g, qkv, d_out, delta, lse):
    in_specs, needs_halo = _dil_in_specs(g, 3)

    def body(*refs):
        qkv_ref, halo_ref = refs[0], refs[1] if needs_halo else None
        do_ref, dl_ref, lse_ref, dq_ref, dk_ref, dv_ref = refs[-6:]
        q, kc, vc, kp, vp, cur, prev, tail = _dil_operands(g, qkv_ref, halo_ref)
        tiles = [(slice(b * BLK, (b + 1) * BLK), h) for b in range(STEP_BLOCKS) for h in range(SLOTS)]
        do = jnp.stack([do_ref[rows, h * HD:(h + 1) * HD] for rows, h in tiles]).astype(bf16)
        lse = jnp.stack([lse_ref[rows, h * HD:h * HD + 1] for rows, h in tiles])
        delta = jnp.stack([dl_ref[rows, h * HD:h * HD + 1] for rows, h in tiles])

        def probs(q, k, mask, lse, do, v, delta):
            p = jnp.exp(jnp.where(mask, _bnt(q, k) * SCALE, NEG) - lse)
            ds = p * (_bnt(do, v) - delta) * SCALE
            return p.astype(bf16), ds.astype(bf16)

        p, ds = probs(q, kc, cur, lse, do, vc, delta)
        dq, dk, dv = _bnn(ds, kc), _btn(ds, q), _btn(p, do)
        if tail:
            p, ds = probs(q[-tail:], kp, prev, lse[-tail:], do[-tail:], vp, delta[-tail:])
            dq = _on_tail(dq, tail, lambda t: t + _bnn(ds, kp))
            dk_p, dv_p = _btn(ds, q[-tail:]), _btn(p, do[-tail:])
            inside = tail - SLOTS if needs_halo else tail
            pad = jnp.zeros((len(tiles) - inside, BLK, HD), f32)
            dk = dk + jnp.concatenate([dk_p[tail - inside:], pad], axis=0)
            dv = dv + jnp.concatenate([dv_p[tail - inside:], pad], axis=0)
        first = pl.multiple_of(pl.program_id(0) * STEP_ROWS, STEP_ROWS)
        for t, (rows, h) in enumerate(tiles):
            hs = slice(h * HD, (h + 1) * HD)
            own = pl.ds(pl.multiple_of(first + rows.start, BLK), BLK)
            dq_ref[rows, hs] = dq[t]
            dk_ref[own, hs] = dk[t]
            dv_ref[own, hs] = dv[t]
        if needs_halo:
            before = pl.ds(pl.multiple_of(jnp.maximum(first - BLK, 0), BLK), BLK)
            for h in range(SLOTS):
                hs = slice(h * HD, (h + 1) * HD)
                dk_ref[before, hs] += dk_p[h]
                dv_ref[before, hs] += dv_p[h]

    whole = pl.BlockSpec((S, DIL_W), lambda i: (0, 0))
    return pl.pallas_call(
        body, grid=(S // STEP_ROWS,), in_specs=in_specs,
        out_specs=[pl.BlockSpec((STEP_ROWS, DIL_W), lambda i: (i, 0)), whole, whole],
        out_shape=[SDS((S, DIL_W), f32)] * 3, name=f"dil_bwd_{g}",
        compiler_params=_params(("arbitrary",), 20 * STEP_ROWS * DIL_W + 8 * S * DIL_W, 2 << 20),
    )(*([qkv] * (2 if needs_halo else 1)), d_out, delta, lse)


def _scan_rows(x, reverse):
    row = lax.broadcasted_iota(jnp.int32, x.shape, 0)
    k = 1
    while k < S:
        if reverse:
            x = x + jnp.where(row < S - k, pltpu.roll(x, S - k, 0), 0.0)
        else:
            x = x + jnp.where(row >= k, pltpu.roll(x, k, 0), 0.0)
        k *= 2
    return x


N_PAIR = N_FOX // 2
_PAIR_Q = pl.BlockSpec((None, S, 128), lambda p: (p, 0, 0))
_PAIR_K = pl.BlockSpec((None, 8, S), lambda p: (p, 0, 0))


def _forget_fwd(fz, b128):
    def body(z_ref, b_ref, fq_ref, fk_ref):
        z = z_ref[...] + b_ref[...]
        logf = jnp.minimum(z, 0.0) - jnp.log1p(jnp.exp(-jnp.abs(z)))
        f_cum = _scan_rows(logf, reverse=False)
        f_cum_t = f_cum.T
        fq_ref[...] = jnp.zeros_like(fq_ref)
        fk_ref[...] = jnp.zeros_like(fk_ref)
        for p in range(N_PAIR):
            fq_ref[p, :, 0:2] = f_cum[:, 2 * p:2 * p + 2]
            fk_ref[p, 0:2, :] = f_cum_t[2 * p:2 * p + 2, :]

    return pl.pallas_call(
        body, grid=(1,), in_specs=[pl.BlockSpec((S, 128), lambda i: (0, 0)), _vec(128)],
        out_specs=[pl.BlockSpec((N_PAIR, S, 128), lambda i: (0, 0, 0)), pl.BlockSpec((N_PAIR, 8, S), lambda i: (0, 0, 0))],
        out_shape=[SDS((N_PAIR, S, 128), f32), SDS((N_PAIR, 8, S), f32)], name="forget_fwd",
        compiler_params=_params(("arbitrary",), 24 * S * 128, 24 * S * 128))(fz, b128)


def _forget_bwd(fz, b128, d_f_cols, d_f_rows):
    def body(z_ref, b_ref, dfc_ref, dfr_ref, dz_ref, db_ref, df_sc):
        z = z_ref[...] + b_ref[...]
        df_sc[...] = jnp.zeros_like(df_sc)
        for p in range(N_PAIR):
            df_sc[:, 2 * p:2 * p + 2] = dfr_ref[p, :, 0:2] + dfc_ref[p].T[:, 0:2]
        dz = _scan_rows(df_sc[...], reverse=True) * jax.nn.sigmoid(-z)
        dz_ref[...] = dz
        db_ref[...] = jnp.sum(dz, axis=0, keepdims=True)

    full = pl.BlockSpec((S, 128), lambda i: (0, 0))
    return pl.pallas_call(
        body, grid=(1,),
        in_specs=[full, _vec(128), pl.BlockSpec((N_PAIR, 8, S), lambda i: (0, 0, 0)), pl.BlockSpec((N_PAIR, S, 128), lambda i: (0, 0, 0))],
        out_specs=[full, _vec(128)], out_shape=[SDS((S, 128), f32), SDS((1, 128), f32)],
        scratch_shapes=[pltpu.VMEM((S, 128), f32)], name="forget_bwd",
        compiler_params=_params(("arbitrary",), 32 * S * 128, 24 * S * 128))(fz, b128, d_f_cols, d_f_rows)


def _fox_scores(q_ref, k_ref, fq_ref, fk_ref, qi, hh):
    n = (qi + 1) * TQ
    rows, hs = slice(qi * TQ, n), slice(hh * HD, (hh + 1) * HD)
    s = _nt(q_ref[rows, hs], k_ref[0:n, hs]) * SCALE + (fq_ref[rows, hh:hh + 1] - fk_ref[hh:hh + 1, 0:n])
    qpos = qi * TQ + lax.broadcasted_iota(jnp.int32, (TQ, n), 0)
    kpos = lax.broadcasted_iota(jnp.int32, (TQ, n), 1)
    return jnp.where(kpos <= qpos, s, NEG)


def _pair_cols(first):
    return pl.BlockSpec((S, 128), lambda p: (0, first + p))


def _fox_fwd(vr, fq, fk):
    def body(q_ref, k_ref, v_ref, fq_ref, fk_ref, o_ref, lse_ref):
        lse_ref[...] = jnp.zeros_like(lse_ref)
        for hh in range(2):
            hs = slice(hh * HD, (hh + 1) * HD)
            for qi in range(S // TQ):
                n = (qi + 1) * TQ
                rows = slice(qi * TQ, n)
                s = _fox_scores(q_ref, k_ref, fq_ref, fk_ref, qi, hh)
                m = jnp.max(s, axis=-1, keepdims=True)
                p = jnp.exp(s - m)
                den = jnp.sum(p, axis=-1, keepdims=True)
                o_ref[rows, hs] = jnp.dot((p * (1.0 / den)).astype(bf16), v_ref[0:n, hs], preferred_element_type=f32)
                lse_ref[rows, hh:hh + 1] = m + jnp.log(den)

    return pl.pallas_call(
        body, grid=(N_PAIR,), in_specs=[_pair_cols(0), _pair_cols(N_PAIR), _pair_cols(2 * N_PAIR), _PAIR_Q, _PAIR_K],
        out_specs=[_pair_cols(0), _PAIR_Q], out_shape=[SDS((S, FOX_W), f32), SDS((N_PAIR, S, 128), f32)],
        name="fox_fwd", compiler_params=_params(("parallel",), 12 * S * 128, 16 * TQ * S),
    )(vr, vr, vr, fq, fk)


def _fox_bwd(vr, fq, fk, lse, d_out, delta):
    def body(q_ref, k_ref, v_ref, do_ref, fq_ref, fk_ref, lse_ref, dl_ref, dq_ref, dk_ref, dv_ref, dfc_ref, dfr_ref,
             dk_sc, dv_sc):
        dfc_ref[...] = jnp.zeros_like(dfc_ref)
        dfr_ref[...] = jnp.zeros_like(dfr_ref)
        for hh in range(2):
            hs = slice(hh * HD, (hh + 1) * HD)
            dk_sc[...] = jnp.zeros_like(dk_sc)
            dv_sc[...] = jnp.zeros_like(dv_sc)
            for qi in range(S // TQ):
                n = (qi + 1) * TQ
                rows = slice(qi * TQ, n)
                q, do, k, v = q_ref[rows, hs], do_ref[rows, hs], k_ref[0:n, hs], v_ref[0:n, hs]
                p = jnp.exp(_fox_scores(q_ref, k_ref, fq_ref, fk_ref, qi, hh) - lse_ref[rows, hh:hh + 1])
                ds = p * (_nt(do, v) - dl_ref[rows, hh:hh + 1])
                dsb = ds.astype(bf16)
                dq_ref[rows, hs] = jnp.dot(dsb, k, preferred_element_type=f32) * SCALE
                dk_sc[0:n, :] += _tn(dsb, q) * SCALE
                dv_sc[0:n, :] += _tn(p.astype(bf16), do)
                dfc_ref[hh:hh + 1, 0:n] -= jnp.sum(ds, axis=0, keepdims=True)
                dfr_ref[rows, hh:hh + 1] = jnp.sum(ds, axis=-1, keepdims=True)
            dk_ref[:, hs] = dk_sc[...]
            dv_ref[:, hs] = dv_sc[...]

    cols = [_pair_cols(k * N_PAIR) for k in range(3)]
    return pl.pallas_call(
        body, grid=(N_PAIR,), in_specs=cols + [_pair_cols(0), _PAIR_Q, _PAIR_K, _PAIR_Q, _PAIR_Q],
        out_specs=[_pair_cols(0)] * 3 + [_PAIR_K, _PAIR_Q],
        out_shape=[SDS((S, FOX_W), f32)] * 3 + [SDS((N_PAIR, 8, S), f32), SDS((N_PAIR, S, 128), f32)],
        scratch_shapes=[pltpu.VMEM((S, HD), f32)] * 2, name="fox_bwd",
        compiler_params=_params(("parallel",), 32 * S * 128, 24 * TQ * S),
    )(vr, vr, vr, d_out, fq, fk, lse, delta)


def _merge_fwd(out_a, out_b, w_a, w_b, gf):
    cw = D // N_SHARD

    def body(oa_ref, ob_ref, wa_ref, wb_ref, ga_ref, gb_ref, ya_ref, yb_ref, mg_ref):
        oa, ob = oa_ref[...].astype(bf16), ob_ref[...].astype(bf16)
        for j in range(N_SHARD):
            cols = slice(j * cw, (j + 1) * cw)
            ya = jnp.dot(oa, wa_ref[j], preferred_element_type=f32)
            yb = jnp.dot(ob, wb_ref[j], preferred_element_type=f32)
            ya_ref[:, cols] = ya
            yb_ref[:, cols] = yb
            mg_ref[:, cols] = (jax.nn.sigmoid(ga_ref[:, cols]) * ya + jax.nn.sigmoid(gb_ref[:, cols]) * yb).astype(bf16)

    full = lambda a: pl.BlockSpec(a.shape, lambda i: (0, 0, 0))
    return pl.pallas_call(
        body, grid=(S // TM,),
        in_specs=[_row(DIL_W), _row(FOX_W), full(w_a), full(w_b), _row(D), pl.BlockSpec((TM, D), lambda i: (i, 1))],
        out_specs=[_row(D)] * 3, out_shape=[SDS((S, D), f32), SDS((S, D), f32), SDS((S, D), bf16)], name="merge_fwd",
        compiler_params=_params(("parallel",), 22 * TM * D + 2 * (DIL_W + FOX_W) * D, 16 * TM * D),
    )(out_a, out_b, w_a, w_b, gf, gf)


def _merge_bwd(d_mix, w_out, ya, yb, gf):
    def body(dx_ref, w_ref, ya_ref, yb_ref, ga_ref, gb_ref, dya_ref, dyb_ref, dg_ref):
        dm = _nt(dx_ref[...], w_ref[...])
        sa, sb = jax.nn.sigmoid(ga_ref[...]), jax.nn.sigmoid(gb_ref[...])
        dya_ref[...] = (dm * sa).astype(bf16)
        dyb_ref[...] = (dm * sb).astype(bf16)
        dg_ref[:, :D] = (dm * ya_ref[...] * sa * (1.0 - sa)).astype(bf16)
        dg_ref[:, D:] = (dm * yb_ref[...] * sb * (1.0 - sb)).astype(bf16)

    return pl.pallas_call(
        body, grid=(S // TM,),
        in_specs=[_row(D), _whole(w_out)] + [_row(D)] * 3 + [pl.BlockSpec((TM, D), lambda i: (i, 1))],
        out_specs=[_row(D), _row(D), _row(2 * D)],
        out_shape=[SDS((S, D), bf16), SDS((S, D), bf16), SDS((S, 2 * D), bf16)], name="proj_out_bwd_merge",
        compiler_params=_params(("parallel",), 26 * TM * D + 2 * D * D, 28 * TM * D))(d_mix, w_out, ya, yb, gf, gf)


def _branch_bwd(d_ya, d_yb, w_a, w_b, out_a, out_b):
    cw = D // N_SHARD

    def body(dya_ref, dyb_ref, wa_ref, wb_ref, oa_ref, ob_ref, doa_ref, dla_ref, dob_ref, dlb_ref):
        doa = jnp.zeros((TM, DIL_W), f32)
        dob = jnp.zeros((TM, FOX_W), f32)
        for j in range(N_SHARD):
            cols = slice(j * cw, (j + 1) * cw)
            doa += _nt(dya_ref[:, cols], wa_ref[j])
            dob += _nt(dyb_ref[:, cols], wb_ref[j])
        doa_ref[...] = doa
        dob_ref[...] = dob.astype(bf16)
        prod_a = doa * oa_ref[...]
        for h in range(SLOTS):
            hs = slice(h * HD, (h + 1) * HD)
            dla_ref[:, hs] = jnp.broadcast_to(jnp.sum(prod_a[:, hs], axis=-1, keepdims=True), (TM, HD))
        prod_b = dob * ob_ref[...]
        dlb_ref[...] = jnp.zeros_like(dlb_ref)
        for h in range(N_FOX):
            dlb_ref[h // 2, :, h % 2:h % 2 + 1] = jnp.sum(prod_b[:, h * HD:(h + 1) * HD], axis=-1, keepdims=True)

    full = lambda a: pl.BlockSpec(a.shape, lambda i: (0, 0, 0))
    return pl.pallas_call(
        body, grid=(S // TM,),
        in_specs=[_row(D), _row(D), full(w_a), full(w_b), _row(DIL_W), _row(FOX_W)],
        out_specs=[_row(DIL_W), _row(DIL_W), _row(FOX_W), pl.BlockSpec((N_PAIR, TM, 128), lambda i: (0, i, 0))],
        out_shape=[SDS((S, DIL_W), f32), SDS((S, DIL_W), f32), SDS((S, FOX_W), bf16), SDS((N_PAIR, S, 128), f32)],
        name="branch_bwd", compiler_params=_params(("parallel",), 8 * TM * D + 2 * (DIL_W + FOX_W) * D, 8 * TM * D),
    )(d_ya, d_yb, w_a, w_b, out_a, out_b)


def _branch_grads(out_a, out_b, d_ya, d_yb):
    cw = D // N_SHARD

    def body(oa_ref, ob_ref, dya_ref, dyb_ref, ga_ref, gb_ref):
        ga_ref[...] = _tn(oa_ref[...].astype(bf16), dya_ref[...]).astype(bf16)
        gb_ref[...] = _tn(ob_ref[...].astype(bf16), dyb_ref[...]).astype(bf16)

    whole = lambda w: pl.BlockSpec((S, w), lambda j: (0, 0))
    cols = pl.BlockSpec((S, cw), lambda j: (0, j))
    return pl.pallas_call(
        body, grid=(N_SHARD,), in_specs=[whole(DIL_W), whole(FOX_W), cols, cols],
        out_specs=[pl.BlockSpec((None, DIL_W, cw), lambda j: (j, 0, 0)), pl.BlockSpec((None, FOX_W, cw), lambda j: (j, 0, 0))],
        out_shape=[SDS((N_SHARD, DIL_W, cw), bf16), SDS((N_SHARD, FOX_W, cw), bf16)], name="grad_w_proj_ab",
        compiler_params=_params(("parallel",), 4 * S * (DIL_W + FOX_W) + 4 * S * cw + 4 * (DIL_W + FOX_W) * cw,
                                4 * S * (DIL_W + FOX_W)))(out_a, out_b, d_ya, d_yb)


FF_TN = F_FF // 2
FF_TM = 1024


def _ffn_fwd(h, w_gate_t, w_up_t):
    def body(h_ref, wg_ref, wu_ref, g_ref, u_ref, a_ref):
        hb = h_ref[...]
        g = _nt(hb, wg_ref[...])
        u = _nt(hb, wu_ref[...])
        g_ref[...] = g
        u_ref[...] = u
        a_ref[...] = (g * jax.nn.sigmoid(g) * u).astype(bf16)

    tile = pl.BlockSpec((FF_TM, FF_TN), lambda j, i: (i, j))
    wspec = pl.BlockSpec((FF_TN, D), lambda j, i: (j, 0))
    return pl.pallas_call(
        body, grid=(F_FF // FF_TN, S // FF_TM),
        in_specs=[pl.BlockSpec((FF_TM, D), lambda j, i: (i, 0)), wspec, wspec], out_specs=[tile] * 3,
        out_shape=[SDS((S, F_FF), f32), SDS((S, F_FF), f32), SDS((S, F_FF), bf16)], name="ffn_fwd",
        compiler_params=_params(("parallel", "parallel"), 2 * FF_TM * D + 4 * D * FF_TN + 10 * FF_TM * FF_TN, 16 * FF_TM * FF_TN),
    )(h, w_gate_t, w_up_t)


def _ffn_bwd_act(d_ff, w_down, g_act, u_act):
    def body(d_ref, wd_ref, g_ref, u_ref, dg_ref, du_ref):
        da = _nt(d_ref[...], wd_ref[...])
        g = g_ref[...]
        sg = jax.nn.sigmoid(g)
        du_ref[...] = (da * g * sg).astype(bf16)
        dg_ref[...] = (da * u_ref[...] * sg * (1.0 + g * (1.0 - sg))).astype(bf16)

    tile = pl.BlockSpec((FF_TM, FF_TN), lambda j, i: (i, j))
    return pl.pallas_call(
        body, grid=(F_FF // FF_TN, S // FF_TM),
        in_specs=[pl.BlockSpec((FF_TM, D), lambda j, i: (i, 0)), pl.BlockSpec((FF_TN, D), lambda j, i: (j, 0)), tile, tile],
        out_specs=[tile, tile], out_shape=[SDS((S, F_FF), bf16)] * 2, name="ffn_bwd_act",
        compiler_params=_params(("parallel", "parallel"), 2 * FF_TM * D + 2 * D * FF_TN + 12 * FF_TM * FF_TN, 16 * FF_TM * FF_TN),
    )(d_ff, w_down, g_act, u_act)


def _row_tile(rows):
    return next(t for t in (376, 128, 176, 64, 32, 16, 8) if rows % t == 0)


def _adamw_math(w, g, m, v):
    c1 = 1.0 - ADAM_B1 ** ADAM_STEP
    c2 = 1.0 - ADAM_B2 ** ADAM_STEP
    m_new = ADAM_B1 * m + (1.0 - ADAM_B1) * g
    v_new = ADAM_B2 * v + (1.0 - ADAM_B2) * (g * g)
    return -ADAM_LR * ((m_new / c1) / (jnp.sqrt(v_new / c2) + ADAM_EPS) + ADAM_WD * w), m_new, v_new


def _adamw(w, g, m, v, name):
    rows, cols = w.shape
    tm = _row_tile(rows)

    def body(w_ref, g_ref, m_ref, v_ref, d_ref, nm_ref, nv_ref):
        d_ref[...], nm_ref[...], nv_ref[...] = _adamw_math(w_ref[...], g_ref[...], m_ref[...], v_ref[...])

    spec = pl.BlockSpec((tm, cols), lambda i: (i, 0))
    return pl.pallas_call(
        body, grid=(rows // tm,), in_specs=[spec] * 4, out_specs=[spec] * 3, out_shape=[SDS(w.shape, f32)] * 3,
        name=name, compiler_params=_params(("parallel",), 28 * tm * cols, 16 * tm * cols))(w, g, m, v)


def _adamw_halves(w, g_mine, g_theirs, m, v, name):
    rows, cols = w.shape
    tm = _row_tile(rows // 2)
    per_half = rows // 2 // tm
    core = lax.axis_index("c").astype(jnp.int32).reshape(1)

    def body(c_ref, w_ref, gm_ref, gt_ref, m_ref, v_ref, g_ref, d_ref, nm_ref, nv_ref):
        mine = pl.program_id(0) // per_half == c_ref[0]
        g = jnp.where(mine, gm_ref[...], gt_ref[...])
        g_ref[...] = g
        d_ref[...], nm_ref[...], nv_ref[...] = _adamw_math(w_ref[...], g, m_ref[...], v_ref[...])

    spec = pl.BlockSpec((tm, cols), lambda i, c_ref: (i, 0))
    in_half = lambda i, first: jnp.clip(i - first * per_half, 0, per_half - 1)
    grid_spec = pltpu.PrefetchScalarGridSpec(
        num_scalar_prefetch=1, grid=(rows // tm,),
        in_specs=[spec, pl.BlockSpec((tm, cols), lambda i, c_ref: (in_half(i, c_ref[0]), 0)),
                  pl.BlockSpec((tm, cols), lambda i, c_ref: (in_half(i, 1 - c_ref[0]), 0)), spec, spec],
        out_specs=[spec] * 4)
    return pl.pallas_call(
        body, grid_spec=grid_spec, out_shape=[SDS(w.shape, f32)] * 4, name=name,
        compiler_params=_params(("parallel",), 36 * tm * cols, 16 * tm * cols))(core, w, g_mine, g_theirs, m, v)


_ANY = pl.BlockSpec(memory_space=pl.ANY)


def _place():
    x, y, c = lax.axis_index("x"), lax.axis_index("y"), lax.axis_index("c")
    chips = [(1 - x, y), (x, 1 - y), (1 - x, 1 - y)]
    return x, y, c, chips


def _halved(t):
    return t.reshape(t.shape[:-2] + (2, t.shape[-2] // 2, t.shape[-1]))


def _gather_body(src, out, send_ici, recv_ici, send_d2d, recv_d2d):
    x, y, c, chips = _place()
    sibling = (x, y, 1 - c)
    me_j = 2 * x + y
    sends = []
    for a in range(len(src)):
        for p in range(3):
            cp = pltpu.make_async_remote_copy(
                src_ref=src[a].at[c], dst_ref=out[a].at[me_j, c], send_sem=send_ici.at[a, p],
                recv_sem=recv_ici.at[a, p], device_id=(*chips[p], c), device_id_type=MESH)
            cp.start()
            sends.append(cp)
    for a in range(len(src)):
        for p, (px, py) in enumerate(chips):
            blk = out[a].at[2 * px + py, c]
            pltpu.make_async_remote_copy(
                src_ref=blk, dst_ref=blk, send_sem=send_ici.at[a, p], recv_sem=recv_ici.at[a, p],
                device_id=sibling, device_id_type=MESH).wait_recv()
            fw = pltpu.make_async_remote_copy(
                src_ref=blk, dst_ref=blk, send_sem=send_d2d.at[a, p], recv_sem=recv_d2d.at[a, p],
                device_id=sibling, device_id_type=MESH)
            fw.start()
            sends.append(fw)
    for a in range(len(src)):
        for p, (px, py) in enumerate(chips):
            blk = out[a].at[2 * px + py, 1 - c]
            pltpu.make_async_remote_copy(
                src_ref=blk, dst_ref=blk, send_sem=send_d2d.at[a, p], recv_sem=recv_d2d.at[a, p],
                device_id=sibling, device_id_type=MESH).wait_recv()
    for cp in sends:
        cp.wait_send()


def _handshake(peers):
    barrier = pltpu.get_barrier_semaphore()
    for peer in peers:
        pl.semaphore_signal(barrier, inc=1, device_id=peer, device_id_type=MESH)
    pl.semaphore_wait(barrier, len(peers))


_SEQUENCER = dict(axis_name="sequencer", num_cores=1)
GATHER_LATE_ID, SCATTER_EARLY_ID, SWAP_EARLY_ID, GATHER_FIRST_ID, SCATTER_LATE_ID = 1, 2, 3, 4, 5


def _all_gather_async(shards, after, name, collective_id):
    n, k = len(shards), len(after)

    def body(*refs):
        x, y, c, chips = _place()
        _handshake([(*chip, c) for chip in chips] + [(x, y, 1 - c)])
        _gather_body(refs[:n], refs[n + k:2 * n + k], *refs[2 * n + k:])

    return pl.kernel(
        body, out_type=[SDS((N_SHARD,) + t.shape, t.dtype) for t in shards],
        mesh=plsc.ScalarSubcoreMesh(**_SEQUENCER), scratch_types=[pltpu.SemaphoreType.DMA((n, 3))] * 4,
        compiler_params=pltpu.CompilerParams(collective_id=collective_id), name=name)(*shards, *after)


def _pair_swap(grads):
    n = len(grads)

    def body(*refs):
        src, out, send_sems, recv_sems = refs[:n], refs[n:2 * n], refs[2 * n], refs[2 * n + 1]
        x, y, c, _ = _place()
        copies = [pltpu.make_async_remote_copy(
            src_ref=src[a].at[:, 1 - c], dst_ref=out[a], send_sem=send_sems.at[a], recv_sem=recv_sems.at[a],
            device_id=(x, y, 1 - c), device_id_type=MESH) for a in range(n)]
        for cp in copies:
            cp.start()
        for cp in copies:
            cp.wait()

    return pl.pallas_call(
        body, in_specs=[_ANY] * n, out_specs=[_ANY] * n,
        out_shape=[SDS((N_SHARD,) + t.shape[2:], t.dtype) for t in grads],
        scratch_shapes=[pltpu.SemaphoreType.DMA((n,)), pltpu.SemaphoreType.DMA((n,))], name="pair_swap",
        compiler_params=pltpu.CompilerParams(has_side_effects=True))(*grads)


def _pair_swap_early(grads):
    n = len(grads)

    def body(*refs):
        src, out, send_sems, recv_sems = refs[:n], refs[n:2 * n], refs[2 * n], refs[2 * n + 1]
        x, y, c, _ = _place()
        _handshake([(x, y, 1 - c)])
        copies = [pltpu.make_async_remote_copy(
            src_ref=src[a].at[:, 1 - c], dst_ref=out[a], send_sem=send_sems.at[a], recv_sem=recv_sems.at[a],
            device_id=(x, y, 1 - c), device_id_type=MESH) for a in range(n)]
        for cp in copies:
            cp.start()
        for cp in copies:
            cp.wait()

    return pl.kernel(
        body, out_type=[SDS((N_SHARD,) + t.shape[2:], t.dtype) for t in grads],
        mesh=plsc.ScalarSubcoreMesh(**_SEQUENCER), scratch_types=[pltpu.SemaphoreType.DMA((n,))] * 2,
        compiler_params=pltpu.CompilerParams(collective_id=SWAP_EARLY_ID), name="pair_swap_early")(*grads)


def _scatter_early(parts):
    n = len(parts)

    def body(*refs):
        part, recv, send_sems, recv_sems = refs[:n], refs[n:2 * n], refs[2 * n], refs[2 * n + 1]
        x, y, c, chips = _place()
        _handshake([(*chip, c) for chip in chips])
        me_j = 2 * x + y
        sends = []
        for a in range(n):
            for p, (px, py) in enumerate(chips):
                cp = pltpu.make_async_remote_copy(
                    src_ref=part[a].at[2 * px + py], dst_ref=recv[a].at[me_j], send_sem=send_sems.at[a, p],
                    recv_sem=recv_sems.at[a, p], device_id=(px, py, c), device_id_type=MESH)
                cp.start()
                sends.append(cp)
        for a in range(n):
            for p, (px, py) in enumerate(chips):
                slot = recv[a].at[2 * px + py]
                pltpu.make_async_remote_copy(
                    src_ref=slot, dst_ref=slot, send_sem=send_sems.at[a, p], recv_sem=recv_sems.at[a, p],
                    device_id=(px, py, c), device_id_type=MESH).wait_recv()
        for cp in sends:
            cp.wait_send()

    return pl.kernel(
        body, out_type=[SDS(t.shape, t.dtype) for t in parts],
        mesh=plsc.ScalarSubcoreMesh(**_SEQUENCER), scratch_types=[pltpu.SemaphoreType.DMA((n, 3))] * 2,
        compiler_params=pltpu.CompilerParams(collective_id=SCATTER_EARLY_ID), name="scatter_early")(*parts)


def _pair_sum(grads, other, name):
    _, _, rows, cols = grads.shape
    tr = _row_tile(rows)
    core = lax.axis_index("c").astype(jnp.int32).reshape(1)

    def body(c_ref, g_ref, o_ref, out_ref):
        out_ref[...] = (g_ref[...].astype(f32) + o_ref[...].astype(f32)).astype(bf16)

    grid_spec = pltpu.PrefetchScalarGridSpec(
        num_scalar_prefetch=1, grid=(N_SHARD, rows // tr),
        in_specs=[pl.BlockSpec((None, None, tr, cols), lambda j, i, c_ref: (j, c_ref[0], i, 0)),
                  pl.BlockSpec((None, tr, cols), lambda j, i, c_ref: (j, i, 0))],
        out_specs=pl.BlockSpec((None, tr, cols), lambda j, i, c_ref: (j, i, 0)))
    return pl.pallas_call(
        body, grid_spec=grid_spec, out_shape=SDS((N_SHARD, rows, cols), bf16), name=name,
        compiler_params=_params(("parallel", "parallel"), 10 * tr * cols, 12 * tr * cols))(core, grads, other)


def _scatter_partials(parts, small):
    n = len(parts)

    def body(*refs):
        part, small_ref, recv, small_all_ref = refs[:n], refs[n], refs[n + 1:2 * n + 1], refs[2 * n + 1]
        send_sems, recv_sems, ssend, srecv, local_sem = refs[2 * n + 2:]
        x, y, c, chips = _place()
        flip = lambda a, bit: 1 - a if bit else a
        peers = [(flip(x, k & 4), flip(y, k & 2), flip(c, k & 1)) for k in range(1, 8)]
        _handshake(peers)
        me_j = 2 * x + y
        me_dev = 4 * x + 2 * y + c
        own = pltpu.make_async_copy(small_ref, small_all_ref.at[me_dev], local_sem)
        own.start()
        sends = []
        for a in range(n):
            for p, (px, py) in enumerate(chips):
                cp = pltpu.make_async_remote_copy(
                    src_ref=part[a].at[2 * px + py], dst_ref=recv[a].at[me_j], send_sem=send_sems.at[a, p],
                    recv_sem=recv_sems.at[a, p], device_id=(px, py, c), device_id_type=MESH)
                cp.start()
                sends.append(cp)
        for k, to in enumerate(peers):
            cp = pltpu.make_async_remote_copy(
                src_ref=small_ref, dst_ref=small_all_ref.at[me_dev],
                send_sem=ssend.at[k], recv_sem=srecv.at[k], device_id=to, device_id_type=MESH)
            cp.start()
            sends.append(cp)
        for a in range(n):
            for p, (px, py) in enumerate(chips):
                slot = recv[a].at[2 * px + py]
                pltpu.make_async_remote_copy(
                    src_ref=slot, dst_ref=slot, send_sem=send_sems.at[a, p], recv_sem=recv_sems.at[a, p],
                    device_id=(px, py, c), device_id_type=MESH).wait_recv()
        for k, (px, py, pc) in enumerate(peers):
            slot = small_all_ref.at[4 * px + 2 * py + pc]
            pltpu.make_async_remote_copy(
                src_ref=slot, dst_ref=slot, send_sem=ssend.at[k], recv_sem=srecv.at[k],
                device_id=(px, py, pc), device_id_type=MESH).wait_recv()
        for cp in sends:
            cp.wait_send()
        own.wait()

    return pl.kernel(
        body, out_type=[SDS(t.shape, t.dtype) for t in parts] + [SDS((8, SMALL_ROWS, D), f32)],
        mesh=plsc.ScalarSubcoreMesh(**_SEQUENCER),
        scratch_types=[pltpu.SemaphoreType.DMA((n, 3)), pltpu.SemaphoreType.DMA((n, 3)),
                       pltpu.SemaphoreType.DMA((7,)), pltpu.SemaphoreType.DMA((7,)), pltpu.SemaphoreType.DMA],
        compiler_params=pltpu.CompilerParams(collective_id=SCATTER_LATE_ID), name="scatter_partials")(*parts, small)


def _sum_partials(part, recv, name):
    _, rows, cols = recv.shape
    tr = _row_tile(rows)
    me = (2 * lax.axis_index("x") + lax.axis_index("y")).astype(jnp.int32).reshape(1)

    def body(me_ref, mine, r0, r1, r2, r3, out_ref):
        acc = None
        for j, r in enumerate((r0, r1, r2, r3)):
            term = jnp.where(me_ref[0] == j, mine[...], r[...]).astype(f32)
            acc = term if acc is None else acc + term
        out_ref[...] = acc

    slot = lambda j: pl.BlockSpec((None, tr, cols), lambda i, me_ref: (jnp.where(me_ref[0] == j, j ^ 1, j), i, 0))
    grid_spec = pltpu.PrefetchScalarGridSpec(
        num_scalar_prefetch=1, grid=(rows // tr,),
        in_specs=[pl.BlockSpec((None, tr, cols), lambda i, me_ref: (me_ref[0], i, 0)), slot(0), slot(1), slot(2), slot(3)],
        out_specs=pl.BlockSpec((tr, cols), lambda i, me_ref: (i, 0)))
    return pl.pallas_call(
        body, grid_spec=grid_spec, out_shape=SDS((rows, cols), f32), name=name,
        compiler_params=_params(("parallel",), 14 * tr * cols, 12 * tr * cols))(me, part, recv, recv, recv, recv)


def _sum_small(small_all):
    def body(small_ref, out_ref):
        tot = small_ref[0]
        for k in range(1, 8):
            tot = tot + small_ref[k]
        out_ref[...] = tot

    return pl.pallas_call(
        body, grid=(1,), in_specs=[pl.BlockSpec((8, SMALL_ROWS, D), lambda i: (0, 0, 0))],
        out_specs=pl.BlockSpec((SMALL_ROWS, D), lambda i: (0, 0)), out_shape=SDS((SMALL_ROWS, D), f32),
        name="sum_small", compiler_params=_params(("arbitrary",), 36 * SMALL_ROWS * D))(small_all)


def _swap_halves(halves, name):
    n = len(halves)

    def body(*refs):
        src, out, send_sems, recv_sems = refs[:n], refs[n:2 * n], refs[2 * n], refs[2 * n + 1]
        x, y, c, _ = _place()
        copies = [pltpu.make_async_remote_copy(
            src_ref=src[a], dst_ref=out[a], send_sem=send_sems.at[a], recv_sem=recv_sems.at[a],
            device_id=(x, y, 1 - c), device_id_type=MESH) for a in range(n)]
        for cp in copies:
            cp.start()
        for cp in copies:
            cp.wait()

    return pl.pallas_call(
        body, in_specs=[_ANY] * n, out_specs=[_ANY] * n, out_shape=[SDS(t.shape, f32) for t in halves],
        scratch_shapes=[pltpu.SemaphoreType.DMA((n,))] * 2, name=name,
        compiler_params=pltpu.CompilerParams(has_side_effects=True))(*halves)


def _kernel_layout(name, t):
    t = t[0]
    if name in TRANSPOSED:
        t = jnp.swapaxes(t, 0, 1)
    return _pad_rows(t, SHARD_SHAPE[name][0])


def _harness_layout(name, t):
    if name == "w_in":
        t = t[:IN_SHARD]
    if name in TRANSPOSED:
        t = jnp.swapaxes(t, 0, 1)
    return t[None]


def _pad_rows(t, rows):
    return t if t.shape[0] == rows else jnp.pad(t, ((0, rows - t.shape[0]), (0, 0)))


_QA, _KA, _VA, _QB, _F, _GAB = 0, 768, 1536, 2304, 3840, 3848


def _spans(a, b):
    return [(j, max(a, j * IN_SHARD) - j * IN_SHARD, max(a, j * IN_SHARD) - a,
             min(b, (j + 1) * IN_SHARD) - max(a, j * IN_SHARD))
            for j in range(N_SHARD) if max(a, j * IN_SHARD) < min(b, (j + 1) * IN_SHARD)]


_LANES = pl.BlockSpec((N_SHARD, IN_SHARD_PAD, 128), lambda c: (0, 0, c))


def _split_w_in(shards):
    group = [[(o + g * DIL_W, o + (g + 1) * DIL_W) for o in (_QA, _KA, _VA)] for g in range(3)]
    fox = [[(_QB + k * FOX_W, _QB + (k + 1) * FOX_W)] for k in range(3)]
    wanted = group + fox + [[(_QB, _F)], [(_F, _GAB)], [(_GAB, IN_COLS)]]
    rows = [sum(b - a for a, b in w) for w in wanted]
    rows[7] = 128

    def body(s_ref, *o_refs):
        for o_ref, want in zip(o_refs, wanted):
            at = 0
            for a, b in want:
                for j, src, off, n in _spans(a, b):
                    o_ref[at + off:at + off + n, :] = s_ref[j, src:src + n, :]
                at += b - a
        o_refs[7][N_FOX:, :] = jnp.zeros((128 - N_FOX, 128), bf16)

    return pl.pallas_call(
        body, grid=(D // 128,), in_specs=[_LANES], out_specs=[pl.BlockSpec((r, 128), lambda c: (0, c)) for r in rows],
        out_shape=[SDS((r, D), bf16) for r in rows], name="split_w_in",
        compiler_params=_params(("parallel",), 2 * 128 * (N_SHARD * IN_SHARD_PAD + sum(rows))))(shards)


def _join_w_in(g_a, g_fox, g_f, g_gab):
    parts = [(g_a[k], o, o + DIL_W) for o in (0, DIL_W, 2 * DIL_W) for k in range(3)]
    parts += [(t, 0, FOX_W) for t in g_fox] + [(g_f, 0, N_FOX), (g_gab, 0, 2 * D)]
    arrays = list(g_a) + list(g_fox) + [g_f, g_gab]
    index = {id(t): i for i, t in enumerate(arrays)}

    def body(*refs):
        o_ref = refs[-1]
        o_ref[:, IN_SHARD:, :] = jnp.zeros((N_SHARD, IN_SHARD_PAD - IN_SHARD, 128), bf16)
        at = 0
        for t, lo, hi in parts:
            src_ref = refs[index[id(t)]]
            for j, dst, off, n in _spans(at, at + hi - lo):
                o_ref[j, dst:dst + n, :] = src_ref[lo + off:lo + off + n, :].astype(bf16)
            at += hi - lo

    return pl.pallas_call(
        body, grid=(D // 128,), in_specs=[pl.BlockSpec((t.shape[0], 128), lambda c: (0, c)) for t in arrays],
        out_specs=_LANES, out_shape=SDS((N_SHARD, IN_SHARD_PAD, D), bf16), name="join_w_in",
        compiler_params=_params(("parallel",), 2 * 128 * (N_SHARD * IN_SHARD_PAD + sum(t.shape[0] for t in arrays))),
    )(*arrays)


def _full_weights(gathered):
    full = {n: t.reshape((N_SHARD,) + SHARD_SHAPE[n]) for n, t in gathered.items()}
    out = {}
    if "w_in" in full:
        pieces = _split_w_in(full["w_in"])
        out.update(w_a_t=pieces[0:3], w_fox_t=pieces[3:6], w_vr_t=pieces[6], w_f_t=pieces[7], w_gab_t=pieces[8])
    if "w_out" in full:
        out.update(
            w_a4=full["w_proj_a"],
            w_b4=full["w_proj_b"],
            w_out=full["w_out"].reshape(D, D),
            w_gate_t=full["w_ffn_gate"].reshape(F_FF, D),
            w_up_t=full["w_ffn_up"].reshape(F_FF, D),
            w_down=full["w_ffn_down"].reshape(F_FF, D))
    return out


def _sharded_grads(g):
    full = dict(w_in=_join_w_in(g["w_a_t"], g["w_fox_t"], g["w_f_t"], g["w_gab_t"]), w_proj_a=g["w_a4"],
                w_proj_b=g["w_b4"], w_out=g["w_out"], w_ffn_gate=g["w_gate_t"], w_ffn_up=g["w_up_t"],
                w_ffn_down=g["w_down"])
    return {n: _halved(full[n].reshape((N_SHARD,) + SHARD_SHAPE[n])) for n in W_NAMES}


def _local_step(x, target, wt, b_forget, g_mix_pre, g_mix_post, g_ffn_pre, g_ffn_post, late=None):
    tables = _rope_tables()
    b128 = jnp.pad(b_forget, ((0, 0), (0, 128 - N_FOX)))
    dils = tuple(d for _, d in DIL_GROUPS[1:])

    hs = _norm_fwd([x] + list(_perm_rows([x], dils, "perm_x")), g_mix_pre)
    h1 = hs[0]
    if callable(wt):
        wt = wt(h1)
    qkv = [_rope_fwd(g, _mm([(hs[g], wt["w_a_t"][g])], "nt", f32, tm=1024, tn=QKV_W, name=f"proj_a_{g}"), tables)
           for g in range(3)]
    vr = _mm([(h1, wt["w_vr_t"])], "nt", bf16, tm=1024, tn=VR_W // 2, name="proj_vr")
    gab = _mm([(h1, wt["w_gab_t"])], "nt", f32, tm=512, tn=2 * D, name="proj_gab")
    fz = _mm([(h1, wt["w_f_t"])], "nt", f32, tm=1024, tn=128, name="proj_f")
    dil = [_dil_fwd(g, qkv[g]) for g in range(3)]
    out_a, lse_a = _dil_combine([o for o, _ in dil], [l for _, l in dil])
    f_q, f_k = _forget_fwd(fz, b128)
    out_b, lse_b = _fox_fwd(vr, f_q, f_k)
    if late is not None:
        wt = {**wt, **late(out_b)}
    ya, yb, merged = _merge_fwd(out_a, out_b, wt["w_a4"], wt["w_b4"], gab)
    mix, x2, h3 = _resid_norm_fwd(x, merged, wt["w_out"], g_mix_post, g_ffn_pre)
    g_act, u_act, a_act = _ffn_fwd(h3, wt["w_gate_t"], wt["w_up_t"])
    sq_err, dy, d_ff, dg_ffn_post = _loss_head(x2, a_act, wt["w_down"], g_ffn_post, target)

    grads = {}
    d_g, d_u = _ffn_bwd_act(d_ff, wt["w_down"], g_act, u_act)
    grads["w_down"] = _mm([(a_act, d_ff)], "tn", bf16, tm=FF_TN, tn=D, name="grad_w_down")
    grads["w_gate_t"] = _mm([(d_g, h3)], "tn", bf16, tm=FF_TN, tn=D, name="grad_w_gate")
    grads["w_up_t"] = _mm([(d_u, h3)], "tn", bf16, tm=FF_TN, tn=D, name="grad_w_up")
    dx2, d_mix, dg_ffn_pre, dg_mix_post = _norm_bwd_mid(dy, d_g, d_u, wt["w_gate_t"], wt["w_up_t"], x2, mix,
                                                        g_ffn_pre, g_mix_post)

    grads["w_out"] = _mm([(merged, d_mix)], "tn", bf16, tm=D, tn=D, name="grad_w_out")
    d_ya, d_yb, d_gab = _merge_bwd(d_mix, wt["w_out"], ya, yb, gab)
    grads["w_a4"], grads["w_b4"] = _branch_grads(out_a, out_b, d_ya, d_yb)
    d_out_a, delta_a, d_out_b, delta_b = _branch_bwd(d_ya, d_yb, wt["w_a4"], wt["w_b4"], out_a, out_b)

    perm = _perm_rows([d_out_a, delta_a, lse_a], dils, "perm_dil_bwd")
    aux = [(d_out_a, delta_a, lse_a)] + [tuple(perm[k * len(dils) + i] for k in range(3)) for i in range(len(dils))]
    d_qkv = []
    for g in range(3):
        dq, dk, dv = _dil_bwd(g, qkv[g], *aux[g])
        d_qkv.append(_rope_bwd(g, dq, dk, dv, tables))
    *d_fox, d_f_cols, d_f_rows = _fox_bwd(vr, f_q, f_k, lse_b, d_out_b, delta_b)
    d_z, d_b128 = _forget_bwd(fz, b128, d_f_cols, d_f_rows)

    grads["w_a_t"] = [_mm([(d_qkv[g], hs[g])], "tn", bf16, tm=QKV_W, tn=D, name=f"grad_w_a_{g}") for g in range(3)]
    grads["w_fox_t"] = [_mm([(d_fox[k], h1)], "tn", bf16, tm=FOX_W, tn=D, name=f"grad_w_fox_{k}") for k in range(3)]
    grads["w_gab_t"] = _mm([(d_gab, h1)], "tn", bf16, tm=D, tn=D, name="grad_w_gab")
    grads["w_f_t"] = _mm([(d_z, h1)], "tn", bf16, tm=128, tn=D, name="grad_w_f")
    d_h1_nat = _mm([(d_qkv[0], wt["w_a_t"][0])] + list(zip(d_fox, wt["w_fox_t"]))
                   + [(d_gab, wt["w_gab_t"]), (d_z, wt["w_f_t"])], "nn", f32, tm=512, tn=D, name="proj_in_bwd")
    d_h1_dil = [_mm([(d_qkv[g], wt["w_a_t"][g])], "nn", f32, tm=1024, tn=D, name=f"proj_a_bwd_{g}") for g in (1, 2)]
    d_h1 = _unperm_sum(d_h1_nat, d_h1_dil, dils, "unperm_d_h1")
    grad_x, dg_mix_pre = _norm_bwd_in(dx2, d_h1, x, g_mix_pre)

    small = dict(b_forget=d_b128[:, :N_FOX], norm_mix_pre=dg_mix_pre, norm_mix_post=dg_mix_post,
                 norm_ffn_pre=dg_ffn_pre, norm_ffn_post=dg_ffn_post)
    grads["mid_backward"] = d_qkv[0]
    return sq_err, grad_x, grads, small


NORMS = ("norm_mix_pre", "norm_mix_post", "norm_ffn_pre", "norm_ffn_post")
ORDER = ("w_in", "w_proj_a", "w_proj_b", "w_out", "b_forget", "w_ffn_gate", "w_ffn_up", "w_ffn_down") + NORMS


def kernel(x, w_in, w_proj_a, w_proj_b, w_out, b_forget, w_ffn_gate, w_ffn_up, w_ffn_down, norm_mix_pre, norm_mix_post, norm_ffn_pre, norm_ffn_post, loss_target, m_w_in, m_w_proj_a, m_w_proj_b, m_w_out, m_b_forget, m_w_ffn_gate, m_w_ffn_up, m_w_ffn_down, m_norm_mix_pre, m_norm_mix_post, m_norm_ffn_pre, m_norm_ffn_post, v_w_in, v_w_proj_a, v_w_proj_b, v_w_out, v_b_forget, v_w_ffn_gate, v_w_ffn_up, v_w_ffn_down, v_norm_mix_pre, v_norm_mix_post, v_norm_ffn_pre, v_norm_ffn_post):
    given = dict(w_in=w_in, w_proj_a=w_proj_a, w_proj_b=w_proj_b, w_out=w_out, w_ffn_gate=w_ffn_gate,
                 w_ffn_up=w_ffn_up, w_ffn_down=w_ffn_down)
    given_m = dict(w_in=m_w_in, w_proj_a=m_w_proj_a, w_proj_b=m_w_proj_b, w_out=m_w_out, w_ffn_gate=m_w_ffn_gate,
                   w_ffn_up=m_w_ffn_up, w_ffn_down=m_w_ffn_down)
    given_v = dict(w_in=v_w_in, w_proj_a=v_w_proj_a, w_proj_b=v_w_proj_b, w_out=v_w_out, w_ffn_gate=v_w_ffn_gate,
                   w_ffn_up=v_w_ffn_up, w_ffn_down=v_w_ffn_down)
    w, m, v = ({n: _kernel_layout(n, t[n]) for n in W_NAMES} for t in (given, given_m, given_v))
    small_w = dict(b_forget=b_forget, norm_mix_pre=norm_mix_pre, norm_mix_post=norm_mix_post,
                   norm_ffn_pre=norm_ffn_pre, norm_ffn_post=norm_ffn_post)
    small_m = dict(b_forget=m_b_forget, norm_mix_pre=m_norm_mix_pre, norm_mix_post=m_norm_mix_post,
                   norm_ffn_pre=m_norm_ffn_pre, norm_ffn_post=m_norm_ffn_post)
    small_v = dict(b_forget=v_b_forget, norm_mix_pre=v_norm_mix_pre, norm_mix_post=v_norm_mix_post,
                   norm_ffn_pre=v_norm_ffn_pre, norm_ffn_post=v_norm_ffn_post)

    own = [_halved(w[n].astype(bf16)) for n in W_NAMES]
    chip = 2 * lax.axis_index("x") + lax.axis_index("y")
    exchanged = {"first": _all_gather_async(own[:1], [], "all_gather_first", GATHER_FIRST_ID)}
    fill = lambda ts, mine: [lax.dynamic_update_index_in_dim(t, o, chip, 0) for t, o in zip(ts, mine)]

    def first_weights(ready):
        arrived, _ = lax.optimization_barrier((list(exchanged["first"]), ready))
        exchanged["late"] = _all_gather_async(own[1:], [arrived[0][0, 0, :16, :128]], "all_gather_late", GATHER_LATE_ID)
        return _full_weights(dict(zip(W_NAMES[:1], fill(arrived, own[:1]))))

    def late_weights(ready):
        arrived, _ = lax.optimization_barrier((list(exchanged["late"]), ready))
        return _full_weights(dict(zip(W_NAMES[1:], fill(arrived, own[1:]))))

    sq_err, grad_x, grads, small = _local_step(x[0], loss_target[0], first_weights, b_forget, norm_mix_pre,
                                               norm_mix_post, norm_ffn_pre, norm_ffn_post, late=late_weights)

    g4 = _sharded_grads(grads)
    stack = lambda t, extra: jnp.concatenate(
        [jnp.pad(t["b_forget"], ((0, 0), (0, D - N_FOX)))] + [t[n] for n in NORMS]
        + [jnp.pad(extra, ((0, SMALL_ROWS - LOSS_ROW - 1), (0, D - extra.shape[1])), constant_values=1.0)], axis=0)
    early, _ = lax.optimization_barrier((list(_pair_swap_early([g4[n] for n in W_NAMES[1:]])), grads["mid_backward"]))
    other = list(_pair_swap([g4["w_in"]])) + early
    parts = [_pair_sum(g4[n], o, "pair_sum_" + n) for n, o in zip(W_NAMES, other)]
    recv_early = _scatter_early(parts[1:])
    recv_in, small_all = _scatter_partials(parts[:1], stack(small, sq_err))

    g_shard, delta, new_m, new_v = {}, {}, {}, {}

    def finish(names, parts, recv):
        halves = [_sum_partials(p, r, "sum_partials_" + n) for n, p, r in zip(names, parts, recv)]
        theirs = _swap_halves(halves, "swap_halves_" + names[0])
        for n, mine, other_half in zip(names, halves, theirs):
            g_shard[n], delta[n], new_m[n], new_v[n] = _adamw_halves(w[n], mine, other_half, m[n], v[n], "adamw_" + n)

    recv_early, _ = lax.optimization_barrier((list(recv_early), parts[0]))
    finish(W_NAMES[1:], parts[1:], recv_early)
    (recv_in, small_all), _ = lax.optimization_barrier(((recv_in, small_all), [delta[n] for n in W_NAMES[1:]]))
    finish(W_NAMES[:1], parts[:1], [recv_in])
    small_sum = _sum_small(small_all)
    loss = small_sum[LOSS_ROW, 0] * (0.5 / D)
    ones = jnp.ones((1, 128), f32)
    sd, sm, sv = _adamw(stack(small_w, ones), small_sum, stack(small_m, ones), stack(small_v, ones), "adamw_small")

    outs = [loss, grad_x[None]]
    for big, st in ((g_shard, small_sum), (delta, sd), (new_m, sm), (new_v, sv)):
        t = {n: _harness_layout(n, big[n]) for n in W_NAMES}
        t["b_forget"] = st[0:1, :N_FOX]
        for i, n in enumerate(NORMS):
            t[n] = st[i + 1:i + 2]
        outs += [t[n] for n in ORDER]
    return tuple(outs)
```

```python
import functools
import math

import jax
import jax.numpy as jnp
import numpy as np
from jax import lax
from jax.experimental import pallas as pl
from jax.experimental.pallas import tpu as pltpu
from jax.experimental.pallas import tpu_sc as plsc

f32 = jnp.float32
bf16 = jnp.bfloat16
SDS = jax.ShapeDtypeStruct
MESH = pl.DeviceIdType.MESH

S = 2048
D = 1024
HD = 64
BLK = 128
N_FOX = 8
FOX_W = N_FOX * HD
DIL_GROUPS = ((128, 1), (512, 4), (2048, 16))
SLOTS = 4
DIL_W = SLOTS * HD
QKV_W = 3 * DIL_W
VR_W = 3 * FOX_W
GF_W = 2 * D + 128
F_FF = 2816
ROPE_DIM = 16
ROPE_THETA = 500000.0
EPS = 1e-6
NEG = -1e30
SCALE = 1.0 / math.sqrt(HD)
IN_COLS = 5896
N_SHARD = 4

ADAM_LR, ADAM_B1, ADAM_B2, ADAM_EPS, ADAM_WD, ADAM_STEP = 0.001, 0.9, 0.999, 1e-08, 0.01, 10

VMEM_V7X = 64 * 1024 * 1024
VMEM_PLAN_MAX = VMEM_V7X - 8 * 1024 * 1024

TM = 512
TQ = 256

W_NAMES = ("w_in", "w_proj_a", "w_proj_b", "w_out", "w_ffn_gate", "w_ffn_up", "w_ffn_down")
TRANSPOSED = ("w_in", "w_ffn_gate", "w_ffn_up")
IN_SHARD = IN_COLS // N_SHARD
IN_SHARD_PAD = 1504
SHARD_SHAPE = dict(w_in=(IN_SHARD_PAD, D), w_proj_a=(DIL_W, D // N_SHARD), w_proj_b=(FOX_W, D // N_SHARD),
                   w_out=(D // N_SHARD, D), w_ffn_gate=(F_FF // N_SHARD, D), w_ffn_up=(F_FF // N_SHARD, D),
                   w_ffn_down=(F_FF // N_SHARD, D))
SMALL_ROWS = 8
LOSS_ROW = 5


def _nbytes(shape, dtype):
    return math.prod(shape) * jnp.dtype(dtype).itemsize


def _params(semantics, block_bytes, temp_bytes=0):
    need = 2 * block_bytes + temp_bytes + (2 << 20)
    return pltpu.CompilerParams(dimension_semantics=semantics, vmem_limit_bytes=int(min(need, VMEM_PLAN_MAX)))


def _row(w, tm=TM):
    return pl.BlockSpec((tm, w), lambda i: (i, 0))


def _vec(w):
    return pl.BlockSpec((1, w), lambda i: (0, 0))


def _mm(pairs, dims, out_dtype, *, tm, tn, name, m_inner=False):
    a0, b0 = pairs[0]
    m_dim = a0.shape[1] if dims == "tn" else a0.shape[0]
    n_dim = b0.shape[0] if dims == "nt" else b0.shape[1]
    contract = {"nn": ((1,), (0,)), "nt": ((1,), (1,)), "tn": ((0,), (0,))}[dims]
    n_pairs = len(pairs)
    assert m_dim % tm == 0 and n_dim % tn == 0, (name, m_dim, n_dim, tm, tn)

    def body(*refs):
        o_ref = refs[-1]
        acc = None
        for p in range(n_pairs):
            a = refs[2 * p][...].astype(bf16)
            b = refs[2 * p + 1][...].astype(bf16)
            t = lax.dot_general(a, b, (contract, ((), ())), preferred_element_type=f32)
            acc = t if acc is None else acc + t
        o_ref[...] = acc.astype(o_ref.dtype)

    if m_inner:
        grid = (n_dim // tn, m_dim // tm)
        mi = lambda j, i: i
        ni = lambda j, i: j
    else:
        grid = (m_dim // tm, n_dim // tn)
        mi = lambda i, j: i
        ni = lambda i, j: j
    in_specs, block_bytes, args = [], 0, []
    for a, b in pairs:
        k_dim = a.shape[0] if dims == "tn" else a.shape[1]
        if dims == "tn":
            in_specs.append(pl.BlockSpec((k_dim, tm), lambda *g: (0, mi(*g))))
        else:
            in_specs.append(pl.BlockSpec((tm, k_dim), lambda *g: (mi(*g), 0)))
        if dims == "nt":
            in_specs.append(pl.BlockSpec((tn, k_dim), lambda *g: (ni(*g), 0)))
        else:
            in_specs.append(pl.BlockSpec((k_dim, tn), lambda *g: (0, ni(*g))))
        block_bytes += _nbytes((tm, k_dim), a.dtype) + _nbytes((tn, k_dim), b.dtype)
        args += [a, b]
    block_bytes += _nbytes((tm, tn), out_dtype)
    temp = _nbytes((tm, tn), f32) * 2 + sum(_nbytes((tm, a.shape[0] if dims == "tn" else a.shape[1]), bf16)
                                            + _nbytes((tn, a.shape[0] if dims == "tn" else a.shape[1]), bf16)
                                            for a, _ in pairs)
    return pl.pallas_call(
        body, grid=grid, in_specs=in_specs,
        out_specs=pl.BlockSpec((tm, tn), lambda *g: (mi(*g), ni(*g))),
        out_shape=SDS((m_dim, n_dim), out_dtype), name=name,
        compiler_params=_params(("parallel", "parallel"), block_bytes, temp),
    )(*args)


def _rms(x, g):
    r = lax.rsqrt(jnp.mean(x * x, axis=-1, keepdims=True) + EPS)
    return x * r * g


def _rms_bwd(x, g, dy):
    r = lax.rsqrt(jnp.mean(x * x, axis=-1, keepdims=True) + EPS)
    xh = x * r
    dxh = dy * g
    dx = r * (dxh - xh * jnp.mean(dxh * xh, axis=-1, keepdims=True))
    return dx, jnp.sum(dy * xh, axis=0, keepdims=True)


def _acc_rows(ref, val):
    @pl.when(pl.program_id(0) == 0)
    def _():
        ref[...] = jnp.zeros_like(ref)
    ref[...] += val


def _norm_fwd(xs, g):
    n = len(xs)

    def body(*refs):
        g = refs[n][...]
        for x_ref, h_ref in zip(refs[:n], refs[n + 1:]):
            h_ref[...] = _rms(x_ref[...], g).astype(bf16)

    return pl.pallas_call(
        body, grid=(S // TM,), in_specs=[_row(D)] * n + [_vec(D)], out_specs=[_row(D)] * n,
        out_shape=[SDS((S, D), bf16)] * n, name="norm_mix_pre",
        compiler_params=_params(("parallel",), 6 * n * TM * D, 8 * n * TM * D))(*xs, g)


def _perm_rows(xs, ds, name):
    n = len(xs)

    def body(*refs):
        outs = iter(refs[n:])
        for x_ref in refs[:n]:
            for d in ds:
                o_ref, rows = next(outs), S // d
                for r in range(d):
                    o_ref[r * rows:(r + 1) * rows, :] = x_ref[pl.ds(r, rows, stride=d), :]

    blk = pl.BlockSpec((S, 128), lambda c: (0, c))
    w = xs[0].shape[1]
    return pl.pallas_call(
        body, grid=(w // 128,), in_specs=[blk] * n, out_specs=[blk] * (n * len(ds)),
        out_shape=[SDS((S, w), f32)] * (n * len(ds)), name=name,
        compiler_params=_params(("parallel",), 4 * S * 128 * n * (1 + len(ds))))(*xs)


def _unperm_sum(nat, perms, ds, name):
    n = len(perms)

    def body(*refs):
        a_ref, o_ref, sc = refs[0], refs[n + 1], refs[n + 2]
        acc = a_ref[...]
        for b_ref, d in zip(refs[1:n + 1], ds):
            rows = S // d
            for r in range(d):
                sc[pl.ds(r, rows, stride=d), :] = b_ref[r * rows:(r + 1) * rows, :]
            acc = acc + sc[...]
        o_ref[...] = acc

    blk = pl.BlockSpec((S, 128), lambda c: (0, c))
    w = nat.shape[1]
    return pl.pallas_call(
        body, grid=(w // 128,), in_specs=[blk] * (n + 1), out_specs=blk, out_shape=SDS((S, w), f32),
        scratch_shapes=[pltpu.VMEM((S, 128), f32)], name=name,
        compiler_params=_params(("parallel",), 4 * S * 128 * (n + 2), 8 * S * 128))(nat, *perms)


def _whole(a):
    return pl.BlockSpec(a.shape, lambda i: (0,) * a.ndim)


def _resid_norm_fwd(x, merged, w_out, g_post, g_pre):
    def body(x_ref, mg_ref, w_ref, gp_ref, gn_ref, mix_ref, x2_ref, h_ref):
        mix = jnp.dot(mg_ref[...], w_ref[...], preferred_element_type=f32)
        x2 = x_ref[...] + _rms(mix, gp_ref[...])
        mix_ref[...] = mix
        x2_ref[...] = x2
        h_ref[...] = _rms(x2, gn_ref[...]).astype(bf16)

    return pl.pallas_call(
        body, grid=(S // TM,), in_specs=[_row(D), _row(D), _whole(w_out), _vec(D), _vec(D)], out_specs=[_row(D)] * 3,
        out_shape=[SDS((S, D), f32), SDS((S, D), f32), SDS((S, D), bf16)], name="proj_out_norm",
        compiler_params=_params(("parallel",), 16 * TM * D + 2 * D * D, 16 * TM * D))(x, merged, w_out, g_post, g_pre)


def _loss_head(x2, a_act, w_down, g_post, target):
    def body(x2_ref, a_ref, w_ref, g_ref, t_ref, loss_ref, dy_ref, dff_ref, dg_ref):
        ff = jnp.dot(a_ref[...], w_ref[...], preferred_element_type=f32)
        g = g_ref[...]
        err = x2_ref[...] + _rms(ff, g) - t_ref[...]
        dy = err * (1.0 / D)
        dff, dg = _rms_bwd(ff, g, dy)
        dy_ref[...] = dy
        dff_ref[...] = dff.astype(bf16)
        _acc_rows(dg_ref, dg)
        _acc_rows(loss_ref, jnp.full((1, 128), jnp.sum(err * err), f32))

    return pl.pallas_call(
        body, grid=(S // TM,), in_specs=[_row(D), _row(F_FF), _whole(w_down), _vec(D), _row(D)],
        out_specs=[_vec(128), _row(D), _row(D), _vec(D)],
        out_shape=[SDS((1, 128), f32), SDS((S, D), f32), SDS((S, D), bf16), SDS((1, D), f32)], name="ffn_down_loss",
        compiler_params=_params(("arbitrary",), 14 * TM * D + 2 * TM * F_FF + 2 * F_FF * D, 28 * TM * D),
    )(x2, a_act, w_down, g_post, target)


def _norm_bwd_mid(dy, d_g, d_u, w_gate_t, w_up_t, x2, mix, g_ffn_pre, g_mix_post):
    def body(dy_ref, dgt_ref, dut_ref, wg_ref, wu_ref, x2_ref, mix_ref, g3_ref, g2_ref, dx2_ref, dmix_ref, dg3_ref, dg2_ref):
        dh = jnp.dot(dgt_ref[...], wg_ref[...], preferred_element_type=f32)
        dh += jnp.dot(dut_ref[...], wu_ref[...], preferred_element_type=f32)
        d3, dg3 = _rms_bwd(x2_ref[...], g3_ref[...], dh)
        dx2 = dy_ref[...] + d3
        dmix, dg2 = _rms_bwd(mix_ref[...], g2_ref[...], dx2)
        dx2_ref[...] = dx2
        dmix_ref[...] = dmix.astype(bf16)
        _acc_rows(dg3_ref, dg3)
        _acc_rows(dg2_ref, dg2)

    tm = TM // 2
    row = lambda w: _row(w, tm)
    return pl.pallas_call(
        body, grid=(S // tm,),
        in_specs=[row(D), row(F_FF), row(F_FF), _whole(w_gate_t), _whole(w_up_t), row(D), row(D), _vec(D), _vec(D)],
        out_specs=[row(D), row(D), _vec(D), _vec(D)],
        out_shape=[SDS((S, D), f32), SDS((S, D), bf16), SDS((1, D), f32), SDS((1, D), f32)], name="ffn_bwd_in_norm",
        compiler_params=_params(("arbitrary",), 18 * tm * D + 4 * tm * F_FF + 4 * F_FF * D, 28 * tm * D),
    )(dy, d_g, d_u, w_gate_t, w_up_t, x2, mix, g_ffn_pre, g_mix_post)


def _norm_bwd_in(dx2, dh1, x, g):
    def body(dx2_ref, dh_ref, x_ref, g_ref, gx_ref, dg_ref):
        d1, dg = _rms_bwd(x_ref[...], g_ref[...], dh_ref[...])
        gx_ref[...] = dx2_ref[...] + d1
        _acc_rows(dg_ref, dg)

    return pl.pallas_call(
        body, grid=(S // TM,), in_specs=[_row(D)] * 3 + [_vec(D)], out_specs=[_row(D), _vec(D)],
        out_shape=[SDS((S, D), f32), SDS((1, D), f32)], name="norm_bwd_in",
        compiler_params=_params(("arbitrary",), 16 * TM * D, 16 * TM * D))(dx2, dh1, x, g)


def _rope_tables():
    half = ROPE_DIM // 2
    inv_freq = np.power(np.float32(ROPE_THETA), -np.arange(0, ROPE_DIM, 2, dtype=np.float32) / np.float32(ROPE_DIM))
    row = np.arange(S)
    groups = []
    for _, d in DIL_GROUPS:
        pos = ((row % (S // d)) * d + row // (S // d)).astype(np.float32)
        ang = pos[:, None] * inv_freq[None, :].astype(np.float32)
        cos, sin = np.cos(ang).astype(np.float32), np.sin(ang).astype(np.float32)
        c = np.concatenate([cos, cos, np.ones((S, HD - ROPE_DIM), np.float32)], axis=1)
        s_lo = np.concatenate([-sin, np.zeros((S, HD - half), np.float32)], axis=1)
        s_hi = np.concatenate([np.zeros((S, half), np.float32), sin, np.zeros((S, HD - ROPE_DIM), np.float32)], axis=1)
        groups.append(np.stack([np.concatenate([t, t], axis=1) for t in (c, s_lo, s_hi)]))
    return jnp.asarray(np.stack(groups))


def _rotate(x, c, lo, hi, sign):
    tile = lambda t: jnp.tile(t, (1, DIL_W // 128))
    return (x * tile(c) + pltpu.roll(x, DIL_W - ROPE_DIM // 2, 1) * (tile(lo) * sign)
            + pltpu.roll(x, ROPE_DIM // 2, 1) * (tile(hi) * sign))


def _table_specs(g):
    return [pl.BlockSpec((None, None, TM, 128), lambda i, k=k: (g, k, i, 0)) for k in range(3)]


def _rope_fwd(g, p_qkv, tables):
    def body(x_ref, c_ref, lo_ref, hi_ref, o_ref):
        c, lo, hi = c_ref[...], lo_ref[...], hi_ref[...]
        for part in range(2):
            cols = slice(part * DIL_W, (part + 1) * DIL_W)
            o_ref[:, cols] = _rotate(x_ref[:, cols], c, lo, hi, 1.0).astype(bf16)
        o_ref[:, 2 * DIL_W:] = x_ref[:, 2 * DIL_W:].astype(bf16)

    return pl.pallas_call(
        body, grid=(S // TM,), in_specs=[_row(QKV_W)] + _table_specs(g), out_specs=_row(QKV_W),
        out_shape=SDS((S, QKV_W), bf16), name=f"rope_fwd_{g}",
        compiler_params=_params(("parallel",), 6 * TM * QKV_W + 12 * TM * 128, 24 * TM * QKV_W))(p_qkv, tables, tables, tables)


def _rope_bwd(g, dq, dk, dv, tables):
    def body(dq_ref, dk_ref, dv_ref, c_ref, lo_ref, hi_ref, o_ref):
        c, lo, hi = c_ref[...], lo_ref[...], hi_ref[...]
        o_ref[:, :DIL_W] = _rotate(dq_ref[...], c, lo, hi, -1.0).astype(bf16)
        o_ref[:, DIL_W:2 * DIL_W] = _rotate(dk_ref[...], c, lo, hi, -1.0).astype(bf16)
        o_ref[:, 2 * DIL_W:] = dv_ref[...].astype(bf16)

    return pl.pallas_call(
        body, grid=(S // TM,), in_specs=[_row(DIL_W)] * 3 + _table_specs(g), out_specs=_row(QKV_W),
        out_shape=SDS((S, QKV_W), bf16), name=f"rope_bwd_{g}",
        compiler_params=_params(("parallel",), 6 * TM * QKV_W + 12 * TM * 128, 24 * TM * QKV_W))(dq, dk, dv, tables, tables, tables)


def _nt(a, b):
    return lax.dot_general(a, b, (((1,), (1,)), ((), ())), preferred_element_type=f32)


def _tn(a, b):
    return lax.dot_general(a, b, (((0,), (0,)), ((), ())), preferred_element_type=f32)


STEP_BLOCKS = 4
STEP_ROWS = STEP_BLOCKS * BLK


def _dil_prev(g, b):
    _, d = DIL_GROUPS[g]
    nb = S // d // BLK
    if nb == 1 or (b == 0 and nb <= STEP_BLOCKS):
        return None
    return "in" if b > 0 else "halo"


def _bnt(a, b):
    return lax.dot_general(a, b, (((2,), (2,)), ((0,), (0,))), preferred_element_type=f32)


def _bnn(a, b):
    return lax.dot_general(a, b, (((2,), (1,)), ((0,), (0,))), preferred_element_type=f32)


def _btn(a, b):
    return lax.dot_general(a, b, (((1,), (1,)), ((0,), (0,))), preferred_element_type=f32)


def _on_tail(x, tail, fn):
    if tail == x.shape[0]:
        return fn(x)
    return jnp.concatenate([x[:-tail], fn(x[-tail:])], axis=0)


def _heads(ref, part):
    n = ref.shape[0] // BLK
    return jnp.stack([ref[b * BLK:(b + 1) * BLK, part * DIL_W + h * HD:part * DIL_W + (h + 1) * HD]
                      for b in range(n) for h in range(SLOTS)])


def _dil_operands(g, qkv_ref, halo_ref):
    q, kc, vc = (_heads(qkv_ref, part) for part in range(3))
    qi = lax.broadcasted_iota(jnp.int32, (1, BLK, BLK), 1)
    kj = lax.broadcasted_iota(jnp.int32, (1, BLK, BLK), 2)
    with_prev = [b for b in range(STEP_BLOCKS) if _dil_prev(g, b) is not None]
    tail = SLOTS * len(with_prev)
    if not tail:
        return q, kc, vc, None, None, kj <= qi, None, 0
    assert with_prev == list(range(STEP_BLOCKS - len(with_prev), STEP_BLOCKS))
    inside = SLOTS * sum(_dil_prev(g, b) == "in" for b in with_prev)
    kp, vp, prev = kc[:inside], vc[:inside], jnp.broadcast_to(kj >= qi, (inside, BLK, BLK))
    if inside < tail:
        no_halo = jnp.where(pl.program_id(0) == 0, BLK + 1, 0)
        kp = jnp.concatenate([_heads(halo_ref, 1), kp], axis=0)
        vp = jnp.concatenate([_heads(halo_ref, 2), vp], axis=0)
        prev = jnp.concatenate([jnp.broadcast_to(kj >= qi + no_halo, (SLOTS, BLK, BLK)), prev], axis=0)
    return q, kc, vc, kp, vp, kj <= qi, prev, tail


def _dil_in_specs(g, n_aux):
    step = lambda w: pl.BlockSpec((STEP_ROWS, w), lambda i: (i, 0))
    halo = [pl.BlockSpec((BLK, QKV_W), lambda i: (jnp.maximum(i * STEP_BLOCKS - 1, 0), 0))]
    needs_halo = _dil_prev(g, 0) == "halo"
    return [step(QKV_W)] + (halo if needs_halo else []) + [step(DIL_W)] * n_aux, needs_halo


def _dil_fwd(g, qkv):
    in_specs, needs_halo = _dil_in_specs(g, 0)

    def body(*refs):
        qkv_ref, halo_ref = refs[0], refs[1] if needs_halo else None
        o_ref, lse_ref = refs[-2:]
        q, kc, vc, kp, vp, cur, prev, tail = _dil_operands(g, qkv_ref, halo_ref)
        sc = jnp.where(cur, _bnt(q, kc) * SCALE, NEG)
        m = jnp.max(sc, axis=-1, keepdims=True)
        if tail:
            sp = jnp.where(prev, _bnt(q[-tail:], kp) * SCALE, NEG)
            m = _on_tail(m, tail, lambda t: jnp.maximum(t, jnp.max(sp, axis=-1, keepdims=True)))
            pp = jnp.exp(sp - m[-tail:])
        pc = jnp.exp(sc - m)
        den = jnp.sum(pc, axis=-1, keepdims=True)
        if tail:
            den = _on_tail(den, tail, lambda t: t + jnp.sum(pp, axis=-1, keepdims=True))
        inv = 1.0 / den
        o = _bnn((pc * inv).astype(bf16), vc)
        if tail:
            o = _on_tail(o, tail, lambda t: t + _bnn((pp * inv[-tail:]).astype(bf16), vp))
        lse = m + jnp.log(den)
        for b in range(STEP_BLOCKS):
            for h in range(SLOTS):
                rows, hs = slice(b * BLK, (b + 1) * BLK), slice(h * HD, (h + 1) * HD)
                o_ref[rows, hs] = o[SLOTS * b + h]
                lse_ref[rows, hs] = jnp.broadcast_to(lse[SLOTS * b + h], (BLK, HD))

    out = pl.BlockSpec((STEP_ROWS, DIL_W), lambda i: (i, 0))
    return pl.pallas_call(
        body, grid=(S // STEP_ROWS,), in_specs=in_specs, out_specs=[out, out], out_shape=[SDS((S, DIL_W), f32)] * 2,
        name=f"dil_fwd_{g}", compiler_params=_params(("parallel",), 12 * STEP_ROWS * DIL_W, 2 << 20),
    )(*([qkv] * (2 if needs_halo else 1)))


def _dil_combine(outs, lses):
    def body(o0, o1, o2, l0, l1, l2, out_ref, lse_ref, so1, so2, sl1, sl2):
        for (_, d), src, dst in ((DIL_GROUPS[1], o1, so1), (DIL_GROUPS[2], o2, so2),
                                 (DIL_GROUPS[1], l1, sl1), (DIL_GROUPS[2], l2, sl2)):
            rows = S // d
            for r in range(d):
                dst[pl.ds(r, rows, stride=d), :] = src[r * rows:(r + 1) * rows, :]
        a, b, c = l0[...], sl1[...], sl2[...]
        m = jnp.maximum(jnp.maximum(a, b), c)
        ea, eb, ec = jnp.exp(a - m), jnp.exp(b - m), jnp.exp(c - m)
        z = ea + eb + ec
        inv = 1.0 / z
        out_ref[...] = (ea * inv) * o0[...] + (eb * inv) * so1[...] + (ec * inv) * so2[...]
        lse_ref[...] = m + jnp.log(z)

    blk = pl.BlockSpec((S, 128), lambda c: (0, c))
    return pl.pallas_call(
        body, grid=(DIL_W // 128,), in_specs=[blk] * 6, out_specs=[blk] * 2,
        out_shape=[SDS((S, DIL_W), f32)] * 2, scratch_shapes=[pltpu.VMEM((S, 128), f32)] * 4, name="dil_combine",
        compiler_params=_params(("parallel",), 32 * S * 128, 32 * S * 128))(*outs, *lses)


def _dil_bwd(g, qkv, d_out, delta, lse):
    in_specs, needs_halo = _dil_in_specs(g, 3)

    def body(*refs):
        qkv_ref, halo_ref = refs[0], refs[1] if needs_halo else None
        do_ref, dl_ref, lse_ref, dq_ref, dk_ref, dv_ref = refs[-6:]
        q, kc, vc, kp, vp, cur, prev, tail = _dil_operands(g, qkv_ref, halo_ref)
        tiles = [(slice(b * BLK, (b + 1) * BLK), h) for b in range(STEP_BLOCKS) for h in range(SLOTS)]
        do = jnp.stack([do_ref[rows, h * HD:(h + 1) * HD] for rows, h in tiles]).astype(bf16)
        lse = jnp.stack([lse_ref[rows, h * HD:h * HD + 1] for rows, h in tiles])
        delta = jnp.stack([dl_ref[rows, h * HD:h * HD + 1] for rows, h in tiles])

        def probs(q, k, mask, lse, do, v, delta):
            p = jnp.exp(jnp.where(mask, _bnt(q, k) * SCALE, NEG) - lse)
            ds = p * (_bnt(do, v) - delta) * SCALE
            return p.astype(bf16), ds.astype(bf16)

        p, ds = probs(q, kc, cur, lse, do, vc, delta)
        dq, dk, dv = _bnn(ds, kc), _btn(ds, q), _btn(p, do)
        if tail:
            p, ds = probs(q[-tail:], kp, prev, lse[-tail:], do[-tail:], vp, delta[-tail:])
            dq = _on_tail(dq, tail, lambda t: t + _bnn(ds, kp))
            dk_p, dv_p = _btn(ds, q[-tail:]), _btn(p, do[-tail:])
            inside = tail - SLOTS if needs_halo else tail
            pad = jnp.zeros((len(tiles) - inside, BLK, HD), f32)
            dk = dk + jnp.concatenate([dk_p[tail - inside:], pad], axis=0)
            dv = dv + jnp.concatenate([dv_p[tail - inside:], pad], axis=0)
        first = pl.multiple_of(pl.program_id(0) * STEP_ROWS, STEP_ROWS)
        for t, (rows, h) in enumerate(tiles):
            hs = slice(h * HD, (h + 1) * HD)
            own = pl.ds(pl.multiple_of(first + rows.start, BLK), BLK)
            dq_ref[rows, hs] = dq[t]
            dk_ref[own, hs] = dk[t]
            dv_ref[own, hs] = dv[t]
        if needs_halo:
            before = pl.ds(pl.multiple_of(jnp.maximum(first - BLK, 0), BLK), BLK)
            for h in range(SLOTS):
                hs = slice(h * HD, (h + 1) * HD)
                dk_ref[before, hs] += dk_p[h]
                dv_ref[before, hs] += dv_p[h]

    whole = pl.BlockSpec((S, DIL_W), lambda i: (0, 0))
    return pl.pallas_call(
        body, grid=(S // STEP_ROWS,), in_specs=in_specs,
        out_specs=[pl.BlockSpec((STEP_ROWS, DIL_W), lambda i: (i, 0)), whole, whole],
        out_shape=[SDS((S, DIL_W), f32)] * 3, name=f"dil_bwd_{g}",
        compiler_params=_params(("arbitrary",), 20 * STEP_ROWS * DIL_W + 8 * S * DIL_W, 2 << 20),
    )(*([qkv] * (2 if needs_halo else 1)), d_out, delta, lse)


def _scan_rows(x, reverse):
    row = lax.broadcasted_iota(jnp.int32, x.shape, 0)
    k = 1
    while k < S:
        if reverse:
            x = x + jnp.where(row < S - k, pltpu.roll(x, S - k, 0), 0.0)
        else:
            x = x + jnp.where(row >= k, pltpu.roll(x, k, 0), 0.0)
        k *= 2
    return x


N_PAIR = N_FOX // 2
_PAIR_Q = pl.BlockSpec((None, S, 128), lambda p: (p, 0, 0))
_PAIR_K = pl.BlockSpec((None, 8, S), lambda p: (p, 0, 0))


def _forget_fwd(fz, b128):
    def body(z_ref, b_ref, fq_ref, fk_ref):
        z = z_ref[...] + b_ref[...]
        logf = jnp.minimum(z, 0.0) - jnp.log1p(jnp.exp(-jnp.abs(z)))
        f_cum = _scan_rows(logf, reverse=False)
        f_cum_t = f_cum.T
        fq_ref[...] = jnp.zeros_like(fq_ref)
        fk_ref[...] = jnp.zeros_like(fk_ref)
        for p in range(N_PAIR):
            fq_ref[p, :, 0:2] = f_cum[:, 2 * p:2 * p + 2]
            fk_ref[p, 0:2, :] = f_cum_t[2 * p:2 * p + 2, :]

    return pl.pallas_call(
        body, grid=(1,), in_specs=[pl.BlockSpec((S, 128), lambda i: (0, 0)), _vec(128)],
        out_specs=[pl.BlockSpec((N_PAIR, S, 128), lambda i: (0, 0, 0)), pl.BlockSpec((N_PAIR, 8, S), lambda i: (0, 0, 0))],
        out_shape=[SDS((N_PAIR, S, 128), f32), SDS((N_PAIR, 8, S), f32)], name="forget_fwd",
        compiler_params=_params(("arbitrary",), 24 * S * 128, 24 * S * 128))(fz, b128)


def _forget_bwd(fz, b128, d_f_cols, d_f_rows):
    def body(z_ref, b_ref, dfc_ref, dfr_ref, dz_ref, db_ref, df_sc):
        z = z_ref[...] + b_ref[...]
        df_sc[...] = jnp.zeros_like(df_sc)
        for p in range(N_PAIR):
            df_sc[:, 2 * p:2 * p + 2] = dfr_ref[p, :, 0:2] + dfc_ref[p].T[:, 0:2]
        dz = _scan_rows(df_sc[...], reverse=True) * jax.nn.sigmoid(-z)
        dz_ref[...] = dz
        db_ref[...] = jnp.sum(dz, axis=0, keepdims=True)

    full = pl.BlockSpec((S, 128), lambda i: (0, 0))
    return pl.pallas_call(
        body, grid=(1,),
        in_specs=[full, _vec(128), pl.BlockSpec((N_PAIR, 8, S), lambda i: (0, 0, 0)), pl.BlockSpec((N_PAIR, S, 128), lambda i: (0, 0, 0))],
        out_specs=[full, _vec(128)], out_shape=[SDS((S, 128), f32), SDS((1, 128), f32)],
        scratch_shapes=[pltpu.VMEM((S, 128), f32)], name="forget_bwd",
        compiler_params=_params(("arbitrary",), 32 * S * 128, 24 * S * 128))(fz, b128, d_f_cols, d_f_rows)


def _fox_scores(q_ref, k_ref, fq_ref, fk_ref, qi, hh):
    n = (qi + 1) * TQ
    rows, hs = slice(qi * TQ, n), slice(hh * HD, (hh + 1) * HD)
    s = _nt(q_ref[rows, hs], k_ref[0:n, hs]) * SCALE + (fq_ref[rows, hh:hh + 1] - fk_ref[hh:hh + 1, 0:n])
    qpos = qi * TQ + lax.broadcasted_iota(jnp.int32, (TQ, n), 0)
    kpos = lax.broadcasted_iota(jnp.int32, (TQ, n), 1)
    return jnp.where(kpos <= qpos, s, NEG)


def _pair_cols(first):
    return pl.BlockSpec((S, 128), lambda p: (0, first + p))


def _fox_fwd(vr, fq, fk):
    def body(q_ref, k_ref, v_ref, fq_ref, fk_ref, o_ref, lse_ref):
        lse_ref[...] = jnp.zeros_like(lse_ref)
        for hh in range(2):
            hs = slice(hh * HD, (hh + 1) * HD)
            for qi in range(S // TQ):
                n = (qi + 1) * TQ
                rows = slice(qi * TQ, n)
                s = _fox_scores(q_ref, k_ref, fq_ref, fk_ref, qi, hh)
                m = jnp.max(s, axis=-1, keepdims=True)
                p = jnp.exp(s - m)
                den = jnp.sum(p, axis=-1, keepdims=True)
                o_ref[rows, hs] = jnp.dot((p * (1.0 / den)).astype(bf16), v_ref[0:n, hs], preferred_element_type=f32)
                lse_ref[rows, hh:hh + 1] = m + jnp.log(den)

    return pl.pallas_call(
        body, grid=(N_PAIR,), in_specs=[_pair_cols(0), _pair_cols(N_PAIR), _pair_cols(2 * N_PAIR), _PAIR_Q, _PAIR_K],
        out_specs=[_pair_cols(0), _PAIR_Q], out_shape=[SDS((S, FOX_W), f32), SDS((N_PAIR, S, 128), f32)],
        name="fox_fwd", compiler_params=_params(("parallel",), 12 * S * 128, 16 * TQ * S),
    )(vr, vr, vr, fq, fk)


def _fox_bwd(vr, fq, fk, lse, d_out, delta):
    def body(q_ref, k_ref, v_ref, do_ref, fq_ref, fk_ref, lse_ref, dl_ref, dq_ref, dk_ref, dv_ref, dfc_ref, dfr_ref,
             dk_sc, dv_sc):
        dfc_ref[...] = jnp.zeros_like(dfc_ref)
        dfr_ref[...] = jnp.zeros_like(dfr_ref)
        for hh in range(2):
            hs = slice(hh * HD, (hh + 1) * HD)
            dk_sc[...] = jnp.zeros_like(dk_sc)
            dv_sc[...] = jnp.zeros_like(dv_sc)
            for qi in range(S // TQ):
                n = (qi + 1) * TQ
                rows = slice(qi * TQ, n)
                q, do, k, v = q_ref[rows, hs], do_ref[rows, hs], k_ref[0:n, hs], v_ref[0:n, hs]
                p = jnp.exp(_fox_scores(q_ref, k_ref, fq_ref, fk_ref, qi, hh) - lse_ref[rows, hh:hh + 1])
                ds = p * (_nt(do, v) - dl_ref[rows, hh:hh + 1])
                dsb = ds.astype(bf16)
                dq_ref[rows, hs] = jnp.dot(dsb, k, preferred_element_type=f32) * SCALE
                dk_sc[0:n, :] += _tn(dsb, q) * SCALE
                dv_sc[0:n, :] += _tn(p.astype(bf16), do)
                dfc_ref[hh:hh + 1, 0:n] -= jnp.sum(ds, axis=0, keepdims=True)
                dfr_ref[rows, hh:hh + 1] = jnp.sum(ds, axis=-1, keepdims=True)
            dk_ref[:, hs] = dk_sc[...]
            dv_ref[:, hs] = dv_sc[...]

    cols = [_pair_cols(k * N_PAIR) for k in range(3)]
    return pl.pallas_call(
        body, grid=(N_PAIR,), in_specs=cols + [_pair_cols(0), _PAIR_Q, _PAIR_K, _PAIR_Q, _PAIR_Q],
        out_specs=[_pair_cols(0)] * 3 + [_PAIR_K, _PAIR_Q],
        out_shape=[SDS((S, FOX_W), f32)] * 3 + [SDS((N_PAIR, 8, S), f32), SDS((N_PAIR, S, 128), f32)],
        scratch_shapes=[pltpu.VMEM((S, HD), f32)] * 2, name="fox_bwd",
        compiler_params=_params(("parallel",), 32 * S * 128, 24 * TQ * S),
    )(vr, vr, vr, d_out, fq, fk, lse, delta)


def _merge_fwd(out_a, out_b, w_a, w_b, gf):
    cw = D // N_SHARD

    def body(oa_ref, ob_ref, wa_ref, wb_ref, ga_ref, gb_ref, ya_ref, yb_ref, mg_ref):
        oa, ob = oa_ref[...].astype(bf16), ob_ref[...].astype(bf16)
        for j in range(N_SHARD):
            cols = slice(j * cw, (j + 1) * cw)
            ya = jnp.dot(oa, wa_ref[j], preferred_element_type=f32)
            yb = jnp.dot(ob, wb_ref[j], preferred_element_type=f32)
            ya_ref[:, cols] = ya
            yb_ref[:, cols] = yb
            mg_ref[:, cols] = (jax.nn.sigmoid(ga_ref[:, cols]) * ya + jax.nn.sigmoid(gb_ref[:, cols]) * yb).astype(bf16)

    full = lambda a: pl.BlockSpec(a.shape, lambda i: (0, 0, 0))
    return pl.pallas_call(
        body, grid=(S // TM,),
        in_specs=[_row(DIL_W), _row(FOX_W), full(w_a), full(w_b), _row(D), pl.BlockSpec((TM, D), lambda i: (i, 1))],
        out_specs=[_row(D)] * 3, out_shape=[SDS((S, D), f32), SDS((S, D), f32), SDS((S, D), bf16)], name="merge_fwd",
        compiler_params=_params(("parallel",), 22 * TM * D + 2 * (DIL_W + FOX_W) * D, 16 * TM * D),
    )(out_a, out_b, w_a, w_b, gf, gf)


def _merge_bwd(d_mix, w_out, ya, yb, gf):
    def body(dx_ref, w_ref, ya_ref, yb_ref, ga_ref, gb_ref, dya_ref, dyb_ref, dg_ref):
        dm = _nt(dx_ref[...], w_ref[...])
        sa, sb = jax.nn.sigmoid(ga_ref[...]), jax.nn.sigmoid(gb_ref[...])
        dya_ref[...] = (dm * sa).astype(bf16)
        dyb_ref[...] = (dm * sb).astype(bf16)
        dg_ref[:, :D] = (dm * ya_ref[...] * sa * (1.0 - sa)).astype(bf16)
        dg_ref[:, D:] = (dm * yb_ref[...] * sb * (1.0 - sb)).astype(bf16)

    return pl.pallas_call(
        body, grid=(S // TM,),
        in_specs=[_row(D), _whole(w_out)] + [_row(D)] * 3 + [pl.BlockSpec((TM, D), lambda i: (i, 1))],
        out_specs=[_row(D), _row(D), _row(2 * D)],
        out_shape=[SDS((S, D), bf16), SDS((S, D), bf16), SDS((S, 2 * D), bf16)], name="proj_out_bwd_merge",
        compiler_params=_params(("parallel",), 26 * TM * D + 2 * D * D, 28 * TM * D))(d_mix, w_out, ya, yb, gf, gf)


def _branch_bwd(d_ya, d_yb, w_a, w_b, out_a, out_b):
    cw = D // N_SHARD

    def body(dya_ref, dyb_ref, wa_ref, wb_ref, oa_ref, ob_ref, doa_ref, dla_ref, dob_ref, dlb_ref):
        doa = jnp.zeros((TM, DIL_W), f32)
        dob = jnp.zeros((TM, FOX_W), f32)
        for j in range(N_SHARD):
            cols = slice(j * cw, (j + 1) * cw)
            doa += _nt(dya_ref[:, cols], wa_ref[j])
            dob += _nt(dyb_ref[:, cols], wb_ref[j])
        doa_ref[...] = doa
        dob_ref[...] = dob.astype(bf16)
        prod_a = doa * oa_ref[...]
        for h in range(SLOTS):
            hs = slice(h * HD, (h + 1) * HD)
            dla_ref[:, hs] = jnp.broadcast_to(jnp.sum(prod_a[:, hs], axis=-1, keepdims=True), (TM, HD))
        prod_b = dob * ob_ref[...]
        dlb_ref[...] = jnp.zeros_like(dlb_ref)
        for h in range(N_FOX):
            dlb_ref[h // 2, :, h % 2:h % 2 + 1] = jnp.sum(prod_b[:, h * HD:(h + 1) * HD], axis=-1, keepdims=True)

    full = lambda a: pl.BlockSpec(a.shape, lambda i: (0, 0, 0))
    return pl.pallas_call(
        body, grid=(S // TM,),
        in_specs=[_row(D), _row(D), full(w_a), full(w_b), _row(DIL_W), _row(FOX_W)],
        out_specs=[_row(DIL_W), _row(DIL_W), _row(FOX_W), pl.BlockSpec((N_PAIR, TM, 128), lambda i: (0, i, 0))],
        out_shape=[SDS((S, DIL_W), f32), SDS((S, DIL_W), f32), SDS((S, FOX_W), bf16), SDS((N_PAIR, S, 128), f32)],
        name="branch_bwd", compiler_params=_params(("parallel",), 8 * TM * D + 2 * (DIL_W + FOX_W) * D, 8 * TM * D),
    )(d_ya, d_yb, w_a, w_b, out_a, out_b)


def _branch_grads(out_a, out_b, d_ya, d_yb):
    cw = D // N_SHARD

    def body(oa_ref, ob_ref, dya_ref, dyb_ref, ga_ref, gb_ref):
        ga_ref[...] = _tn(oa_ref[...].astype(bf16), dya_ref[...]).astype(bf16)
        gb_ref[...] = _tn(ob_ref[...].astype(bf16), dyb_ref[...]).astype(bf16)

    whole = lambda w: pl.BlockSpec((S, w), lambda j: (0, 0))
    cols = pl.BlockSpec((S, cw), lambda j: (0, j))
    return pl.pallas_call(
        body, grid=(N_SHARD,), in_specs=[whole(DIL_W), whole(FOX_W), cols, cols],
        out_specs=[pl.BlockSpec((None, DIL_W, cw), lambda j: (j, 0, 0)), pl.BlockSpec((None, FOX_W, cw), lambda j: (j, 0, 0))],
        out_shape=[SDS((N_SHARD, DIL_W, cw), bf16), SDS((N_SHARD, FOX_W, cw), bf16)], name="grad_w_proj_ab",
        compiler_params=_params(("parallel",), 4 * S * (DIL_W + FOX_W) + 4 * S * cw + 4 * (DIL_W + FOX_W) * cw,
                                4 * S * (DIL_W + FOX_W)))(out_a, out_b, d_ya, d_yb)


FF_TN = F_FF // 2
FF_TM = 1024


def _ffn_fwd(h, w_gate_t, w_up_t):
    def body(h_ref, wg_ref, wu_ref, g_ref, u_ref, a_ref):
        hb = h_ref[...]
        g = _nt(hb, wg_ref[...])
        u = _nt(hb, wu_ref[...])
        g_ref[...] = g
        u_ref[...] = u
        a_ref[...] = (g * jax.nn.sigmoid(g) * u).astype(bf16)

    tile = pl.BlockSpec((FF_TM, FF_TN), lambda j, i: (i, j))
    wspec = pl.BlockSpec((FF_TN, D), lambda j, i: (j, 0))
    return pl.pallas_call(
        body, grid=(F_FF // FF_TN, S // FF_TM),
        in_specs=[pl.BlockSpec((FF_TM, D), lambda j, i: (i, 0)), wspec, wspec], out_specs=[tile] * 3,
        out_shape=[SDS((S, F_FF), f32), SDS((S, F_FF), f32), SDS((S, F_FF), bf16)], name="ffn_fwd",
        compiler_params=_params(("parallel", "parallel"), 2 * FF_TM * D + 4 * D * FF_TN + 10 * FF_TM * FF_TN, 16 * FF_TM * FF_TN),
    )(h, w_gate_t, w_up_t)


def _ffn_bwd_act(d_ff, w_down, g_act, u_act):
    def body(d_ref, wd_ref, g_ref, u_ref, dg_ref, du_ref):
        da = _nt(d_ref[...], wd_ref[...])
        g = g_ref[...]
        sg = jax.nn.sigmoid(g)
        du_ref[...] = (da * g * sg).astype(bf16)
        dg_ref[...] = (da * u_ref[...] * sg * (1.0 + g * (1.0 - sg))).astype(bf16)

    tile = pl.BlockSpec((FF_TM, FF_TN), lambda j, i: (i, j))
    return pl.pallas_call(
        body, grid=(F_FF // FF_TN, S // FF_TM),
        in_specs=[pl.BlockSpec((FF_TM, D), lambda j, i: (i, 0)), pl.BlockSpec((FF_TN, D), lambda j, i: (j, 0)), tile, tile],
        out_specs=[tile, tile], out_shape=[SDS((S, F_FF), bf16)] * 2, name="ffn_bwd_act",
        compiler_params=_params(("parallel", "parallel"), 2 * FF_TM * D + 2 * D * FF_TN + 12 * FF_TM * FF_TN, 16 * FF_TM * FF_TN),
    )(d_ff, w_down, g_act, u_act)


def _row_tile(rows):
    return next(t for t in (376, 128, 176, 64, 32, 16, 8) if rows % t == 0)


def _adamw_math(w, g, m, v):
    c1 = 1.0 - ADAM_B1 ** ADAM_STEP
    c2 = 1.0 - ADAM_B2 ** ADAM_STEP
    m_new = ADAM_B1 * m + (1.0 - ADAM_B1) * g
    v_new = ADAM_B2 * v + (1.0 - ADAM_B2) * (g * g)
    return -ADAM_LR * ((m_new / c1) / (jnp.sqrt(v_new / c2) + ADAM_EPS) + ADAM_WD * w), m_new, v_new


def _adamw(w, g, m, v, name):
    rows, cols = w.shape
    tm = _row_tile(rows)

    def body(w_ref, g_ref, m_ref, v_ref, d_ref, nm_ref, nv_ref):
        d_ref[...], nm_ref[...], nv_ref[...] = _adamw_math(w_ref[...], g_ref[...], m_ref[...], v_ref[...])

    spec = pl.BlockSpec((tm, cols), lambda i: (i, 0))
    return pl.pallas_call(
        body, grid=(rows // tm,), in_specs=[spec] * 4, out_specs=[spec] * 3, out_shape=[SDS(w.shape, f32)] * 3,
        name=name, compiler_params=_params(("parallel",), 28 * tm * cols, 16 * tm * cols))(w, g, m, v)


def _adamw_halves(w, g_mine, g_theirs, m, v, name):
    rows, cols = w.shape
    tm = _row_tile(rows // 2)
    per_half = rows // 2 // tm
    core = lax.axis_index("c").astype(jnp.int32).reshape(1)

    def body(c_ref, w_ref, gm_ref, gt_ref, m_ref, v_ref, g_ref, d_ref, nm_ref, nv_ref):
        mine = pl.program_id(0) // per_half == c_ref[0]
        g = jnp.where(mine, gm_ref[...], gt_ref[...])
        g_ref[...] = g
        d_ref[...], nm_ref[...], nv_ref[...] = _adamw_math(w_ref[...], g, m_ref[...], v_ref[...])

    spec = pl.BlockSpec((tm, cols), lambda i, c_ref: (i, 0))
    in_half = lambda i, first: jnp.clip(i - first * per_half, 0, per_half - 1)
    grid_spec = pltpu.PrefetchScalarGridSpec(
        num_scalar_prefetch=1, grid=(rows // tm,),
        in_specs=[spec, pl.BlockSpec((tm, cols), lambda i, c_ref: (in_half(i, c_ref[0]), 0)),
                  pl.BlockSpec((tm, cols), lambda i, c_ref: (in_half(i, 1 - c_ref[0]), 0)), spec, spec],
        out_specs=[spec] * 4)
    return pl.pallas_call(
        body, grid_spec=grid_spec, out_shape=[SDS(w.shape, f32)] * 4, name=name,
        compiler_params=_params(("parallel",), 36 * tm * cols, 16 * tm * cols))(core, w, g_mine, g_theirs, m, v)


_ANY = pl.BlockSpec(memory_space=pl.ANY)


def _place():
    x, y, c = lax.axis_index("x"), lax.axis_index("y"), lax.axis_index("c")
    chips = [(1 - x, y), (x, 1 - y), (1 - x, 1 - y)]
    return x, y, c, chips


def _halved(t):
    return t.reshape(t.shape[:-2] + (2, t.shape[-2] // 2, t.shape[-1]))


def _gather_body(src, out, send_ici, recv_ici, send_d2d, recv_d2d):
    x, y, c, chips = _place()
    sibling = (x, y, 1 - c)
    me_j = 2 * x + y
    sends = []
    for a in range(len(src)):
        for p in range(3):
            cp = pltpu.make_async_remote_copy(
                src_ref=src[a].at[c], dst_ref=out[a].at[me_j, c], send_sem=send_ici.at[a, p],
                recv_sem=recv_ici.at[a, p], device_id=(*chips[p], c), device_id_type=MESH)
            cp.start()
            sends.append(cp)
    for a in range(len(src)):
        for p, (px, py) in enumerate(chips):
            blk = out[a].at[2 * px + py, c]
            pltpu.make_async_remote_copy(
                src_ref=blk, dst_ref=blk, send_sem=send_ici.at[a, p], recv_sem=recv_ici.at[a, p],
                device_id=sibling, device_id_type=MESH).wait_recv()
            fw = pltpu.make_async_remote_copy(
                src_ref=blk, dst_ref=blk, send_sem=send_d2d.at[a, p], recv_sem=recv_d2d.at[a, p],
                device_id=sibling, device_id_type=MESH)
            fw.start()
            sends.append(fw)
    for a in range(len(src)):
        for p, (px, py) in enumerate(chips):
            blk = out[a].at[2 * px + py, 1 - c]
            pltpu.make_async_remote_copy(
                src_ref=blk, dst_ref=blk, send_sem=send_d2d.at[a, p], recv_sem=recv_d2d.at[a, p],
                device_id=sibling, device_id_type=MESH).wait_recv()
    for cp in sends:
        cp.wait_send()


def _handshake(peers):
    barrier = pltpu.get_barrier_semaphore()
    for peer in peers:
        pl.semaphore_signal(barrier, inc=1, device_id=peer, device_id_type=MESH)
    pl.semaphore_wait(barrier, len(peers))


_SEQUENCER = dict(axis_name="sequencer", num_cores=1)
GATHER_LATE_ID, SCATTER_EARLY_ID, SWAP_EARLY_ID, GATHER_FIRST_ID, SCATTER_LATE_ID = 1, 2, 3, 4, 5


def _all_gather_async(shards, after, name, collective_id):
    n, k = len(shards), len(after)

    def body(*refs):
        x, y, c, chips = _place()
        _handshake([(*chip, c) for chip in chips] + [(x, y, 1 - c)])
        _gather_body(refs[:n], refs[n + k:2 * n + k], *refs[2 * n + k:])

    return pl.kernel(
        body, out_type=[SDS((N_SHARD,) + t.shape, t.dtype) for t in shards],
        mesh=plsc.ScalarSubcoreMesh(**_SEQUENCER), scratch_types=[pltpu.SemaphoreType.DMA((n, 3))] * 4,
        compiler_params=pltpu.CompilerParams(collective_id=collective_id), name=name)(*shards, *after)


def _pair_swap(grads):
    n = len(grads)

    def body(*refs):
        src, out, send_sems, recv_sems = refs[:n], refs[n:2 * n], refs[2 * n], refs[2 * n + 1]
        x, y, c, _ = _place()
        copies = [pltpu.make_async_remote_copy(
            src_ref=src[a].at[:, 1 - c], dst_ref=out[a], send_sem=send_sems.at[a], recv_sem=recv_sems.at[a],
            device_id=(x, y, 1 - c), device_id_type=MESH) for a in range(n)]
        for cp in copies:
            cp.start()
        for cp in copies:
            cp.wait()

    return pl.pallas_call(
        body, in_specs=[_ANY] * n, out_specs=[_ANY] * n,
        out_shape=[SDS((N_SHARD,) + t.shape[2:], t.dtype) for t in grads],
        scratch_shapes=[pltpu.SemaphoreType.DMA((n,)), pltpu.SemaphoreType.DMA((n,))], name="pair_swap",
        compiler_params=pltpu.CompilerParams(has_side_effects=True))(*grads)


def _pair_swap_early(grads):
    n = len(grads)

    def body(*refs):
        src, out, send_sems, recv_sems = refs[:n], refs[n:2 * n], refs[2 * n], refs[2 * n + 1]
        x, y, c, _ = _place()
        _handshake([(x, y, 1 - c)])
        copies = [pltpu.make_async_remote_copy(
            src_ref=src[a].at[:, 1 - c], dst_ref=out[a], send_sem=send_sems.at[a], recv_sem=recv_sems.at[a],
            device_id=(x, y, 1 - c), device_id_type=MESH) for a in range(n)]
        for cp in copies:
            cp.start()
        for cp in copies:
            cp.wait()

    return pl.kernel(
        body, out_type=[SDS((N_SHARD,) + t.shape[2:], t.dtype) for t in grads],
        mesh=plsc.ScalarSubcoreMesh(**_SEQUENCER), scratch_types=[pltpu.SemaphoreType.DMA((n,))] * 2,
        compiler_params=pltpu.CompilerParams(collective_id=SWAP_EARLY_ID), name="pair_swap_early")(*grads)


def _scatter_early(parts):
    n = len(parts)

    def body(*refs):
        part, recv, send_sems, recv_sems = refs[:n], refs[n:2 * n], refs[2 * n], refs[2 * n + 1]
        x, y, c, chips = _place()
        _handshake([(*chip, c) for chip in chips])
        me_j = 2 * x + y
        sends = []
        for a in range(n):
            for p, (px, py) in enumerate(chips):
                cp = pltpu.make_async_remote_copy(
                    src_ref=part[a].at[2 * px + py], dst_ref=recv[a].at[me_j], send_sem=send_sems.at[a, p],
                    recv_sem=recv_sems.at[a, p], device_id=(px, py, c), device_id_type=MESH)
                cp.start()
                sends.append(cp)
        for a in range(n):
            for p, (px, py) in enumerate(chips):
                slot = recv[a].at[2 * px + py]
                pltpu.make_async_remote_copy(
                    src_ref=slot, dst_ref=slot, send_sem=send_sems.at[a, p], recv_sem=recv_sems.at[a, p],
                    device_id=(px, py, c), device_id_type=MESH).wait_recv()
        for cp in sends:
            cp.wait_send()

    return pl.kernel(
        body, out_type=[SDS(t.shape, t.dtype) for t in parts],
        mesh=plsc.ScalarSubcoreMesh(**_SEQUENCER), scratch_types=[pltpu.SemaphoreType.DMA((n, 3))] * 2,
        compiler_params=pltpu.CompilerParams(collective_id=SCATTER_EARLY_ID), name="scatter_early")(*parts)


def _pair_sum(grads, other, name):
    _, _, rows, cols = grads.shape
    tr = _row_tile(rows)
    core = lax.axis_index("c").astype(jnp.int32).reshape(1)

    def body(c_ref, g_ref, o_ref, out_ref):
        out_ref[...] = (g_ref[...].astype(f32) + o_ref[...].astype(f32)).astype(bf16)

    grid_spec = pltpu.PrefetchScalarGridSpec(
        num_scalar_prefetch=1, grid=(N_SHARD, rows // tr),
        in_specs=[pl.BlockSpec((None, None, tr, cols), lambda j, i, c_ref: (j, c_ref[0], i, 0)),
                  pl.BlockSpec((None, tr, cols), lambda j, i, c_ref: (j, i, 0))],
        out_specs=pl.BlockSpec((None, tr, cols), lambda j, i, c_ref: (j, i, 0)))
    return pl.pallas_call(
        body, grid_spec=grid_spec, out_shape=SDS((N_SHARD, rows, cols), bf16), name=name,
        compiler_params=_params(("parallel", "parallel"), 10 * tr * cols, 12 * tr * cols))(core, grads, other)


def _scatter_partials(parts, small):
    n = len(parts)

    def body(*refs):
        part, small_ref, recv, small_all_ref = refs[:n], refs[n], refs[n + 1:2 * n + 1], refs[2 * n + 1]
        send_sems, recv_sems, ssend, srecv, local_sem = refs[2 * n + 2:]
        x, y, c, chips = _place()
        flip = lambda a, bit: 1 - a if bit else a
        peers = [(flip(x, k & 4), flip(y, k & 2), flip(c, k & 1)) for k in range(1, 8)]
        _handshake(peers)
        me_j = 2 * x + y
        me_dev = 4 * x + 2 * y + c
        own = pltpu.make_async_copy(small_ref, small_all_ref.at[me_dev], local_sem)
        own.start()
        sends = []
        for a in range(n):
            for p, (px, py) in enumerate(chips):
                cp = pltpu.make_async_remote_copy(
                    src_ref=part[a].at[2 * px + py], dst_ref=recv[a].at[me_j], send_sem=send_sems.at[a, p],
                    recv_sem=recv_sems.at[a, p], device_id=(px, py, c), device_id_type=MESH)
                cp.start()
                sends.append(cp)
        for k, to in enumerate(peers):
            cp = pltpu.make_async_remote_copy(
                src_ref=small_ref, dst_ref=small_all_ref.at[me_dev],
                send_sem=ssend.at[k], recv_sem=srecv.at[k], device_id=to, device_id_type=MESH)
            cp.start()
            sends.append(cp)
        for a in range(n):
            for p, (px, py) in enumerate(chips):
                slot = recv[a].at[2 * px + py]
                pltpu.make_async_remote_copy(
                    src_ref=slot, dst_ref=slot, send_sem=send_sems.at[a, p], recv_sem=recv_sems.at[a, p],
                    device_id=(px, py, c), device_id_type=MESH).wait_recv()
        for k, (px, py, pc) in enumerate(peers):
            slot = small_all_ref.at[4 * px + 2 * py + pc]
            pltpu.make_async_remote_copy(
                src_ref=slot, dst_ref=slot, send_sem=ssend.at[k], recv_sem=srecv.at[k],
                device_id=(px, py, pc), device_id_type=MESH).wait_recv()
        for cp in sends:
            cp.wait_send()
        own.wait()

    return pl.kernel(
        body, out_type=[SDS(t.shape, t.dtype) for t in parts] + [SDS((8, SMALL_ROWS, D), f32)],
        mesh=plsc.ScalarSubcoreMesh(**_SEQUENCER),
        scratch_types=[pltpu.SemaphoreType.DMA((n, 3)), pltpu.SemaphoreType.DMA((n, 3)),
                       pltpu.SemaphoreType.DMA((7,)), pltpu.SemaphoreType.DMA((7,)), pltpu.SemaphoreType.DMA],
        compiler_params=pltpu.CompilerParams(collective_id=SCATTER_LATE_ID), name="scatter_partials")(*parts, small)


def _sum_partials(part, recv, name):
    _, rows, cols = recv.shape
    tr = _row_tile(rows)
    me = (2 * lax.axis_index("x") + lax.axis_index("y")).astype(jnp.int32).reshape(1)

    def body(me_ref, mine, r0, r1, r2, r3, out_ref):
        acc = None
        for j, r in enumerate((r0, r1, r2, r3)):
            term = jnp.where(me_ref[0] == j, mine[...], r[...]).astype(f32)
            acc = term if acc is None else acc + term
        out_ref[...] = acc

    slot = lambda j: pl.BlockSpec((None, tr, cols), lambda i, me_ref: (jnp.where(me_ref[0] == j, j ^ 1, j), i, 0))
    grid_spec = pltpu.PrefetchScalarGridSpec(
        num_scalar_prefetch=1, grid=(rows // tr,),
        in_specs=[pl.BlockSpec((None, tr, cols), lambda i, me_ref: (me_ref[0], i, 0)), slot(0), slot(1), slot(2), slot(3)],
        out_specs=pl.BlockSpec((tr, cols), lambda i, me_ref: (i, 0)))
    return pl.pallas_call(
        body, grid_spec=grid_spec, out_shape=SDS((rows, cols), f32), name=name,
        compiler_params=_params(("parallel",), 14 * tr * cols, 12 * tr * cols))(me, part, recv, recv, recv, recv)


def _sum_small(small_all):
    def body(small_ref, out_ref):
        tot = small_ref[0]
        for k in range(1, 8):
            tot = tot + small_ref[k]
        out_ref[...] = tot

    return pl.pallas_call(
        body, grid=(1,), in_specs=[pl.BlockSpec((8, SMALL_ROWS, D), lambda i: (0, 0, 0))],
        out_specs=pl.BlockSpec((SMALL_ROWS, D), lambda i: (0, 0)), out_shape=SDS((SMALL_ROWS, D), f32),
        name="sum_small", compiler_params=_params(("arbitrary",), 36 * SMALL_ROWS * D))(small_all)


def _swap_halves(halves, name):
    n = len(halves)

    def body(*refs):
        src, out, send_sems, recv_sems = refs[:n], refs[n:2 * n], refs[2 * n], refs[2 * n + 1]
        x, y, c, _ = _place()
        copies = [pltpu.make_async_remote_copy(
            src_ref=src[a], dst_ref=out[a], send_sem=send_sems.at[a], recv_sem=recv_sems.at[a],
            device_id=(x, y, 1 - c), device_id_type=MESH) for a in range(n)]
        for cp in copies:
            cp.start()
        for cp in copies:
            cp.wait()

    return pl.pallas_call(
        body, in_specs=[_ANY] * n, out_specs=[_ANY] * n, out_shape=[SDS(t.shape, f32) for t in halves],
        scratch_shapes=[pltpu.SemaphoreType.DMA((n,))] * 2, name=name,
        compiler_params=pltpu.CompilerParams(has_side_effects=True))(*halves)


def _kernel_layout(name, t):
    t = t[0]
    if name in TRANSPOSED:
        t = jnp.swapaxes(t, 0, 1)
    return _pad_rows(t, SHARD_SHAPE[name][0])


def _harness_layout(name, t):
    if name == "w_in":
        t = t[:IN_SHARD]
    if name in TRANSPOSED:
        t = jnp.swapaxes(t, 0, 1)
    return t[None]


def _pad_rows(t, rows):
    return t if t.shape[0] == rows else jnp.pad(t, ((0, rows - t.shape[0]), (0, 0)))


_QA, _KA, _VA, _QB, _F, _GAB = 0, 768, 1536, 2304, 3840, 3848


def _spans(a, b):
    return [(j, max(a, j * IN_SHARD) - j * IN_SHARD, max(a, j * IN_SHARD) - a,
             min(b, (j + 1) * IN_SHARD) - max(a, j * IN_SHARD))
            for j in range(N_SHARD) if max(a, j * IN_SHARD) < min(b, (j + 1) * IN_SHARD)]


_LANES = pl.BlockSpec((N_SHARD, IN_SHARD_PAD, 128), lambda c: (0, 0, c))


def _split_w_in(shards):
    group = [[(o + g * DIL_W, o + (g + 1) * DIL_W) for o in (_QA, _KA, _VA)] for g in range(3)]
    fox = [[(_QB + k * FOX_W, _QB + (k + 1) * FOX_W)] for k in range(3)]
    wanted = group + fox + [[(_QB, _F)], [(_F, _GAB)], [(_GAB, IN_COLS)]]
    rows = [sum(b - a for a, b in w) for w in wanted]
    rows[7] = 128

    def body(s_ref, *o_refs):
        for o_ref, want in zip(o_refs, wanted):
            at = 0
            for a, b in want:
                for j, src, off, n in _spans(a, b):
                    o_ref[at + off:at + off + n, :] = s_ref[j, src:src + n, :]
                at += b - a
        o_refs[7][N_FOX:, :] = jnp.zeros((128 - N_FOX, 128), bf16)

    return pl.pallas_call(
        body, grid=(D // 128,), in_specs=[_LANES], out_specs=[pl.BlockSpec((r, 128), lambda c: (0, c)) for r in rows],
        out_shape=[SDS((r, D), bf16) for r in rows], name="split_w_in",
        compiler_params=_params(("parallel",), 2 * 128 * (N_SHARD * IN_SHARD_PAD + sum(rows))))(shards)


def _join_w_in(g_a, g_fox, g_f, g_gab):
    parts = [(g_a[k], o, o + DIL_W) for o in (0, DIL_W, 2 * DIL_W) for k in range(3)]
    parts += [(t, 0, FOX_W) for t in g_fox] + [(g_f, 0, N_FOX), (g_gab, 0, 2 * D)]
    arrays = list(g_a) + list(g_fox) + [g_f, g_gab]
    index = {id(t): i for i, t in enumerate(arrays)}

    def body(*refs):
        o_ref = refs[-1]
        o_ref[:, IN_SHARD:, :] = jnp.zeros((N_SHARD, IN_SHARD_PAD - IN_SHARD, 128), bf16)
        at = 0
        for t, lo, hi in parts:
            src_ref = refs[index[id(t)]]
            for j, dst, off, n in _spans(at, at + hi - lo):
                o_ref[j, dst:dst + n, :] = src_ref[lo + off:lo + off + n, :].astype(bf16)
            at += hi - lo

    return pl.pallas_call(
        body, grid=(D // 128,), in_specs=[pl.BlockSpec((t.shape[0], 128), lambda c: (0, c)) for t in arrays],
        out_specs=_LANES, out_shape=SDS((N_SHARD, IN_SHARD_PAD, D), bf16), name="join_w_in",
        compiler_params=_params(("parallel",), 2 * 128 * (N_SHARD * IN_SHARD_PAD + sum(t.shape[0] for t in arrays))),
    )(*arrays)


def _full_weights(gathered):
    full = {n: t.reshape((N_SHARD,) + SHARD_SHAPE[n]) for n, t in gathered.items()}
    out = {}
    if "w_in" in full:
        pieces = _split_w_in(full["w_in"])
        out.update(w_a_t=pieces[0:3], w_fox_t=pieces[3:6], w_vr_t=pieces[6], w_f_t=pieces[7], w_gab_t=pieces[8])
    if "w_out" in full:
        out.update(
            w_a4=full["w_proj_a"],
            w_b4=full["w_proj_b"],
            w_out=full["w_out"].reshape(D, D),
            w_gate_t=full["w_ffn_gate"].reshape(F_FF, D),
            w_up_t=full["w_ffn_up"].reshape(F_FF, D),
            w_down=full["w_ffn_down"].reshape(F_FF, D))
    return out


def _sharded_grads(g):
    full = dict(w_in=_join_w_in(g["w_a_t"], g["w_fox_t"], g["w_f_t"], g["w_gab_t"]), w_proj_a=g["w_a4"],
                w_proj_b=g["w_b4"], w_out=g["w_out"], w_ffn_gate=g["w_gate_t"], w_ffn_up=g["w_up_t"],
                w_ffn_down=g["w_down"])
    return {n: _halved(full[n].reshape((N_SHARD,) + SHARD_SHAPE[n])) for n in W_NAMES}


def _local_step(x, target, wt, b_forget, g_mix_pre, g_mix_post, g_ffn_pre, g_ffn_post, late=None):
    tables = _rope_tables()
    b128 = jnp.pad(b_forget, ((0, 0), (0, 128 - N_FOX)))
    dils = tuple(d for _, d in DIL_GROUPS[1:])

    hs = _norm_fwd([x] + list(_perm_rows([x], dils, "perm_x")), g_mix_pre)
    h1 = hs[0]
    if callable(wt):
        wt = wt(h1)
    qkv = [_rope_fwd(g, _mm([(hs[g], wt["w_a_t"][g])], "nt", f32, tm=1024, tn=QKV_W, name=f"proj_a_{g}"), tables)
           for g in range(3)]
    vr = _mm([(h1, wt["w_vr_t"])], "nt", bf16, tm=1024, tn=VR_W // 2, name="proj_vr")
    gab = _mm([(h1, wt["w_gab_t"])], "nt", f32, tm=512, tn=2 * D, name="proj_gab")
    fz = _mm([(h1, wt["w_f_t"])], "nt", f32, tm=1024, tn=128, name="proj_f")
    dil = [_dil_fwd(g, qkv[g]) for g in range(3)]
    out_a, lse_a = _dil_combine([o for o, _ in dil], [l for _, l in dil])
    f_q, f_k = _forget_fwd(fz, b128)
    out_b, lse_b = _fox_fwd(vr, f_q, f_k)
    if late is not None:
        wt = {**wt, **late(out_b)}
    ya, yb, merged = _merge_fwd(out_a, out_b, wt["w_a4"], wt["w_b4"], gab)
    mix, x2, h3 = _resid_norm_fwd(x, merged, wt["w_out"], g_mix_post, g_ffn_pre)
    g_act, u_act, a_act = _ffn_fwd(h3, wt["w_gate_t"], wt["w_up_t"])
    sq_err, dy, d_ff, dg_ffn_post = _loss_head(x2, a_act, wt["w_down"], g_ffn_post, target)

    grads = {}
    d_g, d_u = _ffn_bwd_act(d_ff, wt["w_down"], g_act, u_act)
    grads["w_down"] = _mm([(a_act, d_ff)], "tn", bf16, tm=FF_TN, tn=512, name="grad_w_down")
    grads["w_gate_t"] = _mm([(d_g, h3)], "tn", bf16, tm=FF_TN, tn=512, name="grad_w_gate")
    grads["w_up_t"] = _mm([(d_u, h3)], "tn", bf16, tm=FF_TN, tn=512, name="grad_w_up")
    dx2, d_mix, dg_ffn_pre, dg_mix_post = _norm_bwd_mid(dy, d_g, d_u, wt["w_gate_t"], wt["w_up_t"], x2, mix,
                                                        g_ffn_pre, g_mix_post)

    grads["w_out"] = _mm([(merged, d_mix)], "tn", bf16, tm=D, tn=D, name="grad_w_out")
    d_ya, d_yb, d_gab = _merge_bwd(d_mix, wt["w_out"], ya, yb, gab)
    grads["w_a4"], grads["w_b4"] = _branch_grads(out_a, out_b, d_ya, d_yb)
    d_out_a, delta_a, d_out_b, delta_b = _branch_bwd(d_ya, d_yb, wt["w_a4"], wt["w_b4"], out_a, out_b)

    perm = _perm_rows([d_out_a, delta_a, lse_a], dils, "perm_dil_bwd")
    aux = [(d_out_a, delta_a, lse_a)] + [tuple(perm[k * len(dils) + i] for k in range(3)) for i in range(len(dils))]
    d_qkv = []
    for g in range(3):
        dq, dk, dv = _dil_bwd(g, qkv[g], *aux[g])
        d_qkv.append(_rope_bwd(g, dq, dk, dv, tables))
    *d_fox, d_f_cols, d_f_rows = _fox_bwd(vr, f_q, f_k, lse_b, d_out_b, delta_b)
    d_z, d_b128 = _forget_bwd(fz, b128, d_f_cols, d_f_rows)

    grads["w_a_t"] = [_mm([(d_qkv[g], hs[g])], "tn", bf16, tm=QKV_W, tn=D, name=f"grad_w_a_{g}") for g in range(3)]
    grads["w_fox_t"] = [_mm([(d_fox[k], h1)], "tn", bf16, tm=FOX_W, tn=D, name=f"grad_w_fox_{k}") for k in range(3)]
    grads["w_gab_t"] = _mm([(d_gab, h1)], "tn", bf16, tm=D, tn=D, name="grad_w_gab")
    grads["w_f_t"] = _mm([(d_z, h1)], "tn", bf16, tm=128, tn=D, name="grad_w_f")
    d_h1_nat = _mm([(d_qkv[0], wt["w_a_t"][0])] + list(zip(d_fox, wt["w_fox_t"]))
                   + [(d_gab, wt["w_gab_t"]), (d_z, wt["w_f_t"])], "nn", f32, tm=512, tn=D, name="proj_in_bwd")
    d_h1_dil = [_mm([(d_qkv[g], wt["w_a_t"][g])], "nn", f32, tm=1024, tn=D, name=f"proj_a_bwd_{g}") for g in (1, 2)]
    d_h1 = _unperm_sum(d_h1_nat, d_h1_dil, dils, "unperm_d_h1")
    grad_x, dg_mix_pre = _norm_bwd_in(dx2, d_h1, x, g_mix_pre)

    small = dict(b_forget=d_b128[:, :N_FOX], norm_mix_pre=dg_mix_pre, norm_mix_post=dg_mix_post,
                 norm_ffn_pre=dg_ffn_pre, norm_ffn_post=dg_ffn_post)
    grads["mid_backward"] = d_qkv[0]
    return sq_err, grad_x, grads, small


NORMS = ("norm_mix_pre", "norm_mix_post", "norm_ffn_pre", "norm_ffn_post")
ORDER = ("w_in", "w_proj_a", "w_proj_b", "w_out", "b_forget", "w_ffn_gate", "w_ffn_up", "w_ffn_down") + NORMS


def kernel(x, w_in, w_proj_a, w_proj_b, w_out, b_forget, w_ffn_gate, w_ffn_up, w_ffn_down, norm_mix_pre, norm_mix_post, norm_ffn_pre, norm_ffn_post, loss_target, m_w_in, m_w_proj_a, m_w_proj_b, m_w_out, m_b_forget, m_w_ffn_gate, m_w_ffn_up, m_w_ffn_down, m_norm_mix_pre, m_norm_mix_post, m_norm_ffn_pre, m_norm_ffn_post, v_w_in, v_w_proj_a, v_w_proj_b, v_w_out, v_b_forget, v_w_ffn_gate, v_w_ffn_up, v_w_ffn_down, v_norm_mix_pre, v_norm_mix_post, v_norm_ffn_pre, v_norm_ffn_post):
    given = dict(w_in=w_in, w_proj_a=w_proj_a, w_proj_b=w_proj_b, w_out=w_out, w_ffn_gate=w_ffn_gate,
                 w_ffn_up=w_ffn_up, w_ffn_down=w_ffn_down)
    given_m = dict(w_in=m_w_in, w_proj_a=m_w_proj_a, w_proj_b=m_w_proj_b, w_out=m_w_out, w_ffn_gate=m_w_ffn_gate,
                   w_ffn_up=m_w_ffn_up, w_ffn_down=m_w_ffn_down)
    given_v = dict(w_in=v_w_in, w_proj_a=v_w_proj_a, w_proj_b=v_w_proj_b, w_out=v_w_out, w_ffn_gate=v_w_ffn_gate,
                   w_ffn_up=v_w_ffn_up, w_ffn_down=v_w_ffn_down)
    w, m, v = ({n: _kernel_layout(n, t[n]) for n in W_NAMES} for t in (given, given_m, given_v))
    small_w = dict(b_forget=b_forget, norm_mix_pre=norm_mix_pre, norm_mix_post=norm_mix_post,
                   norm_ffn_pre=norm_ffn_pre, norm_ffn_post=norm_ffn_post)
    small_m = dict(b_forget=m_b_forget, norm_mix_pre=m_norm_mix_pre, norm_mix_post=m_norm_mix_post,
                   norm_ffn_pre=m_norm_ffn_pre, norm_ffn_post=m_norm_ffn_post)
    small_v = dict(b_forget=v_b_forget, norm_mix_pre=v_norm_mix_pre, norm_mix_post=v_norm_mix_post,
                   norm_ffn_pre=v_norm_ffn_pre, norm_ffn_post=v_norm_ffn_post)

    own = [_halved(w[n].astype(bf16)) for n in W_NAMES]
    chip = 2 * lax.axis_index("x") + lax.axis_index("y")
    exchanged = {"first": _all_gather_async(own[:1], [], "all_gather_first", GATHER_FIRST_ID)}
    fill = lambda ts, mine: [lax.dynamic_update_index_in_dim(t, o, chip, 0) for t, o in zip(ts, mine)]

    def first_weights(ready):
        arrived, _ = lax.optimization_barrier((list(exchanged["first"]), ready))
        exchanged["late"] = _all_gather_async(own[1:], [arrived[0][0, 0, :16, :128]], "all_gather_late", GATHER_LATE_ID)
        return _full_weights(dict(zip(W_NAMES[:1], fill(arrived, own[:1]))))

    def late_weights(ready):
        arrived, _ = lax.optimization_barrier((list(exchanged["late"]), ready))
        return _full_weights(dict(zip(W_NAMES[1:], fill(arrived, own[1:]))))

    sq_err, grad_x, grads, small = _local_step(x[0], loss_target[0], first_weights, b_forget, norm_mix_pre,
                                               norm_mix_post, norm_ffn_pre, norm_ffn_post, late=late_weights)

    g4 = _sharded_grads(grads)
    stack = lambda t, extra: jnp.concatenate(
        [jnp.pad(t["b_forget"], ((0, 0), (0, D - N_FOX)))] + [t[n] for n in NORMS]
        + [jnp.pad(extra, ((0, SMALL_ROWS - LOSS_ROW - 1), (0, D - extra.shape[1])), constant_values=1.0)], axis=0)
    early, _ = lax.optimization_barrier((list(_pair_swap_early([g4[n] for n in W_NAMES[1:]])), grads["mid_backward"]))
    other = list(_pair_swap([g4["w_in"]])) + early
    parts = [_pair_sum(g4[n], o, "pair_sum_" + n) for n, o in zip(W_NAMES, other)]
    recv_early = _scatter_early(parts[1:])
    recv_in, small_all = _scatter_partials(parts[:1], stack(small, sq_err))

    g_shard, delta, new_m, new_v = {}, {}, {}, {}

    def finish(names, parts, recv):
        halves = [_sum_partials(p, r, "sum_partials_" + n) for n, p, r in zip(names, parts, recv)]
        theirs = _swap_halves(halves, "swap_halves_" + names[0])
        for n, mine, other_half in zip(names, halves, theirs):
            g_shard[n], delta[n], new_m[n], new_v[n] = _adamw_halves(w[n], mine, other_half, m[n], v[n], "adamw_" + n)

    recv_early, _ = lax.optimization_barrier((list(recv_early), parts[0]))
    finish(W_NAMES[1:], parts[1:], recv_early)
    (recv_in, small_all), _ = lax.optimization_barrier(((recv_in, small_all), [delta[n] for n in W_NAMES[1:]]))
    finish(W_NAMES[:1], parts[:1], [recv_in])
    small_sum = _sum_small(small_all)
    loss = small_sum[LOSS_ROW, 0] * (0.5 / D)
    ones = jnp.ones((1, 128), f32)
    sd, sm, sv = _adamw(stack(small_w, ones), small_sum, stack(small_m, ones), stack(small_v, ones), "adamw_small")

    outs = [loss, grad_x[None]]
    for big, st in ((g_shard, small_sum), (delta, sd), (new_m, sm), (new_v, sv)):
        t = {n: _harness_layout(n, big[n]) for n in W_NAMES}
        t["b_forget"] = st[0:1, :N_FOX]
        for i, n in enumerate(NORMS):
            t[n] = st[i + 1:i + 2]
        outs += [t[n] for n in ORDER]
    return tuple(outs)
```

```python
import functools
import math

import jax
import jax.numpy as jnp
import numpy as np
from jax import lax
from jax.experimental import pallas as pl
from jax.experimental.pallas import tpu as pltpu
from jax.experimental.pallas import tpu_sc as plsc

f32 = jnp.float32
bf16 = jnp.bfloat16
SDS = jax.ShapeDtypeStruct
MESH = pl.DeviceIdType.MESH

S = 2048
D = 1024
HD = 64
BLK = 128
N_FOX = 8
FOX_W = N_FOX * HD
DIL_GROUPS = ((128, 1), (512, 4), (2048, 16))
SLOTS = 4
DIL_W = SLOTS * HD
QKV_W = 3 * DIL_W
VR_W = 3 * FOX_W
GF_W = 2 * D + 128
F_FF = 2816
ROPE_DIM = 16
ROPE_THETA = 500000.0
EPS = 1e-6
NEG = -1e30
SCALE = 1.0 / math.sqrt(HD)
IN_COLS = 5896
N_SHARD = 4

ADAM_LR, ADAM_B1, ADAM_B2, ADAM_EPS, ADAM_WD, ADAM_STEP = 0.001, 0.9, 0.999, 1e-08, 0.01, 10

VMEM_V7X = 64 * 1024 * 1024
VMEM_PLAN_MAX = VMEM_V7X - 8 * 1024 * 1024

TM = 512
TQ = 256

W_NAMES = ("w_in", "w_proj_a", "w_proj_b", "w_out", "w_ffn_gate", "w_ffn_up", "w_ffn_down")
TRANSPOSED = ("w_in", "w_ffn_gate", "w_ffn_up")
IN_SHARD = IN_COLS // N_SHARD
IN_SHARD_PAD = 1504
SHARD_SHAPE = dict(w_in=(IN_SHARD_PAD, D), w_proj_a=(DIL_W, D // N_SHARD), w_proj_b=(FOX_W, D // N_SHARD),
                   w_out=(D // N_SHARD, D), w_ffn_gate=(F_FF // N_SHARD, D), w_ffn_up=(F_FF // N_SHARD, D),
                   w_ffn_down=(F_FF // N_SHARD, D))
SMALL_ROWS = 8
LOSS_ROW = 5


def _nbytes(shape, dtype):
    return math.prod(shape) * jnp.dtype(dtype).itemsize


def _params(semantics, block_bytes, temp_bytes=0):
    need = 2 * block_bytes + temp_bytes + (2 << 20)
    return pltpu.CompilerParams(dimension_semantics=semantics, vmem_limit_bytes=int(min(need, VMEM_PLAN_MAX)))


def _row(w, tm=TM):
    return pl.BlockSpec((tm, w), lambda i: (i, 0))


def _vec(w):
    return pl.BlockSpec((1, w), lambda i: (0, 0))


def _mm(pairs, dims, out_dtype, *, tm, tn, name, m_inner=False):
    a0, b0 = pairs[0]
    m_dim = a0.shape[1] if dims == "tn" else a0.shape[0]
    n_dim = b0.shape[0] if dims == "nt" else b0.shape[1]
    contract = {"nn": ((1,), (0,)), "nt": ((1,), (1,)), "tn": ((0,), (0,))}[dims]
    n_pairs = len(pairs)
    assert m_dim % tm == 0 and n_dim % tn == 0, (name, m_dim, n_dim, tm, tn)

    def body(*refs):
        o_ref = refs[-1]
        acc = None
        for p in range(n_pairs):
            a = refs[2 * p][...].astype(bf16)
            b = refs[2 * p + 1][...].astype(bf16)
            t = lax.dot_general(a, b, (contract, ((), ())), preferred_element_type=f32)
            acc = t if acc is None else acc + t
        o_ref[...] = acc.astype(o_ref.dtype)

    if m_inner:
        grid = (n_dim // tn, m_dim // tm)
        mi = lambda j, i: i
        ni = lambda j, i: j
    else:
        grid = (m_dim // tm, n_dim // tn)
        mi = lambda i, j: i
        ni = lambda i, j: j
    in_specs, block_bytes, args = [], 0, []
    for a, b in pairs:
        k_dim = a.shape[0] if dims == "tn" else a.shape[1]
        if dims == "tn":
            in_specs.append(pl.BlockSpec((k_dim, tm), lambda *g: (0, mi(*g))))
        else:
            in_specs.append(pl.BlockSpec((tm, k_dim), lambda *g: (mi(*g), 0)))
        if dims == "nt":
            in_specs.append(pl.BlockSpec((tn, k_dim), lambda *g: (ni(*g), 0)))
        else:
            in_specs.append(pl.BlockSpec((k_dim, tn), lambda *g: (0, ni(*g))))
        block_bytes += _nbytes((tm, k_dim), a.dtype) + _nbytes((tn, k_dim), b.dtype)
        args += [a, b]
    block_bytes += _nbytes((tm, tn), out_dtype)
    temp = _nbytes((tm, tn), f32) * 2 + sum(_nbytes((tm, a.shape[0] if dims == "tn" else a.shape[1]), bf16)
                                            + _nbytes((tn, a.shape[0] if dims == "tn" else a.shape[1]), bf16)
                                            for a, _ in pairs)
    return pl.pallas_call(
        body, grid=grid, in_specs=in_specs,
        out_specs=pl.BlockSpec((tm, tn), lambda *g: (mi(*g), ni(*g))),
        out_shape=SDS((m_dim, n_dim), out_dtype), name=name,
        compiler_params=_params(("parallel", "parallel"), block_bytes, temp),
    )(*args)


def _rms(x, g):
    r = lax.rsqrt(jnp.mean(x * x, axis=-1, keepdims=True) + EPS)
    return x * r * g


def _rms_bwd(x, g, dy):
    r = lax.rsqrt(jnp.mean(x * x, axis=-1, keepdims=True) + EPS)
    xh = x * r
    dxh = dy * g
    dx = r * (dxh - xh * jnp.mean(dxh * xh, axis=-1, keepdims=True))
    return dx, jnp.sum(dy * xh, axis=0, keepdims=True)


def _acc_rows(ref, val):
    @pl.when(pl.program_id(0) == 0)
    def _():
        ref[...] = jnp.zeros_like(ref)
    ref[...] += val


def _norm_fwd(xs, g):
    n = len(xs)

    def body(*refs):
        g = refs[n][...]
        for x_ref, h_ref in zip(refs[:n], refs[n + 1:]):
            h_ref[...] = _rms(x_ref[...], g).astype(bf16)

    return pl.pallas_call(
        body, grid=(S // TM,), in_specs=[_row(D)] * n + [_vec(D)], out_specs=[_row(D)] * n,
        out_shape=[SDS((S, D), bf16)] * n, name="norm_mix_pre",
        compiler_params=_params(("parallel",), 6 * n * TM * D, 8 * n * TM * D))(*xs, g)


def _perm_rows(xs, ds, name):
    n = len(xs)

    def body(*refs):
        outs = iter(refs[n:])
        for x_ref in refs[:n]:
            for d in ds:
                o_ref, rows = next(outs), S // d
                for r in range(d):
                    o_ref[r * rows:(r + 1) * rows, :] = x_ref[pl.ds(r, rows, stride=d), :]

    blk = pl.BlockSpec((S, 128), lambda c: (0, c))
    w = xs[0].shape[1]
    return pl.pallas_call(
        body, grid=(w // 128,), in_specs=[blk] * n, out_specs=[blk] * (n * len(ds)),
        out_shape=[SDS((S, w), f32)] * (n * len(ds)), name=name,
        compiler_params=_params(("parallel",), 4 * S * 128 * n * (1 + len(ds))))(*xs)


def _unperm_sum(nat, perms, ds, name):
    n = len(perms)

    def body(*refs):
        a_ref, o_ref, sc = refs[0], refs[n + 1], refs[n + 2]
        acc = a_ref[...]
        for b_ref, d in zip(refs[1:n + 1], ds):
            rows = S // d
            for r in range(d):
                sc[pl.ds(r, rows, stride=d), :] = b_ref[r * rows:(r + 1) * rows, :]
            acc = acc + sc[...]
        o_ref[...] = acc

    blk = pl.BlockSpec((S, 128), lambda c: (0, c))
    w = nat.shape[1]
    return pl.pallas_call(
        body, grid=(w // 128,), in_specs=[blk] * (n + 1), out_specs=blk, out_shape=SDS((S, w), f32),
        scratch_shapes=[pltpu.VMEM((S, 128), f32)], name=name,
        compiler_params=_params(("parallel",), 4 * S * 128 * (n + 2), 8 * S * 128))(nat, *perms)


def _whole(a):
    return pl.BlockSpec(a.shape, lambda i: (0,) * a.ndim)


def _resid_norm_fwd(x, merged, w_out, g_post, g_pre):
    def body(x_ref, mg_ref, w_ref, gp_ref, gn_ref, mix_ref, x2_ref, h_ref):
        mix = jnp.dot(mg_ref[...], w_ref[...], preferred_element_type=f32)
        x2 = x_ref[...] + _rms(mix, gp_ref[...])
        mix_ref[...] = mix
        x2_ref[...] = x2
        h_ref[...] = _rms(x2, gn_ref[...]).astype(bf16)

    return pl.pallas_call(
        body, grid=(S // TM,), in_specs=[_row(D), _row(D), _whole(w_out), _vec(D), _vec(D)], out_specs=[_row(D)] * 3,
        out_shape=[SDS((S, D), f32), SDS((S, D), f32), SDS((S, D), bf16)], name="proj_out_norm",
        compiler_params=_params(("parallel",), 16 * TM * D + 2 * D * D, 16 * TM * D))(x, merged, w_out, g_post, g_pre)


def _loss_head(x2, a_act, w_down, g_post, target):
    def body(x2_ref, a_ref, w_ref, g_ref, t_ref, loss_ref, dy_ref, dff_ref, dg_ref):
        ff = jnp.dot(a_ref[...], w_ref[...], preferred_element_type=f32)
        g = g_ref[...]
        err = x2_ref[...] + _rms(ff, g) - t_ref[...]
        dy = err * (1.0 / D)
        dff, dg = _rms_bwd(ff, g, dy)
        dy_ref[...] = dy
        dff_ref[...] = dff.astype(bf16)
        _acc_rows(dg_ref, dg)
        _acc_rows(loss_ref, jnp.full((1, 128), jnp.sum(err * err), f32))

    return pl.pallas_call(
        body, grid=(S // TM,), in_specs=[_row(D), _row(F_FF), _whole(w_down), _vec(D), _row(D)],
        out_specs=[_vec(128), _row(D), _row(D), _vec(D)],
        out_shape=[SDS((1, 128), f32), SDS((S, D), f32), SDS((S, D), bf16), SDS((1, D), f32)], name="ffn_down_loss",
        compiler_params=_params(("arbitrary",), 14 * TM * D + 2 * TM * F_FF + 2 * F_FF * D, 28 * TM * D),
    )(x2, a_act, w_down, g_post, target)


def _norm_bwd_mid(dy, d_g, d_u, w_gate_t, w_up_t, x2, mix, g_ffn_pre, g_mix_post):
    def body(dy_ref, dgt_ref, dut_ref, wg_ref, wu_ref, x2_ref, mix_ref, g3_ref, g2_ref, dx2_ref, dmix_ref, dg3_ref, dg2_ref):
        dh = jnp.dot(dgt_ref[...], wg_ref[...], preferred_element_type=f32)
        dh += jnp.dot(dut_ref[...], wu_ref[...], preferred_element_type=f32)
        d3, dg3 = _rms_bwd(x2_ref[...], g3_ref[...], dh)
        dx2 = dy_ref[...] + d3
        dmix, dg2 = _rms_bwd(mix_ref[...], g2_ref[...], dx2)
        dx2_ref[...] = dx2
        dmix_ref[...] = dmix.astype(bf16)
        _acc_rows(dg3_ref, dg3)
        _acc_rows(dg2_ref, dg2)

    tm = TM // 2
    row = lambda w: _row(w, tm)
    return pl.pallas_call(
        body, grid=(S // tm,),
        in_specs=[row(D), row(F_FF), row(F_FF), _whole(w_gate_t), _whole(w_up_t), row(D), row(D), _vec(D), _vec(D)],
        out_specs=[row(D), row(D), _vec(D), _vec(D)],
        out_shape=[SDS((S, D), f32), SDS((S, D), bf16), SDS((1, D), f32), SDS((1, D), f32)], name="ffn_bwd_in_norm",
        compiler_params=_params(("arbitrary",), 18 * tm * D + 4 * tm * F_FF + 4 * F_FF * D, 28 * tm * D),
    )(dy, d_g, d_u, w_gate_t, w_up_t, x2, mix, g_ffn_pre, g_mix_post)


def _norm_bwd_in(dx2, dh1, x, g):
    def body(dx2_ref, dh_ref, x_ref, g_ref, gx_ref, dg_ref):
        d1, dg = _rms_bwd(x_ref[...], g_ref[...], dh_ref[...])
        gx_ref[...] = dx2_ref[...] + d1
        _acc_rows(dg_ref, dg)

    return pl.pallas_call(
        body, grid=(S // TM,), in_specs=[_row(D)] * 3 + [_vec(D)], out_specs=[_row(D), _vec(D)],
        out_shape=[SDS((S, D), f32), SDS((1, D), f32)], name="norm_bwd_in",
        compiler_params=_params(("arbitrary",), 16 * TM * D, 16 * TM * D))(dx2, dh1, x, g)


def _rope_tables():
    half = ROPE_DIM // 2
    inv_freq = np.power(np.float32(ROPE_THETA), -np.arange(0, ROPE_DIM, 2, dtype=np.float32) / np.float32(ROPE_DIM))
    row = np.arange(S)
    groups = []
    for _, d in DIL_GROUPS:
        pos = ((row % (S // d)) * d + row // (S // d)).astype(np.float32)
        ang = pos[:, None] * inv_freq[None, :].astype(np.float32)
        cos, sin = np.cos(ang).astype(np.float32), np.sin(ang).astype(np.float32)
        c = np.concatenate([cos, cos, np.ones((S, HD - ROPE_DIM), np.float32)], axis=1)
        s_lo = np.concatenate([-sin, np.zeros((S, HD - half), np.float32)], axis=1)
        s_hi = np.concatenate([np.zeros((S, half), np.float32), sin, np.zeros((S, HD - ROPE_DIM), np.float32)], axis=1)
        groups.append(np.stack([np.concatenate([t, t], axis=1) for t in (c, s_lo, s_hi)]))
    return jnp.asarray(np.stack(groups))


def _rotate(x, c, lo, hi, sign):
    tile = lambda t: jnp.tile(t, (1, DIL_W // 128))
    return (x * tile(c) + pltpu.roll(x, DIL_W - ROPE_DIM // 2, 1) * (tile(lo) * sign)
            + pltpu.roll(x, ROPE_DIM // 2, 1) * (tile(hi) * sign))


def _table_specs(g):
    return [pl.BlockSpec((None, None, TM, 128), lambda i, k=k: (g, k, i, 0)) for k in range(3)]


def _rope_fwd(g, p_qkv, tables):
    def body(x_ref, c_ref, lo_ref, hi_ref, o_ref):
        c, lo, hi = c_ref[...], lo_ref[...], hi_ref[...]
        for part in range(2):
            cols = slice(part * DIL_W, (part + 1) * DIL_W)
            o_ref[:, cols] = _rotate(x_ref[:, cols], c, lo, hi, 1.0).astype(bf16)
        o_ref[:, 2 * DIL_W:] = x_ref[:, 2 * DIL_W:].astype(bf16)

    return pl.pallas_call(
        body, grid=(S // TM,), in_specs=[_row(QKV_W)] + _table_specs(g), out_specs=_row(QKV_W),
        out_shape=SDS((S, QKV_W), bf16), name=f"rope_fwd_{g}",
        compiler_params=_params(("parallel",), 6 * TM * QKV_W + 12 * TM * 128, 24 * TM * QKV_W))(p_qkv, tables, tables, tables)


def _rope_bwd(g, dq, dk, dv, tables):
    def body(dq_ref, dk_ref, dv_ref, c_ref, lo_ref, hi_ref, o_ref):
        c, lo, hi = c_ref[...], lo_ref[...], hi_ref[...]
        o_ref[:, :DIL_W] = _rotate(dq_ref[...], c, lo, hi, -1.0).astype(bf16)
        o_ref[:, DIL_W:2 * DIL_W] = _rotate(dk_ref[...], c, lo, hi, -1.0).astype(bf16)
        o_ref[:, 2 * DIL_W:] = dv_ref[...].astype(bf16)

    return pl.pallas_call(
        body, grid=(S // TM,), in_specs=[_row(DIL_W)] * 3 + _table_specs(g), out_specs=_row(QKV_W),
        out_shape=SDS((S, QKV_W), bf16), name=f"rope_bwd_{g}",
        compiler_params=_params(("parallel",), 6 * TM * QKV_W + 12 * TM * 128, 24 * TM * QKV_W))(dq, dk, dv, tables, tables, tables)


def _nt(a, b):
    return lax.dot_general(a, b, (((1,), (1,)), ((), ())), preferred_element_type=f32)


def _tn(a, b):
    return lax.dot_general(a, b, (((0,), (0,)), ((), ())), preferred_element_type=f32)


STEP_BLOCKS = 4
STEP_ROWS = STEP_BLOCKS * BLK


def _dil_prev(g, b):
    _, d = DIL_GROUPS[g]
    nb = S // d // BLK
    if nb == 1 or (b == 0 and nb <= STEP_BLOCKS):
        return None
    return "in" if b > 0 else "halo"


def _bnt(a, b):
    return lax.dot_general(a, b, (((2,), (2,)), ((0,), (0,))), preferred_element_type=f32)


def _bnn(a, b):
    return lax.dot_general(a, b, (((2,), (1,)), ((0,), (0,))), preferred_element_type=f32)


def _btn(a, b):
    return lax.dot_general(a, b, (((1,), (1,)), ((0,), (0,))), preferred_element_type=f32)


def _on_tail(x, tail, fn):
    if tail == x.shape[0]:
        return fn(x)
    return jnp.concatenate([x[:-tail], fn(x[-tail:])], axis=0)


def _heads(ref, part):
    n = ref.shape[0] // BLK
    return jnp.stack([ref[b * BLK:(b + 1) * BLK, part * DIL_W + h * HD:part * DIL_W + (h + 1) * HD]
                      for b in range(n) for h in range(SLOTS)])


def _dil_operands(g, qkv_ref, halo_ref):
    q, kc, vc = (_heads(qkv_ref, part) for part in range(3))
    qi = lax.broadcasted_iota(jnp.int32, (1, BLK, BLK), 1)
    kj = lax.broadcasted_iota(jnp.int32, (1, BLK, BLK), 2)
    with_prev = [b for b in range(STEP_BLOCKS) if _dil_prev(g, b) is not None]
    tail = SLOTS * len(with_prev)
    if not tail:
        return q, kc, vc, None, None, kj <= qi, None, 0
    assert with_prev == list(range(STEP_BLOCKS - len(with_prev), STEP_BLOCKS))
    inside = SLOTS * sum(_dil_prev(g, b) == "in" for b in with_prev)
    kp, vp, prev = kc[:inside], vc[:inside], jnp.broadcast_to(kj >= qi, (inside, BLK, BLK))
    if inside < tail:
        no_halo = jnp.where(pl.program_id(0) == 0, BLK + 1, 0)
        kp = jnp.concatenate([_heads(halo_ref, 1), kp], axis=0)
        vp = jnp.concatenate([_heads(halo_ref, 2), vp], axis=0)
        prev = jnp.concatenate([jnp.broadcast_to(kj >= qi + no_halo, (SLOTS, BLK, BLK)), prev], axis=0)
    return q, kc, vc, kp, vp, kj <= qi, prev, tail


def _dil_in_specs(g, n_aux):
    step = lambda w: pl.BlockSpec((STEP_ROWS, w), lambda i: (i, 0))
    halo = [pl.BlockSpec((BLK, QKV_W), lambda i: (jnp.maximum(i * STEP_BLOCKS - 1, 0), 0))]
    needs_halo = _dil_prev(g, 0) == "halo"
    return [step(QKV_W)] + (halo if needs_halo else []) + [step(DIL_W)] * n_aux, needs_halo


def _dil_fwd(g, qkv):
    in_specs, needs_halo = _dil_in_specs(g, 0)

    def body(*refs):
        qkv_ref, halo_ref = refs[0], refs[1] if needs_halo else None
        o_ref, lse_ref = refs[-2:]
        q, kc, vc, kp, vp, cur, prev, tail = _dil_operands(g, qkv_ref, halo_ref)
        sc = jnp.where(cur, _bnt(q, kc) * SCALE, NEG)
        m = jnp.max(sc, axis=-1, keepdims=True)
        if tail:
            sp = jnp.where(prev, _bnt(q[-tail:], kp) * SCALE, NEG)
            m = _on_tail(m, tail, lambda t: jnp.maximum(t, jnp.max(sp, axis=-1, keepdims=True)))
            pp = jnp.exp(sp - m[-tail:])
        pc = jnp.exp(sc - m)
        den = jnp.sum(pc, axis=-1, keepdims=True)
        if tail:
            den = _on_tail(den, tail, lambda t: t + jnp.sum(pp, axis=-1, keepdims=True))
        inv = 1.0 / den
        o = _bnn((pc * inv).astype(bf16), vc)
        if tail:
            o = _on_tail(o, tail, lambda t: t + _bnn((pp * inv[-tail:]).astype(bf16), vp))
        lse = m + jnp.log(den)
        for b in range(STEP_BLOCKS):
            for h in range(SLOTS):
                rows, hs = slice(b * BLK, (b + 1) * BLK), slice(h * HD, (h + 1) * HD)
                o_ref[rows, hs] = o[SLOTS * b + h]
                lse_ref[rows, hs] = jnp.broadcast_to(lse[SLOTS * b + h], (BLK, HD))

    out = pl.BlockSpec((STEP_ROWS, DIL_W), lambda i: (i, 0))
    return pl.pallas_call(
        body, grid=(S // STEP_ROWS,), in_specs=in_specs, out_specs=[out, out], out_shape=[SDS((S, DIL_W), f32)] * 2,
        name=f"dil_fwd_{g}", compiler_params=_params(("parallel",), 12 * STEP_ROWS * DIL_W, 2 << 20),
    )(*([qkv] * (2 if needs_halo else 1)))


def _dil_combine(outs, lses):
    def body(o0, o1, o2, l0, l1, l2, out_ref, lse_ref, so1, so2, sl1, sl2):
        for (_, d), src, dst in ((DIL_GROUPS[1], o1, so1), (DIL_GROUPS[2], o2, so2),
                                 (DIL_GROUPS[1], l1, sl1), (DIL_GROUPS[2], l2, sl2)):
            rows = S // d
            for r in range(d):
                dst[pl.ds(r, rows, stride=d), :] = src[r * rows:(r + 1) * rows, :]
        a, b, c = l0[...], sl1[...], sl2[...]
        m = jnp.maximum(jnp.maximum(a, b), c)
        ea, eb, ec = jnp.exp(a - m), jnp.exp(b - m), jnp.exp(c - m)
        z = ea + eb + ec
        inv = 1.0 / z
        out_ref[...] = (ea * inv) * o0[...] + (eb * inv) * so1[...] + (ec * inv) * so2[...]
        lse_ref[...] = m + jnp.log(z)

    blk = pl.BlockSpec((S, 128), lambda c: (0, c))
    return pl.pallas_call(
        body, grid=(DIL_W // 128,), in_specs=[blk] * 6, out_specs=[blk] * 2,
        out_shape=[SDS((S, DIL_W), f32)] * 2, scratch_shapes=[pltpu.VMEM((S, 128), f32)] * 4, name="dil_combine",
        compiler_params=_params(("parallel",), 32 * S * 128, 32 * S * 128))(*outs, *lses)


def _dil_bwd(g, qkv, d_out, delta, lse):
    in_specs, needs_halo = _dil_in_specs(g, 3)

    def body(*refs):
        qkv_ref, halo_ref = refs[0], refs[1] if needs_halo else None
        do_ref, dl_ref, lse_ref, dq_ref, dk_ref, dv_ref = refs[-6:]
        q, kc, vc, kp, vp, cur, prev, tail = _dil_operands(g, qkv_ref, halo_ref)
        tiles = [(slice(b * BLK, (b + 1) * BLK), h) for b in range(STEP_BLOCKS) for h in range(SLOTS)]
        do = jnp.stack([do_ref[rows, h * HD:(h + 1) * HD] for rows, h in tiles]).astype(bf16)
        lse = jnp.stack([lse_ref[rows, h * HD:h * HD + 1] for rows, h in tiles])
        delta = jnp.stack([dl_ref[rows, h * HD:h * HD + 1] for rows, h in tiles])

        def probs(q, k, mask, lse, do, v, delta):
            p = jnp.exp(jnp.where(mask, _bnt(q, k) * SCALE, NEG) - lse)
            ds = p * (_bnt(do, v) - delta) * SCALE
            return p.astype(bf16), ds.astype(bf16)

        p, ds = probs(q, kc, cur, lse, do, vc, delta)
        dq, dk, dv = _bnn(ds, kc), _btn(ds, q), _btn(p, do)
        if tail:
            p, ds = probs(q[-tail:], kp, prev, lse[-tail:], do[-tail:], vp, delta[-tail:])
            dq = _on_tail(dq, tail, lambda t: t + _bnn(ds, kp))
            dk_p, dv_p = _btn(ds, q[-tail:]), _btn(p, do[-tail:])
            inside = tail - SLOTS if needs_halo else tail
            pad = jnp.zeros((len(tiles) - inside, BLK, HD), f32)
            dk = dk + jnp.concatenate([dk_p[tail - inside:], pad], axis=0)
            dv = dv + jnp.concatenate([dv_p[tail - inside:], pad], axis=0)
        first = pl.multiple_of(pl.program_id(0) * STEP_ROWS, STEP_ROWS)
        for t, (rows, h) in enumerate(tiles):
            hs = slice(h * HD, (h + 1) * HD)
            own = pl.ds(pl.multiple_of(first + rows.start, BLK), BLK)
            dq_ref[rows, hs] = dq[t]
            dk_ref[own, hs] = dk[t]
            dv_ref[own, hs] = dv[t]
        if needs_halo:
            before = pl.ds(pl.multiple_of(jnp.maximum(first - BLK, 0), BLK), BLK)
            for h in range(SLOTS):
                hs = slice(h * HD, (h + 1) * HD)
                dk_ref[before, hs] += dk_p[h]
                dv_ref[before, hs] += dv_p[h]

    whole = pl.BlockSpec((S, DIL_W), lambda i: (0, 0))
    return pl.pallas_call(
        body, grid=(S // STEP_ROWS,), in_specs=in_specs,
        out_specs=[pl.BlockSpec((STEP_ROWS, DIL_W), lambda i: (i, 0)), whole, whole],
        out_shape=[SDS((S, DIL_W), f32)] * 3, name=f"dil_bwd_{g}",
        compiler_params=_params(("arbitrary",), 20 * STEP_ROWS * DIL_W + 8 * S * DIL_W, 2 << 20),
    )(*([qkv] * (2 if needs_halo else 1)), d_out, delta, lse)


def _scan_rows(x, reverse):
    row = lax.broadcasted_iota(jnp.int32, x.shape, 0)
    k = 1
    while k < S:
        if reverse:
            x = x + jnp.where(row < S - k, pltpu.roll(x, S - k, 0), 0.0)
        else:
            x = x + jnp.where(row >= k, pltpu.roll(x, k, 0), 0.0)
        k *= 2
    return x


N_PAIR = N_FOX // 2
_PAIR_Q = pl.BlockSpec((None, S, 128), lambda p: (p, 0, 0))
_PAIR_K = pl.BlockSpec((None, 8, S), lambda p: (p, 0, 0))


def _forget_fwd(fz, b128):
    def body(z_ref, b_ref, fq_ref, fk_ref):
        z = z_ref[...] + b_ref[...]
        logf = jnp.minimum(z, 0.0) - jnp.log1p(jnp.exp(-jnp.abs(z)))
        f_cum = _scan_rows(logf, reverse=False)
        f_cum_t = f_cum.T
        fq_ref[...] = jnp.zeros_like(fq_ref)
        fk_ref[...] = jnp.zeros_like(fk_ref)
        for p in range(N_PAIR):
            fq_ref[p, :, 0:2] = f_cum[:, 2 * p:2 * p + 2]
            fk_ref[p, 0:2, :] = f_cum_t[2 * p:2 * p + 2, :]

    return pl.pallas_call(
        body, grid=(1,), in_specs=[pl.BlockSpec((S, 128), lambda i: (0, 0)), _vec(128)],
        out_specs=[pl.BlockSpec((N_PAIR, S, 128), lambda i: (0, 0, 0)), pl.BlockSpec((N_PAIR, 8, S), lambda i: (0, 0, 0))],
        out_shape=[SDS((N_PAIR, S, 128), f32), SDS((N_PAIR, 8, S), f32)], name="forget_fwd",
        compiler_params=_params(("arbitrary",), 24 * S * 128, 24 * S * 128))(fz, b128)


def _forget_bwd(fz, b128, d_f_cols, d_f_rows):
    def body(z_ref, b_ref, dfc_ref, dfr_ref, dz_ref, db_ref, df_sc):
        z = z_ref[...] + b_ref[...]
        df_sc[...] = jnp.zeros_like(df_sc)
        for p in range(N_PAIR):
            df_sc[:, 2 * p:2 * p + 2] = dfr_ref[p, :, 0:2] + dfc_ref[p].T[:, 0:2]
        dz = _scan_rows(df_sc[...], reverse=True) * jax.nn.sigmoid(-z)
        dz_ref[...] = dz
        db_ref[...] = jnp.sum(dz, axis=0, keepdims=True)

    full = pl.BlockSpec((S, 128), lambda i: (0, 0))
    return pl.pallas_call(
        body, grid=(1,),
        in_specs=[full, _vec(128), pl.BlockSpec((N_PAIR, 8, S), lambda i: (0, 0, 0)), pl.BlockSpec((N_PAIR, S, 128), lambda i: (0, 0, 0))],
        out_specs=[full, _vec(128)], out_shape=[SDS((S, 128), f32), SDS((1, 128), f32)],
        scratch_shapes=[pltpu.VMEM((S, 128), f32)], name="forget_bwd",
        compiler_params=_params(("arbitrary",), 32 * S * 128, 24 * S * 128))(fz, b128, d_f_cols, d_f_rows)


def _fox_scores(q_ref, k_ref, fq_ref, fk_ref, qi, hh):
    n = (qi + 1) * TQ
    rows, hs = slice(qi * TQ, n), slice(hh * HD, (hh + 1) * HD)
    q = q_ref[rows, hs] * SCALE
    s = _nt(q, k_ref[0:n, hs]) + (fq_ref[rows, hh:hh + 1] - fk_ref[hh:hh + 1, 0:n])
    below = lax.broadcasted_iota(jnp.int32, (TQ, TQ), 1) <= lax.broadcasted_iota(jnp.int32, (TQ, TQ), 0)
    diag = jnp.where(below, s[:, n - TQ:], NEG)
    return diag if qi == 0 else jnp.concatenate([s[:, :n - TQ], diag], axis=1)


def _pair_cols(first):
    return pl.BlockSpec((S, 128), lambda p: (0, first + p))


def _fox_fwd(vr, fq, fk):
    def body(q_ref, k_ref, v_ref, fq_ref, fk_ref, o_ref, lse_ref):
        lse_ref[...] = jnp.zeros_like(lse_ref)
        for hh in range(2):
            hs = slice(hh * HD, (hh + 1) * HD)
            for qi in range(S // TQ):
                n = (qi + 1) * TQ
                rows = slice(qi * TQ, n)
                s = _fox_scores(q_ref, k_ref, fq_ref, fk_ref, qi, hh)
                m = jnp.max(s, axis=-1, keepdims=True)
                p = jnp.exp(s - m)
                den = jnp.sum(p, axis=-1, keepdims=True)
                o_ref[rows, hs] = jnp.dot((p * (1.0 / den)).astype(bf16), v_ref[0:n, hs], preferred_element_type=f32)
                lse_ref[rows, hh:hh + 1] = m + jnp.log(den)

    return pl.pallas_call(
        body, grid=(N_PAIR,), in_specs=[_pair_cols(0), _pair_cols(N_PAIR), _pair_cols(2 * N_PAIR), _PAIR_Q, _PAIR_K],
        out_specs=[_pair_cols(0), _PAIR_Q], out_shape=[SDS((S, FOX_W), f32), SDS((N_PAIR, S, 128), f32)],
        name="fox_fwd", compiler_params=_params(("parallel",), 12 * S * 128, 16 * TQ * S),
    )(vr, vr, vr, fq, fk)


def _fox_bwd(vr, fq, fk, lse, d_out, delta):
    def body(q_ref, k_ref, v_ref, do_ref, fq_ref, fk_ref, lse_ref, dl_ref, dq_ref, dk_ref, dv_ref, dfc_ref, dfr_ref,
             dk_sc, dv_sc):
        dfc_ref[...] = jnp.zeros_like(dfc_ref)
        dfr_ref[...] = jnp.zeros_like(dfr_ref)
        for hh in range(2):
            hs = slice(hh * HD, (hh + 1) * HD)
            dk_sc[...] = jnp.zeros_like(dk_sc)
            dv_sc[...] = jnp.zeros_like(dv_sc)
            for qi in range(S // TQ):
                n = (qi + 1) * TQ
                rows = slice(qi * TQ, n)
                q, do, k, v = q_ref[rows, hs], do_ref[rows, hs], k_ref[0:n, hs], v_ref[0:n, hs]
                p = jnp.exp(_fox_scores(q_ref, k_ref, fq_ref, fk_ref, qi, hh) - lse_ref[rows, hh:hh + 1])
                ds = p * (_nt(do, v) - dl_ref[rows, hh:hh + 1])
                dsb = ds.astype(bf16)
                dq_ref[rows, hs] = jnp.dot(dsb, k, preferred_element_type=f32) * SCALE
                dk_sc[0:n, :] += _tn(dsb, q) * SCALE
                dv_sc[0:n, :] += _tn(p.astype(bf16), do)
                dfc_ref[hh:hh + 1, 0:n] -= jnp.sum(ds, axis=0, keepdims=True)
                dfr_ref[rows, hh:hh + 1] = jnp.sum(ds, axis=-1, keepdims=True)
            dk_ref[:, hs] = dk_sc[...]
            dv_ref[:, hs] = dv_sc[...]

    cols = [_pair_cols(k * N_PAIR) for k in range(3)]
    return pl.pallas_call(
        body, grid=(N_PAIR,), in_specs=cols + [_pair_cols(0), _PAIR_Q, _PAIR_K, _PAIR_Q, _PAIR_Q],
        out_specs=[_pair_cols(0)] * 3 + [_PAIR_K, _PAIR_Q],
        out_shape=[SDS((S, FOX_W), f32)] * 3 + [SDS((N_PAIR, 8, S), f32), SDS((N_PAIR, S, 128), f32)],
        scratch_shapes=[pltpu.VMEM((S, HD), f32)] * 2, name="fox_bwd",
        compiler_params=_params(("parallel",), 32 * S * 128, 24 * TQ * S),
    )(vr, vr, vr, d_out, fq, fk, lse, delta)


def _merge_fwd(out_a, out_b, w_a, w_b, gf):
    cw = D // N_SHARD

    def body(oa_ref, ob_ref, wa_ref, wb_ref, ga_ref, gb_ref, ya_ref, yb_ref, mg_ref):
        oa, ob = oa_ref[...].astype(bf16), ob_ref[...].astype(bf16)
        for j in range(N_SHARD):
            cols = slice(j * cw, (j + 1) * cw)
            ya = jnp.dot(oa, wa_ref[j], preferred_element_type=f32)
            yb = jnp.dot(ob, wb_ref[j], preferred_element_type=f32)
            ya_ref[:, cols] = ya
            yb_ref[:, cols] = yb
            mg_ref[:, cols] = (jax.nn.sigmoid(ga_ref[:, cols]) * ya + jax.nn.sigmoid(gb_ref[:, cols]) * yb).astype(bf16)

    full = lambda a: pl.BlockSpec(a.shape, lambda i: (0, 0, 0))
    return pl.pallas_call(
        body, grid=(S // TM,),
        in_specs=[_row(DIL_W), _row(FOX_W), full(w_a), full(w_b), _row(D), pl.BlockSpec((TM, D), lambda i: (i, 1))],
        out_specs=[_row(D)] * 3, out_shape=[SDS((S, D), f32), SDS((S, D), f32), SDS((S, D), bf16)], name="merge_fwd",
        compiler_params=_params(("parallel",), 22 * TM * D + 2 * (DIL_W + FOX_W) * D, 16 * TM * D),
    )(out_a, out_b, w_a, w_b, gf, gf)


def _merge_bwd(d_mix, w_out, ya, yb, gf):
    def body(dx_ref, w_ref, ya_ref, yb_ref, ga_ref, gb_ref, dya_ref, dyb_ref, dg_ref):
        dm = _nt(dx_ref[...], w_ref[...])
        sa, sb = jax.nn.sigmoid(ga_ref[...]), jax.nn.sigmoid(gb_ref[...])
        dya_ref[...] = (dm * sa).astype(bf16)
        dyb_ref[...] = (dm * sb).astype(bf16)
        dg_ref[:, :D] = (dm * ya_ref[...] * sa * (1.0 - sa)).astype(bf16)
        dg_ref[:, D:] = (dm * yb_ref[...] * sb * (1.0 - sb)).astype(bf16)

    return pl.pallas_call(
        body, grid=(S // TM,),
        in_specs=[_row(D), _whole(w_out)] + [_row(D)] * 3 + [pl.BlockSpec((TM, D), lambda i: (i, 1))],
        out_specs=[_row(D), _row(D), _row(2 * D)],
        out_shape=[SDS((S, D), bf16), SDS((S, D), bf16), SDS((S, 2 * D), bf16)], name="proj_out_bwd_merge",
        compiler_params=_params(("parallel",), 26 * TM * D + 2 * D * D, 28 * TM * D))(d_mix, w_out, ya, yb, gf, gf)


def _branch_bwd(d_ya, d_yb, w_a, w_b, out_a, out_b):
    cw = D // N_SHARD

    def body(dya_ref, dyb_ref, wa_ref, wb_ref, oa_ref, ob_ref, doa_ref, dla_ref, dob_ref, dlb_ref):
        doa = jnp.zeros((TM, DIL_W), f32)
        dob = jnp.zeros((TM, FOX_W), f32)
        for j in range(N_SHARD):
            cols = slice(j * cw, (j + 1) * cw)
            doa += _nt(dya_ref[:, cols], wa_ref[j])
            dob += _nt(dyb_ref[:, cols], wb_ref[j])
        doa_ref[...] = doa
        dob_ref[...] = dob.astype(bf16)
        prod_a = doa * oa_ref[...]
        for h in range(SLOTS):
            hs = slice(h * HD, (h + 1) * HD)
            dla_ref[:, hs] = jnp.broadcast_to(jnp.sum(prod_a[:, hs], axis=-1, keepdims=True), (TM, HD))
        prod_b = dob * ob_ref[...]
        dlb_ref[...] = jnp.zeros_like(dlb_ref)
        for h in range(N_FOX):
            dlb_ref[h // 2, :, h % 2:h % 2 + 1] = jnp.sum(prod_b[:, h * HD:(h + 1) * HD], axis=-1, keepdims=True)

    full = lambda a: pl.BlockSpec(a.shape, lambda i: (0, 0, 0))
    return pl.pallas_call(
        body, grid=(S // TM,),
        in_specs=[_row(D), _row(D), full(w_a), full(w_b), _row(DIL_W), _row(FOX_W)],
        out_specs=[_row(DIL_W), _row(DIL_W), _row(FOX_W), pl.BlockSpec((N_PAIR, TM, 128), lambda i: (0, i, 0))],
        out_shape=[SDS((S, DIL_W), f32), SDS((S, DIL_W), f32), SDS((S, FOX_W), bf16), SDS((N_PAIR, S, 128), f32)],
        name="branch_bwd", compiler_params=_params(("parallel",), 8 * TM * D + 2 * (DIL_W + FOX_W) * D, 8 * TM * D),
    )(d_ya, d_yb, w_a, w_b, out_a, out_b)


def _branch_grads(out_a, out_b, d_ya, d_yb):
    cw = D // N_SHARD

    def body(oa_ref, ob_ref, dya_ref, dyb_ref, ga_ref, gb_ref):
        ga_ref[...] = _tn(oa_ref[...].astype(bf16), dya_ref[...]).astype(bf16)
        gb_ref[...] = _tn(ob_ref[...].astype(bf16), dyb_ref[...]).astype(bf16)

    whole = lambda w: pl.BlockSpec((S, w), lambda j: (0, 0))
    cols = pl.BlockSpec((S, cw), lambda j: (0, j))
    return pl.pallas_call(
        body, grid=(N_SHARD,), in_specs=[whole(DIL_W), whole(FOX_W), cols, cols],
        out_specs=[pl.BlockSpec((None, DIL_W, cw), lambda j: (j, 0, 0)), pl.BlockSpec((None, FOX_W, cw), lambda j: (j, 0, 0))],
        out_shape=[SDS((N_SHARD, DIL_W, cw), bf16), SDS((N_SHARD, FOX_W, cw), bf16)], name="grad_w_proj_ab",
        compiler_params=_params(("parallel",), 4 * S * (DIL_W + FOX_W) + 4 * S * cw + 4 * (DIL_W + FOX_W) * cw,
                                4 * S * (DIL_W + FOX_W)))(out_a, out_b, d_ya, d_yb)


FF_TN = F_FF // 2
FF_TM = 1024


def _ffn_fwd(h, w_gate_t, w_up_t):
    def body(h_ref, wg_ref, wu_ref, g_ref, u_ref, a_ref):
        hb = h_ref[...]
        g = _nt(hb, wg_ref[...])
        u = _nt(hb, wu_ref[...])
        g_ref[...] = g
        u_ref[...] = u
        a_ref[...] = (g * jax.nn.sigmoid(g) * u).astype(bf16)

    tile = pl.BlockSpec((FF_TM, FF_TN), lambda j, i: (i, j))
    wspec = pl.BlockSpec((FF_TN, D), lambda j, i: (j, 0))
    return pl.pallas_call(
        body, grid=(F_FF // FF_TN, S // FF_TM),
        in_specs=[pl.BlockSpec((FF_TM, D), lambda j, i: (i, 0)), wspec, wspec], out_specs=[tile] * 3,
        out_shape=[SDS((S, F_FF), f32), SDS((S, F_FF), f32), SDS((S, F_FF), bf16)], name="ffn_fwd",
        compiler_params=_params(("parallel", "parallel"), 2 * FF_TM * D + 4 * D * FF_TN + 10 * FF_TM * FF_TN, 16 * FF_TM * FF_TN),
    )(h, w_gate_t, w_up_t)


def _ffn_bwd_act(d_ff, w_down, g_act, u_act):
    def body(d_ref, wd_ref, g_ref, u_ref, dg_ref, du_ref):
        da = _nt(d_ref[...], wd_ref[...])
        g = g_ref[...]
        sg = jax.nn.sigmoid(g)
        du_ref[...] = (da * g * sg).astype(bf16)
        dg_ref[...] = (da * u_ref[...] * sg * (1.0 + g * (1.0 - sg))).astype(bf16)

    tile = pl.BlockSpec((FF_TM, FF_TN), lambda j, i: (i, j))
    return pl.pallas_call(
        body, grid=(F_FF // FF_TN, S // FF_TM),
        in_specs=[pl.BlockSpec((FF_TM, D), lambda j, i: (i, 0)), pl.BlockSpec((FF_TN, D), lambda j, i: (j, 0)), tile, tile],
        out_specs=[tile, tile], out_shape=[SDS((S, F_FF), bf16)] * 2, name="ffn_bwd_act",
        compiler_params=_params(("parallel", "parallel"), 2 * FF_TM * D + 2 * D * FF_TN + 12 * FF_TM * FF_TN, 16 * FF_TM * FF_TN),
    )(d_ff, w_down, g_act, u_act)


def _row_tile(rows):
    return next(t for t in (376, 128, 176, 64, 32, 16, 8) if rows % t == 0)


def _adamw_math(w, g, m, v):
    c1 = 1.0 - ADAM_B1 ** ADAM_STEP
    c2 = 1.0 - ADAM_B2 ** ADAM_STEP
    m_new = ADAM_B1 * m + (1.0 - ADAM_B1) * g
    v_new = ADAM_B2 * v + (1.0 - ADAM_B2) * (g * g)
    return -ADAM_LR * ((m_new / c1) / (jnp.sqrt(v_new / c2) + ADAM_EPS) + ADAM_WD * w), m_new, v_new


def _adamw(w, g, m, v, name):
    rows, cols = w.shape
    tm = _row_tile(rows)

    def body(w_ref, g_ref, m_ref, v_ref, d_ref, nm_ref, nv_ref):
        d_ref[...], nm_ref[...], nv_ref[...] = _adamw_math(w_ref[...], g_ref[...], m_ref[...], v_ref[...])

    spec = pl.BlockSpec((tm, cols), lambda i: (i, 0))
    return pl.pallas_call(
        body, grid=(rows // tm,), in_specs=[spec] * 4, out_specs=[spec] * 3, out_shape=[SDS(w.shape, f32)] * 3,
        name=name, compiler_params=_params(("parallel",), 28 * tm * cols, 16 * tm * cols))(w, g, m, v)


def _adamw_halves(w, g_mine, g_theirs, m, v, name):
    rows, cols = w.shape
    tm = _row_tile(rows // 2)
    per_half = rows // 2 // tm
    core = lax.axis_index("c").astype(jnp.int32).reshape(1)

    def body(c_ref, w_ref, gm_ref, gt_ref, m_ref, v_ref, g_ref, d_ref, nm_ref, nv_ref):
        mine = pl.program_id(0) // per_half == c_ref[0]
        g = jnp.where(mine, gm_ref[...], gt_ref[...])
        g_ref[...] = g
        d_ref[...], nm_ref[...], nv_ref[...] = _adamw_math(w_ref[...], g, m_ref[...], v_ref[...])

    spec = pl.BlockSpec((tm, cols), lambda i, c_ref: (i, 0))
    in_half = lambda i, first: jnp.clip(i - first * per_half, 0, per_half - 1)
    grid_spec = pltpu.PrefetchScalarGridSpec(
        num_scalar_prefetch=1, grid=(rows // tm,),
        in_specs=[spec, pl.BlockSpec((tm, cols), lambda i, c_ref: (in_half(i, c_ref[0]), 0)),
                  pl.BlockSpec((tm, cols), lambda i, c_ref: (in_half(i, 1 - c_ref[0]), 0)), spec, spec],
        out_specs=[spec] * 4)
    return pl.pallas_call(
        body, grid_spec=grid_spec, out_shape=[SDS(w.shape, f32)] * 4, name=name,
        compiler_params=_params(("parallel",), 36 * tm * cols, 16 * tm * cols))(core, w, g_mine, g_theirs, m, v)


_ANY = pl.BlockSpec(memory_space=pl.ANY)


def _place():
    x, y, c = lax.axis_index("x"), lax.axis_index("y"), lax.axis_index("c")
    chips = [(1 - x, y), (x, 1 - y), (1 - x, 1 - y)]
    return x, y, c, chips


def _halved(t):
    return t.reshape(t.shape[:-2] + (2, t.shape[-2] // 2, t.shape[-1]))


def _gather_body(src, out, send_ici, recv_ici, send_d2d, recv_d2d):
    x, y, c, chips = _place()
    sibling = (x, y, 1 - c)
    me_j = 2 * x + y
    sends = []
    for a in range(len(src)):
        for p in range(3):
            cp = pltpu.make_async_remote_copy(
                src_ref=src[a].at[c], dst_ref=out[a].at[me_j, c], send_sem=send_ici.at[a, p],
                recv_sem=recv_ici.at[a, p], device_id=(*chips[p], c), device_id_type=MESH)
            cp.start()
            sends.append(cp)
    for a in range(len(src)):
        for p, (px, py) in enumerate(chips):
            blk = out[a].at[2 * px + py, c]
            pltpu.make_async_remote_copy(
                src_ref=blk, dst_ref=blk, send_sem=send_ici.at[a, p], recv_sem=recv_ici.at[a, p],
                device_id=sibling, device_id_type=MESH).wait_recv()
            fw = pltpu.make_async_remote_copy(
                src_ref=blk, dst_ref=blk, send_sem=send_d2d.at[a, p], recv_sem=recv_d2d.at[a, p],
                device_id=sibling, device_id_type=MESH)
            fw.start()
            sends.append(fw)
    for a in range(len(src)):
        for p, (px, py) in enumerate(chips):
            blk = out[a].at[2 * px + py, 1 - c]
            pltpu.make_async_remote_copy(
                src_ref=blk, dst_ref=blk, send_sem=send_d2d.at[a, p], recv_sem=recv_d2d.at[a, p],
                device_id=sibling, device_id_type=MESH).wait_recv()
    for cp in sends:
        cp.wait_send()


def _handshake(peers):
    barrier = pltpu.get_barrier_semaphore()
    for peer in peers:
        pl.semaphore_signal(barrier, inc=1, device_id=peer, device_id_type=MESH)
    pl.semaphore_wait(barrier, len(peers))


_SEQUENCER = dict(axis_name="sequencer", num_cores=1)
GATHER_LATE_ID, SCATTER_EARLY_ID, SWAP_EARLY_ID, GATHER_FIRST_ID, SCATTER_LATE_ID = 1, 2, 3, 4, 5


def _all_gather_async(shards, after, name, collective_id):
    n, k = len(shards), len(after)

    def body(*refs):
        x, y, c, chips = _place()
        _handshake([(*chip, c) for chip in chips] + [(x, y, 1 - c)])
        _gather_body(refs[:n], refs[n + k:2 * n + k], *refs[2 * n + k:])

    return pl.kernel(
        body, out_type=[SDS((N_SHARD,) + t.shape, t.dtype) for t in shards],
        mesh=plsc.ScalarSubcoreMesh(**_SEQUENCER), scratch_types=[pltpu.SemaphoreType.DMA((n, 3))] * 4,
        compiler_params=pltpu.CompilerParams(collective_id=collective_id), name=name)(*shards, *after)


def _pair_swap(grads):
    n = len(grads)

    def body(*refs):
        src, out, send_sems, recv_sems = refs[:n], refs[n:2 * n], refs[2 * n], refs[2 * n + 1]
        x, y, c, _ = _place()
        copies = [pltpu.make_async_remote_copy(
            src_ref=src[a].at[:, 1 - c], dst_ref=out[a], send_sem=send_sems.at[a], recv_sem=recv_sems.at[a],
            device_id=(x, y, 1 - c), device_id_type=MESH) for a in range(n)]
        for cp in copies:
            cp.start()
        for cp in copies:
            cp.wait()

    return pl.pallas_call(
        body, in_specs=[_ANY] * n, out_specs=[_ANY] * n,
        out_shape=[SDS((N_SHARD,) + t.shape[2:], t.dtype) for t in grads],
        scratch_shapes=[pltpu.SemaphoreType.DMA((n,)), pltpu.SemaphoreType.DMA((n,))], name="pair_swap",
        compiler_params=pltpu.CompilerParams(has_side_effects=True))(*grads)


def _pair_swap_early(grads):
    n = len(grads)

    def body(*refs):
        src, out, send_sems, recv_sems = refs[:n], refs[n:2 * n], refs[2 * n], refs[2 * n + 1]
        x, y, c, _ = _place()
        _handshake([(x, y, 1 - c)])
        copies = [pltpu.make_async_remote_copy(
            src_ref=src[a].at[:, 1 - c], dst_ref=out[a], send_sem=send_sems.at[a], recv_sem=recv_sems.at[a],
            device_id=(x, y, 1 - c), device_id_type=MESH) for a in range(n)]
        for cp in copies:
            cp.start()
        for cp in copies:
            cp.wait()

    return pl.kernel(
        body, out_type=[SDS((N_SHARD,) + t.shape[2:], t.dtype) for t in grads],
        mesh=plsc.ScalarSubcoreMesh(**_SEQUENCER), scratch_types=[pltpu.SemaphoreType.DMA((n,))] * 2,
        compiler_params=pltpu.CompilerParams(collective_id=SWAP_EARLY_ID), name="pair_swap_early")(*grads)


def _scatter_early(parts):
    n = len(parts)

    def body(*refs):
        part, recv, send_sems, recv_sems = refs[:n], refs[n:2 * n], refs[2 * n], refs[2 * n + 1]
        x, y, c, chips = _place()
        _handshake([(*chip, c) for chip in chips])
        me_j = 2 * x + y
        sends = []
        for a in range(n):
            for p, (px, py) in enumerate(chips):
                cp = pltpu.make_async_remote_copy(
                    src_ref=part[a].at[2 * px + py], dst_ref=recv[a].at[me_j], send_sem=send_sems.at[a, p],
                    recv_sem=recv_sems.at[a, p], device_id=(px, py, c), device_id_type=MESH)
                cp.start()
                sends.append(cp)
        for a in range(n):
            for p, (px, py) in enumerate(chips):
                slot = recv[a].at[2 * px + py]
                pltpu.make_async_remote_copy(
                    src_ref=slot, dst_ref=slot, send_sem=send_sems.at[a, p], recv_sem=recv_sems.at[a, p],
                    device_id=(px, py, c), device_id_type=MESH).wait_recv()
        for cp in sends:
            cp.wait_send()

    return pl.kernel(
        body, out_type=[SDS(t.shape, t.dtype) for t in parts],
        mesh=plsc.ScalarSubcoreMesh(**_SEQUENCER), scratch_types=[pltpu.SemaphoreType.DMA((n, 3))] * 2,
        compiler_params=pltpu.CompilerParams(collective_id=SCATTER_EARLY_ID), name="scatter_early")(*parts)


def _pair_sum(grads, other, name):
    _, _, rows, cols = grads.shape
    tr = _row_tile(rows)
    core = lax.axis_index("c").astype(jnp.int32).reshape(1)

    def body(c_ref, g_ref, o_ref, out_ref):
        out_ref[...] = (g_ref[...].astype(f32) + o_ref[...].astype(f32)).astype(bf16)

    grid_spec = pltpu.PrefetchScalarGridSpec(
        num_scalar_prefetch=1, grid=(N_SHARD, rows // tr),
        in_specs=[pl.BlockSpec((None, None, tr, cols), lambda j, i, c_ref: (j, c_ref[0], i, 0)),
                  pl.BlockSpec((None, tr, cols), lambda j, i, c_ref: (j, i, 0))],
        out_specs=pl.BlockSpec((None, tr, cols), lambda j, i, c_ref: (j, i, 0)))
    return pl.pallas_call(
        body, grid_spec=grid_spec, out_shape=SDS((N_SHARD, rows, cols), bf16), name=name,
        compiler_params=_params(("parallel", "parallel"), 10 * tr * cols, 12 * tr * cols))(core, grads, other)


def _scatter_partials(parts, small):
    n = len(parts)

    def body(*refs):
        part, small_ref, recv, small_all_ref = refs[:n], refs[n], refs[n + 1:2 * n + 1], refs[2 * n + 1]
        send_sems, recv_sems, ssend, srecv, local_sem = refs[2 * n + 2:]
        x, y, c, chips = _place()
        flip = lambda a, bit: 1 - a if bit else a
        peers = [(flip(x, k & 4), flip(y, k & 2), flip(c, k & 1)) for k in range(1, 8)]
        _handshake(peers)
        me_j = 2 * x + y
        me_dev = 4 * x + 2 * y + c
        own = pltpu.make_async_copy(small_ref, small_all_ref.at[me_dev], local_sem)
        own.start()
        sends = []
        for a in range(n):
            for p, (px, py) in enumerate(chips):
                cp = pltpu.make_async_remote_copy(
                    src_ref=part[a].at[2 * px + py], dst_ref=recv[a].at[me_j], send_sem=send_sems.at[a, p],
                    recv_sem=recv_sems.at[a, p], device_id=(px, py, c), device_id_type=MESH)
                cp.start()
                sends.append(cp)
        for k, to in enumerate(peers):
            cp = pltpu.make_async_remote_copy(
                src_ref=small_ref, dst_ref=small_all_ref.at[me_dev],
                send_sem=ssend.at[k], recv_sem=srecv.at[k], device_id=to, device_id_type=MESH)
            cp.start()
            sends.append(cp)
        for a in range(n):
            for p, (px, py) in enumerate(chips):
                slot = recv[a].at[2 * px + py]
                pltpu.make_async_remote_copy(
                    src_ref=slot, dst_ref=slot, send_sem=send_sems.at[a, p], recv_sem=recv_sems.at[a, p],
                    device_id=(px, py, c), device_id_type=MESH).wait_recv()
        for k, (px, py, pc) in enumerate(peers):
            slot = small_all_ref.at[4 * px + 2 * py + pc]
            pltpu.make_async_remote_copy(
                src_ref=slot, dst_ref=slot, send_sem=ssend.at[k], recv_sem=srecv.at[k],
                device_id=(px, py, pc), device_id_type=MESH).wait_recv()
        for cp in sends:
            cp.wait_send()
        own.wait()

    return pl.kernel(
        body, out_type=[SDS(t.shape, t.dtype) for t in parts] + [SDS((8, SMALL_ROWS, D), f32)],
        mesh=plsc.ScalarSubcoreMesh(**_SEQUENCER),
        scratch_types=[pltpu.SemaphoreType.DMA((n, 3)), pltpu.SemaphoreType.DMA((n, 3)),
                       pltpu.SemaphoreType.DMA((7,)), pltpu.SemaphoreType.DMA((7,)), pltpu.SemaphoreType.DMA],
        compiler_params=pltpu.CompilerParams(collective_id=SCATTER_LATE_ID), name="scatter_partials")(*parts, small)


def _sum_partials(part, recv, name):
    _, rows, cols = recv.shape
    tr = _row_tile(rows)
    me = (2 * lax.axis_index("x") + lax.axis_index("y")).astype(jnp.int32).reshape(1)

    def body(me_ref, mine, r0, r1, r2, r3, out_ref):
        acc = None
        for j, r in enumerate((r0, r1, r2, r3)):
            term = jnp.where(me_ref[0] == j, mine[...], r[...]).astype(f32)
            acc = term if acc is None else acc + term
        out_ref[...] = acc

    slot = lambda j: pl.BlockSpec((None, tr, cols), lambda i, me_ref: (jnp.where(me_ref[0] == j, j ^ 1, j), i, 0))
    grid_spec = pltpu.PrefetchScalarGridSpec(
        num_scalar_prefetch=1, grid=(rows // tr,),
        in_specs=[pl.BlockSpec((None, tr, cols), lambda i, me_ref: (me_ref[0], i, 0)), slot(0), slot(1), slot(2), slot(3)],
        out_specs=pl.BlockSpec((tr, cols), lambda i, me_ref: (i, 0)))
    return pl.pallas_call(
        body, grid_spec=grid_spec, out_shape=SDS((rows, cols), f32), name=name,
        compiler_params=_params(("parallel",), 14 * tr * cols, 12 * tr * cols))(me, part, recv, recv, recv, recv)


def _sum_small(small_all):
    def body(small_ref, out_ref):
        tot = small_ref[0]
        for k in range(1, 8):
            tot = tot + small_ref[k]
        out_ref[...] = tot

    return pl.pallas_call(
        body, grid=(1,), in_specs=[pl.BlockSpec((8, SMALL_ROWS, D), lambda i: (0, 0, 0))],
        out_specs=pl.BlockSpec((SMALL_ROWS, D), lambda i: (0, 0)), out_shape=SDS((SMALL_ROWS, D), f32),
        name="sum_small", compiler_params=_params(("arbitrary",), 36 * SMALL_ROWS * D))(small_all)


def _swap_halves(halves, name):
    n = len(halves)

    def body(*refs):
        src, out, send_sems, recv_sems = refs[:n], refs[n:2 * n], refs[2 * n], refs[2 * n + 1]
        x, y, c, _ = _place()
        copies = [pltpu.make_async_remote_copy(
            src_ref=src[a], dst_ref=out[a], send_sem=send_sems.at[a], recv_sem=recv_sems.at[a],
            device_id=(x, y, 1 - c), device_id_type=MESH) for a in range(n)]
        for cp in copies:
            cp.start()
        for cp in copies:
            cp.wait()

    return pl.pallas_call(
        body, in_specs=[_ANY] * n, out_specs=[_ANY] * n, out_shape=[SDS(t.shape, f32) for t in halves],
        scratch_shapes=[pltpu.SemaphoreType.DMA((n,))] * 2, name=name,
        compiler_params=pltpu.CompilerParams(has_side_effects=True))(*halves)


def _kernel_layout(name, t):
    t = t[0]
    if name in TRANSPOSED:
        t = jnp.swapaxes(t, 0, 1)
    return _pad_rows(t, SHARD_SHAPE[name][0])


def _harness_layout(name, t):
    if name == "w_in":
        t = t[:IN_SHARD]
    if name in TRANSPOSED:
        t = jnp.swapaxes(t, 0, 1)
    return t[None]


def _pad_rows(t, rows):
    return t if t.shape[0] == rows else jnp.pad(t, ((0, rows - t.shape[0]), (0, 0)))


_QA, _KA, _VA, _QB, _F, _GAB = 0, 768, 1536, 2304, 3840, 3848


def _spans(a, b):
    return [(j, max(a, j * IN_SHARD) - j * IN_SHARD, max(a, j * IN_SHARD) - a,
             min(b, (j + 1) * IN_SHARD) - max(a, j * IN_SHARD))
            for j in range(N_SHARD) if max(a, j * IN_SHARD) < min(b, (j + 1) * IN_SHARD)]


_LANES = pl.BlockSpec((N_SHARD, IN_SHARD_PAD, 128), lambda c: (0, 0, c))


def _split_w_in(shards):
    group = [[(o + g * DIL_W, o + (g + 1) * DIL_W) for o in (_QA, _KA, _VA)] for g in range(3)]
    fox = [[(_QB + k * FOX_W, _QB + (k + 1) * FOX_W)] for k in range(3)]
    wanted = group + fox + [[(_QB, _F)], [(_F, _GAB)], [(_GAB, IN_COLS)]]
    rows = [sum(b - a for a, b in w) for w in wanted]
    rows[7] = 128

    def body(s_ref, *o_refs):
        for o_ref, want in zip(o_refs, wanted):
            at = 0
            for a, b in want:
                for j, src, off, n in _spans(a, b):
                    o_ref[at + off:at + off + n, :] = s_ref[j, src:src + n, :]
                at += b - a
        o_refs[7][N_FOX:, :] = jnp.zeros((128 - N_FOX, 128), bf16)

    return pl.pallas_call(
        body, grid=(D // 128,), in_specs=[_LANES], out_specs=[pl.BlockSpec((r, 128), lambda c: (0, c)) for r in rows],
        out_shape=[SDS((r, D), bf16) for r in rows], name="split_w_in",
        compiler_params=_params(("parallel",), 2 * 128 * (N_SHARD * IN_SHARD_PAD + sum(rows))))(shards)


def _join_w_in(g_a, g_fox, g_f, g_gab):
    parts = [(g_a[k], o, o + DIL_W) for o in (0, DIL_W, 2 * DIL_W) for k in range(3)]
    parts += [(t, 0, FOX_W) for t in g_fox] + [(g_f, 0, N_FOX), (g_gab, 0, 2 * D)]
    arrays = list(g_a) + list(g_fox) + [g_f, g_gab]
    index = {id(t): i for i, t in enumerate(arrays)}

    def body(*refs):
        o_ref = refs[-1]
        o_ref[:, IN_SHARD:, :] = jnp.zeros((N_SHARD, IN_SHARD_PAD - IN_SHARD, 128), bf16)
        at = 0
        for t, lo, hi in parts:
            src_ref = refs[index[id(t)]]
            for j, dst, off, n in _spans(at, at + hi - lo):
                o_ref[j, dst:dst + n, :] = src_ref[lo + off:lo + off + n, :].astype(bf16)
            at += hi - lo

    return pl.pallas_call(
        body, grid=(D // 128,), in_specs=[pl.BlockSpec((t.shape[0], 128), lambda c: (0, c)) for t in arrays],
        out_specs=_LANES, out_shape=SDS((N_SHARD, IN_SHARD_PAD, D), bf16), name="join_w_in",
        compiler_params=_params(("parallel",), 2 * 128 * (N_SHARD * IN_SHARD_PAD + sum(t.shape[0] for t in arrays))),
    )(*arrays)


def _full_weights(gathered):
    full = {n: t.reshape((N_SHARD,) + SHARD_SHAPE[n]) for n, t in gathered.items()}
    out = {}
    if "w_in" in full:
        pieces = _split_w_in(full["w_in"])
        out.update(w_a_t=pieces[0:3], w_fox_t=pieces[3:6], w_vr_t=pieces[6], w_f_t=pieces[7], w_gab_t=pieces[8])
    if "w_out" in full:
        out.update(
            w_a4=full["w_proj_a"],
            w_b4=full["w_proj_b"],
            w_out=full["w_out"].reshape(D, D),
            w_gate_t=full["w_ffn_gate"].reshape(F_FF, D),
            w_up_t=full["w_ffn_up"].reshape(F_FF, D),
            w_down=full["w_ffn_down"].reshape(F_FF, D))
    return out


def _sharded_grads(g):
    full = dict(w_in=_join_w_in(g["w_a_t"], g["w_fox_t"], g["w_f_t"], g["w_gab_t"]), w_proj_a=g["w_a4"],
                w_proj_b=g["w_b4"], w_out=g["w_out"], w_ffn_gate=g["w_gate_t"], w_ffn_up=g["w_up_t"],
                w_ffn_down=g["w_down"])
    return {n: _halved(full[n].reshape((N_SHARD,) + SHARD_SHAPE[n])) for n in W_NAMES}


def _local_step(x, target, wt, b_forget, g_mix_pre, g_mix_post, g_ffn_pre, g_ffn_post, late=None):
    tables = _rope_tables()
    b128 = jnp.pad(b_forget, ((0, 0), (0, 128 - N_FOX)))
    dils = tuple(d for _, d in DIL_GROUPS[1:])

    hs = _norm_fwd([x] + list(_perm_rows([x], dils, "perm_x")), g_mix_pre)
    h1 = hs[0]
    if callable(wt):
        wt = wt(h1)
    qkv = [_rope_fwd(g, _mm([(hs[g], wt["w_a_t"][g])], "nt", f32, tm=1024, tn=QKV_W, name=f"proj_a_{g}"), tables)
           for g in range(3)]
    vr = _mm([(h1, wt["w_vr_t"])], "nt", bf16, tm=1024, tn=VR_W // 2, name="proj_vr")
    gab = _mm([(h1, wt["w_gab_t"])], "nt", f32, tm=512, tn=2 * D, name="proj_gab")
    fz = _mm([(h1, wt["w_f_t"])], "nt", f32, tm=1024, tn=128, name="proj_f")
    dil = [_dil_fwd(g, qkv[g]) for g in range(3)]
    out_a, lse_a = _dil_combine([o for o, _ in dil], [l for _, l in dil])
    f_q, f_k = _forget_fwd(fz, b128)
    out_b, lse_b = _fox_fwd(vr, f_q, f_k)
    if late is not None:
        wt = {**wt, **late(out_b)}
    ya, yb, merged = _merge_fwd(out_a, out_b, wt["w_a4"], wt["w_b4"], gab)
    mix, x2, h3 = _resid_norm_fwd(x, merged, wt["w_out"], g_mix_post, g_ffn_pre)
    g_act, u_act, a_act = _ffn_fwd(h3, wt["w_gate_t"], wt["w_up_t"])
    sq_err, dy, d_ff, dg_ffn_post = _loss_head(x2, a_act, wt["w_down"], g_ffn_post, target)

    grads = {}
    d_g, d_u = _ffn_bwd_act(d_ff, wt["w_down"], g_act, u_act)
    grads["w_down"] = _mm([(a_act, d_ff)], "tn", bf16, tm=FF_TN, tn=D, name="grad_w_down")
    grads["w_gate_t"] = _mm([(d_g, h3)], "tn", bf16, tm=FF_TN, tn=D, name="grad_w_gate")
    grads["w_up_t"] = _mm([(d_u, h3)], "tn", bf16, tm=FF_TN, tn=D, name="grad_w_up")
    dx2, d_mix, dg_ffn_pre, dg_mix_post = _norm_bwd_mid(dy, d_g, d_u, wt["w_gate_t"], wt["w_up_t"], x2, mix,
                                                        g_ffn_pre, g_mix_post)

    grads["w_out"] = _mm([(merged, d_mix)], "tn", bf16, tm=D, tn=D, name="grad_w_out")
    d_ya, d_yb, d_gab = _merge_bwd(d_mix, wt["w_out"], ya, yb, gab)
    grads["w_a4"], grads["w_b4"] = _branch_grads(out_a, out_b, d_ya, d_yb)
    d_out_a, delta_a, d_out_b, delta_b = _branch_bwd(d_ya, d_yb, wt["w_a4"], wt["w_b4"], out_a, out_b)

    perm = _perm_rows([d_out_a, delta_a, lse_a], dils, "perm_dil_bwd")
    aux = [(d_out_a, delta_a, lse_a)] + [tuple(perm[k * len(dils) + i] for k in range(3)) for i in range(len(dils))]
    d_qkv = []
    for g in range(3):
        dq, dk, dv = _dil_bwd(g, qkv[g], *aux[g])
        d_qkv.append(_rope_bwd(g, dq, dk, dv, tables))
    *d_fox, d_f_cols, d_f_rows = _fox_bwd(vr, f_q, f_k, lse_b, d_out_b, delta_b)
    d_z, d_b128 = _forget_bwd(fz, b128, d_f_cols, d_f_rows)

    grads["w_a_t"] = [_mm([(d_qkv[g], hs[g])], "tn", bf16, tm=QKV_W, tn=D, name=f"grad_w_a_{g}") for g in range(3)]
    grads["w_fox_t"] = [_mm([(d_fox[k], h1)], "tn", bf16, tm=FOX_W, tn=D, name=f"grad_w_fox_{k}") for k in range(3)]
    grads["w_gab_t"] = _mm([(d_gab, h1)], "tn", bf16, tm=D, tn=D, name="grad_w_gab")
    grads["w_f_t"] = _mm([(d_z, h1)], "tn", bf16, tm=128, tn=D, name="grad_w_f")
    d_h1_nat = _mm([(d_qkv[0], wt["w_a_t"][0])] + list(zip(d_fox, wt["w_fox_t"]))
                   + [(d_gab, wt["w_gab_t"]), (d_z, wt["w_f_t"])], "nn", f32, tm=512, tn=D, name="proj_in_bwd")
    d_h1_dil = [_mm([(d_qkv[g], wt["w_a_t"][g])], "nn", f32, tm=1024, tn=D, name=f"proj_a_bwd_{g}") for g in (1, 2)]
    d_h1 = _unperm_sum(d_h1_nat, d_h1_dil, dils, "unperm_d_h1")
    grad_x, dg_mix_pre = _norm_bwd_in(dx2, d_h1, x, g_mix_pre)

    small = dict(b_forget=d_b128[:, :N_FOX], norm_mix_pre=dg_mix_pre, norm_mix_post=dg_mix_post,
                 norm_ffn_pre=dg_ffn_pre, norm_ffn_post=dg_ffn_post)
    grads["mid_backward"] = d_qkv[0]
    return sq_err, grad_x, grads, small


NORMS = ("norm_mix_pre", "norm_mix_post", "norm_ffn_pre", "norm_ffn_post")
ORDER = ("w_in", "w_proj_a", "w_proj_b", "w_out", "b_forget", "w_ffn_gate", "w_ffn_up", "w_ffn_down") + NORMS


def kernel(x, w_in, w_proj_a, w_proj_b, w_out, b_forget, w_ffn_gate, w_ffn_up, w_ffn_down, norm_mix_pre, norm_mix_post, norm_ffn_pre, norm_ffn_post, loss_target, m_w_in, m_w_proj_a, m_w_proj_b, m_w_out, m_b_forget, m_w_ffn_gate, m_w_ffn_up, m_w_ffn_down, m_norm_mix_pre, m_norm_mix_post, m_norm_ffn_pre, m_norm_ffn_post, v_w_in, v_w_proj_a, v_w_proj_b, v_w_out, v_b_forget, v_w_ffn_gate, v_w_ffn_up, v_w_ffn_down, v_norm_mix_pre, v_norm_mix_post, v_norm_ffn_pre, v_norm_ffn_post):
    given = dict(w_in=w_in, w_proj_a=w_proj_a, w_proj_b=w_proj_b, w_out=w_out, w_ffn_gate=w_ffn_gate,
                 w_ffn_up=w_ffn_up, w_ffn_down=w_ffn_down)
    given_m = dict(w_in=m_w_in, w_proj_a=m_w_proj_a, w_proj_b=m_w_proj_b, w_out=m_w_out, w_ffn_gate=m_w_ffn_gate,
                   w_ffn_up=m_w_ffn_up, w_ffn_down=m_w_ffn_down)
    given_v = dict(w_in=v_w_in, w_proj_a=v_w_proj_a, w_proj_b=v_w_proj_b, w_out=v_w_out, w_ffn_gate=v_w_ffn_gate,
                   w_ffn_up=v_w_ffn_up, w_ffn_down=v_w_ffn_down)
    w, m, v = ({n: _kernel_layout(n, t[n]) for n in W_NAMES} for t in (given, given_m, given_v))
    small_w = dict(b_forget=b_forget, norm_mix_pre=norm_mix_pre, norm_mix_post=norm_mix_post,
                   norm_ffn_pre=norm_ffn_pre, norm_ffn_post=norm_ffn_post)
    small_m = dict(b_forget=m_b_forget, norm_mix_pre=m_norm_mix_pre, norm_mix_post=m_norm_mix_post,
                   norm_ffn_pre=m_norm_ffn_pre, norm_ffn_post=m_norm_ffn_post)
    small_v = dict(b_forget=v_b_forget, norm_mix_pre=v_norm_mix_pre, norm_mix_post=v_norm_mix_post,
                   norm_ffn_pre=v_norm_ffn_pre, norm_ffn_post=v_norm_ffn_post)

    own = [_halved(w[n].astype(bf16)) for n in W_NAMES]
    chip = 2 * lax.axis_index("x") + lax.axis_index("y")
    exchanged = {"first": _all_gather_async(own[:1], [], "all_gather_first", GATHER_FIRST_ID)}
    fill = lambda ts, mine: [lax.dynamic_update_index_in_dim(t, o, chip, 0) for t, o in zip(ts, mine)]

    def first_weights(ready):
        arrived, _ = lax.optimization_barrier((list(exchanged["first"]), ready))
        exchanged["late"] = _all_gather_async(own[1:], [arrived[0][0, 0, :16, :128]], "all_gather_late", GATHER_LATE_ID)
        return _full_weights(dict(zip(W_NAMES[:1], fill(arrived, own[:1]))))

    def late_weights(ready):
        arrived, _ = lax.optimization_barrier((list(exchanged["late"]), ready))
        return _full_weights(dict(zip(W_NAMES[1:], fill(arrived, own[1:]))))

    sq_err, grad_x, grads, small = _local_step(x[0], loss_target[0], first_weights, b_forget, norm_mix_pre,
                                               norm_mix_post, norm_ffn_pre, norm_ffn_post, late=late_weights)

    g4 = _sharded_grads(grads)
    stack = lambda t, extra: jnp.concatenate(
        [jnp.pad(t["b_forget"], ((0, 0), (0, D - N_FOX)))] + [t[n] for n in NORMS]
        + [jnp.pad(extra, ((0, SMALL_ROWS - LOSS_ROW - 1), (0, D - extra.shape[1])), constant_values=1.0)], axis=0)
    early, _ = lax.optimization_barrier((list(_pair_swap_early([g4[n] for n in W_NAMES[1:]])), grads["mid_backward"]))
    other = list(_pair_swap([g4["w_in"]])) + early
    parts = [_pair_sum(g4[n], o, "pair_sum_" + n) for n, o in zip(W_NAMES, other)]
    recv_early = _scatter_early(parts[1:])
    recv_in, small_all = _scatter_partials(parts[:1], stack(small, sq_err))

    g_shard, delta, new_m, new_v = {}, {}, {}, {}

    def finish(names, parts, recv):
        halves = [_sum_partials(p, r, "sum_partials_" + n) for n, p, r in zip(names, parts, recv)]
        theirs = _swap_halves(halves, "swap_halves_" + names[0])
        for n, mine, other_half in zip(names, halves, theirs):
            g_shard[n], delta[n], new_m[n], new_v[n] = _adamw_halves(w[n], mine, other_half, m[n], v[n], "adamw_" + n)

    recv_early, _ = lax.optimization_barrier((list(recv_early), parts[0]))
    finish(W_NAMES[1:], parts[1:], recv_early)
    (recv_in, small_all), _ = lax.optimization_barrier(((recv_in, small_all), [delta[n] for n in W_NAMES[1:]]))
    finish(W_NAMES[:1], parts[:1], [recv_in])
    small_sum = _sum_small(small_all)
    loss = small_sum[LOSS_ROW, 0] * (0.5 / D)
    ones = jnp.ones((1, 128), f32)
    sd, sm, sv = _adamw(stack(small_w, ones), small_sum, stack(small_m, ones), stack(small_v, ones), "adamw_small")

    outs = [loss, grad_x[None]]
    for big, st in ((g_shard, small_sum), (delta, sd), (new_m, sm), (new_v, sv)):
        t = {n: _harness_layout(n, big[n]) for n in W_NAMES}
        t["b_forget"] = st[0:1, :N_FOX]
        for i, n in enumerate(NORMS):
            t[n] = st[i + 1:i + 2]
        outs += [t[n] for n in ORDER]
    return tuple(outs)
```

```python
import functools
import math

import jax
import jax.numpy as jnp
import numpy as np
from jax import lax
from jax.experimental import pallas as pl
from jax.experimental.pallas import tpu as pltpu
from jax.experimental.pallas import tpu_sc as plsc

f32 = jnp.float32
bf16 = jnp.bfloat16
SDS = jax.ShapeDtypeStruct
MESH = pl.DeviceIdType.MESH

S = 2048
D = 1024
HD = 64
BLK = 128
N_FOX = 8
FOX_W = N_FOX * HD
DIL_GROUPS = ((128, 1), (512, 4), (2048, 16))
SLOTS = 4
DIL_W = SLOTS * HD
QKV_W = 3 * DIL_W
VR_W = 3 * FOX_W
GF_W = 2 * D + 128
F_FF = 2816
ROPE_DIM = 16
ROPE_THETA = 500000.0
EPS = 1e-6
NEG = -1e30
SCALE = 1.0 / math.sqrt(HD)
IN_COLS = 5896
N_SHARD = 4

ADAM_LR, ADAM_B1, ADAM_B2, ADAM_EPS, ADAM_WD, ADAM_STEP = 0.001, 0.9, 0.999, 1e-08, 0.01, 10

VMEM_V7X = 64 * 1024 * 1024
VMEM_PLAN_MAX = VMEM_V7X - 8 * 1024 * 1024

TM = 512
TQ = 256

W_NAMES = ("w_in", "w_proj_a", "w_proj_b", "w_out", "w_ffn_gate", "w_ffn_up", "w_ffn_down")
TRANSPOSED = ("w_in", "w_ffn_gate", "w_ffn_up")
IN_SHARD = IN_COLS // N_SHARD
IN_SHARD_PAD = 1504
SHARD_SHAPE = dict(w_in=(IN_SHARD_PAD, D), w_proj_a=(DIL_W, D // N_SHARD), w_proj_b=(FOX_W, D // N_SHARD),
                   w_out=(D // N_SHARD, D), w_ffn_gate=(F_FF // N_SHARD, D), w_ffn_up=(F_FF // N_SHARD, D),
                   w_ffn_down=(F_FF // N_SHARD, D))
SMALL_ROWS = 8
LOSS_ROW = 5


def _nbytes(shape, dtype):
    return math.prod(shape) * jnp.dtype(dtype).itemsize


def _params(semantics, block_bytes, temp_bytes=0):
    need = 2 * block_bytes + temp_bytes + (2 << 20)
    return pltpu.CompilerParams(dimension_semantics=semantics, vmem_limit_bytes=int(min(need, VMEM_PLAN_MAX)))


def _row(w, tm=TM):
    return pl.BlockSpec((tm, w), lambda i: (i, 0))


def _vec(w):
    return pl.BlockSpec((1, w), lambda i: (0, 0))


def _mm(pairs, dims, out_dtype, *, tm, tn, name, m_inner=False):
    a0, b0 = pairs[0]
    m_dim = a0.shape[1] if dims == "tn" else a0.shape[0]
    n_dim = b0.shape[0] if dims == "nt" else b0.shape[1]
    contract = {"nn": ((1,), (0,)), "nt": ((1,), (1,)), "tn": ((0,), (0,))}[dims]
    n_pairs = len(pairs)
    assert m_dim % tm == 0 and n_dim % tn == 0, (name, m_dim, n_dim, tm, tn)

    def body(*refs):
        o_ref = refs[-1]
        acc = None
        for p in range(n_pairs):
            a = refs[2 * p][...].astype(bf16)
            b = refs[2 * p + 1][...].astype(bf16)
            t = lax.dot_general(a, b, (contract, ((), ())), preferred_element_type=f32)
            acc = t if acc is None else acc + t
        o_ref[...] = acc.astype(o_ref.dtype)

    if m_inner:
        grid = (n_dim // tn, m_dim // tm)
        mi = lambda j, i: i
        ni = lambda j, i: j
    else:
        grid = (m_dim // tm, n_dim // tn)
        mi = lambda i, j: i
        ni = lambda i, j: j
    in_specs, block_bytes, args = [], 0, []
    for a, b in pairs:
        k_dim = a.shape[0] if dims == "tn" else a.shape[1]
        if dims == "tn":
            in_specs.append(pl.BlockSpec((k_dim, tm), lambda *g: (0, mi(*g))))
        else:
            in_specs.append(pl.BlockSpec((tm, k_dim), lambda *g: (mi(*g), 0)))
        if dims == "nt":
            in_specs.append(pl.BlockSpec((tn, k_dim), lambda *g: (ni(*g), 0)))
        else:
            in_specs.append(pl.BlockSpec((k_dim, tn), lambda *g: (0, ni(*g))))
        block_bytes += _nbytes((tm, k_dim), a.dtype) + _nbytes((tn, k_dim), b.dtype)
        args += [a, b]
    block_bytes += _nbytes((tm, tn), out_dtype)
    temp = _nbytes((tm, tn), f32) * 2 + sum(_nbytes((tm, a.shape[0] if dims == "tn" else a.shape[1]), bf16)
                                            + _nbytes((tn, a.shape[0] if dims == "tn" else a.shape[1]), bf16)
                                            for a, _ in pairs)
    return pl.pallas_call(
        body, grid=grid, in_specs=in_specs,
        out_specs=pl.BlockSpec((tm, tn), lambda *g: (mi(*g), ni(*g))),
        out_shape=SDS((m_dim, n_dim), out_dtype), name=name,
        compiler_params=_params(("parallel", "parallel"), block_bytes, temp),
    )(*args)


def _rms(x, g):
    r = lax.rsqrt(jnp.mean(x * x, axis=-1, keepdims=True) + EPS)
    return x * r * g


def _rms_bwd(x, g, dy):
    r = lax.rsqrt(jnp.mean(x * x, axis=-1, keepdims=True) + EPS)
    xh = x * r
    dxh = dy * g
    dx = r * (dxh - xh * jnp.mean(dxh * xh, axis=-1, keepdims=True))
    return dx, jnp.sum(dy * xh, axis=0, keepdims=True)


def _acc_rows(ref, val):
    @pl.when(pl.program_id(0) == 0)
    def _():
        ref[...] = jnp.zeros_like(ref)
    ref[...] += val


def _norm_fwd(xs, g):
    n = len(xs)

    def body(*refs):
        g = refs[n][...]
        for x_ref, h_ref in zip(refs[:n], refs[n + 1:]):
            h_ref[...] = _rms(x_ref[...], g).astype(bf16)

    return pl.pallas_call(
        body, grid=(S // TM,), in_specs=[_row(D)] * n + [_vec(D)], out_specs=[_row(D)] * n,
        out_shape=[SDS((S, D), bf16)] * n, name="norm_mix_pre",
        compiler_params=_params(("parallel",), 6 * n * TM * D, 8 * n * TM * D))(*xs, g)


def _perm_rows(xs, ds, name):
    n = len(xs)

    def body(*refs):
        outs = iter(refs[n:])
        for x_ref in refs[:n]:
            for d in ds:
                o_ref, rows = next(outs), S // d
                for r in range(d):
                    o_ref[r * rows:(r + 1) * rows, :] = x_ref[pl.ds(r, rows, stride=d), :]

    blk = pl.BlockSpec((S, 128), lambda c: (0, c))
    w = xs[0].shape[1]
    return pl.pallas_call(
        body, grid=(w // 128,), in_specs=[blk] * n, out_specs=[blk] * (n * len(ds)),
        out_shape=[SDS((S, w), f32)] * (n * len(ds)), name=name,
        compiler_params=_params(("parallel",), 4 * S * 128 * n * (1 + len(ds))))(*xs)


def _unperm_sum(nat, perms, ds, name):
    n = len(perms)

    def body(*refs):
        a_ref, o_ref, sc = refs[0], refs[n + 1], refs[n + 2]
        acc = a_ref[...]
        for b_ref, d in zip(refs[1:n + 1], ds):
            rows = S // d
            for r in range(d):
                sc[pl.ds(r, rows, stride=d), :] = b_ref[r * rows:(r + 1) * rows, :]
            acc = acc + sc[...]
        o_ref[...] = acc

    blk = pl.BlockSpec((S, 128), lambda c: (0, c))
    w = nat.shape[1]
    return pl.pallas_call(
        body, grid=(w // 128,), in_specs=[blk] * (n + 1), out_specs=blk, out_shape=SDS((S, w), f32),
        scratch_shapes=[pltpu.VMEM((S, 128), f32)], name=name,
        compiler_params=_params(("parallel",), 4 * S * 128 * (n + 2), 8 * S * 128))(nat, *perms)


def _whole(a):
    return pl.BlockSpec(a.shape, lambda i: (0,) * a.ndim)


def _resid_norm_fwd(x, merged, w_out, g_post, g_pre):
    def body(x_ref, mg_ref, w_ref, gp_ref, gn_ref, mix_ref, x2_ref, h_ref):
        mix = jnp.dot(mg_ref[...], w_ref[...], preferred_element_type=f32)
        x2 = x_ref[...] + _rms(mix, gp_ref[...])
        mix_ref[...] = mix
        x2_ref[...] = x2
        h_ref[...] = _rms(x2, gn_ref[...]).astype(bf16)

    return pl.pallas_call(
        body, grid=(S // TM,), in_specs=[_row(D), _row(D), _whole(w_out), _vec(D), _vec(D)], out_specs=[_row(D)] * 3,
        out_shape=[SDS((S, D), f32), SDS((S, D), f32), SDS((S, D), bf16)], name="proj_out_norm",
        compiler_params=_params(("parallel",), 16 * TM * D + 2 * D * D, 16 * TM * D))(x, merged, w_out, g_post, g_pre)


def _loss_head(x2, a_act, w_down, g_post, target):
    def body(x2_ref, a_ref, w_ref, g_ref, t_ref, loss_ref, dy_ref, dff_ref, dg_ref):
        ff = jnp.dot(a_ref[...], w_ref[...], preferred_element_type=f32)
        g = g_ref[...]
        err = x2_ref[...] + _rms(ff, g) - t_ref[...]
        dy = err * (1.0 / D)
        dff, dg = _rms_bwd(ff, g, dy)
        dy_ref[...] = dy
        dff_ref[...] = dff.astype(bf16)
        _acc_rows(dg_ref, dg)
        _acc_rows(loss_ref, jnp.full((1, 128), jnp.sum(err * err), f32))

    return pl.pallas_call(
        body, grid=(S // TM,), in_specs=[_row(D), _row(F_FF), _whole(w_down), _vec(D), _row(D)],
        out_specs=[_vec(128), _row(D), _row(D), _vec(D)],
        out_shape=[SDS((1, 128), f32), SDS((S, D), f32), SDS((S, D), bf16), SDS((1, D), f32)], name="ffn_down_loss",
        compiler_params=_params(("arbitrary",), 14 * TM * D + 2 * TM * F_FF + 2 * F_FF * D, 28 * TM * D),
    )(x2, a_act, w_down, g_post, target)


def _norm_bwd_mid(dy, d_g, d_u, w_gate_t, w_up_t, x2, mix, g_ffn_pre, g_mix_post):
    def body(dy_ref, dgt_ref, dut_ref, wg_ref, wu_ref, x2_ref, mix_ref, g3_ref, g2_ref, dx2_ref, dmix_ref, dg3_ref, dg2_ref):
        dh = jnp.dot(dgt_ref[...], wg_ref[...], preferred_element_type=f32)
        dh += jnp.dot(dut_ref[...], wu_ref[...], preferred_element_type=f32)
        d3, dg3 = _rms_bwd(x2_ref[...], g3_ref[...], dh)
        dx2 = dy_ref[...] + d3
        dmix, dg2 = _rms_bwd(mix_ref[...], g2_ref[...], dx2)
        dx2_ref[...] = dx2
        dmix_ref[...] = dmix.astype(bf16)
        _acc_rows(dg3_ref, dg3)
        _acc_rows(dg2_ref, dg2)

    tm = TM // 2
    row = lambda w: _row(w, tm)
    return pl.pallas_call(
        body, grid=(S // tm,),
        in_specs=[row(D), row(F_FF), row(F_FF), _whole(w_gate_t), _whole(w_up_t), row(D), row(D), _vec(D), _vec(D)],
        out_specs=[row(D), row(D), _vec(D), _vec(D)],
        out_shape=[SDS((S, D), f32), SDS((S, D), bf16), SDS((1, D), f32), SDS((1, D), f32)], name="ffn_bwd_in_norm",
        compiler_params=_params(("arbitrary",), 18 * tm * D + 4 * tm * F_FF + 4 * F_FF * D, 28 * tm * D),
    )(dy, d_g, d_u, w_gate_t, w_up_t, x2, mix, g_ffn_pre, g_mix_post)


def _norm_bwd_in(dx2, dh1, x, g):
    def body(dx2_ref, dh_ref, x_ref, g_ref, gx_ref, dg_ref):
        d1, dg = _rms_bwd(x_ref[...], g_ref[...], dh_ref[...])
        gx_ref[...] = dx2_ref[...] + d1
        _acc_rows(dg_ref, dg)

    return pl.pallas_call(
        body, grid=(S // TM,), in_specs=[_row(D)] * 3 + [_vec(D)], out_specs=[_row(D), _vec(D)],
        out_shape=[SDS((S, D), f32), SDS((1, D), f32)], name="norm_bwd_in",
        compiler_params=_params(("arbitrary",), 16 * TM * D, 16 * TM * D))(dx2, dh1, x, g)


def _rope_tables():
    half = ROPE_DIM // 2
    inv_freq = np.power(np.float32(ROPE_THETA), -np.arange(0, ROPE_DIM, 2, dtype=np.float32) / np.float32(ROPE_DIM))
    row = np.arange(S)
    groups = []
    for _, d in DIL_GROUPS:
        pos = ((row % (S // d)) * d + row // (S // d)).astype(np.float32)
        ang = pos[:, None] * inv_freq[None, :].astype(np.float32)
        cos, sin = np.cos(ang).astype(np.float32), np.sin(ang).astype(np.float32)
        c = np.concatenate([cos, cos, np.ones((S, HD - ROPE_DIM), np.float32)], axis=1)
        s_lo = np.concatenate([-sin, np.zeros((S, HD - half), np.float32)], axis=1)
        s_hi = np.concatenate([np.zeros((S, half), np.float32), sin, np.zeros((S, HD - ROPE_DIM), np.float32)], axis=1)
        groups.append(np.stack([np.concatenate([t, t], axis=1) for t in (c, s_lo, s_hi)]))
    return jnp.asarray(np.stack(groups))


def _rotate(x, c, lo, hi, sign):
    tile = lambda t: jnp.tile(t, (1, DIL_W // 128))
    return (x * tile(c) + pltpu.roll(x, DIL_W - ROPE_DIM // 2, 1) * (tile(lo) * sign)
            + pltpu.roll(x, ROPE_DIM // 2, 1) * (tile(hi) * sign))


def _table_specs(g):
    return [pl.BlockSpec((None, None, TM, 128), lambda i, k=k: (g, k, i, 0)) for k in range(3)]


def _rope_fwd(g, p_qkv, tables):
    def body(x_ref, c_ref, lo_ref, hi_ref, o_ref):
        c, lo, hi = c_ref[...], lo_ref[...], hi_ref[...]
        for part in range(2):
            cols = slice(part * DIL_W, (part + 1) * DIL_W)
            o_ref[:, cols] = _rotate(x_ref[:, cols], c, lo, hi, 1.0).astype(bf16)
        o_ref[:, 2 * DIL_W:] = x_ref[:, 2 * DIL_W:].astype(bf16)

    return pl.pallas_call(
        body, grid=(S // TM,), in_specs=[_row(QKV_W)] + _table_specs(g), out_specs=_row(QKV_W),
        out_shape=SDS((S, QKV_W), bf16), name=f"rope_fwd_{g}",
        compiler_params=_params(("parallel",), 6 * TM * QKV_W + 12 * TM * 128, 24 * TM * QKV_W))(p_qkv, tables, tables, tables)


def _rope_bwd(g, dq, dk, dv, tables):
    def body(dq_ref, dk_ref, dv_ref, c_ref, lo_ref, hi_ref, o_ref):
        c, lo, hi = c_ref[...], lo_ref[...], hi_ref[...]
        o_ref[:, :DIL_W] = _rotate(dq_ref[...], c, lo, hi, -1.0).astype(bf16)
        o_ref[:, DIL_W:2 * DIL_W] = _rotate(dk_ref[...], c, lo, hi, -1.0).astype(bf16)
        o_ref[:, 2 * DIL_W:] = dv_ref[...].astype(bf16)

    return pl.pallas_call(
        body, grid=(S // TM,), in_specs=[_row(DIL_W)] * 3 + _table_specs(g), out_specs=_row(QKV_W),
        out_shape=SDS((S, QKV_W), bf16), name=f"rope_bwd_{g}",
        compiler_params=_params(("parallel",), 6 * TM * QKV_W + 12 * TM * 128, 24 * TM * QKV_W))(dq, dk, dv, tables, tables, tables)


def _nt(a, b):
    return lax.dot_general(a, b, (((1,), (1,)), ((), ())), preferred_element_type=f32)


def _tn(a, b):
    return lax.dot_general(a, b, (((0,), (0,)), ((), ())), preferred_element_type=f32)


STEP_BLOCKS = 4
STEP_ROWS = STEP_BLOCKS * BLK


def _dil_prev(g, b):
    _, d = DIL_GROUPS[g]
    nb = S // d // BLK
    if nb == 1 or (b == 0 and nb <= STEP_BLOCKS):
        return None
    return "in" if b > 0 else "halo"


def _bnt(a, b):
    return lax.dot_general(a, b, (((2,), (2,)), ((0,), (0,))), preferred_element_type=f32)


def _bnn(a, b):
    return lax.dot_general(a, b, (((2,), (1,)), ((0,), (0,))), preferred_element_type=f32)


def _btn(a, b):
    return lax.dot_general(a, b, (((1,), (1,)), ((0,), (0,))), preferred_element_type=f32)


def _on_tail(x, tail, fn):
    if tail == x.shape[0]:
        return fn(x)
    return jnp.concatenate([x[:-tail], fn(x[-tail:])], axis=0)


def _heads(ref, part):
    n = ref.shape[0] // BLK
    return jnp.stack([ref[b * BLK:(b + 1) * BLK, part * DIL_W + h * HD:part * DIL_W + (h + 1) * HD]
                      for b in range(n) for h in range(SLOTS)])


def _dil_operands(g, qkv_ref, halo_ref):
    q, kc, vc = (_heads(qkv_ref, part) for part in range(3))
    qi = lax.broadcasted_iota(jnp.int32, (1, BLK, BLK), 1)
    kj = lax.broadcasted_iota(jnp.int32, (1, BLK, BLK), 2)
    with_prev = [b for b in range(STEP_BLOCKS) if _dil_prev(g, b) is not None]
    tail = SLOTS * len(with_prev)
    if not tail:
        return q, kc, vc, None, None, kj <= qi, None, 0
    assert with_prev == list(range(STEP_BLOCKS - len(with_prev), STEP_BLOCKS))
    inside = SLOTS * sum(_dil_prev(g, b) == "in" for b in with_prev)
    kp, vp, prev = kc[:inside], vc[:inside], jnp.broadcast_to(kj >= qi, (inside, BLK, BLK))
    if inside < tail:
        no_halo = jnp.where(pl.program_id(0) == 0, BLK + 1, 0)
        kp = jnp.concatenate([_heads(halo_ref, 1), kp], axis=0)
        vp = jnp.concatenate([_heads(halo_ref, 2), vp], axis=0)
        prev = jnp.concatenate([jnp.broadcast_to(kj >= qi + no_halo, (SLOTS, BLK, BLK)), prev], axis=0)
    return q, kc, vc, kp, vp, kj <= qi, prev, tail


def _dil_in_specs(g, n_aux):
    step = lambda w: pl.BlockSpec((STEP_ROWS, w), lambda i: (i, 0))
    halo = [pl.BlockSpec((BLK, QKV_W), lambda i: (jnp.maximum(i * STEP_BLOCKS - 1, 0), 0))]
    needs_halo = _dil_prev(g, 0) == "halo"
    return [step(QKV_W)] + (halo if needs_halo else []) + [step(DIL_W)] * n_aux, needs_halo


def _dil_fwd(g, qkv):
    in_specs, needs_halo = _dil_in_specs(g, 0)

    def body(*refs):
        qkv_ref, halo_ref = refs[0], refs[1] if needs_halo else None
        o_ref, lse_ref = refs[-2:]
        q, kc, vc, kp, vp, cur, prev, tail = _dil_operands(g, qkv_ref, halo_ref)
        sc = jnp.where(cur, _bnt(q, kc) * SCALE, NEG)
        m = jnp.max(sc, axis=-1, keepdims=True)
        if tail:
            sp = jnp.where(prev, _bnt(q[-tail:], kp) * SCALE, NEG)
            m = _on_tail(m, tail, lambda t: jnp.maximum(t, jnp.max(sp, axis=-1, keepdims=True)))
            pp = jnp.exp(sp - m[-tail:])
        pc = jnp.exp(sc - m)
        den = jnp.sum(pc, axis=-1, keepdims=True)
        if tail:
            den = _on_tail(den, tail, lambda t: t + jnp.sum(pp, axis=-1, keepdims=True))
        inv = 1.0 / den
        o = _bnn((pc * inv).astype(bf16), vc)
        if tail:
            o = _on_tail(o, tail, lambda t: t + _bnn((pp * inv[-tail:]).astype(bf16), vp))
        lse = m + jnp.log(den)
        for b in range(STEP_BLOCKS):
            for h in range(SLOTS):
                rows, hs = slice(b * BLK, (b + 1) * BLK), slice(h * HD, (h + 1) * HD)
                o_ref[rows, hs] = o[SLOTS * b + h]
                lse_ref[rows, hs] = jnp.broadcast_to(lse[SLOTS * b + h], (BLK, HD))

    out = pl.BlockSpec((STEP_ROWS, DIL_W), lambda i: (i, 0))
    return pl.pallas_call(
        body, grid=(S // STEP_ROWS,), in_specs=in_specs, out_specs=[out, out], out_shape=[SDS((S, DIL_W), f32)] * 2,
        name=f"dil_fwd_{g}", compiler_params=_params(("parallel",), 12 * STEP_ROWS * DIL_W, 2 << 20),
    )(*([qkv] * (2 if needs_halo else 1)))


def _dil_combine(outs, lses):
    def body(o0, o1, o2, l0, l1, l2, out_ref, lse_ref, so1, so2, sl1, sl2):
        for (_, d), src, dst in ((DIL_GROUPS[1], o1, so1), (DIL_GROUPS[2], o2, so2),
                                 (DIL_GROUPS[1], l1, sl1), (DIL_GROUPS[2], l2, sl2)):
            rows = S // d
            for r in range(d):
                dst[pl.ds(r, rows, stride=d), :] = src[r * rows:(r + 1) * rows, :]
        a, b, c = l0[...], sl1[...], sl2[...]
        m = jnp.maximum(jnp.maximum(a, b), c)
        ea, eb, ec = jnp.exp(a - m), jnp.exp(b - m), jnp.exp(c - m)
        z = ea + eb + ec
        inv = 1.0 / z
        out_ref[...] = (ea * inv) * o0[...] + (eb * inv) * so1[...] + (ec * inv) * so2[...]
        lse_ref[...] = m + jnp.log(z)

    blk = pl.BlockSpec((S, 128), lambda c: (0, c))
    return pl.pallas_call(
        body, grid=(DIL_W // 128,), in_specs=[blk] * 6, out_specs=[blk] * 2,
        out_shape=[SDS((S, DIL_W), f32)] * 2, scratch_shapes=[pltpu.VMEM((S, 128), f32)] * 4, name="dil_combine",
        compiler_params=_params(("parallel",), 32 * S * 128, 32 * S * 128))(*outs, *lses)


def _dil_bwd(g, qkv, d_out, delta, lse):
    in_specs, needs_halo = _dil_in_specs(g, 3)

    def body(*refs):
        qkv_ref, halo_ref = refs[0], refs[1] if needs_halo else None
        do_ref, dl_ref, lse_ref, dq_ref, dk_ref, dv_ref = refs[-6:]
        q, kc, vc, kp, vp, cur, prev, tail = _dil_operands(g, qkv_ref, halo_ref)
        tiles = [(slice(b * BLK, (b + 1) * BLK), h) for b in range(STEP_BLOCKS) for h in range(SLOTS)]
        do = jnp.stack([do_ref[rows, h * HD:(h + 1) * HD] for rows, h in tiles]).astype(bf16)
        lse = jnp.stack([lse_ref[rows, h * HD:h * HD + 1] for rows, h in tiles])
        delta = jnp.stack([dl_ref[rows, h * HD:h * HD + 1] for rows, h in tiles])

        def probs(q, k, mask, lse, do, v, delta):
            p = jnp.exp(jnp.where(mask, _bnt(q, k) * SCALE, NEG) - lse)
            ds = p * (_bnt(do, v) - delta) * SCALE
            return p.astype(bf16), ds.astype(bf16)

        p, ds = probs(q, kc, cur, lse, do, vc, delta)
        dq, dk, dv = _bnn(ds, kc), _btn(ds, q), _btn(p, do)
        if tail:
            p, ds = probs(q[-tail:], kp, prev, lse[-tail:], do[-tail:], vp, delta[-tail:])
            dq = _on_tail(dq, tail, lambda t: t + _bnn(ds, kp))
            dk_p, dv_p = _btn(ds, q[-tail:]), _btn(p, do[-tail:])
            inside = tail - SLOTS if needs_halo else tail
            pad = jnp.zeros((len(tiles) - inside, BLK, HD), f32)
            dk = dk + jnp.concatenate([dk_p[tail - inside:], pad], axis=0)
            dv = dv + jnp.concatenate([dv_p[tail - inside:], pad], axis=0)
        first = pl.multiple_of(pl.program_id(0) * STEP_ROWS, STEP_ROWS)
        for t, (rows, h) in enumerate(tiles):
            hs = slice(h * HD, (h + 1) * HD)
            own = pl.ds(pl.multiple_of(first + rows.start, BLK), BLK)
            dq_ref[rows, hs] = dq[t]
            dk_ref[own, hs] = dk[t]
            dv_ref[own, hs] = dv[t]
        if needs_halo:
            before = pl.ds(pl.multiple_of(jnp.maximum(first - BLK, 0), BLK), BLK)
            for h in range(SLOTS):
                hs = slice(h * HD, (h + 1) * HD)
                dk_ref[before, hs] += dk_p[h]
                dv_ref[before, hs] += dv_p[h]

    whole = pl.BlockSpec((S, DIL_W), lambda i: (0, 0))
    return pl.pallas_call(
        body, grid=(S // STEP_ROWS,), in_specs=in_specs,
        out_specs=[pl.BlockSpec((STEP_ROWS, DIL_W), lambda i: (i, 0)), whole, whole],
        out_shape=[SDS((S, DIL_W), f32)] * 3, name=f"dil_bwd_{g}",
        compiler_params=_params(("arbitrary",), 20 * STEP_ROWS * DIL_W + 8 * S * DIL_W, 2 << 20),
    )(*([qkv] * (2 if needs_halo else 1)), d_out, delta, lse)


def _scan_rows(x, reverse):
    row = lax.broadcasted_iota(jnp.int32, x.shape, 0)
    k = 1
    while k < S:
        if reverse:
            x = x + jnp.where(row < S - k, pltpu.roll(x, S - k, 0), 0.0)
        else:
            x = x + jnp.where(row >= k, pltpu.roll(x, k, 0), 0.0)
        k *= 2
    return x


N_PAIR = N_FOX // 2
_PAIR_Q = pl.BlockSpec((None, S, 128), lambda p: (p, 0, 0))
_PAIR_K = pl.BlockSpec((None, 8, S), lambda p: (p, 0, 0))


def _forget_fwd(fz, b128):
    def body(z_ref, b_ref, fq_ref, fk_ref):
        z = z_ref[...] + b_ref[...]
        logf = jnp.minimum(z, 0.0) - jnp.log1p(jnp.exp(-jnp.abs(z)))
        f_cum = _scan_rows(logf, reverse=False)
        f_cum_t = f_cum.T
        fq_ref[...] = jnp.zeros_like(fq_ref)
        fk_ref[...] = jnp.zeros_like(fk_ref)
        for p in range(N_PAIR):
            fq_ref[p, :, 0:2] = f_cum[:, 2 * p:2 * p + 2]
            fk_ref[p, 0:2, :] = f_cum_t[2 * p:2 * p + 2, :]

    return pl.pallas_call(
        body, grid=(1,), in_specs=[pl.BlockSpec((S, 128), lambda i: (0, 0)), _vec(128)],
        out_specs=[pl.BlockSpec((N_PAIR, S, 128), lambda i: (0, 0, 0)), pl.BlockSpec((N_PAIR, 8, S), lambda i: (0, 0, 0))],
        out_shape=[SDS((N_PAIR, S, 128), f32), SDS((N_PAIR, 8, S), f32)], name="forget_fwd",
        compiler_params=_params(("arbitrary",), 24 * S * 128, 24 * S * 128))(fz, b128)


def _forget_bwd(fz, b128, d_f_cols, d_f_rows):
    def body(z_ref, b_ref, dfc_ref, dfr_ref, dz_ref, db_ref, df_sc):
        z = z_ref[...] + b_ref[...]
        df_sc[...] = jnp.zeros_like(df_sc)
        for p in range(N_PAIR):
            df_sc[:, 2 * p:2 * p + 2] = dfr_ref[p, :, 0:2] + dfc_ref[p].T[:, 0:2]
        dz = _scan_rows(df_sc[...], reverse=True) * jax.nn.sigmoid(-z)
        dz_ref[...] = dz
        db_ref[...] = jnp.sum(dz, axis=0, keepdims=True)

    full = pl.BlockSpec((S, 128), lambda i: (0, 0))
    return pl.pallas_call(
        body, grid=(1,),
        in_specs=[full, _vec(128), pl.BlockSpec((N_PAIR, 8, S), lambda i: (0, 0, 0)), pl.BlockSpec((N_PAIR, S, 128), lambda i: (0, 0, 0))],
        out_specs=[full, _vec(128)], out_shape=[SDS((S, 128), f32), SDS((1, 128), f32)],
        scratch_shapes=[pltpu.VMEM((S, 128), f32)], name="forget_bwd",
        compiler_params=_params(("arbitrary",), 32 * S * 128, 24 * S * 128))(fz, b128, d_f_cols, d_f_rows)


def _fox_scores(q_ref, k_ref, fq_ref, fk_ref, qi, hh):
    n = (qi + 1) * TQ
    rows, hs = slice(qi * TQ, n), slice(hh * HD, (hh + 1) * HD)
    q = q_ref[rows, hs] * SCALE
    s = _nt(q, k_ref[0:n, hs]) + (fq_ref[rows, hh:hh + 1] - fk_ref[hh:hh + 1, 0:n])
    below = lax.broadcasted_iota(jnp.int32, (TQ, TQ), 1) <= lax.broadcasted_iota(jnp.int32, (TQ, TQ), 0)
    diag = jnp.where(below, s[:, n - TQ:], NEG)
    return diag if qi == 0 else jnp.concatenate([s[:, :n - TQ], diag], axis=1)


def _pair_cols(first):
    return pl.BlockSpec((S, 128), lambda p: (0, first + p))


def _fox_fwd(vr, fq, fk):
    def body(q_ref, k_ref, v_ref, fq_ref, fk_ref, o_ref, lse_ref):
        lse_ref[...] = jnp.zeros_like(lse_ref)
        for hh in range(2):
            hs = slice(hh * HD, (hh + 1) * HD)
            for qi in range(S // TQ):
                n = (qi + 1) * TQ
                rows = slice(qi * TQ, n)
                s = _fox_scores(q_ref, k_ref, fq_ref, fk_ref, qi, hh)
                m = jnp.max(s, axis=-1, keepdims=True)
                p = jnp.exp(s - m)
                den = jnp.sum(p, axis=-1, keepdims=True)
                o_ref[rows, hs] = jnp.dot((p * (1.0 / den)).astype(bf16), v_ref[0:n, hs], preferred_element_type=f32)
                lse_ref[rows, hh:hh + 1] = m + jnp.log(den)

    return pl.pallas_call(
        body, grid=(N_PAIR,), in_specs=[_pair_cols(0), _pair_cols(N_PAIR), _pair_cols(2 * N_PAIR), _PAIR_Q, _PAIR_K],
        out_specs=[_pair_cols(0), _PAIR_Q], out_shape=[SDS((S, FOX_W), f32), SDS((N_PAIR, S, 128), f32)],
        name="fox_fwd", compiler_params=_params(("parallel",), 12 * S * 128, 16 * TQ * S),
    )(vr, vr, vr, fq, fk)


def _fox_bwd(vr, fq, fk, lse, d_out, delta):
    def body(q_ref, k_ref, v_ref, do_ref, fq_ref, fk_ref, lse_ref, dl_ref, dq_ref, dk_ref, dv_ref, dfc_ref, dfr_ref,
             dk_sc, dv_sc):
        dfc_ref[...] = jnp.zeros_like(dfc_ref)
        dfr_ref[...] = jnp.zeros_like(dfr_ref)
        for hh in range(2):
            hs = slice(hh * HD, (hh + 1) * HD)
            dk_sc[...] = jnp.zeros_like(dk_sc)
            dv_sc[...] = jnp.zeros_like(dv_sc)
            for qi in range(S // TQ):
                n = (qi + 1) * TQ
                rows = slice(qi * TQ, n)
                q, do, k, v = q_ref[rows, hs], do_ref[rows, hs], k_ref[0:n, hs], v_ref[0:n, hs]
                p = jnp.exp(_fox_scores(q_ref, k_ref, fq_ref, fk_ref, qi, hh) - lse_ref[rows, hh:hh + 1])
                ds = p * (_nt(do, v) - dl_ref[rows, hh:hh + 1])
                dsb = ds.astype(bf16)
                dq_ref[rows, hs] = jnp.dot(dsb, k, preferred_element_type=f32) * SCALE
                dk_sc[0:n, :] += _tn(dsb, q) * SCALE
                dv_sc[0:n, :] += _tn(p.astype(bf16), do)
                dfc_ref[hh:hh + 1, 0:n] -= jnp.sum(ds, axis=0, keepdims=True)
                dfr_ref[rows, hh:hh + 1] = jnp.sum(ds, axis=-1, keepdims=True)
            dk_ref[:, hs] = dk_sc[...]
            dv_ref[:, hs] = dv_sc[...]

    cols = [_pair_cols(k * N_PAIR) for k in range(3)]
    return pl.pallas_call(
        body, grid=(N_PAIR,), in_specs=cols + [_pair_cols(0), _PAIR_Q, _PAIR_K, _PAIR_Q, _PAIR_Q],
        out_specs=[_pair_cols(0)] * 3 + [_PAIR_K, _PAIR_Q],
        out_shape=[SDS((S, FOX_W), f32)] * 3 + [SDS((N_PAIR, 8, S), f32), SDS((N_PAIR, S, 128), f32)],
        scratch_shapes=[pltpu.VMEM((S, HD), f32)] * 2, name="fox_bwd",
        compiler_params=_params(("parallel",), 32 * S * 128, 24 * TQ * S),
    )(vr, vr, vr, d_out, fq, fk, lse, delta)


def _merge_fwd(out_a, out_b, w_a, w_b, gf):
    cw = D // N_SHARD

    def body(oa_ref, ob_ref, wa_ref, wb_ref, ga_ref, gb_ref, ya_ref, yb_ref, mg_ref):
        oa, ob = oa_ref[...].astype(bf16), ob_ref[...].astype(bf16)
        for j in range(N_SHARD):
            cols = slice(j * cw, (j + 1) * cw)
            ya = jnp.dot(oa, wa_ref[j], preferred_element_type=f32)
            yb = jnp.dot(ob, wb_ref[j], preferred_element_type=f32)
            ya_ref[:, cols] = ya
            yb_ref[:, cols] = yb
            mg_ref[:, cols] = (jax.nn.sigmoid(ga_ref[:, cols]) * ya + jax.nn.sigmoid(gb_ref[:, cols]) * yb).astype(bf16)

    full = lambda a: pl.BlockSpec(a.shape, lambda i: (0, 0, 0))
    return pl.pallas_call(
        body, grid=(S // TM,),
        in_specs=[_row(DIL_W), _row(FOX_W), full(w_a), full(w_b), _row(D), pl.BlockSpec((TM, D), lambda i: (i, 1))],
        out_specs=[_row(D)] * 3, out_shape=[SDS((S, D), f32), SDS((S, D), f32), SDS((S, D), bf16)], name="merge_fwd",
        compiler_params=_params(("parallel",), 22 * TM * D + 2 * (DIL_W + FOX_W) * D, 16 * TM * D),
    )(out_a, out_b, w_a, w_b, gf, gf)


def _merge_bwd(d_mix, w_out, ya, yb, gf):
    def body(dx_ref, w_ref, ya_ref, yb_ref, ga_ref, gb_ref, dya_ref, dyb_ref, dg_ref):
        dm = _nt(dx_ref[...], w_ref[...])
        sa, sb = jax.nn.sigmoid(ga_ref[...]), jax.nn.sigmoid(gb_ref[...])
        dya_ref[...] = (dm * sa).astype(bf16)
        dyb_ref[...] = (dm * sb).astype(bf16)
        dg_ref[:, :D] = (dm * ya_ref[...] * sa * (1.0 - sa)).astype(bf16)
        dg_ref[:, D:] = (dm * yb_ref[...] * sb * (1.0 - sb)).astype(bf16)

    return pl.pallas_call(
        body, grid=(S // TM,),
        in_specs=[_row(D), _whole(w_out)] + [_row(D)] * 3 + [pl.BlockSpec((TM, D), lambda i: (i, 1))],
        out_specs=[_row(D), _row(D), _row(2 * D)],
        out_shape=[SDS((S, D), bf16), SDS((S, D), bf16), SDS((S, 2 * D), bf16)], name="proj_out_bwd_merge",
        compiler_params=_params(("parallel",), 26 * TM * D + 2 * D * D, 28 * TM * D))(d_mix, w_out, ya, yb, gf, gf)


def _branch_bwd(d_ya, d_yb, w_a, w_b, out_a, out_b):
    cw = D // N_SHARD

    def body(dya_ref, dyb_ref, wa_ref, wb_ref, oa_ref, ob_ref, doa_ref, dla_ref, dob_ref, dlb_ref):
        doa = jnp.zeros((TM, DIL_W), f32)
        dob = jnp.zeros((TM, FOX_W), f32)
        for j in range(N_SHARD):
            cols = slice(j * cw, (j + 1) * cw)
            doa += _nt(dya_ref[:, cols], wa_ref[j])
            dob += _nt(dyb_ref[:, cols], wb_ref[j])
        doa_ref[...] = doa
        dob_ref[...] = dob.astype(bf16)
        prod_a = doa * oa_ref[...]
        for h in range(SLOTS):
            hs = slice(h * HD, (h + 1) * HD)
            dla_ref[:, hs] = jnp.broadcast_to(jnp.sum(prod_a[:, hs], axis=-1, keepdims=True), (TM, HD))
        prod_b = dob * ob_ref[...]
        dlb_ref[...] = jnp.zeros_like(dlb_ref)
        for h in range(N_FOX):
            dlb_ref[h // 2, :, h % 2:h % 2 + 1] = jnp.sum(prod_b[:, h * HD:(h + 1) * HD], axis=-1, keepdims=True)

    full = lambda a: pl.BlockSpec(a.shape, lambda i: (0, 0, 0))
    return pl.pallas_call(
        body, grid=(S // TM,),
        in_specs=[_row(D), _row(D), full(w_a), full(w_b), _row(DIL_W), _row(FOX_W)],
        out_specs=[_row(DIL_W), _row(DIL_W), _row(FOX_W), pl.BlockSpec((N_PAIR, TM, 128), lambda i: (0, i, 0))],
        out_shape=[SDS((S, DIL_W), f32), SDS((S, DIL_W), f32), SDS((S, FOX_W), bf16), SDS((N_PAIR, S, 128), f32)],
        name="branch_bwd", compiler_params=_params(("parallel",), 8 * TM * D + 2 * (DIL_W + FOX_W) * D, 8 * TM * D),
    )(d_ya, d_yb, w_a, w_b, out_a, out_b)


def _branch_grads(out_a, out_b, d_ya, d_yb):
    cw = D // N_SHARD

    def body(oa_ref, ob_ref, dya_ref, dyb_ref, ga_ref, gb_ref):
        ga_ref[...] = _tn(oa_ref[...].astype(bf16), dya_ref[...]).astype(bf16)
        gb_ref[...] = _tn(ob_ref[...].astype(bf16), dyb_ref[...]).astype(bf16)

    whole = lambda w: pl.BlockSpec((S, w), lambda j: (0, 0))
    cols = pl.BlockSpec((S, cw), lambda j: (0, j))
    return pl.pallas_call(
        body, grid=(N_SHARD,), in_specs=[whole(DIL_W), whole(FOX_W), cols, cols],
        out_specs=[pl.BlockSpec((None, DIL_W, cw), lambda j: (j, 0, 0)), pl.BlockSpec((None, FOX_W, cw), lambda j: (j, 0, 0))],
        out_shape=[SDS((N_SHARD, DIL_W, cw), bf16), SDS((N_SHARD, FOX_W, cw), bf16)], name="grad_w_proj_ab",
        compiler_params=_params(("parallel",), 4 * S * (DIL_W + FOX_W) + 4 * S * cw + 4 * (DIL_W + FOX_W) * cw,
                                4 * S * (DIL_W + FOX_W)))(out_a, out_b, d_ya, d_yb)


FF_TN = F_FF // 2
FF_TM = 1024


def _ffn_fwd(h, w_gate_t, w_up_t):
    def body(h_ref, wg_ref, wu_ref, g_ref, u_ref, a_ref):
        hb = h_ref[...]
        g = _nt(hb, wg_ref[...])
        u = _nt(hb, wu_ref[...])
        g_ref[...] = g
        u_ref[...] = u
        a_ref[...] = (g * jax.nn.sigmoid(g) * u).astype(bf16)

    tile = pl.BlockSpec((FF_TM, FF_TN), lambda j, i: (i, j))
    wspec = pl.BlockSpec((FF_TN, D), lambda j, i: (j, 0))
    return pl.pallas_call(
        body, grid=(F_FF // FF_TN, S // FF_TM),
        in_specs=[pl.BlockSpec((FF_TM, D), lambda j, i: (i, 0)), wspec, wspec], out_specs=[tile] * 3,
        out_shape=[SDS((S, F_FF), f32), SDS((S, F_FF), f32), SDS((S, F_FF), bf16)], name="ffn_fwd",
        compiler_params=_params(("parallel", "parallel"), 2 * FF_TM * D + 4 * D * FF_TN + 10 * FF_TM * FF_TN, 16 * FF_TM * FF_TN),
    )(h, w_gate_t, w_up_t)


def _ffn_bwd_act(d_ff, w_down, g_act, u_act):
    def body(d_ref, wd_ref, g_ref, u_ref, dg_ref, du_ref):
        da = _nt(d_ref[...], wd_ref[...])
        g = g_ref[...]
        sg = jax.nn.sigmoid(g)
        du_ref[...] = (da * g * sg).astype(bf16)
        dg_ref[...] = (da * u_ref[...] * sg * (1.0 + g * (1.0 - sg))).astype(bf16)

    tile = pl.BlockSpec((FF_TM, FF_TN), lambda j, i: (i, j))
    return pl.pallas_call(
        body, grid=(F_FF // FF_TN, S // FF_TM),
        in_specs=[pl.BlockSpec((FF_TM, D), lambda j, i: (i, 0)), pl.BlockSpec((FF_TN, D), lambda j, i: (j, 0)), tile, tile],
        out_specs=[tile, tile], out_shape=[SDS((S, F_FF), bf16)] * 2, name="ffn_bwd_act",
        compiler_params=_params(("parallel", "parallel"), 2 * FF_TM * D + 2 * D * FF_TN + 12 * FF_TM * FF_TN, 16 * FF_TM * FF_TN),
    )(d_ff, w_down, g_act, u_act)


def _row_tile(rows):
    return next(t for t in (376, 128, 176, 64, 32, 16, 8) if rows % t == 0)


def _adamw_math(w, g, m, v):
    c1 = 1.0 - ADAM_B1 ** ADAM_STEP
    c2 = 1.0 - ADAM_B2 ** ADAM_STEP
    m_new = ADAM_B1 * m + (1.0 - ADAM_B1) * g
    v_new = ADAM_B2 * v + (1.0 - ADAM_B2) * (g * g)
    return -ADAM_LR * ((m_new / c1) / (jnp.sqrt(v_new / c2) + ADAM_EPS) + ADAM_WD * w), m_new, v_new


def _adamw(w, g, m, v, name):
    rows, cols = w.shape
    tm = _row_tile(rows)

    def body(w_ref, g_ref, m_ref, v_ref, d_ref, nm_ref, nv_ref):
        d_ref[...], nm_ref[...], nv_ref[...] = _adamw_math(w_ref[...], g_ref[...], m_ref[...], v_ref[...])

    spec = pl.BlockSpec((tm, cols), lambda i: (i, 0))
    return pl.pallas_call(
        body, grid=(rows // tm,), in_specs=[spec] * 4, out_specs=[spec] * 3, out_shape=[SDS(w.shape, f32)] * 3,
        name=name, compiler_params=_params(("parallel",), 28 * tm * cols, 16 * tm * cols))(w, g, m, v)


def _adamw_halves(w, g_mine, g_theirs, m, v, name):
    rows, cols = w.shape
    tm = _row_tile(rows // 2)
    per_half = rows // 2 // tm
    core = lax.axis_index("c").astype(jnp.int32).reshape(1)

    def body(c_ref, w_ref, gm_ref, gt_ref, m_ref, v_ref, g_ref, d_ref, nm_ref, nv_ref):
        mine = pl.program_id(0) // per_half == c_ref[0]
        g = jnp.where(mine, gm_ref[...], gt_ref[...])
        g_ref[...] = g
        d_ref[...], nm_ref[...], nv_ref[...] = _adamw_math(w_ref[...], g, m_ref[...], v_ref[...])

    spec = pl.BlockSpec((tm, cols), lambda i, c_ref: (i, 0))
    in_half = lambda i, first: jnp.clip(i - first * per_half, 0, per_half - 1)
    grid_spec = pltpu.PrefetchScalarGridSpec(
        num_scalar_prefetch=1, grid=(rows // tm,),
        in_specs=[spec, pl.BlockSpec((tm, cols), lambda i, c_ref: (in_half(i, c_ref[0]), 0)),
                  pl.BlockSpec((tm, cols), lambda i, c_ref: (in_half(i, 1 - c_ref[0]), 0)), spec, spec],
        out_specs=[spec] * 4)
    return pl.pallas_call(
        body, grid_spec=grid_spec, out_shape=[SDS(w.shape, f32)] * 4, name=name,
        compiler_params=_params(("parallel",), 36 * tm * cols, 16 * tm * cols))(core, w, g_mine, g_theirs, m, v)


_ANY = pl.BlockSpec(memory_space=pl.ANY)


def _place():
    x, y, c = lax.axis_index("x"), lax.axis_index("y"), lax.axis_index("c")
    chips = [(1 - x, y), (x, 1 - y), (1 - x, 1 - y)]
    return x, y, c, chips


def _halved(t):
    return t.reshape(t.shape[:-2] + (2, t.shape[-2] // 2, t.shape[-1]))


def _gather_body(src, out, send_ici, recv_ici, send_d2d, recv_d2d):
    x, y, c, chips = _place()
    sibling = (x, y, 1 - c)
    me_j = 2 * x + y
    sends = []
    for a in range(len(src)):
        for p in range(3):
            cp = pltpu.make_async_remote_copy(
                src_ref=src[a].at[c], dst_ref=out[a].at[me_j, c], send_sem=send_ici.at[a, p],
                recv_sem=recv_ici.at[a, p], device_id=(*chips[p], c), device_id_type=MESH)
            cp.start()
            sends.append(cp)
    for a in range(len(src)):
        for p, (px, py) in enumerate(chips):
            blk = out[a].at[2 * px + py, c]
            pltpu.make_async_remote_copy(
                src_ref=blk, dst_ref=blk, send_sem=send_ici.at[a, p], recv_sem=recv_ici.at[a, p],
                device_id=sibling, device_id_type=MESH).wait_recv()
            fw = pltpu.make_async_remote_copy(
                src_ref=blk, dst_ref=blk, send_sem=send_d2d.at[a, p], recv_sem=recv_d2d.at[a, p],
                device_id=sibling, device_id_type=MESH)
            fw.start()
            sends.append(fw)
    for a in range(len(src)):
        for p, (px, py) in enumerate(chips):
            blk = out[a].at[2 * px + py, 1 - c]
            pltpu.make_async_remote_copy(
                src_ref=blk, dst_ref=blk, send_sem=send_d2d.at[a, p], recv_sem=recv_d2d.at[a, p],
                device_id=sibling, device_id_type=MESH).wait_recv()
    for cp in sends:
        cp.wait_send()


def _handshake(peers):
    barrier = pltpu.get_barrier_semaphore()
    for peer in peers:
        pl.semaphore_signal(barrier, inc=1, device_id=peer, device_id_type=MESH)
    pl.semaphore_wait(barrier, len(peers))


_SEQUENCER = dict(axis_name="sequencer", num_cores=1)
GATHER_LATE_ID, SCATTER_EARLY_ID, SWAP_EARLY_ID, GATHER_FIRST_ID, SCATTER_LATE_ID, SWAP_HALVES_ID = 1, 2, 3, 4, 5, 6


def _all_gather_async(shards, after, name, collective_id):
    n, k = len(shards), len(after)

    def body(*refs):
        x, y, c, chips = _place()
        _handshake([(*chip, c) for chip in chips] + [(x, y, 1 - c)])
        _gather_body(refs[:n], refs[n + k:2 * n + k], *refs[2 * n + k:])

    return pl.kernel(
        body, out_type=[SDS((N_SHARD,) + t.shape, t.dtype) for t in shards],
        mesh=plsc.ScalarSubcoreMesh(**_SEQUENCER), scratch_types=[pltpu.SemaphoreType.DMA((n, 3))] * 4,
        compiler_params=pltpu.CompilerParams(collective_id=collective_id), name=name)(*shards, *after)


def _pair_swap(grads):
    n = len(grads)

    def body(*refs):
        src, out, send_sems, recv_sems = refs[:n], refs[n:2 * n], refs[2 * n], refs[2 * n + 1]
        x, y, c, _ = _place()
        copies = [pltpu.make_async_remote_copy(
            src_ref=src[a].at[:, 1 - c], dst_ref=out[a], send_sem=send_sems.at[a], recv_sem=recv_sems.at[a],
            device_id=(x, y, 1 - c), device_id_type=MESH) for a in range(n)]
        for cp in copies:
            cp.start()
        for cp in copies:
            cp.wait()

    return pl.pallas_call(
        body, in_specs=[_ANY] * n, out_specs=[_ANY] * n,
        out_shape=[SDS((N_SHARD,) + t.shape[2:], t.dtype) for t in grads],
        scratch_shapes=[pltpu.SemaphoreType.DMA((n,)), pltpu.SemaphoreType.DMA((n,))], name="pair_swap",
        compiler_params=pltpu.CompilerParams(has_side_effects=True))(*grads)


def _pair_swap_early(grads):
    n = len(grads)

    def body(*refs):
        src, out, send_sems, recv_sems = refs[:n], refs[n:2 * n], refs[2 * n], refs[2 * n + 1]
        x, y, c, _ = _place()
        _handshake([(x, y, 1 - c)])
        copies = [pltpu.make_async_remote_copy(
            src_ref=src[a].at[:, 1 - c], dst_ref=out[a], send_sem=send_sems.at[a], recv_sem=recv_sems.at[a],
            device_id=(x, y, 1 - c), device_id_type=MESH) for a in range(n)]
        for cp in copies:
            cp.start()
        for cp in copies:
            cp.wait()

    return pl.kernel(
        body, out_type=[SDS((N_SHARD,) + t.shape[2:], t.dtype) for t in grads],
        mesh=plsc.ScalarSubcoreMesh(**_SEQUENCER), scratch_types=[pltpu.SemaphoreType.DMA((n,))] * 2,
        compiler_params=pltpu.CompilerParams(collective_id=SWAP_EARLY_ID), name="pair_swap_early")(*grads)


def _scatter_early(parts):
    n = len(parts)

    def body(*refs):
        part, recv, send_sems, recv_sems = refs[:n], refs[n:2 * n], refs[2 * n], refs[2 * n + 1]
        x, y, c, chips = _place()
        _handshake([(*chip, c) for chip in chips])
        me_j = 2 * x + y
        sends = []
        for a in range(n):
            for p, (px, py) in enumerate(chips):
                cp = pltpu.make_async_remote_copy(
                    src_ref=part[a].at[2 * px + py], dst_ref=recv[a].at[me_j], send_sem=send_sems.at[a, p],
                    recv_sem=recv_sems.at[a, p], device_id=(px, py, c), device_id_type=MESH)
                cp.start()
                sends.append(cp)
        for a in range(n):
            for p, (px, py) in enumerate(chips):
                slot = recv[a].at[2 * px + py]
                pltpu.make_async_remote_copy(
                    src_ref=slot, dst_ref=slot, send_sem=send_sems.at[a, p], recv_sem=recv_sems.at[a, p],
                    device_id=(px, py, c), device_id_type=MESH).wait_recv()
        for cp in sends:
            cp.wait_send()

    return pl.kernel(
        body, out_type=[SDS(t.shape, t.dtype) for t in parts],
        mesh=plsc.ScalarSubcoreMesh(**_SEQUENCER), scratch_types=[pltpu.SemaphoreType.DMA((n, 3))] * 2,
        compiler_params=pltpu.CompilerParams(collective_id=SCATTER_EARLY_ID), name="scatter_early")(*parts)


def _pair_sum(grads, other, name):
    _, _, rows, cols = grads.shape
    tr = _row_tile(rows)
    core = lax.axis_index("c").astype(jnp.int32).reshape(1)

    def body(c_ref, g_ref, o_ref, out_ref):
        out_ref[...] = (g_ref[...].astype(f32) + o_ref[...].astype(f32)).astype(bf16)

    grid_spec = pltpu.PrefetchScalarGridSpec(
        num_scalar_prefetch=1, grid=(N_SHARD, rows // tr),
        in_specs=[pl.BlockSpec((None, None, tr, cols), lambda j, i, c_ref: (j, c_ref[0], i, 0)),
                  pl.BlockSpec((None, tr, cols), lambda j, i, c_ref: (j, i, 0))],
        out_specs=pl.BlockSpec((None, tr, cols), lambda j, i, c_ref: (j, i, 0)))
    return pl.pallas_call(
        body, grid_spec=grid_spec, out_shape=SDS((N_SHARD, rows, cols), bf16), name=name,
        compiler_params=_params(("parallel", "parallel"), 10 * tr * cols, 12 * tr * cols))(core, grads, other)


def _scatter_partials(parts, small):
    n = len(parts)

    def body(*refs):
        part, small_ref, recv, small_all_ref = refs[:n], refs[n], refs[n + 1:2 * n + 1], refs[2 * n + 1]
        send_sems, recv_sems, ssend, srecv, local_sem = refs[2 * n + 2:]
        x, y, c, chips = _place()
        flip = lambda a, bit: 1 - a if bit else a
        peers = [(flip(x, k & 4), flip(y, k & 2), flip(c, k & 1)) for k in range(1, 8)]
        _handshake(peers)
        me_j = 2 * x + y
        me_dev = 4 * x + 2 * y + c
        own = pltpu.make_async_copy(small_ref, small_all_ref.at[me_dev], local_sem)
        own.start()
        sends = []
        for a in range(n):
            for p, (px, py) in enumerate(chips):
                cp = pltpu.make_async_remote_copy(
                    src_ref=part[a].at[2 * px + py], dst_ref=recv[a].at[me_j], send_sem=send_sems.at[a, p],
                    recv_sem=recv_sems.at[a, p], device_id=(px, py, c), device_id_type=MESH)
                cp.start()
                sends.append(cp)
        for k, to in enumerate(peers):
            cp = pltpu.make_async_remote_copy(
                src_ref=small_ref, dst_ref=small_all_ref.at[me_dev],
                send_sem=ssend.at[k], recv_sem=srecv.at[k], device_id=to, device_id_type=MESH)
            cp.start()
            sends.append(cp)
        for a in range(n):
            for p, (px, py) in enumerate(chips):
                slot = recv[a].at[2 * px + py]
                pltpu.make_async_remote_copy(
                    src_ref=slot, dst_ref=slot, send_sem=send_sems.at[a, p], recv_sem=recv_sems.at[a, p],
                    device_id=(px, py, c), device_id_type=MESH).wait_recv()
        for k, (px, py, pc) in enumerate(peers):
            slot = small_all_ref.at[4 * px + 2 * py + pc]
            pltpu.make_async_remote_copy(
                src_ref=slot, dst_ref=slot, send_sem=ssend.at[k], recv_sem=srecv.at[k],
                device_id=(px, py, pc), device_id_type=MESH).wait_recv()
        for cp in sends:
            cp.wait_send()
        own.wait()

    return pl.kernel(
        body, out_type=[SDS(t.shape, t.dtype) for t in parts] + [SDS((8, SMALL_ROWS, D), f32)],
        mesh=plsc.ScalarSubcoreMesh(**_SEQUENCER),
        scratch_types=[pltpu.SemaphoreType.DMA((n, 3)), pltpu.SemaphoreType.DMA((n, 3)),
                       pltpu.SemaphoreType.DMA((7,)), pltpu.SemaphoreType.DMA((7,)), pltpu.SemaphoreType.DMA],
        compiler_params=pltpu.CompilerParams(collective_id=SCATTER_LATE_ID), name="scatter_partials")(*parts, small)


def _sum_partials(part, recv, name):
    _, rows, cols = recv.shape
    tr = _row_tile(rows)
    me = (2 * lax.axis_index("x") + lax.axis_index("y")).astype(jnp.int32).reshape(1)

    def body(me_ref, mine, r0, r1, r2, r3, out_ref):
        acc = None
        for j, r in enumerate((r0, r1, r2, r3)):
            term = jnp.where(me_ref[0] == j, mine[...], r[...]).astype(f32)
            acc = term if acc is None else acc + term
        out_ref[...] = acc

    slot = lambda j: pl.BlockSpec((None, tr, cols), lambda i, me_ref: (jnp.where(me_ref[0] == j, j ^ 1, j), i, 0))
    grid_spec = pltpu.PrefetchScalarGridSpec(
        num_scalar_prefetch=1, grid=(rows // tr,),
        in_specs=[pl.BlockSpec((None, tr, cols), lambda i, me_ref: (me_ref[0], i, 0)), slot(0), slot(1), slot(2), slot(3)],
        out_specs=pl.BlockSpec((tr, cols), lambda i, me_ref: (i, 0)))
    return pl.pallas_call(
        body, grid_spec=grid_spec, out_shape=SDS((rows, cols), f32), name=name,
        compiler_params=_params(("parallel",), 14 * tr * cols, 12 * tr * cols))(me, part, recv, recv, recv, recv)


def _sum_small(small_all):
    def body(small_ref, out_ref):
        tot = small_ref[0]
        for k in range(1, 8):
            tot = tot + small_ref[k]
        out_ref[...] = tot

    return pl.pallas_call(
        body, grid=(1,), in_specs=[pl.BlockSpec((8, SMALL_ROWS, D), lambda i: (0, 0, 0))],
        out_specs=pl.BlockSpec((SMALL_ROWS, D), lambda i: (0, 0)), out_shape=SDS((SMALL_ROWS, D), f32),
        name="sum_small", compiler_params=_params(("arbitrary",), 36 * SMALL_ROWS * D))(small_all)


def _swap_halves(halves, name, collective_id=None):
    n = len(halves)

    def body(*refs):
        src, out, send_sems, recv_sems = refs[:n], refs[n:2 * n], refs[2 * n], refs[2 * n + 1]
        x, y, c, _ = _place()
        if collective_id is not None:
            _handshake([(x, y, 1 - c)])
        copies = [pltpu.make_async_remote_copy(
            src_ref=src[a], dst_ref=out[a], send_sem=send_sems.at[a], recv_sem=recv_sems.at[a],
            device_id=(x, y, 1 - c), device_id_type=MESH) for a in range(n)]
        for cp in copies:
            cp.start()
        for cp in copies:
            cp.wait()

    if collective_id is not None:
        return pl.kernel(
            body, out_type=[SDS(t.shape, f32) for t in halves], mesh=plsc.ScalarSubcoreMesh(**_SEQUENCER),
            scratch_types=[pltpu.SemaphoreType.DMA((n,))] * 2,
            compiler_params=pltpu.CompilerParams(collective_id=collective_id), name=name)(*halves)
    return pl.pallas_call(
        body, in_specs=[_ANY] * n, out_specs=[_ANY] * n, out_shape=[SDS(t.shape, f32) for t in halves],
        scratch_shapes=[pltpu.SemaphoreType.DMA((n,))] * 2, name=name,
        compiler_params=pltpu.CompilerParams(has_side_effects=True))(*halves)


def _kernel_layout(name, t):
    t = t[0]
    if name in TRANSPOSED:
        t = jnp.swapaxes(t, 0, 1)
    return _pad_rows(t, SHARD_SHAPE[name][0])


def _harness_layout(name, t):
    if name == "w_in":
        t = t[:IN_SHARD]
    if name in TRANSPOSED:
        t = jnp.swapaxes(t, 0, 1)
    return t[None]


def _pad_rows(t, rows):
    return t if t.shape[0] == rows else jnp.pad(t, ((0, rows - t.shape[0]), (0, 0)))


_QA, _KA, _VA, _QB, _F, _GAB = 0, 768, 1536, 2304, 3840, 3848


def _spans(a, b):
    return [(j, max(a, j * IN_SHARD) - j * IN_SHARD, max(a, j * IN_SHARD) - a,
             min(b, (j + 1) * IN_SHARD) - max(a, j * IN_SHARD))
            for j in range(N_SHARD) if max(a, j * IN_SHARD) < min(b, (j + 1) * IN_SHARD)]


_LANES = pl.BlockSpec((N_SHARD, IN_SHARD_PAD, 128), lambda c: (0, 0, c))


def _split_w_in(shards):
    group = [[(o + g * DIL_W, o + (g + 1) * DIL_W) for o in (_QA, _KA, _VA)] for g in range(3)]
    fox = [[(_QB + k * FOX_W, _QB + (k + 1) * FOX_W)] for k in range(3)]
    wanted = group + fox + [[(_QB, _F)], [(_F, _GAB)], [(_GAB, IN_COLS)]]
    rows = [sum(b - a for a, b in w) for w in wanted]
    rows[7] = 128

    def body(s_ref, *o_refs):
        for o_ref, want in zip(o_refs, wanted):
            at = 0
            for a, b in want:
                for j, src, off, n in _spans(a, b):
                    o_ref[at + off:at + off + n, :] = s_ref[j, src:src + n, :]
                at += b - a
        o_refs[7][N_FOX:, :] = jnp.zeros((128 - N_FOX, 128), bf16)

    return pl.pallas_call(
        body, grid=(D // 128,), in_specs=[_LANES], out_specs=[pl.BlockSpec((r, 128), lambda c: (0, c)) for r in rows],
        out_shape=[SDS((r, D), bf16) for r in rows], name="split_w_in",
        compiler_params=_params(("parallel",), 2 * 128 * (N_SHARD * IN_SHARD_PAD + sum(rows))))(shards)


def _join_w_in(g_a, g_fox, g_f, g_gab):
    parts = [(g_a[k], o, o + DIL_W) for o in (0, DIL_W, 2 * DIL_W) for k in range(3)]
    parts += [(t, 0, FOX_W) for t in g_fox] + [(g_f, 0, N_FOX), (g_gab, 0, 2 * D)]
    arrays = list(g_a) + list(g_fox) + [g_f, g_gab]
    index = {id(t): i for i, t in enumerate(arrays)}

    def body(*refs):
        o_ref = refs[-1]
        o_ref[:, IN_SHARD:, :] = jnp.zeros((N_SHARD, IN_SHARD_PAD - IN_SHARD, 128), bf16)
        at = 0
        for t, lo, hi in parts:
            src_ref = refs[index[id(t)]]
            for j, dst, off, n in _spans(at, at + hi - lo):
                o_ref[j, dst:dst + n, :] = src_ref[lo + off:lo + off + n, :].astype(bf16)
            at += hi - lo

    return pl.pallas_call(
        body, grid=(D // 128,), in_specs=[pl.BlockSpec((t.shape[0], 128), lambda c: (0, c)) for t in arrays],
        out_specs=_LANES, out_shape=SDS((N_SHARD, IN_SHARD_PAD, D), bf16), name="join_w_in",
        compiler_params=_params(("parallel",), 2 * 128 * (N_SHARD * IN_SHARD_PAD + sum(t.shape[0] for t in arrays))),
    )(*arrays)


def _full_weights(gathered):
    full = {n: t.reshape((N_SHARD,) + SHARD_SHAPE[n]) for n, t in gathered.items()}
    out = {}
    if "w_in" in full:
        pieces = _split_w_in(full["w_in"])
        out.update(w_a_t=pieces[0:3], w_fox_t=pieces[3:6], w_vr_t=pieces[6], w_f_t=pieces[7], w_gab_t=pieces[8])
    if "w_out" in full:
        out.update(
            w_a4=full["w_proj_a"],
            w_b4=full["w_proj_b"],
            w_out=full["w_out"].reshape(D, D),
            w_gate_t=full["w_ffn_gate"].reshape(F_FF, D),
            w_up_t=full["w_ffn_up"].reshape(F_FF, D),
            w_down=full["w_ffn_down"].reshape(F_FF, D))
    return out


def _sharded_grads(g):
    full = dict(w_in=_join_w_in(g["w_a_t"], g["w_fox_t"], g["w_f_t"], g["w_gab_t"]), w_proj_a=g["w_a4"],
                w_proj_b=g["w_b4"], w_out=g["w_out"], w_ffn_gate=g["w_gate_t"], w_ffn_up=g["w_up_t"],
                w_ffn_down=g["w_down"])
    return {n: _halved(full[n].reshape((N_SHARD,) + SHARD_SHAPE[n])) for n in W_NAMES}


def _local_step(x, target, wt, b_forget, g_mix_pre, g_mix_post, g_ffn_pre, g_ffn_post, late=None):
    tables = _rope_tables()
    b128 = jnp.pad(b_forget, ((0, 0), (0, 128 - N_FOX)))
    dils = tuple(d for _, d in DIL_GROUPS[1:])

    hs = _norm_fwd([x] + list(_perm_rows([x], dils, "perm_x")), g_mix_pre)
    h1 = hs[0]
    if callable(wt):
        wt = wt(h1)
    qkv = [_rope_fwd(g, _mm([(hs[g], wt["w_a_t"][g])], "nt", f32, tm=1024, tn=QKV_W, name=f"proj_a_{g}"), tables)
           for g in range(3)]
    vr = _mm([(h1, wt["w_vr_t"])], "nt", bf16, tm=1024, tn=VR_W // 2, name="proj_vr")
    gab = _mm([(h1, wt["w_gab_t"])], "nt", f32, tm=512, tn=2 * D, name="proj_gab")
    fz = _mm([(h1, wt["w_f_t"])], "nt", f32, tm=1024, tn=128, name="proj_f")
    dil = [_dil_fwd(g, qkv[g]) for g in range(3)]
    out_a, lse_a = _dil_combine([o for o, _ in dil], [l for _, l in dil])
    f_q, f_k = _forget_fwd(fz, b128)
    out_b, lse_b = _fox_fwd(vr, f_q, f_k)
    if late is not None:
        wt = {**wt, **late(out_b)}
    ya, yb, merged = _merge_fwd(out_a, out_b, wt["w_a4"], wt["w_b4"], gab)
    mix, x2, h3 = _resid_norm_fwd(x, merged, wt["w_out"], g_mix_post, g_ffn_pre)
    g_act, u_act, a_act = _ffn_fwd(h3, wt["w_gate_t"], wt["w_up_t"])
    sq_err, dy, d_ff, dg_ffn_post = _loss_head(x2, a_act, wt["w_down"], g_ffn_post, target)

    grads = {}
    d_g, d_u = _ffn_bwd_act(d_ff, wt["w_down"], g_act, u_act)
    grads["w_down"] = _mm([(a_act, d_ff)], "tn", bf16, tm=FF_TN, tn=D, name="grad_w_down")
    grads["w_gate_t"] = _mm([(d_g, h3)], "tn", bf16, tm=FF_TN, tn=D, name="grad_w_gate")
    grads["w_up_t"] = _mm([(d_u, h3)], "tn", bf16, tm=FF_TN, tn=D, name="grad_w_up")
    dx2, d_mix, dg_ffn_pre, dg_mix_post = _norm_bwd_mid(dy, d_g, d_u, wt["w_gate_t"], wt["w_up_t"], x2, mix,
                                                        g_ffn_pre, g_mix_post)

    grads["w_out"] = _mm([(merged, d_mix)], "tn", bf16, tm=D, tn=D, name="grad_w_out")
    d_ya, d_yb, d_gab = _merge_bwd(d_mix, wt["w_out"], ya, yb, gab)
    grads["w_a4"], grads["w_b4"] = _branch_grads(out_a, out_b, d_ya, d_yb)
    d_out_a, delta_a, d_out_b, delta_b = _branch_bwd(d_ya, d_yb, wt["w_a4"], wt["w_b4"], out_a, out_b)

    perm = _perm_rows([d_out_a, delta_a, lse_a], dils, "perm_dil_bwd")
    aux = [(d_out_a, delta_a, lse_a)] + [tuple(perm[k * len(dils) + i] for k in range(3)) for i in range(len(dils))]
    d_qkv = []
    for g in range(3):
        dq, dk, dv = _dil_bwd(g, qkv[g], *aux[g])
        d_qkv.append(_rope_bwd(g, dq, dk, dv, tables))
    *d_fox, d_f_cols, d_f_rows = _fox_bwd(vr, f_q, f_k, lse_b, d_out_b, delta_b)
    d_z, d_b128 = _forget_bwd(fz, b128, d_f_cols, d_f_rows)

    grads["w_a_t"] = [_mm([(d_qkv[g], hs[g])], "tn", bf16, tm=QKV_W, tn=D, name=f"grad_w_a_{g}") for g in range(3)]
    grads["w_fox_t"] = [_mm([(d_fox[k], h1)], "tn", bf16, tm=FOX_W, tn=D, name=f"grad_w_fox_{k}") for k in range(3)]
    grads["w_gab_t"] = _mm([(d_gab, h1)], "tn", bf16, tm=D, tn=D, name="grad_w_gab")
    grads["w_f_t"] = _mm([(d_z, h1)], "tn", bf16, tm=128, tn=D, name="grad_w_f")
    d_h1_nat = _mm([(d_qkv[0], wt["w_a_t"][0])] + list(zip(d_fox, wt["w_fox_t"]))
                   + [(d_gab, wt["w_gab_t"]), (d_z, wt["w_f_t"])], "nn", f32, tm=512, tn=D, name="proj_in_bwd")
    d_h1_dil = [_mm([(d_qkv[g], wt["w_a_t"][g])], "nn", f32, tm=1024, tn=D, name=f"proj_a_bwd_{g}") for g in (1, 2)]
    d_h1 = _unperm_sum(d_h1_nat, d_h1_dil, dils, "unperm_d_h1")
    grad_x, dg_mix_pre = _norm_bwd_in(dx2, d_h1, x, g_mix_pre)

    small = dict(b_forget=d_b128[:, :N_FOX], norm_mix_pre=dg_mix_pre, norm_mix_post=dg_mix_post,
                 norm_ffn_pre=dg_ffn_pre, norm_ffn_post=dg_ffn_post)
    grads["mid_backward"] = d_qkv[0]
    return sq_err, grad_x, grads, small


NORMS = ("norm_mix_pre", "norm_mix_post", "norm_ffn_pre", "norm_ffn_post")
ORDER = ("w_in", "w_proj_a", "w_proj_b", "w_out", "b_forget", "w_ffn_gate", "w_ffn_up", "w_ffn_down") + NORMS


def kernel(x, w_in, w_proj_a, w_proj_b, w_out, b_forget, w_ffn_gate, w_ffn_up, w_ffn_down, norm_mix_pre, norm_mix_post, norm_ffn_pre, norm_ffn_post, loss_target, m_w_in, m_w_proj_a, m_w_proj_b, m_w_out, m_b_forget, m_w_ffn_gate, m_w_ffn_up, m_w_ffn_down, m_norm_mix_pre, m_norm_mix_post, m_norm_ffn_pre, m_norm_ffn_post, v_w_in, v_w_proj_a, v_w_proj_b, v_w_out, v_b_forget, v_w_ffn_gate, v_w_ffn_up, v_w_ffn_down, v_norm_mix_pre, v_norm_mix_post, v_norm_ffn_pre, v_norm_ffn_post):
    given = dict(w_in=w_in, w_proj_a=w_proj_a, w_proj_b=w_proj_b, w_out=w_out, w_ffn_gate=w_ffn_gate,
                 w_ffn_up=w_ffn_up, w_ffn_down=w_ffn_down)
    given_m = dict(w_in=m_w_in, w_proj_a=m_w_proj_a, w_proj_b=m_w_proj_b, w_out=m_w_out, w_ffn_gate=m_w_ffn_gate,
                   w_ffn_up=m_w_ffn_up, w_ffn_down=m_w_ffn_down)
    given_v = dict(w_in=v_w_in, w_proj_a=v_w_proj_a, w_proj_b=v_w_proj_b, w_out=v_w_out, w_ffn_gate=v_w_ffn_gate,
                   w_ffn_up=v_w_ffn_up, w_ffn_down=v_w_ffn_down)
    w, m, v = ({n: _kernel_layout(n, t[n]) for n in W_NAMES} for t in (given, given_m, given_v))
    small_w = dict(b_forget=b_forget, norm_mix_pre=norm_mix_pre, norm_mix_post=norm_mix_post,
                   norm_ffn_pre=norm_ffn_pre, norm_ffn_post=norm_ffn_post)
    small_m = dict(b_forget=m_b_forget, norm_mix_pre=m_norm_mix_pre, norm_mix_post=m_norm_mix_post,
                   norm_ffn_pre=m_norm_ffn_pre, norm_ffn_post=m_norm_ffn_post)
    small_v = dict(b_forget=v_b_forget, norm_mix_pre=v_norm_mix_pre, norm_mix_post=v_norm_mix_post,
                   norm_ffn_pre=v_norm_ffn_pre, norm_ffn_post=v_norm_ffn_post)

    own = [_halved(w[n].astype(bf16)) for n in W_NAMES]
    chip = 2 * lax.axis_index("x") + lax.axis_index("y")
    exchanged = {"first": _all_gather_async(own[:1], [], "all_gather_first", GATHER_FIRST_ID)}
    fill = lambda ts, mine: [lax.dynamic_update_index_in_dim(t, o, chip, 0) for t, o in zip(ts, mine)]

    def first_weights(ready):
        arrived, _ = lax.optimization_barrier((list(exchanged["first"]), ready))
        exchanged["late"] = _all_gather_async(own[1:], [arrived[0][0, 0, :16, :128]], "all_gather_late", GATHER_LATE_ID)
        return _full_weights(dict(zip(W_NAMES[:1], fill(arrived, own[:1]))))

    def late_weights(ready):
        arrived, _ = lax.optimization_barrier((list(exchanged["late"]), ready))
        return _full_weights(dict(zip(W_NAMES[1:], fill(arrived, own[1:]))))

    sq_err, grad_x, grads, small = _local_step(x[0], loss_target[0], first_weights, b_forget, norm_mix_pre,
                                               norm_mix_post, norm_ffn_pre, norm_ffn_post, late=late_weights)

    g4 = _sharded_grads(grads)
    stack = lambda t, extra: jnp.concatenate(
        [jnp.pad(t["b_forget"], ((0, 0), (0, D - N_FOX)))] + [t[n] for n in NORMS]
        + [jnp.pad(extra, ((0, SMALL_ROWS - LOSS_ROW - 1), (0, D - extra.shape[1])), constant_values=1.0)], axis=0)
    early, _ = lax.optimization_barrier((list(_pair_swap_early([g4[n] for n in W_NAMES[1:]])), grads["mid_backward"]))
    other = list(_pair_swap([g4["w_in"]])) + early
    parts = [_pair_sum(g4[n], o, "pair_sum_" + n) for n, o in zip(W_NAMES, other)]
    recv_early = _scatter_early(parts[1:])
    recv_in, small_all = _scatter_partials(parts[:1], stack(small, sq_err))

    g_shard, delta, new_m, new_v = {}, {}, {}, {}

    def finish(names, parts, recv, after=None, collective_id=None):
        halves = [_sum_partials(p, r, "sum_partials_" + n) for n, p, r in zip(names, parts, recv)]
        theirs = list(_swap_halves(halves, "swap_halves_" + names[0], collective_id))
        if after is not None:
            theirs, _ = lax.optimization_barrier((theirs, after))
        for n, mine, other_half in zip(names, halves, theirs):
            g_shard[n], delta[n], new_m[n], new_v[n] = _adamw_halves(w[n], mine, other_half, m[n], v[n], "adamw_" + n)

    recv_early, _ = lax.optimization_barrier((list(recv_early), grads["w_fox_t"][0]))
    finish(W_NAMES[1:], parts[1:], recv_early, after=parts[0], collective_id=SWAP_HALVES_ID)
    (recv_in, small_all), _ = lax.optimization_barrier(((recv_in, small_all), [delta[n] for n in W_NAMES[1:]]))
    finish(W_NAMES[:1], parts[:1], [recv_in])
    small_sum = _sum_small(small_all)
    loss = small_sum[LOSS_ROW, 0] * (0.5 / D)
    ones = jnp.ones((1, 128), f32)
    sd, sm, sv = _adamw(stack(small_w, ones), small_sum, stack(small_m, ones), stack(small_v, ones), "adamw_small")

    outs = [loss, grad_x[None]]
    for big, st in ((g_shard, small_sum), (delta, sd), (new_m, sm), (new_v, sv)):
        t = {n: _harness_layout(n, big[n]) for n in W_NAMES}
        t["b_forget"] = st[0:1, :N_FOX]
        for i, n in enumerate(NORMS):
            t[n] = st[i + 1:i + 2]
        outs += [t[n] for n in ORDER]
    return tuple(outs)
```

```python
import functools
import math

import jax
import jax.numpy as jnp
import numpy as np
from jax import lax
from jax.experimental import pallas as pl
from jax.experimental.pallas import tpu as pltpu
from jax.experimental.pallas import tpu_sc as plsc

f32 = jnp.float32
bf16 = jnp.bfloat16
SDS = jax.ShapeDtypeStruct
MESH = pl.DeviceIdType.MESH

S = 2048
D = 1024
HD = 64
BLK = 128
N_FOX = 8
FOX_W = N_FOX * HD
DIL_GROUPS = ((128, 1), (512, 4), (2048, 16))
SLOTS = 4
DIL_W = SLOTS * HD
QKV_W = 3 * DIL_W
VR_W = 3 * FOX_W
GF_W = 2 * D + 128
F_FF = 2816
ROPE_DIM = 16
ROPE_THETA = 500000.0
EPS = 1e-6
NEG = -1e30
SCALE = 1.0 / math.sqrt(HD)
IN_COLS = 5896
N_SHARD = 4

ADAM_LR, ADAM_B1, ADAM_B2, ADAM_EPS, ADAM_WD, ADAM_STEP = 0.001, 0.9, 0.999, 1e-08, 0.01, 10

VMEM_V7X = 64 * 1024 * 1024
VMEM_PLAN_MAX = VMEM_V7X - 8 * 1024 * 1024

TM = 512
TQ = 256

W_NAMES = ("w_in", "w_proj_a", "w_proj_b", "w_out", "w_ffn_gate", "w_ffn_up", "w_ffn_down")
TRANSPOSED = ("w_in", "w_ffn_gate", "w_ffn_up")
IN_SHARD = IN_COLS // N_SHARD
IN_SHARD_PAD = 1504
SHARD_SHAPE = dict(w_in=(IN_SHARD_PAD, D), w_proj_a=(DIL_W, D // N_SHARD), w_proj_b=(FOX_W, D // N_SHARD),
                   w_out=(D // N_SHARD, D), w_ffn_gate=(F_FF // N_SHARD, D), w_ffn_up=(F_FF // N_SHARD, D),
                   w_ffn_down=(F_FF // N_SHARD, D))
SMALL_ROWS = 8
LOSS_ROW = 5


def _nbytes(shape, dtype):
    return math.prod(shape) * jnp.dtype(dtype).itemsize


def _params(semantics, block_bytes, temp_bytes=0):
    need = 2 * block_bytes + temp_bytes + (2 << 20)
    return pltpu.CompilerParams(dimension_semantics=semantics, vmem_limit_bytes=int(min(need, VMEM_PLAN_MAX)))


def _row(w, tm=TM):
    return pl.BlockSpec((tm, w), lambda i: (i, 0))


def _vec(w):
    return pl.BlockSpec((1, w), lambda i: (0, 0))


def _mm(pairs, dims, out_dtype, *, tm, tn, name, m_inner=False):
    a0, b0 = pairs[0]
    m_dim = a0.shape[1] if dims == "tn" else a0.shape[0]
    n_dim = b0.shape[0] if dims == "nt" else b0.shape[1]
    contract = {"nn": ((1,), (0,)), "nt": ((1,), (1,)), "tn": ((0,), (0,))}[dims]
    n_pairs = len(pairs)
    assert m_dim % tm == 0 and n_dim % tn == 0, (name, m_dim, n_dim, tm, tn)

    def body(*refs):
        o_ref = refs[-1]
        acc = None
        for p in range(n_pairs):
            a = refs[2 * p][...].astype(bf16)
            b = refs[2 * p + 1][...].astype(bf16)
            t = lax.dot_general(a, b, (contract, ((), ())), preferred_element_type=f32)
            acc = t if acc is None else acc + t
        o_ref[...] = acc.astype(o_ref.dtype)

    if m_inner:
        grid = (n_dim // tn, m_dim // tm)
        mi = lambda j, i: i
        ni = lambda j, i: j
    else:
        grid = (m_dim // tm, n_dim // tn)
        mi = lambda i, j: i
        ni = lambda i, j: j
    in_specs, block_bytes, args = [], 0, []
    for a, b in pairs:
        k_dim = a.shape[0] if dims == "tn" else a.shape[1]
        if dims == "tn":
            in_specs.append(pl.BlockSpec((k_dim, tm), lambda *g: (0, mi(*g))))
        else:
            in_specs.append(pl.BlockSpec((tm, k_dim), lambda *g: (mi(*g), 0)))
        if dims == "nt":
            in_specs.append(pl.BlockSpec((tn, k_dim), lambda *g: (ni(*g), 0)))
        else:
            in_specs.append(pl.BlockSpec((k_dim, tn), lambda *g: (0, ni(*g))))
        block_bytes += _nbytes((tm, k_dim), a.dtype) + _nbytes((tn, k_dim), b.dtype)
        args += [a, b]
    block_bytes += _nbytes((tm, tn), out_dtype)
    temp = _nbytes((tm, tn), f32) * 2 + sum(_nbytes((tm, a.shape[0] if dims == "tn" else a.shape[1]), bf16)
                                            + _nbytes((tn, a.shape[0] if dims == "tn" else a.shape[1]), bf16)
                                            for a, _ in pairs)
    return pl.pallas_call(
        body, grid=grid, in_specs=in_specs,
        out_specs=pl.BlockSpec((tm, tn), lambda *g: (mi(*g), ni(*g))),
        out_shape=SDS((m_dim, n_dim), out_dtype), name=name,
        compiler_params=_params(("parallel", "parallel"), block_bytes, temp),
    )(*args)


def _rms(x, g):
    r = lax.rsqrt(jnp.mean(x * x, axis=-1, keepdims=True) + EPS)
    return x * r * g


def _rms_bwd(x, g, dy):
    r = lax.rsqrt(jnp.mean(x * x, axis=-1, keepdims=True) + EPS)
    xh = x * r
    dxh = dy * g
    dx = r * (dxh - xh * jnp.mean(dxh * xh, axis=-1, keepdims=True))
    return dx, jnp.sum(dy * xh, axis=0, keepdims=True)


def _acc_rows(ref, val):
    @pl.when(pl.program_id(0) == 0)
    def _():
        ref[...] = jnp.zeros_like(ref)
    ref[...] += val


def _norm_fwd(xs, g):
    n = len(xs)

    def body(*refs):
        g = refs[n][...]
        for x_ref, h_ref in zip(refs[:n], refs[n + 1:]):
            h_ref[...] = _rms(x_ref[...], g).astype(bf16)

    return pl.pallas_call(
        body, grid=(S // TM,), in_specs=[_row(D)] * n + [_vec(D)], out_specs=[_row(D)] * n,
        out_shape=[SDS((S, D), bf16)] * n, name="norm_mix_pre",
        compiler_params=_params(("parallel",), 6 * n * TM * D, 8 * n * TM * D))(*xs, g)


def _perm_rows(xs, ds, name):
    n = len(xs)

    def body(*refs):
        outs = iter(refs[n:])
        for x_ref in refs[:n]:
            for d in ds:
                o_ref, rows = next(outs), S // d
                for r in range(d):
                    o_ref[r * rows:(r + 1) * rows, :] = x_ref[pl.ds(r, rows, stride=d), :]

    blk = pl.BlockSpec((S, 128), lambda c: (0, c))
    w = xs[0].shape[1]
    return pl.pallas_call(
        body, grid=(w // 128,), in_specs=[blk] * n, out_specs=[blk] * (n * len(ds)),
        out_shape=[SDS((S, w), f32)] * (n * len(ds)), name=name,
        compiler_params=_params(("parallel",), 4 * S * 128 * n * (1 + len(ds))))(*xs)


def _unperm_sum(nat, perms, ds, name):
    n = len(perms)

    def body(*refs):
        a_ref, o_ref, sc = refs[0], refs[n + 1], refs[n + 2]
        acc = a_ref[...]
        for b_ref, d in zip(refs[1:n + 1], ds):
            rows = S // d
            for r in range(d):
                sc[pl.ds(r, rows, stride=d), :] = b_ref[r * rows:(r + 1) * rows, :]
            acc = acc + sc[...]
        o_ref[...] = acc

    blk = pl.BlockSpec((S, 128), lambda c: (0, c))
    w = nat.shape[1]
    return pl.pallas_call(
        body, grid=(w // 128,), in_specs=[blk] * (n + 1), out_specs=blk, out_shape=SDS((S, w), f32),
        scratch_shapes=[pltpu.VMEM((S, 128), f32)], name=name,
        compiler_params=_params(("parallel",), 4 * S * 128 * (n + 2), 8 * S * 128))(nat, *perms)


def _whole(a):
    return pl.BlockSpec(a.shape, lambda i: (0,) * a.ndim)


def _resid_norm_fwd(x, merged, w_out, g_post, g_pre):
    def body(x_ref, mg_ref, w_ref, gp_ref, gn_ref, mix_ref, x2_ref, h_ref):
        mix = jnp.dot(mg_ref[...], w_ref[...], preferred_element_type=f32)
        x2 = x_ref[...] + _rms(mix, gp_ref[...])
        mix_ref[...] = mix
        x2_ref[...] = x2
        h_ref[...] = _rms(x2, gn_ref[...]).astype(bf16)

    return pl.pallas_call(
        body, grid=(S // TM,), in_specs=[_row(D), _row(D), _whole(w_out), _vec(D), _vec(D)], out_specs=[_row(D)] * 3,
        out_shape=[SDS((S, D), f32), SDS((S, D), f32), SDS((S, D), bf16)], name="proj_out_norm",
        compiler_params=_params(("parallel",), 16 * TM * D + 2 * D * D, 16 * TM * D))(x, merged, w_out, g_post, g_pre)


def _loss_head(x2, a_act, w_down, g_post, target):
    def body(x2_ref, a_ref, w_ref, g_ref, t_ref, loss_ref, dy_ref, dff_ref, dg_ref):
        ff = jnp.dot(a_ref[...], w_ref[...], preferred_element_type=f32)
        g = g_ref[...]
        err = x2_ref[...] + _rms(ff, g) - t_ref[...]
        dy = err * (1.0 / D)
        dff, dg = _rms_bwd(ff, g, dy)
        dy_ref[...] = dy
        dff_ref[...] = dff.astype(bf16)
        _acc_rows(dg_ref, dg)
        _acc_rows(loss_ref, jnp.full((1, 128), jnp.sum(err * err), f32))

    return pl.pallas_call(
        body, grid=(S // TM,), in_specs=[_row(D), _row(F_FF), _whole(w_down), _vec(D), _row(D)],
        out_specs=[_vec(128), _row(D), _row(D), _vec(D)],
        out_shape=[SDS((1, 128), f32), SDS((S, D), f32), SDS((S, D), bf16), SDS((1, D), f32)], name="ffn_down_loss",
        compiler_params=_params(("arbitrary",), 14 * TM * D + 2 * TM * F_FF + 2 * F_FF * D, 28 * TM * D),
    )(x2, a_act, w_down, g_post, target)


def _norm_bwd_mid(dy, d_g, d_u, w_gate_t, w_up_t, x2, mix, g_ffn_pre, g_mix_post):
    def body(dy_ref, dgt_ref, dut_ref, wg_ref, wu_ref, x2_ref, mix_ref, g3_ref, g2_ref, dx2_ref, dmix_ref, dg3_ref, dg2_ref):
        dh = jnp.dot(dgt_ref[...], wg_ref[...], preferred_element_type=f32)
        dh += jnp.dot(dut_ref[...], wu_ref[...], preferred_element_type=f32)
        d3, dg3 = _rms_bwd(x2_ref[...], g3_ref[...], dh)
        dx2 = dy_ref[...] + d3
        dmix, dg2 = _rms_bwd(mix_ref[...], g2_ref[...], dx2)
        dx2_ref[...] = dx2
        dmix_ref[...] = dmix.astype(bf16)
        _acc_rows(dg3_ref, dg3)
        _acc_rows(dg2_ref, dg2)

    tm = TM // 2
    row = lambda w: _row(w, tm)
    return pl.pallas_call(
        body, grid=(S // tm,),
        in_specs=[row(D), row(F_FF), row(F_FF), _whole(w_gate_t), _whole(w_up_t), row(D), row(D), _vec(D), _vec(D)],
        out_specs=[row(D), row(D), _vec(D), _vec(D)],
        out_shape=[SDS((S, D), f32), SDS((S, D), bf16), SDS((1, D), f32), SDS((1, D), f32)], name="ffn_bwd_in_norm",
        compiler_params=_params(("arbitrary",), 18 * tm * D + 4 * tm * F_FF + 4 * F_FF * D, 28 * tm * D),
    )(dy, d_g, d_u, w_gate_t, w_up_t, x2, mix, g_ffn_pre, g_mix_post)


def _norm_bwd_in(dx2, dh1, x, g):
    def body(dx2_ref, dh_ref, x_ref, g_ref, gx_ref, dg_ref):
        d1, dg = _rms_bwd(x_ref[...], g_ref[...], dh_ref[...])
        gx_ref[...] = dx2_ref[...] + d1
        _acc_rows(dg_ref, dg)

    return pl.pallas_call(
        body, grid=(S // TM,), in_specs=[_row(D)] * 3 + [_vec(D)], out_specs=[_row(D), _vec(D)],
        out_shape=[SDS((S, D), f32), SDS((1, D), f32)], name="norm_bwd_in",
        compiler_params=_params(("arbitrary",), 16 * TM * D, 16 * TM * D))(dx2, dh1, x, g)


def _rope_tables():
    half = ROPE_DIM // 2
    inv_freq = np.power(np.float32(ROPE_THETA), -np.arange(0, ROPE_DIM, 2, dtype=np.float32) / np.float32(ROPE_DIM))
    row = np.arange(S)
    groups = []
    for _, d in DIL_GROUPS:
        pos = ((row % (S // d)) * d + row // (S // d)).astype(np.float32)
        ang = pos[:, None] * inv_freq[None, :].astype(np.float32)
        cos, sin = np.cos(ang).astype(np.float32), np.sin(ang).astype(np.float32)
        c = np.concatenate([cos, cos, np.ones((S, HD - ROPE_DIM), np.float32)], axis=1)
        s_lo = np.concatenate([-sin, np.zeros((S, HD - half), np.float32)], axis=1)
        s_hi = np.concatenate([np.zeros((S, half), np.float32), sin, np.zeros((S, HD - ROPE_DIM), np.float32)], axis=1)
        groups.append(np.stack([np.concatenate([t, t], axis=1) for t in (c, s_lo, s_hi)]))
    return jnp.asarray(np.stack(groups))


def _rotate(x, c, lo, hi, sign):
    tile = lambda t: jnp.tile(t, (1, DIL_W // 128))
    return (x * tile(c) + pltpu.roll(x, DIL_W - ROPE_DIM // 2, 1) * (tile(lo) * sign)
            + pltpu.roll(x, ROPE_DIM // 2, 1) * (tile(hi) * sign))


def _table_specs(g):
    return [pl.BlockSpec((None, None, TM, 128), lambda i, k=k: (g, k, i, 0)) for k in range(3)]


def _rope_fwd(g, p_qkv, tables):
    def body(x_ref, c_ref, lo_ref, hi_ref, o_ref):
        c, lo, hi = c_ref[...], lo_ref[...], hi_ref[...]
        for part in range(2):
            cols = slice(part * DIL_W, (part + 1) * DIL_W)
            o_ref[:, cols] = _rotate(x_ref[:, cols], c, lo, hi, 1.0).astype(bf16)
        o_ref[:, 2 * DIL_W:] = x_ref[:, 2 * DIL_W:].astype(bf16)

    return pl.pallas_call(
        body, grid=(S // TM,), in_specs=[_row(QKV_W)] + _table_specs(g), out_specs=_row(QKV_W),
        out_shape=SDS((S, QKV_W), bf16), name=f"rope_fwd_{g}",
        compiler_params=_params(("parallel",), 6 * TM * QKV_W + 12 * TM * 128, 24 * TM * QKV_W))(p_qkv, tables, tables, tables)


def _rope_bwd(g, dq, dk, dv, tables):
    def body(dq_ref, dk_ref, dv_ref, c_ref, lo_ref, hi_ref, o_ref):
        c, lo, hi = c_ref[...], lo_ref[...], hi_ref[...]
        o_ref[:, :DIL_W] = _rotate(dq_ref[...], c, lo, hi, -1.0).astype(bf16)
        o_ref[:, DIL_W:2 * DIL_W] = _rotate(dk_ref[...], c, lo, hi, -1.0).astype(bf16)
        o_ref[:, 2 * DIL_W:] = dv_ref[...].astype(bf16)

    return pl.pallas_call(
        body, grid=(S // TM,), in_specs=[_row(DIL_W)] * 3 + _table_specs(g), out_specs=_row(QKV_W),
        out_shape=SDS((S, QKV_W), bf16), name=f"rope_bwd_{g}",
        compiler_params=_params(("parallel",), 6 * TM * QKV_W + 12 * TM * 128, 24 * TM * QKV_W))(dq, dk, dv, tables, tables, tables)


def _nt(a, b):
    return lax.dot_general(a, b, (((1,), (1,)), ((), ())), preferred_element_type=f32)


def _tn(a, b):
    return lax.dot_general(a, b, (((0,), (0,)), ((), ())), preferred_element_type=f32)


STEP_BLOCKS = 4
STEP_ROWS = STEP_BLOCKS * BLK


def _dil_prev(g, b):
    _, d = DIL_GROUPS[g]
    nb = S // d // BLK
    if nb == 1 or (b == 0 and nb <= STEP_BLOCKS):
        return None
    return "in" if b > 0 else "halo"


def _bnt(a, b):
    return lax.dot_general(a, b, (((2,), (2,)), ((0,), (0,))), preferred_element_type=f32)


def _bnn(a, b):
    return lax.dot_general(a, b, (((2,), (1,)), ((0,), (0,))), preferred_element_type=f32)


def _btn(a, b):
    return lax.dot_general(a, b, (((1,), (1,)), ((0,), (0,))), preferred_element_type=f32)


def _on_tail(x, tail, fn):
    if tail == x.shape[0]:
        return fn(x)
    return jnp.concatenate([x[:-tail], fn(x[-tail:])], axis=0)


def _heads(ref, part):
    n = ref.shape[0] // BLK
    return jnp.stack([ref[b * BLK:(b + 1) * BLK, part * DIL_W + h * HD:part * DIL_W + (h + 1) * HD]
                      for b in range(n) for h in range(SLOTS)])


def _dil_operands(g, qkv_ref, halo_ref):
    q, kc, vc = (_heads(qkv_ref, part) for part in range(3))
    qi = lax.broadcasted_iota(jnp.int32, (1, BLK, BLK), 1)
    kj = lax.broadcasted_iota(jnp.int32, (1, BLK, BLK), 2)
    with_prev = [b for b in range(STEP_BLOCKS) if _dil_prev(g, b) is not None]
    tail = SLOTS * len(with_prev)
    if not tail:
        return q, kc, vc, None, None, kj <= qi, None, 0
    assert with_prev == list(range(STEP_BLOCKS - len(with_prev), STEP_BLOCKS))
    inside = SLOTS * sum(_dil_prev(g, b) == "in" for b in with_prev)
    kp, vp, prev = kc[:inside], vc[:inside], jnp.broadcast_to(kj >= qi, (inside, BLK, BLK))
    if inside < tail:
        no_halo = jnp.where(pl.program_id(0) == 0, BLK + 1, 0)
        kp = jnp.concatenate([_heads(halo_ref, 1), kp], axis=0)
        vp = jnp.concatenate([_heads(halo_ref, 2), vp], axis=0)
        prev = jnp.concatenate([jnp.broadcast_to(kj >= qi + no_halo, (SLOTS, BLK, BLK)), prev], axis=0)
    return q, kc, vc, kp, vp, kj <= qi, prev, tail


def _dil_in_specs(g, n_aux):
    step = lambda w: pl.BlockSpec((STEP_ROWS, w), lambda i: (i, 0))
    halo = [pl.BlockSpec((BLK, QKV_W), lambda i: (jnp.maximum(i * STEP_BLOCKS - 1, 0), 0))]
    needs_halo = _dil_prev(g, 0) == "halo"
    return [step(QKV_W)] + (halo if needs_halo else []) + [step(DIL_W)] * n_aux, needs_halo


def _dil_fwd(g, qkv):
    in_specs, needs_halo = _dil_in_specs(g, 0)

    def body(*refs):
        qkv_ref, halo_ref = refs[0], refs[1] if needs_halo else None
        o_ref, lse_ref = refs[-2:]
        q, kc, vc, kp, vp, cur, prev, tail = _dil_operands(g, qkv_ref, halo_ref)
        sc = jnp.where(cur, _bnt(q, kc) * SCALE, NEG)
        m = jnp.max(sc, axis=-1, keepdims=True)
        if tail:
            sp = jnp.where(prev, _bnt(q[-tail:], kp) * SCALE, NEG)
            m = _on_tail(m, tail, lambda t: jnp.maximum(t, jnp.max(sp, axis=-1, keepdims=True)))
            pp = jnp.exp(sp - m[-tail:])
        pc = jnp.exp(sc - m)
        den = jnp.sum(pc, axis=-1, keepdims=True)
        if tail:
            den = _on_tail(den, tail, lambda t: t + jnp.sum(pp, axis=-1, keepdims=True))
        inv = 1.0 / den
        o = _bnn((pc * inv).astype(bf16), vc)
        if tail:
            o = _on_tail(o, tail, lambda t: t + _bnn((pp * inv[-tail:]).astype(bf16), vp))
        lse = m + jnp.log(den)
        for b in range(STEP_BLOCKS):
            for h in range(SLOTS):
                rows, hs = slice(b * BLK, (b + 1) * BLK), slice(h * HD, (h + 1) * HD)
                o_ref[rows, hs] = o[SLOTS * b + h]
                lse_ref[rows, hs] = jnp.broadcast_to(lse[SLOTS * b + h], (BLK, HD))

    out = pl.BlockSpec((STEP_ROWS, DIL_W), lambda i: (i, 0))
    return pl.pallas_call(
        body, grid=(S // STEP_ROWS,), in_specs=in_specs, out_specs=[out, out], out_shape=[SDS((S, DIL_W), f32)] * 2,
        name=f"dil_fwd_{g}", compiler_params=_params(("parallel",), 12 * STEP_ROWS * DIL_W, 2 << 20),
    )(*([qkv] * (2 if needs_halo else 1)))


def _dil_combine(outs, lses):
    def body(o0, o1, o2, l0, l1, l2, out_ref, lse_ref, so1, so2, sl1, sl2):
        for (_, d), src, dst in ((DIL_GROUPS[1], o1, so1), (DIL_GROUPS[2], o2, so2),
                                 (DIL_GROUPS[1], l1, sl1), (DIL_GROUPS[2], l2, sl2)):
            rows = S // d
            for r in range(d):
                dst[pl.ds(r, rows, stride=d), :] = src[r * rows:(r + 1) * rows, :]
        a, b, c = l0[...], sl1[...], sl2[...]
        m = jnp.maximum(jnp.maximum(a, b), c)
        ea, eb, ec = jnp.exp(a - m), jnp.exp(b - m), jnp.exp(c - m)
        z = ea + eb + ec
        inv = 1.0 / z
        out_ref[...] = (ea * inv) * o0[...] + (eb * inv) * so1[...] + (ec * inv) * so2[...]
        lse_ref[...] = m + jnp.log(z)

    blk = pl.BlockSpec((S, 128), lambda c: (0, c))
    return pl.pallas_call(
        body, grid=(DIL_W // 128,), in_specs=[blk] * 6, out_specs=[blk] * 2,
        out_shape=[SDS((S, DIL_W), f32)] * 2, scratch_shapes=[pltpu.VMEM((S, 128), f32)] * 4, name="dil_combine",
        compiler_params=_params(("parallel",), 32 * S * 128, 32 * S * 128))(*outs, *lses)


def _dil_bwd(g, qkv, d_out, delta, lse):
    in_specs, needs_halo = _dil_in_specs(g, 3)

    def body(*refs):
        qkv_ref, halo_ref = refs[0], refs[1] if needs_halo else None
        do_ref, dl_ref, lse_ref, dq_ref, dk_ref, dv_ref = refs[-6:]
        q, kc, vc, kp, vp, cur, prev, tail = _dil_operands(g, qkv_ref, halo_ref)
        tiles = [(slice(b * BLK, (b + 1) * BLK), h) for b in range(STEP_BLOCKS) for h in range(SLOTS)]
        do = jnp.stack([do_ref[rows, h * HD:(h + 1) * HD] for rows, h in tiles]).astype(bf16)
        lse = jnp.stack([lse_ref[rows, h * HD:h * HD + 1] for rows, h in tiles])
        delta = jnp.stack([dl_ref[rows, h * HD:h * HD + 1] for rows, h in tiles])

        def probs(q, k, mask, lse, do, v, delta):
            p = jnp.exp(jnp.where(mask, _bnt(q, k) * SCALE, NEG) - lse)
            ds = p * (_bnt(do, v) - delta) * SCALE
            return p.astype(bf16), ds.astype(bf16)

        p, ds = probs(q, kc, cur, lse, do, vc, delta)
        dq, dk, dv = _bnn(ds, kc), _btn(ds, q), _btn(p, do)
        if tail:
            p, ds = probs(q[-tail:], kp, prev, lse[-tail:], do[-tail:], vp, delta[-tail:])
            dq = _on_tail(dq, tail, lambda t: t + _bnn(ds, kp))
            dk_p, dv_p = _btn(ds, q[-tail:]), _btn(p, do[-tail:])
            inside = tail - SLOTS if needs_halo else tail
            pad = jnp.zeros((len(tiles) - inside, BLK, HD), f32)
            dk = dk + jnp.concatenate([dk_p[tail - inside:], pad], axis=0)
            dv = dv + jnp.concatenate([dv_p[tail - inside:], pad], axis=0)
        first = pl.multiple_of(pl.program_id(0) * STEP_ROWS, STEP_ROWS)
        for t, (rows, h) in enumerate(tiles):
            hs = slice(h * HD, (h + 1) * HD)
            own = pl.ds(pl.multiple_of(first + rows.start, BLK), BLK)
            dq_ref[rows, hs] = dq[t]
            dk_ref[own, hs] = dk[t]
            dv_ref[own, hs] = dv[t]
        if needs_halo:
            before = pl.ds(pl.multiple_of(jnp.maximum(first - BLK, 0), BLK), BLK)
            for h in range(SLOTS):
                hs = slice(h * HD, (h + 1) * HD)
                dk_ref[before, hs] += dk_p[h]
                dv_ref[before, hs] += dv_p[h]

    whole = pl.BlockSpec((S, DIL_W), lambda i: (0, 0))
    return pl.pallas_call(
        body, grid=(S // STEP_ROWS,), in_specs=in_specs,
        out_specs=[pl.BlockSpec((STEP_ROWS, DIL_W), lambda i: (i, 0)), whole, whole],
        out_shape=[SDS((S, DIL_W), f32)] * 3, name=f"dil_bwd_{g}",
        compiler_params=_params(("arbitrary",), 20 * STEP_ROWS * DIL_W + 8 * S * DIL_W, 2 << 20),
    )(*([qkv] * (2 if needs_halo else 1)), d_out, delta, lse)


def _scan_rows(x, reverse):
    row = lax.broadcasted_iota(jnp.int32, x.shape, 0)
    k = 1
    while k < S:
        if reverse:
            x = x + jnp.where(row < S - k, pltpu.roll(x, S - k, 0), 0.0)
        else:
            x = x + jnp.where(row >= k, pltpu.roll(x, k, 0), 0.0)
        k *= 2
    return x


N_PAIR = N_FOX // 2
_PAIR_Q = pl.BlockSpec((None, S, 128), lambda p: (p, 0, 0))
_PAIR_K = pl.BlockSpec((None, 8, S), lambda p: (p, 0, 0))


def _forget_fwd(fz, b128):
    def body(z_ref, b_ref, fq_ref, fk_ref):
        z = z_ref[...] + b_ref[...]
        logf = jnp.minimum(z, 0.0) - jnp.log1p(jnp.exp(-jnp.abs(z)))
        f_cum = _scan_rows(logf, reverse=False)
        f_cum_t = f_cum.T
        fq_ref[...] = jnp.zeros_like(fq_ref)
        fk_ref[...] = jnp.zeros_like(fk_ref)
        for p in range(N_PAIR):
            fq_ref[p, :, 0:2] = f_cum[:, 2 * p:2 * p + 2]
            fk_ref[p, 0:2, :] = f_cum_t[2 * p:2 * p + 2, :]

    return pl.pallas_call(
        body, grid=(1,), in_specs=[pl.BlockSpec((S, 128), lambda i: (0, 0)), _vec(128)],
        out_specs=[pl.BlockSpec((N_PAIR, S, 128), lambda i: (0, 0, 0)), pl.BlockSpec((N_PAIR, 8, S), lambda i: (0, 0, 0))],
        out_shape=[SDS((N_PAIR, S, 128), f32), SDS((N_PAIR, 8, S), f32)], name="forget_fwd",
        compiler_params=_params(("arbitrary",), 24 * S * 128, 24 * S * 128))(fz, b128)


def _forget_bwd(fz, b128, d_f_cols, d_f_rows):
    def body(z_ref, b_ref, dfc_ref, dfr_ref, dz_ref, db_ref, df_sc):
        z = z_ref[...] + b_ref[...]
        df_sc[...] = jnp.zeros_like(df_sc)
        for p in range(N_PAIR):
            df_sc[:, 2 * p:2 * p + 2] = dfr_ref[p, :, 0:2] + dfc_ref[p].T[:, 0:2]
        dz = _scan_rows(df_sc[...], reverse=True) * jax.nn.sigmoid(-z)
        dz_ref[...] = dz
        db_ref[...] = jnp.sum(dz, axis=0, keepdims=True)

    full = pl.BlockSpec((S, 128), lambda i: (0, 0))
    return pl.pallas_call(
        body, grid=(1,),
        in_specs=[full, _vec(128), pl.BlockSpec((N_PAIR, 8, S), lambda i: (0, 0, 0)), pl.BlockSpec((N_PAIR, S, 128), lambda i: (0, 0, 0))],
        out_specs=[full, _vec(128)], out_shape=[SDS((S, 128), f32), SDS((1, 128), f32)],
        scratch_shapes=[pltpu.VMEM((S, 128), f32)], name="forget_bwd",
        compiler_params=_params(("arbitrary",), 32 * S * 128, 24 * S * 128))(fz, b128, d_f_cols, d_f_rows)


def _fox_scores(q_ref, k_ref, fq_ref, fk_ref, qi, hh):
    n = (qi + 1) * TQ
    rows, hs = slice(qi * TQ, n), slice(hh * HD, (hh + 1) * HD)
    q = q_ref[rows, hs] * SCALE
    s = _nt(q, k_ref[0:n, hs]) + (fq_ref[rows, hh:hh + 1] - fk_ref[hh:hh + 1, 0:n])
    below = lax.broadcasted_iota(jnp.int32, (TQ, TQ), 1) <= lax.broadcasted_iota(jnp.int32, (TQ, TQ), 0)
    diag = jnp.where(below, s[:, n - TQ:], NEG)
    return diag if qi == 0 else jnp.concatenate([s[:, :n - TQ], diag], axis=1)


def _pair_cols(first):
    return pl.BlockSpec((S, 128), lambda p: (0, first + p))


def _fox_fwd(vr, fq, fk):
    def body(q_ref, k_ref, v_ref, fq_ref, fk_ref, o_ref, lse_ref):
        lse_ref[...] = jnp.zeros_like(lse_ref)
        for hh in range(2):
            hs = slice(hh * HD, (hh + 1) * HD)
            for qi in range(S // TQ):
                n = (qi + 1) * TQ
                rows = slice(qi * TQ, n)
                s = _fox_scores(q_ref, k_ref, fq_ref, fk_ref, qi, hh)
                m = jnp.max(s, axis=-1, keepdims=True)
                p = jnp.exp(s - m)
                den = jnp.sum(p, axis=-1, keepdims=True)
                o_ref[rows, hs] = jnp.dot((p * (1.0 / den)).astype(bf16), v_ref[0:n, hs], preferred_element_type=f32)
                lse_ref[rows, hh:hh + 1] = m + jnp.log(den)

    return pl.pallas_call(
        body, grid=(N_PAIR,), in_specs=[_pair_cols(0), _pair_cols(N_PAIR), _pair_cols(2 * N_PAIR), _PAIR_Q, _PAIR_K],
        out_specs=[_pair_cols(0), _PAIR_Q], out_shape=[SDS((S, FOX_W), f32), SDS((N_PAIR, S, 128), f32)],
        name="fox_fwd", compiler_params=_params(("parallel",), 12 * S * 128, 16 * TQ * S),
    )(vr, vr, vr, fq, fk)


def _fox_bwd(vr, fq, fk, lse, d_out, delta):
    def body(q_ref, k_ref, v_ref, do_ref, fq_ref, fk_ref, lse_ref, dl_ref, dq_ref, dk_ref, dv_ref, dfc_ref, dfr_ref,
             dk_sc, dv_sc):
        dfc_ref[...] = jnp.zeros_like(dfc_ref)
        dfr_ref[...] = jnp.zeros_like(dfr_ref)
        for hh in range(2):
            hs = slice(hh * HD, (hh + 1) * HD)
            dk_sc[...] = jnp.zeros_like(dk_sc)
            dv_sc[...] = jnp.zeros_like(dv_sc)
            for qi in range(S // TQ):
                n = (qi + 1) * TQ
                rows = slice(qi * TQ, n)
                q, do, k, v = q_ref[rows, hs], do_ref[rows, hs], k_ref[0:n, hs], v_ref[0:n, hs]
                p = jnp.exp(_fox_scores(q_ref, k_ref, fq_ref, fk_ref, qi, hh) - lse_ref[rows, hh:hh + 1])
                ds = p * (_nt(do, v) - dl_ref[rows, hh:hh + 1])
                dsb = ds.astype(bf16)
                dq_ref[rows, hs] = jnp.dot(dsb, k, preferred_element_type=f32) * SCALE
                dk_sc[0:n, :] += _tn(dsb, q) * SCALE
                dv_sc[0:n, :] += _tn(p.astype(bf16), do)
                dfc_ref[hh:hh + 1, 0:n] -= jnp.sum(ds, axis=0, keepdims=True)
                dfr_ref[rows, hh:hh + 1] = jnp.sum(ds, axis=-1, keepdims=True)
            dk_ref[:, hs] = dk_sc[...]
            dv_ref[:, hs] = dv_sc[...]

    cols = [_pair_cols(k * N_PAIR) for k in range(3)]
    return pl.pallas_call(
        body, grid=(N_PAIR,), in_specs=cols + [_pair_cols(0), _PAIR_Q, _PAIR_K, _PAIR_Q, _PAIR_Q],
        out_specs=[_pair_cols(0)] * 3 + [_PAIR_K, _PAIR_Q],
        out_shape=[SDS((S, FOX_W), f32)] * 3 + [SDS((N_PAIR, 8, S), f32), SDS((N_PAIR, S, 128), f32)],
        scratch_shapes=[pltpu.VMEM((S, HD), f32)] * 2, name="fox_bwd",
        compiler_params=_params(("parallel",), 32 * S * 128, 24 * TQ * S),
    )(vr, vr, vr, d_out, fq, fk, lse, delta)


def _merge_fwd(out_a, out_b, w_a, w_b, gf):
    cw = D // N_SHARD

    def body(oa_ref, ob_ref, wa_ref, wb_ref, ga_ref, gb_ref, ya_ref, yb_ref, mg_ref):
        oa, ob = oa_ref[...].astype(bf16), ob_ref[...].astype(bf16)
        for j in range(N_SHARD):
            cols = slice(j * cw, (j + 1) * cw)
            ya = jnp.dot(oa, wa_ref[j], preferred_element_type=f32)
            yb = jnp.dot(ob, wb_ref[j], preferred_element_type=f32)
            ya_ref[:, cols] = ya
            yb_ref[:, cols] = yb
            mg_ref[:, cols] = (jax.nn.sigmoid(ga_ref[:, cols]) * ya + jax.nn.sigmoid(gb_ref[:, cols]) * yb).astype(bf16)

    full = lambda a: pl.BlockSpec(a.shape, lambda i: (0, 0, 0))
    return pl.pallas_call(
        body, grid=(S // TM,),
        in_specs=[_row(DIL_W), _row(FOX_W), full(w_a), full(w_b), _row(D), pl.BlockSpec((TM, D), lambda i: (i, 1))],
        out_specs=[_row(D)] * 3, out_shape=[SDS((S, D), f32), SDS((S, D), f32), SDS((S, D), bf16)], name="merge_fwd",
        compiler_params=_params(("parallel",), 22 * TM * D + 2 * (DIL_W + FOX_W) * D, 16 * TM * D),
    )(out_a, out_b, w_a, w_b, gf, gf)


def _merge_bwd(d_mix, w_out, ya, yb, gf):
    def body(dx_ref, w_ref, ya_ref, yb_ref, ga_ref, gb_ref, dya_ref, dyb_ref, dg_ref):
        dm = _nt(dx_ref[...], w_ref[...])
        sa, sb = jax.nn.sigmoid(ga_ref[...]), jax.nn.sigmoid(gb_ref[...])
        dya_ref[...] = (dm * sa).astype(bf16)
        dyb_ref[...] = (dm * sb).astype(bf16)
        dg_ref[:, :D] = (dm * ya_ref[...] * sa * (1.0 - sa)).astype(bf16)
        dg_ref[:, D:] = (dm * yb_ref[...] * sb * (1.0 - sb)).astype(bf16)

    return pl.pallas_call(
        body, grid=(S // TM,),
        in_specs=[_row(D), _whole(w_out)] + [_row(D)] * 3 + [pl.BlockSpec((TM, D), lambda i: (i, 1))],
        out_specs=[_row(D), _row(D), _row(2 * D)],
        out_shape=[SDS((S, D), bf16), SDS((S, D), bf16), SDS((S, 2 * D), bf16)], name="proj_out_bwd_merge",
        compiler_params=_params(("parallel",), 26 * TM * D + 2 * D * D, 28 * TM * D))(d_mix, w_out, ya, yb, gf, gf)


def _branch_bwd(d_ya, d_yb, w_a, w_b, out_a, out_b):
    cw = D // N_SHARD

    def body(dya_ref, dyb_ref, wa_ref, wb_ref, oa_ref, ob_ref, doa_ref, dla_ref, dob_ref, dlb_ref):
        doa = jnp.zeros((TM, DIL_W), f32)
        dob = jnp.zeros((TM, FOX_W), f32)
        for j in range(N_SHARD):
            cols = slice(j * cw, (j + 1) * cw)
            doa += _nt(dya_ref[:, cols], wa_ref[j])
            dob += _nt(dyb_ref[:, cols], wb_ref[j])
        doa_ref[...] = doa
        dob_ref[...] = dob.astype(bf16)
        prod_a = doa * oa_ref[...]
        for h in range(SLOTS):
            hs = slice(h * HD, (h + 1) * HD)
            dla_ref[:, hs] = jnp.broadcast_to(jnp.sum(prod_a[:, hs], axis=-1, keepdims=True), (TM, HD))
        prod_b = dob * ob_ref[...]
        dlb_ref[...] = jnp.zeros_like(dlb_ref)
        for h in range(N_FOX):
            dlb_ref[h // 2, :, h % 2:h % 2 + 1] = jnp.sum(prod_b[:, h * HD:(h + 1) * HD], axis=-1, keepdims=True)

    full = lambda a: pl.BlockSpec(a.shape, lambda i: (0, 0, 0))
    return pl.pallas_call(
        body, grid=(S // TM,),
        in_specs=[_row(D), _row(D), full(w_a), full(w_b), _row(DIL_W), _row(FOX_W)],
        out_specs=[_row(DIL_W), _row(DIL_W), _row(FOX_W), pl.BlockSpec((N_PAIR, TM, 128), lambda i: (0, i, 0))],
        out_shape=[SDS((S, DIL_W), f32), SDS((S, DIL_W), f32), SDS((S, FOX_W), bf16), SDS((N_PAIR, S, 128), f32)],
        name="branch_bwd", compiler_params=_params(("parallel",), 8 * TM * D + 2 * (DIL_W + FOX_W) * D, 8 * TM * D),
    )(d_ya, d_yb, w_a, w_b, out_a, out_b)


def _branch_grads(out_a, out_b, d_ya, d_yb):
    cw = D // N_SHARD

    def body(oa_ref, ob_ref, dya_ref, dyb_ref, ga_ref, gb_ref):
        ga_ref[...] = _tn(oa_ref[...].astype(bf16), dya_ref[...]).astype(bf16)
        gb_ref[...] = _tn(ob_ref[...].astype(bf16), dyb_ref[...]).astype(bf16)

    whole = lambda w: pl.BlockSpec((S, w), lambda j: (0, 0))
    cols = pl.BlockSpec((S, cw), lambda j: (0, j))
    return pl.pallas_call(
        body, grid=(N_SHARD,), in_specs=[whole(DIL_W), whole(FOX_W), cols, cols],
        out_specs=[pl.BlockSpec((None, DIL_W, cw), lambda j: (j, 0, 0)), pl.BlockSpec((None, FOX_W, cw), lambda j: (j, 0, 0))],
        out_shape=[SDS((N_SHARD, DIL_W, cw), bf16), SDS((N_SHARD, FOX_W, cw), bf16)], name="grad_w_proj_ab",
        compiler_params=_params(("parallel",), 4 * S * (DIL_W + FOX_W) + 4 * S * cw + 4 * (DIL_W + FOX_W) * cw,
                                4 * S * (DIL_W + FOX_W)))(out_a, out_b, d_ya, d_yb)


FF_TN = F_FF // 2
FF_TM = 1024


def _ffn_fwd(h, w_gate_t, w_up_t):
    def body(h_ref, wg_ref, wu_ref, g_ref, u_ref, a_ref):
        hb = h_ref[...]
        g = _nt(hb, wg_ref[...])
        u = _nt(hb, wu_ref[...])
        g_ref[...] = g
        u_ref[...] = u
        a_ref[...] = (g * jax.nn.sigmoid(g) * u).astype(bf16)

    tile = pl.BlockSpec((FF_TM, FF_TN), lambda j, i: (i, j))
    wspec = pl.BlockSpec((FF_TN, D), lambda j, i: (j, 0))
    return pl.pallas_call(
        body, grid=(F_FF // FF_TN, S // FF_TM),
        in_specs=[pl.BlockSpec((FF_TM, D), lambda j, i: (i, 0)), wspec, wspec], out_specs=[tile] * 3,
        out_shape=[SDS((S, F_FF), f32), SDS((S, F_FF), f32), SDS((S, F_FF), bf16)], name="ffn_fwd",
        compiler_params=_params(("parallel", "parallel"), 2 * FF_TM * D + 4 * D * FF_TN + 10 * FF_TM * FF_TN, 16 * FF_TM * FF_TN),
    )(h, w_gate_t, w_up_t)


def _ffn_bwd_act(d_ff, w_down, g_act, u_act):
    def body(d_ref, wd_ref, g_ref, u_ref, dg_ref, du_ref):
        da = _nt(d_ref[...], wd_ref[...])
        g = g_ref[...]
        sg = jax.nn.sigmoid(g)
        du_ref[...] = (da * g * sg).astype(bf16)
        dg_ref[...] = (da * u_ref[...] * sg * (1.0 + g * (1.0 - sg))).astype(bf16)

    tile = pl.BlockSpec((FF_TM, FF_TN), lambda j, i: (i, j))
    return pl.pallas_call(
        body, grid=(F_FF // FF_TN, S // FF_TM),
        in_specs=[pl.BlockSpec((FF_TM, D), lambda j, i: (i, 0)), pl.BlockSpec((FF_TN, D), lambda j, i: (j, 0)), tile, tile],
        out_specs=[tile, tile], out_shape=[SDS((S, F_FF), bf16)] * 2, name="ffn_bwd_act",
        compiler_params=_params(("parallel", "parallel"), 2 * FF_TM * D + 2 * D * FF_TN + 12 * FF_TM * FF_TN, 16 * FF_TM * FF_TN),
    )(d_ff, w_down, g_act, u_act)


def _row_tile(rows):
    return next(t for t in (376, 128, 176, 64, 32, 16, 8) if rows % t == 0)


def _adamw_math(w, g, m, v):
    c1 = 1.0 - ADAM_B1 ** ADAM_STEP
    c2 = 1.0 - ADAM_B2 ** ADAM_STEP
    m_new = ADAM_B1 * m + (1.0 - ADAM_B1) * g
    v_new = ADAM_B2 * v + (1.0 - ADAM_B2) * (g * g)
    return -ADAM_LR * ((m_new / c1) / (jnp.sqrt(v_new / c2) + ADAM_EPS) + ADAM_WD * w), m_new, v_new


def _adamw(w, g, m, v, name):
    rows, cols = w.shape
    tm = _row_tile(rows)

    def body(w_ref, g_ref, m_ref, v_ref, d_ref, nm_ref, nv_ref):
        d_ref[...], nm_ref[...], nv_ref[...] = _adamw_math(w_ref[...], g_ref[...], m_ref[...], v_ref[...])

    spec = pl.BlockSpec((tm, cols), lambda i: (i, 0))
    return pl.pallas_call(
        body, grid=(rows // tm,), in_specs=[spec] * 4, out_specs=[spec] * 3, out_shape=[SDS(w.shape, f32)] * 3,
        name=name, compiler_params=_params(("parallel",), 28 * tm * cols, 16 * tm * cols))(w, g, m, v)


def _adamw_halves(w, g_mine, g_theirs, m, v, name):
    cols = w.shape[1]
    tm = _row_tile(g_mine.shape[0])
    per_half = g_mine.shape[0] // tm
    assert 2 * g_mine.shape[0] - w.shape[0] < tm
    core = lax.axis_index("c").astype(jnp.int32).reshape(1)

    def body(c_ref, w_ref, gm_ref, gt_ref, m_ref, v_ref, g_ref, d_ref, nm_ref, nv_ref):
        mine = pl.program_id(0) // per_half == c_ref[0]
        g = jnp.where(mine, gm_ref[...], gt_ref[...])
        g_ref[...] = g
        d_ref[...], nm_ref[...], nv_ref[...] = _adamw_math(w_ref[...], g, m_ref[...], v_ref[...])

    spec = pl.BlockSpec((tm, cols), lambda i, c_ref: (i, 0))
    in_half = lambda i, first: jnp.clip(i - first * per_half, 0, per_half - 1)
    grid_spec = pltpu.PrefetchScalarGridSpec(
        num_scalar_prefetch=1, grid=(2 * per_half,),
        in_specs=[spec, pl.BlockSpec((tm, cols), lambda i, c_ref: (in_half(i, c_ref[0]), 0)),
                  pl.BlockSpec((tm, cols), lambda i, c_ref: (in_half(i, 1 - c_ref[0]), 0)), spec, spec],
        out_specs=[spec] * 4)
    return pl.pallas_call(
        body, grid_spec=grid_spec, out_shape=[SDS(w.shape, f32)] * 4, name=name,
        compiler_params=_params(("parallel",), 36 * tm * cols, 16 * tm * cols))(core, w, g_mine, g_theirs, m, v)


_ANY = pl.BlockSpec(memory_space=pl.ANY)


def _place():
    x, y, c = lax.axis_index("x"), lax.axis_index("y"), lax.axis_index("c")
    chips = [(1 - x, y), (x, 1 - y), (1 - x, 1 - y)]
    return x, y, c, chips


def _halved(t):
    return t.reshape(t.shape[:-2] + (2, t.shape[-2] // 2, t.shape[-1]))


def _gather_body(src, out, send_ici, recv_ici, send_d2d, recv_d2d):
    x, y, c, chips = _place()
    sibling = (x, y, 1 - c)
    me_j = 2 * x + y
    sends = []
    for a in range(len(src)):
        for p in range(3):
            cp = pltpu.make_async_remote_copy(
                src_ref=src[a].at[c], dst_ref=out[a].at[me_j, c], send_sem=send_ici.at[a, p],
                recv_sem=recv_ici.at[a, p], device_id=(*chips[p], c), device_id_type=MESH)
            cp.start()
            sends.append(cp)
    for a in range(len(src)):
        for p, (px, py) in enumerate(chips):
            blk = out[a].at[2 * px + py, c]
            pltpu.make_async_remote_copy(
                src_ref=blk, dst_ref=blk, send_sem=send_ici.at[a, p], recv_sem=recv_ici.at[a, p],
                device_id=sibling, device_id_type=MESH).wait_recv()
            fw = pltpu.make_async_remote_copy(
                src_ref=blk, dst_ref=blk, send_sem=send_d2d.at[a, p], recv_sem=recv_d2d.at[a, p],
                device_id=sibling, device_id_type=MESH)
            fw.start()
            sends.append(fw)
    for a in range(len(src)):
        for p, (px, py) in enumerate(chips):
            blk = out[a].at[2 * px + py, 1 - c]
            pltpu.make_async_remote_copy(
                src_ref=blk, dst_ref=blk, send_sem=send_d2d.at[a, p], recv_sem=recv_d2d.at[a, p],
                device_id=sibling, device_id_type=MESH).wait_recv()
    for cp in sends:
        cp.wait_send()


def _handshake(peers):
    barrier = pltpu.get_barrier_semaphore()
    for peer in peers:
        pl.semaphore_signal(barrier, inc=1, device_id=peer, device_id_type=MESH)
    pl.semaphore_wait(barrier, len(peers))


_SEQUENCER = dict(axis_name="sequencer", num_cores=1)
GATHER_LATE_ID, SCATTER_EARLY_ID, SWAP_EARLY_ID, GATHER_FIRST_ID, SCATTER_LATE_ID = 1, 2, 3, 4, 5


def _all_gather_async(shards, after, name, collective_id):
    n, k = len(shards), len(after)

    def body(*refs):
        x, y, c, chips = _place()
        _handshake([(*chip, c) for chip in chips] + [(x, y, 1 - c)])
        _gather_body(refs[:n], refs[n + k:2 * n + k], *refs[2 * n + k:])

    return pl.kernel(
        body, out_type=[SDS((N_SHARD,) + t.shape, t.dtype) for t in shards],
        mesh=plsc.ScalarSubcoreMesh(**_SEQUENCER), scratch_types=[pltpu.SemaphoreType.DMA((n, 3))] * 4,
        compiler_params=pltpu.CompilerParams(collective_id=collective_id), name=name)(*shards, *after)


def _pair_swap(grads):
    n = len(grads)

    def body(*refs):
        src, out, send_sems, recv_sems = refs[:n], refs[n:2 * n], refs[2 * n], refs[2 * n + 1]
        x, y, c, _ = _place()
        copies = [pltpu.make_async_remote_copy(
            src_ref=src[a].at[:, 1 - c], dst_ref=out[a], send_sem=send_sems.at[a], recv_sem=recv_sems.at[a],
            device_id=(x, y, 1 - c), device_id_type=MESH) for a in range(n)]
        for cp in copies:
            cp.start()
        for cp in copies:
            cp.wait()

    return pl.pallas_call(
        body, in_specs=[_ANY] * n, out_specs=[_ANY] * n,
        out_shape=[SDS((N_SHARD,) + t.shape[2:], t.dtype) for t in grads],
        scratch_shapes=[pltpu.SemaphoreType.DMA((n,)), pltpu.SemaphoreType.DMA((n,))], name="pair_swap",
        compiler_params=pltpu.CompilerParams(has_side_effects=True))(*grads)


def _pair_swap_early(grads):
    n = len(grads)

    def body(*refs):
        src, out, send_sems, recv_sems = refs[:n], refs[n:2 * n], refs[2 * n], refs[2 * n + 1]
        x, y, c, _ = _place()
        _handshake([(x, y, 1 - c)])
        copies = [pltpu.make_async_remote_copy(
            src_ref=src[a].at[:, 1 - c], dst_ref=out[a], send_sem=send_sems.at[a], recv_sem=recv_sems.at[a],
            device_id=(x, y, 1 - c), device_id_type=MESH) for a in range(n)]
        for cp in copies:
            cp.start()
        for cp in copies:
            cp.wait()

    return pl.kernel(
        body, out_type=[SDS((N_SHARD,) + t.shape[2:], t.dtype) for t in grads],
        mesh=plsc.ScalarSubcoreMesh(**_SEQUENCER), scratch_types=[pltpu.SemaphoreType.DMA((n,))] * 2,
        compiler_params=pltpu.CompilerParams(collective_id=SWAP_EARLY_ID), name="pair_swap_early")(*grads)


def _scatter_early(parts):
    n = len(parts)

    def body(*refs):
        part, recv, send_sems, recv_sems = refs[:n], refs[n:2 * n], refs[2 * n], refs[2 * n + 1]
        x, y, c, chips = _place()
        _handshake([(*chip, c) for chip in chips])
        me_j = 2 * x + y
        sends = []
        for a in range(n):
            for p, (px, py) in enumerate(chips):
                cp = pltpu.make_async_remote_copy(
                    src_ref=part[a].at[2 * px + py], dst_ref=recv[a].at[me_j], send_sem=send_sems.at[a, p],
                    recv_sem=recv_sems.at[a, p], device_id=(px, py, c), device_id_type=MESH)
                cp.start()
                sends.append(cp)
        for a in range(n):
            for p, (px, py) in enumerate(chips):
                slot = recv[a].at[2 * px + py]
                pltpu.make_async_remote_copy(
                    src_ref=slot, dst_ref=slot, send_sem=send_sems.at[a, p], recv_sem=recv_sems.at[a, p],
                    device_id=(px, py, c), device_id_type=MESH).wait_recv()
        for cp in sends:
            cp.wait_send()

    return pl.kernel(
        body, out_type=[SDS(t.shape, t.dtype) for t in parts],
        mesh=plsc.ScalarSubcoreMesh(**_SEQUENCER), scratch_types=[pltpu.SemaphoreType.DMA((n, 3))] * 2,
        compiler_params=pltpu.CompilerParams(collective_id=SCATTER_EARLY_ID), name="scatter_early")(*parts)


def _pair_sum(grads, other, name):
    _, _, rows, cols = grads.shape
    tr = _row_tile(rows)
    core = lax.axis_index("c").astype(jnp.int32).reshape(1)

    def body(c_ref, g_ref, o_ref, out_ref):
        out_ref[...] = (g_ref[...].astype(f32) + o_ref[...].astype(f32)).astype(bf16)

    grid_spec = pltpu.PrefetchScalarGridSpec(
        num_scalar_prefetch=1, grid=(N_SHARD, rows // tr),
        in_specs=[pl.BlockSpec((None, None, tr, cols), lambda j, i, c_ref: (j, c_ref[0], i, 0)),
                  pl.BlockSpec((None, tr, cols), lambda j, i, c_ref: (j, i, 0))],
        out_specs=pl.BlockSpec((None, tr, cols), lambda j, i, c_ref: (j, i, 0)))
    return pl.pallas_call(
        body, grid_spec=grid_spec, out_shape=SDS((N_SHARD, rows, cols), bf16), name=name,
        compiler_params=_params(("parallel", "parallel"), 10 * tr * cols, 12 * tr * cols))(core, grads, other)


def _scatter_partials(parts, small):
    n = len(parts)

    def body(*refs):
        part, small_ref, recv, small_all_ref = refs[:n], refs[n], refs[n + 1:2 * n + 1], refs[2 * n + 1]
        send_sems, recv_sems, ssend, srecv, local_sem = refs[2 * n + 2:]
        x, y, c, chips = _place()
        flip = lambda a, bit: 1 - a if bit else a
        peers = [(flip(x, k & 4), flip(y, k & 2), flip(c, k & 1)) for k in range(1, 8)]
        _handshake(peers)
        me_j = 2 * x + y
        me_dev = 4 * x + 2 * y + c
        own = pltpu.make_async_copy(small_ref, small_all_ref.at[me_dev], local_sem)
        own.start()
        sends = []
        for a in range(n):
            for p, (px, py) in enumerate(chips):
                cp = pltpu.make_async_remote_copy(
                    src_ref=part[a].at[2 * px + py], dst_ref=recv[a].at[me_j], send_sem=send_sems.at[a, p],
                    recv_sem=recv_sems.at[a, p], device_id=(px, py, c), device_id_type=MESH)
                cp.start()
                sends.append(cp)
        for k, to in enumerate(peers):
            cp = pltpu.make_async_remote_copy(
                src_ref=small_ref, dst_ref=small_all_ref.at[me_dev],
                send_sem=ssend.at[k], recv_sem=srecv.at[k], device_id=to, device_id_type=MESH)
            cp.start()
            sends.append(cp)
        for a in range(n):
            for p, (px, py) in enumerate(chips):
                slot = recv[a].at[2 * px + py]
                pltpu.make_async_remote_copy(
                    src_ref=slot, dst_ref=slot, send_sem=send_sems.at[a, p], recv_sem=recv_sems.at[a, p],
                    device_id=(px, py, c), device_id_type=MESH).wait_recv()
        for k, (px, py, pc) in enumerate(peers):
            slot = small_all_ref.at[4 * px + 2 * py + pc]
            pltpu.make_async_remote_copy(
                src_ref=slot, dst_ref=slot, send_sem=ssend.at[k], recv_sem=srecv.at[k],
                device_id=(px, py, pc), device_id_type=MESH).wait_recv()
        for cp in sends:
            cp.wait_send()
        own.wait()

    return pl.kernel(
        body, out_type=[SDS(t.shape, t.dtype) for t in parts] + [SDS((8, SMALL_ROWS, D), f32)],
        mesh=plsc.ScalarSubcoreMesh(**_SEQUENCER),
        scratch_types=[pltpu.SemaphoreType.DMA((n, 3)), pltpu.SemaphoreType.DMA((n, 3)),
                       pltpu.SemaphoreType.DMA((7,)), pltpu.SemaphoreType.DMA((7,)), pltpu.SemaphoreType.DMA],
        compiler_params=pltpu.CompilerParams(collective_id=SCATTER_LATE_ID), name="scatter_partials")(*parts, small)


def _sum_partials(part, recv, name):
    _, rows, cols = recv.shape
    tr = _row_tile(rows)
    me = (2 * lax.axis_index("x") + lax.axis_index("y")).astype(jnp.int32).reshape(1)

    def body(me_ref, mine, r0, r1, r2, r3, out_ref):
        acc = None
        for j, r in enumerate((r0, r1, r2, r3)):
            term = jnp.where(me_ref[0] == j, mine[...], r[...]).astype(f32)
            acc = term if acc is None else acc + term
        out_ref[...] = acc

    slot = lambda j: pl.BlockSpec((None, tr, cols), lambda i, me_ref: (jnp.where(me_ref[0] == j, j ^ 1, j), i, 0))
    grid_spec = pltpu.PrefetchScalarGridSpec(
        num_scalar_prefetch=1, grid=(rows // tr,),
        in_specs=[pl.BlockSpec((None, tr, cols), lambda i, me_ref: (me_ref[0], i, 0)), slot(0), slot(1), slot(2), slot(3)],
        out_specs=pl.BlockSpec((tr, cols), lambda i, me_ref: (i, 0)))
    return pl.pallas_call(
        body, grid_spec=grid_spec, out_shape=SDS((rows, cols), f32), name=name,
        compiler_params=_params(("parallel",), 14 * tr * cols, 12 * tr * cols))(me, part, recv, recv, recv, recv)


def _sum_small(small_all):
    def body(small_ref, out_ref):
        tot = small_ref[0]
        for k in range(1, 8):
            tot = tot + small_ref[k]
        out_ref[...] = tot

    return pl.pallas_call(
        body, grid=(1,), in_specs=[pl.BlockSpec((8, SMALL_ROWS, D), lambda i: (0, 0, 0))],
        out_specs=pl.BlockSpec((SMALL_ROWS, D), lambda i: (0, 0)), out_shape=SDS((SMALL_ROWS, D), f32),
        name="sum_small", compiler_params=_params(("arbitrary",), 36 * SMALL_ROWS * D))(small_all)


def _swap_halves(halves, name):
    n = len(halves)

    def body(*refs):
        src, out, send_sems, recv_sems = refs[:n], refs[n:2 * n], refs[2 * n], refs[2 * n + 1]
        x, y, c, _ = _place()
        copies = [pltpu.make_async_remote_copy(
            src_ref=src[a], dst_ref=out[a], send_sem=send_sems.at[a], recv_sem=recv_sems.at[a],
            device_id=(x, y, 1 - c), device_id_type=MESH) for a in range(n)]
        for cp in copies:
            cp.start()
        for cp in copies:
            cp.wait()

    return pl.pallas_call(
        body, in_specs=[_ANY] * n, out_specs=[_ANY] * n, out_shape=[SDS(t.shape, f32) for t in halves],
        scratch_shapes=[pltpu.SemaphoreType.DMA((n,))] * 2, name=name,
        compiler_params=pltpu.CompilerParams(has_side_effects=True))(*halves)


def _kernel_layout(name, t):
    t = t[0]
    return jnp.swapaxes(t, 0, 1) if name in TRANSPOSED else t


def _harness_layout(name, t):
    if name in TRANSPOSED:
        t = jnp.swapaxes(t, 0, 1)
    return t[None]


def _pad_rows(t, rows):
    return t if t.shape[0] == rows else jnp.pad(t, ((0, rows - t.shape[0]), (0, 0)))


_QA, _KA, _VA, _QB, _F, _GAB = 0, 768, 1536, 2304, 3840, 3848


def _spans(a, b):
    return [(j, max(a, j * IN_SHARD) - j * IN_SHARD, max(a, j * IN_SHARD) - a,
             min(b, (j + 1) * IN_SHARD) - max(a, j * IN_SHARD))
            for j in range(N_SHARD) if max(a, j * IN_SHARD) < min(b, (j + 1) * IN_SHARD)]


_LANES = pl.BlockSpec((N_SHARD, IN_SHARD_PAD, 128), lambda c: (0, 0, c))


def _split_w_in(shards):
    group = [[(o + g * DIL_W, o + (g + 1) * DIL_W) for o in (_QA, _KA, _VA)] for g in range(3)]
    fox = [[(_QB + k * FOX_W, _QB + (k + 1) * FOX_W)] for k in range(3)]
    wanted = group + fox + [[(_QB, _F)], [(_F, _GAB)], [(_GAB, IN_COLS)]]
    rows = [sum(b - a for a, b in w) for w in wanted]
    rows[7] = 128

    def body(s_ref, *o_refs):
        for o_ref, want in zip(o_refs, wanted):
            at = 0
            for a, b in want:
                for j, src, off, n in _spans(a, b):
                    o_ref[at + off:at + off + n, :] = s_ref[j, src:src + n, :]
                at += b - a
        o_refs[7][N_FOX:, :] = jnp.zeros((128 - N_FOX, 128), bf16)

    return pl.pallas_call(
        body, grid=(D // 128,), in_specs=[_LANES], out_specs=[pl.BlockSpec((r, 128), lambda c: (0, c)) for r in rows],
        out_shape=[SDS((r, D), bf16) for r in rows], name="split_w_in",
        compiler_params=_params(("parallel",), 2 * 128 * (N_SHARD * IN_SHARD_PAD + sum(rows))))(shards)


def _join_w_in(g_a, g_fox, g_f, g_gab):
    parts = [(g_a[k], o, o + DIL_W) for o in (0, DIL_W, 2 * DIL_W) for k in range(3)]
    parts += [(t, 0, FOX_W) for t in g_fox] + [(g_f, 0, N_FOX), (g_gab, 0, 2 * D)]
    arrays = list(g_a) + list(g_fox) + [g_f, g_gab]
    index = {id(t): i for i, t in enumerate(arrays)}

    def body(*refs):
        o_ref = refs[-1]
        o_ref[:, IN_SHARD:, :] = jnp.zeros((N_SHARD, IN_SHARD_PAD - IN_SHARD, 128), bf16)
        at = 0
        for t, lo, hi in parts:
            src_ref = refs[index[id(t)]]
            for j, dst, off, n in _spans(at, at + hi - lo):
                o_ref[j, dst:dst + n, :] = src_ref[lo + off:lo + off + n, :].astype(bf16)
            at += hi - lo

    return pl.pallas_call(
        body, grid=(D // 128,), in_specs=[pl.BlockSpec((t.shape[0], 128), lambda c: (0, c)) for t in arrays],
        out_specs=_LANES, out_shape=SDS((N_SHARD, IN_SHARD_PAD, D), bf16), name="join_w_in",
        compiler_params=_params(("parallel",), 2 * 128 * (N_SHARD * IN_SHARD_PAD + sum(t.shape[0] for t in arrays))),
    )(*arrays)


def _full_weights(gathered):
    full = {n: t.reshape((N_SHARD,) + SHARD_SHAPE[n]) for n, t in gathered.items()}
    out = {}
    if "w_in" in full:
        pieces = _split_w_in(full["w_in"])
        out.update(w_a_t=pieces[0:3], w_fox_t=pieces[3:6], w_vr_t=pieces[6], w_f_t=pieces[7], w_gab_t=pieces[8])
    if "w_out" in full:
        out.update(
            w_a4=full["w_proj_a"],
            w_b4=full["w_proj_b"],
            w_out=full["w_out"].reshape(D, D),
            w_gate_t=full["w_ffn_gate"].reshape(F_FF, D),
            w_up_t=full["w_ffn_up"].reshape(F_FF, D),
            w_down=full["w_ffn_down"].reshape(F_FF, D))
    return out


def _sharded_grads(g):
    full = dict(w_in=_join_w_in(g["w_a_t"], g["w_fox_t"], g["w_f_t"], g["w_gab_t"]), w_proj_a=g["w_a4"],
                w_proj_b=g["w_b4"], w_out=g["w_out"], w_ffn_gate=g["w_gate_t"], w_ffn_up=g["w_up_t"],
                w_ffn_down=g["w_down"])
    return {n: _halved(full[n].reshape((N_SHARD,) + SHARD_SHAPE[n])) for n in W_NAMES}


def _local_step(x, target, wt, b_forget, g_mix_pre, g_mix_post, g_ffn_pre, g_ffn_post, late=None):
    tables = _rope_tables()
    b128 = jnp.pad(b_forget, ((0, 0), (0, 128 - N_FOX)))
    dils = tuple(d for _, d in DIL_GROUPS[1:])

    hs = _norm_fwd([x] + list(_perm_rows([x], dils, "perm_x")), g_mix_pre)
    h1 = hs[0]
    if callable(wt):
        wt = wt(h1)
    qkv = [_rope_fwd(g, _mm([(hs[g], wt["w_a_t"][g])], "nt", f32, tm=1024, tn=QKV_W, name=f"proj_a_{g}"), tables)
           for g in range(3)]
    vr = _mm([(h1, wt["w_vr_t"])], "nt", bf16, tm=1024, tn=VR_W // 2, name="proj_vr")
    gab = _mm([(h1, wt["w_gab_t"])], "nt", f32, tm=512, tn=2 * D, name="proj_gab")
    fz = _mm([(h1, wt["w_f_t"])], "nt", f32, tm=1024, tn=128, name="proj_f")
    dil = [_dil_fwd(g, qkv[g]) for g in range(3)]
    out_a, lse_a = _dil_combine([o for o, _ in dil], [l for _, l in dil])
    f_q, f_k = _forget_fwd(fz, b128)
    out_b, lse_b = _fox_fwd(vr, f_q, f_k)
    if late is not None:
        wt = {**wt, **late(out_b)}
    ya, yb, merged = _merge_fwd(out_a, out_b, wt["w_a4"], wt["w_b4"], gab)
    mix, x2, h3 = _resid_norm_fwd(x, merged, wt["w_out"], g_mix_post, g_ffn_pre)
    g_act, u_act, a_act = _ffn_fwd(h3, wt["w_gate_t"], wt["w_up_t"])
    sq_err, dy, d_ff, dg_ffn_post = _loss_head(x2, a_act, wt["w_down"], g_ffn_post, target)

    grads = {}
    d_g, d_u = _ffn_bwd_act(d_ff, wt["w_down"], g_act, u_act)
    grads["w_down"] = _mm([(a_act, d_ff)], "tn", bf16, tm=FF_TN, tn=D, name="grad_w_down")
    grads["w_gate_t"] = _mm([(d_g, h3)], "tn", bf16, tm=FF_TN, tn=D, name="grad_w_gate")
    grads["w_up_t"] = _mm([(d_u, h3)], "tn", bf16, tm=FF_TN, tn=D, name="grad_w_up")
    dx2, d_mix, dg_ffn_pre, dg_mix_post = _norm_bwd_mid(dy, d_g, d_u, wt["w_gate_t"], wt["w_up_t"], x2, mix,
                                                        g_ffn_pre, g_mix_post)

    grads["w_out"] = _mm([(merged, d_mix)], "tn", bf16, tm=D, tn=D, name="grad_w_out")
    d_ya, d_yb, d_gab = _merge_bwd(d_mix, wt["w_out"], ya, yb, gab)
    grads["w_a4"], grads["w_b4"] = _branch_grads(out_a, out_b, d_ya, d_yb)
    d_out_a, delta_a, d_out_b, delta_b = _branch_bwd(d_ya, d_yb, wt["w_a4"], wt["w_b4"], out_a, out_b)

    perm = _perm_rows([d_out_a, delta_a, lse_a], dils, "perm_dil_bwd")
    aux = [(d_out_a, delta_a, lse_a)] + [tuple(perm[k * len(dils) + i] for k in range(3)) for i in range(len(dils))]
    d_qkv = []
    for g in range(3):
        dq, dk, dv = _dil_bwd(g, qkv[g], *aux[g])
        d_qkv.append(_rope_bwd(g, dq, dk, dv, tables))
    *d_fox, d_f_cols, d_f_rows = _fox_bwd(vr, f_q, f_k, lse_b, d_out_b, delta_b)
    d_z, d_b128 = _forget_bwd(fz, b128, d_f_cols, d_f_rows)

    grads["w_a_t"] = [_mm([(d_qkv[g], hs[g])], "tn", bf16, tm=QKV_W, tn=D, name=f"grad_w_a_{g}") for g in range(3)]
    grads["w_fox_t"] = [_mm([(d_fox[k], h1)], "tn", bf16, tm=FOX_W, tn=D, name=f"grad_w_fox_{k}") for k in range(3)]
    grads["w_gab_t"] = _mm([(d_gab, h1)], "tn", bf16, tm=D, tn=D, name="grad_w_gab")
    grads["w_f_t"] = _mm([(d_z, h1)], "tn", bf16, tm=128, tn=D, name="grad_w_f")
    d_h1_nat = _mm([(d_qkv[0], wt["w_a_t"][0])] + list(zip(d_fox, wt["w_fox_t"]))
                   + [(d_gab, wt["w_gab_t"]), (d_z, wt["w_f_t"])], "nn", f32, tm=512, tn=D, name="proj_in_bwd")
    d_h1_dil = [_mm([(d_qkv[g], wt["w_a_t"][g])], "nn", f32, tm=1024, tn=D, name=f"proj_a_bwd_{g}") for g in (1, 2)]
    d_h1 = _unperm_sum(d_h1_nat, d_h1_dil, dils, "unperm_d_h1")
    grad_x, dg_mix_pre = _norm_bwd_in(dx2, d_h1, x, g_mix_pre)

    small = dict(b_forget=d_b128[:, :N_FOX], norm_mix_pre=dg_mix_pre, norm_mix_post=dg_mix_post,
                 norm_ffn_pre=dg_ffn_pre, norm_ffn_post=dg_ffn_post)
    grads["mid_backward"] = d_qkv[0]
    return sq_err, grad_x, grads, small


NORMS = ("norm_mix_pre", "norm_mix_post", "norm_ffn_pre", "norm_ffn_post")
ORDER = ("w_in", "w_proj_a", "w_proj_b", "w_out", "b_forget", "w_ffn_gate", "w_ffn_up", "w_ffn_down") + NORMS


def kernel(x, w_in, w_proj_a, w_proj_b, w_out, b_forget, w_ffn_gate, w_ffn_up, w_ffn_down, norm_mix_pre, norm_mix_post, norm_ffn_pre, norm_ffn_post, loss_target, m_w_in, m_w_proj_a, m_w_proj_b, m_w_out, m_b_forget, m_w_ffn_gate, m_w_ffn_up, m_w_ffn_down, m_norm_mix_pre, m_norm_mix_post, m_norm_ffn_pre, m_norm_ffn_post, v_w_in, v_w_proj_a, v_w_proj_b, v_w_out, v_b_forget, v_w_ffn_gate, v_w_ffn_up, v_w_ffn_down, v_norm_mix_pre, v_norm_mix_post, v_norm_ffn_pre, v_norm_ffn_post):
    given = dict(w_in=w_in, w_proj_a=w_proj_a, w_proj_b=w_proj_b, w_out=w_out, w_ffn_gate=w_ffn_gate,
                 w_ffn_up=w_ffn_up, w_ffn_down=w_ffn_down)
    given_m = dict(w_in=m_w_in, w_proj_a=m_w_proj_a, w_proj_b=m_w_proj_b, w_out=m_w_out, w_ffn_gate=m_w_ffn_gate,
                   w_ffn_up=m_w_ffn_up, w_ffn_down=m_w_ffn_down)
    given_v = dict(w_in=v_w_in, w_proj_a=v_w_proj_a, w_proj_b=v_w_proj_b, w_out=v_w_out, w_ffn_gate=v_w_ffn_gate,
                   w_ffn_up=v_w_ffn_up, w_ffn_down=v_w_ffn_down)
    w, m, v = ({n: _kernel_layout(n, t[n]) for n in W_NAMES} for t in (given, given_m, given_v))
    small_w = dict(b_forget=b_forget, norm_mix_pre=norm_mix_pre, norm_mix_post=norm_mix_post,
                   norm_ffn_pre=norm_ffn_pre, norm_ffn_post=norm_ffn_post)
    small_m = dict(b_forget=m_b_forget, norm_mix_pre=m_norm_mix_pre, norm_mix_post=m_norm_mix_post,
                   norm_ffn_pre=m_norm_ffn_pre, norm_ffn_post=m_norm_ffn_post)
    small_v = dict(b_forget=v_b_forget, norm_mix_pre=v_norm_mix_pre, norm_mix_post=v_norm_mix_post,
                   norm_ffn_pre=v_norm_ffn_pre, norm_ffn_post=v_norm_ffn_post)

    own = [_halved(_pad_rows(w[n].astype(bf16), SHARD_SHAPE[n][0])) for n in W_NAMES]
    chip = 2 * lax.axis_index("x") + lax.axis_index("y")
    exchanged = {"first": _all_gather_async(own[:1], [], "all_gather_first", GATHER_FIRST_ID)}
    fill = lambda ts, mine: [lax.dynamic_update_index_in_dim(t, o, chip, 0) for t, o in zip(ts, mine)]

    def first_weights(ready):
        arrived, _ = lax.optimization_barrier((list(exchanged["first"]), ready))
        exchanged["late"] = _all_gather_async(own[1:], [arrived[0][0, 0, :16, :128]], "all_gather_late", GATHER_LATE_ID)
        return _full_weights(dict(zip(W_NAMES[:1], fill(arrived, own[:1]))))

    def late_weights(ready):
        arrived, _ = lax.optimization_barrier((list(exchanged["late"]), ready))
        return _full_weights(dict(zip(W_NAMES[1:], fill(arrived, own[1:]))))

    sq_err, grad_x, grads, small = _local_step(x[0], loss_target[0], first_weights, b_forget, norm_mix_pre,
                                               norm_mix_post, norm_ffn_pre, norm_ffn_post, late=late_weights)

    g4 = _sharded_grads(grads)
    stack = lambda t, extra: jnp.concatenate(
        [jnp.pad(t["b_forget"], ((0, 0), (0, D - N_FOX)))] + [t[n] for n in NORMS]
        + [jnp.pad(extra, ((0, SMALL_ROWS - LOSS_ROW - 1), (0, D - extra.shape[1])), constant_values=1.0)], axis=0)
    early, _ = lax.optimization_barrier((list(_pair_swap_early([g4[n] for n in W_NAMES[1:]])), grads["mid_backward"]))
    other = list(_pair_swap([g4["w_in"]])) + early
    parts = [_pair_sum(g4[n], o, "pair_sum_" + n) for n, o in zip(W_NAMES, other)]
    recv_early = _scatter_early(parts[1:])
    recv_in, small_all = _scatter_partials(parts[:1], stack(small, sq_err))

    g_shard, delta, new_m, new_v = {}, {}, {}, {}

    def finish(names, parts, recv):
        halves = [_sum_partials(p, r, "sum_partials_" + n) for n, p, r in zip(names, parts, recv)]
        theirs = _swap_halves(halves, "swap_halves_" + names[0])
        for n, mine, other_half in zip(names, halves, theirs):
            g_shard[n], delta[n], new_m[n], new_v[n] = _adamw_halves(w[n], mine, other_half, m[n], v[n], "adamw_" + n)

    recv_early, _ = lax.optimization_barrier((list(recv_early), parts[0]))
    finish(W_NAMES[1:], parts[1:], recv_early)
    (recv_in, small_all), _ = lax.optimization_barrier(((recv_in, small_all), [delta[n] for n in W_NAMES[1:]]))
    finish(W_NAMES[:1], parts[:1], [recv_in])
    small_sum = _sum_small(small_all)
    loss = small_sum[LOSS_ROW, 0] * (0.5 / D)
    ones = jnp.ones((1, 128), f32)
    sd, sm, sv = _adamw(stack(small_w, ones), small_sum, stack(small_m, ones), stack(small_v, ones), "adamw_small")

    outs = [loss, grad_x[None]]
    for big, st in ((g_shard, small_sum), (delta, sd), (new_m, sm), (new_v, sv)):
        t = {n: _harness_layout(n, big[n]) for n in W_NAMES}
        t["b_forget"] = st[0:1, :N_FOX]
        for i, n in enumerate(NORMS):
            t[n] = st[i + 1:i + 2]
        outs += [t[n] for n in ORDER]
    return tuple(outs)
```

```python
import functools
import math

import jax
import jax.numpy as jnp
import numpy as np
from jax import lax
from jax.experimental import pallas as pl
from jax.experimental.pallas import tpu as pltpu
from jax.experimental.pallas import tpu_sc as plsc

f32 = jnp.float32
bf16 = jnp.bfloat16
SDS = jax.ShapeDtypeStruct
MESH = pl.DeviceIdType.MESH

S = 2048
D = 1024
HD = 64
BLK = 128
N_FOX = 8
FOX_W = N_FOX * HD
DIL_GROUPS = ((128, 1), (512, 4), (2048, 16))
SLOTS = 4
DIL_W = SLOTS * HD
QKV_W = 3 * DIL_W
VR_W = 3 * FOX_W
GF_W = 2 * D + 128
F_FF = 2816
ROPE_DIM = 16
ROPE_THETA = 500000.0
EPS = 1e-6
NEG = -1e30
SCALE = 1.0 / math.sqrt(HD)
IN_COLS = 5896
N_SHARD = 4

ADAM_LR, ADAM_B1, ADAM_B2, ADAM_EPS, ADAM_WD, ADAM_STEP = 0.001, 0.9, 0.999, 1e-08, 0.01, 10

VMEM_V7X = 64 * 1024 * 1024
VMEM_PLAN_MAX = VMEM_V7X - 8 * 1024 * 1024

TM = 512
TQ = 256

W_NAMES = ("w_in", "w_proj_a", "w_proj_b", "w_out", "w_ffn_gate", "w_ffn_up", "w_ffn_down")
TRANSPOSED = ("w_in", "w_ffn_gate", "w_ffn_up")
IN_SHARD = IN_COLS // N_SHARD
IN_SHARD_PAD = 1504
SHARD_SHAPE = dict(w_in=(IN_SHARD_PAD, D), w_proj_a=(DIL_W, D // N_SHARD), w_proj_b=(FOX_W, D // N_SHARD),
                   w_out=(D // N_SHARD, D), w_ffn_gate=(F_FF // N_SHARD, D), w_ffn_up=(F_FF // N_SHARD, D),
                   w_ffn_down=(F_FF // N_SHARD, D))
SMALL_ROWS = 8
LOSS_ROW = 5


def _nbytes(shape, dtype):
    return math.prod(shape) * jnp.dtype(dtype).itemsize


def _params(semantics, block_bytes, temp_bytes=0):
    need = 2 * block_bytes + temp_bytes + (2 << 20)
    return pltpu.CompilerParams(dimension_semantics=semantics, vmem_limit_bytes=int(min(need, VMEM_PLAN_MAX)))


def _row(w, tm=TM):
    return pl.BlockSpec((tm, w), lambda i: (i, 0))


def _vec(w):
    return pl.BlockSpec((1, w), lambda i: (0, 0))


def _mm(pairs, dims, out_dtype, *, tm, tn, name, m_inner=False):
    a0, b0 = pairs[0]
    m_dim = a0.shape[1] if dims == "tn" else a0.shape[0]
    n_dim = b0.shape[0] if dims == "nt" else b0.shape[1]
    contract = {"nn": ((1,), (0,)), "nt": ((1,), (1,)), "tn": ((0,), (0,))}[dims]
    n_pairs = len(pairs)
    assert m_dim % tm == 0 and n_dim % tn == 0, (name, m_dim, n_dim, tm, tn)

    def body(*refs):
        o_ref = refs[-1]
        acc = None
        for p in range(n_pairs):
            a = refs[2 * p][...].astype(bf16)
            b = refs[2 * p + 1][...].astype(bf16)
            t = lax.dot_general(a, b, (contract, ((), ())), preferred_element_type=f32)
            acc = t if acc is None else acc + t
        o_ref[...] = acc.astype(o_ref.dtype)

    if m_inner:
        grid = (n_dim // tn, m_dim // tm)
        mi = lambda j, i: i
        ni = lambda j, i: j
    else:
        grid = (m_dim // tm, n_dim // tn)
        mi = lambda i, j: i
        ni = lambda i, j: j
    in_specs, block_bytes, args = [], 0, []
    for a, b in pairs:
        k_dim = a.shape[0] if dims == "tn" else a.shape[1]
        if dims == "tn":
            in_specs.append(pl.BlockSpec((k_dim, tm), lambda *g: (0, mi(*g))))
        else:
            in_specs.append(pl.BlockSpec((tm, k_dim), lambda *g: (mi(*g), 0)))
        if dims == "nt":
            in_specs.append(pl.BlockSpec((tn, k_dim), lambda *g: (ni(*g), 0)))
        else:
            in_specs.append(pl.BlockSpec((k_dim, tn), lambda *g: (0, ni(*g))))
        block_bytes += _nbytes((tm, k_dim), a.dtype) + _nbytes((tn, k_dim), b.dtype)
        args += [a, b]
    block_bytes += _nbytes((tm, tn), out_dtype)
    temp = _nbytes((tm, tn), f32) * 2 + sum(_nbytes((tm, a.shape[0] if dims == "tn" else a.shape[1]), bf16)
                                            + _nbytes((tn, a.shape[0] if dims == "tn" else a.shape[1]), bf16)
                                            for a, _ in pairs)
    return pl.pallas_call(
        body, grid=grid, in_specs=in_specs,
        out_specs=pl.BlockSpec((tm, tn), lambda *g: (mi(*g), ni(*g))),
        out_shape=SDS((m_dim, n_dim), out_dtype), name=name,
        compiler_params=_params(("parallel", "parallel"), block_bytes, temp),
    )(*args)


def _rms(x, g):
    r = lax.rsqrt(jnp.mean(x * x, axis=-1, keepdims=True) + EPS)
    return x * r * g


def _rms_bwd(x, g, dy):
    r = lax.rsqrt(jnp.mean(x * x, axis=-1, keepdims=True) + EPS)
    xh = x * r
    dxh = dy * g
    dx = r * (dxh - xh * jnp.mean(dxh * xh, axis=-1, keepdims=True))
    return dx, jnp.sum(dy * xh, axis=0, keepdims=True)


def _acc_rows(ref, val):
    @pl.when(pl.program_id(0) == 0)
    def _():
        ref[...] = jnp.zeros_like(ref)
    ref[...] += val


def _norm_fwd(xs, g):
    n = len(xs)

    def body(*refs):
        g = refs[n][...]
        for x_ref, h_ref in zip(refs[:n], refs[n + 1:]):
            h_ref[...] = _rms(x_ref[...], g).astype(bf16)

    return pl.pallas_call(
        body, grid=(S // TM,), in_specs=[_row(D)] * n + [_vec(D)], out_specs=[_row(D)] * n,
        out_shape=[SDS((S, D), bf16)] * n, name="norm_mix_pre",
        compiler_params=_params(("parallel",), 6 * n * TM * D, 8 * n * TM * D))(*xs, g)


def _perm_rows(xs, ds, name):
    n = len(xs)

    def body(*refs):
        outs = iter(refs[n:])
        for x_ref in refs[:n]:
            for d in ds:
                o_ref, rows = next(outs), S // d
                for r in range(d):
                    o_ref[r * rows:(r + 1) * rows, :] = x_ref[pl.ds(r, rows, stride=d), :]

    blk = pl.BlockSpec((S, 128), lambda c: (0, c))
    w = xs[0].shape[1]
    return pl.pallas_call(
        body, grid=(w // 128,), in_specs=[blk] * n, out_specs=[blk] * (n * len(ds)),
        out_shape=[SDS((S, w), f32)] * (n * len(ds)), name=name,
        compiler_params=_params(("parallel",), 4 * S * 128 * n * (1 + len(ds))))(*xs)


def _unperm_sum(nat, perms, ds, name):
    n = len(perms)

    def body(*refs):
        a_ref, o_ref, sc = refs[0], refs[n + 1], refs[n + 2]
        acc = a_ref[...]
        for b_ref, d in zip(refs[1:n + 1], ds):
            rows = S // d
            for r in range(d):
                sc[pl.ds(r, rows, stride=d), :] = b_ref[r * rows:(r + 1) * rows, :]
            acc = acc + sc[...]
        o_ref[...] = acc

    blk = pl.BlockSpec((S, 128), lambda c: (0, c))
    w = nat.shape[1]
    return pl.pallas_call(
        body, grid=(w // 128,), in_specs=[blk] * (n + 1), out_specs=blk, out_shape=SDS((S, w), f32),
        scratch_shapes=[pltpu.VMEM((S, 128), f32)], name=name,
        compiler_params=_params(("parallel",), 4 * S * 128 * (n + 2), 8 * S * 128))(nat, *perms)


def _whole(a):
    return pl.BlockSpec(a.shape, lambda i: (0,) * a.ndim)


def _resid_norm_fwd(x, merged, w_out, g_post, g_pre):
    def body(x_ref, mg_ref, w_ref, gp_ref, gn_ref, mix_ref, x2_ref, h_ref):
        mix = jnp.dot(mg_ref[...], w_ref[...], preferred_element_type=f32)
        x2 = x_ref[...] + _rms(mix, gp_ref[...])
        mix_ref[...] = mix
        x2_ref[...] = x2
        h_ref[...] = _rms(x2, gn_ref[...]).astype(bf16)

    return pl.pallas_call(
        body, grid=(S // TM,), in_specs=[_row(D), _row(D), _whole(w_out), _vec(D), _vec(D)], out_specs=[_row(D)] * 3,
        out_shape=[SDS((S, D), f32), SDS((S, D), f32), SDS((S, D), bf16)], name="proj_out_norm",
        compiler_params=_params(("parallel",), 16 * TM * D + 2 * D * D, 16 * TM * D))(x, merged, w_out, g_post, g_pre)


def _loss_head(x2, a_act, w_down, g_post, target):
    def body(x2_ref, a_ref, w_ref, g_ref, t_ref, loss_ref, dy_ref, dff_ref, dg_ref):
        ff = jnp.dot(a_ref[...], w_ref[...], preferred_element_type=f32)
        g = g_ref[...]
        err = x2_ref[...] + _rms(ff, g) - t_ref[...]
        dy = err * (1.0 / D)
        dff, dg = _rms_bwd(ff, g, dy)
        dy_ref[...] = dy
        dff_ref[...] = dff.astype(bf16)
        _acc_rows(dg_ref, dg)
        _acc_rows(loss_ref, jnp.full((1, 128), jnp.sum(err * err), f32))

    return pl.pallas_call(
        body, grid=(S // TM,), in_specs=[_row(D), _row(F_FF), _whole(w_down), _vec(D), _row(D)],
        out_specs=[_vec(128), _row(D), _row(D), _vec(D)],
        out_shape=[SDS((1, 128), f32), SDS((S, D), f32), SDS((S, D), bf16), SDS((1, D), f32)], name="ffn_down_loss",
        compiler_params=_params(("arbitrary",), 14 * TM * D + 2 * TM * F_FF + 2 * F_FF * D, 28 * TM * D),
    )(x2, a_act, w_down, g_post, target)


def _norm_bwd_mid(dy, d_g, d_u, w_gate_t, w_up_t, x2, mix, g_ffn_pre, g_mix_post):
    def body(dy_ref, dgt_ref, dut_ref, wg_ref, wu_ref, x2_ref, mix_ref, g3_ref, g2_ref, dx2_ref, dmix_ref, dg3_ref, dg2_ref):
        dh = jnp.dot(dgt_ref[...], wg_ref[...], preferred_element_type=f32)
        dh += jnp.dot(dut_ref[...], wu_ref[...], preferred_element_type=f32)
        d3, dg3 = _rms_bwd(x2_ref[...], g3_ref[...], dh)
        dx2 = dy_ref[...] + d3
        dmix, dg2 = _rms_bwd(mix_ref[...], g2_ref[...], dx2)
        dx2_ref[...] = dx2
        dmix_ref[...] = dmix.astype(bf16)
        _acc_rows(dg3_ref, dg3)
        _acc_rows(dg2_ref, dg2)

    tm = TM // 2
    row = lambda w: _row(w, tm)
    return pl.pallas_call(
        body, grid=(S // tm,),
        in_specs=[row(D), row(F_FF), row(F_FF), _whole(w_gate_t), _whole(w_up_t), row(D), row(D), _vec(D), _vec(D)],
        out_specs=[row(D), row(D), _vec(D), _vec(D)],
        out_shape=[SDS((S, D), f32), SDS((S, D), bf16), SDS((1, D), f32), SDS((1, D), f32)], name="ffn_bwd_in_norm",
        compiler_params=_params(("arbitrary",), 18 * tm * D + 4 * tm * F_FF + 4 * F_FF * D, 28 * tm * D),
    )(dy, d_g, d_u, w_gate_t, w_up_t, x2, mix, g_ffn_pre, g_mix_post)


def _norm_bwd_in(dx2, dh1, x, g):
    def body(dx2_ref, dh_ref, x_ref, g_ref, gx_ref, dg_ref):
        d1, dg = _rms_bwd(x_ref[...], g_ref[...], dh_ref[...])
        gx_ref[...] = dx2_ref[...] + d1
        _acc_rows(dg_ref, dg)

    return pl.pallas_call(
        body, grid=(S // TM,), in_specs=[_row(D)] * 3 + [_vec(D)], out_specs=[_row(D), _vec(D)],
        out_shape=[SDS((S, D), f32), SDS((1, D), f32)], name="norm_bwd_in",
        compiler_params=_params(("arbitrary",), 16 * TM * D, 16 * TM * D))(dx2, dh1, x, g)


def _rope_tables():
    half = ROPE_DIM // 2
    inv_freq = np.power(np.float32(ROPE_THETA), -np.arange(0, ROPE_DIM, 2, dtype=np.float32) / np.float32(ROPE_DIM))
    row = np.arange(S)
    groups = []
    for _, d in DIL_GROUPS:
        pos = ((row % (S // d)) * d + row // (S // d)).astype(np.float32)
        ang = pos[:, None] * inv_freq[None, :].astype(np.float32)
        cos, sin = np.cos(ang).astype(np.float32), np.sin(ang).astype(np.float32)
        c = np.concatenate([cos, cos, np.ones((S, HD - ROPE_DIM), np.float32)], axis=1)
        s_lo = np.concatenate([-sin, np.zeros((S, HD - half), np.float32)], axis=1)
        s_hi = np.concatenate([np.zeros((S, half), np.float32), sin, np.zeros((S, HD - ROPE_DIM), np.float32)], axis=1)
        groups.append(np.stack([np.concatenate([t, t], axis=1) for t in (c, s_lo, s_hi)]))
    return jnp.asarray(np.stack(groups))


def _rotate(x, c, lo, hi, sign):
    tile = lambda t: jnp.tile(t, (1, DIL_W // 128))
    return (x * tile(c) + pltpu.roll(x, DIL_W - ROPE_DIM // 2, 1) * (tile(lo) * sign)
            + pltpu.roll(x, ROPE_DIM // 2, 1) * (tile(hi) * sign))


def _table_specs(g):
    return [pl.BlockSpec((None, None, TM, 128), lambda i, k=k: (g, k, i, 0)) for k in range(3)]


def _rope_fwd(g, p_qkv, tables):
    def body(x_ref, c_ref, lo_ref, hi_ref, o_ref):
        c, lo, hi = c_ref[...], lo_ref[...], hi_ref[...]
        for part in range(2):
            cols = slice(part * DIL_W, (part + 1) * DIL_W)
            o_ref[:, cols] = _rotate(x_ref[:, cols], c, lo, hi, 1.0).astype(bf16)
        o_ref[:, 2 * DIL_W:] = x_ref[:, 2 * DIL_W:].astype(bf16)

    return pl.pallas_call(
        body, grid=(S // TM,), in_specs=[_row(QKV_W)] + _table_specs(g), out_specs=_row(QKV_W),
        out_shape=SDS((S, QKV_W), bf16), name=f"rope_fwd_{g}",
        compiler_params=_params(("parallel",), 6 * TM * QKV_W + 12 * TM * 128, 24 * TM * QKV_W))(p_qkv, tables, tables, tables)


def _rope_bwd(g, dq, dk, dv, tables):
    def body(dq_ref, dk_ref, dv_ref, c_ref, lo_ref, hi_ref, o_ref):
        c, lo, hi = c_ref[...], lo_ref[...], hi_ref[...]
        o_ref[:, :DIL_W] = _rotate(dq_ref[...], c, lo, hi, -1.0).astype(bf16)
        o_ref[:, DIL_W:2 * DIL_W] = _rotate(dk_ref[...], c, lo, hi, -1.0).astype(bf16)
        o_ref[:, 2 * DIL_W:] = dv_ref[...].astype(bf16)

    return pl.pallas_call(
        body, grid=(S // TM,), in_specs=[_row(DIL_W)] * 3 + _table_specs(g), out_specs=_row(QKV_W),
        out_shape=SDS((S, QKV_W), bf16), name=f"rope_bwd_{g}",
        compiler_params=_params(("parallel",), 6 * TM * QKV_W + 12 * TM * 128, 24 * TM * QKV_W))(dq, dk, dv, tables, tables, tables)


def _nt(a, b):
    return lax.dot_general(a, b, (((1,), (1,)), ((), ())), preferred_element_type=f32)


def _tn(a, b):
    return lax.dot_general(a, b, (((0,), (0,)), ((), ())), preferred_element_type=f32)


STEP_BLOCKS = 4
STEP_ROWS = STEP_BLOCKS * BLK


def _dil_prev(g, b):
    _, d = DIL_GROUPS[g]
    nb = S // d // BLK
    if nb == 1 or (b == 0 and nb <= STEP_BLOCKS):
        return None
    return "in" if b > 0 else "halo"


def _bnt(a, b):
    return lax.dot_general(a, b, (((2,), (2,)), ((0,), (0,))), preferred_element_type=f32)


def _bnn(a, b):
    return lax.dot_general(a, b, (((2,), (1,)), ((0,), (0,))), preferred_element_type=f32)


def _btn(a, b):
    return lax.dot_general(a, b, (((1,), (1,)), ((0,), (0,))), preferred_element_type=f32)


def _on_tail(x, tail, fn):
    if tail == x.shape[0]:
        return fn(x)
    return jnp.concatenate([x[:-tail], fn(x[-tail:])], axis=0)


def _heads(ref, part):
    n = ref.shape[0] // BLK
    return jnp.stack([ref[b * BLK:(b + 1) * BLK, part * DIL_W + h * HD:part * DIL_W + (h + 1) * HD]
                      for b in range(n) for h in range(SLOTS)])


def _dil_operands(g, qkv_ref, halo_ref):
    q, kc, vc = (_heads(qkv_ref, part) for part in range(3))
    qi = lax.broadcasted_iota(jnp.int32, (1, BLK, BLK), 1)
    kj = lax.broadcasted_iota(jnp.int32, (1, BLK, BLK), 2)
    with_prev = [b for b in range(STEP_BLOCKS) if _dil_prev(g, b) is not None]
    tail = SLOTS * len(with_prev)
    if not tail:
        return q, kc, vc, None, None, kj <= qi, None, 0
    assert with_prev == list(range(STEP_BLOCKS - len(with_prev), STEP_BLOCKS))
    inside = SLOTS * sum(_dil_prev(g, b) == "in" for b in with_prev)
    kp, vp, prev = kc[:inside], vc[:inside], jnp.broadcast_to(kj >= qi, (inside, BLK, BLK))
    if inside < tail:
        no_halo = jnp.where(pl.program_id(0) == 0, BLK + 1, 0)
        kp = jnp.concatenate([_heads(halo_ref, 1), kp], axis=0)
        vp = jnp.concatenate([_heads(halo_ref, 2), vp], axis=0)
        prev = jnp.concatenate([jnp.broadcast_to(kj >= qi + no_halo, (SLOTS, BLK, BLK)), prev], axis=0)
    return q, kc, vc, kp, vp, kj <= qi, prev, tail


def _dil_in_specs(g, n_aux):
    step = lambda w: pl.BlockSpec((STEP_ROWS, w), lambda i: (i, 0))
    halo = [pl.BlockSpec((BLK, QKV_W), lambda i: (jnp.maximum(i * STEP_BLOCKS - 1, 0), 0))]
    needs_halo = _dil_prev(g, 0) == "halo"
    return [step(QKV_W)] + (halo if needs_halo else []) + [step(DIL_W)] * n_aux, needs_halo


def _dil_fwd(g, qkv):
    in_specs, needs_halo = _dil_in_specs(g, 0)

    def body(*refs):
        qkv_ref, halo_ref = refs[0], refs[1] if needs_halo else None
        o_ref, lse_ref = refs[-2:]
        q, kc, vc, kp, vp, cur, prev, tail = _dil_operands(g, qkv_ref, halo_ref)
        sc = jnp.where(cur, _bnt(q, kc) * SCALE, NEG)
        m = jnp.max(sc, axis=-1, keepdims=True)
        if tail:
            sp = jnp.where(prev, _bnt(q[-tail:], kp) * SCALE, NEG)
            m = _on_tail(m, tail, lambda t: jnp.maximum(t, jnp.max(sp, axis=-1, keepdims=True)))
            pp = jnp.exp(sp - m[-tail:])
        pc = jnp.exp(sc - m)
        den = jnp.sum(pc, axis=-1, keepdims=True)
        if tail:
            den = _on_tail(den, tail, lambda t: t + jnp.sum(pp, axis=-1, keepdims=True))
        inv = 1.0 / den
        o = _bnn((pc * inv).astype(bf16), vc)
        if tail:
            o = _on_tail(o, tail, lambda t: t + _bnn((pp * inv[-tail:]).astype(bf16), vp))
        lse = m + jnp.log(den)
        for b in range(STEP_BLOCKS):
            for h in range(SLOTS):
                rows, hs = slice(b * BLK, (b + 1) * BLK), slice(h * HD, (h + 1) * HD)
                o_ref[rows, hs] = o[SLOTS * b + h]
                lse_ref[rows, hs] = jnp.broadcast_to(lse[SLOTS * b + h], (BLK, HD))

    out = pl.BlockSpec((STEP_ROWS, DIL_W), lambda i: (i, 0))
    return pl.pallas_call(
        body, grid=(S // STEP_ROWS,), in_specs=in_specs, out_specs=[out, out], out_shape=[SDS((S, DIL_W), f32)] * 2,
        name=f"dil_fwd_{g}", compiler_params=_params(("parallel",), 12 * STEP_ROWS * DIL_W, 2 << 20),
    )(*([qkv] * (2 if needs_halo else 1)))


def _dil_combine(outs, lses):
    def body(o0, o1, o2, l0, l1, l2, out_ref, lse_ref, so1, so2, sl1, sl2):
        for (_, d), src, dst in ((DIL_GROUPS[1], o1, so1), (DIL_GROUPS[2], o2, so2),
                                 (DIL_GROUPS[1], l1, sl1), (DIL_GROUPS[2], l2, sl2)):
            rows = S // d
            for r in range(d):
                dst[pl.ds(r, rows, stride=d), :] = src[r * rows:(r + 1) * rows, :]
        a, b, c = l0[...], sl1[...], sl2[...]
        m = jnp.maximum(jnp.maximum(a, b), c)
        ea, eb, ec = jnp.exp(a - m), jnp.exp(b - m), jnp.exp(c - m)
        z = ea + eb + ec
        inv = 1.0 / z
        out_ref[...] = (ea * inv) * o0[...] + (eb * inv) * so1[...] + (ec * inv) * so2[...]
        lse_ref[...] = m + jnp.log(z)

    blk = pl.BlockSpec((S, 128), lambda c: (0, c))
    return pl.pallas_call(
        body, grid=(DIL_W // 128,), in_specs=[blk] * 6, out_specs=[blk] * 2,
        out_shape=[SDS((S, DIL_W), f32)] * 2, scratch_shapes=[pltpu.VMEM((S, 128), f32)] * 4, name="dil_combine",
        compiler_params=_params(("parallel",), 32 * S * 128, 32 * S * 128))(*outs, *lses)


def _dil_bwd(g, qkv, d_out, delta, lse):
    in_specs, needs_halo = _dil_in_specs(g, 3)

    def body(*refs):
        qkv_ref, halo_ref = refs[0], refs[1] if needs_halo else None
        do_ref, dl_ref, lse_ref, dq_ref, dk_ref, dv_ref = refs[-6:]
        q, kc, vc, kp, vp, cur, prev, tail = _dil_operands(g, qkv_ref, halo_ref)
        tiles = [(slice(b * BLK, (b + 1) * BLK), h) for b in range(STEP_BLOCKS) for h in range(SLOTS)]
        do = jnp.stack([do_ref[rows, h * HD:(h + 1) * HD] for rows, h in tiles]).astype(bf16)
        lse = jnp.stack([lse_ref[rows, h * HD:h * HD + 1] for rows, h in tiles])
        delta = jnp.stack([dl_ref[rows, h * HD:h * HD + 1] for rows, h in tiles])

        def probs(q, k, mask, lse, do, v, delta):
            p = jnp.exp(jnp.where(mask, _bnt(q, k) * SCALE, NEG) - lse)
            ds = p * (_bnt(do, v) - delta) * SCALE
            return p.astype(bf16), ds.astype(bf16)

        p, ds = probs(q, kc, cur, lse, do, vc, delta)
        dq, dk, dv = _bnn(ds, kc), _btn(ds, q), _btn(p, do)
        if tail:
            p, ds = probs(q[-tail:], kp, prev, lse[-tail:], do[-tail:], vp, delta[-tail:])
            dq = _on_tail(dq, tail, lambda t: t + _bnn(ds, kp))
            dk_p, dv_p = _btn(ds, q[-tail:]), _btn(p, do[-tail:])
            inside = tail - SLOTS if needs_halo else tail
            pad = jnp.zeros((len(tiles) - inside, BLK, HD), f32)
            dk = dk + jnp.concatenate([dk_p[tail - inside:], pad], axis=0)
            dv = dv + jnp.concatenate([dv_p[tail - inside:], pad], axis=0)
        first = pl.multiple_of(pl.program_id(0) * STEP_ROWS, STEP_ROWS)
        for t, (rows, h) in enumerate(tiles):
            hs = slice(h * HD, (h + 1) * HD)
            own = pl.ds(pl.multiple_of(first + rows.start, BLK), BLK)
            dq_ref[rows, hs] = dq[t]
            dk_ref[own, hs] = dk[t]
            dv_ref[own, hs] = dv[t]
        if needs_halo:
            before = pl.ds(pl.multiple_of(jnp.maximum(first - BLK, 0), BLK), BLK)
            for h in range(SLOTS):
                hs = slice(h * HD, (h + 1) * HD)
                dk_ref[before, hs] += dk_p[h]
                dv_ref[before, hs] += dv_p[h]

    whole = pl.BlockSpec((S, DIL_W), lambda i: (0, 0))
    return pl.pallas_call(
        body, grid=(S // STEP_ROWS,), in_specs=in_specs,
        out_specs=[pl.BlockSpec((STEP_ROWS, DIL_W), lambda i: (i, 0)), whole, whole],
        out_shape=[SDS((S, DIL_W), f32)] * 3, name=f"dil_bwd_{g}",
        compiler_params=_params(("arbitrary",), 20 * STEP_ROWS * DIL_W + 8 * S * DIL_W, 2 << 20),
    )(*([qkv] * (2 if needs_halo else 1)), d_out, delta, lse)


def _scan_rows(x, reverse):
    row = lax.broadcasted_iota(jnp.int32, x.shape, 0)
    k = 1
    while k < S:
        if reverse:
            x = x + jnp.where(row < S - k, pltpu.roll(x, S - k, 0), 0.0)
        else:
            x = x + jnp.where(row >= k, pltpu.roll(x, k, 0), 0.0)
        k *= 2
    return x


N_PAIR = N_FOX // 2
_PAIR_Q = pl.BlockSpec((None, S, 128), lambda p: (p, 0, 0))
_PAIR_K = pl.BlockSpec((None, 8, S), lambda p: (p, 0, 0))


def _forget_fwd(fz, b128):
    def body(z_ref, b_ref, fq_ref, fk_ref):
        z = z_ref[...] + b_ref[...]
        logf = jnp.minimum(z, 0.0) - jnp.log1p(jnp.exp(-jnp.abs(z)))
        f_cum = _scan_rows(logf, reverse=False)
        f_cum_t = f_cum.T
        fq_ref[...] = jnp.zeros_like(fq_ref)
        fk_ref[...] = jnp.zeros_like(fk_ref)
        for p in range(N_PAIR):
            fq_ref[p, :, 0:2] = f_cum[:, 2 * p:2 * p + 2]
            fk_ref[p, 0:2, :] = f_cum_t[2 * p:2 * p + 2, :]

    return pl.pallas_call(
        body, grid=(1,), in_specs=[pl.BlockSpec((S, 128), lambda i: (0, 0)), _vec(128)],
        out_specs=[pl.BlockSpec((N_PAIR, S, 128), lambda i: (0, 0, 0)), pl.BlockSpec((N_PAIR, 8, S), lambda i: (0, 0, 0))],
        out_shape=[SDS((N_PAIR, S, 128), f32), SDS((N_PAIR, 8, S), f32)], name="forget_fwd",
        compiler_params=_params(("arbitrary",), 24 * S * 128, 24 * S * 128))(fz, b128)


def _forget_bwd(fz, b128, d_f_cols, d_f_rows):
    def body(z_ref, b_ref, dfc_ref, dfr_ref, dz_ref, db_ref, df_sc):
        z = z_ref[...] + b_ref[...]
        df_sc[...] = jnp.zeros_like(df_sc)
        for p in range(N_PAIR):
            df_sc[:, 2 * p:2 * p + 2] = dfr_ref[p, :, 0:2] + dfc_ref[p].T[:, 0:2]
        dz = _scan_rows(df_sc[...], reverse=True) * jax.nn.sigmoid(-z)
        dz_ref[...] = dz
        db_ref[...] = jnp.sum(dz, axis=0, keepdims=True)

    full = pl.BlockSpec((S, 128), lambda i: (0, 0))
    return pl.pallas_call(
        body, grid=(1,),
        in_specs=[full, _vec(128), pl.BlockSpec((N_PAIR, 8, S), lambda i: (0, 0, 0)), pl.BlockSpec((N_PAIR, S, 128), lambda i: (0, 0, 0))],
        out_specs=[full, _vec(128)], out_shape=[SDS((S, 128), f32), SDS((1, 128), f32)],
        scratch_shapes=[pltpu.VMEM((S, 128), f32)], name="forget_bwd",
        compiler_params=_params(("arbitrary",), 32 * S * 128, 24 * S * 128))(fz, b128, d_f_cols, d_f_rows)


def _fox_scores(q_ref, k_ref, fq_ref, fk_ref, qi, hh):
    n = (qi + 1) * TQ
    rows, hs = slice(qi * TQ, n), slice(hh * HD, (hh + 1) * HD)
    q = q_ref[rows, hs] * SCALE
    s = _nt(q, k_ref[0:n, hs]) + (fq_ref[rows, hh:hh + 1] - fk_ref[hh:hh + 1, 0:n])
    below = lax.broadcasted_iota(jnp.int32, (TQ, TQ), 1) <= lax.broadcasted_iota(jnp.int32, (TQ, TQ), 0)
    diag = jnp.where(below, s[:, n - TQ:], NEG)
    return diag if qi == 0 else jnp.concatenate([s[:, :n - TQ], diag], axis=1)


def _pair_cols(first):
    return pl.BlockSpec((S, 128), lambda p: (0, first + p))


def _fox_fwd(vr, fq, fk):
    def body(q_ref, k_ref, v_ref, fq_ref, fk_ref, o_ref, lse_ref):
        lse_ref[...] = jnp.zeros_like(lse_ref)
        for hh in range(2):
            hs = slice(hh * HD, (hh + 1) * HD)
            for qi in range(S // TQ):
                n = (qi + 1) * TQ
                rows = slice(qi * TQ, n)
                s = _fox_scores(q_ref, k_ref, fq_ref, fk_ref, qi, hh)
                m = jnp.max(s, axis=-1, keepdims=True)
                p = jnp.exp(s - m)
                den = jnp.sum(p, axis=-1, keepdims=True)
                o_ref[rows, hs] = jnp.dot((p * (1.0 / den)).astype(bf16), v_ref[0:n, hs], preferred_element_type=f32)
                lse_ref[rows, hh:hh + 1] = m + jnp.log(den)

    return pl.pallas_call(
        body, grid=(N_PAIR,), in_specs=[_pair_cols(0), _pair_cols(N_PAIR), _pair_cols(2 * N_PAIR), _PAIR_Q, _PAIR_K],
        out_specs=[_pair_cols(0), _PAIR_Q], out_shape=[SDS((S, FOX_W), f32), SDS((N_PAIR, S, 128), f32)],
        name="fox_fwd", compiler_params=_params(("parallel",), 12 * S * 128, 16 * TQ * S),
    )(vr, vr, vr, fq, fk)


def _fox_bwd(vr, fq, fk, lse, d_out, delta):
    def body(q_ref, k_ref, v_ref, do_ref, fq_ref, fk_ref, lse_ref, dl_ref, dq_ref, dk_ref, dv_ref, dfc_ref, dfr_ref,
             dk_sc, dv_sc):
        dfc_ref[...] = jnp.zeros_like(dfc_ref)
        dfr_ref[...] = jnp.zeros_like(dfr_ref)
        for hh in range(2):
            hs = slice(hh * HD, (hh + 1) * HD)
            dk_sc[...] = jnp.zeros_like(dk_sc)
            dv_sc[...] = jnp.zeros_like(dv_sc)
            for qi in range(S // TQ):
                n = (qi + 1) * TQ
                rows = slice(qi * TQ, n)
                q, do, k, v = q_ref[rows, hs], do_ref[rows, hs], k_ref[0:n, hs], v_ref[0:n, hs]
                p = jnp.exp(_fox_scores(q_ref, k_ref, fq_ref, fk_ref, qi, hh) - lse_ref[rows, hh:hh + 1])
                ds = p * (_nt(do, v) - dl_ref[rows, hh:hh + 1])
                dsb = ds.astype(bf16)
                dq_ref[rows, hs] = jnp.dot(dsb, k, preferred_element_type=f32) * SCALE
                dk_sc[0:n, :] += _tn(dsb, q) * SCALE
                dv_sc[0:n, :] += _tn(p.astype(bf16), do)
                dfc_ref[hh:hh + 1, 0:n] -= jnp.sum(ds, axis=0, keepdims=True)
                dfr_ref[rows, hh:hh + 1] = jnp.sum(ds, axis=-1, keepdims=True)
            dk_ref[:, hs] = dk_sc[...]
            dv_ref[:, hs] = dv_sc[...]

    cols = [_pair_cols(k * N_PAIR) for k in range(3)]
    return pl.pallas_call(
        body, grid=(N_PAIR,), in_specs=cols + [_pair_cols(0), _PAIR_Q, _PAIR_K, _PAIR_Q, _PAIR_Q],
        out_specs=[_pair_cols(0)] * 3 + [_PAIR_K, _PAIR_Q],
        out_shape=[SDS((S, FOX_W), f32)] * 3 + [SDS((N_PAIR, 8, S), f32), SDS((N_PAIR, S, 128), f32)],
        scratch_shapes=[pltpu.VMEM((S, HD), f32)] * 2, name="fox_bwd",
        compiler_params=_params(("parallel",), 32 * S * 128, 24 * TQ * S),
    )(vr, vr, vr, d_out, fq, fk, lse, delta)


def _merge_fwd(out_a, out_b, w_a, w_b, gf):
    cw = D // N_SHARD

    def body(oa_ref, ob_ref, wa_ref, wb_ref, ga_ref, gb_ref, ya_ref, yb_ref, mg_ref):
        oa, ob = oa_ref[...].astype(bf16), ob_ref[...].astype(bf16)
        for j in range(N_SHARD):
            cols = slice(j * cw, (j + 1) * cw)
            ya = jnp.dot(oa, wa_ref[j], preferred_element_type=f32)
            yb = jnp.dot(ob, wb_ref[j], preferred_element_type=f32)
            ya_ref[:, cols] = ya
            yb_ref[:, cols] = yb
            mg_ref[:, cols] = (jax.nn.sigmoid(ga_ref[:, cols]) * ya + jax.nn.sigmoid(gb_ref[:, cols]) * yb).astype(bf16)

    full = lambda a: pl.BlockSpec(a.shape, lambda i: (0, 0, 0))
    return pl.pallas_call(
        body, grid=(S // TM,),
        in_specs=[_row(DIL_W), _row(FOX_W), full(w_a), full(w_b), _row(D), pl.BlockSpec((TM, D), lambda i: (i, 1))],
        out_specs=[_row(D)] * 3, out_shape=[SDS((S, D), f32), SDS((S, D), f32), SDS((S, D), bf16)], name="merge_fwd",
        compiler_params=_params(("parallel",), 22 * TM * D + 2 * (DIL_W + FOX_W) * D, 16 * TM * D),
    )(out_a, out_b, w_a, w_b, gf, gf)


def _merge_bwd(d_mix, w_out, ya, yb, gf):
    def body(dx_ref, w_ref, ya_ref, yb_ref, ga_ref, gb_ref, dya_ref, dyb_ref, dg_ref):
        dm = _nt(dx_ref[...], w_ref[...])
        sa, sb = jax.nn.sigmoid(ga_ref[...]), jax.nn.sigmoid(gb_ref[...])
        dya_ref[...] = (dm * sa).astype(bf16)
        dyb_ref[...] = (dm * sb).astype(bf16)
        dg_ref[:, :D] = (dm * ya_ref[...] * sa * (1.0 - sa)).astype(bf16)
        dg_ref[:, D:] = (dm * yb_ref[...] * sb * (1.0 - sb)).astype(bf16)

    return pl.pallas_call(
        body, grid=(S // TM,),
        in_specs=[_row(D), _whole(w_out)] + [_row(D)] * 3 + [pl.BlockSpec((TM, D), lambda i: (i, 1))],
        out_specs=[_row(D), _row(D), _row(2 * D)],
        out_shape=[SDS((S, D), bf16), SDS((S, D), bf16), SDS((S, 2 * D), bf16)], name="proj_out_bwd_merge",
        compiler_params=_params(("parallel",), 26 * TM * D + 2 * D * D, 28 * TM * D))(d_mix, w_out, ya, yb, gf, gf)


def _branch_bwd(d_ya, d_yb, w_a, w_b, out_a, out_b):
    cw = D // N_SHARD

    def body(dya_ref, dyb_ref, wa_ref, wb_ref, oa_ref, ob_ref, doa_ref, dla_ref, dob_ref, dlb_ref):
        doa = jnp.zeros((TM, DIL_W), f32)
        dob = jnp.zeros((TM, FOX_W), f32)
        for j in range(N_SHARD):
            cols = slice(j * cw, (j + 1) * cw)
            doa += _nt(dya_ref[:, cols], wa_ref[j])
            dob += _nt(dyb_ref[:, cols], wb_ref[j])
        doa_ref[...] = doa
        dob_ref[...] = dob.astype(bf16)
        prod_a = doa * oa_ref[...]
        for h in range(SLOTS):
            hs = slice(h * HD, (h + 1) * HD)
            dla_ref[:, hs] = jnp.broadcast_to(jnp.sum(prod_a[:, hs], axis=-1, keepdims=True), (TM, HD))
        prod_b = dob * ob_ref[...]
        dlb_ref[...] = jnp.zeros_like(dlb_ref)
        for h in range(N_FOX):
            dlb_ref[h // 2, :, h % 2:h % 2 + 1] = jnp.sum(prod_b[:, h * HD:(h + 1) * HD], axis=-1, keepdims=True)

    full = lambda a: pl.BlockSpec(a.shape, lambda i: (0, 0, 0))
    return pl.pallas_call(
        body, grid=(S // TM,),
        in_specs=[_row(D), _row(D), full(w_a), full(w_b), _row(DIL_W), _row(FOX_W)],
        out_specs=[_row(DIL_W), _row(DIL_W), _row(FOX_W), pl.BlockSpec((N_PAIR, TM, 128), lambda i: (0, i, 0))],
        out_shape=[SDS((S, DIL_W), f32), SDS((S, DIL_W), f32), SDS((S, FOX_W), bf16), SDS((N_PAIR, S, 128), f32)],
        name="branch_bwd", compiler_params=_params(("parallel",), 8 * TM * D + 2 * (DIL_W + FOX_W) * D, 8 * TM * D),
    )(d_ya, d_yb, w_a, w_b, out_a, out_b)


def _branch_grads(out_a, out_b, d_ya, d_yb):
    cw = D // N_SHARD

    def body(oa_ref, ob_ref, dya_ref, dyb_ref, ga_ref, gb_ref):
        ga_ref[...] = _tn(oa_ref[...].astype(bf16), dya_ref[...]).astype(bf16)
        gb_ref[...] = _tn(ob_ref[...].astype(bf16), dyb_ref[...]).astype(bf16)

    whole = lambda w: pl.BlockSpec((S, w), lambda j: (0, 0))
    cols = pl.BlockSpec((S, cw), lambda j: (0, j))
    return pl.pallas_call(
        body, grid=(N_SHARD,), in_specs=[whole(DIL_W), whole(FOX_W), cols, cols],
        out_specs=[pl.BlockSpec((None, DIL_W, cw), lambda j: (j, 0, 0)), pl.BlockSpec((None, FOX_W, cw), lambda j: (j, 0, 0))],
        out_shape=[SDS((N_SHARD, DIL_W, cw), bf16), SDS((N_SHARD, FOX_W, cw), bf16)], name="grad_w_proj_ab",
        compiler_params=_params(("parallel",), 4 * S * (DIL_W + FOX_W) + 4 * S * cw + 4 * (DIL_W + FOX_W) * cw,
                                4 * S * (DIL_W + FOX_W)))(out_a, out_b, d_ya, d_yb)


FF_TN = F_FF // 2
FF_TM = 1024


def _ffn_fwd(h, w_gate_t, w_up_t):
    def body(h_ref, wg_ref, wu_ref, g_ref, u_ref, a_ref):
        hb = h_ref[...]
        g = _nt(hb, wg_ref[...])
        u = _nt(hb, wu_ref[...])
        g_ref[...] = g
        u_ref[...] = u
        a_ref[...] = (g * jax.nn.sigmoid(g) * u).astype(bf16)

    tile = pl.BlockSpec((FF_TM, FF_TN), lambda j, i: (i, j))
    wspec = pl.BlockSpec((FF_TN, D), lambda j, i: (j, 0))
    return pl.pallas_call(
        body, grid=(F_FF // FF_TN, S // FF_TM),
        in_specs=[pl.BlockSpec((FF_TM, D), lambda j, i: (i, 0)), wspec, wspec], out_specs=[tile] * 3,
        out_shape=[SDS((S, F_FF), f32), SDS((S, F_FF), f32), SDS((S, F_FF), bf16)], name="ffn_fwd",
        compiler_params=_params(("parallel", "parallel"), 2 * FF_TM * D + 4 * D * FF_TN + 10 * FF_TM * FF_TN, 16 * FF_TM * FF_TN),
    )(h, w_gate_t, w_up_t)


def _ffn_bwd_act(d_ff, w_down, g_act, u_act):
    def body(d_ref, wd_ref, g_ref, u_ref, dg_ref, du_ref):
        da = _nt(d_ref[...], wd_ref[...])
        g = g_ref[...]
        sg = jax.nn.sigmoid(g)
        du_ref[...] = (da * g * sg).astype(bf16)
        dg_ref[...] = (da * u_ref[...] * sg * (1.0 + g * (1.0 - sg))).astype(bf16)

    tile = pl.BlockSpec((FF_TM, FF_TN), lambda j, i: (i, j))
    return pl.pallas_call(
        body, grid=(F_FF // FF_TN, S // FF_TM),
        in_specs=[pl.BlockSpec((FF_TM, D), lambda j, i: (i, 0)), pl.BlockSpec((FF_TN, D), lambda j, i: (j, 0)), tile, tile],
        out_specs=[tile, tile], out_shape=[SDS((S, F_FF), bf16)] * 2, name="ffn_bwd_act",
        compiler_params=_params(("parallel", "parallel"), 2 * FF_TM * D + 2 * D * FF_TN + 12 * FF_TM * FF_TN, 16 * FF_TM * FF_TN),
    )(d_ff, w_down, g_act, u_act)


def _row_tile(rows):
    return next(t for t in (376, 128, 176, 64, 32, 16, 8) if rows % t == 0)


def _adamw_math(w, g, m, v):
    c1 = 1.0 - ADAM_B1 ** ADAM_STEP
    c2 = 1.0 - ADAM_B2 ** ADAM_STEP
    m_new = ADAM_B1 * m + (1.0 - ADAM_B1) * g
    v_new = ADAM_B2 * v + (1.0 - ADAM_B2) * (g * g)
    return -ADAM_LR * ((m_new / c1) / (jnp.sqrt(v_new / c2) + ADAM_EPS) + ADAM_WD * w), m_new, v_new


def _adamw(w, g, m, v, name):
    rows, cols = w.shape
    tm = _row_tile(rows)

    def body(w_ref, g_ref, m_ref, v_ref, d_ref, nm_ref, nv_ref):
        d_ref[...], nm_ref[...], nv_ref[...] = _adamw_math(w_ref[...], g_ref[...], m_ref[...], v_ref[...])

    spec = pl.BlockSpec((tm, cols), lambda i: (i, 0))
    return pl.pallas_call(
        body, grid=(rows // tm,), in_specs=[spec] * 4, out_specs=[spec] * 3, out_shape=[SDS(w.shape, f32)] * 3,
        name=name, compiler_params=_params(("parallel",), 28 * tm * cols, 16 * tm * cols))(w, g, m, v)


def _adamw_halves(w, g_mine, g_theirs, m, v, name):
    cols = w.shape[1]
    tm = _row_tile(g_mine.shape[0])
    per_half = g_mine.shape[0] // tm
    assert 2 * g_mine.shape[0] - w.shape[0] < tm
    core = lax.axis_index("c").astype(jnp.int32).reshape(1)

    def body(c_ref, w_ref, gm_ref, gt_ref, m_ref, v_ref, g_ref, d_ref, nm_ref, nv_ref):
        mine = pl.program_id(0) // per_half == c_ref[0]
        g = jnp.where(mine, gm_ref[...], gt_ref[...])
        g_ref[...] = g
        d_ref[...], nm_ref[...], nv_ref[...] = _adamw_math(w_ref[...], g, m_ref[...], v_ref[...])

    spec = pl.BlockSpec((tm, cols), lambda i, c_ref: (i, 0))
    in_half = lambda i, first: jnp.clip(i - first * per_half, 0, per_half - 1)
    grid_spec = pltpu.PrefetchScalarGridSpec(
        num_scalar_prefetch=1, grid=(2 * per_half,),
        in_specs=[spec, pl.BlockSpec((tm, cols), lambda i, c_ref: (in_half(i, c_ref[0]), 0)),
                  pl.BlockSpec((tm, cols), lambda i, c_ref: (in_half(i, 1 - c_ref[0]), 0)), spec, spec],
        out_specs=[spec] * 4)
    return pl.pallas_call(
        body, grid_spec=grid_spec, out_shape=[SDS(w.shape, f32)] * 4, name=name,
        compiler_params=_params(("parallel",), 36 * tm * cols, 16 * tm * cols))(core, w, g_mine, g_theirs, m, v)


_ANY = pl.BlockSpec(memory_space=pl.ANY)


def _place():
    x, y, c = lax.axis_index("x"), lax.axis_index("y"), lax.axis_index("c")
    chips = [(1 - x, y), (x, 1 - y), (1 - x, 1 - y)]
    return x, y, c, chips


def _halved(t):
    return t.reshape(t.shape[:-2] + (2, t.shape[-2] // 2, t.shape[-1]))


def _gather_body(src, out, send_ici, recv_ici, send_d2d, recv_d2d):
    x, y, c, chips = _place()
    sibling = (x, y, 1 - c)
    me_j = 2 * x + y
    sends = []
    for a in range(len(src)):
        for p in range(3):
            cp = pltpu.make_async_remote_copy(
                src_ref=src[a].at[c], dst_ref=out[a].at[me_j, c], send_sem=send_ici.at[a, p],
                recv_sem=recv_ici.at[a, p], device_id=(*chips[p], c), device_id_type=MESH)
            cp.start()
            sends.append(cp)
    for a in range(len(src)):
        for p, (px, py) in enumerate(chips):
            blk = out[a].at[2 * px + py, c]
            pltpu.make_async_remote_copy(
                src_ref=blk, dst_ref=blk, send_sem=send_ici.at[a, p], recv_sem=recv_ici.at[a, p],
                device_id=sibling, device_id_type=MESH).wait_recv()
            fw = pltpu.make_async_remote_copy(
                src_ref=blk, dst_ref=blk, send_sem=send_d2d.at[a, p], recv_sem=recv_d2d.at[a, p],
                device_id=sibling, device_id_type=MESH)
            fw.start()
            sends.append(fw)
    for a in range(len(src)):
        for p, (px, py) in enumerate(chips):
            blk = out[a].at[2 * px + py, 1 - c]
            pltpu.make_async_remote_copy(
                src_ref=blk, dst_ref=blk, send_sem=send_d2d.at[a, p], recv_sem=recv_d2d.at[a, p],
                device_id=sibling, device_id_type=MESH).wait_recv()
    for cp in sends:
        cp.wait_send()


def _handshake(peers):
    barrier = pltpu.get_barrier_semaphore()
    for peer in peers:
        pl.semaphore_signal(barrier, inc=1, device_id=peer, device_id_type=MESH)
    pl.semaphore_wait(barrier, len(peers))


_SEQUENCER = dict(axis_name="sequencer", num_cores=1)
GATHER_LATE_ID, SCATTER_EARLY_ID, SWAP_EARLY_ID, GATHER_FIRST_ID, SCATTER_LATE_ID = 1, 2, 3, 4, 5


def _all_gather_async(shards, after, name, collective_id):
    n, k = len(shards), len(after)

    def body(*refs):
        x, y, c, chips = _place()
        _handshake([(*chip, c) for chip in chips] + [(x, y, 1 - c)])
        _gather_body(refs[:n], refs[n + k:2 * n + k], *refs[2 * n + k:])

    return pl.kernel(
        body, out_type=[SDS((N_SHARD,) + t.shape, t.dtype) for t in shards],
        mesh=plsc.ScalarSubcoreMesh(**_SEQUENCER), scratch_types=[pltpu.SemaphoreType.DMA((n, 3))] * 4,
        compiler_params=pltpu.CompilerParams(collective_id=collective_id), name=name)(*shards, *after)


def _pair_swap(grads):
    n = len(grads)

    def body(*refs):
        src, out, send_sems, recv_sems = refs[:n], refs[n:2 * n], refs[2 * n], refs[2 * n + 1]
        x, y, c, _ = _place()
        copies = [pltpu.make_async_remote_copy(
            src_ref=src[a].at[:, 1 - c], dst_ref=out[a], send_sem=send_sems.at[a], recv_sem=recv_sems.at[a],
            device_id=(x, y, 1 - c), device_id_type=MESH) for a in range(n)]
        for cp in copies:
            cp.start()
        for cp in copies:
            cp.wait()

    return pl.pallas_call(
        body, in_specs=[_ANY] * n, out_specs=[_ANY] * n,
        out_shape=[SDS((N_SHARD,) + t.shape[2:], t.dtype) for t in grads],
        scratch_shapes=[pltpu.SemaphoreType.DMA((n,)), pltpu.SemaphoreType.DMA((n,))], name="pair_swap",
        compiler_params=pltpu.CompilerParams(has_side_effects=True))(*grads)


def _pair_swap_early(grads):
    n = len(grads)

    def body(*refs):
        src, out, send_sems, recv_sems = refs[:n], refs[n:2 * n], refs[2 * n], refs[2 * n + 1]
        x, y, c, _ = _place()
        _handshake([(x, y, 1 - c)])
        copies = [pltpu.make_async_remote_copy(
            src_ref=src[a].at[:, 1 - c], dst_ref=out[a], send_sem=send_sems.at[a], recv_sem=recv_sems.at[a],
            device_id=(x, y, 1 - c), device_id_type=MESH) for a in range(n)]
        for cp in copies:
            cp.start()
        for cp in copies:
            cp.wait()

    return pl.kernel(
        body, out_type=[SDS((N_SHARD,) + t.shape[2:], t.dtype) for t in grads],
        mesh=plsc.ScalarSubcoreMesh(**_SEQUENCER), scratch_types=[pltpu.SemaphoreType.DMA((n,))] * 2,
        compiler_params=pltpu.CompilerParams(collective_id=SWAP_EARLY_ID), name="pair_swap_early")(*grads)


def _scatter_early(parts):
    n = len(parts)

    def body(*refs):
        part, recv, send_sems, recv_sems = refs[:n], refs[n:2 * n], refs[2 * n], refs[2 * n + 1]
        x, y, c, chips = _place()
        _handshake([(*chip, c) for chip in chips])
        me_j = 2 * x + y
        sends = []
        for a in range(n):
            for p, (px, py) in enumerate(chips):
                cp = pltpu.make_async_remote_copy(
                    src_ref=part[a].at[2 * px + py], dst_ref=recv[a].at[me_j], send_sem=send_sems.at[a, p],
                    recv_sem=recv_sems.at[a, p], device_id=(px, py, c), device_id_type=MESH)
                cp.start()
                sends.append(cp)
        for a in range(n):
            for p, (px, py) in enumerate(chips):
                slot = recv[a].at[2 * px + py]
                pltpu.make_async_remote_copy(
                    src_ref=slot, dst_ref=slot, send_sem=send_sems.at[a, p], recv_sem=recv_sems.at[a, p],
                    device_id=(px, py, c), device_id_type=MESH).wait_recv()
        for cp in sends:
            cp.wait_send()

    return pl.kernel(
        body, out_type=[SDS(t.shape, t.dtype) for t in parts],
        mesh=plsc.ScalarSubcoreMesh(**_SEQUENCER), scratch_types=[pltpu.SemaphoreType.DMA((n, 3))] * 2,
        compiler_params=pltpu.CompilerParams(collective_id=SCATTER_EARLY_ID), name="scatter_early")(*parts)


def _pair_sum(grads, other, name):
    _, _, rows, cols = grads.shape
    tr = _row_tile(rows)
    core = lax.axis_index("c").astype(jnp.int32).reshape(1)

    def body(c_ref, g_ref, o_ref, out_ref):
        out_ref[...] = (g_ref[...].astype(f32) + o_ref[...].astype(f32)).astype(bf16)

    grid_spec = pltpu.PrefetchScalarGridSpec(
        num_scalar_prefetch=1, grid=(N_SHARD, rows // tr),
        in_specs=[pl.BlockSpec((None, None, tr, cols), lambda j, i, c_ref: (j, c_ref[0], i, 0)),
                  pl.BlockSpec((None, tr, cols), lambda j, i, c_ref: (j, i, 0))],
        out_specs=pl.BlockSpec((None, tr, cols), lambda j, i, c_ref: (j, i, 0)))
    return pl.pallas_call(
        body, grid_spec=grid_spec, out_shape=SDS((N_SHARD, rows, cols), bf16), name=name,
        compiler_params=_params(("parallel", "parallel"), 10 * tr * cols, 12 * tr * cols))(core, grads, other)


def _scatter_partials(parts, small):
    n = len(parts)

    def body(*refs):
        part, small_ref, recv, small_all_ref = refs[:n], refs[n], refs[n + 1:2 * n + 1], refs[2 * n + 1]
        send_sems, recv_sems, ssend, srecv, local_sem = refs[2 * n + 2:]
        x, y, c, chips = _place()
        flip = lambda a, bit: 1 - a if bit else a
        peers = [(flip(x, k & 4), flip(y, k & 2), flip(c, k & 1)) for k in range(1, 8)]
        _handshake(peers)
        me_j = 2 * x + y
        me_dev = 4 * x + 2 * y + c
        own = pltpu.make_async_copy(small_ref, small_all_ref.at[me_dev], local_sem)
        own.start()
        sends = []
        for a in range(n):
            for p, (px, py) in enumerate(chips):
                cp = pltpu.make_async_remote_copy(
                    src_ref=part[a].at[2 * px + py], dst_ref=recv[a].at[me_j], send_sem=send_sems.at[a, p],
                    recv_sem=recv_sems.at[a, p], device_id=(px, py, c), device_id_type=MESH)
                cp.start()
                sends.append(cp)
        for k, to in enumerate(peers):
            cp = pltpu.make_async_remote_copy(
                src_ref=small_ref, dst_ref=small_all_ref.at[me_dev],
                send_sem=ssend.at[k], recv_sem=srecv.at[k], device_id=to, device_id_type=MESH)
            cp.start()
            sends.append(cp)
        for a in range(n):
            for p, (px, py) in enumerate(chips):
                slot = recv[a].at[2 * px + py]
                pltpu.make_async_remote_copy(
                    src_ref=slot, dst_ref=slot, send_sem=send_sems.at[a, p], recv_sem=recv_sems.at[a, p],
                    device_id=(px, py, c), device_id_type=MESH).wait_recv()
        for k, (px, py, pc) in enumerate(peers):
            slot = small_all_ref.at[4 * px + 2 * py + pc]
            pltpu.make_async_remote_copy(
                src_ref=slot, dst_ref=slot, send_sem=ssend.at[k], recv_sem=srecv.at[k],
                device_id=(px, py, pc), device_id_type=MESH).wait_recv()
        for cp in sends:
            cp.wait_send()
        own.wait()

    return pl.kernel(
        body, out_type=[SDS(t.shape, t.dtype) for t in parts] + [SDS((8, SMALL_ROWS, D), f32)],
        mesh=plsc.ScalarSubcoreMesh(**_SEQUENCER),
        scratch_types=[pltpu.SemaphoreType.DMA((n, 3)), pltpu.SemaphoreType.DMA((n, 3)),
                       pltpu.SemaphoreType.DMA((7,)), pltpu.SemaphoreType.DMA((7,)), pltpu.SemaphoreType.DMA],
        compiler_params=pltpu.CompilerParams(collective_id=SCATTER_LATE_ID), name="scatter_partials")(*parts, small)


def _sum_partials(part, recv, name):
    _, rows, cols = recv.shape
    tr = _row_tile(rows)
    me = (2 * lax.axis_index("x") + lax.axis_index("y")).astype(jnp.int32).reshape(1)

    def body(me_ref, mine, r0, r1, r2, r3, out_ref):
        acc = None
        for j, r in enumerate((r0, r1, r2, r3)):
            term = jnp.where(me_ref[0] == j, mine[...], r[...]).astype(f32)
            acc = term if acc is None else acc + term
        out_ref[...] = acc

    slot = lambda j: pl.BlockSpec((None, tr, cols), lambda i, me_ref: (jnp.where(me_ref[0] == j, j ^ 1, j), i, 0))
    grid_spec = pltpu.PrefetchScalarGridSpec(
        num_scalar_prefetch=1, grid=(rows // tr,),
        in_specs=[pl.BlockSpec((None, tr, cols), lambda i, me_ref: (me_ref[0], i, 0)), slot(0), slot(1), slot(2), slot(3)],
        out_specs=pl.BlockSpec((tr, cols), lambda i, me_ref: (i, 0)))
    return pl.pallas_call(
        body, grid_spec=grid_spec, out_shape=SDS((rows, cols), f32), name=name,
        compiler_params=_params(("parallel",), 14 * tr * cols, 12 * tr * cols))(me, part, recv, recv, recv, recv)


def _sum_small(small_all):
    def body(small_ref, out_ref):
        tot = small_ref[0]
        for k in range(1, 8):
            tot = tot + small_ref[k]
        out_ref[...] = tot

    return pl.pallas_call(
        body, grid=(1,), in_specs=[pl.BlockSpec((8, SMALL_ROWS, D), lambda i: (0, 0, 0))],
        out_specs=pl.BlockSpec((SMALL_ROWS, D), lambda i: (0, 0)), out_shape=SDS((SMALL_ROWS, D), f32),
        name="sum_small", compiler_params=_params(("arbitrary",), 36 * SMALL_ROWS * D))(small_all)


def _swap_halves(halves, name):
    n = len(halves)

    def body(*refs):
        src, out, send_sems, recv_sems = refs[:n], refs[n:2 * n], refs[2 * n], refs[2 * n + 1]
        x, y, c, _ = _place()
        copies = [pltpu.make_async_remote_copy(
            src_ref=src[a], dst_ref=out[a], send_sem=send_sems.at[a], recv_sem=recv_sems.at[a],
            device_id=(x, y, 1 - c), device_id_type=MESH) for a in range(n)]
        for cp in copies:
            cp.start()
        for cp in copies:
            cp.wait()

    return pl.pallas_call(
        body, in_specs=[_ANY] * n, out_specs=[_ANY] * n, out_shape=[SDS(t.shape, f32) for t in halves],
        scratch_shapes=[pltpu.SemaphoreType.DMA((n,))] * 2, name=name,
        compiler_params=pltpu.CompilerParams(has_side_effects=True))(*halves)


def _kernel_layout(name, t):
    t = t[0]
    return jnp.swapaxes(t, 0, 1) if name in TRANSPOSED else t


def _harness_layout(name, t):
    if name in TRANSPOSED:
        t = jnp.swapaxes(t, 0, 1)
    return t[None]


def _pad_rows(t, rows):
    return t if t.shape[0] == rows else jnp.pad(t, ((0, rows - t.shape[0]), (0, 0)))


_QA, _KA, _VA, _QB, _F, _GAB = 0, 768, 1536, 2304, 3840, 3848


def _spans(a, b):
    return [(j, max(a, j * IN_SHARD) - j * IN_SHARD, max(a, j * IN_SHARD) - a,
             min(b, (j + 1) * IN_SHARD) - max(a, j * IN_SHARD))
            for j in range(N_SHARD) if max(a, j * IN_SHARD) < min(b, (j + 1) * IN_SHARD)]


_LANES = pl.BlockSpec((N_SHARD, IN_SHARD_PAD, 128), lambda c: (0, 0, c))


def _split_w_in(shards):
    group = [[(o + g * DIL_W, o + (g + 1) * DIL_W) for o in (_QA, _KA, _VA)] for g in range(3)]
    fox = [[(_QB + k * FOX_W, _QB + (k + 1) * FOX_W)] for k in range(3)]
    wanted = group + fox + [[(_QB, _F)], [(_F, _GAB)], [(_GAB, IN_COLS)]]
    rows = [sum(b - a for a, b in w) for w in wanted]
    rows[7] = 128

    def body(s_ref, *o_refs):
        for o_ref, want in zip(o_refs, wanted):
            at = 0
            for a, b in want:
                for j, src, off, n in _spans(a, b):
                    o_ref[at + off:at + off + n, :] = s_ref[j, src:src + n, :]
                at += b - a
        o_refs[7][N_FOX:, :] = jnp.zeros((128 - N_FOX, 128), bf16)

    return pl.pallas_call(
        body, grid=(D // 128,), in_specs=[_LANES], out_specs=[pl.BlockSpec((r, 128), lambda c: (0, c)) for r in rows],
        out_shape=[SDS((r, D), bf16) for r in rows], name="split_w_in",
        compiler_params=_params(("parallel",), 2 * 128 * (N_SHARD * IN_SHARD_PAD + sum(rows))))(shards)


def _join_w_in(g_a, g_fox, g_f, g_gab):
    parts = [(g_a[k], o, o + DIL_W) for o in (0, DIL_W, 2 * DIL_W) for k in range(3)]
    parts += [(t, 0, FOX_W) for t in g_fox] + [(g_f, 0, N_FOX), (g_gab, 0, 2 * D)]
    arrays = list(g_a) + list(g_fox) + [g_f, g_gab]
    index = {id(t): i for i, t in enumerate(arrays)}

    def body(*refs):
        o_ref = refs[-1]
        o_ref[:, IN_SHARD:, :] = jnp.zeros((N_SHARD, IN_SHARD_PAD - IN_SHARD, 128), bf16)
        at = 0
        for t, lo, hi in parts:
            src_ref = refs[index[id(t)]]
            for j, dst, off, n in _spans(at, at + hi - lo):
                o_ref[j, dst:dst + n, :] = src_ref[lo + off:lo + off + n, :].astype(bf16)
            at += hi - lo

    return pl.pallas_call(
        body, grid=(D // 128,), in_specs=[pl.BlockSpec((t.shape[0], 128), lambda c: (0, c)) for t in arrays],
        out_specs=_LANES, out_shape=SDS((N_SHARD, IN_SHARD_PAD, D), bf16), name="join_w_in",
        compiler_params=_params(("parallel",), 2 * 128 * (N_SHARD * IN_SHARD_PAD + sum(t.shape[0] for t in arrays))),
    )(*arrays)


def _full_weights(gathered):
    full = {n: t.reshape((N_SHARD,) + SHARD_SHAPE[n]) for n, t in gathered.items()}
    out = {}
    if "w_in" in full:
        pieces = _split_w_in(full["w_in"])
        out.update(w_a_t=pieces[0:3], w_fox_t=pieces[3:6], w_vr_t=pieces[6], w_f_t=pieces[7], w_gab_t=pieces[8])
    if "w_out" in full:
        out.update(
            w_a4=full["w_proj_a"],
            w_b4=full["w_proj_b"],
            w_out=full["w_out"].reshape(D, D),
            w_gate_t=full["w_ffn_gate"].reshape(F_FF, D),
            w_up_t=full["w_ffn_up"].reshape(F_FF, D),
            w_down=full["w_ffn_down"].reshape(F_FF, D))
    return out


def _sharded_grads(g):
    full = dict(w_in=_join_w_in(g["w_a_t"], g["w_fox_t"], g["w_f_t"], g["w_gab_t"]), w_proj_a=g["w_a4"],
                w_proj_b=g["w_b4"], w_out=g["w_out"], w_ffn_gate=g["w_gate_t"], w_ffn_up=g["w_up_t"],
                w_ffn_down=g["w_down"])
    return {n: _halved(full[n].reshape((N_SHARD,) + SHARD_SHAPE[n])) for n in W_NAMES}


def _local_step(x, target, wt, b_forget, g_mix_pre, g_mix_post, g_ffn_pre, g_ffn_post, late=None):
    tables = _rope_tables()
    b128 = jnp.pad(b_forget, ((0, 0), (0, 128 - N_FOX)))
    dils = tuple(d for _, d in DIL_GROUPS[1:])

    hs = _norm_fwd([x] + list(_perm_rows([x], dils, "perm_x")), g_mix_pre)
    h1 = hs[0]
    if callable(wt):
        wt = wt(h1)
    qkv = [_rope_fwd(g, _mm([(hs[g], wt["w_a_t"][g])], "nt", f32, tm=1024, tn=QKV_W, name=f"proj_a_{g}"), tables)
           for g in range(3)]
    vr = _mm([(h1, wt["w_vr_t"])], "nt", bf16, tm=1024, tn=VR_W // 2, name="proj_vr")
    gab = _mm([(h1, wt["w_gab_t"])], "nt", f32, tm=512, tn=2 * D, name="proj_gab")
    fz = _mm([(h1, wt["w_f_t"])], "nt", f32, tm=1024, tn=128, name="proj_f")
    dil = [_dil_fwd(g, qkv[g]) for g in range(3)]
    out_a, lse_a = _dil_combine([o for o, _ in dil], [l for _, l in dil])
    f_q, f_k = _forget_fwd(fz, b128)
    out_b, lse_b = _fox_fwd(vr, f_q, f_k)
    if late is not None:
        wt = {**wt, **late(out_b)}
    ya, yb, merged = _merge_fwd(out_a, out_b, wt["w_a4"], wt["w_b4"], gab)
    mix, x2, h3 = _resid_norm_fwd(x, merged, wt["w_out"], g_mix_post, g_ffn_pre)
    g_act, u_act, a_act = _ffn_fwd(h3, wt["w_gate_t"], wt["w_up_t"])
    sq_err, dy, d_ff, dg_ffn_post = _loss_head(x2, a_act, wt["w_down"], g_ffn_post, target)

    grads = {}
    d_g, d_u = _ffn_bwd_act(d_ff, wt["w_down"], g_act, u_act)
    grads["w_down"] = _mm([(a_act, d_ff)], "tn", bf16, tm=FF_TN, tn=D, name="grad_w_down")
    grads["w_gate_t"] = _mm([(d_g, h3)], "tn", bf16, tm=FF_TN, tn=D, name="grad_w_gate")
    grads["w_up_t"] = _mm([(d_u, h3)], "tn", bf16, tm=FF_TN, tn=D, name="grad_w_up")
    dx2, d_mix, dg_ffn_pre, dg_mix_post = _norm_bwd_mid(dy, d_g, d_u, wt["w_gate_t"], wt["w_up_t"], x2, mix,
                                                        g_ffn_pre, g_mix_post)

    grads["w_out"] = _mm([(merged, d_mix)], "tn", bf16, tm=D, tn=D, name="grad_w_out")
    d_ya, d_yb, d_gab = _merge_bwd(d_mix, wt["w_out"], ya, yb, gab)
    grads["w_a4"], grads["w_b4"] = _branch_grads(out_a, out_b, d_ya, d_yb)
    d_out_a, delta_a, d_out_b, delta_b = _branch_bwd(d_ya, d_yb, wt["w_a4"], wt["w_b4"], out_a, out_b)

    perm = _perm_rows([d_out_a, delta_a, lse_a], dils, "perm_dil_bwd")
    aux = [(d_out_a, delta_a, lse_a)] + [tuple(perm[k * len(dils) + i] for k in range(3)) for i in range(len(dils))]
    d_qkv = []
    for g in range(3):
        dq, dk, dv = _dil_bwd(g, qkv[g], *aux[g])
        d_qkv.append(_rope_bwd(g, dq, dk, dv, tables))
    *d_fox, d_f_cols, d_f_rows = _fox_bwd(vr, f_q, f_k, lse_b, d_out_b, delta_b)
    d_z, d_b128 = _forget_bwd(fz, b128, d_f_cols, d_f_rows)

    grads["w_a_t"] = [_mm([(d_qkv[g], hs[g])], "tn", bf16, tm=QKV_W, tn=D, name=f"grad_w_a_{g}") for g in range(3)]
    grads["w_fox_t"] = [_mm([(d_fox[k], h1)], "tn", bf16, tm=FOX_W, tn=D, name=f"grad_w_fox_{k}") for k in range(3)]
    grads["w_gab_t"] = _mm([(d_gab, h1)], "tn", bf16, tm=D, tn=D, name="grad_w_gab")
    grads["w_f_t"] = _mm([(d_z, h1)], "tn", bf16, tm=128, tn=D, name="grad_w_f")
    d_h1_nat = _mm([(d_qkv[0], wt["w_a_t"][0])] + list(zip(d_fox, wt["w_fox_t"]))
                   + [(d_gab, wt["w_gab_t"]), (d_z, wt["w_f_t"])], "nn", f32, tm=512, tn=D, name="proj_in_bwd")
    d_h1_dil = [_mm([(d_qkv[g], wt["w_a_t"][g])], "nn", f32, tm=1024, tn=D, name=f"proj_a_bwd_{g}") for g in (1, 2)]
    d_h1 = _unperm_sum(d_h1_nat, d_h1_dil, dils, "unperm_d_h1")
    grad_x, dg_mix_pre = _norm_bwd_in(dx2, d_h1, x, g_mix_pre)

    small = dict(b_forget=d_b128[:, :N_FOX], norm_mix_pre=dg_mix_pre, norm_mix_post=dg_mix_post,
                 norm_ffn_pre=dg_ffn_pre, norm_ffn_post=dg_ffn_post)
    grads["mid_backward"] = d_qkv[0]
    return sq_err, grad_x, grads, small


NORMS = ("norm_mix_pre", "norm_mix_post", "norm_ffn_pre", "norm_ffn_post")
ORDER = ("w_in", "w_proj_a", "w_proj_b", "w_out", "b_forget", "w_ffn_gate", "w_ffn_up", "w_ffn_down") + NORMS


def kernel(x, w_in, w_proj_a, w_proj_b, w_out, b_forget, w_ffn_gate, w_ffn_up, w_ffn_down, norm_mix_pre, norm_mix_post, norm_ffn_pre, norm_ffn_post, loss_target, m_w_in, m_w_proj_a, m_w_proj_b, m_w_out, m_b_forget, m_w_ffn_gate, m_w_ffn_up, m_w_ffn_down, m_norm_mix_pre, m_norm_mix_post, m_norm_ffn_pre, m_norm_ffn_post, v_w_in, v_w_proj_a, v_w_proj_b, v_w_out, v_b_forget, v_w_ffn_gate, v_w_ffn_up, v_w_ffn_down, v_norm_mix_pre, v_norm_mix_post, v_norm_ffn_pre, v_norm_ffn_post):
    given = dict(w_in=w_in, w_proj_a=w_proj_a, w_proj_b=w_proj_b, w_out=w_out, w_ffn_gate=w_ffn_gate,
                 w_ffn_up=w_ffn_up, w_ffn_down=w_ffn_down)
    given_m = dict(w_in=m_w_in, w_proj_a=m_w_proj_a, w_proj_b=m_w_proj_b, w_out=m_w_out, w_ffn_gate=m_w_ffn_gate,
                   w_ffn_up=m_w_ffn_up, w_ffn_down=m_w_ffn_down)
    given_v = dict(w_in=v_w_in, w_proj_a=v_w_proj_a, w_proj_b=v_w_proj_b, w_out=v_w_out, w_ffn_gate=v_w_ffn_gate,
                   w_ffn_up=v_w_ffn_up, w_ffn_down=v_w_ffn_down)
    w, m, v = ({n: _kernel_layout(n, t[n]) for n in W_NAMES} for t in (given, given_m, given_v))
    small_w = dict(b_forget=b_forget, norm_mix_pre=norm_mix_pre, norm_mix_post=norm_mix_post,
                   norm_ffn_pre=norm_ffn_pre, norm_ffn_post=norm_ffn_post)
    small_m = dict(b_forget=m_b_forget, norm_mix_pre=m_norm_mix_pre, norm_mix_post=m_norm_mix_post,
                   norm_ffn_pre=m_norm_ffn_pre, norm_ffn_post=m_norm_ffn_post)
    small_v = dict(b_forget=v_b_forget, norm_mix_pre=v_norm_mix_pre, norm_mix_post=v_norm_mix_post,
                   norm_ffn_pre=v_norm_ffn_pre, norm_ffn_post=v_norm_ffn_post)

    own = [_halved(_pad_rows(w[n].astype(bf16), SHARD_SHAPE[n][0])) for n in W_NAMES]
    chip = 2 * lax.axis_index("x") + lax.axis_index("y")
    exchanged = {"first": _all_gather_async(own[:1], [], "all_gather_first", GATHER_FIRST_ID)}
    fill = lambda ts, mine: [lax.dynamic_update_index_in_dim(t, o, chip, 0) for t, o in zip(ts, mine)]

    def first_weights(ready):
        arrived, _ = lax.optimization_barrier((list(exchanged["first"]), ready))
        exchanged["late"] = _all_gather_async(own[1:], [arrived[0][0, 0, :16, :128]], "all_gather_late", GATHER_LATE_ID)
        return _full_weights(dict(zip(W_NAMES[:1], fill(arrived, own[:1]))))

    def late_weights(ready):
        arrived, _ = lax.optimization_barrier((list(exchanged["late"]), ready))
        return _full_weights(dict(zip(W_NAMES[1:], fill(arrived, own[1:]))))

    sq_err, grad_x, grads, small = _local_step(x[0], loss_target[0], first_weights, b_forget, norm_mix_pre,
                                               norm_mix_post, norm_ffn_pre, norm_ffn_post, late=late_weights)

    g4 = _sharded_grads(grads)
    stack = lambda t, extra: jnp.concatenate(
        [jnp.pad(t["b_forget"], ((0, 0), (0, D - N_FOX)))] + [t[n] for n in NORMS]
        + [jnp.pad(extra, ((0, SMALL_ROWS - LOSS_ROW - 1), (0, D - extra.shape[1])), constant_values=1.0)], axis=0)
    early, _ = lax.optimization_barrier((list(_pair_swap_early([g4[n] for n in W_NAMES[1:]])), grads["mid_backward"]))
    other = list(_pair_swap([g4["w_in"]])) + early
    parts = [_pair_sum(g4[n], o, "pair_sum_" + n) for n, o in zip(W_NAMES, other)]
    recv_early = _scatter_early(parts[1:])
    recv_in, small_all = _scatter_partials(parts[:1], stack(small, sq_err))

    g_shard, delta, new_m, new_v = {}, {}, {}, {}

    def finish(names, parts, recv):
        halves = [_sum_partials(p, r, "sum_partials_" + n) for n, p, r in zip(names, parts, recv)]
        theirs = _swap_halves(halves, "swap_halves_" + names[0])
        for n, mine, other_half in zip(names, halves, theirs):
            g_shard[n], delta[n], new_m[n], new_v[n] = _adamw_halves(w[n], mine, other_half, m[n], v[n], "adamw_" + n)

    (recv_in, small_all), _ = lax.optimization_barrier(((recv_in, small_all), grad_x))
    finish(W_NAMES[:1], parts[:1], [recv_in])
    recv_early, _ = lax.optimization_barrier((list(recv_early), delta["w_in"]))
    finish(W_NAMES[1:], parts[1:], recv_early)
    small_sum = _sum_small(small_all)
    loss = small_sum[LOSS_ROW, 0] * (0.5 / D)
    ones = jnp.ones((1, 128), f32)
    sd, sm, sv = _adamw(stack(small_w, ones), small_sum, stack(small_m, ones), stack(small_v, ones), "adamw_small")

    outs = [loss, grad_x[None]]
    for big, st in ((g_shard, small_sum), (delta, sd), (new_m, sm), (new_v, sv)):
        t = {n: _harness_layout(n, big[n]) for n in W_NAMES}
        t["b_forget"] = st[0:1, :N_FOX]
        for i, n in enumerate(NORMS):
            t[n] = st[i + 1:i + 2]
        outs += [t[n] for n in ORDER]
    return tuple(outs)
```

```python
import functools
import math

import jax
import jax.numpy as jnp
import numpy as np
from jax import lax
from jax.experimental import pallas as pl
from jax.experimental.pallas import tpu as pltpu
from jax.experimental.pallas import tpu_sc as plsc

f32 = jnp.float32
bf16 = jnp.bfloat16
SDS = jax.ShapeDtypeStruct
MESH = pl.DeviceIdType.MESH

S = 2048
D = 1024
HD = 64
BLK = 128
N_FOX = 8
FOX_W = N_FOX * HD
DIL_GROUPS = ((128, 1), (512, 4), (2048, 16))
SLOTS = 4
DIL_W = SLOTS * HD
QKV_W = 3 * DIL_W
VR_W = 3 * FOX_W
GF_W = 2 * D + 128
F_FF = 2816
ROPE_DIM = 16
ROPE_THETA = 500000.0
EPS = 1e-6
NEG = -1e30
SCALE = 1.0 / math.sqrt(HD)
IN_COLS = 5896
N_SHARD = 4

ADAM_LR, ADAM_B1, ADAM_B2, ADAM_EPS, ADAM_WD, ADAM_STEP = 0.001, 0.9, 0.999, 1e-08, 0.01, 10

VMEM_V7X = 64 * 1024 * 1024
VMEM_PLAN_MAX = VMEM_V7X - 8 * 1024 * 1024

TM = 512
TQ = 256

W_NAMES = ("w_in", "w_proj_a", "w_proj_b", "w_out", "w_ffn_gate", "w_ffn_up", "w_ffn_down")
TRANSPOSED = ("w_in", "w_ffn_gate", "w_ffn_up")
IN_SHARD = IN_COLS // N_SHARD
IN_SHARD_PAD = 1504
SHARD_SHAPE = dict(w_in=(IN_SHARD_PAD, D), w_proj_a=(DIL_W, D // N_SHARD), w_proj_b=(FOX_W, D // N_SHARD),
                   w_out=(D // N_SHARD, D), w_ffn_gate=(F_FF // N_SHARD, D), w_ffn_up=(F_FF // N_SHARD, D),
                   w_ffn_down=(F_FF // N_SHARD, D))
SMALL_ROWS = 8
LOSS_ROW = 5


def _nbytes(shape, dtype):
    return math.prod(shape) * jnp.dtype(dtype).itemsize


def _params(semantics, block_bytes, temp_bytes=0):
    need = 2 * block_bytes + temp_bytes + (2 << 20)
    return pltpu.CompilerParams(dimension_semantics=semantics, vmem_limit_bytes=int(min(need, VMEM_PLAN_MAX)))


def _row(w, tm=TM):
    return pl.BlockSpec((tm, w), lambda i: (i, 0))


def _vec(w):
    return pl.BlockSpec((1, w), lambda i: (0, 0))


def _mm(pairs, dims, out_dtype, *, tm, tn, name, m_inner=False):
    a0, b0 = pairs[0]
    m_dim = a0.shape[1] if dims == "tn" else a0.shape[0]
    n_dim = b0.shape[0] if dims == "nt" else b0.shape[1]
    contract = {"nn": ((1,), (0,)), "nt": ((1,), (1,)), "tn": ((0,), (0,))}[dims]
    n_pairs = len(pairs)
    assert m_dim % tm == 0 and n_dim % tn == 0, (name, m_dim, n_dim, tm, tn)

    def body(*refs):
        o_ref = refs[-1]
        acc = None
        for p in range(n_pairs):
            a = refs[2 * p][...].astype(bf16)
            b = refs[2 * p + 1][...].astype(bf16)
            t = lax.dot_general(a, b, (contract, ((), ())), preferred_element_type=f32)
            acc = t if acc is None else acc + t
        o_ref[...] = acc.astype(o_ref.dtype)

    if m_inner:
        grid = (n_dim // tn, m_dim // tm)
        mi = lambda j, i: i
        ni = lambda j, i: j
    else:
        grid = (m_dim // tm, n_dim // tn)
        mi = lambda i, j: i
        ni = lambda i, j: j
    in_specs, block_bytes, args = [], 0, []
    for a, b in pairs:
        k_dim = a.shape[0] if dims == "tn" else a.shape[1]
        if dims == "tn":
            in_specs.append(pl.BlockSpec((k_dim, tm), lambda *g: (0, mi(*g))))
        else:
            in_specs.append(pl.BlockSpec((tm, k_dim), lambda *g: (mi(*g), 0)))
        if dims == "nt":
            in_specs.append(pl.BlockSpec((tn, k_dim), lambda *g: (ni(*g), 0)))
        else:
            in_specs.append(pl.BlockSpec((k_dim, tn), lambda *g: (0, ni(*g))))
        block_bytes += _nbytes((tm, k_dim), a.dtype) + _nbytes((tn, k_dim), b.dtype)
        args += [a, b]
    block_bytes += _nbytes((tm, tn), out_dtype)
    temp = _nbytes((tm, tn), f32) * 2 + sum(_nbytes((tm, a.shape[0] if dims == "tn" else a.shape[1]), bf16)
                                            + _nbytes((tn, a.shape[0] if dims == "tn" else a.shape[1]), bf16)
                                            for a, _ in pairs)
    return pl.pallas_call(
        body, grid=grid, in_specs=in_specs,
        out_specs=pl.BlockSpec((tm, tn), lambda *g: (mi(*g), ni(*g))),
        out_shape=SDS((m_dim, n_dim), out_dtype), name=name,
        compiler_params=_params(("parallel", "parallel"), block_bytes, temp),
    )(*args)


def _rms(x, g):
    r = lax.rsqrt(jnp.mean(x * x, axis=-1, keepdims=True) + EPS)
    return x * r * g


def _rms_bwd(x, g, dy):
    r = lax.rsqrt(jnp.mean(x * x, axis=-1, keepdims=True) + EPS)
    xh = x * r
    dxh = dy * g
    dx = r * (dxh - xh * jnp.mean(dxh * xh, axis=-1, keepdims=True))
    return dx, jnp.sum(dy * xh, axis=0, keepdims=True)


def _acc_rows(ref, val):
    @pl.when(pl.program_id(0) == 0)
    def _():
        ref[...] = jnp.zeros_like(ref)
    ref[...] += val


def _norm_fwd(xs, g):
    n = len(xs)

    def body(*refs):
        g = refs[n][...]
        for x_ref, h_ref in zip(refs[:n], refs[n + 1:]):
            h_ref[...] = _rms(x_ref[...], g).astype(bf16)

    return pl.pallas_call(
        body, grid=(S // TM,), in_specs=[_row(D)] * n + [_vec(D)], out_specs=[_row(D)] * n,
        out_shape=[SDS((S, D), bf16)] * n, name="norm_mix_pre",
        compiler_params=_params(("parallel",), 6 * n * TM * D, 8 * n * TM * D))(*xs, g)


def _perm_rows(xs, ds, name):
    n = len(xs)

    def body(*refs):
        outs = iter(refs[n:])
        for x_ref in refs[:n]:
            for d in ds:
                o_ref, rows = next(outs), S // d
                for r in range(d):
                    o_ref[r * rows:(r + 1) * rows, :] = x_ref[pl.ds(r, rows, stride=d), :]

    blk = pl.BlockSpec((S, 128), lambda c: (0, c))
    w = xs[0].shape[1]
    return pl.pallas_call(
        body, grid=(w // 128,), in_specs=[blk] * n, out_specs=[blk] * (n * len(ds)),
        out_shape=[SDS((S, w), f32)] * (n * len(ds)), name=name,
        compiler_params=_params(("parallel",), 4 * S * 128 * n * (1 + len(ds))))(*xs)


def _unperm_sum(nat, perms, ds, name):
    n = len(perms)

    def body(*refs):
        a_ref, o_ref, sc = refs[0], refs[n + 1], refs[n + 2]
        acc = a_ref[...]
        for b_ref, d in zip(refs[1:n + 1], ds):
            rows = S // d
            for r in range(d):
                sc[pl.ds(r, rows, stride=d), :] = b_ref[r * rows:(r + 1) * rows, :]
            acc = acc + sc[...]
        o_ref[...] = acc

    blk = pl.BlockSpec((S, 128), lambda c: (0, c))
    w = nat.shape[1]
    return pl.pallas_call(
        body, grid=(w // 128,), in_specs=[blk] * (n + 1), out_specs=blk, out_shape=SDS((S, w), f32),
        scratch_shapes=[pltpu.VMEM((S, 128), f32)], name=name,
        compiler_params=_params(("parallel",), 4 * S * 128 * (n + 2), 8 * S * 128))(nat, *perms)


def _whole(a):
    return pl.BlockSpec(a.shape, lambda i: (0,) * a.ndim)


def _resid_norm_fwd(x, merged, w_out, g_post, g_pre):
    def body(x_ref, mg_ref, w_ref, gp_ref, gn_ref, mix_ref, x2_ref, h_ref):
        mix = jnp.dot(mg_ref[...], w_ref[...], preferred_element_type=f32)
        x2 = x_ref[...] + _rms(mix, gp_ref[...])
        mix_ref[...] = mix
        x2_ref[...] = x2
        h_ref[...] = _rms(x2, gn_ref[...]).astype(bf16)

    return pl.pallas_call(
        body, grid=(S // TM,), in_specs=[_row(D), _row(D), _whole(w_out), _vec(D), _vec(D)], out_specs=[_row(D)] * 3,
        out_shape=[SDS((S, D), f32), SDS((S, D), f32), SDS((S, D), bf16)], name="proj_out_norm",
        compiler_params=_params(("parallel",), 16 * TM * D + 2 * D * D, 16 * TM * D))(x, merged, w_out, g_post, g_pre)


def _loss_head(x2, a_act, w_down, g_post, target):
    def body(x2_ref, a_ref, w_ref, g_ref, t_ref, loss_ref, dy_ref, dff_ref, dg_ref):
        ff = jnp.dot(a_ref[...], w_ref[...], preferred_element_type=f32)
        g = g_ref[...]
        err = x2_ref[...] + _rms(ff, g) - t_ref[...]
        dy = err * (1.0 / D)
        dff, dg = _rms_bwd(ff, g, dy)
        dy_ref[...] = dy
        dff_ref[...] = dff.astype(bf16)
        _acc_rows(dg_ref, dg)
        _acc_rows(loss_ref, jnp.full((1, 128), jnp.sum(err * err), f32))

    return pl.pallas_call(
        body, grid=(S // TM,), in_specs=[_row(D), _row(F_FF), _whole(w_down), _vec(D), _row(D)],
        out_specs=[_vec(128), _row(D), _row(D), _vec(D)],
        out_shape=[SDS((1, 128), f32), SDS((S, D), f32), SDS((S, D), bf16), SDS((1, D), f32)], name="ffn_down_loss",
        compiler_params=_params(("arbitrary",), 14 * TM * D + 2 * TM * F_FF + 2 * F_FF * D, 28 * TM * D),
    )(x2, a_act, w_down, g_post, target)


def _norm_bwd_mid(dy, d_g, d_u, w_gate_t, w_up_t, x2, mix, g_ffn_pre, g_mix_post):
    def body(dy_ref, dgt_ref, dut_ref, wg_ref, wu_ref, x2_ref, mix_ref, g3_ref, g2_ref, dx2_ref, dmix_ref, dg3_ref, dg2_ref):
        dh = jnp.dot(dgt_ref[...], wg_ref[...], preferred_element_type=f32)
        dh += jnp.dot(dut_ref[...], wu_ref[...], preferred_element_type=f32)
        d3, dg3 = _rms_bwd(x2_ref[...], g3_ref[...], dh)
        dx2 = dy_ref[...] + d3
        dmix, dg2 = _rms_bwd(mix_ref[...], g2_ref[...], dx2)
        dx2_ref[...] = dx2
        dmix_ref[...] = dmix.astype(bf16)
        _acc_rows(dg3_ref, dg3)
        _acc_rows(dg2_ref, dg2)

    tm = TM // 2
    row = lambda w: _row(w, tm)
    return pl.pallas_call(
        body, grid=(S // tm,),
        in_specs=[row(D), row(F_FF), row(F_FF), _whole(w_gate_t), _whole(w_up_t), row(D), row(D), _vec(D), _vec(D)],
        out_specs=[row(D), row(D), _vec(D), _vec(D)],
        out_shape=[SDS((S, D), f32), SDS((S, D), bf16), SDS((1, D), f32), SDS((1, D), f32)], name="ffn_bwd_in_norm",
        compiler_params=_params(("arbitrary",), 18 * tm * D + 4 * tm * F_FF + 4 * F_FF * D, 28 * tm * D),
    )(dy, d_g, d_u, w_gate_t, w_up_t, x2, mix, g_ffn_pre, g_mix_post)


def _norm_bwd_in(dx2, dh1, x, g):
    def body(dx2_ref, dh_ref, x_ref, g_ref, gx_ref, dg_ref):
        d1, dg = _rms_bwd(x_ref[...], g_ref[...], dh_ref[...])
        gx_ref[...] = dx2_ref[...] + d1
        _acc_rows(dg_ref, dg)

    return pl.pallas_call(
        body, grid=(S // TM,), in_specs=[_row(D)] * 3 + [_vec(D)], out_specs=[_row(D), _vec(D)],
        out_shape=[SDS((S, D), f32), SDS((1, D), f32)], name="norm_bwd_in",
        compiler_params=_params(("arbitrary",), 16 * TM * D, 16 * TM * D))(dx2, dh1, x, g)


def _rope_tables():
    half = ROPE_DIM // 2
    inv_freq = np.power(np.float32(ROPE_THETA), -np.arange(0, ROPE_DIM, 2, dtype=np.float32) / np.float32(ROPE_DIM))
    row = np.arange(S)
    groups = []
    for _, d in DIL_GROUPS:
        pos = ((row % (S // d)) * d + row // (S // d)).astype(np.float32)
        ang = pos[:, None] * inv_freq[None, :].astype(np.float32)
        cos, sin = np.cos(ang).astype(np.float32), np.sin(ang).astype(np.float32)
        c = np.concatenate([cos, cos, np.ones((S, HD - ROPE_DIM), np.float32)], axis=1)
        s_lo = np.concatenate([-sin, np.zeros((S, HD - half), np.float32)], axis=1)
        s_hi = np.concatenate([np.zeros((S, half), np.float32), sin, np.zeros((S, HD - ROPE_DIM), np.float32)], axis=1)
        groups.append(np.stack([np.concatenate([t, t], axis=1) for t in (c, s_lo, s_hi)]))
    return jnp.asarray(np.stack(groups))


def _rotate(x, c, lo, hi, sign):
    tile = lambda t: jnp.tile(t, (1, DIL_W // 128))
    return (x * tile(c) + pltpu.roll(x, DIL_W - ROPE_DIM // 2, 1) * (tile(lo) * sign)
            + pltpu.roll(x, ROPE_DIM // 2, 1) * (tile(hi) * sign))


def _table_specs(g):
    return [pl.BlockSpec((None, None, TM, 128), lambda i, k=k: (g, k, i, 0)) for k in range(3)]


def _rope_fwd(g, p_qkv, tables):
    def body(x_ref, c_ref, lo_ref, hi_ref, o_ref):
        c, lo, hi = c_ref[...], lo_ref[...], hi_ref[...]
        for part in range(2):
            cols = slice(part * DIL_W, (part + 1) * DIL_W)
            o_ref[:, cols] = _rotate(x_ref[:, cols], c, lo, hi, 1.0).astype(bf16)
        o_ref[:, 2 * DIL_W:] = x_ref[:, 2 * DIL_W:].astype(bf16)

    return pl.pallas_call(
        body, grid=(S // TM,), in_specs=[_row(QKV_W)] + _table_specs(g), out_specs=_row(QKV_W),
        out_shape=SDS((S, QKV_W), bf16), name=f"rope_fwd_{g}",
        compiler_params=_params(("parallel",), 6 * TM * QKV_W + 12 * TM * 128, 24 * TM * QKV_W))(p_qkv, tables, tables, tables)


def _rope_bwd(g, dq, dk, dv, tables):
    def body(dq_ref, dk_ref, dv_ref, c_ref, lo_ref, hi_ref, o_ref):
        c, lo, hi = c_ref[...], lo_ref[...], hi_ref[...]
        o_ref[:, :DIL_W] = _rotate(dq_ref[...], c, lo, hi, -1.0).astype(bf16)
        o_ref[:, DIL_W:2 * DIL_W] = _rotate(dk_ref[...], c, lo, hi, -1.0).astype(bf16)
        o_ref[:, 2 * DIL_W:] = dv_ref[...].astype(bf16)

    return pl.pallas_call(
        body, grid=(S // TM,), in_specs=[_row(DIL_W)] * 3 + _table_specs(g), out_specs=_row(QKV_W),
        out_shape=SDS((S, QKV_W), bf16), name=f"rope_bwd_{g}",
        compiler_params=_params(("parallel",), 6 * TM * QKV_W + 12 * TM * 128, 24 * TM * QKV_W))(dq, dk, dv, tables, tables, tables)


def _nt(a, b):
    return lax.dot_general(a, b, (((1,), (1,)), ((), ())), preferred_element_type=f32)


def _tn(a, b):
    return lax.dot_general(a, b, (((0,), (0,)), ((), ())), preferred_element_type=f32)


STEP_BLOCKS = 4
STEP_ROWS = STEP_BLOCKS * BLK


def _dil_prev(g, b):
    _, d = DIL_GROUPS[g]
    nb = S // d // BLK
    if nb == 1 or (b == 0 and nb <= STEP_BLOCKS):
        return None
    return "in" if b > 0 else "halo"


def _bnt(a, b):
    return lax.dot_general(a, b, (((2,), (2,)), ((0,), (0,))), preferred_element_type=f32)


def _bnn(a, b):
    return lax.dot_general(a, b, (((2,), (1,)), ((0,), (0,))), preferred_element_type=f32)


def _btn(a, b):
    return lax.dot_general(a, b, (((1,), (1,)), ((0,), (0,))), preferred_element_type=f32)


def _on_tail(x, tail, fn):
    if tail == x.shape[0]:
        return fn(x)
    return jnp.concatenate([x[:-tail], fn(x[-tail:])], axis=0)


def _heads(ref, part):
    n = ref.shape[0] // BLK
    return jnp.stack([ref[b * BLK:(b + 1) * BLK, part * DIL_W + h * HD:part * DIL_W + (h + 1) * HD]
                      for b in range(n) for h in range(SLOTS)])


def _dil_operands(g, qkv_ref, halo_ref):
    q, kc, vc = (_heads(qkv_ref, part) for part in range(3))
    qi = lax.broadcasted_iota(jnp.int32, (1, BLK, BLK), 1)
    kj = lax.broadcasted_iota(jnp.int32, (1, BLK, BLK), 2)
    with_prev = [b for b in range(STEP_BLOCKS) if _dil_prev(g, b) is not None]
    tail = SLOTS * len(with_prev)
    if not tail:
        return q, kc, vc, None, None, kj <= qi, None, 0
    assert with_prev == list(range(STEP_BLOCKS - len(with_prev), STEP_BLOCKS))
    inside = SLOTS * sum(_dil_prev(g, b) == "in" for b in with_prev)
    kp, vp, prev = kc[:inside], vc[:inside], jnp.broadcast_to(kj >= qi, (inside, BLK, BLK))
    if inside < tail:
        no_halo = jnp.where(pl.program_id(0) == 0, BLK + 1, 0)
        kp = jnp.concatenate([_heads(halo_ref, 1), kp], axis=0)
        vp = jnp.concatenate([_heads(halo_ref, 2), vp], axis=0)
        prev = jnp.concatenate([jnp.broadcast_to(kj >= qi + no_halo, (SLOTS, BLK, BLK)), prev], axis=0)
    return q, kc, vc, kp, vp, kj <= qi, prev, tail


def _dil_in_specs(g, n_aux):
    step = lambda w: pl.BlockSpec((STEP_ROWS, w), lambda i: (i, 0))
    halo = [pl.BlockSpec((BLK, QKV_W), lambda i: (jnp.maximum(i * STEP_BLOCKS - 1, 0), 0))]
    needs_halo = _dil_prev(g, 0) == "halo"
    return [step(QKV_W)] + (halo if needs_halo else []) + [step(DIL_W)] * n_aux, needs_halo


def _dil_fwd(g, qkv):
    in_specs, needs_halo = _dil_in_specs(g, 0)

    def body(*refs):
        qkv_ref, halo_ref = refs[0], refs[1] if needs_halo else None
        o_ref, lse_ref = refs[-2:]
        q, kc, vc, kp, vp, cur, prev, tail = _dil_operands(g, qkv_ref, halo_ref)
        sc = jnp.where(cur, _bnt(q, kc) * SCALE, NEG)
        m = jnp.max(sc, axis=-1, keepdims=True)
        if tail:
            sp = jnp.where(prev, _bnt(q[-tail:], kp) * SCALE, NEG)
            m = _on_tail(m, tail, lambda t: jnp.maximum(t, jnp.max(sp, axis=-1, keepdims=True)))
            pp = jnp.exp(sp - m[-tail:])
        pc = jnp.exp(sc - m)
        den = jnp.sum(pc, axis=-1, keepdims=True)
        if tail:
            den = _on_tail(den, tail, lambda t: t + jnp.sum(pp, axis=-1, keepdims=True))
        inv = 1.0 / den
        o = _bnn((pc * inv).astype(bf16), vc)
        if tail:
            o = _on_tail(o, tail, lambda t: t + _bnn((pp * inv[-tail:]).astype(bf16), vp))
        lse = m + jnp.log(den)
        for b in range(STEP_BLOCKS):
            for h in range(SLOTS):
                rows, hs = slice(b * BLK, (b + 1) * BLK), slice(h * HD, (h + 1) * HD)
                o_ref[rows, hs] = o[SLOTS * b + h]
                lse_ref[rows, hs] = jnp.broadcast_to(lse[SLOTS * b + h], (BLK, HD))

    out = pl.BlockSpec((STEP_ROWS, DIL_W), lambda i: (i, 0))
    return pl.pallas_call(
        body, grid=(S // STEP_ROWS,), in_specs=in_specs, out_specs=[out, out], out_shape=[SDS((S, DIL_W), f32)] * 2,
        name=f"dil_fwd_{g}", compiler_params=_params(("parallel",), 12 * STEP_ROWS * DIL_W, 2 << 20),
    )(*([qkv] * (2 if needs_halo else 1)))


def _dil_combine(outs, lses):
    def body(o0, o1, o2, l0, l1, l2, out_ref, lse_ref, so1, so2, sl1, sl2):
        for (_, d), src, dst in ((DIL_GROUPS[1], o1, so1), (DIL_GROUPS[2], o2, so2),
                                 (DIL_GROUPS[1], l1, sl1), (DIL_GROUPS[2], l2, sl2)):
            rows = S // d
            for r in range(d):
                dst[pl.ds(r, rows, stride=d), :] = src[r * rows:(r + 1) * rows, :]
        a, b, c = l0[...], sl1[...], sl2[...]
        m = jnp.maximum(jnp.maximum(a, b), c)
        ea, eb, ec = jnp.exp(a - m), jnp.exp(b - m), jnp.exp(c - m)
        z = ea + eb + ec
        inv = 1.0 / z
        out_ref[...] = (ea * inv) * o0[...] + (eb * inv) * so1[...] + (ec * inv) * so2[...]
        lse_ref[...] = m + jnp.log(z)

    blk = pl.BlockSpec((S, 128), lambda c: (0, c))
    return pl.pallas_call(
        body, grid=(DIL_W // 128,), in_specs=[blk] * 6, out_specs=[blk] * 2,
        out_shape=[SDS((S, DIL_W), f32)] * 2, scratch_shapes=[pltpu.VMEM((S, 128), f32)] * 4, name="dil_combine",
        compiler_params=_params(("parallel",), 32 * S * 128, 32 * S * 128))(*outs, *lses)


def _dil_bwd(g, qkv, d_out, delta, lse):
    in_specs, needs_halo = _dil_in_specs(g, 3)

    def body(*refs):
        qkv_ref, halo_ref = refs[0], refs[1] if needs_halo else None
        do_ref, dl_ref, lse_ref, dq_ref, dk_ref, dv_ref = refs[-6:]
        q, kc, vc, kp, vp, cur, prev, tail = _dil_operands(g, qkv_ref, halo_ref)
        tiles = [(slice(b * BLK, (b + 1) * BLK), h) for b in range(STEP_BLOCKS) for h in range(SLOTS)]
        do = jnp.stack([do_ref[rows, h * HD:(h + 1) * HD] for rows, h in tiles]).astype(bf16)
        lse = jnp.stack([lse_ref[rows, h * HD:h * HD + 1] for rows, h in tiles])
        delta = jnp.stack([dl_ref[rows, h * HD:h * HD + 1] for rows, h in tiles])

        def probs(q, k, mask, lse, do, v, delta):
            p = jnp.exp(jnp.where(mask, _bnt(q, k) * SCALE, NEG) - lse)
            ds = p * (_bnt(do, v) - delta) * SCALE
            return p.astype(bf16), ds.astype(bf16)

        p, ds = probs(q, kc, cur, lse, do, vc, delta)
        dq, dk, dv = _bnn(ds, kc), _btn(ds, q), _btn(p, do)
        if tail:
            p, ds = probs(q[-tail:], kp, prev, lse[-tail:], do[-tail:], vp, delta[-tail:])
            dq = _on_tail(dq, tail, lambda t: t + _bnn(ds, kp))
            dk_p, dv_p = _btn(ds, q[-tail:]), _btn(p, do[-tail:])
            inside = tail - SLOTS if needs_halo else tail
            pad = jnp.zeros((len(tiles) - inside, BLK, HD), f32)
            dk = dk + jnp.concatenate([dk_p[tail - inside:], pad], axis=0)
            dv = dv + jnp.concatenate([dv_p[tail - inside:], pad], axis=0)
        first = pl.multiple_of(pl.program_id(0) * STEP_ROWS, STEP_ROWS)
        for t, (rows, h) in enumerate(tiles):
            hs = slice(h * HD, (h + 1) * HD)
            own = pl.ds(pl.multiple_of(first + rows.start, BLK), BLK)
            dq_ref[rows, hs] = dq[t]
            dk_ref[own, hs] = dk[t]
            dv_ref[own, hs] = dv[t]
        if needs_halo:
            before = pl.ds(pl.multiple_of(jnp.maximum(first - BLK, 0), BLK), BLK)
            for h in range(SLOTS):
                hs = slice(h * HD, (h + 1) * HD)
                dk_ref[before, hs] += dk_p[h]
                dv_ref[before, hs] += dv_p[h]

    whole = pl.BlockSpec((S, DIL_W), lambda i: (0, 0))
    return pl.pallas_call(
        body, grid=(S // STEP_ROWS,), in_specs=in_specs,
        out_specs=[pl.BlockSpec((STEP_ROWS, DIL_W), lambda i: (i, 0)), whole, whole],
        out_shape=[SDS((S, DIL_W), f32)] * 3, name=f"dil_bwd_{g}",
        compiler_params=_params(("arbitrary",), 20 * STEP_ROWS * DIL_W + 8 * S * DIL_W, 2 << 20),
    )(*([qkv] * (2 if needs_halo else 1)), d_out, delta, lse)


def _scan_rows(x, reverse):
    row = lax.broadcasted_iota(jnp.int32, x.shape, 0)
    k = 1
    while k < S:
        if reverse:
            x = x + jnp.where(row < S - k, pltpu.roll(x, S - k, 0), 0.0)
        else:
            x = x + jnp.where(row >= k, pltpu.roll(x, k, 0), 0.0)
        k *= 2
    return x


N_PAIR = N_FOX // 2
_PAIR_Q = pl.BlockSpec((None, S, 128), lambda p: (p, 0, 0))
_PAIR_K = pl.BlockSpec((None, 8, S), lambda p: (p, 0, 0))


def _forget_fwd(fz, b128):
    def body(z_ref, b_ref, fq_ref, fk_ref):
        z = z_ref[...] + b_ref[...]
        logf = jnp.minimum(z, 0.0) - jnp.log1p(jnp.exp(-jnp.abs(z)))
        f_cum = _scan_rows(logf, reverse=False)
        f_cum_t = f_cum.T
        fq_ref[...] = jnp.zeros_like(fq_ref)
        fk_ref[...] = jnp.zeros_like(fk_ref)
        for p in range(N_PAIR):
            fq_ref[p, :, 0:2] = f_cum[:, 2 * p:2 * p + 2]
            fk_ref[p, 0:2, :] = f_cum_t[2 * p:2 * p + 2, :]

    return pl.pallas_call(
        body, grid=(1,), in_specs=[pl.BlockSpec((S, 128), lambda i: (0, 0)), _vec(128)],
        out_specs=[pl.BlockSpec((N_PAIR, S, 128), lambda i: (0, 0, 0)), pl.BlockSpec((N_PAIR, 8, S), lambda i: (0, 0, 0))],
        out_shape=[SDS((N_PAIR, S, 128), f32), SDS((N_PAIR, 8, S), f32)], name="forget_fwd",
        compiler_params=_params(("arbitrary",), 24 * S * 128, 24 * S * 128))(fz, b128)


def _forget_bwd(fz, b128, d_f_cols, d_f_rows):
    def body(z_ref, b_ref, dfc_ref, dfr_ref, dz_ref, db_ref, df_sc):
        z = z_ref[...] + b_ref[...]
        df_sc[...] = jnp.zeros_like(df_sc)
        for p in range(N_PAIR):
            df_sc[:, 2 * p:2 * p + 2] = dfr_ref[p, :, 0:2] + dfc_ref[p].T[:, 0:2]
        dz = _scan_rows(df_sc[...], reverse=True) * jax.nn.sigmoid(-z)
        dz_ref[...] = dz
        db_ref[...] = jnp.sum(dz, axis=0, keepdims=True)

    full = pl.BlockSpec((S, 128), lambda i: (0, 0))
    return pl.pallas_call(
        body, grid=(1,),
        in_specs=[full, _vec(128), pl.BlockSpec((N_PAIR, 8, S), lambda i: (0, 0, 0)), pl.BlockSpec((N_PAIR, S, 128), lambda i: (0, 0, 0))],
        out_specs=[full, _vec(128)], out_shape=[SDS((S, 128), f32), SDS((1, 128), f32)],
        scratch_shapes=[pltpu.VMEM((S, 128), f32)], name="forget_bwd",
        compiler_params=_params(("arbitrary",), 32 * S * 128, 24 * S * 128))(fz, b128, d_f_cols, d_f_rows)


def _fox_scores(q_ref, k_ref, fq_ref, fk_ref, qi, hh):
    n = (qi + 1) * TQ
    rows, hs = slice(qi * TQ, n), slice(hh * HD, (hh + 1) * HD)
    q = q_ref[rows, hs] * SCALE
    s = _nt(q, k_ref[0:n, hs]) + (fq_ref[rows, hh:hh + 1] - fk_ref[hh:hh + 1, 0:n])
    below = lax.broadcasted_iota(jnp.int32, (TQ, TQ), 1) <= lax.broadcasted_iota(jnp.int32, (TQ, TQ), 0)
    diag = jnp.where(below, s[:, n - TQ:], NEG)
    return diag if qi == 0 else jnp.concatenate([s[:, :n - TQ], diag], axis=1)


def _pair_cols(first):
    return pl.BlockSpec((S, 128), lambda p: (0, first + p))


def _fox_fwd(vr, fq, fk):
    def body(q_ref, k_ref, v_ref, fq_ref, fk_ref, o_ref, lse_ref):
        lse_ref[...] = jnp.zeros_like(lse_ref)
        for hh in range(2):
            hs = slice(hh * HD, (hh + 1) * HD)
            for qi in range(S // TQ):
                n = (qi + 1) * TQ
                rows = slice(qi * TQ, n)
                s = _fox_scores(q_ref, k_ref, fq_ref, fk_ref, qi, hh)
                m = jnp.max(s, axis=-1, keepdims=True)
                p = jnp.exp(s - m)
                den = jnp.sum(p, axis=-1, keepdims=True)
                o_ref[rows, hs] = jnp.dot((p * (1.0 / den)).astype(bf16), v_ref[0:n, hs], preferred_element_type=f32)
                lse_ref[rows, hh:hh + 1] = m + jnp.log(den)

    return pl.pallas_call(
        body, grid=(N_PAIR,), in_specs=[_pair_cols(0), _pair_cols(N_PAIR), _pair_cols(2 * N_PAIR), _PAIR_Q, _PAIR_K],
        out_specs=[_pair_cols(0), _PAIR_Q], out_shape=[SDS((S, FOX_W), f32), SDS((N_PAIR, S, 128), f32)],
        name="fox_fwd", compiler_params=_params(("parallel",), 12 * S * 128, 16 * TQ * S),
    )(vr, vr, vr, fq, fk)


def _fox_bwd(vr, fq, fk, lse, d_out, delta):
    def body(q_ref, k_ref, v_ref, do_ref, fq_ref, fk_ref, lse_ref, dl_ref, dq_ref, dk_ref, dv_ref, dfc_ref, dfr_ref,
             dk_sc, dv_sc):
        dfc_ref[...] = jnp.zeros_like(dfc_ref)
        dfr_ref[...] = jnp.zeros_like(dfr_ref)
        for hh in range(2):
            hs = slice(hh * HD, (hh + 1) * HD)
            dk_sc[...] = jnp.zeros_like(dk_sc)
            dv_sc[...] = jnp.zeros_like(dv_sc)
            for qi in range(S // TQ):
                n = (qi + 1) * TQ
                rows = slice(qi * TQ, n)
                q, do, k, v = q_ref[rows, hs], do_ref[rows, hs], k_ref[0:n, hs], v_ref[0:n, hs]
                p = jnp.exp(_fox_scores(q_ref, k_ref, fq_ref, fk_ref, qi, hh) - lse_ref[rows, hh:hh + 1])
                ds = p * (_nt(do, v) - dl_ref[rows, hh:hh + 1])
                dsb = ds.astype(bf16)
                dq_ref[rows, hs] = jnp.dot(dsb, k, preferred_element_type=f32) * SCALE
                dk_sc[0:n, :] += _tn(dsb, q) * SCALE
                dv_sc[0:n, :] += _tn(p.astype(bf16), do)
                dfc_ref[hh:hh + 1, 0:n] -= jnp.sum(ds, axis=0, keepdims=True)
                dfr_ref[rows, hh:hh + 1] = jnp.sum(ds, axis=-1, keepdims=True)
            dk_ref[:, hs] = dk_sc[...]
            dv_ref[:, hs] = dv_sc[...]

    cols = [_pair_cols(k * N_PAIR) for k in range(3)]
    return pl.pallas_call(
        body, grid=(N_PAIR,), in_specs=cols + [_pair_cols(0), _PAIR_Q, _PAIR_K, _PAIR_Q, _PAIR_Q],
        out_specs=[_pair_cols(0)] * 3 + [_PAIR_K, _PAIR_Q],
        out_shape=[SDS((S, FOX_W), f32)] * 3 + [SDS((N_PAIR, 8, S), f32), SDS((N_PAIR, S, 128), f32)],
        scratch_shapes=[pltpu.VMEM((S, HD), f32)] * 2, name="fox_bwd",
        compiler_params=_params(("parallel",), 32 * S * 128, 24 * TQ * S),
    )(vr, vr, vr, d_out, fq, fk, lse, delta)


def _merge_fwd(out_a, out_b, w_a, w_b, gf):
    cw = D // N_SHARD

    def body(oa_ref, ob_ref, wa_ref, wb_ref, ga_ref, gb_ref, ya_ref, yb_ref, mg_ref):
        oa, ob = oa_ref[...].astype(bf16), ob_ref[...].astype(bf16)
        for j in range(N_SHARD):
            cols = slice(j * cw, (j + 1) * cw)
            ya = jnp.dot(oa, wa_ref[j], preferred_element_type=f32)
            yb = jnp.dot(ob, wb_ref[j], preferred_element_type=f32)
            ya_ref[:, cols] = ya
            yb_ref[:, cols] = yb
            mg_ref[:, cols] = (jax.nn.sigmoid(ga_ref[:, cols]) * ya + jax.nn.sigmoid(gb_ref[:, cols]) * yb).astype(bf16)

    full = lambda a: pl.BlockSpec(a.shape, lambda i: (0, 0, 0))
    return pl.pallas_call(
        body, grid=(S // TM,),
        in_specs=[_row(DIL_W), _row(FOX_W), full(w_a), full(w_b), _row(D), pl.BlockSpec((TM, D), lambda i: (i, 1))],
        out_specs=[_row(D)] * 3, out_shape=[SDS((S, D), f32), SDS((S, D), f32), SDS((S, D), bf16)], name="merge_fwd",
        compiler_params=_params(("parallel",), 22 * TM * D + 2 * (DIL_W + FOX_W) * D, 16 * TM * D),
    )(out_a, out_b, w_a, w_b, gf, gf)


def _merge_bwd(d_mix, w_out, ya, yb, gf):
    def body(dx_ref, w_ref, ya_ref, yb_ref, ga_ref, gb_ref, dya_ref, dyb_ref, dg_ref):
        dm = _nt(dx_ref[...], w_ref[...])
        sa, sb = jax.nn.sigmoid(ga_ref[...]), jax.nn.sigmoid(gb_ref[...])
        dya_ref[...] = (dm * sa).astype(bf16)
        dyb_ref[...] = (dm * sb).astype(bf16)
        dg_ref[:, :D] = (dm * ya_ref[...] * sa * (1.0 - sa)).astype(bf16)
        dg_ref[:, D:] = (dm * yb_ref[...] * sb * (1.0 - sb)).astype(bf16)

    return pl.pallas_call(
        body, grid=(S // TM,),
        in_specs=[_row(D), _whole(w_out)] + [_row(D)] * 3 + [pl.BlockSpec((TM, D), lambda i: (i, 1))],
        out_specs=[_row(D), _row(D), _row(2 * D)],
        out_shape=[SDS((S, D), bf16), SDS((S, D), bf16), SDS((S, 2 * D), bf16)], name="proj_out_bwd_merge",
        compiler_params=_params(("parallel",), 26 * TM * D + 2 * D * D, 28 * TM * D))(d_mix, w_out, ya, yb, gf, gf)


def _branch_bwd(d_ya, d_yb, w_a, w_b, out_a, out_b):
    cw = D // N_SHARD

    def body(dya_ref, dyb_ref, wa_ref, wb_ref, oa_ref, ob_ref, doa_ref, dla_ref, dob_ref, dlb_ref):
        doa = jnp.zeros((TM, DIL_W), f32)
        dob = jnp.zeros((TM, FOX_W), f32)
        for j in range(N_SHARD):
            cols = slice(j * cw, (j + 1) * cw)
            doa += _nt(dya_ref[:, cols], wa_ref[j])
            dob += _nt(dyb_ref[:, cols], wb_ref[j])
        doa_ref[...] = doa
        dob_ref[...] = dob.astype(bf16)
        prod_a = doa * oa_ref[...]
        for h in range(SLOTS):
            hs = slice(h * HD, (h + 1) * HD)
            dla_ref[:, hs] = jnp.broadcast_to(jnp.sum(prod_a[:, hs], axis=-1, keepdims=True), (TM, HD))
        prod_b = dob * ob_ref[...]
        dlb_ref[...] = jnp.zeros_like(dlb_ref)
        for h in range(N_FOX):
            dlb_ref[h // 2, :, h % 2:h % 2 + 1] = jnp.sum(prod_b[:, h * HD:(h + 1) * HD], axis=-1, keepdims=True)

    full = lambda a: pl.BlockSpec(a.shape, lambda i: (0, 0, 0))
    return pl.pallas_call(
        body, grid=(S // TM,),
        in_specs=[_row(D), _row(D), full(w_a), full(w_b), _row(DIL_W), _row(FOX_W)],
        out_specs=[_row(DIL_W), _row(DIL_W), _row(FOX_W), pl.BlockSpec((N_PAIR, TM, 128), lambda i: (0, i, 0))],
        out_shape=[SDS((S, DIL_W), f32), SDS((S, DIL_W), f32), SDS((S, FOX_W), bf16), SDS((N_PAIR, S, 128), f32)],
        name="branch_bwd", compiler_params=_params(("parallel",), 8 * TM * D + 2 * (DIL_W + FOX_W) * D, 8 * TM * D),
    )(d_ya, d_yb, w_a, w_b, out_a, out_b)


def _branch_grads(out_a, out_b, d_ya, d_yb):
    cw = D // N_SHARD

    def body(oa_ref, ob_ref, dya_ref, dyb_ref, ga_ref, gb_ref):
        ga_ref[...] = _tn(oa_ref[...].astype(bf16), dya_ref[...]).astype(bf16)
        gb_ref[...] = _tn(ob_ref[...].astype(bf16), dyb_ref[...]).astype(bf16)

    whole = lambda w: pl.BlockSpec((S, w), lambda j: (0, 0))
    cols = pl.BlockSpec((S, cw), lambda j: (0, j))
    return pl.pallas_call(
        body, grid=(N_SHARD,), in_specs=[whole(DIL_W), whole(FOX_W), cols, cols],
        out_specs=[pl.BlockSpec((None, DIL_W, cw), lambda j: (j, 0, 0)), pl.BlockSpec((None, FOX_W, cw), lambda j: (j, 0, 0))],
        out_shape=[SDS((N_SHARD, DIL_W, cw), bf16), SDS((N_SHARD, FOX_W, cw), bf16)], name="grad_w_proj_ab",
        compiler_params=_params(("parallel",), 4 * S * (DIL_W + FOX_W) + 4 * S * cw + 4 * (DIL_W + FOX_W) * cw,
                                4 * S * (DIL_W + FOX_W)))(out_a, out_b, d_ya, d_yb)


FF_TN = F_FF // 2
FF_TM = 1024


def _ffn_fwd(h, w_gate_t, w_up_t):
    def body(h_ref, wg_ref, wu_ref, g_ref, u_ref, a_ref):
        hb = h_ref[...]
        g = _nt(hb, wg_ref[...])
        u = _nt(hb, wu_ref[...])
        g_ref[...] = g
        u_ref[...] = u
        a_ref[...] = (g * jax.nn.sigmoid(g) * u).astype(bf16)

    tile = pl.BlockSpec((FF_TM, FF_TN), lambda j, i: (i, j))
    wspec = pl.BlockSpec((FF_TN, D), lambda j, i: (j, 0))
    return pl.pallas_call(
        body, grid=(F_FF // FF_TN, S // FF_TM),
        in_specs=[pl.BlockSpec((FF_TM, D), lambda j, i: (i, 0)), wspec, wspec], out_specs=[tile] * 3,
        out_shape=[SDS((S, F_FF), f32), SDS((S, F_FF), f32), SDS((S, F_FF), bf16)], name="ffn_fwd",
        compiler_params=_params(("parallel", "parallel"), 2 * FF_TM * D + 4 * D * FF_TN + 10 * FF_TM * FF_TN, 16 * FF_TM * FF_TN),
    )(h, w_gate_t, w_up_t)


def _ffn_bwd_act(d_ff, w_down, g_act, u_act):
    def body(d_ref, wd_ref, g_ref, u_ref, dg_ref, du_ref):
        da = _nt(d_ref[...], wd_ref[...])
        g = g_ref[...]
        sg = jax.nn.sigmoid(g)
        du_ref[...] = (da * g * sg).astype(bf16)
        dg_ref[...] = (da * u_ref[...] * sg * (1.0 + g * (1.0 - sg))).astype(bf16)

    tile = pl.BlockSpec((FF_TM, FF_TN), lambda j, i: (i, j))
    return pl.pallas_call(
        body, grid=(F_FF // FF_TN, S // FF_TM),
        in_specs=[pl.BlockSpec((FF_TM, D), lambda j, i: (i, 0)), pl.BlockSpec((FF_TN, D), lambda j, i: (j, 0)), tile, tile],
        out_specs=[tile, tile], out_shape=[SDS((S, F_FF), bf16)] * 2, name="ffn_bwd_act",
        compiler_params=_params(("parallel", "parallel"), 2 * FF_TM * D + 2 * D * FF_TN + 12 * FF_TM * FF_TN, 16 * FF_TM * FF_TN),
    )(d_ff, w_down, g_act, u_act)


def _row_tile(rows):
    return next(t for t in (376, 128, 176, 64, 32, 16, 8) if rows % t == 0)


def _adamw_math(w, g, m, v):
    c1 = 1.0 - ADAM_B1 ** ADAM_STEP
    c2 = 1.0 - ADAM_B2 ** ADAM_STEP
    m_new = ADAM_B1 * m + (1.0 - ADAM_B1) * g
    v_new = ADAM_B2 * v + (1.0 - ADAM_B2) * (g * g)
    return -ADAM_LR * ((m_new / c1) / (jnp.sqrt(v_new / c2) + ADAM_EPS) + ADAM_WD * w), m_new, v_new


def _adamw(w, g, m, v, name):
    rows, cols = w.shape
    tm = _row_tile(rows)

    def body(w_ref, g_ref, m_ref, v_ref, d_ref, nm_ref, nv_ref):
        d_ref[...], nm_ref[...], nv_ref[...] = _adamw_math(w_ref[...], g_ref[...], m_ref[...], v_ref[...])

    spec = pl.BlockSpec((tm, cols), lambda i: (i, 0))
    return pl.pallas_call(
        body, grid=(rows // tm,), in_specs=[spec] * 4, out_specs=[spec] * 3, out_shape=[SDS(w.shape, f32)] * 3,
        name=name, compiler_params=_params(("parallel",), 28 * tm * cols, 16 * tm * cols))(w, g, m, v)


def _adamw_halves(w, g_mine, g_theirs, m, v, name):
    cols = w.shape[1]
    tm = _row_tile(g_mine.shape[0])
    per_half = g_mine.shape[0] // tm
    assert 2 * g_mine.shape[0] - w.shape[0] < tm
    core = lax.axis_index("c").astype(jnp.int32).reshape(1)

    def body(c_ref, w_ref, gm_ref, gt_ref, m_ref, v_ref, g_ref, d_ref, nm_ref, nv_ref):
        mine = pl.program_id(0) // per_half == c_ref[0]
        g = jnp.where(mine, gm_ref[...], gt_ref[...])
        g_ref[...] = g
        d_ref[...], nm_ref[...], nv_ref[...] = _adamw_math(w_ref[...], g, m_ref[...], v_ref[...])

    spec = pl.BlockSpec((tm, cols), lambda i, c_ref: (i, 0))
    in_half = lambda i, first: jnp.clip(i - first * per_half, 0, per_half - 1)
    grid_spec = pltpu.PrefetchScalarGridSpec(
        num_scalar_prefetch=1, grid=(2 * per_half,),
        in_specs=[spec, pl.BlockSpec((tm, cols), lambda i, c_ref: (in_half(i, c_ref[0]), 0)),
                  pl.BlockSpec((tm, cols), lambda i, c_ref: (in_half(i, 1 - c_ref[0]), 0)), spec, spec],
        out_specs=[spec] * 4)
    return pl.pallas_call(
        body, grid_spec=grid_spec, out_shape=[SDS(w.shape, f32)] * 4, name=name,
        compiler_params=_params(("parallel",), 36 * tm * cols, 16 * tm * cols))(core, w, g_mine, g_theirs, m, v)


_ANY = pl.BlockSpec(memory_space=pl.ANY)


def _place():
    x, y, c = lax.axis_index("x"), lax.axis_index("y"), lax.axis_index("c")
    chips = [(1 - x, y), (x, 1 - y), (1 - x, 1 - y)]
    return x, y, c, chips


def _halved(t):
    return t.reshape(t.shape[:-2] + (2, t.shape[-2] // 2, t.shape[-1]))


def _gather_body(src, out, send_ici, recv_ici, send_d2d, recv_d2d):
    x, y, c, chips = _place()
    sibling = (x, y, 1 - c)
    me_j = 2 * x + y
    sends = []
    for a in range(len(src)):
        for p in range(3):
            cp = pltpu.make_async_remote_copy(
                src_ref=src[a].at[c], dst_ref=out[a].at[me_j, c], send_sem=send_ici.at[a, p],
                recv_sem=recv_ici.at[a, p], device_id=(*chips[p], c), device_id_type=MESH)
            cp.start()
            sends.append(cp)
    for a in range(len(src)):
        for p, (px, py) in enumerate(chips):
            blk = out[a].at[2 * px + py, c]
            pltpu.make_async_remote_copy(
                src_ref=blk, dst_ref=blk, send_sem=send_ici.at[a, p], recv_sem=recv_ici.at[a, p],
                device_id=sibling, device_id_type=MESH).wait_recv()
            fw = pltpu.make_async_remote_copy(
                src_ref=blk, dst_ref=blk, send_sem=send_d2d.at[a, p], recv_sem=recv_d2d.at[a, p],
                device_id=sibling, device_id_type=MESH)
            fw.start()
            sends.append(fw)
    for a in range(len(src)):
        for p, (px, py) in enumerate(chips):
            blk = out[a].at[2 * px + py, 1 - c]
            pltpu.make_async_remote_copy(
                src_ref=blk, dst_ref=blk, send_sem=send_d2d.at[a, p], recv_sem=recv_d2d.at[a, p],
                device_id=sibling, device_id_type=MESH).wait_recv()
    for cp in sends:
        cp.wait_send()


def _handshake(peers):
    barrier = pltpu.get_barrier_semaphore()
    for peer in peers:
        pl.semaphore_signal(barrier, inc=1, device_id=peer, device_id_type=MESH)
    pl.semaphore_wait(barrier, len(peers))


_SEQUENCER = dict(axis_name="sequencer", num_cores=1)
GATHER_LATE_ID, SCATTER_EARLY_ID, SWAP_EARLY_ID, GATHER_FIRST_ID, SCATTER_LATE_ID = 1, 2, 3, 4, 5


def _all_gather_async(shards, after, name, collective_id):
    n, k = len(shards), len(after)

    def body(*refs):
        x, y, c, chips = _place()
        _handshake([(*chip, c) for chip in chips] + [(x, y, 1 - c)])
        _gather_body(refs[:n], refs[n + k:2 * n + k], *refs[2 * n + k:])

    return pl.kernel(
        body, out_type=[SDS((N_SHARD,) + t.shape, t.dtype) for t in shards],
        mesh=plsc.ScalarSubcoreMesh(**_SEQUENCER), scratch_types=[pltpu.SemaphoreType.DMA((n, 3))] * 4,
        compiler_params=pltpu.CompilerParams(collective_id=collective_id), name=name)(*shards, *after)


def _pair_swap(grads):
    n = len(grads)

    def body(*refs):
        src, out, send_sems, recv_sems = refs[:n], refs[n:2 * n], refs[2 * n], refs[2 * n + 1]
        x, y, c, _ = _place()
        copies = [pltpu.make_async_remote_copy(
            src_ref=src[a].at[:, 1 - c], dst_ref=out[a], send_sem=send_sems.at[a], recv_sem=recv_sems.at[a],
            device_id=(x, y, 1 - c), device_id_type=MESH) for a in range(n)]
        for cp in copies:
            cp.start()
        for cp in copies:
            cp.wait()

    return pl.pallas_call(
        body, in_specs=[_ANY] * n, out_specs=[_ANY] * n,
        out_shape=[SDS((N_SHARD,) + t.shape[2:], t.dtype) for t in grads],
        scratch_shapes=[pltpu.SemaphoreType.DMA((n,)), pltpu.SemaphoreType.DMA((n,))], name="pair_swap",
        compiler_params=pltpu.CompilerParams(has_side_effects=True))(*grads)


def _pair_swap_early(grads):
    n = len(grads)

    def body(*refs):
        src, out, send_sems, recv_sems = refs[:n], refs[n:2 * n], refs[2 * n], refs[2 * n + 1]
        x, y, c, _ = _place()
        _handshake([(x, y, 1 - c)])
        copies = [pltpu.make_async_remote_copy(
            src_ref=src[a].at[:, 1 - c], dst_ref=out[a], send_sem=send_sems.at[a], recv_sem=recv_sems.at[a],
            device_id=(x, y, 1 - c), device_id_type=MESH) for a in range(n)]
        for cp in copies:
            cp.start()
        for cp in copies:
            cp.wait()

    return pl.kernel(
        body, out_type=[SDS((N_SHARD,) + t.shape[2:], t.dtype) for t in grads],
        mesh=plsc.ScalarSubcoreMesh(**_SEQUENCER), scratch_types=[pltpu.SemaphoreType.DMA((n,))] * 2,
        compiler_params=pltpu.CompilerParams(collective_id=SWAP_EARLY_ID), name="pair_swap_early")(*grads)


def _scatter_early(parts):
    n = len(parts)

    def body(*refs):
        part, recv, send_sems, recv_sems = refs[:n], refs[n:2 * n], refs[2 * n], refs[2 * n + 1]
        x, y, c, chips = _place()
        _handshake([(*chip, c) for chip in chips])
        me_j = 2 * x + y
        sends = []
        for a in range(n):
            for p, (px, py) in enumerate(chips):
                cp = pltpu.make_async_remote_copy(
                    src_ref=part[a].at[2 * px + py], dst_ref=recv[a].at[me_j], send_sem=send_sems.at[a, p],
                    recv_sem=recv_sems.at[a, p], device_id=(px, py, c), device_id_type=MESH)
                cp.start()
                sends.append(cp)
        for a in range(n):
            for p, (px, py) in enumerate(chips):
                slot = recv[a].at[2 * px + py]
                pltpu.make_async_remote_copy(
                    src_ref=slot, dst_ref=slot, send_sem=send_sems.at[a, p], recv_sem=recv_sems.at[a, p],
                    device_id=(px, py, c), device_id_type=MESH).wait_recv()
        for cp in sends:
            cp.wait_send()

    return pl.kernel(
        body, out_type=[SDS(t.shape, t.dtype) for t in parts],
        mesh=plsc.ScalarSubcoreMesh(**_SEQUENCER), scratch_types=[pltpu.SemaphoreType.DMA((n, 3))] * 2,
        compiler_params=pltpu.CompilerParams(collective_id=SCATTER_EARLY_ID), name="scatter_early")(*parts)


def _pair_sum(grads, other, name):
    _, _, rows, cols = grads.shape
    tr = _row_tile(rows)
    core = lax.axis_index("c").astype(jnp.int32).reshape(1)

    def body(c_ref, g_ref, o_ref, out_ref):
        out_ref[...] = (g_ref[...].astype(f32) + o_ref[...].astype(f32)).astype(bf16)

    grid_spec = pltpu.PrefetchScalarGridSpec(
        num_scalar_prefetch=1, grid=(N_SHARD, rows // tr),
        in_specs=[pl.BlockSpec((None, None, tr, cols), lambda j, i, c_ref: (j, c_ref[0], i, 0)),
                  pl.BlockSpec((None, tr, cols), lambda j, i, c_ref: (j, i, 0))],
        out_specs=pl.BlockSpec((None, tr, cols), lambda j, i, c_ref: (j, i, 0)))
    return pl.pallas_call(
        body, grid_spec=grid_spec, out_shape=SDS((N_SHARD, rows, cols), bf16), name=name,
        compiler_params=_params(("parallel", "parallel"), 10 * tr * cols, 12 * tr * cols))(core, grads, other)


def _scatter_partials(parts, small):
    n = len(parts)

    def body(*refs):
        part, small_ref, recv, small_all_ref = refs[:n], refs[n], refs[n + 1:2 * n + 1], refs[2 * n + 1]
        send_sems, recv_sems, ssend, srecv, local_sem = refs[2 * n + 2:]
        x, y, c, chips = _place()
        flip = lambda a, bit: 1 - a if bit else a
        peers = [(flip(x, k & 4), flip(y, k & 2), flip(c, k & 1)) for k in range(1, 8)]
        _handshake(peers)
        me_j = 2 * x + y
        me_dev = 4 * x + 2 * y + c
        own = pltpu.make_async_copy(small_ref, small_all_ref.at[me_dev], local_sem)
        own.start()
        sends = []
        for a in range(n):
            for p, (px, py) in enumerate(chips):
                cp = pltpu.make_async_remote_copy(
                    src_ref=part[a].at[2 * px + py], dst_ref=recv[a].at[me_j], send_sem=send_sems.at[a, p],
                    recv_sem=recv_sems.at[a, p], device_id=(px, py, c), device_id_type=MESH)
                cp.start()
                sends.append(cp)
        for k, to in enumerate(peers):
            cp = pltpu.make_async_remote_copy(
                src_ref=small_ref, dst_ref=small_all_ref.at[me_dev],
                send_sem=ssend.at[k], recv_sem=srecv.at[k], device_id=to, device_id_type=MESH)
            cp.start()
            sends.append(cp)
        for a in range(n):
            for p, (px, py) in enumerate(chips):
                slot = recv[a].at[2 * px + py]
                pltpu.make_async_remote_copy(
                    src_ref=slot, dst_ref=slot, send_sem=send_sems.at[a, p], recv_sem=recv_sems.at[a, p],
                    device_id=(px, py, c), device_id_type=MESH).wait_recv()
        for k, (px, py, pc) in enumerate(peers):
            slot = small_all_ref.at[4 * px + 2 * py + pc]
            pltpu.make_async_remote_copy(
                src_ref=slot, dst_ref=slot, send_sem=ssend.at[k], recv_sem=srecv.at[k],
                device_id=(px, py, pc), device_id_type=MESH).wait_recv()
        for cp in sends:
            cp.wait_send()
        own.wait()

    return pl.kernel(
        body, out_type=[SDS(t.shape, t.dtype) for t in parts] + [SDS((8, SMALL_ROWS, D), f32)],
        mesh=plsc.ScalarSubcoreMesh(**_SEQUENCER),
        scratch_types=[pltpu.SemaphoreType.DMA((n, 3)), pltpu.SemaphoreType.DMA((n, 3)),
                       pltpu.SemaphoreType.DMA((7,)), pltpu.SemaphoreType.DMA((7,)), pltpu.SemaphoreType.DMA],
        compiler_params=pltpu.CompilerParams(collective_id=SCATTER_LATE_ID), name="scatter_partials")(*parts, small)


def _sum_partials(part, recv, name):
    _, rows, cols = recv.shape
    tr = _row_tile(rows)
    me = (2 * lax.axis_index("x") + lax.axis_index("y")).astype(jnp.int32).reshape(1)

    def body(me_ref, mine, r0, r1, r2, r3, out_ref):
        acc = None
        for j, r in enumerate((r0, r1, r2, r3)):
            term = jnp.where(me_ref[0] == j, mine[...], r[...]).astype(f32)
            acc = term if acc is None else acc + term
        out_ref[...] = acc

    slot = lambda j: pl.BlockSpec((None, tr, cols), lambda i, me_ref: (jnp.where(me_ref[0] == j, j ^ 1, j), i, 0))
    grid_spec = pltpu.PrefetchScalarGridSpec(
        num_scalar_prefetch=1, grid=(rows // tr,),
        in_specs=[pl.BlockSpec((None, tr, cols), lambda i, me_ref: (me_ref[0], i, 0)), slot(0), slot(1), slot(2), slot(3)],
        out_specs=pl.BlockSpec((tr, cols), lambda i, me_ref: (i, 0)))
    return pl.pallas_call(
        body, grid_spec=grid_spec, out_shape=SDS((rows, cols), f32), name=name,
        compiler_params=_params(("parallel",), 14 * tr * cols, 12 * tr * cols))(me, part, recv, recv, recv, recv)


def _sum_small(small_all):
    def body(small_ref, out_ref):
        tot = small_ref[0]
        for k in range(1, 8):
            tot = tot + small_ref[k]
        out_ref[...] = tot

    return pl.pallas_call(
        body, grid=(1,), in_specs=[pl.BlockSpec((8, SMALL_ROWS, D), lambda i: (0, 0, 0))],
        out_specs=pl.BlockSpec((SMALL_ROWS, D), lambda i: (0, 0)), out_shape=SDS((SMALL_ROWS, D), f32),
        name="sum_small", compiler_params=_params(("arbitrary",), 36 * SMALL_ROWS * D))(small_all)


def _swap_halves(halves, name):
    n = len(halves)

    def body(*refs):
        src, out, send_sems, recv_sems = refs[:n], refs[n:2 * n], refs[2 * n], refs[2 * n + 1]
        x, y, c, _ = _place()
        copies = [pltpu.make_async_remote_copy(
            src_ref=src[a], dst_ref=out[a], send_sem=send_sems.at[a], recv_sem=recv_sems.at[a],
            device_id=(x, y, 1 - c), device_id_type=MESH) for a in range(n)]
        for cp in copies:
            cp.start()
        for cp in copies:
            cp.wait()

    return pl.pallas_call(
        body, in_specs=[_ANY] * n, out_specs=[_ANY] * n, out_shape=[SDS(t.shape, f32) for t in halves],
        scratch_shapes=[pltpu.SemaphoreType.DMA((n,))] * 2, name=name,
        compiler_params=pltpu.CompilerParams(has_side_effects=True))(*halves)


def _kernel_layout(name, t):
    t = t[0]
    return jnp.swapaxes(t, 0, 1) if name in TRANSPOSED else t


def _harness_layout(name, t):
    if name in TRANSPOSED:
        t = jnp.swapaxes(t, 0, 1)
    return t[None]


def _pad_rows(t, rows):
    return t if t.shape[0] == rows else jnp.pad(t, ((0, rows - t.shape[0]), (0, 0)))


_QA, _KA, _VA, _QB, _F, _GAB = 0, 768, 1536, 2304, 3840, 3848


def _spans(a, b):
    return [(j, max(a, j * IN_SHARD) - j * IN_SHARD, max(a, j * IN_SHARD) - a,
             min(b, (j + 1) * IN_SHARD) - max(a, j * IN_SHARD))
            for j in range(N_SHARD) if max(a, j * IN_SHARD) < min(b, (j + 1) * IN_SHARD)]


_LANES = pl.BlockSpec((N_SHARD, IN_SHARD_PAD, 128), lambda c: (0, 0, c))


def _split_w_in(shards):
    group = [[(o + g * DIL_W, o + (g + 1) * DIL_W) for o in (_QA, _KA, _VA)] for g in range(3)]
    fox = [[(_QB + k * FOX_W, _QB + (k + 1) * FOX_W)] for k in range(3)]
    wanted = group + fox + [[(_QB, _F)], [(_F, _GAB)], [(_GAB, IN_COLS)]]
    rows = [sum(b - a for a, b in w) for w in wanted]
    rows[7] = 128

    def body(s_ref, *o_refs):
        for o_ref, want in zip(o_refs, wanted):
            at = 0
            for a, b in want:
                for j, src, off, n in _spans(a, b):
                    o_ref[at + off:at + off + n, :] = s_ref[j, src:src + n, :]
                at += b - a
        o_refs[7][N_FOX:, :] = jnp.zeros((128 - N_FOX, 128), bf16)

    return pl.pallas_call(
        body, grid=(D // 128,), in_specs=[_LANES], out_specs=[pl.BlockSpec((r, 128), lambda c: (0, c)) for r in rows],
        out_shape=[SDS((r, D), bf16) for r in rows], name="split_w_in",
        compiler_params=_params(("parallel",), 2 * 128 * (N_SHARD * IN_SHARD_PAD + sum(rows))))(shards)


def _join_w_in(g_a, g_fox, g_f, g_gab):
    parts = [(g_a[k], o, o + DIL_W) for o in (0, DIL_W, 2 * DIL_W) for k in range(3)]
    parts += [(t, 0, FOX_W) for t in g_fox] + [(g_f, 0, N_FOX), (g_gab, 0, 2 * D)]
    arrays = list(g_a) + list(g_fox) + [g_f, g_gab]
    index = {id(t): i for i, t in enumerate(arrays)}

    def body(*refs):
        o_ref = refs[-1]
        o_ref[:, IN_SHARD:, :] = jnp.zeros((N_SHARD, IN_SHARD_PAD - IN_SHARD, 128), bf16)
        at = 0
        for t, lo, hi in parts:
            src_ref = refs[index[id(t)]]
            for j, dst, off, n in _spans(at, at + hi - lo):
                o_ref[j, dst:dst + n, :] = src_ref[lo + off:lo + off + n, :].astype(bf16)
            at += hi - lo

    return pl.pallas_call(
        body, grid=(D // 128,), in_specs=[pl.BlockSpec((t.shape[0], 128), lambda c: (0, c)) for t in arrays],
        out_specs=_LANES, out_shape=SDS((N_SHARD, IN_SHARD_PAD, D), bf16), name="join_w_in",
        compiler_params=_params(("parallel",), 2 * 128 * (N_SHARD * IN_SHARD_PAD + sum(t.shape[0] for t in arrays))),
    )(*arrays)


def _full_weights(gathered):
    full = {n: t.reshape((N_SHARD,) + SHARD_SHAPE[n]) for n, t in gathered.items()}
    out = {}
    if "w_in" in full:
        pieces = _split_w_in(full["w_in"])
        out.update(w_a_t=pieces[0:3], w_fox_t=pieces[3:6], w_vr_t=pieces[6], w_f_t=pieces[7], w_gab_t=pieces[8])
    if "w_out" in full:
        out.update(
            w_a4=full["w_proj_a"],
            w_b4=full["w_proj_b"],
            w_out=full["w_out"].reshape(D, D),
            w_gate_t=full["w_ffn_gate"].reshape(F_FF, D),
            w_up_t=full["w_ffn_up"].reshape(F_FF, D),
            w_down=full["w_ffn_down"].reshape(F_FF, D))
    return out


def _sharded_grads(g):
    full = dict(w_in=_join_w_in(g["w_a_t"], g["w_fox_t"], g["w_f_t"], g["w_gab_t"]), w_proj_a=g["w_a4"],
                w_proj_b=g["w_b4"], w_out=g["w_out"], w_ffn_gate=g["w_gate_t"], w_ffn_up=g["w_up_t"],
                w_ffn_down=g["w_down"])
    return {n: _halved(full[n].reshape((N_SHARD,) + SHARD_SHAPE[n])) for n in W_NAMES}


def _local_step(x, target, wt, b_forget, g_mix_pre, g_mix_post, g_ffn_pre, g_ffn_post, late=None):
    tables = _rope_tables()
    b128 = jnp.pad(b_forget, ((0, 0), (0, 128 - N_FOX)))
    dils = tuple(d for _, d in DIL_GROUPS[1:])

    hs = _norm_fwd([x] + list(_perm_rows([x], dils, "perm_x")), g_mix_pre)
    h1 = hs[0]
    if callable(wt):
        wt = wt(h1)
    qkv = [_rope_fwd(g, _mm([(hs[g], wt["w_a_t"][g])], "nt", f32, tm=1024, tn=QKV_W, name=f"proj_a_{g}"), tables)
           for g in range(3)]
    vr = _mm([(h1, wt["w_vr_t"])], "nt", bf16, tm=1024, tn=VR_W // 2, name="proj_vr")
    gab = _mm([(h1, wt["w_gab_t"])], "nt", f32, tm=512, tn=2 * D, name="proj_gab")
    fz = _mm([(h1, wt["w_f_t"])], "nt", f32, tm=1024, tn=128, name="proj_f")
    dil = [_dil_fwd(g, qkv[g]) for g in range(3)]
    out_a, lse_a = _dil_combine([o for o, _ in dil], [l for _, l in dil])
    f_q, f_k = _forget_fwd(fz, b128)
    out_b, lse_b = _fox_fwd(vr, f_q, f_k)
    if late is not None:
        wt = {**wt, **late(out_b)}
    ya, yb, merged = _merge_fwd(out_a, out_b, wt["w_a4"], wt["w_b4"], gab)
    mix, x2, h3 = _resid_norm_fwd(x, merged, wt["w_out"], g_mix_post, g_ffn_pre)
    g_act, u_act, a_act = _ffn_fwd(h3, wt["w_gate_t"], wt["w_up_t"])
    sq_err, dy, d_ff, dg_ffn_post = _loss_head(x2, a_act, wt["w_down"], g_ffn_post, target)

    grads = {}
    d_g, d_u = _ffn_bwd_act(d_ff, wt["w_down"], g_act, u_act)
    grads["w_down"] = _mm([(a_act, d_ff)], "tn", bf16, tm=FF_TN, tn=D, name="grad_w_down")
    grads["w_gate_t"] = _mm([(d_g, h3)], "tn", bf16, tm=FF_TN, tn=D, name="grad_w_gate")
    grads["w_up_t"] = _mm([(d_u, h3)], "tn", bf16, tm=FF_TN, tn=D, name="grad_w_up")
    dx2, d_mix, dg_ffn_pre, dg_mix_post = _norm_bwd_mid(dy, d_g, d_u, wt["w_gate_t"], wt["w_up_t"], x2, mix,
                                                        g_ffn_pre, g_mix_post)

    grads["w_out"] = _mm([(merged, d_mix)], "tn", bf16, tm=D, tn=D, name="grad_w_out")
    d_ya, d_yb, d_gab = _merge_bwd(d_mix, wt["w_out"], ya, yb, gab)
    grads["w_a4"], grads["w_b4"] = _branch_grads(out_a, out_b, d_ya, d_yb)
    d_out_a, delta_a, d_out_b, delta_b = _branch_bwd(d_ya, d_yb, wt["w_a4"], wt["w_b4"], out_a, out_b)

    perm = _perm_rows([d_out_a, delta_a, lse_a], dils, "perm_dil_bwd")
    aux = [(d_out_a, delta_a, lse_a)] + [tuple(perm[k * len(dils) + i] for k in range(3)) for i in range(len(dils))]
    d_qkv = []
    for g in range(3):
        dq, dk, dv = _dil_bwd(g, qkv[g], *aux[g])
        d_qkv.append(_rope_bwd(g, dq, dk, dv, tables))
    *d_fox, d_f_cols, d_f_rows = _fox_bwd(vr, f_q, f_k, lse_b, d_out_b, delta_b)
    d_z, d_b128 = _forget_bwd(fz, b128, d_f_cols, d_f_rows)

    grads["w_a_t"] = [_mm([(d_qkv[g], hs[g])], "tn", bf16, tm=QKV_W, tn=D, name=f"grad_w_a_{g}") for g in range(3)]
    grads["w_fox_t"] = [_mm([(d_fox[k], h1)], "tn", bf16, tm=FOX_W, tn=D, name=f"grad_w_fox_{k}") for k in range(3)]
    grads["w_gab_t"] = _mm([(d_gab, h1)], "tn", bf16, tm=D, tn=D, name="grad_w_gab")
    grads["w_f_t"] = _mm([(d_z, h1)], "tn", bf16, tm=128, tn=D, name="grad_w_f")
    d_h1_nat = _mm([(d_qkv[0], wt["w_a_t"][0])] + list(zip(d_fox, wt["w_fox_t"]))
                   + [(d_gab, wt["w_gab_t"]), (d_z, wt["w_f_t"])], "nn", f32, tm=512, tn=D, name="proj_in_bwd")
    d_h1_dil = [_mm([(d_qkv[g], wt["w_a_t"][g])], "nn", f32, tm=1024, tn=D, name=f"proj_a_bwd_{g}") for g in (1, 2)]
    d_h1 = _unperm_sum(d_h1_nat, d_h1_dil, dils, "unperm_d_h1")
    grad_x, dg_mix_pre = _norm_bwd_in(dx2, d_h1, x, g_mix_pre)

    small = dict(b_forget=d_b128[:, :N_FOX], norm_mix_pre=dg_mix_pre, norm_mix_post=dg_mix_post,
                 norm_ffn_pre=dg_ffn_pre, norm_ffn_post=dg_ffn_post)
    grads["mid_backward"] = d_qkv[0]
    return sq_err, grad_x, grads, small


NORMS = ("norm_mix_pre", "norm_mix_post", "norm_ffn_pre", "norm_ffn_post")
ORDER = ("w_in", "w_proj_a", "w_proj_b", "w_out", "b_forget", "w_ffn_gate", "w_ffn_up", "w_ffn_down") + NORMS


def kernel(x, w_in, w_proj_a, w_proj_b, w_out, b_forget, w_ffn_gate, w_ffn_up, w_ffn_down, norm_mix_pre, norm_mix_post, norm_ffn_pre, norm_ffn_post, loss_target, m_w_in, m_w_proj_a, m_w_proj_b, m_w_out, m_b_forget, m_w_ffn_gate, m_w_ffn_up, m_w_ffn_down, m_norm_mix_pre, m_norm_mix_post, m_norm_ffn_pre, m_norm_ffn_post, v_w_in, v_w_proj_a, v_w_proj_b, v_w_out, v_b_forget, v_w_ffn_gate, v_w_ffn_up, v_w_ffn_down, v_norm_mix_pre, v_norm_mix_post, v_norm_ffn_pre, v_norm_ffn_post):
    given = dict(w_in=w_in, w_proj_a=w_proj_a, w_proj_b=w_proj_b, w_out=w_out, w_ffn_gate=w_ffn_gate,
                 w_ffn_up=w_ffn_up, w_ffn_down=w_ffn_down)
    given_m = dict(w_in=m_w_in, w_proj_a=m_w_proj_a, w_proj_b=m_w_proj_b, w_out=m_w_out, w_ffn_gate=m_w_ffn_gate,
                   w_ffn_up=m_w_ffn_up, w_ffn_down=m_w_ffn_down)
    given_v = dict(w_in=v_w_in, w_proj_a=v_w_proj_a, w_proj_b=v_w_proj_b, w_out=v_w_out, w_ffn_gate=v_w_ffn_gate,
                   w_ffn_up=v_w_ffn_up, w_ffn_down=v_w_ffn_down)
    w, m, v = ({n: _kernel_layout(n, t[n]) for n in W_NAMES} for t in (given, given_m, given_v))
    small_w = dict(b_forget=b_forget, norm_mix_pre=norm_mix_pre, norm_mix_post=norm_mix_post,
                   norm_ffn_pre=norm_ffn_pre, norm_ffn_post=norm_ffn_post)
    small_m = dict(b_forget=m_b_forget, norm_mix_pre=m_norm_mix_pre, norm_mix_post=m_norm_mix_post,
                   norm_ffn_pre=m_norm_ffn_pre, norm_ffn_post=m_norm_ffn_post)
    small_v = dict(b_forget=v_b_forget, norm_mix_pre=v_norm_mix_pre, norm_mix_post=v_norm_mix_post,
                   norm_ffn_pre=v_norm_ffn_pre, norm_ffn_post=v_norm_ffn_post)

    own = [_halved(_pad_rows(w[n].astype(bf16), SHARD_SHAPE[n][0])) for n in W_NAMES]
    chip = 2 * lax.axis_index("x") + lax.axis_index("y")
    exchanged = {"first": _all_gather_async(own[:1], [], "all_gather_first", GATHER_FIRST_ID)}
    fill = lambda ts, mine: [lax.dynamic_update_index_in_dim(t, o, chip, 0) for t, o in zip(ts, mine)]

    def first_weights(ready):
        arrived, _ = lax.optimization_barrier((list(exchanged["first"]), ready))
        exchanged["late"] = _all_gather_async(own[1:], [arrived[0][0, 0, :16, :128]], "all_gather_late", GATHER_LATE_ID)
        return _full_weights(dict(zip(W_NAMES[:1], fill(arrived, own[:1]))))

    def late_weights(ready):
        arrived, _ = lax.optimization_barrier((list(exchanged["late"]), ready))
        return _full_weights(dict(zip(W_NAMES[1:], fill(arrived, own[1:]))))

    sq_err, grad_x, grads, small = _local_step(x[0], loss_target[0], first_weights, b_forget, norm_mix_pre,
                                               norm_mix_post, norm_ffn_pre, norm_ffn_post, late=late_weights)

    g4 = _sharded_grads(grads)
    stack = lambda t, extra: jnp.concatenate(
        [jnp.pad(t["b_forget"], ((0, 0), (0, D - N_FOX)))] + [t[n] for n in NORMS]
        + [jnp.pad(extra, ((0, SMALL_ROWS - LOSS_ROW - 1), (0, D - extra.shape[1])), constant_values=1.0)], axis=0)
    early, _ = lax.optimization_barrier((list(_pair_swap_early([g4[n] for n in W_NAMES[1:]])), grads["mid_backward"]))
    other = list(_pair_swap([g4["w_in"]])) + early
    parts = [_pair_sum(g4[n], o, "pair_sum_" + n) for n, o in zip(W_NAMES, other)]
    recv_early = _scatter_early(parts[1:])
    recv_in, small_all = _scatter_partials(parts[:1], stack(small, sq_err))

    g_shard, delta, new_m, new_v = {}, {}, {}, {}

    def summed(names, parts, recv):
        halves = [_sum_partials(p, r, "sum_partials_" + n) for n, p, r in zip(names, parts, recv)]
        return halves, list(_swap_halves(halves, "swap_halves_" + names[0]))

    def update(names, halves, theirs):
        for n, mine, other_half in zip(names, halves, theirs):
            g_shard[n], delta[n], new_m[n], new_v[n] = _adamw_halves(w[n], mine, other_half, m[n], v[n], "adamw_" + n)

    recv_early, _ = lax.optimization_barrier((list(recv_early), parts[0]))
    early_mine, early_theirs = summed(W_NAMES[1:], parts[1:], recv_early)
    (recv_in, small_all), _ = lax.optimization_barrier(((recv_in, small_all), early_theirs))
    update(W_NAMES[:1], *summed(W_NAMES[:1], parts[:1], [recv_in]))
    early_theirs, _ = lax.optimization_barrier((early_theirs, delta["w_in"]))
    update(W_NAMES[1:], early_mine, early_theirs)
    small_sum = _sum_small(small_all)
    loss = small_sum[LOSS_ROW, 0] * (0.5 / D)
    ones = jnp.ones((1, 128), f32)
    sd, sm, sv = _adamw(stack(small_w, ones), small_sum, stack(small_m, ones), stack(small_v, ones), "adamw_small")

    outs = [loss, grad_x[None]]
    for big, st in ((g_shard, small_sum), (delta, sd), (new_m, sm), (new_v, sv)):
        t = {n: _harness_layout(n, big[n]) for n in W_NAMES}
        t["b_forget"] = st[0:1, :N_FOX]
        for i, n in enumerate(NORMS):
            t[n] = st[i + 1:i + 2]
        outs += [t[n] for n in ORDER]
    return tuple(outs)
```

```python
import functools
import math

import jax
import jax.numpy as jnp
import numpy as np
from jax import lax
from jax.experimental import pallas as pl
from jax.experimental.pallas import tpu as pltpu
from jax.experimental.pallas import tpu_sc as plsc

f32 = jnp.float32
bf16 = jnp.bfloat16
SDS = jax.ShapeDtypeStruct
MESH = pl.DeviceIdType.MESH

S = 2048
D = 1024
HD = 64
BLK = 128
N_FOX = 8
FOX_W = N_FOX * HD
DIL_GROUPS = ((128, 1), (512, 4), (2048, 16))
SLOTS = 4
DIL_W = SLOTS * HD
QKV_W = 3 * DIL_W
VR_W = 3 * FOX_W
GF_W = 2 * D + 128
F_FF = 2816
ROPE_DIM = 16
ROPE_THETA = 500000.0
EPS = 1e-6
NEG = -1e30
SCALE = 1.0 / math.sqrt(HD)
IN_COLS = 5896
N_SHARD = 4

ADAM_LR, ADAM_B1, ADAM_B2, ADAM_EPS, ADAM_WD, ADAM_STEP = 0.001, 0.9, 0.999, 1e-08, 0.01, 10

VMEM_V7X = 64 * 1024 * 1024
VMEM_PLAN_MAX = VMEM_V7X - 8 * 1024 * 1024

TM = 512
TQ = 256

W_NAMES = ("w_in", "w_proj_a", "w_proj_b", "w_out", "w_ffn_gate", "w_ffn_up", "w_ffn_down")
TRANSPOSED = ("w_in", "w_ffn_gate", "w_ffn_up")
IN_SHARD = IN_COLS // N_SHARD
IN_SHARD_PAD = 1504
SHARD_SHAPE = dict(w_in=(IN_SHARD_PAD, D), w_proj_a=(DIL_W, D // N_SHARD), w_proj_b=(FOX_W, D // N_SHARD),
                   w_out=(D // N_SHARD, D), w_ffn_gate=(F_FF // N_SHARD, D), w_ffn_up=(F_FF // N_SHARD, D),
                   w_ffn_down=(F_FF // N_SHARD, D))
SMALL_ROWS = 8
LOSS_ROW = 5


def _nbytes(shape, dtype):
    return math.prod(shape) * jnp.dtype(dtype).itemsize


def _params(semantics, block_bytes, temp_bytes=0):
    need = 2 * block_bytes + temp_bytes + (2 << 20)
    return pltpu.CompilerParams(dimension_semantics=semantics, vmem_limit_bytes=int(min(need, VMEM_PLAN_MAX)))


def _row(w, tm=TM):
    return pl.BlockSpec((tm, w), lambda i: (i, 0))


def _vec(w):
    return pl.BlockSpec((1, w), lambda i: (0, 0))


def _mm(pairs, dims, out_dtype, *, tm, tn, name, m_inner=False):
    a0, b0 = pairs[0]
    m_dim = a0.shape[1] if dims == "tn" else a0.shape[0]
    n_dim = b0.shape[0] if dims == "nt" else b0.shape[1]
    contract = {"nn": ((1,), (0,)), "nt": ((1,), (1,)), "tn": ((0,), (0,))}[dims]
    n_pairs = len(pairs)
    assert m_dim % tm == 0 and n_dim % tn == 0, (name, m_dim, n_dim, tm, tn)

    def body(*refs):
        o_ref = refs[-1]
        acc = None
        for p in range(n_pairs):
            a = refs[2 * p][...].astype(bf16)
            b = refs[2 * p + 1][...].astype(bf16)
            t = lax.dot_general(a, b, (contract, ((), ())), preferred_element_type=f32)
            acc = t if acc is None else acc + t
        o_ref[...] = acc.astype(o_ref.dtype)

    if m_inner:
        grid = (n_dim // tn, m_dim // tm)
        mi = lambda j, i: i
        ni = lambda j, i: j
    else:
        grid = (m_dim // tm, n_dim // tn)
        mi = lambda i, j: i
        ni = lambda i, j: j
    in_specs, block_bytes, args = [], 0, []
    for a, b in pairs:
        k_dim = a.shape[0] if dims == "tn" else a.shape[1]
        if dims == "tn":
            in_specs.append(pl.BlockSpec((k_dim, tm), lambda *g: (0, mi(*g))))
        else:
            in_specs.append(pl.BlockSpec((tm, k_dim), lambda *g: (mi(*g), 0)))
        if dims == "nt":
            in_specs.append(pl.BlockSpec((tn, k_dim), lambda *g: (ni(*g), 0)))
        else:
            in_specs.append(pl.BlockSpec((k_dim, tn), lambda *g: (0, ni(*g))))
        block_bytes += _nbytes((tm, k_dim), a.dtype) + _nbytes((tn, k_dim), b.dtype)
        args += [a, b]
    block_bytes += _nbytes((tm, tn), out_dtype)
    temp = _nbytes((tm, tn), f32) * 2 + sum(_nbytes((tm, a.shape[0] if dims == "tn" else a.shape[1]), bf16)
                                            + _nbytes((tn, a.shape[0] if dims == "tn" else a.shape[1]), bf16)
                                            for a, _ in pairs)
    return pl.pallas_call(
        body, grid=grid, in_specs=in_specs,
        out_specs=pl.BlockSpec((tm, tn), lambda *g: (mi(*g), ni(*g))),
        out_shape=SDS((m_dim, n_dim), out_dtype), name=name,
        compiler_params=_params(("parallel", "parallel"), block_bytes, temp),
    )(*args)


def _rms(x, g):
    r = lax.rsqrt(jnp.mean(x * x, axis=-1, keepdims=True) + EPS)
    return x * r * g


def _rms_bwd(x, g, dy):
    r = lax.rsqrt(jnp.mean(x * x, axis=-1, keepdims=True) + EPS)
    xh = x * r
    dxh = dy * g
    dx = r * (dxh - xh * jnp.mean(dxh * xh, axis=-1, keepdims=True))
    return dx, jnp.sum(dy * xh, axis=0, keepdims=True)


def _acc_rows(ref, val):
    @pl.when(pl.program_id(0) == 0)
    def _():
        ref[...] = jnp.zeros_like(ref)
    ref[...] += val


def _norm_fwd(xs, g):
    n = len(xs)

    def body(*refs):
        g = refs[n][...]
        for x_ref, h_ref in zip(refs[:n], refs[n + 1:]):
            h_ref[...] = _rms(x_ref[...], g).astype(bf16)

    return pl.pallas_call(
        body, grid=(S // TM,), in_specs=[_row(D)] * n + [_vec(D)], out_specs=[_row(D)] * n,
        out_shape=[SDS((S, D), bf16)] * n, name="norm_mix_pre",
        compiler_params=_params(("parallel",), 6 * n * TM * D, 8 * n * TM * D))(*xs, g)


def _perm_rows(xs, ds, name):
    n = len(xs)

    def body(*refs):
        outs = iter(refs[n:])
        for x_ref in refs[:n]:
            for d in ds:
                o_ref, rows = next(outs), S // d
                for r in range(d):
                    o_ref[r * rows:(r + 1) * rows, :] = x_ref[pl.ds(r, rows, stride=d), :]

    blk = pl.BlockSpec((S, 128), lambda c: (0, c))
    w = xs[0].shape[1]
    return pl.pallas_call(
        body, grid=(w // 128,), in_specs=[blk] * n, out_specs=[blk] * (n * len(ds)),
        out_shape=[SDS((S, w), f32)] * (n * len(ds)), name=name,
        compiler_params=_params(("parallel",), 4 * S * 128 * n * (1 + len(ds))))(*xs)


def _unperm_sum(nat, perms, ds, name):
    n = len(perms)

    def body(*refs):
        a_ref, o_ref, sc = refs[0], refs[n + 1], refs[n + 2]
        acc = a_ref[...]
        for b_ref, d in zip(refs[1:n + 1], ds):
            rows = S // d
            for r in range(d):
                sc[pl.ds(r, rows, stride=d), :] = b_ref[r * rows:(r + 1) * rows, :]
            acc = acc + sc[...]
        o_ref[...] = acc

    blk = pl.BlockSpec((S, 128), lambda c: (0, c))
    w = nat.shape[1]
    return pl.pallas_call(
        body, grid=(w // 128,), in_specs=[blk] * (n + 1), out_specs=blk, out_shape=SDS((S, w), f32),
        scratch_shapes=[pltpu.VMEM((S, 128), f32)], name=name,
        compiler_params=_params(("parallel",), 4 * S * 128 * (n + 2), 8 * S * 128))(nat, *perms)


def _whole(a):
    return pl.BlockSpec(a.shape, lambda i: (0,) * a.ndim)


def _resid_norm_fwd(x, merged, w_out, g_post, g_pre):
    def body(x_ref, mg_ref, w_ref, gp_ref, gn_ref, mix_ref, x2_ref, h_ref):
        mix = jnp.dot(mg_ref[...], w_ref[...], preferred_element_type=f32)
        x2 = x_ref[...] + _rms(mix, gp_ref[...])
        mix_ref[...] = mix
        x2_ref[...] = x2
        h_ref[...] = _rms(x2, gn_ref[...]).astype(bf16)

    return pl.pallas_call(
        body, grid=(S // TM,), in_specs=[_row(D), _row(D), _whole(w_out), _vec(D), _vec(D)], out_specs=[_row(D)] * 3,
        out_shape=[SDS((S, D), f32), SDS((S, D), f32), SDS((S, D), bf16)], name="proj_out_norm",
        compiler_params=_params(("parallel",), 16 * TM * D + 2 * D * D, 16 * TM * D))(x, merged, w_out, g_post, g_pre)


def _loss_head(x2, a_act, w_down, g_post, target):
    def body(x2_ref, a_ref, w_ref, g_ref, t_ref, loss_ref, dy_ref, dff_ref, dg_ref):
        ff = jnp.dot(a_ref[...], w_ref[...], preferred_element_type=f32)
        g = g_ref[...]
        err = x2_ref[...] + _rms(ff, g) - t_ref[...]
        dy = err * (1.0 / D)
        dff, dg = _rms_bwd(ff, g, dy)
        dy_ref[...] = dy
        dff_ref[...] = dff.astype(bf16)
        _acc_rows(dg_ref, dg)
        _acc_rows(loss_ref, jnp.full((1, 128), jnp.sum(err * err), f32))

    return pl.pallas_call(
        body, grid=(S // TM,), in_specs=[_row(D), _row(F_FF), _whole(w_down), _vec(D), _row(D)],
        out_specs=[_vec(128), _row(D), _row(D), _vec(D)],
        out_shape=[SDS((1, 128), f32), SDS((S, D), f32), SDS((S, D), bf16), SDS((1, D), f32)], name="ffn_down_loss",
        compiler_params=_params(("arbitrary",), 14 * TM * D + 2 * TM * F_FF + 2 * F_FF * D, 28 * TM * D),
    )(x2, a_act, w_down, g_post, target)


def _norm_bwd_mid(dy, d_g, d_u, w_gate_t, w_up_t, x2, mix, g_ffn_pre, g_mix_post):
    def body(dy_ref, dgt_ref, dut_ref, wg_ref, wu_ref, x2_ref, mix_ref, g3_ref, g2_ref, dx2_ref, dmix_ref, dg3_ref, dg2_ref):
        dh = jnp.dot(dgt_ref[...], wg_ref[...], preferred_element_type=f32)
        dh += jnp.dot(dut_ref[...], wu_ref[...], preferred_element_type=f32)
        d3, dg3 = _rms_bwd(x2_ref[...], g3_ref[...], dh)
        dx2 = dy_ref[...] + d3
        dmix, dg2 = _rms_bwd(mix_ref[...], g2_ref[...], dx2)
        dx2_ref[...] = dx2
        dmix_ref[...] = dmix.astype(bf16)
        _acc_rows(dg3_ref, dg3)
        _acc_rows(dg2_ref, dg2)

    tm = TM // 2
    row = lambda w: _row(w, tm)
    return pl.pallas_call(
        body, grid=(S // tm,),
        in_specs=[row(D), row(F_FF), row(F_FF), _whole(w_gate_t), _whole(w_up_t), row(D), row(D), _vec(D), _vec(D)],
        out_specs=[row(D), row(D), _vec(D), _vec(D)],
        out_shape=[SDS((S, D), f32), SDS((S, D), bf16), SDS((1, D), f32), SDS((1, D), f32)], name="ffn_bwd_in_norm",
        compiler_params=_params(("arbitrary",), 18 * tm * D + 4 * tm * F_FF + 4 * F_FF * D, 28 * tm * D),
    )(dy, d_g, d_u, w_gate_t, w_up_t, x2, mix, g_ffn_pre, g_mix_post)


def _norm_bwd_in(dx2, dh1, x, g):
    def body(dx2_ref, dh_ref, x_ref, g_ref, gx_ref, dg_ref):
        d1, dg = _rms_bwd(x_ref[...], g_ref[...], dh_ref[...])
        gx_ref[...] = dx2_ref[...] + d1
        _acc_rows(dg_ref, dg)

    return pl.pallas_call(
        body, grid=(S // TM,), in_specs=[_row(D)] * 3 + [_vec(D)], out_specs=[_row(D), _vec(D)],
        out_shape=[SDS((S, D), f32), SDS((1, D), f32)], name="norm_bwd_in",
        compiler_params=_params(("arbitrary",), 16 * TM * D, 16 * TM * D))(dx2, dh1, x, g)


def _rope_tables():
    half = ROPE_DIM // 2
    inv_freq = np.power(np.float32(ROPE_THETA), -np.arange(0, ROPE_DIM, 2, dtype=np.float32) / np.float32(ROPE_DIM))
    row = np.arange(S)
    groups = []
    for _, d in DIL_GROUPS:
        pos = ((row % (S // d)) * d + row // (S // d)).astype(np.float32)
        ang = pos[:, None] * inv_freq[None, :].astype(np.float32)
        cos, sin = np.cos(ang).astype(np.float32), np.sin(ang).astype(np.float32)
        c = np.concatenate([cos, cos, np.ones((S, HD - ROPE_DIM), np.float32)], axis=1)
        s_lo = np.concatenate([-sin, np.zeros((S, HD - half), np.float32)], axis=1)
        s_hi = np.concatenate([np.zeros((S, half), np.float32), sin, np.zeros((S, HD - ROPE_DIM), np.float32)], axis=1)
        groups.append(np.stack([np.concatenate([t, t], axis=1) for t in (c, s_lo, s_hi)]))
    return jnp.asarray(np.stack(groups))


def _rotate(x, c, lo, hi, sign):
    tile = lambda t: jnp.tile(t, (1, DIL_W // 128))
    return (x * tile(c) + pltpu.roll(x, DIL_W - ROPE_DIM // 2, 1) * (tile(lo) * sign)
            + pltpu.roll(x, ROPE_DIM // 2, 1) * (tile(hi) * sign))


def _table_specs(g):
    return [pl.BlockSpec((None, None, TM, 128), lambda i, k=k: (g, k, i, 0)) for k in range(3)]


def _rope_fwd(g, p_qkv, tables):
    def body(x_ref, c_ref, lo_ref, hi_ref, o_ref):
        c, lo, hi = c_ref[...], lo_ref[...], hi_ref[...]
        for part in range(2):
            cols = slice(part * DIL_W, (part + 1) * DIL_W)
            o_ref[:, cols] = _rotate(x_ref[:, cols], c, lo, hi, 1.0).astype(bf16)
        o_ref[:, 2 * DIL_W:] = x_ref[:, 2 * DIL_W:].astype(bf16)

    return pl.pallas_call(
        body, grid=(S // TM,), in_specs=[_row(QKV_W)] + _table_specs(g), out_specs=_row(QKV_W),
        out_shape=SDS((S, QKV_W), bf16), name=f"rope_fwd_{g}",
        compiler_params=_params(("parallel",), 6 * TM * QKV_W + 12 * TM * 128, 24 * TM * QKV_W))(p_qkv, tables, tables, tables)


def _rope_bwd(g, dq, dk, dv, tables):
    def body(dq_ref, dk_ref, dv_ref, c_ref, lo_ref, hi_ref, o_ref):
        c, lo, hi = c_ref[...], lo_ref[...], hi_ref[...]
        o_ref[:, :DIL_W] = _rotate(dq_ref[...], c, lo, hi, -1.0).astype(bf16)
        o_ref[:, DIL_W:2 * DIL_W] = _rotate(dk_ref[...], c, lo, hi, -1.0).astype(bf16)
        o_ref[:, 2 * DIL_W:] = dv_ref[...].astype(bf16)

    return pl.pallas_call(
        body, grid=(S // TM,), in_specs=[_row(DIL_W)] * 3 + _table_specs(g), out_specs=_row(QKV_W),
        out_shape=SDS((S, QKV_W), bf16), name=f"rope_bwd_{g}",
        compiler_params=_params(("parallel",), 6 * TM * QKV_W + 12 * TM * 128, 24 * TM * QKV_W))(dq, dk, dv, tables, tables, tables)


def _nt(a, b):
    return lax.dot_general(a, b, (((1,), (1,)), ((), ())), preferred_element_type=f32)


def _tn(a, b):
    return lax.dot_general(a, b, (((0,), (0,)), ((), ())), preferred_element_type=f32)


STEP_BLOCKS = 4
STEP_ROWS = STEP_BLOCKS * BLK


def _dil_prev(g, b):
    _, d = DIL_GROUPS[g]
    nb = S // d // BLK
    if nb == 1 or (b == 0 and nb <= STEP_BLOCKS):
        return None
    return "in" if b > 0 else "halo"


def _bnt(a, b):
    return lax.dot_general(a, b, (((2,), (2,)), ((0,), (0,))), preferred_element_type=f32)


def _bnn(a, b):
    return lax.dot_general(a, b, (((2,), (1,)), ((0,), (0,))), preferred_element_type=f32)


def _btn(a, b):
    return lax.dot_general(a, b, (((1,), (1,)), ((0,), (0,))), preferred_element_type=f32)


def _on_tail(x, tail, fn):
    if tail == x.shape[0]:
        return fn(x)
    return jnp.concatenate([x[:-tail], fn(x[-tail:])], axis=0)


def _heads(ref, part):
    n = ref.shape[0] // BLK
    return jnp.stack([ref[b * BLK:(b + 1) * BLK, part * DIL_W + h * HD:part * DIL_W + (h + 1) * HD]
                      for b in range(n) for h in range(SLOTS)])


def _dil_operands(g, qkv_ref, halo_ref):
    q, kc, vc = (_heads(qkv_ref, part) for part in range(3))
    qi = lax.broadcasted_iota(jnp.int32, (1, BLK, BLK), 1)
    kj = lax.broadcasted_iota(jnp.int32, (1, BLK, BLK), 2)
    with_prev = [b for b in range(STEP_BLOCKS) if _dil_prev(g, b) is not None]
    tail = SLOTS * len(with_prev)
    if not tail:
        return q, kc, vc, None, None, kj <= qi, None, 0
    assert with_prev == list(range(STEP_BLOCKS - len(with_prev), STEP_BLOCKS))
    inside = SLOTS * sum(_dil_prev(g, b) == "in" for b in with_prev)
    kp, vp, prev = kc[:inside], vc[:inside], jnp.broadcast_to(kj >= qi, (inside, BLK, BLK))
    if inside < tail:
        no_halo = jnp.where(pl.program_id(0) == 0, BLK + 1, 0)
        kp = jnp.concatenate([_heads(halo_ref, 1), kp], axis=0)
        vp = jnp.concatenate([_heads(halo_ref, 2), vp], axis=0)
        prev = jnp.concatenate([jnp.broadcast_to(kj >= qi + no_halo, (SLOTS, BLK, BLK)), prev], axis=0)
    return q, kc, vc, kp, vp, kj <= qi, prev, tail


def _dil_in_specs(g, n_aux):
    step = lambda w: pl.BlockSpec((STEP_ROWS, w), lambda i: (i, 0))
    halo = [pl.BlockSpec((BLK, QKV_W), lambda i: (jnp.maximum(i * STEP_BLOCKS - 1, 0), 0))]
    needs_halo = _dil_prev(g, 0) == "halo"
    return [step(QKV_W)] + (halo if needs_halo else []) + [step(DIL_W)] * n_aux, needs_halo


def _dil_fwd(g, qkv):
    in_specs, needs_halo = _dil_in_specs(g, 0)

    def body(*refs):
        qkv_ref, halo_ref = refs[0], refs[1] if needs_halo else None
        o_ref, lse_ref = refs[-2:]
        q, kc, vc, kp, vp, cur, prev, tail = _dil_operands(g, qkv_ref, halo_ref)
        sc = jnp.where(cur, _bnt(q, kc) * SCALE, NEG)
        m = jnp.max(sc, axis=-1, keepdims=True)
        if tail:
            sp = jnp.where(prev, _bnt(q[-tail:], kp) * SCALE, NEG)
            m = _on_tail(m, tail, lambda t: jnp.maximum(t, jnp.max(sp, axis=-1, keepdims=True)))
            pp = jnp.exp(sp - m[-tail:])
        pc = jnp.exp(sc - m)
        den = jnp.sum(pc, axis=-1, keepdims=True)
        if tail:
            den = _on_tail(den, tail, lambda t: t + jnp.sum(pp, axis=-1, keepdims=True))
        inv = 1.0 / den
        o = _bnn((pc * inv).astype(bf16), vc)
        if tail:
            o = _on_tail(o, tail, lambda t: t + _bnn((pp * inv[-tail:]).astype(bf16), vp))
        lse = m + jnp.log(den)
        for b in range(STEP_BLOCKS):
            for h in range(SLOTS):
                rows, hs = slice(b * BLK, (b + 1) * BLK), slice(h * HD, (h + 1) * HD)
                o_ref[rows, hs] = o[SLOTS * b + h]
                lse_ref[rows, hs] = jnp.broadcast_to(lse[SLOTS * b + h], (BLK, HD))

    out = pl.BlockSpec((STEP_ROWS, DIL_W), lambda i: (i, 0))
    return pl.pallas_call(
        body, grid=(S // STEP_ROWS,), in_specs=in_specs, out_specs=[out, out], out_shape=[SDS((S, DIL_W), f32)] * 2,
        name=f"dil_fwd_{g}", compiler_params=_params(("parallel",), 12 * STEP_ROWS * DIL_W, 2 << 20),
    )(*([qkv] * (2 if needs_halo else 1)))


def _dil_combine(outs, lses):
    def body(o0, o1, o2, l0, l1, l2, out_ref, lse_ref, so1, so2, sl1, sl2):
        for (_, d), src, dst in ((DIL_GROUPS[1], o1, so1), (DIL_GROUPS[2], o2, so2),
                                 (DIL_GROUPS[1], l1, sl1), (DIL_GROUPS[2], l2, sl2)):
            rows = S // d
            for r in range(d):
                dst[pl.ds(r, rows, stride=d), :] = src[r * rows:(r + 1) * rows, :]
        a, b, c = l0[...], sl1[...], sl2[...]
        m = jnp.maximum(jnp.maximum(a, b), c)
        ea, eb, ec = jnp.exp(a - m), jnp.exp(b - m), jnp.exp(c - m)
        z = ea + eb + ec
        inv = 1.0 / z
        out_ref[...] = (ea * inv) * o0[...] + (eb * inv) * so1[...] + (ec * inv) * so2[...]
        lse_ref[...] = m + jnp.log(z)

    blk = pl.BlockSpec((S, 128), lambda c: (0, c))
    return pl.pallas_call(
        body, grid=(DIL_W // 128,), in_specs=[blk] * 6, out_specs=[blk] * 2,
        out_shape=[SDS((S, DIL_W), f32)] * 2, scratch_shapes=[pltpu.VMEM((S, 128), f32)] * 4, name="dil_combine",
        compiler_params=_params(("parallel",), 32 * S * 128, 32 * S * 128))(*outs, *lses)


def _dil_bwd(g, qkv, d_out, delta, lse):
    in_specs, needs_halo = _dil_in_specs(g, 3)

    def body(*refs):
        qkv_ref, halo_ref = refs[0], refs[1] if needs_halo else None
        do_ref, dl_ref, lse_ref, dq_ref, dk_ref, dv_ref = refs[-6:]
        q, kc, vc, kp, vp, cur, prev, tail = _dil_operands(g, qkv_ref, halo_ref)
        tiles = [(slice(b * BLK, (b + 1) * BLK), h) for b in range(STEP_BLOCKS) for h in range(SLOTS)]
        do = jnp.stack([do_ref[rows, h * HD:(h + 1) * HD] for rows, h in tiles]).astype(bf16)
        lse = jnp.stack([lse_ref[rows, h * HD:h * HD + 1] for rows, h in tiles])
        delta = jnp.stack([dl_ref[rows, h * HD:h * HD + 1] for rows, h in tiles])

        def probs(q, k, mask, lse, do, v, delta):
            p = jnp.exp(jnp.where(mask, _bnt(q, k) * SCALE, NEG) - lse)
            ds = p * (_bnt(do, v) - delta) * SCALE
            return p.astype(bf16), ds.astype(bf16)

        p, ds = probs(q, kc, cur, lse, do, vc, delta)
        dq, dk, dv = _bnn(ds, kc), _btn(ds, q), _btn(p, do)
        if tail:
            p, ds = probs(q[-tail:], kp, prev, lse[-tail:], do[-tail:], vp, delta[-tail:])
            dq = _on_tail(dq, tail, lambda t: t + _bnn(ds, kp))
            dk_p, dv_p = _btn(ds, q[-tail:]), _btn(p, do[-tail:])
            inside = tail - SLOTS if needs_halo else tail
            pad = jnp.zeros((len(tiles) - inside, BLK, HD), f32)
            dk = dk + jnp.concatenate([dk_p[tail - inside:], pad], axis=0)
            dv = dv + jnp.concatenate([dv_p[tail - inside:], pad], axis=0)
        first = pl.multiple_of(pl.program_id(0) * STEP_ROWS, STEP_ROWS)
        for t, (rows, h) in enumerate(tiles):
            hs = slice(h * HD, (h + 1) * HD)
            own = pl.ds(pl.multiple_of(first + rows.start, BLK), BLK)
            dq_ref[rows, hs] = dq[t]
            dk_ref[own, hs] = dk[t]
            dv_ref[own, hs] = dv[t]
        if needs_halo:
            before = pl.ds(pl.multiple_of(jnp.maximum(first - BLK, 0), BLK), BLK)
            for h in range(SLOTS):
                hs = slice(h * HD, (h + 1) * HD)
                dk_ref[before, hs] += dk_p[h]
                dv_ref[before, hs] += dv_p[h]

    whole = pl.BlockSpec((S, DIL_W), lambda i: (0, 0))
    return pl.pallas_call(
        body, grid=(S // STEP_ROWS,), in_specs=in_specs,
        out_specs=[pl.BlockSpec((STEP_ROWS, DIL_W), lambda i: (i, 0)), whole, whole],
        out_shape=[SDS((S, DIL_W), f32)] * 3, name=f"dil_bwd_{g}",
        compiler_params=_params(("arbitrary",), 20 * STEP_ROWS * DIL_W + 8 * S * DIL_W, 2 << 20),
    )(*([qkv] * (2 if needs_halo else 1)), d_out, delta, lse)


def _scan_rows(x, reverse):
    row = lax.broadcasted_iota(jnp.int32, x.shape, 0)
    k = 1
    while k < S:
        if reverse:
            x = x + jnp.where(row < S - k, pltpu.roll(x, S - k, 0), 0.0)
        else:
            x = x + jnp.where(row >= k, pltpu.roll(x, k, 0), 0.0)
        k *= 2
    return x


N_PAIR = N_FOX // 2
_PAIR_Q = pl.BlockSpec((None, S, 128), lambda p: (p, 0, 0))
_PAIR_K = pl.BlockSpec((None, 8, S), lambda p: (p, 0, 0))


def _forget_fwd(fz, b128):
    def body(z_ref, b_ref, fq_ref, fk_ref):
        z = z_ref[...] + b_ref[...]
        logf = jnp.minimum(z, 0.0) - jnp.log1p(jnp.exp(-jnp.abs(z)))
        f_cum = _scan_rows(logf, reverse=False)
        f_cum_t = f_cum.T
        fq_ref[...] = jnp.zeros_like(fq_ref)
        fk_ref[...] = jnp.zeros_like(fk_ref)
        for p in range(N_PAIR):
            fq_ref[p, :, 0:2] = f_cum[:, 2 * p:2 * p + 2]
            fk_ref[p, 0:2, :] = f_cum_t[2 * p:2 * p + 2, :]

    return pl.pallas_call(
        body, grid=(1,), in_specs=[pl.BlockSpec((S, 128), lambda i: (0, 0)), _vec(128)],
        out_specs=[pl.BlockSpec((N_PAIR, S, 128), lambda i: (0, 0, 0)), pl.BlockSpec((N_PAIR, 8, S), lambda i: (0, 0, 0))],
        out_shape=[SDS((N_PAIR, S, 128), f32), SDS((N_PAIR, 8, S), f32)], name="forget_fwd",
        compiler_params=_params(("arbitrary",), 24 * S * 128, 24 * S * 128))(fz, b128)


def _forget_bwd(fz, b128, d_f_cols, d_f_rows):
    def body(z_ref, b_ref, dfc_ref, dfr_ref, dz_ref, db_ref, df_sc):
        z = z_ref[...] + b_ref[...]
        df_sc[...] = jnp.zeros_like(df_sc)
        for p in range(N_PAIR):
            df_sc[:, 2 * p:2 * p + 2] = dfr_ref[p, :, 0:2] + dfc_ref[p].T[:, 0:2]
        dz = _scan_rows(df_sc[...], reverse=True) * jax.nn.sigmoid(-z)
        dz_ref[...] = dz
        db_ref[...] = jnp.sum(dz, axis=0, keepdims=True)

    full = pl.BlockSpec((S, 128), lambda i: (0, 0))
    return pl.pallas_call(
        body, grid=(1,),
        in_specs=[full, _vec(128), pl.BlockSpec((N_PAIR, 8, S), lambda i: (0, 0, 0)), pl.BlockSpec((N_PAIR, S, 128), lambda i: (0, 0, 0))],
        out_specs=[full, _vec(128)], out_shape=[SDS((S, 128), f32), SDS((1, 128), f32)],
        scratch_shapes=[pltpu.VMEM((S, 128), f32)], name="forget_bwd",
        compiler_params=_params(("arbitrary",), 32 * S * 128, 24 * S * 128))(fz, b128, d_f_cols, d_f_rows)


def _fox_scores(q_ref, k_ref, fq_ref, fk_ref, qi, hh):
    n = (qi + 1) * TQ
    rows, hs = slice(qi * TQ, n), slice(hh * HD, (hh + 1) * HD)
    q = q_ref[rows, hs] * SCALE
    s = _nt(q, k_ref[0:n, hs]) + (fq_ref[rows, hh:hh + 1] - fk_ref[hh:hh + 1, 0:n])
    below = lax.broadcasted_iota(jnp.int32, (TQ, TQ), 1) <= lax.broadcasted_iota(jnp.int32, (TQ, TQ), 0)
    diag = jnp.where(below, s[:, n - TQ:], NEG)
    return diag if qi == 0 else jnp.concatenate([s[:, :n - TQ], diag], axis=1)


def _pair_cols(first):
    return pl.BlockSpec((S, 128), lambda p: (0, first + p))


def _fox_fwd(vr, fq, fk):
    def body(q_ref, k_ref, v_ref, fq_ref, fk_ref, o_ref, lse_ref):
        lse_ref[...] = jnp.zeros_like(lse_ref)
        for hh in range(2):
            hs = slice(hh * HD, (hh + 1) * HD)
            for qi in range(S // TQ):
                n = (qi + 1) * TQ
                rows = slice(qi * TQ, n)
                s = _fox_scores(q_ref, k_ref, fq_ref, fk_ref, qi, hh)
                m = jnp.max(s, axis=-1, keepdims=True)
                p = jnp.exp(s - m)
                den = jnp.sum(p, axis=-1, keepdims=True)
                o_ref[rows, hs] = jnp.dot((p * (1.0 / den)).astype(bf16), v_ref[0:n, hs], preferred_element_type=f32)
                lse_ref[rows, hh:hh + 1] = m + jnp.log(den)

    return pl.pallas_call(
        body, grid=(N_PAIR,), in_specs=[_pair_cols(0), _pair_cols(N_PAIR), _pair_cols(2 * N_PAIR), _PAIR_Q, _PAIR_K],
        out_specs=[_pair_cols(0), _PAIR_Q], out_shape=[SDS((S, FOX_W), f32), SDS((N_PAIR, S, 128), f32)],
        name="fox_fwd", compiler_params=_params(("parallel",), 12 * S * 128, 16 * TQ * S),
    )(vr, vr, vr, fq, fk)


def _fox_bwd(vr, fq, fk, lse, d_out, delta):
    def body(q_ref, k_ref, v_ref, do_ref, fq_ref, fk_ref, lse_ref, dl_ref, dq_ref, dk_ref, dv_ref, dfc_ref, dfr_ref,
             dk_sc, dv_sc):
        dfc_ref[...] = jnp.zeros_like(dfc_ref)
        dfr_ref[...] = jnp.zeros_like(dfr_ref)
        for hh in range(2):
            hs = slice(hh * HD, (hh + 1) * HD)
            dk_sc[...] = jnp.zeros_like(dk_sc)
            dv_sc[...] = jnp.zeros_like(dv_sc)
            for qi in range(S // TQ):
                n = (qi + 1) * TQ
                rows = slice(qi * TQ, n)
                q, do, k, v = q_ref[rows, hs], do_ref[rows, hs], k_ref[0:n, hs], v_ref[0:n, hs]
                p = jnp.exp(_fox_scores(q_ref, k_ref, fq_ref, fk_ref, qi, hh) - lse_ref[rows, hh:hh + 1])
                ds = p * (_nt(do, v) - dl_ref[rows, hh:hh + 1])
                dsb = ds.astype(bf16)
                dq_ref[rows, hs] = jnp.dot(dsb, k, preferred_element_type=f32) * SCALE
                dk_sc[0:n, :] += _tn(dsb, q) * SCALE
                dv_sc[0:n, :] += _tn(p.astype(bf16), do)
                dfc_ref[hh:hh + 1, 0:n] -= jnp.sum(ds, axis=0, keepdims=True)
                dfr_ref[rows, hh:hh + 1] = jnp.sum(ds, axis=-1, keepdims=True)
            dk_ref[:, hs] = dk_sc[...]
            dv_ref[:, hs] = dv_sc[...]

    cols = [_pair_cols(k * N_PAIR) for k in range(3)]
    return pl.pallas_call(
        body, grid=(N_PAIR,), in_specs=cols + [_pair_cols(0), _PAIR_Q, _PAIR_K, _PAIR_Q, _PAIR_Q],
        out_specs=[_pair_cols(0)] * 3 + [_PAIR_K, _PAIR_Q],
        out_shape=[SDS((S, FOX_W), f32)] * 3 + [SDS((N_PAIR, 8, S), f32), SDS((N_PAIR, S, 128), f32)],
        scratch_shapes=[pltpu.VMEM((S, HD), f32)] * 2, name="fox_bwd",
        compiler_params=_params(("parallel",), 32 * S * 128, 24 * TQ * S),
    )(vr, vr, vr, d_out, fq, fk, lse, delta)


def _merge_fwd(out_a, out_b, w_a, w_b, gf):
    cw = D // N_SHARD

    def body(oa_ref, ob_ref, wa_ref, wb_ref, ga_ref, gb_ref, ya_ref, yb_ref, mg_ref):
        oa, ob = oa_ref[...].astype(bf16), ob_ref[...].astype(bf16)
        for j in range(N_SHARD):
            cols = slice(j * cw, (j + 1) * cw)
            ya = jnp.dot(oa, wa_ref[j], preferred_element_type=f32)
            yb = jnp.dot(ob, wb_ref[j], preferred_element_type=f32)
            ya_ref[:, cols] = ya
            yb_ref[:, cols] = yb
            mg_ref[:, cols] = (jax.nn.sigmoid(ga_ref[:, cols]) * ya + jax.nn.sigmoid(gb_ref[:, cols]) * yb).astype(bf16)

    full = lambda a: pl.BlockSpec(a.shape, lambda i: (0, 0, 0))
    return pl.pallas_call(
        body, grid=(S // TM,),
        in_specs=[_row(DIL_W), _row(FOX_W), full(w_a), full(w_b), _row(D), pl.BlockSpec((TM, D), lambda i: (i, 1))],
        out_specs=[_row(D)] * 3, out_shape=[SDS((S, D), f32), SDS((S, D), f32), SDS((S, D), bf16)], name="merge_fwd",
        compiler_params=_params(("parallel",), 22 * TM * D + 2 * (DIL_W + FOX_W) * D, 16 * TM * D),
    )(out_a, out_b, w_a, w_b, gf, gf)


def _merge_bwd(d_mix, w_out, ya, yb, gf):
    def body(dx_ref, w_ref, ya_ref, yb_ref, ga_ref, gb_ref, dya_ref, dyb_ref, dg_ref):
        dm = _nt(dx_ref[...], w_ref[...])
        sa, sb = jax.nn.sigmoid(ga_ref[...]), jax.nn.sigmoid(gb_ref[...])
        dya_ref[...] = (dm * sa).astype(bf16)
        dyb_ref[...] = (dm * sb).astype(bf16)
        dg_ref[:, :D] = (dm * ya_ref[...] * sa * (1.0 - sa)).astype(bf16)
        dg_ref[:, D:] = (dm * yb_ref[...] * sb * (1.0 - sb)).astype(bf16)

    return pl.pallas_call(
        body, grid=(S // TM,),
        in_specs=[_row(D), _whole(w_out)] + [_row(D)] * 3 + [pl.BlockSpec((TM, D), lambda i: (i, 1))],
        out_specs=[_row(D), _row(D), _row(2 * D)],
        out_shape=[SDS((S, D), bf16), SDS((S, D), bf16), SDS((S, 2 * D), bf16)], name="proj_out_bwd_merge",
        compiler_params=_params(("parallel",), 26 * TM * D + 2 * D * D, 28 * TM * D))(d_mix, w_out, ya, yb, gf, gf)


def _branch_bwd(d_ya, d_yb, w_a, w_b, out_a, out_b):
    cw = D // N_SHARD

    def body(dya_ref, dyb_ref, wa_ref, wb_ref, oa_ref, ob_ref, doa_ref, dla_ref, dob_ref, dlb_ref):
        doa = jnp.zeros((TM, DIL_W), f32)
        dob = jnp.zeros((TM, FOX_W), f32)
        for j in range(N_SHARD):
            cols = slice(j * cw, (j + 1) * cw)
            doa += _nt(dya_ref[:, cols], wa_ref[j])
            dob += _nt(dyb_ref[:, cols], wb_ref[j])
        doa_ref[...] = doa
        dob_ref[...] = dob.astype(bf16)
        prod_a = doa * oa_ref[...]
        for h in range(SLOTS):
            hs = slice(h * HD, (h + 1) * HD)
            dla_ref[:, hs] = jnp.broadcast_to(jnp.sum(prod_a[:, hs], axis=-1, keepdims=True), (TM, HD))
        prod_b = dob * ob_ref[...]
        dlb_ref[...] = jnp.zeros_like(dlb_ref)
        for h in range(N_FOX):
            dlb_ref[h // 2, :, h % 2:h % 2 + 1] = jnp.sum(prod_b[:, h * HD:(h + 1) * HD], axis=-1, keepdims=True)

    full = lambda a: pl.BlockSpec(a.shape, lambda i: (0, 0, 0))
    return pl.pallas_call(
        body, grid=(S // TM,),
        in_specs=[_row(D), _row(D), full(w_a), full(w_b), _row(DIL_W), _row(FOX_W)],
        out_specs=[_row(DIL_W), _row(DIL_W), _row(FOX_W), pl.BlockSpec((N_PAIR, TM, 128), lambda i: (0, i, 0))],
        out_shape=[SDS((S, DIL_W), f32), SDS((S, DIL_W), f32), SDS((S, FOX_W), bf16), SDS((N_PAIR, S, 128), f32)],
        name="branch_bwd", compiler_params=_params(("parallel",), 8 * TM * D + 2 * (DIL_W + FOX_W) * D, 8 * TM * D),
    )(d_ya, d_yb, w_a, w_b, out_a, out_b)


def _branch_grads(out_a, out_b, d_ya, d_yb):
    cw = D // N_SHARD

    def body(oa_ref, ob_ref, dya_ref, dyb_ref, ga_ref, gb_ref):
        ga_ref[...] = _tn(oa_ref[...].astype(bf16), dya_ref[...]).astype(bf16)
        gb_ref[...] = _tn(ob_ref[...].astype(bf16), dyb_ref[...]).astype(bf16)

    whole = lambda w: pl.BlockSpec((S, w), lambda j: (0, 0))
    cols = pl.BlockSpec((S, cw), lambda j: (0, j))
    return pl.pallas_call(
        body, grid=(N_SHARD,), in_specs=[whole(DIL_W), whole(FOX_W), cols, cols],
        out_specs=[pl.BlockSpec((None, DIL_W, cw), lambda j: (j, 0, 0)), pl.BlockSpec((None, FOX_W, cw), lambda j: (j, 0, 0))],
        out_shape=[SDS((N_SHARD, DIL_W, cw), bf16), SDS((N_SHARD, FOX_W, cw), bf16)], name="grad_w_proj_ab",
        compiler_params=_params(("parallel",), 4 * S * (DIL_W + FOX_W) + 4 * S * cw + 4 * (DIL_W + FOX_W) * cw,
                                4 * S * (DIL_W + FOX_W)))(out_a, out_b, d_ya, d_yb)


FF_TN = F_FF // 2
FF_TM = 1024


def _ffn_fwd(h, w_gate_t, w_up_t):
    def body(h_ref, wg_ref, wu_ref, g_ref, u_ref, a_ref):
        hb = h_ref[...]
        g = _nt(hb, wg_ref[...])
        u = _nt(hb, wu_ref[...])
        g_ref[...] = g
        u_ref[...] = u
        a_ref[...] = (g * jax.nn.sigmoid(g) * u).astype(bf16)

    tile = pl.BlockSpec((FF_TM, FF_TN), lambda j, i: (i, j))
    wspec = pl.BlockSpec((FF_TN, D), lambda j, i: (j, 0))
    return pl.pallas_call(
        body, grid=(F_FF // FF_TN, S // FF_TM),
        in_specs=[pl.BlockSpec((FF_TM, D), lambda j, i: (i, 0)), wspec, wspec], out_specs=[tile] * 3,
        out_shape=[SDS((S, F_FF), f32), SDS((S, F_FF), f32), SDS((S, F_FF), bf16)], name="ffn_fwd",
        compiler_params=_params(("parallel", "parallel"), 2 * FF_TM * D + 4 * D * FF_TN + 10 * FF_TM * FF_TN, 16 * FF_TM * FF_TN),
    )(h, w_gate_t, w_up_t)


def _ffn_bwd_act(d_ff, w_down, g_act, u_act):
    def body(d_ref, wd_ref, g_ref, u_ref, dg_ref, du_ref):
        da = _nt(d_ref[...], wd_ref[...])
        g = g_ref[...]
        sg = jax.nn.sigmoid(g)
        du_ref[...] = (da * g * sg).astype(bf16)
        dg_ref[...] = (da * u_ref[...] * sg * (1.0 + g * (1.0 - sg))).astype(bf16)

    tile = pl.BlockSpec((FF_TM, FF_TN), lambda j, i: (i, j))
    return pl.pallas_call(
        body, grid=(F_FF // FF_TN, S // FF_TM),
        in_specs=[pl.BlockSpec((FF_TM, D), lambda j, i: (i, 0)), pl.BlockSpec((FF_TN, D), lambda j, i: (j, 0)), tile, tile],
        out_specs=[tile, tile], out_shape=[SDS((S, F_FF), bf16)] * 2, name="ffn_bwd_act",
        compiler_params=_params(("parallel", "parallel"), 2 * FF_TM * D + 2 * D * FF_TN + 12 * FF_TM * FF_TN, 16 * FF_TM * FF_TN),
    )(d_ff, w_down, g_act, u_act)


def _row_tile(rows):
    return next(t for t in (376, 128, 176, 64, 32, 16, 8) if rows % t == 0)


def _adamw_math(w, g, m, v):
    c1 = 1.0 - ADAM_B1 ** ADAM_STEP
    c2 = 1.0 - ADAM_B2 ** ADAM_STEP
    m_new = ADAM_B1 * m + (1.0 - ADAM_B1) * g
    v_new = ADAM_B2 * v + (1.0 - ADAM_B2) * (g * g)
    return -ADAM_LR * ((m_new / c1) / (jnp.sqrt(v_new / c2) + ADAM_EPS) + ADAM_WD * w), m_new, v_new


def _adamw(w, g, m, v, name):
    rows, cols = w.shape
    tm = _row_tile(rows)

    def body(w_ref, g_ref, m_ref, v_ref, d_ref, nm_ref, nv_ref):
        d_ref[...], nm_ref[...], nv_ref[...] = _adamw_math(w_ref[...], g_ref[...], m_ref[...], v_ref[...])

    spec = pl.BlockSpec((tm, cols), lambda i: (i, 0))
    return pl.pallas_call(
        body, grid=(rows // tm,), in_specs=[spec] * 4, out_specs=[spec] * 3, out_shape=[SDS(w.shape, f32)] * 3,
        name=name, compiler_params=_params(("parallel",), 28 * tm * cols, 16 * tm * cols))(w, g, m, v)


def _adamw_halves(w, g_mine, g_theirs, m, v, name):
    cols = w.shape[1]
    tm = _row_tile(g_mine.shape[0])
    per_half = g_mine.shape[0] // tm
    assert 2 * g_mine.shape[0] - w.shape[0] < tm
    core = lax.axis_index("c").astype(jnp.int32).reshape(1)

    def body(c_ref, w_ref, gm_ref, gt_ref, m_ref, v_ref, g_ref, d_ref, nm_ref, nv_ref):
        mine = pl.program_id(0) // per_half == c_ref[0]
        g = jnp.where(mine, gm_ref[...], gt_ref[...])
        g_ref[...] = g
        d_ref[...], nm_ref[...], nv_ref[...] = _adamw_math(w_ref[...], g, m_ref[...], v_ref[...])

    spec = pl.BlockSpec((tm, cols), lambda i, c_ref: (i, 0))
    in_half = lambda i, first: jnp.clip(i - first * per_half, 0, per_half - 1)
    grid_spec = pltpu.PrefetchScalarGridSpec(
        num_scalar_prefetch=1, grid=(2 * per_half,),
        in_specs=[spec, pl.BlockSpec((tm, cols), lambda i, c_ref: (in_half(i, c_ref[0]), 0)),
                  pl.BlockSpec((tm, cols), lambda i, c_ref: (in_half(i, 1 - c_ref[0]), 0)), spec, spec],
        out_specs=[spec] * 4)
    return pl.pallas_call(
        body, grid_spec=grid_spec, out_shape=[SDS(w.shape, f32)] * 4, name=name,
        compiler_params=_params(("parallel",), 36 * tm * cols, 16 * tm * cols))(core, w, g_mine, g_theirs, m, v)


_ANY = pl.BlockSpec(memory_space=pl.ANY)


def _place():
    x, y, c = lax.axis_index("x"), lax.axis_index("y"), lax.axis_index("c")
    chips = [(1 - x, y), (x, 1 - y), (1 - x, 1 - y)]
    return x, y, c, chips


def _halved(t):
    return t.reshape(t.shape[:-2] + (2, t.shape[-2] // 2, t.shape[-1]))


def _gather_body(src, out, send_ici, recv_ici, send_d2d, recv_d2d):
    x, y, c, chips = _place()
    sibling = (x, y, 1 - c)
    me_j = 2 * x + y
    sends = []
    for a in range(len(src)):
        for p in range(3):
            cp = pltpu.make_async_remote_copy(
                src_ref=src[a].at[c], dst_ref=out[a].at[me_j, c], send_sem=send_ici.at[a, p],
                recv_sem=recv_ici.at[a, p], device_id=(*chips[p], c), device_id_type=MESH)
            cp.start()
            sends.append(cp)
    for a in range(len(src)):
        for p, (px, py) in enumerate(chips):
            blk = out[a].at[2 * px + py, c]
            pltpu.make_async_remote_copy(
                src_ref=blk, dst_ref=blk, send_sem=send_ici.at[a, p], recv_sem=recv_ici.at[a, p],
                device_id=sibling, device_id_type=MESH).wait_recv()
            fw = pltpu.make_async_remote_copy(
                src_ref=blk, dst_ref=blk, send_sem=send_d2d.at[a, p], recv_sem=recv_d2d.at[a, p],
                device_id=sibling, device_id_type=MESH)
            fw.start()
            sends.append(fw)
    for a in range(len(src)):
        for p, (px, py) in enumerate(chips):
            blk = out[a].at[2 * px + py, 1 - c]
            pltpu.make_async_remote_copy(
                src_ref=blk, dst_ref=blk, send_sem=send_d2d.at[a, p], recv_sem=recv_d2d.at[a, p],
                device_id=sibling, device_id_type=MESH).wait_recv()
    for cp in sends:
        cp.wait_send()


def _handshake(peers):
    barrier = pltpu.get_barrier_semaphore()
    for peer in peers:
        pl.semaphore_signal(barrier, inc=1, device_id=peer, device_id_type=MESH)
    pl.semaphore_wait(barrier, len(peers))


_SEQUENCER = dict(axis_name="sequencer", num_cores=1)
GATHER_LATE_ID, SCATTER_EARLY_ID, SWAP_EARLY_ID, GATHER_FIRST_ID, SCATTER_LATE_ID, SHARE_SMALL_ID = 1, 2, 3, 4, 5, 6


def _all_gather_async(shards, after, name, collective_id):
    n, k = len(shards), len(after)

    def body(*refs):
        x, y, c, chips = _place()
        _handshake([(*chip, c) for chip in chips] + [(x, y, 1 - c)])
        _gather_body(refs[:n], refs[n + k:2 * n + k], *refs[2 * n + k:])

    return pl.kernel(
        body, out_type=[SDS((N_SHARD,) + t.shape, t.dtype) for t in shards],
        mesh=plsc.ScalarSubcoreMesh(**_SEQUENCER), scratch_types=[pltpu.SemaphoreType.DMA((n, 3))] * 4,
        compiler_params=pltpu.CompilerParams(collective_id=collective_id), name=name)(*shards, *after)


def _pair_swap(grads):
    n = len(grads)

    def body(*refs):
        src, out, send_sems, recv_sems = refs[:n], refs[n:2 * n], refs[2 * n], refs[2 * n + 1]
        x, y, c, _ = _place()
        copies = [pltpu.make_async_remote_copy(
            src_ref=src[a].at[:, 1 - c], dst_ref=out[a], send_sem=send_sems.at[a], recv_sem=recv_sems.at[a],
            device_id=(x, y, 1 - c), device_id_type=MESH) for a in range(n)]
        for cp in copies:
            cp.start()
        for cp in copies:
            cp.wait()

    return pl.pallas_call(
        body, in_specs=[_ANY] * n, out_specs=[_ANY] * n,
        out_shape=[SDS((N_SHARD,) + t.shape[2:], t.dtype) for t in grads],
        scratch_shapes=[pltpu.SemaphoreType.DMA((n,)), pltpu.SemaphoreType.DMA((n,))], name="pair_swap",
        compiler_params=pltpu.CompilerParams(has_side_effects=True))(*grads)


def _pair_swap_early(grads):
    n = len(grads)

    def body(*refs):
        src, out, send_sems, recv_sems = refs[:n], refs[n:2 * n], refs[2 * n], refs[2 * n + 1]
        x, y, c, _ = _place()
        _handshake([(x, y, 1 - c)])
        copies = [pltpu.make_async_remote_copy(
            src_ref=src[a].at[:, 1 - c], dst_ref=out[a], send_sem=send_sems.at[a], recv_sem=recv_sems.at[a],
            device_id=(x, y, 1 - c), device_id_type=MESH) for a in range(n)]
        for cp in copies:
            cp.start()
        for cp in copies:
            cp.wait()

    return pl.kernel(
        body, out_type=[SDS((N_SHARD,) + t.shape[2:], t.dtype) for t in grads],
        mesh=plsc.ScalarSubcoreMesh(**_SEQUENCER), scratch_types=[pltpu.SemaphoreType.DMA((n,))] * 2,
        compiler_params=pltpu.CompilerParams(collective_id=SWAP_EARLY_ID), name="pair_swap_early")(*grads)


def _scatter_parts(parts, name, collective_id):
    n = len(parts)

    def body(*refs):
        part, recv, send_sems, recv_sems = refs[:n], refs[n:2 * n], refs[2 * n], refs[2 * n + 1]
        x, y, c, chips = _place()
        _handshake([(*chip, c) for chip in chips])
        me_j = 2 * x + y
        sends = []
        for a in range(n):
            for p, (px, py) in enumerate(chips):
                cp = pltpu.make_async_remote_copy(
                    src_ref=part[a].at[2 * px + py], dst_ref=recv[a].at[me_j], send_sem=send_sems.at[a, p],
                    recv_sem=recv_sems.at[a, p], device_id=(px, py, c), device_id_type=MESH)
                cp.start()
                sends.append(cp)
        for a in range(n):
            for p, (px, py) in enumerate(chips):
                slot = recv[a].at[2 * px + py]
                pltpu.make_async_remote_copy(
                    src_ref=slot, dst_ref=slot, send_sem=send_sems.at[a, p], recv_sem=recv_sems.at[a, p],
                    device_id=(px, py, c), device_id_type=MESH).wait_recv()
        for cp in sends:
            cp.wait_send()

    return pl.kernel(
        body, out_type=[SDS(t.shape, t.dtype) for t in parts],
        mesh=plsc.ScalarSubcoreMesh(**_SEQUENCER), scratch_types=[pltpu.SemaphoreType.DMA((n, 3))] * 2,
        compiler_params=pltpu.CompilerParams(collective_id=collective_id), name=name)(*parts)


def _pair_sum(grads, other, name):
    _, _, rows, cols = grads.shape
    tr = _row_tile(rows)
    core = lax.axis_index("c").astype(jnp.int32).reshape(1)

    def body(c_ref, g_ref, o_ref, out_ref):
        out_ref[...] = (g_ref[...].astype(f32) + o_ref[...].astype(f32)).astype(bf16)

    grid_spec = pltpu.PrefetchScalarGridSpec(
        num_scalar_prefetch=1, grid=(N_SHARD, rows // tr),
        in_specs=[pl.BlockSpec((None, None, tr, cols), lambda j, i, c_ref: (j, c_ref[0], i, 0)),
                  pl.BlockSpec((None, tr, cols), lambda j, i, c_ref: (j, i, 0))],
        out_specs=pl.BlockSpec((None, tr, cols), lambda j, i, c_ref: (j, i, 0)))
    return pl.pallas_call(
        body, grid_spec=grid_spec, out_shape=SDS((N_SHARD, rows, cols), bf16), name=name,
        compiler_params=_params(("parallel", "parallel"), 10 * tr * cols, 12 * tr * cols))(core, grads, other)


def _share_small(small):
    def body(small_ref, small_all_ref, ssend, srecv, local_sem):
        x, y, c, _ = _place()
        flip = lambda a, bit: 1 - a if bit else a
        peers = [(flip(x, k & 4), flip(y, k & 2), flip(c, k & 1)) for k in range(1, 8)]
        _handshake(peers)
        me_dev = 4 * x + 2 * y + c
        own = pltpu.make_async_copy(small_ref, small_all_ref.at[me_dev], local_sem)
        own.start()
        sends = []
        for k, to in enumerate(peers):
            cp = pltpu.make_async_remote_copy(
                src_ref=small_ref, dst_ref=small_all_ref.at[me_dev],
                send_sem=ssend.at[k], recv_sem=srecv.at[k], device_id=to, device_id_type=MESH)
            cp.start()
            sends.append(cp)
        for k, (px, py, pc) in enumerate(peers):
            slot = small_all_ref.at[4 * px + 2 * py + pc]
            pltpu.make_async_remote_copy(
                src_ref=slot, dst_ref=slot, send_sem=ssend.at[k], recv_sem=srecv.at[k],
                device_id=(px, py, pc), device_id_type=MESH).wait_recv()
        for cp in sends:
            cp.wait_send()
        own.wait()

    return pl.kernel(
        body, out_type=SDS((8, SMALL_ROWS, D), f32), mesh=plsc.ScalarSubcoreMesh(**_SEQUENCER),
        scratch_types=[pltpu.SemaphoreType.DMA((7,)), pltpu.SemaphoreType.DMA((7,)), pltpu.SemaphoreType.DMA],
        compiler_params=pltpu.CompilerParams(collective_id=SHARE_SMALL_ID), name="share_small")(small)


def _sum_partials(part, recv, name):
    _, rows, cols = recv.shape
    tr = _row_tile(rows)
    me = (2 * lax.axis_index("x") + lax.axis_index("y")).astype(jnp.int32).reshape(1)

    def body(me_ref, mine, r0, r1, r2, r3, out_ref):
        acc = None
        for j, r in enumerate((r0, r1, r2, r3)):
            term = jnp.where(me_ref[0] == j, mine[...], r[...]).astype(f32)
            acc = term if acc is None else acc + term
        out_ref[...] = acc

    slot = lambda j: pl.BlockSpec((None, tr, cols), lambda i, me_ref: (jnp.where(me_ref[0] == j, j ^ 1, j), i, 0))
    grid_spec = pltpu.PrefetchScalarGridSpec(
        num_scalar_prefetch=1, grid=(rows // tr,),
        in_specs=[pl.BlockSpec((None, tr, cols), lambda i, me_ref: (me_ref[0], i, 0)), slot(0), slot(1), slot(2), slot(3)],
        out_specs=pl.BlockSpec((tr, cols), lambda i, me_ref: (i, 0)))
    return pl.pallas_call(
        body, grid_spec=grid_spec, out_shape=SDS((rows, cols), f32), name=name,
        compiler_params=_params(("parallel",), 14 * tr * cols, 12 * tr * cols))(me, part, recv, recv, recv, recv)


def _sum_small(small_all):
    def body(small_ref, out_ref):
        tot = small_ref[0]
        for k in range(1, 8):
            tot = tot + small_ref[k]
        out_ref[...] = tot

    return pl.pallas_call(
        body, grid=(1,), in_specs=[pl.BlockSpec((8, SMALL_ROWS, D), lambda i: (0, 0, 0))],
        out_specs=pl.BlockSpec((SMALL_ROWS, D), lambda i: (0, 0)), out_shape=SDS((SMALL_ROWS, D), f32),
        name="sum_small", compiler_params=_params(("arbitrary",), 36 * SMALL_ROWS * D))(small_all)


def _swap_halves(halves, name):
    n = len(halves)

    def body(*refs):
        src, out, send_sems, recv_sems = refs[:n], refs[n:2 * n], refs[2 * n], refs[2 * n + 1]
        x, y, c, _ = _place()
        copies = [pltpu.make_async_remote_copy(
            src_ref=src[a], dst_ref=out[a], send_sem=send_sems.at[a], recv_sem=recv_sems.at[a],
            device_id=(x, y, 1 - c), device_id_type=MESH) for a in range(n)]
        for cp in copies:
            cp.start()
        for cp in copies:
            cp.wait()

    return pl.pallas_call(
        body, in_specs=[_ANY] * n, out_specs=[_ANY] * n, out_shape=[SDS(t.shape, f32) for t in halves],
        scratch_shapes=[pltpu.SemaphoreType.DMA((n,))] * 2, name=name,
        compiler_params=pltpu.CompilerParams(has_side_effects=True))(*halves)


def _kernel_layout(name, t):
    t = t[0]
    return jnp.swapaxes(t, 0, 1) if name in TRANSPOSED else t


def _harness_layout(name, t):
    if name in TRANSPOSED:
        t = jnp.swapaxes(t, 0, 1)
    return t[None]


def _pad_rows(t, rows):
    return t if t.shape[0] == rows else jnp.pad(t, ((0, rows - t.shape[0]), (0, 0)))


_QA, _KA, _VA, _QB, _F, _GAB = 0, 768, 1536, 2304, 3840, 3848


def _spans(a, b):
    return [(j, max(a, j * IN_SHARD) - j * IN_SHARD, max(a, j * IN_SHARD) - a,
             min(b, (j + 1) * IN_SHARD) - max(a, j * IN_SHARD))
            for j in range(N_SHARD) if max(a, j * IN_SHARD) < min(b, (j + 1) * IN_SHARD)]


_LANES = pl.BlockSpec((N_SHARD, IN_SHARD_PAD, 128), lambda c: (0, 0, c))


def _split_w_in(shards):
    group = [[(o + g * DIL_W, o + (g + 1) * DIL_W) for o in (_QA, _KA, _VA)] for g in range(3)]
    fox = [[(_QB + k * FOX_W, _QB + (k + 1) * FOX_W)] for k in range(3)]
    wanted = group + fox + [[(_QB, _F)], [(_F, _GAB)], [(_GAB, IN_COLS)]]
    rows = [sum(b - a for a, b in w) for w in wanted]
    rows[7] = 128

    def body(s_ref, *o_refs):
        for o_ref, want in zip(o_refs, wanted):
            at = 0
            for a, b in want:
                for j, src, off, n in _spans(a, b):
                    o_ref[at + off:at + off + n, :] = s_ref[j, src:src + n, :]
                at += b - a
        o_refs[7][N_FOX:, :] = jnp.zeros((128 - N_FOX, 128), bf16)

    return pl.pallas_call(
        body, grid=(D // 128,), in_specs=[_LANES], out_specs=[pl.BlockSpec((r, 128), lambda c: (0, c)) for r in rows],
        out_shape=[SDS((r, D), bf16) for r in rows], name="split_w_in",
        compiler_params=_params(("parallel",), 2 * 128 * (N_SHARD * IN_SHARD_PAD + sum(rows))))(shards)


def _join_w_in(g_a, g_fox, g_f, g_gab):
    parts = [(g_a[k], o, o + DIL_W) for o in (0, DIL_W, 2 * DIL_W) for k in range(3)]
    parts += [(t, 0, FOX_W) for t in g_fox] + [(g_f, 0, N_FOX), (g_gab, 0, 2 * D)]
    arrays = list(g_a) + list(g_fox) + [g_f, g_gab]
    index = {id(t): i for i, t in enumerate(arrays)}

    def body(*refs):
        o_ref = refs[-1]
        o_ref[:, IN_SHARD:, :] = jnp.zeros((N_SHARD, IN_SHARD_PAD - IN_SHARD, 128), bf16)
        at = 0
        for t, lo, hi in parts:
            src_ref = refs[index[id(t)]]
            for j, dst, off, n in _spans(at, at + hi - lo):
                o_ref[j, dst:dst + n, :] = src_ref[lo + off:lo + off + n, :].astype(bf16)
            at += hi - lo

    return pl.pallas_call(
        body, grid=(D // 128,), in_specs=[pl.BlockSpec((t.shape[0], 128), lambda c: (0, c)) for t in arrays],
        out_specs=_LANES, out_shape=SDS((N_SHARD, IN_SHARD_PAD, D), bf16), name="join_w_in",
        compiler_params=_params(("parallel",), 2 * 128 * (N_SHARD * IN_SHARD_PAD + sum(t.shape[0] for t in arrays))),
    )(*arrays)


def _full_weights(gathered):
    full = {n: t.reshape((N_SHARD,) + SHARD_SHAPE[n]) for n, t in gathered.items()}
    out = {}
    if "w_in" in full:
        pieces = _split_w_in(full["w_in"])
        out.update(w_a_t=pieces[0:3], w_fox_t=pieces[3:6], w_vr_t=pieces[6], w_f_t=pieces[7], w_gab_t=pieces[8])
    if "w_out" in full:
        out.update(
            w_a4=full["w_proj_a"],
            w_b4=full["w_proj_b"],
            w_out=full["w_out"].reshape(D, D),
            w_gate_t=full["w_ffn_gate"].reshape(F_FF, D),
            w_up_t=full["w_ffn_up"].reshape(F_FF, D),
            w_down=full["w_ffn_down"].reshape(F_FF, D))
    return out


def _sharded_grads(g):
    full = dict(w_in=_join_w_in(g["w_a_t"], g["w_fox_t"], g["w_f_t"], g["w_gab_t"]), w_proj_a=g["w_a4"],
                w_proj_b=g["w_b4"], w_out=g["w_out"], w_ffn_gate=g["w_gate_t"], w_ffn_up=g["w_up_t"],
                w_ffn_down=g["w_down"])
    return {n: _halved(full[n].reshape((N_SHARD,) + SHARD_SHAPE[n])) for n in W_NAMES}


def _local_step(x, target, wt, b_forget, g_mix_pre, g_mix_post, g_ffn_pre, g_ffn_post, late=None):
    tables = _rope_tables()
    b128 = jnp.pad(b_forget, ((0, 0), (0, 128 - N_FOX)))
    dils = tuple(d for _, d in DIL_GROUPS[1:])

    hs = _norm_fwd([x] + list(_perm_rows([x], dils, "perm_x")), g_mix_pre)
    h1 = hs[0]
    if callable(wt):
        wt = wt(h1)
    qkv = [_rope_fwd(g, _mm([(hs[g], wt["w_a_t"][g])], "nt", f32, tm=1024, tn=QKV_W, name=f"proj_a_{g}"), tables)
           for g in range(3)]
    vr = _mm([(h1, wt["w_vr_t"])], "nt", bf16, tm=1024, tn=VR_W // 2, name="proj_vr")
    gab = _mm([(h1, wt["w_gab_t"])], "nt", f32, tm=512, tn=2 * D, name="proj_gab")
    fz = _mm([(h1, wt["w_f_t"])], "nt", f32, tm=1024, tn=128, name="proj_f")
    dil = [_dil_fwd(g, qkv[g]) for g in range(3)]
    out_a, lse_a = _dil_combine([o for o, _ in dil], [l for _, l in dil])
    f_q, f_k = _forget_fwd(fz, b128)
    out_b, lse_b = _fox_fwd(vr, f_q, f_k)
    if late is not None:
        wt = {**wt, **late(out_b)}
    ya, yb, merged = _merge_fwd(out_a, out_b, wt["w_a4"], wt["w_b4"], gab)
    mix, x2, h3 = _resid_norm_fwd(x, merged, wt["w_out"], g_mix_post, g_ffn_pre)
    g_act, u_act, a_act = _ffn_fwd(h3, wt["w_gate_t"], wt["w_up_t"])
    sq_err, dy, d_ff, dg_ffn_post = _loss_head(x2, a_act, wt["w_down"], g_ffn_post, target)

    grads = {}
    d_g, d_u = _ffn_bwd_act(d_ff, wt["w_down"], g_act, u_act)
    grads["w_down"] = _mm([(a_act, d_ff)], "tn", bf16, tm=FF_TN, tn=D, name="grad_w_down")
    grads["w_gate_t"] = _mm([(d_g, h3)], "tn", bf16, tm=FF_TN, tn=D, name="grad_w_gate")
    grads["w_up_t"] = _mm([(d_u, h3)], "tn", bf16, tm=FF_TN, tn=D, name="grad_w_up")
    dx2, d_mix, dg_ffn_pre, dg_mix_post = _norm_bwd_mid(dy, d_g, d_u, wt["w_gate_t"], wt["w_up_t"], x2, mix,
                                                        g_ffn_pre, g_mix_post)

    grads["w_out"] = _mm([(merged, d_mix)], "tn", bf16, tm=D, tn=D, name="grad_w_out")
    d_ya, d_yb, d_gab = _merge_bwd(d_mix, wt["w_out"], ya, yb, gab)
    grads["w_a4"], grads["w_b4"] = _branch_grads(out_a, out_b, d_ya, d_yb)
    d_out_a, delta_a, d_out_b, delta_b = _branch_bwd(d_ya, d_yb, wt["w_a4"], wt["w_b4"], out_a, out_b)

    perm = _perm_rows([d_out_a, delta_a, lse_a], dils, "perm_dil_bwd")
    aux = [(d_out_a, delta_a, lse_a)] + [tuple(perm[k * len(dils) + i] for k in range(3)) for i in range(len(dils))]
    d_qkv = []
    for g in range(3):
        dq, dk, dv = _dil_bwd(g, qkv[g], *aux[g])
        d_qkv.append(_rope_bwd(g, dq, dk, dv, tables))
    *d_fox, d_f_cols, d_f_rows = _fox_bwd(vr, f_q, f_k, lse_b, d_out_b, delta_b)
    d_z, d_b128 = _forget_bwd(fz, b128, d_f_cols, d_f_rows)

    grads["w_a_t"] = [_mm([(d_qkv[g], hs[g])], "tn", bf16, tm=QKV_W, tn=D, name=f"grad_w_a_{g}") for g in range(3)]
    grads["w_fox_t"] = [_mm([(d_fox[k], h1)], "tn", bf16, tm=FOX_W, tn=D, name=f"grad_w_fox_{k}") for k in range(3)]
    grads["w_gab_t"] = _mm([(d_gab, h1)], "tn", bf16, tm=D, tn=D, name="grad_w_gab")
    grads["w_f_t"] = _mm([(d_z, h1)], "tn", bf16, tm=128, tn=D, name="grad_w_f")
    d_h1_nat = _mm([(d_qkv[0], wt["w_a_t"][0])] + list(zip(d_fox, wt["w_fox_t"]))
                   + [(d_gab, wt["w_gab_t"]), (d_z, wt["w_f_t"])], "nn", f32, tm=512, tn=D, name="proj_in_bwd")
    d_h1_dil = [_mm([(d_qkv[g], wt["w_a_t"][g])], "nn", f32, tm=1024, tn=D, name=f"proj_a_bwd_{g}") for g in (1, 2)]
    d_h1 = _unperm_sum(d_h1_nat, d_h1_dil, dils, "unperm_d_h1")
    grad_x, dg_mix_pre = _norm_bwd_in(dx2, d_h1, x, g_mix_pre)

    small = dict(b_forget=d_b128[:, :N_FOX], norm_mix_pre=dg_mix_pre, norm_mix_post=dg_mix_post,
                 norm_ffn_pre=dg_ffn_pre, norm_ffn_post=dg_ffn_post)
    grads["mid_backward"] = d_qkv[0]
    return sq_err, grad_x, grads, small


NORMS = ("norm_mix_pre", "norm_mix_post", "norm_ffn_pre", "norm_ffn_post")
ORDER = ("w_in", "w_proj_a", "w_proj_b", "w_out", "b_forget", "w_ffn_gate", "w_ffn_up", "w_ffn_down") + NORMS


def kernel(x, w_in, w_proj_a, w_proj_b, w_out, b_forget, w_ffn_gate, w_ffn_up, w_ffn_down, norm_mix_pre, norm_mix_post, norm_ffn_pre, norm_ffn_post, loss_target, m_w_in, m_w_proj_a, m_w_proj_b, m_w_out, m_b_forget, m_w_ffn_gate, m_w_ffn_up, m_w_ffn_down, m_norm_mix_pre, m_norm_mix_post, m_norm_ffn_pre, m_norm_ffn_post, v_w_in, v_w_proj_a, v_w_proj_b, v_w_out, v_b_forget, v_w_ffn_gate, v_w_ffn_up, v_w_ffn_down, v_norm_mix_pre, v_norm_mix_post, v_norm_ffn_pre, v_norm_ffn_post):
    given = dict(w_in=w_in, w_proj_a=w_proj_a, w_proj_b=w_proj_b, w_out=w_out, w_ffn_gate=w_ffn_gate,
                 w_ffn_up=w_ffn_up, w_ffn_down=w_ffn_down)
    given_m = dict(w_in=m_w_in, w_proj_a=m_w_proj_a, w_proj_b=m_w_proj_b, w_out=m_w_out, w_ffn_gate=m_w_ffn_gate,
                   w_ffn_up=m_w_ffn_up, w_ffn_down=m_w_ffn_down)
    given_v = dict(w_in=v_w_in, w_proj_a=v_w_proj_a, w_proj_b=v_w_proj_b, w_out=v_w_out, w_ffn_gate=v_w_ffn_gate,
                   w_ffn_up=v_w_ffn_up, w_ffn_down=v_w_ffn_down)
    w, m, v = ({n: _kernel_layout(n, t[n]) for n in W_NAMES} for t in (given, given_m, given_v))
    small_w = dict(b_forget=b_forget, norm_mix_pre=norm_mix_pre, norm_mix_post=norm_mix_post,
                   norm_ffn_pre=norm_ffn_pre, norm_ffn_post=norm_ffn_post)
    small_m = dict(b_forget=m_b_forget, norm_mix_pre=m_norm_mix_pre, norm_mix_post=m_norm_mix_post,
                   norm_ffn_pre=m_norm_ffn_pre, norm_ffn_post=m_norm_ffn_post)
    small_v = dict(b_forget=v_b_forget, norm_mix_pre=v_norm_mix_pre, norm_mix_post=v_norm_mix_post,
                   norm_ffn_pre=v_norm_ffn_pre, norm_ffn_post=v_norm_ffn_post)

    own = [_halved(_pad_rows(w[n].astype(bf16), SHARD_SHAPE[n][0])) for n in W_NAMES]
    chip = 2 * lax.axis_index("x") + lax.axis_index("y")
    exchanged = {"first": _all_gather_async(own[:1], [], "all_gather_first", GATHER_FIRST_ID)}
    fill = lambda ts, mine: [lax.dynamic_update_index_in_dim(t, o, chip, 0) for t, o in zip(ts, mine)]

    def first_weights(ready):
        arrived, _ = lax.optimization_barrier((list(exchanged["first"]), ready))
        exchanged["late"] = _all_gather_async(own[1:], [arrived[0][0, 0, :16, :128]], "all_gather_late", GATHER_LATE_ID)
        return _full_weights(dict(zip(W_NAMES[:1], fill(arrived, own[:1]))))

    def late_weights(ready):
        arrived, _ = lax.optimization_barrier((list(exchanged["late"]), ready))
        return _full_weights(dict(zip(W_NAMES[1:], fill(arrived, own[1:]))))

    sq_err, grad_x, grads, small = _local_step(x[0], loss_target[0], first_weights, b_forget, norm_mix_pre,
                                               norm_mix_post, norm_ffn_pre, norm_ffn_post, late=late_weights)

    g4 = _sharded_grads(grads)
    stack = lambda t, extra: jnp.concatenate(
        [jnp.pad(t["b_forget"], ((0, 0), (0, D - N_FOX)))] + [t[n] for n in NORMS]
        + [jnp.pad(extra, ((0, SMALL_ROWS - LOSS_ROW - 1), (0, D - extra.shape[1])), constant_values=1.0)], axis=0)
    early, _ = lax.optimization_barrier((list(_pair_swap_early([g4[n] for n in W_NAMES[1:]])), grads["mid_backward"]))
    other = list(_pair_swap([g4["w_in"]])) + early
    parts = [_pair_sum(g4[n], o, "pair_sum_" + n) for n, o in zip(W_NAMES, other)]
    recv_early = _scatter_parts(parts[1:], "scatter_early", SCATTER_EARLY_ID)
    recv_in = _scatter_parts(parts[:1], "scatter_partials", SCATTER_LATE_ID)
    small_all = _share_small(stack(small, sq_err))

    g_shard, delta, new_m, new_v = {}, {}, {}, {}

    def summed(names, parts, recv):
        halves = [_sum_partials(p, r, "sum_partials_" + n) for n, p, r in zip(names, parts, recv)]
        return halves, list(_swap_halves(halves, "swap_halves_" + names[0]))

    def update(names, halves, theirs):
        for n, mine, other_half in zip(names, halves, theirs):
            g_shard[n], delta[n], new_m[n], new_v[n] = _adamw_halves(w[n], mine, other_half, m[n], v[n], "adamw_" + n)

    recv_early, _ = lax.optimization_barrier((list(recv_early), parts[0]))
    early_mine, early_theirs = summed(W_NAMES[1:], parts[1:], recv_early)
    recv_in, _ = lax.optimization_barrier((list(recv_in), early_theirs))
    update(W_NAMES[:1], *summed(W_NAMES[:1], parts[:1], recv_in))
    (early_theirs, small_all), _ = lax.optimization_barrier(((early_theirs, small_all), delta["w_in"]))
    update(W_NAMES[1:], early_mine, early_theirs)
    small_sum = _sum_small(small_all)
    loss = small_sum[LOSS_ROW, 0] * (0.5 / D)
    ones = jnp.ones((1, 128), f32)
    sd, sm, sv = _adamw(stack(small_w, ones), small_sum, stack(small_m, ones), stack(small_v, ones), "adamw_small")

    outs = [loss, grad_x[None]]
    for big, st in ((g_shard, small_sum), (delta, sd), (new_m, sm), (new_v, sv)):
        t = {n: _harness_layout(n, big[n]) for n in W_NAMES}
        t["b_forget"] = st[0:1, :N_FOX]
        for i, n in enumerate(NORMS):
            t[n] = st[i + 1:i + 2]
        outs += [t[n] for n in ORDER]
    return tuple(outs)
```

```python
import functools
import math

import jax
import jax.numpy as jnp
import numpy as np
from jax import lax
from jax.experimental import pallas as pl
from jax.experimental.pallas import tpu as pltpu
from jax.experimental.pallas import tpu_sc as plsc

f32 = jnp.float32
bf16 = jnp.bfloat16
SDS = jax.ShapeDtypeStruct
MESH = pl.DeviceIdType.MESH

S = 2048
D = 1024
HD = 64
BLK = 128
N_FOX = 8
FOX_W = N_FOX * HD
DIL_GROUPS = ((128, 1), (512, 4), (2048, 16))
SLOTS = 4
DIL_W = SLOTS * HD
QKV_W = 3 * DIL_W
VR_W = 3 * FOX_W
GF_W = 2 * D + 128
F_FF = 2816
ROPE_DIM = 16
ROPE_THETA = 500000.0
EPS = 1e-6
NEG = -1e30
SCALE = 1.0 / math.sqrt(HD)
IN_COLS = 5896
N_SHARD = 4

ADAM_LR, ADAM_B1, ADAM_B2, ADAM_EPS, ADAM_WD, ADAM_STEP = 0.001, 0.9, 0.999, 1e-08, 0.01, 10

VMEM_V7X = 64 * 1024 * 1024
VMEM_PLAN_MAX = VMEM_V7X - 8 * 1024 * 1024

TM = 512
TQ = 256

W_NAMES = ("w_in", "w_proj_a", "w_proj_b", "w_out", "w_ffn_gate", "w_ffn_up", "w_ffn_down")
TRANSPOSED = ("w_in", "w_ffn_gate", "w_ffn_up")
IN_SHARD = IN_COLS // N_SHARD
IN_SHARD_PAD = 1504
SHARD_SHAPE = dict(w_in=(IN_SHARD_PAD, D), w_proj_a=(DIL_W, D // N_SHARD), w_proj_b=(FOX_W, D // N_SHARD),
                   w_out=(D // N_SHARD, D), w_ffn_gate=(F_FF // N_SHARD, D), w_ffn_up=(F_FF // N_SHARD, D),
                   w_ffn_down=(F_FF // N_SHARD, D))
SMALL_ROWS = 8
LOSS_ROW = 5


def _nbytes(shape, dtype):
    return math.prod(shape) * jnp.dtype(dtype).itemsize


def _params(semantics, block_bytes, temp_bytes=0):
    need = 2 * block_bytes + temp_bytes + (2 << 20)
    return pltpu.CompilerParams(dimension_semantics=semantics, vmem_limit_bytes=int(min(need, VMEM_PLAN_MAX)))


def _row(w, tm=TM):
    return pl.BlockSpec((tm, w), lambda i: (i, 0))


def _vec(w):
    return pl.BlockSpec((1, w), lambda i: (0, 0))


def _mm(pairs, dims, out_dtype, *, tm, tn, name, m_inner=False):
    a0, b0 = pairs[0]
    m_dim = a0.shape[1] if dims == "tn" else a0.shape[0]
    n_dim = b0.shape[0] if dims == "nt" else b0.shape[1]
    contract = {"nn": ((1,), (0,)), "nt": ((1,), (1,)), "tn": ((0,), (0,))}[dims]
    n_pairs = len(pairs)
    assert m_dim % tm == 0 and n_dim % tn == 0, (name, m_dim, n_dim, tm, tn)

    def body(*refs):
        o_ref = refs[-1]
        acc = None
        for p in range(n_pairs):
            a = refs[2 * p][...].astype(bf16)
            b = refs[2 * p + 1][...].astype(bf16)
            t = lax.dot_general(a, b, (contract, ((), ())), preferred_element_type=f32)
            acc = t if acc is None else acc + t
        o_ref[...] = acc.astype(o_ref.dtype)

    if m_inner:
        grid = (n_dim // tn, m_dim // tm)
        mi = lambda j, i: i
        ni = lambda j, i: j
    else:
        grid = (m_dim // tm, n_dim // tn)
        mi = lambda i, j: i
        ni = lambda i, j: j
    in_specs, block_bytes, args = [], 0, []
    for a, b in pairs:
        k_dim = a.shape[0] if dims == "tn" else a.shape[1]
        if dims == "tn":
            in_specs.append(pl.BlockSpec((k_dim, tm), lambda *g: (0, mi(*g))))
        else:
            in_specs.append(pl.BlockSpec((tm, k_dim), lambda *g: (mi(*g), 0)))
        if dims == "nt":
            in_specs.append(pl.BlockSpec((tn, k_dim), lambda *g: (ni(*g), 0)))
        else:
            in_specs.append(pl.BlockSpec((k_dim, tn), lambda *g: (0, ni(*g))))
        block_bytes += _nbytes((tm, k_dim), a.dtype) + _nbytes((tn, k_dim), b.dtype)
        args += [a, b]
    block_bytes += _nbytes((tm, tn), out_dtype)
    temp = _nbytes((tm, tn), f32) * 2 + sum(_nbytes((tm, a.shape[0] if dims == "tn" else a.shape[1]), bf16)
                                            + _nbytes((tn, a.shape[0] if dims == "tn" else a.shape[1]), bf16)
                                            for a, _ in pairs)
    return pl.pallas_call(
        body, grid=grid, in_specs=in_specs,
        out_specs=pl.BlockSpec((tm, tn), lambda *g: (mi(*g), ni(*g))),
        out_shape=SDS((m_dim, n_dim), out_dtype), name=name,
        compiler_params=_params(("parallel", "parallel"), block_bytes, temp),
    )(*args)


def _rms(x, g):
    r = lax.rsqrt(jnp.mean(x * x, axis=-1, keepdims=True) + EPS)
    return x * r * g


def _rms_bwd(x, g, dy):
    r = lax.rsqrt(jnp.mean(x * x, axis=-1, keepdims=True) + EPS)
    xh = x * r
    dxh = dy * g
    dx = r * (dxh - xh * jnp.mean(dxh * xh, axis=-1, keepdims=True))
    return dx, jnp.sum(dy * xh, axis=0, keepdims=True)


def _acc_rows(ref, val):
    @pl.when(pl.program_id(0) == 0)
    def _():
        ref[...] = jnp.zeros_like(ref)
    ref[...] += val


def _norm_fwd(xs, g):
    n = len(xs)

    def body(*refs):
        g = refs[n][...]
        for x_ref, h_ref in zip(refs[:n], refs[n + 1:]):
            h_ref[...] = _rms(x_ref[...], g).astype(bf16)

    return pl.pallas_call(
        body, grid=(S // TM,), in_specs=[_row(D)] * n + [_vec(D)], out_specs=[_row(D)] * n,
        out_shape=[SDS((S, D), bf16)] * n, name="norm_mix_pre",
        compiler_params=_params(("parallel",), 6 * n * TM * D, 8 * n * TM * D))(*xs, g)


def _perm_rows(xs, ds, name):
    n = len(xs)

    def body(*refs):
        outs = iter(refs[n:])
        for x_ref in refs[:n]:
            for d in ds:
                o_ref, rows = next(outs), S // d
                for r in range(d):
                    o_ref[r * rows:(r + 1) * rows, :] = x_ref[pl.ds(r, rows, stride=d), :]

    blk = pl.BlockSpec((S, 128), lambda c: (0, c))
    w = xs[0].shape[1]
    return pl.pallas_call(
        body, grid=(w // 128,), in_specs=[blk] * n, out_specs=[blk] * (n * len(ds)),
        out_shape=[SDS((S, w), f32)] * (n * len(ds)), name=name,
        compiler_params=_params(("parallel",), 4 * S * 128 * n * (1 + len(ds))))(*xs)


def _unperm_sum(nat, perms, ds, name):
    n = len(perms)

    def body(*refs):
        a_ref, o_ref, sc = refs[0], refs[n + 1], refs[n + 2]
        acc = a_ref[...]
        for b_ref, d in zip(refs[1:n + 1], ds):
            rows = S // d
            for r in range(d):
                sc[pl.ds(r, rows, stride=d), :] = b_ref[r * rows:(r + 1) * rows, :]
            acc = acc + sc[...]
        o_ref[...] = acc

    blk = pl.BlockSpec((S, 128), lambda c: (0, c))
    w = nat.shape[1]
    return pl.pallas_call(
        body, grid=(w // 128,), in_specs=[blk] * (n + 1), out_specs=blk, out_shape=SDS((S, w), f32),
        scratch_shapes=[pltpu.VMEM((S, 128), f32)], name=name,
        compiler_params=_params(("parallel",), 4 * S * 128 * (n + 2), 8 * S * 128))(nat, *perms)


def _whole(a):
    return pl.BlockSpec(a.shape, lambda i: (0,) * a.ndim)


def _resid_norm_fwd(x, merged, w_out, g_post, g_pre):
    def body(x_ref, mg_ref, w_ref, gp_ref, gn_ref, mix_ref, x2_ref, h_ref):
        mix = jnp.dot(mg_ref[...], w_ref[...], preferred_element_type=f32)
        x2 = x_ref[...] + _rms(mix, gp_ref[...])
        mix_ref[...] = mix
        x2_ref[...] = x2
        h_ref[...] = _rms(x2, gn_ref[...]).astype(bf16)

    return pl.pallas_call(
        body, grid=(S // TM,), in_specs=[_row(D), _row(D), _whole(w_out), _vec(D), _vec(D)], out_specs=[_row(D)] * 3,
        out_shape=[SDS((S, D), f32), SDS((S, D), f32), SDS((S, D), bf16)], name="proj_out_norm",
        compiler_params=_params(("parallel",), 16 * TM * D + 2 * D * D, 16 * TM * D))(x, merged, w_out, g_post, g_pre)


def _loss_head(x2, a_act, w_down, g_post, target):
    def body(x2_ref, a_ref, w_ref, g_ref, t_ref, loss_ref, dy_ref, dff_ref, dg_ref):
        ff = jnp.dot(a_ref[...], w_ref[...], preferred_element_type=f32)
        g = g_ref[...]
        err = x2_ref[...] + _rms(ff, g) - t_ref[...]
        dy = err * (1.0 / D)
        dff, dg = _rms_bwd(ff, g, dy)
        dy_ref[...] = dy
        dff_ref[...] = dff.astype(bf16)
        _acc_rows(dg_ref, dg)
        _acc_rows(loss_ref, jnp.full((1, 128), jnp.sum(err * err), f32))

    return pl.pallas_call(
        body, grid=(S // TM,), in_specs=[_row(D), _row(F_FF), _whole(w_down), _vec(D), _row(D)],
        out_specs=[_vec(128), _row(D), _row(D), _vec(D)],
        out_shape=[SDS((1, 128), f32), SDS((S, D), f32), SDS((S, D), bf16), SDS((1, D), f32)], name="ffn_down_loss",
        compiler_params=_params(("arbitrary",), 14 * TM * D + 2 * TM * F_FF + 2 * F_FF * D, 28 * TM * D),
    )(x2, a_act, w_down, g_post, target)


def _norm_bwd_mid(dy, d_g, d_u, w_gate_t, w_up_t, x2, mix, g_ffn_pre, g_mix_post):
    def body(dy_ref, dgt_ref, dut_ref, wg_ref, wu_ref, x2_ref, mix_ref, g3_ref, g2_ref, dx2_ref, dmix_ref, dg3_ref, dg2_ref):
        dh = jnp.dot(dgt_ref[...], wg_ref[...], preferred_element_type=f32)
        dh += jnp.dot(dut_ref[...], wu_ref[...], preferred_element_type=f32)
        d3, dg3 = _rms_bwd(x2_ref[...], g3_ref[...], dh)
        dx2 = dy_ref[...] + d3
        dmix, dg2 = _rms_bwd(mix_ref[...], g2_ref[...], dx2)
        dx2_ref[...] = dx2
        dmix_ref[...] = dmix.astype(bf16)
        _acc_rows(dg3_ref, dg3)
        _acc_rows(dg2_ref, dg2)

    tm = TM // 2
    row = lambda w: _row(w, tm)
    return pl.pallas_call(
        body, grid=(S // tm,),
        in_specs=[row(D), row(F_FF), row(F_FF), _whole(w_gate_t), _whole(w_up_t), row(D), row(D), _vec(D), _vec(D)],
        out_specs=[row(D), row(D), _vec(D), _vec(D)],
        out_shape=[SDS((S, D), f32), SDS((S, D), bf16), SDS((1, D), f32), SDS((1, D), f32)], name="ffn_bwd_in_norm",
        compiler_params=_params(("arbitrary",), 18 * tm * D + 4 * tm * F_FF + 4 * F_FF * D, 28 * tm * D),
    )(dy, d_g, d_u, w_gate_t, w_up_t, x2, mix, g_ffn_pre, g_mix_post)


def _norm_bwd_in(dx2, dh1, x, g):
    def body(dx2_ref, dh_ref, x_ref, g_ref, gx_ref, dg_ref):
        d1, dg = _rms_bwd(x_ref[...], g_ref[...], dh_ref[...])
        gx_ref[...] = dx2_ref[...] + d1
        _acc_rows(dg_ref, dg)

    return pl.pallas_call(
        body, grid=(S // TM,), in_specs=[_row(D)] * 3 + [_vec(D)], out_specs=[_row(D), _vec(D)],
        out_shape=[SDS((S, D), f32), SDS((1, D), f32)], name="norm_bwd_in",
        compiler_params=_params(("arbitrary",), 16 * TM * D, 16 * TM * D))(dx2, dh1, x, g)


def _rope_tables():
    half = ROPE_DIM // 2
    inv_freq = np.power(np.float32(ROPE_THETA), -np.arange(0, ROPE_DIM, 2, dtype=np.float32) / np.float32(ROPE_DIM))
    row = np.arange(S)
    groups = []
    for _, d in DIL_GROUPS:
        pos = ((row % (S // d)) * d + row // (S // d)).astype(np.float32)
        ang = pos[:, None] * inv_freq[None, :].astype(np.float32)
        cos, sin = np.cos(ang).astype(np.float32), np.sin(ang).astype(np.float32)
        c = np.concatenate([cos, cos, np.ones((S, HD - ROPE_DIM), np.float32)], axis=1)
        s_lo = np.concatenate([-sin, np.zeros((S, HD - half), np.float32)], axis=1)
        s_hi = np.concatenate([np.zeros((S, half), np.float32), sin, np.zeros((S, HD - ROPE_DIM), np.float32)], axis=1)
        groups.append(np.stack([np.concatenate([t, t], axis=1) for t in (c, s_lo, s_hi)]))
    return jnp.asarray(np.stack(groups))


def _rotate(x, c, lo, hi, sign):
    tile = lambda t: jnp.tile(t, (1, DIL_W // 128))
    return (x * tile(c) + pltpu.roll(x, DIL_W - ROPE_DIM // 2, 1) * (tile(lo) * sign)
            + pltpu.roll(x, ROPE_DIM // 2, 1) * (tile(hi) * sign))


def _table_specs(g):
    return [pl.BlockSpec((None, None, TM, 128), lambda i, k=k: (g, k, i, 0)) for k in range(3)]


def _rope_fwd(g, p_qkv, tables):
    def body(x_ref, c_ref, lo_ref, hi_ref, o_ref):
        c, lo, hi = c_ref[...], lo_ref[...], hi_ref[...]
        for part in range(2):
            cols = slice(part * DIL_W, (part + 1) * DIL_W)
            o_ref[:, cols] = _rotate(x_ref[:, cols], c, lo, hi, 1.0).astype(bf16)
        o_ref[:, 2 * DIL_W:] = x_ref[:, 2 * DIL_W:].astype(bf16)

    return pl.pallas_call(
        body, grid=(S // TM,), in_specs=[_row(QKV_W)] + _table_specs(g), out_specs=_row(QKV_W),
        out_shape=SDS((S, QKV_W), bf16), name=f"rope_fwd_{g}",
        compiler_params=_params(("parallel",), 6 * TM * QKV_W + 12 * TM * 128, 24 * TM * QKV_W))(p_qkv, tables, tables, tables)


def _rope_bwd(g, dq, dk, dv, tables):
    def body(dq_ref, dk_ref, dv_ref, c_ref, lo_ref, hi_ref, o_ref):
        c, lo, hi = c_ref[...], lo_ref[...], hi_ref[...]
        o_ref[:, :DIL_W] = _rotate(dq_ref[...], c, lo, hi, -1.0).astype(bf16)
        o_ref[:, DIL_W:2 * DIL_W] = _rotate(dk_ref[...], c, lo, hi, -1.0).astype(bf16)
        o_ref[:, 2 * DIL_W:] = dv_ref[...].astype(bf16)

    return pl.pallas_call(
        body, grid=(S // TM,), in_specs=[_row(DIL_W)] * 3 + _table_specs(g), out_specs=_row(QKV_W),
        out_shape=SDS((S, QKV_W), bf16), name=f"rope_bwd_{g}",
        compiler_params=_params(("parallel",), 6 * TM * QKV_W + 12 * TM * 128, 24 * TM * QKV_W))(dq, dk, dv, tables, tables, tables)


def _nt(a, b):
    return lax.dot_general(a, b, (((1,), (1,)), ((), ())), preferred_element_type=f32)


def _tn(a, b):
    return lax.dot_general(a, b, (((0,), (0,)), ((), ())), preferred_element_type=f32)


STEP_BLOCKS = 4
STEP_ROWS = STEP_BLOCKS * BLK


def _dil_prev(g, b):
    _, d = DIL_GROUPS[g]
    nb = S // d // BLK
    if nb == 1 or (b == 0 and nb <= STEP_BLOCKS):
        return None
    return "in" if b > 0 else "halo"


def _bnt(a, b):
    return lax.dot_general(a, b, (((2,), (2,)), ((0,), (0,))), preferred_element_type=f32)


def _bnn(a, b):
    return lax.dot_general(a, b, (((2,), (1,)), ((0,), (0,))), preferred_element_type=f32)


def _btn(a, b):
    return lax.dot_general(a, b, (((1,), (1,)), ((0,), (0,))), preferred_element_type=f32)


def _on_tail(x, tail, fn):
    if tail == x.shape[0]:
        return fn(x)
    return jnp.concatenate([x[:-tail], fn(x[-tail:])], axis=0)


def _heads(ref, part):
    n = ref.shape[0] // BLK
    return jnp.stack([ref[b * BLK:(b + 1) * BLK, part * DIL_W + h * HD:part * DIL_W + (h + 1) * HD]
                      for b in range(n) for h in range(SLOTS)])


def _dil_operands(g, qkv_ref, halo_ref):
    q, kc, vc = (_heads(qkv_ref, part) for part in range(3))
    qi = lax.broadcasted_iota(jnp.int32, (1, BLK, BLK), 1)
    kj = lax.broadcasted_iota(jnp.int32, (1, BLK, BLK), 2)
    with_prev = [b for b in range(STEP_BLOCKS) if _dil_prev(g, b) is not None]
    tail = SLOTS * len(with_prev)
    if not tail:
        return q, kc, vc, None, None, kj <= qi, None, 0
    assert with_prev == list(range(STEP_BLOCKS - len(with_prev), STEP_BLOCKS))
    inside = SLOTS * sum(_dil_prev(g, b) == "in" for b in with_prev)
    kp, vp, prev = kc[:inside], vc[:inside], jnp.broadcast_to(kj >= qi, (inside, BLK, BLK))
    if inside < tail:
        no_halo = jnp.where(pl.program_id(0) == 0, BLK + 1, 0)
        kp = jnp.concatenate([_heads(halo_ref, 1), kp], axis=0)
        vp = jnp.concatenate([_heads(halo_ref, 2), vp], axis=0)
        prev = jnp.concatenate([jnp.broadcast_to(kj >= qi + no_halo, (SLOTS, BLK, BLK)), prev], axis=0)
    return q, kc, vc, kp, vp, kj <= qi, prev, tail


def _dil_in_specs(g, n_aux):
    step = lambda w: pl.BlockSpec((STEP_ROWS, w), lambda i: (i, 0))
    halo = [pl.BlockSpec((BLK, QKV_W), lambda i: (jnp.maximum(i * STEP_BLOCKS - 1, 0), 0))]
    needs_halo = _dil_prev(g, 0) == "halo"
    return [step(QKV_W)] + (halo if needs_halo else []) + [step(DIL_W)] * n_aux, needs_halo


def _dil_fwd(g, qkv):
    in_specs, needs_halo = _dil_in_specs(g, 0)

    def body(*refs):
        qkv_ref, halo_ref = refs[0], refs[1] if needs_halo else None
        o_ref, lse_ref = refs[-2:]
        q, kc, vc, kp, vp, cur, prev, tail = _dil_operands(g, qkv_ref, halo_ref)
        sc = jnp.where(cur, _bnt(q, kc) * SCALE, NEG)
        m = jnp.max(sc, axis=-1, keepdims=True)
        if tail:
            sp = jnp.where(prev, _bnt(q[-tail:], kp) * SCALE, NEG)
            m = _on_tail(m, tail, lambda t: jnp.maximum(t, jnp.max(sp, axis=-1, keepdims=True)))
            pp = jnp.exp(sp - m[-tail:])
        pc = jnp.exp(sc - m)
        den = jnp.sum(pc, axis=-1, keepdims=True)
        if tail:
            den = _on_tail(den, tail, lambda t: t + jnp.sum(pp, axis=-1, keepdims=True))
        inv = 1.0 / den
        o = _bnn((pc * inv).astype(bf16), vc)
        if tail:
            o = _on_tail(o, tail, lambda t: t + _bnn((pp * inv[-tail:]).astype(bf16), vp))
        lse = m + jnp.log(den)
        for b in range(STEP_BLOCKS):
            for h in range(SLOTS):
                rows, hs = slice(b * BLK, (b + 1) * BLK), slice(h * HD, (h + 1) * HD)
                o_ref[rows, hs] = o[SLOTS * b + h]
                lse_ref[rows, hs] = jnp.broadcast_to(lse[SLOTS * b + h], (BLK, HD))

    out = pl.BlockSpec((STEP_ROWS, DIL_W), lambda i: (i, 0))
    return pl.pallas_call(
        body, grid=(S // STEP_ROWS,), in_specs=in_specs, out_specs=[out, out], out_shape=[SDS((S, DIL_W), f32)] * 2,
        name=f"dil_fwd_{g}", compiler_params=_params(("parallel",), 12 * STEP_ROWS * DIL_W, 2 << 20),
    )(*([qkv] * (2 if needs_halo else 1)))


def _dil_combine(outs, lses):
    def body(o0, o1, o2, l0, l1, l2, out_ref, lse_ref, so1, so2, sl1, sl2):
        for (_, d), src, dst in ((DIL_GROUPS[1], o1, so1), (DIL_GROUPS[2], o2, so2),
                                 (DIL_GROUPS[1], l1, sl1), (DIL_GROUPS[2], l2, sl2)):
            rows = S // d
            for r in range(d):
                dst[pl.ds(r, rows, stride=d), :] = src[r * rows:(r + 1) * rows, :]
        a, b, c = l0[...], sl1[...], sl2[...]
        m = jnp.maximum(jnp.maximum(a, b), c)
        ea, eb, ec = jnp.exp(a - m), jnp.exp(b - m), jnp.exp(c - m)
        z = ea + eb + ec
        inv = 1.0 / z
        out_ref[...] = (ea * inv) * o0[...] + (eb * inv) * so1[...] + (ec * inv) * so2[...]
        lse_ref[...] = m + jnp.log(z)

    blk = pl.BlockSpec((S, 128), lambda c: (0, c))
    return pl.pallas_call(
        body, grid=(DIL_W // 128,), in_specs=[blk] * 6, out_specs=[blk] * 2,
        out_shape=[SDS((S, DIL_W), f32)] * 2, scratch_shapes=[pltpu.VMEM((S, 128), f32)] * 4, name="dil_combine",
        compiler_params=_params(("parallel",), 32 * S * 128, 32 * S * 128))(*outs, *lses)


def _dil_bwd(g, qkv, d_out, delta, lse):
    in_specs, needs_halo = _dil_in_specs(g, 3)

    def body(*refs):
        qkv_ref, halo_ref = refs[0], refs[1] if needs_halo else None
        do_ref, dl_ref, lse_ref, dq_ref, dk_ref, dv_ref = refs[-6:]
        q, kc, vc, kp, vp, cur, prev, tail = _dil_operands(g, qkv_ref, halo_ref)
        tiles = [(slice(b * BLK, (b + 1) * BLK), h) for b in range(STEP_BLOCKS) for h in range(SLOTS)]
        do = jnp.stack([do_ref[rows, h * HD:(h + 1) * HD] for rows, h in tiles]).astype(bf16)
        lse = jnp.stack([lse_ref[rows, h * HD:h * HD + 1] for rows, h in tiles])
        delta = jnp.stack([dl_ref[rows, h * HD:h * HD + 1] for rows, h in tiles])

        def probs(q, k, mask, lse, do, v, delta):
            p = jnp.exp(jnp.where(mask, _bnt(q, k) * SCALE, NEG) - lse)
            ds = p * (_bnt(do, v) - delta) * SCALE
            return p.astype(bf16), ds.astype(bf16)

        p, ds = probs(q, kc, cur, lse, do, vc, delta)
        dq, dk, dv = _bnn(ds, kc), _btn(ds, q), _btn(p, do)
        if tail:
            p, ds = probs(q[-tail:], kp, prev, lse[-tail:], do[-tail:], vp, delta[-tail:])
            dq = _on_tail(dq, tail, lambda t: t + _bnn(ds, kp))
            dk_p, dv_p = _btn(ds, q[-tail:]), _btn(p, do[-tail:])
            inside = tail - SLOTS if needs_halo else tail
            pad = jnp.zeros((len(tiles) - inside, BLK, HD), f32)
            dk = dk + jnp.concatenate([dk_p[tail - inside:], pad], axis=0)
            dv = dv + jnp.concatenate([dv_p[tail - inside:], pad], axis=0)
        first = pl.multiple_of(pl.program_id(0) * STEP_ROWS, STEP_ROWS)
        for t, (rows, h) in enumerate(tiles):
            hs = slice(h * HD, (h + 1) * HD)
            own = pl.ds(pl.multiple_of(first + rows.start, BLK), BLK)
            dq_ref[rows, hs] = dq[t]
            dk_ref[own, hs] = dk[t]
            dv_ref[own, hs] = dv[t]
        if needs_halo:
            before = pl.ds(pl.multiple_of(jnp.maximum(first - BLK, 0), BLK), BLK)
            for h in range(SLOTS):
                hs = slice(h * HD, (h + 1) * HD)
                dk_ref[before, hs] += dk_p[h]
                dv_ref[before, hs] += dv_p[h]

    whole = pl.BlockSpec((S, DIL_W), lambda i: (0, 0))
    return pl.pallas_call(
        body, grid=(S // STEP_ROWS,), in_specs=in_specs,
        out_specs=[pl.BlockSpec((STEP_ROWS, DIL_W), lambda i: (i, 0)), whole, whole],
        out_shape=[SDS((S, DIL_W), f32)] * 3, name=f"dil_bwd_{g}",
        compiler_params=_params(("arbitrary",), 20 * STEP_ROWS * DIL_W + 8 * S * DIL_W, 2 << 20),
    )(*([qkv] * (2 if needs_halo else 1)), d_out, delta, lse)


def _scan_rows(x, reverse):
    row = lax.broadcasted_iota(jnp.int32, x.shape, 0)
    k = 1
    while k < S:
        if reverse:
            x = x + jnp.where(row < S - k, pltpu.roll(x, S - k, 0), 0.0)
        else:
            x = x + jnp.where(row >= k, pltpu.roll(x, k, 0), 0.0)
        k *= 2
    return x


N_PAIR = N_FOX // 2
_PAIR_Q = pl.BlockSpec((None, S, 128), lambda p: (p, 0, 0))
_PAIR_K = pl.BlockSpec((None, 8, S), lambda p: (p, 0, 0))


def _forget_fwd(fz, b128):
    def body(z_ref, b_ref, fq_ref, fk_ref):
        z = z_ref[...] + b_ref[...]
        logf = jnp.minimum(z, 0.0) - jnp.log1p(jnp.exp(-jnp.abs(z)))
        f_cum = _scan_rows(logf, reverse=False)
        f_cum_t = f_cum.T
        fq_ref[...] = jnp.zeros_like(fq_ref)
        fk_ref[...] = jnp.zeros_like(fk_ref)
        for p in range(N_PAIR):
            fq_ref[p, :, 0:2] = f_cum[:, 2 * p:2 * p + 2]
            fk_ref[p, 0:2, :] = f_cum_t[2 * p:2 * p + 2, :]

    return pl.pallas_call(
        body, grid=(1,), in_specs=[pl.BlockSpec((S, 128), lambda i: (0, 0)), _vec(128)],
        out_specs=[pl.BlockSpec((N_PAIR, S, 128), lambda i: (0, 0, 0)), pl.BlockSpec((N_PAIR, 8, S), lambda i: (0, 0, 0))],
        out_shape=[SDS((N_PAIR, S, 128), f32), SDS((N_PAIR, 8, S), f32)], name="forget_fwd",
        compiler_params=_params(("arbitrary",), 24 * S * 128, 24 * S * 128))(fz, b128)


def _forget_bwd(fz, b128, d_f_cols, d_f_rows):
    def body(z_ref, b_ref, dfc_ref, dfr_ref, dz_ref, db_ref, df_sc):
        z = z_ref[...] + b_ref[...]
        df_sc[...] = jnp.zeros_like(df_sc)
        for p in range(N_PAIR):
            df_sc[:, 2 * p:2 * p + 2] = dfr_ref[p, :, 0:2] + dfc_ref[p].T[:, 0:2]
        dz = _scan_rows(df_sc[...], reverse=True) * jax.nn.sigmoid(-z)
        dz_ref[...] = dz
        db_ref[...] = jnp.sum(dz, axis=0, keepdims=True)

    full = pl.BlockSpec((S, 128), lambda i: (0, 0))
    return pl.pallas_call(
        body, grid=(1,),
        in_specs=[full, _vec(128), pl.BlockSpec((N_PAIR, 8, S), lambda i: (0, 0, 0)), pl.BlockSpec((N_PAIR, S, 128), lambda i: (0, 0, 0))],
        out_specs=[full, _vec(128)], out_shape=[SDS((S, 128), f32), SDS((1, 128), f32)],
        scratch_shapes=[pltpu.VMEM((S, 128), f32)], name="forget_bwd",
        compiler_params=_params(("arbitrary",), 32 * S * 128, 24 * S * 128))(fz, b128, d_f_cols, d_f_rows)


def _fox_scores(q_ref, k_ref, fq_ref, fk_ref, qi, hh):
    n = (qi + 1) * TQ
    rows, hs = slice(qi * TQ, n), slice(hh * HD, (hh + 1) * HD)
    q = q_ref[rows, hs] * SCALE
    s = _nt(q, k_ref[0:n, hs]) + (fq_ref[rows, hh:hh + 1] - fk_ref[hh:hh + 1, 0:n])
    below = lax.broadcasted_iota(jnp.int32, (TQ, TQ), 1) <= lax.broadcasted_iota(jnp.int32, (TQ, TQ), 0)
    diag = jnp.where(below, s[:, n - TQ:], NEG)
    return diag if qi == 0 else jnp.concatenate([s[:, :n - TQ], diag], axis=1)


def _pair_cols(first):
    return pl.BlockSpec((S, 128), lambda p: (0, first + p))


def _fox_fwd(vr, fq, fk):
    def body(q_ref, k_ref, v_ref, fq_ref, fk_ref, o_ref, lse_ref):
        lse_ref[...] = jnp.zeros_like(lse_ref)
        for hh in range(2):
            hs = slice(hh * HD, (hh + 1) * HD)
            for qi in range(S // TQ):
                n = (qi + 1) * TQ
                rows = slice(qi * TQ, n)
                s = _fox_scores(q_ref, k_ref, fq_ref, fk_ref, qi, hh)
                m = jnp.max(s, axis=-1, keepdims=True)
                p = jnp.exp(s - m)
                den = jnp.sum(p, axis=-1, keepdims=True)
                o_ref[rows, hs] = jnp.dot((p * (1.0 / den)).astype(bf16), v_ref[0:n, hs], preferred_element_type=f32)
                lse_ref[rows, hh:hh + 1] = m + jnp.log(den)

    return pl.pallas_call(
        body, grid=(N_PAIR,), in_specs=[_pair_cols(0), _pair_cols(N_PAIR), _pair_cols(2 * N_PAIR), _PAIR_Q, _PAIR_K],
        out_specs=[_pair_cols(0), _PAIR_Q], out_shape=[SDS((S, FOX_W), f32), SDS((N_PAIR, S, 128), f32)],
        name="fox_fwd", compiler_params=_params(("parallel",), 12 * S * 128, 16 * TQ * S),
    )(vr, vr, vr, fq, fk)


def _fox_bwd(vr, fq, fk, lse, d_out, delta):
    def body(q_ref, k_ref, v_ref, do_ref, fq_ref, fk_ref, lse_ref, dl_ref, dq_ref, dk_ref, dv_ref, dfc_ref, dfr_ref,
             dk_sc, dv_sc):
        dfc_ref[...] = jnp.zeros_like(dfc_ref)
        dfr_ref[...] = jnp.zeros_like(dfr_ref)
        for hh in range(2):
            hs = slice(hh * HD, (hh + 1) * HD)
            dk_sc[...] = jnp.zeros_like(dk_sc)
            dv_sc[...] = jnp.zeros_like(dv_sc)
            for qi in range(S // TQ):
                n = (qi + 1) * TQ
                rows = slice(qi * TQ, n)
                q, do, k, v = q_ref[rows, hs], do_ref[rows, hs], k_ref[0:n, hs], v_ref[0:n, hs]
                p = jnp.exp(_fox_scores(q_ref, k_ref, fq_ref, fk_ref, qi, hh) - lse_ref[rows, hh:hh + 1])
                ds = p * (_nt(do, v) - dl_ref[rows, hh:hh + 1])
                dsb = ds.astype(bf16)
                dq_ref[rows, hs] = jnp.dot(dsb, k, preferred_element_type=f32) * SCALE
                dk_sc[0:n, :] += _tn(dsb, q) * SCALE
                dv_sc[0:n, :] += _tn(p.astype(bf16), do)
                dfc_ref[hh:hh + 1, 0:n] -= jnp.sum(ds, axis=0, keepdims=True)
                dfr_ref[rows, hh:hh + 1] = jnp.sum(ds, axis=-1, keepdims=True)
            dk_ref[:, hs] = dk_sc[...]
            dv_ref[:, hs] = dv_sc[...]

    cols = [_pair_cols(k * N_PAIR) for k in range(3)]
    return pl.pallas_call(
        body, grid=(N_PAIR,), in_specs=cols + [_pair_cols(0), _PAIR_Q, _PAIR_K, _PAIR_Q, _PAIR_Q],
        out_specs=[_pair_cols(0)] * 3 + [_PAIR_K, _PAIR_Q],
        out_shape=[SDS((S, FOX_W), f32)] * 3 + [SDS((N_PAIR, 8, S), f32), SDS((N_PAIR, S, 128), f32)],
        scratch_shapes=[pltpu.VMEM((S, HD), f32)] * 2, name="fox_bwd",
        compiler_params=_params(("parallel",), 32 * S * 128, 24 * TQ * S),
    )(vr, vr, vr, d_out, fq, fk, lse, delta)


def _merge_fwd(out_a, out_b, w_a, w_b, gf):
    cw = D // N_SHARD

    def body(oa_ref, ob_ref, wa_ref, wb_ref, ga_ref, gb_ref, ya_ref, yb_ref, mg_ref):
        oa, ob = oa_ref[...].astype(bf16), ob_ref[...].astype(bf16)
        for j in range(N_SHARD):
            cols = slice(j * cw, (j + 1) * cw)
            ya = jnp.dot(oa, wa_ref[j], preferred_element_type=f32)
            yb = jnp.dot(ob, wb_ref[j], preferred_element_type=f32)
            ya_ref[:, cols] = ya
            yb_ref[:, cols] = yb
            mg_ref[:, cols] = (jax.nn.sigmoid(ga_ref[:, cols]) * ya + jax.nn.sigmoid(gb_ref[:, cols]) * yb).astype(bf16)

    full = lambda a: pl.BlockSpec(a.shape, lambda i: (0, 0, 0))
    return pl.pallas_call(
        body, grid=(S // TM,),
        in_specs=[_row(DIL_W), _row(FOX_W), full(w_a), full(w_b), _row(D), pl.BlockSpec((TM, D), lambda i: (i, 1))],
        out_specs=[_row(D)] * 3, out_shape=[SDS((S, D), f32), SDS((S, D), f32), SDS((S, D), bf16)], name="merge_fwd",
        compiler_params=_params(("parallel",), 22 * TM * D + 2 * (DIL_W + FOX_W) * D, 16 * TM * D),
    )(out_a, out_b, w_a, w_b, gf, gf)


def _merge_bwd(d_mix, w_out, ya, yb, gf):
    def body(dx_ref, w_ref, ya_ref, yb_ref, ga_ref, gb_ref, dya_ref, dyb_ref, dg_ref):
        dm = _nt(dx_ref[...], w_ref[...])
        sa, sb = jax.nn.sigmoid(ga_ref[...]), jax.nn.sigmoid(gb_ref[...])
        dya_ref[...] = (dm * sa).astype(bf16)
        dyb_ref[...] = (dm * sb).astype(bf16)
        dg_ref[:, :D] = (dm * ya_ref[...] * sa * (1.0 - sa)).astype(bf16)
        dg_ref[:, D:] = (dm * yb_ref[...] * sb * (1.0 - sb)).astype(bf16)

    return pl.pallas_call(
        body, grid=(S // TM,),
        in_specs=[_row(D), _whole(w_out)] + [_row(D)] * 3 + [pl.BlockSpec((TM, D), lambda i: (i, 1))],
        out_specs=[_row(D), _row(D), _row(2 * D)],
        out_shape=[SDS((S, D), bf16), SDS((S, D), bf16), SDS((S, 2 * D), bf16)], name="proj_out_bwd_merge",
        compiler_params=_params(("parallel",), 26 * TM * D + 2 * D * D, 28 * TM * D))(d_mix, w_out, ya, yb, gf, gf)


def _branch_bwd(d_ya, d_yb, w_a, w_b, out_a, out_b):
    cw = D // N_SHARD

    def body(dya_ref, dyb_ref, wa_ref, wb_ref, oa_ref, ob_ref, doa_ref, dla_ref, dob_ref, dlb_ref):
        doa = jnp.zeros((TM, DIL_W), f32)
        dob = jnp.zeros((TM, FOX_W), f32)
        for j in range(N_SHARD):
            cols = slice(j * cw, (j + 1) * cw)
            doa += _nt(dya_ref[:, cols], wa_ref[j])
            dob += _nt(dyb_ref[:, cols], wb_ref[j])
        doa_ref[...] = doa
        dob_ref[...] = dob.astype(bf16)
        prod_a = doa * oa_ref[...]
        for h in range(SLOTS):
            hs = slice(h * HD, (h + 1) * HD)
            dla_ref[:, hs] = jnp.broadcast_to(jnp.sum(prod_a[:, hs], axis=-1, keepdims=True), (TM, HD))
        prod_b = dob * ob_ref[...]
        dlb_ref[...] = jnp.zeros_like(dlb_ref)
        for h in range(N_FOX):
            dlb_ref[h // 2, :, h % 2:h % 2 + 1] = jnp.sum(prod_b[:, h * HD:(h + 1) * HD], axis=-1, keepdims=True)

    full = lambda a: pl.BlockSpec(a.shape, lambda i: (0, 0, 0))
    return pl.pallas_call(
        body, grid=(S // TM,),
        in_specs=[_row(D), _row(D), full(w_a), full(w_b), _row(DIL_W), _row(FOX_W)],
        out_specs=[_row(DIL_W), _row(DIL_W), _row(FOX_W), pl.BlockSpec((N_PAIR, TM, 128), lambda i: (0, i, 0))],
        out_shape=[SDS((S, DIL_W), f32), SDS((S, DIL_W), f32), SDS((S, FOX_W), bf16), SDS((N_PAIR, S, 128), f32)],
        name="branch_bwd", compiler_params=_params(("parallel",), 8 * TM * D + 2 * (DIL_W + FOX_W) * D, 8 * TM * D),
    )(d_ya, d_yb, w_a, w_b, out_a, out_b)


def _branch_grads(out_a, out_b, d_ya, d_yb):
    cw = D // N_SHARD

    def body(oa_ref, ob_ref, dya_ref, dyb_ref, ga_ref, gb_ref):
        ga_ref[...] = _tn(oa_ref[...].astype(bf16), dya_ref[...]).astype(bf16)
        gb_ref[...] = _tn(ob_ref[...].astype(bf16), dyb_ref[...]).astype(bf16)

    whole = lambda w: pl.BlockSpec((S, w), lambda j: (0, 0))
    cols = pl.BlockSpec((S, cw), lambda j: (0, j))
    return pl.pallas_call(
        body, grid=(N_SHARD,), in_specs=[whole(DIL_W), whole(FOX_W), cols, cols],
        out_specs=[pl.BlockSpec((None, DIL_W, cw), lambda j: (j, 0, 0)), pl.BlockSpec((None, FOX_W, cw), lambda j: (j, 0, 0))],
        out_shape=[SDS((N_SHARD, DIL_W, cw), bf16), SDS((N_SHARD, FOX_W, cw), bf16)], name="grad_w_proj_ab",
        compiler_params=_params(("parallel",), 4 * S * (DIL_W + FOX_W) + 4 * S * cw + 4 * (DIL_W + FOX_W) * cw,
                                4 * S * (DIL_W + FOX_W)))(out_a, out_b, d_ya, d_yb)


FF_TN = F_FF // 2
FF_TM = 1024


def _ffn_fwd(h, w_gate_t, w_up_t):
    def body(h_ref, wg_ref, wu_ref, g_ref, u_ref, a_ref):
        hb = h_ref[...]
        g = _nt(hb, wg_ref[...])
        u = _nt(hb, wu_ref[...])
        g_ref[...] = g
        u_ref[...] = u
        a_ref[...] = (g * jax.nn.sigmoid(g) * u).astype(bf16)

    tile = pl.BlockSpec((FF_TM, FF_TN), lambda j, i: (i, j))
    wspec = pl.BlockSpec((FF_TN, D), lambda j, i: (j, 0))
    return pl.pallas_call(
        body, grid=(F_FF // FF_TN, S // FF_TM),
        in_specs=[pl.BlockSpec((FF_TM, D), lambda j, i: (i, 0)), wspec, wspec], out_specs=[tile] * 3,
        out_shape=[SDS((S, F_FF), f32), SDS((S, F_FF), f32), SDS((S, F_FF), bf16)], name="ffn_fwd",
        compiler_params=_params(("parallel", "parallel"), 2 * FF_TM * D + 4 * D * FF_TN + 10 * FF_TM * FF_TN, 16 * FF_TM * FF_TN),
    )(h, w_gate_t, w_up_t)


def _ffn_bwd_act(d_ff, w_down, g_act, u_act):
    def body(d_ref, wd_ref, g_ref, u_ref, dg_ref, du_ref):
        da = _nt(d_ref[...], wd_ref[...])
        g = g_ref[...]
        sg = jax.nn.sigmoid(g)
        du_ref[...] = (da * g * sg).astype(bf16)
        dg_ref[...] = (da * u_ref[...] * sg * (1.0 + g * (1.0 - sg))).astype(bf16)

    tile = pl.BlockSpec((FF_TM, FF_TN), lambda j, i: (i, j))
    return pl.pallas_call(
        body, grid=(F_FF // FF_TN, S // FF_TM),
        in_specs=[pl.BlockSpec((FF_TM, D), lambda j, i: (i, 0)), pl.BlockSpec((FF_TN, D), lambda j, i: (j, 0)), tile, tile],
        out_specs=[tile, tile], out_shape=[SDS((S, F_FF), bf16)] * 2, name="ffn_bwd_act",
        compiler_params=_params(("parallel", "parallel"), 2 * FF_TM * D + 2 * D * FF_TN + 12 * FF_TM * FF_TN, 16 * FF_TM * FF_TN),
    )(d_ff, w_down, g_act, u_act)


def _row_tile(rows):
    return next(t for t in (376, 128, 176, 64, 32, 16, 8) if rows % t == 0)


def _adamw_math(w, g, m, v):
    c1 = 1.0 - ADAM_B1 ** ADAM_STEP
    c2 = 1.0 - ADAM_B2 ** ADAM_STEP
    m_new = ADAM_B1 * m + (1.0 - ADAM_B1) * g
    v_new = ADAM_B2 * v + (1.0 - ADAM_B2) * (g * g)
    return -ADAM_LR * ((m_new / c1) / (jnp.sqrt(v_new / c2) + ADAM_EPS) + ADAM_WD * w), m_new, v_new


def _adamw(w, g, m, v, name):
    rows, cols = w.shape
    tm = _row_tile(rows)

    def body(w_ref, g_ref, m_ref, v_ref, d_ref, nm_ref, nv_ref):
        d_ref[...], nm_ref[...], nv_ref[...] = _adamw_math(w_ref[...], g_ref[...], m_ref[...], v_ref[...])

    spec = pl.BlockSpec((tm, cols), lambda i: (i, 0))
    return pl.pallas_call(
        body, grid=(rows // tm,), in_specs=[spec] * 4, out_specs=[spec] * 3, out_shape=[SDS(w.shape, f32)] * 3,
        name=name, compiler_params=_params(("parallel",), 28 * tm * cols, 16 * tm * cols))(w, g, m, v)


def _adamw_halves(w, g_mine, g_theirs, m, v, name):
    cols = w.shape[1]
    tm = _row_tile(g_mine.shape[0])
    per_half = g_mine.shape[0] // tm
    assert 2 * g_mine.shape[0] - w.shape[0] < tm
    core = lax.axis_index("c").astype(jnp.int32).reshape(1)

    def body(c_ref, w_ref, gm_ref, gt_ref, m_ref, v_ref, g_ref, d_ref, nm_ref, nv_ref):
        mine = pl.program_id(0) // per_half == c_ref[0]
        g = jnp.where(mine, gm_ref[...], gt_ref[...])
        g_ref[...] = g
        d_ref[...], nm_ref[...], nv_ref[...] = _adamw_math(w_ref[...], g, m_ref[...], v_ref[...])

    spec = pl.BlockSpec((tm, cols), lambda i, c_ref: (i, 0))
    in_half = lambda i, first: jnp.clip(i - first * per_half, 0, per_half - 1)
    grid_spec = pltpu.PrefetchScalarGridSpec(
        num_scalar_prefetch=1, grid=(2 * per_half,),
        in_specs=[spec, pl.BlockSpec((tm, cols), lambda i, c_ref: (in_half(i, c_ref[0]), 0)),
                  pl.BlockSpec((tm, cols), lambda i, c_ref: (in_half(i, 1 - c_ref[0]), 0)), spec, spec],
        out_specs=[spec] * 4)
    return pl.pallas_call(
        body, grid_spec=grid_spec, out_shape=[SDS(w.shape, f32)] * 4, name=name,
        compiler_params=_params(("parallel",), 36 * tm * cols, 16 * tm * cols))(core, w, g_mine, g_theirs, m, v)


_ANY = pl.BlockSpec(memory_space=pl.ANY)


def _place():
    x, y, c = lax.axis_index("x"), lax.axis_index("y"), lax.axis_index("c")
    chips = [(1 - x, y), (x, 1 - y), (1 - x, 1 - y)]
    return x, y, c, chips


def _halved(t):
    return t.reshape(t.shape[:-2] + (2, t.shape[-2] // 2, t.shape[-1]))


def _gather_body(src, out, send_ici, recv_ici, send_d2d, recv_d2d):
    x, y, c, chips = _place()
    sibling = (x, y, 1 - c)
    me_j = 2 * x + y
    sends = []
    for a in range(len(src)):
        for p in range(3):
            cp = pltpu.make_async_remote_copy(
                src_ref=src[a].at[c], dst_ref=out[a].at[me_j, c], send_sem=send_ici.at[a, p],
                recv_sem=recv_ici.at[a, p], device_id=(*chips[p], c), device_id_type=MESH)
            cp.start()
            sends.append(cp)
    for a in range(len(src)):
        for p, (px, py) in enumerate(chips):
            blk = out[a].at[2 * px + py, c]
            pltpu.make_async_remote_copy(
                src_ref=blk, dst_ref=blk, send_sem=send_ici.at[a, p], recv_sem=recv_ici.at[a, p],
                device_id=sibling, device_id_type=MESH).wait_recv()
            fw = pltpu.make_async_remote_copy(
                src_ref=blk, dst_ref=blk, send_sem=send_d2d.at[a, p], recv_sem=recv_d2d.at[a, p],
                device_id=sibling, device_id_type=MESH)
            fw.start()
            sends.append(fw)
    for a in range(len(src)):
        for p, (px, py) in enumerate(chips):
            blk = out[a].at[2 * px + py, 1 - c]
            pltpu.make_async_remote_copy(
                src_ref=blk, dst_ref=blk, send_sem=send_d2d.at[a, p], recv_sem=recv_d2d.at[a, p],
                device_id=sibling, device_id_type=MESH).wait_recv()
    for cp in sends:
        cp.wait_send()


def _handshake(peers):
    barrier = pltpu.get_barrier_semaphore()
    for peer in peers:
        pl.semaphore_signal(barrier, inc=1, device_id=peer, device_id_type=MESH)
    pl.semaphore_wait(barrier, len(peers))


_SEQUENCER = dict(axis_name="sequencer", num_cores=1)
GATHER_LATE_ID, SCATTER_EARLY_ID, SWAP_EARLY_ID, GATHER_FIRST_ID, SCATTER_LATE_ID, SHARE_SMALL_ID = 1, 2, 3, 4, 5, 6


def _all_gather_async(shards, after, name, collective_id):
    n, k = len(shards), len(after)

    def body(*refs):
        x, y, c, chips = _place()
        _handshake([(*chip, c) for chip in chips] + [(x, y, 1 - c)])
        _gather_body(refs[:n], refs[n + k:2 * n + k], *refs[2 * n + k:])

    return pl.kernel(
        body, out_type=[SDS((N_SHARD,) + t.shape, t.dtype) for t in shards],
        mesh=plsc.ScalarSubcoreMesh(**_SEQUENCER), scratch_types=[pltpu.SemaphoreType.DMA((n, 3))] * 4,
        compiler_params=pltpu.CompilerParams(collective_id=collective_id), name=name)(*shards, *after)


def _pair_swap(grads):
    n = len(grads)

    def body(*refs):
        src, out, send_sems, recv_sems = refs[:n], refs[n:2 * n], refs[2 * n], refs[2 * n + 1]
        x, y, c, _ = _place()
        copies = [pltpu.make_async_remote_copy(
            src_ref=src[a].at[:, 1 - c], dst_ref=out[a], send_sem=send_sems.at[a], recv_sem=recv_sems.at[a],
            device_id=(x, y, 1 - c), device_id_type=MESH) for a in range(n)]
        for cp in copies:
            cp.start()
        for cp in copies:
            cp.wait()

    return pl.pallas_call(
        body, in_specs=[_ANY] * n, out_specs=[_ANY] * n,
        out_shape=[SDS((N_SHARD,) + t.shape[2:], t.dtype) for t in grads],
        scratch_shapes=[pltpu.SemaphoreType.DMA((n,)), pltpu.SemaphoreType.DMA((n,))], name="pair_swap",
        compiler_params=pltpu.CompilerParams(has_side_effects=True))(*grads)


def _pair_swap_early(grads):
    n = len(grads)

    def body(*refs):
        src, out, send_sems, recv_sems = refs[:n], refs[n:2 * n], refs[2 * n], refs[2 * n + 1]
        x, y, c, _ = _place()
        _handshake([(x, y, 1 - c)])
        copies = [pltpu.make_async_remote_copy(
            src_ref=src[a].at[:, 1 - c], dst_ref=out[a], send_sem=send_sems.at[a], recv_sem=recv_sems.at[a],
            device_id=(x, y, 1 - c), device_id_type=MESH) for a in range(n)]
        for cp in copies:
            cp.start()
        for cp in copies:
            cp.wait()

    return pl.kernel(
        body, out_type=[SDS((N_SHARD,) + t.shape[2:], t.dtype) for t in grads],
        mesh=plsc.ScalarSubcoreMesh(**_SEQUENCER), scratch_types=[pltpu.SemaphoreType.DMA((n,))] * 2,
        compiler_params=pltpu.CompilerParams(collective_id=SWAP_EARLY_ID), name="pair_swap_early")(*grads)


def _scatter_parts(parts, name, collective_id):
    n = len(parts)

    def body(*refs):
        part, recv, send_sems, recv_sems = refs[:n], refs[n:2 * n], refs[2 * n], refs[2 * n + 1]
        x, y, c, chips = _place()
        _handshake([(*chip, c) for chip in chips])
        me_j = 2 * x + y
        sends = []
        for a in range(n):
            for p, (px, py) in enumerate(chips):
                cp = pltpu.make_async_remote_copy(
                    src_ref=part[a].at[2 * px + py], dst_ref=recv[a].at[me_j], send_sem=send_sems.at[a, p],
                    recv_sem=recv_sems.at[a, p], device_id=(px, py, c), device_id_type=MESH)
                cp.start()
                sends.append(cp)
        for a in range(n):
            for p, (px, py) in enumerate(chips):
                slot = recv[a].at[2 * px + py]
                pltpu.make_async_remote_copy(
                    src_ref=slot, dst_ref=slot, send_sem=send_sems.at[a, p], recv_sem=recv_sems.at[a, p],
                    device_id=(px, py, c), device_id_type=MESH).wait_recv()
        for cp in sends:
            cp.wait_send()

    return pl.kernel(
        body, out_type=[SDS(t.shape, t.dtype) for t in parts],
        mesh=plsc.ScalarSubcoreMesh(**_SEQUENCER), scratch_types=[pltpu.SemaphoreType.DMA((n, 3))] * 2,
        compiler_params=pltpu.CompilerParams(collective_id=collective_id), name=name)(*parts)


def _pair_sum(grads, other, name):
    _, _, rows, cols = grads.shape
    tr = _row_tile(rows)
    core = lax.axis_index("c").astype(jnp.int32).reshape(1)

    def body(c_ref, g_ref, o_ref, out_ref):
        out_ref[...] = (g_ref[...].astype(f32) + o_ref[...].astype(f32)).astype(bf16)

    grid_spec = pltpu.PrefetchScalarGridSpec(
        num_scalar_prefetch=1, grid=(N_SHARD, rows // tr),
        in_specs=[pl.BlockSpec((None, None, tr, cols), lambda j, i, c_ref: (j, c_ref[0], i, 0)),
                  pl.BlockSpec((None, tr, cols), lambda j, i, c_ref: (j, i, 0))],
        out_specs=pl.BlockSpec((None, tr, cols), lambda j, i, c_ref: (j, i, 0)))
    return pl.pallas_call(
        body, grid_spec=grid_spec, out_shape=SDS((N_SHARD, rows, cols), bf16), name=name,
        compiler_params=_params(("parallel", "parallel"), 10 * tr * cols, 12 * tr * cols))(core, grads, other)


def _share_small(small):
    def body(small_ref, small_all_ref, ssend, srecv, local_sem):
        x, y, c, _ = _place()
        flip = lambda a, bit: 1 - a if bit else a
        peers = [(flip(x, k & 4), flip(y, k & 2), flip(c, k & 1)) for k in range(1, 8)]
        _handshake(peers)
        me_dev = 4 * x + 2 * y + c
        own = pltpu.make_async_copy(small_ref, small_all_ref.at[me_dev], local_sem)
        own.start()
        sends = []
        for k, to in enumerate(peers):
            cp = pltpu.make_async_remote_copy(
                src_ref=small_ref, dst_ref=small_all_ref.at[me_dev],
                send_sem=ssend.at[k], recv_sem=srecv.at[k], device_id=to, device_id_type=MESH)
            cp.start()
            sends.append(cp)
        for k, (px, py, pc) in enumerate(peers):
            slot = small_all_ref.at[4 * px + 2 * py + pc]
            pltpu.make_async_remote_copy(
                src_ref=slot, dst_ref=slot, send_sem=ssend.at[k], recv_sem=srecv.at[k],
                device_id=(px, py, pc), device_id_type=MESH).wait_recv()
        for cp in sends:
            cp.wait_send()
        own.wait()

    return pl.kernel(
        body, out_type=SDS((8, SMALL_ROWS, D), f32), mesh=plsc.ScalarSubcoreMesh(**_SEQUENCER),
        scratch_types=[pltpu.SemaphoreType.DMA((7,)), pltpu.SemaphoreType.DMA((7,)), pltpu.SemaphoreType.DMA],
        compiler_params=pltpu.CompilerParams(collective_id=SHARE_SMALL_ID), name="share_small")(small)


def _sum_partials(part, recv, name):
    _, rows, cols = recv.shape
    tr = _row_tile(rows)
    me = (2 * lax.axis_index("x") + lax.axis_index("y")).astype(jnp.int32).reshape(1)

    def body(me_ref, mine, r0, r1, r2, r3, out_ref):
        acc = None
        for j, r in enumerate((r0, r1, r2, r3)):
            term = jnp.where(me_ref[0] == j, mine[...], r[...]).astype(f32)
            acc = term if acc is None else acc + term
        out_ref[...] = acc

    slot = lambda j: pl.BlockSpec((None, tr, cols), lambda i, me_ref: (jnp.where(me_ref[0] == j, j ^ 1, j), i, 0))
    grid_spec = pltpu.PrefetchScalarGridSpec(
        num_scalar_prefetch=1, grid=(rows // tr,),
        in_specs=[pl.BlockSpec((None, tr, cols), lambda i, me_ref: (me_ref[0], i, 0)), slot(0), slot(1), slot(2), slot(3)],
        out_specs=pl.BlockSpec((tr, cols), lambda i, me_ref: (i, 0)))
    return pl.pallas_call(
        body, grid_spec=grid_spec, out_shape=SDS((rows, cols), f32), name=name,
        compiler_params=_params(("parallel",), 14 * tr * cols, 12 * tr * cols))(me, part, recv, recv, recv, recv)


def _sum_small(small_all):
    def body(small_ref, out_ref):
        tot = small_ref[0]
        for k in range(1, 8):
            tot = tot + small_ref[k]
        out_ref[...] = tot

    return pl.pallas_call(
        body, grid=(1,), in_specs=[pl.BlockSpec((8, SMALL_ROWS, D), lambda i: (0, 0, 0))],
        out_specs=pl.BlockSpec((SMALL_ROWS, D), lambda i: (0, 0)), out_shape=SDS((SMALL_ROWS, D), f32),
        name="sum_small", compiler_params=_params(("arbitrary",), 36 * SMALL_ROWS * D))(small_all)


def _swap_halves(halves, name):
    n = len(halves)

    def body(*refs):
        src, out, send_sems, recv_sems = refs[:n], refs[n:2 * n], refs[2 * n], refs[2 * n + 1]
        x, y, c, _ = _place()
        copies = [pltpu.make_async_remote_copy(
            src_ref=src[a], dst_ref=out[a], send_sem=send_sems.at[a], recv_sem=recv_sems.at[a],
            device_id=(x, y, 1 - c), device_id_type=MESH) for a in range(n)]
        for cp in copies:
            cp.start()
        for cp in copies:
            cp.wait()

    return pl.pallas_call(
        body, in_specs=[_ANY] * n, out_specs=[_ANY] * n, out_shape=[SDS(t.shape, f32) for t in halves],
        scratch_shapes=[pltpu.SemaphoreType.DMA((n,))] * 2, name=name,
        compiler_params=pltpu.CompilerParams(has_side_effects=True))(*halves)


def _kernel_layout(name, t):
    t = t[0]
    return jnp.swapaxes(t, 0, 1) if name in TRANSPOSED else t


def _harness_layout(name, t):
    if name in TRANSPOSED:
        t = jnp.swapaxes(t, 0, 1)
    return t[None]


def _pad_rows(t, rows):
    return t if t.shape[0] == rows else jnp.pad(t, ((0, rows - t.shape[0]), (0, 0)))


_QA, _KA, _VA, _QB, _F, _GAB = 0, 768, 1536, 2304, 3840, 3848


def _spans(a, b):
    return [(j, max(a, j * IN_SHARD) - j * IN_SHARD, max(a, j * IN_SHARD) - a,
             min(b, (j + 1) * IN_SHARD) - max(a, j * IN_SHARD))
            for j in range(N_SHARD) if max(a, j * IN_SHARD) < min(b, (j + 1) * IN_SHARD)]


_LANES = pl.BlockSpec((N_SHARD, IN_SHARD_PAD, 128), lambda c: (0, 0, c))


def _split_w_in(shards):
    group = [[(o + g * DIL_W, o + (g + 1) * DIL_W) for o in (_QA, _KA, _VA)] for g in range(3)]
    fox = [[(_QB + k * FOX_W, _QB + (k + 1) * FOX_W)] for k in range(3)]
    wanted = group + fox + [[(_QB, _F)], [(_F, _GAB)], [(_GAB, IN_COLS)]]
    rows = [sum(b - a for a, b in w) for w in wanted]
    rows[7] = 128

    def body(s_ref, *o_refs):
        for o_ref, want in zip(o_refs, wanted):
            at = 0
            for a, b in want:
                for j, src, off, n in _spans(a, b):
                    o_ref[at + off:at + off + n, :] = s_ref[j, src:src + n, :]
                at += b - a
        o_refs[7][N_FOX:, :] = jnp.zeros((128 - N_FOX, 128), bf16)

    return pl.pallas_call(
        body, grid=(D // 128,), in_specs=[_LANES], out_specs=[pl.BlockSpec((r, 128), lambda c: (0, c)) for r in rows],
        out_shape=[SDS((r, D), bf16) for r in rows], name="split_w_in",
        compiler_params=_params(("parallel",), 2 * 128 * (N_SHARD * IN_SHARD_PAD + sum(rows))))(shards)


def _join_w_in(g_a, g_fox, g_f, g_gab):
    parts = [(g_a[k], o, o + DIL_W) for o in (0, DIL_W, 2 * DIL_W) for k in range(3)]
    parts += [(t, 0, FOX_W) for t in g_fox] + [(g_f, 0, N_FOX), (g_gab, 0, 2 * D)]
    arrays = list(g_a) + list(g_fox) + [g_f, g_gab]
    index = {id(t): i for i, t in enumerate(arrays)}

    def body(*refs):
        o_ref = refs[-1]
        o_ref[:, IN_SHARD:, :] = jnp.zeros((N_SHARD, IN_SHARD_PAD - IN_SHARD, 128), bf16)
        at = 0
        for t, lo, hi in parts:
            src_ref = refs[index[id(t)]]
            for j, dst, off, n in _spans(at, at + hi - lo):
                o_ref[j, dst:dst + n, :] = src_ref[lo + off:lo + off + n, :].astype(bf16)
            at += hi - lo

    return pl.pallas_call(
        body, grid=(D // 128,), in_specs=[pl.BlockSpec((t.shape[0], 128), lambda c: (0, c)) for t in arrays],
        out_specs=_LANES, out_shape=SDS((N_SHARD, IN_SHARD_PAD, D), bf16), name="join_w_in",
        compiler_params=_params(("parallel",), 2 * 128 * (N_SHARD * IN_SHARD_PAD + sum(t.shape[0] for t in arrays))),
    )(*arrays)


def _full_weights(gathered):
    full = {n: t.reshape((N_SHARD,) + SHARD_SHAPE[n]) for n, t in gathered.items()}
    out = {}
    if "w_in" in full:
        pieces = _split_w_in(full["w_in"])
        out.update(w_a_t=pieces[0:3], w_fox_t=pieces[3:6], w_vr_t=pieces[6], w_f_t=pieces[7], w_gab_t=pieces[8])
    if "w_out" in full:
        out.update(
            w_a4=full["w_proj_a"],
            w_b4=full["w_proj_b"],
            w_out=full["w_out"].reshape(D, D),
            w_gate_t=full["w_ffn_gate"].reshape(F_FF, D),
            w_up_t=full["w_ffn_up"].reshape(F_FF, D),
            w_down=full["w_ffn_down"].reshape(F_FF, D))
    return out


def _sharded_grads(g):
    full = dict(w_in=_join_w_in(g["w_a_t"], g["w_fox_t"], g["w_f_t"], g["w_gab_t"]), w_proj_a=g["w_a4"],
                w_proj_b=g["w_b4"], w_out=g["w_out"], w_ffn_gate=g["w_gate_t"], w_ffn_up=g["w_up_t"],
                w_ffn_down=g["w_down"])
    return {n: _halved(full[n].reshape((N_SHARD,) + SHARD_SHAPE[n])) for n in W_NAMES}


def _local_step(x, target, wt, b_forget, g_mix_pre, g_mix_post, g_ffn_pre, g_ffn_post, late=None):
    tables = _rope_tables()
    b128 = jnp.pad(b_forget, ((0, 0), (0, 128 - N_FOX)))
    dils = tuple(d for _, d in DIL_GROUPS[1:])

    hs = _norm_fwd([x] + list(_perm_rows([x], dils, "perm_x")), g_mix_pre)
    h1 = hs[0]
    if callable(wt):
        wt = wt(h1)
    qkv = [_rope_fwd(g, _mm([(hs[g], wt["w_a_t"][g])], "nt", f32, tm=1024, tn=QKV_W, name=f"proj_a_{g}"), tables)
           for g in range(3)]
    vr = _mm([(h1, wt["w_vr_t"])], "nt", bf16, tm=1024, tn=VR_W // 2, name="proj_vr")
    gab = _mm([(h1, wt["w_gab_t"])], "nt", f32, tm=512, tn=2 * D, name="proj_gab")
    fz = _mm([(h1, wt["w_f_t"])], "nt", f32, tm=1024, tn=128, name="proj_f")
    dil = [_dil_fwd(g, qkv[g]) for g in range(3)]
    out_a, lse_a = _dil_combine([o for o, _ in dil], [l for _, l in dil])
    f_q, f_k = _forget_fwd(fz, b128)
    out_b, lse_b = _fox_fwd(vr, f_q, f_k)
    if late is not None:
        wt = {**wt, **late(out_b)}
    ya, yb, merged = _merge_fwd(out_a, out_b, wt["w_a4"], wt["w_b4"], gab)
    mix, x2, h3 = _resid_norm_fwd(x, merged, wt["w_out"], g_mix_post, g_ffn_pre)
    g_act, u_act, a_act = _ffn_fwd(h3, wt["w_gate_t"], wt["w_up_t"])
    sq_err, dy, d_ff, dg_ffn_post = _loss_head(x2, a_act, wt["w_down"], g_ffn_post, target)

    grads = {}
    d_g, d_u = _ffn_bwd_act(d_ff, wt["w_down"], g_act, u_act)
    grads["w_down"] = _mm([(a_act, d_ff)], "tn", bf16, tm=FF_TN, tn=D, name="grad_w_down")
    grads["w_gate_t"] = _mm([(d_g, h3)], "tn", bf16, tm=FF_TN, tn=D, name="grad_w_gate")
    grads["w_up_t"] = _mm([(d_u, h3)], "tn", bf16, tm=FF_TN, tn=D, name="grad_w_up")
    dx2, d_mix, dg_ffn_pre, dg_mix_post = _norm_bwd_mid(dy, d_g, d_u, wt["w_gate_t"], wt["w_up_t"], x2, mix,
                                                        g_ffn_pre, g_mix_post)

    grads["w_out"] = _mm([(merged, d_mix)], "tn", bf16, tm=D, tn=D, name="grad_w_out")
    d_ya, d_yb, d_gab = _merge_bwd(d_mix, wt["w_out"], ya, yb, gab)
    grads["w_a4"], grads["w_b4"] = _branch_grads(out_a, out_b, d_ya, d_yb)
    d_out_a, delta_a, d_out_b, delta_b = _branch_bwd(d_ya, d_yb, wt["w_a4"], wt["w_b4"], out_a, out_b)

    perm = _perm_rows([d_out_a, delta_a, lse_a], dils, "perm_dil_bwd")
    aux = [(d_out_a, delta_a, lse_a)] + [tuple(perm[k * len(dils) + i] for k in range(3)) for i in range(len(dils))]
    d_qkv = []
    for g in range(3):
        dq, dk, dv = _dil_bwd(g, qkv[g], *aux[g])
        d_qkv.append(_rope_bwd(g, dq, dk, dv, tables))
    *d_fox, d_f_cols, d_f_rows = _fox_bwd(vr, f_q, f_k, lse_b, d_out_b, delta_b)
    d_z, d_b128 = _forget_bwd(fz, b128, d_f_cols, d_f_rows)

    grads["w_a_t"] = [_mm([(d_qkv[g], hs[g])], "tn", bf16, tm=QKV_W, tn=D, name=f"grad_w_a_{g}") for g in range(3)]
    grads["w_fox_t"] = [_mm([(d_fox[k], h1)], "tn", bf16, tm=FOX_W, tn=D, name=f"grad_w_fox_{k}") for k in range(3)]
    grads["w_gab_t"] = _mm([(d_gab, h1)], "tn", bf16, tm=D, tn=D, name="grad_w_gab")
    grads["w_f_t"] = _mm([(d_z, h1)], "tn", bf16, tm=128, tn=D, name="grad_w_f")
    d_h1_nat = _mm([(d_qkv[0], wt["w_a_t"][0])] + list(zip(d_fox, wt["w_fox_t"]))
                   + [(d_gab, wt["w_gab_t"]), (d_z, wt["w_f_t"])], "nn", f32, tm=512, tn=D, name="proj_in_bwd")
    d_h1_dil = [_mm([(d_qkv[g], wt["w_a_t"][g])], "nn", f32, tm=1024, tn=D, name=f"proj_a_bwd_{g}") for g in (1, 2)]
    d_h1 = _unperm_sum(d_h1_nat, d_h1_dil, dils, "unperm_d_h1")
    grad_x, dg_mix_pre = _norm_bwd_in(dx2, d_h1, x, g_mix_pre)

    small = dict(b_forget=d_b128[:, :N_FOX], norm_mix_pre=dg_mix_pre, norm_mix_post=dg_mix_post,
                 norm_ffn_pre=dg_ffn_pre, norm_ffn_post=dg_ffn_post)
    grads["mid_backward"] = d_qkv[0]
    return sq_err, grad_x, grads, small


NORMS = ("norm_mix_pre", "norm_mix_post", "norm_ffn_pre", "norm_ffn_post")
ORDER = ("w_in", "w_proj_a", "w_proj_b", "w_out", "b_forget", "w_ffn_gate", "w_ffn_up", "w_ffn_down") + NORMS


def kernel(x, w_in, w_proj_a, w_proj_b, w_out, b_forget, w_ffn_gate, w_ffn_up, w_ffn_down, norm_mix_pre, norm_mix_post, norm_ffn_pre, norm_ffn_post, loss_target, m_w_in, m_w_proj_a, m_w_proj_b, m_w_out, m_b_forget, m_w_ffn_gate, m_w_ffn_up, m_w_ffn_down, m_norm_mix_pre, m_norm_mix_post, m_norm_ffn_pre, m_norm_ffn_post, v_w_in, v_w_proj_a, v_w_proj_b, v_w_out, v_b_forget, v_w_ffn_gate, v_w_ffn_up, v_w_ffn_down, v_norm_mix_pre, v_norm_mix_post, v_norm_ffn_pre, v_norm_ffn_post):
    given = dict(w_in=w_in, w_proj_a=w_proj_a, w_proj_b=w_proj_b, w_out=w_out, w_ffn_gate=w_ffn_gate,
                 w_ffn_up=w_ffn_up, w_ffn_down=w_ffn_down)
    given_m = dict(w_in=m_w_in, w_proj_a=m_w_proj_a, w_proj_b=m_w_proj_b, w_out=m_w_out, w_ffn_gate=m_w_ffn_gate,
                   w_ffn_up=m_w_ffn_up, w_ffn_down=m_w_ffn_down)
    given_v = dict(w_in=v_w_in, w_proj_a=v_w_proj_a, w_proj_b=v_w_proj_b, w_out=v_w_out, w_ffn_gate=v_w_ffn_gate,
                   w_ffn_up=v_w_ffn_up, w_ffn_down=v_w_ffn_down)
    w, m, v = ({n: _kernel_layout(n, t[n]) for n in W_NAMES} for t in (given, given_m, given_v))
    small_w = dict(b_forget=b_forget, norm_mix_pre=norm_mix_pre, norm_mix_post=norm_mix_post,
                   norm_ffn_pre=norm_ffn_pre, norm_ffn_post=norm_ffn_post)
    small_m = dict(b_forget=m_b_forget, norm_mix_pre=m_norm_mix_pre, norm_mix_post=m_norm_mix_post,
                   norm_ffn_pre=m_norm_ffn_pre, norm_ffn_post=m_norm_ffn_post)
    small_v = dict(b_forget=v_b_forget, norm_mix_pre=v_norm_mix_pre, norm_mix_post=v_norm_mix_post,
                   norm_ffn_pre=v_norm_ffn_pre, norm_ffn_post=v_norm_ffn_post)

    own = [_halved(_pad_rows(w[n].astype(bf16), SHARD_SHAPE[n][0])) for n in W_NAMES]
    chip = 2 * lax.axis_index("x") + lax.axis_index("y")
    exchanged = {"first": _all_gather_async(own[:1], [], "all_gather_first", GATHER_FIRST_ID)}
    fill = lambda ts, mine: [lax.dynamic_update_index_in_dim(t, o, chip, 0) for t, o in zip(ts, mine)]

    def first_weights(ready):
        arrived, _ = lax.optimization_barrier((list(exchanged["first"]), ready))
        exchanged["late"] = _all_gather_async(own[1:], [arrived[0][0, 0, :16, :128]], "all_gather_late", GATHER_LATE_ID)
        return _full_weights(dict(zip(W_NAMES[:1], fill(arrived, own[:1]))))

    def late_weights(ready):
        arrived, _ = lax.optimization_barrier((list(exchanged["late"]), ready))
        return _full_weights(dict(zip(W_NAMES[1:], fill(arrived, own[1:]))))

    x0, m["w_in"], v["w_in"] = lax.optimization_barrier((x[0], m["w_in"], v["w_in"]))
    sq_err, grad_x, grads, small = _local_step(x0, loss_target[0], first_weights, b_forget, norm_mix_pre,
                                               norm_mix_post, norm_ffn_pre, norm_ffn_post, late=late_weights)

    g4 = _sharded_grads(grads)
    stack = lambda t, extra: jnp.concatenate(
        [jnp.pad(t["b_forget"], ((0, 0), (0, D - N_FOX)))] + [t[n] for n in NORMS]
        + [jnp.pad(extra, ((0, SMALL_ROWS - LOSS_ROW - 1), (0, D - extra.shape[1])), constant_values=1.0)], axis=0)
    early, _ = lax.optimization_barrier((list(_pair_swap_early([g4[n] for n in W_NAMES[1:]])), grads["mid_backward"]))
    other = list(_pair_swap([g4["w_in"]])) + early
    parts = [_pair_sum(g4[n], o, "pair_sum_" + n) for n, o in zip(W_NAMES, other)]
    recv_early = _scatter_parts(parts[1:], "scatter_early", SCATTER_EARLY_ID)
    recv_in = _scatter_parts(parts[:1], "scatter_partials", SCATTER_LATE_ID)
    small_all = _share_small(stack(small, sq_err))

    g_shard, delta, new_m, new_v = {}, {}, {}, {}

    def summed(names, parts, recv):
        halves = [_sum_partials(p, r, "sum_partials_" + n) for n, p, r in zip(names, parts, recv)]
        return halves, list(_swap_halves(halves, "swap_halves_" + names[0]))

    def update(names, halves, theirs):
        for n, mine, other_half in zip(names, halves, theirs):
            g_shard[n], delta[n], new_m[n], new_v[n] = _adamw_halves(w[n], mine, other_half, m[n], v[n], "adamw_" + n)

    recv_early, _ = lax.optimization_barrier((list(recv_early), parts[0]))
    early_mine, early_theirs = summed(W_NAMES[1:], parts[1:], recv_early)
    recv_in, _ = lax.optimization_barrier((list(recv_in), early_theirs))
    update(W_NAMES[:1], *summed(W_NAMES[:1], parts[:1], recv_in))
    (early_theirs, small_all), _ = lax.optimization_barrier(((early_theirs, small_all), delta["w_in"]))
    update(W_NAMES[1:], early_mine, early_theirs)
    small_sum = _sum_small(small_all)
    loss = small_sum[LOSS_ROW, 0] * (0.5 / D)
    ones = jnp.ones((1, 128), f32)
    sd, sm, sv = _adamw(stack(small_w, ones), small_sum, stack(small_m, ones), stack(small_v, ones), "adamw_small")

    outs = [loss, grad_x[None]]
    for big, st in ((g_shard, small_sum), (delta, sd), (new_m, sm), (new_v, sv)):
        t = {n: _harness_layout(n, big[n]) for n in W_NAMES}
        t["b_forget"] = st[0:1, :N_FOX]
        for i, n in enumerate(NORMS):
            t[n] = st[i + 1:i + 2]
        outs += [t[n] for n in ORDER]
    return tuple(outs)
```

```python
import functools
import math

import jax
import jax.numpy as jnp
import numpy as np
from jax import lax
from jax.experimental import pallas as pl
from jax.experimental.pallas import tpu as pltpu
from jax.experimental.pallas import tpu_sc as plsc

f32 = jnp.float32
bf16 = jnp.bfloat16
SDS = jax.ShapeDtypeStruct
MESH = pl.DeviceIdType.MESH

S = 2048
D = 1024
HD = 64
BLK = 128
N_FOX = 8
FOX_W = N_FOX * HD
DIL_GROUPS = ((128, 1), (512, 4), (2048, 16))
SLOTS = 4
DIL_W = SLOTS * HD
QKV_W = 3 * DIL_W
VR_W = 3 * FOX_W
GF_W = 2 * D + 128
F_FF = 2816
ROPE_DIM = 16
ROPE_THETA = 500000.0
EPS = 1e-6
NEG = -1e30
SCALE = 1.0 / math.sqrt(HD)
IN_COLS = 5896
N_SHARD = 4

ADAM_LR, ADAM_B1, ADAM_B2, ADAM_EPS, ADAM_WD, ADAM_STEP = 0.001, 0.9, 0.999, 1e-08, 0.01, 10

VMEM_V7X = 64 * 1024 * 1024
VMEM_PLAN_MAX = VMEM_V7X - 8 * 1024 * 1024

TM = 512
TQ = 256

W_NAMES = ("w_in", "w_proj_a", "w_proj_b", "w_out", "w_ffn_gate", "w_ffn_up", "w_ffn_down")
TRANSPOSED = ("w_in", "w_ffn_gate", "w_ffn_up")
IN_SHARD = IN_COLS // N_SHARD
IN_SHARD_PAD = 1504
SHARD_SHAPE = dict(w_in=(IN_SHARD_PAD, D), w_proj_a=(DIL_W, D // N_SHARD), w_proj_b=(FOX_W, D // N_SHARD),
                   w_out=(D // N_SHARD, D), w_ffn_gate=(F_FF // N_SHARD, D), w_ffn_up=(F_FF // N_SHARD, D),
                   w_ffn_down=(F_FF // N_SHARD, D))
SMALL_ROWS = 8
LOSS_ROW = 5


def _nbytes(shape, dtype):
    return math.prod(shape) * jnp.dtype(dtype).itemsize


def _params(semantics, block_bytes, temp_bytes=0):
    need = 2 * block_bytes + temp_bytes + (2 << 20)
    return pltpu.CompilerParams(dimension_semantics=semantics, vmem_limit_bytes=int(min(need, VMEM_PLAN_MAX)))


def _row(w, tm=TM):
    return pl.BlockSpec((tm, w), lambda i: (i, 0))


def _vec(w):
    return pl.BlockSpec((1, w), lambda i: (0, 0))


def _mm(pairs, dims, out_dtype, *, tm, tn, name, m_inner=False):
    a0, b0 = pairs[0]
    m_dim = a0.shape[1] if dims == "tn" else a0.shape[0]
    n_dim = b0.shape[0] if dims == "nt" else b0.shape[1]
    contract = {"nn": ((1,), (0,)), "nt": ((1,), (1,)), "tn": ((0,), (0,))}[dims]
    n_pairs = len(pairs)
    assert m_dim % tm == 0 and n_dim % tn == 0, (name, m_dim, n_dim, tm, tn)

    def body(*refs):
        o_ref = refs[-1]
        acc = None
        for p in range(n_pairs):
            a = refs[2 * p][...].astype(bf16)
            b = refs[2 * p + 1][...].astype(bf16)
            t = lax.dot_general(a, b, (contract, ((), ())), preferred_element_type=f32)
            acc = t if acc is None else acc + t
        o_ref[...] = acc.astype(o_ref.dtype)

    if m_inner:
        grid = (n_dim // tn, m_dim // tm)
        mi = lambda j, i: i
        ni = lambda j, i: j
    else:
        grid = (m_dim // tm, n_dim // tn)
        mi = lambda i, j: i
        ni = lambda i, j: j
    in_specs, block_bytes, args = [], 0, []
    for a, b in pairs:
        k_dim = a.shape[0] if dims == "tn" else a.shape[1]
        if dims == "tn":
            in_specs.append(pl.BlockSpec((k_dim, tm), lambda *g: (0, mi(*g))))
        else:
            in_specs.append(pl.BlockSpec((tm, k_dim), lambda *g: (mi(*g), 0)))
        if dims == "nt":
            in_specs.append(pl.BlockSpec((tn, k_dim), lambda *g: (ni(*g), 0)))
        else:
            in_specs.append(pl.BlockSpec((k_dim, tn), lambda *g: (0, ni(*g))))
        block_bytes += _nbytes((tm, k_dim), a.dtype) + _nbytes((tn, k_dim), b.dtype)
        args += [a, b]
    block_bytes += _nbytes((tm, tn), out_dtype)
    temp = _nbytes((tm, tn), f32) * 2 + sum(_nbytes((tm, a.shape[0] if dims == "tn" else a.shape[1]), bf16)
                                            + _nbytes((tn, a.shape[0] if dims == "tn" else a.shape[1]), bf16)
                                            for a, _ in pairs)
    return pl.pallas_call(
        body, grid=grid, in_specs=in_specs,
        out_specs=pl.BlockSpec((tm, tn), lambda *g: (mi(*g), ni(*g))),
        out_shape=SDS((m_dim, n_dim), out_dtype), name=name,
        compiler_params=_params(("parallel", "parallel"), block_bytes, temp),
    )(*args)


def _rms(x, g):
    r = lax.rsqrt(jnp.mean(x * x, axis=-1, keepdims=True) + EPS)
    return x * r * g


def _rms_bwd(x, g, dy):
    r = lax.rsqrt(jnp.mean(x * x, axis=-1, keepdims=True) + EPS)
    xh = x * r
    dxh = dy * g
    dx = r * (dxh - xh * jnp.mean(dxh * xh, axis=-1, keepdims=True))
    return dx, jnp.sum(dy * xh, axis=0, keepdims=True)


def _acc_rows(ref, val):
    @pl.when(pl.program_id(0) == 0)
    def _():
        ref[...] = jnp.zeros_like(ref)
    ref[...] += val


def _norm_fwd(xs, g):
    n = len(xs)

    def body(*refs):
        g = refs[n][...]
        for x_ref, h_ref in zip(refs[:n], refs[n + 1:]):
            h_ref[...] = _rms(x_ref[...], g).astype(bf16)

    return pl.pallas_call(
        body, grid=(S // TM,), in_specs=[_row(D)] * n + [_vec(D)], out_specs=[_row(D)] * n,
        out_shape=[SDS((S, D), bf16)] * n, name="norm_mix_pre",
        compiler_params=_params(("parallel",), 6 * n * TM * D, 8 * n * TM * D))(*xs, g)


def _perm_rows(xs, ds, name):
    n = len(xs)

    def body(*refs):
        outs = iter(refs[n:])
        for x_ref in refs[:n]:
            for d in ds:
                o_ref, rows = next(outs), S // d
                for r in range(d):
                    o_ref[r * rows:(r + 1) * rows, :] = x_ref[pl.ds(r, rows, stride=d), :]

    blk = pl.BlockSpec((S, 128), lambda c: (0, c))
    w = xs[0].shape[1]
    return pl.pallas_call(
        body, grid=(w // 128,), in_specs=[blk] * n, out_specs=[blk] * (n * len(ds)),
        out_shape=[SDS((S, w), f32)] * (n * len(ds)), name=name,
        compiler_params=_params(("parallel",), 4 * S * 128 * n * (1 + len(ds))))(*xs)


def _unperm_sum(nat, perms, ds, name):
    n = len(perms)

    def body(*refs):
        a_ref, o_ref, sc = refs[0], refs[n + 1], refs[n + 2]
        acc = a_ref[...]
        for b_ref, d in zip(refs[1:n + 1], ds):
            rows = S // d
            for r in range(d):
                sc[pl.ds(r, rows, stride=d), :] = b_ref[r * rows:(r + 1) * rows, :]
            acc = acc + sc[...]
        o_ref[...] = acc

    blk = pl.BlockSpec((S, 128), lambda c: (0, c))
    w = nat.shape[1]
    return pl.pallas_call(
        body, grid=(w // 128,), in_specs=[blk] * (n + 1), out_specs=blk, out_shape=SDS((S, w), f32),
        scratch_shapes=[pltpu.VMEM((S, 128), f32)], name=name,
        compiler_params=_params(("parallel",), 4 * S * 128 * (n + 2), 8 * S * 128))(nat, *perms)


def _whole(a):
    return pl.BlockSpec(a.shape, lambda i: (0,) * a.ndim)


def _resid_norm_fwd(x, merged, w_out, g_post, g_pre):
    def body(x_ref, mg_ref, w_ref, gp_ref, gn_ref, mix_ref, x2_ref, h_ref):
        mix = jnp.dot(mg_ref[...], w_ref[...], preferred_element_type=f32)
        x2 = x_ref[...] + _rms(mix, gp_ref[...])
        mix_ref[...] = mix
        x2_ref[...] = x2
        h_ref[...] = _rms(x2, gn_ref[...]).astype(bf16)

    return pl.pallas_call(
        body, grid=(S // TM,), in_specs=[_row(D), _row(D), _whole(w_out), _vec(D), _vec(D)], out_specs=[_row(D)] * 3,
        out_shape=[SDS((S, D), f32), SDS((S, D), f32), SDS((S, D), bf16)], name="proj_out_norm",
        compiler_params=_params(("parallel",), 16 * TM * D + 2 * D * D, 16 * TM * D))(x, merged, w_out, g_post, g_pre)


def _loss_head(x2, a_act, w_down, g_post, target):
    def body(x2_ref, a_ref, w_ref, g_ref, t_ref, loss_ref, dy_ref, dff_ref, dg_ref):
        ff = jnp.dot(a_ref[...], w_ref[...], preferred_element_type=f32)
        g = g_ref[...]
        err = x2_ref[...] + _rms(ff, g) - t_ref[...]
        dy = err * (1.0 / D)
        dff, dg = _rms_bwd(ff, g, dy)
        dy_ref[...] = dy
        dff_ref[...] = dff.astype(bf16)
        _acc_rows(dg_ref, dg)
        _acc_rows(loss_ref, jnp.full((1, 128), jnp.sum(err * err), f32))

    return pl.pallas_call(
        body, grid=(S // TM,), in_specs=[_row(D), _row(F_FF), _whole(w_down), _vec(D), _row(D)],
        out_specs=[_vec(128), _row(D), _row(D), _vec(D)],
        out_shape=[SDS((1, 128), f32), SDS((S, D), f32), SDS((S, D), bf16), SDS((1, D), f32)], name="ffn_down_loss",
        compiler_params=_params(("arbitrary",), 14 * TM * D + 2 * TM * F_FF + 2 * F_FF * D, 28 * TM * D),
    )(x2, a_act, w_down, g_post, target)


def _norm_bwd_mid(dy, d_g, d_u, w_gate_t, w_up_t, x2, mix, g_ffn_pre, g_mix_post):
    def body(dy_ref, dgt_ref, dut_ref, wg_ref, wu_ref, x2_ref, mix_ref, g3_ref, g2_ref, dx2_ref, dmix_ref, dg3_ref, dg2_ref):
        dh = jnp.dot(dgt_ref[...], wg_ref[...], preferred_element_type=f32)
        dh += jnp.dot(dut_ref[...], wu_ref[...], preferred_element_type=f32)
        d3, dg3 = _rms_bwd(x2_ref[...], g3_ref[...], dh)
        dx2 = dy_ref[...] + d3
        dmix, dg2 = _rms_bwd(mix_ref[...], g2_ref[...], dx2)
        dx2_ref[...] = dx2
        dmix_ref[...] = dmix.astype(bf16)
        _acc_rows(dg3_ref, dg3)
        _acc_rows(dg2_ref, dg2)

    tm = TM // 2
    row = lambda w: _row(w, tm)
    return pl.pallas_call(
        body, grid=(S // tm,),
        in_specs=[row(D), row(F_FF), row(F_FF), _whole(w_gate_t), _whole(w_up_t), row(D), row(D), _vec(D), _vec(D)],
        out_specs=[row(D), row(D), _vec(D), _vec(D)],
        out_shape=[SDS((S, D), f32), SDS((S, D), bf16), SDS((1, D), f32), SDS((1, D), f32)], name="ffn_bwd_in_norm",
        compiler_params=_params(("arbitrary",), 18 * tm * D + 4 * tm * F_FF + 4 * F_FF * D, 28 * tm * D),
    )(dy, d_g, d_u, w_gate_t, w_up_t, x2, mix, g_ffn_pre, g_mix_post)


def _norm_bwd_in(dx2, dh1, x, g):
    def body(dx2_ref, dh_ref, x_ref, g_ref, gx_ref, dg_ref):
        d1, dg = _rms_bwd(x_ref[...], g_ref[...], dh_ref[...])
        gx_ref[...] = dx2_ref[...] + d1
        _acc_rows(dg_ref, dg)

    return pl.pallas_call(
        body, grid=(S // TM,), in_specs=[_row(D)] * 3 + [_vec(D)], out_specs=[_row(D), _vec(D)],
        out_shape=[SDS((S, D), f32), SDS((1, D), f32)], name="norm_bwd_in",
        compiler_params=_params(("arbitrary",), 16 * TM * D, 16 * TM * D))(dx2, dh1, x, g)


def _rope_tables():
    half = ROPE_DIM // 2
    inv_freq = np.power(np.float32(ROPE_THETA), -np.arange(0, ROPE_DIM, 2, dtype=np.float32) / np.float32(ROPE_DIM))
    row = np.arange(S)
    groups = []
    for _, d in DIL_GROUPS:
        pos = ((row % (S // d)) * d + row // (S // d)).astype(np.float32)
        ang = pos[:, None] * inv_freq[None, :].astype(np.float32)
        cos, sin = np.cos(ang).astype(np.float32), np.sin(ang).astype(np.float32)
        c = np.concatenate([cos, cos, np.ones((S, HD - ROPE_DIM), np.float32)], axis=1)
        s_lo = np.concatenate([-sin, np.zeros((S, HD - half), np.float32)], axis=1)
        s_hi = np.concatenate([np.zeros((S, half), np.float32), sin, np.zeros((S, HD - ROPE_DIM), np.float32)], axis=1)
        groups.append(np.stack([np.concatenate([t, t], axis=1) for t in (c, s_lo, s_hi)]))
    return jnp.asarray(np.stack(groups))


def _rotate(x, c, lo, hi, sign):
    tile = lambda t: jnp.tile(t, (1, DIL_W // 128))
    return (x * tile(c) + pltpu.roll(x, DIL_W - ROPE_DIM // 2, 1) * (tile(lo) * sign)
            + pltpu.roll(x, ROPE_DIM // 2, 1) * (tile(hi) * sign))


def _table_specs(g):
    return [pl.BlockSpec((None, None, TM, 128), lambda i, k=k: (g, k, i, 0)) for k in range(3)]


def _rope_fwd(g, p_qkv, tables):
    def body(x_ref, c_ref, lo_ref, hi_ref, o_ref):
        c, lo, hi = c_ref[...], lo_ref[...], hi_ref[...]
        for part in range(2):
            cols = slice(part * DIL_W, (part + 1) * DIL_W)
            o_ref[:, cols] = _rotate(x_ref[:, cols], c, lo, hi, 1.0).astype(bf16)
        o_ref[:, 2 * DIL_W:] = x_ref[:, 2 * DIL_W:].astype(bf16)

    return pl.pallas_call(
        body, grid=(S // TM,), in_specs=[_row(QKV_W)] + _table_specs(g), out_specs=_row(QKV_W),
        out_shape=SDS((S, QKV_W), bf16), name=f"rope_fwd_{g}",
        compiler_params=_params(("parallel",), 6 * TM * QKV_W + 12 * TM * 128, 24 * TM * QKV_W))(p_qkv, tables, tables, tables)


def _rope_bwd(g, dq, dk, dv, tables):
    def body(dq_ref, dk_ref, dv_ref, c_ref, lo_ref, hi_ref, o_ref):
        c, lo, hi = c_ref[...], lo_ref[...], hi_ref[...]
        o_ref[:, :DIL_W] = _rotate(dq_ref[...], c, lo, hi, -1.0).astype(bf16)
        o_ref[:, DIL_W:2 * DIL_W] = _rotate(dk_ref[...], c, lo, hi, -1.0).astype(bf16)
        o_ref[:, 2 * DIL_W:] = dv_ref[...].astype(bf16)

    return pl.pallas_call(
        body, grid=(S // TM,), in_specs=[_row(DIL_W)] * 3 + _table_specs(g), out_specs=_row(QKV_W),
        out_shape=SDS((S, QKV_W), bf16), name=f"rope_bwd_{g}",
        compiler_params=_params(("parallel",), 6 * TM * QKV_W + 12 * TM * 128, 24 * TM * QKV_W))(dq, dk, dv, tables, tables, tables)


def _nt(a, b):
    return lax.dot_general(a, b, (((1,), (1,)), ((), ())), preferred_element_type=f32)


def _tn(a, b):
    return lax.dot_general(a, b, (((0,), (0,)), ((), ())), preferred_element_type=f32)


STEP_BLOCKS = 4
STEP_ROWS = STEP_BLOCKS * BLK


def _dil_prev(g, b):
    _, d = DIL_GROUPS[g]
    nb = S // d // BLK
    if nb == 1 or (b == 0 and nb <= STEP_BLOCKS):
        return None
    return "in" if b > 0 else "halo"


def _bnt(a, b):
    return lax.dot_general(a, b, (((2,), (2,)), ((0,), (0,))), preferred_element_type=f32)


def _bnn(a, b):
    return lax.dot_general(a, b, (((2,), (1,)), ((0,), (0,))), preferred_element_type=f32)


def _btn(a, b):
    return lax.dot_general(a, b, (((1,), (1,)), ((0,), (0,))), preferred_element_type=f32)


def _on_tail(x, tail, fn):
    if tail == x.shape[0]:
        return fn(x)
    return jnp.concatenate([x[:-tail], fn(x[-tail:])], axis=0)


def _heads(ref, part):
    n = ref.shape[0] // BLK
    return jnp.stack([ref[b * BLK:(b + 1) * BLK, part * DIL_W + h * HD:part * DIL_W + (h + 1) * HD]
                      for b in range(n) for h in range(SLOTS)])


def _dil_operands(g, qkv_ref, halo_ref):
    q, kc, vc = (_heads(qkv_ref, part) for part in range(3))
    qi = lax.broadcasted_iota(jnp.int32, (1, BLK, BLK), 1)
    kj = lax.broadcasted_iota(jnp.int32, (1, BLK, BLK), 2)
    with_prev = [b for b in range(STEP_BLOCKS) if _dil_prev(g, b) is not None]
    tail = SLOTS * len(with_prev)
    if not tail:
        return q, kc, vc, None, None, kj <= qi, None, 0
    assert with_prev == list(range(STEP_BLOCKS - len(with_prev), STEP_BLOCKS))
    inside = SLOTS * sum(_dil_prev(g, b) == "in" for b in with_prev)
    kp, vp, prev = kc[:inside], vc[:inside], jnp.broadcast_to(kj >= qi, (inside, BLK, BLK))
    if inside < tail:
        no_halo = jnp.where(pl.program_id(0) == 0, BLK + 1, 0)
        kp = jnp.concatenate([_heads(halo_ref, 1), kp], axis=0)
        vp = jnp.concatenate([_heads(halo_ref, 2), vp], axis=0)
        prev = jnp.concatenate([jnp.broadcast_to(kj >= qi + no_halo, (SLOTS, BLK, BLK)), prev], axis=0)
    return q, kc, vc, kp, vp, kj <= qi, prev, tail


def _dil_in_specs(g, n_aux):
    step = lambda w: pl.BlockSpec((STEP_ROWS, w), lambda i: (i, 0))
    halo = [pl.BlockSpec((BLK, QKV_W), lambda i: (jnp.maximum(i * STEP_BLOCKS - 1, 0), 0))]
    needs_halo = _dil_prev(g, 0) == "halo"
    return [step(QKV_W)] + (halo if needs_halo else []) + [step(DIL_W)] * n_aux, needs_halo


def _dil_fwd(g, qkv):
    in_specs, needs_halo = _dil_in_specs(g, 0)

    def body(*refs):
        qkv_ref, halo_ref = refs[0], refs[1] if needs_halo else None
        o_ref, lse_ref = refs[-2:]
        q, kc, vc, kp, vp, cur, prev, tail = _dil_operands(g, qkv_ref, halo_ref)
        sc = jnp.where(cur, _bnt(q, kc) * SCALE, NEG)
        m = jnp.max(sc, axis=-1, keepdims=True)
        if tail:
            sp = jnp.where(prev, _bnt(q[-tail:], kp) * SCALE, NEG)
            m = _on_tail(m, tail, lambda t: jnp.maximum(t, jnp.max(sp, axis=-1, keepdims=True)))
            pp = jnp.exp(sp - m[-tail:])
        pc = jnp.exp(sc - m)
        den = jnp.sum(pc, axis=-1, keepdims=True)
        if tail:
            den = _on_tail(den, tail, lambda t: t + jnp.sum(pp, axis=-1, keepdims=True))
        inv = 1.0 / den
        o = _bnn((pc * inv).astype(bf16), vc)
        if tail:
            o = _on_tail(o, tail, lambda t: t + _bnn((pp * inv[-tail:]).astype(bf16), vp))
        lse = m + jnp.log(den)
        for b in range(STEP_BLOCKS):
            for h in range(SLOTS):
                rows, hs = slice(b * BLK, (b + 1) * BLK), slice(h * HD, (h + 1) * HD)
                o_ref[rows, hs] = o[SLOTS * b + h]
                lse_ref[rows, hs] = jnp.broadcast_to(lse[SLOTS * b + h], (BLK, HD))

    out = pl.BlockSpec((STEP_ROWS, DIL_W), lambda i: (i, 0))
    return pl.pallas_call(
        body, grid=(S // STEP_ROWS,), in_specs=in_specs, out_specs=[out, out], out_shape=[SDS((S, DIL_W), f32)] * 2,
        name=f"dil_fwd_{g}", compiler_params=_params(("parallel",), 12 * STEP_ROWS * DIL_W, 2 << 20),
    )(*([qkv] * (2 if needs_halo else 1)))


def _dil_combine(outs, lses):
    def body(o0, o1, o2, l0, l1, l2, out_ref, lse_ref, so1, so2, sl1, sl2):
        for (_, d), src, dst in ((DIL_GROUPS[1], o1, so1), (DIL_GROUPS[2], o2, so2),
                                 (DIL_GROUPS[1], l1, sl1), (DIL_GROUPS[2], l2, sl2)):
            rows = S // d
            for r in range(d):
                dst[pl.ds(r, rows, stride=d), :] = src[r * rows:(r + 1) * rows, :]
        a, b, c = l0[...], sl1[...], sl2[...]
        m = jnp.maximum(jnp.maximum(a, b), c)
        ea, eb, ec = jnp.exp(a - m), jnp.exp(b - m), jnp.exp(c - m)
        z = ea + eb + ec
        inv = 1.0 / z
        out_ref[...] = (ea * inv) * o0[...] + (eb * inv) * so1[...] + (ec * inv) * so2[...]
        lse_ref[...] = m + jnp.log(z)

    blk = pl.BlockSpec((S, 128), lambda c: (0, c))
    return pl.pallas_call(
        body, grid=(DIL_W // 128,), in_specs=[blk] * 6, out_specs=[blk] * 2,
        out_shape=[SDS((S, DIL_W), f32)] * 2, scratch_shapes=[pltpu.VMEM((S, 128), f32)] * 4, name="dil_combine",
        compiler_params=_params(("parallel",), 32 * S * 128, 32 * S * 128))(*outs, *lses)


def _dil_bwd(g, qkv, d_out, delta, lse):
    in_specs, needs_halo = _dil_in_specs(g, 3)

    def body(*refs):
        qkv_ref, halo_ref = refs[0], refs[1] if needs_halo else None
        do_ref, dl_ref, lse_ref, dq_ref, dk_ref, dv_ref = refs[-6:]
        q, kc, vc, kp, vp, cur, prev, tail = _dil_operands(g, qkv_ref, halo_ref)
        tiles = [(slice(b * BLK, (b + 1) * BLK), h) for b in range(STEP_BLOCKS) for h in range(SLOTS)]
        do = jnp.stack([do_ref[rows, h * HD:(h + 1) * HD] for rows, h in tiles]).astype(bf16)
        lse = jnp.stack([lse_ref[rows, h * HD:h * HD + 1] for rows, h in tiles])
        delta = jnp.stack([dl_ref[rows, h * HD:h * HD + 1] for rows, h in tiles])

        def probs(q, k, mask, lse, do, v, delta):
            p = jnp.exp(jnp.where(mask, _bnt(q, k) * SCALE, NEG) - lse)
            ds = p * (_bnt(do, v) - delta) * SCALE
            return p.astype(bf16), ds.astype(bf16)

        p, ds = probs(q, kc, cur, lse, do, vc, delta)
        dq, dk, dv = _bnn(ds, kc), _btn(ds, q), _btn(p, do)
        if tail:
            p, ds = probs(q[-tail:], kp, prev, lse[-tail:], do[-tail:], vp, delta[-tail:])
            dq = _on_tail(dq, tail, lambda t: t + _bnn(ds, kp))
            dk_p, dv_p = _btn(ds, q[-tail:]), _btn(p, do[-tail:])
            inside = tail - SLOTS if needs_halo else tail
            pad = jnp.zeros((len(tiles) - inside, BLK, HD), f32)
            dk = dk + jnp.concatenate([dk_p[tail - inside:], pad], axis=0)
            dv = dv + jnp.concatenate([dv_p[tail - inside:], pad], axis=0)
        first = pl.multiple_of(pl.program_id(0) * STEP_ROWS, STEP_ROWS)
        for t, (rows, h) in enumerate(tiles):
            hs = slice(h * HD, (h + 1) * HD)
            own = pl.ds(pl.multiple_of(first + rows.start, BLK), BLK)
            dq_ref[rows, hs] = dq[t]
            dk_ref[own, hs] = dk[t]
            dv_ref[own, hs] = dv[t]
        if needs_halo:
            before = pl.ds(pl.multiple_of(jnp.maximum(first - BLK, 0), BLK), BLK)
            for h in range(SLOTS):
                hs = slice(h * HD, (h + 1) * HD)
                dk_ref[before, hs] += dk_p[h]
                dv_ref[before, hs] += dv_p[h]

    whole = pl.BlockSpec((S, DIL_W), lambda i: (0, 0))
    return pl.pallas_call(
        body, grid=(S // STEP_ROWS,), in_specs=in_specs,
        out_specs=[pl.BlockSpec((STEP_ROWS, DIL_W), lambda i: (i, 0)), whole, whole],
        out_shape=[SDS((S, DIL_W), f32)] * 3, name=f"dil_bwd_{g}",
        compiler_params=_params(("arbitrary",), 20 * STEP_ROWS * DIL_W + 8 * S * DIL_W, 2 << 20),
    )(*([qkv] * (2 if needs_halo else 1)), d_out, delta, lse)


def _scan_rows(x, reverse):
    row = lax.broadcasted_iota(jnp.int32, x.shape, 0)
    k = 1
    while k < S:
        if reverse:
            x = x + jnp.where(row < S - k, pltpu.roll(x, S - k, 0), 0.0)
        else:
            x = x + jnp.where(row >= k, pltpu.roll(x, k, 0), 0.0)
        k *= 2
    return x


N_PAIR = N_FOX // 2
_PAIR_Q = pl.BlockSpec((None, S, 128), lambda p: (p, 0, 0))
_PAIR_K = pl.BlockSpec((None, 8, S), lambda p: (p, 0, 0))


def _forget_fwd(fz, b128):
    def body(z_ref, b_ref, fq_ref, fk_ref):
        z = z_ref[...] + b_ref[...]
        logf = jnp.minimum(z, 0.0) - jnp.log1p(jnp.exp(-jnp.abs(z)))
        f_cum = _scan_rows(logf, reverse=False)
        f_cum_t = f_cum.T
        fq_ref[...] = jnp.zeros_like(fq_ref)
        fk_ref[...] = jnp.zeros_like(fk_ref)
        for p in range(N_PAIR):
            fq_ref[p, :, 0:2] = f_cum[:, 2 * p:2 * p + 2]
            fk_ref[p, 0:2, :] = f_cum_t[2 * p:2 * p + 2, :]

    return pl.pallas_call(
        body, grid=(1,), in_specs=[pl.BlockSpec((S, 128), lambda i: (0, 0)), _vec(128)],
        out_specs=[pl.BlockSpec((N_PAIR, S, 128), lambda i: (0, 0, 0)), pl.BlockSpec((N_PAIR, 8, S), lambda i: (0, 0, 0))],
        out_shape=[SDS((N_PAIR, S, 128), f32), SDS((N_PAIR, 8, S), f32)], name="forget_fwd",
        compiler_params=_params(("arbitrary",), 24 * S * 128, 24 * S * 128))(fz, b128)


def _forget_bwd(fz, b128, d_f_cols, d_f_rows):
    def body(z_ref, b_ref, dfc_ref, dfr_ref, dz_ref, db_ref, df_sc):
        z = z_ref[...] + b_ref[...]
        df_sc[...] = jnp.zeros_like(df_sc)
        for p in range(N_PAIR):
            df_sc[:, 2 * p:2 * p + 2] = dfr_ref[p, :, 0:2] + dfc_ref[p].T[:, 0:2]
        dz = _scan_rows(df_sc[...], reverse=True) * jax.nn.sigmoid(-z)
        dz_ref[...] = dz
        db_ref[...] = jnp.sum(dz, axis=0, keepdims=True)

    full = pl.BlockSpec((S, 128), lambda i: (0, 0))
    return pl.pallas_call(
        body, grid=(1,),
        in_specs=[full, _vec(128), pl.BlockSpec((N_PAIR, 8, S), lambda i: (0, 0, 0)), pl.BlockSpec((N_PAIR, S, 128), lambda i: (0, 0, 0))],
        out_specs=[full, _vec(128)], out_shape=[SDS((S, 128), f32), SDS((1, 128), f32)],
        scratch_shapes=[pltpu.VMEM((S, 128), f32)], name="forget_bwd",
        compiler_params=_params(("arbitrary",), 32 * S * 128, 24 * S * 128))(fz, b128, d_f_cols, d_f_rows)


def _fox_scores(q_ref, k_ref, fq_ref, fk_ref, qi, hh):
    n = (qi + 1) * TQ
    rows, hs = slice(qi * TQ, n), slice(hh * HD, (hh + 1) * HD)
    q = q_ref[rows, hs] * SCALE
    s = _nt(q, k_ref[0:n, hs]) + (fq_ref[rows, hh:hh + 1] - fk_ref[hh:hh + 1, 0:n])
    below = lax.broadcasted_iota(jnp.int32, (TQ, TQ), 1) <= lax.broadcasted_iota(jnp.int32, (TQ, TQ), 0)
    diag = jnp.where(below, s[:, n - TQ:], NEG)
    return diag if qi == 0 else jnp.concatenate([s[:, :n - TQ], diag], axis=1)


def _pair_cols(first):
    return pl.BlockSpec((S, 128), lambda p: (0, first + p))


def _fox_fwd(vr, fq, fk):
    def body(q_ref, k_ref, v_ref, fq_ref, fk_ref, o_ref, lse_ref):
        lse_ref[...] = jnp.zeros_like(lse_ref)
        for hh in range(2):
            hs = slice(hh * HD, (hh + 1) * HD)
            for qi in range(S // TQ):
                n = (qi + 1) * TQ
                rows = slice(qi * TQ, n)
                s = _fox_scores(q_ref, k_ref, fq_ref, fk_ref, qi, hh)
                m = jnp.max(s, axis=-1, keepdims=True)
                p = jnp.exp(s - m)
                den = jnp.sum(p, axis=-1, keepdims=True)
                o_ref[rows, hs] = jnp.dot((p * (1.0 / den)).astype(bf16), v_ref[0:n, hs], preferred_element_type=f32)
                lse_ref[rows, hh:hh + 1] = m + jnp.log(den)

    return pl.pallas_call(
        body, grid=(N_PAIR,), in_specs=[_pair_cols(0), _pair_cols(N_PAIR), _pair_cols(2 * N_PAIR), _PAIR_Q, _PAIR_K],
        out_specs=[_pair_cols(0), _PAIR_Q], out_shape=[SDS((S, FOX_W), f32), SDS((N_PAIR, S, 128), f32)],
        name="fox_fwd", compiler_params=_params(("parallel",), 12 * S * 128, 16 * TQ * S),
    )(vr, vr, vr, fq, fk)


def _fox_bwd(vr, fq, fk, lse, d_out, delta):
    def body(q_ref, k_ref, v_ref, do_ref, fq_ref, fk_ref, lse_ref, dl_ref, dq_ref, dk_ref, dv_ref, dfc_ref, dfr_ref,
             dk_sc, dv_sc):
        dfc_ref[...] = jnp.zeros_like(dfc_ref)
        dfr_ref[...] = jnp.zeros_like(dfr_ref)
        for hh in range(2):
            hs = slice(hh * HD, (hh + 1) * HD)
            dk_sc[...] = jnp.zeros_like(dk_sc)
            dv_sc[...] = jnp.zeros_like(dv_sc)
            for qi in range(S // TQ):
                n = (qi + 1) * TQ
                rows = slice(qi * TQ, n)
                q, do, k, v = q_ref[rows, hs], do_ref[rows, hs], k_ref[0:n, hs], v_ref[0:n, hs]
                p = jnp.exp(_fox_scores(q_ref, k_ref, fq_ref, fk_ref, qi, hh) - lse_ref[rows, hh:hh + 1])
                ds = p * (_nt(do, v) - dl_ref[rows, hh:hh + 1])
                dsb = ds.astype(bf16)
                dq_ref[rows, hs] = jnp.dot(dsb, k, preferred_element_type=f32) * SCALE
                dk_sc[0:n, :] += _tn(dsb, q) * SCALE
                dv_sc[0:n, :] += _tn(p.astype(bf16), do)
                dfc_ref[hh:hh + 1, 0:n] -= jnp.sum(ds, axis=0, keepdims=True)
                dfr_ref[rows, hh:hh + 1] = jnp.sum(ds, axis=-1, keepdims=True)
            dk_ref[:, hs] = dk_sc[...]
            dv_ref[:, hs] = dv_sc[...]

    cols = [_pair_cols(k * N_PAIR) for k in range(3)]
    return pl.pallas_call(
        body, grid=(N_PAIR,), in_specs=cols + [_pair_cols(0), _PAIR_Q, _PAIR_K, _PAIR_Q, _PAIR_Q],
        out_specs=[_pair_cols(0)] * 3 + [_PAIR_K, _PAIR_Q],
        out_shape=[SDS((S, FOX_W), f32)] * 3 + [SDS((N_PAIR, 8, S), f32), SDS((N_PAIR, S, 128), f32)],
        scratch_shapes=[pltpu.VMEM((S, HD), f32)] * 2, name="fox_bwd",
        compiler_params=_params(("parallel",), 32 * S * 128, 24 * TQ * S),
    )(vr, vr, vr, d_out, fq, fk, lse, delta)


def _merge_fwd(out_a, out_b, w_a, w_b, gf):
    cw = D // N_SHARD

    def body(oa_ref, ob_ref, wa_ref, wb_ref, ga_ref, gb_ref, ya_ref, yb_ref, mg_ref):
        oa, ob = oa_ref[...].astype(bf16), ob_ref[...].astype(bf16)
        for j in range(N_SHARD):
            cols = slice(j * cw, (j + 1) * cw)
            ya = jnp.dot(oa, wa_ref[j], preferred_element_type=f32)
            yb = jnp.dot(ob, wb_ref[j], preferred_element_type=f32)
            ya_ref[:, cols] = ya
            yb_ref[:, cols] = yb
            mg_ref[:, cols] = (jax.nn.sigmoid(ga_ref[:, cols]) * ya + jax.nn.sigmoid(gb_ref[:, cols]) * yb).astype(bf16)

    full = lambda a: pl.BlockSpec(a.shape, lambda i: (0, 0, 0))
    return pl.pallas_call(
        body, grid=(S // TM,),
        in_specs=[_row(DIL_W), _row(FOX_W), full(w_a), full(w_b), _row(D), pl.BlockSpec((TM, D), lambda i: (i, 1))],
        out_specs=[_row(D)] * 3, out_shape=[SDS((S, D), f32), SDS((S, D), f32), SDS((S, D), bf16)], name="merge_fwd",
        compiler_params=_params(("parallel",), 22 * TM * D + 2 * (DIL_W + FOX_W) * D, 16 * TM * D),
    )(out_a, out_b, w_a, w_b, gf, gf)


def _merge_bwd(d_mix, w_out, ya, yb, gf):
    def body(dx_ref, w_ref, ya_ref, yb_ref, ga_ref, gb_ref, dya_ref, dyb_ref, dg_ref):
        dm = _nt(dx_ref[...], w_ref[...])
        sa, sb = jax.nn.sigmoid(ga_ref[...]), jax.nn.sigmoid(gb_ref[...])
        dya_ref[...] = (dm * sa).astype(bf16)
        dyb_ref[...] = (dm * sb).astype(bf16)
        dg_ref[:, :D] = (dm * ya_ref[...] * sa * (1.0 - sa)).astype(bf16)
        dg_ref[:, D:] = (dm * yb_ref[...] * sb * (1.0 - sb)).astype(bf16)

    return pl.pallas_call(
        body, grid=(S // TM,),
        in_specs=[_row(D), _whole(w_out)] + [_row(D)] * 3 + [pl.BlockSpec((TM, D), lambda i: (i, 1))],
        out_specs=[_row(D), _row(D), _row(2 * D)],
        out_shape=[SDS((S, D), bf16), SDS((S, D), bf16), SDS((S, 2 * D), bf16)], name="proj_out_bwd_merge",
        compiler_params=_params(("parallel",), 26 * TM * D + 2 * D * D, 28 * TM * D))(d_mix, w_out, ya, yb, gf, gf)


def _branch_bwd(d_ya, d_yb, w_a, w_b, out_a, out_b):
    cw = D // N_SHARD

    def body(dya_ref, dyb_ref, wa_ref, wb_ref, oa_ref, ob_ref, doa_ref, dla_ref, dob_ref, dlb_ref):
        doa = jnp.zeros((TM, DIL_W), f32)
        dob = jnp.zeros((TM, FOX_W), f32)
        for j in range(N_SHARD):
            cols = slice(j * cw, (j + 1) * cw)
            doa += _nt(dya_ref[:, cols], wa_ref[j])
            dob += _nt(dyb_ref[:, cols], wb_ref[j])
        doa_ref[...] = doa
        dob_ref[...] = dob.astype(bf16)
        prod_a = doa * oa_ref[...]
        for h in range(SLOTS):
            hs = slice(h * HD, (h + 1) * HD)
            dla_ref[:, hs] = jnp.broadcast_to(jnp.sum(prod_a[:, hs], axis=-1, keepdims=True), (TM, HD))
        prod_b = dob * ob_ref[...]
        dlb_ref[...] = jnp.zeros_like(dlb_ref)
        for h in range(N_FOX):
            dlb_ref[h // 2, :, h % 2:h % 2 + 1] = jnp.sum(prod_b[:, h * HD:(h + 1) * HD], axis=-1, keepdims=True)

    full = lambda a: pl.BlockSpec(a.shape, lambda i: (0, 0, 0))
    return pl.pallas_call(
        body, grid=(S // TM,),
        in_specs=[_row(D), _row(D), full(w_a), full(w_b), _row(DIL_W), _row(FOX_W)],
        out_specs=[_row(DIL_W), _row(DIL_W), _row(FOX_W), pl.BlockSpec((N_PAIR, TM, 128), lambda i: (0, i, 0))],
        out_shape=[SDS((S, DIL_W), f32), SDS((S, DIL_W), f32), SDS((S, FOX_W), bf16), SDS((N_PAIR, S, 128), f32)],
        name="branch_bwd", compiler_params=_params(("parallel",), 8 * TM * D + 2 * (DIL_W + FOX_W) * D, 8 * TM * D),
    )(d_ya, d_yb, w_a, w_b, out_a, out_b)


def _branch_grads(out_a, out_b, d_ya, d_yb):
    cw = D // N_SHARD

    def body(oa_ref, ob_ref, dya_ref, dyb_ref, ga_ref, gb_ref):
        ga_ref[...] = _tn(oa_ref[...].astype(bf16), dya_ref[...]).astype(bf16)
        gb_ref[...] = _tn(ob_ref[...].astype(bf16), dyb_ref[...]).astype(bf16)

    whole = lambda w: pl.BlockSpec((S, w), lambda j: (0, 0))
    cols = pl.BlockSpec((S, cw), lambda j: (0, j))
    return pl.pallas_call(
        body, grid=(N_SHARD,), in_specs=[whole(DIL_W), whole(FOX_W), cols, cols],
        out_specs=[pl.BlockSpec((None, DIL_W, cw), lambda j: (j, 0, 0)), pl.BlockSpec((None, FOX_W, cw), lambda j: (j, 0, 0))],
        out_shape=[SDS((N_SHARD, DIL_W, cw), bf16), SDS((N_SHARD, FOX_W, cw), bf16)], name="grad_w_proj_ab",
        compiler_params=_params(("parallel",), 4 * S * (DIL_W + FOX_W) + 4 * S * cw + 4 * (DIL_W + FOX_W) * cw,
                                4 * S * (DIL_W + FOX_W)))(out_a, out_b, d_ya, d_yb)


FF_TN = F_FF // 2
FF_TM = 1024


def _ffn_fwd(h, w_gate_t, w_up_t):
    def body(h_ref, wg_ref, wu_ref, g_ref, u_ref, a_ref):
        hb = h_ref[...]
        g = _nt(hb, wg_ref[...])
        u = _nt(hb, wu_ref[...])
        g_ref[...] = g
        u_ref[...] = u
        a_ref[...] = (g * jax.nn.sigmoid(g) * u).astype(bf16)

    tile = pl.BlockSpec((FF_TM, FF_TN), lambda j, i: (i, j))
    wspec = pl.BlockSpec((FF_TN, D), lambda j, i: (j, 0))
    return pl.pallas_call(
        body, grid=(F_FF // FF_TN, S // FF_TM),
        in_specs=[pl.BlockSpec((FF_TM, D), lambda j, i: (i, 0)), wspec, wspec], out_specs=[tile] * 3,
        out_shape=[SDS((S, F_FF), f32), SDS((S, F_FF), f32), SDS((S, F_FF), bf16)], name="ffn_fwd",
        compiler_params=_params(("parallel", "parallel"), 2 * FF_TM * D + 4 * D * FF_TN + 10 * FF_TM * FF_TN, 16 * FF_TM * FF_TN),
    )(h, w_gate_t, w_up_t)


def _ffn_bwd_act(d_ff, w_down, g_act, u_act):
    def body(d_ref, wd_ref, g_ref, u_ref, dg_ref, du_ref):
        da = _nt(d_ref[...], wd_ref[...])
        g = g_ref[...]
        sg = jax.nn.sigmoid(g)
        du_ref[...] = (da * g * sg).astype(bf16)
        dg_ref[...] = (da * u_ref[...] * sg * (1.0 + g * (1.0 - sg))).astype(bf16)

    tile = pl.BlockSpec((FF_TM, FF_TN), lambda j, i: (i, j))
    return pl.pallas_call(
        body, grid=(F_FF // FF_TN, S // FF_TM),
        in_specs=[pl.BlockSpec((FF_TM, D), lambda j, i: (i, 0)), pl.BlockSpec((FF_TN, D), lambda j, i: (j, 0)), tile, tile],
        out_specs=[tile, tile], out_shape=[SDS((S, F_FF), bf16)] * 2, name="ffn_bwd_act",
        compiler_params=_params(("parallel", "parallel"), 2 * FF_TM * D + 2 * D * FF_TN + 12 * FF_TM * FF_TN, 16 * FF_TM * FF_TN),
    )(d_ff, w_down, g_act, u_act)


def _row_tile(rows):
    return next(t for t in (376, 128, 176, 64, 32, 16, 8) if rows % t == 0)


def _adamw_math(w, g, m, v):
    c1 = 1.0 - ADAM_B1 ** ADAM_STEP
    c2 = 1.0 - ADAM_B2 ** ADAM_STEP
    m_new = ADAM_B1 * m + (1.0 - ADAM_B1) * g
    v_new = ADAM_B2 * v + (1.0 - ADAM_B2) * (g * g)
    return -ADAM_LR * ((m_new / c1) / (jnp.sqrt(v_new / c2) + ADAM_EPS) + ADAM_WD * w), m_new, v_new


def _adamw(w, g, m, v, name):
    rows, cols = w.shape
    tm = _row_tile(rows)

    def body(w_ref, g_ref, m_ref, v_ref, d_ref, nm_ref, nv_ref):
        d_ref[...], nm_ref[...], nv_ref[...] = _adamw_math(w_ref[...], g_ref[...], m_ref[...], v_ref[...])

    spec = pl.BlockSpec((tm, cols), lambda i: (i, 0))
    return pl.pallas_call(
        body, grid=(rows // tm,), in_specs=[spec] * 4, out_specs=[spec] * 3, out_shape=[SDS(w.shape, f32)] * 3,
        name=name, compiler_params=_params(("parallel",), 28 * tm * cols, 16 * tm * cols))(w, g, m, v)


def _adamw_halves(w, g_mine, g_theirs, m, v, name):
    cols = w.shape[1]
    tm = _row_tile(g_mine.shape[0])
    per_half = g_mine.shape[0] // tm
    assert 2 * g_mine.shape[0] - w.shape[0] < tm
    core = lax.axis_index("c").astype(jnp.int32).reshape(1)

    def body(c_ref, w_ref, gm_ref, gt_ref, m_ref, v_ref, g_ref, d_ref, nm_ref, nv_ref):
        mine = pl.program_id(0) // per_half == c_ref[0]
        g = jnp.where(mine, gm_ref[...], gt_ref[...])
        g_ref[...] = g
        d_ref[...], nm_ref[...], nv_ref[...] = _adamw_math(w_ref[...], g, m_ref[...], v_ref[...])

    spec = pl.BlockSpec((tm, cols), lambda i, c_ref: (i, 0))
    in_half = lambda i, first: jnp.clip(i - first * per_half, 0, per_half - 1)
    grid_spec = pltpu.PrefetchScalarGridSpec(
        num_scalar_prefetch=1, grid=(2 * per_half,),
        in_specs=[spec, pl.BlockSpec((tm, cols), lambda i, c_ref: (in_half(i, c_ref[0]), 0)),
                  pl.BlockSpec((tm, cols), lambda i, c_ref: (in_half(i, 1 - c_ref[0]), 0)), spec, spec],
        out_specs=[spec] * 4)
    return pl.pallas_call(
        body, grid_spec=grid_spec, out_shape=[SDS(w.shape, f32)] * 4, name=name,
        compiler_params=_params(("parallel",), 36 * tm * cols, 16 * tm * cols))(core, w, g_mine, g_theirs, m, v)


_ANY = pl.BlockSpec(memory_space=pl.ANY)


def _place():
    x, y, c = lax.axis_index("x"), lax.axis_index("y"), lax.axis_index("c")
    chips = [(1 - x, y), (x, 1 - y), (1 - x, 1 - y)]
    return x, y, c, chips


def _halved(t):
    return t.reshape(t.shape[:-2] + (2, t.shape[-2] // 2, t.shape[-1]))


def _gather_body(src, out, send_ici, recv_ici, send_d2d, recv_d2d):
    x, y, c, chips = _place()
    sibling = (x, y, 1 - c)
    me_j = 2 * x + y
    sends = []
    for a in range(len(src)):
        for p in range(3):
            cp = pltpu.make_async_remote_copy(
                src_ref=src[a].at[c], dst_ref=out[a].at[me_j, c], send_sem=send_ici.at[a, p],
                recv_sem=recv_ici.at[a, p], device_id=(*chips[p], c), device_id_type=MESH)
            cp.start()
            sends.append(cp)
    for a in range(len(src)):
        for p, (px, py) in enumerate(chips):
            blk = out[a].at[2 * px + py, c]
            pltpu.make_async_remote_copy(
                src_ref=blk, dst_ref=blk, send_sem=send_ici.at[a, p], recv_sem=recv_ici.at[a, p],
                device_id=sibling, device_id_type=MESH).wait_recv()
            fw = pltpu.make_async_remote_copy(
                src_ref=blk, dst_ref=blk, send_sem=send_d2d.at[a, p], recv_sem=recv_d2d.at[a, p],
                device_id=sibling, device_id_type=MESH)
            fw.start()
            sends.append(fw)
    for a in range(len(src)):
        for p, (px, py) in enumerate(chips):
            blk = out[a].at[2 * px + py, 1 - c]
            pltpu.make_async_remote_copy(
                src_ref=blk, dst_ref=blk, send_sem=send_d2d.at[a, p], recv_sem=recv_d2d.at[a, p],
                device_id=sibling, device_id_type=MESH).wait_recv()
    for cp in sends:
        cp.wait_send()


def _handshake(peers):
    barrier = pltpu.get_barrier_semaphore()
    for peer in peers:
        pl.semaphore_signal(barrier, inc=1, device_id=peer, device_id_type=MESH)
    pl.semaphore_wait(barrier, len(peers))


_SEQUENCER = dict(axis_name="sequencer", num_cores=1)
GATHER_LATE_ID, SCATTER_EARLY_ID, SWAP_EARLY_ID, GATHER_FIRST_ID, SCATTER_LATE_ID, SHARE_SMALL_ID = 1, 2, 3, 4, 5, 6


def _all_gather_async(shards, after, name, collective_id):
    n, k = len(shards), len(after)

    def body(*refs):
        x, y, c, chips = _place()
        _handshake([(*chip, c) for chip in chips] + [(x, y, 1 - c)])
        _gather_body(refs[:n], refs[n + k:2 * n + k], *refs[2 * n + k:])

    return pl.kernel(
        body, out_type=[SDS((N_SHARD,) + t.shape, t.dtype) for t in shards],
        mesh=plsc.ScalarSubcoreMesh(**_SEQUENCER), scratch_types=[pltpu.SemaphoreType.DMA((n, 3))] * 4,
        compiler_params=pltpu.CompilerParams(collective_id=collective_id), name=name)(*shards, *after)


def _pair_swap(grads):
    n = len(grads)

    def body(*refs):
        src, out, send_sems, recv_sems = refs[:n], refs[n:2 * n], refs[2 * n], refs[2 * n + 1]
        x, y, c, _ = _place()
        copies = [pltpu.make_async_remote_copy(
            src_ref=src[a].at[:, 1 - c], dst_ref=out[a], send_sem=send_sems.at[a], recv_sem=recv_sems.at[a],
            device_id=(x, y, 1 - c), device_id_type=MESH) for a in range(n)]
        for cp in copies:
            cp.start()
        for cp in copies:
            cp.wait()

    return pl.pallas_call(
        body, in_specs=[_ANY] * n, out_specs=[_ANY] * n,
        out_shape=[SDS((N_SHARD,) + t.shape[2:], t.dtype) for t in grads],
        scratch_shapes=[pltpu.SemaphoreType.DMA((n,)), pltpu.SemaphoreType.DMA((n,))], name="pair_swap",
        compiler_params=pltpu.CompilerParams(has_side_effects=True))(*grads)


def _pair_swap_early(grads):
    n = len(grads)

    def body(*refs):
        src, out, send_sems, recv_sems = refs[:n], refs[n:2 * n], refs[2 * n], refs[2 * n + 1]
        x, y, c, _ = _place()
        _handshake([(x, y, 1 - c)])
        copies = [pltpu.make_async_remote_copy(
            src_ref=src[a].at[:, 1 - c], dst_ref=out[a], send_sem=send_sems.at[a], recv_sem=recv_sems.at[a],
            device_id=(x, y, 1 - c), device_id_type=MESH) for a in range(n)]
        for cp in copies:
            cp.start()
        for cp in copies:
            cp.wait()

    return pl.kernel(
        body, out_type=[SDS((N_SHARD,) + t.shape[2:], t.dtype) for t in grads],
        mesh=plsc.ScalarSubcoreMesh(**_SEQUENCER), scratch_types=[pltpu.SemaphoreType.DMA((n,))] * 2,
        compiler_params=pltpu.CompilerParams(collective_id=SWAP_EARLY_ID), name="pair_swap_early")(*grads)


def _scatter_parts(parts, name, collective_id):
    n = len(parts)

    def body(*refs):
        part, recv, send_sems, recv_sems = refs[:n], refs[n:2 * n], refs[2 * n], refs[2 * n + 1]
        x, y, c, chips = _place()
        _handshake([(*chip, c) for chip in chips])
        me_j = 2 * x + y
        sends = []
        for a in range(n):
            for p, (px, py) in enumerate(chips):
                cp = pltpu.make_async_remote_copy(
                    src_ref=part[a].at[2 * px + py], dst_ref=recv[a].at[me_j], send_sem=send_sems.at[a, p],
                    recv_sem=recv_sems.at[a, p], device_id=(px, py, c), device_id_type=MESH)
                cp.start()
                sends.append(cp)
        for a in range(n):
            for p, (px, py) in enumerate(chips):
                slot = recv[a].at[2 * px + py]
                pltpu.make_async_remote_copy(
                    src_ref=slot, dst_ref=slot, send_sem=send_sems.at[a, p], recv_sem=recv_sems.at[a, p],
                    device_id=(px, py, c), device_id_type=MESH).wait_recv()
        for cp in sends:
            cp.wait_send()

    return pl.kernel(
        body, out_type=[SDS(t.shape, t.dtype) for t in parts],
        mesh=plsc.ScalarSubcoreMesh(**_SEQUENCER), scratch_types=[pltpu.SemaphoreType.DMA((n, 3))] * 2,
        compiler_params=pltpu.CompilerParams(collective_id=collective_id), name=name)(*parts)


def _pair_sum(grads, other, name):
    _, _, rows, cols = grads.shape
    tr = _row_tile(rows)
    core = lax.axis_index("c").astype(jnp.int32).reshape(1)

    def body(c_ref, g_ref, o_ref, out_ref):
        out_ref[...] = (g_ref[...].astype(f32) + o_ref[...].astype(f32)).astype(bf16)

    grid_spec = pltpu.PrefetchScalarGridSpec(
        num_scalar_prefetch=1, grid=(N_SHARD, rows // tr),
        in_specs=[pl.BlockSpec((None, None, tr, cols), lambda j, i, c_ref: (j, c_ref[0], i, 0)),
                  pl.BlockSpec((None, tr, cols), lambda j, i, c_ref: (j, i, 0))],
        out_specs=pl.BlockSpec((None, tr, cols), lambda j, i, c_ref: (j, i, 0)))
    return pl.pallas_call(
        body, grid_spec=grid_spec, out_shape=SDS((N_SHARD, rows, cols), bf16), name=name,
        compiler_params=_params(("parallel", "parallel"), 10 * tr * cols, 12 * tr * cols))(core, grads, other)


def _share_small(small):
    def body(small_ref, small_all_ref, ssend, srecv, local_sem):
        x, y, c, _ = _place()
        flip = lambda a, bit: 1 - a if bit else a
        peers = [(flip(x, k & 4), flip(y, k & 2), flip(c, k & 1)) for k in range(1, 8)]
        _handshake(peers)
        me_dev = 4 * x + 2 * y + c
        own = pltpu.make_async_copy(small_ref, small_all_ref.at[me_dev], local_sem)
        own.start()
        sends = []
        for k, to in enumerate(peers):
            cp = pltpu.make_async_remote_copy(
                src_ref=small_ref, dst_ref=small_all_ref.at[me_dev],
                send_sem=ssend.at[k], recv_sem=srecv.at[k], device_id=to, device_id_type=MESH)
            cp.start()
            sends.append(cp)
        for k, (px, py, pc) in enumerate(peers):
            slot = small_all_ref.at[4 * px + 2 * py + pc]
            pltpu.make_async_remote_copy(
                src_ref=slot, dst_ref=slot, send_sem=ssend.at[k], recv_sem=srecv.at[k],
                device_id=(px, py, pc), device_id_type=MESH).wait_recv()
        for cp in sends:
            cp.wait_send()
        own.wait()

    return pl.kernel(
        body, out_type=SDS((8, SMALL_ROWS, D), f32), mesh=plsc.ScalarSubcoreMesh(**_SEQUENCER),
        scratch_types=[pltpu.SemaphoreType.DMA((7,)), pltpu.SemaphoreType.DMA((7,)), pltpu.SemaphoreType.DMA],
        compiler_params=pltpu.CompilerParams(collective_id=SHARE_SMALL_ID), name="share_small")(small)


def _sum_partials(part, recv, name):
    _, rows, cols = recv.shape
    tr = _row_tile(rows)
    me = (2 * lax.axis_index("x") + lax.axis_index("y")).astype(jnp.int32).reshape(1)

    def body(me_ref, mine, r0, r1, r2, r3, out_ref):
        acc = None
        for j, r in enumerate((r0, r1, r2, r3)):
            term = jnp.where(me_ref[0] == j, mine[...], r[...]).astype(f32)
            acc = term if acc is None else acc + term
        out_ref[...] = acc

    slot = lambda j: pl.BlockSpec((None, tr, cols), lambda i, me_ref: (jnp.where(me_ref[0] == j, j ^ 1, j), i, 0))
    grid_spec = pltpu.PrefetchScalarGridSpec(
        num_scalar_prefetch=1, grid=(rows // tr,),
        in_specs=[pl.BlockSpec((None, tr, cols), lambda i, me_ref: (me_ref[0], i, 0)), slot(0), slot(1), slot(2), slot(3)],
        out_specs=pl.BlockSpec((tr, cols), lambda i, me_ref: (i, 0)))
    return pl.pallas_call(
        body, grid_spec=grid_spec, out_shape=SDS((rows, cols), f32), name=name,
        compiler_params=_params(("parallel",), 14 * tr * cols, 12 * tr * cols))(me, part, recv, recv, recv, recv)


def _sum_small(small_all):
    def body(small_ref, out_ref):
        tot = small_ref[0]
        for k in range(1, 8):
            tot = tot + small_ref[k]
        out_ref[...] = tot

    return pl.pallas_call(
        body, grid=(1,), in_specs=[pl.BlockSpec((8, SMALL_ROWS, D), lambda i: (0, 0, 0))],
        out_specs=pl.BlockSpec((SMALL_ROWS, D), lambda i: (0, 0)), out_shape=SDS((SMALL_ROWS, D), f32),
        name="sum_small", compiler_params=_params(("arbitrary",), 36 * SMALL_ROWS * D))(small_all)


def _swap_halves(halves, name):
    n = len(halves)

    def body(*refs):
        src, out, send_sems, recv_sems = refs[:n], refs[n:2 * n], refs[2 * n], refs[2 * n + 1]
        x, y, c, _ = _place()
        copies = [pltpu.make_async_remote_copy(
            src_ref=src[a], dst_ref=out[a], send_sem=send_sems.at[a], recv_sem=recv_sems.at[a],
            device_id=(x, y, 1 - c), device_id_type=MESH) for a in range(n)]
        for cp in copies:
            cp.start()
        for cp in copies:
            cp.wait()

    return pl.pallas_call(
        body, in_specs=[_ANY] * n, out_specs=[_ANY] * n, out_shape=[SDS(t.shape, f32) for t in halves],
        scratch_shapes=[pltpu.SemaphoreType.DMA((n,))] * 2, name=name,
        compiler_params=pltpu.CompilerParams(has_side_effects=True))(*halves)


def _kernel_layout(name, t):
    t = t[0]
    return jnp.swapaxes(t, 0, 1) if name in TRANSPOSED else t


def _harness_layout(name, t):
    if name in TRANSPOSED:
        t = jnp.swapaxes(t, 0, 1)
    return t[None]


def _pad_rows(t, rows):
    return t if t.shape[0] == rows else jnp.pad(t, ((0, rows - t.shape[0]), (0, 0)))


_QA, _KA, _VA, _QB, _F, _GAB = 0, 768, 1536, 2304, 3840, 3848


def _spans(a, b):
    return [(j, max(a, j * IN_SHARD) - j * IN_SHARD, max(a, j * IN_SHARD) - a,
             min(b, (j + 1) * IN_SHARD) - max(a, j * IN_SHARD))
            for j in range(N_SHARD) if max(a, j * IN_SHARD) < min(b, (j + 1) * IN_SHARD)]


_LANES = pl.BlockSpec((N_SHARD, IN_SHARD_PAD, 128), lambda c: (0, 0, c))


def _split_w_in(shards):
    group = [[(o + g * DIL_W, o + (g + 1) * DIL_W) for o in (_QA, _KA, _VA)] for g in range(3)]
    fox = [[(_QB + k * FOX_W, _QB + (k + 1) * FOX_W)] for k in range(3)]
    wanted = group + fox + [[(_QB, _F)], [(_F, _GAB)], [(_GAB, IN_COLS)]]
    rows = [sum(b - a for a, b in w) for w in wanted]
    rows[7] = 128

    def body(s_ref, *o_refs):
        for o_ref, want in zip(o_refs, wanted):
            at = 0
            for a, b in want:
                for j, src, off, n in _spans(a, b):
                    o_ref[at + off:at + off + n, :] = s_ref[j, src:src + n, :]
                at += b - a
        o_refs[7][N_FOX:, :] = jnp.zeros((128 - N_FOX, 128), bf16)

    return pl.pallas_call(
        body, grid=(D // 128,), in_specs=[_LANES], out_specs=[pl.BlockSpec((r, 128), lambda c: (0, c)) for r in rows],
        out_shape=[SDS((r, D), bf16) for r in rows], name="split_w_in",
        compiler_params=_params(("parallel",), 2 * 128 * (N_SHARD * IN_SHARD_PAD + sum(rows))))(shards)


def _join_w_in(g_a, g_fox, g_f, g_gab):
    parts = [(g_a[k], o, o + DIL_W) for o in (0, DIL_W, 2 * DIL_W) for k in range(3)]
    parts += [(t, 0, FOX_W) for t in g_fox] + [(g_f, 0, N_FOX), (g_gab, 0, 2 * D)]
    arrays = list(g_a) + list(g_fox) + [g_f, g_gab]
    index = {id(t): i for i, t in enumerate(arrays)}

    def body(*refs):
        o_ref = refs[-1]
        o_ref[:, IN_SHARD:, :] = jnp.zeros((N_SHARD, IN_SHARD_PAD - IN_SHARD, 128), bf16)
        at = 0
        for t, lo, hi in parts:
            src_ref = refs[index[id(t)]]
            for j, dst, off, n in _spans(at, at + hi - lo):
                o_ref[j, dst:dst + n, :] = src_ref[lo + off:lo + off + n, :].astype(bf16)
            at += hi - lo

    return pl.pallas_call(
        body, grid=(D // 128,), in_specs=[pl.BlockSpec((t.shape[0], 128), lambda c: (0, c)) for t in arrays],
        out_specs=_LANES, out_shape=SDS((N_SHARD, IN_SHARD_PAD, D), bf16), name="join_w_in",
        compiler_params=_params(("parallel",), 2 * 128 * (N_SHARD * IN_SHARD_PAD + sum(t.shape[0] for t in arrays))),
    )(*arrays)


def _full_weights(gathered):
    full = {n: t.reshape((N_SHARD,) + SHARD_SHAPE[n]) for n, t in gathered.items()}
    out = {}
    if "w_in" in full:
        pieces = _split_w_in(full["w_in"])
        out.update(w_a_t=pieces[0:3], w_fox_t=pieces[3:6], w_vr_t=pieces[6], w_f_t=pieces[7], w_gab_t=pieces[8])
    if "w_out" in full:
        out.update(
            w_a4=full["w_proj_a"],
            w_b4=full["w_proj_b"],
            w_out=full["w_out"].reshape(D, D),
            w_gate_t=full["w_ffn_gate"].reshape(F_FF, D),
            w_up_t=full["w_ffn_up"].reshape(F_FF, D),
            w_down=full["w_ffn_down"].reshape(F_FF, D))
    return out


def _sharded_grads(g):
    full = dict(w_in=_join_w_in(g["w_a_t"], g["w_fox_t"], g["w_f_t"], g["w_gab_t"]), w_proj_a=g["w_a4"],
                w_proj_b=g["w_b4"], w_out=g["w_out"], w_ffn_gate=g["w_gate_t"], w_ffn_up=g["w_up_t"],
                w_ffn_down=g["w_down"])
    return {n: _halved(full[n].reshape((N_SHARD,) + SHARD_SHAPE[n])) for n in W_NAMES}


def _local_step(x, target, wt, b_forget, g_mix_pre, g_mix_post, g_ffn_pre, g_ffn_post, late=None):
    tables = _rope_tables()
    b128 = jnp.pad(b_forget, ((0, 0), (0, 128 - N_FOX)))
    dils = tuple(d for _, d in DIL_GROUPS[1:])

    hs = _norm_fwd([x] + list(_perm_rows([x], dils, "perm_x")), g_mix_pre)
    h1 = hs[0]
    if callable(wt):
        wt = wt(h1)
    qkv = [_rope_fwd(g, _mm([(hs[g], wt["w_a_t"][g])], "nt", f32, tm=1024, tn=QKV_W, name=f"proj_a_{g}"), tables)
           for g in range(3)]
    vr = _mm([(h1, wt["w_vr_t"])], "nt", bf16, tm=1024, tn=VR_W // 2, name="proj_vr")
    gab = _mm([(h1, wt["w_gab_t"])], "nt", f32, tm=512, tn=2 * D, name="proj_gab")
    fz = _mm([(h1, wt["w_f_t"])], "nt", f32, tm=1024, tn=128, name="proj_f")
    dil = [_dil_fwd(g, qkv[g]) for g in range(3)]
    out_a, lse_a = _dil_combine([o for o, _ in dil], [l for _, l in dil])
    f_q, f_k = _forget_fwd(fz, b128)
    out_b, lse_b = _fox_fwd(vr, f_q, f_k)
    if late is not None:
        wt = {**wt, **late(out_b)}
    ya, yb, merged = _merge_fwd(out_a, out_b, wt["w_a4"], wt["w_b4"], gab)
    mix, x2, h3 = _resid_norm_fwd(x, merged, wt["w_out"], g_mix_post, g_ffn_pre)
    g_act, u_act, a_act = _ffn_fwd(h3, wt["w_gate_t"], wt["w_up_t"])
    sq_err, dy, d_ff, dg_ffn_post = _loss_head(x2, a_act, wt["w_down"], g_ffn_post, target)

    grads = {}
    d_g, d_u = _ffn_bwd_act(d_ff, wt["w_down"], g_act, u_act)
    grads["w_down"] = _mm([(a_act, d_ff)], "tn", bf16, tm=FF_TN, tn=D, name="grad_w_down")
    grads["w_gate_t"] = _mm([(d_g, h3)], "tn", bf16, tm=FF_TN, tn=D, name="grad_w_gate")
    grads["w_up_t"] = _mm([(d_u, h3)], "tn", bf16, tm=FF_TN, tn=D, name="grad_w_up")
    dx2, d_mix, dg_ffn_pre, dg_mix_post = _norm_bwd_mid(dy, d_g, d_u, wt["w_gate_t"], wt["w_up_t"], x2, mix,
                                                        g_ffn_pre, g_mix_post)

    grads["w_out"] = _mm([(merged, d_mix)], "tn", bf16, tm=D, tn=D, name="grad_w_out")
    d_ya, d_yb, d_gab = _merge_bwd(d_mix, wt["w_out"], ya, yb, gab)
    grads["w_a4"], grads["w_b4"] = _branch_grads(out_a, out_b, d_ya, d_yb)
    d_out_a, delta_a, d_out_b, delta_b = _branch_bwd(d_ya, d_yb, wt["w_a4"], wt["w_b4"], out_a, out_b)

    perm = _perm_rows([d_out_a, delta_a, lse_a], dils, "perm_dil_bwd")
    aux = [(d_out_a, delta_a, lse_a)] + [tuple(perm[k * len(dils) + i] for k in range(3)) for i in range(len(dils))]
    d_qkv = []
    for g in range(3):
        dq, dk, dv = _dil_bwd(g, qkv[g], *aux[g])
        d_qkv.append(_rope_bwd(g, dq, dk, dv, tables))
    *d_fox, d_f_cols, d_f_rows = _fox_bwd(vr, f_q, f_k, lse_b, d_out_b, delta_b)
    d_z, d_b128 = _forget_bwd(fz, b128, d_f_cols, d_f_rows)

    grads["w_a_t"] = [_mm([(d_qkv[g], hs[g])], "tn", bf16, tm=QKV_W, tn=D, name=f"grad_w_a_{g}") for g in range(3)]
    grads["w_fox_t"] = [_mm([(d_fox[k], h1)], "tn", bf16, tm=FOX_W, tn=D, name=f"grad_w_fox_{k}") for k in range(3)]
    grads["w_gab_t"] = _mm([(d_gab, h1)], "tn", bf16, tm=D, tn=D, name="grad_w_gab")
    grads["w_f_t"] = _mm([(d_z, h1)], "tn", bf16, tm=128, tn=D, name="grad_w_f")
    d_h1_nat = _mm([(d_qkv[0], wt["w_a_t"][0])] + list(zip(d_fox, wt["w_fox_t"]))
                   + [(d_gab, wt["w_gab_t"]), (d_z, wt["w_f_t"])], "nn", f32, tm=512, tn=D, name="proj_in_bwd")
    d_h1_dil = [_mm([(d_qkv[g], wt["w_a_t"][g])], "nn", f32, tm=1024, tn=D, name=f"proj_a_bwd_{g}") for g in (1, 2)]
    d_h1 = _unperm_sum(d_h1_nat, d_h1_dil, dils, "unperm_d_h1")
    grad_x, dg_mix_pre = _norm_bwd_in(dx2, d_h1, x, g_mix_pre)

    small = dict(b_forget=d_b128[:, :N_FOX], norm_mix_pre=dg_mix_pre, norm_mix_post=dg_mix_post,
                 norm_ffn_pre=dg_ffn_pre, norm_ffn_post=dg_ffn_post)
    grads["mid_backward"] = d_qkv[0]
    return sq_err, grad_x, grads, small


NORMS = ("norm_mix_pre", "norm_mix_post", "norm_ffn_pre", "norm_ffn_post")
ORDER = ("w_in", "w_proj_a", "w_proj_b", "w_out", "b_forget", "w_ffn_gate", "w_ffn_up", "w_ffn_down") + NORMS


def kernel(x, w_in, w_proj_a, w_proj_b, w_out, b_forget, w_ffn_gate, w_ffn_up, w_ffn_down, norm_mix_pre, norm_mix_post, norm_ffn_pre, norm_ffn_post, loss_target, m_w_in, m_w_proj_a, m_w_proj_b, m_w_out, m_b_forget, m_w_ffn_gate, m_w_ffn_up, m_w_ffn_down, m_norm_mix_pre, m_norm_mix_post, m_norm_ffn_pre, m_norm_ffn_post, v_w_in, v_w_proj_a, v_w_proj_b, v_w_out, v_b_forget, v_w_ffn_gate, v_w_ffn_up, v_w_ffn_down, v_norm_mix_pre, v_norm_mix_post, v_norm_ffn_pre, v_norm_ffn_post):
    given = dict(w_in=w_in, w_proj_a=w_proj_a, w_proj_b=w_proj_b, w_out=w_out, w_ffn_gate=w_ffn_gate,
                 w_ffn_up=w_ffn_up, w_ffn_down=w_ffn_down)
    given_m = dict(w_in=m_w_in, w_proj_a=m_w_proj_a, w_proj_b=m_w_proj_b, w_out=m_w_out, w_ffn_gate=m_w_ffn_gate,
                   w_ffn_up=m_w_ffn_up, w_ffn_down=m_w_ffn_down)
    given_v = dict(w_in=v_w_in, w_proj_a=v_w_proj_a, w_proj_b=v_w_proj_b, w_out=v_w_out, w_ffn_gate=v_w_ffn_gate,
                   w_ffn_up=v_w_ffn_up, w_ffn_down=v_w_ffn_down)
    w, m, v = ({n: _kernel_layout(n, t[n]) for n in W_NAMES} for t in (given, given_m, given_v))
    small_w = dict(b_forget=b_forget, norm_mix_pre=norm_mix_pre, norm_mix_post=norm_mix_post,
                   norm_ffn_pre=norm_ffn_pre, norm_ffn_post=norm_ffn_post)
    small_m = dict(b_forget=m_b_forget, norm_mix_pre=m_norm_mix_pre, norm_mix_post=m_norm_mix_post,
                   norm_ffn_pre=m_norm_ffn_pre, norm_ffn_post=m_norm_ffn_post)
    small_v = dict(b_forget=v_b_forget, norm_mix_pre=v_norm_mix_pre, norm_mix_post=v_norm_mix_post,
                   norm_ffn_pre=v_norm_ffn_pre, norm_ffn_post=v_norm_ffn_post)

    own = [_halved(_pad_rows(w[n].astype(bf16), SHARD_SHAPE[n][0])) for n in W_NAMES]
    chip = 2 * lax.axis_index("x") + lax.axis_index("y")
    exchanged = {"first": _all_gather_async(own[:1], [], "all_gather_first", GATHER_FIRST_ID)}
    fill = lambda ts, mine: [lax.dynamic_update_index_in_dim(t, o, chip, 0) for t, o in zip(ts, mine)]

    def first_weights(ready):
        arrived, _ = lax.optimization_barrier((list(exchanged["first"]), ready))
        exchanged["late"] = _all_gather_async(own[1:], [arrived[0][0, 0, :16, :128]], "all_gather_late", GATHER_LATE_ID)
        return _full_weights(dict(zip(W_NAMES[:1], fill(arrived, own[:1]))))

    def late_weights(ready):
        arrived, _ = lax.optimization_barrier((list(exchanged["late"]), ready))
        return _full_weights(dict(zip(W_NAMES[1:], fill(arrived, own[1:]))))

    norm_mix_pre, m["w_in"], v["w_in"] = lax.optimization_barrier((norm_mix_pre, m["w_in"], v["w_in"]))
    sq_err, grad_x, grads, small = _local_step(x[0], loss_target[0], first_weights, b_forget, norm_mix_pre,
                                               norm_mix_post, norm_ffn_pre, norm_ffn_post, late=late_weights)

    g4 = _sharded_grads(grads)
    stack = lambda t, extra: jnp.concatenate(
        [jnp.pad(t["b_forget"], ((0, 0), (0, D - N_FOX)))] + [t[n] for n in NORMS]
        + [jnp.pad(extra, ((0, SMALL_ROWS - LOSS_ROW - 1), (0, D - extra.shape[1])), constant_values=1.0)], axis=0)
    early, _ = lax.optimization_barrier((list(_pair_swap_early([g4[n] for n in W_NAMES[1:]])), grads["mid_backward"]))
    other = list(_pair_swap([g4["w_in"]])) + early
    parts = [_pair_sum(g4[n], o, "pair_sum_" + n) for n, o in zip(W_NAMES, other)]
    recv_early = _scatter_parts(parts[1:], "scatter_early", SCATTER_EARLY_ID)
    recv_in = _scatter_parts(parts[:1], "scatter_partials", SCATTER_LATE_ID)
    small_all = _share_small(stack(small, sq_err))

    g_shard, delta, new_m, new_v = {}, {}, {}, {}

    def summed(names, parts, recv):
        halves = [_sum_partials(p, r, "sum_partials_" + n) for n, p, r in zip(names, parts, recv)]
        return halves, list(_swap_halves(halves, "swap_halves_" + names[0]))

    def update(names, halves, theirs):
        for n, mine, other_half in zip(names, halves, theirs):
            g_shard[n], delta[n], new_m[n], new_v[n] = _adamw_halves(w[n], mine, other_half, m[n], v[n], "adamw_" + n)

    recv_early, _ = lax.optimization_barrier((list(recv_early), parts[0]))
    early_mine, early_theirs = summed(W_NAMES[1:], parts[1:], recv_early)
    recv_in, _ = lax.optimization_barrier((list(recv_in), early_theirs))
    update(W_NAMES[:1], *summed(W_NAMES[:1], parts[:1], recv_in))
    (early_theirs, small_all), _ = lax.optimization_barrier(((early_theirs, small_all), delta["w_in"]))
    update(W_NAMES[1:], early_mine, early_theirs)
    small_sum = _sum_small(small_all)
    loss = small_sum[LOSS_ROW, 0] * (0.5 / D)
    ones = jnp.ones((1, 128), f32)
    sd, sm, sv = _adamw(stack(small_w, ones), small_sum, stack(small_m, ones), stack(small_v, ones), "adamw_small")

    outs = [loss, grad_x[None]]
    for big, st in ((g_shard, small_sum), (delta, sd), (new_m, sm), (new_v, sv)):
        t = {n: _harness_layout(n, big[n]) for n in W_NAMES}
        t["b_forget"] = st[0:1, :N_FOX]
        for i, n in enumerate(NORMS):
            t[n] = st[i + 1:i + 2]
        outs += [t[n] for n in ORDER]
    return tuple(outs)
```

```python
import functools
import math

import jax
import jax.numpy as jnp
import numpy as np
from jax import lax
from jax.experimental import pallas as pl
from jax.experimental.pallas import tpu as pltpu
from jax.experimental.pallas import tpu_sc as plsc

f32 = jnp.float32
bf16 = jnp.bfloat16
SDS = jax.ShapeDtypeStruct
MESH = pl.DeviceIdType.MESH

S = 2048
D = 1024
HD = 64
BLK = 128
N_FOX = 8
FOX_W = N_FOX * HD
DIL_GROUPS = ((128, 1), (512, 4), (2048, 16))
SLOTS = 4
DIL_W = SLOTS * HD
QKV_W = 3 * DIL_W
VR_W = 3 * FOX_W
GF_W = 2 * D + 128
F_FF = 2816
ROPE_DIM = 16
ROPE_THETA = 500000.0
EPS = 1e-6
NEG = -1e30
SCALE = 1.0 / math.sqrt(HD)
IN_COLS = 5896
N_SHARD = 4

ADAM_LR, ADAM_B1, ADAM_B2, ADAM_EPS, ADAM_WD, ADAM_STEP = 0.001, 0.9, 0.999, 1e-08, 0.01, 10

VMEM_V7X = 64 * 1024 * 1024
VMEM_PLAN_MAX = VMEM_V7X - 8 * 1024 * 1024

TM = 512
TQ = 256

W_NAMES = ("w_in", "w_proj_a", "w_proj_b", "w_out", "w_ffn_gate", "w_ffn_up", "w_ffn_down")
TRANSPOSED = ("w_in", "w_ffn_gate", "w_ffn_up")
IN_SHARD = IN_COLS // N_SHARD
IN_SHARD_PAD = 1504
SHARD_SHAPE = dict(w_in=(IN_SHARD_PAD, D), w_proj_a=(DIL_W, D // N_SHARD), w_proj_b=(FOX_W, D // N_SHARD),
                   w_out=(D // N_SHARD, D), w_ffn_gate=(F_FF // N_SHARD, D), w_ffn_up=(F_FF // N_SHARD, D),
                   w_ffn_down=(F_FF // N_SHARD, D))
SMALL_ROWS = 8
LOSS_ROW = 5


def _nbytes(shape, dtype):
    return math.prod(shape) * jnp.dtype(dtype).itemsize


def _params(semantics, block_bytes, temp_bytes=0):
    need = 2 * block_bytes + temp_bytes + (2 << 20)
    return pltpu.CompilerParams(dimension_semantics=semantics, vmem_limit_bytes=int(min(need, VMEM_PLAN_MAX)))


def _row(w, tm=TM):
    return pl.BlockSpec((tm, w), lambda i: (i, 0))


def _vec(w):
    return pl.BlockSpec((1, w), lambda i: (0, 0))


def _mm(pairs, dims, out_dtype, *, tm, tn, name, m_inner=False):
    a0, b0 = pairs[0]
    m_dim = a0.shape[1] if dims == "tn" else a0.shape[0]
    n_dim = b0.shape[0] if dims == "nt" else b0.shape[1]
    contract = {"nn": ((1,), (0,)), "nt": ((1,), (1,)), "tn": ((0,), (0,))}[dims]
    n_pairs = len(pairs)
    assert m_dim % tm == 0 and n_dim % tn == 0, (name, m_dim, n_dim, tm, tn)

    def body(*refs):
        o_ref = refs[-1]
        acc = None
        for p in range(n_pairs):
            a = refs[2 * p][...].astype(bf16)
            b = refs[2 * p + 1][...].astype(bf16)
            t = lax.dot_general(a, b, (contract, ((), ())), preferred_element_type=f32)
            acc = t if acc is None else acc + t
        o_ref[...] = acc.astype(o_ref.dtype)

    if m_inner:
        grid = (n_dim // tn, m_dim // tm)
        mi = lambda j, i: i
        ni = lambda j, i: j
    else:
        grid = (m_dim // tm, n_dim // tn)
        mi = lambda i, j: i
        ni = lambda i, j: j
    in_specs, block_bytes, args = [], 0, []
    for a, b in pairs:
        k_dim = a.shape[0] if dims == "tn" else a.shape[1]
        if dims == "tn":
            in_specs.append(pl.BlockSpec((k_dim, tm), lambda *g: (0, mi(*g))))
        else:
            in_specs.append(pl.BlockSpec((tm, k_dim), lambda *g: (mi(*g), 0)))
        if dims == "nt":
            in_specs.append(pl.BlockSpec((tn, k_dim), lambda *g: (ni(*g), 0)))
        else:
            in_specs.append(pl.BlockSpec((k_dim, tn), lambda *g: (0, ni(*g))))
        block_bytes += _nbytes((tm, k_dim), a.dtype) + _nbytes((tn, k_dim), b.dtype)
        args += [a, b]
    block_bytes += _nbytes((tm, tn), out_dtype)
    temp = _nbytes((tm, tn), f32) * 2 + sum(_nbytes((tm, a.shape[0] if dims == "tn" else a.shape[1]), bf16)
                                            + _nbytes((tn, a.shape[0] if dims == "tn" else a.shape[1]), bf16)
                                            for a, _ in pairs)
    return pl.pallas_call(
        body, grid=grid, in_specs=in_specs,
        out_specs=pl.BlockSpec((tm, tn), lambda *g: (mi(*g), ni(*g))),
        out_shape=SDS((m_dim, n_dim), out_dtype), name=name,
        compiler_params=_params(("parallel", "parallel"), block_bytes, temp),
    )(*args)


def _rms(x, g):
    r = lax.rsqrt(jnp.mean(x * x, axis=-1, keepdims=True) + EPS)
    return x * r * g


def _rms_bwd(x, g, dy):
    r = lax.rsqrt(jnp.mean(x * x, axis=-1, keepdims=True) + EPS)
    xh = x * r
    dxh = dy * g
    dx = r * (dxh - xh * jnp.mean(dxh * xh, axis=-1, keepdims=True))
    return dx, jnp.sum(dy * xh, axis=0, keepdims=True)


def _acc_rows(ref, val):
    @pl.when(pl.program_id(0) == 0)
    def _():
        ref[...] = jnp.zeros_like(ref)
    ref[...] += val


def _norm_fwd(xs, g):
    n = len(xs)

    def body(*refs):
        g = refs[n][...]
        for x_ref, h_ref in zip(refs[:n], refs[n + 1:]):
            h_ref[...] = _rms(x_ref[...], g).astype(bf16)

    return pl.pallas_call(
        body, grid=(S // TM,), in_specs=[_row(D)] * n + [_vec(D)], out_specs=[_row(D)] * n,
        out_shape=[SDS((S, D), bf16)] * n, name="norm_mix_pre",
        compiler_params=_params(("parallel",), 6 * n * TM * D, 8 * n * TM * D))(*xs, g)


def _perm_rows(xs, ds, name):
    n = len(xs)

    def body(*refs):
        outs = iter(refs[n:])
        for x_ref in refs[:n]:
            for d in ds:
                o_ref, rows = next(outs), S // d
                for r in range(d):
                    o_ref[r * rows:(r + 1) * rows, :] = x_ref[pl.ds(r, rows, stride=d), :]

    blk = pl.BlockSpec((S, 128), lambda c: (0, c))
    w = xs[0].shape[1]
    return pl.pallas_call(
        body, grid=(w // 128,), in_specs=[blk] * n, out_specs=[blk] * (n * len(ds)),
        out_shape=[SDS((S, w), f32)] * (n * len(ds)), name=name,
        compiler_params=_params(("parallel",), 4 * S * 128 * n * (1 + len(ds))))(*xs)


def _unperm_sum(nat, perms, ds, name):
    n = len(perms)

    def body(*refs):
        a_ref, o_ref, sc = refs[0], refs[n + 1], refs[n + 2]
        acc = a_ref[...]
        for b_ref, d in zip(refs[1:n + 1], ds):
            rows = S // d
            for r in range(d):
                sc[pl.ds(r, rows, stride=d), :] = b_ref[r * rows:(r + 1) * rows, :]
            acc = acc + sc[...]
        o_ref[...] = acc

    blk = pl.BlockSpec((S, 128), lambda c: (0, c))
    w = nat.shape[1]
    return pl.pallas_call(
        body, grid=(w // 128,), in_specs=[blk] * (n + 1), out_specs=blk, out_shape=SDS((S, w), f32),
        scratch_shapes=[pltpu.VMEM((S, 128), f32)], name=name,
        compiler_params=_params(("parallel",), 4 * S * 128 * (n + 2), 8 * S * 128))(nat, *perms)


def _whole(a):
    return pl.BlockSpec(a.shape, lambda i: (0,) * a.ndim)


def _resid_norm_fwd(x, merged, w_out, g_post, g_pre):
    def body(x_ref, mg_ref, w_ref, gp_ref, gn_ref, mix_ref, x2_ref, h_ref):
        mix = jnp.dot(mg_ref[...], w_ref[...], preferred_element_type=f32)
        x2 = x_ref[...] + _rms(mix, gp_ref[...])
        mix_ref[...] = mix
        x2_ref[...] = x2
        h_ref[...] = _rms(x2, gn_ref[...]).astype(bf16)

    return pl.pallas_call(
        body, grid=(S // TM,), in_specs=[_row(D), _row(D), _whole(w_out), _vec(D), _vec(D)], out_specs=[_row(D)] * 3,
        out_shape=[SDS((S, D), f32), SDS((S, D), f32), SDS((S, D), bf16)], name="proj_out_norm",
        compiler_params=_params(("parallel",), 16 * TM * D + 2 * D * D, 16 * TM * D))(x, merged, w_out, g_post, g_pre)


def _loss_head(x2, a_act, w_down, g_post, target):
    def body(x2_ref, a_ref, w_ref, g_ref, t_ref, loss_ref, dy_ref, dff_ref, dg_ref):
        ff = jnp.dot(a_ref[...], w_ref[...], preferred_element_type=f32)
        g = g_ref[...]
        err = x2_ref[...] + _rms(ff, g) - t_ref[...]
        dy = err * (1.0 / D)
        dff, dg = _rms_bwd(ff, g, dy)
        dy_ref[...] = dy
        dff_ref[...] = dff.astype(bf16)
        _acc_rows(dg_ref, dg)
        _acc_rows(loss_ref, jnp.full((1, 128), jnp.sum(err * err), f32))

    return pl.pallas_call(
        body, grid=(S // TM,), in_specs=[_row(D), _row(F_FF), _whole(w_down), _vec(D), _row(D)],
        out_specs=[_vec(128), _row(D), _row(D), _vec(D)],
        out_shape=[SDS((1, 128), f32), SDS((S, D), f32), SDS((S, D), bf16), SDS((1, D), f32)], name="ffn_down_loss",
        compiler_params=_params(("arbitrary",), 14 * TM * D + 2 * TM * F_FF + 2 * F_FF * D, 28 * TM * D),
    )(x2, a_act, w_down, g_post, target)


def _norm_bwd_mid(dy, d_g, d_u, w_gate_t, w_up_t, x2, mix, g_ffn_pre, g_mix_post):
    def body(dy_ref, dgt_ref, dut_ref, wg_ref, wu_ref, x2_ref, mix_ref, g3_ref, g2_ref, dx2_ref, dmix_ref, dg3_ref, dg2_ref):
        dh = jnp.dot(dgt_ref[...], wg_ref[...], preferred_element_type=f32)
        dh += jnp.dot(dut_ref[...], wu_ref[...], preferred_element_type=f32)
        d3, dg3 = _rms_bwd(x2_ref[...], g3_ref[...], dh)
        dx2 = dy_ref[...] + d3
        dmix, dg2 = _rms_bwd(mix_ref[...], g2_ref[...], dx2)
        dx2_ref[...] = dx2
        dmix_ref[...] = dmix.astype(bf16)
        _acc_rows(dg3_ref, dg3)
        _acc_rows(dg2_ref, dg2)

    tm = TM // 2
    row = lambda w: _row(w, tm)
    return pl.pallas_call(
        body, grid=(S // tm,),
        in_specs=[row(D), row(F_FF), row(F_FF), _whole(w_gate_t), _whole(w_up_t), row(D), row(D), _vec(D), _vec(D)],
        out_specs=[row(D), row(D), _vec(D), _vec(D)],
        out_shape=[SDS((S, D), f32), SDS((S, D), bf16), SDS((1, D), f32), SDS((1, D), f32)], name="ffn_bwd_in_norm",
        compiler_params=_params(("arbitrary",), 18 * tm * D + 4 * tm * F_FF + 4 * F_FF * D, 28 * tm * D),
    )(dy, d_g, d_u, w_gate_t, w_up_t, x2, mix, g_ffn_pre, g_mix_post)


def _norm_bwd_in(dx2, dh1, x, g):
    def body(dx2_ref, dh_ref, x_ref, g_ref, gx_ref, dg_ref):
        d1, dg = _rms_bwd(x_ref[...], g_ref[...], dh_ref[...])
        gx_ref[...] = dx2_ref[...] + d1
        _acc_rows(dg_ref, dg)

    return pl.pallas_call(
        body, grid=(S // TM,), in_specs=[_row(D)] * 3 + [_vec(D)], out_specs=[_row(D), _vec(D)],
        out_shape=[SDS((S, D), f32), SDS((1, D), f32)], name="norm_bwd_in",
        compiler_params=_params(("arbitrary",), 16 * TM * D, 16 * TM * D))(dx2, dh1, x, g)


def _rope_tables():
    half = ROPE_DIM // 2
    inv_freq = np.power(np.float32(ROPE_THETA), -np.arange(0, ROPE_DIM, 2, dtype=np.float32) / np.float32(ROPE_DIM))
    row = np.arange(S)
    groups = []
    for _, d in DIL_GROUPS:
        pos = ((row % (S // d)) * d + row // (S // d)).astype(np.float32)
        ang = pos[:, None] * inv_freq[None, :].astype(np.float32)
        cos, sin = np.cos(ang).astype(np.float32), np.sin(ang).astype(np.float32)
        c = np.concatenate([cos, cos, np.ones((S, HD - ROPE_DIM), np.float32)], axis=1)
        s_lo = np.concatenate([-sin, np.zeros((S, HD - half), np.float32)], axis=1)
        s_hi = np.concatenate([np.zeros((S, half), np.float32), sin, np.zeros((S, HD - ROPE_DIM), np.float32)], axis=1)
        groups.append(np.stack([np.concatenate([t, t], axis=1) for t in (c, s_lo, s_hi)]))
    return jnp.asarray(np.stack(groups))


def _rotate(x, c, lo, hi, sign):
    tile = lambda t: jnp.tile(t, (1, DIL_W // 128))
    return (x * tile(c) + pltpu.roll(x, DIL_W - ROPE_DIM // 2, 1) * (tile(lo) * sign)
            + pltpu.roll(x, ROPE_DIM // 2, 1) * (tile(hi) * sign))


def _table_specs(g):
    return [pl.BlockSpec((None, None, TM, 128), lambda i, k=k: (g, k, i, 0)) for k in range(3)]


def _rope_fwd(g, p_qkv, tables):
    def body(x_ref, c_ref, lo_ref, hi_ref, o_ref):
        c, lo, hi = c_ref[...], lo_ref[...], hi_ref[...]
        for part in range(2):
            cols = slice(part * DIL_W, (part + 1) * DIL_W)
            o_ref[:, cols] = _rotate(x_ref[:, cols], c, lo, hi, 1.0).astype(bf16)
        o_ref[:, 2 * DIL_W:] = x_ref[:, 2 * DIL_W:].astype(bf16)

    return pl.pallas_call(
        body, grid=(S // TM,), in_specs=[_row(QKV_W)] + _table_specs(g), out_specs=_row(QKV_W),
        out_shape=SDS((S, QKV_W), bf16), name=f"rope_fwd_{g}",
        compiler_params=_params(("parallel",), 6 * TM * QKV_W + 12 * TM * 128, 24 * TM * QKV_W))(p_qkv, tables, tables, tables)


def _rope_bwd(g, dq, dk, dv, tables):
    def body(dq_ref, dk_ref, dv_ref, c_ref, lo_ref, hi_ref, o_ref):
        c, lo, hi = c_ref[...], lo_ref[...], hi_ref[...]
        o_ref[:, :DIL_W] = _rotate(dq_ref[...], c, lo, hi, -1.0).astype(bf16)
        o_ref[:, DIL_W:2 * DIL_W] = _rotate(dk_ref[...], c, lo, hi, -1.0).astype(bf16)
        o_ref[:, 2 * DIL_W:] = dv_ref[...].astype(bf16)

    return pl.pallas_call(
        body, grid=(S // TM,), in_specs=[_row(DIL_W)] * 3 + _table_specs(g), out_specs=_row(QKV_W),
        out_shape=SDS((S, QKV_W), bf16), name=f"rope_bwd_{g}",
        compiler_params=_params(("parallel",), 6 * TM * QKV_W + 12 * TM * 128, 24 * TM * QKV_W))(dq, dk, dv, tables, tables, tables)


def _nt(a, b):
    return lax.dot_general(a, b, (((1,), (1,)), ((), ())), preferred_element_type=f32)


def _tn(a, b):
    return lax.dot_general(a, b, (((0,), (0,)), ((), ())), preferred_element_type=f32)


STEP_BLOCKS = 4
STEP_ROWS = STEP_BLOCKS * BLK


def _dil_prev(g, b):
    _, d = DIL_GROUPS[g]
    nb = S // d // BLK
    if nb == 1 or (b == 0 and nb <= STEP_BLOCKS):
        return None
    return "in" if b > 0 else "halo"


def _bnt(a, b):
    return lax.dot_general(a, b, (((2,), (2,)), ((0,), (0,))), preferred_element_type=f32)


def _bnn(a, b):
    return lax.dot_general(a, b, (((2,), (1,)), ((0,), (0,))), preferred_element_type=f32)


def _btn(a, b):
    return lax.dot_general(a, b, (((1,), (1,)), ((0,), (0,))), preferred_element_type=f32)


def _on_tail(x, tail, fn):
    if tail == x.shape[0]:
        return fn(x)
    return jnp.concatenate([x[:-tail], fn(x[-tail:])], axis=0)


def _heads(ref, part):
    n = ref.shape[0] // BLK
    return jnp.stack([ref[b * BLK:(b + 1) * BLK, part * DIL_W + h * HD:part * DIL_W + (h + 1) * HD]
                      for b in range(n) for h in range(SLOTS)])


def _dil_operands(g, qkv_ref, halo_ref):
    q, kc, vc = (_heads(qkv_ref, part) for part in range(3))
    qi = lax.broadcasted_iota(jnp.int32, (1, BLK, BLK), 1)
    kj = lax.broadcasted_iota(jnp.int32, (1, BLK, BLK), 2)
    with_prev = [b for b in range(STEP_BLOCKS) if _dil_prev(g, b) is not None]
    tail = SLOTS * len(with_prev)
    if not tail:
        return q, kc, vc, None, None, kj <= qi, None, 0
    assert with_prev == list(range(STEP_BLOCKS - len(with_prev), STEP_BLOCKS))
    inside = SLOTS * sum(_dil_prev(g, b) == "in" for b in with_prev)
    kp, vp, prev = kc[:inside], vc[:inside], jnp.broadcast_to(kj >= qi, (inside, BLK, BLK))
    if inside < tail:
        no_halo = jnp.where(pl.program_id(0) == 0, BLK + 1, 0)
        kp = jnp.concatenate([_heads(halo_ref, 1), kp], axis=0)
        vp = jnp.concatenate([_heads(halo_ref, 2), vp], axis=0)
        prev = jnp.concatenate([jnp.broadcast_to(kj >= qi + no_halo, (SLOTS, BLK, BLK)), prev], axis=0)
    return q, kc, vc, kp, vp, kj <= qi, prev, tail


def _dil_in_specs(g, n_aux):
    step = lambda w: pl.BlockSpec((STEP_ROWS, w), lambda i: (i, 0))
    halo = [pl.BlockSpec((BLK, QKV_W), lambda i: (jnp.maximum(i * STEP_BLOCKS - 1, 0), 0))]
    needs_halo = _dil_prev(g, 0) == "halo"
    return [step(QKV_W)] + (halo if needs_halo else []) + [step(DIL_W)] * n_aux, needs_halo


def _dil_fwd(g, qkv):
    in_specs, needs_halo = _dil_in_specs(g, 0)

    def body(*refs):
        qkv_ref, halo_ref = refs[0], refs[1] if needs_halo else None
        o_ref, lse_ref = refs[-2:]
        q, kc, vc, kp, vp, cur, prev, tail = _dil_operands(g, qkv_ref, halo_ref)
        sc = jnp.where(cur, _bnt(q, kc) * SCALE, NEG)
        m = jnp.max(sc, axis=-1, keepdims=True)
        if tail:
            sp = jnp.where(prev, _bnt(q[-tail:], kp) * SCALE, NEG)
            m = _on_tail(m, tail, lambda t: jnp.maximum(t, jnp.max(sp, axis=-1, keepdims=True)))
            pp = jnp.exp(sp - m[-tail:])
        pc = jnp.exp(sc - m)
        den = jnp.sum(pc, axis=-1, keepdims=True)
        if tail:
            den = _on_tail(den, tail, lambda t: t + jnp.sum(pp, axis=-1, keepdims=True))
        inv = 1.0 / den
        o = _bnn((pc * inv).astype(bf16), vc)
        if tail:
            o = _on_tail(o, tail, lambda t: t + _bnn((pp * inv[-tail:]).astype(bf16), vp))
        lse = m + jnp.log(den)
        for b in range(STEP_BLOCKS):
            for h in range(SLOTS):
                rows, hs = slice(b * BLK, (b + 1) * BLK), slice(h * HD, (h + 1) * HD)
                o_ref[rows, hs] = o[SLOTS * b + h]
                lse_ref[rows, hs] = jnp.broadcast_to(lse[SLOTS * b + h], (BLK, HD))

    out = pl.BlockSpec((STEP_ROWS, DIL_W), lambda i: (i, 0))
    return pl.pallas_call(
        body, grid=(S // STEP_ROWS,), in_specs=in_specs, out_specs=[out, out], out_shape=[SDS((S, DIL_W), f32)] * 2,
        name=f"dil_fwd_{g}", compiler_params=_params(("parallel",), 12 * STEP_ROWS * DIL_W, 2 << 20),
    )(*([qkv] * (2 if needs_halo else 1)))


def _dil_combine(outs, lses):
    def body(o0, o1, o2, l0, l1, l2, out_ref, lse_ref, so1, so2, sl1, sl2):
        for (_, d), src, dst in ((DIL_GROUPS[1], o1, so1), (DIL_GROUPS[2], o2, so2),
                                 (DIL_GROUPS[1], l1, sl1), (DIL_GROUPS[2], l2, sl2)):
            rows = S // d
            for r in range(d):
                dst[pl.ds(r, rows, stride=d), :] = src[r * rows:(r + 1) * rows, :]
        a, b, c = l0[...], sl1[...], sl2[...]
        m = jnp.maximum(jnp.maximum(a, b), c)
        ea, eb, ec = jnp.exp(a - m), jnp.exp(b - m), jnp.exp(c - m)
        z = ea + eb + ec
        inv = 1.0 / z
        out_ref[...] = (ea * inv) * o0[...] + (eb * inv) * so1[...] + (ec * inv) * so2[...]
        lse_ref[...] = m + jnp.log(z)

    blk = pl.BlockSpec((S, 128), lambda c: (0, c))
    return pl.pallas_call(
        body, grid=(DIL_W // 128,), in_specs=[blk] * 6, out_specs=[blk] * 2,
        out_shape=[SDS((S, DIL_W), f32)] * 2, scratch_shapes=[pltpu.VMEM((S, 128), f32)] * 4, name="dil_combine",
        compiler_params=_params(("parallel",), 32 * S * 128, 32 * S * 128))(*outs, *lses)


def _dil_bwd(g, qkv, d_out, delta, lse):
    in_specs, needs_halo = _dil_in_specs(g, 3)

    def body(*refs):
        qkv_ref, halo_ref = refs[0], refs[1] if needs_halo else None
        do_ref, dl_ref, lse_ref, dq_ref, dk_ref, dv_ref = refs[-6:]
        q, kc, vc, kp, vp, cur, prev, tail = _dil_operands(g, qkv_ref, halo_ref)
        tiles = [(slice(b * BLK, (b + 1) * BLK), h) for b in range(STEP_BLOCKS) for h in range(SLOTS)]
        do = jnp.stack([do_ref[rows, h * HD:(h + 1) * HD] for rows, h in tiles]).astype(bf16)
        lse = jnp.stack([lse_ref[rows, h * HD:h * HD + 1] for rows, h in tiles])
        delta = jnp.stack([dl_ref[rows, h * HD:h * HD + 1] for rows, h in tiles])

        def probs(q, k, mask, lse, do, v, delta):
            p = jnp.exp(jnp.where(mask, _bnt(q, k) * SCALE, NEG) - lse)
            ds = p * (_bnt(do, v) - delta) * SCALE
            return p.astype(bf16), ds.astype(bf16)

        p, ds = probs(q, kc, cur, lse, do, vc, delta)
        dq, dk, dv = _bnn(ds, kc), _btn(ds, q), _btn(p, do)
        if tail:
            p, ds = probs(q[-tail:], kp, prev, lse[-tail:], do[-tail:], vp, delta[-tail:])
            dq = _on_tail(dq, tail, lambda t: t + _bnn(ds, kp))
            dk_p, dv_p = _btn(ds, q[-tail:]), _btn(p, do[-tail:])
            inside = tail - SLOTS if needs_halo else tail
            pad = jnp.zeros((len(tiles) - inside, BLK, HD), f32)
            dk = dk + jnp.concatenate([dk_p[tail - inside:], pad], axis=0)
            dv = dv + jnp.concatenate([dv_p[tail - inside:], pad], axis=0)
        first = pl.multiple_of(pl.program_id(0) * STEP_ROWS, STEP_ROWS)
        for t, (rows, h) in enumerate(tiles):
            hs = slice(h * HD, (h + 1) * HD)
            own = pl.ds(pl.multiple_of(first + rows.start, BLK), BLK)
            dq_ref[rows, hs] = dq[t]
            dk_ref[own, hs] = dk[t]
            dv_ref[own, hs] = dv[t]
        if needs_halo:
            before = pl.ds(pl.multiple_of(jnp.maximum(first - BLK, 0), BLK), BLK)
            for h in range(SLOTS):
                hs = slice(h * HD, (h + 1) * HD)
                dk_ref[before, hs] += dk_p[h]
                dv_ref[before, hs] += dv_p[h]

    whole = pl.BlockSpec((S, DIL_W), lambda i: (0, 0))
    return pl.pallas_call(
        body, grid=(S // STEP_ROWS,), in_specs=in_specs,
        out_specs=[pl.BlockSpec((STEP_ROWS, DIL_W), lambda i: (i, 0)), whole, whole],
        out_shape=[SDS((S, DIL_W), f32)] * 3, name=f"dil_bwd_{g}",
        compiler_params=_params(("arbitrary",), 20 * STEP_ROWS * DIL_W + 8 * S * DIL_W, 2 << 20),
    )(*([qkv] * (2 if needs_halo else 1)), d_out, delta, lse)


def _scan_rows(x, reverse):
    row = lax.broadcasted_iota(jnp.int32, x.shape, 0)
    k = 1
    while k < S:
        if reverse:
            x = x + jnp.where(row < S - k, pltpu.roll(x, S - k, 0), 0.0)
        else:
            x = x + jnp.where(row >= k, pltpu.roll(x, k, 0), 0.0)
        k *= 2
    return x


N_PAIR = N_FOX // 2
_PAIR_Q = pl.BlockSpec((None, S, 128), lambda p: (p, 0, 0))
_PAIR_K = pl.BlockSpec((None, 8, S), lambda p: (p, 0, 0))


def _forget_fwd(fz, b128):
    def body(z_ref, b_ref, fq_ref, fk_ref):
        z = z_ref[...] + b_ref[...]
        logf = jnp.minimum(z, 0.0) - jnp.log1p(jnp.exp(-jnp.abs(z)))
        f_cum = _scan_rows(logf, reverse=False)
        f_cum_t = f_cum.T
        fq_ref[...] = jnp.zeros_like(fq_ref)
        fk_ref[...] = jnp.zeros_like(fk_ref)
        for p in range(N_PAIR):
            fq_ref[p, :, 0:2] = f_cum[:, 2 * p:2 * p + 2]
            fk_ref[p, 0:2, :] = f_cum_t[2 * p:2 * p + 2, :]

    return pl.pallas_call(
        body, grid=(1,), in_specs=[pl.BlockSpec((S, 128), lambda i: (0, 0)), _vec(128)],
        out_specs=[pl.BlockSpec((N_PAIR, S, 128), lambda i: (0, 0, 0)), pl.BlockSpec((N_PAIR, 8, S), lambda i: (0, 0, 0))],
        out_shape=[SDS((N_PAIR, S, 128), f32), SDS((N_PAIR, 8, S), f32)], name="forget_fwd",
        compiler_params=_params(("arbitrary",), 24 * S * 128, 24 * S * 128))(fz, b128)


def _forget_bwd(fz, b128, d_f_cols, d_f_rows):
    def body(z_ref, b_ref, dfc_ref, dfr_ref, dz_ref, db_ref, df_sc):
        z = z_ref[...] + b_ref[...]
        df_sc[...] = jnp.zeros_like(df_sc)
        for p in range(N_PAIR):
            df_sc[:, 2 * p:2 * p + 2] = dfr_ref[p, :, 0:2] + dfc_ref[p].T[:, 0:2]
        dz = _scan_rows(df_sc[...], reverse=True) * jax.nn.sigmoid(-z)
        dz_ref[...] = dz
        db_ref[...] = jnp.sum(dz, axis=0, keepdims=True)

    full = pl.BlockSpec((S, 128), lambda i: (0, 0))
    return pl.pallas_call(
        body, grid=(1,),
        in_specs=[full, _vec(128), pl.BlockSpec((N_PAIR, 8, S), lambda i: (0, 0, 0)), pl.BlockSpec((N_PAIR, S, 128), lambda i: (0, 0, 0))],
        out_specs=[full, _vec(128)], out_shape=[SDS((S, 128), f32), SDS((1, 128), f32)],
        scratch_shapes=[pltpu.VMEM((S, 128), f32)], name="forget_bwd",
        compiler_params=_params(("arbitrary",), 32 * S * 128, 24 * S * 128))(fz, b128, d_f_cols, d_f_rows)


def _fox_scores(q_ref, k_ref, fq_ref, fk_ref, qi, hh):
    n = (qi + 1) * TQ
    rows, hs = slice(qi * TQ, n), slice(hh * HD, (hh + 1) * HD)
    q = q_ref[rows, hs] * SCALE
    s = _nt(q, k_ref[0:n, hs]) + (fq_ref[rows, hh:hh + 1] - fk_ref[hh:hh + 1, 0:n])
    below = lax.broadcasted_iota(jnp.int32, (TQ, TQ), 1) <= lax.broadcasted_iota(jnp.int32, (TQ, TQ), 0)
    diag = jnp.where(below, s[:, n - TQ:], NEG)
    return diag if qi == 0 else jnp.concatenate([s[:, :n - TQ], diag], axis=1)


def _pair_cols(first):
    return pl.BlockSpec((S, 128), lambda p: (0, first + p))


def _fox_fwd(vr, fq, fk):
    def body(q_ref, k_ref, v_ref, fq_ref, fk_ref, o_ref, lse_ref):
        lse_ref[...] = jnp.zeros_like(lse_ref)
        for hh in range(2):
            hs = slice(hh * HD, (hh + 1) * HD)
            for qi in range(S // TQ):
                n = (qi + 1) * TQ
                rows = slice(qi * TQ, n)
                s = _fox_scores(q_ref, k_ref, fq_ref, fk_ref, qi, hh)
                m = jnp.max(s, axis=-1, keepdims=True)
                p = jnp.exp(s - m)
                den = jnp.sum(p, axis=-1, keepdims=True)
                o_ref[rows, hs] = jnp.dot((p * (1.0 / den)).astype(bf16), v_ref[0:n, hs], preferred_element_type=f32)
                lse_ref[rows, hh:hh + 1] = m + jnp.log(den)

    return pl.pallas_call(
        body, grid=(N_PAIR,), in_specs=[_pair_cols(0), _pair_cols(N_PAIR), _pair_cols(2 * N_PAIR), _PAIR_Q, _PAIR_K],
        out_specs=[_pair_cols(0), _PAIR_Q], out_shape=[SDS((S, FOX_W), f32), SDS((N_PAIR, S, 128), f32)],
        name="fox_fwd", compiler_params=_params(("parallel",), 12 * S * 128, 16 * TQ * S),
    )(vr, vr, vr, fq, fk)


def _fox_bwd(vr, fq, fk, lse, d_out, delta):
    def body(q_ref, k_ref, v_ref, do_ref, fq_ref, fk_ref, lse_ref, dl_ref, dq_ref, dk_ref, dv_ref, dfc_ref, dfr_ref,
             dk_sc, dv_sc):
        dfc_ref[...] = jnp.zeros_like(dfc_ref)
        dfr_ref[...] = jnp.zeros_like(dfr_ref)
        for hh in range(2):
            hs = slice(hh * HD, (hh + 1) * HD)
            dk_sc[...] = jnp.zeros_like(dk_sc)
            dv_sc[...] = jnp.zeros_like(dv_sc)
            for qi in range(S // TQ):
                n = (qi + 1) * TQ
                rows = slice(qi * TQ, n)
                q, do, k, v = q_ref[rows, hs], do_ref[rows, hs], k_ref[0:n, hs], v_ref[0:n, hs]
                p = jnp.exp(_fox_scores(q_ref, k_ref, fq_ref, fk_ref, qi, hh) - lse_ref[rows, hh:hh + 1])
                ds = p * (_nt(do, v) - dl_ref[rows, hh:hh + 1])
                dsb = ds.astype(bf16)
                dq_ref[rows, hs] = jnp.dot(dsb, k, preferred_element_type=f32) * SCALE
                dk_sc[0:n, :] += _tn(dsb, q) * SCALE
                dv_sc[0:n, :] += _tn(p.astype(bf16), do)
                dfc_ref[hh:hh + 1, 0:n] -= jnp.sum(ds, axis=0, keepdims=True)
                dfr_ref[rows, hh:hh + 1] = jnp.sum(ds, axis=-1, keepdims=True)
            dk_ref[:, hs] = dk_sc[...]
            dv_ref[:, hs] = dv_sc[...]

    cols = [_pair_cols(k * N_PAIR) for k in range(3)]
    return pl.pallas_call(
        body, grid=(N_PAIR,), in_specs=cols + [_pair_cols(0), _PAIR_Q, _PAIR_K, _PAIR_Q, _PAIR_Q],
        out_specs=[_pair_cols(0)] * 3 + [_PAIR_K, _PAIR_Q],
        out_shape=[SDS((S, FOX_W), f32)] * 3 + [SDS((N_PAIR, 8, S), f32), SDS((N_PAIR, S, 128), f32)],
        scratch_shapes=[pltpu.VMEM((S, HD), f32)] * 2, name="fox_bwd",
        compiler_params=_params(("parallel",), 32 * S * 128, 24 * TQ * S),
    )(vr, vr, vr, d_out, fq, fk, lse, delta)


def _merge_fwd(out_a, out_b, w_a, w_b, gf):
    cw = D // N_SHARD

    def body(oa_ref, ob_ref, wa_ref, wb_ref, ga_ref, gb_ref, ya_ref, yb_ref, mg_ref):
        oa, ob = oa_ref[...].astype(bf16), ob_ref[...].astype(bf16)
        for j in range(N_SHARD):
            cols = slice(j * cw, (j + 1) * cw)
            ya = jnp.dot(oa, wa_ref[j], preferred_element_type=f32)
            yb = jnp.dot(ob, wb_ref[j], preferred_element_type=f32)
            ya_ref[:, cols] = ya
            yb_ref[:, cols] = yb
            mg_ref[:, cols] = (jax.nn.sigmoid(ga_ref[:, cols]) * ya + jax.nn.sigmoid(gb_ref[:, cols]) * yb).astype(bf16)

    full = lambda a: pl.BlockSpec(a.shape, lambda i: (0, 0, 0))
    return pl.pallas_call(
        body, grid=(S // TM,),
        in_specs=[_row(DIL_W), _row(FOX_W), full(w_a), full(w_b), _row(D), pl.BlockSpec((TM, D), lambda i: (i, 1))],
        out_specs=[_row(D)] * 3, out_shape=[SDS((S, D), f32), SDS((S, D), f32), SDS((S, D), bf16)], name="merge_fwd",
        compiler_params=_params(("parallel",), 22 * TM * D + 2 * (DIL_W + FOX_W) * D, 16 * TM * D),
    )(out_a, out_b, w_a, w_b, gf, gf)


def _merge_bwd(d_mix, w_out, ya, yb, gf):
    def body(dx_ref, w_ref, ya_ref, yb_ref, ga_ref, gb_ref, dya_ref, dyb_ref, dg_ref):
        dm = _nt(dx_ref[...], w_ref[...])
        sa, sb = jax.nn.sigmoid(ga_ref[...]), jax.nn.sigmoid(gb_ref[...])
        dya_ref[...] = (dm * sa).astype(bf16)
        dyb_ref[...] = (dm * sb).astype(bf16)
        dg_ref[:, :D] = (dm * ya_ref[...] * sa * (1.0 - sa)).astype(bf16)
        dg_ref[:, D:] = (dm * yb_ref[...] * sb * (1.0 - sb)).astype(bf16)

    return pl.pallas_call(
        body, grid=(S // TM,),
        in_specs=[_row(D), _whole(w_out)] + [_row(D)] * 3 + [pl.BlockSpec((TM, D), lambda i: (i, 1))],
        out_specs=[_row(D), _row(D), _row(2 * D)],
        out_shape=[SDS((S, D), bf16), SDS((S, D), bf16), SDS((S, 2 * D), bf16)], name="proj_out_bwd_merge",
        compiler_params=_params(("parallel",), 26 * TM * D + 2 * D * D, 28 * TM * D))(d_mix, w_out, ya, yb, gf, gf)


def _branch_bwd(d_ya, d_yb, w_a, w_b, out_a, out_b):
    cw = D // N_SHARD

    def body(dya_ref, dyb_ref, wa_ref, wb_ref, oa_ref, ob_ref, doa_ref, dla_ref, dob_ref, dlb_ref):
        doa = jnp.zeros((TM, DIL_W), f32)
        dob = jnp.zeros((TM, FOX_W), f32)
        for j in range(N_SHARD):
            cols = slice(j * cw, (j + 1) * cw)
            doa += _nt(dya_ref[:, cols], wa_ref[j])
            dob += _nt(dyb_ref[:, cols], wb_ref[j])
        doa_ref[...] = doa
        dob_ref[...] = dob.astype(bf16)
        prod_a = doa * oa_ref[...]
        for h in range(SLOTS):
            hs = slice(h * HD, (h + 1) * HD)
            dla_ref[:, hs] = jnp.broadcast_to(jnp.sum(prod_a[:, hs], axis=-1, keepdims=True), (TM, HD))
        prod_b = dob * ob_ref[...]
        dlb_ref[...] = jnp.zeros_like(dlb_ref)
        for h in range(N_FOX):
            dlb_ref[h // 2, :, h % 2:h % 2 + 1] = jnp.sum(prod_b[:, h * HD:(h + 1) * HD], axis=-1, keepdims=True)

    full = lambda a: pl.BlockSpec(a.shape, lambda i: (0, 0, 0))
    return pl.pallas_call(
        body, grid=(S // TM,),
        in_specs=[_row(D), _row(D), full(w_a), full(w_b), _row(DIL_W), _row(FOX_W)],
        out_specs=[_row(DIL_W), _row(DIL_W), _row(FOX_W), pl.BlockSpec((N_PAIR, TM, 128), lambda i: (0, i, 0))],
        out_shape=[SDS((S, DIL_W), f32), SDS((S, DIL_W), f32), SDS((S, FOX_W), bf16), SDS((N_PAIR, S, 128), f32)],
        name="branch_bwd", compiler_params=_params(("parallel",), 8 * TM * D + 2 * (DIL_W + FOX_W) * D, 8 * TM * D),
    )(d_ya, d_yb, w_a, w_b, out_a, out_b)


def _branch_grads(out_a, out_b, d_ya, d_yb):
    cw = D // N_SHARD

    def body(oa_ref, ob_ref, dya_ref, dyb_ref, ga_ref, gb_ref):
        ga_ref[...] = _tn(oa_ref[...].astype(bf16), dya_ref[...]).astype(bf16)
        gb_ref[...] = _tn(ob_ref[...].astype(bf16), dyb_ref[...]).astype(bf16)

    whole = lambda w: pl.BlockSpec((S, w), lambda j: (0, 0))
    cols = pl.BlockSpec((S, cw), lambda j: (0, j))
    return pl.pallas_call(
        body, grid=(N_SHARD,), in_specs=[whole(DIL_W), whole(FOX_W), cols, cols],
        out_specs=[pl.BlockSpec((None, DIL_W, cw), lambda j: (j, 0, 0)), pl.BlockSpec((None, FOX_W, cw), lambda j: (j, 0, 0))],
        out_shape=[SDS((N_SHARD, DIL_W, cw), bf16), SDS((N_SHARD, FOX_W, cw), bf16)], name="grad_w_proj_ab",
        compiler_params=_params(("parallel",), 4 * S * (DIL_W + FOX_W) + 4 * S * cw + 4 * (DIL_W + FOX_W) * cw,
                                4 * S * (DIL_W + FOX_W)))(out_a, out_b, d_ya, d_yb)


FF_TN = F_FF // 2
FF_TM = 1024


def _ffn_fwd(h, w_gate_t, w_up_t):
    def body(h_ref, wg_ref, wu_ref, g_ref, u_ref, a_ref):
        hb = h_ref[...]
        g = _nt(hb, wg_ref[...])
        u = _nt(hb, wu_ref[...])
        g_ref[...] = g
        u_ref[...] = u
        a_ref[...] = (g * jax.nn.sigmoid(g) * u).astype(bf16)

    tile = pl.BlockSpec((FF_TM, FF_TN), lambda j, i: (i, j))
    wspec = pl.BlockSpec((FF_TN, D), lambda j, i: (j, 0))
    return pl.pallas_call(
        body, grid=(F_FF // FF_TN, S // FF_TM),
        in_specs=[pl.BlockSpec((FF_TM, D), lambda j, i: (i, 0)), wspec, wspec], out_specs=[tile] * 3,
        out_shape=[SDS((S, F_FF), f32), SDS((S, F_FF), f32), SDS((S, F_FF), bf16)], name="ffn_fwd",
        compiler_params=_params(("parallel", "parallel"), 2 * FF_TM * D + 4 * D * FF_TN + 10 * FF_TM * FF_TN, 16 * FF_TM * FF_TN),
    )(h, w_gate_t, w_up_t)


def _ffn_bwd_act(d_ff, w_down, g_act, u_act):
    def body(d_ref, wd_ref, g_ref, u_ref, dg_ref, du_ref):
        da = _nt(d_ref[...], wd_ref[...])
        g = g_ref[...]
        sg = jax.nn.sigmoid(g)
        du_ref[...] = (da * g * sg).astype(bf16)
        dg_ref[...] = (da * u_ref[...] * sg * (1.0 + g * (1.0 - sg))).astype(bf16)

    tile = pl.BlockSpec((FF_TM, FF_TN), lambda j, i: (i, j))
    return pl.pallas_call(
        body, grid=(F_FF // FF_TN, S // FF_TM),
        in_specs=[pl.BlockSpec((FF_TM, D), lambda j, i: (i, 0)), pl.BlockSpec((FF_TN, D), lambda j, i: (j, 0)), tile, tile],
        out_specs=[tile, tile], out_shape=[SDS((S, F_FF), bf16)] * 2, name="ffn_bwd_act",
        compiler_params=_params(("parallel", "parallel"), 2 * FF_TM * D + 2 * D * FF_TN + 12 * FF_TM * FF_TN, 16 * FF_TM * FF_TN),
    )(d_ff, w_down, g_act, u_act)


def _row_tile(rows):
    return next(t for t in (376, 128, 176, 64, 32, 16, 8) if rows % t == 0)


def _adamw_math(w, g, m, v):
    c1 = 1.0 - ADAM_B1 ** ADAM_STEP
    c2 = 1.0 - ADAM_B2 ** ADAM_STEP
    m_new = ADAM_B1 * m + (1.0 - ADAM_B1) * g
    v_new = ADAM_B2 * v + (1.0 - ADAM_B2) * (g * g)
    return -ADAM_LR * ((m_new / c1) / (jnp.sqrt(v_new / c2) + ADAM_EPS) + ADAM_WD * w), m_new, v_new


def _adamw(w, g, m, v, name):
    rows, cols = w.shape
    tm = _row_tile(rows)

    def body(w_ref, g_ref, m_ref, v_ref, d_ref, nm_ref, nv_ref):
        d_ref[...], nm_ref[...], nv_ref[...] = _adamw_math(w_ref[...], g_ref[...], m_ref[...], v_ref[...])

    spec = pl.BlockSpec((tm, cols), lambda i: (i, 0))
    return pl.pallas_call(
        body, grid=(rows // tm,), in_specs=[spec] * 4, out_specs=[spec] * 3, out_shape=[SDS(w.shape, f32)] * 3,
        name=name, compiler_params=_params(("parallel",), 28 * tm * cols, 16 * tm * cols))(w, g, m, v)


def _adamw_halves(w, g_mine, g_theirs, m, v, name):
    cols = w.shape[1]
    tm = _row_tile(g_mine.shape[0])
    per_half = g_mine.shape[0] // tm
    assert 2 * g_mine.shape[0] - w.shape[0] < tm
    core = lax.axis_index("c").astype(jnp.int32).reshape(1)

    def body(c_ref, w_ref, gm_ref, gt_ref, m_ref, v_ref, g_ref, d_ref, nm_ref, nv_ref):
        mine = pl.program_id(0) // per_half == c_ref[0]
        g = jnp.where(mine, gm_ref[...], gt_ref[...])
        g_ref[...] = g
        d_ref[...], nm_ref[...], nv_ref[...] = _adamw_math(w_ref[...], g, m_ref[...], v_ref[...])

    spec = pl.BlockSpec((tm, cols), lambda i, c_ref: (i, 0))
    in_half = lambda i, first: jnp.clip(i - first * per_half, 0, per_half - 1)
    grid_spec = pltpu.PrefetchScalarGridSpec(
        num_scalar_prefetch=1, grid=(2 * per_half,),
        in_specs=[spec, pl.BlockSpec((tm, cols), lambda i, c_ref: (in_half(i, c_ref[0]), 0)),
                  pl.BlockSpec((tm, cols), lambda i, c_ref: (in_half(i, 1 - c_ref[0]), 0)), spec, spec],
        out_specs=[spec] * 4)
    return pl.pallas_call(
        body, grid_spec=grid_spec, out_shape=[SDS(w.shape, f32)] * 4, name=name,
        compiler_params=_params(("parallel",), 36 * tm * cols, 16 * tm * cols))(core, w, g_mine, g_theirs, m, v)


_ANY = pl.BlockSpec(memory_space=pl.ANY)


def _place():
    x, y, c = lax.axis_index("x"), lax.axis_index("y"), lax.axis_index("c")
    chips = [(1 - x, y), (x, 1 - y), (1 - x, 1 - y)]
    return x, y, c, chips


def _halved(t):
    return t.reshape(t.shape[:-2] + (2, t.shape[-2] // 2, t.shape[-1]))


def _gather_body(src, out, send_ici, recv_ici, send_d2d, recv_d2d):
    x, y, c, chips = _place()
    sibling = (x, y, 1 - c)
    me_j = 2 * x + y
    sends = []
    for a in range(len(src)):
        for p in range(3):
            cp = pltpu.make_async_remote_copy(
                src_ref=src[a].at[c], dst_ref=out[a].at[me_j, c], send_sem=send_ici.at[a, p],
                recv_sem=recv_ici.at[a, p], device_id=(*chips[p], c), device_id_type=MESH)
            cp.start()
            sends.append(cp)
    for a in range(len(src)):
        for p, (px, py) in enumerate(chips):
            blk = out[a].at[2 * px + py, c]
            pltpu.make_async_remote_copy(
                src_ref=blk, dst_ref=blk, send_sem=send_ici.at[a, p], recv_sem=recv_ici.at[a, p],
                device_id=sibling, device_id_type=MESH).wait_recv()
            fw = pltpu.make_async_remote_copy(
                src_ref=blk, dst_ref=blk, send_sem=send_d2d.at[a, p], recv_sem=recv_d2d.at[a, p],
                device_id=sibling, device_id_type=MESH)
            fw.start()
            sends.append(fw)
    for a in range(len(src)):
        for p, (px, py) in enumerate(chips):
            blk = out[a].at[2 * px + py, 1 - c]
            pltpu.make_async_remote_copy(
                src_ref=blk, dst_ref=blk, send_sem=send_d2d.at[a, p], recv_sem=recv_d2d.at[a, p],
                device_id=sibling, device_id_type=MESH).wait_recv()
    for cp in sends:
        cp.wait_send()


def _handshake(peers):
    barrier = pltpu.get_barrier_semaphore()
    for peer in peers:
        pl.semaphore_signal(barrier, inc=1, device_id=peer, device_id_type=MESH)
    pl.semaphore_wait(barrier, len(peers))


_SEQUENCER = dict(axis_name="sequencer", num_cores=1)
GATHER_LATE_ID, SCATTER_EARLY_ID, SWAP_EARLY_ID, GATHER_FIRST_ID, SCATTER_LATE_ID, SHARE_SMALL_ID = 1, 2, 3, 4, 5, 6


def _all_gather_async(shards, after, name, collective_id):
    n, k = len(shards), len(after)

    def body(*refs):
        x, y, c, chips = _place()
        _handshake([(*chip, c) for chip in chips] + [(x, y, 1 - c)])
        _gather_body(refs[:n], refs[n + k:2 * n + k], *refs[2 * n + k:])

    return pl.kernel(
        body, out_type=[SDS((N_SHARD,) + t.shape, t.dtype) for t in shards],
        mesh=plsc.ScalarSubcoreMesh(**_SEQUENCER), scratch_types=[pltpu.SemaphoreType.DMA((n, 3))] * 4,
        compiler_params=pltpu.CompilerParams(collective_id=collective_id), name=name)(*shards, *after)


def _pair_swap(grads):
    n = len(grads)

    def body(*refs):
        src, out, send_sems, recv_sems = refs[:n], refs[n:2 * n], refs[2 * n], refs[2 * n + 1]
        x, y, c, _ = _place()
        copies = [pltpu.make_async_remote_copy(
            src_ref=src[a].at[:, 1 - c], dst_ref=out[a], send_sem=send_sems.at[a], recv_sem=recv_sems.at[a],
            device_id=(x, y, 1 - c), device_id_type=MESH) for a in range(n)]
        for cp in copies:
            cp.start()
        for cp in copies:
            cp.wait()

    return pl.pallas_call(
        body, in_specs=[_ANY] * n, out_specs=[_ANY] * n,
        out_shape=[SDS((N_SHARD,) + t.shape[2:], t.dtype) for t in grads],
        scratch_shapes=[pltpu.SemaphoreType.DMA((n,)), pltpu.SemaphoreType.DMA((n,))], name="pair_swap",
        compiler_params=pltpu.CompilerParams(has_side_effects=True))(*grads)


def _pair_swap_early(grads):
    n = len(grads)

    def body(*refs):
        src, out, send_sems, recv_sems = refs[:n], refs[n:2 * n], refs[2 * n], refs[2 * n + 1]
        x, y, c, _ = _place()
        _handshake([(x, y, 1 - c)])
        copies = [pltpu.make_async_remote_copy(
            src_ref=src[a].at[:, 1 - c], dst_ref=out[a], send_sem=send_sems.at[a], recv_sem=recv_sems.at[a],
            device_id=(x, y, 1 - c), device_id_type=MESH) for a in range(n)]
        for cp in copies:
            cp.start()
        for cp in copies:
            cp.wait()

    return pl.kernel(
        body, out_type=[SDS((N_SHARD,) + t.shape[2:], t.dtype) for t in grads],
        mesh=plsc.ScalarSubcoreMesh(**_SEQUENCER), scratch_types=[pltpu.SemaphoreType.DMA((n,))] * 2,
        compiler_params=pltpu.CompilerParams(collective_id=SWAP_EARLY_ID), name="pair_swap_early")(*grads)


def _scatter_parts(parts, name, collective_id):
    n = len(parts)

    def body(*refs):
        part, recv, send_sems, recv_sems = refs[:n], refs[n:2 * n], refs[2 * n], refs[2 * n + 1]
        x, y, c, chips = _place()
        _handshake([(*chip, c) for chip in chips])
        me_j = 2 * x + y
        sends = []
        for a in range(n):
            for p, (px, py) in enumerate(chips):
                cp = pltpu.make_async_remote_copy(
                    src_ref=part[a].at[2 * px + py], dst_ref=recv[a].at[me_j], send_sem=send_sems.at[a, p],
                    recv_sem=recv_sems.at[a, p], device_id=(px, py, c), device_id_type=MESH)
                cp.start()
                sends.append(cp)
        for a in range(n):
            for p, (px, py) in enumerate(chips):
                slot = recv[a].at[2 * px + py]
                pltpu.make_async_remote_copy(
                    src_ref=slot, dst_ref=slot, send_sem=send_sems.at[a, p], recv_sem=recv_sems.at[a, p],
                    device_id=(px, py, c), device_id_type=MESH).wait_recv()
        for cp in sends:
            cp.wait_send()

    return pl.kernel(
        body, out_type=[SDS(t.shape, t.dtype) for t in parts],
        mesh=plsc.ScalarSubcoreMesh(**_SEQUENCER), scratch_types=[pltpu.SemaphoreType.DMA((n, 3))] * 2,
        compiler_params=pltpu.CompilerParams(collective_id=collective_id), name=name)(*parts)


def _pair_sum(grads, other, name):
    _, _, rows, cols = grads.shape
    tr = _row_tile(rows)
    core = lax.axis_index("c").astype(jnp.int32).reshape(1)

    def body(c_ref, g_ref, o_ref, out_ref):
        out_ref[...] = (g_ref[...].astype(f32) + o_ref[...].astype(f32)).astype(bf16)

    grid_spec = pltpu.PrefetchScalarGridSpec(
        num_scalar_prefetch=1, grid=(N_SHARD, rows // tr),
        in_specs=[pl.BlockSpec((None, None, tr, cols), lambda j, i, c_ref: (j, c_ref[0], i, 0)),
                  pl.BlockSpec((None, tr, cols), lambda j, i, c_ref: (j, i, 0))],
        out_specs=pl.BlockSpec((None, tr, cols), lambda j, i, c_ref: (j, i, 0)))
    return pl.pallas_call(
        body, grid_spec=grid_spec, out_shape=SDS((N_SHARD, rows, cols), bf16), name=name,
        compiler_params=_params(("parallel", "parallel"), 10 * tr * cols, 12 * tr * cols))(core, grads, other)


def _share_small(small):
    def body(small_ref, small_all_ref, ssend, srecv, local_sem):
        x, y, c, _ = _place()
        flip = lambda a, bit: 1 - a if bit else a
        peers = [(flip(x, k & 4), flip(y, k & 2), flip(c, k & 1)) for k in range(1, 8)]
        _handshake(peers)
        me_dev = 4 * x + 2 * y + c
        own = pltpu.make_async_copy(small_ref, small_all_ref.at[me_dev], local_sem)
        own.start()
        sends = []
        for k, to in enumerate(peers):
            cp = pltpu.make_async_remote_copy(
                src_ref=small_ref, dst_ref=small_all_ref.at[me_dev],
                send_sem=ssend.at[k], recv_sem=srecv.at[k], device_id=to, device_id_type=MESH)
            cp.start()
            sends.append(cp)
        for k, (px, py, pc) in enumerate(peers):
            slot = small_all_ref.at[4 * px + 2 * py + pc]
            pltpu.make_async_remote_copy(
                src_ref=slot, dst_ref=slot, send_sem=ssend.at[k], recv_sem=srecv.at[k],
                device_id=(px, py, pc), device_id_type=MESH).wait_recv()
        for cp in sends:
            cp.wait_send()
        own.wait()

    return pl.kernel(
        body, out_type=SDS((8, SMALL_ROWS, D), f32), mesh=plsc.ScalarSubcoreMesh(**_SEQUENCER),
        scratch_types=[pltpu.SemaphoreType.DMA((7,)), pltpu.SemaphoreType.DMA((7,)), pltpu.SemaphoreType.DMA],
        compiler_params=pltpu.CompilerParams(collective_id=SHARE_SMALL_ID), name="share_small")(small)


def _sum_partials(part, recv, name):
    _, rows, cols = recv.shape
    tr = _row_tile(rows)
    me = (2 * lax.axis_index("x") + lax.axis_index("y")).astype(jnp.int32).reshape(1)

    def body(me_ref, mine, r0, r1, r2, r3, out_ref):
        acc = None
        for j, r in enumerate((r0, r1, r2, r3)):
            term = jnp.where(me_ref[0] == j, mine[...], r[...]).astype(f32)
            acc = term if acc is None else acc + term
        out_ref[...] = acc

    slot = lambda j: pl.BlockSpec((None, tr, cols), lambda i, me_ref: (jnp.where(me_ref[0] == j, j ^ 1, j), i, 0))
    grid_spec = pltpu.PrefetchScalarGridSpec(
        num_scalar_prefetch=1, grid=(rows // tr,),
        in_specs=[pl.BlockSpec((None, tr, cols), lambda i, me_ref: (me_ref[0], i, 0)), slot(0), slot(1), slot(2), slot(3)],
        out_specs=pl.BlockSpec((tr, cols), lambda i, me_ref: (i, 0)))
    return pl.pallas_call(
        body, grid_spec=grid_spec, out_shape=SDS((rows, cols), f32), name=name,
        compiler_params=_params(("parallel",), 14 * tr * cols, 12 * tr * cols))(me, part, recv, recv, recv, recv)


def _sum_small(small_all):
    def body(small_ref, out_ref):
        tot = small_ref[0]
        for k in range(1, 8):
            tot = tot + small_ref[k]
        out_ref[...] = tot

    return pl.pallas_call(
        body, grid=(1,), in_specs=[pl.BlockSpec((8, SMALL_ROWS, D), lambda i: (0, 0, 0))],
        out_specs=pl.BlockSpec((SMALL_ROWS, D), lambda i: (0, 0)), out_shape=SDS((SMALL_ROWS, D), f32),
        name="sum_small", compiler_params=_params(("arbitrary",), 36 * SMALL_ROWS * D))(small_all)


def _swap_halves(halves, name):
    n = len(halves)

    def body(*refs):
        src, out, send_sems, recv_sems = refs[:n], refs[n:2 * n], refs[2 * n], refs[2 * n + 1]
        x, y, c, _ = _place()
        copies = [pltpu.make_async_remote_copy(
            src_ref=src[a], dst_ref=out[a], send_sem=send_sems.at[a], recv_sem=recv_sems.at[a],
            device_id=(x, y, 1 - c), device_id_type=MESH) for a in range(n)]
        for cp in copies:
            cp.start()
        for cp in copies:
            cp.wait()

    return pl.pallas_call(
        body, in_specs=[_ANY] * n, out_specs=[_ANY] * n, out_shape=[SDS(t.shape, f32) for t in halves],
        scratch_shapes=[pltpu.SemaphoreType.DMA((n,))] * 2, name=name,
        compiler_params=pltpu.CompilerParams(has_side_effects=True))(*halves)


def _kernel_layout(name, t):
    t = t[0]
    return jnp.swapaxes(t, 0, 1) if name in TRANSPOSED else t


def _harness_layout(name, t):
    if name in TRANSPOSED:
        t = jnp.swapaxes(t, 0, 1)
    return t[None]


def _pad_rows(t, rows):
    return t if t.shape[0] == rows else jnp.pad(t, ((0, rows - t.shape[0]), (0, 0)))


_QA, _KA, _VA, _QB, _F, _GAB = 0, 768, 1536, 2304, 3840, 3848


def _spans(a, b):
    return [(j, max(a, j * IN_SHARD) - j * IN_SHARD, max(a, j * IN_SHARD) - a,
             min(b, (j + 1) * IN_SHARD) - max(a, j * IN_SHARD))
            for j in range(N_SHARD) if max(a, j * IN_SHARD) < min(b, (j + 1) * IN_SHARD)]


_LANES = pl.BlockSpec((N_SHARD, IN_SHARD_PAD, 128), lambda c: (0, 0, c))


def _split_w_in(shards):
    group = [[(o + g * DIL_W, o + (g + 1) * DIL_W) for o in (_QA, _KA, _VA)] for g in range(3)]
    fox = [[(_QB + k * FOX_W, _QB + (k + 1) * FOX_W)] for k in range(3)]
    wanted = group + fox + [[(_QB, _F)], [(_F, _GAB)], [(_GAB, IN_COLS)]]
    rows = [sum(b - a for a, b in w) for w in wanted]
    rows[7] = 128

    def body(s_ref, *o_refs):
        for o_ref, want in zip(o_refs, wanted):
            at = 0
            for a, b in want:
                for j, src, off, n in _spans(a, b):
                    o_ref[at + off:at + off + n, :] = s_ref[j, src:src + n, :]
                at += b - a
        o_refs[7][N_FOX:, :] = jnp.zeros((128 - N_FOX, 128), bf16)

    return pl.pallas_call(
        body, grid=(D // 128,), in_specs=[_LANES], out_specs=[pl.BlockSpec((r, 128), lambda c: (0, c)) for r in rows],
        out_shape=[SDS((r, D), bf16) for r in rows], name="split_w_in",
        compiler_params=_params(("parallel",), 2 * 128 * (N_SHARD * IN_SHARD_PAD + sum(rows))))(shards)


def _join_w_in(g_a, g_fox, g_f, g_gab):
    parts = [(g_a[k], o, o + DIL_W) for o in (0, DIL_W, 2 * DIL_W) for k in range(3)]
    parts += [(t, 0, FOX_W) for t in g_fox] + [(g_f, 0, N_FOX), (g_gab, 0, 2 * D)]
    arrays = list(g_a) + list(g_fox) + [g_f, g_gab]
    index = {id(t): i for i, t in enumerate(arrays)}

    def body(*refs):
        o_ref = refs[-1]
        o_ref[:, IN_SHARD:, :] = jnp.zeros((N_SHARD, IN_SHARD_PAD - IN_SHARD, 128), bf16)
        at = 0
        for t, lo, hi in parts:
            src_ref = refs[index[id(t)]]
            for j, dst, off, n in _spans(at, at + hi - lo):
                o_ref[j, dst:dst + n, :] = src_ref[lo + off:lo + off + n, :].astype(bf16)
            at += hi - lo

    return pl.pallas_call(
        body, grid=(D // 128,), in_specs=[pl.BlockSpec((t.shape[0], 128), lambda c: (0, c)) for t in arrays],
        out_specs=_LANES, out_shape=SDS((N_SHARD, IN_SHARD_PAD, D), bf16), name="join_w_in",
        compiler_params=_params(("parallel",), 2 * 128 * (N_SHARD * IN_SHARD_PAD + sum(t.shape[0] for t in arrays))),
    )(*arrays)


def _full_weights(gathered):
    full = {n: t.reshape((N_SHARD,) + SHARD_SHAPE[n]) for n, t in gathered.items()}
    out = {}
    if "w_in" in full:
        pieces = _split_w_in(full["w_in"])
        out.update(w_a_t=pieces[0:3], w_fox_t=pieces[3:6], w_vr_t=pieces[6], w_f_t=pieces[7], w_gab_t=pieces[8])
    if "w_out" in full:
        out.update(
            w_a4=full["w_proj_a"],
            w_b4=full["w_proj_b"],
            w_out=full["w_out"].reshape(D, D),
            w_gate_t=full["w_ffn_gate"].reshape(F_FF, D),
            w_up_t=full["w_ffn_up"].reshape(F_FF, D),
            w_down=full["w_ffn_down"].reshape(F_FF, D))
    return out


def _sharded_grads(g):
    full = dict(w_in=_join_w_in(g["w_a_t"], g["w_fox_t"], g["w_f_t"], g["w_gab_t"]), w_proj_a=g["w_a4"],
                w_proj_b=g["w_b4"], w_out=g["w_out"], w_ffn_gate=g["w_gate_t"], w_ffn_up=g["w_up_t"],
                w_ffn_down=g["w_down"])
    return {n: _halved(full[n].reshape((N_SHARD,) + SHARD_SHAPE[n])) for n in W_NAMES}


def _local_step(x, target, wt, b_forget, g_mix_pre, g_mix_post, g_ffn_pre, g_ffn_post, late=None):
    tables = _rope_tables()
    b128 = jnp.pad(b_forget, ((0, 0), (0, 128 - N_FOX)))
    dils = tuple(d for _, d in DIL_GROUPS[1:])

    hs = _norm_fwd([x] + list(_perm_rows([x], dils, "perm_x")), g_mix_pre)
    h1 = hs[0]
    if callable(wt):
        wt = wt(h1)
    qkv = [_rope_fwd(g, _mm([(hs[g], wt["w_a_t"][g])], "nt", f32, tm=1024, tn=QKV_W, name=f"proj_a_{g}"), tables)
           for g in range(3)]
    vr = _mm([(h1, wt["w_vr_t"])], "nt", bf16, tm=1024, tn=VR_W // 2, name="proj_vr")
    gab = _mm([(h1, wt["w_gab_t"])], "nt", f32, tm=512, tn=2 * D, name="proj_gab")
    fz = _mm([(h1, wt["w_f_t"])], "nt", f32, tm=1024, tn=128, name="proj_f")
    dil = [_dil_fwd(g, qkv[g]) for g in range(3)]
    out_a, lse_a = _dil_combine([o for o, _ in dil], [l for _, l in dil])
    f_q, f_k = _forget_fwd(fz, b128)
    out_b, lse_b = _fox_fwd(vr, f_q, f_k)
    if late is not None:
        wt = {**wt, **late(out_b)}
    ya, yb, merged = _merge_fwd(out_a, out_b, wt["w_a4"], wt["w_b4"], gab)
    mix, x2, h3 = _resid_norm_fwd(x, merged, wt["w_out"], g_mix_post, g_ffn_pre)
    g_act, u_act, a_act = _ffn_fwd(h3, wt["w_gate_t"], wt["w_up_t"])
    sq_err, dy, d_ff, dg_ffn_post = _loss_head(x2, a_act, wt["w_down"], g_ffn_post, target)

    grads = {}
    d_g, d_u = _ffn_bwd_act(d_ff, wt["w_down"], g_act, u_act)
    grads["w_down"] = _mm([(a_act, d_ff)], "tn", bf16, tm=FF_TN, tn=D, name="grad_w_down")
    grads["w_gate_t"] = _mm([(d_g, h3)], "tn", bf16, tm=FF_TN, tn=D, name="grad_w_gate")
    grads["w_up_t"] = _mm([(d_u, h3)], "tn", bf16, tm=FF_TN, tn=D, name="grad_w_up")
    dx2, d_mix, dg_ffn_pre, dg_mix_post = _norm_bwd_mid(dy, d_g, d_u, wt["w_gate_t"], wt["w_up_t"], x2, mix,
                                                        g_ffn_pre, g_mix_post)

    grads["w_out"] = _mm([(merged, d_mix)], "tn", bf16, tm=D, tn=D, name="grad_w_out")
    d_ya, d_yb, d_gab = _merge_bwd(d_mix, wt["w_out"], ya, yb, gab)
    grads["w_a4"], grads["w_b4"] = _branch_grads(out_a, out_b, d_ya, d_yb)
    d_out_a, delta_a, d_out_b, delta_b = _branch_bwd(d_ya, d_yb, wt["w_a4"], wt["w_b4"], out_a, out_b)

    perm = _perm_rows([d_out_a, delta_a, lse_a], dils, "perm_dil_bwd")
    aux = [(d_out_a, delta_a, lse_a)] + [tuple(perm[k * len(dils) + i] for k in range(3)) for i in range(len(dils))]
    d_qkv = []
    for g in range(3):
        dq, dk, dv = _dil_bwd(g, qkv[g], *aux[g])
        d_qkv.append(_rope_bwd(g, dq, dk, dv, tables))
    *d_fox, d_f_cols, d_f_rows = _fox_bwd(vr, f_q, f_k, lse_b, d_out_b, delta_b)
    d_z, d_b128 = _forget_bwd(fz, b128, d_f_cols, d_f_rows)

    grads["w_a_t"] = [_mm([(d_qkv[g], hs[g])], "tn", bf16, tm=QKV_W, tn=D, name=f"grad_w_a_{g}") for g in range(3)]
    grads["w_fox_t"] = [_mm([(d_fox[k], h1)], "tn", bf16, tm=FOX_W, tn=D, name=f"grad_w_fox_{k}") for k in range(3)]
    grads["w_gab_t"] = _mm([(d_gab, h1)], "tn", bf16, tm=D, tn=D, name="grad_w_gab")
    grads["w_f_t"] = _mm([(d_z, h1)], "tn", bf16, tm=128, tn=D, name="grad_w_f")
    d_h1_nat = _mm([(d_qkv[0], wt["w_a_t"][0])] + list(zip(d_fox, wt["w_fox_t"]))
                   + [(d_gab, wt["w_gab_t"]), (d_z, wt["w_f_t"])], "nn", f32, tm=512, tn=D, name="proj_in_bwd")
    d_h1_dil = [_mm([(d_qkv[g], wt["w_a_t"][g])], "nn", f32, tm=1024, tn=D, name=f"proj_a_bwd_{g}") for g in (1, 2)]
    d_h1 = _unperm_sum(d_h1_nat, d_h1_dil, dils, "unperm_d_h1")
    grad_x, dg_mix_pre = _norm_bwd_in(dx2, d_h1, x, g_mix_pre)

    small = dict(b_forget=d_b128[:, :N_FOX], norm_mix_pre=dg_mix_pre, norm_mix_post=dg_mix_post,
                 norm_ffn_pre=dg_ffn_pre, norm_ffn_post=dg_ffn_post)
    grads["mid_backward"] = d_qkv[0]
    return sq_err, grad_x, grads, small


NORMS = ("norm_mix_pre", "norm_mix_post", "norm_ffn_pre", "norm_ffn_post")
ORDER = ("w_in", "w_proj_a", "w_proj_b", "w_out", "b_forget", "w_ffn_gate", "w_ffn_up", "w_ffn_down") + NORMS


def kernel(x, w_in, w_proj_a, w_proj_b, w_out, b_forget, w_ffn_gate, w_ffn_up, w_ffn_down, norm_mix_pre, norm_mix_post, norm_ffn_pre, norm_ffn_post, loss_target, m_w_in, m_w_proj_a, m_w_proj_b, m_w_out, m_b_forget, m_w_ffn_gate, m_w_ffn_up, m_w_ffn_down, m_norm_mix_pre, m_norm_mix_post, m_norm_ffn_pre, m_norm_ffn_post, v_w_in, v_w_proj_a, v_w_proj_b, v_w_out, v_b_forget, v_w_ffn_gate, v_w_ffn_up, v_w_ffn_down, v_norm_mix_pre, v_norm_mix_post, v_norm_ffn_pre, v_norm_ffn_post):
    given = dict(w_in=w_in, w_proj_a=w_proj_a, w_proj_b=w_proj_b, w_out=w_out, w_ffn_gate=w_ffn_gate,
                 w_ffn_up=w_ffn_up, w_ffn_down=w_ffn_down)
    given_m = dict(w_in=m_w_in, w_proj_a=m_w_proj_a, w_proj_b=m_w_proj_b, w_out=m_w_out, w_ffn_gate=m_w_ffn_gate,
                   w_ffn_up=m_w_ffn_up, w_ffn_down=m_w_ffn_down)
    given_v = dict(w_in=v_w_in, w_proj_a=v_w_proj_a, w_proj_b=v_w_proj_b, w_out=v_w_out, w_ffn_gate=v_w_ffn_gate,
                   w_ffn_up=v_w_ffn_up, w_ffn_down=v_w_ffn_down)
    w, m, v = ({n: _kernel_layout(n, t[n]) for n in W_NAMES} for t in (given, given_m, given_v))
    small_w = dict(b_forget=b_forget, norm_mix_pre=norm_mix_pre, norm_mix_post=norm_mix_post,
                   norm_ffn_pre=norm_ffn_pre, norm_ffn_post=norm_ffn_post)
    small_m = dict(b_forget=m_b_forget, norm_mix_pre=m_norm_mix_pre, norm_mix_post=m_norm_mix_post,
                   norm_ffn_pre=m_norm_ffn_pre, norm_ffn_post=m_norm_ffn_post)
    small_v = dict(b_forget=v_b_forget, norm_mix_pre=v_norm_mix_pre, norm_mix_post=v_norm_mix_post,
                   norm_ffn_pre=v_norm_ffn_pre, norm_ffn_post=v_norm_ffn_post)

    own = [_halved(_pad_rows(w[n].astype(bf16), SHARD_SHAPE[n][0])) for n in W_NAMES]
    chip = 2 * lax.axis_index("x") + lax.axis_index("y")
    exchanged = {"first": _all_gather_async(own[:1], [], "all_gather_first", GATHER_FIRST_ID)}
    fill = lambda ts, mine: [lax.dynamic_update_index_in_dim(t, o, chip, 0) for t, o in zip(ts, mine)]

    def first_weights(ready):
        arrived, _ = lax.optimization_barrier((list(exchanged["first"]), ready))
        exchanged["late"] = _all_gather_async(own[1:], [arrived[0][0, 0, :16, :128]], "all_gather_late", GATHER_LATE_ID)
        return _full_weights(dict(zip(W_NAMES[:1], fill(arrived, own[:1]))))

    def late_weights(ready):
        arrived, _ = lax.optimization_barrier((list(exchanged["late"]), ready))
        return _full_weights(dict(zip(W_NAMES[1:], fill(arrived, own[1:]))))

    sq_err, grad_x, grads, small = _local_step(x[0], loss_target[0], first_weights, b_forget, norm_mix_pre,
                                               norm_mix_post, norm_ffn_pre, norm_ffn_post, late=late_weights)

    g4 = _sharded_grads(grads)
    stack = lambda t, extra: jnp.concatenate(
        [jnp.pad(t["b_forget"], ((0, 0), (0, D - N_FOX)))] + [t[n] for n in NORMS]
        + [jnp.pad(extra, ((0, SMALL_ROWS - LOSS_ROW - 1), (0, D - extra.shape[1])), constant_values=1.0)], axis=0)
    early, _ = lax.optimization_barrier((list(_pair_swap_early([g4[n] for n in W_NAMES[1:]])), grads["mid_backward"]))
    other = list(_pair_swap([g4["w_in"]])) + early
    parts = [_pair_sum(g4[n], o, "pair_sum_" + n) for n, o in zip(W_NAMES, other)]
    recv_early = _scatter_parts(parts[1:], "scatter_early", SCATTER_EARLY_ID)
    recv_in = _scatter_parts(parts[:1], "scatter_partials", SCATTER_LATE_ID)
    small_all = _share_small(stack(small, sq_err))

    g_shard, delta, new_m, new_v = {}, {}, {}, {}

    def summed(names, parts, recv):
        return [_sum_partials(p, r, "sum_partials_" + n) for n, p, r in zip(names, parts, recv)]

    def update(names, halves, theirs):
        for n, mine, other_half in zip(names, halves, theirs):
            g_shard[n], delta[n], new_m[n], new_v[n] = _adamw_halves(w[n], mine, other_half, m[n], v[n], "adamw_" + n)

    recv_early, _ = lax.optimization_barrier((list(recv_early), parts[0]))
    early_mine = summed(W_NAMES[1:], parts[1:], recv_early)
    recv_in, _ = lax.optimization_barrier((list(recv_in), early_mine))
    mine = summed(W_NAMES[:1], parts[:1], recv_in) + early_mine
    theirs = list(_swap_halves(mine, "swap_halves"))
    update(W_NAMES[:1], mine[:1], theirs[:1])
    (early_theirs, small_all), _ = lax.optimization_barrier(((theirs[1:], small_all), delta["w_in"]))
    update(W_NAMES[1:], early_mine, early_theirs)
    small_sum = _sum_small(small_all)
    loss = small_sum[LOSS_ROW, 0] * (0.5 / D)
    ones = jnp.ones((1, 128), f32)
    sd, sm, sv = _adamw(stack(small_w, ones), small_sum, stack(small_m, ones), stack(small_v, ones), "adamw_small")

    outs = [loss, grad_x[None]]
    for big, st in ((g_shard, small_sum), (delta, sd), (new_m, sm), (new_v, sv)):
        t = {n: _harness_layout(n, big[n]) for n in W_NAMES}
        t["b_forget"] = st[0:1, :N_FOX]
        for i, n in enumerate(NORMS):
            t[n] = st[i + 1:i + 2]
        outs += [t[n] for n in ORDER]
    return tuple(outs)
```

```python
import functools
import math

import jax
import jax.numpy as jnp
import numpy as np
from jax import lax
from jax.experimental import pallas as pl
from jax.experimental.pallas import tpu as pltpu
from jax.experimental.pallas import tpu_sc as plsc

f32 = jnp.float32
bf16 = jnp.bfloat16
SDS = jax.ShapeDtypeStruct
MESH = pl.DeviceIdType.MESH

S = 2048
D = 1024
HD = 64
BLK = 128
N_FOX = 8
FOX_W = N_FOX * HD
DIL_GROUPS = ((128, 1), (512, 4), (2048, 16))
SLOTS = 4
DIL_W = SLOTS * HD
QKV_W = 3 * DIL_W
VR_W = 3 * FOX_W
GF_W = 2 * D + 128
F_FF = 2816
ROPE_DIM = 16
ROPE_THETA = 500000.0
EPS = 1e-6
NEG = -1e30
SCALE = 1.0 / math.sqrt(HD)
IN_COLS = 5896
N_SHARD = 4

ADAM_LR, ADAM_B1, ADAM_B2, ADAM_EPS, ADAM_WD, ADAM_STEP = 0.001, 0.9, 0.999, 1e-08, 0.01, 10

VMEM_V7X = 64 * 1024 * 1024
VMEM_PLAN_MAX = VMEM_V7X - 8 * 1024 * 1024

TM = 512
TQ = 256

W_NAMES = ("w_in", "w_proj_a", "w_proj_b", "w_out", "w_ffn_gate", "w_ffn_up", "w_ffn_down")
TRANSPOSED = ("w_in", "w_ffn_gate", "w_ffn_up")
IN_SHARD = IN_COLS // N_SHARD
IN_SHARD_PAD = 1504
SHARD_SHAPE = dict(w_in=(IN_SHARD_PAD, D), w_proj_a=(DIL_W, D // N_SHARD), w_proj_b=(FOX_W, D // N_SHARD),
                   w_out=(D // N_SHARD, D), w_ffn_gate=(F_FF // N_SHARD, D), w_ffn_up=(F_FF // N_SHARD, D),
                   w_ffn_down=(F_FF // N_SHARD, D))
SMALL_ROWS = 8
LOSS_ROW = 5


def _nbytes(shape, dtype):
    return math.prod(shape) * jnp.dtype(dtype).itemsize


def _params(semantics, block_bytes, temp_bytes=0):
    need = 2 * block_bytes + temp_bytes + (2 << 20)
    return pltpu.CompilerParams(dimension_semantics=semantics, vmem_limit_bytes=int(min(need, VMEM_PLAN_MAX)))


def _row(w, tm=TM):
    return pl.BlockSpec((tm, w), lambda i: (i, 0))


def _vec(w):
    return pl.BlockSpec((1, w), lambda i: (0, 0))


def _mm(pairs, dims, out_dtype, *, tm, tn, name, m_inner=False):
    a0, b0 = pairs[0]
    m_dim = a0.shape[1] if dims == "tn" else a0.shape[0]
    n_dim = b0.shape[0] if dims == "nt" else b0.shape[1]
    contract = {"nn": ((1,), (0,)), "nt": ((1,), (1,)), "tn": ((0,), (0,))}[dims]
    n_pairs = len(pairs)
    assert m_dim % tm == 0 and n_dim % tn == 0, (name, m_dim, n_dim, tm, tn)

    def body(*refs):
        o_ref = refs[-1]
        acc = None
        for p in range(n_pairs):
            a = refs[2 * p][...].astype(bf16)
            b = refs[2 * p + 1][...].astype(bf16)
            t = lax.dot_general(a, b, (contract, ((), ())), preferred_element_type=f32)
            acc = t if acc is None else acc + t
        o_ref[...] = acc.astype(o_ref.dtype)

    if m_inner:
        grid = (n_dim // tn, m_dim // tm)
        mi = lambda j, i: i
        ni = lambda j, i: j
    else:
        grid = (m_dim // tm, n_dim // tn)
        mi = lambda i, j: i
        ni = lambda i, j: j
    in_specs, block_bytes, args = [], 0, []
    for a, b in pairs:
        k_dim = a.shape[0] if dims == "tn" else a.shape[1]
        if dims == "tn":
            in_specs.append(pl.BlockSpec((k_dim, tm), lambda *g: (0, mi(*g))))
        else:
            in_specs.append(pl.BlockSpec((tm, k_dim), lambda *g: (mi(*g), 0)))
        if dims == "nt":
            in_specs.append(pl.BlockSpec((tn, k_dim), lambda *g: (ni(*g), 0)))
        else:
            in_specs.append(pl.BlockSpec((k_dim, tn), lambda *g: (0, ni(*g))))
        block_bytes += _nbytes((tm, k_dim), a.dtype) + _nbytes((tn, k_dim), b.dtype)
        args += [a, b]
    block_bytes += _nbytes((tm, tn), out_dtype)
    temp = _nbytes((tm, tn), f32) * 2 + sum(_nbytes((tm, a.shape[0] if dims == "tn" else a.shape[1]), bf16)
                                            + _nbytes((tn, a.shape[0] if dims == "tn" else a.shape[1]), bf16)
                                            for a, _ in pairs)
    return pl.pallas_call(
        body, grid=grid, in_specs=in_specs,
        out_specs=pl.BlockSpec((tm, tn), lambda *g: (mi(*g), ni(*g))),
        out_shape=SDS((m_dim, n_dim), out_dtype), name=name,
        compiler_params=_params(("parallel", "parallel"), block_bytes, temp),
    )(*args)


def _rms(x, g):
    r = lax.rsqrt(jnp.mean(x * x, axis=-1, keepdims=True) + EPS)
    return x * r * g


def _rms_bwd(x, g, dy):
    r = lax.rsqrt(jnp.mean(x * x, axis=-1, keepdims=True) + EPS)
    xh = x * r
    dxh = dy * g
    dx = r * (dxh - xh * jnp.mean(dxh * xh, axis=-1, keepdims=True))
    return dx, jnp.sum(dy * xh, axis=0, keepdims=True)


def _acc_rows(ref, val):
    @pl.when(pl.program_id(0) == 0)
    def _():
        ref[...] = jnp.zeros_like(ref)
    ref[...] += val


def _norm_fwd(xs, g):
    n = len(xs)

    def body(*refs):
        g = refs[n][...]
        for x_ref, h_ref in zip(refs[:n], refs[n + 1:]):
            h_ref[...] = _rms(x_ref[...], g).astype(bf16)

    return pl.pallas_call(
        body, grid=(S // TM,), in_specs=[_row(D)] * n + [_vec(D)], out_specs=[_row(D)] * n,
        out_shape=[SDS((S, D), bf16)] * n, name="norm_mix_pre",
        compiler_params=_params(("parallel",), 6 * n * TM * D, 8 * n * TM * D))(*xs, g)


def _perm_rows(xs, ds, name):
    n = len(xs)

    def body(*refs):
        outs = iter(refs[n:])
        for x_ref in refs[:n]:
            for d in ds:
                o_ref, rows = next(outs), S // d
                for r in range(d):
                    o_ref[r * rows:(r + 1) * rows, :] = x_ref[pl.ds(r, rows, stride=d), :]

    blk = pl.BlockSpec((S, 128), lambda c: (0, c))
    w = xs[0].shape[1]
    return pl.pallas_call(
        body, grid=(w // 128,), in_specs=[blk] * n, out_specs=[blk] * (n * len(ds)),
        out_shape=[SDS((S, w), f32)] * (n * len(ds)), name=name,
        compiler_params=_params(("parallel",), 4 * S * 128 * n * (1 + len(ds))))(*xs)


def _unperm_sum(nat, perms, ds, name):
    n = len(perms)

    def body(*refs):
        a_ref, o_ref, sc = refs[0], refs[n + 1], refs[n + 2]
        acc = a_ref[...]
        for b_ref, d in zip(refs[1:n + 1], ds):
            rows = S // d
            for r in range(d):
                sc[pl.ds(r, rows, stride=d), :] = b_ref[r * rows:(r + 1) * rows, :]
            acc = acc + sc[...]
        o_ref[...] = acc

    blk = pl.BlockSpec((S, 128), lambda c: (0, c))
    w = nat.shape[1]
    return pl.pallas_call(
        body, grid=(w // 128,), in_specs=[blk] * (n + 1), out_specs=blk, out_shape=SDS((S, w), f32),
        scratch_shapes=[pltpu.VMEM((S, 128), f32)], name=name,
        compiler_params=_params(("parallel",), 4 * S * 128 * (n + 2), 8 * S * 128))(nat, *perms)


def _whole(a):
    return pl.BlockSpec(a.shape, lambda i: (0,) * a.ndim)


def _resid_norm_fwd(x, merged, w_out, g_post, g_pre):
    def body(x_ref, mg_ref, w_ref, gp_ref, gn_ref, mix_ref, x2_ref, h_ref):
        mix = jnp.dot(mg_ref[...], w_ref[...], preferred_element_type=f32)
        x2 = x_ref[...] + _rms(mix, gp_ref[...])
        mix_ref[...] = mix
        x2_ref[...] = x2
        h_ref[...] = _rms(x2, gn_ref[...]).astype(bf16)

    return pl.pallas_call(
        body, grid=(S // TM,), in_specs=[_row(D), _row(D), _whole(w_out), _vec(D), _vec(D)], out_specs=[_row(D)] * 3,
        out_shape=[SDS((S, D), f32), SDS((S, D), f32), SDS((S, D), bf16)], name="proj_out_norm",
        compiler_params=_params(("parallel",), 16 * TM * D + 2 * D * D, 16 * TM * D))(x, merged, w_out, g_post, g_pre)


def _loss_head(x2, a_act, w_down, g_post, target):
    def body(x2_ref, a_ref, w_ref, g_ref, t_ref, loss_ref, dy_ref, dff_ref, dg_ref):
        ff = jnp.dot(a_ref[...], w_ref[...], preferred_element_type=f32)
        g = g_ref[...]
        err = x2_ref[...] + _rms(ff, g) - t_ref[...]
        dy = err * (1.0 / D)
        dff, dg = _rms_bwd(ff, g, dy)
        dy_ref[...] = dy
        dff_ref[...] = dff.astype(bf16)
        _acc_rows(dg_ref, dg)
        _acc_rows(loss_ref, jnp.full((1, 128), jnp.sum(err * err), f32))

    return pl.pallas_call(
        body, grid=(S // TM,), in_specs=[_row(D), _row(F_FF), _whole(w_down), _vec(D), _row(D)],
        out_specs=[_vec(128), _row(D), _row(D), _vec(D)],
        out_shape=[SDS((1, 128), f32), SDS((S, D), f32), SDS((S, D), bf16), SDS((1, D), f32)], name="ffn_down_loss",
        compiler_params=_params(("arbitrary",), 14 * TM * D + 2 * TM * F_FF + 2 * F_FF * D, 28 * TM * D),
    )(x2, a_act, w_down, g_post, target)


def _norm_bwd_mid(dy, d_g, d_u, w_gate_t, w_up_t, x2, mix, g_ffn_pre, g_mix_post):
    def body(dy_ref, dgt_ref, dut_ref, wg_ref, wu_ref, x2_ref, mix_ref, g3_ref, g2_ref, dx2_ref, dmix_ref, dg3_ref, dg2_ref):
        dh = jnp.dot(dgt_ref[...], wg_ref[...], preferred_element_type=f32)
        dh += jnp.dot(dut_ref[...], wu_ref[...], preferred_element_type=f32)
        d3, dg3 = _rms_bwd(x2_ref[...], g3_ref[...], dh)
        dx2 = dy_ref[...] + d3
        dmix, dg2 = _rms_bwd(mix_ref[...], g2_ref[...], dx2)
        dx2_ref[...] = dx2
        dmix_ref[...] = dmix.astype(bf16)
        _acc_rows(dg3_ref, dg3)
        _acc_rows(dg2_ref, dg2)

    tm = TM // 2
    row = lambda w: _row(w, tm)
    return pl.pallas_call(
        body, grid=(S // tm,),
        in_specs=[row(D), row(F_FF), row(F_FF), _whole(w_gate_t), _whole(w_up_t), row(D), row(D), _vec(D), _vec(D)],
        out_specs=[row(D), row(D), _vec(D), _vec(D)],
        out_shape=[SDS((S, D), f32), SDS((S, D), bf16), SDS((1, D), f32), SDS((1, D), f32)], name="ffn_bwd_in_norm",
        compiler_params=_params(("arbitrary",), 18 * tm * D + 4 * tm * F_FF + 4 * F_FF * D, 28 * tm * D),
    )(dy, d_g, d_u, w_gate_t, w_up_t, x2, mix, g_ffn_pre, g_mix_post)


def _norm_bwd_in(dx2, dh1, x, g):
    def body(dx2_ref, dh_ref, x_ref, g_ref, gx_ref, dg_ref):
        d1, dg = _rms_bwd(x_ref[...], g_ref[...], dh_ref[...])
        gx_ref[...] = dx2_ref[...] + d1
        _acc_rows(dg_ref, dg)

    return pl.pallas_call(
        body, grid=(S // TM,), in_specs=[_row(D)] * 3 + [_vec(D)], out_specs=[_row(D), _vec(D)],
        out_shape=[SDS((S, D), f32), SDS((1, D), f32)], name="norm_bwd_in",
        compiler_params=_params(("arbitrary",), 16 * TM * D, 16 * TM * D))(dx2, dh1, x, g)


def _rope_tables():
    half = ROPE_DIM // 2
    inv_freq = np.power(np.float32(ROPE_THETA), -np.arange(0, ROPE_DIM, 2, dtype=np.float32) / np.float32(ROPE_DIM))
    row = np.arange(S)
    groups = []
    for _, d in DIL_GROUPS:
        pos = ((row % (S // d)) * d + row // (S // d)).astype(np.float32)
        ang = pos[:, None] * inv_freq[None, :].astype(np.float32)
        cos, sin = np.cos(ang).astype(np.float32), np.sin(ang).astype(np.float32)
        c = np.concatenate([cos, cos, np.ones((S, HD - ROPE_DIM), np.float32)], axis=1)
        s_lo = np.concatenate([-sin, np.zeros((S, HD - half), np.float32)], axis=1)
        s_hi = np.concatenate([np.zeros((S, half), np.float32), sin, np.zeros((S, HD - ROPE_DIM), np.float32)], axis=1)
        groups.append(np.stack([np.concatenate([t, t], axis=1) for t in (c, s_lo, s_hi)]))
    return jnp.asarray(np.stack(groups))


def _rotate(x, c, lo, hi, sign):
    tile = lambda t: jnp.tile(t, (1, DIL_W // 128))
    return (x * tile(c) + pltpu.roll(x, DIL_W - ROPE_DIM // 2, 1) * (tile(lo) * sign)
            + pltpu.roll(x, ROPE_DIM // 2, 1) * (tile(hi) * sign))


def _table_specs(g):
    return [pl.BlockSpec((None, None, TM, 128), lambda i, k=k: (g, k, i, 0)) for k in range(3)]


def _rope_fwd(g, p_qkv, tables):
    def body(x_ref, c_ref, lo_ref, hi_ref, o_ref):
        c, lo, hi = c_ref[...], lo_ref[...], hi_ref[...]
        for part in range(2):
            cols = slice(part * DIL_W, (part + 1) * DIL_W)
            o_ref[:, cols] = _rotate(x_ref[:, cols], c, lo, hi, 1.0).astype(bf16)
        o_ref[:, 2 * DIL_W:] = x_ref[:, 2 * DIL_W:].astype(bf16)

    return pl.pallas_call(
        body, grid=(S // TM,), in_specs=[_row(QKV_W)] + _table_specs(g), out_specs=_row(QKV_W),
        out_shape=SDS((S, QKV_W), bf16), name=f"rope_fwd_{g}",
        compiler_params=_params(("parallel",), 6 * TM * QKV_W + 12 * TM * 128, 24 * TM * QKV_W))(p_qkv, tables, tables, tables)


def _rope_bwd(g, dq, dk, dv, tables):
    def body(dq_ref, dk_ref, dv_ref, c_ref, lo_ref, hi_ref, o_ref):
        c, lo, hi = c_ref[...], lo_ref[...], hi_ref[...]
        o_ref[:, :DIL_W] = _rotate(dq_ref[...], c, lo, hi, -1.0).astype(bf16)
        o_ref[:, DIL_W:2 * DIL_W] = _rotate(dk_ref[...], c, lo, hi, -1.0).astype(bf16)
        o_ref[:, 2 * DIL_W:] = dv_ref[...].astype(bf16)

    return pl.pallas_call(
        body, grid=(S // TM,), in_specs=[_row(DIL_W)] * 3 + _table_specs(g), out_specs=_row(QKV_W),
        out_shape=SDS((S, QKV_W), bf16), name=f"rope_bwd_{g}",
        compiler_params=_params(("parallel",), 6 * TM * QKV_W + 12 * TM * 128, 24 * TM * QKV_W))(dq, dk, dv, tables, tables, tables)


def _nt(a, b):
    return lax.dot_general(a, b, (((1,), (1,)), ((), ())), preferred_element_type=f32)


def _tn(a, b):
    return lax.dot_general(a, b, (((0,), (0,)), ((), ())), preferred_element_type=f32)


STEP_BLOCKS = 4
STEP_ROWS = STEP_BLOCKS * BLK


def _dil_prev(g, b):
    _, d = DIL_GROUPS[g]
    nb = S // d // BLK
    if nb == 1 or (b == 0 and nb <= STEP_BLOCKS):
        return None
    return "in" if b > 0 else "halo"


def _bnt(a, b):
    return lax.dot_general(a, b, (((2,), (2,)), ((0,), (0,))), preferred_element_type=f32)


def _bnn(a, b):
    return lax.dot_general(a, b, (((2,), (1,)), ((0,), (0,))), preferred_element_type=f32)


def _btn(a, b):
    return lax.dot_general(a, b, (((1,), (1,)), ((0,), (0,))), preferred_element_type=f32)


def _on_tail(x, tail, fn):
    if tail == x.shape[0]:
        return fn(x)
    return jnp.concatenate([x[:-tail], fn(x[-tail:])], axis=0)


def _heads(ref, part):
    n = ref.shape[0] // BLK
    return jnp.stack([ref[b * BLK:(b + 1) * BLK, part * DIL_W + h * HD:part * DIL_W + (h + 1) * HD]
                      for b in range(n) for h in range(SLOTS)])


def _dil_operands(g, qkv_ref, halo_ref):
    q, kc, vc = (_heads(qkv_ref, part) for part in range(3))
    qi = lax.broadcasted_iota(jnp.int32, (1, BLK, BLK), 1)
    kj = lax.broadcasted_iota(jnp.int32, (1, BLK, BLK), 2)
    with_prev = [b for b in range(STEP_BLOCKS) if _dil_prev(g, b) is not None]
    tail = SLOTS * len(with_prev)
    if not tail:
        return q, kc, vc, None, None, kj <= qi, None, 0
    assert with_prev == list(range(STEP_BLOCKS - len(with_prev), STEP_BLOCKS))
    inside = SLOTS * sum(_dil_prev(g, b) == "in" for b in with_prev)
    kp, vp, prev = kc[:inside], vc[:inside], jnp.broadcast_to(kj >= qi, (inside, BLK, BLK))
    if inside < tail:
        no_halo = jnp.where(pl.program_id(0) == 0, BLK + 1, 0)
        kp = jnp.concatenate([_heads(halo_ref, 1), kp], axis=0)
        vp = jnp.concatenate([_heads(halo_ref, 2), vp], axis=0)
        prev = jnp.concatenate([jnp.broadcast_to(kj >= qi + no_halo, (SLOTS, BLK, BLK)), prev], axis=0)
    return q, kc, vc, kp, vp, kj <= qi, prev, tail


def _dil_in_specs(g, n_aux):
    step = lambda w: pl.BlockSpec((STEP_ROWS, w), lambda i: (i, 0))
    halo = [pl.BlockSpec((BLK, QKV_W), lambda i: (jnp.maximum(i * STEP_BLOCKS - 1, 0), 0))]
    needs_halo = _dil_prev(g, 0) == "halo"
    return [step(QKV_W)] + (halo if needs_halo else []) + [step(DIL_W)] * n_aux, needs_halo


def _dil_fwd(g, qkv):
    in_specs, needs_halo = _dil_in_specs(g, 0)

    def body(*refs):
        qkv_ref, halo_ref = refs[0], refs[1] if needs_halo else None
        o_ref, lse_ref = refs[-2:]
        q, kc, vc, kp, vp, cur, prev, tail = _dil_operands(g, qkv_ref, halo_ref)
        sc = jnp.where(cur, _bnt(q, kc) * SCALE, NEG)
        m = jnp.max(sc, axis=-1, keepdims=True)
        if tail:
            sp = jnp.where(prev, _bnt(q[-tail:], kp) * SCALE, NEG)
            m = _on_tail(m, tail, lambda t: jnp.maximum(t, jnp.max(sp, axis=-1, keepdims=True)))
            pp = jnp.exp(sp - m[-tail:])
        pc = jnp.exp(sc - m)
        den = jnp.sum(pc, axis=-1, keepdims=True)
        if tail:
            den = _on_tail(den, tail, lambda t: t + jnp.sum(pp, axis=-1, keepdims=True))
        inv = 1.0 / den
        o = _bnn((pc * inv).astype(bf16), vc)
        if tail:
            o = _on_tail(o, tail, lambda t: t + _bnn((pp * inv[-tail:]).astype(bf16), vp))
        lse = m + jnp.log(den)
        for b in range(STEP_BLOCKS):
            for h in range(SLOTS):
                rows, hs = slice(b * BLK, (b + 1) * BLK), slice(h * HD, (h + 1) * HD)
                o_ref[rows, hs] = o[SLOTS * b + h]
                lse_ref[rows, hs] = jnp.broadcast_to(lse[SLOTS * b + h], (BLK, HD))

    out = pl.BlockSpec((STEP_ROWS, DIL_W), lambda i: (i, 0))
    return pl.pallas_call(
        body, grid=(S // STEP_ROWS,), in_specs=in_specs, out_specs=[out, out], out_shape=[SDS((S, DIL_W), f32)] * 2,
        name=f"dil_fwd_{g}", compiler_params=_params(("parallel",), 12 * STEP_ROWS * DIL_W, 2 << 20),
    )(*([qkv] * (2 if needs_halo else 1)))


def _dil_combine(outs, lses):
    def body(o0, o1, o2, l0, l1, l2, out_ref, lse_ref, so1, so2, sl1, sl2):
        for (_, d), src, dst in ((DIL_GROUPS[1], o1, so1), (DIL_GROUPS[2], o2, so2),
                                 (DIL_GROUPS[1], l1, sl1), (DIL_GROUPS[2], l2, sl2)):
            rows = S // d
            for r in range(d):
                dst[pl.ds(r, rows, stride=d), :] = src[r * rows:(r + 1) * rows, :]
        a, b, c = l0[...], sl1[...], sl2[...]
        m = jnp.maximum(jnp.maximum(a, b), c)
        ea, eb, ec = jnp.exp(a - m), jnp.exp(b - m), jnp.exp(c - m)
        z = ea + eb + ec
        inv = 1.0 / z
        out_ref[...] = (ea * inv) * o0[...] + (eb * inv) * so1[...] + (ec * inv) * so2[...]
        lse_ref[...] = m + jnp.log(z)

    blk = pl.BlockSpec((S, 128), lambda c: (0, c))
    return pl.pallas_call(
        body, grid=(DIL_W // 128,), in_specs=[blk] * 6, out_specs=[blk] * 2,
        out_shape=[SDS((S, DIL_W), f32)] * 2, scratch_shapes=[pltpu.VMEM((S, 128), f32)] * 4, name="dil_combine",
        compiler_params=_params(("parallel",), 32 * S * 128, 32 * S * 128))(*outs, *lses)


def _dil_bwd(g, qkv, d_out, delta, lse):
    in_specs, needs_halo = _dil_in_specs(g, 3)

    def body(*refs):
        qkv_ref, halo_ref = refs[0], refs[1] if needs_halo else None
        do_ref, dl_ref, lse_ref, dq_ref, dk_ref, dv_ref = refs[-6:]
        q, kc, vc, kp, vp, cur, prev, tail = _dil_operands(g, qkv_ref, halo_ref)
        tiles = [(slice(b * BLK, (b + 1) * BLK), h) for b in range(STEP_BLOCKS) for h in range(SLOTS)]
        do = jnp.stack([do_ref[rows, h * HD:(h + 1) * HD] for rows, h in tiles]).astype(bf16)
        lse = jnp.stack([lse_ref[rows, h * HD:h * HD + 1] for rows, h in tiles])
        delta = jnp.stack([dl_ref[rows, h * HD:h * HD + 1] for rows, h in tiles])

        def probs(q, k, mask, lse, do, v, delta):
            p = jnp.exp(jnp.where(mask, _bnt(q, k) * SCALE, NEG) - lse)
            ds = p * (_bnt(do, v) - delta) * SCALE
            return p.astype(bf16), ds.astype(bf16)

        p, ds = probs(q, kc, cur, lse, do, vc, delta)
        dq, dk, dv = _bnn(ds, kc), _btn(ds, q), _btn(p, do)
        if tail:
            p, ds = probs(q[-tail:], kp, prev, lse[-tail:], do[-tail:], vp, delta[-tail:])
            dq = _on_tail(dq, tail, lambda t: t + _bnn(ds, kp))
            dk_p, dv_p = _btn(ds, q[-tail:]), _btn(p, do[-tail:])
            inside = tail - SLOTS if needs_halo else tail
            pad = jnp.zeros((len(tiles) - inside, BLK, HD), f32)
            dk = dk + jnp.concatenate([dk_p[tail - inside:], pad], axis=0)
            dv = dv + jnp.concatenate([dv_p[tail - inside:], pad], axis=0)
        first = pl.multiple_of(pl.program_id(0) * STEP_ROWS, STEP_ROWS)
        for t, (rows, h) in enumerate(tiles):
            hs = slice(h * HD, (h + 1) * HD)
            own = pl.ds(pl.multiple_of(first + rows.start, BLK), BLK)
            dq_ref[rows, hs] = dq[t]
            dk_ref[own, hs] = dk[t]
            dv_ref[own, hs] = dv[t]
        if needs_halo:
            before = pl.ds(pl.multiple_of(jnp.maximum(first - BLK, 0), BLK), BLK)
            for h in range(SLOTS):
                hs = slice(h * HD, (h + 1) * HD)
                dk_ref[before, hs] += dk_p[h]
                dv_ref[before, hs] += dv_p[h]

    whole = pl.BlockSpec((S, DIL_W), lambda i: (0, 0))
    return pl.pallas_call(
        body, grid=(S // STEP_ROWS,), in_specs=in_specs,
        out_specs=[pl.BlockSpec((STEP_ROWS, DIL_W), lambda i: (i, 0)), whole, whole],
        out_shape=[SDS((S, DIL_W), f32)] * 3, name=f"dil_bwd_{g}",
        compiler_params=_params(("arbitrary",), 20 * STEP_ROWS * DIL_W + 8 * S * DIL_W, 2 << 20),
    )(*([qkv] * (2 if needs_halo else 1)), d_out, delta, lse)


def _scan_rows(x, reverse):
    row = lax.broadcasted_iota(jnp.int32, x.shape, 0)
    k = 1
    while k < S:
        if reverse:
            x = x + jnp.where(row < S - k, pltpu.roll(x, S - k, 0), 0.0)
        else:
            x = x + jnp.where(row >= k, pltpu.roll(x, k, 0), 0.0)
        k *= 2
    return x


N_PAIR = N_FOX // 2
_PAIR_Q = pl.BlockSpec((None, S, 128), lambda p: (p, 0, 0))
_PAIR_K = pl.BlockSpec((None, 8, S), lambda p: (p, 0, 0))


def _forget_fwd(fz, b128):
    def body(z_ref, b_ref, fq_ref, fk_ref):
        z = z_ref[...] + b_ref[...]
        logf = jnp.minimum(z, 0.0) - jnp.log1p(jnp.exp(-jnp.abs(z)))
        f_cum = _scan_rows(logf, reverse=False)
        f_cum_t = f_cum.T
        fq_ref[...] = jnp.zeros_like(fq_ref)
        fk_ref[...] = jnp.zeros_like(fk_ref)
        for p in range(N_PAIR):
            fq_ref[p, :, 0:2] = f_cum[:, 2 * p:2 * p + 2]
            fk_ref[p, 0:2, :] = f_cum_t[2 * p:2 * p + 2, :]

    return pl.pallas_call(
        body, grid=(1,), in_specs=[pl.BlockSpec((S, 128), lambda i: (0, 0)), _vec(128)],
        out_specs=[pl.BlockSpec((N_PAIR, S, 128), lambda i: (0, 0, 0)), pl.BlockSpec((N_PAIR, 8, S), lambda i: (0, 0, 0))],
        out_shape=[SDS((N_PAIR, S, 128), f32), SDS((N_PAIR, 8, S), f32)], name="forget_fwd",
        compiler_params=_params(("arbitrary",), 24 * S * 128, 24 * S * 128))(fz, b128)


def _forget_bwd(fz, b128, d_f_cols, d_f_rows):
    def body(z_ref, b_ref, dfc_ref, dfr_ref, dz_ref, db_ref, df_sc):
        z = z_ref[...] + b_ref[...]
        df_sc[...] = jnp.zeros_like(df_sc)
        for p in range(N_PAIR):
            df_sc[:, 2 * p:2 * p + 2] = dfr_ref[p, :, 0:2] + dfc_ref[p].T[:, 0:2]
        dz = _scan_rows(df_sc[...], reverse=True) * jax.nn.sigmoid(-z)
        dz_ref[...] = dz
        db_ref[...] = jnp.sum(dz, axis=0, keepdims=True)

    full = pl.BlockSpec((S, 128), lambda i: (0, 0))
    return pl.pallas_call(
        body, grid=(1,),
        in_specs=[full, _vec(128), pl.BlockSpec((N_PAIR, 8, S), lambda i: (0, 0, 0)), pl.BlockSpec((N_PAIR, S, 128), lambda i: (0, 0, 0))],
        out_specs=[full, _vec(128)], out_shape=[SDS((S, 128), f32), SDS((1, 128), f32)],
        scratch_shapes=[pltpu.VMEM((S, 128), f32)], name="forget_bwd",
        compiler_params=_params(("arbitrary",), 32 * S * 128, 24 * S * 128))(fz, b128, d_f_cols, d_f_rows)


def _fox_scores(q_ref, k_ref, fq_ref, fk_ref, qi, hh):
    n = (qi + 1) * TQ
    rows, hs = slice(qi * TQ, n), slice(hh * HD, (hh + 1) * HD)
    q = q_ref[rows, hs] * SCALE
    s = _nt(q, k_ref[0:n, hs]) + (fq_ref[rows, hh:hh + 1] - fk_ref[hh:hh + 1, 0:n])
    below = lax.broadcasted_iota(jnp.int32, (TQ, TQ), 1) <= lax.broadcasted_iota(jnp.int32, (TQ, TQ), 0)
    diag = jnp.where(below, s[:, n - TQ:], NEG)
    return diag if qi == 0 else jnp.concatenate([s[:, :n - TQ], diag], axis=1)


def _pair_cols(first):
    return pl.BlockSpec((S, 128), lambda p: (0, first + p))


def _fox_fwd(vr, fq, fk):
    def body(q_ref, k_ref, v_ref, fq_ref, fk_ref, o_ref, lse_ref):
        lse_ref[...] = jnp.zeros_like(lse_ref)
        for hh in range(2):
            hs = slice(hh * HD, (hh + 1) * HD)
            for qi in range(S // TQ):
                n = (qi + 1) * TQ
                rows = slice(qi * TQ, n)
                s = _fox_scores(q_ref, k_ref, fq_ref, fk_ref, qi, hh)
                m = jnp.max(s, axis=-1, keepdims=True)
                p = jnp.exp(s - m)
                den = jnp.sum(p, axis=-1, keepdims=True)
                o_ref[rows, hs] = jnp.dot((p * (1.0 / den)).astype(bf16), v_ref[0:n, hs], preferred_element_type=f32)
                lse_ref[rows, hh:hh + 1] = m + jnp.log(den)

    return pl.pallas_call(
        body, grid=(N_PAIR,), in_specs=[_pair_cols(0), _pair_cols(N_PAIR), _pair_cols(2 * N_PAIR), _PAIR_Q, _PAIR_K],
        out_specs=[_pair_cols(0), _PAIR_Q], out_shape=[SDS((S, FOX_W), f32), SDS((N_PAIR, S, 128), f32)],
        name="fox_fwd", compiler_params=_params(("parallel",), 12 * S * 128, 16 * TQ * S),
    )(vr, vr, vr, fq, fk)


def _fox_bwd(vr, fq, fk, lse, d_out, delta):
    def body(q_ref, k_ref, v_ref, do_ref, fq_ref, fk_ref, lse_ref, dl_ref, dq_ref, dk_ref, dv_ref, dfc_ref, dfr_ref,
             dk_sc, dv_sc):
        dfc_ref[...] = jnp.zeros_like(dfc_ref)
        dfr_ref[...] = jnp.zeros_like(dfr_ref)
        for hh in range(2):
            hs = slice(hh * HD, (hh + 1) * HD)
            dk_sc[...] = jnp.zeros_like(dk_sc)
            dv_sc[...] = jnp.zeros_like(dv_sc)
            for qi in range(S // TQ):
                n = (qi + 1) * TQ
                rows = slice(qi * TQ, n)
                q, do, k, v = q_ref[rows, hs], do_ref[rows, hs], k_ref[0:n, hs], v_ref[0:n, hs]
                p = jnp.exp(_fox_scores(q_ref, k_ref, fq_ref, fk_ref, qi, hh) - lse_ref[rows, hh:hh + 1])
                ds = p * (_nt(do, v) - dl_ref[rows, hh:hh + 1])
                dsb = ds.astype(bf16)
                dq_ref[rows, hs] = jnp.dot(dsb, k, preferred_element_type=f32) * SCALE
                dk_sc[0:n, :] += _tn(dsb, q) * SCALE
                dv_sc[0:n, :] += _tn(p.astype(bf16), do)
                dfc_ref[hh:hh + 1, 0:n] -= jnp.sum(ds, axis=0, keepdims=True)
                dfr_ref[rows, hh:hh + 1] = jnp.sum(ds, axis=-1, keepdims=True)
            dk_ref[:, hs] = dk_sc[...]
            dv_ref[:, hs] = dv_sc[...]

    cols = [_pair_cols(k * N_PAIR) for k in range(3)]
    return pl.pallas_call(
        body, grid=(N_PAIR,), in_specs=cols + [_pair_cols(0), _PAIR_Q, _PAIR_K, _PAIR_Q, _PAIR_Q],
        out_specs=[_pair_cols(0)] * 3 + [_PAIR_K, _PAIR_Q],
        out_shape=[SDS((S, FOX_W), f32)] * 3 + [SDS((N_PAIR, 8, S), f32), SDS((N_PAIR, S, 128), f32)],
        scratch_shapes=[pltpu.VMEM((S, HD), f32)] * 2, name="fox_bwd",
        compiler_params=_params(("parallel",), 32 * S * 128, 24 * TQ * S),
    )(vr, vr, vr, d_out, fq, fk, lse, delta)


def _merge_fwd(out_a, out_b, w_a, w_b, gf):
    cw = D // N_SHARD

    def body(oa_ref, ob_ref, wa_ref, wb_ref, ga_ref, gb_ref, ya_ref, yb_ref, mg_ref):
        oa, ob = oa_ref[...].astype(bf16), ob_ref[...].astype(bf16)
        for j in range(N_SHARD):
            cols = slice(j * cw, (j + 1) * cw)
            ya = jnp.dot(oa, wa_ref[j], preferred_element_type=f32)
            yb = jnp.dot(ob, wb_ref[j], preferred_element_type=f32)
            ya_ref[:, cols] = ya
            yb_ref[:, cols] = yb
            mg_ref[:, cols] = (jax.nn.sigmoid(ga_ref[:, cols]) * ya + jax.nn.sigmoid(gb_ref[:, cols]) * yb).astype(bf16)

    full = lambda a: pl.BlockSpec(a.shape, lambda i: (0, 0, 0))
    return pl.pallas_call(
        body, grid=(S // TM,),
        in_specs=[_row(DIL_W), _row(FOX_W), full(w_a), full(w_b), _row(D), pl.BlockSpec((TM, D), lambda i: (i, 1))],
        out_specs=[_row(D)] * 3, out_shape=[SDS((S, D), f32), SDS((S, D), f32), SDS((S, D), bf16)], name="merge_fwd",
        compiler_params=_params(("parallel",), 22 * TM * D + 2 * (DIL_W + FOX_W) * D, 16 * TM * D),
    )(out_a, out_b, w_a, w_b, gf, gf)


def _merge_bwd(d_mix, w_out, ya, yb, gf):
    def body(dx_ref, w_ref, ya_ref, yb_ref, ga_ref, gb_ref, dya_ref, dyb_ref, dg_ref):
        dm = _nt(dx_ref[...], w_ref[...])
        sa, sb = jax.nn.sigmoid(ga_ref[...]), jax.nn.sigmoid(gb_ref[...])
        dya_ref[...] = (dm * sa).astype(bf16)
        dyb_ref[...] = (dm * sb).astype(bf16)
        dg_ref[:, :D] = (dm * ya_ref[...] * sa * (1.0 - sa)).astype(bf16)
        dg_ref[:, D:] = (dm * yb_ref[...] * sb * (1.0 - sb)).astype(bf16)

    return pl.pallas_call(
        body, grid=(S // TM,),
        in_specs=[_row(D), _whole(w_out)] + [_row(D)] * 3 + [pl.BlockSpec((TM, D), lambda i: (i, 1))],
        out_specs=[_row(D), _row(D), _row(2 * D)],
        out_shape=[SDS((S, D), bf16), SDS((S, D), bf16), SDS((S, 2 * D), bf16)], name="proj_out_bwd_merge",
        compiler_params=_params(("parallel",), 26 * TM * D + 2 * D * D, 28 * TM * D))(d_mix, w_out, ya, yb, gf, gf)


def _branch_bwd(d_ya, d_yb, w_a, w_b, out_a, out_b):
    cw = D // N_SHARD

    def body(dya_ref, dyb_ref, wa_ref, wb_ref, oa_ref, ob_ref, doa_ref, dla_ref, dob_ref, dlb_ref):
        doa = jnp.zeros((TM, DIL_W), f32)
        dob = jnp.zeros((TM, FOX_W), f32)
        for j in range(N_SHARD):
            cols = slice(j * cw, (j + 1) * cw)
            doa += _nt(dya_ref[:, cols], wa_ref[j])
            dob += _nt(dyb_ref[:, cols], wb_ref[j])
        doa_ref[...] = doa
        dob_ref[...] = dob.astype(bf16)
        prod_a = doa * oa_ref[...]
        for h in range(SLOTS):
            hs = slice(h * HD, (h + 1) * HD)
            dla_ref[:, hs] = jnp.broadcast_to(jnp.sum(prod_a[:, hs], axis=-1, keepdims=True), (TM, HD))
        prod_b = dob * ob_ref[...]
        dlb_ref[...] = jnp.zeros_like(dlb_ref)
        for h in range(N_FOX):
            dlb_ref[h // 2, :, h % 2:h % 2 + 1] = jnp.sum(prod_b[:, h * HD:(h + 1) * HD], axis=-1, keepdims=True)

    full = lambda a: pl.BlockSpec(a.shape, lambda i: (0, 0, 0))
    return pl.pallas_call(
        body, grid=(S // TM,),
        in_specs=[_row(D), _row(D), full(w_a), full(w_b), _row(DIL_W), _row(FOX_W)],
        out_specs=[_row(DIL_W), _row(DIL_W), _row(FOX_W), pl.BlockSpec((N_PAIR, TM, 128), lambda i: (0, i, 0))],
        out_shape=[SDS((S, DIL_W), f32), SDS((S, DIL_W), f32), SDS((S, FOX_W), bf16), SDS((N_PAIR, S, 128), f32)],
        name="branch_bwd", compiler_params=_params(("parallel",), 8 * TM * D + 2 * (DIL_W + FOX_W) * D, 8 * TM * D),
    )(d_ya, d_yb, w_a, w_b, out_a, out_b)


def _branch_grads(out_a, out_b, d_ya, d_yb):
    cw = D // N_SHARD

    def body(oa_ref, ob_ref, dya_ref, dyb_ref, ga_ref, gb_ref):
        ga_ref[...] = _tn(oa_ref[...].astype(bf16), dya_ref[...]).astype(bf16)
        gb_ref[...] = _tn(ob_ref[...].astype(bf16), dyb_ref[...]).astype(bf16)

    whole = lambda w: pl.BlockSpec((S, w), lambda j: (0, 0))
    cols = pl.BlockSpec((S, cw), lambda j: (0, j))
    return pl.pallas_call(
        body, grid=(N_SHARD,), in_specs=[whole(DIL_W), whole(FOX_W), cols, cols],
        out_specs=[pl.BlockSpec((None, DIL_W, cw), lambda j: (j, 0, 0)), pl.BlockSpec((None, FOX_W, cw), lambda j: (j, 0, 0))],
        out_shape=[SDS((N_SHARD, DIL_W, cw), bf16), SDS((N_SHARD, FOX_W, cw), bf16)], name="grad_w_proj_ab",
        compiler_params=_params(("parallel",), 4 * S * (DIL_W + FOX_W) + 4 * S * cw + 4 * (DIL_W + FOX_W) * cw,
                                4 * S * (DIL_W + FOX_W)))(out_a, out_b, d_ya, d_yb)


FF_TN = F_FF // 2
FF_TM = 1024


def _ffn_fwd(h, w_gate_t, w_up_t):
    def body(h_ref, wg_ref, wu_ref, g_ref, u_ref, a_ref):
        hb = h_ref[...]
        g = _nt(hb, wg_ref[...])
        u = _nt(hb, wu_ref[...])
        g_ref[...] = g
        u_ref[...] = u
        a_ref[...] = (g * jax.nn.sigmoid(g) * u).astype(bf16)

    tile = pl.BlockSpec((FF_TM, FF_TN), lambda j, i: (i, j))
    wspec = pl.BlockSpec((FF_TN, D), lambda j, i: (j, 0))
    return pl.pallas_call(
        body, grid=(F_FF // FF_TN, S // FF_TM),
        in_specs=[pl.BlockSpec((FF_TM, D), lambda j, i: (i, 0)), wspec, wspec], out_specs=[tile] * 3,
        out_shape=[SDS((S, F_FF), f32), SDS((S, F_FF), f32), SDS((S, F_FF), bf16)], name="ffn_fwd",
        compiler_params=_params(("parallel", "parallel"), 2 * FF_TM * D + 4 * D * FF_TN + 10 * FF_TM * FF_TN, 16 * FF_TM * FF_TN),
    )(h, w_gate_t, w_up_t)


def _ffn_bwd_act(d_ff, w_down, g_act, u_act):
    def body(d_ref, wd_ref, g_ref, u_ref, dg_ref, du_ref):
        da = _nt(d_ref[...], wd_ref[...])
        g = g_ref[...]
        sg = jax.nn.sigmoid(g)
        du_ref[...] = (da * g * sg).astype(bf16)
        dg_ref[...] = (da * u_ref[...] * sg * (1.0 + g * (1.0 - sg))).astype(bf16)

    tile = pl.BlockSpec((FF_TM, FF_TN), lambda j, i: (i, j))
    return pl.pallas_call(
        body, grid=(F_FF // FF_TN, S // FF_TM),
        in_specs=[pl.BlockSpec((FF_TM, D), lambda j, i: (i, 0)), pl.BlockSpec((FF_TN, D), lambda j, i: (j, 0)), tile, tile],
        out_specs=[tile, tile], out_shape=[SDS((S, F_FF), bf16)] * 2, name="ffn_bwd_act",
        compiler_params=_params(("parallel", "parallel"), 2 * FF_TM * D + 2 * D * FF_TN + 12 * FF_TM * FF_TN, 16 * FF_TM * FF_TN),
    )(d_ff, w_down, g_act, u_act)


def _row_tile(rows):
    return next(t for t in (376, 128, 176, 64, 32, 16, 8) if rows % t == 0)


def _adamw_math(w, g, m, v):
    c1 = 1.0 - ADAM_B1 ** ADAM_STEP
    c2 = 1.0 - ADAM_B2 ** ADAM_STEP
    m_new = ADAM_B1 * m + (1.0 - ADAM_B1) * g
    v_new = ADAM_B2 * v + (1.0 - ADAM_B2) * (g * g)
    return -ADAM_LR * ((m_new / c1) / (jnp.sqrt(v_new / c2) + ADAM_EPS) + ADAM_WD * w), m_new, v_new


def _adamw(w, g, m, v, name):
    rows, cols = w.shape
    tm = _row_tile(rows)

    def body(w_ref, g_ref, m_ref, v_ref, d_ref, nm_ref, nv_ref):
        d_ref[...], nm_ref[...], nv_ref[...] = _adamw_math(w_ref[...], g_ref[...], m_ref[...], v_ref[...])

    spec = pl.BlockSpec((tm, cols), lambda i: (i, 0))
    return pl.pallas_call(
        body, grid=(rows // tm,), in_specs=[spec] * 4, out_specs=[spec] * 3, out_shape=[SDS(w.shape, f32)] * 3,
        name=name, compiler_params=_params(("parallel",), 28 * tm * cols, 16 * tm * cols))(w, g, m, v)


def _adamw_halves(w, g_mine, g_theirs, m, v, name):
    cols = w.shape[1]
    tm = _row_tile(g_mine.shape[0])
    per_half = g_mine.shape[0] // tm
    assert 2 * g_mine.shape[0] - w.shape[0] < tm
    core = lax.axis_index("c").astype(jnp.int32).reshape(1)

    def body(c_ref, w_ref, gm_ref, gt_ref, m_ref, v_ref, g_ref, d_ref, nm_ref, nv_ref):
        mine = pl.program_id(0) // per_half == c_ref[0]
        g = jnp.where(mine, gm_ref[...], gt_ref[...])
        g_ref[...] = g
        d_ref[...], nm_ref[...], nv_ref[...] = _adamw_math(w_ref[...], g, m_ref[...], v_ref[...])

    spec = pl.BlockSpec((tm, cols), lambda i, c_ref: (i, 0))
    in_half = lambda i, first: jnp.clip(i - first * per_half, 0, per_half - 1)
    grid_spec = pltpu.PrefetchScalarGridSpec(
        num_scalar_prefetch=1, grid=(2 * per_half,),
        in_specs=[spec, pl.BlockSpec((tm, cols), lambda i, c_ref: (in_half(i, c_ref[0]), 0)),
                  pl.BlockSpec((tm, cols), lambda i, c_ref: (in_half(i, 1 - c_ref[0]), 0)), spec, spec],
        out_specs=[spec] * 4)
    return pl.pallas_call(
        body, grid_spec=grid_spec, out_shape=[SDS(w.shape, f32)] * 4, name=name,
        compiler_params=_params(("parallel",), 36 * tm * cols, 16 * tm * cols))(core, w, g_mine, g_theirs, m, v)


_ANY = pl.BlockSpec(memory_space=pl.ANY)


def _place():
    x, y, c = lax.axis_index("x"), lax.axis_index("y"), lax.axis_index("c")
    chips = [(1 - x, y), (x, 1 - y), (1 - x, 1 - y)]
    return x, y, c, chips


def _halved(t):
    return t.reshape(t.shape[:-2] + (2, t.shape[-2] // 2, t.shape[-1]))


def _gather_body(src, out, send_ici, recv_ici, send_d2d, recv_d2d):
    x, y, c, chips = _place()
    sibling = (x, y, 1 - c)
    me_j = 2 * x + y
    sends = []
    for a in range(len(src)):
        for p in range(3):
            cp = pltpu.make_async_remote_copy(
                src_ref=src[a].at[c], dst_ref=out[a].at[me_j, c], send_sem=send_ici.at[a, p],
                recv_sem=recv_ici.at[a, p], device_id=(*chips[p], c), device_id_type=MESH)
            cp.start()
            sends.append(cp)
    for a in range(len(src)):
        for p, (px, py) in enumerate(chips):
            blk = out[a].at[2 * px + py, c]
            pltpu.make_async_remote_copy(
                src_ref=blk, dst_ref=blk, send_sem=send_ici.at[a, p], recv_sem=recv_ici.at[a, p],
                device_id=sibling, device_id_type=MESH).wait_recv()
            fw = pltpu.make_async_remote_copy(
                src_ref=blk, dst_ref=blk, send_sem=send_d2d.at[a, p], recv_sem=recv_d2d.at[a, p],
                device_id=sibling, device_id_type=MESH)
            fw.start()
            sends.append(fw)
    for a in range(len(src)):
        for p, (px, py) in enumerate(chips):
            blk = out[a].at[2 * px + py, 1 - c]
            pltpu.make_async_remote_copy(
                src_ref=blk, dst_ref=blk, send_sem=send_d2d.at[a, p], recv_sem=recv_d2d.at[a, p],
                device_id=sibling, device_id_type=MESH).wait_recv()
    for cp in sends:
        cp.wait_send()


def _handshake(peers):
    barrier = pltpu.get_barrier_semaphore()
    for peer in peers:
        pl.semaphore_signal(barrier, inc=1, device_id=peer, device_id_type=MESH)
    pl.semaphore_wait(barrier, len(peers))


_SEQUENCER = dict(axis_name="sequencer", num_cores=1)
GATHER_LATE_ID, SCATTER_EARLY_ID, SWAP_EARLY_ID, GATHER_FIRST_ID, SCATTER_LATE_ID, SHARE_SMALL_ID = 1, 2, 3, 4, 5, 6


def _all_gather_async(shards, after, name, collective_id):
    n, k = len(shards), len(after)

    def body(*refs):
        x, y, c, chips = _place()
        _handshake([(*chip, c) for chip in chips] + [(x, y, 1 - c)])
        _gather_body(refs[:n], refs[n + k:2 * n + k], *refs[2 * n + k:])

    return pl.kernel(
        body, out_type=[SDS((N_SHARD,) + t.shape, t.dtype) for t in shards],
        mesh=plsc.ScalarSubcoreMesh(**_SEQUENCER), scratch_types=[pltpu.SemaphoreType.DMA((n, 3))] * 4,
        compiler_params=pltpu.CompilerParams(collective_id=collective_id), name=name)(*shards, *after)


def _pair_swap(grads):
    n = len(grads)

    def body(*refs):
        src, out, send_sems, recv_sems = refs[:n], refs[n:2 * n], refs[2 * n], refs[2 * n + 1]
        x, y, c, _ = _place()
        copies = [pltpu.make_async_remote_copy(
            src_ref=src[a].at[:, 1 - c], dst_ref=out[a], send_sem=send_sems.at[a], recv_sem=recv_sems.at[a],
            device_id=(x, y, 1 - c), device_id_type=MESH) for a in range(n)]
        for cp in copies:
            cp.start()
        for cp in copies:
            cp.wait()

    return pl.pallas_call(
        body, in_specs=[_ANY] * n, out_specs=[_ANY] * n,
        out_shape=[SDS((N_SHARD,) + t.shape[2:], t.dtype) for t in grads],
        scratch_shapes=[pltpu.SemaphoreType.DMA((n,)), pltpu.SemaphoreType.DMA((n,))], name="pair_swap",
        compiler_params=pltpu.CompilerParams(has_side_effects=True))(*grads)


def _pair_swap_early(grads):
    n = len(grads)

    def body(*refs):
        src, out, send_sems, recv_sems = refs[:n], refs[n:2 * n], refs[2 * n], refs[2 * n + 1]
        x, y, c, _ = _place()
        _handshake([(x, y, 1 - c)])
        copies = [pltpu.make_async_remote_copy(
            src_ref=src[a].at[:, 1 - c], dst_ref=out[a], send_sem=send_sems.at[a], recv_sem=recv_sems.at[a],
            device_id=(x, y, 1 - c), device_id_type=MESH) for a in range(n)]
        for cp in copies:
            cp.start()
        for cp in copies:
            cp.wait()

    return pl.kernel(
        body, out_type=[SDS((N_SHARD,) + t.shape[2:], t.dtype) for t in grads],
        mesh=plsc.ScalarSubcoreMesh(**_SEQUENCER), scratch_types=[pltpu.SemaphoreType.DMA((n,))] * 2,
        compiler_params=pltpu.CompilerParams(collective_id=SWAP_EARLY_ID), name="pair_swap_early")(*grads)


def _scatter_parts(parts, name, collective_id):
    n = len(parts)

    def body(*refs):
        part, recv, send_sems, recv_sems = refs[:n], refs[n:2 * n], refs[2 * n], refs[2 * n + 1]
        x, y, c, chips = _place()
        _handshake([(*chip, c) for chip in chips])
        me_j = 2 * x + y
        sends = []
        for a in range(n):
            for p, (px, py) in enumerate(chips):
                cp = pltpu.make_async_remote_copy(
                    src_ref=part[a].at[2 * px + py], dst_ref=recv[a].at[me_j], send_sem=send_sems.at[a, p],
                    recv_sem=recv_sems.at[a, p], device_id=(px, py, c), device_id_type=MESH)
                cp.start()
                sends.append(cp)
        for a in range(n):
            for p, (px, py) in enumerate(chips):
                slot = recv[a].at[2 * px + py]
                pltpu.make_async_remote_copy(
                    src_ref=slot, dst_ref=slot, send_sem=send_sems.at[a, p], recv_sem=recv_sems.at[a, p],
                    device_id=(px, py, c), device_id_type=MESH).wait_recv()
        for cp in sends:
            cp.wait_send()

    return pl.kernel(
        body, out_type=[SDS(t.shape, t.dtype) for t in parts],
        mesh=plsc.ScalarSubcoreMesh(**_SEQUENCER), scratch_types=[pltpu.SemaphoreType.DMA((n, 3))] * 2,
        compiler_params=pltpu.CompilerParams(collective_id=collective_id), name=name)(*parts)


def _pair_sum(grads, other, name):
    _, _, rows, cols = grads.shape
    tr = _row_tile(rows)
    core = lax.axis_index("c").astype(jnp.int32).reshape(1)

    def body(c_ref, g_ref, o_ref, out_ref):
        out_ref[...] = (g_ref[...].astype(f32) + o_ref[...].astype(f32)).astype(bf16)

    grid_spec = pltpu.PrefetchScalarGridSpec(
        num_scalar_prefetch=1, grid=(N_SHARD, rows // tr),
        in_specs=[pl.BlockSpec((None, None, tr, cols), lambda j, i, c_ref: (j, c_ref[0], i, 0)),
                  pl.BlockSpec((None, tr, cols), lambda j, i, c_ref: (j, i, 0))],
        out_specs=pl.BlockSpec((None, tr, cols), lambda j, i, c_ref: (j, i, 0)))
    return pl.pallas_call(
        body, grid_spec=grid_spec, out_shape=SDS((N_SHARD, rows, cols), bf16), name=name,
        compiler_params=_params(("parallel", "parallel"), 10 * tr * cols, 12 * tr * cols))(core, grads, other)


def _share_small(small):
    def body(small_ref, small_all_ref, ssend, srecv, local_sem):
        x, y, c, _ = _place()
        flip = lambda a, bit: 1 - a if bit else a
        peers = [(flip(x, k & 4), flip(y, k & 2), flip(c, k & 1)) for k in range(1, 8)]
        _handshake(peers)
        me_dev = 4 * x + 2 * y + c
        own = pltpu.make_async_copy(small_ref, small_all_ref.at[me_dev], local_sem)
        own.start()
        sends = []
        for k, to in enumerate(peers):
            cp = pltpu.make_async_remote_copy(
                src_ref=small_ref, dst_ref=small_all_ref.at[me_dev],
                send_sem=ssend.at[k], recv_sem=srecv.at[k], device_id=to, device_id_type=MESH)
            cp.start()
            sends.append(cp)
        for k, (px, py, pc) in enumerate(peers):
            slot = small_all_ref.at[4 * px + 2 * py + pc]
            pltpu.make_async_remote_copy(
                src_ref=slot, dst_ref=slot, send_sem=ssend.at[k], recv_sem=srecv.at[k],
                device_id=(px, py, pc), device_id_type=MESH).wait_recv()
        for cp in sends:
            cp.wait_send()
        own.wait()

    return pl.kernel(
        body, out_type=SDS((8, SMALL_ROWS, D), f32), mesh=plsc.ScalarSubcoreMesh(**_SEQUENCER),
        scratch_types=[pltpu.SemaphoreType.DMA((7,)), pltpu.SemaphoreType.DMA((7,)), pltpu.SemaphoreType.DMA],
        compiler_params=pltpu.CompilerParams(collective_id=SHARE_SMALL_ID), name="share_small")(small)


def _sum_partials(part, recv, name):
    _, rows, cols = recv.shape
    tr = _row_tile(rows)
    me = (2 * lax.axis_index("x") + lax.axis_index("y")).astype(jnp.int32).reshape(1)

    def body(me_ref, mine, r0, r1, r2, r3, out_ref):
        acc = None
        for j, r in enumerate((r0, r1, r2, r3)):
            term = jnp.where(me_ref[0] == j, mine[...], r[...]).astype(f32)
            acc = term if acc is None else acc + term
        out_ref[...] = acc

    slot = lambda j: pl.BlockSpec((None, tr, cols), lambda i, me_ref: (jnp.where(me_ref[0] == j, j ^ 1, j), i, 0))
    grid_spec = pltpu.PrefetchScalarGridSpec(
        num_scalar_prefetch=1, grid=(rows // tr,),
        in_specs=[pl.BlockSpec((None, tr, cols), lambda i, me_ref: (me_ref[0], i, 0)), slot(0), slot(1), slot(2), slot(3)],
        out_specs=pl.BlockSpec((tr, cols), lambda i, me_ref: (i, 0)))
    return pl.pallas_call(
        body, grid_spec=grid_spec, out_shape=SDS((rows, cols), f32), name=name,
        compiler_params=_params(("parallel",), 14 * tr * cols, 12 * tr * cols))(me, part, recv, recv, recv, recv)


def _sum_small(small_all):
    def body(small_ref, out_ref):
        tot = small_ref[0]
        for k in range(1, 8):
            tot = tot + small_ref[k]
        out_ref[...] = tot

    return pl.pallas_call(
        body, grid=(1,), in_specs=[pl.BlockSpec((8, SMALL_ROWS, D), lambda i: (0, 0, 0))],
        out_specs=pl.BlockSpec((SMALL_ROWS, D), lambda i: (0, 0)), out_shape=SDS((SMALL_ROWS, D), f32),
        name="sum_small", compiler_params=_params(("arbitrary",), 36 * SMALL_ROWS * D))(small_all)


def _swap_halves(halves, name):
    n = len(halves)

    def body(*refs):
        src, out, send_sems, recv_sems = refs[:n], refs[n:2 * n], refs[2 * n], refs[2 * n + 1]
        x, y, c, _ = _place()
        copies = [pltpu.make_async_remote_copy(
            src_ref=src[a], dst_ref=out[a], send_sem=send_sems.at[a], recv_sem=recv_sems.at[a],
            device_id=(x, y, 1 - c), device_id_type=MESH) for a in range(n)]
        for cp in copies:
            cp.start()
        for cp in copies:
            cp.wait()

    return pl.pallas_call(
        body, in_specs=[_ANY] * n, out_specs=[_ANY] * n, out_shape=[SDS(t.shape, f32) for t in halves],
        scratch_shapes=[pltpu.SemaphoreType.DMA((n,))] * 2, name=name,
        compiler_params=pltpu.CompilerParams(has_side_effects=True))(*halves)


def _kernel_layout(name, t):
    t = t[0]
    return jnp.swapaxes(t, 0, 1) if name in TRANSPOSED else t


def _harness_layout(name, t):
    if name in TRANSPOSED:
        t = jnp.swapaxes(t, 0, 1)
    return t[None]


def _pad_rows(t, rows):
    return t if t.shape[0] == rows else jnp.pad(t, ((0, rows - t.shape[0]), (0, 0)))


_QA, _KA, _VA, _QB, _F, _GAB = 0, 768, 1536, 2304, 3840, 3848


def _spans(a, b):
    return [(j, max(a, j * IN_SHARD) - j * IN_SHARD, max(a, j * IN_SHARD) - a,
             min(b, (j + 1) * IN_SHARD) - max(a, j * IN_SHARD))
            for j in range(N_SHARD) if max(a, j * IN_SHARD) < min(b, (j + 1) * IN_SHARD)]


_LANES = pl.BlockSpec((N_SHARD, IN_SHARD_PAD, 128), lambda c: (0, 0, c))


def _split_w_in(shards):
    group = [[(o + g * DIL_W, o + (g + 1) * DIL_W) for o in (_QA, _KA, _VA)] for g in range(3)]
    fox = [[(_QB + k * FOX_W, _QB + (k + 1) * FOX_W)] for k in range(3)]
    wanted = group + fox + [[(_QB, _F)], [(_F, _GAB)], [(_GAB, IN_COLS)]]
    rows = [sum(b - a for a, b in w) for w in wanted]
    rows[7] = 128

    def body(s_ref, *o_refs):
        for o_ref, want in zip(o_refs, wanted):
            at = 0
            for a, b in want:
                for j, src, off, n in _spans(a, b):
                    o_ref[at + off:at + off + n, :] = s_ref[j, src:src + n, :]
                at += b - a
        o_refs[7][N_FOX:, :] = jnp.zeros((128 - N_FOX, 128), bf16)

    return pl.pallas_call(
        body, grid=(D // 128,), in_specs=[_LANES], out_specs=[pl.BlockSpec((r, 128), lambda c: (0, c)) for r in rows],
        out_shape=[SDS((r, D), bf16) for r in rows], name="split_w_in",
        compiler_params=_params(("parallel",), 2 * 128 * (N_SHARD * IN_SHARD_PAD + sum(rows))))(shards)


def _join_w_in(g_a, g_fox, g_f, g_gab):
    parts = [(g_a[k], o, o + DIL_W) for o in (0, DIL_W, 2 * DIL_W) for k in range(3)]
    parts += [(t, 0, FOX_W) for t in g_fox] + [(g_f, 0, N_FOX), (g_gab, 0, 2 * D)]
    arrays = list(g_a) + list(g_fox) + [g_f, g_gab]
    index = {id(t): i for i, t in enumerate(arrays)}

    def body(*refs):
        o_ref = refs[-1]
        o_ref[:, IN_SHARD:, :] = jnp.zeros((N_SHARD, IN_SHARD_PAD - IN_SHARD, 128), bf16)
        at = 0
        for t, lo, hi in parts:
            src_ref = refs[index[id(t)]]
            for j, dst, off, n in _spans(at, at + hi - lo):
                o_ref[j, dst:dst + n, :] = src_ref[lo + off:lo + off + n, :].astype(bf16)
            at += hi - lo

    return pl.pallas_call(
        body, grid=(D // 128,), in_specs=[pl.BlockSpec((t.shape[0], 128), lambda c: (0, c)) for t in arrays],
        out_specs=_LANES, out_shape=SDS((N_SHARD, IN_SHARD_PAD, D), bf16), name="join_w_in",
        compiler_params=_params(("parallel",), 2 * 128 * (N_SHARD * IN_SHARD_PAD + sum(t.shape[0] for t in arrays))),
    )(*arrays)


def _full_weights(gathered):
    full = {n: t.reshape((N_SHARD,) + SHARD_SHAPE[n]) for n, t in gathered.items()}
    out = {}
    if "w_in" in full:
        pieces = _split_w_in(full["w_in"])
        out.update(w_a_t=pieces[0:3], w_fox_t=pieces[3:6], w_vr_t=pieces[6], w_f_t=pieces[7], w_gab_t=pieces[8])
    if "w_out" in full:
        out.update(
            w_a4=full["w_proj_a"],
            w_b4=full["w_proj_b"],
            w_out=full["w_out"].reshape(D, D),
            w_gate_t=full["w_ffn_gate"].reshape(F_FF, D),
            w_up_t=full["w_ffn_up"].reshape(F_FF, D),
            w_down=full["w_ffn_down"].reshape(F_FF, D))
    return out


def _sharded_grads(g):
    full = dict(w_in=_join_w_in(g["w_a_t"], g["w_fox_t"], g["w_f_t"], g["w_gab_t"]), w_proj_a=g["w_a4"],
                w_proj_b=g["w_b4"], w_out=g["w_out"], w_ffn_gate=g["w_gate_t"], w_ffn_up=g["w_up_t"],
                w_ffn_down=g["w_down"])
    return {n: _halved(full[n].reshape((N_SHARD,) + SHARD_SHAPE[n])) for n in W_NAMES}


def _local_step(x, target, wt, b_forget, g_mix_pre, g_mix_post, g_ffn_pre, g_ffn_post, late=None):
    tables = _rope_tables()
    b128 = jnp.pad(b_forget, ((0, 0), (0, 128 - N_FOX)))
    dils = tuple(d for _, d in DIL_GROUPS[1:])

    hs = _norm_fwd([x] + list(_perm_rows([x], dils, "perm_x")), g_mix_pre)
    h1 = hs[0]
    if callable(wt):
        wt = wt(h1)
    qkv = [_rope_fwd(g, _mm([(hs[g], wt["w_a_t"][g])], "nt", f32, tm=1024, tn=QKV_W, name=f"proj_a_{g}"), tables)
           for g in range(3)]
    vr = _mm([(h1, wt["w_vr_t"])], "nt", bf16, tm=1024, tn=VR_W // 2, name="proj_vr")
    gab = _mm([(h1, wt["w_gab_t"])], "nt", f32, tm=512, tn=2 * D, name="proj_gab")
    fz = _mm([(h1, wt["w_f_t"])], "nt", f32, tm=1024, tn=128, name="proj_f")
    dil = [_dil_fwd(g, qkv[g]) for g in range(3)]
    out_a, lse_a = _dil_combine([o for o, _ in dil], [l for _, l in dil])
    f_q, f_k = _forget_fwd(fz, b128)
    out_b, lse_b = _fox_fwd(vr, f_q, f_k)
    if late is not None:
        wt = {**wt, **late(out_b)}
    ya, yb, merged = _merge_fwd(out_a, out_b, wt["w_a4"], wt["w_b4"], gab)
    mix, x2, h3 = _resid_norm_fwd(x, merged, wt["w_out"], g_mix_post, g_ffn_pre)
    g_act, u_act, a_act = _ffn_fwd(h3, wt["w_gate_t"], wt["w_up_t"])
    sq_err, dy, d_ff, dg_ffn_post = _loss_head(x2, a_act, wt["w_down"], g_ffn_post, target)

    grads = {}
    d_g, d_u = _ffn_bwd_act(d_ff, wt["w_down"], g_act, u_act)
    grads["w_down"] = _mm([(a_act, d_ff)], "tn", bf16, tm=FF_TN, tn=D, name="grad_w_down")
    grads["w_gate_t"] = _mm([(d_g, h3)], "tn", bf16, tm=FF_TN, tn=D, name="grad_w_gate")
    grads["w_up_t"] = _mm([(d_u, h3)], "tn", bf16, tm=FF_TN, tn=D, name="grad_w_up")
    dx2, d_mix, dg_ffn_pre, dg_mix_post = _norm_bwd_mid(dy, d_g, d_u, wt["w_gate_t"], wt["w_up_t"], x2, mix,
                                                        g_ffn_pre, g_mix_post)

    grads["w_out"] = _mm([(merged, d_mix)], "tn", bf16, tm=D, tn=D, name="grad_w_out")
    d_ya, d_yb, d_gab = _merge_bwd(d_mix, wt["w_out"], ya, yb, gab)
    grads["w_a4"], grads["w_b4"] = _branch_grads(out_a, out_b, d_ya, d_yb)
    d_out_a, delta_a, d_out_b, delta_b = _branch_bwd(d_ya, d_yb, wt["w_a4"], wt["w_b4"], out_a, out_b)

    perm = _perm_rows([d_out_a, delta_a, lse_a], dils, "perm_dil_bwd")
    aux = [(d_out_a, delta_a, lse_a)] + [tuple(perm[k * len(dils) + i] for k in range(3)) for i in range(len(dils))]
    d_qkv = []
    for g in range(3):
        dq, dk, dv = _dil_bwd(g, qkv[g], *aux[g])
        d_qkv.append(_rope_bwd(g, dq, dk, dv, tables))
    *d_fox, d_f_cols, d_f_rows = _fox_bwd(vr, f_q, f_k, lse_b, d_out_b, delta_b)
    d_z, d_b128 = _forget_bwd(fz, b128, d_f_cols, d_f_rows)

    grads["w_a_t"] = [_mm([(d_qkv[g], hs[g])], "tn", bf16, tm=QKV_W, tn=D, name=f"grad_w_a_{g}") for g in range(3)]
    grads["w_fox_t"] = [_mm([(d_fox[k], h1)], "tn", bf16, tm=FOX_W, tn=D, name=f"grad_w_fox_{k}") for k in range(3)]
    grads["w_gab_t"] = _mm([(d_gab, h1)], "tn", bf16, tm=D, tn=D, name="grad_w_gab")
    grads["w_f_t"] = _mm([(d_z, h1)], "tn", bf16, tm=128, tn=D, name="grad_w_f")
    d_h1_nat = _mm([(d_qkv[0], wt["w_a_t"][0])] + list(zip(d_fox, wt["w_fox_t"]))
                   + [(d_gab, wt["w_gab_t"]), (d_z, wt["w_f_t"])], "nn", f32, tm=512, tn=D, name="proj_in_bwd")
    d_h1_dil = [_mm([(d_qkv[g], wt["w_a_t"][g])], "nn", f32, tm=1024, tn=D, name=f"proj_a_bwd_{g}") for g in (1, 2)]
    d_h1 = _unperm_sum(d_h1_nat, d_h1_dil, dils, "unperm_d_h1")
    grad_x, dg_mix_pre = _norm_bwd_in(dx2, d_h1, x, g_mix_pre)

    small = dict(b_forget=d_b128[:, :N_FOX], norm_mix_pre=dg_mix_pre, norm_mix_post=dg_mix_post,
                 norm_ffn_pre=dg_ffn_pre, norm_ffn_post=dg_ffn_post)
    grads["mid_backward"] = d_qkv[0]
    return sq_err, grad_x, grads, small


NORMS = ("norm_mix_pre", "norm_mix_post", "norm_ffn_pre", "norm_ffn_post")
ORDER = ("w_in", "w_proj_a", "w_proj_b", "w_out", "b_forget", "w_ffn_gate", "w_ffn_up", "w_ffn_down") + NORMS


def kernel(x, w_in, w_proj_a, w_proj_b, w_out, b_forget, w_ffn_gate, w_ffn_up, w_ffn_down, norm_mix_pre, norm_mix_post, norm_ffn_pre, norm_ffn_post, loss_target, m_w_in, m_w_proj_a, m_w_proj_b, m_w_out, m_b_forget, m_w_ffn_gate, m_w_ffn_up, m_w_ffn_down, m_norm_mix_pre, m_norm_mix_post, m_norm_ffn_pre, m_norm_ffn_post, v_w_in, v_w_proj_a, v_w_proj_b, v_w_out, v_b_forget, v_w_ffn_gate, v_w_ffn_up, v_w_ffn_down, v_norm_mix_pre, v_norm_mix_post, v_norm_ffn_pre, v_norm_ffn_post):
    given = dict(w_in=w_in, w_proj_a=w_proj_a, w_proj_b=w_proj_b, w_out=w_out, w_ffn_gate=w_ffn_gate,
                 w_ffn_up=w_ffn_up, w_ffn_down=w_ffn_down)
    given_m = dict(w_in=m_w_in, w_proj_a=m_w_proj_a, w_proj_b=m_w_proj_b, w_out=m_w_out, w_ffn_gate=m_w_ffn_gate,
                   w_ffn_up=m_w_ffn_up, w_ffn_down=m_w_ffn_down)
    given_v = dict(w_in=v_w_in, w_proj_a=v_w_proj_a, w_proj_b=v_w_proj_b, w_out=v_w_out, w_ffn_gate=v_w_ffn_gate,
                   w_ffn_up=v_w_ffn_up, w_ffn_down=v_w_ffn_down)
    w, m, v = ({n: _kernel_layout(n, t[n]) for n in W_NAMES} for t in (given, given_m, given_v))
    small_w = dict(b_forget=b_forget, norm_mix_pre=norm_mix_pre, norm_mix_post=norm_mix_post,
                   norm_ffn_pre=norm_ffn_pre, norm_ffn_post=norm_ffn_post)
    small_m = dict(b_forget=m_b_forget, norm_mix_pre=m_norm_mix_pre, norm_mix_post=m_norm_mix_post,
                   norm_ffn_pre=m_norm_ffn_pre, norm_ffn_post=m_norm_ffn_post)
    small_v = dict(b_forget=v_b_forget, norm_mix_pre=v_norm_mix_pre, norm_mix_post=v_norm_mix_post,
                   norm_ffn_pre=v_norm_ffn_pre, norm_ffn_post=v_norm_ffn_post)

    own = [_halved(_pad_rows(w[n].astype(bf16), SHARD_SHAPE[n][0])) for n in W_NAMES]
    chip = 2 * lax.axis_index("x") + lax.axis_index("y")
    exchanged = {"first": _all_gather_async(own[:1], [], "all_gather_first", GATHER_FIRST_ID)}
    fill = lambda ts, mine: [lax.dynamic_update_index_in_dim(t, o, chip, 0) for t, o in zip(ts, mine)]

    def first_weights(ready):
        arrived, _ = lax.optimization_barrier((list(exchanged["first"]), ready))
        exchanged["late"] = _all_gather_async(own[1:], [arrived[0][0, 0, :16, :128]], "all_gather_late", GATHER_LATE_ID)
        return _full_weights(dict(zip(W_NAMES[:1], fill(arrived, own[:1]))))

    def late_weights(ready):
        arrived, _ = lax.optimization_barrier((list(exchanged["late"]), ready))
        return _full_weights(dict(zip(W_NAMES[1:], fill(arrived, own[1:]))))

    m["w_in"] = lax.optimization_barrier(m["w_in"])
    v["w_in"] = lax.optimization_barrier(v["w_in"])
    sq_err, grad_x, grads, small = _local_step(x[0], loss_target[0], first_weights, b_forget, norm_mix_pre,
                                               norm_mix_post, norm_ffn_pre, norm_ffn_post, late=late_weights)

    g4 = _sharded_grads(grads)
    stack = lambda t, extra: jnp.concatenate(
        [jnp.pad(t["b_forget"], ((0, 0), (0, D - N_FOX)))] + [t[n] for n in NORMS]
        + [jnp.pad(extra, ((0, SMALL_ROWS - LOSS_ROW - 1), (0, D - extra.shape[1])), constant_values=1.0)], axis=0)
    early, _ = lax.optimization_barrier((list(_pair_swap_early([g4[n] for n in W_NAMES[1:]])), grads["mid_backward"]))
    other = list(_pair_swap([g4["w_in"]])) + early
    parts = [_pair_sum(g4[n], o, "pair_sum_" + n) for n, o in zip(W_NAMES, other)]
    recv_early = _scatter_parts(parts[1:], "scatter_early", SCATTER_EARLY_ID)
    recv_in = _scatter_parts(parts[:1], "scatter_partials", SCATTER_LATE_ID)
    small_all = _share_small(stack(small, sq_err))

    g_shard, delta, new_m, new_v = {}, {}, {}, {}

    def summed(names, parts, recv):
        return [_sum_partials(p, r, "sum_partials_" + n) for n, p, r in zip(names, parts, recv)]

    def update(names, halves, theirs):
        for n, mine, other_half in zip(names, halves, theirs):
            g_shard[n], delta[n], new_m[n], new_v[n] = _adamw_halves(w[n], mine, other_half, m[n], v[n], "adamw_" + n)

    recv_early, _ = lax.optimization_barrier((list(recv_early), parts[0]))
    early_mine = summed(W_NAMES[1:], parts[1:], recv_early)
    recv_in, _ = lax.optimization_barrier((list(recv_in), early_mine))
    mine = summed(W_NAMES[:1], parts[:1], recv_in) + early_mine
    theirs = list(_swap_halves(mine, "swap_halves"))
    update(W_NAMES[:1], mine[:1], theirs[:1])
    (early_theirs, small_all), _ = lax.optimization_barrier(((theirs[1:], small_all), delta["w_in"]))
    update(W_NAMES[1:], early_mine, early_theirs)
    small_sum = _sum_small(small_all)
    loss = small_sum[LOSS_ROW, 0] * (0.5 / D)
    ones = jnp.ones((1, 128), f32)
    sd, sm, sv = _adamw(stack(small_w, ones), small_sum, stack(small_m, ones), stack(small_v, ones), "adamw_small")

    outs = [loss, grad_x[None]]
    for big, st in ((g_shard, small_sum), (delta, sd), (new_m, sm), (new_v, sv)):
        t = {n: _harness_layout(n, big[n]) for n in W_NAMES}
        t["b_forget"] = st[0:1, :N_FOX]
        for i, n in enumerate(NORMS):
            t[n] = st[i + 1:i + 2]
        outs += [t[n] for n in ORDER]
    return tuple(outs)
```

```python
import functools
import math

import jax
import jax.numpy as jnp
import numpy as np
from jax import lax
from jax.experimental import pallas as pl
from jax.experimental.pallas import tpu as pltpu
from jax.experimental.pallas import tpu_sc as plsc

f32 = jnp.float32
bf16 = jnp.bfloat16
SDS = jax.ShapeDtypeStruct
MESH = pl.DeviceIdType.MESH

S = 2048
D = 1024
HD = 64
BLK = 128
N_FOX = 8
FOX_W = N_FOX * HD
DIL_GROUPS = ((128, 1), (512, 4), (2048, 16))
SLOTS = 4
DIL_W = SLOTS * HD
QKV_W = 3 * DIL_W
VR_W = 3 * FOX_W
GF_W = 2 * D + 128
F_FF = 2816
ROPE_DIM = 16
ROPE_THETA = 500000.0
EPS = 1e-6
NEG = -1e30
SCALE = 1.0 / math.sqrt(HD)
IN_COLS = 5896
N_SHARD = 4

ADAM_LR, ADAM_B1, ADAM_B2, ADAM_EPS, ADAM_WD, ADAM_STEP = 0.001, 0.9, 0.999, 1e-08, 0.01, 10

VMEM_V7X = 64 * 1024 * 1024
VMEM_PLAN_MAX = VMEM_V7X - 8 * 1024 * 1024

TM = 512
TQ = 256

W_NAMES = ("w_in", "w_proj_a", "w_proj_b", "w_out", "w_ffn_gate", "w_ffn_up", "w_ffn_down")
TRANSPOSED = ("w_in", "w_ffn_gate", "w_ffn_up")
IN_SHARD = IN_COLS // N_SHARD
IN_SHARD_PAD = 1504
SHARD_SHAPE = dict(w_in=(IN_SHARD_PAD, D), w_proj_a=(DIL_W, D // N_SHARD), w_proj_b=(FOX_W, D // N_SHARD),
                   w_out=(D // N_SHARD, D), w_ffn_gate=(F_FF // N_SHARD, D), w_ffn_up=(F_FF // N_SHARD, D),
                   w_ffn_down=(F_FF // N_SHARD, D))
SMALL_ROWS = 8
LOSS_ROW = 5


def _nbytes(shape, dtype):
    return math.prod(shape) * jnp.dtype(dtype).itemsize


def _params(semantics, block_bytes, temp_bytes=0):
    need = 2 * block_bytes + temp_bytes + (2 << 20)
    return pltpu.CompilerParams(dimension_semantics=semantics, vmem_limit_bytes=int(min(need, VMEM_PLAN_MAX)))


def _row(w, tm=TM):
    return pl.BlockSpec((tm, w), lambda i: (i, 0))


def _vec(w):
    return pl.BlockSpec((1, w), lambda i: (0, 0))


def _mm(pairs, dims, out_dtype, *, tm, tn, name, m_inner=False):
    a0, b0 = pairs[0]
    m_dim = a0.shape[1] if dims == "tn" else a0.shape[0]
    n_dim = b0.shape[0] if dims == "nt" else b0.shape[1]
    contract = {"nn": ((1,), (0,)), "nt": ((1,), (1,)), "tn": ((0,), (0,))}[dims]
    n_pairs = len(pairs)
    assert m_dim % tm == 0 and n_dim % tn == 0, (name, m_dim, n_dim, tm, tn)

    def body(*refs):
        o_ref = refs[-1]
        acc = None
        for p in range(n_pairs):
            a = refs[2 * p][...].astype(bf16)
            b = refs[2 * p + 1][...].astype(bf16)
            t = lax.dot_general(a, b, (contract, ((), ())), preferred_element_type=f32)
            acc = t if acc is None else acc + t
        o_ref[...] = acc.astype(o_ref.dtype)

    if m_inner:
        grid = (n_dim // tn, m_dim // tm)
        mi = lambda j, i: i
        ni = lambda j, i: j
    else:
        grid = (m_dim // tm, n_dim // tn)
        mi = lambda i, j: i
        ni = lambda i, j: j
    in_specs, block_bytes, args = [], 0, []
    for a, b in pairs:
        k_dim = a.shape[0] if dims == "tn" else a.shape[1]
        if dims == "tn":
            in_specs.append(pl.BlockSpec((k_dim, tm), lambda *g: (0, mi(*g))))
        else:
            in_specs.append(pl.BlockSpec((tm, k_dim), lambda *g: (mi(*g), 0)))
        if dims == "nt":
            in_specs.append(pl.BlockSpec((tn, k_dim), lambda *g: (ni(*g), 0)))
        else:
            in_specs.append(pl.BlockSpec((k_dim, tn), lambda *g: (0, ni(*g))))
        block_bytes += _nbytes((tm, k_dim), a.dtype) + _nbytes((tn, k_dim), b.dtype)
        args += [a, b]
    block_bytes += _nbytes((tm, tn), out_dtype)
    temp = _nbytes((tm, tn), f32) * 2 + sum(_nbytes((tm, a.shape[0] if dims == "tn" else a.shape[1]), bf16)
                                            + _nbytes((tn, a.shape[0] if dims == "tn" else a.shape[1]), bf16)
                                            for a, _ in pairs)
    return pl.pallas_call(
        body, grid=grid, in_specs=in_specs,
        out_specs=pl.BlockSpec((tm, tn), lambda *g: (mi(*g), ni(*g))),
        out_shape=SDS((m_dim, n_dim), out_dtype), name=name,
        compiler_params=_params(("parallel", "parallel"), block_bytes, temp),
    )(*args)


def _rms(x, g):
    r = lax.rsqrt(jnp.mean(x * x, axis=-1, keepdims=True) + EPS)
    return x * r * g


def _rms_bwd(x, g, dy):
    r = lax.rsqrt(jnp.mean(x * x, axis=-1, keepdims=True) + EPS)
    xh = x * r
    dxh = dy * g
    dx = r * (dxh - xh * jnp.mean(dxh * xh, axis=-1, keepdims=True))
    return dx, jnp.sum(dy * xh, axis=0, keepdims=True)


def _acc_rows(ref, val):
    @pl.when(pl.program_id(0) == 0)
    def _():
        ref[...] = jnp.zeros_like(ref)
    ref[...] += val


def _norm_fwd(xs, g):
    n = len(xs)

    def body(*refs):
        g = refs[n][...]
        for x_ref, h_ref in zip(refs[:n], refs[n + 1:]):
            h_ref[...] = _rms(x_ref[...], g).astype(bf16)

    return pl.pallas_call(
        body, grid=(S // TM,), in_specs=[_row(D)] * n + [_vec(D)], out_specs=[_row(D)] * n,
        out_shape=[SDS((S, D), bf16)] * n, name="norm_mix_pre",
        compiler_params=_params(("parallel",), 6 * n * TM * D, 8 * n * TM * D))(*xs, g)


def _perm_rows(xs, ds, name):
    n = len(xs)

    def body(*refs):
        outs = iter(refs[n:])
        for x_ref in refs[:n]:
            for d in ds:
                o_ref, rows = next(outs), S // d
                for r in range(d):
                    o_ref[r * rows:(r + 1) * rows, :] = x_ref[pl.ds(r, rows, stride=d), :]

    blk = pl.BlockSpec((S, 128), lambda c: (0, c))
    w = xs[0].shape[1]
    return pl.pallas_call(
        body, grid=(w // 128,), in_specs=[blk] * n, out_specs=[blk] * (n * len(ds)),
        out_shape=[SDS((S, w), f32)] * (n * len(ds)), name=name,
        compiler_params=_params(("parallel",), 4 * S * 128 * n * (1 + len(ds))))(*xs)


def _unperm_sum(nat, perms, ds, name):
    n = len(perms)

    def body(*refs):
        a_ref, o_ref, sc = refs[0], refs[n + 1], refs[n + 2]
        acc = a_ref[...]
        for b_ref, d in zip(refs[1:n + 1], ds):
            rows = S // d
            for r in range(d):
                sc[pl.ds(r, rows, stride=d), :] = b_ref[r * rows:(r + 1) * rows, :]
            acc = acc + sc[...]
        o_ref[...] = acc

    blk = pl.BlockSpec((S, 128), lambda c: (0, c))
    w = nat.shape[1]
    return pl.pallas_call(
        body, grid=(w // 128,), in_specs=[blk] * (n + 1), out_specs=blk, out_shape=SDS((S, w), f32),
        scratch_shapes=[pltpu.VMEM((S, 128), f32)], name=name,
        compiler_params=_params(("parallel",), 4 * S * 128 * (n + 2), 8 * S * 128))(nat, *perms)


def _whole(a):
    return pl.BlockSpec(a.shape, lambda i: (0,) * a.ndim)


def _resid_norm_fwd(x, merged, w_out, g_post, g_pre):
    def body(x_ref, mg_ref, w_ref, gp_ref, gn_ref, mix_ref, x2_ref, h_ref):
        mix = jnp.dot(mg_ref[...], w_ref[...], preferred_element_type=f32)
        x2 = x_ref[...] + _rms(mix, gp_ref[...])
        mix_ref[...] = mix
        x2_ref[...] = x2
        h_ref[...] = _rms(x2, gn_ref[...]).astype(bf16)

    return pl.pallas_call(
        body, grid=(S // TM,), in_specs=[_row(D), _row(D), _whole(w_out), _vec(D), _vec(D)], out_specs=[_row(D)] * 3,
        out_shape=[SDS((S, D), f32), SDS((S, D), f32), SDS((S, D), bf16)], name="proj_out_norm",
        compiler_params=_params(("parallel",), 16 * TM * D + 2 * D * D, 16 * TM * D))(x, merged, w_out, g_post, g_pre)


def _loss_head(x2, a_act, w_down, g_post, target):
    def body(x2_ref, a_ref, w_ref, g_ref, t_ref, loss_ref, dy_ref, dff_ref, dg_ref):
        ff = jnp.dot(a_ref[...], w_ref[...], preferred_element_type=f32)
        g = g_ref[...]
        err = x2_ref[...] + _rms(ff, g) - t_ref[...]
        dy = err * (1.0 / D)
        dff, dg = _rms_bwd(ff, g, dy)
        dy_ref[...] = dy
        dff_ref[...] = dff.astype(bf16)
        _acc_rows(dg_ref, dg)
        _acc_rows(loss_ref, jnp.full((1, 128), jnp.sum(err * err), f32))

    return pl.pallas_call(
        body, grid=(S // TM,), in_specs=[_row(D), _row(F_FF), _whole(w_down), _vec(D), _row(D)],
        out_specs=[_vec(128), _row(D), _row(D), _vec(D)],
        out_shape=[SDS((1, 128), f32), SDS((S, D), f32), SDS((S, D), bf16), SDS((1, D), f32)], name="ffn_down_loss",
        compiler_params=_params(("arbitrary",), 14 * TM * D + 2 * TM * F_FF + 2 * F_FF * D, 28 * TM * D),
    )(x2, a_act, w_down, g_post, target)


def _norm_bwd_mid(dy, d_g, d_u, w_gate_t, w_up_t, x2, mix, g_ffn_pre, g_mix_post):
    def body(dy_ref, dgt_ref, dut_ref, wg_ref, wu_ref, x2_ref, mix_ref, g3_ref, g2_ref, dx2_ref, dmix_ref, dg3_ref, dg2_ref):
        dh = jnp.dot(dgt_ref[...], wg_ref[...], preferred_element_type=f32)
        dh += jnp.dot(dut_ref[...], wu_ref[...], preferred_element_type=f32)
        d3, dg3 = _rms_bwd(x2_ref[...], g3_ref[...], dh)
        dx2 = dy_ref[...] + d3
        dmix, dg2 = _rms_bwd(mix_ref[...], g2_ref[...], dx2)
        dx2_ref[...] = dx2
        dmix_ref[...] = dmix.astype(bf16)
        _acc_rows(dg3_ref, dg3)
        _acc_rows(dg2_ref, dg2)

    tm = TM // 2
    row = lambda w: _row(w, tm)
    return pl.pallas_call(
        body, grid=(S // tm,),
        in_specs=[row(D), row(F_FF), row(F_FF), _whole(w_gate_t), _whole(w_up_t), row(D), row(D), _vec(D), _vec(D)],
        out_specs=[row(D), row(D), _vec(D), _vec(D)],
        out_shape=[SDS((S, D), f32), SDS((S, D), bf16), SDS((1, D), f32), SDS((1, D), f32)], name="ffn_bwd_in_norm",
        compiler_params=_params(("arbitrary",), 18 * tm * D + 4 * tm * F_FF + 4 * F_FF * D, 28 * tm * D),
    )(dy, d_g, d_u, w_gate_t, w_up_t, x2, mix, g_ffn_pre, g_mix_post)


def _norm_bwd_in(dx2, dh1, x, g):
    def body(dx2_ref, dh_ref, x_ref, g_ref, gx_ref, dg_ref):
        d1, dg = _rms_bwd(x_ref[...], g_ref[...], dh_ref[...])
        gx_ref[...] = dx2_ref[...] + d1
        _acc_rows(dg_ref, dg)

    return pl.pallas_call(
        body, grid=(S // TM,), in_specs=[_row(D)] * 3 + [_vec(D)], out_specs=[_row(D), _vec(D)],
        out_shape=[SDS((S, D), f32), SDS((1, D), f32)], name="norm_bwd_in",
        compiler_params=_params(("arbitrary",), 16 * TM * D, 16 * TM * D))(dx2, dh1, x, g)


def _rope_tables():
    half = ROPE_DIM // 2
    inv_freq = np.power(np.float32(ROPE_THETA), -np.arange(0, ROPE_DIM, 2, dtype=np.float32) / np.float32(ROPE_DIM))
    row = np.arange(S)
    groups = []
    for _, d in DIL_GROUPS:
        pos = ((row % (S // d)) * d + row // (S // d)).astype(np.float32)
        ang = pos[:, None] * inv_freq[None, :].astype(np.float32)
        cos, sin = np.cos(ang).astype(np.float32), np.sin(ang).astype(np.float32)
        c = np.concatenate([cos, cos, np.ones((S, HD - ROPE_DIM), np.float32)], axis=1)
        s_lo = np.concatenate([-sin, np.zeros((S, HD - half), np.float32)], axis=1)
        s_hi = np.concatenate([np.zeros((S, half), np.float32), sin, np.zeros((S, HD - ROPE_DIM), np.float32)], axis=1)
        groups.append(np.stack([np.concatenate([t, t], axis=1) for t in (c, s_lo, s_hi)]))
    return jnp.asarray(np.stack(groups))


def _rotate(x, c, lo, hi, sign):
    tile = lambda t: jnp.tile(t, (1, DIL_W // 128))
    return (x * tile(c) + pltpu.roll(x, DIL_W - ROPE_DIM // 2, 1) * (tile(lo) * sign)
            + pltpu.roll(x, ROPE_DIM // 2, 1) * (tile(hi) * sign))


def _table_specs(g):
    return [pl.BlockSpec((None, None, TM, 128), lambda i, k=k: (g, k, i, 0)) for k in range(3)]


def _rope_fwd(g, p_qkv, tables):
    def body(x_ref, c_ref, lo_ref, hi_ref, o_ref):
        c, lo, hi = c_ref[...], lo_ref[...], hi_ref[...]
        for part in range(2):
            cols = slice(part * DIL_W, (part + 1) * DIL_W)
            o_ref[:, cols] = _rotate(x_ref[:, cols], c, lo, hi, 1.0).astype(bf16)
        o_ref[:, 2 * DIL_W:] = x_ref[:, 2 * DIL_W:].astype(bf16)

    return pl.pallas_call(
        body, grid=(S // TM,), in_specs=[_row(QKV_W)] + _table_specs(g), out_specs=_row(QKV_W),
        out_shape=SDS((S, QKV_W), bf16), name=f"rope_fwd_{g}",
        compiler_params=_params(("parallel",), 6 * TM * QKV_W + 12 * TM * 128, 24 * TM * QKV_W))(p_qkv, tables, tables, tables)


def _rope_bwd(g, dq, dk, dv, tables):
    def body(dq_ref, dk_ref, dv_ref, c_ref, lo_ref, hi_ref, o_ref):
        c, lo, hi = c_ref[...], lo_ref[...], hi_ref[...]
        o_ref[:, :DIL_W] = _rotate(dq_ref[...], c, lo, hi, -1.0).astype(bf16)
        o_ref[:, DIL_W:2 * DIL_W] = _rotate(dk_ref[...], c, lo, hi, -1.0).astype(bf16)
        o_ref[:, 2 * DIL_W:] = dv_ref[...].astype(bf16)

    return pl.pallas_call(
        body, grid=(S // TM,), in_specs=[_row(DIL_W)] * 3 + _table_specs(g), out_specs=_row(QKV_W),
        out_shape=SDS((S, QKV_W), bf16), name=f"rope_bwd_{g}",
        compiler_params=_params(("parallel",), 6 * TM * QKV_W + 12 * TM * 128, 24 * TM * QKV_W))(dq, dk, dv, tables, tables, tables)


def _nt(a, b):
    return lax.dot_general(a, b, (((1,), (1,)), ((), ())), preferred_element_type=f32)


def _tn(a, b):
    return lax.dot_general(a, b, (((0,), (0,)), ((), ())), preferred_element_type=f32)


STEP_BLOCKS = 4
STEP_ROWS = STEP_BLOCKS * BLK


def _dil_prev(g, b):
    _, d = DIL_GROUPS[g]
    nb = S // d // BLK
    if nb == 1 or (b == 0 and nb <= STEP_BLOCKS):
        return None
    return "in" if b > 0 else "halo"


def _bnt(a, b):
    return lax.dot_general(a, b, (((2,), (2,)), ((0,), (0,))), preferred_element_type=f32)


def _bnn(a, b):
    return lax.dot_general(a, b, (((2,), (1,)), ((0,), (0,))), preferred_element_type=f32)


def _btn(a, b):
    return lax.dot_general(a, b, (((1,), (1,)), ((0,), (0,))), preferred_element_type=f32)


def _on_tail(x, tail, fn):
    if tail == x.shape[0]:
        return fn(x)
    return jnp.concatenate([x[:-tail], fn(x[-tail:])], axis=0)


def _heads(ref, part):
    n = ref.shape[0] // BLK
    return jnp.stack([ref[b * BLK:(b + 1) * BLK, part * DIL_W + h * HD:part * DIL_W + (h + 1) * HD]
                      for b in range(n) for h in range(SLOTS)])


def _dil_operands(g, qkv_ref, halo_ref):
    q, kc, vc = (_heads(qkv_ref, part) for part in range(3))
    qi = lax.broadcasted_iota(jnp.int32, (1, BLK, BLK), 1)
    kj = lax.broadcasted_iota(jnp.int32, (1, BLK, BLK), 2)
    with_prev = [b for b in range(STEP_BLOCKS) if _dil_prev(g, b) is not None]
    tail = SLOTS * len(with_prev)
    if not tail:
        return q, kc, vc, None, None, kj <= qi, None, 0
    assert with_prev == list(range(STEP_BLOCKS - len(with_prev), STEP_BLOCKS))
    inside = SLOTS * sum(_dil_prev(g, b) == "in" for b in with_prev)
    kp, vp, prev = kc[:inside], vc[:inside], jnp.broadcast_to(kj >= qi, (inside, BLK, BLK))
    if inside < tail:
        no_halo = jnp.where(pl.program_id(0) == 0, BLK + 1, 0)
        kp = jnp.concatenate([_heads(halo_ref, 1), kp], axis=0)
        vp = jnp.concatenate([_heads(halo_ref, 2), vp], axis=0)
        prev = jnp.concatenate([jnp.broadcast_to(kj >= qi + no_halo, (SLOTS, BLK, BLK)), prev], axis=0)
    return q, kc, vc, kp, vp, kj <= qi, prev, tail


def _dil_in_specs(g, n_aux):
    step = lambda w: pl.BlockSpec((STEP_ROWS, w), lambda i: (i, 0))
    halo = [pl.BlockSpec((BLK, QKV_W), lambda i: (jnp.maximum(i * STEP_BLOCKS - 1, 0), 0))]
    needs_halo = _dil_prev(g, 0) == "halo"
    return [step(QKV_W)] + (halo if needs_halo else []) + [step(DIL_W)] * n_aux, needs_halo


def _dil_fwd(g, qkv):
    in_specs, needs_halo = _dil_in_specs(g, 0)

    def body(*refs):
        qkv_ref, halo_ref = refs[0], refs[1] if needs_halo else None
        o_ref, lse_ref = refs[-2:]
        q, kc, vc, kp, vp, cur, prev, tail = _dil_operands(g, qkv_ref, halo_ref)
        sc = jnp.where(cur, _bnt(q, kc) * SCALE, NEG)
        m = jnp.max(sc, axis=-1, keepdims=True)
        if tail:
            sp = jnp.where(prev, _bnt(q[-tail:], kp) * SCALE, NEG)
            m = _on_tail(m, tail, lambda t: jnp.maximum(t, jnp.max(sp, axis=-1, keepdims=True)))
            pp = jnp.exp(sp - m[-tail:])
        pc = jnp.exp(sc - m)
        den = jnp.sum(pc, axis=-1, keepdims=True)
        if tail:
            den = _on_tail(den, tail, lambda t: t + jnp.sum(pp, axis=-1, keepdims=True))
        inv = 1.0 / den
        o = _bnn((pc * inv).astype(bf16), vc)
        if tail:
            o = _on_tail(o, tail, lambda t: t + _bnn((pp * inv[-tail:]).astype(bf16), vp))
        lse = m + jnp.log(den)
        for b in range(STEP_BLOCKS):
            for h in range(SLOTS):
                rows, hs = slice(b * BLK, (b + 1) * BLK), slice(h * HD, (h + 1) * HD)
                o_ref[rows, hs] = o[SLOTS * b + h]
                lse_ref[rows, hs] = jnp.broadcast_to(lse[SLOTS * b + h], (BLK, HD))

    out = pl.BlockSpec((STEP_ROWS, DIL_W), lambda i: (i, 0))
    return pl.pallas_call(
        body, grid=(S // STEP_ROWS,), in_specs=in_specs, out_specs=[out, out], out_shape=[SDS((S, DIL_W), f32)] * 2,
        name=f"dil_fwd_{g}", compiler_params=_params(("parallel",), 12 * STEP_ROWS * DIL_W, 2 << 20),
    )(*([qkv] * (2 if needs_halo else 1)))


def _dil_combine(outs, lses):
    def body(o0, o1, o2, l0, l1, l2, out_ref, lse_ref, so1, so2, sl1, sl2):
        for (_, d), src, dst in ((DIL_GROUPS[1], o1, so1), (DIL_GROUPS[2], o2, so2),
                                 (DIL_GROUPS[1], l1, sl1), (DIL_GROUPS[2], l2, sl2)):
            rows = S // d
            for r in range(d):
                dst[pl.ds(r, rows, stride=d), :] = src[r * rows:(r + 1) * rows, :]
        a, b, c = l0[...], sl1[...], sl2[...]
        m = jnp.maximum(jnp.maximum(a, b), c)
        ea, eb, ec = jnp.exp(a - m), jnp.exp(b - m), jnp.exp(c - m)
        z = ea + eb + ec
        inv = 1.0 / z
        out_ref[...] = (ea * inv) * o0[...] + (eb * inv) * so1[...] + (ec * inv) * so2[...]
        lse_ref[...] = m + jnp.log(z)

    blk = pl.BlockSpec((S, 128), lambda c: (0, c))
    return pl.pallas_call(
        body, grid=(DIL_W // 128,), in_specs=[blk] * 6, out_specs=[blk] * 2,
        out_shape=[SDS((S, DIL_W), f32)] * 2, scratch_shapes=[pltpu.VMEM((S, 128), f32)] * 4, name="dil_combine",
        compiler_params=_params(("parallel",), 32 * S * 128, 32 * S * 128))(*outs, *lses)


def _dil_bwd(g, qkv, d_out, delta, lse):
    in_specs, needs_halo = _dil_in_specs(g, 3)

    def body(*refs):
        qkv_ref, halo_ref = refs[0], refs[1] if needs_halo else None
        do_ref, dl_ref, lse_ref, dq_ref, dk_ref, dv_ref = refs[-6:]
        q, kc, vc, kp, vp, cur, prev, tail = _dil_operands(g, qkv_ref, halo_ref)
        tiles = [(slice(b * BLK, (b + 1) * BLK), h) for b in range(STEP_BLOCKS) for h in range(SLOTS)]
        do = jnp.stack([do_ref[rows, h * HD:(h + 1) * HD] for rows, h in tiles]).astype(bf16)
        lse = jnp.stack([lse_ref[rows, h * HD:h * HD + 1] for rows, h in tiles])
        delta = jnp.stack([dl_ref[rows, h * HD:h * HD + 1] for rows, h in tiles])

        def probs(q, k, mask, lse, do, v, delta):
            p = jnp.exp(jnp.where(mask, _bnt(q, k) * SCALE, NEG) - lse)
            ds = p * (_bnt(do, v) - delta) * SCALE
            return p.astype(bf16), ds.astype(bf16)

        p, ds = probs(q, kc, cur, lse, do, vc, delta)
        dq, dk, dv = _bnn(ds, kc), _btn(ds, q), _btn(p, do)
        if tail:
            p, ds = probs(q[-tail:], kp, prev, lse[-tail:], do[-tail:], vp, delta[-tail:])
            dq = _on_tail(dq, tail, lambda t: t + _bnn(ds, kp))
            dk_p, dv_p = _btn(ds, q[-tail:]), _btn(p, do[-tail:])
            inside = tail - SLOTS if needs_halo else tail
            pad = jnp.zeros((len(tiles) - inside, BLK, HD), f32)
            dk = dk + jnp.concatenate([dk_p[tail - inside:], pad], axis=0)
            dv = dv + jnp.concatenate([dv_p[tail - inside:], pad], axis=0)
        first = pl.multiple_of(pl.program_id(0) * STEP_ROWS, STEP_ROWS)
        for t, (rows, h) in enumerate(tiles):
            hs = slice(h * HD, (h + 1) * HD)
            own = pl.ds(pl.multiple_of(first + rows.start, BLK), BLK)
            dq_ref[rows, hs] = dq[t]
            dk_ref[own, hs] = dk[t]
            dv_ref[own, hs] = dv[t]
        if needs_halo:
            before = pl.ds(pl.multiple_of(jnp.maximum(first - BLK, 0), BLK), BLK)
            for h in range(SLOTS):
                hs = slice(h * HD, (h + 1) * HD)
                dk_ref[before, hs] += dk_p[h]
                dv_ref[before, hs] += dv_p[h]

    whole = pl.BlockSpec((S, DIL_W), lambda i: (0, 0))
    return pl.pallas_call(
        body, grid=(S // STEP_ROWS,), in_specs=in_specs,
        out_specs=[pl.BlockSpec((STEP_ROWS, DIL_W), lambda i: (i, 0)), whole, whole],
        out_shape=[SDS((S, DIL_W), f32)] * 3, name=f"dil_bwd_{g}",
        compiler_params=_params(("arbitrary",), 20 * STEP_ROWS * DIL_W + 8 * S * DIL_W, 2 << 20),
    )(*([qkv] * (2 if needs_halo else 1)), d_out, delta, lse)


def _scan_rows(x, reverse):
    row = lax.broadcasted_iota(jnp.int32, x.shape, 0)
    k = 1
    while k < S:
        if reverse:
            x = x + jnp.where(row < S - k, pltpu.roll(x, S - k, 0), 0.0)
        else:
            x = x + jnp.where(row >= k, pltpu.roll(x, k, 0), 0.0)
        k *= 2
    return x


N_PAIR = N_FOX // 2
_PAIR_Q = pl.BlockSpec((None, S, 128), lambda p: (p, 0, 0))
_PAIR_K = pl.BlockSpec((None, 8, S), lambda p: (p, 0, 0))


def _forget_fwd(fz, b128):
    def body(z_ref, b_ref, fq_ref, fk_ref):
        z = z_ref[...] + b_ref[...]
        logf = jnp.minimum(z, 0.0) - jnp.log1p(jnp.exp(-jnp.abs(z)))
        f_cum = _scan_rows(logf, reverse=False)
        f_cum_t = f_cum.T
        fq_ref[...] = jnp.zeros_like(fq_ref)
        fk_ref[...] = jnp.zeros_like(fk_ref)
        for p in range(N_PAIR):
            fq_ref[p, :, 0:2] = f_cum[:, 2 * p:2 * p + 2]
            fk_ref[p, 0:2, :] = f_cum_t[2 * p:2 * p + 2, :]

    return pl.pallas_call(
        body, grid=(1,), in_specs=[pl.BlockSpec((S, 128), lambda i: (0, 0)), _vec(128)],
        out_specs=[pl.BlockSpec((N_PAIR, S, 128), lambda i: (0, 0, 0)), pl.BlockSpec((N_PAIR, 8, S), lambda i: (0, 0, 0))],
        out_shape=[SDS((N_PAIR, S, 128), f32), SDS((N_PAIR, 8, S), f32)], name="forget_fwd",
        compiler_params=_params(("arbitrary",), 24 * S * 128, 24 * S * 128))(fz, b128)


def _forget_bwd(fz, b128, d_f_cols, d_f_rows):
    def body(z_ref, b_ref, dfc_ref, dfr_ref, dz_ref, db_ref, df_sc):
        z = z_ref[...] + b_ref[...]
        df_sc[...] = jnp.zeros_like(df_sc)
        for p in range(N_PAIR):
            df_sc[:, 2 * p:2 * p + 2] = dfr_ref[p, :, 0:2] + dfc_ref[p].T[:, 0:2]
        dz = _scan_rows(df_sc[...], reverse=True) * jax.nn.sigmoid(-z)
        dz_ref[...] = dz
        db_ref[...] = jnp.sum(dz, axis=0, keepdims=True)

    full = pl.BlockSpec((S, 128), lambda i: (0, 0))
    return pl.pallas_call(
        body, grid=(1,),
        in_specs=[full, _vec(128), pl.BlockSpec((N_PAIR, 8, S), lambda i: (0, 0, 0)), pl.BlockSpec((N_PAIR, S, 128), lambda i: (0, 0, 0))],
        out_specs=[full, _vec(128)], out_shape=[SDS((S, 128), f32), SDS((1, 128), f32)],
        scratch_shapes=[pltpu.VMEM((S, 128), f32)], name="forget_bwd",
        compiler_params=_params(("arbitrary",), 32 * S * 128, 24 * S * 128))(fz, b128, d_f_cols, d_f_rows)


def _fox_scores(q_ref, k_ref, fq_ref, fk_ref, qi, hh):
    n = (qi + 1) * TQ
    rows, hs = slice(qi * TQ, n), slice(hh * HD, (hh + 1) * HD)
    q = q_ref[rows, hs] * SCALE
    s = _nt(q, k_ref[0:n, hs]) + (fq_ref[rows, hh:hh + 1] - fk_ref[hh:hh + 1, 0:n])
    below = lax.broadcasted_iota(jnp.int32, (TQ, TQ), 1) <= lax.broadcasted_iota(jnp.int32, (TQ, TQ), 0)
    diag = jnp.where(below, s[:, n - TQ:], NEG)
    return diag if qi == 0 else jnp.concatenate([s[:, :n - TQ], diag], axis=1)


def _pair_cols(first):
    return pl.BlockSpec((S, 128), lambda p: (0, first + p))


def _fox_fwd(vr, fq, fk):
    def body(q_ref, k_ref, v_ref, fq_ref, fk_ref, o_ref, lse_ref):
        lse_ref[...] = jnp.zeros_like(lse_ref)
        for hh in range(2):
            hs = slice(hh * HD, (hh + 1) * HD)
            for qi in range(S // TQ):
                n = (qi + 1) * TQ
                rows = slice(qi * TQ, n)
                s = _fox_scores(q_ref, k_ref, fq_ref, fk_ref, qi, hh)
                m = jnp.max(s, axis=-1, keepdims=True)
                p = jnp.exp(s - m)
                den = jnp.sum(p, axis=-1, keepdims=True)
                o_ref[rows, hs] = jnp.dot((p * (1.0 / den)).astype(bf16), v_ref[0:n, hs], preferred_element_type=f32)
                lse_ref[rows, hh:hh + 1] = m + jnp.log(den)

    return pl.pallas_call(
        body, grid=(N_PAIR,), in_specs=[_pair_cols(0), _pair_cols(N_PAIR), _pair_cols(2 * N_PAIR), _PAIR_Q, _PAIR_K],
        out_specs=[_pair_cols(0), _PAIR_Q], out_shape=[SDS((S, FOX_W), f32), SDS((N_PAIR, S, 128), f32)],
        name="fox_fwd", compiler_params=_params(("parallel",), 12 * S * 128, 16 * TQ * S),
    )(vr, vr, vr, fq, fk)


def _fox_bwd(vr, fq, fk, lse, d_out, delta):
    def body(q_ref, k_ref, v_ref, do_ref, fq_ref, fk_ref, lse_ref, dl_ref, dq_ref, dk_ref, dv_ref, dfc_ref, dfr_ref,
             dk_sc, dv_sc):
        dfc_ref[...] = jnp.zeros_like(dfc_ref)
        dfr_ref[...] = jnp.zeros_like(dfr_ref)
        for hh in range(2):
            hs = slice(hh * HD, (hh + 1) * HD)
            dk_sc[...] = jnp.zeros_like(dk_sc)
            dv_sc[...] = jnp.zeros_like(dv_sc)
            for qi in range(S // TQ):
                n = (qi + 1) * TQ
                rows = slice(qi * TQ, n)
                q, do, k, v = q_ref[rows, hs], do_ref[rows, hs], k_ref[0:n, hs], v_ref[0:n, hs]
                p = jnp.exp(_fox_scores(q_ref, k_ref, fq_ref, fk_ref, qi, hh) - lse_ref[rows, hh:hh + 1])
                ds = p * (_nt(do, v) - dl_ref[rows, hh:hh + 1])
                dsb = ds.astype(bf16)
                dq_ref[rows, hs] = jnp.dot(dsb, k, preferred_element_type=f32) * SCALE
                dk_sc[0:n, :] += _tn(dsb, q) * SCALE
                dv_sc[0:n, :] += _tn(p.astype(bf16), do)
                dfc_ref[hh:hh + 1, 0:n] -= jnp.sum(ds, axis=0, keepdims=True)
                dfr_ref[rows, hh:hh + 1] = jnp.sum(ds, axis=-1, keepdims=True)
            dk_ref[:, hs] = dk_sc[...]
            dv_ref[:, hs] = dv_sc[...]

    cols = [_pair_cols(k * N_PAIR) for k in range(3)]
    return pl.pallas_call(
        body, grid=(N_PAIR,), in_specs=cols + [_pair_cols(0), _PAIR_Q, _PAIR_K, _PAIR_Q, _PAIR_Q],
        out_specs=[_pair_cols(0)] * 3 + [_PAIR_K, _PAIR_Q],
        out_shape=[SDS((S, FOX_W), f32)] * 3 + [SDS((N_PAIR, 8, S), f32), SDS((N_PAIR, S, 128), f32)],
        scratch_shapes=[pltpu.VMEM((S, HD), f32)] * 2, name="fox_bwd",
        compiler_params=_params(("parallel",), 32 * S * 128, 24 * TQ * S),
    )(vr, vr, vr, d_out, fq, fk, lse, delta)


def _merge_fwd(out_a, out_b, w_a, w_b, gf):
    cw = D // N_SHARD

    def body(oa_ref, ob_ref, wa_ref, wb_ref, ga_ref, gb_ref, ya_ref, yb_ref, mg_ref):
        oa, ob = oa_ref[...].astype(bf16), ob_ref[...].astype(bf16)
        for j in range(N_SHARD):
            cols = slice(j * cw, (j + 1) * cw)
            ya = jnp.dot(oa, wa_ref[j], preferred_element_type=f32)
            yb = jnp.dot(ob, wb_ref[j], preferred_element_type=f32)
            ya_ref[:, cols] = ya
            yb_ref[:, cols] = yb
            mg_ref[:, cols] = (jax.nn.sigmoid(ga_ref[:, cols]) * ya + jax.nn.sigmoid(gb_ref[:, cols]) * yb).astype(bf16)

    full = lambda a: pl.BlockSpec(a.shape, lambda i: (0, 0, 0))
    return pl.pallas_call(
        body, grid=(S // TM,),
        in_specs=[_row(DIL_W), _row(FOX_W), full(w_a), full(w_b), _row(D), pl.BlockSpec((TM, D), lambda i: (i, 1))],
        out_specs=[_row(D)] * 3, out_shape=[SDS((S, D), f32), SDS((S, D), f32), SDS((S, D), bf16)], name="merge_fwd",
        compiler_params=_params(("parallel",), 22 * TM * D + 2 * (DIL_W + FOX_W) * D, 16 * TM * D),
    )(out_a, out_b, w_a, w_b, gf, gf)


def _merge_bwd(d_mix, w_out, ya, yb, gf):
    def body(dx_ref, w_ref, ya_ref, yb_ref, ga_ref, gb_ref, dya_ref, dyb_ref, dg_ref):
        dm = _nt(dx_ref[...], w_ref[...])
        sa, sb = jax.nn.sigmoid(ga_ref[...]), jax.nn.sigmoid(gb_ref[...])
        dya_ref[...] = (dm * sa).astype(bf16)
        dyb_ref[...] = (dm * sb).astype(bf16)
        dg_ref[:, :D] = (dm * ya_ref[...] * sa * (1.0 - sa)).astype(bf16)
        dg_ref[:, D:] = (dm * yb_ref[...] * sb * (1.0 - sb)).astype(bf16)

    return pl.pallas_call(
        body, grid=(S // TM,),
        in_specs=[_row(D), _whole(w_out)] + [_row(D)] * 3 + [pl.BlockSpec((TM, D), lambda i: (i, 1))],
        out_specs=[_row(D), _row(D), _row(2 * D)],
        out_shape=[SDS((S, D), bf16), SDS((S, D), bf16), SDS((S, 2 * D), bf16)], name="proj_out_bwd_merge",
        compiler_params=_params(("parallel",), 26 * TM * D + 2 * D * D, 28 * TM * D))(d_mix, w_out, ya, yb, gf, gf)


def _branch_bwd(d_ya, d_yb, w_a, w_b, out_a, out_b):
    cw = D // N_SHARD

    def body(dya_ref, dyb_ref, wa_ref, wb_ref, oa_ref, ob_ref, doa_ref, dla_ref, dob_ref, dlb_ref):
        doa = jnp.zeros((TM, DIL_W), f32)
        dob = jnp.zeros((TM, FOX_W), f32)
        for j in range(N_SHARD):
            cols = slice(j * cw, (j + 1) * cw)
            doa += _nt(dya_ref[:, cols], wa_ref[j])
            dob += _nt(dyb_ref[:, cols], wb_ref[j])
        doa_ref[...] = doa
        dob_ref[...] = dob.astype(bf16)
        prod_a = doa * oa_ref[...]
        for h in range(SLOTS):
            hs = slice(h * HD, (h + 1) * HD)
            dla_ref[:, hs] = jnp.broadcast_to(jnp.sum(prod_a[:, hs], axis=-1, keepdims=True), (TM, HD))
        prod_b = dob * ob_ref[...]
        dlb_ref[...] = jnp.zeros_like(dlb_ref)
        for h in range(N_FOX):
            dlb_ref[h // 2, :, h % 2:h % 2 + 1] = jnp.sum(prod_b[:, h * HD:(h + 1) * HD], axis=-1, keepdims=True)

    full = lambda a: pl.BlockSpec(a.shape, lambda i: (0, 0, 0))
    return pl.pallas_call(
        body, grid=(S // TM,),
        in_specs=[_row(D), _row(D), full(w_a), full(w_b), _row(DIL_W), _row(FOX_W)],
        out_specs=[_row(DIL_W), _row(DIL_W), _row(FOX_W), pl.BlockSpec((N_PAIR, TM, 128), lambda i: (0, i, 0))],
        out_shape=[SDS((S, DIL_W), f32), SDS((S, DIL_W), f32), SDS((S, FOX_W), bf16), SDS((N_PAIR, S, 128), f32)],
        name="branch_bwd", compiler_params=_params(("parallel",), 8 * TM * D + 2 * (DIL_W + FOX_W) * D, 8 * TM * D),
    )(d_ya, d_yb, w_a, w_b, out_a, out_b)


def _branch_grads(out_a, out_b, d_ya, d_yb):
    cw = D // N_SHARD

    def body(oa_ref, ob_ref, dya_ref, dyb_ref, ga_ref, gb_ref):
        ga_ref[...] = _tn(oa_ref[...].astype(bf16), dya_ref[...]).astype(bf16)
        gb_ref[...] = _tn(ob_ref[...].astype(bf16), dyb_ref[...]).astype(bf16)

    whole = lambda w: pl.BlockSpec((S, w), lambda j: (0, 0))
    cols = pl.BlockSpec((S, cw), lambda j: (0, j))
    return pl.pallas_call(
        body, grid=(N_SHARD,), in_specs=[whole(DIL_W), whole(FOX_W), cols, cols],
        out_specs=[pl.BlockSpec((None, DIL_W, cw), lambda j: (j, 0, 0)), pl.BlockSpec((None, FOX_W, cw), lambda j: (j, 0, 0))],
        out_shape=[SDS((N_SHARD, DIL_W, cw), bf16), SDS((N_SHARD, FOX_W, cw), bf16)], name="grad_w_proj_ab",
        compiler_params=_params(("parallel",), 4 * S * (DIL_W + FOX_W) + 4 * S * cw + 4 * (DIL_W + FOX_W) * cw,
                                4 * S * (DIL_W + FOX_W)))(out_a, out_b, d_ya, d_yb)


FF_TN = F_FF // 2
FF_TM = 1024


def _ffn_fwd(h, w_gate_t, w_up_t):
    def body(h_ref, wg_ref, wu_ref, g_ref, u_ref, a_ref):
        hb = h_ref[...]
        g = _nt(hb, wg_ref[...])
        u = _nt(hb, wu_ref[...])
        g_ref[...] = g
        u_ref[...] = u
        a_ref[...] = (g * jax.nn.sigmoid(g) * u).astype(bf16)

    tile = pl.BlockSpec((FF_TM, FF_TN), lambda j, i: (i, j))
    wspec = pl.BlockSpec((FF_TN, D), lambda j, i: (j, 0))
    return pl.pallas_call(
        body, grid=(F_FF // FF_TN, S // FF_TM),
        in_specs=[pl.BlockSpec((FF_TM, D), lambda j, i: (i, 0)), wspec, wspec], out_specs=[tile] * 3,
        out_shape=[SDS((S, F_FF), f32), SDS((S, F_FF), f32), SDS((S, F_FF), bf16)], name="ffn_fwd",
        compiler_params=_params(("parallel", "parallel"), 2 * FF_TM * D + 4 * D * FF_TN + 10 * FF_TM * FF_TN, 16 * FF_TM * FF_TN),
    )(h, w_gate_t, w_up_t)


def _ffn_bwd_act(d_ff, w_down, g_act, u_act):
    def body(d_ref, wd_ref, g_ref, u_ref, dg_ref, du_ref):
        da = _nt(d_ref[...], wd_ref[...])
        g = g_ref[...]
        sg = jax.nn.sigmoid(g)
        du_ref[...] = (da * g * sg).astype(bf16)
        dg_ref[...] = (da * u_ref[...] * sg * (1.0 + g * (1.0 - sg))).astype(bf16)

    tile = pl.BlockSpec((FF_TM, FF_TN), lambda j, i: (i, j))
    return pl.pallas_call(
        body, grid=(F_FF // FF_TN, S // FF_TM),
        in_specs=[pl.BlockSpec((FF_TM, D), lambda j, i: (i, 0)), pl.BlockSpec((FF_TN, D), lambda j, i: (j, 0)), tile, tile],
        out_specs=[tile, tile], out_shape=[SDS((S, F_FF), bf16)] * 2, name="ffn_bwd_act",
        compiler_params=_params(("parallel", "parallel"), 2 * FF_TM * D + 2 * D * FF_TN + 12 * FF_TM * FF_TN, 16 * FF_TM * FF_TN),
    )(d_ff, w_down, g_act, u_act)


def _row_tile(rows):
    return next(t for t in (376, 128, 176, 64, 32, 16, 8) if rows % t == 0)


def _adamw_math(w, g, m, v):
    c1 = 1.0 - ADAM_B1 ** ADAM_STEP
    c2 = 1.0 - ADAM_B2 ** ADAM_STEP
    m_new = ADAM_B1 * m + (1.0 - ADAM_B1) * g
    v_new = ADAM_B2 * v + (1.0 - ADAM_B2) * (g * g)
    return -ADAM_LR * ((m_new / c1) / (jnp.sqrt(v_new / c2) + ADAM_EPS) + ADAM_WD * w), m_new, v_new


def _adamw(w, g, m, v, name):
    rows, cols = w.shape
    tm = _row_tile(rows)

    def body(w_ref, g_ref, m_ref, v_ref, d_ref, nm_ref, nv_ref):
        d_ref[...], nm_ref[...], nv_ref[...] = _adamw_math(w_ref[...], g_ref[...], m_ref[...], v_ref[...])

    spec = pl.BlockSpec((tm, cols), lambda i: (i, 0))
    return pl.pallas_call(
        body, grid=(rows // tm,), in_specs=[spec] * 4, out_specs=[spec] * 3, out_shape=[SDS(w.shape, f32)] * 3,
        name=name, compiler_params=_params(("parallel",), 28 * tm * cols, 16 * tm * cols))(w, g, m, v)


def _adamw_halves(w, g_mine, g_theirs, m, v, name):
    cols = w.shape[1]
    tm = _row_tile(g_mine.shape[0])
    per_half = g_mine.shape[0] // tm
    assert 2 * g_mine.shape[0] - w.shape[0] < tm
    core = lax.axis_index("c").astype(jnp.int32).reshape(1)

    def body(c_ref, w_ref, gm_ref, gt_ref, m_ref, v_ref, g_ref, d_ref, nm_ref, nv_ref):
        mine = pl.program_id(0) // per_half == c_ref[0]
        g = jnp.where(mine, gm_ref[...], gt_ref[...])
        g_ref[...] = g
        d_ref[...], nm_ref[...], nv_ref[...] = _adamw_math(w_ref[...], g, m_ref[...], v_ref[...])

    spec = pl.BlockSpec((tm, cols), lambda i, c_ref: (i, 0))
    in_half = lambda i, first: jnp.clip(i - first * per_half, 0, per_half - 1)
    grid_spec = pltpu.PrefetchScalarGridSpec(
        num_scalar_prefetch=1, grid=(2 * per_half,),
        in_specs=[spec, pl.BlockSpec((tm, cols), lambda i, c_ref: (in_half(i, c_ref[0]), 0)),
                  pl.BlockSpec((tm, cols), lambda i, c_ref: (in_half(i, 1 - c_ref[0]), 0)), spec, spec],
        out_specs=[spec] * 4)
    return pl.pallas_call(
        body, grid_spec=grid_spec, out_shape=[SDS(w.shape, f32)] * 4, name=name,
        compiler_params=_params(("parallel",), 36 * tm * cols, 16 * tm * cols))(core, w, g_mine, g_theirs, m, v)


_ANY = pl.BlockSpec(memory_space=pl.ANY)


def _place():
    x, y, c = lax.axis_index("x"), lax.axis_index("y"), lax.axis_index("c")
    chips = [(1 - x, y), (x, 1 - y), (1 - x, 1 - y)]
    return x, y, c, chips


def _halved(t):
    return t.reshape(t.shape[:-2] + (2, t.shape[-2] // 2, t.shape[-1]))


def _gather_body(src, out, send_ici, recv_ici, send_d2d, recv_d2d):
    x, y, c, chips = _place()
    sibling = (x, y, 1 - c)
    me_j = 2 * x + y
    sends = []
    for a in range(len(src)):
        for p in range(3):
            cp = pltpu.make_async_remote_copy(
                src_ref=src[a].at[c], dst_ref=out[a].at[me_j, c], send_sem=send_ici.at[a, p],
                recv_sem=recv_ici.at[a, p], device_id=(*chips[p], c), device_id_type=MESH)
            cp.start()
            sends.append(cp)
    for a in range(len(src)):
        for p, (px, py) in enumerate(chips):
            blk = out[a].at[2 * px + py, c]
            pltpu.make_async_remote_copy(
                src_ref=blk, dst_ref=blk, send_sem=send_ici.at[a, p], recv_sem=recv_ici.at[a, p],
                device_id=sibling, device_id_type=MESH).wait_recv()
            fw = pltpu.make_async_remote_copy(
                src_ref=blk, dst_ref=blk, send_sem=send_d2d.at[a, p], recv_sem=recv_d2d.at[a, p],
                device_id=sibling, device_id_type=MESH)
            fw.start()
            sends.append(fw)
    for a in range(len(src)):
        for p, (px, py) in enumerate(chips):
            blk = out[a].at[2 * px + py, 1 - c]
            pltpu.make_async_remote_copy(
                src_ref=blk, dst_ref=blk, send_sem=send_d2d.at[a, p], recv_sem=recv_d2d.at[a, p],
                device_id=sibling, device_id_type=MESH).wait_recv()
    for cp in sends:
        cp.wait_send()


def _handshake(peers):
    barrier = pltpu.get_barrier_semaphore()
    for peer in peers:
        pl.semaphore_signal(barrier, inc=1, device_id=peer, device_id_type=MESH)
    pl.semaphore_wait(barrier, len(peers))


_SEQUENCER = dict(axis_name="sequencer", num_cores=1)
GATHER_LATE_ID, SCATTER_EARLY_ID, SWAP_EARLY_ID, GATHER_FIRST_ID, SCATTER_LATE_ID, SHARE_SMALL_ID = 1, 2, 3, 4, 5, 6


def _all_gather_async(shards, after, name, collective_id):
    n, k = len(shards), len(after)

    def body(*refs):
        x, y, c, chips = _place()
        _handshake([(*chip, c) for chip in chips] + [(x, y, 1 - c)])
        _gather_body(refs[:n], refs[n + k:2 * n + k], *refs[2 * n + k:])

    return pl.kernel(
        body, out_type=[SDS((N_SHARD,) + t.shape, t.dtype) for t in shards],
        mesh=plsc.ScalarSubcoreMesh(**_SEQUENCER), scratch_types=[pltpu.SemaphoreType.DMA((n, 3))] * 4,
        compiler_params=pltpu.CompilerParams(collective_id=collective_id), name=name)(*shards, *after)


def _pair_swap(grads):
    n = len(grads)

    def body(*refs):
        src, out, send_sems, recv_sems = refs[:n], refs[n:2 * n], refs[2 * n], refs[2 * n + 1]
        x, y, c, _ = _place()
        copies = [pltpu.make_async_remote_copy(
            src_ref=src[a].at[:, 1 - c], dst_ref=out[a], send_sem=send_sems.at[a], recv_sem=recv_sems.at[a],
            device_id=(x, y, 1 - c), device_id_type=MESH) for a in range(n)]
        for cp in copies:
            cp.start()
        for cp in copies:
            cp.wait()

    return pl.pallas_call(
        body, in_specs=[_ANY] * n, out_specs=[_ANY] * n,
        out_shape=[SDS((N_SHARD,) + t.shape[2:], t.dtype) for t in grads],
        scratch_shapes=[pltpu.SemaphoreType.DMA((n,)), pltpu.SemaphoreType.DMA((n,))], name="pair_swap",
        compiler_params=pltpu.CompilerParams(has_side_effects=True))(*grads)


def _pair_swap_early(grads):
    n = len(grads)

    def body(*refs):
        src, out, send_sems, recv_sems = refs[:n], refs[n:2 * n], refs[2 * n], refs[2 * n + 1]
        x, y, c, _ = _place()
        _handshake([(x, y, 1 - c)])
        copies = [pltpu.make_async_remote_copy(
            src_ref=src[a].at[:, 1 - c], dst_ref=out[a], send_sem=send_sems.at[a], recv_sem=recv_sems.at[a],
            device_id=(x, y, 1 - c), device_id_type=MESH) for a in range(n)]
        for cp in copies:
            cp.start()
        for cp in copies:
            cp.wait()

    return pl.kernel(
        body, out_type=[SDS((N_SHARD,) + t.shape[2:], t.dtype) for t in grads],
        mesh=plsc.ScalarSubcoreMesh(**_SEQUENCER), scratch_types=[pltpu.SemaphoreType.DMA((n,))] * 2,
        compiler_params=pltpu.CompilerParams(collective_id=SWAP_EARLY_ID), name="pair_swap_early")(*grads)


def _scatter_parts(parts, name, collective_id):
    n = len(parts)

    def body(*refs):
        part, recv, send_sems, recv_sems = refs[:n], refs[n:2 * n], refs[2 * n], refs[2 * n + 1]
        x, y, c, chips = _place()
        _handshake([(*chip, c) for chip in chips])
        me_j = 2 * x + y
        sends = []
        for a in range(n):
            for p, (px, py) in enumerate(chips):
                cp = pltpu.make_async_remote_copy(
                    src_ref=part[a].at[2 * px + py], dst_ref=recv[a].at[me_j], send_sem=send_sems.at[a, p],
                    recv_sem=recv_sems.at[a, p], device_id=(px, py, c), device_id_type=MESH)
                cp.start()
                sends.append(cp)
        for a in range(n):
            for p, (px, py) in enumerate(chips):
                slot = recv[a].at[2 * px + py]
                pltpu.make_async_remote_copy(
                    src_ref=slot, dst_ref=slot, send_sem=send_sems.at[a, p], recv_sem=recv_sems.at[a, p],
                    device_id=(px, py, c), device_id_type=MESH).wait_recv()
        for cp in sends:
            cp.wait_send()

    return pl.kernel(
        body, out_type=[SDS(t.shape, t.dtype) for t in parts],
        mesh=plsc.ScalarSubcoreMesh(**_SEQUENCER), scratch_types=[pltpu.SemaphoreType.DMA((n, 3))] * 2,
        compiler_params=pltpu.CompilerParams(collective_id=collective_id), name=name)(*parts)


def _pair_sum(grads, other, name):
    _, _, rows, cols = grads.shape
    tr = _row_tile(rows)
    core = lax.axis_index("c").astype(jnp.int32).reshape(1)

    def body(c_ref, g_ref, o_ref, out_ref):
        out_ref[...] = (g_ref[...].astype(f32) + o_ref[...].astype(f32)).astype(bf16)

    grid_spec = pltpu.PrefetchScalarGridSpec(
        num_scalar_prefetch=1, grid=(N_SHARD, rows // tr),
        in_specs=[pl.BlockSpec((None, None, tr, cols), lambda j, i, c_ref: (j, c_ref[0], i, 0)),
                  pl.BlockSpec((None, tr, cols), lambda j, i, c_ref: (j, i, 0))],
        out_specs=pl.BlockSpec((None, tr, cols), lambda j, i, c_ref: (j, i, 0)))
    return pl.pallas_call(
        body, grid_spec=grid_spec, out_shape=SDS((N_SHARD, rows, cols), bf16), name=name,
        compiler_params=_params(("parallel", "parallel"), 10 * tr * cols, 12 * tr * cols))(core, grads, other)


def _share_small(small):
    def body(small_ref, small_all_ref, ssend, srecv, local_sem):
        x, y, c, _ = _place()
        flip = lambda a, bit: 1 - a if bit else a
        peers = [(flip(x, k & 4), flip(y, k & 2), flip(c, k & 1)) for k in range(1, 8)]
        _handshake(peers)
        me_dev = 4 * x + 2 * y + c
        own = pltpu.make_async_copy(small_ref, small_all_ref.at[me_dev], local_sem)
        own.start()
        sends = []
        for k, to in enumerate(peers):
            cp = pltpu.make_async_remote_copy(
                src_ref=small_ref, dst_ref=small_all_ref.at[me_dev],
                send_sem=ssend.at[k], recv_sem=srecv.at[k], device_id=to, device_id_type=MESH)
            cp.start()
            sends.append(cp)
        for k, (px, py, pc) in enumerate(peers):
            slot = small_all_ref.at[4 * px + 2 * py + pc]
            pltpu.make_async_remote_copy(
                src_ref=slot, dst_ref=slot, send_sem=ssend.at[k], recv_sem=srecv.at[k],
                device_id=(px, py, pc), device_id_type=MESH).wait_recv()
        for cp in sends:
            cp.wait_send()
        own.wait()

    return pl.kernel(
        body, out_type=SDS((8, SMALL_ROWS, D), f32), mesh=plsc.ScalarSubcoreMesh(**_SEQUENCER),
        scratch_types=[pltpu.SemaphoreType.DMA((7,)), pltpu.SemaphoreType.DMA((7,)), pltpu.SemaphoreType.DMA],
        compiler_params=pltpu.CompilerParams(collective_id=SHARE_SMALL_ID), name="share_small")(small)


def _sum_partials(part, recv, name):
    _, rows, cols = recv.shape
    me = (2 * lax.axis_index("x") + lax.axis_index("y")).astype(jnp.int32).reshape(1)

    def body(me_ref, part_ref, recv_ref, out_ref, buf, sems):
        def fetch(j, own):
            return pltpu.make_async_copy(part_ref.at[j] if own else recv_ref.at[j], buf.at[j], sems.at[j])

        for j in range(N_SHARD):
            pl.when(me_ref[0] == j)(fetch(j, True).start)
            pl.when(me_ref[0] != j)(fetch(j, False).start)
        acc = None
        for j in range(N_SHARD):
            fetch(j, False).wait()
            term = buf[j].astype(f32)
            acc = term if acc is None else acc + term
        out_ref[...] = acc

    grid_spec = pltpu.PrefetchScalarGridSpec(
        num_scalar_prefetch=1, grid=(1,), in_specs=[_ANY, _ANY],
        out_specs=pl.BlockSpec((rows, cols), lambda i, me_ref: (0, 0)),
        scratch_shapes=[pltpu.VMEM((N_SHARD, rows, cols), bf16), pltpu.SemaphoreType.DMA((N_SHARD,))])
    return pl.pallas_call(
        body, grid_spec=grid_spec, out_shape=SDS((rows, cols), f32), name=name,
        compiler_params=_params(("arbitrary",), 4 * rows * cols, 16 * rows * cols))(me, part, recv)


def _sum_small(small_all):
    def body(small_ref, out_ref):
        tot = small_ref[0]
        for k in range(1, 8):
            tot = tot + small_ref[k]
        out_ref[...] = tot

    return pl.pallas_call(
        body, grid=(1,), in_specs=[pl.BlockSpec((8, SMALL_ROWS, D), lambda i: (0, 0, 0))],
        out_specs=pl.BlockSpec((SMALL_ROWS, D), lambda i: (0, 0)), out_shape=SDS((SMALL_ROWS, D), f32),
        name="sum_small", compiler_params=_params(("arbitrary",), 36 * SMALL_ROWS * D))(small_all)


def _swap_halves(halves, name):
    n = len(halves)

    def body(*refs):
        src, out, send_sems, recv_sems = refs[:n], refs[n:2 * n], refs[2 * n], refs[2 * n + 1]
        x, y, c, _ = _place()
        copies = [pltpu.make_async_remote_copy(
            src_ref=src[a], dst_ref=out[a], send_sem=send_sems.at[a], recv_sem=recv_sems.at[a],
            device_id=(x, y, 1 - c), device_id_type=MESH) for a in range(n)]
        for cp in copies:
            cp.start()
        for cp in copies:
            cp.wait()

    return pl.pallas_call(
        body, in_specs=[_ANY] * n, out_specs=[_ANY] * n, out_shape=[SDS(t.shape, f32) for t in halves],
        scratch_shapes=[pltpu.SemaphoreType.DMA((n,))] * 2, name=name,
        compiler_params=pltpu.CompilerParams(has_side_effects=True))(*halves)


def _kernel_layout(name, t):
    t = t[0]
    return jnp.swapaxes(t, 0, 1) if name in TRANSPOSED else t


def _harness_layout(name, t):
    if name in TRANSPOSED:
        t = jnp.swapaxes(t, 0, 1)
    return t[None]


def _pad_rows(t, rows):
    return t if t.shape[0] == rows else jnp.pad(t, ((0, rows - t.shape[0]), (0, 0)))


_QA, _KA, _VA, _QB, _F, _GAB = 0, 768, 1536, 2304, 3840, 3848


def _spans(a, b):
    return [(j, max(a, j * IN_SHARD) - j * IN_SHARD, max(a, j * IN_SHARD) - a,
             min(b, (j + 1) * IN_SHARD) - max(a, j * IN_SHARD))
            for j in range(N_SHARD) if max(a, j * IN_SHARD) < min(b, (j + 1) * IN_SHARD)]


_LANES = pl.BlockSpec((N_SHARD, IN_SHARD_PAD, 128), lambda c: (0, 0, c))


def _split_w_in(shards):
    group = [[(o + g * DIL_W, o + (g + 1) * DIL_W) for o in (_QA, _KA, _VA)] for g in range(3)]
    fox = [[(_QB + k * FOX_W, _QB + (k + 1) * FOX_W)] for k in range(3)]
    wanted = group + fox + [[(_QB, _F)], [(_F, _GAB)], [(_GAB, IN_COLS)]]
    rows = [sum(b - a for a, b in w) for w in wanted]
    rows[7] = 128

    def body(s_ref, *o_refs):
        for o_ref, want in zip(o_refs, wanted):
            at = 0
            for a, b in want:
                for j, src, off, n in _spans(a, b):
                    o_ref[at + off:at + off + n, :] = s_ref[j, src:src + n, :]
                at += b - a
        o_refs[7][N_FOX:, :] = jnp.zeros((128 - N_FOX, 128), bf16)

    return pl.pallas_call(
        body, grid=(D // 128,), in_specs=[_LANES], out_specs=[pl.BlockSpec((r, 128), lambda c: (0, c)) for r in rows],
        out_shape=[SDS((r, D), bf16) for r in rows], name="split_w_in",
        compiler_params=_params(("parallel",), 2 * 128 * (N_SHARD * IN_SHARD_PAD + sum(rows))))(shards)


def _join_w_in(g_a, g_fox, g_f, g_gab):
    parts = [(g_a[k], o, o + DIL_W) for o in (0, DIL_W, 2 * DIL_W) for k in range(3)]
    parts += [(t, 0, FOX_W) for t in g_fox] + [(g_f, 0, N_FOX), (g_gab, 0, 2 * D)]
    arrays = list(g_a) + list(g_fox) + [g_f, g_gab]
    index = {id(t): i for i, t in enumerate(arrays)}

    def body(*refs):
        o_ref = refs[-1]
        o_ref[:, IN_SHARD:, :] = jnp.zeros((N_SHARD, IN_SHARD_PAD - IN_SHARD, 128), bf16)
        at = 0
        for t, lo, hi in parts:
            src_ref = refs[index[id(t)]]
            for j, dst, off, n in _spans(at, at + hi - lo):
                o_ref[j, dst:dst + n, :] = src_ref[lo + off:lo + off + n, :].astype(bf16)
            at += hi - lo

    return pl.pallas_call(
        body, grid=(D // 128,), in_specs=[pl.BlockSpec((t.shape[0], 128), lambda c: (0, c)) for t in arrays],
        out_specs=_LANES, out_shape=SDS((N_SHARD, IN_SHARD_PAD, D), bf16), name="join_w_in",
        compiler_params=_params(("parallel",), 2 * 128 * (N_SHARD * IN_SHARD_PAD + sum(t.shape[0] for t in arrays))),
    )(*arrays)


def _full_weights(gathered):
    full = {n: t.reshape((N_SHARD,) + SHARD_SHAPE[n]) for n, t in gathered.items()}
    out = {}
    if "w_in" in full:
        pieces = _split_w_in(full["w_in"])
        out.update(w_a_t=pieces[0:3], w_fox_t=pieces[3:6], w_vr_t=pieces[6], w_f_t=pieces[7], w_gab_t=pieces[8])
    if "w_out" in full:
        out.update(
            w_a4=full["w_proj_a"],
            w_b4=full["w_proj_b"],
            w_out=full["w_out"].reshape(D, D),
            w_gate_t=full["w_ffn_gate"].reshape(F_FF, D),
            w_up_t=full["w_ffn_up"].reshape(F_FF, D),
            w_down=full["w_ffn_down"].reshape(F_FF, D))
    return out


def _sharded_grads(g):
    full = dict(w_in=_join_w_in(g["w_a_t"], g["w_fox_t"], g["w_f_t"], g["w_gab_t"]), w_proj_a=g["w_a4"],
                w_proj_b=g["w_b4"], w_out=g["w_out"], w_ffn_gate=g["w_gate_t"], w_ffn_up=g["w_up_t"],
                w_ffn_down=g["w_down"])
    return {n: _halved(full[n].reshape((N_SHARD,) + SHARD_SHAPE[n])) for n in W_NAMES}


def _local_step(x, target, wt, b_forget, g_mix_pre, g_mix_post, g_ffn_pre, g_ffn_post, late=None):
    tables = _rope_tables()
    b128 = jnp.pad(b_forget, ((0, 0), (0, 128 - N_FOX)))
    dils = tuple(d for _, d in DIL_GROUPS[1:])

    hs = _norm_fwd([x] + list(_perm_rows([x], dils, "perm_x")), g_mix_pre)
    h1 = hs[0]
    if callable(wt):
        wt = wt(h1)
    qkv = [_rope_fwd(g, _mm([(hs[g], wt["w_a_t"][g])], "nt", f32, tm=1024, tn=QKV_W, name=f"proj_a_{g}"), tables)
           for g in range(3)]
    vr = _mm([(h1, wt["w_vr_t"])], "nt", bf16, tm=1024, tn=VR_W // 2, name="proj_vr")
    gab = _mm([(h1, wt["w_gab_t"])], "nt", f32, tm=512, tn=2 * D, name="proj_gab")
    fz = _mm([(h1, wt["w_f_t"])], "nt", f32, tm=1024, tn=128, name="proj_f")
    dil = [_dil_fwd(g, qkv[g]) for g in range(3)]
    out_a, lse_a = _dil_combine([o for o, _ in dil], [l for _, l in dil])
    f_q, f_k = _forget_fwd(fz, b128)
    out_b, lse_b = _fox_fwd(vr, f_q, f_k)
    if late is not None:
        wt = {**wt, **late(out_b)}
    ya, yb, merged = _merge_fwd(out_a, out_b, wt["w_a4"], wt["w_b4"], gab)
    mix, x2, h3 = _resid_norm_fwd(x, merged, wt["w_out"], g_mix_post, g_ffn_pre)
    g_act, u_act, a_act = _ffn_fwd(h3, wt["w_gate_t"], wt["w_up_t"])
    sq_err, dy, d_ff, dg_ffn_post = _loss_head(x2, a_act, wt["w_down"], g_ffn_post, target)

    grads = {}
    d_g, d_u = _ffn_bwd_act(d_ff, wt["w_down"], g_act, u_act)
    grads["w_down"] = _mm([(a_act, d_ff)], "tn", bf16, tm=FF_TN, tn=D, name="grad_w_down")
    grads["w_gate_t"] = _mm([(d_g, h3)], "tn", bf16, tm=FF_TN, tn=D, name="grad_w_gate")
    grads["w_up_t"] = _mm([(d_u, h3)], "tn", bf16, tm=FF_TN, tn=D, name="grad_w_up")
    dx2, d_mix, dg_ffn_pre, dg_mix_post = _norm_bwd_mid(dy, d_g, d_u, wt["w_gate_t"], wt["w_up_t"], x2, mix,
                                                        g_ffn_pre, g_mix_post)

    grads["w_out"] = _mm([(merged, d_mix)], "tn", bf16, tm=D, tn=D, name="grad_w_out")
    d_ya, d_yb, d_gab = _merge_bwd(d_mix, wt["w_out"], ya, yb, gab)
    grads["w_a4"], grads["w_b4"] = _branch_grads(out_a, out_b, d_ya, d_yb)
    d_out_a, delta_a, d_out_b, delta_b = _branch_bwd(d_ya, d_yb, wt["w_a4"], wt["w_b4"], out_a, out_b)

    perm = _perm_rows([d_out_a, delta_a, lse_a], dils, "perm_dil_bwd")
    aux = [(d_out_a, delta_a, lse_a)] + [tuple(perm[k * len(dils) + i] for k in range(3)) for i in range(len(dils))]
    d_qkv = []
    for g in range(3):
        dq, dk, dv = _dil_bwd(g, qkv[g], *aux[g])
        d_qkv.append(_rope_bwd(g, dq, dk, dv, tables))
    *d_fox, d_f_cols, d_f_rows = _fox_bwd(vr, f_q, f_k, lse_b, d_out_b, delta_b)
    d_z, d_b128 = _forget_bwd(fz, b128, d_f_cols, d_f_rows)

    grads["w_a_t"] = [_mm([(d_qkv[g], hs[g])], "tn", bf16, tm=QKV_W, tn=D, name=f"grad_w_a_{g}") for g in range(3)]
    grads["w_fox_t"] = [_mm([(d_fox[k], h1)], "tn", bf16, tm=FOX_W, tn=D, name=f"grad_w_fox_{k}") for k in range(3)]
    grads["w_gab_t"] = _mm([(d_gab, h1)], "tn", bf16, tm=D, tn=D, name="grad_w_gab")
    grads["w_f_t"] = _mm([(d_z, h1)], "tn", bf16, tm=128, tn=D, name="grad_w_f")
    d_h1_nat = _mm([(d_qkv[0], wt["w_a_t"][0])] + list(zip(d_fox, wt["w_fox_t"]))
                   + [(d_gab, wt["w_gab_t"]), (d_z, wt["w_f_t"])], "nn", f32, tm=512, tn=D, name="proj_in_bwd")
    d_h1_dil = [_mm([(d_qkv[g], wt["w_a_t"][g])], "nn", f32, tm=1024, tn=D, name=f"proj_a_bwd_{g}") for g in (1, 2)]
    d_h1 = _unperm_sum(d_h1_nat, d_h1_dil, dils, "unperm_d_h1")
    grad_x, dg_mix_pre = _norm_bwd_in(dx2, d_h1, x, g_mix_pre)

    small = dict(b_forget=d_b128[:, :N_FOX], norm_mix_pre=dg_mix_pre, norm_mix_post=dg_mix_post,
                 norm_ffn_pre=dg_ffn_pre, norm_ffn_post=dg_ffn_post)
    grads["mid_backward"] = d_qkv[0]
    return sq_err, grad_x, grads, small


NORMS = ("norm_mix_pre", "norm_mix_post", "norm_ffn_pre", "norm_ffn_post")
ORDER = ("w_in", "w_proj_a", "w_proj_b", "w_out", "b_forget", "w_ffn_gate", "w_ffn_up", "w_ffn_down") + NORMS


def kernel(x, w_in, w_proj_a, w_proj_b, w_out, b_forget, w_ffn_gate, w_ffn_up, w_ffn_down, norm_mix_pre, norm_mix_post, norm_ffn_pre, norm_ffn_post, loss_target, m_w_in, m_w_proj_a, m_w_proj_b, m_w_out, m_b_forget, m_w_ffn_gate, m_w_ffn_up, m_w_ffn_down, m_norm_mix_pre, m_norm_mix_post, m_norm_ffn_pre, m_norm_ffn_post, v_w_in, v_w_proj_a, v_w_proj_b, v_w_out, v_b_forget, v_w_ffn_gate, v_w_ffn_up, v_w_ffn_down, v_norm_mix_pre, v_norm_mix_post, v_norm_ffn_pre, v_norm_ffn_post):
    given = dict(w_in=w_in, w_proj_a=w_proj_a, w_proj_b=w_proj_b, w_out=w_out, w_ffn_gate=w_ffn_gate,
                 w_ffn_up=w_ffn_up, w_ffn_down=w_ffn_down)
    given_m = dict(w_in=m_w_in, w_proj_a=m_w_proj_a, w_proj_b=m_w_proj_b, w_out=m_w_out, w_ffn_gate=m_w_ffn_gate,
                   w_ffn_up=m_w_ffn_up, w_ffn_down=m_w_ffn_down)
    given_v = dict(w_in=v_w_in, w_proj_a=v_w_proj_a, w_proj_b=v_w_proj_b, w_out=v_w_out, w_ffn_gate=v_w_ffn_gate,
                   w_ffn_up=v_w_ffn_up, w_ffn_down=v_w_ffn_down)
    w, m, v = ({n: _kernel_layout(n, t[n]) for n in W_NAMES} for t in (given, given_m, given_v))
    small_w = dict(b_forget=b_forget, norm_mix_pre=norm_mix_pre, norm_mix_post=norm_mix_post,
                   norm_ffn_pre=norm_ffn_pre, norm_ffn_post=norm_ffn_post)
    small_m = dict(b_forget=m_b_forget, norm_mix_pre=m_norm_mix_pre, norm_mix_post=m_norm_mix_post,
                   norm_ffn_pre=m_norm_ffn_pre, norm_ffn_post=m_norm_ffn_post)
    small_v = dict(b_forget=v_b_forget, norm_mix_pre=v_norm_mix_pre, norm_mix_post=v_norm_mix_post,
                   norm_ffn_pre=v_norm_ffn_pre, norm_ffn_post=v_norm_ffn_post)

    own = [_halved(_pad_rows(w[n].astype(bf16), SHARD_SHAPE[n][0])) for n in W_NAMES]
    chip = 2 * lax.axis_index("x") + lax.axis_index("y")
    exchanged = {"first": _all_gather_async(own[:1], [], "all_gather_first", GATHER_FIRST_ID)}
    fill = lambda ts, mine: [lax.dynamic_update_index_in_dim(t, o, chip, 0) for t, o in zip(ts, mine)]

    def first_weights(ready):
        arrived, _ = lax.optimization_barrier((list(exchanged["first"]), ready))
        exchanged["late"] = _all_gather_async(own[1:], [arrived[0][0, 0, :16, :128]], "all_gather_late", GATHER_LATE_ID)
        return _full_weights(dict(zip(W_NAMES[:1], fill(arrived, own[:1]))))

    def late_weights(ready):
        arrived, _ = lax.optimization_barrier((list(exchanged["late"]), ready))
        return _full_weights(dict(zip(W_NAMES[1:], fill(arrived, own[1:]))))

    m["w_in"] = lax.optimization_barrier(m["w_in"])
    v["w_in"] = lax.optimization_barrier(v["w_in"])

    sq_err, grad_x, grads, small = _local_step(x[0], loss_target[0], first_weights, b_forget, norm_mix_pre,
                                               norm_mix_post, norm_ffn_pre, norm_ffn_post, late=late_weights)

    g4 = _sharded_grads(grads)
    stack = lambda t, extra: jnp.concatenate(
        [jnp.pad(t["b_forget"], ((0, 0), (0, D - N_FOX)))] + [t[n] for n in NORMS]
        + [jnp.pad(extra, ((0, SMALL_ROWS - LOSS_ROW - 1), (0, D - extra.shape[1])), constant_values=1.0)], axis=0)
    early, _ = lax.optimization_barrier((list(_pair_swap_early([g4[n] for n in W_NAMES[1:]])), grads["mid_backward"]))
    other = list(_pair_swap([g4["w_in"]])) + early
    parts = [_pair_sum(g4[n], o, "pair_sum_" + n) for n, o in zip(W_NAMES, other)]
    recv_early = _scatter_parts(parts[1:], "scatter_early", SCATTER_EARLY_ID)
    recv_in = _scatter_parts(parts[:1], "scatter_partials", SCATTER_LATE_ID)
    small_all = _share_small(stack(small, sq_err))

    g_shard, delta, new_m, new_v = {}, {}, {}, {}

    def summed(names, parts, recv):
        return [_sum_partials(p, r, "sum_partials_" + n) for n, p, r in zip(names, parts, recv)]

    def update(names, halves, theirs):
        for n, mine, other_half in zip(names, halves, theirs):
            g_shard[n], delta[n], new_m[n], new_v[n] = _adamw_halves(w[n], mine, other_half, m[n], v[n], "adamw_" + n)

    recv_early, _ = lax.optimization_barrier((list(recv_early), parts[0]))
    early_mine = summed(W_NAMES[1:], parts[1:], recv_early)
    recv_in, _ = lax.optimization_barrier((list(recv_in), early_mine))
    mine = summed(W_NAMES[:1], parts[:1], recv_in) + early_mine
    theirs = list(_swap_halves(mine, "swap_halves"))
    update(W_NAMES[:1], mine[:1], theirs[:1])
    (early_theirs, small_all), _ = lax.optimization_barrier(((theirs[1:], small_all), delta["w_in"]))
    update(W_NAMES[1:], early_mine, early_theirs)
    small_sum = _sum_small(small_all)
    loss = small_sum[LOSS_ROW, 0] * (0.5 / D)
    ones = jnp.ones((1, 128), f32)
    sd, sm, sv = _adamw(stack(small_w, ones), small_sum, stack(small_m, ones), stack(small_v, ones), "adamw_small")

    outs = [loss, grad_x[None]]
    for big, st in ((g_shard, small_sum), (delta, sd), (new_m, sm), (new_v, sv)):
        t = {n: _harness_layout(n, big[n]) for n in W_NAMES}
        t["b_forget"] = st[0:1, :N_FOX]
        for i, n in enumerate(NORMS):
            t[n] = st[i + 1:i + 2]
        outs += [t[n] for n in ORDER]
    return tuple(outs)
```

```python
import functools
import math

import jax
import jax.numpy as jnp
import numpy as np
from jax import lax
from jax.experimental import pallas as pl
from jax.experimental.pallas import tpu as pltpu
from jax.experimental.pallas import tpu_sc as plsc

f32 = jnp.float32
bf16 = jnp.bfloat16
SDS = jax.ShapeDtypeStruct
MESH = pl.DeviceIdType.MESH

S = 2048
D = 1024
HD = 64
BLK = 128
N_FOX = 8
FOX_W = N_FOX * HD
DIL_GROUPS = ((128, 1), (512, 4), (2048, 16))
SLOTS = 4
DIL_W = SLOTS * HD
QKV_W = 3 * DIL_W
VR_W = 3 * FOX_W
GF_W = 2 * D + 128
F_FF = 2816
ROPE_DIM = 16
ROPE_THETA = 500000.0
EPS = 1e-6
NEG = -1e30
SCALE = 1.0 / math.sqrt(HD)
IN_COLS = 5896
N_SHARD = 4

ADAM_LR, ADAM_B1, ADAM_B2, ADAM_EPS, ADAM_WD, ADAM_STEP = 0.001, 0.9, 0.999, 1e-08, 0.01, 10

VMEM_V7X = 64 * 1024 * 1024
VMEM_PLAN_MAX = VMEM_V7X - 8 * 1024 * 1024

TM = 512
TQ = 256

W_NAMES = ("w_in", "w_proj_a", "w_proj_b", "w_out", "w_ffn_gate", "w_ffn_up", "w_ffn_down")
TRANSPOSED = ("w_in", "w_ffn_gate", "w_ffn_up")
IN_SHARD = IN_COLS // N_SHARD
IN_SHARD_PAD = 1504
SHARD_SHAPE = dict(w_in=(IN_SHARD_PAD, D), w_proj_a=(DIL_W, D // N_SHARD), w_proj_b=(FOX_W, D // N_SHARD),
                   w_out=(D // N_SHARD, D), w_ffn_gate=(F_FF // N_SHARD, D), w_ffn_up=(F_FF // N_SHARD, D),
                   w_ffn_down=(F_FF // N_SHARD, D))
SMALL_ROWS = 8
LOSS_ROW = 5


def _nbytes(shape, dtype):
    return math.prod(shape) * jnp.dtype(dtype).itemsize


def _params(semantics, block_bytes, temp_bytes=0):
    need = 2 * block_bytes + temp_bytes + (2 << 20)
    return pltpu.CompilerParams(dimension_semantics=semantics, vmem_limit_bytes=int(min(need, VMEM_PLAN_MAX)))


def _row(w, tm=TM):
    return pl.BlockSpec((tm, w), lambda i: (i, 0))


def _vec(w):
    return pl.BlockSpec((1, w), lambda i: (0, 0))


def _mm(pairs, dims, out_dtype, *, tm, tn, name, m_inner=False):
    a0, b0 = pairs[0]
    m_dim = a0.shape[1] if dims == "tn" else a0.shape[0]
    n_dim = b0.shape[0] if dims == "nt" else b0.shape[1]
    contract = {"nn": ((1,), (0,)), "nt": ((1,), (1,)), "tn": ((0,), (0,))}[dims]
    n_pairs = len(pairs)
    assert m_dim % tm == 0 and n_dim % tn == 0, (name, m_dim, n_dim, tm, tn)

    def body(*refs):
        o_ref = refs[-1]
        acc = None
        for p in range(n_pairs):
            a = refs[2 * p][...].astype(bf16)
            b = refs[2 * p + 1][...].astype(bf16)
            t = lax.dot_general(a, b, (contract, ((), ())), preferred_element_type=f32)
            acc = t if acc is None else acc + t
        o_ref[...] = acc.astype(o_ref.dtype)

    if m_inner:
        grid = (n_dim // tn, m_dim // tm)
        mi = lambda j, i: i
        ni = lambda j, i: j
    else:
        grid = (m_dim // tm, n_dim // tn)
        mi = lambda i, j: i
        ni = lambda i, j: j
    in_specs, block_bytes, args = [], 0, []
    for a, b in pairs:
        k_dim = a.shape[0] if dims == "tn" else a.shape[1]
        if dims == "tn":
            in_specs.append(pl.BlockSpec((k_dim, tm), lambda *g: (0, mi(*g))))
        else:
            in_specs.append(pl.BlockSpec((tm, k_dim), lambda *g: (mi(*g), 0)))
        if dims == "nt":
            in_specs.append(pl.BlockSpec((tn, k_dim), lambda *g: (ni(*g), 0)))
        else:
            in_specs.append(pl.BlockSpec((k_dim, tn), lambda *g: (0, ni(*g))))
        block_bytes += _nbytes((tm, k_dim), a.dtype) + _nbytes((tn, k_dim), b.dtype)
        args += [a, b]
    block_bytes += _nbytes((tm, tn), out_dtype)
    temp = _nbytes((tm, tn), f32) * 2 + sum(_nbytes((tm, a.shape[0] if dims == "tn" else a.shape[1]), bf16)
                                            + _nbytes((tn, a.shape[0] if dims == "tn" else a.shape[1]), bf16)
                                            for a, _ in pairs)
    return pl.pallas_call(
        body, grid=grid, in_specs=in_specs,
        out_specs=pl.BlockSpec((tm, tn), lambda *g: (mi(*g), ni(*g))),
        out_shape=SDS((m_dim, n_dim), out_dtype), name=name,
        compiler_params=_params(("parallel", "parallel"), block_bytes, temp),
    )(*args)


def _rms(x, g):
    r = lax.rsqrt(jnp.mean(x * x, axis=-1, keepdims=True) + EPS)
    return x * r * g


def _rms_bwd(x, g, dy):
    r = lax.rsqrt(jnp.mean(x * x, axis=-1, keepdims=True) + EPS)
    xh = x * r
    dxh = dy * g
    dx = r * (dxh - xh * jnp.mean(dxh * xh, axis=-1, keepdims=True))
    return dx, jnp.sum(dy * xh, axis=0, keepdims=True)


def _acc_rows(ref, val):
    @pl.when(pl.program_id(0) == 0)
    def _():
        ref[...] = jnp.zeros_like(ref)
    ref[...] += val


def _norm_fwd(xs, g):
    n = len(xs)

    def body(*refs):
        g = refs[n][...]
        for x_ref, h_ref in zip(refs[:n], refs[n + 1:]):
            h_ref[...] = _rms(x_ref[...], g).astype(bf16)

    return pl.pallas_call(
        body, grid=(S // TM,), in_specs=[_row(D)] * n + [_vec(D)], out_specs=[_row(D)] * n,
        out_shape=[SDS((S, D), bf16)] * n, name="norm_mix_pre",
        compiler_params=_params(("parallel",), 6 * n * TM * D, 8 * n * TM * D))(*xs, g)


def _perm_rows(xs, ds, name):
    n = len(xs)

    def body(*refs):
        outs = iter(refs[n:])
        for x_ref in refs[:n]:
            for d in ds:
                o_ref, rows = next(outs), S // d
                for r in range(d):
                    o_ref[r * rows:(r + 1) * rows, :] = x_ref[pl.ds(r, rows, stride=d), :]

    blk = pl.BlockSpec((S, 128), lambda c: (0, c))
    w = xs[0].shape[1]
    return pl.pallas_call(
        body, grid=(w // 128,), in_specs=[blk] * n, out_specs=[blk] * (n * len(ds)),
        out_shape=[SDS((S, w), f32)] * (n * len(ds)), name=name,
        compiler_params=_params(("parallel",), 4 * S * 128 * n * (1 + len(ds))))(*xs)


def _unperm_sum(nat, perms, ds, name):
    n = len(perms)

    def body(*refs):
        a_ref, o_ref, sc = refs[0], refs[n + 1], refs[n + 2]
        acc = a_ref[...]
        for b_ref, d in zip(refs[1:n + 1], ds):
            rows = S // d
            for r in range(d):
                sc[pl.ds(r, rows, stride=d), :] = b_ref[r * rows:(r + 1) * rows, :]
            acc = acc + sc[...]
        o_ref[...] = acc

    blk = pl.BlockSpec((S, 128), lambda c: (0, c))
    w = nat.shape[1]
    return pl.pallas_call(
        body, grid=(w // 128,), in_specs=[blk] * (n + 1), out_specs=blk, out_shape=SDS((S, w), f32),
        scratch_shapes=[pltpu.VMEM((S, 128), f32)], name=name,
        compiler_params=_params(("parallel",), 4 * S * 128 * (n + 2), 8 * S * 128))(nat, *perms)


def _whole(a):
    return pl.BlockSpec(a.shape, lambda i: (0,) * a.ndim)


def _resid_norm_fwd(x, merged, w_out, g_post, g_pre):
    def body(x_ref, mg_ref, w_ref, gp_ref, gn_ref, mix_ref, x2_ref, h_ref):
        mix = jnp.dot(mg_ref[...], w_ref[...], preferred_element_type=f32)
        x2 = x_ref[...] + _rms(mix, gp_ref[...])
        mix_ref[...] = mix
        x2_ref[...] = x2
        h_ref[...] = _rms(x2, gn_ref[...]).astype(bf16)

    return pl.pallas_call(
        body, grid=(S // TM,), in_specs=[_row(D), _row(D), _whole(w_out), _vec(D), _vec(D)], out_specs=[_row(D)] * 3,
        out_shape=[SDS((S, D), f32), SDS((S, D), f32), SDS((S, D), bf16)], name="proj_out_norm",
        compiler_params=_params(("parallel",), 16 * TM * D + 2 * D * D, 16 * TM * D))(x, merged, w_out, g_post, g_pre)


def _loss_head(x2, a_act, w_down, g_post, target):
    def body(x2_ref, a_ref, w_ref, g_ref, t_ref, loss_ref, dy_ref, dff_ref, dg_ref):
        ff = jnp.dot(a_ref[...], w_ref[...], preferred_element_type=f32)
        g = g_ref[...]
        err = x2_ref[...] + _rms(ff, g) - t_ref[...]
        dy = err * (1.0 / D)
        dff, dg = _rms_bwd(ff, g, dy)
        dy_ref[...] = dy
        dff_ref[...] = dff.astype(bf16)
        _acc_rows(dg_ref, dg)
        _acc_rows(loss_ref, jnp.full((1, 128), jnp.sum(err * err), f32))

    return pl.pallas_call(
        body, grid=(S // TM,), in_specs=[_row(D), _row(F_FF), _whole(w_down), _vec(D), _row(D)],
        out_specs=[_vec(128), _row(D), _row(D), _vec(D)],
        out_shape=[SDS((1, 128), f32), SDS((S, D), f32), SDS((S, D), bf16), SDS((1, D), f32)], name="ffn_down_loss",
        compiler_params=_params(("arbitrary",), 14 * TM * D + 2 * TM * F_FF + 2 * F_FF * D, 28 * TM * D),
    )(x2, a_act, w_down, g_post, target)


def _norm_bwd_mid(dy, d_g, d_u, w_gate_t, w_up_t, x2, mix, g_ffn_pre, g_mix_post):
    def body(dy_ref, dgt_ref, dut_ref, wg_ref, wu_ref, x2_ref, mix_ref, g3_ref, g2_ref, dx2_ref, dmix_ref, dg3_ref, dg2_ref):
        dh = jnp.dot(dgt_ref[...], wg_ref[...], preferred_element_type=f32)
        dh += jnp.dot(dut_ref[...], wu_ref[...], preferred_element_type=f32)
        d3, dg3 = _rms_bwd(x2_ref[...], g3_ref[...], dh)
        dx2 = dy_ref[...] + d3
        dmix, dg2 = _rms_bwd(mix_ref[...], g2_ref[...], dx2)
        dx2_ref[...] = dx2
        dmix_ref[...] = dmix.astype(bf16)
        _acc_rows(dg3_ref, dg3)
        _acc_rows(dg2_ref, dg2)

    tm = TM // 2
    row = lambda w: _row(w, tm)
    return pl.pallas_call(
        body, grid=(S // tm,),
        in_specs=[row(D), row(F_FF), row(F_FF), _whole(w_gate_t), _whole(w_up_t), row(D), row(D), _vec(D), _vec(D)],
        out_specs=[row(D), row(D), _vec(D), _vec(D)],
        out_shape=[SDS((S, D), f32), SDS((S, D), bf16), SDS((1, D), f32), SDS((1, D), f32)], name="ffn_bwd_in_norm",
        compiler_params=_params(("arbitrary",), 18 * tm * D + 4 * tm * F_FF + 4 * F_FF * D, 28 * tm * D),
    )(dy, d_g, d_u, w_gate_t, w_up_t, x2, mix, g_ffn_pre, g_mix_post)


def _norm_bwd_in(dx2, dh1, x, g):
    def body(dx2_ref, dh_ref, x_ref, g_ref, gx_ref, dg_ref):
        d1, dg = _rms_bwd(x_ref[...], g_ref[...], dh_ref[...])
        gx_ref[...] = dx2_ref[...] + d1
        _acc_rows(dg_ref, dg)

    return pl.pallas_call(
        body, grid=(S // TM,), in_specs=[_row(D)] * 3 + [_vec(D)], out_specs=[_row(D), _vec(D)],
        out_shape=[SDS((S, D), f32), SDS((1, D), f32)], name="norm_bwd_in",
        compiler_params=_params(("arbitrary",), 16 * TM * D, 16 * TM * D))(dx2, dh1, x, g)


def _rope_tables():
    half = ROPE_DIM // 2
    inv_freq = np.power(np.float32(ROPE_THETA), -np.arange(0, ROPE_DIM, 2, dtype=np.float32) / np.float32(ROPE_DIM))
    row = np.arange(S)
    groups = []
    for _, d in DIL_GROUPS:
        pos = ((row % (S // d)) * d + row // (S // d)).astype(np.float32)
        ang = pos[:, None] * inv_freq[None, :].astype(np.float32)
        cos, sin = np.cos(ang).astype(np.float32), np.sin(ang).astype(np.float32)
        c = np.concatenate([cos, cos, np.ones((S, HD - ROPE_DIM), np.float32)], axis=1)
        s_lo = np.concatenate([-sin, np.zeros((S, HD - half), np.float32)], axis=1)
        s_hi = np.concatenate([np.zeros((S, half), np.float32), sin, np.zeros((S, HD - ROPE_DIM), np.float32)], axis=1)
        groups.append(np.stack([np.concatenate([t, t], axis=1) for t in (c, s_lo, s_hi)]))
    return jnp.asarray(np.stack(groups))


def _rotate(x, c, lo, hi, sign):
    tile = lambda t: jnp.tile(t, (1, DIL_W // 128))
    return (x * tile(c) + pltpu.roll(x, DIL_W - ROPE_DIM // 2, 1) * (tile(lo) * sign)
            + pltpu.roll(x, ROPE_DIM // 2, 1) * (tile(hi) * sign))


def _table_specs(g):
    return [pl.BlockSpec((None, None, TM, 128), lambda i, k=k: (g, k, i, 0)) for k in range(3)]


def _rope_fwd(g, p_qkv, tables):
    def body(x_ref, c_ref, lo_ref, hi_ref, o_ref):
        c, lo, hi = c_ref[...], lo_ref[...], hi_ref[...]
        for part in range(2):
            cols = slice(part * DIL_W, (part + 1) * DIL_W)
            o_ref[:, cols] = _rotate(x_ref[:, cols], c, lo, hi, 1.0).astype(bf16)
        o_ref[:, 2 * DIL_W:] = x_ref[:, 2 * DIL_W:].astype(bf16)

    return pl.pallas_call(
        body, grid=(S // TM,), in_specs=[_row(QKV_W)] + _table_specs(g), out_specs=_row(QKV_W),
        out_shape=SDS((S, QKV_W), bf16), name=f"rope_fwd_{g}",
        compiler_params=_params(("parallel",), 6 * TM * QKV_W + 12 * TM * 128, 24 * TM * QKV_W))(p_qkv, tables, tables, tables)


def _rope_bwd(g, dq, dk, dv, tables):
    def body(dq_ref, dk_ref, dv_ref, c_ref, lo_ref, hi_ref, o_ref):
        c, lo, hi = c_ref[...], lo_ref[...], hi_ref[...]
        o_ref[:, :DIL_W] = _rotate(dq_ref[...], c, lo, hi, -1.0).astype(bf16)
        o_ref[:, DIL_W:2 * DIL_W] = _rotate(dk_ref[...], c, lo, hi, -1.0).astype(bf16)
        o_ref[:, 2 * DIL_W:] = dv_ref[...].astype(bf16)

    return pl.pallas_call(
        body, grid=(S // TM,), in_specs=[_row(DIL_W)] * 3 + _table_specs(g), out_specs=_row(QKV_W),
        out_shape=SDS((S, QKV_W), bf16), name=f"rope_bwd_{g}",
        compiler_params=_params(("parallel",), 6 * TM * QKV_W + 12 * TM * 128, 24 * TM * QKV_W))(dq, dk, dv, tables, tables, tables)


def _nt(a, b):
    return lax.dot_general(a, b, (((1,), (1,)), ((), ())), preferred_element_type=f32)


def _tn(a, b):
    return lax.dot_general(a, b, (((0,), (0,)), ((), ())), preferred_element_type=f32)


STEP_BLOCKS = 4
STEP_ROWS = STEP_BLOCKS * BLK


def _dil_prev(g, b):
    _, d = DIL_GROUPS[g]
    nb = S // d // BLK
    if nb == 1 or (b == 0 and nb <= STEP_BLOCKS):
        return None
    return "in" if b > 0 else "halo"


def _bnt(a, b):
    return lax.dot_general(a, b, (((2,), (2,)), ((0,), (0,))), preferred_element_type=f32)


def _bnn(a, b):
    return lax.dot_general(a, b, (((2,), (1,)), ((0,), (0,))), preferred_element_type=f32)


def _btn(a, b):
    return lax.dot_general(a, b, (((1,), (1,)), ((0,), (0,))), preferred_element_type=f32)


def _on_tail(x, tail, fn):
    if tail == x.shape[0]:
        return fn(x)
    return jnp.concatenate([x[:-tail], fn(x[-tail:])], axis=0)


def _heads(ref, part):
    n = ref.shape[0] // BLK
    return jnp.stack([ref[b * BLK:(b + 1) * BLK, part * DIL_W + h * HD:part * DIL_W + (h + 1) * HD]
                      for b in range(n) for h in range(SLOTS)])


def _dil_operands(g, qkv_ref, halo_ref):
    q, kc, vc = (_heads(qkv_ref, part) for part in range(3))
    qi = lax.broadcasted_iota(jnp.int32, (1, BLK, BLK), 1)
    kj = lax.broadcasted_iota(jnp.int32, (1, BLK, BLK), 2)
    with_prev = [b for b in range(STEP_BLOCKS) if _dil_prev(g, b) is not None]
    tail = SLOTS * len(with_prev)
    if not tail:
        return q, kc, vc, None, None, kj <= qi, None, 0
    assert with_prev == list(range(STEP_BLOCKS - len(with_prev), STEP_BLOCKS))
    inside = SLOTS * sum(_dil_prev(g, b) == "in" for b in with_prev)
    kp, vp, prev = kc[:inside], vc[:inside], jnp.broadcast_to(kj >= qi, (inside, BLK, BLK))
    if inside < tail:
        no_halo = jnp.where(pl.program_id(0) == 0, BLK + 1, 0)
        kp = jnp.concatenate([_heads(halo_ref, 1), kp], axis=0)
        vp = jnp.concatenate([_heads(halo_ref, 2), vp], axis=0)
        prev = jnp.concatenate([jnp.broadcast_to(kj >= qi + no_halo, (SLOTS, BLK, BLK)), prev], axis=0)
    return q, kc, vc, kp, vp, kj <= qi, prev, tail


def _dil_in_specs(g, n_aux):
    step = lambda w: pl.BlockSpec((STEP_ROWS, w), lambda i: (i, 0))
    halo = [pl.BlockSpec((BLK, QKV_W), lambda i: (jnp.maximum(i * STEP_BLOCKS - 1, 0), 0))]
    needs_halo = _dil_prev(g, 0) == "halo"
    return [step(QKV_W)] + (halo if needs_halo else []) + [step(DIL_W)] * n_aux, needs_halo


def _dil_fwd(g, qkv):
    in_specs, needs_halo = _dil_in_specs(g, 0)

    def body(*refs):
        qkv_ref, halo_ref = refs[0], refs[1] if needs_halo else None
        o_ref, lse_ref = refs[-2:]
        q, kc, vc, kp, vp, cur, prev, tail = _dil_operands(g, qkv_ref, halo_ref)
        sc = jnp.where(cur, _bnt(q, kc) * SCALE, NEG)
        m = jnp.max(sc, axis=-1, keepdims=True)
        if tail:
            sp = jnp.where(prev, _bnt(q[-tail:], kp) * SCALE, NEG)
            m = _on_tail(m, tail, lambda t: jnp.maximum(t, jnp.max(sp, axis=-1, keepdims=True)))
            pp = jnp.exp(sp - m[-tail:])
        pc = jnp.exp(sc - m)
        den = jnp.sum(pc, axis=-1, keepdims=True)
        if tail:
            den = _on_tail(den, tail, lambda t: t + jnp.sum(pp, axis=-1, keepdims=True))
        inv = 1.0 / den
        o = _bnn((pc * inv).astype(bf16), vc)
        if tail:
            o = _on_tail(o, tail, lambda t: t + _bnn((pp * inv[-tail:]).astype(bf16), vp))
        lse = m + jnp.log(den)
        for b in range(STEP_BLOCKS):
            for h in range(SLOTS):
                rows, hs = slice(b * BLK, (b + 1) * BLK), slice(h * HD, (h + 1) * HD)
                o_ref[rows, hs] = o[SLOTS * b + h]
                lse_ref[rows, hs] = jnp.broadcast_to(lse[SLOTS * b + h], (BLK, HD))

    out = pl.BlockSpec((STEP_ROWS, DIL_W), lambda i: (i, 0))
    return pl.pallas_call(
        body, grid=(S // STEP_ROWS,), in_specs=in_specs, out_specs=[out, out], out_shape=[SDS((S, DIL_W), f32)] * 2,
        name=f"dil_fwd_{g}", compiler_params=_params(("parallel",), 12 * STEP_ROWS * DIL_W, 2 << 20),
    )(*([qkv] * (2 if needs_halo else 1)))


def _dil_combine(outs, lses):
    def body(o0, o1, o2, l0, l1, l2, out_ref, lse_ref, so1, so2, sl1, sl2):
        for (_, d), src, dst in ((DIL_GROUPS[1], o1, so1), (DIL_GROUPS[2], o2, so2),
                                 (DIL_GROUPS[1], l1, sl1), (DIL_GROUPS[2], l2, sl2)):
            rows = S // d
            for r in range(d):
                dst[pl.ds(r, rows, stride=d), :] = src[r * rows:(r + 1) * rows, :]
        a, b, c = l0[...], sl1[...], sl2[...]
        m = jnp.maximum(jnp.maximum(a, b), c)
        ea, eb, ec = jnp.exp(a - m), jnp.exp(b - m), jnp.exp(c - m)
        z = ea + eb + ec
        inv = 1.0 / z
        out_ref[...] = (ea * inv) * o0[...] + (eb * inv) * so1[...] + (ec * inv) * so2[...]
        lse_ref[...] = m + jnp.log(z)

    blk = pl.BlockSpec((S, 128), lambda c: (0, c))
    return pl.pallas_call(
        body, grid=(DIL_W // 128,), in_specs=[blk] * 6, out_specs=[blk] * 2,
        out_shape=[SDS((S, DIL_W), f32)] * 2, scratch_shapes=[pltpu.VMEM((S, 128), f32)] * 4, name="dil_combine",
        compiler_params=_params(("parallel",), 32 * S * 128, 32 * S * 128))(*outs, *lses)


def _dil_bwd(g, qkv, d_out, delta, lse):
    in_specs, needs_halo = _dil_in_specs(g, 3)

    def body(*refs):
        qkv_ref, halo_ref = refs[0], refs[1] if needs_halo else None
        do_ref, dl_ref, lse_ref, dq_ref, dk_ref, dv_ref = refs[-6:]
        q, kc, vc, kp, vp, cur, prev, tail = _dil_operands(g, qkv_ref, halo_ref)
        tiles = [(slice(b * BLK, (b + 1) * BLK), h) for b in range(STEP_BLOCKS) for h in range(SLOTS)]
        do = jnp.stack([do_ref[rows, h * HD:(h + 1) * HD] for rows, h in tiles]).astype(bf16)
        lse = jnp.stack([lse_ref[rows, h * HD:h * HD + 1] for rows, h in tiles])
        delta = jnp.stack([dl_ref[rows, h * HD:h * HD + 1] for rows, h in tiles])

        def probs(q, k, mask, lse, do, v, delta):
            p = jnp.exp(jnp.where(mask, _bnt(q, k) * SCALE, NEG) - lse)
            ds = p * (_bnt(do, v) - delta) * SCALE
            return p.astype(bf16), ds.astype(bf16)

        p, ds = probs(q, kc, cur, lse, do, vc, delta)
        dq, dk, dv = _bnn(ds, kc), _btn(ds, q), _btn(p, do)
        if tail:
            p, ds = probs(q[-tail:], kp, prev, lse[-tail:], do[-tail:], vp, delta[-tail:])
            dq = _on_tail(dq, tail, lambda t: t + _bnn(ds, kp))
            dk_p, dv_p = _btn(ds, q[-tail:]), _btn(p, do[-tail:])
            inside = tail - SLOTS if needs_halo else tail
            pad = jnp.zeros((len(tiles) - inside, BLK, HD), f32)
            dk = dk + jnp.concatenate([dk_p[tail - inside:], pad], axis=0)
            dv = dv + jnp.concatenate([dv_p[tail - inside:], pad], axis=0)
        first = pl.multiple_of(pl.program_id(0) * STEP_ROWS, STEP_ROWS)
        for t, (rows, h) in enumerate(tiles):
            hs = slice(h * HD, (h + 1) * HD)
            own = pl.ds(pl.multiple_of(first + rows.start, BLK), BLK)
            dq_ref[rows, hs] = dq[t]
            dk_ref[own, hs] = dk[t]
            dv_ref[own, hs] = dv[t]
        if needs_halo:
            before = pl.ds(pl.multiple_of(jnp.maximum(first - BLK, 0), BLK), BLK)
            for h in range(SLOTS):
                hs = slice(h * HD, (h + 1) * HD)
                dk_ref[before, hs] += dk_p[h]
                dv_ref[before, hs] += dv_p[h]

    whole = pl.BlockSpec((S, DIL_W), lambda i: (0, 0))
    return pl.pallas_call(
        body, grid=(S // STEP_ROWS,), in_specs=in_specs,
        out_specs=[pl.BlockSpec((STEP_ROWS, DIL_W), lambda i: (i, 0)), whole, whole],
        out_shape=[SDS((S, DIL_W), f32)] * 3, name=f"dil_bwd_{g}",
        compiler_params=_params(("arbitrary",), 20 * STEP_ROWS * DIL_W + 8 * S * DIL_W, 2 << 20),
    )(*([qkv] * (2 if needs_halo else 1)), d_out, delta, lse)


def _scan_rows(x, reverse):
    row = lax.broadcasted_iota(jnp.int32, x.shape, 0)
    k = 1
    while k < S:
        if reverse:
            x = x + jnp.where(row < S - k, pltpu.roll(x, S - k, 0), 0.0)
        else:
            x = x + jnp.where(row >= k, pltpu.roll(x, k, 0), 0.0)
        k *= 2
    return x


N_PAIR = N_FOX // 2
_PAIR_Q = pl.BlockSpec((None, S, 128), lambda p: (p, 0, 0))
_PAIR_K = pl.BlockSpec((None, 8, S), lambda p: (p, 0, 0))


def _forget_fwd(fz, b128):
    def body(z_ref, b_ref, fq_ref, fk_ref):
        z = z_ref[...] + b_ref[...]
        logf = jnp.minimum(z, 0.0) - jnp.log1p(jnp.exp(-jnp.abs(z)))
        f_cum = _scan_rows(logf, reverse=False)
        f_cum_t = f_cum.T
        fq_ref[...] = jnp.zeros_like(fq_ref)
        fk_ref[...] = jnp.zeros_like(fk_ref)
        for p in range(N_PAIR):
            fq_ref[p, :, 0:2] = f_cum[:, 2 * p:2 * p + 2]
            fk_ref[p, 0:2, :] = f_cum_t[2 * p:2 * p + 2, :]

    return pl.pallas_call(
        body, grid=(1,), in_specs=[pl.BlockSpec((S, 128), lambda i: (0, 0)), _vec(128)],
        out_specs=[pl.BlockSpec((N_PAIR, S, 128), lambda i: (0, 0, 0)), pl.BlockSpec((N_PAIR, 8, S), lambda i: (0, 0, 0))],
        out_shape=[SDS((N_PAIR, S, 128), f32), SDS((N_PAIR, 8, S), f32)], name="forget_fwd",
        compiler_params=_params(("arbitrary",), 24 * S * 128, 24 * S * 128))(fz, b128)


def _forget_bwd(fz, b128, d_f_cols, d_f_rows):
    def body(z_ref, b_ref, dfc_ref, dfr_ref, dz_ref, db_ref, df_sc):
        z = z_ref[...] + b_ref[...]
        df_sc[...] = jnp.zeros_like(df_sc)
        for p in range(N_PAIR):
            df_sc[:, 2 * p:2 * p + 2] = dfr_ref[p, :, 0:2] + dfc_ref[p].T[:, 0:2]
        dz = _scan_rows(df_sc[...], reverse=True) * jax.nn.sigmoid(-z)
        dz_ref[...] = dz
        db_ref[...] = jnp.sum(dz, axis=0, keepdims=True)

    full = pl.BlockSpec((S, 128), lambda i: (0, 0))
    return pl.pallas_call(
        body, grid=(1,),
        in_specs=[full, _vec(128), pl.BlockSpec((N_PAIR, 8, S), lambda i: (0, 0, 0)), pl.BlockSpec((N_PAIR, S, 128), lambda i: (0, 0, 0))],
        out_specs=[full, _vec(128)], out_shape=[SDS((S, 128), f32), SDS((1, 128), f32)],
        scratch_shapes=[pltpu.VMEM((S, 128), f32)], name="forget_bwd",
        compiler_params=_params(("arbitrary",), 32 * S * 128, 24 * S * 128))(fz, b128, d_f_cols, d_f_rows)


def _fox_scores(q_ref, k_ref, fq_ref, fk_ref, qi, hh):
    n = (qi + 1) * TQ
    rows, hs = slice(qi * TQ, n), slice(hh * HD, (hh + 1) * HD)
    q = q_ref[rows, hs] * SCALE
    s = _nt(q, k_ref[0:n, hs]) + (fq_ref[rows, hh:hh + 1] - fk_ref[hh:hh + 1, 0:n])
    below = lax.broadcasted_iota(jnp.int32, (TQ, TQ), 1) <= lax.broadcasted_iota(jnp.int32, (TQ, TQ), 0)
    diag = jnp.where(below, s[:, n - TQ:], NEG)
    return diag if qi == 0 else jnp.concatenate([s[:, :n - TQ], diag], axis=1)


def _pair_cols(first):
    return pl.BlockSpec((S, 128), lambda p: (0, first + p))


def _fox_fwd(vr, fq, fk):
    def body(q_ref, k_ref, v_ref, fq_ref, fk_ref, o_ref, lse_ref):
        lse_ref[...] = jnp.zeros_like(lse_ref)
        for hh in range(2):
            hs = slice(hh * HD, (hh + 1) * HD)
            for qi in range(S // TQ):
                n = (qi + 1) * TQ
                rows = slice(qi * TQ, n)
                s = _fox_scores(q_ref, k_ref, fq_ref, fk_ref, qi, hh)
                m = jnp.max(s, axis=-1, keepdims=True)
                p = jnp.exp(s - m)
                den = jnp.sum(p, axis=-1, keepdims=True)
                o_ref[rows, hs] = jnp.dot((p * (1.0 / den)).astype(bf16), v_ref[0:n, hs], preferred_element_type=f32)
                lse_ref[rows, hh:hh + 1] = m + jnp.log(den)

    return pl.pallas_call(
        body, grid=(N_PAIR,), in_specs=[_pair_cols(0), _pair_cols(N_PAIR), _pair_cols(2 * N_PAIR), _PAIR_Q, _PAIR_K],
        out_specs=[_pair_cols(0), _PAIR_Q], out_shape=[SDS((S, FOX_W), f32), SDS((N_PAIR, S, 128), f32)],
        name="fox_fwd", compiler_params=_params(("parallel",), 12 * S * 128, 16 * TQ * S),
    )(vr, vr, vr, fq, fk)


def _fox_bwd(vr, fq, fk, lse, d_out, delta):
    def body(q_ref, k_ref, v_ref, do_ref, fq_ref, fk_ref, lse_ref, dl_ref, dq_ref, dk_ref, dv_ref, dfc_ref, dfr_ref,
             dk_sc, dv_sc):
        dfc_ref[...] = jnp.zeros_like(dfc_ref)
        dfr_ref[...] = jnp.zeros_like(dfr_ref)
        for hh in range(2):
            hs = slice(hh * HD, (hh + 1) * HD)
            dk_sc[...] = jnp.zeros_like(dk_sc)
            dv_sc[...] = jnp.zeros_like(dv_sc)
            for qi in range(S // TQ):
                n = (qi + 1) * TQ
                rows = slice(qi * TQ, n)
                q, do, k, v = q_ref[rows, hs], do_ref[rows, hs], k_ref[0:n, hs], v_ref[0:n, hs]
                p = jnp.exp(_fox_scores(q_ref, k_ref, fq_ref, fk_ref, qi, hh) - lse_ref[rows, hh:hh + 1])
                ds = p * (_nt(do, v) - dl_ref[rows, hh:hh + 1])
                dsb = ds.astype(bf16)
                dq_ref[rows, hs] = jnp.dot(dsb, k, preferred_element_type=f32) * SCALE
                dk_sc[0:n, :] += _tn(dsb, q) * SCALE
                dv_sc[0:n, :] += _tn(p.astype(bf16), do)
                dfc_ref[hh:hh + 1, 0:n] -= jnp.sum(ds, axis=0, keepdims=True)
                dfr_ref[rows, hh:hh + 1] = jnp.sum(ds, axis=-1, keepdims=True)
            dk_ref[:, hs] = dk_sc[...]
            dv_ref[:, hs] = dv_sc[...]

    cols = [_pair_cols(k * N_PAIR) for k in range(3)]
    return pl.pallas_call(
        body, grid=(N_PAIR,), in_specs=cols + [_pair_cols(0), _PAIR_Q, _PAIR_K, _PAIR_Q, _PAIR_Q],
        out_specs=[_pair_cols(0)] * 3 + [_PAIR_K, _PAIR_Q],
        out_shape=[SDS((S, FOX_W), f32)] * 3 + [SDS((N_PAIR, 8, S), f32), SDS((N_PAIR, S, 128), f32)],
        scratch_shapes=[pltpu.VMEM((S, HD), f32)] * 2, name="fox_bwd",
        compiler_params=_params(("parallel",), 32 * S * 128, 24 * TQ * S),
    )(vr, vr, vr, d_out, fq, fk, lse, delta)


def _merge_fwd(out_a, out_b, w_a, w_b, gf):
    cw = D // N_SHARD

    def body(oa_ref, ob_ref, wa_ref, wb_ref, ga_ref, gb_ref, ya_ref, yb_ref, mg_ref):
        oa, ob = oa_ref[...].astype(bf16), ob_ref[...].astype(bf16)
        for j in range(N_SHARD):
            cols = slice(j * cw, (j + 1) * cw)
            ya = jnp.dot(oa, wa_ref[j], preferred_element_type=f32)
            yb = jnp.dot(ob, wb_ref[j], preferred_element_type=f32)
            ya_ref[:, cols] = ya
            yb_ref[:, cols] = yb
            mg_ref[:, cols] = (jax.nn.sigmoid(ga_ref[:, cols]) * ya + jax.nn.sigmoid(gb_ref[:, cols]) * yb).astype(bf16)

    full = lambda a: pl.BlockSpec(a.shape, lambda i: (0, 0, 0))
    return pl.pallas_call(
        body, grid=(S // TM,),
        in_specs=[_row(DIL_W), _row(FOX_W), full(w_a), full(w_b), _row(D), pl.BlockSpec((TM, D), lambda i: (i, 1))],
        out_specs=[_row(D)] * 3, out_shape=[SDS((S, D), f32), SDS((S, D), f32), SDS((S, D), bf16)], name="merge_fwd",
        compiler_params=_params(("parallel",), 22 * TM * D + 2 * (DIL_W + FOX_W) * D, 16 * TM * D),
    )(out_a, out_b, w_a, w_b, gf, gf)


def _merge_bwd(d_mix, w_out, ya, yb, gf):
    def body(dx_ref, w_ref, ya_ref, yb_ref, ga_ref, gb_ref, dya_ref, dyb_ref, dg_ref):
        dm = _nt(dx_ref[...], w_ref[...])
        sa, sb = jax.nn.sigmoid(ga_ref[...]), jax.nn.sigmoid(gb_ref[...])
        dya_ref[...] = (dm * sa).astype(bf16)
        dyb_ref[...] = (dm * sb).astype(bf16)
        dg_ref[:, :D] = (dm * ya_ref[...] * sa * (1.0 - sa)).astype(bf16)
        dg_ref[:, D:] = (dm * yb_ref[...] * sb * (1.0 - sb)).astype(bf16)

    return pl.pallas_call(
        body, grid=(S // TM,),
        in_specs=[_row(D), _whole(w_out)] + [_row(D)] * 3 + [pl.BlockSpec((TM, D), lambda i: (i, 1))],
        out_specs=[_row(D), _row(D), _row(2 * D)],
        out_shape=[SDS((S, D), bf16), SDS((S, D), bf16), SDS((S, 2 * D), bf16)], name="proj_out_bwd_merge",
        compiler_params=_params(("parallel",), 26 * TM * D + 2 * D * D, 28 * TM * D))(d_mix, w_out, ya, yb, gf, gf)


def _branch_bwd(d_ya, d_yb, w_a, w_b, out_a, out_b):
    cw = D // N_SHARD

    def body(dya_ref, dyb_ref, wa_ref, wb_ref, oa_ref, ob_ref, doa_ref, dla_ref, dob_ref, dlb_ref):
        doa = jnp.zeros((TM, DIL_W), f32)
        dob = jnp.zeros((TM, FOX_W), f32)
        for j in range(N_SHARD):
            cols = slice(j * cw, (j + 1) * cw)
            doa += _nt(dya_ref[:, cols], wa_ref[j])
            dob += _nt(dyb_ref[:, cols], wb_ref[j])
        doa_ref[...] = doa
        dob_ref[...] = dob.astype(bf16)
        prod_a = doa * oa_ref[...]
        for h in range(SLOTS):
            hs = slice(h * HD, (h + 1) * HD)
            dla_ref[:, hs] = jnp.broadcast_to(jnp.sum(prod_a[:, hs], axis=-1, keepdims=True), (TM, HD))
        prod_b = dob * ob_ref[...]
        dlb_ref[...] = jnp.zeros_like(dlb_ref)
        for h in range(N_FOX):
            dlb_ref[h // 2, :, h % 2:h % 2 + 1] = jnp.sum(prod_b[:, h * HD:(h + 1) * HD], axis=-1, keepdims=True)

    full = lambda a: pl.BlockSpec(a.shape, lambda i: (0, 0, 0))
    return pl.pallas_call(
        body, grid=(S // TM,),
        in_specs=[_row(D), _row(D), full(w_a), full(w_b), _row(DIL_W), _row(FOX_W)],
        out_specs=[_row(DIL_W), _row(DIL_W), _row(FOX_W), pl.BlockSpec((N_PAIR, TM, 128), lambda i: (0, i, 0))],
        out_shape=[SDS((S, DIL_W), f32), SDS((S, DIL_W), f32), SDS((S, FOX_W), bf16), SDS((N_PAIR, S, 128), f32)],
        name="branch_bwd", compiler_params=_params(("parallel",), 8 * TM * D + 2 * (DIL_W + FOX_W) * D, 8 * TM * D),
    )(d_ya, d_yb, w_a, w_b, out_a, out_b)


def _branch_grads(out_a, out_b, d_ya, d_yb):
    cw = D // N_SHARD

    def body(oa_ref, ob_ref, dya_ref, dyb_ref, ga_ref, gb_ref):
        ga_ref[...] = _tn(oa_ref[...].astype(bf16), dya_ref[...]).astype(bf16)
        gb_ref[...] = _tn(ob_ref[...].astype(bf16), dyb_ref[...]).astype(bf16)

    whole = lambda w: pl.BlockSpec((S, w), lambda j: (0, 0))
    cols = pl.BlockSpec((S, cw), lambda j: (0, j))
    return pl.pallas_call(
        body, grid=(N_SHARD,), in_specs=[whole(DIL_W), whole(FOX_W), cols, cols],
        out_specs=[pl.BlockSpec((None, DIL_W, cw), lambda j: (j, 0, 0)), pl.BlockSpec((None, FOX_W, cw), lambda j: (j, 0, 0))],
        out_shape=[SDS((N_SHARD, DIL_W, cw), bf16), SDS((N_SHARD, FOX_W, cw), bf16)], name="grad_w_proj_ab",
        compiler_params=_params(("parallel",), 4 * S * (DIL_W + FOX_W) + 4 * S * cw + 4 * (DIL_W + FOX_W) * cw,
                                4 * S * (DIL_W + FOX_W)))(out_a, out_b, d_ya, d_yb)


FF_TN = F_FF // 2
FF_TM = 1024


def _ffn_fwd(h, w_gate_t, w_up_t):
    def body(h_ref, wg_ref, wu_ref, g_ref, u_ref, a_ref):
        hb = h_ref[...]
        g = _nt(hb, wg_ref[...])
        u = _nt(hb, wu_ref[...])
        g_ref[...] = g
        u_ref[...] = u
        a_ref[...] = (g * jax.nn.sigmoid(g) * u).astype(bf16)

    tile = pl.BlockSpec((FF_TM, FF_TN), lambda j, i: (i, j))
    wspec = pl.BlockSpec((FF_TN, D), lambda j, i: (j, 0))
    return pl.pallas_call(
        body, grid=(F_FF // FF_TN, S // FF_TM),
        in_specs=[pl.BlockSpec((FF_TM, D), lambda j, i: (i, 0)), wspec, wspec], out_specs=[tile] * 3,
        out_shape=[SDS((S, F_FF), f32), SDS((S, F_FF), f32), SDS((S, F_FF), bf16)], name="ffn_fwd",
        compiler_params=_params(("parallel", "parallel"), 2 * FF_TM * D + 4 * D * FF_TN + 10 * FF_TM * FF_TN, 16 * FF_TM * FF_TN),
    )(h, w_gate_t, w_up_t)


def _ffn_bwd_act(d_ff, w_down, g_act, u_act):
    def body(d_ref, wd_ref, g_ref, u_ref, dg_ref, du_ref):
        da = _nt(d_ref[...], wd_ref[...])
        g = g_ref[...]
        sg = jax.nn.sigmoid(g)
        du_ref[...] = (da * g * sg).astype(bf16)
        dg_ref[...] = (da * u_ref[...] * sg * (1.0 + g * (1.0 - sg))).astype(bf16)

    tile = pl.BlockSpec((FF_TM, FF_TN), lambda j, i: (i, j))
    return pl.pallas_call(
        body, grid=(F_FF // FF_TN, S // FF_TM),
        in_specs=[pl.BlockSpec((FF_TM, D), lambda j, i: (i, 0)), pl.BlockSpec((FF_TN, D), lambda j, i: (j, 0)), tile, tile],
        out_specs=[tile, tile], out_shape=[SDS((S, F_FF), bf16)] * 2, name="ffn_bwd_act",
        compiler_params=_params(("parallel", "parallel"), 2 * FF_TM * D + 2 * D * FF_TN + 12 * FF_TM * FF_TN, 16 * FF_TM * FF_TN),
    )(d_ff, w_down, g_act, u_act)


def _row_tile(rows):
    return next(t for t in (376, 128, 176, 64, 32, 16, 8) if rows % t == 0)


def _adamw_math(w, g, m, v):
    c1 = 1.0 - ADAM_B1 ** ADAM_STEP
    c2 = 1.0 - ADAM_B2 ** ADAM_STEP
    m_new = ADAM_B1 * m + (1.0 - ADAM_B1) * g
    v_new = ADAM_B2 * v + (1.0 - ADAM_B2) * (g * g)
    return -ADAM_LR * ((m_new / c1) / (jnp.sqrt(v_new / c2) + ADAM_EPS) + ADAM_WD * w), m_new, v_new


def _adamw(w, g, m, v, name):
    rows, cols = w.shape
    tm = _row_tile(rows)

    def body(w_ref, g_ref, m_ref, v_ref, d_ref, nm_ref, nv_ref):
        d_ref[...], nm_ref[...], nv_ref[...] = _adamw_math(w_ref[...], g_ref[...], m_ref[...], v_ref[...])

    spec = pl.BlockSpec((tm, cols), lambda i: (i, 0))
    return pl.pallas_call(
        body, grid=(rows // tm,), in_specs=[spec] * 4, out_specs=[spec] * 3, out_shape=[SDS(w.shape, f32)] * 3,
        name=name, compiler_params=_params(("parallel",), 28 * tm * cols, 16 * tm * cols))(w, g, m, v)


def _adamw_halves(w, g_mine, g_theirs, m, v, name):
    cols = w.shape[1]
    tm = _row_tile(g_mine.shape[0])
    per_half = g_mine.shape[0] // tm
    assert 2 * g_mine.shape[0] - w.shape[0] < tm
    core = lax.axis_index("c").astype(jnp.int32).reshape(1)

    def body(c_ref, w_ref, gm_ref, gt_ref, m_ref, v_ref, g_ref, d_ref, nm_ref, nv_ref):
        mine = pl.program_id(0) // per_half == c_ref[0]
        g = jnp.where(mine, gm_ref[...], gt_ref[...])
        g_ref[...] = g
        d_ref[...], nm_ref[...], nv_ref[...] = _adamw_math(w_ref[...], g, m_ref[...], v_ref[...])

    spec = pl.BlockSpec((tm, cols), lambda i, c_ref: (i, 0))
    in_half = lambda i, first: jnp.clip(i - first * per_half, 0, per_half - 1)
    grid_spec = pltpu.PrefetchScalarGridSpec(
        num_scalar_prefetch=1, grid=(2 * per_half,),
        in_specs=[spec, pl.BlockSpec((tm, cols), lambda i, c_ref: (in_half(i, c_ref[0]), 0)),
                  pl.BlockSpec((tm, cols), lambda i, c_ref: (in_half(i, 1 - c_ref[0]), 0)), spec, spec],
        out_specs=[spec] * 4)
    return pl.pallas_call(
        body, grid_spec=grid_spec, out_shape=[SDS(w.shape, f32)] * 4, name=name,
        compiler_params=_params(("parallel",), 36 * tm * cols, 16 * tm * cols))(core, w, g_mine, g_theirs, m, v)


_ANY = pl.BlockSpec(memory_space=pl.ANY)


def _place():
    x, y, c = lax.axis_index("x"), lax.axis_index("y"), lax.axis_index("c")
    chips = [(1 - x, y), (x, 1 - y), (1 - x, 1 - y)]
    return x, y, c, chips


def _halved(t):
    return t.reshape(t.shape[:-2] + (2, t.shape[-2] // 2, t.shape[-1]))


def _gather_body(src, out, send_ici, recv_ici, send_d2d, recv_d2d):
    x, y, c, chips = _place()
    sibling = (x, y, 1 - c)
    me_j = 2 * x + y
    sends = []
    for a in range(len(src)):
        for p in range(3):
            cp = pltpu.make_async_remote_copy(
                src_ref=src[a].at[c], dst_ref=out[a].at[me_j, c], send_sem=send_ici.at[a, p],
                recv_sem=recv_ici.at[a, p], device_id=(*chips[p], c), device_id_type=MESH)
            cp.start()
            sends.append(cp)
    for a in range(len(src)):
        for p, (px, py) in enumerate(chips):
            blk = out[a].at[2 * px + py, c]
            pltpu.make_async_remote_copy(
                src_ref=blk, dst_ref=blk, send_sem=send_ici.at[a, p], recv_sem=recv_ici.at[a, p],
                device_id=sibling, device_id_type=MESH).wait_recv()
            fw = pltpu.make_async_remote_copy(
                src_ref=blk, dst_ref=blk, send_sem=send_d2d.at[a, p], recv_sem=recv_d2d.at[a, p],
                device_id=sibling, device_id_type=MESH)
            fw.start()
            sends.append(fw)
    for a in range(len(src)):
        for p, (px, py) in enumerate(chips):
            blk = out[a].at[2 * px + py, 1 - c]
            pltpu.make_async_remote_copy(
                src_ref=blk, dst_ref=blk, send_sem=send_d2d.at[a, p], recv_sem=recv_d2d.at[a, p],
                device_id=sibling, device_id_type=MESH).wait_recv()
    for cp in sends:
        cp.wait_send()


def _handshake(peers):
    barrier = pltpu.get_barrier_semaphore()
    for peer in peers:
        pl.semaphore_signal(barrier, inc=1, device_id=peer, device_id_type=MESH)
    pl.semaphore_wait(barrier, len(peers))


_SEQUENCER = dict(axis_name="sequencer", num_cores=1)
GATHER_LATE_ID, SCATTER_EARLY_ID, SWAP_EARLY_ID, GATHER_FIRST_ID, SCATTER_LATE_ID, SHARE_SMALL_ID = 1, 2, 3, 4, 5, 6


def _all_gather_async(shards, after, name, collective_id):
    n, k = len(shards), len(after)

    def body(*refs):
        x, y, c, chips = _place()
        _handshake([(*chip, c) for chip in chips] + [(x, y, 1 - c)])
        _gather_body(refs[:n], refs[n + k:2 * n + k], *refs[2 * n + k:])

    return pl.kernel(
        body, out_type=[SDS((N_SHARD,) + t.shape, t.dtype) for t in shards],
        mesh=plsc.ScalarSubcoreMesh(**_SEQUENCER), scratch_types=[pltpu.SemaphoreType.DMA((n, 3))] * 4,
        compiler_params=pltpu.CompilerParams(collective_id=collective_id), name=name)(*shards, *after)


def _pair_swap(grads):
    n = len(grads)

    def body(*refs):
        src, out, send_sems, recv_sems = refs[:n], refs[n:2 * n], refs[2 * n], refs[2 * n + 1]
        x, y, c, _ = _place()
        copies = [pltpu.make_async_remote_copy(
            src_ref=src[a].at[:, 1 - c], dst_ref=out[a], send_sem=send_sems.at[a], recv_sem=recv_sems.at[a],
            device_id=(x, y, 1 - c), device_id_type=MESH) for a in range(n)]
        for cp in copies:
            cp.start()
        for cp in copies:
            cp.wait()

    return pl.pallas_call(
        body, in_specs=[_ANY] * n, out_specs=[_ANY] * n,
        out_shape=[SDS((N_SHARD,) + t.shape[2:], t.dtype) for t in grads],
        scratch_shapes=[pltpu.SemaphoreType.DMA((n,)), pltpu.SemaphoreType.DMA((n,))], name="pair_swap",
        compiler_params=pltpu.CompilerParams(has_side_effects=True))(*grads)


def _pair_swap_sum(grads, name):
    _, _, rows, cols = grads.shape

    def body(g_ref, out_ref, mine, theirs, send_sem, recv_sem, local_sem):
        x, y, c, _ = _place()
        swap = pltpu.make_async_remote_copy(
            src_ref=g_ref.at[:, 1 - c], dst_ref=theirs, send_sem=send_sem, recv_sem=recv_sem,
            device_id=(x, y, 1 - c), device_id_type=MESH)
        own = pltpu.make_async_copy(g_ref.at[:, c], mine, local_sem)
        swap.start()
        own.start()
        own.wait()
        swap.wait()
        for j in range(N_SHARD):
            out_ref[j] = (mine[j].astype(f32) + theirs[j].astype(f32)).astype(bf16)

    block = (N_SHARD, rows, cols)
    return pl.pallas_call(
        body, in_specs=[_ANY], out_specs=pl.BlockSpec(block, lambda: (0, 0, 0)), out_shape=SDS(block, bf16),
        scratch_shapes=[pltpu.VMEM(block, bf16)] * 2 + [pltpu.SemaphoreType.DMA] * 3, name=name,
        compiler_params=pltpu.CompilerParams(
            has_side_effects=True, vmem_limit_bytes=8 * _nbytes(block, bf16) + (2 << 20)))(grads)


def _pair_swap_early(grads):
    n = len(grads)

    def body(*refs):
        src, out, send_sems, recv_sems = refs[:n], refs[n:2 * n], refs[2 * n], refs[2 * n + 1]
        x, y, c, _ = _place()
        _handshake([(x, y, 1 - c)])
        copies = [pltpu.make_async_remote_copy(
            src_ref=src[a].at[:, 1 - c], dst_ref=out[a], send_sem=send_sems.at[a], recv_sem=recv_sems.at[a],
            device_id=(x, y, 1 - c), device_id_type=MESH) for a in range(n)]
        for cp in copies:
            cp.start()
        for cp in copies:
            cp.wait()

    return pl.kernel(
        body, out_type=[SDS((N_SHARD,) + t.shape[2:], t.dtype) for t in grads],
        mesh=plsc.ScalarSubcoreMesh(**_SEQUENCER), scratch_types=[pltpu.SemaphoreType.DMA((n,))] * 2,
        compiler_params=pltpu.CompilerParams(collective_id=SWAP_EARLY_ID), name="pair_swap_early")(*grads)


def _scatter_parts(parts, name, collective_id):
    n = len(parts)

    def body(*refs):
        part, recv, send_sems, recv_sems = refs[:n], refs[n:2 * n], refs[2 * n], refs[2 * n + 1]
        x, y, c, chips = _place()
        _handshake([(*chip, c) for chip in chips])
        me_j = 2 * x + y
        sends = []
        for a in range(n):
            for p, (px, py) in enumerate(chips):
                cp = pltpu.make_async_remote_copy(
                    src_ref=part[a].at[2 * px + py], dst_ref=recv[a].at[me_j], send_sem=send_sems.at[a, p],
                    recv_sem=recv_sems.at[a, p], device_id=(px, py, c), device_id_type=MESH)
                cp.start()
                sends.append(cp)
        for a in range(n):
            for p, (px, py) in enumerate(chips):
                slot = recv[a].at[2 * px + py]
                pltpu.make_async_remote_copy(
                    src_ref=slot, dst_ref=slot, send_sem=send_sems.at[a, p], recv_sem=recv_sems.at[a, p],
                    device_id=(px, py, c), device_id_type=MESH).wait_recv()
        for cp in sends:
            cp.wait_send()

    return pl.kernel(
        body, out_type=[SDS(t.shape, t.dtype) for t in parts],
        mesh=plsc.ScalarSubcoreMesh(**_SEQUENCER), scratch_types=[pltpu.SemaphoreType.DMA((n, 3))] * 2,
        compiler_params=pltpu.CompilerParams(collective_id=collective_id), name=name)(*parts)


def _pair_sum(grads, other, name):
    _, _, rows, cols = grads.shape
    tr = _row_tile(rows)
    core = lax.axis_index("c").astype(jnp.int32).reshape(1)

    def body(c_ref, g_ref, o_ref, out_ref):
        out_ref[...] = (g_ref[...].astype(f32) + o_ref[...].astype(f32)).astype(bf16)

    grid_spec = pltpu.PrefetchScalarGridSpec(
        num_scalar_prefetch=1, grid=(N_SHARD, rows // tr),
        in_specs=[pl.BlockSpec((None, None, tr, cols), lambda j, i, c_ref: (j, c_ref[0], i, 0)),
                  pl.BlockSpec((None, tr, cols), lambda j, i, c_ref: (j, i, 0))],
        out_specs=pl.BlockSpec((None, tr, cols), lambda j, i, c_ref: (j, i, 0)))
    return pl.pallas_call(
        body, grid_spec=grid_spec, out_shape=SDS((N_SHARD, rows, cols), bf16), name=name,
        compiler_params=_params(("parallel", "parallel"), 10 * tr * cols, 12 * tr * cols))(core, grads, other)


def _share_small(small):
    def body(small_ref, small_all_ref, ssend, srecv, local_sem):
        x, y, c, _ = _place()
        flip = lambda a, bit: 1 - a if bit else a
        peers = [(flip(x, k & 4), flip(y, k & 2), flip(c, k & 1)) for k in range(1, 8)]
        _handshake(peers)
        me_dev = 4 * x + 2 * y + c
        own = pltpu.make_async_copy(small_ref, small_all_ref.at[me_dev], local_sem)
        own.start()
        sends = []
        for k, to in enumerate(peers):
            cp = pltpu.make_async_remote_copy(
                src_ref=small_ref, dst_ref=small_all_ref.at[me_dev],
                send_sem=ssend.at[k], recv_sem=srecv.at[k], device_id=to, device_id_type=MESH)
            cp.start()
            sends.append(cp)
        for k, (px, py, pc) in enumerate(peers):
            slot = small_all_ref.at[4 * px + 2 * py + pc]
            pltpu.make_async_remote_copy(
                src_ref=slot, dst_ref=slot, send_sem=ssend.at[k], recv_sem=srecv.at[k],
                device_id=(px, py, pc), device_id_type=MESH).wait_recv()
        for cp in sends:
            cp.wait_send()
        own.wait()

    return pl.kernel(
        body, out_type=SDS((8, SMALL_ROWS, D), f32), mesh=plsc.ScalarSubcoreMesh(**_SEQUENCER),
        scratch_types=[pltpu.SemaphoreType.DMA((7,)), pltpu.SemaphoreType.DMA((7,)), pltpu.SemaphoreType.DMA],
        compiler_params=pltpu.CompilerParams(collective_id=SHARE_SMALL_ID), name="share_small")(small)


def _sum_partials(part, recv, name):
    _, rows, cols = recv.shape
    tr = _row_tile(rows)
    me = (2 * lax.axis_index("x") + lax.axis_index("y")).astype(jnp.int32).reshape(1)

    def body(me_ref, mine, r0, r1, r2, r3, out_ref):
        acc = None
        for j, r in enumerate((r0, r1, r2, r3)):
            term = jnp.where(me_ref[0] == j, mine[...], r[...]).astype(f32)
            acc = term if acc is None else acc + term
        out_ref[...] = acc

    slot = lambda j: pl.BlockSpec((None, tr, cols), lambda i, me_ref: (jnp.where(me_ref[0] == j, j ^ 1, j), i, 0))
    grid_spec = pltpu.PrefetchScalarGridSpec(
        num_scalar_prefetch=1, grid=(rows // tr,),
        in_specs=[pl.BlockSpec((None, tr, cols), lambda i, me_ref: (me_ref[0], i, 0)), slot(0), slot(1), slot(2), slot(3)],
        out_specs=pl.BlockSpec((tr, cols), lambda i, me_ref: (i, 0)))
    return pl.pallas_call(
        body, grid_spec=grid_spec, out_shape=SDS((rows, cols), f32), name=name,
        compiler_params=_params(("parallel",), 14 * tr * cols, 12 * tr * cols))(me, part, recv, recv, recv, recv)


def _sum_small(small_all):
    def body(small_ref, out_ref):
        tot = small_ref[0]
        for k in range(1, 8):
            tot = tot + small_ref[k]
        out_ref[...] = tot

    return pl.pallas_call(
        body, grid=(1,), in_specs=[pl.BlockSpec((8, SMALL_ROWS, D), lambda i: (0, 0, 0))],
        out_specs=pl.BlockSpec((SMALL_ROWS, D), lambda i: (0, 0)), out_shape=SDS((SMALL_ROWS, D), f32),
        name="sum_small", compiler_params=_params(("arbitrary",), 36 * SMALL_ROWS * D))(small_all)


def _swap_halves(halves, name):
    n = len(halves)

    def body(*refs):
        src, out, send_sems, recv_sems = refs[:n], refs[n:2 * n], refs[2 * n], refs[2 * n + 1]
        x, y, c, _ = _place()
        copies = [pltpu.make_async_remote_copy(
            src_ref=src[a], dst_ref=out[a], send_sem=send_sems.at[a], recv_sem=recv_sems.at[a],
            device_id=(x, y, 1 - c), device_id_type=MESH) for a in range(n)]
        for cp in copies:
            cp.start()
        for cp in copies:
            cp.wait()

    return pl.pallas_call(
        body, in_specs=[_ANY] * n, out_specs=[_ANY] * n, out_shape=[SDS(t.shape, f32) for t in halves],
        scratch_shapes=[pltpu.SemaphoreType.DMA((n,))] * 2, name=name,
        compiler_params=pltpu.CompilerParams(has_side_effects=True))(*halves)


def _kernel_layout(name, t):
    t = t[0]
    return jnp.swapaxes(t, 0, 1) if name in TRANSPOSED else t


def _harness_layout(name, t):
    if name in TRANSPOSED:
        t = jnp.swapaxes(t, 0, 1)
    return t[None]


def _pad_rows(t, rows):
    return t if t.shape[0] == rows else jnp.pad(t, ((0, rows - t.shape[0]), (0, 0)))


_QA, _KA, _VA, _QB, _F, _GAB = 0, 768, 1536, 2304, 3840, 3848


def _spans(a, b):
    return [(j, max(a, j * IN_SHARD) - j * IN_SHARD, max(a, j * IN_SHARD) - a,
             min(b, (j + 1) * IN_SHARD) - max(a, j * IN_SHARD))
            for j in range(N_SHARD) if max(a, j * IN_SHARD) < min(b, (j + 1) * IN_SHARD)]


_LANES = pl.BlockSpec((N_SHARD, IN_SHARD_PAD, 128), lambda c: (0, 0, c))


def _split_w_in(shards):
    group = [[(o + g * DIL_W, o + (g + 1) * DIL_W) for o in (_QA, _KA, _VA)] for g in range(3)]
    fox = [[(_QB + k * FOX_W, _QB + (k + 1) * FOX_W)] for k in range(3)]
    wanted = group + fox + [[(_QB, _F)], [(_F, _GAB)], [(_GAB, IN_COLS)]]
    rows = [sum(b - a for a, b in w) for w in wanted]
    rows[7] = 128

    def body(s_ref, *o_refs):
        for o_ref, want in zip(o_refs, wanted):
            at = 0
            for a, b in want:
                for j, src, off, n in _spans(a, b):
                    o_ref[at + off:at + off + n, :] = s_ref[j, src:src + n, :]
                at += b - a
        o_refs[7][N_FOX:, :] = jnp.zeros((128 - N_FOX, 128), bf16)

    return pl.pallas_call(
        body, grid=(D // 128,), in_specs=[_LANES], out_specs=[pl.BlockSpec((r, 128), lambda c: (0, c)) for r in rows],
        out_shape=[SDS((r, D), bf16) for r in rows], name="split_w_in",
        compiler_params=_params(("parallel",), 2 * 128 * (N_SHARD * IN_SHARD_PAD + sum(rows))))(shards)


def _join_w_in(g_a, g_fox, g_f, g_gab):
    parts = [(g_a[k], o, o + DIL_W) for o in (0, DIL_W, 2 * DIL_W) for k in range(3)]
    parts += [(t, 0, FOX_W) for t in g_fox] + [(g_f, 0, N_FOX), (g_gab, 0, 2 * D)]
    arrays = list(g_a) + list(g_fox) + [g_f, g_gab]
    index = {id(t): i for i, t in enumerate(arrays)}

    def body(*refs):
        o_ref = refs[-1]
        o_ref[:, IN_SHARD:, :] = jnp.zeros((N_SHARD, IN_SHARD_PAD - IN_SHARD, 128), bf16)
        at = 0
        for t, lo, hi in parts:
            src_ref = refs[index[id(t)]]
            for j, dst, off, n in _spans(at, at + hi - lo):
                o_ref[j, dst:dst + n, :] = src_ref[lo + off:lo + off + n, :].astype(bf16)
            at += hi - lo

    return pl.pallas_call(
        body, grid=(D // 128,), in_specs=[pl.BlockSpec((t.shape[0], 128), lambda c: (0, c)) for t in arrays],
        out_specs=_LANES, out_shape=SDS((N_SHARD, IN_SHARD_PAD, D), bf16), name="join_w_in",
        compiler_params=_params(("parallel",), 2 * 128 * (N_SHARD * IN_SHARD_PAD + sum(t.shape[0] for t in arrays))),
    )(*arrays)


def _full_weights(gathered):
    full = {n: t.reshape((N_SHARD,) + SHARD_SHAPE[n]) for n, t in gathered.items()}
    out = {}
    if "w_in" in full:
        pieces = _split_w_in(full["w_in"])
        out.update(w_a_t=pieces[0:3], w_fox_t=pieces[3:6], w_vr_t=pieces[6], w_f_t=pieces[7], w_gab_t=pieces[8])
    if "w_out" in full:
        out.update(
            w_a4=full["w_proj_a"],
            w_b4=full["w_proj_b"],
            w_out=full["w_out"].reshape(D, D),
            w_gate_t=full["w_ffn_gate"].reshape(F_FF, D),
            w_up_t=full["w_ffn_up"].reshape(F_FF, D),
            w_down=full["w_ffn_down"].reshape(F_FF, D))
    return out


def _sharded_grads(g):
    full = dict(w_in=_join_w_in(g["w_a_t"], g["w_fox_t"], g["w_f_t"], g["w_gab_t"]), w_proj_a=g["w_a4"],
                w_proj_b=g["w_b4"], w_out=g["w_out"], w_ffn_gate=g["w_gate_t"], w_ffn_up=g["w_up_t"],
                w_ffn_down=g["w_down"])
    return {n: _halved(full[n].reshape((N_SHARD,) + SHARD_SHAPE[n])) for n in W_NAMES}


def _local_step(x, target, wt, b_forget, g_mix_pre, g_mix_post, g_ffn_pre, g_ffn_post, late=None):
    tables = _rope_tables()
    b128 = jnp.pad(b_forget, ((0, 0), (0, 128 - N_FOX)))
    dils = tuple(d for _, d in DIL_GROUPS[1:])

    hs = _norm_fwd([x] + list(_perm_rows([x], dils, "perm_x")), g_mix_pre)
    h1 = hs[0]
    if callable(wt):
        wt = wt(h1)
    qkv = [_rope_fwd(g, _mm([(hs[g], wt["w_a_t"][g])], "nt", f32, tm=1024, tn=QKV_W, name=f"proj_a_{g}"), tables)
           for g in range(3)]
    vr = _mm([(h1, wt["w_vr_t"])], "nt", bf16, tm=1024, tn=VR_W // 2, name="proj_vr")
    gab = _mm([(h1, wt["w_gab_t"])], "nt", f32, tm=512, tn=2 * D, name="proj_gab")
    fz = _mm([(h1, wt["w_f_t"])], "nt", f32, tm=1024, tn=128, name="proj_f")
    dil = [_dil_fwd(g, qkv[g]) for g in range(3)]
    out_a, lse_a = _dil_combine([o for o, _ in dil], [l for _, l in dil])
    f_q, f_k = _forget_fwd(fz, b128)
    out_b, lse_b = _fox_fwd(vr, f_q, f_k)
    if late is not None:
        wt = {**wt, **late(out_b)}
    ya, yb, merged = _merge_fwd(out_a, out_b, wt["w_a4"], wt["w_b4"], gab)
    mix, x2, h3 = _resid_norm_fwd(x, merged, wt["w_out"], g_mix_post, g_ffn_pre)
    g_act, u_act, a_act = _ffn_fwd(h3, wt["w_gate_t"], wt["w_up_t"])
    sq_err, dy, d_ff, dg_ffn_post = _loss_head(x2, a_act, wt["w_down"], g_ffn_post, target)

    grads = {}
    d_g, d_u = _ffn_bwd_act(d_ff, wt["w_down"], g_act, u_act)
    grads["w_down"] = _mm([(a_act, d_ff)], "tn", bf16, tm=FF_TN, tn=D, name="grad_w_down")
    grads["w_gate_t"] = _mm([(d_g, h3)], "tn", bf16, tm=FF_TN, tn=D, name="grad_w_gate")
    grads["w_up_t"] = _mm([(d_u, h3)], "tn", bf16, tm=FF_TN, tn=D, name="grad_w_up")
    dx2, d_mix, dg_ffn_pre, dg_mix_post = _norm_bwd_mid(dy, d_g, d_u, wt["w_gate_t"], wt["w_up_t"], x2, mix,
                                                        g_ffn_pre, g_mix_post)

    grads["w_out"] = _mm([(merged, d_mix)], "tn", bf16, tm=D, tn=D, name="grad_w_out")
    d_ya, d_yb, d_gab = _merge_bwd(d_mix, wt["w_out"], ya, yb, gab)
    grads["w_a4"], grads["w_b4"] = _branch_grads(out_a, out_b, d_ya, d_yb)
    d_out_a, delta_a, d_out_b, delta_b = _branch_bwd(d_ya, d_yb, wt["w_a4"], wt["w_b4"], out_a, out_b)

    perm = _perm_rows([d_out_a, delta_a, lse_a], dils, "perm_dil_bwd")
    aux = [(d_out_a, delta_a, lse_a)] + [tuple(perm[k * len(dils) + i] for k in range(3)) for i in range(len(dils))]
    d_qkv = []
    for g in range(3):
        dq, dk, dv = _dil_bwd(g, qkv[g], *aux[g])
        d_qkv.append(_rope_bwd(g, dq, dk, dv, tables))
    *d_fox, d_f_cols, d_f_rows = _fox_bwd(vr, f_q, f_k, lse_b, d_out_b, delta_b)
    d_z, d_b128 = _forget_bwd(fz, b128, d_f_cols, d_f_rows)

    grads["w_a_t"] = [_mm([(d_qkv[g], hs[g])], "tn", bf16, tm=QKV_W, tn=D, name=f"grad_w_a_{g}") for g in range(3)]
    grads["w_fox_t"] = [_mm([(d_fox[k], h1)], "tn", bf16, tm=FOX_W, tn=D, name=f"grad_w_fox_{k}") for k in range(3)]
    grads["w_gab_t"] = _mm([(d_gab, h1)], "tn", bf16, tm=D, tn=D, name="grad_w_gab")
    grads["w_f_t"] = _mm([(d_z, h1)], "tn", bf16, tm=128, tn=D, name="grad_w_f")
    d_h1_nat = _mm([(d_qkv[0], wt["w_a_t"][0])] + list(zip(d_fox, wt["w_fox_t"]))
                   + [(d_gab, wt["w_gab_t"]), (d_z, wt["w_f_t"])], "nn", f32, tm=512, tn=D, name="proj_in_bwd")
    d_h1_dil = [_mm([(d_qkv[g], wt["w_a_t"][g])], "nn", f32, tm=1024, tn=D, name=f"proj_a_bwd_{g}") for g in (1, 2)]
    d_h1 = _unperm_sum(d_h1_nat, d_h1_dil, dils, "unperm_d_h1")
    grad_x, dg_mix_pre = _norm_bwd_in(dx2, d_h1, x, g_mix_pre)

    small = dict(b_forget=d_b128[:, :N_FOX], norm_mix_pre=dg_mix_pre, norm_mix_post=dg_mix_post,
                 norm_ffn_pre=dg_ffn_pre, norm_ffn_post=dg_ffn_post)
    grads["mid_backward"] = d_qkv[0]
    return sq_err, grad_x, grads, small


NORMS = ("norm_mix_pre", "norm_mix_post", "norm_ffn_pre", "norm_ffn_post")
ORDER = ("w_in", "w_proj_a", "w_proj_b", "w_out", "b_forget", "w_ffn_gate", "w_ffn_up", "w_ffn_down") + NORMS


def kernel(x, w_in, w_proj_a, w_proj_b, w_out, b_forget, w_ffn_gate, w_ffn_up, w_ffn_down, norm_mix_pre, norm_mix_post, norm_ffn_pre, norm_ffn_post, loss_target, m_w_in, m_w_proj_a, m_w_proj_b, m_w_out, m_b_forget, m_w_ffn_gate, m_w_ffn_up, m_w_ffn_down, m_norm_mix_pre, m_norm_mix_post, m_norm_ffn_pre, m_norm_ffn_post, v_w_in, v_w_proj_a, v_w_proj_b, v_w_out, v_b_forget, v_w_ffn_gate, v_w_ffn_up, v_w_ffn_down, v_norm_mix_pre, v_norm_mix_post, v_norm_ffn_pre, v_norm_ffn_post):
    given = dict(w_in=w_in, w_proj_a=w_proj_a, w_proj_b=w_proj_b, w_out=w_out, w_ffn_gate=w_ffn_gate,
                 w_ffn_up=w_ffn_up, w_ffn_down=w_ffn_down)
    given_m = dict(w_in=m_w_in, w_proj_a=m_w_proj_a, w_proj_b=m_w_proj_b, w_out=m_w_out, w_ffn_gate=m_w_ffn_gate,
                   w_ffn_up=m_w_ffn_up, w_ffn_down=m_w_ffn_down)
    given_v = dict(w_in=v_w_in, w_proj_a=v_w_proj_a, w_proj_b=v_w_proj_b, w_out=v_w_out, w_ffn_gate=v_w_ffn_gate,
                   w_ffn_up=v_w_ffn_up, w_ffn_down=v_w_ffn_down)
    w, m, v = ({n: _kernel_layout(n, t[n]) for n in W_NAMES} for t in (given, given_m, given_v))
    small_w = dict(b_forget=b_forget, norm_mix_pre=norm_mix_pre, norm_mix_post=norm_mix_post,
                   norm_ffn_pre=norm_ffn_pre, norm_ffn_post=norm_ffn_post)
    small_m = dict(b_forget=m_b_forget, norm_mix_pre=m_norm_mix_pre, norm_mix_post=m_norm_mix_post,
                   norm_ffn_pre=m_norm_ffn_pre, norm_ffn_post=m_norm_ffn_post)
    small_v = dict(b_forget=v_b_forget, norm_mix_pre=v_norm_mix_pre, norm_mix_post=v_norm_mix_post,
                   norm_ffn_pre=v_norm_ffn_pre, norm_ffn_post=v_norm_ffn_post)

    own = [_halved(_pad_rows(w[n].astype(bf16), SHARD_SHAPE[n][0])) for n in W_NAMES]
    chip = 2 * lax.axis_index("x") + lax.axis_index("y")
    exchanged = {"first": _all_gather_async(own[:1], [], "all_gather_first", GATHER_FIRST_ID)}
    fill = lambda ts, mine: [lax.dynamic_update_index_in_dim(t, o, chip, 0) for t, o in zip(ts, mine)]

    def first_weights(ready):
        arrived, _ = lax.optimization_barrier((list(exchanged["first"]), ready))
        exchanged["late"] = _all_gather_async(own[1:], [arrived[0][0, 0, :16, :128]], "all_gather_late", GATHER_LATE_ID)
        return _full_weights(dict(zip(W_NAMES[:1], fill(arrived, own[:1]))))

    def late_weights(ready):
        arrived, _ = lax.optimization_barrier((list(exchanged["late"]), ready))
        return _full_weights(dict(zip(W_NAMES[1:], fill(arrived, own[1:]))))

    m["w_in"] = lax.optimization_barrier(m["w_in"])
    v["w_in"] = lax.optimization_barrier(v["w_in"])

    sq_err, grad_x, grads, small = _local_step(x[0], loss_target[0], first_weights, b_forget, norm_mix_pre,
                                               norm_mix_post, norm_ffn_pre, norm_ffn_post, late=late_weights)

    g4 = _sharded_grads(grads)
    stack = lambda t, extra: jnp.concatenate(
        [jnp.pad(t["b_forget"], ((0, 0), (0, D - N_FOX)))] + [t[n] for n in NORMS]
        + [jnp.pad(extra, ((0, SMALL_ROWS - LOSS_ROW - 1), (0, D - extra.shape[1])), constant_values=1.0)], axis=0)
    early, _ = lax.optimization_barrier((list(_pair_swap_early([g4[n] for n in W_NAMES[1:]])), grads["mid_backward"]))
    parts = [_pair_swap_sum(g4["w_in"], "pair_swap_sum_w_in")]
    parts += [_pair_sum(g4[n], o, "pair_sum_" + n) for n, o in zip(W_NAMES[1:], early)]
    recv_early = _scatter_parts(parts[1:], "scatter_early", SCATTER_EARLY_ID)
    recv_in = _scatter_parts(parts[:1], "scatter_partials", SCATTER_LATE_ID)
    small_all = _share_small(stack(small, sq_err))

    g_shard, delta, new_m, new_v = {}, {}, {}, {}

    def summed(names, parts, recv):
        return [_sum_partials(p, r, "sum_partials_" + n) for n, p, r in zip(names, parts, recv)]

    def update(names, halves, theirs):
        for n, mine, other_half in zip(names, halves, theirs):
            g_shard[n], delta[n], new_m[n], new_v[n] = _adamw_halves(w[n], mine, other_half, m[n], v[n], "adamw_" + n)

    recv_early, _ = lax.optimization_barrier((list(recv_early), parts[0]))
    early_mine = summed(W_NAMES[1:], parts[1:], recv_early)
    recv_in, _ = lax.optimization_barrier((list(recv_in), early_mine))
    mine = summed(W_NAMES[:1], parts[:1], recv_in) + early_mine
    theirs = list(_swap_halves(mine, "swap_halves"))
    update(W_NAMES[:1], mine[:1], theirs[:1])
    (early_theirs, small_all), _ = lax.optimization_barrier(((theirs[1:], small_all), delta["w_in"]))
    update(W_NAMES[1:], early_mine, early_theirs)
    small_sum = _sum_small(small_all)
    loss = small_sum[LOSS_ROW, 0] * (0.5 / D)
    ones = jnp.ones((1, 128), f32)
    sd, sm, sv = _adamw(stack(small_w, ones), small_sum, stack(small_m, ones), stack(small_v, ones), "adamw_small")

    outs = [loss, grad_x[None]]
    for big, st in ((g_shard, small_sum), (delta, sd), (new_m, sm), (new_v, sv)):
        t = {n: _harness_layout(n, big[n]) for n in W_NAMES}
        t["b_forget"] = st[0:1, :N_FOX]
        for i, n in enumerate(NORMS):
            t[n] = st[i + 1:i + 2]
        outs += [t[n] for n in ORDER]
    return tuple(outs)
```
